```python
import math
import jax, jax.numpy as jnp
from jax import lax
import numpy as np

D_MODEL = 1024
BATCH = 8
SEQ = 4096
DEPTH = 1

N_META = 16
CHUNK = 128
Q_BLOCK = 128
PAD = CHUNK - N_META
D_SSD = 2 * D_MODEL
SSD_HEAD_DIM = 64
H_SSD = D_SSD // SSD_HEAD_DIM
SSD_GROUPS = 4
D_STATE = 128
CONV_K = 4
CONV_DIM = D_SSD + 2 * SSD_GROUPS * D_STATE
H_ATT = 16
ATT_HEAD_DIM = 64
D_ATT = H_ATT * ATT_HEAD_DIM
N_COLS = D_SSD + CONV_DIM + H_SSD + D_ATT + 3 * D_ATT + H_ATT + 2 * D_MODEL
EPS = 1e-6

kernel_name = "hybrid_ssd_fox_gated_merge"


def rmsnorm(x, g):
    xf = x.astype(jnp.float32)
    y = xf * lax.rsqrt(jnp.mean(xf * xf, axis=-1, keepdims=True) + EPS)
    return (y * g.astype(jnp.float32)).astype(x.dtype)


def gated_group_rmsnorm(y, z, g):
    u = (y * jax.nn.silu(z)).astype(jnp.float32)
    shp = u.shape
    u = u.reshape(shp[:-1] + (SSD_GROUPS, shp[-1] // SSD_GROUPS))
    u = u * lax.rsqrt(jnp.mean(u * u, axis=-1, keepdims=True) + EPS)
    return (u.reshape(shp) * g.astype(jnp.float32)).astype(y.dtype)


def causal_depthwise_conv(u, w, b):
    C = u.shape[-1]
    out = lax.conv_general_dilated(u, w[:, None, :].astype(u.dtype), window_strides=(1,),
                                   padding=[(CONV_K - 1, 0)],
                                   dimension_numbers=("NWC", "WIO", "NWC"),
                                   feature_group_count=C)
    return out + b


def ssd_chunked(xh, dt, a, bmat, cmat):
    Bsz, Lp, H, P = xh.shape
    G, N = bmat.shape[-2:]
    R = H // G
    nc = Lp // CHUNK
    xdt = (xh.astype(jnp.float32) * dt[..., None]).reshape(Bsz, nc, CHUNK, G, R, P)
    adt = (dt * a).reshape(Bsz, nc, CHUNK, G, R).transpose(0, 1, 3, 4, 2)
    a_cs = jnp.cumsum(adt, axis=-1)
    bm = bmat.astype(jnp.float32).reshape(Bsz, nc, CHUNK, G, N)
    cm = cmat.astype(jnp.float32).reshape(Bsz, nc, CHUNK, G, N)
    causal = jnp.tril(jnp.ones((CHUNK, CHUNK), dtype=bool))
    seg = a_cs[..., :, None] - a_cs[..., None, :]
    decay = jnp.exp(jnp.where(causal, seg, -jnp.inf))
    cb = jnp.einsum("bclgn,bcsgn->bcgls", cm, bm)
    y_diag = jnp.einsum("bcgls,bcgrls,bcsgrp->bclgrp", cb, decay, xdt)
    decay_states = jnp.exp(a_cs[..., -1:] - a_cs)
    states = jnp.einsum("bclgn,bcgrl,bclgrp->bcgrpn", bm, decay_states, xdt)
    chunk_decay = jnp.exp(a_cs[..., -1])

    def step(h, inp):
        s, d = inp
        return h * d[..., None, None] + s, h

    h0 = jnp.zeros((Bsz, G, R, P, N), jnp.float32)
    _, h_in = lax.scan(step, h0, (jnp.swapaxes(states, 0, 1), jnp.swapaxes(chunk_decay, 0, 1)))
    h_in = jnp.swapaxes(h_in, 0, 1)
    y_off = jnp.einsum("bclgn,bcgrpn,bcgrl->bclgrp", cm, h_in, jnp.exp(a_cs))
    return (y_diag + y_off).reshape(Bsz, Lp, H, P)


def forgetting_attention(q, k, v, logf):
    Bsz, Lp, H, Dh = q.shape
    scale = 1.0 / math.sqrt(Dh)
    c = jnp.cumsum(logf, axis=1)
    c_k = jnp.transpose(c, (0, 2, 1))
    nb = Lp // Q_BLOCK
    qb = q.reshape(Bsz, nb, Q_BLOCK, H, Dh).transpose(1, 0, 2, 3, 4)
    cqb = c.reshape(Bsz, nb, Q_BLOCK, H).transpose(1, 0, 3, 2)
    kpos = jnp.arange(Lp)

    def block(args):
        i, qi, cqi = args
        qpos = i * Q_BLOCK + jnp.arange(Q_BLOCK)
        s = jnp.einsum("bthd,bshd->bhts", qi, k).astype(jnp.float32) * scale
        s = s + (cqi[..., :, None] - c_k[..., None, :])
        s = jnp.where(kpos[None, :] <= qpos[:, None], s, -jnp.inf)
        p = jax.nn.softmax(s, axis=-1).astype(v.dtype)
        return jnp.einsum("bhts,bshd->bthd", p, v)

    out = lax.map(block, (jnp.arange(nb), qb, cqb))
    return out.transpose(1, 0, 2, 3, 4).reshape(Bsz, Lp, H, Dh)


def hybrid_layer(h, norm_pre, w_in, conv_w, conv_b, dt_bias, a_log, d_skip, ssd_norm,
                 fgate_bias, gate_bias, w_proj_ssd, w_proj_att, w_out, norm_post):
    Bsz, L, _ = h.shape
    Lp = L + PAD
    u = rmsnorm(h, norm_pre)
    proj = u @ w_in
    cuts = [D_SSD, D_SSD + CONV_DIM, D_SSD + CONV_DIM + H_SSD,
            D_SSD + CONV_DIM + H_SSD + D_ATT,
            D_SSD + CONV_DIM + H_SSD + 2 * D_ATT,
            D_SSD + CONV_DIM + H_SSD + 3 * D_ATT,
            D_SSD + CONV_DIM + H_SSD + 4 * D_ATT,
            D_SSD + CONV_DIM + H_SSD + 4 * D_ATT + H_ATT]
    z_ssd, xbc, dt_raw, z_att, q, k, v, f_raw, g_raw = jnp.split(proj, cuts, axis=-1)

    xbc = jax.nn.silu(causal_depthwise_conv(xbc, conv_w, conv_b))
    xs, bm, cm = jnp.split(xbc, [D_SSD, D_SSD + SSD_GROUPS * D_STATE], axis=-1)
    dt = jax.nn.softplus(dt_raw.astype(jnp.float32) + dt_bias.astype(jnp.float32))
    a = -jnp.exp(a_log.astype(jnp.float32))
    front = ((0, 0), (PAD, 0), (0, 0))
    xs_p = jnp.pad(xs, front).reshape(Bsz, Lp, H_SSD, SSD_HEAD_DIM)
    bm_p = jnp.pad(bm, front).reshape(Bsz, Lp, SSD_GROUPS, D_STATE)
    cm_p = jnp.pad(cm, front).reshape(Bsz, Lp, SSD_GROUPS, D_STATE)
    dt_p = jnp.pad(dt, front)
    y = ssd_chunked(xs_p, dt_p, a, bm_p, cm_p)
    y = y + d_skip.astype(jnp.float32)[:, None] * xs_p.astype(jnp.float32)
    y = y[:, PAD:].reshape(Bsz, L, D_SSD).astype(h.dtype)
    y_ssd = gated_group_rmsnorm(y, z_ssd, ssd_norm)

    logf = jax.nn.log_sigmoid(f_raw.astype(jnp.float32) + fgate_bias.astype(jnp.float32))
    back = ((0, 0), (0, PAD), (0, 0))
    qh = jnp.pad(q, back).reshape(Bsz, Lp, H_ATT, ATT_HEAD_DIM)
    kh = jnp.pad(k, back).reshape(Bsz, Lp, H_ATT, ATT_HEAD_DIM)
    vh = jnp.pad(v, back).reshape(Bsz, Lp, H_ATT, ATT_HEAD_DIM)
    o = forgetting_attention(qh, kh, vh, jnp.pad(logf, back))
    o = o[:, :L].reshape(Bsz, L, D_ATT)
    y_att = o * jax.nn.silu(z_att)

    gates = jax.nn.sigmoid(g_raw + gate_bias)
    g_ssd, g_att = jnp.split(gates, 2, axis=-1)
    merged = g_ssd * (y_ssd @ w_proj_ssd) + g_att * (y_att @ w_proj_att)
    return h + rmsnorm(merged @ w_out, norm_post)


def _fwd_setup_inputs(seed: int = 0) -> dict:
    key = jax.random.key(seed)
    ks = jax.random.split(key, 16)
    D = D_MODEL
    nrm = jax.random.normal
    x = nrm(ks[0], (BATCH, SEQ, D), jnp.float32)
    meta_tokens = nrm(ks[1], (N_META, D), jnp.float32)
    norm_pre = 1.0 + 0.05 * nrm(ks[2], (DEPTH, D), jnp.float32)
    w_in = nrm(ks[3], (DEPTH, D, N_COLS), jnp.float32) * D ** -0.5
    conv_w = jax.random.uniform(ks[4], (DEPTH, CONV_K, CONV_DIM), jnp.float32, -0.5, 0.5)
    conv_b = 0.05 * nrm(ks[5], (DEPTH, CONV_DIM), jnp.float32)
    dt0 = jnp.exp(jax.random.uniform(ks[6], (DEPTH, H_SSD), jnp.float32,
                                     math.log(1e-3), math.log(1e-1)))
    dt_bias = dt0 + jnp.log(-jnp.expm1(-dt0))
    a_log = jnp.log(jax.random.uniform(ks[7], (DEPTH, H_SSD), jnp.float32, 1.0, 16.0))
    d_skip = 1.0 + 0.1 * nrm(ks[8], (DEPTH, H_SSD), jnp.float32)
    ssd_norm = 1.0 + 0.05 * nrm(ks[9], (DEPTH, D_SSD), jnp.float32)
    fgate_bias = jax.random.uniform(ks[10], (DEPTH, H_ATT), jnp.float32, 1.0, 6.0)
    gate_bias = 0.1 * nrm(ks[11], (DEPTH, 2 * D), jnp.float32)
    w_proj_ssd = nrm(ks[12], (DEPTH, D_SSD, D), jnp.float32) * D_SSD ** -0.5
    w_proj_att = nrm(ks[13], (DEPTH, D_ATT, D), jnp.float32) * D_ATT ** -0.5
    w_out = nrm(ks[14], (DEPTH, D, D), jnp.float32) * D ** -0.5
    norm_post = 1.0 + 0.05 * nrm(ks[15], (DEPTH, D), jnp.float32)
    return {"x": x, "meta_tokens": meta_tokens, "norm_pre": norm_pre, "w_in": w_in,
            "conv_w": conv_w, "conv_b": conv_b, "dt_bias": dt_bias, "a_log": a_log,
            "d_skip": d_skip, "ssd_norm": ssd_norm, "fgate_bias": fgate_bias,
            "gate_bias": gate_bias, "w_proj_ssd": w_proj_ssd, "w_proj_att": w_proj_att,
            "w_out": w_out, "norm_post": norm_post}


def _fwd_reference(x, meta_tokens, norm_pre, w_in, conv_w, conv_b, dt_bias, a_log, d_skip,
              ssd_norm, fgate_bias, gate_bias, w_proj_ssd, w_proj_att, w_out, norm_post):
    Bsz = x.shape[0]
    meta = jnp.broadcast_to(meta_tokens[None].astype(x.dtype), (Bsz, N_META, D_MODEL))
    h = jnp.concatenate([meta, x], axis=1)
    for i in range(DEPTH):
        h = hybrid_layer(h, norm_pre[i], w_in[i], conv_w[i], conv_b[i], dt_bias[i], a_log[i],
                         d_skip[i], ssd_norm[i], fgate_bias[i], gate_bias[i], w_proj_ssd[i],
                         w_proj_att[i], w_out[i], norm_post[i])
    return h[:, N_META:]


import jax as _jax
import jax.numpy as _jnp

TWIN_FORMAT = 'train_step'
FWD_PARAMS = ['x', 'meta_tokens', 'norm_pre', 'w_in', 'conv_w', 'conv_b', 'dt_bias', 'a_log', 'd_skip', 'ssd_norm', 'fgate_bias', 'gate_bias', 'w_proj_ssd', 'w_proj_att', 'w_out', 'norm_post']
TWIN_WEIGHTS = ['meta_tokens', 'norm_pre', 'w_in', 'conv_w', 'conv_b', 'dt_bias', 'a_log', 'd_skip', 'ssd_norm', 'fgate_bias', 'gate_bias', 'w_proj_ssd', 'w_proj_att', 'w_out', 'norm_post']
TWIN_DIFF_INPUT = 'x'
TWIN_INPUTS = ['x', 'meta_tokens', 'norm_pre', 'w_in', 'conv_w', 'conv_b', 'dt_bias', 'a_log', 'd_skip', 'ssd_norm', 'fgate_bias', 'gate_bias', 'w_proj_ssd', 'w_proj_att', 'w_out', 'norm_post', 'loss_target', 'm_meta_tokens', 'm_norm_pre', 'm_w_in', 'm_conv_w', 'm_conv_b', 'm_dt_bias', 'm_a_log', 'm_d_skip', 'm_ssd_norm', 'm_fgate_bias', 'm_gate_bias', 'm_w_proj_ssd', 'm_w_proj_att', 'm_w_out', 'm_norm_post', 'v_meta_tokens', 'v_norm_pre', 'v_w_in', 'v_conv_w', 'v_conv_b', 'v_dt_bias', 'v_a_log', 'v_d_skip', 'v_ssd_norm', 'v_fgate_bias', 'v_gate_bias', 'v_w_proj_ssd', 'v_w_proj_att', 'v_w_out', 'v_norm_post']
TWIN_OUTPUTS = ['loss', 'grad_x', 'grad_meta_tokens', 'grad_norm_pre', 'grad_w_in', 'grad_conv_w', 'grad_conv_b', 'grad_dt_bias', 'grad_a_log', 'grad_d_skip', 'grad_ssd_norm', 'grad_fgate_bias', 'grad_gate_bias', 'grad_w_proj_ssd', 'grad_w_proj_att', 'grad_w_out', 'grad_norm_post', 'delta_meta_tokens', 'delta_norm_pre', 'delta_w_in', 'delta_conv_w', 'delta_conv_b', 'delta_dt_bias', 'delta_a_log', 'delta_d_skip', 'delta_ssd_norm', 'delta_fgate_bias', 'delta_gate_bias', 'delta_w_proj_ssd', 'delta_w_proj_att', 'delta_w_out', 'delta_norm_post', 'new_m_meta_tokens', 'new_m_norm_pre', 'new_m_w_in', 'new_m_conv_w', 'new_m_conv_b', 'new_m_dt_bias', 'new_m_a_log', 'new_m_d_skip', 'new_m_ssd_norm', 'new_m_fgate_bias', 'new_m_gate_bias', 'new_m_w_proj_ssd', 'new_m_w_proj_att', 'new_m_w_out', 'new_m_norm_post', 'new_v_meta_tokens', 'new_v_norm_pre', 'new_v_w_in', 'new_v_conv_w', 'new_v_conv_b', 'new_v_dt_bias', 'new_v_a_log', 'new_v_d_skip', 'new_v_ssd_norm', 'new_v_fgate_bias', 'new_v_gate_bias', 'new_v_w_proj_ssd', 'new_v_w_proj_att', 'new_v_w_out', 'new_v_norm_post']
TWIN_LEAF_KINDS = {'loss': 'loss', 'grad_x': 'grad_x', 'grad_meta_tokens': 'grad_w', 'grad_norm_pre': 'grad_w', 'grad_w_in': 'grad_w', 'grad_conv_w': 'grad_w', 'grad_conv_b': 'grad_w', 'grad_dt_bias': 'grad_w', 'grad_a_log': 'grad_w', 'grad_d_skip': 'grad_w', 'grad_ssd_norm': 'grad_w', 'grad_fgate_bias': 'grad_w', 'grad_gate_bias': 'grad_w', 'grad_w_proj_ssd': 'grad_w', 'grad_w_proj_att': 'grad_w', 'grad_w_out': 'grad_w', 'grad_norm_post': 'grad_w', 'delta_meta_tokens': 'delta_w', 'delta_norm_pre': 'delta_w', 'delta_w_in': 'delta_w', 'delta_conv_w': 'delta_w', 'delta_conv_b': 'delta_w', 'delta_dt_bias': 'delta_w', 'delta_a_log': 'delta_w', 'delta_d_skip': 'delta_w', 'delta_ssd_norm': 'delta_w', 'delta_fgate_bias': 'delta_w', 'delta_gate_bias': 'delta_w', 'delta_w_proj_ssd': 'delta_w', 'delta_w_proj_att': 'delta_w', 'delta_w_out': 'delta_w', 'delta_norm_post': 'delta_w', 'new_m_meta_tokens': 'new_m', 'new_m_norm_pre': 'new_m', 'new_m_w_in': 'new_m', 'new_m_conv_w': 'new_m', 'new_m_conv_b': 'new_m', 'new_m_dt_bias': 'new_m', 'new_m_a_log': 'new_m', 'new_m_d_skip': 'new_m', 'new_m_ssd_norm': 'new_m', 'new_m_fgate_bias': 'new_m', 'new_m_gate_bias': 'new_m', 'new_m_w_proj_ssd': 'new_m', 'new_m_w_proj_att': 'new_m', 'new_m_w_out': 'new_m', 'new_m_norm_post': 'new_m', 'new_v_meta_tokens': 'new_v', 'new_v_norm_pre': 'new_v', 'new_v_w_in': 'new_v', 'new_v_conv_w': 'new_v', 'new_v_conv_b': 'new_v', 'new_v_dt_bias': 'new_v', 'new_v_a_log': 'new_v', 'new_v_d_skip': 'new_v', 'new_v_ssd_norm': 'new_v', 'new_v_fgate_bias': 'new_v', 'new_v_gate_bias': 'new_v', 'new_v_w_proj_ssd': 'new_v', 'new_v_w_proj_att': 'new_v', 'new_v_w_out': 'new_v', 'new_v_norm_post': 'new_v'}


def _forward(args):
    return _fwd_reference(*[args[k] for k in FWD_PARAMS])


def _output_shape():
    out = _jax.eval_shape(lambda: _forward(_fwd_setup_inputs(0)))
    return out.shape, out.dtype

N_MICROBATCH = 1
ADAM_LR = 0.001
ADAM_B1 = 0.9
ADAM_B2 = 0.999
ADAM_EPS = 1e-08
ADAM_WD = 0.01
ADAM_STEP = 10
PER_EXAMPLE_BATCH_AXIS = {'x': 0, 'loss_target': 0}
SHARED_INPUTS = []
_WEIGHT_DTYPES = {'meta_tokens': _jnp.float32, 'norm_pre': _jnp.float32, 'w_in': _jnp.float32, 'conv_w': _jnp.float32, 'conv_b': _jnp.float32, 'dt_bias': _jnp.float32, 'a_log': _jnp.float32, 'd_skip': _jnp.float32, 'ssd_norm': _jnp.float32, 'fgate_bias': _jnp.float32, 'gate_bias': _jnp.float32, 'w_proj_ssd': _jnp.float32, 'w_proj_att': _jnp.float32, 'w_out': _jnp.float32, 'norm_post': _jnp.float32}
MOMENT_SCALE = {'meta_tokens': 6.052413e-03, 'norm_pre': 3.856263e-01, 'w_in': 1.239238e-01, 'conv_w': 2.867880e-01, 'conv_b': 5.780744e-01, 'dt_bias': 1.962016e-01, 'a_log': 2.203945e-01, 'd_skip': 1.217739e+00, 'ssd_norm': 2.043672e-01, 'fgate_bias': 3.217772e-01, 'gate_bias': 8.775647e-02, 'w_proj_ssd': 2.915138e-01, 'w_proj_att': 4.448441e-02, 'w_out': 2.996008e-01, 'norm_post': 3.200987e+01}


def _to_microbatches(a, axis):
    t = _jnp.moveaxis(a, axis, 0)
    t = t.reshape((N_MICROBATCH, t.shape[0] // N_MICROBATCH) + t.shape[1:])
    return _jnp.moveaxis(t, 1, axis + 1)


def setup_inputs(seed: int = 0) -> dict:
    inp = _fwd_setup_inputs(seed)
    key = _jax.random.fold_in(_jax.random.key(seed), 7919)
    shape, _ = _output_shape()
    out = dict(inp)
    out["loss_target"] = _jax.random.normal(_jax.random.fold_in(key, 0), shape, _jnp.float32)
    for i, name in enumerate(TWIN_WEIGHTS):
        w = inp[name].astype(_jnp.float32)
        if MOMENT_SCALE is None:
            s = _jnp.sqrt(_jnp.mean(_jnp.square(w)) + 1e-30)
        else:
            s = MOMENT_SCALE[name]
        km, kv = _jax.random.split(_jax.random.fold_in(key, i + 1))
        out[name] = w
        out["m_" + name] = s * _jax.random.normal(km, w.shape, _jnp.float32)
        out["v_" + name] = (s * s) * _jax.random.uniform(kv, w.shape, _jnp.float32, 0.5, 1.5)
    if N_MICROBATCH > 1:
        for name, axis in PER_EXAMPLE_BATCH_AXIS.items():
            out[name] = _to_microbatches(out[name], axis)
    return {'x': out['x'], 'meta_tokens': out['meta_tokens'], 'norm_pre': out['norm_pre'], 'w_in': out['w_in'], 'conv_w': out['conv_w'], 'conv_b': out['conv_b'], 'dt_bias': out['dt_bias'], 'a_log': out['a_log'], 'd_skip': out['d_skip'], 'ssd_norm': out['ssd_norm'], 'fgate_bias': out['fgate_bias'], 'gate_bias': out['gate_bias'], 'w_proj_ssd': out['w_proj_ssd'], 'w_proj_att': out['w_proj_att'], 'w_out': out['w_out'], 'norm_post': out['norm_post'], 'loss_target': out['loss_target'], 'm_meta_tokens': out['m_meta_tokens'], 'm_norm_pre': out['m_norm_pre'], 'm_w_in': out['m_w_in'], 'm_conv_w': out['m_conv_w'], 'm_conv_b': out['m_conv_b'], 'm_dt_bias': out['m_dt_bias'], 'm_a_log': out['m_a_log'], 'm_d_skip': out['m_d_skip'], 'm_ssd_norm': out['m_ssd_norm'], 'm_fgate_bias': out['m_fgate_bias'], 'm_gate_bias': out['m_gate_bias'], 'm_w_proj_ssd': out['m_w_proj_ssd'], 'm_w_proj_att': out['m_w_proj_att'], 'm_w_out': out['m_w_out'], 'm_norm_post': out['m_norm_post'], 'v_meta_tokens': out['v_meta_tokens'], 'v_norm_pre': out['v_norm_pre'], 'v_w_in': out['v_w_in'], 'v_conv_w': out['v_conv_w'], 'v_conv_b': out['v_conv_b'], 'v_dt_bias': out['v_dt_bias'], 'v_a_log': out['v_a_log'], 'v_d_skip': out['v_d_skip'], 'v_ssd_norm': out['v_ssd_norm'], 'v_fgate_bias': out['v_fgate_bias'], 'v_gate_bias': out['v_gate_bias'], 'v_w_proj_ssd': out['v_w_proj_ssd'], 'v_w_proj_att': out['v_w_proj_att'], 'v_w_out': out['v_w_out'], 'v_norm_post': out['v_norm_post']}


def _loss(weights, diff, rest, loss_target):
    with _jax.named_scope("forward"):
        args = {**rest, TWIN_DIFF_INPUT: diff, **{k: w.astype(_WEIGHT_DTYPES[k]) for k, w in weights.items()}}
        y = _forward(args)
    with _jax.named_scope("loss_head"):
        err = _jnp.square(y.astype(_jnp.float32) - loss_target)
        return 0.5 * _jnp.sum(_jnp.mean(err, axis=-1)) if err.ndim else 0.5 * err


def _adamw(w, g, m, v):
    m = ADAM_B1 * m + (1.0 - ADAM_B1) * g
    v = ADAM_B2 * v + (1.0 - ADAM_B2) * _jnp.square(g)
    m_hat = m / (1.0 - ADAM_B1 ** ADAM_STEP)
    v_hat = v / (1.0 - ADAM_B2 ** ADAM_STEP)
    delta = -ADAM_LR * (m_hat / (_jnp.sqrt(v_hat) + ADAM_EPS) + ADAM_WD * w)
    return delta, m, v


def reference(x, meta_tokens, norm_pre, w_in, conv_w, conv_b, dt_bias, a_log, d_skip, ssd_norm, fgate_bias, gate_bias, w_proj_ssd, w_proj_att, w_out, norm_post, loss_target, m_meta_tokens, m_norm_pre, m_w_in, m_conv_w, m_conv_b, m_dt_bias, m_a_log, m_d_skip, m_ssd_norm, m_fgate_bias, m_gate_bias, m_w_proj_ssd, m_w_proj_att, m_w_out, m_norm_post, v_meta_tokens, v_norm_pre, v_w_in, v_conv_w, v_conv_b, v_dt_bias, v_a_log, v_d_skip, v_ssd_norm, v_fgate_bias, v_gate_bias, v_w_proj_ssd, v_w_proj_att, v_w_out, v_norm_post):
    given = dict(x=x, meta_tokens=meta_tokens, norm_pre=norm_pre, w_in=w_in, conv_w=conv_w, conv_b=conv_b, dt_bias=dt_bias, a_log=a_log, d_skip=d_skip, ssd_norm=ssd_norm, fgate_bias=fgate_bias, gate_bias=gate_bias, w_proj_ssd=w_proj_ssd, w_proj_att=w_proj_att, w_out=w_out, norm_post=norm_post, loss_target=loss_target, m_meta_tokens=m_meta_tokens, m_norm_pre=m_norm_pre, m_w_in=m_w_in, m_conv_w=m_conv_w, m_conv_b=m_conv_b, m_dt_bias=m_dt_bias, m_a_log=m_a_log, m_d_skip=m_d_skip, m_ssd_norm=m_ssd_norm, m_fgate_bias=m_fgate_bias, m_gate_bias=m_gate_bias, m_w_proj_ssd=m_w_proj_ssd, m_w_proj_att=m_w_proj_att, m_w_out=m_w_out, m_norm_post=m_norm_post, v_meta_tokens=v_meta_tokens, v_norm_pre=v_norm_pre, v_w_in=v_w_in, v_conv_w=v_conv_w, v_conv_b=v_conv_b, v_dt_bias=v_dt_bias, v_a_log=v_a_log, v_d_skip=v_d_skip, v_ssd_norm=v_ssd_norm, v_fgate_bias=v_fgate_bias, v_gate_bias=v_gate_bias, v_w_proj_ssd=v_w_proj_ssd, v_w_proj_att=v_w_proj_att, v_w_out=v_w_out, v_norm_post=v_norm_post)
    weights = {n: given[n] for n in TWIN_WEIGHTS}
    shared = {n: given[n] for n in SHARED_INPUTS}
    per_example = {n: given[n] for n in ['x']}
    grad_fn = _jax.value_and_grad(_loss, argnums=(0, 1))

    def one_microbatch(ex, loss_target):
        ex = dict(ex)
        diff = ex.pop(TWIN_DIFF_INPUT)
        return grad_fn(weights, diff, {**shared, **ex}, loss_target)

    if N_MICROBATCH == 1:
        loss, (grad_w, grad_x) = one_microbatch(per_example, given["loss_target"])
    else:
        def body(carry, xs):
            loss_sum, grad_sum = carry
            l_k, (gw_k, gx_k) = one_microbatch(xs[0], xs[1])
            with _jax.named_scope("update"):
                return (loss_sum + l_k, _jax.tree.map(_jnp.add, grad_sum, gw_k)), gx_k

        init = (_jnp.zeros((), _jnp.float32), _jax.tree.map(_jnp.zeros_like, weights))
        (loss, grad_w), grad_x = _jax.lax.scan(body, init, (per_example, given["loss_target"]))
    with _jax.named_scope("update"):
        delta_w, new_m, new_v = {}, {}, {}
        for n in TWIN_WEIGHTS:
            delta_w[n], new_m[n], new_v[n] = _adamw(weights[n], grad_w[n], given["m_" + n], given["v_" + n])
    return (loss, grad_x, *[grad_w[n] for n in TWIN_WEIGHTS], *[delta_w[n] for n in TWIN_WEIGHTS],
            *[new_m[n] for n in TWIN_WEIGHTS], *[new_v[n] for n in TWIN_WEIGHTS])
```

```python
import functools
import math

import jax
import jax.numpy as jnp
from jax import lax
from jax.experimental import pallas as pl
from jax.experimental.pallas import tpu as pltpu

F32 = jnp.float32
BF16 = jnp.bfloat16
HIGHEST = lax.Precision.HIGHEST

D_MODEL = 1024
N_META = 16
CHUNK = 128
PADF = CHUNK - N_META
D_SSD = 2048
H_SSD = 32
G_SSD = 4
N_STATE = 128
CONV_K = 4
CONV_DIM = D_SSD + 2 * G_SSD * N_STATE
H_ATT = 16
D_ATT = 1024
EPS = 1e-6
N_COLS = 11312

C_Z, C_XBC, C_ZA, C_Q, C_K, C_V, C_G = 0, 2048, 5120, 6144, 7168, 8192, 9216
N_MAIN = 11264
N_SMALL = 128
O_Z, O_XBC, O_DT, O_ZA, O_Q, O_K, O_V, O_F, O_G = (
    (0, 2048), (2048, 3072), (5120, 32), (5152, 1024), (6176, 1024), (7200, 1024),
    (8224, 1024), (9248, 16), (9264, 2048))

ADAM_LR, ADAM_B1, ADAM_B2, ADAM_EPS, ADAM_WD, ADAM_STEP = 0.001, 0.9, 0.999, 1e-08, 0.01, 10

VMEM_LIMIT = 56 * 1024 * 1024


def _cp(*sem):
    return pltpu.CompilerParams(dimension_semantics=sem, vmem_limit_bytes=VMEM_LIMIT)


def _tile(n, prefs):
    for p in prefs:
        if n % p == 0:
            return p
    raise ValueError(f"no tile for {n} in {prefs}")


def _iota(shape, dim):
    return lax.broadcasted_iota(jnp.int32, shape, dim)


def _sigmoid(x):
    return 1.0 / (1.0 + jnp.exp(-x))


def _softplus_tail(x):
    return jnp.log(1.0 + jnp.exp(-jnp.abs(x)))


_NN = (((1,), (0,)), ((), ()))
_NT = (((1,), (1,)), ((), ()))
_TN = (((0,), (0,)), ((), ()))


def _dot(a, b, dims=_NN):
    return lax.dot_general(a, b, dims, preferred_element_type=F32)


def _dot_exact(a, b, dims=_NN):
    return lax.dot_general(a, b, dims, precision=HIGHEST, preferred_element_type=F32)


def _matmul(a, b, mode, out_dtype, name, tm, tn, tk):
    if mode == "tn":
        kdim, m = a.shape
    else:
        m, kdim = a.shape
    n = b.shape[0] if mode == "nt" else b.shape[1]
    nk = kdim // tk
    dims = {"nn": _NN, "nt": _NT, "tn": _TN}[mode]
    a_spec = (pl.BlockSpec((tk, tm), lambda i, j, k: (k, i)) if mode == "tn"
              else pl.BlockSpec((tm, tk), lambda i, j, k: (i, k)))
    b_spec = (pl.BlockSpec((tn, tk), lambda i, j, k: (j, k)) if mode == "nt"
              else pl.BlockSpec((tk, tn), lambda i, j, k: (k, j)))

    def body(a_ref, b_ref, o_ref, acc_ref):
        k = pl.program_id(2)
        p = _dot(a_ref[...].astype(BF16), b_ref[...].astype(BF16), dims)
        if nk == 1:
            o_ref[...] = p.astype(out_dtype)
        else:
            @pl.when(k == 0)
            def _():
                acc_ref[...] = p

            @pl.when(k > 0)
            def _():
                acc_ref[...] += p

            @pl.when(k == nk - 1)
            def _():
                o_ref[...] = acc_ref[...].astype(out_dtype)

    return pl.pallas_call(
        body, name=name,
        out_shape=jax.ShapeDtypeStruct((m, n), out_dtype),
        grid=(m // tm, n // tn, nk),
        in_specs=[a_spec, b_spec],
        out_specs=pl.BlockSpec((tm, tn), lambda i, j, k: (i, j)),
        scratch_shapes=[pltpu.VMEM((tm, tn), F32)],
        compiler_params=_cp("parallel", "parallel", "arbitrary"),
    )(a, b)


def _row_tile(t):
    return _tile(t, (352, 128))


def _row_tile_wide(t):
    return _tile(t, (176, 128))


def _norm1_fwd(h, g):
    t = h.shape[0]
    tm = _row_tile(t)

    def body(h_ref, g_ref, u_ref):
        x = h_ref[...]
        r = lax.rsqrt(jnp.mean(x * x, axis=-1, keepdims=True) + EPS)
        u_ref[...] = (x * r * g_ref[...]).astype(BF16)

    return pl.pallas_call(
        body, name="norm1_fwd",
        out_shape=jax.ShapeDtypeStruct((t, D_MODEL), BF16),
        grid=(t // tm,),
        in_specs=[pl.BlockSpec((tm, D_MODEL), lambda i: (i, 0)),
                  pl.BlockSpec((1, D_MODEL), lambda i: (0, 0))],
        out_specs=pl.BlockSpec((tm, D_MODEL), lambda i: (i, 0)),
        compiler_params=_cp("parallel"),
    )(h, g)


def _norm1_bwd(du_a, du_b, h, g, dy):
    t = h.shape[0]
    tm = _row_tile(t)

    def body(a_ref, b_ref, h_ref, g_ref, dy_ref, dh_ref, dg_ref):
        i = pl.program_id(0)
        x = h_ref[...]
        du = a_ref[...] + b_ref[...]
        r = lax.rsqrt(jnp.mean(x * x, axis=-1, keepdims=True) + EPS)
        gdu = du * g_ref[...]
        dh_ref[...] = dy_ref[...] + r * (gdu - x * (r * r) * jnp.mean(gdu * x, axis=-1, keepdims=True))
        part = jnp.sum(du * x * r, axis=0, keepdims=True)

        @pl.when(i == 0)
        def _():
            dg_ref[...] = part

        @pl.when(i > 0)
        def _():
            dg_ref[...] += part

    row = pl.BlockSpec((tm, D_MODEL), lambda i: (i, 0))
    vec = pl.BlockSpec((1, D_MODEL), lambda i: (0, 0))
    return pl.pallas_call(
        body, name="norm1_bwd",
        out_shape=(jax.ShapeDtypeStruct((t, D_MODEL), F32), jax.ShapeDtypeStruct((1, D_MODEL), F32)),
        grid=(t // tm,),
        in_specs=[row, row, row, vec, row],
        out_specs=(row, vec),
        compiler_params=_cp("arbitrary"),
    )(du_a, du_b, h, g, dy)


def _small_fwd(small, bias_row):
    t = small.shape[0]

    def body(s_ref, b_ref, o_ref, carry_ref):
        c = pl.program_id(0)

        @pl.when(c == 0)
        def _():
            carry_ref[...] = jnp.zeros_like(carry_ref)

        x = s_ref[...] + b_ref[...]
        r0 = _iota((CHUNK, CHUNK), 0)
        r1 = _iota((CHUNK, CHUNK), 1)
        valid = (c * CHUNK + r0) >= PADF
        tail = _softplus_tail(x)
        dt = jnp.where(valid & (r1 < H_SSD), jnp.maximum(x, 0.0) + tail, 0.0)
        lf = jnp.where(valid & (r1 >= H_SSD) & (r1 < H_SSD + H_ATT), jnp.minimum(x, 0.0) - tail, 0.0)
        tri = (r0 >= r1).astype(F32)
        cs = _dot_exact(tri, lf) + carry_ref[...]
        carry_ref[...] = cs[CHUNK - 1:CHUNK, :]
        o_ref[...] = dt + cs

    return pl.pallas_call(
        body, name="small_fwd",
        out_shape=jax.ShapeDtypeStruct((t, N_SMALL), F32),
        grid=(t // CHUNK,),
        in_specs=[pl.BlockSpec((CHUNK, N_SMALL), lambda c: (c, 0)),
                  pl.BlockSpec((1, N_SMALL), lambda c: (0, 0))],
        out_specs=pl.BlockSpec((CHUNK, N_SMALL), lambda c: (c, 0)),
        scratch_shapes=[pltpu.VMEM((1, N_SMALL), F32)],
        compiler_params=_cp("arbitrary"),
    )(small, bias_row)


def _small_bwd(dsm, small, bias_row):
    t = small.shape[0]
    nc = t // CHUNK

    def body(d_ref, s_ref, b_ref, o_ref, db_ref, carry_ref):
        step = pl.program_id(0)
        c = nc - 1 - step

        @pl.when(step == 0)
        def _():
            carry_ref[...] = jnp.zeros_like(carry_ref)
            db_ref[...] = jnp.zeros_like(db_ref)

        x = s_ref[...] + b_ref[...]
        d = d_ref[...]
        r0 = _iota((CHUNK, CHUNK), 0)
        r1 = _iota((CHUNK, CHUNK), 1)
        valid = (c * CHUNK + r0) >= PADF
        is_dt = r1 < H_SSD
        is_f = (r1 >= H_SSD) & (r1 < H_SSD + H_ATT)
        triu = (r1 >= r0).astype(F32)
        dc = jnp.where(is_f, d, 0.0)
        dlf = _dot_exact(triu, dc) + carry_ref[...]
        carry_ref[...] = dlf[0:1, :]
        sg = _sigmoid(x)
        out = jnp.where(valid & is_dt, d * sg, 0.0) + jnp.where(valid & is_f, dlf * (1.0 - sg), 0.0)
        o_ref[...] = out.astype(BF16)
        db_ref[...] += jnp.sum(out, axis=0, keepdims=True)

    blk = pl.BlockSpec((CHUNK, N_SMALL), lambda s: (nc - 1 - s, 0))
    vec = pl.BlockSpec((1, N_SMALL), lambda s: (0, 0))
    return pl.pallas_call(
        body, name="small_bwd",
        out_shape=(jax.ShapeDtypeStruct((t, N_SMALL), BF16), jax.ShapeDtypeStruct((1, N_SMALL), F32)),
        grid=(nc,),
        in_specs=[blk, blk, vec],
        out_specs=(blk, vec),
        scratch_shapes=[pltpu.VMEM((1, N_SMALL), F32)],
        compiler_params=_cp("arbitrary"),
    )(dsm, small, bias_row)


_CONV_TC = 1024
_XBC_BLK = C_XBC // _CONV_TC


def _shift_down(cur, prev8, j):
    rc = pltpu.roll(cur, j, 0)
    rid = _iota(prev8.shape, 0)
    top = jnp.where(rid < j, pltpu.roll(prev8, j, 0), rc[0:8, :])
    return jnp.concatenate([top, rc[8:, :]], axis=0)


def _shift_up(cur, next8, j):
    n = cur.shape[0]
    ru = pltpu.roll(cur, n - j, 0)
    rid = _iota(next8.shape, 0)
    bot = jnp.where(rid >= 8 - j, pltpu.roll(next8, 8 - j, 0), ru[n - 8:, :])
    return jnp.concatenate([ru[:n - 8, :], bot], axis=0)


def _conv_pre(x_ref, p_ref, w_ref, b_ref, i):
    cur = x_ref[...]
    prev = jnp.where(i > 0, p_ref[...], 0.0)
    w = w_ref[...]
    taps = [cur] + [_shift_down(cur, prev, j) for j in (1, 2, 3)]
    acc = b_ref[...] + taps[0] * w[3:4, :]
    for j in (1, 2, 3):
        acc = acc + taps[j] * w[3 - j:4 - j, :]
    return acc, taps


def _conv_fwd(proj, conv_w, conv_b):
    t = proj.shape[0]
    tr = _row_tile(t)

    def body(x_ref, p_ref, w_ref, b_ref, o_ref):
        i = pl.program_id(0)
        acc, _ = _conv_pre(x_ref, p_ref, w_ref, b_ref, i)
        valid = (i * tr + _iota(acc.shape, 0)) >= PADF
        o_ref[...] = jnp.where(valid, acc * _sigmoid(acc), 0.0)

    return pl.pallas_call(
        body, name="conv_fwd",
        out_shape=jax.ShapeDtypeStruct((t, CONV_DIM), F32),
        grid=(t // tr, CONV_DIM // _CONV_TC),
        in_specs=[pl.BlockSpec((tr, _CONV_TC), lambda i, j: (i, _XBC_BLK + j)),
                  pl.BlockSpec((8, _CONV_TC), lambda i, j: (jnp.maximum(i * (tr // 8) - 1, 0), _XBC_BLK + j)),
                  pl.BlockSpec((CONV_K, _CONV_TC), lambda i, j: (0, j)),
                  pl.BlockSpec((1, _CONV_TC), lambda i, j: (0, j))],
        out_specs=pl.BlockSpec((tr, _CONV_TC), lambda i, j: (i, j)),
        compiler_params=_cp("parallel", "parallel"),
    )(proj, proj, conv_w, conv_b)


def _conv_bwd_act(dxbc, proj, conv_w, conv_b):
    t = proj.shape[0]
    tr = _row_tile(t)

    def body(d_ref, x_ref, p_ref, w_ref, b_ref, da_ref, dw_ref, db_ref):
        i = pl.program_id(1)
        acc, taps = _conv_pre(x_ref, p_ref, w_ref, b_ref, i)
        valid = (i * tr + _iota(acc.shape, 0)) >= PADF
        sg = _sigmoid(acc)
        da = jnp.where(valid, d_ref[...] * sg * (1.0 + acc * (1.0 - sg)), 0.0)
        da_ref[...] = da
        dw = jnp.concatenate([jnp.sum(da * taps[3 - k], axis=0, keepdims=True) for k in range(CONV_K)], axis=0)
        db = jnp.sum(da, axis=0, keepdims=True)

        @pl.when(i == 0)
        def _():
            dw_ref[...] = dw
            db_ref[...] = db

        @pl.when(i > 0)
        def _():
            dw_ref[...] += dw
            db_ref[...] += db

    return pl.pallas_call(
        body, name="conv_bwd_act",
        out_shape=(jax.ShapeDtypeStruct((t, CONV_DIM), F32),
                   jax.ShapeDtypeStruct((CONV_K, CONV_DIM), F32),
                   jax.ShapeDtypeStruct((1, CONV_DIM), F32)),
        grid=(CONV_DIM // _CONV_TC, t // tr),
        in_specs=[pl.BlockSpec((tr, _CONV_TC), lambda j, i: (i, j)),
                  pl.BlockSpec((tr, _CONV_TC), lambda j, i: (i, _XBC_BLK + j)),
                  pl.BlockSpec((8, _CONV_TC), lambda j, i: (jnp.maximum(i * (tr // 8) - 1, 0), _XBC_BLK + j)),
                  pl.BlockSpec((CONV_K, _CONV_TC), lambda j, i: (0, j)),
                  pl.BlockSpec((1, _CONV_TC), lambda j, i: (0, j))],
        out_specs=(pl.BlockSpec((tr, _CONV_TC), lambda j, i: (i, j)),
                   pl.BlockSpec((CONV_K, _CONV_TC), lambda j, i: (0, j)),
                   pl.BlockSpec((1, _CONV_TC), lambda j, i: (0, j))),
        compiler_params=_cp("parallel", "arbitrary"),
    )(dxbc, proj, proj, conv_w, conv_b)


def _conv_bwd_in(da, conv_w):
    t = da.shape[0]
    tr = _row_tile(t)
    last8 = t // 8 - 1

    def body(d_ref, n_ref, w_ref, o_ref):
        i = pl.program_id(0)
        cur = d_ref[...]
        nxt = jnp.where(i < pl.num_programs(0) - 1, n_ref[...], 0.0)
        w = w_ref[...]
        acc = cur * w[3:4, :]
        for j in (1, 2, 3):
            acc = acc + _shift_up(cur, nxt, j) * w[3 - j:4 - j, :]
        o_ref[...] = acc.astype(BF16)

    return pl.pallas_call(
        body, name="conv_bwd_in",
        out_shape=jax.ShapeDtypeStruct((t, CONV_DIM), BF16),
        grid=(t // tr, CONV_DIM // _CONV_TC),
        in_specs=[pl.BlockSpec((tr, _CONV_TC), lambda i, j: (i, j)),
                  pl.BlockSpec((8, _CONV_TC), lambda i, j: (jnp.minimum((i + 1) * (tr // 8), last8), j)),
                  pl.BlockSpec((CONV_K, _CONV_TC), lambda i, j: (0, j))],
        out_specs=pl.BlockSpec((tr, _CONV_TC), lambda i, j: (i, j)),
        compiler_params=_cp("parallel", "parallel"),
    )(da, da, conv_w)


_GW = D_SSD // G_SSD


def _ssd_prelude(dt_ref, a_ref, e_scr, es_scr, dte_scr):
    r0 = _iota((CHUNK, CHUNK), 0)
    r1 = _iota((CHUNK, CHUNK), 1)
    dt = jnp.where(r1 < H_SSD, dt_ref[...], 0.0)
    adt = dt * a_ref[...]
    acs = _dot_exact((r0 >= r1).astype(F32), adt)
    acs_t = acs.T
    alast = acs[CHUNK - 1:CHUNK, :]
    exp_a = jnp.exp(acs)
    dec_s = jnp.exp(alast - acs)
    lo = r1 < 64
    for j in range(H_SSD // 2):
        sl = slice(CHUNK * j, CHUNK * (j + 1))
        e_scr[:, sl] = jnp.where(lo, exp_a[:, 2 * j:2 * j + 1], exp_a[:, 2 * j + 1:2 * j + 2])
        es_scr[:, sl] = jnp.where(lo, dec_s[:, 2 * j:2 * j + 1], dec_s[:, 2 * j + 1:2 * j + 2])
        dte_scr[:, sl] = jnp.where(lo, dt[:, 2 * j:2 * j + 1], dt[:, 2 * j + 1:2 * j + 2])
    return dt, acs, acs_t, r0, r1, lo


def _chunk_decay_rows(acs_t, g):
    cd_t = jnp.exp(acs_t[:, CHUNK - 1:CHUNK])
    return jnp.concatenate(
        [jnp.broadcast_to(cd_t[8 * g + hh:8 * g + hh + 1, :], (64, N_STATE)) for hh in range(8)], axis=0)


def _ssd_fwd(xbc, dtlf, a_row, dsk_row):
    t = xbc.shape[0]
    nc = t // CHUNK

    def body(xs_ref, b_ref, c_ref, dt_ref, a_ref, dsk_ref, y_ref, hin_ref, h_scr, e_scr, es_scr, dte_scr):
        c = pl.program_id(0)

        @pl.when(c == 0)
        def _():
            h_scr[...] = jnp.zeros_like(h_scr)

        dt, acs, acs_t, r0, r1, lo = _ssd_prelude(dt_ref, a_ref, e_scr, es_scr, dte_scr)
        causal = r0 >= r1
        for g in range(G_SSD):
            gs = slice(_GW * g, _GW * (g + 1))
            bg = b_ref[:, N_STATE * g:N_STATE * (g + 1)].astype(BF16)
            cg = c_ref[:, N_STATE * g:N_STATE * (g + 1)].astype(BF16)
            cb = _dot(cg, bg, _NT)
            hg = h_scr[gs, :]
            hin_ref[0, gs, :] = hg
            xg = xs_ref[:, gs] * dte_scr[:, gs]
            yoff = _dot(cg, hg.astype(BF16), _NT) * e_scr[:, gs]
            st = _dot((xg * es_scr[:, gs]).astype(BF16), bg, _TN)
            h_scr[gs, :] = hg * _chunk_decay_rows(acs_t, g) + st
            for jj in range(4):
                j = 4 * g + jj
                sl = slice(CHUNK * j, CHUNK * (j + 1))
                xp = xg[:, CHUNK * jj:CHUNK * (jj + 1)]
                acc = yoff[:, CHUNK * jj:CHUNK * (jj + 1)] + dsk_ref[:, sl] * xs_ref[:, sl]
                for hh in range(2):
                    h = 2 * j + hh
                    seg = acs[:, h:h + 1] - acs_t[h:h + 1, :]
                    lm = jnp.exp(jnp.where(causal, seg, -1e30))
                    m = (cb * lm).astype(BF16)
                    xh = jnp.where(lo if hh == 0 else ~lo, xp, 0.0).astype(BF16)
                    acc = acc + _dot(m, xh)
                y_ref[:, sl] = acc

    return pl.pallas_call(
        body, name="ssd_fwd",
        out_shape=(jax.ShapeDtypeStruct((t, D_SSD), F32), jax.ShapeDtypeStruct((nc, D_SSD, N_STATE), F32)),
        grid=(nc,),
        in_specs=[pl.BlockSpec((CHUNK, D_SSD), lambda c: (c, 0)),
                  pl.BlockSpec((CHUNK, _GW), lambda c: (c, 4)),
                  pl.BlockSpec((CHUNK, _GW), lambda c: (c, 5)),
                  pl.BlockSpec((CHUNK, N_SMALL), lambda c: (c, 0)),
                  pl.BlockSpec((1, N_SMALL), lambda c: (0, 0)),
                  pl.BlockSpec((1, D_SSD), lambda c: (0, 0))],
        out_specs=(pl.BlockSpec((CHUNK, D_SSD), lambda c: (c, 0)),
                   pl.BlockSpec((1, D_SSD, N_STATE), lambda c: (c, 0, 0))),
        scratch_shapes=[pltpu.VMEM((D_SSD, N_STATE), F32)] + [pltpu.VMEM((CHUNK, D_SSD), F32)] * 3,
        compiler_params=_cp("arbitrary"),
    )(xbc, xbc, xbc, dtlf, a_row, dsk_row)


def _ssd_bwd(xbc, dtlf, a_row, dsk_row, hin, dy):
    t = xbc.shape[0]
    nc = t // CHUNK

    def body(xs_ref, b_ref, c_ref, dt_ref, a_ref, dsk_ref, hin_ref, dy_ref,
             dxbc_ref, ddt_ref, da_ref, ddsk_ref, dh_scr, e_scr, es_scr, dte_scr, dx_scr):
        step = pl.program_id(0)

        @pl.when(step == 0)
        def _():
            dh_scr[...] = jnp.zeros_like(dh_scr)
            da_ref[...] = jnp.zeros_like(da_ref)
            ddsk_ref[...] = jnp.zeros_like(ddsk_ref)

        dt, acs, acs_t, r0, r1, lo = _ssd_prelude(dt_ref, a_ref, e_scr, es_scr, dte_scr)
        causal = r0 >= r1
        lane_row = _iota((1, CHUNK), 1)
        dacs = jnp.zeros((CHUNK, CHUNK), F32)
        dacs_t = jnp.zeros((CHUNK, CHUNK), F32)
        dalast = jnp.zeros((1, CHUNK), F32)
        ddt_dir = jnp.zeros((CHUNK, CHUNK), F32)
        ddsk_ref[...] += jnp.sum(dy_ref[...] * xs_ref[...], axis=0, keepdims=True)

        def head_rowsum(x, hh):
            return jnp.sum(jnp.where(lo if hh == 0 else ~lo, x, 0.0), axis=1, keepdims=True)

        for g in range(G_SSD):
            gs = slice(_GW * g, _GW * (g + 1))
            bg = b_ref[:, N_STATE * g:N_STATE * (g + 1)].astype(BF16)
            cg = c_ref[:, N_STATE * g:N_STATE * (g + 1)].astype(BF16)
            cb = _dot(cg, bg, _NT)
            hg = hin_ref[0, gs, :]
            hgb = hg.astype(BF16)
            dhn = dh_scr[gs, :]
            dhnb = dhn.astype(BF16)
            eg = e_scr[:, gs]
            esg = es_scr[:, gs]
            dyg = dy_ref[:, gs]
            xg = xs_ref[:, gs] * dte_scr[:, gs]
            ch = _dot(cg, hgb, _NT)
            dye = dyg * eg
            dyeb = dye.astype(BF16)
            dc = _dot(dyeb, hgb)
            dh_y = _dot(dyeb, cg, _TN)
            dxs = _dot(bg, dhnb, _NT)
            db = _dot((xg * esg).astype(BF16), dhnb)
            t_acs = dye * ch - dxs * xg * esg
            t_last = dxs * xg * esg
            cd = _chunk_decay_rows(acs_t, g)
            dh_scr[gs, :] = dhn * cd + dh_y
            t_cd = jnp.sum(dhn * hg * cd, axis=1, keepdims=True)
            dcb = jnp.zeros((CHUNK, CHUNK), F32)
            for jj in range(4):
                j = 4 * g + jj
                sl = slice(CHUNK * j, CHUNK * (j + 1))
                ps = slice(CHUNK * jj, CHUNK * (jj + 1))
                xp = xg[:, ps]
                xpb = xp.astype(BF16)
                dyp = dyg[:, ps]
                dxp = dxs[:, ps] * esg[:, ps]
                for hh in range(2):
                    h = 2 * j + hh
                    sel = lo if hh == 0 else ~lo
                    seg = acs[:, h:h + 1] - acs_t[h:h + 1, :]
                    lm = jnp.exp(jnp.where(causal, seg, -1e30))
                    mf = cb * lm
                    dyh = jnp.where(sel, dyp, 0.0).astype(BF16)
                    gm = _dot(dyh, xpb, _NT)
                    dcb = dcb + gm * lm
                    w = gm * mf
                    on_h = r1 == h
                    dacs = dacs + jnp.where(on_h, jnp.sum(w, axis=1, keepdims=True)
                                            + head_rowsum(t_acs[:, ps], hh), 0.0)
                    dacs_t = dacs_t - jnp.where(r0 == h, jnp.sum(w, axis=0, keepdims=True), 0.0)
                    dxp = dxp + _dot(mf.astype(BF16), dyh, _TN)
                    hrow = slice(CHUNK * jj + 64 * hh, CHUNK * jj + 64 * hh + 64)
                    s_last = (jnp.sum(head_rowsum(t_last[:, ps], hh), axis=0, keepdims=True)
                              + jnp.sum(t_cd[hrow, :], axis=0, keepdims=True))
                    dalast = dalast + jnp.where(lane_row == h, s_last, 0.0)
                dx_scr[:, sl] = dxp
                for hh in range(2):
                    ddt_dir = ddt_dir + jnp.where(r1 == 2 * j + hh,
                                                  head_rowsum(dxp * xs_ref[:, sl], hh), 0.0)
            dcbb = dcb.astype(BF16)
            dxbc_ref[:, D_SSD + N_STATE * g:D_SSD + N_STATE * (g + 1)] = db + _dot(dcbb, cg, _TN)
            dxbc_ref[:, D_SSD + _GW + N_STATE * g:D_SSD + _GW + N_STATE * (g + 1)] = dc + _dot(dcbb, bg)
        dxbc_ref[:, 0:D_SSD] = dx_scr[...] * dte_scr[...] + dsk_ref[...] * dy_ref[...]
        dacs = dacs + dacs_t.T + jnp.where(r0 == CHUNK - 1, dalast, 0.0)
        dadt = _dot_exact((r1 >= r0).astype(F32), dacs)
        ddt_ref[...] = dadt * a_ref[...] + ddt_dir
        da_ref[...] += jnp.sum(dadt * dt, axis=0, keepdims=True)

    rev = lambda s: (nc - 1 - s, 0)
    return pl.pallas_call(
        body, name="ssd_bwd",
        out_shape=(jax.ShapeDtypeStruct((t, CONV_DIM), F32), jax.ShapeDtypeStruct((t, N_SMALL), F32),
                   jax.ShapeDtypeStruct((1, N_SMALL), F32), jax.ShapeDtypeStruct((1, D_SSD), F32)),
        grid=(nc,),
        in_specs=[pl.BlockSpec((CHUNK, D_SSD), rev),
                  pl.BlockSpec((CHUNK, _GW), lambda s: (nc - 1 - s, 4)),
                  pl.BlockSpec((CHUNK, _GW), lambda s: (nc - 1 - s, 5)),
                  pl.BlockSpec((CHUNK, N_SMALL), rev),
                  pl.BlockSpec((1, N_SMALL), lambda s: (0, 0)),
                  pl.BlockSpec((1, D_SSD), lambda s: (0, 0)),
                  pl.BlockSpec((1, D_SSD, N_STATE), lambda s: (nc - 1 - s, 0, 0)),
                  pl.BlockSpec((CHUNK, D_SSD), rev)],
        out_specs=(pl.BlockSpec((CHUNK, CONV_DIM), rev),
                   pl.BlockSpec((CHUNK, N_SMALL), rev),
                   pl.BlockSpec((1, N_SMALL), lambda s: (0, 0)),
                   pl.BlockSpec((1, D_SSD), lambda s: (0, 0))),
        scratch_shapes=[pltpu.VMEM((D_SSD, N_STATE), F32)] + [pltpu.VMEM((CHUNK, D_SSD), F32)] * 4,
        compiler_params=_cp("arbitrary"),
    )(xbc, xbc, xbc, dtlf, a_row, dsk_row, hin, dy)


_NPAIR = H_ATT // 2
_QB, _KB, _VB = C_Q // 128, C_K // 128, C_V // 128
_SCALE = 1.0 / math.sqrt(64.0)


def _attn_blocks(t):
    return _tile(t, (1408, 384, 256, 128)), _tile(t, (384, 128))


def _attn_scores(q_ref, k_ref, cc_ref, cr_ref, i, kk, bq, bk):
    lane = _iota((bq, 128), 1)
    lo = lane < 64
    q = q_ref[...] * _SCALE
    qa = jnp.where(lo, q, 0.0).astype(BF16)
    qb = jnp.where(lo, 0.0, q).astype(BF16)
    kb16 = k_ref[...].astype(BF16)
    rows = i * bq + _iota((bq, bk), 0)
    cols = kk * bk + _iota((bq, bk), 1)
    ok = (cols <= rows) & ((cols >= PADF) | (rows < PADF))
    cc = cc_ref[0]
    cr = cr_ref[0]
    sa = jnp.where(ok, _dot(qa, kb16, _NT) + (cc[:, 0:1] - cr[0:1, :]), -1e30)
    sb = jnp.where(ok, _dot(qb, kb16, _NT) + (cc[:, 1:2] - cr[1:2, :]), -1e30)
    return sa, sb, qa, qb, kb16, lo, ok


def _attn_fwd(proj, c_col, c_row):
    t = proj.shape[0]
    bq, bk = _attn_blocks(t)
    nq, nk = t // bq, t // bk

    def last_kv(i):
        return (i * bq + bq - 1) // bk

    def body(q_ref, k_ref, v_ref, cc_ref, cr_ref, o_ref, lse_ref, m_scr, l_scr, acc_scr):
        i = pl.program_id(1)
        kk = pl.program_id(2)

        @pl.when(kk == 0)
        def _():
            m_scr[...] = jnp.full_like(m_scr, -1e30)
            l_scr[...] = jnp.zeros_like(l_scr)
            acc_scr[...] = jnp.zeros_like(acc_scr)

        @pl.when(kk <= last_kv(i))
        def _():
            sa, sb, _, _, _, lo, _ = _attn_scores(q_ref, k_ref, cc_ref, cr_ref, i, kk, bq, bk)
            v = v_ref[...]
            lo_k = _iota((bk, 128), 1) < 64
            va = jnp.where(lo_k, v, 0.0).astype(BF16)
            vb = jnp.where(lo_k, 0.0, v).astype(BF16)
            lane2 = _iota((bq, 2), 1)
            m_old = m_scr[...]
            m_new = jnp.maximum(m_old, jnp.where(lane2 == 0, jnp.max(sa, axis=1, keepdims=True),
                                                 jnp.max(sb, axis=1, keepdims=True)))
            alpha = jnp.exp(m_old - m_new)
            pa = jnp.exp(sa - m_new[:, 0:1])
            pb = jnp.exp(sb - m_new[:, 1:2])
            l_scr[...] = alpha * l_scr[...] + jnp.where(lane2 == 0, jnp.sum(pa, axis=1, keepdims=True),
                                                        jnp.sum(pb, axis=1, keepdims=True))
            m_scr[...] = m_new
            acc_scr[...] = (acc_scr[...] * jnp.where(lo, alpha[:, 0:1], alpha[:, 1:2])
                            + _dot(pa.astype(BF16), va) + _dot(pb.astype(BF16), vb))

        @pl.when(kk == nk - 1)
        def _():
            lane = _iota((bq, 128), 1)
            l = l_scr[...]
            o_ref[...] = acc_scr[...] / jnp.where(lane < 64, l[:, 0:1], l[:, 1:2])
            lse_ref[0] = m_scr[...] + jnp.log(l)

    kv = lambda off: pl.BlockSpec((bk, 128), lambda j, i, kk: (jnp.minimum(kk, last_kv(i)), off + j))
    return pl.pallas_call(
        body, name="attn_fwd",
        out_shape=(jax.ShapeDtypeStruct((t, D_ATT), F32), jax.ShapeDtypeStruct((_NPAIR, t, 2), F32)),
        grid=(_NPAIR, nq, nk),
        in_specs=[pl.BlockSpec((bq, 128), lambda j, i, kk: (i, _QB + j)),
                  kv(_KB), kv(_VB),
                  pl.BlockSpec((1, bq, 2), lambda j, i, kk: (j, i, 0)),
                  pl.BlockSpec((1, 2, bk), lambda j, i, kk: (j, 0, jnp.minimum(kk, last_kv(i))))],
        out_specs=(pl.BlockSpec((bq, 128), lambda j, i, kk: (i, j)),
                   pl.BlockSpec((1, bq, 2), lambda j, i, kk: (j, i, 0))),
        scratch_shapes=[pltpu.VMEM((bq, 2), F32), pltpu.VMEM((bq, 2), F32), pltpu.VMEM((bq, 128), F32)],
        compiler_params=_cp("parallel", "parallel", "arbitrary"),
    )(proj, proj, proj, c_col, c_row)


def _attn_bwd(proj, c_col, c_row, o, lse, do):
    t = proj.shape[0]
    bq, bk = _attn_blocks(t)
    nq, nk = t // bq, t // bk

    def first_q(kk):
        return (kk * bk) // bq

    def body(q_ref, k_ref, v_ref, cc_ref, cr_ref, o_ref, lse_ref, do_ref,
             dq_ref, dk_ref, dv_ref, dc_ref, dcq_ref, dq_scr, dk_scr, dv_scr, dcq_scr):
        kk = pl.program_id(1)
        i = pl.program_id(2)

        @pl.when((kk == 0) & (i == 0))
        def _():
            dq_scr[...] = jnp.zeros_like(dq_scr)
            dcq_scr[...] = jnp.zeros_like(dcq_scr)

        @pl.when(i == 0)
        def _():
            dk_scr[...] = jnp.zeros_like(dk_scr)
            dv_scr[...] = jnp.zeros_like(dv_scr)
            dc_ref[...] = jnp.zeros_like(dc_ref)

        @pl.when(i >= first_q(kk))
        def _():
            sa, sb, qa, qb, kb16, lo, ok = _attn_scores(q_ref, k_ref, cc_ref, cr_ref, i, kk, bq, bk)
            lse = lse_ref[0]
            pa = jnp.where(ok, jnp.exp(sa - lse[:, 0:1]), 0.0)
            pb = jnp.where(ok, jnp.exp(sb - lse[:, 1:2]), 0.0)
            do_ = do_ref[...]
            doa = jnp.where(lo, do_, 0.0).astype(BF16)
            dob = jnp.where(lo, 0.0, do_).astype(BF16)
            vb16 = v_ref[...].astype(BF16)
            prod = do_ * o_ref[...]
            da = jnp.sum(jnp.where(lo, prod, 0.0), axis=1, keepdims=True)
            db = jnp.sum(jnp.where(lo, 0.0, prod), axis=1, keepdims=True)
            dsa = pa * (_dot(doa, vb16, _NT) - da)
            dsb = pb * (_dot(dob, vb16, _NT) - db)
            dsa16 = dsa.astype(BF16)
            dsb16 = dsb.astype(BF16)
            dv_scr[...] += _dot(pa.astype(BF16), doa, _TN) + _dot(pb.astype(BF16), dob, _TN)
            dk_scr[...] += _dot(dsa16, qa, _TN) + _dot(dsb16, qb, _TN)
            lo_k = _iota((bk, 128), 1) < 64
            ka = jnp.where(lo_k, kb16, jnp.zeros_like(kb16))
            kb_ = jnp.where(lo_k, jnp.zeros_like(kb16), kb16)
            rs = pl.ds(pl.multiple_of(i * bq, 8), bq)
            dq_scr[rs, :] += (_dot(dsa16, ka) + _dot(dsb16, kb_)) * _SCALE
            dc_ref[0] -= jnp.concatenate([jnp.sum(dsa, axis=0, keepdims=True),
                                          jnp.sum(dsb, axis=0, keepdims=True)], axis=0)
            dcq_scr[rs, :] += jnp.where(_iota((bq, 2), 1) == 0, jnp.sum(dsa, axis=1, keepdims=True),
                                        jnp.sum(dsb, axis=1, keepdims=True))

        @pl.when(i == nq - 1)
        def _():
            dk_ref[...] = dk_scr[...].astype(BF16)
            dv_ref[...] = dv_scr[...].astype(BF16)

        @pl.when((kk == nk - 1) & (i == nq - 1))
        def _():
            dq_ref[...] = dq_scr[...].astype(BF16)
            dcq_ref[0] = dcq_scr[...]

    qi = lambda j, kk, i: jnp.maximum(i, first_q(kk))
    qspec = lambda off: pl.BlockSpec((bq, 128), lambda j, kk, i: (qi(j, kk, i), off + j))
    kspec = lambda off: pl.BlockSpec((bk, 128), lambda j, kk, i: (kk, off + j))
    return pl.pallas_call(
        body, name="attn_bwd",
        out_shape=(jax.ShapeDtypeStruct((t, D_ATT), BF16), jax.ShapeDtypeStruct((t, D_ATT), BF16),
                   jax.ShapeDtypeStruct((t, D_ATT), BF16), jax.ShapeDtypeStruct((_NPAIR, 2, t), F32),
                   jax.ShapeDtypeStruct((_NPAIR, t, 2), F32)),
        grid=(_NPAIR, nk, nq),
        in_specs=[qspec(_QB), kspec(_KB), kspec(_VB),
                  pl.BlockSpec((1, bq, 2), lambda j, kk, i: (j, qi(j, kk, i), 0)),
                  pl.BlockSpec((1, 2, bk), lambda j, kk, i: (j, 0, kk)),
                  qspec(0),
                  pl.BlockSpec((1, bq, 2), lambda j, kk, i: (j, qi(j, kk, i), 0)),
                  qspec(0)],
        out_specs=(pl.BlockSpec((t, 128), lambda j, kk, i: (0, j)),
                   pl.BlockSpec((bk, 128), lambda j, kk, i: (kk, j)),
                   pl.BlockSpec((bk, 128), lambda j, kk, i: (kk, j)),
                   pl.BlockSpec((1, 2, bk), lambda j, kk, i: (j, 0, kk)),
                   pl.BlockSpec((1, t, 2), lambda j, kk, i: (j, 0, 0))),
        scratch_shapes=[pltpu.VMEM((t, 128), F32), pltpu.VMEM((bk, 128), F32), pltpu.VMEM((bk, 128), F32),
                        pltpu.VMEM((t, 2), F32)],
        compiler_params=_cp("parallel", "arbitrary", "arbitrary"),
    )(proj, proj, proj, c_col, c_row, o, lse, do)


def _premerge_fwd(y, o, proj, gamma):
    t = y.shape[0]
    tm = _row_tile_wide(t)

    def body(y_ref, z_ref, o_ref, za_ref, g_ref, ys_ref, ya_ref):
        z = z_ref[...]
        u = y_ref[...] * (z * _sigmoid(z))
        for g in range(G_SSD):
            gs = slice(_GW * g, _GW * (g + 1))
            ug = u[:, gs]
            r = lax.rsqrt(jnp.mean(ug * ug, axis=-1, keepdims=True) + EPS)
            ys_ref[:, gs] = (ug * r * g_ref[:, gs]).astype(BF16)
        za = za_ref[...]
        ya_ref[...] = (o_ref[...] * (za * _sigmoid(za))).astype(BF16)

    return pl.pallas_call(
        body, name="premerge_fwd",
        out_shape=(jax.ShapeDtypeStruct((t, D_SSD), BF16), jax.ShapeDtypeStruct((t, D_ATT), BF16)),
        grid=(t // tm,),
        in_specs=[pl.BlockSpec((tm, D_SSD), lambda i: (i, 0)),
                  pl.BlockSpec((tm, D_SSD), lambda i: (i, C_Z // D_SSD)),
                  pl.BlockSpec((tm, D_ATT), lambda i: (i, 0)),
                  pl.BlockSpec((tm, D_ATT), lambda i: (i, C_ZA // D_ATT)),
                  pl.BlockSpec((1, D_SSD), lambda i: (0, 0))],
        out_specs=(pl.BlockSpec((tm, D_SSD), lambda i: (i, 0)), pl.BlockSpec((tm, D_ATT), lambda i: (i, 0))),
        compiler_params=_cp("parallel"),
    )(y, proj, o, proj, gamma)


def _premerge_bwd(dys, dya, y, o, proj, gamma):
    t = y.shape[0]
    tm = _row_tile_wide(t)

    def body(dys_ref, dya_ref, y_ref, z_ref, o_ref, za_ref, g_ref, dy_ref, dz_ref, do_ref, dza_ref, dg_ref):
        i = pl.program_id(0)
        z = z_ref[...]
        sz = _sigmoid(z)
        silu = z * sz
        dsilu = sz * (1.0 + z * (1.0 - sz))
        yv = y_ref[...]
        u = yv * silu
        parts = []
        for g in range(G_SSD):
            gs = slice(_GW * g, _GW * (g + 1))
            ug = u[:, gs]
            r = lax.rsqrt(jnp.mean(ug * ug, axis=-1, keepdims=True) + EPS)
            n = ug * r
            dout = dys_ref[:, gs]
            dn = dout * g_ref[:, gs]
            du = r * (dn - n * jnp.mean(dn * n, axis=-1, keepdims=True))
            dy_ref[:, gs] = du * silu[:, gs]
            dz_ref[:, gs] = (du * yv[:, gs] * dsilu[:, gs]).astype(BF16)
            parts.append(jnp.sum(dout * n, axis=0, keepdims=True))
        dg = jnp.concatenate(parts, axis=1)
        za = za_ref[...]
        sa = _sigmoid(za)
        dya_ = dya_ref[...]
        do_ref[...] = dya_ * (za * sa)
        dza_ref[...] = (dya_ * o_ref[...] * (sa * (1.0 + za * (1.0 - sa)))).astype(BF16)

        @pl.when(i == 0)
        def _():
            dg_ref[...] = dg

        @pl.when(i > 0)
        def _():
            dg_ref[...] += dg

    ssd = pl.BlockSpec((tm, D_SSD), lambda i: (i, 0))
    att = pl.BlockSpec((tm, D_ATT), lambda i: (i, 0))
    vec = pl.BlockSpec((1, D_SSD), lambda i: (0, 0))
    return pl.pallas_call(
        body, name="premerge_bwd",
        out_shape=(jax.ShapeDtypeStruct((t, D_SSD), F32), jax.ShapeDtypeStruct((t, D_SSD), BF16),
                   jax.ShapeDtypeStruct((t, D_ATT), F32), jax.ShapeDtypeStruct((t, D_ATT), BF16),
                   jax.ShapeDtypeStruct((1, D_SSD), F32)),
        grid=(t // tm,),
        in_specs=[ssd, att, ssd, pl.BlockSpec((tm, D_SSD), lambda i: (i, C_Z // D_SSD)), att,
                  pl.BlockSpec((tm, D_ATT), lambda i: (i, C_ZA // D_ATT)), vec],
        out_specs=(ssd, ssd, att, att, vec),
        compiler_params=_cp("arbitrary"),
    )(dys, dya, y, proj, o, proj, gamma)


_G_BLK = C_G // D_MODEL


def _merge_fwd(a, b, proj, gate_bias):
    t = a.shape[0]
    tm = _row_tile(t)

    def body(a_ref, b_ref, gs_ref, ga_ref, bias_ref, m_ref):
        g_ssd = _sigmoid(gs_ref[...] + bias_ref[:, 0:D_MODEL])
        g_att = _sigmoid(ga_ref[...] + bias_ref[:, D_MODEL:2 * D_MODEL])
        m_ref[...] = (g_ssd * a_ref[...] + g_att * b_ref[...]).astype(BF16)

    row = pl.BlockSpec((tm, D_MODEL), lambda i: (i, 0))
    return pl.pallas_call(
        body, name="merge_fwd",
        out_shape=jax.ShapeDtypeStruct((t, D_MODEL), BF16),
        grid=(t // tm,),
        in_specs=[row, row,
                  pl.BlockSpec((tm, D_MODEL), lambda i: (i, _G_BLK)),
                  pl.BlockSpec((tm, D_MODEL), lambda i: (i, _G_BLK + 1)),
                  pl.BlockSpec((1, 2 * D_MODEL), lambda i: (0, 0))],
        out_specs=row,
        compiler_params=_cp("parallel"),
    )(a, b, proj, proj, gate_bias)


def _merge_bwd(dm, a, b, proj, gate_bias):
    t = a.shape[0]
    tm = _row_tile(t)

    def body(dm_ref, a_ref, b_ref, gs_ref, ga_ref, bias_ref, da_ref, db_ref, dg_ref, dbias_ref):
        i = pl.program_id(0)
        dm_ = dm_ref[...]
        g_ssd = _sigmoid(gs_ref[...] + bias_ref[:, 0:D_MODEL])
        g_att = _sigmoid(ga_ref[...] + bias_ref[:, D_MODEL:2 * D_MODEL])
        da_ref[...] = (dm_ * g_ssd).astype(BF16)
        db_ref[...] = (dm_ * g_att).astype(BF16)
        dgs = dm_ * a_ref[...] * g_ssd * (1.0 - g_ssd)
        dga = dm_ * b_ref[...] * g_att * (1.0 - g_att)
        dg_ref[:, 0:D_MODEL] = dgs.astype(BF16)
        dg_ref[:, D_MODEL:2 * D_MODEL] = dga.astype(BF16)
        part = jnp.concatenate([jnp.sum(dgs, axis=0, keepdims=True), jnp.sum(dga, axis=0, keepdims=True)], axis=1)

        @pl.when(i == 0)
        def _():
            dbias_ref[...] = part

        @pl.when(i > 0)
        def _():
            dbias_ref[...] += part

    row = pl.BlockSpec((tm, D_MODEL), lambda i: (i, 0))
    wide = pl.BlockSpec((tm, 2 * D_MODEL), lambda i: (i, 0))
    vec = pl.BlockSpec((1, 2 * D_MODEL), lambda i: (0, 0))
    return pl.pallas_call(
        body, name="merge_bwd",
        out_shape=(jax.ShapeDtypeStruct((t, D_MODEL), BF16), jax.ShapeDtypeStruct((t, D_MODEL), BF16),
                   jax.ShapeDtypeStruct((t, 2 * D_MODEL), BF16), jax.ShapeDtypeStruct((1, 2 * D_MODEL), F32)),
        grid=(t // tm,),
        in_specs=[row, row, row,
                  pl.BlockSpec((tm, D_MODEL), lambda i: (i, _G_BLK)),
                  pl.BlockSpec((tm, D_MODEL), lambda i: (i, _G_BLK + 1)), vec],
        out_specs=(row, row, wide, vec),
        compiler_params=_cp("arbitrary"),
    )(dm, a, b, proj, proj, gate_bias)


def _post(o2, h, target, g):
    t = o2.shape[0]
    nc = t // CHUNK

    def body(o_ref, h_ref, t_ref, g_ref, dy_ref, do_ref, dg_ref, loss_ref):
        c = pl.program_id(0)
        x = o_ref[...]
        r = lax.rsqrt(jnp.mean(x * x, axis=-1, keepdims=True) + EPS)
        n = x * r
        y = h_ref[...] + n * g_ref[...]
        diff = jnp.where(c > 0, y - t_ref[...], 0.0)
        dy = diff * (1.0 / D_MODEL)
        dy_ref[...] = dy
        gdy = dy * g_ref[...]
        do_ref[...] = (r * (gdy - n * jnp.mean(gdy * n, axis=-1, keepdims=True))).astype(BF16)
        dg = jnp.sum(dy * n, axis=0, keepdims=True)
        lpart = 0.5 * jnp.sum(jnp.sum(diff * diff, axis=1, keepdims=True), axis=0, keepdims=True) * (1.0 / D_MODEL)
        sel = (_iota((8, 128), 0) == 0) & (_iota((8, 128), 1) == 0)

        @pl.when(c == 0)
        def _():
            dg_ref[...] = dg
            loss_ref[...] = jnp.zeros_like(loss_ref)

        @pl.when(c > 0)
        def _():
            dg_ref[...] += dg
            loss_ref[...] += jnp.where(sel, lpart, 0.0)

    row = pl.BlockSpec((CHUNK, D_MODEL), lambda c: (c, 0))
    vec = pl.BlockSpec((1, D_MODEL), lambda c: (0, 0))
    return pl.pallas_call(
        body, name="post",
        out_shape=(jax.ShapeDtypeStruct((t, D_MODEL), F32), jax.ShapeDtypeStruct((t, D_MODEL), BF16),
                   jax.ShapeDtypeStruct((1, D_MODEL), F32), jax.ShapeDtypeStruct((8, 128), F32)),
        grid=(nc,),
        in_specs=[row, row, pl.BlockSpec((CHUNK, D_MODEL), lambda c: (jnp.maximum(c - 1, 0), 0)), vec],
        out_specs=(row, row, vec, pl.BlockSpec((8, 128), lambda c: (0, 0))),
        compiler_params=_cp("arbitrary"),
    )(o2, h, target, g)


def _mm_tiles(t):
    return _tile(t, (704, 384, 128))


def _local_step(h, target, w_main, w_small, wps, wpa, wout, norm_pre, conv_w, conv_b, bias_row, a_row,
                dsk_row, ssd_norm, gate_bias, norm_post):
    t = h.shape[0]
    tm = _mm_tiles(t)
    u = _norm1_fwd(h, norm_pre)
    proj = _matmul(u, w_main, "nn", F32, "inproj", tm, 1024, D_MODEL)
    small = _matmul(u, w_small, "nn", F32, "inproj_small", tm, N_SMALL, D_MODEL)
    dtlf = _small_fwd(small, bias_row)
    xbc = _conv_fwd(proj, conv_w, conv_b)
    y, hin = _ssd_fwd(xbc, dtlf, a_row, dsk_row)
    c_tok = dtlf[:, H_SSD:H_SSD + H_ATT]
    c_col = c_tok.reshape(t, _NPAIR, 2).transpose(1, 0, 2)
    c_row = c_tok.T.reshape(_NPAIR, 2, t)
    o, lse = _attn_fwd(proj, c_col, c_row)
    ys, ya = _premerge_fwd(y, o, proj, ssd_norm)
    a = _matmul(ys, wps, "nn", F32, "proj_ssd", tm, D_MODEL, D_SSD)
    b = _matmul(ya, wpa, "nn", F32, "proj_att", tm, D_MODEL, D_ATT)
    merged = _merge_fwd(a, b, proj, gate_bias)
    o2 = _matmul(merged, wout, "nn", F32, "out_proj", tm, D_MODEL, D_MODEL)
    dy_out, do2, d_norm_post, loss_blk = _post(o2, h, target, norm_post)

    dm = _matmul(do2, wout, "nt", F32, "out_proj_dx", tm, D_MODEL, D_MODEL)
    d_wout = _matmul(merged, do2, "tn", F32, "out_proj_dw", D_MODEL, D_MODEL, tm)
    da, db, dgraw, d_gate_bias = _merge_bwd(dm, a, b, proj, gate_bias)
    dys = _matmul(da, wps, "nt", F32, "proj_ssd_dx", tm, D_SSD, D_MODEL)
    d_wps = _matmul(ys, da, "tn", F32, "proj_ssd_dw", D_SSD, D_MODEL, tm)
    dya = _matmul(db, wpa, "nt", F32, "proj_att_dx", tm, D_ATT, D_MODEL)
    d_wpa = _matmul(ya, db, "tn", F32, "proj_att_dw", D_ATT, D_MODEL, tm)
    dy, dz, do, dza, d_ssd_norm = _premerge_bwd(dys, dya, y, o, proj, ssd_norm)
    dq, dk, dv, dc_key, dc_qry = _attn_bwd(proj, c_col, c_row, o, lse, do)
    dxbc, ddt, d_a, d_dsk = _ssd_bwd(xbc, dtlf, a_row, dsk_row, hin, dy)
    dact, d_conv_w, d_conv_b = _conv_bwd_act(dxbc, proj, conv_w, conv_b)
    dxbc_raw = _conv_bwd_in(dact, conv_w)
    dc_tok = dc_key.reshape(H_ATT, t).T + jnp.transpose(dc_qry, (1, 0, 2)).reshape(t, H_ATT)
    dsm = ddt + jnp.pad(dc_tok, ((0, 0), (H_SSD, N_SMALL - H_SSD - H_ATT)))
    dsmall, d_bias_row = _small_bwd(dsm, small, bias_row)
    dproj = jnp.concatenate([dz, dxbc_raw, dza, dq, dk, dv, dgraw], axis=1)
    du_a = _matmul(dproj, w_main, "nt", F32, "inproj_dx", tm, D_MODEL, 1024)
    du_b = _matmul(dsmall, w_small, "nt", F32, "inproj_small_dx", tm, D_MODEL, N_SMALL)
    d_w_main = _matmul(u, dproj, "tn", F32, "inproj_dw", D_MODEL, 1024, tm)
    d_w_small = _matmul(u, dsmall, "tn", F32, "inproj_small_dw", D_MODEL, N_SMALL, tm)
    dh, d_norm_pre = _norm1_bwd(du_a, du_b, h, norm_pre, dy_out)
    return dict(loss_blk=loss_blk, dh=dh, d_w_main=d_w_main, d_w_small=d_w_small, d_wps=d_wps, d_wpa=d_wpa,
                d_wout=d_wout, d_norm_pre=d_norm_pre, d_conv_w=d_conv_w, d_conv_b=d_conv_b,
                d_bias_row=d_bias_row, d_a=d_a, d_dsk=d_dsk, d_ssd_norm=d_ssd_norm,
                d_gate_bias=d_gate_bias, d_norm_post=d_norm_post)


def _to_aligned_cols(w):
    def cut(o):
        return w[..., o[0]:o[0] + o[1]]
    main = jnp.concatenate([cut(O_Z), cut(O_XBC), cut(O_ZA), cut(O_Q), cut(O_K), cut(O_V), cut(O_G)], axis=-1)
    pad = jnp.zeros(w.shape[:-1] + (N_SMALL - H_SSD - H_ATT,), w.dtype)
    small = jnp.concatenate([cut(O_DT), cut(O_F), pad], axis=-1)
    return main, small


def _from_aligned_cols(main, small):
    def cm(c0, n):
        return main[..., c0:c0 + n]
    return jnp.concatenate([cm(C_Z, 2048), cm(C_XBC, 3072), small[..., 0:H_SSD], cm(C_ZA, 1024),
                            cm(C_Q, 1024), cm(C_K, 1024), cm(C_V, 1024), small[..., H_SSD:H_SSD + H_ATT],
                            cm(C_G, 2048)], axis=-1)


_MESH = pl.DeviceIdType.MESH
_ANY = pl.BlockSpec(memory_space=pl.ANY)
_VM = pl.BlockSpec(memory_space=pltpu.VMEM)
_HALF = 512
N_DEV = 8


def _coords():
    return lax.axis_index("x"), lax.axis_index("y"), lax.axis_index("c")


def _other_chips(x, y):
    return [(1 - x, y), (x, 1 - y), (1 - x, 1 - y)]


def _half(cc):
    return pl.ds(cc * _HALF, _HALF)


def _gather_shards(shards):
    n = len(shards)

    def body(*refs):
        src, dst = refs[:n], refs[n:2 * n]
        send_sems, recv_sems, local_sems = refs[2 * n:]
        x, y, c = _coords()
        chip = 2 * x + y
        sibling = (x, y, 1 - c)
        chips = _other_chips(x, y)

        def remote(s, d, k, to):
            return pltpu.make_async_remote_copy(src_ref=s, dst_ref=d, send_sem=send_sems.at[k], recv_sem=recv_sems.at[k],
                                                device_id=to, device_id_type=_MESH)

        def over_ici(i, k, frm):
            rows = dst[i].at[frm, _half(c)]
            return remote(src[i].at[_half(c)], rows, 6 * i + k, (*chips[k], c))

        def to_sibling(i, k, cc):
            rows = dst[i].at[2 * chips[k][0] + chips[k][1], _half(cc)]
            return remote(rows, rows, 6 * i + 3 + k, sibling)

        local = [pltpu.make_async_copy(src[i], dst[i].at[chip], local_sems.at[i]) for i in range(n)]
        for cp in local:
            cp.start()
        first = [over_ici(i, k, chip) for k in range(3) for i in range(n)]
        for cp in first:
            cp.start()
        passed = []
        for k in range(3):
            for i in range(n):
                over_ici(i, k, 2 * chips[k][0] + chips[k][1]).wait_recv()
                passed.append(to_sibling(i, k, c))
                passed[-1].start()
        for k in range(3):
            for i in range(n):
                to_sibling(i, k, 1 - c).wait_recv()
        for cp in first + passed:
            cp.wait_send()
        for cp in local:
            cp.wait()

    return pl.pallas_call(
        body, name="gather_shards",
        out_shape=tuple(jax.ShapeDtypeStruct((4,) + s.shape, s.dtype) for s in shards),
        in_specs=[_ANY] * n, out_specs=tuple([_ANY] * n),
        scratch_shapes=[pltpu.SemaphoreType.DMA((6 * n,)), pltpu.SemaphoreType.DMA((6 * n,)),
                        pltpu.SemaphoreType.DMA((n,))],
    )(*shards)


def _allgather8(block, name):
    rows, width = block.shape

    def body(x_ref, out_ref, send_sems, recv_sems, local_sem):
        x, y, c = _coords()
        me, sibling = (x, y, c), (x, y, 1 - c)
        chips = _other_chips(x, y)

        def slot(px, py, pc):
            return out_ref.at[4 * px + 2 * py + pc]

        def copy(k, blk, to, src=None):
            return pltpu.make_async_remote_copy(src_ref=slot(*blk) if src is None else src, dst_ref=slot(*blk),
                                                send_sem=send_sems.at[k], recv_sem=recv_sems.at[k],
                                                device_id=to, device_id_type=_MESH)

        mine = pltpu.make_async_copy(x_ref, slot(*me), local_sem)
        mine.start()
        first = [copy(0, me, sibling, src=x_ref)]
        first += [copy(1 + j, me, (*chip, c), src=x_ref) for j, chip in enumerate(chips)]
        for cp in first:
            cp.start()
        passed = [copy(4 + j, (*chip, c), sibling) for j, chip in enumerate(chips)]
        for j, chip in enumerate(chips):
            copy(1 + j, (*chip, c), me).wait_recv()
            passed[j].start()
        copy(0, sibling, me).wait_recv()
        for j, chip in enumerate(chips):
            copy(4 + j, (*chip, 1 - c), me).wait_recv()
        for cp in first + passed:
            cp.wait_send()
        mine.wait()

    return pl.pallas_call(
        body, name=name,
        out_shape=jax.ShapeDtypeStruct((N_DEV, rows, width), block.dtype),
        in_specs=[_VM], out_specs=_VM,
        scratch_shapes=[pltpu.SemaphoreType.DMA((7,)), pltpu.SemaphoreType.DMA((7,)), pltpu.SemaphoreType.DMA],
    )(block)


def _pair_swap_halves(arrs):
    n = len(arrs)

    def body(*refs):
        src, dst = refs[:n], refs[n:2 * n]
        send_sems, recv_sems = refs[2 * n:]
        x, y, c = _coords()
        cps = [pltpu.make_async_remote_copy(src_ref=src[i].at[pl.ds(0, 4), _half(1 - c)], dst_ref=dst[i],
                                            send_sem=send_sems.at[i], recv_sem=recv_sems.at[i],
                                            device_id=(x, y, 1 - c), device_id_type=_MESH) for i in range(n)]
        for cp in cps:
            cp.start()
        for cp in cps:
            cp.wait()

    return pl.pallas_call(
        body, name="pair_swap_halves",
        out_shape=tuple(jax.ShapeDtypeStruct((4, _HALF, a.shape[2]), a.dtype) for a in arrs),
        in_specs=[_ANY] * n, out_specs=tuple([_ANY] * n),
        scratch_shapes=[pltpu.SemaphoreType.DMA((n,)), pltpu.SemaphoreType.DMA((n,))],
    )(*arrs)


def _chip_exchange(arrs):
    n = len(arrs)

    def body(*refs):
        src, dst = refs[:n], refs[n:2 * n]
        send_sems, recv_sems = refs[2 * n:]
        x, y, c = _coords()
        chips = _other_chips(x, y)
        cps = [pltpu.make_async_remote_copy(src_ref=src[i].at[2 * chips[k][0] + chips[k][1]], dst_ref=dst[i].at[k],
                                            send_sem=send_sems.at[3 * i + k], recv_sem=recv_sems.at[3 * i + k],
                                            device_id=(*chips[k], c), device_id_type=_MESH)
               for k in range(3) for i in range(n)]
        for cp in cps:
            cp.start()
        for cp in cps:
            cp.wait()

    return pl.pallas_call(
        body, name="chip_exchange",
        out_shape=tuple(jax.ShapeDtypeStruct((3, _HALF, a.shape[2]), a.dtype) for a in arrs),
        in_specs=[_ANY] * n, out_specs=tuple([_ANY] * n),
        scratch_shapes=[pltpu.SemaphoreType.DMA((3 * n,)), pltpu.SemaphoreType.DMA((3 * n,))],
    )(*arrs)


def _pair_join_halves(halves):
    n = len(halves)

    def body(*refs):
        src, dst = refs[:n], refs[n:2 * n]
        send_sems, recv_sems, local_sems = refs[2 * n:]
        x, y, c = _coords()
        local = [pltpu.make_async_copy(src[i], dst[i].at[_half(c)], local_sems.at[i]) for i in range(n)]
        for cp in local:
            cp.start()

        def remote(i, cc):
            return pltpu.make_async_remote_copy(src_ref=src[i], dst_ref=dst[i].at[_half(cc)],
                                                send_sem=send_sems.at[i], recv_sem=recv_sems.at[i],
                                                device_id=(x, y, 1 - c), device_id_type=_MESH)

        for i in range(n):
            remote(i, c).start()
        for i in range(n):
            remote(i, c).wait_send()
            remote(i, 1 - c).wait_recv()
        for cp in local:
            cp.wait()

    return pl.pallas_call(
        body, name="pair_join_halves",
        out_shape=tuple(jax.ShapeDtypeStruct((2 * _HALF, a.shape[1]), a.dtype) for a in halves),
        in_specs=[_ANY] * n, out_specs=tuple([_ANY] * n),
        scratch_shapes=[pltpu.SemaphoreType.DMA((n,)), pltpu.SemaphoreType.DMA((n,)), pltpu.SemaphoreType.DMA((n,))],
    )(*halves)


_RED_TR = 128


def _add_pair(ids, g32, recv_a):
    width = g32.shape[2]
    nt = _HALF // _RED_TR

    def body(ids_ref, g_ref, r_ref, o_ref):
        o_ref[...] = (g_ref[...] + r_ref[...]).astype(BF16)

    blk = pl.BlockSpec((1, _RED_TR, width), lambda j, i, ids: (j, i, 0))
    return pl.pallas_call(
        body, name="add_pair",
        out_shape=jax.ShapeDtypeStruct((4, _HALF, width), BF16),
        grid_spec=pltpu.PrefetchScalarGridSpec(
            num_scalar_prefetch=1, grid=(4, nt),
            in_specs=[pl.BlockSpec((1, _RED_TR, width), lambda j, i, ids: (j, ids[0] * nt + i, 0)), blk],
            out_specs=blk),
        compiler_params=_cp("parallel", "parallel"),
    )(ids, g32, recv_a)


def _add_chips(ids, g32, recv_a, recv_b):
    width = g32.shape[2]
    nt = _HALF // _RED_TR

    def body(ids_ref, g_ref, a_ref, b_ref, o_ref):
        acc = g_ref[0] + a_ref[0]
        for k in range(3):
            acc = acc + b_ref[k].astype(F32)
        o_ref[...] = acc

    return pl.pallas_call(
        body, name="add_chips",
        out_shape=jax.ShapeDtypeStruct((_HALF, width), F32),
        grid_spec=pltpu.PrefetchScalarGridSpec(
            num_scalar_prefetch=1, grid=(nt,),
            in_specs=[pl.BlockSpec((1, _RED_TR, width), lambda i, ids: (ids[1], ids[0] * nt + i, 0)),
                      pl.BlockSpec((1, _RED_TR, width), lambda i, ids: (ids[1], i, 0)),
                      pl.BlockSpec((3, _RED_TR, width), lambda i, ids: (0, i, 0))],
            out_specs=pl.BlockSpec((_RED_TR, width), lambda i, ids: (i, 0))),
        compiler_params=_cp("parallel"),
    )(ids, g32, recv_a, recv_b)


def _sum8(gathered):
    _, rows, width = gathered.shape

    def body(g_ref, o_ref):
        acc = g_ref[0]
        for d in range(1, N_DEV):
            acc = acc + g_ref[d]
        o_ref[...] = acc

    return pl.pallas_call(
        body, name="sum8",
        out_shape=jax.ShapeDtypeStruct((rows, width), F32),
        in_specs=[_VM], out_specs=_VM,
    )(gathered)


def _adamw(w, g, m, v, name):
    rows, cols = w.shape
    tr = rows
    for cand in (rows, 512, 256, 128, 64, 32, 16, 8):
        if rows % cand == 0 and cand * cols * 4 <= (3 << 20) // 2:
            tr = cand
            break
    c1 = 1.0 - ADAM_B1 ** ADAM_STEP
    c2 = 1.0 - ADAM_B2 ** ADAM_STEP

    def body(w_ref, g_ref, m_ref, v_ref, d_ref, mo_ref, vo_ref):
        gg = g_ref[...]
        mn = ADAM_B1 * m_ref[...] + (1.0 - ADAM_B1) * gg
        vn = ADAM_B2 * v_ref[...] + (1.0 - ADAM_B2) * (gg * gg)
        mo_ref[...] = mn
        vo_ref[...] = vn
        d_ref[...] = -ADAM_LR * ((mn / c1) / (jnp.sqrt(vn / c2) + ADAM_EPS) + ADAM_WD * w_ref[...])

    blk = pl.BlockSpec((tr, cols), lambda i: (i, 0))
    shp = jax.ShapeDtypeStruct((rows, cols), F32)
    return pl.pallas_call(
        body, name=name, out_shape=(shp, shp, shp), grid=(rows // tr,),
        in_specs=[blk] * 4, out_specs=(blk, blk, blk),
        compiler_params=_cp("parallel"),
    )(w, g, m, v)


def _rows128(a):
    return a.reshape(-1, 128)


def _pack_small(norm_pre, conv_b, ssd_norm, gate_bias, norm_post, dt_bias, a_log, d_skip, fgate_bias):
    tiny = jnp.concatenate([dt_bias.reshape(-1), a_log.reshape(-1), d_skip.reshape(-1), fgate_bias.reshape(-1),
                            jnp.zeros((16,), F32)])
    return jnp.concatenate([_rows128(norm_pre), _rows128(conv_b), _rows128(ssd_norm), _rows128(gate_bias),
                            _rows128(norm_post), tiny.reshape(1, 128)], axis=0)


_SMALL_ROWS = 73
_SMALL_PAD = 80


def _unpack_small(p):
    tiny = p[72]
    return dict(norm_pre=p[0:8].reshape(1, 1024), conv_b=p[8:32].reshape(1, 3072), ssd_norm=p[32:48].reshape(1, 2048),
                gate_bias=p[48:64].reshape(1, 2048), norm_post=p[64:72].reshape(1, 1024),
                dt_bias=tiny[0:32].reshape(1, 32), a_log=tiny[32:64].reshape(1, 32),
                d_skip=tiny[64:96].reshape(1, 32), fgate_bias=tiny[96:112].reshape(1, 16))


def _pad_rows(a, rows):
    return jnp.concatenate([a, jnp.zeros((rows - a.shape[0], a.shape[1]), a.dtype)], axis=0)


def kernel(x, meta_tokens, norm_pre, w_in, conv_w, conv_b, dt_bias, a_log, d_skip, ssd_norm, fgate_bias, gate_bias, w_proj_ssd, w_proj_att, w_out, norm_post, loss_target, m_meta_tokens, m_norm_pre, m_w_in, m_conv_w, m_conv_b, m_dt_bias, m_a_log, m_d_skip, m_ssd_norm, m_fgate_bias, m_gate_bias, m_w_proj_ssd, m_w_proj_att, m_w_out, m_norm_post, v_meta_tokens, v_norm_pre, v_w_in, v_conv_w, v_conv_b, v_dt_bias, v_a_log, v_d_skip, v_ssd_norm, v_fgate_bias, v_gate_bias, v_w_proj_ssd, v_w_proj_att, v_w_out, v_norm_post):
    cx, cy, cc = _coords()
    chip = 2 * cx + cy
    ids = jnp.stack([cc, chip]).astype(jnp.int32)
    seq = x.shape[1]

    w_in_sh = w_in[0].astype(BF16)
    w_pr_sh = jnp.concatenate([w_proj_ssd[0], w_proj_att[0], w_out[0]], axis=0).astype(BF16)
    g_in, g_pr = _gather_shards([w_in_sh, w_pr_sh])
    w_full = jnp.transpose(g_in, (1, 0, 2)).reshape(D_MODEL, N_COLS)
    w_main, w_small = _to_aligned_cols(w_full)
    wps = g_pr[:, 0:512].reshape(D_SSD, D_MODEL)
    wpa = g_pr[:, 512:768].reshape(D_ATT, D_MODEL)
    wout = g_pr[:, 768:1024].reshape(D_MODEL, D_MODEL)
    sm_sh = jnp.concatenate([_rows128(meta_tokens), _rows128(conv_w[0])], axis=0)
    sm_all = _allgather8(sm_sh, "gather_small_weights")[0::2]
    meta_full = jnp.transpose(sm_all[:, 0:32].reshape(4, N_META, 256), (1, 0, 2)).reshape(N_META, D_MODEL)
    conv_w_full = jnp.transpose(sm_all[:, 32:56].reshape(4, CONV_K, 768), (1, 0, 2)).reshape(CONV_K, CONV_DIM)

    h = jnp.concatenate([jnp.zeros((PADF, D_MODEL), F32), meta_full, x[0]], axis=0)
    bias_row = jnp.concatenate([dt_bias[0], fgate_bias[0], jnp.zeros((N_SMALL - H_SSD - H_ATT,), F32)]).reshape(1, N_SMALL)
    a_neg = -jnp.exp(a_log[0])
    a_row = jnp.concatenate([a_neg, jnp.zeros((N_SMALL - H_SSD,), F32)]).reshape(1, N_SMALL)
    dsk_row = jnp.repeat(d_skip[0], 64).reshape(1, D_SSD)
    r = _local_step(h, loss_target[0], w_main, w_small, wps, wpa, wout, norm_pre, conv_w_full, conv_b, bias_row,
                    a_row, dsk_row, ssd_norm, gate_bias, norm_post)
    dh = r["dh"]
    grad_x = dh[PADF + N_META:].reshape(1, seq, D_MODEL)

    tiny = r["d_bias_row"][0]
    part_small = _pack_small(r["d_norm_pre"], r["d_conv_b"], r["d_ssd_norm"], r["d_gate_bias"], r["d_norm_post"],
                             tiny[0:H_SSD], r["d_a"][0, 0:H_SSD] * a_neg, r["d_dsk"].reshape(H_SSD, 64).sum(axis=1),
                             tiny[H_SSD:H_SSD + H_ATT])
    part = jnp.concatenate([_pad_rows(part_small, _SMALL_PAD), _rows128(r["d_conv_w"]),
                            _rows128(dh[PADF:PADF + N_META]), r["loss_blk"]], axis=0)
    tot = _sum8(_allgather8(part, "gather_small_grads"))
    loss = tot[_SMALL_PAD + 96 + 128, 0]
    g_small = tot[0:_SMALL_PAD]
    g_conv_w = lax.dynamic_slice_in_dim(tot[_SMALL_PAD:_SMALL_PAD + 96].reshape(CONV_K, CONV_DIM), chip * 768, 768, axis=1)
    g_meta = lax.dynamic_slice_in_dim(tot[_SMALL_PAD + 96:_SMALL_PAD + 224].reshape(N_META, D_MODEL), chip * 256, 256, axis=1)

    d_w_in = _from_aligned_cols(r["d_w_main"], r["d_w_small"])
    g32_in = jnp.transpose(d_w_in.reshape(D_MODEL, 4, N_COLS // 4), (1, 0, 2))
    g32_pr = jnp.concatenate([r["d_wps"].reshape(4, 512, D_MODEL), r["d_wpa"].reshape(4, 256, D_MODEL),
                              r["d_wout"].reshape(4, 256, D_MODEL)], axis=1)
    ra_in, ra_pr = _pair_swap_halves([g32_in, g32_pr])
    pb_in = _add_pair(ids, g32_in, ra_in)
    pb_pr = _add_pair(ids, g32_pr, ra_pr)
    rb_in, rb_pr = _chip_exchange([pb_in, pb_pr])
    half_in = _add_chips(ids, g32_in, ra_in, rb_in)
    half_pr = _add_chips(ids, g32_pr, ra_pr, rb_pr)
    gw_in, gw_pr = _pair_join_halves([half_in, half_pr])

    upd = {}
    upd["w_in"] = (gw_in,) + _adamw(w_in[0], gw_in, m_w_in[0], v_w_in[0], "adamw_w_in")
    w_pr32 = jnp.concatenate([w_proj_ssd[0], w_proj_att[0], w_out[0]], axis=0)
    m_pr = jnp.concatenate([m_w_proj_ssd[0], m_w_proj_att[0], m_w_out[0]], axis=0)
    v_pr = jnp.concatenate([v_w_proj_ssd[0], v_w_proj_att[0], v_w_out[0]], axis=0)
    pr = (gw_pr,) + _adamw(w_pr32, gw_pr, m_pr, v_pr, "adamw_w_proj")
    upd["w_proj_ssd"] = tuple(a[0:512] for a in pr)
    upd["w_proj_att"] = tuple(a[512:768] for a in pr)
    upd["w_out"] = tuple(a[768:1024] for a in pr)
    upd["conv_w"] = (g_conv_w,) + _adamw(conv_w[0], g_conv_w, m_conv_w[0], v_conv_w[0], "adamw_conv_w")
    upd["meta_tokens"] = (g_meta,) + _adamw(meta_tokens, g_meta, m_meta_tokens, v_meta_tokens, "adamw_meta")
    pk = lambda np_, cb, sn, gb, npo, dtb, al, ds, fg: _pad_rows(_pack_small(np_, cb, sn, gb, npo, dtb, al, ds, fg), _SMALL_PAD)
    w_sm = pk(norm_pre, conv_b, ssd_norm, gate_bias, norm_post, dt_bias, a_log, d_skip, fgate_bias)
    m_sm = pk(m_norm_pre, m_conv_b, m_ssd_norm, m_gate_bias, m_norm_post, m_dt_bias, m_a_log, m_d_skip, m_fgate_bias)
    v_sm = pk(v_norm_pre, v_conv_b, v_ssd_norm, v_gate_bias, v_norm_post, v_dt_bias, v_a_log, v_d_skip, v_fgate_bias)
    sm = [_unpack_small(a) for a in (g_small,) + _adamw(w_sm, g_small, m_sm, v_sm, "adamw_small")]
    for name in ("norm_pre", "conv_b", "dt_bias", "a_log", "d_skip", "ssd_norm", "fgate_bias", "gate_bias", "norm_post"):
        upd[name] = tuple(s[name] for s in sm)
    lead = ("w_in", "conv_w", "w_proj_ssd", "w_proj_att", "w_out")
    order = ("meta_tokens", "norm_pre", "w_in", "conv_w", "conv_b", "dt_bias", "a_log", "d_skip", "ssd_norm",
             "fgate_bias", "gate_bias", "w_proj_ssd", "w_proj_att", "w_out", "norm_post")
    outs = [loss, grad_x]
    for part_i in range(4):
        for name in order:
            a = upd[name][part_i]
            outs.append(a[None] if name in lead else a)
    return tuple(outs)
```

```python
import functools
import math

import jax
import jax.numpy as jnp
from jax import lax
from jax.experimental import pallas as pl
from jax.experimental.pallas import tpu as pltpu

F32 = jnp.float32
BF16 = jnp.bfloat16
HIGHEST = lax.Precision.HIGHEST

D_MODEL = 1024
N_META = 16
CHUNK = 128
PADF = CHUNK - N_META
D_SSD = 2048
H_SSD = 32
G_SSD = 4
N_STATE = 128
CONV_K = 4
CONV_DIM = D_SSD + 2 * G_SSD * N_STATE
H_ATT = 16
D_ATT = 1024
EPS = 1e-6
N_COLS = 11312

C_Z, C_XBC, C_ZA, C_Q, C_K, C_V, C_G = 0, 2048, 5120, 6144, 7168, 8192, 9216
N_MAIN = 11264
N_SMALL = 128
O_Z, O_XBC, O_DT, O_ZA, O_Q, O_K, O_V, O_F, O_G = (
    (0, 2048), (2048, 3072), (5120, 32), (5152, 1024), (6176, 1024), (7200, 1024),
    (8224, 1024), (9248, 16), (9264, 2048))

ADAM_LR, ADAM_B1, ADAM_B2, ADAM_EPS, ADAM_WD, ADAM_STEP = 0.001, 0.9, 0.999, 1e-08, 0.01, 10

VMEM_LIMIT = 56 * 1024 * 1024


def _cp(*sem):
    return pltpu.CompilerParams(dimension_semantics=sem, vmem_limit_bytes=VMEM_LIMIT)


def _tile(n, prefs):
    for p in prefs:
        if n % p == 0:
            return p
    raise ValueError(f"no tile for {n} in {prefs}")


def _iota(shape, dim):
    return lax.broadcasted_iota(jnp.int32, shape, dim)


def _sigmoid(x):
    return 1.0 / (1.0 + jnp.exp(-x))


def _softplus_tail(x):
    return jnp.log(1.0 + jnp.exp(-jnp.abs(x)))


_NN = (((1,), (0,)), ((), ()))
_NT = (((1,), (1,)), ((), ()))
_TN = (((0,), (0,)), ((), ()))


def _dot(a, b, dims=_NN):
    return lax.dot_general(a, b, dims, preferred_element_type=F32)


def _dot_exact(a, b, dims=_NN):
    return lax.dot_general(a, b, dims, precision=HIGHEST, preferred_element_type=F32)


def _matmul(a, b, mode, out_dtype, name, tm, tn, tk):
    if mode == "tn":
        kdim, m = a.shape
    else:
        m, kdim = a.shape
    n = b.shape[0] if mode == "nt" else b.shape[1]
    nk = kdim // tk
    dims = {"nn": _NN, "nt": _NT, "tn": _TN}[mode]
    a_spec = (pl.BlockSpec((tk, tm), lambda i, j, k: (k, i)) if mode == "tn"
              else pl.BlockSpec((tm, tk), lambda i, j, k: (i, k)))
    b_spec = (pl.BlockSpec((tn, tk), lambda i, j, k: (j, k)) if mode == "nt"
              else pl.BlockSpec((tk, tn), lambda i, j, k: (k, j)))

    def body(a_ref, b_ref, o_ref, acc_ref):
        k = pl.program_id(2)
        p = _dot(a_ref[...].astype(BF16), b_ref[...].astype(BF16), dims)
        if nk == 1:
            o_ref[...] = p.astype(out_dtype)
        else:
            @pl.when(k == 0)
            def _():
                acc_ref[...] = p

            @pl.when(k > 0)
            def _():
                acc_ref[...] += p

            @pl.when(k == nk - 1)
            def _():
                o_ref[...] = acc_ref[...].astype(out_dtype)

    return pl.pallas_call(
        body, name=name,
        out_shape=jax.ShapeDtypeStruct((m, n), out_dtype),
        grid=(m // tm, n // tn, nk),
        in_specs=[a_spec, b_spec],
        out_specs=pl.BlockSpec((tm, tn), lambda i, j, k: (i, j)),
        scratch_shapes=[pltpu.VMEM((tm, tn), F32)],
        compiler_params=_cp("parallel", "parallel", "arbitrary"),
    )(a, b)


def _row_tile(t):
    return _tile(t, (352, 128))


def _row_tile_wide(t):
    return _tile(t, (176, 128))


def _norm1_fwd(h, g):
    t = h.shape[0]
    tm = _row_tile(t)

    def body(h_ref, g_ref, u_ref):
        x = h_ref[...]
        r = lax.rsqrt(jnp.mean(x * x, axis=-1, keepdims=True) + EPS)
        u_ref[...] = (x * r * g_ref[...]).astype(BF16)

    return pl.pallas_call(
        body, name="norm1_fwd",
        out_shape=jax.ShapeDtypeStruct((t, D_MODEL), BF16),
        grid=(t // tm,),
        in_specs=[pl.BlockSpec((tm, D_MODEL), lambda i: (i, 0)),
                  pl.BlockSpec((1, D_MODEL), lambda i: (0, 0))],
        out_specs=pl.BlockSpec((tm, D_MODEL), lambda i: (i, 0)),
        compiler_params=_cp("parallel"),
    )(h, g)


def _norm1_bwd(du_a, du_b, h, g, dy):
    t = h.shape[0]
    tm = _row_tile(t)

    def body(a_ref, b_ref, h_ref, g_ref, dy_ref, dh_ref, dg_ref):
        i = pl.program_id(0)
        x = h_ref[...]
        du = a_ref[...] + b_ref[...]
        r = lax.rsqrt(jnp.mean(x * x, axis=-1, keepdims=True) + EPS)
        gdu = du * g_ref[...]
        dh_ref[...] = dy_ref[...] + r * (gdu - x * (r * r) * jnp.mean(gdu * x, axis=-1, keepdims=True))
        part = jnp.sum(du * x * r, axis=0, keepdims=True)

        @pl.when(i == 0)
        def _():
            dg_ref[...] = part

        @pl.when(i > 0)
        def _():
            dg_ref[...] += part

    row = pl.BlockSpec((tm, D_MODEL), lambda i: (i, 0))
    vec = pl.BlockSpec((1, D_MODEL), lambda i: (0, 0))
    return pl.pallas_call(
        body, name="norm1_bwd",
        out_shape=(jax.ShapeDtypeStruct((t, D_MODEL), F32), jax.ShapeDtypeStruct((1, D_MODEL), F32)),
        grid=(t // tm,),
        in_specs=[row, row, row, vec, row],
        out_specs=(row, vec),
        compiler_params=_cp("arbitrary"),
    )(du_a, du_b, h, g, dy)


def _small_fwd(small, bias_row):
    t = small.shape[0]

    def body(s_ref, b_ref, o_ref, carry_ref):
        c = pl.program_id(0)

        @pl.when(c == 0)
        def _():
            carry_ref[...] = jnp.zeros_like(carry_ref)

        x = s_ref[...] + b_ref[...]
        r0 = _iota((CHUNK, CHUNK), 0)
        r1 = _iota((CHUNK, CHUNK), 1)
        valid = (c * CHUNK + r0) >= PADF
        tail = _softplus_tail(x)
        dt = jnp.where(valid & (r1 < H_SSD), jnp.maximum(x, 0.0) + tail, 0.0)
        lf = jnp.where(valid & (r1 >= H_SSD) & (r1 < H_SSD + H_ATT), jnp.minimum(x, 0.0) - tail, 0.0)
        tri = (r0 >= r1).astype(F32)
        cs = _dot_exact(tri, lf) + carry_ref[...]
        carry_ref[...] = cs[CHUNK - 1:CHUNK, :]
        o_ref[...] = dt + cs

    return pl.pallas_call(
        body, name="small_fwd",
        out_shape=jax.ShapeDtypeStruct((t, N_SMALL), F32),
        grid=(t // CHUNK,),
        in_specs=[pl.BlockSpec((CHUNK, N_SMALL), lambda c: (c, 0)),
                  pl.BlockSpec((1, N_SMALL), lambda c: (0, 0))],
        out_specs=pl.BlockSpec((CHUNK, N_SMALL), lambda c: (c, 0)),
        scratch_shapes=[pltpu.VMEM((1, N_SMALL), F32)],
        compiler_params=_cp("arbitrary"),
    )(small, bias_row)


def _small_bwd(dsm, small, bias_row):
    t = small.shape[0]
    nc = t // CHUNK

    def body(d_ref, s_ref, b_ref, o_ref, db_ref, carry_ref):
        step = pl.program_id(0)
        c = nc - 1 - step

        @pl.when(step == 0)
        def _():
            carry_ref[...] = jnp.zeros_like(carry_ref)
            db_ref[...] = jnp.zeros_like(db_ref)

        x = s_ref[...] + b_ref[...]
        d = d_ref[...]
        r0 = _iota((CHUNK, CHUNK), 0)
        r1 = _iota((CHUNK, CHUNK), 1)
        valid = (c * CHUNK + r0) >= PADF
        is_dt = r1 < H_SSD
        is_f = (r1 >= H_SSD) & (r1 < H_SSD + H_ATT)
        triu = (r1 >= r0).astype(F32)
        dc = jnp.where(is_f, d, 0.0)
        dlf = _dot_exact(triu, dc) + carry_ref[...]
        carry_ref[...] = dlf[0:1, :]
        sg = _sigmoid(x)
        out = jnp.where(valid & is_dt, d * sg, 0.0) + jnp.where(valid & is_f, dlf * (1.0 - sg), 0.0)
        o_ref[...] = out.astype(BF16)
        db_ref[...] += jnp.sum(out, axis=0, keepdims=True)

    blk = pl.BlockSpec((CHUNK, N_SMALL), lambda s: (nc - 1 - s, 0))
    vec = pl.BlockSpec((1, N_SMALL), lambda s: (0, 0))
    return pl.pallas_call(
        body, name="small_bwd",
        out_shape=(jax.ShapeDtypeStruct((t, N_SMALL), BF16), jax.ShapeDtypeStruct((1, N_SMALL), F32)),
        grid=(nc,),
        in_specs=[blk, blk, vec],
        out_specs=(blk, vec),
        scratch_shapes=[pltpu.VMEM((1, N_SMALL), F32)],
        compiler_params=_cp("arbitrary"),
    )(dsm, small, bias_row)


_CONV_TC = 1024
_XBC_BLK = C_XBC // _CONV_TC


def _shift_down(cur, prev8, j):
    rc = pltpu.roll(cur, j, 0)
    rid = _iota(prev8.shape, 0)
    top = jnp.where(rid < j, pltpu.roll(prev8, j, 0), rc[0:8, :])
    return jnp.concatenate([top, rc[8:, :]], axis=0)


def _shift_up(cur, next8, j):
    n = cur.shape[0]
    ru = pltpu.roll(cur, n - j, 0)
    rid = _iota(next8.shape, 0)
    bot = jnp.where(rid >= 8 - j, pltpu.roll(next8, 8 - j, 0), ru[n - 8:, :])
    return jnp.concatenate([ru[:n - 8, :], bot], axis=0)


def _conv_pre(x_ref, p_ref, w_ref, b_ref, i):
    cur = x_ref[...]
    prev = jnp.where(i > 0, p_ref[...], 0.0)
    w = w_ref[...]
    taps = [cur] + [_shift_down(cur, prev, j) for j in (1, 2, 3)]
    acc = b_ref[...] + taps[0] * w[3:4, :]
    for j in (1, 2, 3):
        acc = acc + taps[j] * w[3 - j:4 - j, :]
    return acc, taps


def _conv_fwd(proj, conv_w, conv_b):
    t = proj.shape[0]
    tr = _row_tile(t)

    def body(x_ref, p_ref, w_ref, b_ref, o_ref):
        i = pl.program_id(0)
        acc, _ = _conv_pre(x_ref, p_ref, w_ref, b_ref, i)
        valid = (i * tr + _iota(acc.shape, 0)) >= PADF
        o_ref[...] = jnp.where(valid, acc * _sigmoid(acc), 0.0)

    return pl.pallas_call(
        body, name="conv_fwd",
        out_shape=jax.ShapeDtypeStruct((t, CONV_DIM), F32),
        grid=(t // tr, CONV_DIM // _CONV_TC),
        in_specs=[pl.BlockSpec((tr, _CONV_TC), lambda i, j: (i, _XBC_BLK + j)),
                  pl.BlockSpec((8, _CONV_TC), lambda i, j: (jnp.maximum(i * (tr // 8) - 1, 0), _XBC_BLK + j)),
                  pl.BlockSpec((CONV_K, _CONV_TC), lambda i, j: (0, j)),
                  pl.BlockSpec((1, _CONV_TC), lambda i, j: (0, j))],
        out_specs=pl.BlockSpec((tr, _CONV_TC), lambda i, j: (i, j)),
        compiler_params=_cp("parallel", "parallel"),
    )(proj, proj, conv_w, conv_b)


def _conv_bwd_act(dxbc, proj, conv_w, conv_b):
    t = proj.shape[0]
    tr = _row_tile(t)

    def body(d_ref, x_ref, p_ref, w_ref, b_ref, da_ref, dw_ref, db_ref):
        i = pl.program_id(1)
        acc, taps = _conv_pre(x_ref, p_ref, w_ref, b_ref, i)
        valid = (i * tr + _iota(acc.shape, 0)) >= PADF
        sg = _sigmoid(acc)
        da = jnp.where(valid, d_ref[...] * sg * (1.0 + acc * (1.0 - sg)), 0.0)
        da_ref[...] = da
        dw = jnp.concatenate([jnp.sum(da * taps[3 - k], axis=0, keepdims=True) for k in range(CONV_K)], axis=0)
        db = jnp.sum(da, axis=0, keepdims=True)

        @pl.when(i == 0)
        def _():
            dw_ref[...] = dw
            db_ref[...] = db

        @pl.when(i > 0)
        def _():
            dw_ref[...] += dw
            db_ref[...] += db

    return pl.pallas_call(
        body, name="conv_bwd_act",
        out_shape=(jax.ShapeDtypeStruct((t, CONV_DIM), F32),
                   jax.ShapeDtypeStruct((CONV_K, CONV_DIM), F32),
                   jax.ShapeDtypeStruct((1, CONV_DIM), F32)),
        grid=(CONV_DIM // _CONV_TC, t // tr),
        in_specs=[pl.BlockSpec((tr, _CONV_TC), lambda j, i: (i, j)),
                  pl.BlockSpec((tr, _CONV_TC), lambda j, i: (i, _XBC_BLK + j)),
                  pl.BlockSpec((8, _CONV_TC), lambda j, i: (jnp.maximum(i * (tr // 8) - 1, 0), _XBC_BLK + j)),
                  pl.BlockSpec((CONV_K, _CONV_TC), lambda j, i: (0, j)),
                  pl.BlockSpec((1, _CONV_TC), lambda j, i: (0, j))],
        out_specs=(pl.BlockSpec((tr, _CONV_TC), lambda j, i: (i, j)),
                   pl.BlockSpec((CONV_K, _CONV_TC), lambda j, i: (0, j)),
                   pl.BlockSpec((1, _CONV_TC), lambda j, i: (0, j))),
        compiler_params=_cp("parallel", "arbitrary"),
    )(dxbc, proj, proj, conv_w, conv_b)


def _conv_bwd_in(da, conv_w):
    t = da.shape[0]
    tr = _row_tile(t)
    last8 = t // 8 - 1

    def body(d_ref, n_ref, w_ref, o_ref):
        i = pl.program_id(0)
        cur = d_ref[...]
        nxt = jnp.where(i < pl.num_programs(0) - 1, n_ref[...], 0.0)
        w = w_ref[...]
        acc = cur * w[3:4, :]
        for j in (1, 2, 3):
            acc = acc + _shift_up(cur, nxt, j) * w[3 - j:4 - j, :]
        o_ref[...] = acc.astype(BF16)

    return pl.pallas_call(
        body, name="conv_bwd_in",
        out_shape=jax.ShapeDtypeStruct((t, CONV_DIM), BF16),
        grid=(t // tr, CONV_DIM // _CONV_TC),
        in_specs=[pl.BlockSpec((tr, _CONV_TC), lambda i, j: (i, j)),
                  pl.BlockSpec((8, _CONV_TC), lambda i, j: (jnp.minimum((i + 1) * (tr // 8), last8), j)),
                  pl.BlockSpec((CONV_K, _CONV_TC), lambda i, j: (0, j))],
        out_specs=pl.BlockSpec((tr, _CONV_TC), lambda i, j: (i, j)),
        compiler_params=_cp("parallel", "parallel"),
    )(da, da, conv_w)


_GW = D_SSD // G_SSD


def _ssd_prelude(dt_ref, a_ref, e_scr, es_scr, dte_scr):
    r0 = _iota((CHUNK, CHUNK), 0)
    r1 = _iota((CHUNK, CHUNK), 1)
    dt = jnp.where(r1 < H_SSD, dt_ref[...], 0.0)
    adt = dt * a_ref[...]
    acs = _dot_exact((r0 >= r1).astype(F32), adt)
    acs_t = acs.T
    alast = acs[CHUNK - 1:CHUNK, :]
    exp_a = jnp.exp(acs)
    dec_s = jnp.exp(alast - acs)
    lo = r1 < 64
    for j in range(H_SSD // 2):
        sl = slice(CHUNK * j, CHUNK * (j + 1))
        e_scr[:, sl] = jnp.where(lo, exp_a[:, 2 * j:2 * j + 1], exp_a[:, 2 * j + 1:2 * j + 2])
        es_scr[:, sl] = jnp.where(lo, dec_s[:, 2 * j:2 * j + 1], dec_s[:, 2 * j + 1:2 * j + 2])
        dte_scr[:, sl] = jnp.where(lo, dt[:, 2 * j:2 * j + 1], dt[:, 2 * j + 1:2 * j + 2])
    return dt, acs, acs_t, r0, r1, lo


def _chunk_decay_rows(acs_t, g):
    cd_t = jnp.exp(acs_t[:, CHUNK - 1:CHUNK])
    return jnp.concatenate(
        [jnp.broadcast_to(cd_t[8 * g + hh:8 * g + hh + 1, :], (64, N_STATE)) for hh in range(8)], axis=0)


def _ssd_fwd(xbc, dtlf, a_row, dsk_row):
    t = xbc.shape[0]
    nc = t // CHUNK

    def body(xs_ref, b_ref, c_ref, dt_ref, a_ref, dsk_ref, y_ref, hin_ref, h_scr, e_scr, es_scr, dte_scr):
        c = pl.program_id(0)

        @pl.when(c == 0)
        def _():
            h_scr[...] = jnp.zeros_like(h_scr)

        dt, acs, acs_t, r0, r1, lo = _ssd_prelude(dt_ref, a_ref, e_scr, es_scr, dte_scr)
        causal = r0 >= r1
        for g in range(G_SSD):
            gs = slice(_GW * g, _GW * (g + 1))
            bg = b_ref[:, N_STATE * g:N_STATE * (g + 1)].astype(BF16)
            cg = c_ref[:, N_STATE * g:N_STATE * (g + 1)].astype(BF16)
            cb = _dot(cg, bg, _NT)
            hg = h_scr[gs, :]
            hin_ref[0, gs, :] = hg
            xg = xs_ref[:, gs] * dte_scr[:, gs]
            yoff = _dot(cg, hg.astype(BF16), _NT) * e_scr[:, gs]
            st = _dot((xg * es_scr[:, gs]).astype(BF16), bg, _TN)
            h_scr[gs, :] = hg * _chunk_decay_rows(acs_t, g) + st
            for jj in range(4):
                j = 4 * g + jj
                sl = slice(CHUNK * j, CHUNK * (j + 1))
                xp = xg[:, CHUNK * jj:CHUNK * (jj + 1)]
                acc = yoff[:, CHUNK * jj:CHUNK * (jj + 1)] + dsk_ref[:, sl] * xs_ref[:, sl]
                for hh in range(2):
                    h = 2 * j + hh
                    seg = acs[:, h:h + 1] - acs_t[h:h + 1, :]
                    lm = jnp.exp(jnp.where(causal, seg, -1e30))
                    m = (cb * lm).astype(BF16)
                    xh = jnp.where(lo if hh == 0 else ~lo, xp, 0.0).astype(BF16)
                    acc = acc + _dot(m, xh)
                y_ref[:, sl] = acc

    return pl.pallas_call(
        body, name="ssd_fwd",
        out_shape=(jax.ShapeDtypeStruct((t, D_SSD), F32), jax.ShapeDtypeStruct((nc, D_SSD, N_STATE), F32)),
        grid=(nc,),
        in_specs=[pl.BlockSpec((CHUNK, D_SSD), lambda c: (c, 0)),
                  pl.BlockSpec((CHUNK, _GW), lambda c: (c, 4)),
                  pl.BlockSpec((CHUNK, _GW), lambda c: (c, 5)),
                  pl.BlockSpec((CHUNK, N_SMALL), lambda c: (c, 0)),
                  pl.BlockSpec((1, N_SMALL), lambda c: (0, 0)),
                  pl.BlockSpec((1, D_SSD), lambda c: (0, 0))],
        out_specs=(pl.BlockSpec((CHUNK, D_SSD), lambda c: (c, 0)),
                   pl.BlockSpec((1, D_SSD, N_STATE), lambda c: (c, 0, 0))),
        scratch_shapes=[pltpu.VMEM((D_SSD, N_STATE), F32)] + [pltpu.VMEM((CHUNK, D_SSD), F32)] * 3,
        compiler_params=_cp("arbitrary"),
    )(xbc, xbc, xbc, dtlf, a_row, dsk_row)


def _ssd_bwd(xbc, dtlf, a_row, dsk_row, hin, dy):
    t = xbc.shape[0]
    nc = t // CHUNK

    def body(xs_ref, b_ref, c_ref, dt_ref, a_ref, dsk_ref, hin_ref, dy_ref,
             dxbc_ref, ddt_ref, da_ref, ddsk_ref, dh_scr, e_scr, es_scr, dte_scr, dx_scr):
        step = pl.program_id(0)

        @pl.when(step == 0)
        def _():
            dh_scr[...] = jnp.zeros_like(dh_scr)
            da_ref[...] = jnp.zeros_like(da_ref)
            ddsk_ref[...] = jnp.zeros_like(ddsk_ref)

        dt, acs, acs_t, r0, r1, lo = _ssd_prelude(dt_ref, a_ref, e_scr, es_scr, dte_scr)
        causal = r0 >= r1
        lane_row = _iota((1, CHUNK), 1)
        dacs = jnp.zeros((CHUNK, CHUNK), F32)
        dacs_t = jnp.zeros((CHUNK, CHUNK), F32)
        dalast = jnp.zeros((1, CHUNK), F32)
        ddt_dir = jnp.zeros((CHUNK, CHUNK), F32)
        ddsk_ref[...] += jnp.sum(dy_ref[...] * xs_ref[...], axis=0, keepdims=True)

        def head_rowsum(x, hh):
            return jnp.sum(jnp.where(lo if hh == 0 else ~lo, x, 0.0), axis=1, keepdims=True)

        for g in range(G_SSD):
            gs = slice(_GW * g, _GW * (g + 1))
            bg = b_ref[:, N_STATE * g:N_STATE * (g + 1)].astype(BF16)
            cg = c_ref[:, N_STATE * g:N_STATE * (g + 1)].astype(BF16)
            cb = _dot(cg, bg, _NT)
            hg = hin_ref[0, gs, :]
            hgb = hg.astype(BF16)
            dhn = dh_scr[gs, :]
            dhnb = dhn.astype(BF16)
            eg = e_scr[:, gs]
            esg = es_scr[:, gs]
            dyg = dy_ref[:, gs]
            xg = xs_ref[:, gs] * dte_scr[:, gs]
            ch = _dot(cg, hgb, _NT)
            dye = dyg * eg
            dyeb = dye.astype(BF16)
            dc = _dot(dyeb, hgb)
            dh_y = _dot(dyeb, cg, _TN)
            dxs = _dot(bg, dhnb, _NT)
            db = _dot((xg * esg).astype(BF16), dhnb)
            t_acs = dye * ch - dxs * xg * esg
            t_last = dxs * xg * esg
            cd = _chunk_decay_rows(acs_t, g)
            dh_scr[gs, :] = dhn * cd + dh_y
            t_cd = jnp.sum(dhn * hg * cd, axis=1, keepdims=True)
            dcb = jnp.zeros((CHUNK, CHUNK), F32)
            for jj in range(4):
                j = 4 * g + jj
                sl = slice(CHUNK * j, CHUNK * (j + 1))
                ps = slice(CHUNK * jj, CHUNK * (jj + 1))
                xp = xg[:, ps]
                xpb = xp.astype(BF16)
                dyp = dyg[:, ps]
                dxp = dxs[:, ps] * esg[:, ps]
                for hh in range(2):
                    h = 2 * j + hh
                    sel = lo if hh == 0 else ~lo
                    seg = acs[:, h:h + 1] - acs_t[h:h + 1, :]
                    lm = jnp.exp(jnp.where(causal, seg, -1e30))
                    mf = cb * lm
                    dyh = jnp.where(sel, dyp, 0.0).astype(BF16)
                    gm = _dot(dyh, xpb, _NT)
                    dcb = dcb + gm * lm
                    w = gm * mf
                    on_h = r1 == h
                    dacs = dacs + jnp.where(on_h, jnp.sum(w, axis=1, keepdims=True)
                                            + head_rowsum(t_acs[:, ps], hh), 0.0)
                    dacs_t = dacs_t - jnp.where(r0 == h, jnp.sum(w, axis=0, keepdims=True), 0.0)
                    dxp = dxp + _dot(mf.astype(BF16), dyh, _TN)
                    hrow = slice(CHUNK * jj + 64 * hh, CHUNK * jj + 64 * hh + 64)
                    s_last = (jnp.sum(head_rowsum(t_last[:, ps], hh), axis=0, keepdims=True)
                              + jnp.sum(t_cd[hrow, :], axis=0, keepdims=True))
                    dalast = dalast + jnp.where(lane_row == h, s_last, 0.0)
                dx_scr[:, sl] = dxp
                for hh in range(2):
                    ddt_dir = ddt_dir + jnp.where(r1 == 2 * j + hh,
                                                  head_rowsum(dxp * xs_ref[:, sl], hh), 0.0)
            dcbb = dcb.astype(BF16)
            dxbc_ref[:, D_SSD + N_STATE * g:D_SSD + N_STATE * (g + 1)] = db + _dot(dcbb, cg, _TN)
            dxbc_ref[:, D_SSD + _GW + N_STATE * g:D_SSD + _GW + N_STATE * (g + 1)] = dc + _dot(dcbb, bg)
        dxbc_ref[:, 0:D_SSD] = dx_scr[...] * dte_scr[...] + dsk_ref[...] * dy_ref[...]
        dacs = dacs + dacs_t.T + jnp.where(r0 == CHUNK - 1, dalast, 0.0)
        dadt = _dot_exact((r1 >= r0).astype(F32), dacs)
        ddt_ref[...] = dadt * a_ref[...] + ddt_dir
        da_ref[...] += jnp.sum(dadt * dt, axis=0, keepdims=True)

    rev = lambda s: (nc - 1 - s, 0)
    return pl.pallas_call(
        body, name="ssd_bwd",
        out_shape=(jax.ShapeDtypeStruct((t, CONV_DIM), F32), jax.ShapeDtypeStruct((t, N_SMALL), F32),
                   jax.ShapeDtypeStruct((1, N_SMALL), F32), jax.ShapeDtypeStruct((1, D_SSD), F32)),
        grid=(nc,),
        in_specs=[pl.BlockSpec((CHUNK, D_SSD), rev),
                  pl.BlockSpec((CHUNK, _GW), lambda s: (nc - 1 - s, 4)),
                  pl.BlockSpec((CHUNK, _GW), lambda s: (nc - 1 - s, 5)),
                  pl.BlockSpec((CHUNK, N_SMALL), rev),
                  pl.BlockSpec((1, N_SMALL), lambda s: (0, 0)),
                  pl.BlockSpec((1, D_SSD), lambda s: (0, 0)),
                  pl.BlockSpec((1, D_SSD, N_STATE), lambda s: (nc - 1 - s, 0, 0)),
                  pl.BlockSpec((CHUNK, D_SSD), rev)],
        out_specs=(pl.BlockSpec((CHUNK, CONV_DIM), rev),
                   pl.BlockSpec((CHUNK, N_SMALL), rev),
                   pl.BlockSpec((1, N_SMALL), lambda s: (0, 0)),
                   pl.BlockSpec((1, D_SSD), lambda s: (0, 0))),
        scratch_shapes=[pltpu.VMEM((D_SSD, N_STATE), F32)] + [pltpu.VMEM((CHUNK, D_SSD), F32)] * 4,
        compiler_params=_cp("arbitrary"),
    )(xbc, xbc, xbc, dtlf, a_row, dsk_row, hin, dy)


_NPAIR = H_ATT // 2
_QB, _KB, _VB = C_Q // 128, C_K // 128, C_V // 128
_SCALE = 1.0 / math.sqrt(64.0)


def _attn_blocks(t):
    return _tile(t, (1408, 384, 256, 128)), _tile(t, (384, 128))


def _split3(c):
    hi = c.astype(BF16).astype(F32)
    rest = c - hi
    mid = rest.astype(BF16).astype(F32)
    return hi, mid, rest - mid


def _head_lanes(lane, hh):
    return (lane < 64, 64) if hh == 0 else (lane >= 64, 0)


def _q_operand(q, cq, lane, hh):
    sel, first = _head_lanes(lane, hh)
    out = jnp.where(sel, q, 0.0)
    for n, col in enumerate(_split3(cq) + (1.0, 1.0, 1.0)):
        out = jnp.where(lane == first + n, col, out)
    return out.astype(BF16)


def _k_operand(k, ck, lane, hh):
    sel, first = _head_lanes(lane, hh)
    hi, mid, lo = _split3(ck)
    out = jnp.where(sel, k, 0.0)
    for n, col in enumerate((1.0, 1.0, 1.0, -hi, -mid, -lo)):
        out = jnp.where(lane == first + n, col, out)
    return out.astype(BF16)


def _needs_mask(i, kk, bq, bk):
    return kk * bk + bk - 1 > i * bq


_C_FILLER = 2.0 ** 30


def _attn_fwd(proj, c_col):
    t = proj.shape[0]
    bq, bk = _attn_blocks(t)
    nq, nk = t // bq, t // bk
    rs = 16

    def last_kv(i):
        return (i * bq + bq - 1) // bk

    def body(q_ref, k_ref, v_ref, cq_ref, ck_ref, o_ref, lse_ref, qs_scr, s_scr, p_scr, m_scr, acc_scr):
        i = pl.program_id(1)
        kk = pl.program_id(2)
        lane_q = _iota((bq, 128), 1)

        @pl.when(kk == 0)
        def _():
            m_scr[...] = jnp.full_like(m_scr, -1e30)
            acc_scr[...] = jnp.zeros_like(acc_scr)
            q = q_ref[...] * _SCALE
            cq = cq_ref[0]
            for hh in range(2):
                qs_scr[hh] = _q_operand(q, cq[:, hh:hh + 1], lane_q, hh)

        def step(masked):
            lane_k = _iota((bk, 128), 1)
            k = k_ref[...]
            v = v_ref[...]
            ck = ck_ref[0]
            ahead = _iota((rs, bq), 0) - _iota((rs, bq), 1)
            for hh in range(2):
                sel, first = _head_lanes(lane_k, hh)
                ks = _k_operand(k, ck[:, hh:hh + 1], lane_k, hh)
                vs = jnp.where(sel, v, jnp.where(lane_k == first, 1.0, 0.0)).astype(BF16)
                s_scr[...] = _dot(ks, qs_scr[hh], _NT)

                def block_max(r, mx):
                    rows = pl.ds(pl.multiple_of(r * rs, rs), rs)
                    s = s_scr[rows, :]
                    if masked:
                        s = jnp.where(ahead <= i * bq - kk * bk - r * rs, s, -1e30)
                        s_scr[rows, :] = s
                    return jnp.maximum(mx, s)

                mx = lax.fori_loop(0, bk // rs, block_max, jnp.full((rs, bq), -1e30, F32), unroll=True)
                m_old = m_scr[hh]
                m_new = jnp.maximum(m_old, jnp.max(mx, axis=0, keepdims=True))
                m_scr[hh] = m_new

                def probs(r, carry):
                    rows = pl.ds(pl.multiple_of(r * rs, rs), rs)
                    p_scr[rows, :] = jnp.exp(s_scr[rows, :] - m_new).astype(BF16)
                    return carry

                lax.fori_loop(0, bk // rs, probs, 0, unroll=True)
                acc_scr[hh] = acc_scr[hh] * jnp.exp(m_old - m_new) + _dot(vs, p_scr[...], _TN)

        active = kk <= last_kv(i)
        masked = _needs_mask(i, kk, bq, bk)

        @pl.when(active & masked)
        def _():
            step(True)

        @pl.when(active & jnp.logical_not(masked))
        def _():
            step(False)

        @pl.when(kk == nk - 1)
        def _():
            a = acc_scr[0]
            b = acc_scr[1]
            la = a[64:65, :]
            lb = b[0:1, :]
            o_ref[...] = jnp.where(lane_q < 64, (a / la).T, (b / lb).T)
            lse_ref[0] = jnp.concatenate([m_scr[0] + jnp.log(la), m_scr[1] + jnp.log(lb)], axis=0)

    kvi = lambda i, kk: jnp.minimum(kk, last_kv(i))
    kv = lambda off: pl.BlockSpec((bk, 128), lambda j, i, kk: (kvi(i, kk), off + j))
    return pl.pallas_call(
        body, name="attn_fwd",
        out_shape=(jax.ShapeDtypeStruct((t, D_ATT), F32), jax.ShapeDtypeStruct((_NPAIR, 2, t), F32)),
        grid=(_NPAIR, nq, nk),
        in_specs=[pl.BlockSpec((bq, 128), lambda j, i, kk: (i, _QB + j)),
                  kv(_KB), kv(_VB),
                  pl.BlockSpec((1, bq, 2), lambda j, i, kk: (j, i, 0)),
                  pl.BlockSpec((1, bk, 2), lambda j, i, kk: (j, kvi(i, kk), 0))],
        out_specs=(pl.BlockSpec((bq, 128), lambda j, i, kk: (i, j)),
                   pl.BlockSpec((1, 2, bq), lambda j, i, kk: (j, 0, i))),
        scratch_shapes=[pltpu.VMEM((2, bq, 128), BF16), pltpu.VMEM((bk, bq), F32), pltpu.VMEM((bk, bq), BF16),
                        pltpu.VMEM((2, 1, bq), F32), pltpu.VMEM((2, 128, bq), F32)],
        compiler_params=_cp("parallel", "parallel", "arbitrary"),
    )(proj, proj, proj, c_col, c_col)


def _attn_delta(do, o):
    t = do.shape[0]
    tm = _row_tile(t)

    def body(do_ref, o_ref, d_ref):
        pick = (jnp.right_shift(_iota((D_ATT, 128), 0), 6) == _iota((D_ATT, 128), 1)).astype(F32)
        d_ref[...] = _dot_exact(do_ref[...] * o_ref[...], pick)

    row = pl.BlockSpec((tm, D_ATT), lambda i: (i, 0))
    return pl.pallas_call(
        body, name="attn_delta",
        out_shape=jax.ShapeDtypeStruct((t, 128), F32),
        grid=(t // tm,), in_specs=[row, row], out_specs=pl.BlockSpec((tm, 128), lambda i: (i, 0)),
        compiler_params=_cp("parallel"),
    )(do, o)


def _attn_bwd(proj, c_col, lse_row, dl_row, do):
    t = proj.shape[0]
    bq, bk = _attn_blocks(t)
    nq, nk = t // bq, t // bk
    rs = 16

    def first_q(kk):
        return (kk * bk) // bq

    def body(q_ref, k_ref, v_ref, cq_ref, ck_ref, lse_ref, dl_ref, do_ref,
             dq_ref, dk_ref, dv_ref, dck_ref, dcq_ref,
             qs_scr, doh_scr, ks_scr, s_scr, dp_scr, p_scr, ds_scr, dq_scr, dk_scr, dv_scr):
        kk = pl.program_id(1)
        i = pl.program_id(2)
        lane_q = _iota((bq, 128), 1)
        lane_k = _iota((bk, 128), 1)
        qrows = pl.ds(pl.multiple_of(i * bq, 128), bq)

        @pl.when(kk == 0)
        def _():
            q = q_ref[...] * _SCALE
            cq = cq_ref[0]
            do_ = do_ref[...]
            for hh in range(2):
                qs_scr[hh, qrows, :] = _q_operand(q, cq[:, hh:hh + 1], lane_q, hh)
                doh_scr[hh, qrows, :] = jnp.where(_head_lanes(lane_q, hh)[0], do_, 0.0).astype(BF16)
                dq_scr[hh, qrows, :] = jnp.zeros((bq, 128), F32)

        @pl.when(i == 0)
        def _():
            dk_scr[...] = jnp.zeros_like(dk_scr)
            dv_scr[...] = jnp.zeros_like(dv_scr)
            k = k_ref[...]
            ck = ck_ref[0]
            for hh in range(2):
                ks_scr[hh] = _k_operand(k, ck[:, hh:hh + 1], lane_k, hh)

        def step(masked):
            v16 = v_ref[...].astype(BF16)
            lse = lse_ref[0]
            dl = dl_ref[0]
            ahead = _iota((rs, bq), 0) - _iota((rs, bq), 1)
            for hh in range(2):
                qs = qs_scr[hh, qrows, :]
                doh = doh_scr[hh, qrows, :]
                s_scr[...] = _dot(ks_scr[hh], qs, _NT)
                dp_scr[...] = _dot(v16, doh, _NT)

                def strip(r, carry):
                    rows = pl.ds(pl.multiple_of(r * rs, rs), rs)
                    p = jnp.exp(s_scr[rows, :] - lse[hh:hh + 1, :])
                    if masked:
                        p = jnp.where(ahead <= i * bq - kk * bk - r * rs, p, 0.0)
                    p_scr[rows, :] = p.astype(BF16)
                    ds_scr[rows, :] = (p * (dp_scr[rows, :] - dl[hh:hh + 1, :])).astype(BF16)
                    return carry

                lax.fori_loop(0, bk // rs, strip, 0, unroll=True)
                dv_scr[...] += _dot(p_scr[...], doh)
                dk_scr[hh] += _dot(ds_scr[...], qs)
                dq_scr[hh, qrows, :] += _dot(ds_scr[...], ks_scr[hh], _TN)

        active = i >= first_q(kk)
        masked = _needs_mask(i, kk, bq, bk)

        @pl.when(active & masked)
        def _():
            step(True)

        @pl.when(active & jnp.logical_not(masked))
        def _():
            step(False)

        @pl.when(i == nq - 1)
        def _():
            dka = dk_scr[0]
            dkb = dk_scr[1]
            dk_ref[...] = jnp.where(lane_k < 64, dka, dkb).astype(BF16)
            dv_ref[...] = dv_scr[...].astype(BF16)
            dck_ref[0] = -jnp.where(_iota((bk, 2), 1) == 0, dka[:, 67:68], dkb[:, 3:4])

        @pl.when((kk == nk - 1) & (i == nq - 1))
        def _():
            lane_t = _iota((t, 128), 1)
            dqa = dq_scr[0]
            dqb = dq_scr[1]
            dq_ref[...] = (jnp.where(lane_t < 64, dqa, dqb) * _SCALE).astype(BF16)
            dcq_ref[0] = jnp.where(_iota((t, 2), 1) == 0, dqa[:, 64:65], dqb[:, 0:1])

    qi = lambda kk, i: jnp.where(kk == 0, i, nq - 1)
    qspec = lambda off: pl.BlockSpec((bq, 128), lambda j, kk, i: (qi(kk, i), off + j))
    kspec = lambda off: pl.BlockSpec((bk, 128), lambda j, kk, i: (kk, off + j))
    rowspec = pl.BlockSpec((1, 2, bq), lambda j, kk, i: (j, 0, jnp.maximum(i, first_q(kk))))
    return pl.pallas_call(
        body, name="attn_bwd",
        out_shape=(jax.ShapeDtypeStruct((t, D_ATT), BF16), jax.ShapeDtypeStruct((t, D_ATT), BF16),
                   jax.ShapeDtypeStruct((t, D_ATT), BF16), jax.ShapeDtypeStruct((_NPAIR, t, 2), F32),
                   jax.ShapeDtypeStruct((_NPAIR, t, 2), F32)),
        grid=(_NPAIR, nk, nq),
        in_specs=[qspec(_QB), kspec(_KB), kspec(_VB),
                  pl.BlockSpec((1, bq, 2), lambda j, kk, i: (j, qi(kk, i), 0)),
                  pl.BlockSpec((1, bk, 2), lambda j, kk, i: (j, kk, 0)),
                  rowspec, rowspec, qspec(0)],
        out_specs=(pl.BlockSpec((t, 128), lambda j, kk, i: (0, j)),
                   pl.BlockSpec((bk, 128), lambda j, kk, i: (kk, j)),
                   pl.BlockSpec((bk, 128), lambda j, kk, i: (kk, j)),
                   pl.BlockSpec((1, bk, 2), lambda j, kk, i: (j, kk, 0)),
                   pl.BlockSpec((1, t, 2), lambda j, kk, i: (j, 0, 0))),
        scratch_shapes=[pltpu.VMEM((2, t, 128), BF16), pltpu.VMEM((2, t, 128), BF16), pltpu.VMEM((2, bk, 128), BF16),
                        pltpu.VMEM((bk, bq), F32), pltpu.VMEM((bk, bq), F32),
                        pltpu.VMEM((bk, bq), BF16), pltpu.VMEM((bk, bq), BF16),
                        pltpu.VMEM((2, t, 128), F32), pltpu.VMEM((2, bk, 128), F32), pltpu.VMEM((bk, 128), F32)],
        compiler_params=_cp("parallel", "arbitrary", "arbitrary"),
    )(proj, proj, proj, c_col, c_col, lse_row, dl_row, do)


def _premerge_fwd(y, o, proj, gamma):
    t = y.shape[0]
    tm = _row_tile_wide(t)

    def body(y_ref, z_ref, o_ref, za_ref, g_ref, ys_ref, ya_ref):
        z = z_ref[...]
        u = y_ref[...] * (z * _sigmoid(z))
        for g in range(G_SSD):
            gs = slice(_GW * g, _GW * (g + 1))
            ug = u[:, gs]
            r = lax.rsqrt(jnp.mean(ug * ug, axis=-1, keepdims=True) + EPS)
            ys_ref[:, gs] = (ug * r * g_ref[:, gs]).astype(BF16)
        za = za_ref[...]
        ya_ref[...] = (o_ref[...] * (za * _sigmoid(za))).astype(BF16)

    return pl.pallas_call(
        body, name="premerge_fwd",
        out_shape=(jax.ShapeDtypeStruct((t, D_SSD), BF16), jax.ShapeDtypeStruct((t, D_ATT), BF16)),
        grid=(t // tm,),
        in_specs=[pl.BlockSpec((tm, D_SSD), lambda i: (i, 0)),
                  pl.BlockSpec((tm, D_SSD), lambda i: (i, C_Z // D_SSD)),
                  pl.BlockSpec((tm, D_ATT), lambda i: (i, 0)),
                  pl.BlockSpec((tm, D_ATT), lambda i: (i, C_ZA // D_ATT)),
                  pl.BlockSpec((1, D_SSD), lambda i: (0, 0))],
        out_specs=(pl.BlockSpec((tm, D_SSD), lambda i: (i, 0)), pl.BlockSpec((tm, D_ATT), lambda i: (i, 0))),
        compiler_params=_cp("parallel"),
    )(y, proj, o, proj, gamma)


def _premerge_bwd(dys, dya, y, o, proj, gamma):
    t = y.shape[0]
    tm = _row_tile_wide(t)

    def body(dys_ref, dya_ref, y_ref, z_ref, o_ref, za_ref, g_ref, dy_ref, dz_ref, do_ref, dza_ref, dg_ref):
        i = pl.program_id(0)
        z = z_ref[...]
        sz = _sigmoid(z)
        silu = z * sz
        dsilu = sz * (1.0 + z * (1.0 - sz))
        yv = y_ref[...]
        u = yv * silu
        parts = []
        for g in range(G_SSD):
            gs = slice(_GW * g, _GW * (g + 1))
            ug = u[:, gs]
            r = lax.rsqrt(jnp.mean(ug * ug, axis=-1, keepdims=True) + EPS)
            n = ug * r
            dout = dys_ref[:, gs]
            dn = dout * g_ref[:, gs]
            du = r * (dn - n * jnp.mean(dn * n, axis=-1, keepdims=True))
            dy_ref[:, gs] = du * silu[:, gs]
            dz_ref[:, gs] = (du * yv[:, gs] * dsilu[:, gs]).astype(BF16)
            parts.append(jnp.sum(dout * n, axis=0, keepdims=True))
        dg = jnp.concatenate(parts, axis=1)
        za = za_ref[...]
        sa = _sigmoid(za)
        dya_ = dya_ref[...]
        do_ref[...] = dya_ * (za * sa)
        dza_ref[...] = (dya_ * o_ref[...] * (sa * (1.0 + za * (1.0 - sa)))).astype(BF16)

        @pl.when(i == 0)
        def _():
            dg_ref[...] = dg

        @pl.when(i > 0)
        def _():
            dg_ref[...] += dg

    ssd = pl.BlockSpec((tm, D_SSD), lambda i: (i, 0))
    att = pl.BlockSpec((tm, D_ATT), lambda i: (i, 0))
    vec = pl.BlockSpec((1, D_SSD), lambda i: (0, 0))
    return pl.pallas_call(
        body, name="premerge_bwd",
        out_shape=(jax.ShapeDtypeStruct((t, D_SSD), F32), jax.ShapeDtypeStruct((t, D_SSD), BF16),
                   jax.ShapeDtypeStruct((t, D_ATT), F32), jax.ShapeDtypeStruct((t, D_ATT), BF16),
                   jax.ShapeDtypeStruct((1, D_SSD), F32)),
        grid=(t // tm,),
        in_specs=[ssd, att, ssd, pl.BlockSpec((tm, D_SSD), lambda i: (i, C_Z // D_SSD)), att,
                  pl.BlockSpec((tm, D_ATT), lambda i: (i, C_ZA // D_ATT)), vec],
        out_specs=(ssd, ssd, att, att, vec),
        compiler_params=_cp("arbitrary"),
    )(dys, dya, y, proj, o, proj, gamma)


_G_BLK = C_G // D_MODEL


def _merge_fwd(a, b, proj, gate_bias):
    t = a.shape[0]
    tm = _row_tile(t)

    def body(a_ref, b_ref, gs_ref, ga_ref, bias_ref, m_ref):
        g_ssd = _sigmoid(gs_ref[...] + bias_ref[:, 0:D_MODEL])
        g_att = _sigmoid(ga_ref[...] + bias_ref[:, D_MODEL:2 * D_MODEL])
        m_ref[...] = (g_ssd * a_ref[...] + g_att * b_ref[...]).astype(BF16)

    row = pl.BlockSpec((tm, D_MODEL), lambda i: (i, 0))
    return pl.pallas_call(
        body, name="merge_fwd",
        out_shape=jax.ShapeDtypeStruct((t, D_MODEL), BF16),
        grid=(t // tm,),
        in_specs=[row, row,
                  pl.BlockSpec((tm, D_MODEL), lambda i: (i, _G_BLK)),
                  pl.BlockSpec((tm, D_MODEL), lambda i: (i, _G_BLK + 1)),
                  pl.BlockSpec((1, 2 * D_MODEL), lambda i: (0, 0))],
        out_specs=row,
        compiler_params=_cp("parallel"),
    )(a, b, proj, proj, gate_bias)


def _merge_bwd(dm, a, b, proj, gate_bias):
    t = a.shape[0]
    tm = _row_tile(t)

    def body(dm_ref, a_ref, b_ref, gs_ref, ga_ref, bias_ref, da_ref, db_ref, dg_ref, dbias_ref):
        i = pl.program_id(0)
        dm_ = dm_ref[...]
        g_ssd = _sigmoid(gs_ref[...] + bias_ref[:, 0:D_MODEL])
        g_att = _sigmoid(ga_ref[...] + bias_ref[:, D_MODEL:2 * D_MODEL])
        da_ref[...] = (dm_ * g_ssd).astype(BF16)
        db_ref[...] = (dm_ * g_att).astype(BF16)
        dgs = dm_ * a_ref[...] * g_ssd * (1.0 - g_ssd)
        dga = dm_ * b_ref[...] * g_att * (1.0 - g_att)
        dg_ref[:, 0:D_MODEL] = dgs.astype(BF16)
        dg_ref[:, D_MODEL:2 * D_MODEL] = dga.astype(BF16)
        part = jnp.concatenate([jnp.sum(dgs, axis=0, keepdims=True), jnp.sum(dga, axis=0, keepdims=True)], axis=1)

        @pl.when(i == 0)
        def _():
            dbias_ref[...] = part

        @pl.when(i > 0)
        def _():
            dbias_ref[...] += part

    row = pl.BlockSpec((tm, D_MODEL), lambda i: (i, 0))
    wide = pl.BlockSpec((tm, 2 * D_MODEL), lambda i: (i, 0))
    vec = pl.BlockSpec((1, 2 * D_MODEL), lambda i: (0, 0))
    return pl.pallas_call(
        body, name="merge_bwd",
        out_shape=(jax.ShapeDtypeStruct((t, D_MODEL), BF16), jax.ShapeDtypeStruct((t, D_MODEL), BF16),
                   jax.ShapeDtypeStruct((t, 2 * D_MODEL), BF16), jax.ShapeDtypeStruct((1, 2 * D_MODEL), F32)),
        grid=(t // tm,),
        in_specs=[row, row, row,
                  pl.BlockSpec((tm, D_MODEL), lambda i: (i, _G_BLK)),
                  pl.BlockSpec((tm, D_MODEL), lambda i: (i, _G_BLK + 1)), vec],
        out_specs=(row, row, wide, vec),
        compiler_params=_cp("arbitrary"),
    )(dm, a, b, proj, proj, gate_bias)


def _post(o2, h, target, g):
    t = o2.shape[0]
    nc = t // CHUNK

    def body(o_ref, h_ref, t_ref, g_ref, dy_ref, do_ref, dg_ref, loss_ref):
        c = pl.program_id(0)
        x = o_ref[...]
        r = lax.rsqrt(jnp.mean(x * x, axis=-1, keepdims=True) + EPS)
        n = x * r
        y = h_ref[...] + n * g_ref[...]
        diff = jnp.where(c > 0, y - t_ref[...], 0.0)
        dy = diff * (1.0 / D_MODEL)
        dy_ref[...] = dy
        gdy = dy * g_ref[...]
        do_ref[...] = (r * (gdy - n * jnp.mean(gdy * n, axis=-1, keepdims=True))).astype(BF16)
        dg = jnp.sum(dy * n, axis=0, keepdims=True)
        lpart = 0.5 * jnp.sum(jnp.sum(diff * diff, axis=1, keepdims=True), axis=0, keepdims=True) * (1.0 / D_MODEL)
        sel = (_iota((8, 128), 0) == 0) & (_iota((8, 128), 1) == 0)

        @pl.when(c == 0)
        def _():
            dg_ref[...] = dg
            loss_ref[...] = jnp.zeros_like(loss_ref)

        @pl.when(c > 0)
        def _():
            dg_ref[...] += dg
            loss_ref[...] += jnp.where(sel, lpart, 0.0)

    row = pl.BlockSpec((CHUNK, D_MODEL), lambda c: (c, 0))
    vec = pl.BlockSpec((1, D_MODEL), lambda c: (0, 0))
    return pl.pallas_call(
        body, name="post",
        out_shape=(jax.ShapeDtypeStruct((t, D_MODEL), F32), jax.ShapeDtypeStruct((t, D_MODEL), BF16),
                   jax.ShapeDtypeStruct((1, D_MODEL), F32), jax.ShapeDtypeStruct((8, 128), F32)),
        grid=(nc,),
        in_specs=[row, row, pl.BlockSpec((CHUNK, D_MODEL), lambda c: (jnp.maximum(c - 1, 0), 0)), vec],
        out_specs=(row, row, vec, pl.BlockSpec((8, 128), lambda c: (0, 0))),
        compiler_params=_cp("arbitrary"),
    )(o2, h, target, g)


def _mm_tiles(t):
    return _tile(t, (704, 384, 128))


def _local_step(h, target, w_main, w_small, wps, wpa, wout, norm_pre, conv_w, conv_b, bias_row, a_row,
                dsk_row, ssd_norm, gate_bias, norm_post):
    t = h.shape[0]
    tm = _mm_tiles(t)
    u = _norm1_fwd(h, norm_pre)
    proj = _matmul(u, w_main, "nn", F32, "inproj", tm, 1024, D_MODEL)
    small = _matmul(u, w_small, "nn", F32, "inproj_small", tm, N_SMALL, D_MODEL)
    dtlf = _small_fwd(small, bias_row)
    xbc = _conv_fwd(proj, conv_w, conv_b)
    y, hin = _ssd_fwd(xbc, dtlf, a_row, dsk_row)
    c_tok = dtlf[:, H_SSD:H_SSD + H_ATT]
    c_tok = jnp.where(jnp.arange(t)[:, None] < PADF, _C_FILLER, c_tok)
    c_col = c_tok.reshape(t, _NPAIR, 2).transpose(1, 0, 2)
    o, lse = _attn_fwd(proj, c_col)
    ys, ya = _premerge_fwd(y, o, proj, ssd_norm)
    a = _matmul(ys, wps, "nn", F32, "proj_ssd", tm, D_MODEL, D_SSD)
    b = _matmul(ya, wpa, "nn", F32, "proj_att", tm, D_MODEL, D_ATT)
    merged = _merge_fwd(a, b, proj, gate_bias)
    o2 = _matmul(merged, wout, "nn", F32, "out_proj", tm, D_MODEL, D_MODEL)
    dy_out, do2, d_norm_post, loss_blk = _post(o2, h, target, norm_post)

    dm = _matmul(do2, wout, "nt", F32, "out_proj_dx", tm, D_MODEL, D_MODEL)
    d_wout = _matmul(merged, do2, "tn", F32, "out_proj_dw", D_MODEL, D_MODEL, tm)
    da, db, dgraw, d_gate_bias = _merge_bwd(dm, a, b, proj, gate_bias)
    dys = _matmul(da, wps, "nt", F32, "proj_ssd_dx", tm, D_SSD, D_MODEL)
    d_wps = _matmul(ys, da, "tn", F32, "proj_ssd_dw", D_SSD, D_MODEL, tm)
    dya = _matmul(db, wpa, "nt", F32, "proj_att_dx", tm, D_ATT, D_MODEL)
    d_wpa = _matmul(ya, db, "tn", F32, "proj_att_dw", D_ATT, D_MODEL, tm)
    dy, dz, do, dza, d_ssd_norm = _premerge_bwd(dys, dya, y, o, proj, ssd_norm)
    dl_row = _attn_delta(do, o)[:, 0:H_ATT].T.reshape(_NPAIR, 2, t)
    dq, dk, dv, dc_key, dc_qry = _attn_bwd(proj, c_col, lse, dl_row, do)
    dxbc, ddt, d_a, d_dsk = _ssd_bwd(xbc, dtlf, a_row, dsk_row, hin, dy)
    dact, d_conv_w, d_conv_b = _conv_bwd_act(dxbc, proj, conv_w, conv_b)
    dxbc_raw = _conv_bwd_in(dact, conv_w)
    dc_tok = jnp.transpose(dc_key + dc_qry, (1, 0, 2)).reshape(t, H_ATT)
    dsm = ddt + jnp.pad(dc_tok, ((0, 0), (H_SSD, N_SMALL - H_SSD - H_ATT)))
    dsmall, d_bias_row = _small_bwd(dsm, small, bias_row)
    dproj = jnp.concatenate([dz, dxbc_raw, dza, dq, dk, dv, dgraw], axis=1)
    du_a = _matmul(dproj, w_main, "nt", F32, "inproj_dx", tm, D_MODEL, 1024)
    du_b = _matmul(dsmall, w_small, "nt", F32, "inproj_small_dx", tm, D_MODEL, N_SMALL)
    d_w_main = _matmul(u, dproj, "tn", F32, "inproj_dw", D_MODEL, 1024, tm)
    d_w_small = _matmul(u, dsmall, "tn", F32, "inproj_small_dw", D_MODEL, N_SMALL, tm)
    dh, d_norm_pre = _norm1_bwd(du_a, du_b, h, norm_pre, dy_out)
    return dict(loss_blk=loss_blk, dh=dh, d_w_main=d_w_main, d_w_small=d_w_small, d_wps=d_wps, d_wpa=d_wpa,
                d_wout=d_wout, d_norm_pre=d_norm_pre, d_conv_w=d_conv_w, d_conv_b=d_conv_b,
                d_bias_row=d_bias_row, d_a=d_a, d_dsk=d_dsk, d_ssd_norm=d_ssd_norm,
                d_gate_bias=d_gate_bias, d_norm_post=d_norm_post)


def _to_aligned_cols(w):
    def cut(o):
        return w[..., o[0]:o[0] + o[1]]
    main = jnp.concatenate([cut(O_Z), cut(O_XBC), cut(O_ZA), cut(O_Q), cut(O_K), cut(O_V), cut(O_G)], axis=-1)
    pad = jnp.zeros(w.shape[:-1] + (N_SMALL - H_SSD - H_ATT,), w.dtype)
    small = jnp.concatenate([cut(O_DT), cut(O_F), pad], axis=-1)
    return main, small


def _from_aligned_cols(main, small):
    def cm(c0, n):
        return main[..., c0:c0 + n]
    return jnp.concatenate([cm(C_Z, 2048), cm(C_XBC, 3072), small[..., 0:H_SSD], cm(C_ZA, 1024),
                            cm(C_Q, 1024), cm(C_K, 1024), cm(C_V, 1024), small[..., H_SSD:H_SSD + H_ATT],
                            cm(C_G, 2048)], axis=-1)


_MESH = pl.DeviceIdType.MESH
_ANY = pl.BlockSpec(memory_space=pl.ANY)
_VM = pl.BlockSpec(memory_space=pltpu.VMEM)
_HALF = 512
N_DEV = 8


def _coords():
    return lax.axis_index("x"), lax.axis_index("y"), lax.axis_index("c")


def _other_chips(x, y):
    return [(1 - x, y), (x, 1 - y), (1 - x, 1 - y)]


def _half(cc):
    return pl.ds(cc * _HALF, _HALF)


def _gather_shards(shards):
    n = len(shards)

    def body(*refs):
        src, dst = refs[:n], refs[n:2 * n]
        send_sems, recv_sems, local_sems = refs[2 * n:]
        x, y, c = _coords()
        chip = 2 * x + y
        sibling = (x, y, 1 - c)
        chips = _other_chips(x, y)

        def remote(s, d, k, to):
            return pltpu.make_async_remote_copy(src_ref=s, dst_ref=d, send_sem=send_sems.at[k], recv_sem=recv_sems.at[k],
                                                device_id=to, device_id_type=_MESH)

        def over_ici(i, k, frm):
            rows = dst[i].at[frm, _half(c)]
            return remote(src[i].at[_half(c)], rows, 6 * i + k, (*chips[k], c))

        def to_sibling(i, k, cc):
            rows = dst[i].at[2 * chips[k][0] + chips[k][1], _half(cc)]
            return remote(rows, rows, 6 * i + 3 + k, sibling)

        local = [pltpu.make_async_copy(src[i], dst[i].at[chip], local_sems.at[i]) for i in range(n)]
        for cp in local:
            cp.start()
        first = [over_ici(i, k, chip) for k in range(3) for i in range(n)]
        for cp in first:
            cp.start()
        passed = []
        for k in range(3):
            for i in range(n):
                over_ici(i, k, 2 * chips[k][0] + chips[k][1]).wait_recv()
                passed.append(to_sibling(i, k, c))
                passed[-1].start()
        for k in range(3):
            for i in range(n):
                to_sibling(i, k, 1 - c).wait_recv()
        for cp in first + passed:
            cp.wait_send()
        for cp in local:
            cp.wait()

    return pl.pallas_call(
        body, name="gather_shards",
        out_shape=tuple(jax.ShapeDtypeStruct((4,) + s.shape, s.dtype) for s in shards),
        in_specs=[_ANY] * n, out_specs=tuple([_ANY] * n),
        scratch_shapes=[pltpu.SemaphoreType.DMA((6 * n,)), pltpu.SemaphoreType.DMA((6 * n,)),
                        pltpu.SemaphoreType.DMA((n,))],
    )(*shards)


def _allgather8(block, name):
    rows, width = block.shape

    def body(x_ref, out_ref, send_sems, recv_sems, local_sem):
        x, y, c = _coords()
        me, sibling = (x, y, c), (x, y, 1 - c)
        chips = _other_chips(x, y)

        def slot(px, py, pc):
            return out_ref.at[4 * px + 2 * py + pc]

        def copy(k, blk, to, src=None):
            return pltpu.make_async_remote_copy(src_ref=slot(*blk) if src is None else src, dst_ref=slot(*blk),
                                                send_sem=send_sems.at[k], recv_sem=recv_sems.at[k],
                                                device_id=to, device_id_type=_MESH)

        mine = pltpu.make_async_copy(x_ref, slot(*me), local_sem)
        mine.start()
        first = [copy(0, me, sibling, src=x_ref)]
        first += [copy(1 + j, me, (*chip, c), src=x_ref) for j, chip in enumerate(chips)]
        for cp in first:
            cp.start()
        passed = [copy(4 + j, (*chip, c), sibling) for j, chip in enumerate(chips)]
        for j, chip in enumerate(chips):
            copy(1 + j, (*chip, c), me).wait_recv()
            passed[j].start()
        copy(0, sibling, me).wait_recv()
        for j, chip in enumerate(chips):
            copy(4 + j, (*chip, 1 - c), me).wait_recv()
        for cp in first + passed:
            cp.wait_send()
        mine.wait()

    return pl.pallas_call(
        body, name=name,
        out_shape=jax.ShapeDtypeStruct((N_DEV, rows, width), block.dtype),
        in_specs=[_VM], out_specs=_VM,
        scratch_shapes=[pltpu.SemaphoreType.DMA((7,)), pltpu.SemaphoreType.DMA((7,)), pltpu.SemaphoreType.DMA],
    )(block)


def _pair_swap_halves(arrs):
    n = len(arrs)

    def body(*refs):
        src, dst = refs[:n], refs[n:2 * n]
        send_sems, recv_sems = refs[2 * n:]
        x, y, c = _coords()
        cps = [pltpu.make_async_remote_copy(src_ref=src[i].at[pl.ds(0, 4), _half(1 - c)], dst_ref=dst[i],
                                            send_sem=send_sems.at[i], recv_sem=recv_sems.at[i],
                                            device_id=(x, y, 1 - c), device_id_type=_MESH) for i in range(n)]
        for cp in cps:
            cp.start()
        for cp in cps:
            cp.wait()

    return pl.pallas_call(
        body, name="pair_swap_halves",
        out_shape=tuple(jax.ShapeDtypeStruct((4, _HALF, a.shape[2]), a.dtype) for a in arrs),
        in_specs=[_ANY] * n, out_specs=tuple([_ANY] * n),
        scratch_shapes=[pltpu.SemaphoreType.DMA((n,)), pltpu.SemaphoreType.DMA((n,))],
    )(*arrs)


def _chip_exchange(arrs):
    n = len(arrs)

    def body(*refs):
        src, dst = refs[:n], refs[n:2 * n]
        send_sems, recv_sems = refs[2 * n:]
        x, y, c = _coords()
        chips = _other_chips(x, y)
        cps = [pltpu.make_async_remote_copy(src_ref=src[i].at[2 * chips[k][0] + chips[k][1]], dst_ref=dst[i].at[k],
                                            send_sem=send_sems.at[3 * i + k], recv_sem=recv_sems.at[3 * i + k],
                                            device_id=(*chips[k], c), device_id_type=_MESH)
               for k in range(3) for i in range(n)]
        for cp in cps:
            cp.start()
        for cp in cps:
            cp.wait()

    return pl.pallas_call(
        body, name="chip_exchange",
        out_shape=tuple(jax.ShapeDtypeStruct((3, _HALF, a.shape[2]), a.dtype) for a in arrs),
        in_specs=[_ANY] * n, out_specs=tuple([_ANY] * n),
        scratch_shapes=[pltpu.SemaphoreType.DMA((3 * n,)), pltpu.SemaphoreType.DMA((3 * n,))],
    )(*arrs)


def _pair_join_halves(halves):
    n = len(halves)

    def body(*refs):
        src, dst = refs[:n], refs[n:2 * n]
        send_sems, recv_sems, local_sems = refs[2 * n:]
        x, y, c = _coords()
        local = [pltpu.make_async_copy(src[i], dst[i].at[_half(c)], local_sems.at[i]) for i in range(n)]
        for cp in local:
            cp.start()

        def remote(i, cc):
            return pltpu.make_async_remote_copy(src_ref=src[i], dst_ref=dst[i].at[_half(cc)],
                                                send_sem=send_sems.at[i], recv_sem=recv_sems.at[i],
                                                device_id=(x, y, 1 - c), device_id_type=_MESH)

        for i in range(n):
            remote(i, c).start()
        for i in range(n):
            remote(i, c).wait_send()
            remote(i, 1 - c).wait_recv()
        for cp in local:
            cp.wait()

    return pl.pallas_call(
        body, name="pair_join_halves",
        out_shape=tuple(jax.ShapeDtypeStruct((2 * _HALF, a.shape[1]), a.dtype) for a in halves),
        in_specs=[_ANY] * n, out_specs=tuple([_ANY] * n),
        scratch_shapes=[pltpu.SemaphoreType.DMA((n,)), pltpu.SemaphoreType.DMA((n,)), pltpu.SemaphoreType.DMA((n,))],
    )(*halves)


_RED_TR = 128


def _add_pair(ids, g32, recv_a):
    width = g32.shape[2]
    nt = _HALF // _RED_TR

    def body(ids_ref, g_ref, r_ref, o_ref):
        o_ref[...] = (g_ref[...] + r_ref[...]).astype(BF16)

    blk = pl.BlockSpec((1, _RED_TR, width), lambda j, i, ids: (j, i, 0))
    return pl.pallas_call(
        body, name="add_pair",
        out_shape=jax.ShapeDtypeStruct((4, _HALF, width), BF16),
        grid_spec=pltpu.PrefetchScalarGridSpec(
            num_scalar_prefetch=1, grid=(4, nt),
            in_specs=[pl.BlockSpec((1, _RED_TR, width), lambda j, i, ids: (j, ids[0] * nt + i, 0)), blk],
            out_specs=blk),
        compiler_params=_cp("parallel", "parallel"),
    )(ids, g32, recv_a)


def _add_chips(ids, g32, recv_a, recv_b):
    width = g32.shape[2]
    nt = _HALF // _RED_TR

    def body(ids_ref, g_ref, a_ref, b_ref, o_ref):
        acc = g_ref[0] + a_ref[0]
        for k in range(3):
            acc = acc + b_ref[k].astype(F32)
        o_ref[...] = acc

    return pl.pallas_call(
        body, name="add_chips",
        out_shape=jax.ShapeDtypeStruct((_HALF, width), F32),
        grid_spec=pltpu.PrefetchScalarGridSpec(
            num_scalar_prefetch=1, grid=(nt,),
            in_specs=[pl.BlockSpec((1, _RED_TR, width), lambda i, ids: (ids[1], ids[0] * nt + i, 0)),
                      pl.BlockSpec((1, _RED_TR, width), lambda i, ids: (ids[1], i, 0)),
                      pl.BlockSpec((3, _RED_TR, width), lambda i, ids: (0, i, 0))],
            out_specs=pl.BlockSpec((_RED_TR, width), lambda i, ids: (i, 0))),
        compiler_params=_cp("parallel"),
    )(ids, g32, recv_a, recv_b)


def _sum8(gathered):
    _, rows, width = gathered.shape

    def body(g_ref, o_ref):
        acc = g_ref[0]
        for d in range(1, N_DEV):
            acc = acc + g_ref[d]
        o_ref[...] = acc

    return pl.pallas_call(
        body, name="sum8",
        out_shape=jax.ShapeDtypeStruct((rows, width), F32),
        in_specs=[_VM], out_specs=_VM,
    )(gathered)


def _adamw(w, g, m, v, name):
    rows, cols = w.shape
    tr = rows
    for cand in (rows, 512, 256, 128, 64, 32, 16, 8):
        if rows % cand == 0 and cand * cols * 4 <= (3 << 20) // 2:
            tr = cand
            break
    c1 = 1.0 - ADAM_B1 ** ADAM_STEP
    c2 = 1.0 - ADAM_B2 ** ADAM_STEP

    def body(w_ref, g_ref, m_ref, v_ref, d_ref, mo_ref, vo_ref):
        gg = g_ref[...]
        mn = ADAM_B1 * m_ref[...] + (1.0 - ADAM_B1) * gg
        vn = ADAM_B2 * v_ref[...] + (1.0 - ADAM_B2) * (gg * gg)
        mo_ref[...] = mn
        vo_ref[...] = vn
        d_ref[...] = -ADAM_LR * ((mn / c1) / (jnp.sqrt(vn / c2) + ADAM_EPS) + ADAM_WD * w_ref[...])

    blk = pl.BlockSpec((tr, cols), lambda i: (i, 0))
    shp = jax.ShapeDtypeStruct((rows, cols), F32)
    return pl.pallas_call(
        body, name=name, out_shape=(shp, shp, shp), grid=(rows // tr,),
        in_specs=[blk] * 4, out_specs=(blk, blk, blk),
        compiler_params=_cp("parallel"),
    )(w, g, m, v)


def _rows128(a):
    return a.reshape(-1, 128)


def _pack_small(norm_pre, conv_b, ssd_norm, gate_bias, norm_post, dt_bias, a_log, d_skip, fgate_bias):
    tiny = jnp.concatenate([dt_bias.reshape(-1), a_log.reshape(-1), d_skip.reshape(-1), fgate_bias.reshape(-1),
                            jnp.zeros((16,), F32)])
    return jnp.concatenate([_rows128(norm_pre), _rows128(conv_b), _rows128(ssd_norm), _rows128(gate_bias),
                            _rows128(norm_post), tiny.reshape(1, 128)], axis=0)


_SMALL_ROWS = 73
_SMALL_PAD = 80


def _unpack_small(p):
    tiny = p[72]
    return dict(norm_pre=p[0:8].reshape(1, 1024), conv_b=p[8:32].reshape(1, 3072), ssd_norm=p[32:48].reshape(1, 2048),
                gate_bias=p[48:64].reshape(1, 2048), norm_post=p[64:72].reshape(1, 1024),
                dt_bias=tiny[0:32].reshape(1, 32), a_log=tiny[32:64].reshape(1, 32),
                d_skip=tiny[64:96].reshape(1, 32), fgate_bias=tiny[96:112].reshape(1, 16))


def _pad_rows(a, rows):
    return jnp.concatenate([a, jnp.zeros((rows - a.shape[0], a.shape[1]), a.dtype)], axis=0)


def kernel(x, meta_tokens, norm_pre, w_in, conv_w, conv_b, dt_bias, a_log, d_skip, ssd_norm, fgate_bias, gate_bias, w_proj_ssd, w_proj_att, w_out, norm_post, loss_target, m_meta_tokens, m_norm_pre, m_w_in, m_conv_w, m_conv_b, m_dt_bias, m_a_log, m_d_skip, m_ssd_norm, m_fgate_bias, m_gate_bias, m_w_proj_ssd, m_w_proj_att, m_w_out, m_norm_post, v_meta_tokens, v_norm_pre, v_w_in, v_conv_w, v_conv_b, v_dt_bias, v_a_log, v_d_skip, v_ssd_norm, v_fgate_bias, v_gate_bias, v_w_proj_ssd, v_w_proj_att, v_w_out, v_norm_post):
    cx, cy, cc = _coords()
    chip = 2 * cx + cy
    ids = jnp.stack([cc, chip]).astype(jnp.int32)
    seq = x.shape[1]

    w_in_sh = w_in[0].astype(BF16)
    w_pr_sh = jnp.concatenate([w_proj_ssd[0], w_proj_att[0], w_out[0]], axis=0).astype(BF16)
    g_in, g_pr = _gather_shards([w_in_sh, w_pr_sh])
    w_full = jnp.transpose(g_in, (1, 0, 2)).reshape(D_MODEL, N_COLS)
    w_main, w_small = _to_aligned_cols(w_full)
    wps = g_pr[:, 0:512].reshape(D_SSD, D_MODEL)
    wpa = g_pr[:, 512:768].reshape(D_ATT, D_MODEL)
    wout = g_pr[:, 768:1024].reshape(D_MODEL, D_MODEL)
    sm_sh = jnp.concatenate([_rows128(meta_tokens), _rows128(conv_w[0])], axis=0)
    sm_all = _allgather8(sm_sh, "gather_small_weights")[0::2]
    meta_full = jnp.transpose(sm_all[:, 0:32].reshape(4, N_META, 256), (1, 0, 2)).reshape(N_META, D_MODEL)
    conv_w_full = jnp.transpose(sm_all[:, 32:56].reshape(4, CONV_K, 768), (1, 0, 2)).reshape(CONV_K, CONV_DIM)

    h = jnp.concatenate([jnp.zeros((PADF, D_MODEL), F32), meta_full, x[0]], axis=0)
    bias_row = jnp.concatenate([dt_bias[0], fgate_bias[0], jnp.zeros((N_SMALL - H_SSD - H_ATT,), F32)]).reshape(1, N_SMALL)
    a_neg = -jnp.exp(a_log[0])
    a_row = jnp.concatenate([a_neg, jnp.zeros((N_SMALL - H_SSD,), F32)]).reshape(1, N_SMALL)
    dsk_row = jnp.repeat(d_skip[0], 64).reshape(1, D_SSD)
    r = _local_step(h, loss_target[0], w_main, w_small, wps, wpa, wout, norm_pre, conv_w_full, conv_b, bias_row,
                    a_row, dsk_row, ssd_norm, gate_bias, norm_post)
    dh = r["dh"]
    grad_x = dh[PADF + N_META:].reshape(1, seq, D_MODEL)

    tiny = r["d_bias_row"][0]
    part_small = _pack_small(r["d_norm_pre"], r["d_conv_b"], r["d_ssd_norm"], r["d_gate_bias"], r["d_norm_post"],
                             tiny[0:H_SSD], r["d_a"][0, 0:H_SSD] * a_neg, r["d_dsk"].reshape(H_SSD, 64).sum(axis=1),
                             tiny[H_SSD:H_SSD + H_ATT])
    part = jnp.concatenate([_pad_rows(part_small, _SMALL_PAD), _rows128(r["d_conv_w"]),
                            _rows128(dh[PADF:PADF + N_META]), r["loss_blk"]], axis=0)
    tot = _sum8(_allgather8(part, "gather_small_grads"))
    loss = tot[_SMALL_PAD + 96 + 128, 0]
    g_small = tot[0:_SMALL_PAD]
    g_conv_w = lax.dynamic_slice_in_dim(tot[_SMALL_PAD:_SMALL_PAD + 96].reshape(CONV_K, CONV_DIM), chip * 768, 768, axis=1)
    g_meta = lax.dynamic_slice_in_dim(tot[_SMALL_PAD + 96:_SMALL_PAD + 224].reshape(N_META, D_MODEL), chip * 256, 256, axis=1)

    d_w_in = _from_aligned_cols(r["d_w_main"], r["d_w_small"])
    g32_in = jnp.transpose(d_w_in.reshape(D_MODEL, 4, N_COLS // 4), (1, 0, 2))
    g32_pr = jnp.concatenate([r["d_wps"].reshape(4, 512, D_MODEL), r["d_wpa"].reshape(4, 256, D_MODEL),
                              r["d_wout"].reshape(4, 256, D_MODEL)], axis=1)
    ra_in, ra_pr = _pair_swap_halves([g32_in, g32_pr])
    pb_in = _add_pair(ids, g32_in, ra_in)
    pb_pr = _add_pair(ids, g32_pr, ra_pr)
    rb_in, rb_pr = _chip_exchange([pb_in, pb_pr])
    half_in = _add_chips(ids, g32_in, ra_in, rb_in)
    half_pr = _add_chips(ids, g32_pr, ra_pr, rb_pr)
    gw_in, gw_pr = _pair_join_halves([half_in, half_pr])

    upd = {}
    upd["w_in"] = (gw_in,) + _adamw(w_in[0], gw_in, m_w_in[0], v_w_in[0], "adamw_w_in")
    w_pr32 = jnp.concatenate([w_proj_ssd[0], w_proj_att[0], w_out[0]], axis=0)
    m_pr = jnp.concatenate([m_w_proj_ssd[0], m_w_proj_att[0], m_w_out[0]], axis=0)
    v_pr = jnp.concatenate([v_w_proj_ssd[0], v_w_proj_att[0], v_w_out[0]], axis=0)
    pr = (gw_pr,) + _adamw(w_pr32, gw_pr, m_pr, v_pr, "adamw_w_proj")
    upd["w_proj_ssd"] = tuple(a[0:512] for a in pr)
    upd["w_proj_att"] = tuple(a[512:768] for a in pr)
    upd["w_out"] = tuple(a[768:1024] for a in pr)
    upd["conv_w"] = (g_conv_w,) + _adamw(conv_w[0], g_conv_w, m_conv_w[0], v_conv_w[0], "adamw_conv_w")
    upd["meta_tokens"] = (g_meta,) + _adamw(meta_tokens, g_meta, m_meta_tokens, v_meta_tokens, "adamw_meta")
    pk = lambda np_, cb, sn, gb, npo, dtb, al, ds, fg: _pad_rows(_pack_small(np_, cb, sn, gb, npo, dtb, al, ds, fg), _SMALL_PAD)
    w_sm = pk(norm_pre, conv_b, ssd_norm, gate_bias, norm_post, dt_bias, a_log, d_skip, fgate_bias)
    m_sm = pk(m_norm_pre, m_conv_b, m_ssd_norm, m_gate_bias, m_norm_post, m_dt_bias, m_a_log, m_d_skip, m_fgate_bias)
    v_sm = pk(v_norm_pre, v_conv_b, v_ssd_norm, v_gate_bias, v_norm_post, v_dt_bias, v_a_log, v_d_skip, v_fgate_bias)
    sm = [_unpack_small(a) for a in (g_small,) + _adamw(w_sm, g_small, m_sm, v_sm, "adamw_small")]
    for name in ("norm_pre", "conv_b", "dt_bias", "a_log", "d_skip", "ssd_norm", "fgate_bias", "gate_bias", "norm_post"):
        upd[name] = tuple(s[name] for s in sm)
    lead = ("w_in", "conv_w", "w_proj_ssd", "w_proj_att", "w_out")
    order = ("meta_tokens", "norm_pre", "w_in", "conv_w", "conv_b", "dt_bias", "a_log", "d_skip", "ssd_norm",
             "fgate_bias", "gate_bias", "w_proj_ssd", "w_proj_att", "w_out", "norm_post")
    outs = [loss, grad_x]
    for part_i in range(4):
        for name in order:
            a = upd[name][part_i]
            outs.append(a[None] if name in lead else a)
    return tuple(outs)
```

```python
import functools
import math

import jax
import jax.numpy as jnp
from jax import lax
from jax.experimental import pallas as pl
from jax.experimental.pallas import tpu as pltpu

F32 = jnp.float32
BF16 = jnp.bfloat16
HIGHEST = lax.Precision.HIGHEST

D_MODEL = 1024
N_META = 16
CHUNK = 128
PADF = CHUNK - N_META
D_SSD = 2048
H_SSD = 32
G_SSD = 4
N_STATE = 128
CONV_K = 4
CONV_DIM = D_SSD + 2 * G_SSD * N_STATE
H_ATT = 16
D_ATT = 1024
EPS = 1e-6
N_COLS = 11312

C_Z, C_XBC, C_ZA, C_Q, C_K, C_V, C_G = 0, 2048, 5120, 6144, 7168, 8192, 9216
N_MAIN = 11264
N_SMALL = 128
O_Z, O_XBC, O_DT, O_ZA, O_Q, O_K, O_V, O_F, O_G = (
    (0, 2048), (2048, 3072), (5120, 32), (5152, 1024), (6176, 1024), (7200, 1024),
    (8224, 1024), (9248, 16), (9264, 2048))

ADAM_LR, ADAM_B1, ADAM_B2, ADAM_EPS, ADAM_WD, ADAM_STEP = 0.001, 0.9, 0.999, 1e-08, 0.01, 10

VMEM_LIMIT = 56 * 1024 * 1024


def _cp(*sem):
    return pltpu.CompilerParams(dimension_semantics=sem, vmem_limit_bytes=VMEM_LIMIT)


def _tile(n, prefs):
    for p in prefs:
        if n % p == 0:
            return p
    raise ValueError(f"no tile for {n} in {prefs}")


def _iota(shape, dim):
    return lax.broadcasted_iota(jnp.int32, shape, dim)


def _sigmoid(x):
    return 1.0 / (1.0 + jnp.exp(-x))


def _softplus_tail(x):
    return jnp.log(1.0 + jnp.exp(-jnp.abs(x)))


_NN = (((1,), (0,)), ((), ()))
_NT = (((1,), (1,)), ((), ()))
_TN = (((0,), (0,)), ((), ()))


def _dot(a, b, dims=_NN):
    return lax.dot_general(a, b, dims, preferred_element_type=F32)


def _dot_exact(a, b, dims=_NN):
    return lax.dot_general(a, b, dims, precision=HIGHEST, preferred_element_type=F32)


def _matmul(a, b, mode, out_dtype, name, tm, tn, tk):
    if mode == "tn":
        kdim, m = a.shape
    else:
        m, kdim = a.shape
    n = b.shape[0] if mode == "nt" else b.shape[1]
    nk = kdim // tk
    dims = {"nn": _NN, "nt": _NT, "tn": _TN}[mode]
    a_spec = (pl.BlockSpec((tk, tm), lambda i, j, k: (k, i)) if mode == "tn"
              else pl.BlockSpec((tm, tk), lambda i, j, k: (i, k)))
    b_spec = (pl.BlockSpec((tn, tk), lambda i, j, k: (j, k)) if mode == "nt"
              else pl.BlockSpec((tk, tn), lambda i, j, k: (k, j)))

    def body(a_ref, b_ref, o_ref, acc_ref):
        k = pl.program_id(2)
        p = _dot(a_ref[...].astype(BF16), b_ref[...].astype(BF16), dims)
        if nk == 1:
            o_ref[...] = p.astype(out_dtype)
        else:
            @pl.when(k == 0)
            def _():
                acc_ref[...] = p

            @pl.when(k > 0)
            def _():
                acc_ref[...] += p

            @pl.when(k == nk - 1)
            def _():
                o_ref[...] = acc_ref[...].astype(out_dtype)

    return pl.pallas_call(
        body, name=name,
        out_shape=jax.ShapeDtypeStruct((m, n), out_dtype),
        grid=(m // tm, n // tn, nk),
        in_specs=[a_spec, b_spec],
        out_specs=pl.BlockSpec((tm, tn), lambda i, j, k: (i, j)),
        scratch_shapes=[pltpu.VMEM((tm, tn), F32)],
        compiler_params=_cp("parallel", "parallel", "arbitrary"),
    )(a, b)


def _row_tile(t):
    return _tile(t, (352, 128))


def _row_tile_wide(t):
    return _tile(t, (176, 128))


def _norm1_fwd(h, g):
    t = h.shape[0]
    tm = _row_tile(t)

    def body(h_ref, g_ref, u_ref):
        x = h_ref[...]
        r = lax.rsqrt(jnp.mean(x * x, axis=-1, keepdims=True) + EPS)
        u_ref[...] = (x * r * g_ref[...]).astype(BF16)

    return pl.pallas_call(
        body, name="norm1_fwd",
        out_shape=jax.ShapeDtypeStruct((t, D_MODEL), BF16),
        grid=(t // tm,),
        in_specs=[pl.BlockSpec((tm, D_MODEL), lambda i: (i, 0)),
                  pl.BlockSpec((1, D_MODEL), lambda i: (0, 0))],
        out_specs=pl.BlockSpec((tm, D_MODEL), lambda i: (i, 0)),
        compiler_params=_cp("parallel"),
    )(h, g)


def _norm1_bwd(du_a, du_b, h, g, dy):
    t = h.shape[0]
    tm = _row_tile(t)

    def body(a_ref, b_ref, h_ref, g_ref, dy_ref, dh_ref, dg_ref):
        i = pl.program_id(0)
        x = h_ref[...]
        du = a_ref[...] + b_ref[...]
        r = lax.rsqrt(jnp.mean(x * x, axis=-1, keepdims=True) + EPS)
        gdu = du * g_ref[...]
        dh_ref[...] = dy_ref[...] + r * (gdu - x * (r * r) * jnp.mean(gdu * x, axis=-1, keepdims=True))
        part = jnp.sum(du * x * r, axis=0, keepdims=True)

        @pl.when(i == 0)
        def _():
            dg_ref[...] = part

        @pl.when(i > 0)
        def _():
            dg_ref[...] += part

    row = pl.BlockSpec((tm, D_MODEL), lambda i: (i, 0))
    vec = pl.BlockSpec((1, D_MODEL), lambda i: (0, 0))
    return pl.pallas_call(
        body, name="norm1_bwd",
        out_shape=(jax.ShapeDtypeStruct((t, D_MODEL), F32), jax.ShapeDtypeStruct((1, D_MODEL), F32)),
        grid=(t // tm,),
        in_specs=[row, row, row, vec, row],
        out_specs=(row, vec),
        compiler_params=_cp("arbitrary"),
    )(du_a, du_b, h, g, dy)


def _small_fwd(small, bias_row):
    t = small.shape[0]

    def body(s_ref, b_ref, o_ref, carry_ref):
        c = pl.program_id(0)

        @pl.when(c == 0)
        def _():
            carry_ref[...] = jnp.zeros_like(carry_ref)

        x = s_ref[...] + b_ref[...]
        r0 = _iota((CHUNK, CHUNK), 0)
        r1 = _iota((CHUNK, CHUNK), 1)
        valid = (c * CHUNK + r0) >= PADF
        tail = _softplus_tail(x)
        dt = jnp.where(valid & (r1 < H_SSD), jnp.maximum(x, 0.0) + tail, 0.0)
        lf = jnp.where(valid & (r1 >= H_SSD) & (r1 < H_SSD + H_ATT), jnp.minimum(x, 0.0) - tail, 0.0)
        tri = (r0 >= r1).astype(F32)
        cs = _dot_exact(tri, lf) + carry_ref[...]
        carry_ref[...] = cs[CHUNK - 1:CHUNK, :]
        o_ref[...] = dt + cs

    return pl.pallas_call(
        body, name="small_fwd",
        out_shape=jax.ShapeDtypeStruct((t, N_SMALL), F32),
        grid=(t // CHUNK,),
        in_specs=[pl.BlockSpec((CHUNK, N_SMALL), lambda c: (c, 0)),
                  pl.BlockSpec((1, N_SMALL), lambda c: (0, 0))],
        out_specs=pl.BlockSpec((CHUNK, N_SMALL), lambda c: (c, 0)),
        scratch_shapes=[pltpu.VMEM((1, N_SMALL), F32)],
        compiler_params=_cp("arbitrary"),
    )(small, bias_row)


def _small_bwd(dsm, small, bias_row):
    t = small.shape[0]
    nc = t // CHUNK

    def body(d_ref, s_ref, b_ref, o_ref, db_ref, carry_ref):
        step = pl.program_id(0)
        c = nc - 1 - step

        @pl.when(step == 0)
        def _():
            carry_ref[...] = jnp.zeros_like(carry_ref)
            db_ref[...] = jnp.zeros_like(db_ref)

        x = s_ref[...] + b_ref[...]
        d = d_ref[...]
        r0 = _iota((CHUNK, CHUNK), 0)
        r1 = _iota((CHUNK, CHUNK), 1)
        valid = (c * CHUNK + r0) >= PADF
        is_dt = r1 < H_SSD
        is_f = (r1 >= H_SSD) & (r1 < H_SSD + H_ATT)
        triu = (r1 >= r0).astype(F32)
        dc = jnp.where(is_f, d, 0.0)
        dlf = _dot_exact(triu, dc) + carry_ref[...]
        carry_ref[...] = dlf[0:1, :]
        sg = _sigmoid(x)
        out = jnp.where(valid & is_dt, d * sg, 0.0) + jnp.where(valid & is_f, dlf * (1.0 - sg), 0.0)
        o_ref[...] = out.astype(BF16)
        db_ref[...] += jnp.sum(out, axis=0, keepdims=True)

    blk = pl.BlockSpec((CHUNK, N_SMALL), lambda s: (nc - 1 - s, 0))
    vec = pl.BlockSpec((1, N_SMALL), lambda s: (0, 0))
    return pl.pallas_call(
        body, name="small_bwd",
        out_shape=(jax.ShapeDtypeStruct((t, N_SMALL), BF16), jax.ShapeDtypeStruct((1, N_SMALL), F32)),
        grid=(nc,),
        in_specs=[blk, blk, vec],
        out_specs=(blk, vec),
        scratch_shapes=[pltpu.VMEM((1, N_SMALL), F32)],
        compiler_params=_cp("arbitrary"),
    )(dsm, small, bias_row)


_CONV_TC = 1024
_XBC_BLK = C_XBC // _CONV_TC


def _shift_down(cur, prev8, j):
    rc = pltpu.roll(cur, j, 0)
    rid = _iota(prev8.shape, 0)
    top = jnp.where(rid < j, pltpu.roll(prev8, j, 0), rc[0:8, :])
    return jnp.concatenate([top, rc[8:, :]], axis=0)


def _shift_up(cur, next8, j):
    n = cur.shape[0]
    ru = pltpu.roll(cur, n - j, 0)
    rid = _iota(next8.shape, 0)
    bot = jnp.where(rid >= 8 - j, pltpu.roll(next8, 8 - j, 0), ru[n - 8:, :])
    return jnp.concatenate([ru[:n - 8, :], bot], axis=0)


def _conv_pre(x_ref, p_ref, w_ref, b_ref, i):
    cur = x_ref[...]
    prev = jnp.where(i > 0, p_ref[...], 0.0)
    w = w_ref[...]
    taps = [cur] + [_shift_down(cur, prev, j) for j in (1, 2, 3)]
    acc = b_ref[...] + taps[0] * w[3:4, :]
    for j in (1, 2, 3):
        acc = acc + taps[j] * w[3 - j:4 - j, :]
    return acc, taps


def _conv_fwd(proj, conv_w, conv_b):
    t = proj.shape[0]
    tr = _row_tile(t)

    def body(x_ref, p_ref, w_ref, b_ref, o_ref):
        i = pl.program_id(0)
        acc, _ = _conv_pre(x_ref, p_ref, w_ref, b_ref, i)
        valid = (i * tr + _iota(acc.shape, 0)) >= PADF
        o_ref[...] = jnp.where(valid, acc * _sigmoid(acc), 0.0)

    return pl.pallas_call(
        body, name="conv_fwd",
        out_shape=jax.ShapeDtypeStruct((t, CONV_DIM), F32),
        grid=(t // tr, CONV_DIM // _CONV_TC),
        in_specs=[pl.BlockSpec((tr, _CONV_TC), lambda i, j: (i, _XBC_BLK + j)),
                  pl.BlockSpec((8, _CONV_TC), lambda i, j: (jnp.maximum(i * (tr // 8) - 1, 0), _XBC_BLK + j)),
                  pl.BlockSpec((CONV_K, _CONV_TC), lambda i, j: (0, j)),
                  pl.BlockSpec((1, _CONV_TC), lambda i, j: (0, j))],
        out_specs=pl.BlockSpec((tr, _CONV_TC), lambda i, j: (i, j)),
        compiler_params=_cp("parallel", "parallel"),
    )(proj, proj, conv_w, conv_b)


def _conv_bwd_act(dxbc, proj, conv_w, conv_b):
    t = proj.shape[0]
    tr = _row_tile(t)

    def body(d_ref, x_ref, p_ref, w_ref, b_ref, da_ref, dw_ref, db_ref):
        i = pl.program_id(1)
        acc, taps = _conv_pre(x_ref, p_ref, w_ref, b_ref, i)
        valid = (i * tr + _iota(acc.shape, 0)) >= PADF
        sg = _sigmoid(acc)
        da = jnp.where(valid, d_ref[...] * sg * (1.0 + acc * (1.0 - sg)), 0.0)
        da_ref[...] = da
        dw = jnp.concatenate([jnp.sum(da * taps[3 - k], axis=0, keepdims=True) for k in range(CONV_K)], axis=0)
        db = jnp.sum(da, axis=0, keepdims=True)

        @pl.when(i == 0)
        def _():
            dw_ref[...] = dw
            db_ref[...] = db

        @pl.when(i > 0)
        def _():
            dw_ref[...] += dw
            db_ref[...] += db

    return pl.pallas_call(
        body, name="conv_bwd_act",
        out_shape=(jax.ShapeDtypeStruct((t, CONV_DIM), F32),
                   jax.ShapeDtypeStruct((CONV_K, CONV_DIM), F32),
                   jax.ShapeDtypeStruct((1, CONV_DIM), F32)),
        grid=(CONV_DIM // _CONV_TC, t // tr),
        in_specs=[pl.BlockSpec((tr, _CONV_TC), lambda j, i: (i, j)),
                  pl.BlockSpec((tr, _CONV_TC), lambda j, i: (i, _XBC_BLK + j)),
                  pl.BlockSpec((8, _CONV_TC), lambda j, i: (jnp.maximum(i * (tr // 8) - 1, 0), _XBC_BLK + j)),
                  pl.BlockSpec((CONV_K, _CONV_TC), lambda j, i: (0, j)),
                  pl.BlockSpec((1, _CONV_TC), lambda j, i: (0, j))],
        out_specs=(pl.BlockSpec((tr, _CONV_TC), lambda j, i: (i, j)),
                   pl.BlockSpec((CONV_K, _CONV_TC), lambda j, i: (0, j)),
                   pl.BlockSpec((1, _CONV_TC), lambda j, i: (0, j))),
        compiler_params=_cp("parallel", "arbitrary"),
    )(dxbc, proj, proj, conv_w, conv_b)


def _conv_bwd_in(da, conv_w):
    t = da.shape[0]
    tr = _row_tile(t)
    last8 = t // 8 - 1

    def body(d_ref, n_ref, w_ref, o_ref):
        i = pl.program_id(0)
        cur = d_ref[...]
        nxt = jnp.where(i < pl.num_programs(0) - 1, n_ref[...], 0.0)
        w = w_ref[...]
        acc = cur * w[3:4, :]
        for j in (1, 2, 3):
            acc = acc + _shift_up(cur, nxt, j) * w[3 - j:4 - j, :]
        o_ref[...] = acc.astype(BF16)

    return pl.pallas_call(
        body, name="conv_bwd_in",
        out_shape=jax.ShapeDtypeStruct((t, CONV_DIM), BF16),
        grid=(t // tr, CONV_DIM // _CONV_TC),
        in_specs=[pl.BlockSpec((tr, _CONV_TC), lambda i, j: (i, j)),
                  pl.BlockSpec((8, _CONV_TC), lambda i, j: (jnp.minimum((i + 1) * (tr // 8), last8), j)),
                  pl.BlockSpec((CONV_K, _CONV_TC), lambda i, j: (0, j))],
        out_specs=pl.BlockSpec((tr, _CONV_TC), lambda i, j: (i, j)),
        compiler_params=_cp("parallel", "parallel"),
    )(da, da, conv_w)


_GW = D_SSD // G_SSD


def _ssd_prelude(dt_ref, a_ref, e_scr, es_scr, dte_scr):
    r0 = _iota((CHUNK, CHUNK), 0)
    r1 = _iota((CHUNK, CHUNK), 1)
    dt = jnp.where(r1 < H_SSD, dt_ref[...], 0.0)
    adt = dt * a_ref[...]
    acs = _dot_exact((r0 >= r1).astype(F32), adt)
    acs_t = acs.T
    alast = acs[CHUNK - 1:CHUNK, :]
    exp_a = jnp.exp(acs)
    dec_s = jnp.exp(alast - acs)
    lo = r1 < 64
    for j in range(H_SSD // 2):
        sl = slice(CHUNK * j, CHUNK * (j + 1))
        e_scr[:, sl] = jnp.where(lo, exp_a[:, 2 * j:2 * j + 1], exp_a[:, 2 * j + 1:2 * j + 2])
        es_scr[:, sl] = jnp.where(lo, dec_s[:, 2 * j:2 * j + 1], dec_s[:, 2 * j + 1:2 * j + 2])
        dte_scr[:, sl] = jnp.where(lo, dt[:, 2 * j:2 * j + 1], dt[:, 2 * j + 1:2 * j + 2])
    return dt, acs, acs_t, r0, r1, lo


def _chunk_decay_rows(acs_t, g):
    cd_t = jnp.exp(acs_t[:, CHUNK - 1:CHUNK])
    return jnp.concatenate(
        [jnp.broadcast_to(cd_t[8 * g + hh:8 * g + hh + 1, :], (64, N_STATE)) for hh in range(8)], axis=0)


def _ssd_fwd(xbc, dtlf, a_row, dsk_row):
    t = xbc.shape[0]
    nc = t // CHUNK

    def body(xs_ref, b_ref, c_ref, dt_ref, a_ref, dsk_ref, y_ref, hin_ref, h_scr, e_scr, es_scr, dte_scr):
        c = pl.program_id(0)

        @pl.when(c == 0)
        def _():
            h_scr[...] = jnp.zeros_like(h_scr)

        dt, acs, acs_t, r0, r1, lo = _ssd_prelude(dt_ref, a_ref, e_scr, es_scr, dte_scr)
        causal = r0 >= r1
        for g in range(G_SSD):
            gs = slice(_GW * g, _GW * (g + 1))
            bg = b_ref[:, N_STATE * g:N_STATE * (g + 1)].astype(BF16)
            cg = c_ref[:, N_STATE * g:N_STATE * (g + 1)].astype(BF16)
            cb = _dot(cg, bg, _NT)
            hg = h_scr[gs, :]
            hin_ref[0, gs, :] = hg
            xg = xs_ref[:, gs] * dte_scr[:, gs]
            yoff = _dot(cg, hg.astype(BF16), _NT) * e_scr[:, gs]
            st = _dot((xg * es_scr[:, gs]).astype(BF16), bg, _TN)
            h_scr[gs, :] = hg * _chunk_decay_rows(acs_t, g) + st
            for jj in range(4):
                j = 4 * g + jj
                sl = slice(CHUNK * j, CHUNK * (j + 1))
                xp = xg[:, CHUNK * jj:CHUNK * (jj + 1)]
                acc = yoff[:, CHUNK * jj:CHUNK * (jj + 1)] + dsk_ref[:, sl] * xs_ref[:, sl]
                for hh in range(2):
                    h = 2 * j + hh
                    seg = acs[:, h:h + 1] - acs_t[h:h + 1, :]
                    lm = jnp.exp(jnp.where(causal, seg, -1e30))
                    m = (cb * lm).astype(BF16)
                    xh = jnp.where(lo if hh == 0 else ~lo, xp, 0.0).astype(BF16)
                    acc = acc + _dot(m, xh)
                y_ref[:, sl] = acc

    return pl.pallas_call(
        body, name="ssd_fwd",
        out_shape=(jax.ShapeDtypeStruct((t, D_SSD), F32), jax.ShapeDtypeStruct((nc, D_SSD, N_STATE), F32)),
        grid=(nc,),
        in_specs=[pl.BlockSpec((CHUNK, D_SSD), lambda c: (c, 0)),
                  pl.BlockSpec((CHUNK, _GW), lambda c: (c, 4)),
                  pl.BlockSpec((CHUNK, _GW), lambda c: (c, 5)),
                  pl.BlockSpec((CHUNK, N_SMALL), lambda c: (c, 0)),
                  pl.BlockSpec((1, N_SMALL), lambda c: (0, 0)),
                  pl.BlockSpec((1, D_SSD), lambda c: (0, 0))],
        out_specs=(pl.BlockSpec((CHUNK, D_SSD), lambda c: (c, 0)),
                   pl.BlockSpec((1, D_SSD, N_STATE), lambda c: (c, 0, 0))),
        scratch_shapes=[pltpu.VMEM((D_SSD, N_STATE), F32)] + [pltpu.VMEM((CHUNK, D_SSD), F32)] * 3,
        compiler_params=_cp("arbitrary"),
    )(xbc, xbc, xbc, dtlf, a_row, dsk_row)


def _ssd_bwd(xbc, dtlf, a_row, dsk_row, hin, dy):
    t = xbc.shape[0]
    nc = t // CHUNK

    def body(xs_ref, b_ref, c_ref, dt_ref, a_ref, dsk_ref, hin_ref, dy_ref,
             dxbc_ref, ddt_ref, da_ref, ddsk_ref, dh_scr, e_scr, es_scr, dte_scr, dx_scr):
        step = pl.program_id(0)

        @pl.when(step == 0)
        def _():
            dh_scr[...] = jnp.zeros_like(dh_scr)
            da_ref[...] = jnp.zeros_like(da_ref)
            ddsk_ref[...] = jnp.zeros_like(ddsk_ref)

        dt, acs, acs_t, r0, r1, lo = _ssd_prelude(dt_ref, a_ref, e_scr, es_scr, dte_scr)
        causal = r0 >= r1
        lane_row = _iota((1, CHUNK), 1)
        dacs = jnp.zeros((CHUNK, CHUNK), F32)
        dacs_t = jnp.zeros((CHUNK, CHUNK), F32)
        dalast = jnp.zeros((1, CHUNK), F32)
        ddt_dir = jnp.zeros((CHUNK, CHUNK), F32)
        ddsk_ref[...] += jnp.sum(dy_ref[...] * xs_ref[...], axis=0, keepdims=True)

        def head_rowsum(x, hh):
            return jnp.sum(jnp.where(lo if hh == 0 else ~lo, x, 0.0), axis=1, keepdims=True)

        for g in range(G_SSD):
            gs = slice(_GW * g, _GW * (g + 1))
            bg = b_ref[:, N_STATE * g:N_STATE * (g + 1)].astype(BF16)
            cg = c_ref[:, N_STATE * g:N_STATE * (g + 1)].astype(BF16)
            cb = _dot(cg, bg, _NT)
            hg = hin_ref[0, gs, :]
            hgb = hg.astype(BF16)
            dhn = dh_scr[gs, :]
            dhnb = dhn.astype(BF16)
            eg = e_scr[:, gs]
            esg = es_scr[:, gs]
            dyg = dy_ref[:, gs]
            xg = xs_ref[:, gs] * dte_scr[:, gs]
            ch = _dot(cg, hgb, _NT)
            dye = dyg * eg
            dyeb = dye.astype(BF16)
            dc = _dot(dyeb, hgb)
            dh_y = _dot(dyeb, cg, _TN)
            dxs = _dot(bg, dhnb, _NT)
            db = _dot((xg * esg).astype(BF16), dhnb)
            t_acs = dye * ch - dxs * xg * esg
            t_last = dxs * xg * esg
            cd = _chunk_decay_rows(acs_t, g)
            dh_scr[gs, :] = dhn * cd + dh_y
            t_cd = jnp.sum(dhn * hg * cd, axis=1, keepdims=True)
            dcb = jnp.zeros((CHUNK, CHUNK), F32)
            for jj in range(4):
                j = 4 * g + jj
                sl = slice(CHUNK * j, CHUNK * (j + 1))
                ps = slice(CHUNK * jj, CHUNK * (jj + 1))
                xp = xg[:, ps]
                xpb = xp.astype(BF16)
                dyp = dyg[:, ps]
                dxp = dxs[:, ps] * esg[:, ps]
                for hh in range(2):
                    h = 2 * j + hh
                    sel = lo if hh == 0 else ~lo
                    seg = acs[:, h:h + 1] - acs_t[h:h + 1, :]
                    lm = jnp.exp(jnp.where(causal, seg, -1e30))
                    mf = cb * lm
                    dyh = jnp.where(sel, dyp, 0.0).astype(BF16)
                    gm = _dot(dyh, xpb, _NT)
                    dcb = dcb + gm * lm
                    w = gm * mf
                    on_h = r1 == h
                    dacs = dacs + jnp.where(on_h, jnp.sum(w, axis=1, keepdims=True)
                                            + head_rowsum(t_acs[:, ps], hh), 0.0)
                    dacs_t = dacs_t - jnp.where(r0 == h, jnp.sum(w, axis=0, keepdims=True), 0.0)
                    dxp = dxp + _dot(mf.astype(BF16), dyh, _TN)
                    hrow = slice(CHUNK * jj + 64 * hh, CHUNK * jj + 64 * hh + 64)
                    s_last = (jnp.sum(head_rowsum(t_last[:, ps], hh), axis=0, keepdims=True)
                              + jnp.sum(t_cd[hrow, :], axis=0, keepdims=True))
                    dalast = dalast + jnp.where(lane_row == h, s_last, 0.0)
                dx_scr[:, sl] = dxp
                for hh in range(2):
                    ddt_dir = ddt_dir + jnp.where(r1 == 2 * j + hh,
                                                  head_rowsum(dxp * xs_ref[:, sl], hh), 0.0)
            dcbb = dcb.astype(BF16)
            dxbc_ref[:, D_SSD + N_STATE * g:D_SSD + N_STATE * (g + 1)] = db + _dot(dcbb, cg, _TN)
            dxbc_ref[:, D_SSD + _GW + N_STATE * g:D_SSD + _GW + N_STATE * (g + 1)] = dc + _dot(dcbb, bg)
        dxbc_ref[:, 0:D_SSD] = dx_scr[...] * dte_scr[...] + dsk_ref[...] * dy_ref[...]
        dacs = dacs + dacs_t.T + jnp.where(r0 == CHUNK - 1, dalast, 0.0)
        dadt = _dot_exact((r1 >= r0).astype(F32), dacs)
        ddt_ref[...] = dadt * a_ref[...] + ddt_dir
        da_ref[...] += jnp.sum(dadt * dt, axis=0, keepdims=True)

    rev = lambda s: (nc - 1 - s, 0)
    return pl.pallas_call(
        body, name="ssd_bwd",
        out_shape=(jax.ShapeDtypeStruct((t, CONV_DIM), F32), jax.ShapeDtypeStruct((t, N_SMALL), F32),
                   jax.ShapeDtypeStruct((1, N_SMALL), F32), jax.ShapeDtypeStruct((1, D_SSD), F32)),
        grid=(nc,),
        in_specs=[pl.BlockSpec((CHUNK, D_SSD), rev),
                  pl.BlockSpec((CHUNK, _GW), lambda s: (nc - 1 - s, 4)),
                  pl.BlockSpec((CHUNK, _GW), lambda s: (nc - 1 - s, 5)),
                  pl.BlockSpec((CHUNK, N_SMALL), rev),
                  pl.BlockSpec((1, N_SMALL), lambda s: (0, 0)),
                  pl.BlockSpec((1, D_SSD), lambda s: (0, 0)),
                  pl.BlockSpec((1, D_SSD, N_STATE), lambda s: (nc - 1 - s, 0, 0)),
                  pl.BlockSpec((CHUNK, D_SSD), rev)],
        out_specs=(pl.BlockSpec((CHUNK, CONV_DIM), rev),
                   pl.BlockSpec((CHUNK, N_SMALL), rev),
                   pl.BlockSpec((1, N_SMALL), lambda s: (0, 0)),
                   pl.BlockSpec((1, D_SSD), lambda s: (0, 0))),
        scratch_shapes=[pltpu.VMEM((D_SSD, N_STATE), F32)] + [pltpu.VMEM((CHUNK, D_SSD), F32)] * 4,
        compiler_params=_cp("arbitrary"),
    )(xbc, xbc, xbc, dtlf, a_row, dsk_row, hin, dy)


_NPAIR = H_ATT // 2
_QB, _KB, _VB = C_Q // 128, C_K // 128, C_V // 128
_SCALE = 1.0 / math.sqrt(64.0)


def _attn_blocks(t):
    return _tile(t, (1408, 384, 256, 128)), _tile(t, (384, 128))


def _split3(c):
    hi = c.astype(BF16).astype(F32)
    rest = c - hi
    mid = rest.astype(BF16).astype(F32)
    return hi, mid, rest - mid


def _head_lanes(lane, hh):
    return (lane < 64, 64) if hh == 0 else (lane >= 64, 0)


def _q_operand(q, cq, lane, hh):
    sel, first = _head_lanes(lane, hh)
    out = jnp.where(sel, q, 0.0)
    for n, col in enumerate(_split3(cq) + (1.0, 1.0, 1.0)):
        out = jnp.where(lane == first + n, col, out)
    return out.astype(BF16)


def _k_operand(k, ck, lane, hh):
    sel, first = _head_lanes(lane, hh)
    hi, mid, lo = _split3(ck)
    out = jnp.where(sel, k, 0.0)
    for n, col in enumerate((1.0, 1.0, 1.0, -hi, -mid, -lo)):
        out = jnp.where(lane == first + n, col, out)
    return out.astype(BF16)


def _needs_mask(i, kk, bq, bk):
    return kk * bk + bk - 1 > i * bq


_C_FILLER = 2.0 ** 30


def _attn_fwd(proj, c_col):
    t = proj.shape[0]
    bq, bk = _attn_blocks(t)
    nq, nk = t // bq, t // bk
    rs = 16

    def last_kv(i):
        return (i * bq + bq - 1) // bk

    def body(q_ref, k_ref, v_ref, cq_ref, ck_ref, o_ref, lse_ref, qs_scr, s_scr, p_scr, m_scr, acc_scr):
        i = pl.program_id(1)
        kk = pl.program_id(2)
        lane_q = _iota((bq, 128), 1)

        @pl.when(kk == 0)
        def _():
            m_scr[...] = jnp.full_like(m_scr, -1e30)
            acc_scr[...] = jnp.zeros_like(acc_scr)
            q = q_ref[...] * _SCALE
            cq = cq_ref[0]
            for hh in range(2):
                qs_scr[hh] = _q_operand(q, cq[:, hh:hh + 1], lane_q, hh)

        def step(masked):
            lane_k = _iota((bk, 128), 1)
            k = k_ref[...]
            v = v_ref[...]
            ck = ck_ref[0]
            ahead = _iota((rs, bq), 0) - _iota((rs, bq), 1)
            for hh in range(2):
                sel, first = _head_lanes(lane_k, hh)
                ks = _k_operand(k, ck[:, hh:hh + 1], lane_k, hh)
                vs = jnp.where(sel, v, jnp.where(lane_k == first, 1.0, 0.0)).astype(BF16)
                s_scr[...] = _dot(ks, qs_scr[hh], _NT)

                def block_max(r, mx):
                    rows = pl.ds(pl.multiple_of(r * rs, rs), rs)
                    s = s_scr[rows, :]
                    if masked:
                        s = jnp.where(ahead <= i * bq - kk * bk - r * rs, s, -1e30)
                        s_scr[rows, :] = s
                    return jnp.maximum(mx, s)

                mx = lax.fori_loop(0, bk // rs, block_max, jnp.full((rs, bq), -1e30, F32), unroll=True)
                m_old = m_scr[hh]
                m_new = jnp.maximum(m_old, jnp.max(mx, axis=0, keepdims=True))
                m_scr[hh] = m_new

                def probs(r, carry):
                    rows = pl.ds(pl.multiple_of(r * rs, rs), rs)
                    p_scr[rows, :] = jnp.exp(s_scr[rows, :] - m_new).astype(BF16)
                    return carry

                lax.fori_loop(0, bk // rs, probs, 0, unroll=True)
                acc_scr[hh] = acc_scr[hh] * jnp.exp(m_old - m_new) + _dot(vs, p_scr[...], _TN)

        active = kk <= last_kv(i)
        masked = _needs_mask(i, kk, bq, bk)

        @pl.when(active & masked)
        def _():
            step(True)

        @pl.when(active & jnp.logical_not(masked))
        def _():
            step(False)

        @pl.when(kk == nk - 1)
        def _():
            a = acc_scr[0]
            b = acc_scr[1]
            la = a[64:65, :]
            lb = b[0:1, :]
            o_ref[...] = jnp.where(lane_q < 64, (a / la).T, (b / lb).T)
            lse_ref[0] = jnp.concatenate([m_scr[0] + jnp.log(la), m_scr[1] + jnp.log(lb)], axis=0)

    kvi = lambda i, kk: jnp.minimum(kk, last_kv(i))
    kv = lambda off: pl.BlockSpec((bk, 128), lambda j, i, kk: (kvi(i, kk), off + j))
    return pl.pallas_call(
        body, name="attn_fwd",
        out_shape=(jax.ShapeDtypeStruct((t, D_ATT), F32), jax.ShapeDtypeStruct((_NPAIR, 2, t), F32)),
        grid=(_NPAIR, nq, nk),
        in_specs=[pl.BlockSpec((bq, 128), lambda j, i, kk: (i, _QB + j)),
                  kv(_KB), kv(_VB),
                  pl.BlockSpec((1, bq, 2), lambda j, i, kk: (j, i, 0)),
                  pl.BlockSpec((1, bk, 2), lambda j, i, kk: (j, kvi(i, kk), 0))],
        out_specs=(pl.BlockSpec((bq, 128), lambda j, i, kk: (i, j)),
                   pl.BlockSpec((1, 2, bq), lambda j, i, kk: (j, 0, i))),
        scratch_shapes=[pltpu.VMEM((2, bq, 128), BF16), pltpu.VMEM((bk, bq), F32), pltpu.VMEM((bk, bq), BF16),
                        pltpu.VMEM((2, 1, bq), F32), pltpu.VMEM((2, 128, bq), F32)],
        compiler_params=_cp("parallel", "parallel", "arbitrary"),
    )(proj, proj, proj, c_col, c_col)


def _attn_delta(do, o):
    t = do.shape[0]
    tm = _row_tile(t)

    def body(do_ref, o_ref, d_ref):
        pick = (jnp.right_shift(_iota((D_ATT, 128), 0), 6) == _iota((D_ATT, 128), 1)).astype(F32)
        d_ref[...] = _dot_exact(do_ref[...] * o_ref[...], pick)

    row = pl.BlockSpec((tm, D_ATT), lambda i: (i, 0))
    return pl.pallas_call(
        body, name="attn_delta",
        out_shape=jax.ShapeDtypeStruct((t, 128), F32),
        grid=(t // tm,), in_specs=[row, row], out_specs=pl.BlockSpec((tm, 128), lambda i: (i, 0)),
        compiler_params=_cp("parallel"),
    )(do, o)


def _attn_bwd(proj, c_col, lse_row, dl_row, do):
    t = proj.shape[0]
    bq, bk = _attn_blocks(t)
    nq, nk = t // bq, t // bk
    rs = 16

    def first_q(kk):
        return (kk * bk) // bq

    def body(q_ref, k_ref, v_ref, cq_ref, ck_ref, lse_ref, dl_ref, do_ref,
             dq_ref, dk_ref, dv_ref, dck_ref, dcq_ref,
             qs_scr, doh_scr, ks_scr, s_scr, dp_scr, p_scr, ds_scr, dq_scr, dk_scr, dv_scr):
        kk = pl.program_id(1)
        i = pl.program_id(2)
        lane_q = _iota((bq, 128), 1)
        lane_k = _iota((bk, 128), 1)
        qrows = pl.ds(pl.multiple_of(i * bq, 128), bq)

        @pl.when(kk == 0)
        def _():
            q = q_ref[...] * _SCALE
            cq = cq_ref[0]
            do_ = do_ref[...]
            for hh in range(2):
                qs_scr[hh, qrows, :] = _q_operand(q, cq[:, hh:hh + 1], lane_q, hh)
                doh_scr[hh, qrows, :] = jnp.where(_head_lanes(lane_q, hh)[0], do_, 0.0).astype(BF16)
                dq_scr[hh, qrows, :] = jnp.zeros((bq, 128), F32)

        @pl.when(i == 0)
        def _():
            dk_scr[...] = jnp.zeros_like(dk_scr)
            dv_scr[...] = jnp.zeros_like(dv_scr)
            k = k_ref[...]
            ck = ck_ref[0]
            for hh in range(2):
                ks_scr[hh] = _k_operand(k, ck[:, hh:hh + 1], lane_k, hh)

        def step(masked):
            v16 = v_ref[...].astype(BF16)
            lse = lse_ref[0]
            dl = dl_ref[0]
            ahead = _iota((rs, bq), 0) - _iota((rs, bq), 1)
            for hh in range(2):
                qs = qs_scr[hh, qrows, :]
                doh = doh_scr[hh, qrows, :]
                s_scr[...] = _dot(ks_scr[hh], qs, _NT)
                dp_scr[...] = _dot(v16, doh, _NT)

                def strip(r, carry):
                    rows = pl.ds(pl.multiple_of(r * rs, rs), rs)
                    p = jnp.exp(s_scr[rows, :] - lse[hh:hh + 1, :])
                    if masked:
                        p = jnp.where(ahead <= i * bq - kk * bk - r * rs, p, 0.0)
                    p_scr[rows, :] = p.astype(BF16)
                    ds_scr[rows, :] = (p * (dp_scr[rows, :] - dl[hh:hh + 1, :])).astype(BF16)
                    return carry

                lax.fori_loop(0, bk // rs, strip, 0, unroll=True)
                dv_scr[...] += _dot(p_scr[...], doh)
                dk_scr[hh] += _dot(ds_scr[...], qs)
                dq_scr[hh, qrows, :] += _dot(ds_scr[...], ks_scr[hh], _TN)

        active = i >= first_q(kk)
        masked = _needs_mask(i, kk, bq, bk)

        @pl.when(active & masked)
        def _():
            step(True)

        @pl.when(active & jnp.logical_not(masked))
        def _():
            step(False)

        @pl.when(i == nq - 1)
        def _():
            dka = dk_scr[0]
            dkb = dk_scr[1]
            dk_ref[...] = jnp.where(lane_k < 64, dka, dkb).astype(BF16)
            dv_ref[...] = dv_scr[...].astype(BF16)
            dck_ref[0] = -jnp.where(_iota((bk, 2), 1) == 0, dka[:, 67:68], dkb[:, 3:4])

        @pl.when((kk == nk - 1) & (i == nq - 1))
        def _():
            lane_t = _iota((t, 128), 1)
            dqa = dq_scr[0]
            dqb = dq_scr[1]
            dq_ref[...] = (jnp.where(lane_t < 64, dqa, dqb) * _SCALE).astype(BF16)
            dcq_ref[0] = jnp.where(_iota((t, 2), 1) == 0, dqa[:, 64:65], dqb[:, 0:1])

    qi = lambda kk, i: jnp.where(kk == 0, i, nq - 1)
    qspec = lambda off: pl.BlockSpec((bq, 128), lambda j, kk, i: (qi(kk, i), off + j))
    kspec = lambda off: pl.BlockSpec((bk, 128), lambda j, kk, i: (kk, off + j))
    rowspec = pl.BlockSpec((1, 2, bq), lambda j, kk, i: (j, 0, jnp.maximum(i, first_q(kk))))
    return pl.pallas_call(
        body, name="attn_bwd",
        out_shape=(jax.ShapeDtypeStruct((t, D_ATT), BF16), jax.ShapeDtypeStruct((t, D_ATT), BF16),
                   jax.ShapeDtypeStruct((t, D_ATT), BF16), jax.ShapeDtypeStruct((_NPAIR, t, 2), F32),
                   jax.ShapeDtypeStruct((_NPAIR, t, 2), F32)),
        grid=(_NPAIR, nk, nq),
        in_specs=[qspec(_QB), kspec(_KB), kspec(_VB),
                  pl.BlockSpec((1, bq, 2), lambda j, kk, i: (j, qi(kk, i), 0)),
                  pl.BlockSpec((1, bk, 2), lambda j, kk, i: (j, kk, 0)),
                  rowspec, rowspec, qspec(0)],
        out_specs=(pl.BlockSpec((t, 128), lambda j, kk, i: (0, j)),
                   pl.BlockSpec((bk, 128), lambda j, kk, i: (kk, j)),
                   pl.BlockSpec((bk, 128), lambda j, kk, i: (kk, j)),
                   pl.BlockSpec((1, bk, 2), lambda j, kk, i: (j, kk, 0)),
                   pl.BlockSpec((1, t, 2), lambda j, kk, i: (j, 0, 0))),
        scratch_shapes=[pltpu.VMEM((2, t, 128), BF16), pltpu.VMEM((2, t, 128), BF16), pltpu.VMEM((2, bk, 128), BF16),
                        pltpu.VMEM((bk, bq), F32), pltpu.VMEM((bk, bq), F32),
                        pltpu.VMEM((bk, bq), BF16), pltpu.VMEM((bk, bq), BF16),
                        pltpu.VMEM((2, t, 128), F32), pltpu.VMEM((2, bk, 128), F32), pltpu.VMEM((bk, 128), F32)],
        compiler_params=_cp("parallel", "arbitrary", "arbitrary"),
    )(proj, proj, proj, c_col, c_col, lse_row, dl_row, do)


def _premerge_fwd(y, o, proj, gamma):
    t = y.shape[0]
    tm = _row_tile_wide(t)

    def body(y_ref, z_ref, o_ref, za_ref, g_ref, ys_ref, ya_ref):
        z = z_ref[...]
        u = y_ref[...] * (z * _sigmoid(z))
        for g in range(G_SSD):
            gs = slice(_GW * g, _GW * (g + 1))
            ug = u[:, gs]
            r = lax.rsqrt(jnp.mean(ug * ug, axis=-1, keepdims=True) + EPS)
            ys_ref[:, gs] = (ug * r * g_ref[:, gs]).astype(BF16)
        za = za_ref[...]
        ya_ref[...] = (o_ref[...] * (za * _sigmoid(za))).astype(BF16)

    return pl.pallas_call(
        body, name="premerge_fwd",
        out_shape=(jax.ShapeDtypeStruct((t, D_SSD), BF16), jax.ShapeDtypeStruct((t, D_ATT), BF16)),
        grid=(t // tm,),
        in_specs=[pl.BlockSpec((tm, D_SSD), lambda i: (i, 0)),
                  pl.BlockSpec((tm, D_SSD), lambda i: (i, C_Z // D_SSD)),
                  pl.BlockSpec((tm, D_ATT), lambda i: (i, 0)),
                  pl.BlockSpec((tm, D_ATT), lambda i: (i, C_ZA // D_ATT)),
                  pl.BlockSpec((1, D_SSD), lambda i: (0, 0))],
        out_specs=(pl.BlockSpec((tm, D_SSD), lambda i: (i, 0)), pl.BlockSpec((tm, D_ATT), lambda i: (i, 0))),
        compiler_params=_cp("parallel"),
    )(y, proj, o, proj, gamma)


def _premerge_bwd(dys, dya, y, o, proj, gamma):
    t = y.shape[0]
    tm = _row_tile_wide(t)

    def body(dys_ref, dya_ref, y_ref, z_ref, o_ref, za_ref, g_ref, dy_ref, dz_ref, do_ref, dza_ref, dg_ref):
        i = pl.program_id(0)
        z = z_ref[...]
        sz = _sigmoid(z)
        silu = z * sz
        dsilu = sz * (1.0 + z * (1.0 - sz))
        yv = y_ref[...]
        u = yv * silu
        parts = []
        for g in range(G_SSD):
            gs = slice(_GW * g, _GW * (g + 1))
            ug = u[:, gs]
            r = lax.rsqrt(jnp.mean(ug * ug, axis=-1, keepdims=True) + EPS)
            n = ug * r
            dout = dys_ref[:, gs]
            dn = dout * g_ref[:, gs]
            du = r * (dn - n * jnp.mean(dn * n, axis=-1, keepdims=True))
            dy_ref[:, gs] = du * silu[:, gs]
            dz_ref[:, gs] = (du * yv[:, gs] * dsilu[:, gs]).astype(BF16)
            parts.append(jnp.sum(dout * n, axis=0, keepdims=True))
        dg = jnp.concatenate(parts, axis=1)
        za = za_ref[...]
        sa = _sigmoid(za)
        dya_ = dya_ref[...]
        do_ref[...] = dya_ * (za * sa)
        dza_ref[...] = (dya_ * o_ref[...] * (sa * (1.0 + za * (1.0 - sa)))).astype(BF16)

        @pl.when(i == 0)
        def _():
            dg_ref[...] = dg

        @pl.when(i > 0)
        def _():
            dg_ref[...] += dg

    ssd = pl.BlockSpec((tm, D_SSD), lambda i: (i, 0))
    att = pl.BlockSpec((tm, D_ATT), lambda i: (i, 0))
    vec = pl.BlockSpec((1, D_SSD), lambda i: (0, 0))
    return pl.pallas_call(
        body, name="premerge_bwd",
        out_shape=(jax.ShapeDtypeStruct((t, D_SSD), F32), jax.ShapeDtypeStruct((t, D_SSD), BF16),
                   jax.ShapeDtypeStruct((t, D_ATT), F32), jax.ShapeDtypeStruct((t, D_ATT), BF16),
                   jax.ShapeDtypeStruct((1, D_SSD), F32)),
        grid=(t // tm,),
        in_specs=[ssd, att, ssd, pl.BlockSpec((tm, D_SSD), lambda i: (i, C_Z // D_SSD)), att,
                  pl.BlockSpec((tm, D_ATT), lambda i: (i, C_ZA // D_ATT)), vec],
        out_specs=(ssd, ssd, att, att, vec),
        compiler_params=_cp("arbitrary"),
    )(dys, dya, y, proj, o, proj, gamma)


_G_BLK = C_G // D_MODEL


def _merge_fwd(a, b, proj, gate_bias):
    t = a.shape[0]
    tm = _row_tile(t)

    def body(a_ref, b_ref, gs_ref, ga_ref, bias_ref, m_ref):
        g_ssd = _sigmoid(gs_ref[...] + bias_ref[:, 0:D_MODEL])
        g_att = _sigmoid(ga_ref[...] + bias_ref[:, D_MODEL:2 * D_MODEL])
        m_ref[...] = (g_ssd * a_ref[...] + g_att * b_ref[...]).astype(BF16)

    row = pl.BlockSpec((tm, D_MODEL), lambda i: (i, 0))
    return pl.pallas_call(
        body, name="merge_fwd",
        out_shape=jax.ShapeDtypeStruct((t, D_MODEL), BF16),
        grid=(t // tm,),
        in_specs=[row, row,
                  pl.BlockSpec((tm, D_MODEL), lambda i: (i, _G_BLK)),
                  pl.BlockSpec((tm, D_MODEL), lambda i: (i, _G_BLK + 1)),
                  pl.BlockSpec((1, 2 * D_MODEL), lambda i: (0, 0))],
        out_specs=row,
        compiler_params=_cp("parallel"),
    )(a, b, proj, proj, gate_bias)


def _merge_bwd(dm, a, b, proj, gate_bias):
    t = a.shape[0]
    tm = _row_tile(t)

    def body(dm_ref, a_ref, b_ref, gs_ref, ga_ref, bias_ref, da_ref, db_ref, dg_ref, dbias_ref):
        i = pl.program_id(0)
        dm_ = dm_ref[...]
        g_ssd = _sigmoid(gs_ref[...] + bias_ref[:, 0:D_MODEL])
        g_att = _sigmoid(ga_ref[...] + bias_ref[:, D_MODEL:2 * D_MODEL])
        da_ref[...] = (dm_ * g_ssd).astype(BF16)
        db_ref[...] = (dm_ * g_att).astype(BF16)
        dgs = dm_ * a_ref[...] * g_ssd * (1.0 - g_ssd)
        dga = dm_ * b_ref[...] * g_att * (1.0 - g_att)
        dg_ref[:, 0:D_MODEL] = dgs.astype(BF16)
        dg_ref[:, D_MODEL:2 * D_MODEL] = dga.astype(BF16)
        part = jnp.concatenate([jnp.sum(dgs, axis=0, keepdims=True), jnp.sum(dga, axis=0, keepdims=True)], axis=1)

        @pl.when(i == 0)
        def _():
            dbias_ref[...] = part

        @pl.when(i > 0)
        def _():
            dbias_ref[...] += part

    row = pl.BlockSpec((tm, D_MODEL), lambda i: (i, 0))
    wide = pl.BlockSpec((tm, 2 * D_MODEL), lambda i: (i, 0))
    vec = pl.BlockSpec((1, 2 * D_MODEL), lambda i: (0, 0))
    return pl.pallas_call(
        body, name="merge_bwd",
        out_shape=(jax.ShapeDtypeStruct((t, D_MODEL), BF16), jax.ShapeDtypeStruct((t, D_MODEL), BF16),
                   jax.ShapeDtypeStruct((t, 2 * D_MODEL), BF16), jax.ShapeDtypeStruct((1, 2 * D_MODEL), F32)),
        grid=(t // tm,),
        in_specs=[row, row, row,
                  pl.BlockSpec((tm, D_MODEL), lambda i: (i, _G_BLK)),
                  pl.BlockSpec((tm, D_MODEL), lambda i: (i, _G_BLK + 1)), vec],
        out_specs=(row, row, wide, vec),
        compiler_params=_cp("arbitrary"),
    )(dm, a, b, proj, proj, gate_bias)


def _post(o2, h, target, g):
    t = o2.shape[0]
    nc = t // CHUNK

    def body(o_ref, h_ref, t_ref, g_ref, dy_ref, do_ref, dg_ref, loss_ref):
        c = pl.program_id(0)
        x = o_ref[...]
        r = lax.rsqrt(jnp.mean(x * x, axis=-1, keepdims=True) + EPS)
        n = x * r
        y = h_ref[...] + n * g_ref[...]
        diff = jnp.where(c > 0, y - t_ref[...], 0.0)
        dy = diff * (1.0 / D_MODEL)
        dy_ref[...] = dy
        gdy = dy * g_ref[...]
        do_ref[...] = (r * (gdy - n * jnp.mean(gdy * n, axis=-1, keepdims=True))).astype(BF16)
        dg = jnp.sum(dy * n, axis=0, keepdims=True)
        lpart = 0.5 * jnp.sum(jnp.sum(diff * diff, axis=1, keepdims=True), axis=0, keepdims=True) * (1.0 / D_MODEL)
        sel = (_iota((8, 128), 0) == 0) & (_iota((8, 128), 1) == 0)

        @pl.when(c == 0)
        def _():
            dg_ref[...] = dg
            loss_ref[...] = jnp.zeros_like(loss_ref)

        @pl.when(c > 0)
        def _():
            dg_ref[...] += dg
            loss_ref[...] += jnp.where(sel, lpart, 0.0)

    row = pl.BlockSpec((CHUNK, D_MODEL), lambda c: (c, 0))
    vec = pl.BlockSpec((1, D_MODEL), lambda c: (0, 0))
    return pl.pallas_call(
        body, name="post",
        out_shape=(jax.ShapeDtypeStruct((t, D_MODEL), F32), jax.ShapeDtypeStruct((t, D_MODEL), BF16),
                   jax.ShapeDtypeStruct((1, D_MODEL), F32), jax.ShapeDtypeStruct((8, 128), F32)),
        grid=(nc,),
        in_specs=[row, row, pl.BlockSpec((CHUNK, D_MODEL), lambda c: (jnp.maximum(c - 1, 0), 0)), vec],
        out_specs=(row, row, vec, pl.BlockSpec((8, 128), lambda c: (0, 0))),
        compiler_params=_cp("arbitrary"),
    )(o2, h, target, g)


def _mm_tiles(t):
    return _tile(t, (704, 384, 128))


def _local_step(h, target, w_main, w_small, wps, wpa, wout, norm_pre, conv_w, conv_b, bias_row, a_row,
                dsk_row, ssd_norm, gate_bias, norm_post):
    t = h.shape[0]
    tm = _mm_tiles(t)
    u = _norm1_fwd(h, norm_pre)
    proj = _matmul(u, w_main, "nt", F32, "inproj", tm, 1024, D_MODEL)
    small = _matmul(u, w_small, "nt", F32, "inproj_small", tm, N_SMALL, D_MODEL)
    dtlf = _small_fwd(small, bias_row)
    xbc = _conv_fwd(proj, conv_w, conv_b)
    y, hin = _ssd_fwd(xbc, dtlf, a_row, dsk_row)
    c_tok = dtlf[:, H_SSD:H_SSD + H_ATT]
    c_tok = jnp.where(jnp.arange(t)[:, None] < PADF, _C_FILLER, c_tok)
    c_col = c_tok.reshape(t, _NPAIR, 2).transpose(1, 0, 2)
    o, lse = _attn_fwd(proj, c_col)
    ys, ya = _premerge_fwd(y, o, proj, ssd_norm)
    a = _matmul(ys, wps, "nn", F32, "proj_ssd", tm, D_MODEL, D_SSD)
    b = _matmul(ya, wpa, "nn", F32, "proj_att", tm, D_MODEL, D_ATT)
    merged = _merge_fwd(a, b, proj, gate_bias)
    o2 = _matmul(merged, wout, "nn", F32, "out_proj", tm, D_MODEL, D_MODEL)
    dy_out, do2, d_norm_post, loss_blk = _post(o2, h, target, norm_post)

    dm = _matmul(do2, wout, "nt", F32, "out_proj_dx", tm, D_MODEL, D_MODEL)
    d_wout = _matmul(merged, do2, "tn", F32, "out_proj_dw", D_MODEL, D_MODEL, tm)
    da, db, dgraw, d_gate_bias = _merge_bwd(dm, a, b, proj, gate_bias)
    dys = _matmul(da, wps, "nt", F32, "proj_ssd_dx", tm, D_SSD, D_MODEL)
    d_wps = _matmul(ys, da, "tn", F32, "proj_ssd_dw", D_SSD, D_MODEL, tm)
    dya = _matmul(db, wpa, "nt", F32, "proj_att_dx", tm, D_ATT, D_MODEL)
    d_wpa = _matmul(ya, db, "tn", F32, "proj_att_dw", D_ATT, D_MODEL, tm)
    dy, dz, do, dza, d_ssd_norm = _premerge_bwd(dys, dya, y, o, proj, ssd_norm)
    dl_row = _attn_delta(do, o)[:, 0:H_ATT].T.reshape(_NPAIR, 2, t)
    dq, dk, dv, dc_key, dc_qry = _attn_bwd(proj, c_col, lse, dl_row, do)
    dxbc, ddt, d_a, d_dsk = _ssd_bwd(xbc, dtlf, a_row, dsk_row, hin, dy)
    dact, d_conv_w, d_conv_b = _conv_bwd_act(dxbc, proj, conv_w, conv_b)
    dxbc_raw = _conv_bwd_in(dact, conv_w)
    dc_tok = jnp.transpose(dc_key + dc_qry, (1, 0, 2)).reshape(t, H_ATT)
    dsm = ddt + jnp.pad(dc_tok, ((0, 0), (H_SSD, N_SMALL - H_SSD - H_ATT)))
    dsmall, d_bias_row = _small_bwd(dsm, small, bias_row)
    dproj = jnp.concatenate([dz, dxbc_raw, dza, dq, dk, dv, dgraw], axis=1)
    du_a = _matmul(dproj, w_main, "nn", F32, "inproj_dx", tm, D_MODEL, 1024)
    du_b = _matmul(dsmall, w_small, "nn", F32, "inproj_small_dx", tm, D_MODEL, N_SMALL)
    d_w_main = _matmul(dproj, u, "tn", F32, "inproj_dw", 1024, D_MODEL, tm)
    d_w_small = _matmul(dsmall, u, "tn", F32, "inproj_small_dw", N_SMALL, D_MODEL, tm)
    dh, d_norm_pre = _norm1_bwd(du_a, du_b, h, norm_pre, dy_out)
    return dict(loss_blk=loss_blk, dh=dh, d_w_main=d_w_main, d_w_small=d_w_small, d_wps=d_wps, d_wpa=d_wpa,
                d_wout=d_wout, d_norm_pre=d_norm_pre, d_conv_w=d_conv_w, d_conv_b=d_conv_b,
                d_bias_row=d_bias_row, d_a=d_a, d_dsk=d_dsk, d_ssd_norm=d_ssd_norm,
                d_gate_bias=d_gate_bias, d_norm_post=d_norm_post)


def _to_aligned_rows(w):
    def cut(o):
        return w[o[0]:o[0] + o[1]]
    main = jnp.concatenate([cut(O_Z), cut(O_XBC), cut(O_ZA), cut(O_Q), cut(O_K), cut(O_V), cut(O_G)], axis=0)
    pad = jnp.zeros((N_SMALL - H_SSD - H_ATT, w.shape[1]), w.dtype)
    small = jnp.concatenate([cut(O_DT), cut(O_F), pad], axis=0)
    return main, small


def _from_aligned_rows(main, small):
    def cm(c0, n):
        return main[c0:c0 + n]
    return jnp.concatenate([cm(C_Z, 2048), cm(C_XBC, 3072), small[0:H_SSD], cm(C_ZA, 1024),
                            cm(C_Q, 1024), cm(C_K, 1024), cm(C_V, 1024), small[H_SSD:H_SSD + H_ATT],
                            cm(C_G, 2048)], axis=0)


_MESH = pl.DeviceIdType.MESH
_ANY = pl.BlockSpec(memory_space=pl.ANY)
_VM = pl.BlockSpec(memory_space=pltpu.VMEM)
_HALF = 512
N_DEV = 8


def _coords():
    return lax.axis_index("x"), lax.axis_index("y"), lax.axis_index("c")


def _other_chips(x, y):
    return [(1 - x, y), (x, 1 - y), (1 - x, 1 - y)]


def _half(cc):
    return pl.ds(cc * _HALF, _HALF)


def _gather_shards(slots):
    n = len(slots)

    def body(*refs):
        buf = refs[n:2 * n]
        send_sems, recv_sems = refs[2 * n:]
        x, y, c = _coords()
        chip = 2 * x + y
        sibling = (x, y, 1 - c)
        chips = _other_chips(x, y)

        def copy(i, frm, cc, k, to):
            part = buf[i].at[frm, :, _half(cc)]
            return pltpu.make_async_remote_copy(src_ref=part, dst_ref=part, send_sem=send_sems.at[6 * i + k],
                                                recv_sem=recv_sems.at[6 * i + k], device_id=to, device_id_type=_MESH)

        def chip_of(k):
            return 2 * chips[k][0] + chips[k][1]

        first = [copy(i, chip, c, k, (*chips[k], c)) for k in range(3) for i in range(n)]
        for cp in first:
            cp.start()
        passed = []
        for k in range(3):
            for i in range(n):
                copy(i, chip_of(k), c, k, (*chips[k], c)).wait_recv()
                passed.append(copy(i, chip_of(k), c, 3 + k, sibling))
                passed[-1].start()
        for k in range(3):
            for i in range(n):
                copy(i, chip_of(k), 1 - c, 3 + k, sibling).wait_recv()
        for cp in first + passed:
            cp.wait_send()

    return pl.pallas_call(
        body, name="gather_shards",
        out_shape=tuple(jax.ShapeDtypeStruct(s.shape, s.dtype) for s in slots),
        in_specs=[_ANY] * n, out_specs=tuple([_ANY] * n),
        input_output_aliases={i: i for i in range(n)},
        scratch_shapes=[pltpu.SemaphoreType.DMA((6 * n,)), pltpu.SemaphoreType.DMA((6 * n,))],
    )(*slots)


def _allgather8(block, name):
    rows, width = block.shape

    def body(x_ref, out_ref, send_sems, recv_sems, local_sem):
        x, y, c = _coords()
        me, sibling = (x, y, c), (x, y, 1 - c)
        chips = _other_chips(x, y)

        def slot(px, py, pc):
            return out_ref.at[4 * px + 2 * py + pc]

        def copy(k, blk, to, src=None):
            return pltpu.make_async_remote_copy(src_ref=slot(*blk) if src is None else src, dst_ref=slot(*blk),
                                                send_sem=send_sems.at[k], recv_sem=recv_sems.at[k],
                                                device_id=to, device_id_type=_MESH)

        mine = pltpu.make_async_copy(x_ref, slot(*me), local_sem)
        mine.start()
        first = [copy(0, me, sibling, src=x_ref)]
        first += [copy(1 + j, me, (*chip, c), src=x_ref) for j, chip in enumerate(chips)]
        for cp in first:
            cp.start()
        passed = [copy(4 + j, (*chip, c), sibling) for j, chip in enumerate(chips)]
        for j, chip in enumerate(chips):
            copy(1 + j, (*chip, c), me).wait_recv()
            passed[j].start()
        copy(0, sibling, me).wait_recv()
        for j, chip in enumerate(chips):
            copy(4 + j, (*chip, 1 - c), me).wait_recv()
        for cp in first + passed:
            cp.wait_send()
        mine.wait()

    return pl.pallas_call(
        body, name=name,
        out_shape=jax.ShapeDtypeStruct((N_DEV, rows, width), block.dtype),
        in_specs=[_VM], out_specs=_VM,
        scratch_shapes=[pltpu.SemaphoreType.DMA((7,)), pltpu.SemaphoreType.DMA((7,)), pltpu.SemaphoreType.DMA],
    )(block)


def _pair_swap_halves(arrs):
    n = len(arrs)

    def body(*refs):
        src, dst = refs[:n], refs[n:2 * n]
        send_sems, recv_sems = refs[2 * n:]
        x, y, c = _coords()
        cps = [pltpu.make_async_remote_copy(src_ref=src[i].at[:, :, _half(1 - c)], dst_ref=dst[i],
                                            send_sem=send_sems.at[i], recv_sem=recv_sems.at[i],
                                            device_id=(x, y, 1 - c), device_id_type=_MESH) for i in range(n)]
        for cp in cps:
            cp.start()
        for cp in cps:
            cp.wait()

    return pl.pallas_call(
        body, name="pair_swap_halves",
        out_shape=tuple(jax.ShapeDtypeStruct((4, a.shape[1], _HALF), a.dtype) for a in arrs),
        in_specs=[_ANY] * n, out_specs=tuple([_ANY] * n),
        scratch_shapes=[pltpu.SemaphoreType.DMA((n,)), pltpu.SemaphoreType.DMA((n,))],
    )(*arrs)


def _chip_exchange(arrs):
    n = len(arrs)

    def body(*refs):
        src, dst = refs[:n], refs[n:2 * n]
        send_sems, recv_sems = refs[2 * n:]
        x, y, c = _coords()
        chips = _other_chips(x, y)
        cps = [pltpu.make_async_remote_copy(src_ref=src[i].at[2 * chips[k][0] + chips[k][1]], dst_ref=dst[i].at[k],
                                            send_sem=send_sems.at[3 * i + k], recv_sem=recv_sems.at[3 * i + k],
                                            device_id=(*chips[k], c), device_id_type=_MESH)
               for k in range(3) for i in range(n)]
        for cp in cps:
            cp.start()
        for cp in cps:
            cp.wait()

    return pl.pallas_call(
        body, name="chip_exchange",
        out_shape=tuple(jax.ShapeDtypeStruct((3,) + a.shape[1:], a.dtype) for a in arrs),
        in_specs=[_ANY] * n, out_specs=tuple([_ANY] * n),
        scratch_shapes=[pltpu.SemaphoreType.DMA((3 * n,)), pltpu.SemaphoreType.DMA((3 * n,))],
    )(*arrs)


def _pair_join_halves(fulls):
    n = len(fulls)

    def body(*refs):
        buf = refs[n:2 * n]
        send_sems, recv_sems = refs[2 * n:]
        x, y, c = _coords()

        def remote(i, cc):
            part = buf[i].at[:, _half(cc)]
            return pltpu.make_async_remote_copy(src_ref=part, dst_ref=part, send_sem=send_sems.at[i],
                                                recv_sem=recv_sems.at[i], device_id=(x, y, 1 - c), device_id_type=_MESH)

        for i in range(n):
            remote(i, c).start()
        for i in range(n):
            remote(i, c).wait_send()
            remote(i, 1 - c).wait_recv()

    return pl.pallas_call(
        body, name="pair_join_halves",
        out_shape=tuple(jax.ShapeDtypeStruct(a.shape, a.dtype) for a in fulls),
        in_specs=[_ANY] * n, out_specs=tuple([_ANY] * n),
        input_output_aliases={i: i for i in range(n)},
        scratch_shapes=[pltpu.SemaphoreType.DMA((n,)), pltpu.SemaphoreType.DMA((n,))],
    )(*fulls)


_RED_TC = 128
_RED_NT = _HALF // _RED_TC


def _add_pair(ids, g32, recv_a):
    rows = g32.shape[1]

    def body(ids_ref, g_ref, r_ref, o_ref):
        o_ref[...] = (g_ref[...] + r_ref[...]).astype(BF16)

    blk = pl.BlockSpec((1, rows, _RED_TC), lambda j, l, ids: (j, 0, l))
    return pl.pallas_call(
        body, name="add_pair",
        out_shape=jax.ShapeDtypeStruct((4, rows, _HALF), BF16),
        grid_spec=pltpu.PrefetchScalarGridSpec(
            num_scalar_prefetch=1, grid=(4, _RED_NT),
            in_specs=[pl.BlockSpec((1, rows, _RED_TC), lambda j, l, ids: (j, 0, ids[0] * _RED_NT + l)), blk],
            out_specs=blk),
        compiler_params=_cp("parallel", "parallel"),
    )(ids, g32, recv_a)


def _add_chips(ids, g32, recv_a, recv_b):
    rows = g32.shape[1]

    def body(ids_ref, g_ref, a_ref, b_ref, o_ref):
        acc = g_ref[0] + a_ref[0]
        for k in range(3):
            acc = acc + b_ref[k].astype(F32)
        o_ref[...] = acc

    return pl.pallas_call(
        body, name="add_chips",
        out_shape=jax.ShapeDtypeStruct((rows, 2 * _HALF), F32),
        grid_spec=pltpu.PrefetchScalarGridSpec(
            num_scalar_prefetch=1, grid=(_RED_NT,),
            in_specs=[pl.BlockSpec((1, rows, _RED_TC), lambda l, ids: (ids[1], 0, ids[0] * _RED_NT + l)),
                      pl.BlockSpec((1, rows, _RED_TC), lambda l, ids: (ids[1], 0, l)),
                      pl.BlockSpec((3, rows, _RED_TC), lambda l, ids: (0, 0, l))],
            out_specs=pl.BlockSpec((rows, _RED_TC), lambda l, ids: (0, ids[0] * _RED_NT + l))),
        compiler_params=_cp("parallel"),
    )(ids, g32, recv_a, recv_b)


def _sum8(gathered):
    _, rows, width = gathered.shape

    def body(g_ref, o_ref):
        acc = g_ref[0]
        for d in range(1, N_DEV):
            acc = acc + g_ref[d]
        o_ref[...] = acc

    return pl.pallas_call(
        body, name="sum8",
        out_shape=jax.ShapeDtypeStruct((rows, width), F32),
        in_specs=[_VM], out_specs=_VM,
    )(gathered)


def _adamw(w, g, m, v, name):
    rows, cols = w.shape
    budget = (3 << 20) // 2
    tr, tc = rows, cols
    if rows * cols * 4 > budget:
        if rows % 8 == 0:
            tr = next(c for c in (512, 256, 128, 64, 32, 16, 8) if rows % c == 0 and c * cols * 4 <= budget)
        else:
            tc = next(c for c in (512, 256, 128) if cols % c == 0 and rows * c * 4 <= budget)
    c1 = 1.0 - ADAM_B1 ** ADAM_STEP
    c2 = 1.0 - ADAM_B2 ** ADAM_STEP

    def body(w_ref, g_ref, m_ref, v_ref, d_ref, mo_ref, vo_ref):
        gg = g_ref[...]
        mn = ADAM_B1 * m_ref[...] + (1.0 - ADAM_B1) * gg
        vn = ADAM_B2 * v_ref[...] + (1.0 - ADAM_B2) * (gg * gg)
        mo_ref[...] = mn
        vo_ref[...] = vn
        d_ref[...] = -ADAM_LR * ((mn / c1) / (jnp.sqrt(vn / c2) + ADAM_EPS) + ADAM_WD * w_ref[...])

    blk = pl.BlockSpec((tr, tc), lambda i, j: (i, j))
    shp = jax.ShapeDtypeStruct((rows, cols), F32)
    return pl.pallas_call(
        body, name=name, out_shape=(shp, shp, shp), grid=(rows // tr, cols // tc),
        in_specs=[blk] * 4, out_specs=(blk, blk, blk),
        compiler_params=_cp("parallel", "parallel"),
    )(w, g, m, v)


def _rows128(a):
    return a.reshape(-1, 128)


def _pack_small(norm_pre, conv_b, ssd_norm, gate_bias, norm_post, dt_bias, a_log, d_skip, fgate_bias):
    tiny = jnp.concatenate([dt_bias.reshape(-1), a_log.reshape(-1), d_skip.reshape(-1), fgate_bias.reshape(-1),
                            jnp.zeros((16,), F32)])
    return jnp.concatenate([_rows128(norm_pre), _rows128(conv_b), _rows128(ssd_norm), _rows128(gate_bias),
                            _rows128(norm_post), tiny.reshape(1, 128)], axis=0)


_SMALL_ROWS = 73
_SMALL_PAD = 80


def _unpack_small(p):
    tiny = p[72]
    return dict(norm_pre=p[0:8].reshape(1, 1024), conv_b=p[8:32].reshape(1, 3072), ssd_norm=p[32:48].reshape(1, 2048),
                gate_bias=p[48:64].reshape(1, 2048), norm_post=p[64:72].reshape(1, 1024),
                dt_bias=tiny[0:32].reshape(1, 32), a_log=tiny[32:64].reshape(1, 32),
                d_skip=tiny[64:96].reshape(1, 32), fgate_bias=tiny[96:112].reshape(1, 16))


def _pad_rows(a, rows):
    return jnp.concatenate([a, jnp.zeros((rows - a.shape[0], a.shape[1]), a.dtype)], axis=0)


def kernel(x, meta_tokens, norm_pre, w_in, conv_w, conv_b, dt_bias, a_log, d_skip, ssd_norm, fgate_bias, gate_bias, w_proj_ssd, w_proj_att, w_out, norm_post, loss_target, m_meta_tokens, m_norm_pre, m_w_in, m_conv_w, m_conv_b, m_dt_bias, m_a_log, m_d_skip, m_ssd_norm, m_fgate_bias, m_gate_bias, m_w_proj_ssd, m_w_proj_att, m_w_out, m_norm_post, v_meta_tokens, v_norm_pre, v_w_in, v_conv_w, v_conv_b, v_dt_bias, v_a_log, v_d_skip, v_ssd_norm, v_fgate_bias, v_gate_bias, v_w_proj_ssd, v_w_proj_att, v_w_out, v_norm_post):
    cx, cy, cc = _coords()
    chip = 2 * cx + cy
    ids = jnp.stack([cc, chip]).astype(jnp.int32)
    seq = x.shape[1]

    w_in_sh = jnp.transpose(w_in[0]).astype(BF16)
    w_pr_sh = jnp.concatenate([w_proj_ssd[0], w_proj_att[0], w_out[0]], axis=0).astype(BF16)

    def own_slot(sh):
        return lax.dynamic_update_slice(jnp.zeros((4,) + sh.shape, sh.dtype), sh[None], (chip, 0, 0))

    g_in, g_pr = _gather_shards([own_slot(w_in_sh), own_slot(w_pr_sh)])
    w_main, w_small = _to_aligned_rows(g_in.reshape(N_COLS, D_MODEL))
    wps = g_pr[:, 0:512].reshape(D_SSD, D_MODEL)
    wpa = g_pr[:, 512:768].reshape(D_ATT, D_MODEL)
    wout = g_pr[:, 768:1024].reshape(D_MODEL, D_MODEL)
    sm_sh = jnp.concatenate([_rows128(meta_tokens), _rows128(conv_w[0])], axis=0)
    sm_all = _allgather8(sm_sh, "gather_small_weights")[0::2]
    meta_full = jnp.transpose(sm_all[:, 0:32].reshape(4, N_META, 256), (1, 0, 2)).reshape(N_META, D_MODEL)
    conv_w_full = jnp.transpose(sm_all[:, 32:56].reshape(4, CONV_K, 768), (1, 0, 2)).reshape(CONV_K, CONV_DIM)

    h = jnp.concatenate([jnp.zeros((PADF, D_MODEL), F32), meta_full, x[0]], axis=0)
    bias_row = jnp.concatenate([dt_bias[0], fgate_bias[0], jnp.zeros((N_SMALL - H_SSD - H_ATT,), F32)]).reshape(1, N_SMALL)
    a_neg = -jnp.exp(a_log[0])
    a_row = jnp.concatenate([a_neg, jnp.zeros((N_SMALL - H_SSD,), F32)]).reshape(1, N_SMALL)
    dsk_row = jnp.repeat(d_skip[0], 64).reshape(1, D_SSD)
    r = _local_step(h, loss_target[0], w_main, w_small, wps, wpa, wout, norm_pre, conv_w_full, conv_b, bias_row,
                    a_row, dsk_row, ssd_norm, gate_bias, norm_post)
    dh = r["dh"]
    grad_x = dh[PADF + N_META:].reshape(1, seq, D_MODEL)

    tiny = r["d_bias_row"][0]
    part_small = _pack_small(r["d_norm_pre"], r["d_conv_b"], r["d_ssd_norm"], r["d_gate_bias"], r["d_norm_post"],
                             tiny[0:H_SSD], r["d_a"][0, 0:H_SSD] * a_neg, r["d_dsk"].reshape(H_SSD, 64).sum(axis=1),
                             tiny[H_SSD:H_SSD + H_ATT])
    part = jnp.concatenate([_pad_rows(part_small, _SMALL_PAD), _rows128(r["d_conv_w"]),
                            _rows128(dh[PADF:PADF + N_META]), r["loss_blk"]], axis=0)
    tot = _sum8(_allgather8(part, "gather_small_grads"))
    loss = tot[_SMALL_PAD + 96 + 128, 0]
    g_small = tot[0:_SMALL_PAD]
    g_conv_w = lax.dynamic_slice_in_dim(tot[_SMALL_PAD:_SMALL_PAD + 96].reshape(CONV_K, CONV_DIM), chip * 768, 768, axis=1)
    g_meta = lax.dynamic_slice_in_dim(tot[_SMALL_PAD + 96:_SMALL_PAD + 224].reshape(N_META, D_MODEL), chip * 256, 256, axis=1)

    g32_in = _from_aligned_rows(r["d_w_main"], r["d_w_small"]).reshape(4, N_COLS // 4, D_MODEL)
    g32_pr = jnp.concatenate([r["d_wps"].reshape(4, 512, D_MODEL), r["d_wpa"].reshape(4, 256, D_MODEL),
                              r["d_wout"].reshape(4, 256, D_MODEL)], axis=1)
    ra_in, ra_pr = _pair_swap_halves([g32_in, g32_pr])
    pb_in = _add_pair(ids, g32_in, ra_in)
    pb_pr = _add_pair(ids, g32_pr, ra_pr)
    rb_in, rb_pr = _chip_exchange([pb_in, pb_pr])
    half_in = _add_chips(ids, g32_in, ra_in, rb_in)
    half_pr = _add_chips(ids, g32_pr, ra_pr, rb_pr)
    gw_in, gw_pr = _pair_join_halves([half_in, half_pr])

    upd = {}
    upd["w_in"] = tuple(jnp.transpose(a) for a in (gw_in,) + _adamw(
        jnp.transpose(w_in[0]), gw_in, jnp.transpose(m_w_in[0]), jnp.transpose(v_w_in[0]), "adamw_w_in"))
    w_pr32 = jnp.concatenate([w_proj_ssd[0], w_proj_att[0], w_out[0]], axis=0)
    m_pr = jnp.concatenate([m_w_proj_ssd[0], m_w_proj_att[0], m_w_out[0]], axis=0)
    v_pr = jnp.concatenate([v_w_proj_ssd[0], v_w_proj_att[0], v_w_out[0]], axis=0)
    pr = (gw_pr,) + _adamw(w_pr32, gw_pr, m_pr, v_pr, "adamw_w_proj")
    upd["w_proj_ssd"] = tuple(a[0:512] for a in pr)
    upd["w_proj_att"] = tuple(a[512:768] for a in pr)
    upd["w_out"] = tuple(a[768:1024] for a in pr)
    upd["conv_w"] = (g_conv_w,) + _adamw(conv_w[0], g_conv_w, m_conv_w[0], v_conv_w[0], "adamw_conv_w")
    upd["meta_tokens"] = (g_meta,) + _adamw(meta_tokens, g_meta, m_meta_tokens, v_meta_tokens, "adamw_meta")
    pk = lambda np_, cb, sn, gb, npo, dtb, al, ds, fg: _pad_rows(_pack_small(np_, cb, sn, gb, npo, dtb, al, ds, fg), _SMALL_PAD)
    w_sm = pk(norm_pre, conv_b, ssd_norm, gate_bias, norm_post, dt_bias, a_log, d_skip, fgate_bias)
    m_sm = pk(m_norm_pre, m_conv_b, m_ssd_norm, m_gate_bias, m_norm_post, m_dt_bias, m_a_log, m_d_skip, m_fgate_bias)
    v_sm = pk(v_norm_pre, v_conv_b, v_ssd_norm, v_gate_bias, v_norm_post, v_dt_bias, v_a_log, v_d_skip, v_fgate_bias)
    sm = [_unpack_small(a) for a in (g_small,) + _adamw(w_sm, g_small, m_sm, v_sm, "adamw_small")]
    for name in ("norm_pre", "conv_b", "dt_bias", "a_log", "d_skip", "ssd_norm", "fgate_bias", "gate_bias", "norm_post"):
        upd[name] = tuple(s[name] for s in sm)
    lead = ("w_in", "conv_w", "w_proj_ssd", "w_proj_att", "w_out")
    order = ("meta_tokens", "norm_pre", "w_in", "conv_w", "conv_b", "dt_bias", "a_log", "d_skip", "ssd_norm",
             "fgate_bias", "gate_bias", "w_proj_ssd", "w_proj_att", "w_out", "norm_post")
    outs = [loss, grad_x]
    for part_i in range(4):
        for name in order:
            a = upd[name][part_i]
            outs.append(a[None] if name in lead else a)
    return tuple(outs)
```

```python
import functools
import math

import jax
import jax.numpy as jnp
from jax import lax
from jax.experimental import pallas as pl
from jax.experimental.pallas import tpu as pltpu

F32 = jnp.float32
BF16 = jnp.bfloat16
HIGHEST = lax.Precision.HIGHEST

D_MODEL = 1024
N_META = 16
CHUNK = 128
PADF = CHUNK - N_META
D_SSD = 2048
H_SSD = 32
G_SSD = 4
N_STATE = 128
CONV_K = 4
CONV_DIM = D_SSD + 2 * G_SSD * N_STATE
H_ATT = 16
D_ATT = 1024
EPS = 1e-6
N_COLS = 11312

C_Z, C_XBC, C_ZA, C_Q, C_K, C_V, C_G = 0, 2048, 5120, 6144, 7168, 8192, 9216
N_MAIN = 11264
N_SMALL = 128
O_Z, O_XBC, O_DT, O_ZA, O_Q, O_K, O_V, O_F, O_G = (
    (0, 2048), (2048, 3072), (5120, 32), (5152, 1024), (6176, 1024), (7200, 1024),
    (8224, 1024), (9248, 16), (9264, 2048))

ADAM_LR, ADAM_B1, ADAM_B2, ADAM_EPS, ADAM_WD, ADAM_STEP = 0.001, 0.9, 0.999, 1e-08, 0.01, 10

VMEM_LIMIT = 56 * 1024 * 1024


def _cp(*sem):
    return pltpu.CompilerParams(dimension_semantics=sem, vmem_limit_bytes=VMEM_LIMIT)


def _tile(n, prefs):
    for p in prefs:
        if n % p == 0:
            return p
    raise ValueError(f"no tile for {n} in {prefs}")


def _iota(shape, dim):
    return lax.broadcasted_iota(jnp.int32, shape, dim)


def _sigmoid(x):
    return 1.0 / (1.0 + jnp.exp(-x))


def _softplus_tail(x):
    return jnp.log(1.0 + jnp.exp(-jnp.abs(x)))


_NN = (((1,), (0,)), ((), ()))
_NT = (((1,), (1,)), ((), ()))
_TN = (((0,), (0,)), ((), ()))


def _dot(a, b, dims=_NN):
    return lax.dot_general(a, b, dims, preferred_element_type=F32)


def _dot_exact(a, b, dims=_NN):
    return lax.dot_general(a, b, dims, precision=HIGHEST, preferred_element_type=F32)


def _matmul(a, b, mode, out_dtype, name, tm, tn, tk):
    if mode == "tn":
        kdim, m = a.shape
    else:
        m, kdim = a.shape
    n = b.shape[0] if mode == "nt" else b.shape[1]
    nk = kdim // tk
    dims = {"nn": _NN, "nt": _NT, "tn": _TN}[mode]
    a_spec = (pl.BlockSpec((tk, tm), lambda i, j, k: (k, i)) if mode == "tn"
              else pl.BlockSpec((tm, tk), lambda i, j, k: (i, k)))
    b_spec = (pl.BlockSpec((tn, tk), lambda i, j, k: (j, k)) if mode == "nt"
              else pl.BlockSpec((tk, tn), lambda i, j, k: (k, j)))

    def body(a_ref, b_ref, o_ref, acc_ref):
        k = pl.program_id(2)
        p = _dot(a_ref[...].astype(BF16), b_ref[...].astype(BF16), dims)
        if nk == 1:
            o_ref[...] = p.astype(out_dtype)
        else:
            @pl.when(k == 0)
            def _():
                acc_ref[...] = p

            @pl.when(k > 0)
            def _():
                acc_ref[...] += p

            @pl.when(k == nk - 1)
            def _():
                o_ref[...] = acc_ref[...].astype(out_dtype)

    return pl.pallas_call(
        body, name=name,
        out_shape=jax.ShapeDtypeStruct((m, n), out_dtype),
        grid=(m // tm, n // tn, nk),
        in_specs=[a_spec, b_spec],
        out_specs=pl.BlockSpec((tm, tn), lambda i, j, k: (i, j)),
        scratch_shapes=[pltpu.VMEM((tm, tn), F32)],
        compiler_params=_cp("parallel", "parallel", "arbitrary"),
    )(a, b)


def _row_tile(t):
    return _tile(t, (352, 128))


def _row_tile_wide(t):
    return _tile(t, (176, 128))


def _norm1_fwd(h, g):
    t = h.shape[0]
    tm = _row_tile(t)

    def body(h_ref, g_ref, u_ref):
        x = h_ref[...]
        r = lax.rsqrt(jnp.mean(x * x, axis=-1, keepdims=True) + EPS)
        u_ref[...] = (x * r * g_ref[...]).astype(BF16)

    return pl.pallas_call(
        body, name="norm1_fwd",
        out_shape=jax.ShapeDtypeStruct((t, D_MODEL), BF16),
        grid=(t // tm,),
        in_specs=[pl.BlockSpec((tm, D_MODEL), lambda i: (i, 0)),
                  pl.BlockSpec((1, D_MODEL), lambda i: (0, 0))],
        out_specs=pl.BlockSpec((tm, D_MODEL), lambda i: (i, 0)),
        compiler_params=_cp("parallel"),
    )(h, g)


def _norm1_bwd(du_a, du_b, h, g, dy):
    t = h.shape[0]
    tm = _row_tile(t)

    def body(a_ref, b_ref, h_ref, g_ref, dy_ref, dh_ref, dg_ref):
        i = pl.program_id(0)
        x = h_ref[...]
        du = a_ref[...] + b_ref[...]
        r = lax.rsqrt(jnp.mean(x * x, axis=-1, keepdims=True) + EPS)
        gdu = du * g_ref[...]
        dh_ref[...] = dy_ref[...] + r * (gdu - x * (r * r) * jnp.mean(gdu * x, axis=-1, keepdims=True))
        part = jnp.sum(du * x * r, axis=0, keepdims=True)

        @pl.when(i == 0)
        def _():
            dg_ref[...] = part

        @pl.when(i > 0)
        def _():
            dg_ref[...] += part

    row = pl.BlockSpec((tm, D_MODEL), lambda i: (i, 0))
    vec = pl.BlockSpec((1, D_MODEL), lambda i: (0, 0))
    return pl.pallas_call(
        body, name="norm1_bwd",
        out_shape=(jax.ShapeDtypeStruct((t, D_MODEL), F32), jax.ShapeDtypeStruct((1, D_MODEL), F32)),
        grid=(t // tm,),
        in_specs=[row, row, row, vec, row],
        out_specs=(row, vec),
        compiler_params=_cp("arbitrary"),
    )(du_a, du_b, h, g, dy)


def _small_fwd(small, bias_row):
    t = small.shape[0]

    def body(s_ref, b_ref, o_ref, carry_ref):
        c = pl.program_id(0)

        @pl.when(c == 0)
        def _():
            carry_ref[...] = jnp.zeros_like(carry_ref)

        x = s_ref[...] + b_ref[...]
        r0 = _iota((CHUNK, CHUNK), 0)
        r1 = _iota((CHUNK, CHUNK), 1)
        valid = (c * CHUNK + r0) >= PADF
        tail = _softplus_tail(x)
        dt = jnp.where(valid & (r1 < H_SSD), jnp.maximum(x, 0.0) + tail, 0.0)
        lf = jnp.where(valid & (r1 >= H_SSD) & (r1 < H_SSD + H_ATT), jnp.minimum(x, 0.0) - tail, 0.0)
        tri = (r0 >= r1).astype(F32)
        cs = _dot_exact(tri, lf) + carry_ref[...]
        carry_ref[...] = cs[CHUNK - 1:CHUNK, :]
        o_ref[...] = dt + cs

    return pl.pallas_call(
        body, name="small_fwd",
        out_shape=jax.ShapeDtypeStruct((t, N_SMALL), F32),
        grid=(t // CHUNK,),
        in_specs=[pl.BlockSpec((CHUNK, N_SMALL), lambda c: (c, 0)),
                  pl.BlockSpec((1, N_SMALL), lambda c: (0, 0))],
        out_specs=pl.BlockSpec((CHUNK, N_SMALL), lambda c: (c, 0)),
        scratch_shapes=[pltpu.VMEM((1, N_SMALL), F32)],
        compiler_params=_cp("arbitrary"),
    )(small, bias_row)


def _small_bwd(dsm, small, bias_row):
    t = small.shape[0]
    nc = t // CHUNK

    def body(d_ref, s_ref, b_ref, o_ref, db_ref, carry_ref):
        step = pl.program_id(0)
        c = nc - 1 - step

        @pl.when(step == 0)
        def _():
            carry_ref[...] = jnp.zeros_like(carry_ref)
            db_ref[...] = jnp.zeros_like(db_ref)

        x = s_ref[...] + b_ref[...]
        d = d_ref[...]
        r0 = _iota((CHUNK, CHUNK), 0)
        r1 = _iota((CHUNK, CHUNK), 1)
        valid = (c * CHUNK + r0) >= PADF
        is_dt = r1 < H_SSD
        is_f = (r1 >= H_SSD) & (r1 < H_SSD + H_ATT)
        triu = (r1 >= r0).astype(F32)
        dc = jnp.where(is_f, d, 0.0)
        dlf = _dot_exact(triu, dc) + carry_ref[...]
        carry_ref[...] = dlf[0:1, :]
        sg = _sigmoid(x)
        out = jnp.where(valid & is_dt, d * sg, 0.0) + jnp.where(valid & is_f, dlf * (1.0 - sg), 0.0)
        o_ref[...] = out.astype(BF16)
        db_ref[...] += jnp.sum(out, axis=0, keepdims=True)

    blk = pl.BlockSpec((CHUNK, N_SMALL), lambda s: (nc - 1 - s, 0))
    vec = pl.BlockSpec((1, N_SMALL), lambda s: (0, 0))
    return pl.pallas_call(
        body, name="small_bwd",
        out_shape=(jax.ShapeDtypeStruct((t, N_SMALL), BF16), jax.ShapeDtypeStruct((1, N_SMALL), F32)),
        grid=(nc,),
        in_specs=[blk, blk, vec],
        out_specs=(blk, vec),
        scratch_shapes=[pltpu.VMEM((1, N_SMALL), F32)],
        compiler_params=_cp("arbitrary"),
    )(dsm, small, bias_row)


_CONV_TC = 1024
_XBC_BLK = C_XBC // _CONV_TC


def _shift_down(cur, prev8, j):
    rc = pltpu.roll(cur, j, 0)
    rid = _iota(prev8.shape, 0)
    top = jnp.where(rid < j, pltpu.roll(prev8, j, 0), rc[0:8, :])
    return jnp.concatenate([top, rc[8:, :]], axis=0)


def _shift_up(cur, next8, j):
    n = cur.shape[0]
    ru = pltpu.roll(cur, n - j, 0)
    rid = _iota(next8.shape, 0)
    bot = jnp.where(rid >= 8 - j, pltpu.roll(next8, 8 - j, 0), ru[n - 8:, :])
    return jnp.concatenate([ru[:n - 8, :], bot], axis=0)


def _conv_pre(x_ref, p_ref, w_ref, b_ref, i):
    cur = x_ref[...]
    prev = jnp.where(i > 0, p_ref[...], 0.0)
    w = w_ref[...]
    taps = [cur] + [_shift_down(cur, prev, j) for j in (1, 2, 3)]
    acc = b_ref[...] + taps[0] * w[3:4, :]
    for j in (1, 2, 3):
        acc = acc + taps[j] * w[3 - j:4 - j, :]
    return acc, taps


def _conv_fwd(proj, conv_w, conv_b):
    t = proj.shape[0]
    tr = _row_tile(t)

    def body(x_ref, p_ref, w_ref, b_ref, o_ref):
        i = pl.program_id(0)
        acc, _ = _conv_pre(x_ref, p_ref, w_ref, b_ref, i)
        valid = (i * tr + _iota(acc.shape, 0)) >= PADF
        o_ref[...] = jnp.where(valid, acc * _sigmoid(acc), 0.0)

    return pl.pallas_call(
        body, name="conv_fwd",
        out_shape=jax.ShapeDtypeStruct((t, CONV_DIM), F32),
        grid=(t // tr, CONV_DIM // _CONV_TC),
        in_specs=[pl.BlockSpec((tr, _CONV_TC), lambda i, j: (i, _XBC_BLK + j)),
                  pl.BlockSpec((8, _CONV_TC), lambda i, j: (jnp.maximum(i * (tr // 8) - 1, 0), _XBC_BLK + j)),
                  pl.BlockSpec((CONV_K, _CONV_TC), lambda i, j: (0, j)),
                  pl.BlockSpec((1, _CONV_TC), lambda i, j: (0, j))],
        out_specs=pl.BlockSpec((tr, _CONV_TC), lambda i, j: (i, j)),
        compiler_params=_cp("parallel", "parallel"),
    )(proj, proj, conv_w, conv_b)


def _conv_bwd_act(dxbc, proj, conv_w, conv_b):
    t = proj.shape[0]
    tr = _row_tile(t)

    def body(d_ref, x_ref, p_ref, w_ref, b_ref, da_ref, dw_ref, db_ref):
        i = pl.program_id(1)
        acc, taps = _conv_pre(x_ref, p_ref, w_ref, b_ref, i)
        valid = (i * tr + _iota(acc.shape, 0)) >= PADF
        sg = _sigmoid(acc)
        da = jnp.where(valid, d_ref[...] * sg * (1.0 + acc * (1.0 - sg)), 0.0)
        da_ref[...] = da
        dw = jnp.concatenate([jnp.sum(da * taps[3 - k], axis=0, keepdims=True) for k in range(CONV_K)], axis=0)
        db = jnp.sum(da, axis=0, keepdims=True)

        @pl.when(i == 0)
        def _():
            dw_ref[...] = dw
            db_ref[...] = db

        @pl.when(i > 0)
        def _():
            dw_ref[...] += dw
            db_ref[...] += db

    return pl.pallas_call(
        body, name="conv_bwd_act",
        out_shape=(jax.ShapeDtypeStruct((t, CONV_DIM), F32),
                   jax.ShapeDtypeStruct((CONV_K, CONV_DIM), F32),
                   jax.ShapeDtypeStruct((1, CONV_DIM), F32)),
        grid=(CONV_DIM // _CONV_TC, t // tr),
        in_specs=[pl.BlockSpec((tr, _CONV_TC), lambda j, i: (i, j)),
                  pl.BlockSpec((tr, _CONV_TC), lambda j, i: (i, _XBC_BLK + j)),
                  pl.BlockSpec((8, _CONV_TC), lambda j, i: (jnp.maximum(i * (tr // 8) - 1, 0), _XBC_BLK + j)),
                  pl.BlockSpec((CONV_K, _CONV_TC), lambda j, i: (0, j)),
                  pl.BlockSpec((1, _CONV_TC), lambda j, i: (0, j))],
        out_specs=(pl.BlockSpec((tr, _CONV_TC), lambda j, i: (i, j)),
                   pl.BlockSpec((CONV_K, _CONV_TC), lambda j, i: (0, j)),
                   pl.BlockSpec((1, _CONV_TC), lambda j, i: (0, j))),
        compiler_params=_cp("parallel", "arbitrary"),
    )(dxbc, proj, proj, conv_w, conv_b)


def _conv_bwd_in(da, conv_w):
    t = da.shape[0]
    tr = _row_tile(t)
    last8 = t // 8 - 1

    def body(d_ref, n_ref, w_ref, o_ref):
        i = pl.program_id(0)
        cur = d_ref[...]
        nxt = jnp.where(i < pl.num_programs(0) - 1, n_ref[...], 0.0)
        w = w_ref[...]
        acc = cur * w[3:4, :]
        for j in (1, 2, 3):
            acc = acc + _shift_up(cur, nxt, j) * w[3 - j:4 - j, :]
        o_ref[...] = acc.astype(BF16)

    return pl.pallas_call(
        body, name="conv_bwd_in",
        out_shape=jax.ShapeDtypeStruct((t, CONV_DIM), BF16),
        grid=(t // tr, CONV_DIM // _CONV_TC),
        in_specs=[pl.BlockSpec((tr, _CONV_TC), lambda i, j: (i, j)),
                  pl.BlockSpec((8, _CONV_TC), lambda i, j: (jnp.minimum((i + 1) * (tr // 8), last8), j)),
                  pl.BlockSpec((CONV_K, _CONV_TC), lambda i, j: (0, j))],
        out_specs=pl.BlockSpec((tr, _CONV_TC), lambda i, j: (i, j)),
        compiler_params=_cp("parallel", "parallel"),
    )(da, da, conv_w)


_GW = D_SSD // G_SSD


def _ssd_prelude(dt_ref, a_ref, e_scr, es_scr, dte_scr):
    r0 = _iota((CHUNK, CHUNK), 0)
    r1 = _iota((CHUNK, CHUNK), 1)
    dt = jnp.where(r1 < H_SSD, dt_ref[...], 0.0)
    adt = dt * a_ref[...]
    acs = _dot_exact((r0 >= r1).astype(F32), adt)
    acs_t = acs.T
    alast = acs[CHUNK - 1:CHUNK, :]
    exp_a = jnp.exp(acs)
    dec_s = jnp.exp(alast - acs)
    lo = r1 < 64
    for j in range(H_SSD // 2):
        sl = slice(CHUNK * j, CHUNK * (j + 1))
        e_scr[:, sl] = jnp.where(lo, exp_a[:, 2 * j:2 * j + 1], exp_a[:, 2 * j + 1:2 * j + 2])
        es_scr[:, sl] = jnp.where(lo, dec_s[:, 2 * j:2 * j + 1], dec_s[:, 2 * j + 1:2 * j + 2])
        dte_scr[:, sl] = jnp.where(lo, dt[:, 2 * j:2 * j + 1], dt[:, 2 * j + 1:2 * j + 2])
    return dt, acs, acs_t, r0, r1, lo


def _chunk_decay_rows(acs_t, g):
    cd_t = jnp.exp(acs_t[:, CHUNK - 1:CHUNK])
    return jnp.concatenate(
        [jnp.broadcast_to(cd_t[8 * g + hh:8 * g + hh + 1, :], (64, N_STATE)) for hh in range(8)], axis=0)


def _ssd_fwd(xbc, dtlf, a_row, dsk_row):
    t = xbc.shape[0]
    nc = t // CHUNK

    def body(xs_ref, b_ref, c_ref, dt_ref, a_ref, dsk_ref, y_ref, hin_ref, h_scr, e_scr, es_scr, dte_scr):
        c = pl.program_id(0)

        @pl.when(c == 0)
        def _():
            h_scr[...] = jnp.zeros_like(h_scr)

        dt, acs, acs_t, r0, r1, lo = _ssd_prelude(dt_ref, a_ref, e_scr, es_scr, dte_scr)
        causal = r0 >= r1
        for g in range(G_SSD):
            gs = slice(_GW * g, _GW * (g + 1))
            bg = b_ref[:, N_STATE * g:N_STATE * (g + 1)].astype(BF16)
            cg = c_ref[:, N_STATE * g:N_STATE * (g + 1)].astype(BF16)
            cb = _dot(cg, bg, _NT)
            hg = h_scr[gs, :]
            hin_ref[0, gs, :] = hg
            xg = xs_ref[:, gs] * dte_scr[:, gs]
            yoff = _dot(cg, hg.astype(BF16), _NT) * e_scr[:, gs]
            st = _dot((xg * es_scr[:, gs]).astype(BF16), bg, _TN)
            h_scr[gs, :] = hg * _chunk_decay_rows(acs_t, g) + st
            for jj in range(4):
                j = 4 * g + jj
                sl = slice(CHUNK * j, CHUNK * (j + 1))
                xp = xg[:, CHUNK * jj:CHUNK * (jj + 1)]
                acc = yoff[:, CHUNK * jj:CHUNK * (jj + 1)] + dsk_ref[:, sl] * xs_ref[:, sl]
                for hh in range(2):
                    h = 2 * j + hh
                    seg = acs[:, h:h + 1] - acs_t[h:h + 1, :]
                    lm = jnp.exp(jnp.where(causal, seg, -1e30))
                    m = (cb * lm).astype(BF16)
                    xh = jnp.where(lo if hh == 0 else ~lo, xp, 0.0).astype(BF16)
                    acc = acc + _dot(m, xh)
                y_ref[:, sl] = acc

    return pl.pallas_call(
        body, name="ssd_fwd",
        out_shape=(jax.ShapeDtypeStruct((t, D_SSD), F32), jax.ShapeDtypeStruct((nc, D_SSD, N_STATE), F32)),
        grid=(nc,),
        in_specs=[pl.BlockSpec((CHUNK, D_SSD), lambda c: (c, 0)),
                  pl.BlockSpec((CHUNK, _GW), lambda c: (c, 4)),
                  pl.BlockSpec((CHUNK, _GW), lambda c: (c, 5)),
                  pl.BlockSpec((CHUNK, N_SMALL), lambda c: (c, 0)),
                  pl.BlockSpec((1, N_SMALL), lambda c: (0, 0)),
                  pl.BlockSpec((1, D_SSD), lambda c: (0, 0))],
        out_specs=(pl.BlockSpec((CHUNK, D_SSD), lambda c: (c, 0)),
                   pl.BlockSpec((1, D_SSD, N_STATE), lambda c: (c, 0, 0))),
        scratch_shapes=[pltpu.VMEM((D_SSD, N_STATE), F32)] + [pltpu.VMEM((CHUNK, D_SSD), F32)] * 3,
        compiler_params=_cp("arbitrary"),
    )(xbc, xbc, xbc, dtlf, a_row, dsk_row)


def _ssd_bwd(xbc, dtlf, a_row, dsk_row, hin, dy):
    t = xbc.shape[0]
    nc = t // CHUNK

    def body(xs_ref, b_ref, c_ref, dt_ref, a_ref, dsk_ref, hin_ref, dy_ref,
             dxbc_ref, ddt_ref, da_ref, ddsk_ref, dh_scr, e_scr, es_scr, dte_scr, dx_scr, whi_scr, wlo_scr):
        step = pl.program_id(0)

        @pl.when(step == 0)
        def _():
            dh_scr[...] = jnp.zeros_like(dh_scr)
            da_ref[...] = jnp.zeros_like(da_ref)
            ddsk_ref[...] = jnp.zeros_like(ddsk_ref)

        dt, acs, acs_t, r0, r1, lo = _ssd_prelude(dt_ref, a_ref, e_scr, es_scr, dte_scr)
        causal = r0 >= r1
        lane_row = _iota((1, CHUNK), 1)
        dacs = jnp.zeros((CHUNK, CHUNK), F32)
        dacs_t = jnp.zeros((CHUNK, CHUNK), F32)
        dalast = jnp.zeros((1, CHUNK), F32)
        ddt_dir = jnp.zeros((CHUNK, CHUNK), F32)
        ddsk_ref[...] += jnp.sum(dy_ref[...] * xs_ref[...], axis=0, keepdims=True)

        def head_sums(z, pick):
            hi = z.astype(BF16)
            return _dot(hi, pick) + _dot((z - hi.astype(F32)).astype(BF16), pick)

        for g in range(G_SSD):
            gs = slice(_GW * g, _GW * (g + 1))
            pick = (jnp.right_shift(_iota((_GW, CHUNK), 0), 6) + 8 * g == _iota((_GW, CHUNK), 1)).astype(BF16)
            bg = b_ref[:, N_STATE * g:N_STATE * (g + 1)].astype(BF16)
            cg = c_ref[:, N_STATE * g:N_STATE * (g + 1)].astype(BF16)
            cb = _dot(cg, bg, _NT)
            hg = hin_ref[0, gs, :]
            hgb = hg.astype(BF16)
            dhn = dh_scr[gs, :]
            dhnb = dhn.astype(BF16)
            esg = es_scr[:, gs]
            dyg = dy_ref[:, gs]
            xsg = xs_ref[:, gs]
            xg = xsg * dte_scr[:, gs]
            dyeb = (dyg * e_scr[:, gs]).astype(BF16)
            dc = _dot(dyeb, hgb)
            dh_y = _dot(dyeb, cg, _TN)
            dxs = _dot(bg, dhnb, _NT) * esg
            db = _dot((xg * esg).astype(BF16), dhnb)
            cd = _chunk_decay_rows(acs_t, g)
            dh_scr[gs, :] = dhn * cd + dh_y
            end_state = head_sums(jnp.broadcast_to(jnp.sum(xg * dxs, axis=0, keepdims=True), (8, _GW)), pick)[0:1, :]
            carried = dhn * hg * cd
            per_head = jnp.concatenate([jnp.sum(carried[64 * hh:64 * hh + 64, :], axis=0, keepdims=True)
                                        for hh in range(8)], axis=0)
            per_head = jnp.sum(per_head, axis=1, keepdims=True)
            for hh in range(8):
                end_state = end_state + jnp.where(lane_row == 8 * g + hh, per_head[hh:hh + 1, :], 0.0)
            dalast = dalast + end_state
            dcb = jnp.zeros((CHUNK, CHUNK), F32)
            for jj in range(4):
                j = 4 * g + jj
                sl = slice(CHUNK * j, CHUNK * (j + 1))
                ps = slice(CHUNK * jj, CHUNK * (jj + 1))
                xpb = xg[:, ps].astype(BF16)
                dyp = dyg[:, ps]
                dxp = dxs[:, ps]
                for hh in range(2):
                    h = 2 * j + hh
                    ws = slice(CHUNK * (2 * jj + hh), CHUNK * (2 * jj + hh + 1))
                    seg = acs[:, h:h + 1] - acs_t[h:h + 1, :]
                    lm = jnp.exp(jnp.where(causal, seg, -1e30))
                    mf = cb * lm
                    dyh = jnp.where(lo if hh == 0 else ~lo, dyp, 0.0).astype(BF16)
                    gm = _dot(dyh, xpb, _NT)
                    dcb = dcb + gm * lm
                    w = gm * mf
                    whi = w.astype(BF16)
                    whi_scr[:, ws] = whi
                    wlo_scr[:, ws] = (w - whi.astype(F32)).astype(BF16)
                    dacs_t = dacs_t - jnp.where(r0 == h, jnp.sum(w, axis=0, keepdims=True), 0.0)
                    dxp = dxp + _dot(mf.astype(BF16), dyh, _TN)
                dx_scr[:, sl] = dxp
            dxg = dx_scr[:, gs]
            pick_w = (jnp.right_shift(_iota((8 * CHUNK, CHUNK), 0), 7) + 8 * g == _iota((8 * CHUNK, CHUNK), 1)).astype(BF16)
            ch = _dot(cg, hgb, _NT)
            dacs = (dacs + _dot(whi_scr[...], pick_w) + _dot(wlo_scr[...], pick_w)
                    + head_sums(dyg * e_scr[:, gs] * ch - xg * dxs, pick))
            ddt_dir = ddt_dir + head_sums(dxg * xsg, pick)
            dcbb = dcb.astype(BF16)
            dxbc_ref[:, D_SSD + N_STATE * g:D_SSD + N_STATE * (g + 1)] = db + _dot(dcbb, cg, _TN)
            dxbc_ref[:, D_SSD + _GW + N_STATE * g:D_SSD + _GW + N_STATE * (g + 1)] = dc + _dot(dcbb, bg)
        dxbc_ref[:, 0:D_SSD] = dx_scr[...] * dte_scr[...] + dsk_ref[...] * dy_ref[...]
        dacs = dacs + dacs_t.T + jnp.where(r0 == CHUNK - 1, dalast, 0.0)
        dadt = _dot_exact((r1 >= r0).astype(F32), dacs)
        ddt_ref[...] = dadt * a_ref[...] + ddt_dir
        da_ref[...] += jnp.sum(dadt * dt, axis=0, keepdims=True)

    rev = lambda s: (nc - 1 - s, 0)
    return pl.pallas_call(
        body, name="ssd_bwd",
        out_shape=(jax.ShapeDtypeStruct((t, CONV_DIM), F32), jax.ShapeDtypeStruct((t, N_SMALL), F32),
                   jax.ShapeDtypeStruct((1, N_SMALL), F32), jax.ShapeDtypeStruct((1, D_SSD), F32)),
        grid=(nc,),
        in_specs=[pl.BlockSpec((CHUNK, D_SSD), rev),
                  pl.BlockSpec((CHUNK, _GW), lambda s: (nc - 1 - s, 4)),
                  pl.BlockSpec((CHUNK, _GW), lambda s: (nc - 1 - s, 5)),
                  pl.BlockSpec((CHUNK, N_SMALL), rev),
                  pl.BlockSpec((1, N_SMALL), lambda s: (0, 0)),
                  pl.BlockSpec((1, D_SSD), lambda s: (0, 0)),
                  pl.BlockSpec((1, D_SSD, N_STATE), lambda s: (nc - 1 - s, 0, 0)),
                  pl.BlockSpec((CHUNK, D_SSD), rev)],
        out_specs=(pl.BlockSpec((CHUNK, CONV_DIM), rev),
                   pl.BlockSpec((CHUNK, N_SMALL), rev),
                   pl.BlockSpec((1, N_SMALL), lambda s: (0, 0)),
                   pl.BlockSpec((1, D_SSD), lambda s: (0, 0))),
        scratch_shapes=([pltpu.VMEM((D_SSD, N_STATE), F32)] + [pltpu.VMEM((CHUNK, D_SSD), F32)] * 4
                        + [pltpu.VMEM((CHUNK, 8 * CHUNK), BF16)] * 2),
        compiler_params=_cp("arbitrary"),
    )(xbc, xbc, xbc, dtlf, a_row, dsk_row, hin, dy)


_NPAIR = H_ATT // 2
_QB, _KB, _VB = C_Q // 128, C_K // 128, C_V // 128
_SCALE = 1.0 / math.sqrt(64.0)


def _attn_blocks(t):
    return _tile(t, (1408, 384, 256, 128)), _tile(t, (384, 128))


def _split3(c):
    hi = c.astype(BF16).astype(F32)
    rest = c - hi
    mid = rest.astype(BF16).astype(F32)
    return hi, mid, rest - mid


def _head_lanes(lane, hh):
    return (lane < 64, 64) if hh == 0 else (lane >= 64, 0)


def _q_operand(q, cq, lane, hh):
    sel, first = _head_lanes(lane, hh)
    out = jnp.where(sel, q, 0.0)
    for n, col in enumerate(_split3(cq) + (1.0, 1.0, 1.0)):
        out = jnp.where(lane == first + n, col, out)
    return out.astype(BF16)


def _k_operand(k, ck, lane, hh):
    sel, first = _head_lanes(lane, hh)
    hi, mid, lo = _split3(ck)
    out = jnp.where(sel, k, 0.0)
    for n, col in enumerate((1.0, 1.0, 1.0, -hi, -mid, -lo)):
        out = jnp.where(lane == first + n, col, out)
    return out.astype(BF16)


def _needs_mask(i, kk, bq, bk):
    return kk * bk + bk - 1 > i * bq


_C_FILLER = 2.0 ** 30


def _attn_fwd(proj, c_col):
    t = proj.shape[0]
    bq, bk = _attn_blocks(t)
    nq, nk = t // bq, t // bk
    rs = 16

    def last_kv(i):
        return (i * bq + bq - 1) // bk

    def body(q_ref, k_ref, v_ref, cq_ref, ck_ref, o_ref, lse_ref, qs_scr, s_scr, p_scr, m_scr, acc_scr):
        i = pl.program_id(1)
        kk = pl.program_id(2)
        lane_q = _iota((bq, 128), 1)

        @pl.when(kk == 0)
        def _():
            m_scr[...] = jnp.full_like(m_scr, -1e30)
            acc_scr[...] = jnp.zeros_like(acc_scr)
            q = q_ref[...] * _SCALE
            cq = cq_ref[0]
            for hh in range(2):
                qs_scr[hh] = _q_operand(q, cq[:, hh:hh + 1], lane_q, hh)

        def step(masked):
            lane_k = _iota((bk, 128), 1)
            k = k_ref[...]
            v = v_ref[...]
            ck = ck_ref[0]
            ahead = _iota((rs, bq), 0) - _iota((rs, bq), 1)
            for hh in range(2):
                sel, first = _head_lanes(lane_k, hh)
                ks = _k_operand(k, ck[:, hh:hh + 1], lane_k, hh)
                vs = jnp.where(sel, v, jnp.where(lane_k == first, 1.0, 0.0)).astype(BF16)
                s_scr[hh] = _dot(ks, qs_scr[hh], _NT)

                def block_max(r, mx):
                    rows = pl.ds(pl.multiple_of(r * rs, rs), rs)
                    s = s_scr[hh, rows, :]
                    if masked:
                        s = jnp.where(ahead <= i * bq - kk * bk - r * rs, s, -1e30)
                        s_scr[hh, rows, :] = s
                    return jnp.maximum(mx, s)

                mx = lax.fori_loop(0, bk // rs, block_max, jnp.full((rs, bq), -1e30, F32), unroll=True)
                m_old = m_scr[hh]
                m_new = jnp.maximum(m_old, jnp.max(mx, axis=0, keepdims=True))
                m_scr[hh] = m_new

                def probs(r, carry):
                    rows = pl.ds(pl.multiple_of(r * rs, rs), rs)
                    p_scr[hh, rows, :] = jnp.exp(s_scr[hh, rows, :] - m_new).astype(BF16)
                    return carry

                lax.fori_loop(0, bk // rs, probs, 0, unroll=True)
                acc_scr[hh] = acc_scr[hh] * jnp.exp(m_old - m_new) + _dot(vs, p_scr[hh], _TN)

        active = kk <= last_kv(i)
        masked = _needs_mask(i, kk, bq, bk)

        @pl.when(active & masked)
        def _():
            step(True)

        @pl.when(active & jnp.logical_not(masked))
        def _():
            step(False)

        @pl.when(kk == nk - 1)
        def _():
            a = acc_scr[0]
            b = acc_scr[1]
            la = a[64:65, :]
            lb = b[0:1, :]
            o_ref[...] = jnp.where(lane_q < 64, (a / la).T, (b / lb).T)
            lse_ref[0] = jnp.concatenate([m_scr[0] + jnp.log(la), m_scr[1] + jnp.log(lb)], axis=0)

    kvi = lambda i, kk: jnp.minimum(kk, last_kv(i))
    kv = lambda off: pl.BlockSpec((bk, 128), lambda j, i, kk: (kvi(i, kk), off + j))
    return pl.pallas_call(
        body, name="attn_fwd",
        out_shape=(jax.ShapeDtypeStruct((t, D_ATT), F32), jax.ShapeDtypeStruct((_NPAIR, 2, t), F32)),
        grid=(_NPAIR, nq, nk),
        in_specs=[pl.BlockSpec((bq, 128), lambda j, i, kk: (i, _QB + j)),
                  kv(_KB), kv(_VB),
                  pl.BlockSpec((1, bq, 2), lambda j, i, kk: (j, i, 0)),
                  pl.BlockSpec((1, bk, 2), lambda j, i, kk: (j, kvi(i, kk), 0))],
        out_specs=(pl.BlockSpec((bq, 128), lambda j, i, kk: (i, j)),
                   pl.BlockSpec((1, 2, bq), lambda j, i, kk: (j, 0, i))),
        scratch_shapes=[pltpu.VMEM((2, bq, 128), BF16), pltpu.VMEM((2, bk, bq), F32), pltpu.VMEM((2, bk, bq), BF16),
                        pltpu.VMEM((2, 1, bq), F32), pltpu.VMEM((2, 128, bq), F32)],
        compiler_params=_cp("parallel", "parallel", "arbitrary"),
    )(proj, proj, proj, c_col, c_col)


def _attn_delta(do, o):
    t = do.shape[0]
    tm = _row_tile(t)

    def body(do_ref, o_ref, d_ref):
        pick = (jnp.right_shift(_iota((D_ATT, 128), 0), 6) == _iota((D_ATT, 128), 1)).astype(F32)
        d_ref[...] = _dot_exact(do_ref[...] * o_ref[...], pick)

    row = pl.BlockSpec((tm, D_ATT), lambda i: (i, 0))
    return pl.pallas_call(
        body, name="attn_delta",
        out_shape=jax.ShapeDtypeStruct((t, 128), F32),
        grid=(t // tm,), in_specs=[row, row], out_specs=pl.BlockSpec((tm, 128), lambda i: (i, 0)),
        compiler_params=_cp("parallel"),
    )(do, o)


def _attn_bwd(proj, c_col, lse_row, dl_row, do):
    t = proj.shape[0]
    bq, bk = _attn_blocks(t)
    nq, nk = t // bq, t // bk
    rs = 16

    def first_q(kk):
        return (kk * bk) // bq

    def body(q_ref, k_ref, v_ref, cq_ref, ck_ref, lse_ref, dl_ref, do_ref,
             dq_ref, dk_ref, dv_ref, dck_ref, dcq_ref,
             qs_scr, doh_scr, ks_scr, s_scr, dp_scr, p_scr, ds_scr, dq_scr, dk_scr, dv_scr):
        kk = pl.program_id(1)
        i = pl.program_id(2)
        lane_q = _iota((bq, 128), 1)
        lane_k = _iota((bk, 128), 1)
        qrows = pl.ds(pl.multiple_of(i * bq, 128), bq)

        @pl.when(kk == 0)
        def _():
            q = q_ref[...] * _SCALE
            cq = cq_ref[0]
            do_ = do_ref[...]
            for hh in range(2):
                qs_scr[hh, qrows, :] = _q_operand(q, cq[:, hh:hh + 1], lane_q, hh)
                doh_scr[hh, qrows, :] = jnp.where(_head_lanes(lane_q, hh)[0], do_, 0.0).astype(BF16)
                dq_scr[hh, qrows, :] = jnp.zeros((bq, 128), F32)

        @pl.when(i == 0)
        def _():
            dk_scr[...] = jnp.zeros_like(dk_scr)
            dv_scr[...] = jnp.zeros_like(dv_scr)
            k = k_ref[...]
            ck = ck_ref[0]
            for hh in range(2):
                ks_scr[hh] = _k_operand(k, ck[:, hh:hh + 1], lane_k, hh)

        def step(masked):
            v16 = v_ref[...].astype(BF16)
            lse = lse_ref[0]
            dl = dl_ref[0]
            ahead = _iota((rs, bq), 0) - _iota((rs, bq), 1)
            for hh in range(2):
                qs = qs_scr[hh, qrows, :]
                doh = doh_scr[hh, qrows, :]
                s_scr[hh] = _dot(ks_scr[hh], qs, _NT)
                dp_scr[hh] = _dot(v16, doh, _NT)

                def strip(r, carry):
                    rows = pl.ds(pl.multiple_of(r * rs, rs), rs)
                    p = jnp.exp(s_scr[hh, rows, :] - lse[hh:hh + 1, :])
                    if masked:
                        p = jnp.where(ahead <= i * bq - kk * bk - r * rs, p, 0.0)
                    p_scr[hh, rows, :] = p.astype(BF16)
                    ds_scr[hh, rows, :] = (p * (dp_scr[hh, rows, :] - dl[hh:hh + 1, :])).astype(BF16)
                    return carry

                lax.fori_loop(0, bk // rs, strip, 0, unroll=True)
                dv_scr[...] += _dot(p_scr[hh], doh)
                dk_scr[hh] += _dot(ds_scr[hh], qs)
                dq_scr[hh, qrows, :] += _dot(ds_scr[hh], ks_scr[hh], _TN)

        active = i >= first_q(kk)
        masked = _needs_mask(i, kk, bq, bk)

        @pl.when(active & masked)
        def _():
            step(True)

        @pl.when(active & jnp.logical_not(masked))
        def _():
            step(False)

        @pl.when(i == nq - 1)
        def _():
            dka = dk_scr[0]
            dkb = dk_scr[1]
            dk_ref[...] = jnp.where(lane_k < 64, dka, dkb).astype(BF16)
            dv_ref[...] = dv_scr[...].astype(BF16)
            dck_ref[0] = -jnp.where(_iota((bk, 2), 1) == 0, dka[:, 67:68], dkb[:, 3:4])

        @pl.when((kk == nk - 1) & (i == nq - 1))
        def _():
            lane_t = _iota((t, 128), 1)
            dqa = dq_scr[0]
            dqb = dq_scr[1]
            dq_ref[...] = (jnp.where(lane_t < 64, dqa, dqb) * _SCALE).astype(BF16)
            dcq_ref[0] = jnp.where(_iota((t, 2), 1) == 0, dqa[:, 64:65], dqb[:, 0:1])

    qi = lambda kk, i: jnp.where(kk == 0, i, nq - 1)
    qspec = lambda off: pl.BlockSpec((bq, 128), lambda j, kk, i: (qi(kk, i), off + j))
    kspec = lambda off: pl.BlockSpec((bk, 128), lambda j, kk, i: (kk, off + j))
    rowspec = pl.BlockSpec((1, 2, bq), lambda j, kk, i: (j, 0, jnp.maximum(i, first_q(kk))))
    return pl.pallas_call(
        body, name="attn_bwd",
        out_shape=(jax.ShapeDtypeStruct((t, D_ATT), BF16), jax.ShapeDtypeStruct((t, D_ATT), BF16),
                   jax.ShapeDtypeStruct((t, D_ATT), BF16), jax.ShapeDtypeStruct((_NPAIR, t, 2), F32),
                   jax.ShapeDtypeStruct((_NPAIR, t, 2), F32)),
        grid=(_NPAIR, nk, nq),
        in_specs=[qspec(_QB), kspec(_KB), kspec(_VB),
                  pl.BlockSpec((1, bq, 2), lambda j, kk, i: (j, qi(kk, i), 0)),
                  pl.BlockSpec((1, bk, 2), lambda j, kk, i: (j, kk, 0)),
                  rowspec, rowspec, qspec(0)],
        out_specs=(pl.BlockSpec((t, 128), lambda j, kk, i: (0, j)),
                   pl.BlockSpec((bk, 128), lambda j, kk, i: (kk, j)),
                   pl.BlockSpec((bk, 128), lambda j, kk, i: (kk, j)),
                   pl.BlockSpec((1, bk, 2), lambda j, kk, i: (j, kk, 0)),
                   pl.BlockSpec((1, t, 2), lambda j, kk, i: (j, 0, 0))),
        scratch_shapes=[pltpu.VMEM((2, t, 128), BF16), pltpu.VMEM((2, t, 128), BF16), pltpu.VMEM((2, bk, 128), BF16),
                        pltpu.VMEM((2, bk, bq), F32), pltpu.VMEM((2, bk, bq), F32),
                        pltpu.VMEM((2, bk, bq), BF16), pltpu.VMEM((2, bk, bq), BF16),
                        pltpu.VMEM((2, t, 128), F32), pltpu.VMEM((2, bk, 128), F32), pltpu.VMEM((bk, 128), F32)],
        compiler_params=_cp("parallel", "arbitrary", "arbitrary"),
    )(proj, proj, proj, c_col, c_col, lse_row, dl_row, do)


def _premerge_fwd(y, o, proj, gamma):
    t = y.shape[0]
    tm = _row_tile_wide(t)

    def body(y_ref, z_ref, o_ref, za_ref, g_ref, ys_ref, ya_ref):
        z = z_ref[...]
        u = y_ref[...] * (z * _sigmoid(z))
        for g in range(G_SSD):
            gs = slice(_GW * g, _GW * (g + 1))
            ug = u[:, gs]
            r = lax.rsqrt(jnp.mean(ug * ug, axis=-1, keepdims=True) + EPS)
            ys_ref[:, gs] = (ug * r * g_ref[:, gs]).astype(BF16)
        za = za_ref[...]
        ya_ref[...] = (o_ref[...] * (za * _sigmoid(za))).astype(BF16)

    return pl.pallas_call(
        body, name="premerge_fwd",
        out_shape=(jax.ShapeDtypeStruct((t, D_SSD), BF16), jax.ShapeDtypeStruct((t, D_ATT), BF16)),
        grid=(t // tm,),
        in_specs=[pl.BlockSpec((tm, D_SSD), lambda i: (i, 0)),
                  pl.BlockSpec((tm, D_SSD), lambda i: (i, C_Z // D_SSD)),
                  pl.BlockSpec((tm, D_ATT), lambda i: (i, 0)),
                  pl.BlockSpec((tm, D_ATT), lambda i: (i, C_ZA // D_ATT)),
                  pl.BlockSpec((1, D_SSD), lambda i: (0, 0))],
        out_specs=(pl.BlockSpec((tm, D_SSD), lambda i: (i, 0)), pl.BlockSpec((tm, D_ATT), lambda i: (i, 0))),
        compiler_params=_cp("parallel"),
    )(y, proj, o, proj, gamma)


def _premerge_bwd(dys, dya, y, o, proj, gamma):
    t = y.shape[0]
    tm = _row_tile_wide(t)

    def body(dys_ref, dya_ref, y_ref, z_ref, o_ref, za_ref, g_ref, dy_ref, dz_ref, do_ref, dza_ref, dg_ref):
        i = pl.program_id(0)
        z = z_ref[...]
        sz = _sigmoid(z)
        silu = z * sz
        dsilu = sz * (1.0 + z * (1.0 - sz))
        yv = y_ref[...]
        u = yv * silu
        parts = []
        for g in range(G_SSD):
            gs = slice(_GW * g, _GW * (g + 1))
            ug = u[:, gs]
            r = lax.rsqrt(jnp.mean(ug * ug, axis=-1, keepdims=True) + EPS)
            n = ug * r
            dout = dys_ref[:, gs]
            dn = dout * g_ref[:, gs]
            du = r * (dn - n * jnp.mean(dn * n, axis=-1, keepdims=True))
            dy_ref[:, gs] = du * silu[:, gs]
            dz_ref[:, gs] = (du * yv[:, gs] * dsilu[:, gs]).astype(BF16)
            parts.append(jnp.sum(dout * n, axis=0, keepdims=True))
        dg = jnp.concatenate(parts, axis=1)
        za = za_ref[...]
        sa = _sigmoid(za)
        dya_ = dya_ref[...]
        do_ref[...] = dya_ * (za * sa)
        dza_ref[...] = (dya_ * o_ref[...] * (sa * (1.0 + za * (1.0 - sa)))).astype(BF16)

        @pl.when(i == 0)
        def _():
            dg_ref[...] = dg

        @pl.when(i > 0)
        def _():
            dg_ref[...] += dg

    ssd = pl.BlockSpec((tm, D_SSD), lambda i: (i, 0))
    att = pl.BlockSpec((tm, D_ATT), lambda i: (i, 0))
    vec = pl.BlockSpec((1, D_SSD), lambda i: (0, 0))
    return pl.pallas_call(
        body, name="premerge_bwd",
        out_shape=(jax.ShapeDtypeStruct((t, D_SSD), F32), jax.ShapeDtypeStruct((t, D_SSD), BF16),
                   jax.ShapeDtypeStruct((t, D_ATT), F32), jax.ShapeDtypeStruct((t, D_ATT), BF16),
                   jax.ShapeDtypeStruct((1, D_SSD), F32)),
        grid=(t // tm,),
        in_specs=[ssd, att, ssd, pl.BlockSpec((tm, D_SSD), lambda i: (i, C_Z // D_SSD)), att,
                  pl.BlockSpec((tm, D_ATT), lambda i: (i, C_ZA // D_ATT)), vec],
        out_specs=(ssd, ssd, att, att, vec),
        compiler_params=_cp("arbitrary"),
    )(dys, dya, y, proj, o, proj, gamma)


_G_BLK = C_G // D_MODEL


def _merge_fwd(a, b, proj, gate_bias):
    t = a.shape[0]
    tm = _row_tile(t)

    def body(a_ref, b_ref, gs_ref, ga_ref, bias_ref, m_ref):
        g_ssd = _sigmoid(gs_ref[...] + bias_ref[:, 0:D_MODEL])
        g_att = _sigmoid(ga_ref[...] + bias_ref[:, D_MODEL:2 * D_MODEL])
        m_ref[...] = (g_ssd * a_ref[...] + g_att * b_ref[...]).astype(BF16)

    row = pl.BlockSpec((tm, D_MODEL), lambda i: (i, 0))
    return pl.pallas_call(
        body, name="merge_fwd",
        out_shape=jax.ShapeDtypeStruct((t, D_MODEL), BF16),
        grid=(t // tm,),
        in_specs=[row, row,
                  pl.BlockSpec((tm, D_MODEL), lambda i: (i, _G_BLK)),
                  pl.BlockSpec((tm, D_MODEL), lambda i: (i, _G_BLK + 1)),
                  pl.BlockSpec((1, 2 * D_MODEL), lambda i: (0, 0))],
        out_specs=row,
        compiler_params=_cp("parallel"),
    )(a, b, proj, proj, gate_bias)


def _merge_bwd(dm, a, b, proj, gate_bias):
    t = a.shape[0]
    tm = _row_tile(t)

    def body(dm_ref, a_ref, b_ref, gs_ref, ga_ref, bias_ref, da_ref, db_ref, dg_ref, dbias_ref):
        i = pl.program_id(0)
        dm_ = dm_ref[...]
        g_ssd = _sigmoid(gs_ref[...] + bias_ref[:, 0:D_MODEL])
        g_att = _sigmoid(ga_ref[...] + bias_ref[:, D_MODEL:2 * D_MODEL])
        da_ref[...] = (dm_ * g_ssd).astype(BF16)
        db_ref[...] = (dm_ * g_att).astype(BF16)
        dgs = dm_ * a_ref[...] * g_ssd * (1.0 - g_ssd)
        dga = dm_ * b_ref[...] * g_att * (1.0 - g_att)
        dg_ref[:, 0:D_MODEL] = dgs.astype(BF16)
        dg_ref[:, D_MODEL:2 * D_MODEL] = dga.astype(BF16)
        part = jnp.concatenate([jnp.sum(dgs, axis=0, keepdims=True), jnp.sum(dga, axis=0, keepdims=True)], axis=1)

        @pl.when(i == 0)
        def _():
            dbias_ref[...] = part

        @pl.when(i > 0)
        def _():
            dbias_ref[...] += part

    row = pl.BlockSpec((tm, D_MODEL), lambda i: (i, 0))
    wide = pl.BlockSpec((tm, 2 * D_MODEL), lambda i: (i, 0))
    vec = pl.BlockSpec((1, 2 * D_MODEL), lambda i: (0, 0))
    return pl.pallas_call(
        body, name="merge_bwd",
        out_shape=(jax.ShapeDtypeStruct((t, D_MODEL), BF16), jax.ShapeDtypeStruct((t, D_MODEL), BF16),
                   jax.ShapeDtypeStruct((t, 2 * D_MODEL), BF16), jax.ShapeDtypeStruct((1, 2 * D_MODEL), F32)),
        grid=(t // tm,),
        in_specs=[row, row, row,
                  pl.BlockSpec((tm, D_MODEL), lambda i: (i, _G_BLK)),
                  pl.BlockSpec((tm, D_MODEL), lambda i: (i, _G_BLK + 1)), vec],
        out_specs=(row, row, wide, vec),
        compiler_params=_cp("arbitrary"),
    )(dm, a, b, proj, proj, gate_bias)


def _post(o2, h, target, g):
    t = o2.shape[0]
    nc = t // CHUNK

    def body(o_ref, h_ref, t_ref, g_ref, dy_ref, do_ref, dg_ref, loss_ref):
        c = pl.program_id(0)
        x = o_ref[...]
        r = lax.rsqrt(jnp.mean(x * x, axis=-1, keepdims=True) + EPS)
        n = x * r
        y = h_ref[...] + n * g_ref[...]
        diff = jnp.where(c > 0, y - t_ref[...], 0.0)
        dy = diff * (1.0 / D_MODEL)
        dy_ref[...] = dy
        gdy = dy * g_ref[...]
        do_ref[...] = (r * (gdy - n * jnp.mean(gdy * n, axis=-1, keepdims=True))).astype(BF16)
        dg = jnp.sum(dy * n, axis=0, keepdims=True)
        lpart = 0.5 * jnp.sum(jnp.sum(diff * diff, axis=1, keepdims=True), axis=0, keepdims=True) * (1.0 / D_MODEL)
        sel = (_iota((8, 128), 0) == 0) & (_iota((8, 128), 1) == 0)

        @pl.when(c == 0)
        def _():
            dg_ref[...] = dg
            loss_ref[...] = jnp.zeros_like(loss_ref)

        @pl.when(c > 0)
        def _():
            dg_ref[...] += dg
            loss_ref[...] += jnp.where(sel, lpart, 0.0)

    row = pl.BlockSpec((CHUNK, D_MODEL), lambda c: (c, 0))
    vec = pl.BlockSpec((1, D_MODEL), lambda c: (0, 0))
    return pl.pallas_call(
        body, name="post",
        out_shape=(jax.ShapeDtypeStruct((t, D_MODEL), F32), jax.ShapeDtypeStruct((t, D_MODEL), BF16),
                   jax.ShapeDtypeStruct((1, D_MODEL), F32), jax.ShapeDtypeStruct((8, 128), F32)),
        grid=(nc,),
        in_specs=[row, row, pl.BlockSpec((CHUNK, D_MODEL), lambda c: (jnp.maximum(c - 1, 0), 0)), vec],
        out_specs=(row, row, vec, pl.BlockSpec((8, 128), lambda c: (0, 0))),
        compiler_params=_cp("arbitrary"),
    )(o2, h, target, g)


def _mm_tiles(t):
    return _tile(t, (704, 384, 128))


def _local_step(h, target, w_main, w_small, wps, wpa, wout, norm_pre, conv_w, conv_b, bias_row, a_row,
                dsk_row, ssd_norm, gate_bias, norm_post):
    t = h.shape[0]
    tm = _mm_tiles(t)
    u = _norm1_fwd(h, norm_pre)
    proj = _matmul(u, w_main, "nt", F32, "inproj", tm, 1024, D_MODEL)
    small = _matmul(u, w_small, "nt", F32, "inproj_small", tm, N_SMALL, D_MODEL)
    dtlf = _small_fwd(small, bias_row)
    xbc = _conv_fwd(proj, conv_w, conv_b)
    y, hin = _ssd_fwd(xbc, dtlf, a_row, dsk_row)
    c_tok = dtlf[:, H_SSD:H_SSD + H_ATT]
    c_tok = jnp.where(jnp.arange(t)[:, None] < PADF, _C_FILLER, c_tok)
    c_col = c_tok.reshape(t, _NPAIR, 2).transpose(1, 0, 2)
    o, lse = _attn_fwd(proj, c_col)
    ys, ya = _premerge_fwd(y, o, proj, ssd_norm)
    a = _matmul(ys, wps, "nn", F32, "proj_ssd", tm, D_MODEL, D_SSD)
    b = _matmul(ya, wpa, "nn", F32, "proj_att", tm, D_MODEL, D_ATT)
    merged = _merge_fwd(a, b, proj, gate_bias)
    o2 = _matmul(merged, wout, "nn", F32, "out_proj", tm, D_MODEL, D_MODEL)
    dy_out, do2, d_norm_post, loss_blk = _post(o2, h, target, norm_post)

    dm = _matmul(do2, wout, "nt", F32, "out_proj_dx", tm, D_MODEL, D_MODEL)
    d_wout = _matmul(merged, do2, "tn", F32, "out_proj_dw", D_MODEL, D_MODEL, tm)
    da, db, dgraw, d_gate_bias = _merge_bwd(dm, a, b, proj, gate_bias)
    dys = _matmul(da, wps, "nt", F32, "proj_ssd_dx", tm, D_SSD, D_MODEL)
    d_wps = _matmul(ys, da, "tn", F32, "proj_ssd_dw", D_SSD, D_MODEL, tm)
    dya = _matmul(db, wpa, "nt", F32, "proj_att_dx", tm, D_ATT, D_MODEL)
    d_wpa = _matmul(ya, db, "tn", F32, "proj_att_dw", D_ATT, D_MODEL, tm)
    dy, dz, do, dza, d_ssd_norm = _premerge_bwd(dys, dya, y, o, proj, ssd_norm)
    dl_row = _attn_delta(do, o)[:, 0:H_ATT].T.reshape(_NPAIR, 2, t)
    dq, dk, dv, dc_key, dc_qry = _attn_bwd(proj, c_col, lse, dl_row, do)
    dxbc, ddt, d_a, d_dsk = _ssd_bwd(xbc, dtlf, a_row, dsk_row, hin, dy)
    dact, d_conv_w, d_conv_b = _conv_bwd_act(dxbc, proj, conv_w, conv_b)
    dxbc_raw = _conv_bwd_in(dact, conv_w)
    dc_tok = jnp.transpose(dc_key + dc_qry, (1, 0, 2)).reshape(t, H_ATT)
    dsm = ddt + jnp.pad(dc_tok, ((0, 0), (H_SSD, N_SMALL - H_SSD - H_ATT)))
    dsmall, d_bias_row = _small_bwd(dsm, small, bias_row)
    dproj = jnp.concatenate([dz, dxbc_raw, dza, dq, dk, dv, dgraw], axis=1)
    du_a = _matmul(dproj, w_main, "nn", F32, "inproj_dx", tm, D_MODEL, 1024)
    du_b = _matmul(dsmall, w_small, "nn", F32, "inproj_small_dx", tm, D_MODEL, N_SMALL)
    d_w_main = _matmul(dproj, u, "tn", F32, "inproj_dw", 1024, D_MODEL, tm)
    d_w_small = _matmul(dsmall, u, "tn", F32, "inproj_small_dw", N_SMALL, D_MODEL, tm)
    dh, d_norm_pre = _norm1_bwd(du_a, du_b, h, norm_pre, dy_out)
    return dict(loss_blk=loss_blk, dh=dh, d_w_main=d_w_main, d_w_small=d_w_small, d_wps=d_wps, d_wpa=d_wpa,
                d_wout=d_wout, d_norm_pre=d_norm_pre, d_conv_w=d_conv_w, d_conv_b=d_conv_b,
                d_bias_row=d_bias_row, d_a=d_a, d_dsk=d_dsk, d_ssd_norm=d_ssd_norm,
                d_gate_bias=d_gate_bias, d_norm_post=d_norm_post)


def _to_aligned_rows(w):
    def cut(o):
        return w[o[0]:o[0] + o[1]]
    main = jnp.concatenate([cut(O_Z), cut(O_XBC), cut(O_ZA), cut(O_Q), cut(O_K), cut(O_V), cut(O_G)], axis=0)
    pad = jnp.zeros((N_SMALL - H_SSD - H_ATT, w.shape[1]), w.dtype)
    small = jnp.concatenate([cut(O_DT), cut(O_F), pad], axis=0)
    return main, small


def _from_aligned_rows(main, small):
    def cm(c0, n):
        return main[c0:c0 + n]
    return jnp.concatenate([cm(C_Z, 2048), cm(C_XBC, 3072), small[0:H_SSD], cm(C_ZA, 1024),
                            cm(C_Q, 1024), cm(C_K, 1024), cm(C_V, 1024), small[H_SSD:H_SSD + H_ATT],
                            cm(C_G, 2048)], axis=0)


_MESH = pl.DeviceIdType.MESH
_ANY = pl.BlockSpec(memory_space=pl.ANY)
_VM = pl.BlockSpec(memory_space=pltpu.VMEM)
_HALF = 512
N_DEV = 8


def _coords():
    return lax.axis_index("x"), lax.axis_index("y"), lax.axis_index("c")


def _other_chips(x, y):
    return [(1 - x, y), (x, 1 - y), (1 - x, 1 - y)]


def _half(cc):
    return pl.ds(cc * _HALF, _HALF)


def _gather_shards(slots):
    n = len(slots)

    def body(*refs):
        buf = refs[n:2 * n]
        send_sems, recv_sems = refs[2 * n:]
        x, y, c = _coords()
        chip = 2 * x + y
        sibling = (x, y, 1 - c)
        chips = _other_chips(x, y)

        def copy(i, frm, cc, k, to):
            part = buf[i].at[frm, :, _half(cc)]
            return pltpu.make_async_remote_copy(src_ref=part, dst_ref=part, send_sem=send_sems.at[6 * i + k],
                                                recv_sem=recv_sems.at[6 * i + k], device_id=to, device_id_type=_MESH)

        def chip_of(k):
            return 2 * chips[k][0] + chips[k][1]

        first = [copy(i, chip, c, k, (*chips[k], c)) for k in range(3) for i in range(n)]
        for cp in first:
            cp.start()
        passed = []
        for k in range(3):
            for i in range(n):
                copy(i, chip_of(k), c, k, (*chips[k], c)).wait_recv()
                passed.append(copy(i, chip_of(k), c, 3 + k, sibling))
                passed[-1].start()
        for k in range(3):
            for i in range(n):
                copy(i, chip_of(k), 1 - c, 3 + k, sibling).wait_recv()
        for cp in first + passed:
            cp.wait_send()

    return pl.pallas_call(
        body, name="gather_shards",
        out_shape=tuple(jax.ShapeDtypeStruct(s.shape, s.dtype) for s in slots),
        in_specs=[_ANY] * n, out_specs=tuple([_ANY] * n),
        input_output_aliases={i: i for i in range(n)},
        scratch_shapes=[pltpu.SemaphoreType.DMA((6 * n,)), pltpu.SemaphoreType.DMA((6 * n,))],
    )(*slots)


def _allgather8(block, name):
    rows, width = block.shape

    def body(x_ref, out_ref, send_sems, recv_sems, local_sem):
        x, y, c = _coords()
        me, sibling = (x, y, c), (x, y, 1 - c)
        chips = _other_chips(x, y)

        def slot(px, py, pc):
            return out_ref.at[4 * px + 2 * py + pc]

        def copy(k, blk, to, src=None):
            return pltpu.make_async_remote_copy(src_ref=slot(*blk) if src is None else src, dst_ref=slot(*blk),
                                                send_sem=send_sems.at[k], recv_sem=recv_sems.at[k],
                                                device_id=to, device_id_type=_MESH)

        mine = pltpu.make_async_copy(x_ref, slot(*me), local_sem)
        mine.start()
        first = [copy(0, me, sibling, src=x_ref)]
        first += [copy(1 + j, me, (*chip, c), src=x_ref) for j, chip in enumerate(chips)]
        for cp in first:
            cp.start()
        passed = [copy(4 + j, (*chip, c), sibling) for j, chip in enumerate(chips)]
        for j, chip in enumerate(chips):
            copy(1 + j, (*chip, c), me).wait_recv()
            passed[j].start()
        copy(0, sibling, me).wait_recv()
        for j, chip in enumerate(chips):
            copy(4 + j, (*chip, 1 - c), me).wait_recv()
        for cp in first + passed:
            cp.wait_send()
        mine.wait()

    return pl.pallas_call(
        body, name=name,
        out_shape=jax.ShapeDtypeStruct((N_DEV, rows, width), block.dtype),
        in_specs=[_VM], out_specs=_VM,
        scratch_shapes=[pltpu.SemaphoreType.DMA((7,)), pltpu.SemaphoreType.DMA((7,)), pltpu.SemaphoreType.DMA],
    )(block)


def _pair_swap_halves(arrs):
    n = len(arrs)

    def body(*refs):
        src, dst = refs[:n], refs[n:2 * n]
        send_sems, recv_sems = refs[2 * n:]
        x, y, c = _coords()
        cps = [pltpu.make_async_remote_copy(src_ref=src[i].at[:, :, _half(1 - c)], dst_ref=dst[i],
                                            send_sem=send_sems.at[i], recv_sem=recv_sems.at[i],
                                            device_id=(x, y, 1 - c), device_id_type=_MESH) for i in range(n)]
        for cp in cps:
            cp.start()
        for cp in cps:
            cp.wait()

    return pl.pallas_call(
        body, name="pair_swap_halves",
        out_shape=tuple(jax.ShapeDtypeStruct((4, a.shape[1], _HALF), a.dtype) for a in arrs),
        in_specs=[_ANY] * n, out_specs=tuple([_ANY] * n),
        scratch_shapes=[pltpu.SemaphoreType.DMA((n,)), pltpu.SemaphoreType.DMA((n,))],
    )(*arrs)


def _chip_exchange(arrs):
    n = len(arrs)

    def body(*refs):
        src, dst = refs[:n], refs[n:2 * n]
        send_sems, recv_sems = refs[2 * n:]
        x, y, c = _coords()
        chips = _other_chips(x, y)
        cps = [pltpu.make_async_remote_copy(src_ref=src[i].at[2 * chips[k][0] + chips[k][1]], dst_ref=dst[i].at[k],
                                            send_sem=send_sems.at[3 * i + k], recv_sem=recv_sems.at[3 * i + k],
                                            device_id=(*chips[k], c), device_id_type=_MESH)
               for k in range(3) for i in range(n)]
        for cp in cps:
            cp.start()
        for cp in cps:
            cp.wait()

    return pl.pallas_call(
        body, name="chip_exchange",
        out_shape=tuple(jax.ShapeDtypeStruct((3,) + a.shape[1:], a.dtype) for a in arrs),
        in_specs=[_ANY] * n, out_specs=tuple([_ANY] * n),
        scratch_shapes=[pltpu.SemaphoreType.DMA((3 * n,)), pltpu.SemaphoreType.DMA((3 * n,))],
    )(*arrs)


def _pair_join_halves(fulls):
    n = len(fulls)

    def body(*refs):
        buf = refs[n:2 * n]
        send_sems, recv_sems = refs[2 * n:]
        x, y, c = _coords()

        def remote(i, cc):
            part = buf[i].at[:, _half(cc)]
            return pltpu.make_async_remote_copy(src_ref=part, dst_ref=part, send_sem=send_sems.at[i],
                                                recv_sem=recv_sems.at[i], device_id=(x, y, 1 - c), device_id_type=_MESH)

        for i in range(n):
            remote(i, c).start()
        for i in range(n):
            remote(i, c).wait_send()
            remote(i, 1 - c).wait_recv()

    return pl.pallas_call(
        body, name="pair_join_halves",
        out_shape=tuple(jax.ShapeDtypeStruct(a.shape, a.dtype) for a in fulls),
        in_specs=[_ANY] * n, out_specs=tuple([_ANY] * n),
        input_output_aliases={i: i for i in range(n)},
        scratch_shapes=[pltpu.SemaphoreType.DMA((n,)), pltpu.SemaphoreType.DMA((n,))],
    )(*fulls)


_RED_TC = 128
_RED_NT = _HALF // _RED_TC


def _add_pair(ids, g32, recv_a):
    rows = g32.shape[1]

    def body(ids_ref, g_ref, r_ref, o_ref):
        o_ref[...] = (g_ref[...] + r_ref[...]).astype(BF16)

    blk = pl.BlockSpec((1, rows, _RED_TC), lambda j, l, ids: (j, 0, l))
    return pl.pallas_call(
        body, name="add_pair",
        out_shape=jax.ShapeDtypeStruct((4, rows, _HALF), BF16),
        grid_spec=pltpu.PrefetchScalarGridSpec(
            num_scalar_prefetch=1, grid=(4, _RED_NT),
            in_specs=[pl.BlockSpec((1, rows, _RED_TC), lambda j, l, ids: (j, 0, ids[0] * _RED_NT + l)), blk],
            out_specs=blk),
        compiler_params=_cp("parallel", "parallel"),
    )(ids, g32, recv_a)


def _add_chips(ids, g32, recv_a, recv_b):
    rows = g32.shape[1]

    def body(ids_ref, g_ref, a_ref, b_ref, o_ref):
        acc = g_ref[0] + a_ref[0]
        for k in range(3):
            acc = acc + b_ref[k].astype(F32)
        o_ref[...] = acc

    return pl.pallas_call(
        body, name="add_chips",
        out_shape=jax.ShapeDtypeStruct((rows, 2 * _HALF), F32),
        grid_spec=pltpu.PrefetchScalarGridSpec(
            num_scalar_prefetch=1, grid=(_RED_NT,),
            in_specs=[pl.BlockSpec((1, rows, _RED_TC), lambda l, ids: (ids[1], 0, ids[0] * _RED_NT + l)),
                      pl.BlockSpec((1, rows, _RED_TC), lambda l, ids: (ids[1], 0, l)),
                      pl.BlockSpec((3, rows, _RED_TC), lambda l, ids: (0, 0, l))],
            out_specs=pl.BlockSpec((rows, _RED_TC), lambda l, ids: (0, ids[0] * _RED_NT + l))),
        compiler_params=_cp("parallel"),
    )(ids, g32, recv_a, recv_b)


def _sum8(gathered):
    _, rows, width = gathered.shape

    def body(g_ref, o_ref):
        acc = g_ref[0]
        for d in range(1, N_DEV):
            acc = acc + g_ref[d]
        o_ref[...] = acc

    return pl.pallas_call(
        body, name="sum8",
        out_shape=jax.ShapeDtypeStruct((rows, width), F32),
        in_specs=[_VM], out_specs=_VM,
    )(gathered)


def _adamw(w, g, m, v, name):
    rows, cols = w.shape
    budget = (3 << 20) // 2
    tr, tc = rows, cols
    if rows * cols * 4 > budget:
        if rows % 8 == 0:
            tr = next(c for c in (512, 256, 128, 64, 32, 16, 8) if rows % c == 0 and c * cols * 4 <= budget)
        else:
            tc = next(c for c in (512, 256, 128) if cols % c == 0 and rows * c * 4 <= budget)
    c1 = 1.0 - ADAM_B1 ** ADAM_STEP
    c2 = 1.0 - ADAM_B2 ** ADAM_STEP

    def body(w_ref, g_ref, m_ref, v_ref, d_ref, mo_ref, vo_ref):
        gg = g_ref[...]
        mn = ADAM_B1 * m_ref[...] + (1.0 - ADAM_B1) * gg
        vn = ADAM_B2 * v_ref[...] + (1.0 - ADAM_B2) * (gg * gg)
        mo_ref[...] = mn
        vo_ref[...] = vn
        d_ref[...] = -ADAM_LR * ((mn / c1) / (jnp.sqrt(vn / c2) + ADAM_EPS) + ADAM_WD * w_ref[...])

    blk = pl.BlockSpec((tr, tc), lambda i, j: (i, j))
    shp = jax.ShapeDtypeStruct((rows, cols), F32)
    return pl.pallas_call(
        body, name=name, out_shape=(shp, shp, shp), grid=(rows // tr, cols // tc),
        in_specs=[blk] * 4, out_specs=(blk, blk, blk),
        compiler_params=_cp("parallel", "parallel"),
    )(w, g, m, v)


def _rows128(a):
    return a.reshape(-1, 128)


def _pack_small(norm_pre, conv_b, ssd_norm, gate_bias, norm_post, dt_bias, a_log, d_skip, fgate_bias):
    tiny = jnp.concatenate([dt_bias.reshape(-1), a_log.reshape(-1), d_skip.reshape(-1), fgate_bias.reshape(-1),
                            jnp.zeros((16,), F32)])
    return jnp.concatenate([_rows128(norm_pre), _rows128(conv_b), _rows128(ssd_norm), _rows128(gate_bias),
                            _rows128(norm_post), tiny.reshape(1, 128)], axis=0)


_SMALL_ROWS = 73
_SMALL_PAD = 80


def _unpack_small(p):
    tiny = p[72]
    return dict(norm_pre=p[0:8].reshape(1, 1024), conv_b=p[8:32].reshape(1, 3072), ssd_norm=p[32:48].reshape(1, 2048),
                gate_bias=p[48:64].reshape(1, 2048), norm_post=p[64:72].reshape(1, 1024),
                dt_bias=tiny[0:32].reshape(1, 32), a_log=tiny[32:64].reshape(1, 32),
                d_skip=tiny[64:96].reshape(1, 32), fgate_bias=tiny[96:112].reshape(1, 16))


def _pad_rows(a, rows):
    return jnp.concatenate([a, jnp.zeros((rows - a.shape[0], a.shape[1]), a.dtype)], axis=0)


def kernel(x, meta_tokens, norm_pre, w_in, conv_w, conv_b, dt_bias, a_log, d_skip, ssd_norm, fgate_bias, gate_bias, w_proj_ssd, w_proj_att, w_out, norm_post, loss_target, m_meta_tokens, m_norm_pre, m_w_in, m_conv_w, m_conv_b, m_dt_bias, m_a_log, m_d_skip, m_ssd_norm, m_fgate_bias, m_gate_bias, m_w_proj_ssd, m_w_proj_att, m_w_out, m_norm_post, v_meta_tokens, v_norm_pre, v_w_in, v_conv_w, v_conv_b, v_dt_bias, v_a_log, v_d_skip, v_ssd_norm, v_fgate_bias, v_gate_bias, v_w_proj_ssd, v_w_proj_att, v_w_out, v_norm_post):
    cx, cy, cc = _coords()
    chip = 2 * cx + cy
    ids = jnp.stack([cc, chip]).astype(jnp.int32)
    seq = x.shape[1]

    w_in_sh = jnp.transpose(w_in[0]).astype(BF16)
    w_pr_sh = jnp.concatenate([w_proj_ssd[0], w_proj_att[0], w_out[0]], axis=0).astype(BF16)

    def own_slot(sh):
        return lax.dynamic_update_slice(lax.empty((4,) + sh.shape, sh.dtype), sh[None], (chip, 0, 0))

    g_in, g_pr = _gather_shards([own_slot(w_in_sh), own_slot(w_pr_sh)])
    w_main, w_small = _to_aligned_rows(g_in.reshape(N_COLS, D_MODEL))
    wps = g_pr[:, 0:512].reshape(D_SSD, D_MODEL)
    wpa = g_pr[:, 512:768].reshape(D_ATT, D_MODEL)
    wout = g_pr[:, 768:1024].reshape(D_MODEL, D_MODEL)
    sm_sh = jnp.concatenate([_rows128(meta_tokens), _rows128(conv_w[0])], axis=0)
    sm_all = _allgather8(sm_sh, "gather_small_weights")[0::2]
    meta_full = jnp.transpose(sm_all[:, 0:32].reshape(4, N_META, 256), (1, 0, 2)).reshape(N_META, D_MODEL)
    conv_w_full = jnp.transpose(sm_all[:, 32:56].reshape(4, CONV_K, 768), (1, 0, 2)).reshape(CONV_K, CONV_DIM)

    h = jnp.concatenate([jnp.zeros((PADF, D_MODEL), F32), meta_full, x[0]], axis=0)
    bias_row = jnp.concatenate([dt_bias[0], fgate_bias[0], jnp.zeros((N_SMALL - H_SSD - H_ATT,), F32)]).reshape(1, N_SMALL)
    a_neg = -jnp.exp(a_log[0])
    a_row = jnp.concatenate([a_neg, jnp.zeros((N_SMALL - H_SSD,), F32)]).reshape(1, N_SMALL)
    dsk_row = jnp.repeat(d_skip[0], 64).reshape(1, D_SSD)
    r = _local_step(h, loss_target[0], w_main, w_small, wps, wpa, wout, norm_pre, conv_w_full, conv_b, bias_row,
                    a_row, dsk_row, ssd_norm, gate_bias, norm_post)
    dh = r["dh"]
    grad_x = dh[PADF + N_META:].reshape(1, seq, D_MODEL)

    tiny = r["d_bias_row"][0]
    part_small = _pack_small(r["d_norm_pre"], r["d_conv_b"], r["d_ssd_norm"], r["d_gate_bias"], r["d_norm_post"],
                             tiny[0:H_SSD], r["d_a"][0, 0:H_SSD] * a_neg, r["d_dsk"].reshape(H_SSD, 64).sum(axis=1),
                             tiny[H_SSD:H_SSD + H_ATT])
    part = jnp.concatenate([_pad_rows(part_small, _SMALL_PAD), _rows128(r["d_conv_w"]),
                            _rows128(dh[PADF:PADF + N_META]), r["loss_blk"]], axis=0)
    tot = _sum8(_allgather8(part, "gather_small_grads"))
    loss = tot[_SMALL_PAD + 96 + 128, 0]
    g_small = tot[0:_SMALL_PAD]
    g_conv_w = lax.dynamic_slice_in_dim(tot[_SMALL_PAD:_SMALL_PAD + 96].reshape(CONV_K, CONV_DIM), chip * 768, 768, axis=1)
    g_meta = lax.dynamic_slice_in_dim(tot[_SMALL_PAD + 96:_SMALL_PAD + 224].reshape(N_META, D_MODEL), chip * 256, 256, axis=1)

    g32_in = _from_aligned_rows(r["d_w_main"], r["d_w_small"]).reshape(4, N_COLS // 4, D_MODEL)
    g32_pr = jnp.concatenate([r["d_wps"].reshape(4, 512, D_MODEL), r["d_wpa"].reshape(4, 256, D_MODEL),
                              r["d_wout"].reshape(4, 256, D_MODEL)], axis=1)
    ra_in, ra_pr = _pair_swap_halves([g32_in, g32_pr])
    pb_in = _add_pair(ids, g32_in, ra_in)
    pb_pr = _add_pair(ids, g32_pr, ra_pr)
    rb_in, rb_pr = _chip_exchange([pb_in, pb_pr])
    half_in = _add_chips(ids, g32_in, ra_in, rb_in)
    half_pr = _add_chips(ids, g32_pr, ra_pr, rb_pr)
    gw_in, gw_pr = _pair_join_halves([half_in, half_pr])

    upd = {}
    upd["w_in"] = tuple(jnp.transpose(a) for a in (gw_in,) + _adamw(
        jnp.transpose(w_in[0]), gw_in, jnp.transpose(m_w_in[0]), jnp.transpose(v_w_in[0]), "adamw_w_in"))
    w_pr32 = jnp.concatenate([w_proj_ssd[0], w_proj_att[0], w_out[0]], axis=0)
    m_pr = jnp.concatenate([m_w_proj_ssd[0], m_w_proj_att[0], m_w_out[0]], axis=0)
    v_pr = jnp.concatenate([v_w_proj_ssd[0], v_w_proj_att[0], v_w_out[0]], axis=0)
    pr = (gw_pr,) + _adamw(w_pr32, gw_pr, m_pr, v_pr, "adamw_w_proj")
    upd["w_proj_ssd"] = tuple(a[0:512] for a in pr)
    upd["w_proj_att"] = tuple(a[512:768] for a in pr)
    upd["w_out"] = tuple(a[768:1024] for a in pr)
    upd["conv_w"] = (g_conv_w,) + _adamw(conv_w[0], g_conv_w, m_conv_w[0], v_conv_w[0], "adamw_conv_w")
    upd["meta_tokens"] = (g_meta,) + _adamw(meta_tokens, g_meta, m_meta_tokens, v_meta_tokens, "adamw_meta")
    pk = lambda np_, cb, sn, gb, npo, dtb, al, ds, fg: _pad_rows(_pack_small(np_, cb, sn, gb, npo, dtb, al, ds, fg), _SMALL_PAD)
    w_sm = pk(norm_pre, conv_b, ssd_norm, gate_bias, norm_post, dt_bias, a_log, d_skip, fgate_bias)
    m_sm = pk(m_norm_pre, m_conv_b, m_ssd_norm, m_gate_bias, m_norm_post, m_dt_bias, m_a_log, m_d_skip, m_fgate_bias)
    v_sm = pk(v_norm_pre, v_conv_b, v_ssd_norm, v_gate_bias, v_norm_post, v_dt_bias, v_a_log, v_d_skip, v_fgate_bias)
    sm = [_unpack_small(a) for a in (g_small,) + _adamw(w_sm, g_small, m_sm, v_sm, "adamw_small")]
    for name in ("norm_pre", "conv_b", "dt_bias", "a_log", "d_skip", "ssd_norm", "fgate_bias", "gate_bias", "norm_post"):
        upd[name] = tuple(s[name] for s in sm)
    lead = ("w_in", "conv_w", "w_proj_ssd", "w_proj_att", "w_out")
    order = ("meta_tokens", "norm_pre", "w_in", "conv_w", "conv_b", "dt_bias", "a_log", "d_skip", "ssd_norm",
             "fgate_bias", "gate_bias", "w_proj_ssd", "w_proj_att", "w_out", "norm_post")
    outs = [loss, grad_x]
    for part_i in range(4):
        for name in order:
            a = upd[name][part_i]
            outs.append(a[None] if name in lead else a)
    return tuple(outs)
```

```python
import functools
import math

import jax
import jax.numpy as jnp
from jax import lax
from jax.experimental import pallas as pl
from jax.experimental.pallas import tpu as pltpu

F32 = jnp.float32
BF16 = jnp.bfloat16
HIGHEST = lax.Precision.HIGHEST

D_MODEL = 1024
N_META = 16
CHUNK = 128
PADF = CHUNK - N_META
D_SSD = 2048
H_SSD = 32
G_SSD = 4
N_STATE = 128
CONV_K = 4
CONV_DIM = D_SSD + 2 * G_SSD * N_STATE
H_ATT = 16
D_ATT = 1024
EPS = 1e-6
N_COLS = 11312

C_Z, C_XBC, C_ZA, C_Q, C_K, C_V, C_G = 0, 2048, 5120, 6144, 7168, 8192, 9216
N_MAIN = 11264
N_SMALL = 128
O_Z, O_XBC, O_DT, O_ZA, O_Q, O_K, O_V, O_F, O_G = (
    (0, 2048), (2048, 3072), (5120, 32), (5152, 1024), (6176, 1024), (7200, 1024),
    (8224, 1024), (9248, 16), (9264, 2048))

ADAM_LR, ADAM_B1, ADAM_B2, ADAM_EPS, ADAM_WD, ADAM_STEP = 0.001, 0.9, 0.999, 1e-08, 0.01, 10

VMEM_LIMIT = 56 * 1024 * 1024


def _cp(*sem):
    return pltpu.CompilerParams(dimension_semantics=sem, vmem_limit_bytes=VMEM_LIMIT)


def _tile(n, prefs):
    for p in prefs:
        if n % p == 0:
            return p
    raise ValueError(f"no tile for {n} in {prefs}")


def _iota(shape, dim):
    return lax.broadcasted_iota(jnp.int32, shape, dim)


def _sigmoid(x):
    return 1.0 / (1.0 + jnp.exp(-x))


def _softplus_tail(x):
    return jnp.log(1.0 + jnp.exp(-jnp.abs(x)))


_NN = (((1,), (0,)), ((), ()))
_NT = (((1,), (1,)), ((), ()))
_TN = (((0,), (0,)), ((), ()))


def _dot(a, b, dims=_NN):
    return lax.dot_general(a, b, dims, preferred_element_type=F32)


def _dot_exact(a, b, dims=_NN):
    return lax.dot_general(a, b, dims, precision=HIGHEST, preferred_element_type=F32)


def _matmul(a, b, mode, out_dtype, name, tm, tn, tk):
    if mode == "tn":
        kdim, m = a.shape
    else:
        m, kdim = a.shape
    n = b.shape[0] if mode == "nt" else b.shape[1]
    nk = kdim // tk
    dims = {"nn": _NN, "nt": _NT, "tn": _TN}[mode]
    a_spec = (pl.BlockSpec((tk, tm), lambda i, j, k: (k, i)) if mode == "tn"
              else pl.BlockSpec((tm, tk), lambda i, j, k: (i, k)))
    b_spec = (pl.BlockSpec((tn, tk), lambda i, j, k: (j, k)) if mode == "nt"
              else pl.BlockSpec((tk, tn), lambda i, j, k: (k, j)))

    def body(a_ref, b_ref, o_ref, acc_ref):
        k = pl.program_id(2)
        p = _dot(a_ref[...].astype(BF16), b_ref[...].astype(BF16), dims)
        if nk == 1:
            o_ref[...] = p.astype(out_dtype)
        else:
            @pl.when(k == 0)
            def _():
                acc_ref[...] = p

            @pl.when(k > 0)
            def _():
                acc_ref[...] += p

            @pl.when(k == nk - 1)
            def _():
                o_ref[...] = acc_ref[...].astype(out_dtype)

    return pl.pallas_call(
        body, name=name,
        out_shape=jax.ShapeDtypeStruct((m, n), out_dtype),
        grid=(m // tm, n // tn, nk),
        in_specs=[a_spec, b_spec],
        out_specs=pl.BlockSpec((tm, tn), lambda i, j, k: (i, j)),
        scratch_shapes=[pltpu.VMEM((tm, tn), F32)],
        compiler_params=_cp("parallel", "parallel", "arbitrary"),
    )(a, b)


_CAT_BLK = 1024


def _piece_ranges(pieces):
    out, off = [], 0
    for p in pieces:
        nb = p.shape[1] // _CAT_BLK
        out.append((off, nb))
        off += nb
    return out, off


def _matmul_cat_nn(pieces, b, name, tm):
    t = pieces[0].shape[0]
    n = b.shape[1]
    ranges, nk = _piece_ranges(pieces)

    def body(*refs):
        a_refs, b_ref, o_ref, acc_ref = refs[:len(pieces)], refs[-3], refs[-2], refs[-1]
        k = pl.program_id(1)

        @pl.when(k == 0)
        def _():
            acc_ref[...] = jnp.zeros_like(acc_ref)

        for a_ref, (off, nb) in zip(a_refs, ranges):
            @pl.when((k >= off) & (k < off + nb))
            def _(a_ref=a_ref):
                acc_ref[...] += _dot(a_ref[...], b_ref[...])

        @pl.when(k == nk - 1)
        def _():
            o_ref[...] = acc_ref[...]

    def a_spec(off, nb):
        return pl.BlockSpec((tm, _CAT_BLK), lambda i, k: (i, jnp.clip(k - off, 0, nb - 1)))

    return pl.pallas_call(
        body, name=name,
        out_shape=jax.ShapeDtypeStruct((t, n), F32),
        grid=(t // tm, nk),
        in_specs=[a_spec(off, nb) for off, nb in ranges] + [pl.BlockSpec((_CAT_BLK, n), lambda i, k: (k, 0))],
        out_specs=pl.BlockSpec((tm, n), lambda i, k: (i, 0)),
        scratch_shapes=[pltpu.VMEM((tm, n), F32)],
        compiler_params=_cp("parallel", "arbitrary"),
    )(*pieces, b)


def _matmul_cat_tn(pieces, b, name, tk):
    t = pieces[0].shape[0]
    n = b.shape[1]
    ranges, nm = _piece_ranges(pieces)
    nk = t // tk

    def body(*refs):
        a_refs, b_ref, o_ref, acc_ref = refs[:len(pieces)], refs[-3], refs[-2], refs[-1]
        m = pl.program_id(0)
        k = pl.program_id(1)

        @pl.when(k == 0)
        def _():
            acc_ref[...] = jnp.zeros_like(acc_ref)

        for a_ref, (off, nb) in zip(a_refs, ranges):
            @pl.when((m >= off) & (m < off + nb))
            def _(a_ref=a_ref):
                acc_ref[...] += _dot(a_ref[...], b_ref[...], _TN)

        @pl.when(k == nk - 1)
        def _():
            o_ref[...] = acc_ref[...]

    def a_spec(off, nb):
        def index(m, k):
            mine = (m >= off) & (m < off + nb)
            return jnp.where(mine, k, 0), jnp.clip(m - off, 0, nb - 1)
        return pl.BlockSpec((tk, _CAT_BLK), index)

    return pl.pallas_call(
        body, name=name,
        out_shape=jax.ShapeDtypeStruct((nm * _CAT_BLK, n), F32),
        grid=(nm, nk),
        in_specs=[a_spec(off, nb) for off, nb in ranges] + [pl.BlockSpec((tk, n), lambda m, k: (k, 0))],
        out_specs=pl.BlockSpec((_CAT_BLK, n), lambda m, k: (m, 0)),
        scratch_shapes=[pltpu.VMEM((_CAT_BLK, n), F32)],
        compiler_params=_cp("parallel", "arbitrary"),
    )(*pieces, b)


def _row_tile(t):
    return _tile(t, (352, 128))


def _row_tile_wide(t):
    return _tile(t, (176, 128))


def _norm1_fwd(h, g):
    t = h.shape[0]
    tm = _row_tile(t)

    def body(h_ref, g_ref, u_ref):
        x = h_ref[...]
        r = lax.rsqrt(jnp.mean(x * x, axis=-1, keepdims=True) + EPS)
        u_ref[...] = (x * r * g_ref[...]).astype(BF16)

    return pl.pallas_call(
        body, name="norm1_fwd",
        out_shape=jax.ShapeDtypeStruct((t, D_MODEL), BF16),
        grid=(t // tm,),
        in_specs=[pl.BlockSpec((tm, D_MODEL), lambda i: (i, 0)),
                  pl.BlockSpec((1, D_MODEL), lambda i: (0, 0))],
        out_specs=pl.BlockSpec((tm, D_MODEL), lambda i: (i, 0)),
        compiler_params=_cp("parallel"),
    )(h, g)


def _norm1_bwd(du_a, du_b, h, g, dy):
    t = h.shape[0]
    tm = _row_tile(t)

    def body(a_ref, b_ref, h_ref, g_ref, dy_ref, dh_ref, dg_ref):
        i = pl.program_id(0)
        x = h_ref[...]
        du = a_ref[...] + b_ref[...]
        r = lax.rsqrt(jnp.mean(x * x, axis=-1, keepdims=True) + EPS)
        gdu = du * g_ref[...]
        dh_ref[...] = dy_ref[...] + r * (gdu - x * (r * r) * jnp.mean(gdu * x, axis=-1, keepdims=True))
        part = jnp.sum(du * x * r, axis=0, keepdims=True)

        @pl.when(i == 0)
        def _():
            dg_ref[...] = part

        @pl.when(i > 0)
        def _():
            dg_ref[...] += part

    row = pl.BlockSpec((tm, D_MODEL), lambda i: (i, 0))
    vec = pl.BlockSpec((1, D_MODEL), lambda i: (0, 0))
    return pl.pallas_call(
        body, name="norm1_bwd",
        out_shape=(jax.ShapeDtypeStruct((t, D_MODEL), F32), jax.ShapeDtypeStruct((1, D_MODEL), F32)),
        grid=(t // tm,),
        in_specs=[row, row, row, vec, row],
        out_specs=(row, vec),
        compiler_params=_cp("arbitrary"),
    )(du_a, du_b, h, g, dy)


def _small_fwd(small, bias_row):
    t = small.shape[0]

    def body(s_ref, b_ref, o_ref, carry_ref):
        c = pl.program_id(0)

        @pl.when(c == 0)
        def _():
            carry_ref[...] = jnp.zeros_like(carry_ref)

        x = s_ref[...] + b_ref[...]
        r0 = _iota((CHUNK, CHUNK), 0)
        r1 = _iota((CHUNK, CHUNK), 1)
        valid = (c * CHUNK + r0) >= PADF
        tail = _softplus_tail(x)
        dt = jnp.where(valid & (r1 < H_SSD), jnp.maximum(x, 0.0) + tail, 0.0)
        lf = jnp.where(valid & (r1 >= H_SSD) & (r1 < H_SSD + H_ATT), jnp.minimum(x, 0.0) - tail, 0.0)
        tri = (r0 >= r1).astype(F32)
        cs = _dot_exact(tri, lf) + carry_ref[...]
        carry_ref[...] = cs[CHUNK - 1:CHUNK, :]
        o_ref[...] = dt + cs

    return pl.pallas_call(
        body, name="small_fwd",
        out_shape=jax.ShapeDtypeStruct((t, N_SMALL), F32),
        grid=(t // CHUNK,),
        in_specs=[pl.BlockSpec((CHUNK, N_SMALL), lambda c: (c, 0)),
                  pl.BlockSpec((1, N_SMALL), lambda c: (0, 0))],
        out_specs=pl.BlockSpec((CHUNK, N_SMALL), lambda c: (c, 0)),
        scratch_shapes=[pltpu.VMEM((1, N_SMALL), F32)],
        compiler_params=_cp("arbitrary"),
    )(small, bias_row)


def _small_bwd(dsm, small, bias_row):
    t = small.shape[0]
    nc = t // CHUNK

    def body(d_ref, s_ref, b_ref, o_ref, db_ref, carry_ref):
        step = pl.program_id(0)
        c = nc - 1 - step

        @pl.when(step == 0)
        def _():
            carry_ref[...] = jnp.zeros_like(carry_ref)
            db_ref[...] = jnp.zeros_like(db_ref)

        x = s_ref[...] + b_ref[...]
        d = d_ref[...]
        r0 = _iota((CHUNK, CHUNK), 0)
        r1 = _iota((CHUNK, CHUNK), 1)
        valid = (c * CHUNK + r0) >= PADF
        is_dt = r1 < H_SSD
        is_f = (r1 >= H_SSD) & (r1 < H_SSD + H_ATT)
        triu = (r1 >= r0).astype(F32)
        dc = jnp.where(is_f, d, 0.0)
        dlf = _dot_exact(triu, dc) + carry_ref[...]
        carry_ref[...] = dlf[0:1, :]
        sg = _sigmoid(x)
        out = jnp.where(valid & is_dt, d * sg, 0.0) + jnp.where(valid & is_f, dlf * (1.0 - sg), 0.0)
        o_ref[...] = out.astype(BF16)
        db_ref[...] += jnp.sum(out, axis=0, keepdims=True)

    blk = pl.BlockSpec((CHUNK, N_SMALL), lambda s: (nc - 1 - s, 0))
    vec = pl.BlockSpec((1, N_SMALL), lambda s: (0, 0))
    return pl.pallas_call(
        body, name="small_bwd",
        out_shape=(jax.ShapeDtypeStruct((t, N_SMALL), BF16), jax.ShapeDtypeStruct((1, N_SMALL), F32)),
        grid=(nc,),
        in_specs=[blk, blk, vec],
        out_specs=(blk, vec),
        scratch_shapes=[pltpu.VMEM((1, N_SMALL), F32)],
        compiler_params=_cp("arbitrary"),
    )(dsm, small, bias_row)


_CONV_TC = 1024
_XBC_BLK = C_XBC // _CONV_TC


def _shift_down(cur, prev8, j):
    rc = pltpu.roll(cur, j, 0)
    rid = _iota(prev8.shape, 0)
    top = jnp.where(rid < j, pltpu.roll(prev8, j, 0), rc[0:8, :])
    return jnp.concatenate([top, rc[8:, :]], axis=0)


def _shift_up(cur, next8, j):
    n = cur.shape[0]
    ru = pltpu.roll(cur, n - j, 0)
    rid = _iota(next8.shape, 0)
    bot = jnp.where(rid >= 8 - j, pltpu.roll(next8, 8 - j, 0), ru[n - 8:, :])
    return jnp.concatenate([ru[:n - 8, :], bot], axis=0)


def _conv_pre(x_ref, p_ref, w_ref, b_ref, i):
    cur = x_ref[...]
    prev = jnp.where(i > 0, p_ref[...], 0.0)
    w = w_ref[...]
    taps = [cur] + [_shift_down(cur, prev, j) for j in (1, 2, 3)]
    acc = b_ref[...] + taps[0] * w[3:4, :]
    for j in (1, 2, 3):
        acc = acc + taps[j] * w[3 - j:4 - j, :]
    return acc, taps


def _conv_fwd(proj, conv_w, conv_b):
    t = proj.shape[0]
    tr = _row_tile(t)

    def body(x_ref, p_ref, w_ref, b_ref, o_ref):
        i = pl.program_id(0)
        acc, _ = _conv_pre(x_ref, p_ref, w_ref, b_ref, i)
        valid = (i * tr + _iota(acc.shape, 0)) >= PADF
        o_ref[...] = jnp.where(valid, acc * _sigmoid(acc), 0.0)

    return pl.pallas_call(
        body, name="conv_fwd",
        out_shape=jax.ShapeDtypeStruct((t, CONV_DIM), F32),
        grid=(t // tr, CONV_DIM // _CONV_TC),
        in_specs=[pl.BlockSpec((tr, _CONV_TC), lambda i, j: (i, _XBC_BLK + j)),
                  pl.BlockSpec((8, _CONV_TC), lambda i, j: (jnp.maximum(i * (tr // 8) - 1, 0), _XBC_BLK + j)),
                  pl.BlockSpec((CONV_K, _CONV_TC), lambda i, j: (0, j)),
                  pl.BlockSpec((1, _CONV_TC), lambda i, j: (0, j))],
        out_specs=pl.BlockSpec((tr, _CONV_TC), lambda i, j: (i, j)),
        compiler_params=_cp("parallel", "parallel"),
    )(proj, proj, conv_w, conv_b)


def _conv_bwd_act(dxbc, proj, conv_w, conv_b):
    t = proj.shape[0]
    tr = _row_tile(t)

    def body(d_ref, x_ref, p_ref, w_ref, b_ref, da_ref, dw_ref, db_ref):
        i = pl.program_id(1)
        acc, taps = _conv_pre(x_ref, p_ref, w_ref, b_ref, i)
        valid = (i * tr + _iota(acc.shape, 0)) >= PADF
        sg = _sigmoid(acc)
        da = jnp.where(valid, d_ref[...] * sg * (1.0 + acc * (1.0 - sg)), 0.0)
        da_ref[...] = da
        dw = jnp.concatenate([jnp.sum(da * taps[3 - k], axis=0, keepdims=True) for k in range(CONV_K)], axis=0)
        db = jnp.sum(da, axis=0, keepdims=True)

        @pl.when(i == 0)
        def _():
            dw_ref[...] = dw
            db_ref[...] = db

        @pl.when(i > 0)
        def _():
            dw_ref[...] += dw
            db_ref[...] += db

    return pl.pallas_call(
        body, name="conv_bwd_act",
        out_shape=(jax.ShapeDtypeStruct((t, CONV_DIM), F32),
                   jax.ShapeDtypeStruct((CONV_K, CONV_DIM), F32),
                   jax.ShapeDtypeStruct((1, CONV_DIM), F32)),
        grid=(CONV_DIM // _CONV_TC, t // tr),
        in_specs=[pl.BlockSpec((tr, _CONV_TC), lambda j, i: (i, j)),
                  pl.BlockSpec((tr, _CONV_TC), lambda j, i: (i, _XBC_BLK + j)),
                  pl.BlockSpec((8, _CONV_TC), lambda j, i: (jnp.maximum(i * (tr // 8) - 1, 0), _XBC_BLK + j)),
                  pl.BlockSpec((CONV_K, _CONV_TC), lambda j, i: (0, j)),
                  pl.BlockSpec((1, _CONV_TC), lambda j, i: (0, j))],
        out_specs=(pl.BlockSpec((tr, _CONV_TC), lambda j, i: (i, j)),
                   pl.BlockSpec((CONV_K, _CONV_TC), lambda j, i: (0, j)),
                   pl.BlockSpec((1, _CONV_TC), lambda j, i: (0, j))),
        compiler_params=_cp("parallel", "arbitrary"),
    )(dxbc, proj, proj, conv_w, conv_b)


def _conv_bwd_in(da, conv_w):
    t = da.shape[0]
    tr = _row_tile(t)
    last8 = t // 8 - 1

    def body(d_ref, n_ref, w_ref, o_ref):
        i = pl.program_id(0)
        cur = d_ref[...]
        nxt = jnp.where(i < pl.num_programs(0) - 1, n_ref[...], 0.0)
        w = w_ref[...]
        acc = cur * w[3:4, :]
        for j in (1, 2, 3):
            acc = acc + _shift_up(cur, nxt, j) * w[3 - j:4 - j, :]
        o_ref[...] = acc.astype(BF16)

    return pl.pallas_call(
        body, name="conv_bwd_in",
        out_shape=jax.ShapeDtypeStruct((t, CONV_DIM), BF16),
        grid=(t // tr, CONV_DIM // _CONV_TC),
        in_specs=[pl.BlockSpec((tr, _CONV_TC), lambda i, j: (i, j)),
                  pl.BlockSpec((8, _CONV_TC), lambda i, j: (jnp.minimum((i + 1) * (tr // 8), last8), j)),
                  pl.BlockSpec((CONV_K, _CONV_TC), lambda i, j: (0, j))],
        out_specs=pl.BlockSpec((tr, _CONV_TC), lambda i, j: (i, j)),
        compiler_params=_cp("parallel", "parallel"),
    )(da, da, conv_w)


_GW = D_SSD // G_SSD


def _ssd_prelude(dt_ref, a_ref, e_scr, es_scr, dte_scr):
    r0 = _iota((CHUNK, CHUNK), 0)
    r1 = _iota((CHUNK, CHUNK), 1)
    dt = jnp.where(r1 < H_SSD, dt_ref[...], 0.0)
    adt = dt * a_ref[...]
    acs = _dot_exact((r0 >= r1).astype(F32), adt)
    acs_t = acs.T
    alast = acs[CHUNK - 1:CHUNK, :]
    exp_a = jnp.exp(acs)
    dec_s = jnp.exp(alast - acs)
    lo = r1 < 64
    for j in range(H_SSD // 2):
        sl = slice(CHUNK * j, CHUNK * (j + 1))
        e_scr[:, sl] = jnp.where(lo, exp_a[:, 2 * j:2 * j + 1], exp_a[:, 2 * j + 1:2 * j + 2])
        es_scr[:, sl] = jnp.where(lo, dec_s[:, 2 * j:2 * j + 1], dec_s[:, 2 * j + 1:2 * j + 2])
        dte_scr[:, sl] = jnp.where(lo, dt[:, 2 * j:2 * j + 1], dt[:, 2 * j + 1:2 * j + 2])
    return dt, acs, acs_t, r0, r1, lo


def _chunk_decay_rows(acs_t, g):
    cd_t = jnp.exp(acs_t[:, CHUNK - 1:CHUNK])
    return jnp.concatenate(
        [jnp.broadcast_to(cd_t[8 * g + hh:8 * g + hh + 1, :], (64, N_STATE)) for hh in range(8)], axis=0)


def _ssd_fwd(xbc, dtlf, a_row, dsk_row):
    t = xbc.shape[0]
    nc = t // CHUNK

    def body(xs_ref, b_ref, c_ref, dt_ref, a_ref, dsk_ref, y_ref, hin_ref, h_scr, e_scr, es_scr, dte_scr):
        c = pl.program_id(0)

        @pl.when(c == 0)
        def _():
            h_scr[...] = jnp.zeros_like(h_scr)

        dt, acs, acs_t, r0, r1, lo = _ssd_prelude(dt_ref, a_ref, e_scr, es_scr, dte_scr)
        causal = r0 >= r1
        for g in range(G_SSD):
            gs = slice(_GW * g, _GW * (g + 1))
            bg = b_ref[:, N_STATE * g:N_STATE * (g + 1)].astype(BF16)
            cg = c_ref[:, N_STATE * g:N_STATE * (g + 1)].astype(BF16)
            cb = _dot(cg, bg, _NT)
            hg = h_scr[gs, :]
            hin_ref[0, gs, :] = hg
            xg = xs_ref[:, gs] * dte_scr[:, gs]
            yoff = _dot(cg, hg.astype(BF16), _NT) * e_scr[:, gs]
            st = _dot((xg * es_scr[:, gs]).astype(BF16), bg, _TN)
            h_scr[gs, :] = hg * _chunk_decay_rows(acs_t, g) + st
            for jj in range(4):
                j = 4 * g + jj
                sl = slice(CHUNK * j, CHUNK * (j + 1))
                xp = xg[:, CHUNK * jj:CHUNK * (jj + 1)]
                acc = yoff[:, CHUNK * jj:CHUNK * (jj + 1)] + dsk_ref[:, sl] * xs_ref[:, sl]
                for hh in range(2):
                    h = 2 * j + hh
                    seg = acs[:, h:h + 1] - acs_t[h:h + 1, :]
                    lm = jnp.exp(jnp.where(causal, seg, -1e30))
                    m = (cb * lm).astype(BF16)
                    xh = jnp.where(lo if hh == 0 else ~lo, xp, 0.0).astype(BF16)
                    acc = acc + _dot(m, xh)
                y_ref[:, sl] = acc

    return pl.pallas_call(
        body, name="ssd_fwd",
        out_shape=(jax.ShapeDtypeStruct((t, D_SSD), F32), jax.ShapeDtypeStruct((nc, D_SSD, N_STATE), F32)),
        grid=(nc,),
        in_specs=[pl.BlockSpec((CHUNK, D_SSD), lambda c: (c, 0)),
                  pl.BlockSpec((CHUNK, _GW), lambda c: (c, 4)),
                  pl.BlockSpec((CHUNK, _GW), lambda c: (c, 5)),
                  pl.BlockSpec((CHUNK, N_SMALL), lambda c: (c, 0)),
                  pl.BlockSpec((1, N_SMALL), lambda c: (0, 0)),
                  pl.BlockSpec((1, D_SSD), lambda c: (0, 0))],
        out_specs=(pl.BlockSpec((CHUNK, D_SSD), lambda c: (c, 0)),
                   pl.BlockSpec((1, D_SSD, N_STATE), lambda c: (c, 0, 0))),
        scratch_shapes=[pltpu.VMEM((D_SSD, N_STATE), F32)] + [pltpu.VMEM((CHUNK, D_SSD), F32)] * 3,
        compiler_params=_cp("arbitrary"),
    )(xbc, xbc, xbc, dtlf, a_row, dsk_row)


def _ssd_bwd(xbc, dtlf, a_row, dsk_row, hin, dy):
    t = xbc.shape[0]
    nc = t // CHUNK

    def body(xs_ref, b_ref, c_ref, dt_ref, a_ref, dsk_ref, hin_ref, dy_ref,
             dxbc_ref, ddt_ref, da_ref, ddsk_ref, dh_scr, e_scr, es_scr, dte_scr, dx_scr, whi_scr, wlo_scr):
        step = pl.program_id(0)

        @pl.when(step == 0)
        def _():
            dh_scr[...] = jnp.zeros_like(dh_scr)
            da_ref[...] = jnp.zeros_like(da_ref)
            ddsk_ref[...] = jnp.zeros_like(ddsk_ref)

        dt, acs, acs_t, r0, r1, lo = _ssd_prelude(dt_ref, a_ref, e_scr, es_scr, dte_scr)
        causal = r0 >= r1
        lane_row = _iota((1, CHUNK), 1)
        dacs = jnp.zeros((CHUNK, CHUNK), F32)
        dacs_t = jnp.zeros((CHUNK, CHUNK), F32)
        dalast = jnp.zeros((1, CHUNK), F32)
        ddt_dir = jnp.zeros((CHUNK, CHUNK), F32)
        ddsk_ref[...] += jnp.sum(dy_ref[...] * xs_ref[...], axis=0, keepdims=True)

        def head_sums(z, pick):
            hi = z.astype(BF16)
            return _dot(hi, pick) + _dot((z - hi.astype(F32)).astype(BF16), pick)

        for g in range(G_SSD):
            gs = slice(_GW * g, _GW * (g + 1))
            pick = (jnp.right_shift(_iota((_GW, CHUNK), 0), 6) + 8 * g == _iota((_GW, CHUNK), 1)).astype(BF16)
            bg = b_ref[:, N_STATE * g:N_STATE * (g + 1)].astype(BF16)
            cg = c_ref[:, N_STATE * g:N_STATE * (g + 1)].astype(BF16)
            cb = _dot(cg, bg, _NT)
            hg = hin_ref[0, gs, :]
            hgb = hg.astype(BF16)
            dhn = dh_scr[gs, :]
            dhnb = dhn.astype(BF16)
            esg = es_scr[:, gs]
            dyg = dy_ref[:, gs]
            xsg = xs_ref[:, gs]
            xg = xsg * dte_scr[:, gs]
            dyeb = (dyg * e_scr[:, gs]).astype(BF16)
            dc = _dot(dyeb, hgb)
            dh_y = _dot(dyeb, cg, _TN)
            dxs = _dot(bg, dhnb, _NT) * esg
            db = _dot((xg * esg).astype(BF16), dhnb)
            cd = _chunk_decay_rows(acs_t, g)
            dh_scr[gs, :] = dhn * cd + dh_y
            end_state = head_sums(jnp.broadcast_to(jnp.sum(xg * dxs, axis=0, keepdims=True), (8, _GW)), pick)[0:1, :]
            carried = dhn * hg * cd
            per_head = jnp.concatenate([jnp.sum(carried[64 * hh:64 * hh + 64, :], axis=0, keepdims=True)
                                        for hh in range(8)], axis=0)
            per_head = jnp.sum(per_head, axis=1, keepdims=True)
            for hh in range(8):
                end_state = end_state + jnp.where(lane_row == 8 * g + hh, per_head[hh:hh + 1, :], 0.0)
            dalast = dalast + end_state
            dcb = jnp.zeros((CHUNK, CHUNK), F32)
            for jj in range(4):
                j = 4 * g + jj
                sl = slice(CHUNK * j, CHUNK * (j + 1))
                ps = slice(CHUNK * jj, CHUNK * (jj + 1))
                xpb = xg[:, ps].astype(BF16)
                dyp = dyg[:, ps]
                dxp = dxs[:, ps]
                for hh in range(2):
                    h = 2 * j + hh
                    ws = slice(CHUNK * (2 * jj + hh), CHUNK * (2 * jj + hh + 1))
                    seg = acs[:, h:h + 1] - acs_t[h:h + 1, :]
                    lm = jnp.exp(jnp.where(causal, seg, -1e30))
                    mf = cb * lm
                    dyh = jnp.where(lo if hh == 0 else ~lo, dyp, 0.0).astype(BF16)
                    gm = _dot(dyh, xpb, _NT)
                    dcb = dcb + gm * lm
                    w = gm * mf
                    whi = w.astype(BF16)
                    whi_scr[:, ws] = whi
                    wlo_scr[:, ws] = (w - whi.astype(F32)).astype(BF16)
                    dacs_t = dacs_t - jnp.where(r0 == h, jnp.sum(w, axis=0, keepdims=True), 0.0)
                    dxp = dxp + _dot(mf.astype(BF16), dyh, _TN)
                dx_scr[:, sl] = dxp
            dxg = dx_scr[:, gs]
            pick_w = (jnp.right_shift(_iota((8 * CHUNK, CHUNK), 0), 7) + 8 * g == _iota((8 * CHUNK, CHUNK), 1)).astype(BF16)
            ch = _dot(cg, hgb, _NT)
            dacs = (dacs + _dot(whi_scr[...], pick_w) + _dot(wlo_scr[...], pick_w)
                    + head_sums(dyg * e_scr[:, gs] * ch - xg * dxs, pick))
            ddt_dir = ddt_dir + head_sums(dxg * xsg, pick)
            dcbb = dcb.astype(BF16)
            dxbc_ref[:, D_SSD + N_STATE * g:D_SSD + N_STATE * (g + 1)] = db + _dot(dcbb, cg, _TN)
            dxbc_ref[:, D_SSD + _GW + N_STATE * g:D_SSD + _GW + N_STATE * (g + 1)] = dc + _dot(dcbb, bg)
        dxbc_ref[:, 0:D_SSD] = dx_scr[...] * dte_scr[...] + dsk_ref[...] * dy_ref[...]
        dacs = dacs + dacs_t.T + jnp.where(r0 == CHUNK - 1, dalast, 0.0)
        dadt = _dot_exact((r1 >= r0).astype(F32), dacs)
        ddt_ref[...] = dadt * a_ref[...] + ddt_dir
        da_ref[...] += jnp.sum(dadt * dt, axis=0, keepdims=True)

    rev = lambda s: (nc - 1 - s, 0)
    return pl.pallas_call(
        body, name="ssd_bwd",
        out_shape=(jax.ShapeDtypeStruct((t, CONV_DIM), F32), jax.ShapeDtypeStruct((t, N_SMALL), F32),
                   jax.ShapeDtypeStruct((1, N_SMALL), F32), jax.ShapeDtypeStruct((1, D_SSD), F32)),
        grid=(nc,),
        in_specs=[pl.BlockSpec((CHUNK, D_SSD), rev),
                  pl.BlockSpec((CHUNK, _GW), lambda s: (nc - 1 - s, 4)),
                  pl.BlockSpec((CHUNK, _GW), lambda s: (nc - 1 - s, 5)),
                  pl.BlockSpec((CHUNK, N_SMALL), rev),
                  pl.BlockSpec((1, N_SMALL), lambda s: (0, 0)),
                  pl.BlockSpec((1, D_SSD), lambda s: (0, 0)),
                  pl.BlockSpec((1, D_SSD, N_STATE), lambda s: (nc - 1 - s, 0, 0)),
                  pl.BlockSpec((CHUNK, D_SSD), rev)],
        out_specs=(pl.BlockSpec((CHUNK, CONV_DIM), rev),
                   pl.BlockSpec((CHUNK, N_SMALL), rev),
                   pl.BlockSpec((1, N_SMALL), lambda s: (0, 0)),
                   pl.BlockSpec((1, D_SSD), lambda s: (0, 0))),
        scratch_shapes=([pltpu.VMEM((D_SSD, N_STATE), F32)] + [pltpu.VMEM((CHUNK, D_SSD), F32)] * 4
                        + [pltpu.VMEM((CHUNK, 8 * CHUNK), BF16)] * 2),
        compiler_params=_cp("arbitrary"),
    )(xbc, xbc, xbc, dtlf, a_row, dsk_row, hin, dy)


_NPAIR = H_ATT // 2
_QB, _KB, _VB = C_Q // 128, C_K // 128, C_V // 128
_SCALE = 1.0 / math.sqrt(64.0)


def _attn_blocks(t):
    return _tile(t, (1408, 384, 256, 128)), _tile(t, (384, 128))


def _split3(c):
    hi = c.astype(BF16).astype(F32)
    rest = c - hi
    mid = rest.astype(BF16).astype(F32)
    return hi, mid, rest - mid


def _head_lanes(lane, hh):
    return (lane < 64, 64) if hh == 0 else (lane >= 64, 0)


def _q_operand(q, cq, lane, hh):
    sel, first = _head_lanes(lane, hh)
    out = jnp.where(sel, q, 0.0)
    for n, col in enumerate(_split3(cq) + (1.0, 1.0, 1.0)):
        out = jnp.where(lane == first + n, col, out)
    return out.astype(BF16)


def _k_operand(k, ck, lane, hh):
    sel, first = _head_lanes(lane, hh)
    hi, mid, lo = _split3(ck)
    out = jnp.where(sel, k, 0.0)
    for n, col in enumerate((1.0, 1.0, 1.0, -hi, -mid, -lo)):
        out = jnp.where(lane == first + n, col, out)
    return out.astype(BF16)


def _needs_mask(i, kk, bq, bk):
    return kk * bk + bk - 1 > i * bq


_C_FILLER = 2.0 ** 30


def _attn_fwd(proj, c_col):
    t = proj.shape[0]
    bq, bk = _attn_blocks(t)
    nq, nk = t // bq, t // bk
    rs = 16

    def last_kv(i):
        return (i * bq + bq - 1) // bk

    def body(q_ref, k_ref, v_ref, cq_ref, ck_ref, o_ref, lse_ref, qs_scr, s_scr, p_scr, m_scr, acc_scr):
        i = pl.program_id(1)
        kk = pl.program_id(2)
        lane_q = _iota((bq, 128), 1)

        @pl.when(kk == 0)
        def _():
            m_scr[...] = jnp.full_like(m_scr, -1e30)
            acc_scr[...] = jnp.zeros_like(acc_scr)
            q = q_ref[...] * _SCALE
            cq = cq_ref[0]
            for hh in range(2):
                qs_scr[hh] = _q_operand(q, cq[:, hh:hh + 1], lane_q, hh)

        def step(masked):
            lane_k = _iota((bk, 128), 1)
            k = k_ref[...]
            v = v_ref[...]
            ck = ck_ref[0]
            ahead = _iota((rs, bq), 0) - _iota((rs, bq), 1)
            for hh in range(2):
                sel, first = _head_lanes(lane_k, hh)
                ks = _k_operand(k, ck[:, hh:hh + 1], lane_k, hh)
                vs = jnp.where(sel, v, jnp.where(lane_k == first, 1.0, 0.0)).astype(BF16)
                s_scr[hh] = _dot(ks, qs_scr[hh], _NT)

                def block_max(r, mx):
                    rows = pl.ds(pl.multiple_of(r * rs, rs), rs)
                    s = s_scr[hh, rows, :]
                    if masked:
                        s = jnp.where(ahead <= i * bq - kk * bk - r * rs, s, -1e30)
                        s_scr[hh, rows, :] = s
                    return jnp.maximum(mx, s)

                mx = lax.fori_loop(0, bk // rs, block_max, jnp.full((rs, bq), -1e30, F32), unroll=True)
                m_old = m_scr[hh]
                m_new = jnp.maximum(m_old, jnp.max(mx, axis=0, keepdims=True))
                m_scr[hh] = m_new

                def probs(r, carry):
                    rows = pl.ds(pl.multiple_of(r * rs, rs), rs)
                    p_scr[hh, rows, :] = jnp.exp(s_scr[hh, rows, :] - m_new).astype(BF16)
                    return carry

                lax.fori_loop(0, bk // rs, probs, 0, unroll=True)
                acc_scr[hh] = acc_scr[hh] * jnp.exp(m_old - m_new) + _dot(vs, p_scr[hh], _TN)

        active = kk <= last_kv(i)
        masked = _needs_mask(i, kk, bq, bk)

        @pl.when(active & masked)
        def _():
            step(True)

        @pl.when(active & jnp.logical_not(masked))
        def _():
            step(False)

        @pl.when(kk == nk - 1)
        def _():
            a = acc_scr[0]
            b = acc_scr[1]
            la = a[64:65, :]
            lb = b[0:1, :]
            o_ref[...] = jnp.where(lane_q < 64, (a / la).T, (b / lb).T)
            lse_ref[0] = jnp.concatenate([m_scr[0] + jnp.log(la), m_scr[1] + jnp.log(lb)], axis=0)

    kvi = lambda i, kk: jnp.minimum(kk, last_kv(i))
    kv = lambda off: pl.BlockSpec((bk, 128), lambda j, i, kk: (kvi(i, kk), off + j))
    return pl.pallas_call(
        body, name="attn_fwd",
        out_shape=(jax.ShapeDtypeStruct((t, D_ATT), F32), jax.ShapeDtypeStruct((_NPAIR, 2, t), F32)),
        grid=(_NPAIR, nq, nk),
        in_specs=[pl.BlockSpec((bq, 128), lambda j, i, kk: (i, _QB + j)),
                  kv(_KB), kv(_VB),
                  pl.BlockSpec((1, bq, 2), lambda j, i, kk: (j, i, 0)),
                  pl.BlockSpec((1, bk, 2), lambda j, i, kk: (j, kvi(i, kk), 0))],
        out_specs=(pl.BlockSpec((bq, 128), lambda j, i, kk: (i, j)),
                   pl.BlockSpec((1, 2, bq), lambda j, i, kk: (j, 0, i))),
        scratch_shapes=[pltpu.VMEM((2, bq, 128), BF16), pltpu.VMEM((2, bk, bq), F32), pltpu.VMEM((2, bk, bq), BF16),
                        pltpu.VMEM((2, 1, bq), F32), pltpu.VMEM((2, 128, bq), F32)],
        compiler_params=_cp("parallel", "parallel", "arbitrary"),
    )(proj, proj, proj, c_col, c_col)


def _attn_delta(do, o):
    t = do.shape[0]
    tm = _row_tile(t)

    def body(do_ref, o_ref, d_ref):
        pick = (jnp.right_shift(_iota((D_ATT, 128), 0), 6) == _iota((D_ATT, 128), 1)).astype(F32)
        d_ref[...] = _dot_exact(do_ref[...] * o_ref[...], pick)

    row = pl.BlockSpec((tm, D_ATT), lambda i: (i, 0))
    return pl.pallas_call(
        body, name="attn_delta",
        out_shape=jax.ShapeDtypeStruct((t, 128), F32),
        grid=(t // tm,), in_specs=[row, row], out_specs=pl.BlockSpec((tm, 128), lambda i: (i, 0)),
        compiler_params=_cp("parallel"),
    )(do, o)


def _attn_bwd(proj, c_col, lse_row, dl_row, do):
    t = proj.shape[0]
    bq, bk = _attn_blocks(t)
    nq, nk = t // bq, t // bk
    rs = 16

    def first_q(kk):
        return (kk * bk) // bq

    def body(q_ref, k_ref, v_ref, cq_ref, ck_ref, lse_ref, dl_ref, do_ref,
             dq_ref, dk_ref, dv_ref, dck_ref, dcq_ref,
             qs_scr, doh_scr, ks_scr, s_scr, dp_scr, p_scr, ds_scr, dq_scr, dk_scr, dv_scr):
        kk = pl.program_id(1)
        i = pl.program_id(2)
        lane_q = _iota((bq, 128), 1)
        lane_k = _iota((bk, 128), 1)
        qrows = pl.ds(pl.multiple_of(i * bq, 128), bq)

        @pl.when(kk == 0)
        def _():
            q = q_ref[...] * _SCALE
            cq = cq_ref[0]
            do_ = do_ref[...]
            for hh in range(2):
                qs_scr[hh, qrows, :] = _q_operand(q, cq[:, hh:hh + 1], lane_q, hh)
                doh_scr[hh, qrows, :] = jnp.where(_head_lanes(lane_q, hh)[0], do_, 0.0).astype(BF16)
                dq_scr[hh, qrows, :] = jnp.zeros((bq, 128), F32)

        @pl.when(i == 0)
        def _():
            dk_scr[...] = jnp.zeros_like(dk_scr)
            dv_scr[...] = jnp.zeros_like(dv_scr)
            k = k_ref[...]
            ck = ck_ref[0]
            for hh in range(2):
                ks_scr[hh] = _k_operand(k, ck[:, hh:hh + 1], lane_k, hh)

        def step(masked):
            v16 = v_ref[...].astype(BF16)
            lse = lse_ref[0]
            dl = dl_ref[0]
            ahead = _iota((rs, bq), 0) - _iota((rs, bq), 1)
            for hh in range(2):
                qs = qs_scr[hh, qrows, :]
                doh = doh_scr[hh, qrows, :]
                s_scr[hh] = _dot(ks_scr[hh], qs, _NT)
                dp_scr[hh] = _dot(v16, doh, _NT)

                def strip(r, carry):
                    rows = pl.ds(pl.multiple_of(r * rs, rs), rs)
                    p = jnp.exp(s_scr[hh, rows, :] - lse[hh:hh + 1, :])
                    if masked:
                        p = jnp.where(ahead <= i * bq - kk * bk - r * rs, p, 0.0)
                    p_scr[hh, rows, :] = p.astype(BF16)
                    ds_scr[hh, rows, :] = (p * (dp_scr[hh, rows, :] - dl[hh:hh + 1, :])).astype(BF16)
                    return carry

                lax.fori_loop(0, bk // rs, strip, 0, unroll=True)
                dv_scr[...] += _dot(p_scr[hh], doh)
                dk_scr[hh] += _dot(ds_scr[hh], qs)
                dq_scr[hh, qrows, :] += _dot(ds_scr[hh], ks_scr[hh], _TN)

        active = i >= first_q(kk)
        masked = _needs_mask(i, kk, bq, bk)

        @pl.when(active & masked)
        def _():
            step(True)

        @pl.when(active & jnp.logical_not(masked))
        def _():
            step(False)

        @pl.when(i == nq - 1)
        def _():
            dka = dk_scr[0]
            dkb = dk_scr[1]
            dk_ref[...] = jnp.where(lane_k < 64, dka, dkb).astype(BF16)
            dv_ref[...] = dv_scr[...].astype(BF16)
            dck_ref[0] = -jnp.where(_iota((bk, 2), 1) == 0, dka[:, 67:68], dkb[:, 3:4])

        @pl.when((kk == nk - 1) & (i == nq - 1))
        def _():
            lane_t = _iota((t, 128), 1)
            dqa = dq_scr[0]
            dqb = dq_scr[1]
            dq_ref[...] = (jnp.where(lane_t < 64, dqa, dqb) * _SCALE).astype(BF16)
            dcq_ref[0] = jnp.where(_iota((t, 2), 1) == 0, dqa[:, 64:65], dqb[:, 0:1])

    qi = lambda kk, i: jnp.where(kk == 0, i, nq - 1)
    qspec = lambda off: pl.BlockSpec((bq, 128), lambda j, kk, i: (qi(kk, i), off + j))
    kspec = lambda off: pl.BlockSpec((bk, 128), lambda j, kk, i: (kk, off + j))
    rowspec = pl.BlockSpec((1, 2, bq), lambda j, kk, i: (j, 0, jnp.maximum(i, first_q(kk))))
    return pl.pallas_call(
        body, name="attn_bwd",
        out_shape=(jax.ShapeDtypeStruct((t, D_ATT), BF16), jax.ShapeDtypeStruct((t, D_ATT), BF16),
                   jax.ShapeDtypeStruct((t, D_ATT), BF16), jax.ShapeDtypeStruct((_NPAIR, t, 2), F32),
                   jax.ShapeDtypeStruct((_NPAIR, t, 2), F32)),
        grid=(_NPAIR, nk, nq),
        in_specs=[qspec(_QB), kspec(_KB), kspec(_VB),
                  pl.BlockSpec((1, bq, 2), lambda j, kk, i: (j, qi(kk, i), 0)),
                  pl.BlockSpec((1, bk, 2), lambda j, kk, i: (j, kk, 0)),
                  rowspec, rowspec, qspec(0)],
        out_specs=(pl.BlockSpec((t, 128), lambda j, kk, i: (0, j)),
                   pl.BlockSpec((bk, 128), lambda j, kk, i: (kk, j)),
                   pl.BlockSpec((bk, 128), lambda j, kk, i: (kk, j)),
                   pl.BlockSpec((1, bk, 2), lambda j, kk, i: (j, kk, 0)),
                   pl.BlockSpec((1, t, 2), lambda j, kk, i: (j, 0, 0))),
        scratch_shapes=[pltpu.VMEM((2, t, 128), BF16), pltpu.VMEM((2, t, 128), BF16), pltpu.VMEM((2, bk, 128), BF16),
                        pltpu.VMEM((2, bk, bq), F32), pltpu.VMEM((2, bk, bq), F32),
                        pltpu.VMEM((2, bk, bq), BF16), pltpu.VMEM((2, bk, bq), BF16),
                        pltpu.VMEM((2, t, 128), F32), pltpu.VMEM((2, bk, 128), F32), pltpu.VMEM((bk, 128), F32)],
        compiler_params=_cp("parallel", "arbitrary", "arbitrary"),
    )(proj, proj, proj, c_col, c_col, lse_row, dl_row, do)


def _premerge_fwd(y, o, proj, gamma):
    t = y.shape[0]
    tm = _row_tile_wide(t)

    def body(y_ref, z_ref, o_ref, za_ref, g_ref, ys_ref, ya_ref):
        z = z_ref[...]
        u = y_ref[...] * (z * _sigmoid(z))
        for g in range(G_SSD):
            gs = slice(_GW * g, _GW * (g + 1))
            ug = u[:, gs]
            r = lax.rsqrt(jnp.mean(ug * ug, axis=-1, keepdims=True) + EPS)
            ys_ref[:, gs] = (ug * r * g_ref[:, gs]).astype(BF16)
        za = za_ref[...]
        ya_ref[...] = (o_ref[...] * (za * _sigmoid(za))).astype(BF16)

    return pl.pallas_call(
        body, name="premerge_fwd",
        out_shape=(jax.ShapeDtypeStruct((t, D_SSD), BF16), jax.ShapeDtypeStruct((t, D_ATT), BF16)),
        grid=(t // tm,),
        in_specs=[pl.BlockSpec((tm, D_SSD), lambda i: (i, 0)),
                  pl.BlockSpec((tm, D_SSD), lambda i: (i, C_Z // D_SSD)),
                  pl.BlockSpec((tm, D_ATT), lambda i: (i, 0)),
                  pl.BlockSpec((tm, D_ATT), lambda i: (i, C_ZA // D_ATT)),
                  pl.BlockSpec((1, D_SSD), lambda i: (0, 0))],
        out_specs=(pl.BlockSpec((tm, D_SSD), lambda i: (i, 0)), pl.BlockSpec((tm, D_ATT), lambda i: (i, 0))),
        compiler_params=_cp("parallel"),
    )(y, proj, o, proj, gamma)


def _premerge_bwd(dys, dya, y, o, proj, gamma):
    t = y.shape[0]
    tm = _row_tile_wide(t)

    def body(dys_ref, dya_ref, y_ref, z_ref, o_ref, za_ref, g_ref, dy_ref, dz_ref, do_ref, dza_ref, dg_ref):
        i = pl.program_id(0)
        z = z_ref[...]
        sz = _sigmoid(z)
        silu = z * sz
        dsilu = sz * (1.0 + z * (1.0 - sz))
        yv = y_ref[...]
        u = yv * silu
        parts = []
        for g in range(G_SSD):
            gs = slice(_GW * g, _GW * (g + 1))
            ug = u[:, gs]
            r = lax.rsqrt(jnp.mean(ug * ug, axis=-1, keepdims=True) + EPS)
            n = ug * r
            dout = dys_ref[:, gs]
            dn = dout * g_ref[:, gs]
            du = r * (dn - n * jnp.mean(dn * n, axis=-1, keepdims=True))
            dy_ref[:, gs] = du * silu[:, gs]
            dz_ref[:, gs] = (du * yv[:, gs] * dsilu[:, gs]).astype(BF16)
            parts.append(jnp.sum(dout * n, axis=0, keepdims=True))
        dg = jnp.concatenate(parts, axis=1)
        za = za_ref[...]
        sa = _sigmoid(za)
        dya_ = dya_ref[...]
        do_ref[...] = dya_ * (za * sa)
        dza_ref[...] = (dya_ * o_ref[...] * (sa * (1.0 + za * (1.0 - sa)))).astype(BF16)

        @pl.when(i == 0)
        def _():
            dg_ref[...] = dg

        @pl.when(i > 0)
        def _():
            dg_ref[...] += dg

    ssd = pl.BlockSpec((tm, D_SSD), lambda i: (i, 0))
    att = pl.BlockSpec((tm, D_ATT), lambda i: (i, 0))
    vec = pl.BlockSpec((1, D_SSD), lambda i: (0, 0))
    return pl.pallas_call(
        body, name="premerge_bwd",
        out_shape=(jax.ShapeDtypeStruct((t, D_SSD), F32), jax.ShapeDtypeStruct((t, D_SSD), BF16),
                   jax.ShapeDtypeStruct((t, D_ATT), F32), jax.ShapeDtypeStruct((t, D_ATT), BF16),
                   jax.ShapeDtypeStruct((1, D_SSD), F32)),
        grid=(t // tm,),
        in_specs=[ssd, att, ssd, pl.BlockSpec((tm, D_SSD), lambda i: (i, C_Z // D_SSD)), att,
                  pl.BlockSpec((tm, D_ATT), lambda i: (i, C_ZA // D_ATT)), vec],
        out_specs=(ssd, ssd, att, att, vec),
        compiler_params=_cp("arbitrary"),
    )(dys, dya, y, proj, o, proj, gamma)


_G_BLK = C_G // D_MODEL


def _merge_fwd(a, b, proj, gate_bias):
    t = a.shape[0]
    tm = _row_tile(t)

    def body(a_ref, b_ref, gs_ref, ga_ref, bias_ref, m_ref):
        g_ssd = _sigmoid(gs_ref[...] + bias_ref[:, 0:D_MODEL])
        g_att = _sigmoid(ga_ref[...] + bias_ref[:, D_MODEL:2 * D_MODEL])
        m_ref[...] = (g_ssd * a_ref[...] + g_att * b_ref[...]).astype(BF16)

    row = pl.BlockSpec((tm, D_MODEL), lambda i: (i, 0))
    return pl.pallas_call(
        body, name="merge_fwd",
        out_shape=jax.ShapeDtypeStruct((t, D_MODEL), BF16),
        grid=(t // tm,),
        in_specs=[row, row,
                  pl.BlockSpec((tm, D_MODEL), lambda i: (i, _G_BLK)),
                  pl.BlockSpec((tm, D_MODEL), lambda i: (i, _G_BLK + 1)),
                  pl.BlockSpec((1, 2 * D_MODEL), lambda i: (0, 0))],
        out_specs=row,
        compiler_params=_cp("parallel"),
    )(a, b, proj, proj, gate_bias)


def _merge_bwd(dm, a, b, proj, gate_bias):
    t = a.shape[0]
    tm = _row_tile(t)

    def body(dm_ref, a_ref, b_ref, gs_ref, ga_ref, bias_ref, da_ref, db_ref, dg_ref, dbias_ref):
        i = pl.program_id(0)
        dm_ = dm_ref[...]
        g_ssd = _sigmoid(gs_ref[...] + bias_ref[:, 0:D_MODEL])
        g_att = _sigmoid(ga_ref[...] + bias_ref[:, D_MODEL:2 * D_MODEL])
        da_ref[...] = (dm_ * g_ssd).astype(BF16)
        db_ref[...] = (dm_ * g_att).astype(BF16)
        dgs = dm_ * a_ref[...] * g_ssd * (1.0 - g_ssd)
        dga = dm_ * b_ref[...] * g_att * (1.0 - g_att)
        dg_ref[:, 0:D_MODEL] = dgs.astype(BF16)
        dg_ref[:, D_MODEL:2 * D_MODEL] = dga.astype(BF16)
        part = jnp.concatenate([jnp.sum(dgs, axis=0, keepdims=True), jnp.sum(dga, axis=0, keepdims=True)], axis=1)

        @pl.when(i == 0)
        def _():
            dbias_ref[...] = part

        @pl.when(i > 0)
        def _():
            dbias_ref[...] += part

    row = pl.BlockSpec((tm, D_MODEL), lambda i: (i, 0))
    wide = pl.BlockSpec((tm, 2 * D_MODEL), lambda i: (i, 0))
    vec = pl.BlockSpec((1, 2 * D_MODEL), lambda i: (0, 0))
    return pl.pallas_call(
        body, name="merge_bwd",
        out_shape=(jax.ShapeDtypeStruct((t, D_MODEL), BF16), jax.ShapeDtypeStruct((t, D_MODEL), BF16),
                   jax.ShapeDtypeStruct((t, 2 * D_MODEL), BF16), jax.ShapeDtypeStruct((1, 2 * D_MODEL), F32)),
        grid=(t // tm,),
        in_specs=[row, row, row,
                  pl.BlockSpec((tm, D_MODEL), lambda i: (i, _G_BLK)),
                  pl.BlockSpec((tm, D_MODEL), lambda i: (i, _G_BLK + 1)), vec],
        out_specs=(row, row, wide, vec),
        compiler_params=_cp("arbitrary"),
    )(dm, a, b, proj, proj, gate_bias)


def _post(o2, h, target, g):
    t = o2.shape[0]
    nc = t // CHUNK

    def body(o_ref, h_ref, t_ref, g_ref, dy_ref, do_ref, dg_ref, loss_ref):
        c = pl.program_id(0)
        x = o_ref[...]
        r = lax.rsqrt(jnp.mean(x * x, axis=-1, keepdims=True) + EPS)
        n = x * r
        y = h_ref[...] + n * g_ref[...]
        diff = jnp.where(c > 0, y - t_ref[...], 0.0)
        dy = diff * (1.0 / D_MODEL)
        dy_ref[...] = dy
        gdy = dy * g_ref[...]
        do_ref[...] = (r * (gdy - n * jnp.mean(gdy * n, axis=-1, keepdims=True))).astype(BF16)
        dg = jnp.sum(dy * n, axis=0, keepdims=True)
        lpart = 0.5 * jnp.sum(jnp.sum(diff * diff, axis=1, keepdims=True), axis=0, keepdims=True) * (1.0 / D_MODEL)
        sel = (_iota((8, 128), 0) == 0) & (_iota((8, 128), 1) == 0)

        @pl.when(c == 0)
        def _():
            dg_ref[...] = dg
            loss_ref[...] = jnp.zeros_like(loss_ref)

        @pl.when(c > 0)
        def _():
            dg_ref[...] += dg
            loss_ref[...] += jnp.where(sel, lpart, 0.0)

    row = pl.BlockSpec((CHUNK, D_MODEL), lambda c: (c, 0))
    vec = pl.BlockSpec((1, D_MODEL), lambda c: (0, 0))
    return pl.pallas_call(
        body, name="post",
        out_shape=(jax.ShapeDtypeStruct((t, D_MODEL), F32), jax.ShapeDtypeStruct((t, D_MODEL), BF16),
                   jax.ShapeDtypeStruct((1, D_MODEL), F32), jax.ShapeDtypeStruct((8, 128), F32)),
        grid=(nc,),
        in_specs=[row, row, pl.BlockSpec((CHUNK, D_MODEL), lambda c: (jnp.maximum(c - 1, 0), 0)), vec],
        out_specs=(row, row, vec, pl.BlockSpec((8, 128), lambda c: (0, 0))),
        compiler_params=_cp("arbitrary"),
    )(o2, h, target, g)


def _mm_tiles(t):
    return _tile(t, (704, 384, 128))


def _local_step(h, target, w_main, w_small, wps, wpa, wout, norm_pre, conv_w, conv_b, bias_row, a_row,
                dsk_row, ssd_norm, gate_bias, norm_post):
    t = h.shape[0]
    tm = _mm_tiles(t)
    u = _norm1_fwd(h, norm_pre)
    proj = _matmul(u, w_main, "nt", F32, "inproj", tm, 1024, D_MODEL)
    small = _matmul(u, w_small, "nt", F32, "inproj_small", tm, N_SMALL, D_MODEL)
    dtlf = _small_fwd(small, bias_row)
    xbc = _conv_fwd(proj, conv_w, conv_b)
    y, hin = _ssd_fwd(xbc, dtlf, a_row, dsk_row)
    c_tok = dtlf[:, H_SSD:H_SSD + H_ATT]
    c_tok = jnp.where(jnp.arange(t)[:, None] < PADF, _C_FILLER, c_tok)
    c_col = c_tok.reshape(t, _NPAIR, 2).transpose(1, 0, 2)
    o, lse = _attn_fwd(proj, c_col)
    ys, ya = _premerge_fwd(y, o, proj, ssd_norm)
    a = _matmul(ys, wps, "nn", F32, "proj_ssd", tm, D_MODEL, D_SSD)
    b = _matmul(ya, wpa, "nn", F32, "proj_att", tm, D_MODEL, D_ATT)
    merged = _merge_fwd(a, b, proj, gate_bias)
    o2 = _matmul(merged, wout, "nn", F32, "out_proj", tm, D_MODEL, D_MODEL)
    dy_out, do2, d_norm_post, loss_blk = _post(o2, h, target, norm_post)

    dm = _matmul(do2, wout, "nt", F32, "out_proj_dx", tm, D_MODEL, D_MODEL)
    d_wout = _matmul(merged, do2, "tn", F32, "out_proj_dw", D_MODEL, D_MODEL, tm)
    da, db, dgraw, d_gate_bias = _merge_bwd(dm, a, b, proj, gate_bias)
    dys = _matmul(da, wps, "nt", F32, "proj_ssd_dx", tm, D_SSD, D_MODEL)
    d_wps = _matmul(ys, da, "tn", F32, "proj_ssd_dw", D_SSD, D_MODEL, tm)
    dya = _matmul(db, wpa, "nt", F32, "proj_att_dx", tm, D_ATT, D_MODEL)
    d_wpa = _matmul(ya, db, "tn", F32, "proj_att_dw", D_ATT, D_MODEL, tm)
    dy, dz, do, dza, d_ssd_norm = _premerge_bwd(dys, dya, y, o, proj, ssd_norm)
    dl_row = _attn_delta(do, o)[:, 0:H_ATT].T.reshape(_NPAIR, 2, t)
    dq, dk, dv, dc_key, dc_qry = _attn_bwd(proj, c_col, lse, dl_row, do)
    dxbc, ddt, d_a, d_dsk = _ssd_bwd(xbc, dtlf, a_row, dsk_row, hin, dy)
    dact, d_conv_w, d_conv_b = _conv_bwd_act(dxbc, proj, conv_w, conv_b)
    dxbc_raw = _conv_bwd_in(dact, conv_w)
    dc_tok = jnp.transpose(dc_key + dc_qry, (1, 0, 2)).reshape(t, H_ATT)
    dsm = ddt + jnp.pad(dc_tok, ((0, 0), (H_SSD, N_SMALL - H_SSD - H_ATT)))
    dsmall, d_bias_row = _small_bwd(dsm, small, bias_row)
    dproj = [dz, dxbc_raw, dza, dq, dk, dv, dgraw]
    du_a = _matmul_cat_nn(dproj, w_main, "inproj_dx", tm)
    du_b = _matmul(dsmall, w_small, "nn", F32, "inproj_small_dx", tm, D_MODEL, N_SMALL)
    d_w_main = _matmul_cat_tn(dproj, u, "inproj_dw", tm)
    d_w_small = _matmul(dsmall, u, "tn", F32, "inproj_small_dw", N_SMALL, D_MODEL, tm)
    dh, d_norm_pre = _norm1_bwd(du_a, du_b, h, norm_pre, dy_out)
    return dict(loss_blk=loss_blk, dh=dh, d_w_main=d_w_main, d_w_small=d_w_small, d_wps=d_wps, d_wpa=d_wpa,
                d_wout=d_wout, d_norm_pre=d_norm_pre, d_conv_w=d_conv_w, d_conv_b=d_conv_b,
                d_bias_row=d_bias_row, d_a=d_a, d_dsk=d_dsk, d_ssd_norm=d_ssd_norm,
                d_gate_bias=d_gate_bias, d_norm_post=d_norm_post)


def _to_aligned_rows(w):
    def cut(o):
        return w[o[0]:o[0] + o[1]]
    main = jnp.concatenate([cut(O_Z), cut(O_XBC), cut(O_ZA), cut(O_Q), cut(O_K), cut(O_V), cut(O_G)], axis=0)
    pad = jnp.zeros((N_SMALL - H_SSD - H_ATT, w.shape[1]), w.dtype)
    small = jnp.concatenate([cut(O_DT), cut(O_F), pad], axis=0)
    return main, small


def _from_aligned_rows(main, small):
    def cm(c0, n):
        return main[c0:c0 + n]
    return jnp.concatenate([cm(C_Z, 2048), cm(C_XBC, 3072), small[0:H_SSD], cm(C_ZA, 1024),
                            cm(C_Q, 1024), cm(C_K, 1024), cm(C_V, 1024), small[H_SSD:H_SSD + H_ATT],
                            cm(C_G, 2048)], axis=0)


_MESH = pl.DeviceIdType.MESH
_ANY = pl.BlockSpec(memory_space=pl.ANY)
_VM = pl.BlockSpec(memory_space=pltpu.VMEM)
_HALF = 512
N_DEV = 8


def _coords():
    return lax.axis_index("x"), lax.axis_index("y"), lax.axis_index("c")


def _other_chips(x, y):
    return [(1 - x, y), (x, 1 - y), (1 - x, 1 - y)]


def _half(cc):
    return pl.ds(cc * _HALF, _HALF)


def _gather_shards(slots):
    n = len(slots)

    def body(*refs):
        buf = refs[n:2 * n]
        send_sems, recv_sems = refs[2 * n:]
        x, y, c = _coords()
        chip = 2 * x + y
        sibling = (x, y, 1 - c)
        chips = _other_chips(x, y)

        def copy(i, frm, cc, k, to):
            part = buf[i].at[frm, :, _half(cc)]
            return pltpu.make_async_remote_copy(src_ref=part, dst_ref=part, send_sem=send_sems.at[6 * i + k],
                                                recv_sem=recv_sems.at[6 * i + k], device_id=to, device_id_type=_MESH)

        def chip_of(k):
            return 2 * chips[k][0] + chips[k][1]

        first = [copy(i, chip, c, k, (*chips[k], c)) for k in range(3) for i in range(n)]
        for cp in first:
            cp.start()
        passed = []
        for k in range(3):
            for i in range(n):
                copy(i, chip_of(k), c, k, (*chips[k], c)).wait_recv()
                passed.append(copy(i, chip_of(k), c, 3 + k, sibling))
                passed[-1].start()
        for k in range(3):
            for i in range(n):
                copy(i, chip_of(k), 1 - c, 3 + k, sibling).wait_recv()
        for cp in first + passed:
            cp.wait_send()

    return pl.pallas_call(
        body, name="gather_shards",
        out_shape=tuple(jax.ShapeDtypeStruct(s.shape, s.dtype) for s in slots),
        in_specs=[_ANY] * n, out_specs=tuple([_ANY] * n),
        input_output_aliases={i: i for i in range(n)},
        scratch_shapes=[pltpu.SemaphoreType.DMA((6 * n,)), pltpu.SemaphoreType.DMA((6 * n,))],
    )(*slots)


def _allgather8(block, name):
    rows, width = block.shape

    def body(x_ref, out_ref, send_sems, recv_sems, local_sem):
        x, y, c = _coords()
        me, sibling = (x, y, c), (x, y, 1 - c)
        chips = _other_chips(x, y)

        def slot(px, py, pc):
            return out_ref.at[4 * px + 2 * py + pc]

        def copy(k, blk, to, src=None):
            return pltpu.make_async_remote_copy(src_ref=slot(*blk) if src is None else src, dst_ref=slot(*blk),
                                                send_sem=send_sems.at[k], recv_sem=recv_sems.at[k],
                                                device_id=to, device_id_type=_MESH)

        mine = pltpu.make_async_copy(x_ref, slot(*me), local_sem)
        mine.start()
        first = [copy(0, me, sibling, src=x_ref)]
        first += [copy(1 + j, me, (*chip, c), src=x_ref) for j, chip in enumerate(chips)]
        for cp in first:
            cp.start()
        passed = [copy(4 + j, (*chip, c), sibling) for j, chip in enumerate(chips)]
        for j, chip in enumerate(chips):
            copy(1 + j, (*chip, c), me).wait_recv()
            passed[j].start()
        copy(0, sibling, me).wait_recv()
        for j, chip in enumerate(chips):
            copy(4 + j, (*chip, 1 - c), me).wait_recv()
        for cp in first + passed:
            cp.wait_send()
        mine.wait()

    return pl.pallas_call(
        body, name=name,
        out_shape=jax.ShapeDtypeStruct((N_DEV, rows, width), block.dtype),
        in_specs=[_VM], out_specs=_VM,
        scratch_shapes=[pltpu.SemaphoreType.DMA((7,)), pltpu.SemaphoreType.DMA((7,)), pltpu.SemaphoreType.DMA],
    )(block)


def _pair_swap_halves(arrs):
    n = len(arrs)

    def body(*refs):
        src, dst = refs[:n], refs[n:2 * n]
        send_sems, recv_sems = refs[2 * n:]
        x, y, c = _coords()
        cps = [pltpu.make_async_remote_copy(src_ref=src[i].at[:, :, _half(1 - c)], dst_ref=dst[i],
                                            send_sem=send_sems.at[i], recv_sem=recv_sems.at[i],
                                            device_id=(x, y, 1 - c), device_id_type=_MESH) for i in range(n)]
        for cp in cps:
            cp.start()
        for cp in cps:
            cp.wait()

    return pl.pallas_call(
        body, name="pair_swap_halves",
        out_shape=tuple(jax.ShapeDtypeStruct((4, a.shape[1], _HALF), a.dtype) for a in arrs),
        in_specs=[_ANY] * n, out_specs=tuple([_ANY] * n),
        scratch_shapes=[pltpu.SemaphoreType.DMA((n,)), pltpu.SemaphoreType.DMA((n,))],
    )(*arrs)


def _chip_exchange(arrs):
    n = len(arrs)

    def body(*refs):
        src, dst = refs[:n], refs[n:2 * n]
        send_sems, recv_sems = refs[2 * n:]
        x, y, c = _coords()
        chips = _other_chips(x, y)
        cps = [pltpu.make_async_remote_copy(src_ref=src[i].at[2 * chips[k][0] + chips[k][1]], dst_ref=dst[i].at[k],
                                            send_sem=send_sems.at[3 * i + k], recv_sem=recv_sems.at[3 * i + k],
                                            device_id=(*chips[k], c), device_id_type=_MESH)
               for k in range(3) for i in range(n)]
        for cp in cps:
            cp.start()
        for cp in cps:
            cp.wait()

    return pl.pallas_call(
        body, name="chip_exchange",
        out_shape=tuple(jax.ShapeDtypeStruct((3,) + a.shape[1:], a.dtype) for a in arrs),
        in_specs=[_ANY] * n, out_specs=tuple([_ANY] * n),
        scratch_shapes=[pltpu.SemaphoreType.DMA((3 * n,)), pltpu.SemaphoreType.DMA((3 * n,))],
    )(*arrs)


def _pair_join_halves(fulls):
    n = len(fulls)

    def body(*refs):
        buf = refs[n:2 * n]
        send_sems, recv_sems = refs[2 * n:]
        x, y, c = _coords()

        def remote(i, cc):
            part = buf[i].at[:, _half(cc)]
            return pltpu.make_async_remote_copy(src_ref=part, dst_ref=part, send_sem=send_sems.at[i],
                                                recv_sem=recv_sems.at[i], device_id=(x, y, 1 - c), device_id_type=_MESH)

        for i in range(n):
            remote(i, c).start()
        for i in range(n):
            remote(i, c).wait_send()
            remote(i, 1 - c).wait_recv()

    return pl.pallas_call(
        body, name="pair_join_halves",
        out_shape=tuple(jax.ShapeDtypeStruct(a.shape, a.dtype) for a in fulls),
        in_specs=[_ANY] * n, out_specs=tuple([_ANY] * n),
        input_output_aliases={i: i for i in range(n)},
        scratch_shapes=[pltpu.SemaphoreType.DMA((n,)), pltpu.SemaphoreType.DMA((n,))],
    )(*fulls)


_RED_TC = 128
_RED_NT = _HALF // _RED_TC


def _add_pair(ids, g32, recv_a):
    rows = g32.shape[1]

    def body(ids_ref, g_ref, r_ref, o_ref):
        o_ref[...] = (g_ref[...] + r_ref[...]).astype(BF16)

    blk = pl.BlockSpec((1, rows, _RED_TC), lambda j, l, ids: (j, 0, l))
    return pl.pallas_call(
        body, name="add_pair",
        out_shape=jax.ShapeDtypeStruct((4, rows, _HALF), BF16),
        grid_spec=pltpu.PrefetchScalarGridSpec(
            num_scalar_prefetch=1, grid=(4, _RED_NT),
            in_specs=[pl.BlockSpec((1, rows, _RED_TC), lambda j, l, ids: (j, 0, ids[0] * _RED_NT + l)), blk],
            out_specs=blk),
        compiler_params=_cp("parallel", "parallel"),
    )(ids, g32, recv_a)


def _add_chips(ids, g32, recv_a, recv_b):
    rows = g32.shape[1]

    def body(ids_ref, g_ref, a_ref, b_ref, o_ref):
        acc = g_ref[0] + a_ref[0]
        for k in range(3):
            acc = acc + b_ref[k].astype(F32)
        o_ref[...] = acc

    return pl.pallas_call(
        body, name="add_chips",
        out_shape=jax.ShapeDtypeStruct((rows, 2 * _HALF), F32),
        grid_spec=pltpu.PrefetchScalarGridSpec(
            num_scalar_prefetch=1, grid=(_RED_NT,),
            in_specs=[pl.BlockSpec((1, rows, _RED_TC), lambda l, ids: (ids[1], 0, ids[0] * _RED_NT + l)),
                      pl.BlockSpec((1, rows, _RED_TC), lambda l, ids: (ids[1], 0, l)),
                      pl.BlockSpec((3, rows, _RED_TC), lambda l, ids: (0, 0, l))],
            out_specs=pl.BlockSpec((rows, _RED_TC), lambda l, ids: (0, ids[0] * _RED_NT + l))),
        compiler_params=_cp("parallel"),
    )(ids, g32, recv_a, recv_b)


def _sum8(gathered):
    _, rows, width = gathered.shape

    def body(g_ref, o_ref):
        acc = g_ref[0]
        for d in range(1, N_DEV):
            acc = acc + g_ref[d]
        o_ref[...] = acc

    return pl.pallas_call(
        body, name="sum8",
        out_shape=jax.ShapeDtypeStruct((rows, width), F32),
        in_specs=[_VM], out_specs=_VM,
    )(gathered)


def _adamw(w, g, m, v, name):
    rows, cols = w.shape
    budget = (3 << 20) // 2
    tr, tc = rows, cols
    if rows * cols * 4 > budget:
        if rows % 8 == 0:
            tr = next(c for c in (512, 256, 128, 64, 32, 16, 8) if rows % c == 0 and c * cols * 4 <= budget)
        else:
            tc = next(c for c in (512, 256, 128) if cols % c == 0 and rows * c * 4 <= budget)
    c1 = 1.0 - ADAM_B1 ** ADAM_STEP
    c2 = 1.0 - ADAM_B2 ** ADAM_STEP

    def body(w_ref, g_ref, m_ref, v_ref, d_ref, mo_ref, vo_ref):
        gg = g_ref[...]
        mn = ADAM_B1 * m_ref[...] + (1.0 - ADAM_B1) * gg
        vn = ADAM_B2 * v_ref[...] + (1.0 - ADAM_B2) * (gg * gg)
        mo_ref[...] = mn
        vo_ref[...] = vn
        d_ref[...] = -ADAM_LR * ((mn / c1) / (jnp.sqrt(vn / c2) + ADAM_EPS) + ADAM_WD * w_ref[...])

    blk = pl.BlockSpec((tr, tc), lambda i, j: (i, j))
    shp = jax.ShapeDtypeStruct((rows, cols), F32)
    return pl.pallas_call(
        body, name=name, out_shape=(shp, shp, shp), grid=(rows // tr, cols // tc),
        in_specs=[blk] * 4, out_specs=(blk, blk, blk),
        compiler_params=_cp("parallel", "parallel"),
    )(w, g, m, v)


def _rows128(a):
    return a.reshape(-1, 128)


def _pack_small(norm_pre, conv_b, ssd_norm, gate_bias, norm_post, dt_bias, a_log, d_skip, fgate_bias):
    tiny = jnp.concatenate([dt_bias.reshape(-1), a_log.reshape(-1), d_skip.reshape(-1), fgate_bias.reshape(-1),
                            jnp.zeros((16,), F32)])
    return jnp.concatenate([_rows128(norm_pre), _rows128(conv_b), _rows128(ssd_norm), _rows128(gate_bias),
                            _rows128(norm_post), tiny.reshape(1, 128)], axis=0)


_SMALL_ROWS = 73
_SMALL_PAD = 80


def _unpack_small(p):
    tiny = p[72]
    return dict(norm_pre=p[0:8].reshape(1, 1024), conv_b=p[8:32].reshape(1, 3072), ssd_norm=p[32:48].reshape(1, 2048),
                gate_bias=p[48:64].reshape(1, 2048), norm_post=p[64:72].reshape(1, 1024),
                dt_bias=tiny[0:32].reshape(1, 32), a_log=tiny[32:64].reshape(1, 32),
                d_skip=tiny[64:96].reshape(1, 32), fgate_bias=tiny[96:112].reshape(1, 16))


def _pad_rows(a, rows):
    return jnp.concatenate([a, jnp.zeros((rows - a.shape[0], a.shape[1]), a.dtype)], axis=0)


def kernel(x, meta_tokens, norm_pre, w_in, conv_w, conv_b, dt_bias, a_log, d_skip, ssd_norm, fgate_bias, gate_bias, w_proj_ssd, w_proj_att, w_out, norm_post, loss_target, m_meta_tokens, m_norm_pre, m_w_in, m_conv_w, m_conv_b, m_dt_bias, m_a_log, m_d_skip, m_ssd_norm, m_fgate_bias, m_gate_bias, m_w_proj_ssd, m_w_proj_att, m_w_out, m_norm_post, v_meta_tokens, v_norm_pre, v_w_in, v_conv_w, v_conv_b, v_dt_bias, v_a_log, v_d_skip, v_ssd_norm, v_fgate_bias, v_gate_bias, v_w_proj_ssd, v_w_proj_att, v_w_out, v_norm_post):
    cx, cy, cc = _coords()
    chip = 2 * cx + cy
    ids = jnp.stack([cc, chip]).astype(jnp.int32)
    seq = x.shape[1]

    w_in_sh = jnp.transpose(w_in[0]).astype(BF16)
    w_pr_sh = jnp.concatenate([w_proj_ssd[0], w_proj_att[0], w_out[0]], axis=0).astype(BF16)

    def own_slot(sh):
        return lax.dynamic_update_slice(lax.empty((4,) + sh.shape, sh.dtype), sh[None], (chip, 0, 0))

    g_in, g_pr = _gather_shards([own_slot(w_in_sh), own_slot(w_pr_sh)])
    w_main, w_small = _to_aligned_rows(g_in.reshape(N_COLS, D_MODEL))
    wps = g_pr[:, 0:512].reshape(D_SSD, D_MODEL)
    wpa = g_pr[:, 512:768].reshape(D_ATT, D_MODEL)
    wout = g_pr[:, 768:1024].reshape(D_MODEL, D_MODEL)
    sm_sh = jnp.concatenate([_rows128(meta_tokens), _rows128(conv_w[0])], axis=0)
    sm_all = _allgather8(sm_sh, "gather_small_weights")[0::2]
    meta_full = jnp.transpose(sm_all[:, 0:32].reshape(4, N_META, 256), (1, 0, 2)).reshape(N_META, D_MODEL)
    conv_w_full = jnp.transpose(sm_all[:, 32:56].reshape(4, CONV_K, 768), (1, 0, 2)).reshape(CONV_K, CONV_DIM)

    h = jnp.concatenate([jnp.zeros((PADF, D_MODEL), F32), meta_full, x[0]], axis=0)
    bias_row = jnp.concatenate([dt_bias[0], fgate_bias[0], jnp.zeros((N_SMALL - H_SSD - H_ATT,), F32)]).reshape(1, N_SMALL)
    a_neg = -jnp.exp(a_log[0])
    a_row = jnp.concatenate([a_neg, jnp.zeros((N_SMALL - H_SSD,), F32)]).reshape(1, N_SMALL)
    dsk_row = jnp.repeat(d_skip[0], 64).reshape(1, D_SSD)
    r = _local_step(h, loss_target[0], w_main, w_small, wps, wpa, wout, norm_pre, conv_w_full, conv_b, bias_row,
                    a_row, dsk_row, ssd_norm, gate_bias, norm_post)
    dh = r["dh"]
    grad_x = dh[PADF + N_META:].reshape(1, seq, D_MODEL)

    tiny = r["d_bias_row"][0]
    part_small = _pack_small(r["d_norm_pre"], r["d_conv_b"], r["d_ssd_norm"], r["d_gate_bias"], r["d_norm_post"],
                             tiny[0:H_SSD], r["d_a"][0, 0:H_SSD] * a_neg, r["d_dsk"].reshape(H_SSD, 64).sum(axis=1),
                             tiny[H_SSD:H_SSD + H_ATT])
    part = jnp.concatenate([_pad_rows(part_small, _SMALL_PAD), _rows128(r["d_conv_w"]),
                            _rows128(dh[PADF:PADF + N_META]), r["loss_blk"]], axis=0)
    tot = _sum8(_allgather8(part, "gather_small_grads"))
    loss = tot[_SMALL_PAD + 96 + 128, 0]
    g_small = tot[0:_SMALL_PAD]
    g_conv_w = lax.dynamic_slice_in_dim(tot[_SMALL_PAD:_SMALL_PAD + 96].reshape(CONV_K, CONV_DIM), chip * 768, 768, axis=1)
    g_meta = lax.dynamic_slice_in_dim(tot[_SMALL_PAD + 96:_SMALL_PAD + 224].reshape(N_META, D_MODEL), chip * 256, 256, axis=1)

    g32_in = _from_aligned_rows(r["d_w_main"], r["d_w_small"]).reshape(4, N_COLS // 4, D_MODEL)
    g32_pr = jnp.concatenate([r["d_wps"].reshape(4, 512, D_MODEL), r["d_wpa"].reshape(4, 256, D_MODEL),
                              r["d_wout"].reshape(4, 256, D_MODEL)], axis=1)
    ra_in, ra_pr = _pair_swap_halves([g32_in, g32_pr])
    pb_in = _add_pair(ids, g32_in, ra_in)
    pb_pr = _add_pair(ids, g32_pr, ra_pr)
    rb_in, rb_pr = _chip_exchange([pb_in, pb_pr])
    half_in = _add_chips(ids, g32_in, ra_in, rb_in)
    half_pr = _add_chips(ids, g32_pr, ra_pr, rb_pr)
    gw_in, gw_pr = _pair_join_halves([half_in, half_pr])

    upd = {}
    upd["w_in"] = tuple(jnp.transpose(a) for a in (gw_in,) + _adamw(
        jnp.transpose(w_in[0]), gw_in, jnp.transpose(m_w_in[0]), jnp.transpose(v_w_in[0]), "adamw_w_in"))
    w_pr32 = jnp.concatenate([w_proj_ssd[0], w_proj_att[0], w_out[0]], axis=0)
    m_pr = jnp.concatenate([m_w_proj_ssd[0], m_w_proj_att[0], m_w_out[0]], axis=0)
    v_pr = jnp.concatenate([v_w_proj_ssd[0], v_w_proj_att[0], v_w_out[0]], axis=0)
    pr = (gw_pr,) + _adamw(w_pr32, gw_pr, m_pr, v_pr, "adamw_w_proj")
    upd["w_proj_ssd"] = tuple(a[0:512] for a in pr)
    upd["w_proj_att"] = tuple(a[512:768] for a in pr)
    upd["w_out"] = tuple(a[768:1024] for a in pr)
    upd["conv_w"] = (g_conv_w,) + _adamw(conv_w[0], g_conv_w, m_conv_w[0], v_conv_w[0], "adamw_conv_w")
    upd["meta_tokens"] = (g_meta,) + _adamw(meta_tokens, g_meta, m_meta_tokens, v_meta_tokens, "adamw_meta")
    pk = lambda np_, cb, sn, gb, npo, dtb, al, ds, fg: _pad_rows(_pack_small(np_, cb, sn, gb, npo, dtb, al, ds, fg), _SMALL_PAD)
    w_sm = pk(norm_pre, conv_b, ssd_norm, gate_bias, norm_post, dt_bias, a_log, d_skip, fgate_bias)
    m_sm = pk(m_norm_pre, m_conv_b, m_ssd_norm, m_gate_bias, m_norm_post, m_dt_bias, m_a_log, m_d_skip, m_fgate_bias)
    v_sm = pk(v_norm_pre, v_conv_b, v_ssd_norm, v_gate_bias, v_norm_post, v_dt_bias, v_a_log, v_d_skip, v_fgate_bias)
    sm = [_unpack_small(a) for a in (g_small,) + _adamw(w_sm, g_small, m_sm, v_sm, "adamw_small")]
    for name in ("norm_pre", "conv_b", "dt_bias", "a_log", "d_skip", "ssd_norm", "fgate_bias", "gate_bias", "norm_post"):
        upd[name] = tuple(s[name] for s in sm)
    lead = ("w_in", "conv_w", "w_proj_ssd", "w_proj_att", "w_out")
    order = ("meta_tokens", "norm_pre", "w_in", "conv_w", "conv_b", "dt_bias", "a_log", "d_skip", "ssd_norm",
             "fgate_bias", "gate_bias", "w_proj_ssd", "w_proj_att", "w_out", "norm_post")
    outs = [loss, grad_x]
    for part_i in range(4):
        for name in order:
            a = upd[name][part_i]
            outs.append(a[None] if name in lead else a)
    return tuple(outs)
```

```python
import functools
import math

import jax
import jax.numpy as jnp
from jax import lax
from jax.experimental import pallas as pl
from jax.experimental.pallas import tpu as pltpu

F32 = jnp.float32
BF16 = jnp.bfloat16
HIGHEST = lax.Precision.HIGHEST

D_MODEL = 1024
N_META = 16
CHUNK = 128
PADF = CHUNK - N_META
D_SSD = 2048
H_SSD = 32
G_SSD = 4
N_STATE = 128
CONV_K = 4
CONV_DIM = D_SSD + 2 * G_SSD * N_STATE
H_ATT = 16
D_ATT = 1024
EPS = 1e-6
N_COLS = 11312

C_Z, C_XBC, C_ZA, C_Q, C_K, C_V, C_G = 0, 2048, 5120, 6144, 7168, 8192, 9216
N_MAIN = 11264
N_SMALL = 128
O_Z, O_XBC, O_DT, O_ZA, O_Q, O_K, O_V, O_F, O_G = (
    (0, 2048), (2048, 3072), (5120, 32), (5152, 1024), (6176, 1024), (7200, 1024),
    (8224, 1024), (9248, 16), (9264, 2048))

ADAM_LR, ADAM_B1, ADAM_B2, ADAM_EPS, ADAM_WD, ADAM_STEP = 0.001, 0.9, 0.999, 1e-08, 0.01, 10

VMEM_LIMIT = 56 * 1024 * 1024


def _cp(*sem):
    return pltpu.CompilerParams(dimension_semantics=sem, vmem_limit_bytes=VMEM_LIMIT)


def _tile(n, prefs):
    for p in prefs:
        if n % p == 0:
            return p
    raise ValueError(f"no tile for {n} in {prefs}")


def _iota(shape, dim):
    return lax.broadcasted_iota(jnp.int32, shape, dim)


def _sigmoid(x):
    return 1.0 / (1.0 + jnp.exp(-x))


def _softplus_tail(x):
    return jnp.log(1.0 + jnp.exp(-jnp.abs(x)))


_NN = (((1,), (0,)), ((), ()))
_NT = (((1,), (1,)), ((), ()))
_TN = (((0,), (0,)), ((), ()))


def _dot(a, b, dims=_NN):
    return lax.dot_general(a, b, dims, preferred_element_type=F32)


def _dot_exact(a, b, dims=_NN):
    return lax.dot_general(a, b, dims, precision=HIGHEST, preferred_element_type=F32)


def _matmul(a, b, mode, out_dtype, name, tm, tn, tk):
    if mode == "tn":
        kdim, m = a.shape
    else:
        m, kdim = a.shape
    n = b.shape[0] if mode == "nt" else b.shape[1]
    nk = kdim // tk
    dims = {"nn": _NN, "nt": _NT, "tn": _TN}[mode]
    a_spec = (pl.BlockSpec((tk, tm), lambda i, j, k: (k, i)) if mode == "tn"
              else pl.BlockSpec((tm, tk), lambda i, j, k: (i, k)))
    b_spec = (pl.BlockSpec((tn, tk), lambda i, j, k: (j, k)) if mode == "nt"
              else pl.BlockSpec((tk, tn), lambda i, j, k: (k, j)))

    def body(a_ref, b_ref, o_ref, acc_ref):
        k = pl.program_id(2)
        p = _dot(a_ref[...].astype(BF16), b_ref[...].astype(BF16), dims)
        if nk == 1:
            o_ref[...] = p.astype(out_dtype)
        else:
            @pl.when(k == 0)
            def _():
                acc_ref[...] = p

            @pl.when(k > 0)
            def _():
                acc_ref[...] += p

            @pl.when(k == nk - 1)
            def _():
                o_ref[...] = acc_ref[...].astype(out_dtype)

    return pl.pallas_call(
        body, name=name,
        out_shape=jax.ShapeDtypeStruct((m, n), out_dtype),
        grid=(m // tm, n // tn, nk),
        in_specs=[a_spec, b_spec],
        out_specs=pl.BlockSpec((tm, tn), lambda i, j, k: (i, j)),
        scratch_shapes=[pltpu.VMEM((tm, tn), F32)],
        compiler_params=_cp("parallel", "parallel", "arbitrary"),
    )(a, b)


_CAT_BLK = 1024


def _piece_ranges(pieces):
    out, off = [], 0
    for p in pieces:
        nb = p.shape[1] // _CAT_BLK
        out.append((off, nb))
        off += nb
    return out, off


def _matmul_cat_nn(pieces, b, name, tm, rows=None, fill=None, exchange=None):
    t = pieces[0].shape[0]
    n = b.shape[1]
    ranges, nk = _piece_ranges(pieces)
    first, ni = rows if rows is not None else (0, t // tm)
    ex_ops, ex_shapes, ex_copies, n_sems = exchange if exchange is not None else ((), (), None, 0)
    n_in = len(pieces) + 1 + (fill is not None) + len(ex_ops)

    def body(*refs):
        a_refs, b_ref = refs[:len(pieces)], refs[len(pieces)]
        o_ref = refs[n_in]
        acc_ref = refs[n_in + 1 + len(ex_shapes)]
        i = pl.program_id(0)
        k = pl.program_id(1)
        if ex_copies is not None:
            def copies():
                return ex_copies(refs[n_in - len(ex_ops):n_in], refs[n_in + 1:n_in + 1 + len(ex_shapes)],
                                 refs[-2], refs[-1])

            @pl.when((i == 0) & (k == 0))
            def _():
                for cp in copies():
                    cp.start()

        @pl.when(k == 0)
        def _():
            acc_ref[...] = jnp.zeros_like(acc_ref)

        for a_ref, (off, nb) in zip(a_refs, ranges):
            @pl.when((k >= off) & (k < off + nb))
            def _(a_ref=a_ref):
                acc_ref[...] += _dot(a_ref[...], b_ref[...])

        @pl.when(k == nk - 1)
        def _():
            o_ref[...] = acc_ref[...]

        if ex_copies is not None:
            @pl.when((i == ni - 1) & (k == nk - 1))
            def _():
                for cp in copies():
                    cp.wait()

    def a_spec(off, nb):
        return pl.BlockSpec((tm, _CAT_BLK), lambda i, k: (first + i, jnp.clip(k - off, 0, nb - 1)))

    in_specs = [a_spec(off, nb) for off, nb in ranges] + [pl.BlockSpec((_CAT_BLK, n), lambda i, k: (k, 0))]
    operands = list(pieces) + [b]
    if fill is not None:
        in_specs.append(_ANY)
        operands.append(fill)
    in_specs += [_ANY] * len(ex_ops)
    operands += list(ex_ops)
    scratch = [pltpu.VMEM((tm, n), F32)]
    if n_sems:
        scratch += [pltpu.SemaphoreType.DMA((n_sems,)), pltpu.SemaphoreType.DMA((n_sems,))]
    out = pl.pallas_call(
        body, name=name,
        out_shape=(jax.ShapeDtypeStruct((t, n), F32),) + tuple(ex_shapes),
        grid=(ni, nk),
        in_specs=in_specs,
        out_specs=(pl.BlockSpec((tm, n), lambda i, k: (first + i, 0)),) + (_ANY,) * len(ex_shapes),
        input_output_aliases={len(pieces) + 1: 0} if fill is not None else {},
        scratch_shapes=scratch,
        compiler_params=_cp("arbitrary" if n_sems else "parallel", "arbitrary"),
    )(*operands)
    return out if exchange is not None else out[0]


def _matmul_cat_tn(pieces, b, name, tk):
    t = pieces[0].shape[0]
    n = b.shape[1]
    ranges, nm = _piece_ranges(pieces)
    nk = t // tk

    def body(*refs):
        a_refs, b_ref, o_ref, acc_ref = refs[:len(pieces)], refs[-3], refs[-2], refs[-1]
        m = pl.program_id(0)
        k = pl.program_id(1)

        @pl.when(k == 0)
        def _():
            acc_ref[...] = jnp.zeros_like(acc_ref)

        for a_ref, (off, nb) in zip(a_refs, ranges):
            @pl.when((m >= off) & (m < off + nb))
            def _(a_ref=a_ref):
                acc_ref[...] += _dot(a_ref[...], b_ref[...], _TN)

        @pl.when(k == nk - 1)
        def _():
            o_ref[...] = acc_ref[...]

    def a_spec(off, nb):
        def index(m, k):
            mine = (m >= off) & (m < off + nb)
            return jnp.where(mine, k, 0), jnp.clip(m - off, 0, nb - 1)
        return pl.BlockSpec((tk, _CAT_BLK), index)

    return pl.pallas_call(
        body, name=name,
        out_shape=jax.ShapeDtypeStruct((nm * _CAT_BLK, n), F32),
        grid=(nm, nk),
        in_specs=[a_spec(off, nb) for off, nb in ranges] + [pl.BlockSpec((tk, n), lambda m, k: (k, 0))],
        out_specs=pl.BlockSpec((_CAT_BLK, n), lambda m, k: (m, 0)),
        scratch_shapes=[pltpu.VMEM((_CAT_BLK, n), F32)],
        compiler_params=_cp("parallel", "arbitrary"),
    )(*pieces, b)


def _row_tile(t):
    return _tile(t, (352, 128))


def _row_tile_wide(t):
    return _tile(t, (176, 128))


def _norm1_fwd(h, g):
    t = h.shape[0]
    tm = _row_tile(t)

    def body(h_ref, g_ref, u_ref):
        x = h_ref[...]
        r = lax.rsqrt(jnp.mean(x * x, axis=-1, keepdims=True) + EPS)
        u_ref[...] = (x * r * g_ref[...]).astype(BF16)

    return pl.pallas_call(
        body, name="norm1_fwd",
        out_shape=jax.ShapeDtypeStruct((t, D_MODEL), BF16),
        grid=(t // tm,),
        in_specs=[pl.BlockSpec((tm, D_MODEL), lambda i: (i, 0)),
                  pl.BlockSpec((1, D_MODEL), lambda i: (0, 0))],
        out_specs=pl.BlockSpec((tm, D_MODEL), lambda i: (i, 0)),
        compiler_params=_cp("parallel"),
    )(h, g)


def _norm1_bwd(du_a, du_b, h, g, dy):
    t = h.shape[0]
    tm = _row_tile(t)

    def body(a_ref, b_ref, h_ref, g_ref, dy_ref, dh_ref, dg_ref):
        i = pl.program_id(0)
        x = h_ref[...]
        du = a_ref[...] + b_ref[...]
        r = lax.rsqrt(jnp.mean(x * x, axis=-1, keepdims=True) + EPS)
        gdu = du * g_ref[...]
        dh_ref[...] = dy_ref[...] + r * (gdu - x * (r * r) * jnp.mean(gdu * x, axis=-1, keepdims=True))
        part = jnp.sum(du * x * r, axis=0, keepdims=True)

        @pl.when(i == 0)
        def _():
            dg_ref[...] = part

        @pl.when(i > 0)
        def _():
            dg_ref[...] += part

    row = pl.BlockSpec((tm, D_MODEL), lambda i: (i, 0))
    vec = pl.BlockSpec((1, D_MODEL), lambda i: (0, 0))
    return pl.pallas_call(
        body, name="norm1_bwd",
        out_shape=(jax.ShapeDtypeStruct((t, D_MODEL), F32), jax.ShapeDtypeStruct((1, D_MODEL), F32)),
        grid=(t // tm,),
        in_specs=[row, row, row, vec, row],
        out_specs=(row, vec),
        compiler_params=_cp("arbitrary"),
    )(du_a, du_b, h, g, dy)


def _small_fwd(small, bias_row):
    t = small.shape[0]

    def body(s_ref, b_ref, o_ref, carry_ref):
        c = pl.program_id(0)

        @pl.when(c == 0)
        def _():
            carry_ref[...] = jnp.zeros_like(carry_ref)

        x = s_ref[...] + b_ref[...]
        r0 = _iota((CHUNK, CHUNK), 0)
        r1 = _iota((CHUNK, CHUNK), 1)
        valid = (c * CHUNK + r0) >= PADF
        tail = _softplus_tail(x)
        dt = jnp.where(valid & (r1 < H_SSD), jnp.maximum(x, 0.0) + tail, 0.0)
        lf = jnp.where(valid & (r1 >= H_SSD) & (r1 < H_SSD + H_ATT), jnp.minimum(x, 0.0) - tail, 0.0)
        tri = (r0 >= r1).astype(F32)
        cs = _dot_exact(tri, lf) + carry_ref[...]
        carry_ref[...] = cs[CHUNK - 1:CHUNK, :]
        o_ref[...] = dt + cs

    return pl.pallas_call(
        body, name="small_fwd",
        out_shape=jax.ShapeDtypeStruct((t, N_SMALL), F32),
        grid=(t // CHUNK,),
        in_specs=[pl.BlockSpec((CHUNK, N_SMALL), lambda c: (c, 0)),
                  pl.BlockSpec((1, N_SMALL), lambda c: (0, 0))],
        out_specs=pl.BlockSpec((CHUNK, N_SMALL), lambda c: (c, 0)),
        scratch_shapes=[pltpu.VMEM((1, N_SMALL), F32)],
        compiler_params=_cp("arbitrary"),
    )(small, bias_row)


def _small_bwd(dsm, small, bias_row):
    t = small.shape[0]
    nc = t // CHUNK

    def body(d_ref, s_ref, b_ref, o_ref, db_ref, carry_ref):
        step = pl.program_id(0)
        c = nc - 1 - step

        @pl.when(step == 0)
        def _():
            carry_ref[...] = jnp.zeros_like(carry_ref)
            db_ref[...] = jnp.zeros_like(db_ref)

        x = s_ref[...] + b_ref[...]
        d = d_ref[...]
        r0 = _iota((CHUNK, CHUNK), 0)
        r1 = _iota((CHUNK, CHUNK), 1)
        valid = (c * CHUNK + r0) >= PADF
        is_dt = r1 < H_SSD
        is_f = (r1 >= H_SSD) & (r1 < H_SSD + H_ATT)
        triu = (r1 >= r0).astype(F32)
        dc = jnp.where(is_f, d, 0.0)
        dlf = _dot_exact(triu, dc) + carry_ref[...]
        carry_ref[...] = dlf[0:1, :]
        sg = _sigmoid(x)
        out = jnp.where(valid & is_dt, d * sg, 0.0) + jnp.where(valid & is_f, dlf * (1.0 - sg), 0.0)
        o_ref[...] = out.astype(BF16)
        db_ref[...] += jnp.sum(out, axis=0, keepdims=True)

    blk = pl.BlockSpec((CHUNK, N_SMALL), lambda s: (nc - 1 - s, 0))
    vec = pl.BlockSpec((1, N_SMALL), lambda s: (0, 0))
    return pl.pallas_call(
        body, name="small_bwd",
        out_shape=(jax.ShapeDtypeStruct((t, N_SMALL), BF16), jax.ShapeDtypeStruct((1, N_SMALL), F32)),
        grid=(nc,),
        in_specs=[blk, blk, vec],
        out_specs=(blk, vec),
        scratch_shapes=[pltpu.VMEM((1, N_SMALL), F32)],
        compiler_params=_cp("arbitrary"),
    )(dsm, small, bias_row)


_CONV_TC = 1024
_XBC_BLK = C_XBC // _CONV_TC


def _shift_down(cur, prev8, j):
    rc = pltpu.roll(cur, j, 0)
    rid = _iota(prev8.shape, 0)
    top = jnp.where(rid < j, pltpu.roll(prev8, j, 0), rc[0:8, :])
    return jnp.concatenate([top, rc[8:, :]], axis=0)


def _shift_up(cur, next8, j):
    n = cur.shape[0]
    ru = pltpu.roll(cur, n - j, 0)
    rid = _iota(next8.shape, 0)
    bot = jnp.where(rid >= 8 - j, pltpu.roll(next8, 8 - j, 0), ru[n - 8:, :])
    return jnp.concatenate([ru[:n - 8, :], bot], axis=0)


def _conv_pre(x_ref, p_ref, w_ref, b_ref, i):
    cur = x_ref[...]
    prev = jnp.where(i > 0, p_ref[...], 0.0)
    w = w_ref[...]
    taps = [cur] + [_shift_down(cur, prev, j) for j in (1, 2, 3)]
    acc = b_ref[...] + taps[0] * w[3:4, :]
    for j in (1, 2, 3):
        acc = acc + taps[j] * w[3 - j:4 - j, :]
    return acc, taps


def _conv_fwd(proj, conv_w, conv_b):
    t = proj.shape[0]
    tr = _row_tile(t)

    def body(x_ref, p_ref, w_ref, b_ref, o_ref):
        i = pl.program_id(0)
        acc, _ = _conv_pre(x_ref, p_ref, w_ref, b_ref, i)
        valid = (i * tr + _iota(acc.shape, 0)) >= PADF
        o_ref[...] = jnp.where(valid, acc * _sigmoid(acc), 0.0)

    return pl.pallas_call(
        body, name="conv_fwd",
        out_shape=jax.ShapeDtypeStruct((t, CONV_DIM), F32),
        grid=(t // tr, CONV_DIM // _CONV_TC),
        in_specs=[pl.BlockSpec((tr, _CONV_TC), lambda i, j: (i, _XBC_BLK + j)),
                  pl.BlockSpec((8, _CONV_TC), lambda i, j: (jnp.maximum(i * (tr // 8) - 1, 0), _XBC_BLK + j)),
                  pl.BlockSpec((CONV_K, _CONV_TC), lambda i, j: (0, j)),
                  pl.BlockSpec((1, _CONV_TC), lambda i, j: (0, j))],
        out_specs=pl.BlockSpec((tr, _CONV_TC), lambda i, j: (i, j)),
        compiler_params=_cp("parallel", "parallel"),
    )(proj, proj, conv_w, conv_b)


def _conv_bwd_act(dxbc, proj, conv_w, conv_b):
    t = proj.shape[0]
    tr = _row_tile(t)

    def body(d_ref, x_ref, p_ref, w_ref, b_ref, da_ref, dw_ref, db_ref):
        i = pl.program_id(1)
        acc, taps = _conv_pre(x_ref, p_ref, w_ref, b_ref, i)
        valid = (i * tr + _iota(acc.shape, 0)) >= PADF
        sg = _sigmoid(acc)
        da = jnp.where(valid, d_ref[...] * sg * (1.0 + acc * (1.0 - sg)), 0.0)
        da_ref[...] = da
        dw = jnp.concatenate([jnp.sum(da * taps[3 - k], axis=0, keepdims=True) for k in range(CONV_K)], axis=0)
        db = jnp.sum(da, axis=0, keepdims=True)

        @pl.when(i == 0)
        def _():
            dw_ref[...] = dw
            db_ref[...] = db

        @pl.when(i > 0)
        def _():
            dw_ref[...] += dw
            db_ref[...] += db

    return pl.pallas_call(
        body, name="conv_bwd_act",
        out_shape=(jax.ShapeDtypeStruct((t, CONV_DIM), F32),
                   jax.ShapeDtypeStruct((CONV_K, CONV_DIM), F32),
                   jax.ShapeDtypeStruct((1, CONV_DIM), F32)),
        grid=(CONV_DIM // _CONV_TC, t // tr),
        in_specs=[pl.BlockSpec((tr, _CONV_TC), lambda j, i: (i, j)),
                  pl.BlockSpec((tr, _CONV_TC), lambda j, i: (i, _XBC_BLK + j)),
                  pl.BlockSpec((8, _CONV_TC), lambda j, i: (jnp.maximum(i * (tr // 8) - 1, 0), _XBC_BLK + j)),
                  pl.BlockSpec((CONV_K, _CONV_TC), lambda j, i: (0, j)),
                  pl.BlockSpec((1, _CONV_TC), lambda j, i: (0, j))],
        out_specs=(pl.BlockSpec((tr, _CONV_TC), lambda j, i: (i, j)),
                   pl.BlockSpec((CONV_K, _CONV_TC), lambda j, i: (0, j)),
                   pl.BlockSpec((1, _CONV_TC), lambda j, i: (0, j))),
        compiler_params=_cp("parallel", "arbitrary"),
    )(dxbc, proj, proj, conv_w, conv_b)


def _conv_bwd_in(da, conv_w):
    t = da.shape[0]
    tr = _row_tile(t)
    last8 = t // 8 - 1

    def body(d_ref, n_ref, w_ref, o_ref):
        i = pl.program_id(0)
        cur = d_ref[...]
        nxt = jnp.where(i < pl.num_programs(0) - 1, n_ref[...], 0.0)
        w = w_ref[...]
        acc = cur * w[3:4, :]
        for j in (1, 2, 3):
            acc = acc + _shift_up(cur, nxt, j) * w[3 - j:4 - j, :]
        o_ref[...] = acc.astype(BF16)

    return pl.pallas_call(
        body, name="conv_bwd_in",
        out_shape=jax.ShapeDtypeStruct((t, CONV_DIM), BF16),
        grid=(t // tr, CONV_DIM // _CONV_TC),
        in_specs=[pl.BlockSpec((tr, _CONV_TC), lambda i, j: (i, j)),
                  pl.BlockSpec((8, _CONV_TC), lambda i, j: (jnp.minimum((i + 1) * (tr // 8), last8), j)),
                  pl.BlockSpec((CONV_K, _CONV_TC), lambda i, j: (0, j))],
        out_specs=pl.BlockSpec((tr, _CONV_TC), lambda i, j: (i, j)),
        compiler_params=_cp("parallel", "parallel"),
    )(da, da, conv_w)


_GW = D_SSD // G_SSD


def _ssd_prelude(dt_ref, a_ref, e_scr, es_scr, dte_scr):
    r0 = _iota((CHUNK, CHUNK), 0)
    r1 = _iota((CHUNK, CHUNK), 1)
    dt = jnp.where(r1 < H_SSD, dt_ref[...], 0.0)
    adt = dt * a_ref[...]
    acs = _dot_exact((r0 >= r1).astype(F32), adt)
    acs_t = acs.T
    alast = acs[CHUNK - 1:CHUNK, :]
    exp_a = jnp.exp(acs)
    dec_s = jnp.exp(alast - acs)
    lo = r1 < 64
    for j in range(H_SSD // 2):
        sl = slice(CHUNK * j, CHUNK * (j + 1))
        e_scr[:, sl] = jnp.where(lo, exp_a[:, 2 * j:2 * j + 1], exp_a[:, 2 * j + 1:2 * j + 2])
        es_scr[:, sl] = jnp.where(lo, dec_s[:, 2 * j:2 * j + 1], dec_s[:, 2 * j + 1:2 * j + 2])
        dte_scr[:, sl] = jnp.where(lo, dt[:, 2 * j:2 * j + 1], dt[:, 2 * j + 1:2 * j + 2])
    return dt, acs, acs_t, r0, r1, lo


def _chunk_decay_rows(acs_t, g):
    cd_t = jnp.exp(acs_t[:, CHUNK - 1:CHUNK])
    return jnp.concatenate(
        [jnp.broadcast_to(cd_t[8 * g + hh:8 * g + hh + 1, :], (64, N_STATE)) for hh in range(8)], axis=0)


def _ssd_fwd(xbc, dtlf, a_row, dsk_row):
    t = xbc.shape[0]
    nc = t // CHUNK

    def body(xs_ref, b_ref, c_ref, dt_ref, a_ref, dsk_ref, y_ref, hin_ref, h_scr, e_scr, es_scr, dte_scr):
        c = pl.program_id(0)

        @pl.when(c == 0)
        def _():
            h_scr[...] = jnp.zeros_like(h_scr)

        dt, acs, acs_t, r0, r1, lo = _ssd_prelude(dt_ref, a_ref, e_scr, es_scr, dte_scr)
        causal = r0 >= r1
        for g in range(G_SSD):
            gs = slice(_GW * g, _GW * (g + 1))
            bg = b_ref[:, N_STATE * g:N_STATE * (g + 1)].astype(BF16)
            cg = c_ref[:, N_STATE * g:N_STATE * (g + 1)].astype(BF16)
            cb = _dot(cg, bg, _NT)
            hg = h_scr[gs, :]
            hin_ref[0, gs, :] = hg
            xg = xs_ref[:, gs] * dte_scr[:, gs]
            yoff = _dot(cg, hg.astype(BF16), _NT) * e_scr[:, gs]
            st = _dot((xg * es_scr[:, gs]).astype(BF16), bg, _TN)
            h_scr[gs, :] = hg * _chunk_decay_rows(acs_t, g) + st
            for jj in range(4):
                j = 4 * g + jj
                sl = slice(CHUNK * j, CHUNK * (j + 1))
                xp = xg[:, CHUNK * jj:CHUNK * (jj + 1)]
                acc = yoff[:, CHUNK * jj:CHUNK * (jj + 1)] + dsk_ref[:, sl] * xs_ref[:, sl]
                for hh in range(2):
                    h = 2 * j + hh
                    seg = acs[:, h:h + 1] - acs_t[h:h + 1, :]
                    lm = jnp.exp(jnp.where(causal, seg, -1e30))
                    m = (cb * lm).astype(BF16)
                    xh = jnp.where(lo if hh == 0 else ~lo, xp, 0.0).astype(BF16)
                    acc = acc + _dot(m, xh)
                y_ref[:, sl] = acc

    return pl.pallas_call(
        body, name="ssd_fwd",
        out_shape=(jax.ShapeDtypeStruct((t, D_SSD), F32), jax.ShapeDtypeStruct((nc, D_SSD, N_STATE), F32)),
        grid=(nc,),
        in_specs=[pl.BlockSpec((CHUNK, D_SSD), lambda c: (c, 0)),
                  pl.BlockSpec((CHUNK, _GW), lambda c: (c, 4)),
                  pl.BlockSpec((CHUNK, _GW), lambda c: (c, 5)),
                  pl.BlockSpec((CHUNK, N_SMALL), lambda c: (c, 0)),
                  pl.BlockSpec((1, N_SMALL), lambda c: (0, 0)),
                  pl.BlockSpec((1, D_SSD), lambda c: (0, 0))],
        out_specs=(pl.BlockSpec((CHUNK, D_SSD), lambda c: (c, 0)),
                   pl.BlockSpec((1, D_SSD, N_STATE), lambda c: (c, 0, 0))),
        scratch_shapes=[pltpu.VMEM((D_SSD, N_STATE), F32)] + [pltpu.VMEM((CHUNK, D_SSD), F32)] * 3,
        compiler_params=_cp("arbitrary"),
    )(xbc, xbc, xbc, dtlf, a_row, dsk_row)


def _ssd_bwd(xbc, dtlf, a_row, dsk_row, hin, dy):
    t = xbc.shape[0]
    nc = t // CHUNK

    def body(xs_ref, b_ref, c_ref, dt_ref, a_ref, dsk_ref, hin_ref, dy_ref,
             dxbc_ref, ddt_ref, da_ref, ddsk_ref, dh_scr, e_scr, es_scr, dte_scr, dx_scr, whi_scr, wlo_scr):
        step = pl.program_id(0)

        @pl.when(step == 0)
        def _():
            dh_scr[...] = jnp.zeros_like(dh_scr)
            da_ref[...] = jnp.zeros_like(da_ref)
            ddsk_ref[...] = jnp.zeros_like(ddsk_ref)

        dt, acs, acs_t, r0, r1, lo = _ssd_prelude(dt_ref, a_ref, e_scr, es_scr, dte_scr)
        causal = r0 >= r1
        lane_row = _iota((1, CHUNK), 1)
        dacs = jnp.zeros((CHUNK, CHUNK), F32)
        dacs_t = jnp.zeros((CHUNK, CHUNK), F32)
        dalast = jnp.zeros((1, CHUNK), F32)
        ddt_dir = jnp.zeros((CHUNK, CHUNK), F32)
        ddsk_ref[...] += jnp.sum(dy_ref[...] * xs_ref[...], axis=0, keepdims=True)

        def head_sums(z, pick):
            hi = z.astype(BF16)
            return _dot(hi, pick) + _dot((z - hi.astype(F32)).astype(BF16), pick)

        for g in range(G_SSD):
            gs = slice(_GW * g, _GW * (g + 1))
            pick = (jnp.right_shift(_iota((_GW, CHUNK), 0), 6) + 8 * g == _iota((_GW, CHUNK), 1)).astype(BF16)
            bg = b_ref[:, N_STATE * g:N_STATE * (g + 1)].astype(BF16)
            cg = c_ref[:, N_STATE * g:N_STATE * (g + 1)].astype(BF16)
            cb = _dot(cg, bg, _NT)
            hg = hin_ref[0, gs, :]
            hgb = hg.astype(BF16)
            dhn = dh_scr[gs, :]
            dhnb = dhn.astype(BF16)
            esg = es_scr[:, gs]
            dyg = dy_ref[:, gs]
            xsg = xs_ref[:, gs]
            xg = xsg * dte_scr[:, gs]
            dyeb = (dyg * e_scr[:, gs]).astype(BF16)
            dc = _dot(dyeb, hgb)
            dh_y = _dot(dyeb, cg, _TN)
            dxs = _dot(bg, dhnb, _NT) * esg
            db = _dot((xg * esg).astype(BF16), dhnb)
            cd = _chunk_decay_rows(acs_t, g)
            dh_scr[gs, :] = dhn * cd + dh_y
            end_state = head_sums(jnp.broadcast_to(jnp.sum(xg * dxs, axis=0, keepdims=True), (8, _GW)), pick)[0:1, :]
            carried = dhn * hg * cd
            per_head = jnp.concatenate([jnp.sum(carried[64 * hh:64 * hh + 64, :], axis=0, keepdims=True)
                                        for hh in range(8)], axis=0)
            per_head = jnp.sum(per_head, axis=1, keepdims=True)
            for hh in range(8):
                end_state = end_state + jnp.where(lane_row == 8 * g + hh, per_head[hh:hh + 1, :], 0.0)
            dalast = dalast + end_state
            dcb = jnp.zeros((CHUNK, CHUNK), F32)
            for jj in range(4):
                j = 4 * g + jj
                sl = slice(CHUNK * j, CHUNK * (j + 1))
                ps = slice(CHUNK * jj, CHUNK * (jj + 1))
                xpb = xg[:, ps].astype(BF16)
                dyp = dyg[:, ps]
                dxp = dxs[:, ps]
                for hh in range(2):
                    h = 2 * j + hh
                    ws = slice(CHUNK * (2 * jj + hh), CHUNK * (2 * jj + hh + 1))
                    seg = acs[:, h:h + 1] - acs_t[h:h + 1, :]
                    lm = jnp.exp(jnp.where(causal, seg, -1e30))
                    mf = cb * lm
                    dyh = jnp.where(lo if hh == 0 else ~lo, dyp, 0.0).astype(BF16)
                    gm = _dot(dyh, xpb, _NT)
                    dcb = dcb + gm * lm
                    w = gm * mf
                    whi = w.astype(BF16)
                    whi_scr[:, ws] = whi
                    wlo_scr[:, ws] = (w - whi.astype(F32)).astype(BF16)
                    dacs_t = dacs_t - jnp.where(r0 == h, jnp.sum(w, axis=0, keepdims=True), 0.0)
                    dxp = dxp + _dot(mf.astype(BF16), dyh, _TN)
                dx_scr[:, sl] = dxp
            dxg = dx_scr[:, gs]
            pick_w = (jnp.right_shift(_iota((8 * CHUNK, CHUNK), 0), 7) + 8 * g == _iota((8 * CHUNK, CHUNK), 1)).astype(BF16)
            ch = _dot(cg, hgb, _NT)
            dacs = (dacs + _dot(whi_scr[...], pick_w) + _dot(wlo_scr[...], pick_w)
                    + head_sums(dyg * e_scr[:, gs] * ch - xg * dxs, pick))
            ddt_dir = ddt_dir + head_sums(dxg * xsg, pick)
            dcbb = dcb.astype(BF16)
            dxbc_ref[:, D_SSD + N_STATE * g:D_SSD + N_STATE * (g + 1)] = db + _dot(dcbb, cg, _TN)
            dxbc_ref[:, D_SSD + _GW + N_STATE * g:D_SSD + _GW + N_STATE * (g + 1)] = dc + _dot(dcbb, bg)
        dxbc_ref[:, 0:D_SSD] = dx_scr[...] * dte_scr[...] + dsk_ref[...] * dy_ref[...]
        dacs = dacs + dacs_t.T + jnp.where(r0 == CHUNK - 1, dalast, 0.0)
        dadt = _dot_exact((r1 >= r0).astype(F32), dacs)
        ddt_ref[...] = dadt * a_ref[...] + ddt_dir
        da_ref[...] += jnp.sum(dadt * dt, axis=0, keepdims=True)

    rev = lambda s: (nc - 1 - s, 0)
    return pl.pallas_call(
        body, name="ssd_bwd",
        out_shape=(jax.ShapeDtypeStruct((t, CONV_DIM), F32), jax.ShapeDtypeStruct((t, N_SMALL), F32),
                   jax.ShapeDtypeStruct((1, N_SMALL), F32), jax.ShapeDtypeStruct((1, D_SSD), F32)),
        grid=(nc,),
        in_specs=[pl.BlockSpec((CHUNK, D_SSD), rev),
                  pl.BlockSpec((CHUNK, _GW), lambda s: (nc - 1 - s, 4)),
                  pl.BlockSpec((CHUNK, _GW), lambda s: (nc - 1 - s, 5)),
                  pl.BlockSpec((CHUNK, N_SMALL), rev),
                  pl.BlockSpec((1, N_SMALL), lambda s: (0, 0)),
                  pl.BlockSpec((1, D_SSD), lambda s: (0, 0)),
                  pl.BlockSpec((1, D_SSD, N_STATE), lambda s: (nc - 1 - s, 0, 0)),
                  pl.BlockSpec((CHUNK, D_SSD), rev)],
        out_specs=(pl.BlockSpec((CHUNK, CONV_DIM), rev),
                   pl.BlockSpec((CHUNK, N_SMALL), rev),
                   pl.BlockSpec((1, N_SMALL), lambda s: (0, 0)),
                   pl.BlockSpec((1, D_SSD), lambda s: (0, 0))),
        scratch_shapes=([pltpu.VMEM((D_SSD, N_STATE), F32)] + [pltpu.VMEM((CHUNK, D_SSD), F32)] * 4
                        + [pltpu.VMEM((CHUNK, 8 * CHUNK), BF16)] * 2),
        compiler_params=_cp("arbitrary"),
    )(xbc, xbc, xbc, dtlf, a_row, dsk_row, hin, dy)


_NPAIR = H_ATT // 2
_QB, _KB, _VB = C_Q // 128, C_K // 128, C_V // 128
_SCALE = 1.0 / math.sqrt(64.0)


def _attn_blocks(t):
    return _tile(t, (1408, 384, 256, 128)), _tile(t, (384, 128))


def _split3(c):
    hi = c.astype(BF16).astype(F32)
    rest = c - hi
    mid = rest.astype(BF16).astype(F32)
    return hi, mid, rest - mid


def _head_lanes(lane, hh):
    return (lane < 64, 64) if hh == 0 else (lane >= 64, 0)


def _q_operand(q, cq, lane, hh):
    sel, first = _head_lanes(lane, hh)
    out = jnp.where(sel, q, 0.0)
    for n, col in enumerate(_split3(cq) + (1.0, 1.0, 1.0)):
        out = jnp.where(lane == first + n, col, out)
    return out.astype(BF16)


def _k_operand(k, ck, lane, hh):
    sel, first = _head_lanes(lane, hh)
    hi, mid, lo = _split3(ck)
    out = jnp.where(sel, k, 0.0)
    for n, col in enumerate((1.0, 1.0, 1.0, -hi, -mid, -lo)):
        out = jnp.where(lane == first + n, col, out)
    return out.astype(BF16)


def _needs_mask(i, kk, bq, bk):
    return kk * bk + bk - 1 > i * bq


_C_FILLER = 2.0 ** 30


def _attn_fwd(proj, c_col):
    t = proj.shape[0]
    bq, bk = _attn_blocks(t)
    nq, nk = t // bq, t // bk
    rs = 16

    def last_kv(i):
        return (i * bq + bq - 1) // bk

    def body(q_ref, k_ref, v_ref, cq_ref, ck_ref, o_ref, lse_ref, qs_scr, s_scr, p_scr, m_scr, acc_scr):
        i = pl.program_id(1)
        kk = pl.program_id(2)
        lane_q = _iota((bq, 128), 1)

        @pl.when(kk == 0)
        def _():
            m_scr[...] = jnp.full_like(m_scr, -1e30)
            acc_scr[...] = jnp.zeros_like(acc_scr)
            q = q_ref[...] * _SCALE
            cq = cq_ref[0]
            for hh in range(2):
                qs_scr[hh] = _q_operand(q, cq[:, hh:hh + 1], lane_q, hh)

        def step(masked):
            lane_k = _iota((bk, 128), 1)
            k = k_ref[...]
            v = v_ref[...]
            ck = ck_ref[0]
            ahead = _iota((rs, bq), 0) - _iota((rs, bq), 1)
            for hh in range(2):
                sel, first = _head_lanes(lane_k, hh)
                ks = _k_operand(k, ck[:, hh:hh + 1], lane_k, hh)
                vs = jnp.where(sel, v, jnp.where(lane_k == first, 1.0, 0.0)).astype(BF16)
                s_scr[hh] = _dot(ks, qs_scr[hh], _NT)

                def block_max(r, mx):
                    rows = pl.ds(pl.multiple_of(r * rs, rs), rs)
                    s = s_scr[hh, rows, :]
                    if masked:
                        s = jnp.where(ahead <= i * bq - kk * bk - r * rs, s, -1e30)
                        s_scr[hh, rows, :] = s
                    return jnp.maximum(mx, s)

                mx = lax.fori_loop(0, bk // rs, block_max, jnp.full((rs, bq), -1e30, F32), unroll=True)
                m_old = m_scr[hh]
                m_new = jnp.maximum(m_old, jnp.max(mx, axis=0, keepdims=True))
                m_scr[hh] = m_new

                def probs(r, carry):
                    rows = pl.ds(pl.multiple_of(r * rs, rs), rs)
                    p_scr[hh, rows, :] = jnp.exp(s_scr[hh, rows, :] - m_new).astype(BF16)
                    return carry

                lax.fori_loop(0, bk // rs, probs, 0, unroll=True)
                acc_scr[hh] = acc_scr[hh] * jnp.exp(m_old - m_new) + _dot(vs, p_scr[hh], _TN)

        active = kk <= last_kv(i)
        masked = _needs_mask(i, kk, bq, bk)

        @pl.when(active & masked)
        def _():
            step(True)

        @pl.when(active & jnp.logical_not(masked))
        def _():
            step(False)

        @pl.when(kk == nk - 1)
        def _():
            a = acc_scr[0]
            b = acc_scr[1]
            la = a[64:65, :]
            lb = b[0:1, :]
            o_ref[...] = jnp.where(lane_q < 64, (a / la).T, (b / lb).T)
            lse_ref[0] = jnp.concatenate([m_scr[0] + jnp.log(la), m_scr[1] + jnp.log(lb)], axis=0)

    kvi = lambda i, kk: jnp.minimum(kk, last_kv(i))
    kv = lambda off: pl.BlockSpec((bk, 128), lambda j, i, kk: (kvi(i, kk), off + j))
    return pl.pallas_call(
        body, name="attn_fwd",
        out_shape=(jax.ShapeDtypeStruct((t, D_ATT), F32), jax.ShapeDtypeStruct((_NPAIR, 2, t), F32)),
        grid=(_NPAIR, nq, nk),
        in_specs=[pl.BlockSpec((bq, 128), lambda j, i, kk: (i, _QB + j)),
                  kv(_KB), kv(_VB),
                  pl.BlockSpec((1, bq, 2), lambda j, i, kk: (j, i, 0)),
                  pl.BlockSpec((1, bk, 2), lambda j, i, kk: (j, kvi(i, kk), 0))],
        out_specs=(pl.BlockSpec((bq, 128), lambda j, i, kk: (i, j)),
                   pl.BlockSpec((1, 2, bq), lambda j, i, kk: (j, 0, i))),
        scratch_shapes=[pltpu.VMEM((2, bq, 128), BF16), pltpu.VMEM((2, bk, bq), F32), pltpu.VMEM((2, bk, bq), BF16),
                        pltpu.VMEM((2, 1, bq), F32), pltpu.VMEM((2, 128, bq), F32)],
        compiler_params=_cp("parallel", "parallel", "arbitrary"),
    )(proj, proj, proj, c_col, c_col)


def _attn_delta(do, o):
    t = do.shape[0]
    tm = _row_tile(t)

    def body(do_ref, o_ref, d_ref):
        pick = (jnp.right_shift(_iota((D_ATT, 128), 0), 6) == _iota((D_ATT, 128), 1)).astype(F32)
        d_ref[...] = _dot_exact(do_ref[...] * o_ref[...], pick)

    row = pl.BlockSpec((tm, D_ATT), lambda i: (i, 0))
    return pl.pallas_call(
        body, name="attn_delta",
        out_shape=jax.ShapeDtypeStruct((t, 128), F32),
        grid=(t // tm,), in_specs=[row, row], out_specs=pl.BlockSpec((tm, 128), lambda i: (i, 0)),
        compiler_params=_cp("parallel"),
    )(do, o)


def _attn_bwd(proj, c_col, lse_row, dl_row, do):
    t = proj.shape[0]
    bq, bk = _attn_blocks(t)
    nq, nk = t // bq, t // bk
    rs = 16

    def first_q(kk):
        return (kk * bk) // bq

    def body(q_ref, k_ref, v_ref, cq_ref, ck_ref, lse_ref, dl_ref, do_ref,
             dq_ref, dk_ref, dv_ref, dck_ref, dcq_ref,
             qs_scr, doh_scr, ks_scr, s_scr, dp_scr, p_scr, ds_scr, dq_scr, dk_scr, dv_scr):
        kk = pl.program_id(1)
        i = pl.program_id(2)
        lane_q = _iota((bq, 128), 1)
        lane_k = _iota((bk, 128), 1)
        qrows = pl.ds(pl.multiple_of(i * bq, 128), bq)

        @pl.when(kk == 0)
        def _():
            q = q_ref[...] * _SCALE
            cq = cq_ref[0]
            do_ = do_ref[...]
            for hh in range(2):
                qs_scr[hh, qrows, :] = _q_operand(q, cq[:, hh:hh + 1], lane_q, hh)
                doh_scr[hh, qrows, :] = jnp.where(_head_lanes(lane_q, hh)[0], do_, 0.0).astype(BF16)
                dq_scr[hh, qrows, :] = jnp.zeros((bq, 128), F32)

        @pl.when(i == 0)
        def _():
            dk_scr[...] = jnp.zeros_like(dk_scr)
            dv_scr[...] = jnp.zeros_like(dv_scr)
            k = k_ref[...]
            ck = ck_ref[0]
            for hh in range(2):
                ks_scr[hh] = _k_operand(k, ck[:, hh:hh + 1], lane_k, hh)

        def step(masked):
            v16 = v_ref[...].astype(BF16)
            lse = lse_ref[0]
            dl = dl_ref[0]
            ahead = _iota((rs, bq), 0) - _iota((rs, bq), 1)
            for hh in range(2):
                qs = qs_scr[hh, qrows, :]
                doh = doh_scr[hh, qrows, :]
                s_scr[hh] = _dot(ks_scr[hh], qs, _NT)
                dp_scr[hh] = _dot(v16, doh, _NT)

                def strip(r, carry):
                    rows = pl.ds(pl.multiple_of(r * rs, rs), rs)
                    p = jnp.exp(s_scr[hh, rows, :] - lse[hh:hh + 1, :])
                    if masked:
                        p = jnp.where(ahead <= i * bq - kk * bk - r * rs, p, 0.0)
                    p_scr[hh, rows, :] = p.astype(BF16)
                    ds_scr[hh, rows, :] = (p * (dp_scr[hh, rows, :] - dl[hh:hh + 1, :])).astype(BF16)
                    return carry

                lax.fori_loop(0, bk // rs, strip, 0, unroll=True)
                dv_scr[...] += _dot(p_scr[hh], doh)
                dk_scr[hh] += _dot(ds_scr[hh], qs)
                dq_scr[hh, qrows, :] += _dot(ds_scr[hh], ks_scr[hh], _TN)

        active = i >= first_q(kk)
        masked = _needs_mask(i, kk, bq, bk)

        @pl.when(active & masked)
        def _():
            step(True)

        @pl.when(active & jnp.logical_not(masked))
        def _():
            step(False)

        @pl.when(i == nq - 1)
        def _():
            dka = dk_scr[0]
            dkb = dk_scr[1]
            dk_ref[...] = jnp.where(lane_k < 64, dka, dkb).astype(BF16)
            dv_ref[...] = dv_scr[...].astype(BF16)
            dck_ref[0] = -jnp.where(_iota((bk, 2), 1) == 0, dka[:, 67:68], dkb[:, 3:4])

        @pl.when((kk == nk - 1) & (i == nq - 1))
        def _():
            lane_t = _iota((t, 128), 1)
            dqa = dq_scr[0]
            dqb = dq_scr[1]
            dq_ref[...] = (jnp.where(lane_t < 64, dqa, dqb) * _SCALE).astype(BF16)
            dcq_ref[0] = jnp.where(_iota((t, 2), 1) == 0, dqa[:, 64:65], dqb[:, 0:1])

    qi = lambda kk, i: jnp.where(kk == 0, i, nq - 1)
    qspec = lambda off: pl.BlockSpec((bq, 128), lambda j, kk, i: (qi(kk, i), off + j))
    kspec = lambda off: pl.BlockSpec((bk, 128), lambda j, kk, i: (kk, off + j))
    rowspec = pl.BlockSpec((1, 2, bq), lambda j, kk, i: (j, 0, jnp.maximum(i, first_q(kk))))
    return pl.pallas_call(
        body, name="attn_bwd",
        out_shape=(jax.ShapeDtypeStruct((t, D_ATT), BF16), jax.ShapeDtypeStruct((t, D_ATT), BF16),
                   jax.ShapeDtypeStruct((t, D_ATT), BF16), jax.ShapeDtypeStruct((_NPAIR, t, 2), F32),
                   jax.ShapeDtypeStruct((_NPAIR, t, 2), F32)),
        grid=(_NPAIR, nk, nq),
        in_specs=[qspec(_QB), kspec(_KB), kspec(_VB),
                  pl.BlockSpec((1, bq, 2), lambda j, kk, i: (j, qi(kk, i), 0)),
                  pl.BlockSpec((1, bk, 2), lambda j, kk, i: (j, kk, 0)),
                  rowspec, rowspec, qspec(0)],
        out_specs=(pl.BlockSpec((t, 128), lambda j, kk, i: (0, j)),
                   pl.BlockSpec((bk, 128), lambda j, kk, i: (kk, j)),
                   pl.BlockSpec((bk, 128), lambda j, kk, i: (kk, j)),
                   pl.BlockSpec((1, bk, 2), lambda j, kk, i: (j, kk, 0)),
                   pl.BlockSpec((1, t, 2), lambda j, kk, i: (j, 0, 0))),
        scratch_shapes=[pltpu.VMEM((2, t, 128), BF16), pltpu.VMEM((2, t, 128), BF16), pltpu.VMEM((2, bk, 128), BF16),
                        pltpu.VMEM((2, bk, bq), F32), pltpu.VMEM((2, bk, bq), F32),
                        pltpu.VMEM((2, bk, bq), BF16), pltpu.VMEM((2, bk, bq), BF16),
                        pltpu.VMEM((2, t, 128), F32), pltpu.VMEM((2, bk, 128), F32), pltpu.VMEM((bk, 128), F32)],
        compiler_params=_cp("parallel", "arbitrary", "arbitrary"),
    )(proj, proj, proj, c_col, c_col, lse_row, dl_row, do)


def _premerge_fwd(y, o, proj, gamma):
    t = y.shape[0]
    tm = _row_tile_wide(t)

    def body(y_ref, z_ref, o_ref, za_ref, g_ref, ys_ref, ya_ref):
        z = z_ref[...]
        u = y_ref[...] * (z * _sigmoid(z))
        for g in range(G_SSD):
            gs = slice(_GW * g, _GW * (g + 1))
            ug = u[:, gs]
            r = lax.rsqrt(jnp.mean(ug * ug, axis=-1, keepdims=True) + EPS)
            ys_ref[:, gs] = (ug * r * g_ref[:, gs]).astype(BF16)
        za = za_ref[...]
        ya_ref[...] = (o_ref[...] * (za * _sigmoid(za))).astype(BF16)

    return pl.pallas_call(
        body, name="premerge_fwd",
        out_shape=(jax.ShapeDtypeStruct((t, D_SSD), BF16), jax.ShapeDtypeStruct((t, D_ATT), BF16)),
        grid=(t // tm,),
        in_specs=[pl.BlockSpec((tm, D_SSD), lambda i: (i, 0)),
                  pl.BlockSpec((tm, D_SSD), lambda i: (i, C_Z // D_SSD)),
                  pl.BlockSpec((tm, D_ATT), lambda i: (i, 0)),
                  pl.BlockSpec((tm, D_ATT), lambda i: (i, C_ZA // D_ATT)),
                  pl.BlockSpec((1, D_SSD), lambda i: (0, 0))],
        out_specs=(pl.BlockSpec((tm, D_SSD), lambda i: (i, 0)), pl.BlockSpec((tm, D_ATT), lambda i: (i, 0))),
        compiler_params=_cp("parallel"),
    )(y, proj, o, proj, gamma)


def _premerge_bwd(dys, dya, y, o, proj, gamma):
    t = y.shape[0]
    tm = _row_tile_wide(t)

    def body(dys_ref, dya_ref, y_ref, z_ref, o_ref, za_ref, g_ref, dy_ref, dz_ref, do_ref, dza_ref, dg_ref):
        i = pl.program_id(0)
        z = z_ref[...]
        sz = _sigmoid(z)
        silu = z * sz
        dsilu = sz * (1.0 + z * (1.0 - sz))
        yv = y_ref[...]
        u = yv * silu
        parts = []
        for g in range(G_SSD):
            gs = slice(_GW * g, _GW * (g + 1))
            ug = u[:, gs]
            r = lax.rsqrt(jnp.mean(ug * ug, axis=-1, keepdims=True) + EPS)
            n = ug * r
            dout = dys_ref[:, gs]
            dn = dout * g_ref[:, gs]
            du = r * (dn - n * jnp.mean(dn * n, axis=-1, keepdims=True))
            dy_ref[:, gs] = du * silu[:, gs]
            dz_ref[:, gs] = (du * yv[:, gs] * dsilu[:, gs]).astype(BF16)
            parts.append(jnp.sum(dout * n, axis=0, keepdims=True))
        dg = jnp.concatenate(parts, axis=1)
        za = za_ref[...]
        sa = _sigmoid(za)
        dya_ = dya_ref[...]
        do_ref[...] = dya_ * (za * sa)
        dza_ref[...] = (dya_ * o_ref[...] * (sa * (1.0 + za * (1.0 - sa)))).astype(BF16)

        @pl.when(i == 0)
        def _():
            dg_ref[...] = dg

        @pl.when(i > 0)
        def _():
            dg_ref[...] += dg

    ssd = pl.BlockSpec((tm, D_SSD), lambda i: (i, 0))
    att = pl.BlockSpec((tm, D_ATT), lambda i: (i, 0))
    vec = pl.BlockSpec((1, D_SSD), lambda i: (0, 0))
    return pl.pallas_call(
        body, name="premerge_bwd",
        out_shape=(jax.ShapeDtypeStruct((t, D_SSD), F32), jax.ShapeDtypeStruct((t, D_SSD), BF16),
                   jax.ShapeDtypeStruct((t, D_ATT), F32), jax.ShapeDtypeStruct((t, D_ATT), BF16),
                   jax.ShapeDtypeStruct((1, D_SSD), F32)),
        grid=(t // tm,),
        in_specs=[ssd, att, ssd, pl.BlockSpec((tm, D_SSD), lambda i: (i, C_Z // D_SSD)), att,
                  pl.BlockSpec((tm, D_ATT), lambda i: (i, C_ZA // D_ATT)), vec],
        out_specs=(ssd, ssd, att, att, vec),
        compiler_params=_cp("arbitrary"),
    )(dys, dya, y, proj, o, proj, gamma)


_G_BLK = C_G // D_MODEL


def _merge_fwd(a, b, proj, gate_bias):
    t = a.shape[0]
    tm = _row_tile(t)

    def body(a_ref, b_ref, gs_ref, ga_ref, bias_ref, m_ref):
        g_ssd = _sigmoid(gs_ref[...] + bias_ref[:, 0:D_MODEL])
        g_att = _sigmoid(ga_ref[...] + bias_ref[:, D_MODEL:2 * D_MODEL])
        m_ref[...] = (g_ssd * a_ref[...] + g_att * b_ref[...]).astype(BF16)

    row = pl.BlockSpec((tm, D_MODEL), lambda i: (i, 0))
    return pl.pallas_call(
        body, name="merge_fwd",
        out_shape=jax.ShapeDtypeStruct((t, D_MODEL), BF16),
        grid=(t // tm,),
        in_specs=[row, row,
                  pl.BlockSpec((tm, D_MODEL), lambda i: (i, _G_BLK)),
                  pl.BlockSpec((tm, D_MODEL), lambda i: (i, _G_BLK + 1)),
                  pl.BlockSpec((1, 2 * D_MODEL), lambda i: (0, 0))],
        out_specs=row,
        compiler_params=_cp("parallel"),
    )(a, b, proj, proj, gate_bias)


def _merge_bwd(dm, a, b, proj, gate_bias):
    t = a.shape[0]
    tm = _row_tile(t)

    def body(dm_ref, a_ref, b_ref, gs_ref, ga_ref, bias_ref, da_ref, db_ref, dg_ref, dbias_ref):
        i = pl.program_id(0)
        dm_ = dm_ref[...]
        g_ssd = _sigmoid(gs_ref[...] + bias_ref[:, 0:D_MODEL])
        g_att = _sigmoid(ga_ref[...] + bias_ref[:, D_MODEL:2 * D_MODEL])
        da_ref[...] = (dm_ * g_ssd).astype(BF16)
        db_ref[...] = (dm_ * g_att).astype(BF16)
        dgs = dm_ * a_ref[...] * g_ssd * (1.0 - g_ssd)
        dga = dm_ * b_ref[...] * g_att * (1.0 - g_att)
        dg_ref[:, 0:D_MODEL] = dgs.astype(BF16)
        dg_ref[:, D_MODEL:2 * D_MODEL] = dga.astype(BF16)
        part = jnp.concatenate([jnp.sum(dgs, axis=0, keepdims=True), jnp.sum(dga, axis=0, keepdims=True)], axis=1)

        @pl.when(i == 0)
        def _():
            dbias_ref[...] = part

        @pl.when(i > 0)
        def _():
            dbias_ref[...] += part

    row = pl.BlockSpec((tm, D_MODEL), lambda i: (i, 0))
    wide = pl.BlockSpec((tm, 2 * D_MODEL), lambda i: (i, 0))
    vec = pl.BlockSpec((1, 2 * D_MODEL), lambda i: (0, 0))
    return pl.pallas_call(
        body, name="merge_bwd",
        out_shape=(jax.ShapeDtypeStruct((t, D_MODEL), BF16), jax.ShapeDtypeStruct((t, D_MODEL), BF16),
                   jax.ShapeDtypeStruct((t, 2 * D_MODEL), BF16), jax.ShapeDtypeStruct((1, 2 * D_MODEL), F32)),
        grid=(t // tm,),
        in_specs=[row, row, row,
                  pl.BlockSpec((tm, D_MODEL), lambda i: (i, _G_BLK)),
                  pl.BlockSpec((tm, D_MODEL), lambda i: (i, _G_BLK + 1)), vec],
        out_specs=(row, row, wide, vec),
        compiler_params=_cp("arbitrary"),
    )(dm, a, b, proj, proj, gate_bias)


def _post(o2, h, target, g):
    t = o2.shape[0]
    nc = t // CHUNK

    def body(o_ref, h_ref, t_ref, g_ref, dy_ref, do_ref, dg_ref, loss_ref):
        c = pl.program_id(0)
        x = o_ref[...]
        r = lax.rsqrt(jnp.mean(x * x, axis=-1, keepdims=True) + EPS)
        n = x * r
        y = h_ref[...] + n * g_ref[...]
        diff = jnp.where(c > 0, y - t_ref[...], 0.0)
        dy = diff * (1.0 / D_MODEL)
        dy_ref[...] = dy
        gdy = dy * g_ref[...]
        do_ref[...] = (r * (gdy - n * jnp.mean(gdy * n, axis=-1, keepdims=True))).astype(BF16)
        dg = jnp.sum(dy * n, axis=0, keepdims=True)
        lpart = 0.5 * jnp.sum(jnp.sum(diff * diff, axis=1, keepdims=True), axis=0, keepdims=True) * (1.0 / D_MODEL)
        sel = (_iota((8, 128), 0) == 0) & (_iota((8, 128), 1) == 0)

        @pl.when(c == 0)
        def _():
            dg_ref[...] = dg
            loss_ref[...] = jnp.zeros_like(loss_ref)

        @pl.when(c > 0)
        def _():
            dg_ref[...] += dg
            loss_ref[...] += jnp.where(sel, lpart, 0.0)

    row = pl.BlockSpec((CHUNK, D_MODEL), lambda c: (c, 0))
    vec = pl.BlockSpec((1, D_MODEL), lambda c: (0, 0))
    return pl.pallas_call(
        body, name="post",
        out_shape=(jax.ShapeDtypeStruct((t, D_MODEL), F32), jax.ShapeDtypeStruct((t, D_MODEL), BF16),
                   jax.ShapeDtypeStruct((1, D_MODEL), F32), jax.ShapeDtypeStruct((8, 128), F32)),
        grid=(nc,),
        in_specs=[row, row, pl.BlockSpec((CHUNK, D_MODEL), lambda c: (jnp.maximum(c - 1, 0), 0)), vec],
        out_specs=(row, row, vec, pl.BlockSpec((8, 128), lambda c: (0, 0))),
        compiler_params=_cp("arbitrary"),
    )(o2, h, target, g)


def _mm_tiles(t):
    return _tile(t, (704, 384, 128))


def _local_step(h, target, w_main, w_small, wps, wpa, wout, norm_pre, conv_w, conv_b, bias_row, a_row,
                dsk_row, ssd_norm, gate_bias, norm_post):
    t = h.shape[0]
    tm = _mm_tiles(t)
    u = _norm1_fwd(h, norm_pre)
    proj = _matmul(u, w_main, "nt", F32, "inproj", tm, 1024, D_MODEL)
    small = _matmul(u, w_small, "nt", F32, "inproj_small", tm, N_SMALL, D_MODEL)
    dtlf = _small_fwd(small, bias_row)
    xbc = _conv_fwd(proj, conv_w, conv_b)
    y, hin = _ssd_fwd(xbc, dtlf, a_row, dsk_row)
    c_tok = dtlf[:, H_SSD:H_SSD + H_ATT]
    c_tok = jnp.where(jnp.arange(t)[:, None] < PADF, _C_FILLER, c_tok)
    c_col = c_tok.reshape(t, _NPAIR, 2).transpose(1, 0, 2)
    o, lse = _attn_fwd(proj, c_col)
    ys, ya = _premerge_fwd(y, o, proj, ssd_norm)
    a = _matmul(ys, wps, "nn", F32, "proj_ssd", tm, D_MODEL, D_SSD)
    b = _matmul(ya, wpa, "nn", F32, "proj_att", tm, D_MODEL, D_ATT)
    merged = _merge_fwd(a, b, proj, gate_bias)
    o2 = _matmul(merged, wout, "nn", F32, "out_proj", tm, D_MODEL, D_MODEL)
    dy_out, do2, d_norm_post, loss_blk = _post(o2, h, target, norm_post)

    dm = _matmul(do2, wout, "nt", F32, "out_proj_dx", tm, D_MODEL, D_MODEL)
    d_wout = _matmul(merged, do2, "tn", F32, "out_proj_dw", D_MODEL, D_MODEL, tm)
    da, db, dgraw, d_gate_bias = _merge_bwd(dm, a, b, proj, gate_bias)
    dys = _matmul(da, wps, "nt", F32, "proj_ssd_dx", tm, D_SSD, D_MODEL)
    d_wps = _matmul(ys, da, "tn", F32, "proj_ssd_dw", D_SSD, D_MODEL, tm)
    dya = _matmul(db, wpa, "nt", F32, "proj_att_dx", tm, D_ATT, D_MODEL)
    d_wpa = _matmul(ya, db, "tn", F32, "proj_att_dw", D_ATT, D_MODEL, tm)
    dy, dz, do, dza, d_ssd_norm = _premerge_bwd(dys, dya, y, o, proj, ssd_norm)
    dl_row = _attn_delta(do, o)[:, 0:H_ATT].T.reshape(_NPAIR, 2, t)
    dq, dk, dv, dc_key, dc_qry = _attn_bwd(proj, c_col, lse, dl_row, do)
    dxbc, ddt, d_a, d_dsk = _ssd_bwd(xbc, dtlf, a_row, dsk_row, hin, dy)
    dact, d_conv_w, d_conv_b = _conv_bwd_act(dxbc, proj, conv_w, conv_b)
    dxbc_raw = _conv_bwd_in(dact, conv_w)
    dc_tok = jnp.transpose(dc_key + dc_qry, (1, 0, 2)).reshape(t, H_ATT)
    dsm = ddt + jnp.pad(dc_tok, ((0, 0), (H_SSD, N_SMALL - H_SSD - H_ATT)))
    dsmall, d_bias_row = _small_bwd(dsm, small, bias_row)
    dproj = [dz, dxbc_raw, dza, dq, dk, dv, dgraw]
    return dict(loss_blk=loss_blk, u=u, dy_out=dy_out, dproj=dproj, dsmall=dsmall, d_wps=d_wps, d_wpa=d_wpa,
                d_wout=d_wout, d_conv_w=d_conv_w, d_conv_b=d_conv_b,
                d_bias_row=d_bias_row, d_a=d_a, d_dsk=d_dsk, d_ssd_norm=d_ssd_norm,
                d_gate_bias=d_gate_bias, d_norm_post=d_norm_post)


def _to_aligned_rows(w):
    def cut(o):
        return w[o[0]:o[0] + o[1]]
    main = jnp.concatenate([cut(O_Z), cut(O_XBC), cut(O_ZA), cut(O_Q), cut(O_K), cut(O_V), cut(O_G)], axis=0)
    pad = jnp.zeros((N_SMALL - H_SSD - H_ATT, w.shape[1]), w.dtype)
    small = jnp.concatenate([cut(O_DT), cut(O_F), pad], axis=0)
    return main, small


def _from_aligned_rows(main, small):
    def cm(c0, n):
        return main[c0:c0 + n]
    return jnp.concatenate([cm(C_Z, 2048), cm(C_XBC, 3072), small[0:H_SSD], cm(C_ZA, 1024),
                            cm(C_Q, 1024), cm(C_K, 1024), cm(C_V, 1024), small[H_SSD:H_SSD + H_ATT],
                            cm(C_G, 2048)], axis=0)


_MESH = pl.DeviceIdType.MESH
_ANY = pl.BlockSpec(memory_space=pl.ANY)
_VM = pl.BlockSpec(memory_space=pltpu.VMEM)
_HALF = 512
N_DEV = 8


def _coords():
    return lax.axis_index("x"), lax.axis_index("y"), lax.axis_index("c")


def _other_chips(x, y):
    return [(1 - x, y), (x, 1 - y), (1 - x, 1 - y)]


def _half(cc):
    return pl.ds(cc * _HALF, _HALF)


def _gather_shards(slots):
    n = len(slots)

    def body(*refs):
        buf = refs[n:2 * n]
        send_sems, recv_sems = refs[2 * n:]
        x, y, c = _coords()
        chip = 2 * x + y
        sibling = (x, y, 1 - c)
        chips = _other_chips(x, y)

        def copy(i, frm, cc, k, to):
            part = buf[i].at[frm, :, _half(cc)]
            return pltpu.make_async_remote_copy(src_ref=part, dst_ref=part, send_sem=send_sems.at[6 * i + k],
                                                recv_sem=recv_sems.at[6 * i + k], device_id=to, device_id_type=_MESH)

        def chip_of(k):
            return 2 * chips[k][0] + chips[k][1]

        first = [copy(i, chip, c, k, (*chips[k], c)) for k in range(3) for i in range(n)]
        for cp in first:
            cp.start()
        passed = []
        for k in range(3):
            for i in range(n):
                copy(i, chip_of(k), c, k, (*chips[k], c)).wait_recv()
                passed.append(copy(i, chip_of(k), c, 3 + k, sibling))
                passed[-1].start()
        for k in range(3):
            for i in range(n):
                copy(i, chip_of(k), 1 - c, 3 + k, sibling).wait_recv()
        for cp in first + passed:
            cp.wait_send()

    return pl.pallas_call(
        body, name="gather_shards",
        out_shape=tuple(jax.ShapeDtypeStruct(s.shape, s.dtype) for s in slots),
        in_specs=[_ANY] * n, out_specs=tuple([_ANY] * n),
        input_output_aliases={i: i for i in range(n)},
        scratch_shapes=[pltpu.SemaphoreType.DMA((6 * n,)), pltpu.SemaphoreType.DMA((6 * n,))],
    )(*slots)


def _allgather8(block, name):
    rows, width = block.shape

    def body(x_ref, out_ref, send_sems, recv_sems, local_sem):
        x, y, c = _coords()
        me, sibling = (x, y, c), (x, y, 1 - c)
        chips = _other_chips(x, y)

        def slot(px, py, pc):
            return out_ref.at[4 * px + 2 * py + pc]

        def copy(k, blk, to, src=None):
            return pltpu.make_async_remote_copy(src_ref=slot(*blk) if src is None else src, dst_ref=slot(*blk),
                                                send_sem=send_sems.at[k], recv_sem=recv_sems.at[k],
                                                device_id=to, device_id_type=_MESH)

        mine = pltpu.make_async_copy(x_ref, slot(*me), local_sem)
        mine.start()
        first = [copy(0, me, sibling, src=x_ref)]
        first += [copy(1 + j, me, (*chip, c), src=x_ref) for j, chip in enumerate(chips)]
        for cp in first:
            cp.start()
        passed = [copy(4 + j, (*chip, c), sibling) for j, chip in enumerate(chips)]
        for j, chip in enumerate(chips):
            copy(1 + j, (*chip, c), me).wait_recv()
            passed[j].start()
        copy(0, sibling, me).wait_recv()
        for j, chip in enumerate(chips):
            copy(4 + j, (*chip, 1 - c), me).wait_recv()
        for cp in first + passed:
            cp.wait_send()
        mine.wait()

    return pl.pallas_call(
        body, name=name,
        out_shape=jax.ShapeDtypeStruct((N_DEV, rows, width), block.dtype),
        in_specs=[_VM], out_specs=_VM,
        scratch_shapes=[pltpu.SemaphoreType.DMA((7,)), pltpu.SemaphoreType.DMA((7,)), pltpu.SemaphoreType.DMA],
    )(block)


def _pair_swap(arrs):
    def copies(src, dst, send_sems, recv_sems):
        x, y, c = _coords()
        return [pltpu.make_async_remote_copy(src_ref=src[i].at[:, :, _half(1 - c)], dst_ref=dst[i],
                                             send_sem=send_sems.at[i], recv_sem=recv_sems.at[i],
                                             device_id=(x, y, 1 - c), device_id_type=_MESH) for i in range(len(src))]

    shapes = tuple(jax.ShapeDtypeStruct((4, a.shape[1], _HALF), a.dtype) for a in arrs)
    return tuple(arrs), shapes, copies, len(arrs)


def _chip_exchange(arrs):
    def copies(src, dst, send_sems, recv_sems):
        x, y, c = _coords()
        chips = _other_chips(x, y)
        return [pltpu.make_async_remote_copy(src_ref=src[i].at[2 * chips[k][0] + chips[k][1]], dst_ref=dst[i].at[k],
                                             send_sem=send_sems.at[3 * i + k], recv_sem=recv_sems.at[3 * i + k],
                                             device_id=(*chips[k], c), device_id_type=_MESH)
                for k in range(3) for i in range(len(src))]

    shapes = tuple(jax.ShapeDtypeStruct((3,) + a.shape[1:], a.dtype) for a in arrs)
    return tuple(arrs), shapes, copies, 3 * len(arrs)


def _pair_join_halves(fulls):
    n = len(fulls)

    def body(*refs):
        buf = refs[n:2 * n]
        send_sems, recv_sems = refs[2 * n:]
        x, y, c = _coords()

        def remote(i, cc):
            part = buf[i].at[:, _half(cc)]
            return pltpu.make_async_remote_copy(src_ref=part, dst_ref=part, send_sem=send_sems.at[i],
                                                recv_sem=recv_sems.at[i], device_id=(x, y, 1 - c), device_id_type=_MESH)

        for i in range(n):
            remote(i, c).start()
        for i in range(n):
            remote(i, c).wait_send()
            remote(i, 1 - c).wait_recv()

    return pl.pallas_call(
        body, name="pair_join_halves",
        out_shape=tuple(jax.ShapeDtypeStruct(a.shape, a.dtype) for a in fulls),
        in_specs=[_ANY] * n, out_specs=tuple([_ANY] * n),
        input_output_aliases={i: i for i in range(n)},
        scratch_shapes=[pltpu.SemaphoreType.DMA((n,)), pltpu.SemaphoreType.DMA((n,))],
    )(*fulls)


_RED_TC = 128
_RED_NT = _HALF // _RED_TC


def _add_pair(ids, g32, recv_a):
    rows = g32.shape[1]

    def body(ids_ref, g_ref, r_ref, o_ref):
        o_ref[...] = (g_ref[...] + r_ref[...]).astype(BF16)

    blk = pl.BlockSpec((1, rows, _RED_TC), lambda j, l, ids: (j, 0, l))
    return pl.pallas_call(
        body, name="add_pair",
        out_shape=jax.ShapeDtypeStruct((4, rows, _HALF), BF16),
        grid_spec=pltpu.PrefetchScalarGridSpec(
            num_scalar_prefetch=1, grid=(4, _RED_NT),
            in_specs=[pl.BlockSpec((1, rows, _RED_TC), lambda j, l, ids: (j, 0, ids[0] * _RED_NT + l)), blk],
            out_specs=blk),
        compiler_params=_cp("parallel", "parallel"),
    )(ids, g32, recv_a)


def _add_chips(ids, g32, recv_a, recv_b):
    rows = g32.shape[1]

    def body(ids_ref, g_ref, a_ref, b_ref, o_ref):
        acc = g_ref[0] + a_ref[0]
        for k in range(3):
            acc = acc + b_ref[k].astype(F32)
        o_ref[...] = acc

    return pl.pallas_call(
        body, name="add_chips",
        out_shape=jax.ShapeDtypeStruct((rows, 2 * _HALF), F32),
        grid_spec=pltpu.PrefetchScalarGridSpec(
            num_scalar_prefetch=1, grid=(_RED_NT,),
            in_specs=[pl.BlockSpec((1, rows, _RED_TC), lambda l, ids: (ids[1], 0, ids[0] * _RED_NT + l)),
                      pl.BlockSpec((1, rows, _RED_TC), lambda l, ids: (ids[1], 0, l)),
                      pl.BlockSpec((3, rows, _RED_TC), lambda l, ids: (0, 0, l))],
            out_specs=pl.BlockSpec((rows, _RED_TC), lambda l, ids: (0, ids[0] * _RED_NT + l))),
        compiler_params=_cp("parallel"),
    )(ids, g32, recv_a, recv_b)


def _sum8(gathered):
    _, rows, width = gathered.shape

    def body(g_ref, o_ref):
        acc = g_ref[0]
        for d in range(1, N_DEV):
            acc = acc + g_ref[d]
        o_ref[...] = acc

    return pl.pallas_call(
        body, name="sum8",
        out_shape=jax.ShapeDtypeStruct((rows, width), F32),
        in_specs=[_VM], out_specs=_VM,
    )(gathered)


def _adamw(w, g, m, v, name):
    rows, cols = w.shape
    budget = (3 << 20) // 2
    tr, tc = rows, cols
    if rows * cols * 4 > budget:
        if rows % 8 == 0:
            tr = next(c for c in (512, 256, 128, 64, 32, 16, 8) if rows % c == 0 and c * cols * 4 <= budget)
        else:
            tc = next(c for c in (512, 256, 128) if cols % c == 0 and rows * c * 4 <= budget)
    c1 = 1.0 - ADAM_B1 ** ADAM_STEP
    c2 = 1.0 - ADAM_B2 ** ADAM_STEP

    def body(w_ref, g_ref, m_ref, v_ref, d_ref, mo_ref, vo_ref):
        gg = g_ref[...]
        mn = ADAM_B1 * m_ref[...] + (1.0 - ADAM_B1) * gg
        vn = ADAM_B2 * v_ref[...] + (1.0 - ADAM_B2) * (gg * gg)
        mo_ref[...] = mn
        vo_ref[...] = vn
        d_ref[...] = -ADAM_LR * ((mn / c1) / (jnp.sqrt(vn / c2) + ADAM_EPS) + ADAM_WD * w_ref[...])

    blk = pl.BlockSpec((tr, tc), lambda i, j: (i, j))
    shp = jax.ShapeDtypeStruct((rows, cols), F32)
    return pl.pallas_call(
        body, name=name, out_shape=(shp, shp, shp), grid=(rows // tr, cols // tc),
        in_specs=[blk] * 4, out_specs=(blk, blk, blk),
        compiler_params=_cp("parallel", "parallel"),
    )(w, g, m, v)


def _rows128(a):
    return a.reshape(-1, 128)


def _pack_small(norm_pre, conv_b, ssd_norm, gate_bias, norm_post, dt_bias, a_log, d_skip, fgate_bias):
    tiny = jnp.concatenate([dt_bias.reshape(-1), a_log.reshape(-1), d_skip.reshape(-1), fgate_bias.reshape(-1),
                            jnp.zeros((16,), F32)])
    return jnp.concatenate([_rows128(norm_pre), _rows128(conv_b), _rows128(ssd_norm), _rows128(gate_bias),
                            _rows128(norm_post), tiny.reshape(1, 128)], axis=0)


_SMALL_ROWS = 73
_SMALL_PAD = 80


def _unpack_small(p):
    tiny = p[72]
    return dict(norm_pre=p[0:8].reshape(1, 1024), conv_b=p[8:32].reshape(1, 3072), ssd_norm=p[32:48].reshape(1, 2048),
                gate_bias=p[48:64].reshape(1, 2048), norm_post=p[64:72].reshape(1, 1024),
                dt_bias=tiny[0:32].reshape(1, 32), a_log=tiny[32:64].reshape(1, 32),
                d_skip=tiny[64:96].reshape(1, 32), fgate_bias=tiny[96:112].reshape(1, 16))


def _pad_rows(a, rows):
    return jnp.concatenate([a, jnp.zeros((rows - a.shape[0], a.shape[1]), a.dtype)], axis=0)


def kernel(x, meta_tokens, norm_pre, w_in, conv_w, conv_b, dt_bias, a_log, d_skip, ssd_norm, fgate_bias, gate_bias, w_proj_ssd, w_proj_att, w_out, norm_post, loss_target, m_meta_tokens, m_norm_pre, m_w_in, m_conv_w, m_conv_b, m_dt_bias, m_a_log, m_d_skip, m_ssd_norm, m_fgate_bias, m_gate_bias, m_w_proj_ssd, m_w_proj_att, m_w_out, m_norm_post, v_meta_tokens, v_norm_pre, v_w_in, v_conv_w, v_conv_b, v_dt_bias, v_a_log, v_d_skip, v_ssd_norm, v_fgate_bias, v_gate_bias, v_w_proj_ssd, v_w_proj_att, v_w_out, v_norm_post):
    cx, cy, cc = _coords()
    chip = 2 * cx + cy
    ids = jnp.stack([cc, chip]).astype(jnp.int32)
    seq = x.shape[1]

    w_in_sh = jnp.transpose(w_in[0]).astype(BF16)
    w_pr_sh = jnp.concatenate([w_proj_ssd[0], w_proj_att[0], w_out[0]], axis=0).astype(BF16)

    def own_slot(sh):
        return lax.dynamic_update_slice(lax.empty((4,) + sh.shape, sh.dtype), sh[None], (chip, 0, 0))

    g_in, g_pr = _gather_shards([own_slot(w_in_sh), own_slot(w_pr_sh)])
    w_main, w_small = _to_aligned_rows(g_in.reshape(N_COLS, D_MODEL))
    wps = g_pr[:, 0:512].reshape(D_SSD, D_MODEL)
    wpa = g_pr[:, 512:768].reshape(D_ATT, D_MODEL)
    wout = g_pr[:, 768:1024].reshape(D_MODEL, D_MODEL)
    sm_sh = jnp.concatenate([_rows128(meta_tokens), _rows128(conv_w[0])], axis=0)
    sm_all = _allgather8(sm_sh, "gather_small_weights")[0::2]
    meta_full = jnp.transpose(sm_all[:, 0:32].reshape(4, N_META, 256), (1, 0, 2)).reshape(N_META, D_MODEL)
    conv_w_full = jnp.transpose(sm_all[:, 32:56].reshape(4, CONV_K, 768), (1, 0, 2)).reshape(CONV_K, CONV_DIM)

    h = jnp.concatenate([jnp.zeros((PADF, D_MODEL), F32), meta_full, x[0]], axis=0)
    bias_row = jnp.concatenate([dt_bias[0], fgate_bias[0], jnp.zeros((N_SMALL - H_SSD - H_ATT,), F32)]).reshape(1, N_SMALL)
    a_neg = -jnp.exp(a_log[0])
    a_row = jnp.concatenate([a_neg, jnp.zeros((N_SMALL - H_SSD,), F32)]).reshape(1, N_SMALL)
    dsk_row = jnp.repeat(d_skip[0], 64).reshape(1, D_SSD)
    r = _local_step(h, loss_target[0], w_main, w_small, wps, wpa, wout, norm_pre, conv_w_full, conv_b, bias_row,
                    a_row, dsk_row, ssd_norm, gate_bias, norm_post)

    tm = _mm_tiles(h.shape[0])
    n_row_tiles = h.shape[0] // tm
    d_w_main = _matmul_cat_tn(r["dproj"], r["u"], "inproj_dw", tm)
    d_w_small = _matmul(r["dsmall"], r["u"], "tn", F32, "inproj_small_dw", N_SMALL, D_MODEL, tm)
    g32_in = _from_aligned_rows(d_w_main, d_w_small).reshape(4, N_COLS // 4, D_MODEL)
    g32_pr = jnp.concatenate([r["d_wps"].reshape(4, 512, D_MODEL), r["d_wpa"].reshape(4, 256, D_MODEL),
                              r["d_wout"].reshape(4, 256, D_MODEL)], axis=1)
    first = n_row_tiles // 2
    du_first, ra_in, ra_pr = _matmul_cat_nn(r["dproj"], w_main, "inproj_dx_swap", tm, rows=(0, first),
                                            exchange=_pair_swap([g32_in, g32_pr]))
    pb_in = _add_pair(ids, g32_in, ra_in)
    pb_pr = _add_pair(ids, g32_pr, ra_pr)
    du_a, rb_in, rb_pr = _matmul_cat_nn(r["dproj"], w_main, "inproj_dx_exchange", tm,
                                        rows=(first, n_row_tiles - first), fill=du_first,
                                        exchange=_chip_exchange([pb_in, pb_pr]))
    du_b = _matmul(r["dsmall"], w_small, "nn", F32, "inproj_small_dx", tm, D_MODEL, N_SMALL)
    dh, d_norm_pre = _norm1_bwd(du_a, du_b, h, norm_pre, r["dy_out"])
    grad_x = dh[PADF + N_META:].reshape(1, seq, D_MODEL)
    half_in = _add_chips(ids, g32_in, ra_in, rb_in)
    half_pr = _add_chips(ids, g32_pr, ra_pr, rb_pr)
    gw_in, gw_pr = _pair_join_halves([half_in, half_pr])

    tiny = r["d_bias_row"][0]
    part_small = _pack_small(d_norm_pre, r["d_conv_b"], r["d_ssd_norm"], r["d_gate_bias"], r["d_norm_post"],
                             tiny[0:H_SSD], r["d_a"][0, 0:H_SSD] * a_neg, r["d_dsk"].reshape(H_SSD, 64).sum(axis=1),
                             tiny[H_SSD:H_SSD + H_ATT])
    part = jnp.concatenate([_pad_rows(part_small, _SMALL_PAD), _rows128(r["d_conv_w"]),
                            _rows128(dh[PADF:PADF + N_META]), r["loss_blk"]], axis=0)
    tot = _sum8(_allgather8(part, "gather_small_grads"))
    loss = tot[_SMALL_PAD + 96 + 128, 0]
    g_small = tot[0:_SMALL_PAD]
    g_conv_w = lax.dynamic_slice_in_dim(tot[_SMALL_PAD:_SMALL_PAD + 96].reshape(CONV_K, CONV_DIM), chip * 768, 768, axis=1)
    g_meta = lax.dynamic_slice_in_dim(tot[_SMALL_PAD + 96:_SMALL_PAD + 224].reshape(N_META, D_MODEL), chip * 256, 256, axis=1)

    upd = {}
    upd["w_in"] = tuple(jnp.transpose(a) for a in (gw_in,) + _adamw(
        jnp.transpose(w_in[0]), gw_in, jnp.transpose(m_w_in[0]), jnp.transpose(v_w_in[0]), "adamw_w_in"))
    w_pr32 = jnp.concatenate([w_proj_ssd[0], w_proj_att[0], w_out[0]], axis=0)
    m_pr = jnp.concatenate([m_w_proj_ssd[0], m_w_proj_att[0], m_w_out[0]], axis=0)
    v_pr = jnp.concatenate([v_w_proj_ssd[0], v_w_proj_att[0], v_w_out[0]], axis=0)
    pr = (gw_pr,) + _adamw(w_pr32, gw_pr, m_pr, v_pr, "adamw_w_proj")
    upd["w_proj_ssd"] = tuple(a[0:512] for a in pr)
    upd["w_proj_att"] = tuple(a[512:768] for a in pr)
    upd["w_out"] = tuple(a[768:1024] for a in pr)
    upd["conv_w"] = (g_conv_w,) + _adamw(conv_w[0], g_conv_w, m_conv_w[0], v_conv_w[0], "adamw_conv_w")
    upd["meta_tokens"] = (g_meta,) + _adamw(meta_tokens, g_meta, m_meta_tokens, v_meta_tokens, "adamw_meta")
    pk = lambda np_, cb, sn, gb, npo, dtb, al, ds, fg: _pad_rows(_pack_small(np_, cb, sn, gb, npo, dtb, al, ds, fg), _SMALL_PAD)
    w_sm = pk(norm_pre, conv_b, ssd_norm, gate_bias, norm_post, dt_bias, a_log, d_skip, fgate_bias)
    m_sm = pk(m_norm_pre, m_conv_b, m_ssd_norm, m_gate_bias, m_norm_post, m_dt_bias, m_a_log, m_d_skip, m_fgate_bias)
    v_sm = pk(v_norm_pre, v_conv_b, v_ssd_norm, v_gate_bias, v_norm_post, v_dt_bias, v_a_log, v_d_skip, v_fgate_bias)
    sm = [_unpack_small(a) for a in (g_small,) + _adamw(w_sm, g_small, m_sm, v_sm, "adamw_small")]
    for name in ("norm_pre", "conv_b", "dt_bias", "a_log", "d_skip", "ssd_norm", "fgate_bias", "gate_bias", "norm_post"):
        upd[name] = tuple(s[name] for s in sm)
    lead = ("w_in", "conv_w", "w_proj_ssd", "w_proj_att", "w_out")
    order = ("meta_tokens", "norm_pre", "w_in", "conv_w", "conv_b", "dt_bias", "a_log", "d_skip", "ssd_norm",
             "fgate_bias", "gate_bias", "w_proj_ssd", "w_proj_att", "w_out", "norm_post")
    outs = [loss, grad_x]
    for part_i in range(4):
        for name in order:
            a = upd[name][part_i]
            outs.append(a[None] if name in lead else a)
    return tuple(outs)
```

```python
import functools
import math

import jax
import jax.numpy as jnp
from jax import lax
from jax.experimental import pallas as pl
from jax.experimental.pallas import tpu as pltpu

F32 = jnp.float32
BF16 = jnp.bfloat16
HIGHEST = lax.Precision.HIGHEST

D_MODEL = 1024
N_META = 16
CHUNK = 128
PADF = CHUNK - N_META
D_SSD = 2048
H_SSD = 32
G_SSD = 4
N_STATE = 128
CONV_K = 4
CONV_DIM = D_SSD + 2 * G_SSD * N_STATE
H_ATT = 16
D_ATT = 1024
EPS = 1e-6
N_COLS = 11312

C_Z, C_XBC, C_ZA, C_Q, C_K, C_V, C_G = 0, 2048, 5120, 6144, 7168, 8192, 9216
N_MAIN = 11264
N_SMALL = 128
O_Z, O_XBC, O_DT, O_ZA, O_Q, O_K, O_V, O_F, O_G = (
    (0, 2048), (2048, 3072), (5120, 32), (5152, 1024), (6176, 1024), (7200, 1024),
    (8224, 1024), (9248, 16), (9264, 2048))

ADAM_LR, ADAM_B1, ADAM_B2, ADAM_EPS, ADAM_WD, ADAM_STEP = 0.001, 0.9, 0.999, 1e-08, 0.01, 10

VMEM_LIMIT = 56 * 1024 * 1024


def _cp(*sem):
    return pltpu.CompilerParams(dimension_semantics=sem, vmem_limit_bytes=VMEM_LIMIT)


def _tile(n, prefs):
    for p in prefs:
        if n % p == 0:
            return p
    raise ValueError(f"no tile for {n} in {prefs}")


def _iota(shape, dim):
    return lax.broadcasted_iota(jnp.int32, shape, dim)


def _sigmoid(x):
    return 1.0 / (1.0 + jnp.exp(-x))


def _softplus_tail(x):
    return jnp.log(1.0 + jnp.exp(-jnp.abs(x)))


_NN = (((1,), (0,)), ((), ()))
_NT = (((1,), (1,)), ((), ()))
_TN = (((0,), (0,)), ((), ()))


def _dot(a, b, dims=_NN):
    return lax.dot_general(a, b, dims, preferred_element_type=F32)


def _dot_exact(a, b, dims=_NN):
    return lax.dot_general(a, b, dims, precision=HIGHEST, preferred_element_type=F32)


def _matmul(a, b, mode, out_dtype, name, tm, tn, tk):
    if mode == "tn":
        kdim, m = a.shape
    else:
        m, kdim = a.shape
    n = b.shape[0] if mode == "nt" else b.shape[1]
    nk = kdim // tk
    dims = {"nn": _NN, "nt": _NT, "tn": _TN}[mode]
    a_spec = (pl.BlockSpec((tk, tm), lambda i, j, k: (k, i)) if mode == "tn"
              else pl.BlockSpec((tm, tk), lambda i, j, k: (i, k)))
    b_spec = (pl.BlockSpec((tn, tk), lambda i, j, k: (j, k)) if mode == "nt"
              else pl.BlockSpec((tk, tn), lambda i, j, k: (k, j)))

    def body(a_ref, b_ref, o_ref, acc_ref):
        k = pl.program_id(2)
        p = _dot(a_ref[...].astype(BF16), b_ref[...].astype(BF16), dims)
        if nk == 1:
            o_ref[...] = p.astype(out_dtype)
        else:
            @pl.when(k == 0)
            def _():
                acc_ref[...] = p

            @pl.when(k > 0)
            def _():
                acc_ref[...] += p

            @pl.when(k == nk - 1)
            def _():
                o_ref[...] = acc_ref[...].astype(out_dtype)

    return pl.pallas_call(
        body, name=name,
        out_shape=jax.ShapeDtypeStruct((m, n), out_dtype),
        grid=(m // tm, n // tn, nk),
        in_specs=[a_spec, b_spec],
        out_specs=pl.BlockSpec((tm, tn), lambda i, j, k: (i, j)),
        scratch_shapes=[pltpu.VMEM((tm, tn), F32)],
        compiler_params=_cp("parallel", "parallel", "arbitrary"),
    )(a, b)


_CAT_BLK = 1024


def _piece_ranges(pieces):
    out, off = [], 0
    for p in pieces:
        nb = p.shape[1] // _CAT_BLK
        out.append((off, nb))
        off += nb
    return out, off


def _matmul_cat_nn(pieces, b, name, tm, rows=None, fill=None, exchange=None):
    t = pieces[0].shape[0]
    n = b.shape[1]
    ranges, nk = _piece_ranges(pieces)
    first, ni = rows if rows is not None else (0, t // tm)
    ex_ops, ex_shapes, ex_copies, n_sems = exchange if exchange is not None else ((), (), None, 0)
    n_in = len(pieces) + 1 + (fill is not None) + len(ex_ops)

    def body(*refs):
        a_refs, b_ref = refs[:len(pieces)], refs[len(pieces)]
        o_ref = refs[n_in]
        acc_ref = refs[n_in + 1 + len(ex_shapes)]
        i = pl.program_id(0)
        k = pl.program_id(1)
        if ex_copies is not None:
            def copies():
                return ex_copies(refs[n_in - len(ex_ops):n_in], refs[n_in + 1:n_in + 1 + len(ex_shapes)],
                                 refs[-2], refs[-1])

            @pl.when((i == 0) & (k == 0))
            def _():
                for cp in copies():
                    cp.start()

        @pl.when(k == 0)
        def _():
            acc_ref[...] = jnp.zeros_like(acc_ref)

        for a_ref, (off, nb) in zip(a_refs, ranges):
            @pl.when((k >= off) & (k < off + nb))
            def _(a_ref=a_ref):
                acc_ref[...] += _dot(a_ref[...], b_ref[...])

        @pl.when(k == nk - 1)
        def _():
            o_ref[...] = acc_ref[...]

        if ex_copies is not None:
            @pl.when((i == ni - 1) & (k == nk - 1))
            def _():
                for cp in copies():
                    cp.wait()

    def a_spec(off, nb):
        return pl.BlockSpec((tm, _CAT_BLK), lambda i, k: (first + i, jnp.clip(k - off, 0, nb - 1)))

    in_specs = [a_spec(off, nb) for off, nb in ranges] + [pl.BlockSpec((_CAT_BLK, n), lambda i, k: (k, 0))]
    operands = list(pieces) + [b]
    if fill is not None:
        in_specs.append(_ANY)
        operands.append(fill)
    in_specs += [_ANY] * len(ex_ops)
    operands += list(ex_ops)
    scratch = [pltpu.VMEM((tm, n), F32)]
    if n_sems:
        scratch += [pltpu.SemaphoreType.DMA((n_sems,)), pltpu.SemaphoreType.DMA((n_sems,))]
    out = pl.pallas_call(
        body, name=name,
        out_shape=(jax.ShapeDtypeStruct((t, n), F32),) + tuple(ex_shapes),
        grid=(ni, nk),
        in_specs=in_specs,
        out_specs=(pl.BlockSpec((tm, n), lambda i, k: (first + i, 0)),) + (_ANY,) * len(ex_shapes),
        input_output_aliases={len(pieces) + 1: 0} if fill is not None else {},
        scratch_shapes=scratch,
        compiler_params=_cp("arbitrary" if n_sems else "parallel", "arbitrary"),
    )(*operands)
    return out if exchange is not None else out[0]


def _matmul_cat_tn(pieces, b, name, tk):
    t = pieces[0].shape[0]
    n = b.shape[1]
    ranges, nm = _piece_ranges(pieces)
    nk = t // tk

    def body(*refs):
        a_refs, b_ref, o_ref, acc_ref = refs[:len(pieces)], refs[-3], refs[-2], refs[-1]
        m = pl.program_id(0)
        k = pl.program_id(1)

        @pl.when(k == 0)
        def _():
            acc_ref[...] = jnp.zeros_like(acc_ref)

        for a_ref, (off, nb) in zip(a_refs, ranges):
            @pl.when((m >= off) & (m < off + nb))
            def _(a_ref=a_ref):
                acc_ref[...] += _dot(a_ref[...], b_ref[...], _TN)

        @pl.when(k == nk - 1)
        def _():
            o_ref[...] = acc_ref[...]

    def a_spec(off, nb):
        def index(m, k):
            mine = (m >= off) & (m < off + nb)
            return jnp.where(mine, k, 0), jnp.clip(m - off, 0, nb - 1)
        return pl.BlockSpec((tk, _CAT_BLK), index)

    return pl.pallas_call(
        body, name=name,
        out_shape=jax.ShapeDtypeStruct((nm * _CAT_BLK, n), F32),
        grid=(nm, nk),
        in_specs=[a_spec(off, nb) for off, nb in ranges] + [pl.BlockSpec((tk, n), lambda m, k: (k, 0))],
        out_specs=pl.BlockSpec((_CAT_BLK, n), lambda m, k: (m, 0)),
        scratch_shapes=[pltpu.VMEM((_CAT_BLK, n), F32)],
        compiler_params=_cp("parallel", "arbitrary"),
    )(*pieces, b)


def _row_tile(t):
    return _tile(t, (352, 128))


def _row_tile_wide(t):
    return _tile(t, (176, 128))


def _norm1_fwd(h, g):
    t = h.shape[0]
    tm = _row_tile(t)

    def body(h_ref, g_ref, u_ref):
        x = h_ref[...]
        r = lax.rsqrt(jnp.mean(x * x, axis=-1, keepdims=True) + EPS)
        u_ref[...] = (x * r * g_ref[...]).astype(BF16)

    return pl.pallas_call(
        body, name="norm1_fwd",
        out_shape=jax.ShapeDtypeStruct((t, D_MODEL), BF16),
        grid=(t // tm,),
        in_specs=[pl.BlockSpec((tm, D_MODEL), lambda i: (i, 0)),
                  pl.BlockSpec((1, D_MODEL), lambda i: (0, 0))],
        out_specs=pl.BlockSpec((tm, D_MODEL), lambda i: (i, 0)),
        compiler_params=_cp("parallel"),
    )(h, g)


def _norm1_bwd(du_a, du_b, h, g, dy):
    t = h.shape[0]
    tm = _row_tile(t)

    def body(a_ref, b_ref, h_ref, g_ref, dy_ref, dh_ref, dg_ref):
        i = pl.program_id(0)
        x = h_ref[...]
        du = a_ref[...] + b_ref[...]
        r = lax.rsqrt(jnp.mean(x * x, axis=-1, keepdims=True) + EPS)
        gdu = du * g_ref[...]
        dh_ref[...] = dy_ref[...] + r * (gdu - x * (r * r) * jnp.mean(gdu * x, axis=-1, keepdims=True))
        part = jnp.sum(du * x * r, axis=0, keepdims=True)

        @pl.when(i == 0)
        def _():
            dg_ref[...] = part

        @pl.when(i > 0)
        def _():
            dg_ref[...] += part

    row = pl.BlockSpec((tm, D_MODEL), lambda i: (i, 0))
    vec = pl.BlockSpec((1, D_MODEL), lambda i: (0, 0))
    return pl.pallas_call(
        body, name="norm1_bwd",
        out_shape=(jax.ShapeDtypeStruct((t, D_MODEL), F32), jax.ShapeDtypeStruct((1, D_MODEL), F32)),
        grid=(t // tm,),
        in_specs=[row, row, row, vec, row],
        out_specs=(row, vec),
        compiler_params=_cp("arbitrary"),
    )(du_a, du_b, h, g, dy)


def _small_fwd(small, bias_row):
    t = small.shape[0]

    def body(s_ref, b_ref, o_ref, carry_ref):
        c = pl.program_id(0)

        @pl.when(c == 0)
        def _():
            carry_ref[...] = jnp.zeros_like(carry_ref)

        x = s_ref[...] + b_ref[...]
        r0 = _iota((CHUNK, CHUNK), 0)
        r1 = _iota((CHUNK, CHUNK), 1)
        valid = (c * CHUNK + r0) >= PADF
        tail = _softplus_tail(x)
        dt = jnp.where(valid & (r1 < H_SSD), jnp.maximum(x, 0.0) + tail, 0.0)
        lf = jnp.where(valid & (r1 >= H_SSD) & (r1 < H_SSD + H_ATT), jnp.minimum(x, 0.0) - tail, 0.0)
        tri = (r0 >= r1).astype(F32)
        cs = _dot_exact(tri, lf) + carry_ref[...]
        carry_ref[...] = cs[CHUNK - 1:CHUNK, :]
        o_ref[...] = dt + cs

    return pl.pallas_call(
        body, name="small_fwd",
        out_shape=jax.ShapeDtypeStruct((t, N_SMALL), F32),
        grid=(t // CHUNK,),
        in_specs=[pl.BlockSpec((CHUNK, N_SMALL), lambda c: (c, 0)),
                  pl.BlockSpec((1, N_SMALL), lambda c: (0, 0))],
        out_specs=pl.BlockSpec((CHUNK, N_SMALL), lambda c: (c, 0)),
        scratch_shapes=[pltpu.VMEM((1, N_SMALL), F32)],
        compiler_params=_cp("arbitrary"),
    )(small, bias_row)


def _small_bwd(dsm, small, bias_row):
    t = small.shape[0]
    nc = t // CHUNK

    def body(d_ref, s_ref, b_ref, o_ref, db_ref, carry_ref):
        step = pl.program_id(0)
        c = nc - 1 - step

        @pl.when(step == 0)
        def _():
            carry_ref[...] = jnp.zeros_like(carry_ref)
            db_ref[...] = jnp.zeros_like(db_ref)

        x = s_ref[...] + b_ref[...]
        d = d_ref[...]
        r0 = _iota((CHUNK, CHUNK), 0)
        r1 = _iota((CHUNK, CHUNK), 1)
        valid = (c * CHUNK + r0) >= PADF
        is_dt = r1 < H_SSD
        is_f = (r1 >= H_SSD) & (r1 < H_SSD + H_ATT)
        triu = (r1 >= r0).astype(F32)
        dc = jnp.where(is_f, d, 0.0)
        dlf = _dot_exact(triu, dc) + carry_ref[...]
        carry_ref[...] = dlf[0:1, :]
        sg = _sigmoid(x)
        out = jnp.where(valid & is_dt, d * sg, 0.0) + jnp.where(valid & is_f, dlf * (1.0 - sg), 0.0)
        o_ref[...] = out.astype(BF16)
        db_ref[...] += jnp.sum(out, axis=0, keepdims=True)

    blk = pl.BlockSpec((CHUNK, N_SMALL), lambda s: (nc - 1 - s, 0))
    vec = pl.BlockSpec((1, N_SMALL), lambda s: (0, 0))
    return pl.pallas_call(
        body, name="small_bwd",
        out_shape=(jax.ShapeDtypeStruct((t, N_SMALL), BF16), jax.ShapeDtypeStruct((1, N_SMALL), F32)),
        grid=(nc,),
        in_specs=[blk, blk, vec],
        out_specs=(blk, vec),
        scratch_shapes=[pltpu.VMEM((1, N_SMALL), F32)],
        compiler_params=_cp("arbitrary"),
    )(dsm, small, bias_row)


_CONV_TC = 1024
_XBC_BLK = C_XBC // _CONV_TC


def _shift_down(cur, prev8, j):
    rc = pltpu.roll(cur, j, 0)
    rid = _iota(prev8.shape, 0)
    top = jnp.where(rid < j, pltpu.roll(prev8, j, 0), rc[0:8, :])
    return jnp.concatenate([top, rc[8:, :]], axis=0)


def _shift_up(cur, next8, j):
    n = cur.shape[0]
    ru = pltpu.roll(cur, n - j, 0)
    rid = _iota(next8.shape, 0)
    bot = jnp.where(rid >= 8 - j, pltpu.roll(next8, 8 - j, 0), ru[n - 8:, :])
    return jnp.concatenate([ru[:n - 8, :], bot], axis=0)


def _conv_pre(x_ref, p_ref, w_ref, b_ref, i):
    cur = x_ref[...]
    prev = jnp.where(i > 0, p_ref[...], 0.0)
    w = w_ref[...]
    taps = [cur] + [_shift_down(cur, prev, j) for j in (1, 2, 3)]
    acc = b_ref[...] + taps[0] * w[3:4, :]
    for j in (1, 2, 3):
        acc = acc + taps[j] * w[3 - j:4 - j, :]
    return acc, taps


def _conv_fwd(proj, conv_w, conv_b):
    t = proj.shape[0]
    tr = _row_tile(t)

    def body(x_ref, p_ref, w_ref, b_ref, o_ref):
        i = pl.program_id(0)
        acc, _ = _conv_pre(x_ref, p_ref, w_ref, b_ref, i)
        valid = (i * tr + _iota(acc.shape, 0)) >= PADF
        o_ref[...] = jnp.where(valid, acc * _sigmoid(acc), 0.0)

    return pl.pallas_call(
        body, name="conv_fwd",
        out_shape=jax.ShapeDtypeStruct((t, CONV_DIM), F32),
        grid=(t // tr, CONV_DIM // _CONV_TC),
        in_specs=[pl.BlockSpec((tr, _CONV_TC), lambda i, j: (i, _XBC_BLK + j)),
                  pl.BlockSpec((8, _CONV_TC), lambda i, j: (jnp.maximum(i * (tr // 8) - 1, 0), _XBC_BLK + j)),
                  pl.BlockSpec((CONV_K, _CONV_TC), lambda i, j: (0, j)),
                  pl.BlockSpec((1, _CONV_TC), lambda i, j: (0, j))],
        out_specs=pl.BlockSpec((tr, _CONV_TC), lambda i, j: (i, j)),
        compiler_params=_cp("parallel", "parallel"),
    )(proj, proj, conv_w, conv_b)


def _conv_bwd_act(dxbc, proj, conv_w, conv_b):
    t = proj.shape[0]
    tr = _row_tile(t)

    def body(d_ref, x_ref, p_ref, w_ref, b_ref, da_ref, dw_ref, db_ref):
        i = pl.program_id(1)
        acc, taps = _conv_pre(x_ref, p_ref, w_ref, b_ref, i)
        valid = (i * tr + _iota(acc.shape, 0)) >= PADF
        sg = _sigmoid(acc)
        da = jnp.where(valid, d_ref[...] * sg * (1.0 + acc * (1.0 - sg)), 0.0)
        da_ref[...] = da
        dw = jnp.concatenate([jnp.sum(da * taps[3 - k], axis=0, keepdims=True) for k in range(CONV_K)], axis=0)
        db = jnp.sum(da, axis=0, keepdims=True)

        @pl.when(i == 0)
        def _():
            dw_ref[...] = dw
            db_ref[...] = db

        @pl.when(i > 0)
        def _():
            dw_ref[...] += dw
            db_ref[...] += db

    return pl.pallas_call(
        body, name="conv_bwd_act",
        out_shape=(jax.ShapeDtypeStruct((t, CONV_DIM), F32),
                   jax.ShapeDtypeStruct((CONV_K, CONV_DIM), F32),
                   jax.ShapeDtypeStruct((1, CONV_DIM), F32)),
        grid=(CONV_DIM // _CONV_TC, t // tr),
        in_specs=[pl.BlockSpec((tr, _CONV_TC), lambda j, i: (i, j)),
                  pl.BlockSpec((tr, _CONV_TC), lambda j, i: (i, _XBC_BLK + j)),
                  pl.BlockSpec((8, _CONV_TC), lambda j, i: (jnp.maximum(i * (tr // 8) - 1, 0), _XBC_BLK + j)),
                  pl.BlockSpec((CONV_K, _CONV_TC), lambda j, i: (0, j)),
                  pl.BlockSpec((1, _CONV_TC), lambda j, i: (0, j))],
        out_specs=(pl.BlockSpec((tr, _CONV_TC), lambda j, i: (i, j)),
                   pl.BlockSpec((CONV_K, _CONV_TC), lambda j, i: (0, j)),
                   pl.BlockSpec((1, _CONV_TC), lambda j, i: (0, j))),
        compiler_params=_cp("parallel", "arbitrary"),
    )(dxbc, proj, proj, conv_w, conv_b)


def _conv_bwd_in(da, conv_w):
    t = da.shape[0]
    tr = _row_tile(t)
    last8 = t // 8 - 1

    def body(d_ref, n_ref, w_ref, o_ref):
        i = pl.program_id(0)
        cur = d_ref[...]
        nxt = jnp.where(i < pl.num_programs(0) - 1, n_ref[...], 0.0)
        w = w_ref[...]
        acc = cur * w[3:4, :]
        for j in (1, 2, 3):
            acc = acc + _shift_up(cur, nxt, j) * w[3 - j:4 - j, :]
        o_ref[...] = acc.astype(BF16)

    return pl.pallas_call(
        body, name="conv_bwd_in",
        out_shape=jax.ShapeDtypeStruct((t, CONV_DIM), BF16),
        grid=(t // tr, CONV_DIM // _CONV_TC),
        in_specs=[pl.BlockSpec((tr, _CONV_TC), lambda i, j: (i, j)),
                  pl.BlockSpec((8, _CONV_TC), lambda i, j: (jnp.minimum((i + 1) * (tr // 8), last8), j)),
                  pl.BlockSpec((CONV_K, _CONV_TC), lambda i, j: (0, j))],
        out_specs=pl.BlockSpec((tr, _CONV_TC), lambda i, j: (i, j)),
        compiler_params=_cp("parallel", "parallel"),
    )(da, da, conv_w)


_GW = D_SSD // G_SSD


def _ssd_prelude(dt_ref, a_ref, e_scr, es_scr, dte_scr):
    r0 = _iota((CHUNK, CHUNK), 0)
    r1 = _iota((CHUNK, CHUNK), 1)
    dt = jnp.where(r1 < H_SSD, dt_ref[...], 0.0)
    adt = dt * a_ref[...]
    acs = _dot_exact((r0 >= r1).astype(F32), adt)
    acs_t = acs.T
    alast = acs[CHUNK - 1:CHUNK, :]
    exp_a = jnp.exp(acs)
    dec_s = jnp.exp(alast - acs)
    lo = r1 < 64
    for j in range(H_SSD // 2):
        sl = slice(CHUNK * j, CHUNK * (j + 1))
        e_scr[:, sl] = jnp.where(lo, exp_a[:, 2 * j:2 * j + 1], exp_a[:, 2 * j + 1:2 * j + 2])
        es_scr[:, sl] = jnp.where(lo, dec_s[:, 2 * j:2 * j + 1], dec_s[:, 2 * j + 1:2 * j + 2])
        dte_scr[:, sl] = jnp.where(lo, dt[:, 2 * j:2 * j + 1], dt[:, 2 * j + 1:2 * j + 2])
    return dt, acs, acs_t, r0, r1, lo


def _chunk_decay_rows(acs_t, g):
    cd_t = jnp.exp(acs_t[:, CHUNK - 1:CHUNK])
    return jnp.concatenate(
        [jnp.broadcast_to(cd_t[8 * g + hh:8 * g + hh + 1, :], (64, N_STATE)) for hh in range(8)], axis=0)


def _ssd_fwd(xbc, dtlf, a_row, dsk_row):
    t = xbc.shape[0]
    nc = t // CHUNK

    def body(xs_ref, b_ref, c_ref, dt_ref, a_ref, dsk_ref, y_ref, hin_ref, h_scr, e_scr, es_scr, dte_scr):
        c = pl.program_id(0)

        @pl.when(c == 0)
        def _():
            h_scr[...] = jnp.zeros_like(h_scr)

        dt, acs, acs_t, r0, r1, lo = _ssd_prelude(dt_ref, a_ref, e_scr, es_scr, dte_scr)
        causal = r0 >= r1
        for g in range(G_SSD):
            gs = slice(_GW * g, _GW * (g + 1))
            bg = b_ref[:, N_STATE * g:N_STATE * (g + 1)].astype(BF16)
            cg = c_ref[:, N_STATE * g:N_STATE * (g + 1)].astype(BF16)
            cb = _dot(cg, bg, _NT)
            hg = h_scr[gs, :]
            hin_ref[0, gs, :] = hg
            xg = xs_ref[:, gs] * dte_scr[:, gs]
            yoff = _dot(cg, hg.astype(BF16), _NT) * e_scr[:, gs]
            st = _dot((xg * es_scr[:, gs]).astype(BF16), bg, _TN)
            h_scr[gs, :] = hg * _chunk_decay_rows(acs_t, g) + st
            for jj in range(4):
                j = 4 * g + jj
                sl = slice(CHUNK * j, CHUNK * (j + 1))
                xp = xg[:, CHUNK * jj:CHUNK * (jj + 1)]
                acc = yoff[:, CHUNK * jj:CHUNK * (jj + 1)] + dsk_ref[:, sl] * xs_ref[:, sl]
                for hh in range(2):
                    h = 2 * j + hh
                    seg = acs[:, h:h + 1] - acs_t[h:h + 1, :]
                    lm = jnp.exp(jnp.where(causal, seg, -1e30))
                    m = (cb * lm).astype(BF16)
                    xh = jnp.where(lo if hh == 0 else ~lo, xp, 0.0).astype(BF16)
                    acc = acc + _dot(m, xh)
                y_ref[:, sl] = acc

    return pl.pallas_call(
        body, name="ssd_fwd",
        out_shape=(jax.ShapeDtypeStruct((t, D_SSD), F32), jax.ShapeDtypeStruct((nc, D_SSD, N_STATE), F32)),
        grid=(nc,),
        in_specs=[pl.BlockSpec((CHUNK, D_SSD), lambda c: (c, 0)),
                  pl.BlockSpec((CHUNK, _GW), lambda c: (c, 4)),
                  pl.BlockSpec((CHUNK, _GW), lambda c: (c, 5)),
                  pl.BlockSpec((CHUNK, N_SMALL), lambda c: (c, 0)),
                  pl.BlockSpec((1, N_SMALL), lambda c: (0, 0)),
                  pl.BlockSpec((1, D_SSD), lambda c: (0, 0))],
        out_specs=(pl.BlockSpec((CHUNK, D_SSD), lambda c: (c, 0)),
                   pl.BlockSpec((1, D_SSD, N_STATE), lambda c: (c, 0, 0))),
        scratch_shapes=[pltpu.VMEM((D_SSD, N_STATE), F32)] + [pltpu.VMEM((CHUNK, D_SSD), F32)] * 3,
        compiler_params=_cp("arbitrary"),
    )(xbc, xbc, xbc, dtlf, a_row, dsk_row)


def _ssd_bwd(xbc, dtlf, a_row, dsk_row, hin, dy):
    t = xbc.shape[0]
    nc = t // CHUNK

    def body(xs_ref, b_ref, c_ref, dt_ref, a_ref, dsk_ref, hin_ref, dy_ref,
             dxbc_ref, ddt_ref, da_ref, ddsk_ref, dh_scr, e_scr, es_scr, dte_scr, dx_scr, whi_scr, wlo_scr):
        step = pl.program_id(0)

        @pl.when(step == 0)
        def _():
            dh_scr[...] = jnp.zeros_like(dh_scr)
            da_ref[...] = jnp.zeros_like(da_ref)
            ddsk_ref[...] = jnp.zeros_like(ddsk_ref)

        dt, acs, acs_t, r0, r1, lo = _ssd_prelude(dt_ref, a_ref, e_scr, es_scr, dte_scr)
        causal = r0 >= r1
        lane_row = _iota((1, CHUNK), 1)
        dacs = jnp.zeros((CHUNK, CHUNK), F32)
        dacs_t = jnp.zeros((CHUNK, CHUNK), F32)
        dalast = jnp.zeros((1, CHUNK), F32)
        ddt_dir = jnp.zeros((CHUNK, CHUNK), F32)
        ddsk_ref[...] += jnp.sum(dy_ref[...] * xs_ref[...], axis=0, keepdims=True)

        def head_sums(z, pick):
            hi = z.astype(BF16)
            return _dot(hi, pick) + _dot((z - hi.astype(F32)).astype(BF16), pick)

        for g in range(G_SSD):
            gs = slice(_GW * g, _GW * (g + 1))
            pick = (jnp.right_shift(_iota((_GW, CHUNK), 0), 6) + 8 * g == _iota((_GW, CHUNK), 1)).astype(BF16)
            bg = b_ref[:, N_STATE * g:N_STATE * (g + 1)].astype(BF16)
            cg = c_ref[:, N_STATE * g:N_STATE * (g + 1)].astype(BF16)
            cb = _dot(cg, bg, _NT)
            hg = hin_ref[0, gs, :]
            hgb = hg.astype(BF16)
            dhn = dh_scr[gs, :]
            dhnb = dhn.astype(BF16)
            esg = es_scr[:, gs]
            dyg = dy_ref[:, gs]
            xsg = xs_ref[:, gs]
            xg = xsg * dte_scr[:, gs]
            dyeb = (dyg * e_scr[:, gs]).astype(BF16)
            dc = _dot(dyeb, hgb)
            dh_y = _dot(dyeb, cg, _TN)
            dxs = _dot(bg, dhnb, _NT) * esg
            db = _dot((xg * esg).astype(BF16), dhnb)
            cd = _chunk_decay_rows(acs_t, g)
            dh_scr[gs, :] = dhn * cd + dh_y
            end_state = head_sums(jnp.broadcast_to(jnp.sum(xg * dxs, axis=0, keepdims=True), (8, _GW)), pick)[0:1, :]
            carried = dhn * hg * cd
            per_head = jnp.concatenate([jnp.sum(carried[64 * hh:64 * hh + 64, :], axis=0, keepdims=True)
                                        for hh in range(8)], axis=0)
            per_head = jnp.sum(per_head, axis=1, keepdims=True)
            for hh in range(8):
                end_state = end_state + jnp.where(lane_row == 8 * g + hh, per_head[hh:hh + 1, :], 0.0)
            dalast = dalast + end_state
            dcb = jnp.zeros((CHUNK, CHUNK), F32)
            for jj in range(4):
                j = 4 * g + jj
                sl = slice(CHUNK * j, CHUNK * (j + 1))
                ps = slice(CHUNK * jj, CHUNK * (jj + 1))
                xpb = xg[:, ps].astype(BF16)
                dyp = dyg[:, ps]
                dxp = dxs[:, ps]
                for hh in range(2):
                    h = 2 * j + hh
                    ws = slice(CHUNK * (2 * jj + hh), CHUNK * (2 * jj + hh + 1))
                    seg = acs[:, h:h + 1] - acs_t[h:h + 1, :]
                    lm = jnp.exp(jnp.where(causal, seg, -1e30))
                    mf = cb * lm
                    dyh = jnp.where(lo if hh == 0 else ~lo, dyp, 0.0).astype(BF16)
                    gm = _dot(dyh, xpb, _NT)
                    dcb = dcb + gm * lm
                    w = gm * mf
                    whi = w.astype(BF16)
                    whi_scr[:, ws] = whi
                    wlo_scr[:, ws] = (w - whi.astype(F32)).astype(BF16)
                    dacs_t = dacs_t - jnp.where(r0 == h, jnp.sum(w, axis=0, keepdims=True), 0.0)
                    dxp = dxp + _dot(mf.astype(BF16), dyh, _TN)
                dx_scr[:, sl] = dxp
            dxg = dx_scr[:, gs]
            pick_w = (jnp.right_shift(_iota((8 * CHUNK, CHUNK), 0), 7) + 8 * g == _iota((8 * CHUNK, CHUNK), 1)).astype(BF16)
            ch = _dot(cg, hgb, _NT)
            dacs = (dacs + _dot(whi_scr[...], pick_w) + _dot(wlo_scr[...], pick_w)
                    + head_sums(dyg * e_scr[:, gs] * ch - xg * dxs, pick))
            ddt_dir = ddt_dir + head_sums(dxg * xsg, pick)
            dcbb = dcb.astype(BF16)
            dxbc_ref[:, D_SSD + N_STATE * g:D_SSD + N_STATE * (g + 1)] = db + _dot(dcbb, cg, _TN)
            dxbc_ref[:, D_SSD + _GW + N_STATE * g:D_SSD + _GW + N_STATE * (g + 1)] = dc + _dot(dcbb, bg)
        dxbc_ref[:, 0:D_SSD] = dx_scr[...] * dte_scr[...] + dsk_ref[...] * dy_ref[...]
        dacs = dacs + dacs_t.T + jnp.where(r0 == CHUNK - 1, dalast, 0.0)
        dadt = _dot_exact((r1 >= r0).astype(F32), dacs)
        ddt_ref[...] = dadt * a_ref[...] + ddt_dir
        da_ref[...] += jnp.sum(dadt * dt, axis=0, keepdims=True)

    rev = lambda s: (nc - 1 - s, 0)
    return pl.pallas_call(
        body, name="ssd_bwd",
        out_shape=(jax.ShapeDtypeStruct((t, CONV_DIM), F32), jax.ShapeDtypeStruct((t, N_SMALL), F32),
                   jax.ShapeDtypeStruct((1, N_SMALL), F32), jax.ShapeDtypeStruct((1, D_SSD), F32)),
        grid=(nc,),
        in_specs=[pl.BlockSpec((CHUNK, D_SSD), rev),
                  pl.BlockSpec((CHUNK, _GW), lambda s: (nc - 1 - s, 4)),
                  pl.BlockSpec((CHUNK, _GW), lambda s: (nc - 1 - s, 5)),
                  pl.BlockSpec((CHUNK, N_SMALL), rev),
                  pl.BlockSpec((1, N_SMALL), lambda s: (0, 0)),
                  pl.BlockSpec((1, D_SSD), lambda s: (0, 0)),
                  pl.BlockSpec((1, D_SSD, N_STATE), lambda s: (nc - 1 - s, 0, 0)),
                  pl.BlockSpec((CHUNK, D_SSD), rev)],
        out_specs=(pl.BlockSpec((CHUNK, CONV_DIM), rev),
                   pl.BlockSpec((CHUNK, N_SMALL), rev),
                   pl.BlockSpec((1, N_SMALL), lambda s: (0, 0)),
                   pl.BlockSpec((1, D_SSD), lambda s: (0, 0))),
        scratch_shapes=([pltpu.VMEM((D_SSD, N_STATE), F32)] + [pltpu.VMEM((CHUNK, D_SSD), F32)] * 4
                        + [pltpu.VMEM((CHUNK, 8 * CHUNK), BF16)] * 2),
        compiler_params=_cp("arbitrary"),
    )(xbc, xbc, xbc, dtlf, a_row, dsk_row, hin, dy)


_NPAIR = H_ATT // 2
_QB, _KB, _VB = C_Q // 128, C_K // 128, C_V // 128
_SCALE = 1.0 / math.sqrt(64.0)


def _attn_blocks(t):
    return _tile(t, (1408, 384, 256, 128)), _tile(t, (384, 128))


def _split3(c):
    hi = c.astype(BF16).astype(F32)
    rest = c - hi
    mid = rest.astype(BF16).astype(F32)
    return hi, mid, rest - mid


def _head_lanes(lane, hh):
    return (lane < 64, 64) if hh == 0 else (lane >= 64, 0)


def _q_operand(q, cq, lane, hh):
    sel, first = _head_lanes(lane, hh)
    out = jnp.where(sel, q, 0.0)
    for n, col in enumerate(_split3(cq) + (1.0, 1.0, 1.0)):
        out = jnp.where(lane == first + n, col, out)
    return out.astype(BF16)


def _k_operand(k, ck, lane, hh):
    sel, first = _head_lanes(lane, hh)
    hi, mid, lo = _split3(ck)
    out = jnp.where(sel, k, 0.0)
    for n, col in enumerate((1.0, 1.0, 1.0, -hi, -mid, -lo)):
        out = jnp.where(lane == first + n, col, out)
    return out.astype(BF16)


def _needs_mask(i, kk, bq, bk):
    return kk * bk + bk - 1 > i * bq


_C_FILLER = 2.0 ** 30


def _attn_fwd(proj, c_col):
    t = proj.shape[0]
    bq, bk = _attn_blocks(t)
    nq, nk = t // bq, t // bk
    rs = 16

    def last_kv(i):
        return (i * bq + bq - 1) // bk

    def body(q_ref, k_ref, v_ref, cq_ref, ck_ref, o_ref, lse_ref, qs_scr, s_scr, p_scr, m_scr, acc_scr):
        i = pl.program_id(1)
        kk = pl.program_id(2)
        lane_q = _iota((bq, 128), 1)

        @pl.when(kk == 0)
        def _():
            m_scr[...] = jnp.full_like(m_scr, -1e30)
            acc_scr[...] = jnp.zeros_like(acc_scr)
            q = q_ref[...] * _SCALE
            cq = cq_ref[0]
            for hh in range(2):
                qs_scr[hh] = _q_operand(q, cq[:, hh:hh + 1], lane_q, hh)

        def step(masked):
            lane_k = _iota((bk, 128), 1)
            k = k_ref[...]
            v = v_ref[...]
            ck = ck_ref[0]
            ahead = _iota((rs, bq), 0) - _iota((rs, bq), 1)
            vss = []
            for hh in range(2):
                sel, first = _head_lanes(lane_k, hh)
                ks = _k_operand(k, ck[:, hh:hh + 1], lane_k, hh)
                vss.append(jnp.where(sel, v, jnp.where(lane_k == first, 1.0, 0.0)).astype(BF16))
                s_scr[hh] = _dot(ks, qs_scr[hh], _NT)
            for hh in range(2):
                vs = vss[hh]

                def block_max(r, mx):
                    rows = pl.ds(pl.multiple_of(r * rs, rs), rs)
                    s = s_scr[hh, rows, :]
                    if masked:
                        s = jnp.where(ahead <= i * bq - kk * bk - r * rs, s, -1e30)
                        s_scr[hh, rows, :] = s
                    return jnp.maximum(mx, s)

                mx = lax.fori_loop(0, bk // rs, block_max, jnp.full((rs, bq), -1e30, F32), unroll=True)
                m_old = m_scr[hh]
                m_new = jnp.maximum(m_old, jnp.max(mx, axis=0, keepdims=True))
                m_scr[hh] = m_new

                def probs(r, carry):
                    rows = pl.ds(pl.multiple_of(r * rs, rs), rs)
                    p_scr[hh, rows, :] = jnp.exp(s_scr[hh, rows, :] - m_new).astype(BF16)
                    return carry

                lax.fori_loop(0, bk // rs, probs, 0, unroll=True)
                acc_scr[hh] = acc_scr[hh] * jnp.exp(m_old - m_new) + _dot(vs, p_scr[hh], _TN)

        active = kk <= last_kv(i)
        masked = _needs_mask(i, kk, bq, bk)

        @pl.when(active & masked)
        def _():
            step(True)

        @pl.when(active & jnp.logical_not(masked))
        def _():
            step(False)

        @pl.when(kk == nk - 1)
        def _():
            a = acc_scr[0]
            b = acc_scr[1]
            la = a[64:65, :]
            lb = b[0:1, :]
            o_ref[...] = jnp.where(lane_q < 64, (a / la).T, (b / lb).T)
            lse_ref[0] = jnp.concatenate([m_scr[0] + jnp.log(la), m_scr[1] + jnp.log(lb)], axis=0)

    kvi = lambda i, kk: jnp.minimum(kk, last_kv(i))
    kv = lambda off: pl.BlockSpec((bk, 128), lambda j, i, kk: (kvi(i, kk), off + j))
    return pl.pallas_call(
        body, name="attn_fwd",
        out_shape=(jax.ShapeDtypeStruct((t, D_ATT), F32), jax.ShapeDtypeStruct((_NPAIR, 2, t), F32)),
        grid=(_NPAIR, nq, nk),
        in_specs=[pl.BlockSpec((bq, 128), lambda j, i, kk: (i, _QB + j)),
                  kv(_KB), kv(_VB),
                  pl.BlockSpec((1, bq, 2), lambda j, i, kk: (j, i, 0)),
                  pl.BlockSpec((1, bk, 2), lambda j, i, kk: (j, kvi(i, kk), 0))],
        out_specs=(pl.BlockSpec((bq, 128), lambda j, i, kk: (i, j)),
                   pl.BlockSpec((1, 2, bq), lambda j, i, kk: (j, 0, i))),
        scratch_shapes=[pltpu.VMEM((2, bq, 128), BF16), pltpu.VMEM((2, bk, bq), F32), pltpu.VMEM((2, bk, bq), BF16),
                        pltpu.VMEM((2, 1, bq), F32), pltpu.VMEM((2, 128, bq), F32)],
        compiler_params=_cp("parallel", "parallel", "arbitrary"),
    )(proj, proj, proj, c_col, c_col)


def _attn_delta(do, o):
    t = do.shape[0]
    tm = _row_tile(t)

    def body(do_ref, o_ref, d_ref):
        pick = (jnp.right_shift(_iota((D_ATT, 128), 0), 6) == _iota((D_ATT, 128), 1)).astype(F32)
        d_ref[...] = _dot_exact(do_ref[...] * o_ref[...], pick)

    row = pl.BlockSpec((tm, D_ATT), lambda i: (i, 0))
    return pl.pallas_call(
        body, name="attn_delta",
        out_shape=jax.ShapeDtypeStruct((t, 128), F32),
        grid=(t // tm,), in_specs=[row, row], out_specs=pl.BlockSpec((tm, 128), lambda i: (i, 0)),
        compiler_params=_cp("parallel"),
    )(do, o)


def _attn_bwd(proj, c_col, lse_row, dl_row, do):
    t = proj.shape[0]
    bq, bk = _attn_blocks(t)
    nq, nk = t // bq, t // bk
    rs = 16

    def first_q(kk):
        return (kk * bk) // bq

    def body(q_ref, k_ref, v_ref, cq_ref, ck_ref, lse_ref, dl_ref, do_ref,
             dq_ref, dk_ref, dv_ref, dck_ref, dcq_ref,
             qs_scr, doh_scr, ks_scr, s_scr, dp_scr, p_scr, ds_scr, dq_scr, dk_scr, dv_scr):
        kk = pl.program_id(1)
        i = pl.program_id(2)
        lane_q = _iota((bq, 128), 1)
        lane_k = _iota((bk, 128), 1)
        qrows = pl.ds(pl.multiple_of(i * bq, 128), bq)

        @pl.when(kk == 0)
        def _():
            q = q_ref[...] * _SCALE
            cq = cq_ref[0]
            do_ = do_ref[...]
            for hh in range(2):
                qs_scr[hh, qrows, :] = _q_operand(q, cq[:, hh:hh + 1], lane_q, hh)
                doh_scr[hh, qrows, :] = jnp.where(_head_lanes(lane_q, hh)[0], do_, 0.0).astype(BF16)
                dq_scr[hh, qrows, :] = jnp.zeros((bq, 128), F32)

        @pl.when(i == 0)
        def _():
            dk_scr[...] = jnp.zeros_like(dk_scr)
            dv_scr[...] = jnp.zeros_like(dv_scr)
            k = k_ref[...]
            ck = ck_ref[0]
            for hh in range(2):
                ks_scr[hh] = _k_operand(k, ck[:, hh:hh + 1], lane_k, hh)

        def step(masked):
            v16 = v_ref[...].astype(BF16)
            lse = lse_ref[0]
            dl = dl_ref[0]
            ahead = _iota((rs, bq), 0) - _iota((rs, bq), 1)
            for hh in range(2):
                s_scr[hh] = _dot(ks_scr[hh], qs_scr[hh, qrows, :], _NT)
                dp_scr[hh] = _dot(v16, doh_scr[hh, qrows, :], _NT)
            for hh in range(2):
                qs = qs_scr[hh, qrows, :]
                doh = doh_scr[hh, qrows, :]

                def strip(r, carry):
                    rows = pl.ds(pl.multiple_of(r * rs, rs), rs)
                    p = jnp.exp(s_scr[hh, rows, :] - lse[hh:hh + 1, :])
                    if masked:
                        p = jnp.where(ahead <= i * bq - kk * bk - r * rs, p, 0.0)
                    p_scr[hh, rows, :] = p.astype(BF16)
                    ds_scr[hh, rows, :] = (p * (dp_scr[hh, rows, :] - dl[hh:hh + 1, :])).astype(BF16)
                    return carry

                lax.fori_loop(0, bk // rs, strip, 0, unroll=True)
                dv_scr[...] += _dot(p_scr[hh], doh)
                dk_scr[hh] += _dot(ds_scr[hh], qs)
                dq_scr[hh, qrows, :] += _dot(ds_scr[hh], ks_scr[hh], _TN)

        active = i >= first_q(kk)
        masked = _needs_mask(i, kk, bq, bk)

        @pl.when(active & masked)
        def _():
            step(True)

        @pl.when(active & jnp.logical_not(masked))
        def _():
            step(False)

        @pl.when(i == nq - 1)
        def _():
            dka = dk_scr[0]
            dkb = dk_scr[1]
            dk_ref[...] = jnp.where(lane_k < 64, dka, dkb).astype(BF16)
            dv_ref[...] = dv_scr[...].astype(BF16)
            dck_ref[0] = -jnp.where(_iota((bk, 2), 1) == 0, dka[:, 67:68], dkb[:, 3:4])

        @pl.when((kk == nk - 1) & (i == nq - 1))
        def _():
            lane_t = _iota((t, 128), 1)
            dqa = dq_scr[0]
            dqb = dq_scr[1]
            dq_ref[...] = (jnp.where(lane_t < 64, dqa, dqb) * _SCALE).astype(BF16)
            dcq_ref[0] = jnp.where(_iota((t, 2), 1) == 0, dqa[:, 64:65], dqb[:, 0:1])

    qi = lambda kk, i: jnp.where(kk == 0, i, nq - 1)
    qspec = lambda off: pl.BlockSpec((bq, 128), lambda j, kk, i: (qi(kk, i), off + j))
    kspec = lambda off: pl.BlockSpec((bk, 128), lambda j, kk, i: (kk, off + j))
    rowspec = pl.BlockSpec((1, 2, bq), lambda j, kk, i: (j, 0, jnp.maximum(i, first_q(kk))))
    return pl.pallas_call(
        body, name="attn_bwd",
        out_shape=(jax.ShapeDtypeStruct((t, D_ATT), BF16), jax.ShapeDtypeStruct((t, D_ATT), BF16),
                   jax.ShapeDtypeStruct((t, D_ATT), BF16), jax.ShapeDtypeStruct((_NPAIR, t, 2), F32),
                   jax.ShapeDtypeStruct((_NPAIR, t, 2), F32)),
        grid=(_NPAIR, nk, nq),
        in_specs=[qspec(_QB), kspec(_KB), kspec(_VB),
                  pl.BlockSpec((1, bq, 2), lambda j, kk, i: (j, qi(kk, i), 0)),
                  pl.BlockSpec((1, bk, 2), lambda j, kk, i: (j, kk, 0)),
                  rowspec, rowspec, qspec(0)],
        out_specs=(pl.BlockSpec((t, 128), lambda j, kk, i: (0, j)),
                   pl.BlockSpec((bk, 128), lambda j, kk, i: (kk, j)),
                   pl.BlockSpec((bk, 128), lambda j, kk, i: (kk, j)),
                   pl.BlockSpec((1, bk, 2), lambda j, kk, i: (j, kk, 0)),
                   pl.BlockSpec((1, t, 2), lambda j, kk, i: (j, 0, 0))),
        scratch_shapes=[pltpu.VMEM((2, t, 128), BF16), pltpu.VMEM((2, t, 128), BF16), pltpu.VMEM((2, bk, 128), BF16),
                        pltpu.VMEM((2, bk, bq), F32), pltpu.VMEM((2, bk, bq), F32),
                        pltpu.VMEM((2, bk, bq), BF16), pltpu.VMEM((2, bk, bq), BF16),
                        pltpu.VMEM((2, t, 128), F32), pltpu.VMEM((2, bk, 128), F32), pltpu.VMEM((bk, 128), F32)],
        compiler_params=_cp("parallel", "arbitrary", "arbitrary"),
    )(proj, proj, proj, c_col, c_col, lse_row, dl_row, do)


def _premerge_fwd(y, o, proj, gamma):
    t = y.shape[0]
    tm = _row_tile_wide(t)

    def body(y_ref, z_ref, o_ref, za_ref, g_ref, ys_ref, ya_ref):
        z = z_ref[...]
        u = y_ref[...] * (z * _sigmoid(z))
        for g in range(G_SSD):
            gs = slice(_GW * g, _GW * (g + 1))
            ug = u[:, gs]
            r = lax.rsqrt(jnp.mean(ug * ug, axis=-1, keepdims=True) + EPS)
            ys_ref[:, gs] = (ug * r * g_ref[:, gs]).astype(BF16)
        za = za_ref[...]
        ya_ref[...] = (o_ref[...] * (za * _sigmoid(za))).astype(BF16)

    return pl.pallas_call(
        body, name="premerge_fwd",
        out_shape=(jax.ShapeDtypeStruct((t, D_SSD), BF16), jax.ShapeDtypeStruct((t, D_ATT), BF16)),
        grid=(t // tm,),
        in_specs=[pl.BlockSpec((tm, D_SSD), lambda i: (i, 0)),
                  pl.BlockSpec((tm, D_SSD), lambda i: (i, C_Z // D_SSD)),
                  pl.BlockSpec((tm, D_ATT), lambda i: (i, 0)),
                  pl.BlockSpec((tm, D_ATT), lambda i: (i, C_ZA // D_ATT)),
                  pl.BlockSpec((1, D_SSD), lambda i: (0, 0))],
        out_specs=(pl.BlockSpec((tm, D_SSD), lambda i: (i, 0)), pl.BlockSpec((tm, D_ATT), lambda i: (i, 0))),
        compiler_params=_cp("parallel"),
    )(y, proj, o, proj, gamma)


def _premerge_bwd(dys, dya, y, o, proj, gamma):
    t = y.shape[0]
    tm = _row_tile_wide(t)

    def body(dys_ref, dya_ref, y_ref, z_ref, o_ref, za_ref, g_ref, dy_ref, dz_ref, do_ref, dza_ref, dg_ref):
        i = pl.program_id(0)
        z = z_ref[...]
        sz = _sigmoid(z)
        silu = z * sz
        dsilu = sz * (1.0 + z * (1.0 - sz))
        yv = y_ref[...]
        u = yv * silu
        parts = []
        for g in range(G_SSD):
            gs = slice(_GW * g, _GW * (g + 1))
            ug = u[:, gs]
            r = lax.rsqrt(jnp.mean(ug * ug, axis=-1, keepdims=True) + EPS)
            n = ug * r
            dout = dys_ref[:, gs]
            dn = dout * g_ref[:, gs]
            du = r * (dn - n * jnp.mean(dn * n, axis=-1, keepdims=True))
            dy_ref[:, gs] = du * silu[:, gs]
            dz_ref[:, gs] = (du * yv[:, gs] * dsilu[:, gs]).astype(BF16)
            parts.append(jnp.sum(dout * n, axis=0, keepdims=True))
        dg = jnp.concatenate(parts, axis=1)
        za = za_ref[...]
        sa = _sigmoid(za)
        dya_ = dya_ref[...]
        do_ref[...] = dya_ * (za * sa)
        dza_ref[...] = (dya_ * o_ref[...] * (sa * (1.0 + za * (1.0 - sa)))).astype(BF16)

        @pl.when(i == 0)
        def _():
            dg_ref[...] = dg

        @pl.when(i > 0)
        def _():
            dg_ref[...] += dg

    ssd = pl.BlockSpec((tm, D_SSD), lambda i: (i, 0))
    att = pl.BlockSpec((tm, D_ATT), lambda i: (i, 0))
    vec = pl.BlockSpec((1, D_SSD), lambda i: (0, 0))
    return pl.pallas_call(
        body, name="premerge_bwd",
        out_shape=(jax.ShapeDtypeStruct((t, D_SSD), F32), jax.ShapeDtypeStruct((t, D_SSD), BF16),
                   jax.ShapeDtypeStruct((t, D_ATT), F32), jax.ShapeDtypeStruct((t, D_ATT), BF16),
                   jax.ShapeDtypeStruct((1, D_SSD), F32)),
        grid=(t // tm,),
        in_specs=[ssd, att, ssd, pl.BlockSpec((tm, D_SSD), lambda i: (i, C_Z // D_SSD)), att,
                  pl.BlockSpec((tm, D_ATT), lambda i: (i, C_ZA // D_ATT)), vec],
        out_specs=(ssd, ssd, att, att, vec),
        compiler_params=_cp("arbitrary"),
    )(dys, dya, y, proj, o, proj, gamma)


_G_BLK = C_G // D_MODEL


def _merge_fwd(a, b, proj, gate_bias):
    t = a.shape[0]
    tm = _row_tile(t)

    def body(a_ref, b_ref, gs_ref, ga_ref, bias_ref, m_ref):
        g_ssd = _sigmoid(gs_ref[...] + bias_ref[:, 0:D_MODEL])
        g_att = _sigmoid(ga_ref[...] + bias_ref[:, D_MODEL:2 * D_MODEL])
        m_ref[...] = (g_ssd * a_ref[...] + g_att * b_ref[...]).astype(BF16)

    row = pl.BlockSpec((tm, D_MODEL), lambda i: (i, 0))
    return pl.pallas_call(
        body, name="merge_fwd",
        out_shape=jax.ShapeDtypeStruct((t, D_MODEL), BF16),
        grid=(t // tm,),
        in_specs=[row, row,
                  pl.BlockSpec((tm, D_MODEL), lambda i: (i, _G_BLK)),
                  pl.BlockSpec((tm, D_MODEL), lambda i: (i, _G_BLK + 1)),
                  pl.BlockSpec((1, 2 * D_MODEL), lambda i: (0, 0))],
        out_specs=row,
        compiler_params=_cp("parallel"),
    )(a, b, proj, proj, gate_bias)


def _merge_bwd(dm, a, b, proj, gate_bias):
    t = a.shape[0]
    tm = _row_tile(t)

    def body(dm_ref, a_ref, b_ref, gs_ref, ga_ref, bias_ref, da_ref, db_ref, dg_ref, dbias_ref):
        i = pl.program_id(0)
        dm_ = dm_ref[...]
        g_ssd = _sigmoid(gs_ref[...] + bias_ref[:, 0:D_MODEL])
        g_att = _sigmoid(ga_ref[...] + bias_ref[:, D_MODEL:2 * D_MODEL])
        da_ref[...] = (dm_ * g_ssd).astype(BF16)
        db_ref[...] = (dm_ * g_att).astype(BF16)
        dgs = dm_ * a_ref[...] * g_ssd * (1.0 - g_ssd)
        dga = dm_ * b_ref[...] * g_att * (1.0 - g_att)
        dg_ref[:, 0:D_MODEL] = dgs.astype(BF16)
        dg_ref[:, D_MODEL:2 * D_MODEL] = dga.astype(BF16)
        part = jnp.concatenate([jnp.sum(dgs, axis=0, keepdims=True), jnp.sum(dga, axis=0, keepdims=True)], axis=1)

        @pl.when(i == 0)
        def _():
            dbias_ref[...] = part

        @pl.when(i > 0)
        def _():
            dbias_ref[...] += part

    row = pl.BlockSpec((tm, D_MODEL), lambda i: (i, 0))
    wide = pl.BlockSpec((tm, 2 * D_MODEL), lambda i: (i, 0))
    vec = pl.BlockSpec((1, 2 * D_MODEL), lambda i: (0, 0))
    return pl.pallas_call(
        body, name="merge_bwd",
        out_shape=(jax.ShapeDtypeStruct((t, D_MODEL), BF16), jax.ShapeDtypeStruct((t, D_MODEL), BF16),
                   jax.ShapeDtypeStruct((t, 2 * D_MODEL), BF16), jax.ShapeDtypeStruct((1, 2 * D_MODEL), F32)),
        grid=(t // tm,),
        in_specs=[row, row, row,
                  pl.BlockSpec((tm, D_MODEL), lambda i: (i, _G_BLK)),
                  pl.BlockSpec((tm, D_MODEL), lambda i: (i, _G_BLK + 1)), vec],
        out_specs=(row, row, wide, vec),
        compiler_params=_cp("arbitrary"),
    )(dm, a, b, proj, proj, gate_bias)


def _post(o2, h, target, g):
    t = o2.shape[0]
    nc = t // CHUNK

    def body(o_ref, h_ref, t_ref, g_ref, dy_ref, do_ref, dg_ref, loss_ref):
        c = pl.program_id(0)
        x = o_ref[...]
        r = lax.rsqrt(jnp.mean(x * x, axis=-1, keepdims=True) + EPS)
        n = x * r
        y = h_ref[...] + n * g_ref[...]
        diff = jnp.where(c > 0, y - t_ref[...], 0.0)
        dy = diff * (1.0 / D_MODEL)
        dy_ref[...] = dy
        gdy = dy * g_ref[...]
        do_ref[...] = (r * (gdy - n * jnp.mean(gdy * n, axis=-1, keepdims=True))).astype(BF16)
        dg = jnp.sum(dy * n, axis=0, keepdims=True)
        lpart = 0.5 * jnp.sum(jnp.sum(diff * diff, axis=1, keepdims=True), axis=0, keepdims=True) * (1.0 / D_MODEL)
        sel = (_iota((8, 128), 0) == 0) & (_iota((8, 128), 1) == 0)

        @pl.when(c == 0)
        def _():
            dg_ref[...] = dg
            loss_ref[...] = jnp.zeros_like(loss_ref)

        @pl.when(c > 0)
        def _():
            dg_ref[...] += dg
            loss_ref[...] += jnp.where(sel, lpart, 0.0)

    row = pl.BlockSpec((CHUNK, D_MODEL), lambda c: (c, 0))
    vec = pl.BlockSpec((1, D_MODEL), lambda c: (0, 0))
    return pl.pallas_call(
        body, name="post",
        out_shape=(jax.ShapeDtypeStruct((t, D_MODEL), F32), jax.ShapeDtypeStruct((t, D_MODEL), BF16),
                   jax.ShapeDtypeStruct((1, D_MODEL), F32), jax.ShapeDtypeStruct((8, 128), F32)),
        grid=(nc,),
        in_specs=[row, row, pl.BlockSpec((CHUNK, D_MODEL), lambda c: (jnp.maximum(c - 1, 0), 0)), vec],
        out_specs=(row, row, vec, pl.BlockSpec((8, 128), lambda c: (0, 0))),
        compiler_params=_cp("arbitrary"),
    )(o2, h, target, g)


def _mm_tiles(t):
    return _tile(t, (704, 384, 128))


def _local_step(h, target, w_main, w_small, wps, wpa, wout, norm_pre, conv_w, conv_b, bias_row, a_row,
                dsk_row, ssd_norm, gate_bias, norm_post):
    t = h.shape[0]
    tm = _mm_tiles(t)
    u = _norm1_fwd(h, norm_pre)
    proj = _matmul(u, w_main, "nt", F32, "inproj", tm, 1024, D_MODEL)
    small = _matmul(u, w_small, "nt", F32, "inproj_small", tm, N_SMALL, D_MODEL)
    dtlf = _small_fwd(small, bias_row)
    xbc = _conv_fwd(proj, conv_w, conv_b)
    y, hin = _ssd_fwd(xbc, dtlf, a_row, dsk_row)
    c_tok = dtlf[:, H_SSD:H_SSD + H_ATT]
    c_tok = jnp.where(jnp.arange(t)[:, None] < PADF, _C_FILLER, c_tok)
    c_col = c_tok.reshape(t, _NPAIR, 2).transpose(1, 0, 2)
    o, lse = _attn_fwd(proj, c_col)
    ys, ya = _premerge_fwd(y, o, proj, ssd_norm)
    a = _matmul(ys, wps, "nn", F32, "proj_ssd", tm, D_MODEL, D_SSD)
    b = _matmul(ya, wpa, "nn", F32, "proj_att", tm, D_MODEL, D_ATT)
    merged = _merge_fwd(a, b, proj, gate_bias)
    o2 = _matmul(merged, wout, "nn", F32, "out_proj", tm, D_MODEL, D_MODEL)
    dy_out, do2, d_norm_post, loss_blk = _post(o2, h, target, norm_post)

    dm = _matmul(do2, wout, "nt", F32, "out_proj_dx", tm, D_MODEL, D_MODEL)
    d_wout = _matmul(merged, do2, "tn", F32, "out_proj_dw", D_MODEL, D_MODEL, tm)
    da, db, dgraw, d_gate_bias = _merge_bwd(dm, a, b, proj, gate_bias)
    dys = _matmul(da, wps, "nt", F32, "proj_ssd_dx", tm, D_SSD, D_MODEL)
    d_wps = _matmul(ys, da, "tn", F32, "proj_ssd_dw", D_SSD, D_MODEL, tm)
    dya = _matmul(db, wpa, "nt", F32, "proj_att_dx", tm, D_ATT, D_MODEL)
    d_wpa = _matmul(ya, db, "tn", F32, "proj_att_dw", D_ATT, D_MODEL, tm)
    dy, dz, do, dza, d_ssd_norm = _premerge_bwd(dys, dya, y, o, proj, ssd_norm)
    dl_row = _attn_delta(do, o)[:, 0:H_ATT].T.reshape(_NPAIR, 2, t)
    dq, dk, dv, dc_key, dc_qry = _attn_bwd(proj, c_col, lse, dl_row, do)
    dxbc, ddt, d_a, d_dsk = _ssd_bwd(xbc, dtlf, a_row, dsk_row, hin, dy)
    dact, d_conv_w, d_conv_b = _conv_bwd_act(dxbc, proj, conv_w, conv_b)
    dxbc_raw = _conv_bwd_in(dact, conv_w)
    dc_tok = jnp.transpose(dc_key + dc_qry, (1, 0, 2)).reshape(t, H_ATT)
    dsm = ddt + jnp.pad(dc_tok, ((0, 0), (H_SSD, N_SMALL - H_SSD - H_ATT)))
    dsmall, d_bias_row = _small_bwd(dsm, small, bias_row)
    dproj = [dz, dxbc_raw, dza, dq, dk, dv, dgraw]
    return dict(loss_blk=loss_blk, u=u, dy_out=dy_out, dproj=dproj, dsmall=dsmall, d_wps=d_wps, d_wpa=d_wpa,
                d_wout=d_wout, d_conv_w=d_conv_w, d_conv_b=d_conv_b,
                d_bias_row=d_bias_row, d_a=d_a, d_dsk=d_dsk, d_ssd_norm=d_ssd_norm,
                d_gate_bias=d_gate_bias, d_norm_post=d_norm_post)


def _to_aligned_rows(slots):
    w = slots.reshape(N_COLS, slots.shape[2])

    def cut(o):
        return w[o[0]:o[0] + o[1]]
    main = jnp.concatenate([cut(O_Z), cut(O_XBC), cut(O_ZA), cut(O_Q), cut(O_K), cut(O_V), cut(O_G)], axis=0)
    pad = jnp.zeros((N_SMALL - H_SSD - H_ATT, w.shape[1]), w.dtype)
    small = jnp.concatenate([cut(O_DT), cut(O_F), pad], axis=0)
    return main, small


def _from_aligned_rows(main, small):
    def cm(c0, n):
        return main[c0:c0 + n]
    flat = jnp.concatenate([cm(C_Z, 2048), cm(C_XBC, 3072), small[0:H_SSD], cm(C_ZA, 1024),
                            cm(C_Q, 1024), cm(C_K, 1024), cm(C_V, 1024), small[H_SSD:H_SSD + H_ATT],
                            cm(C_G, 2048)], axis=0)
    return flat.reshape(4, N_COLS // 4, flat.shape[1])


_MESH = pl.DeviceIdType.MESH
_ANY = pl.BlockSpec(memory_space=pl.ANY)
_VM = pl.BlockSpec(memory_space=pltpu.VMEM)
_HALF = 512
N_DEV = 8


def _coords():
    return lax.axis_index("x"), lax.axis_index("y"), lax.axis_index("c")


def _other_chips(x, y):
    return [(1 - x, y), (x, 1 - y), (1 - x, 1 - y)]


def _half(cc):
    return pl.ds(cc * _HALF, _HALF)


def _gather_shards(slots):
    n = len(slots)

    def body(*refs):
        buf = refs[n:2 * n]
        send_sems, recv_sems = refs[2 * n:]
        x, y, c = _coords()
        chip = 2 * x + y
        sibling = (x, y, 1 - c)
        chips = _other_chips(x, y)

        def copy(i, frm, cc, k, to):
            part = buf[i].at[frm, :, _half(cc)]
            return pltpu.make_async_remote_copy(src_ref=part, dst_ref=part, send_sem=send_sems.at[6 * i + k],
                                                recv_sem=recv_sems.at[6 * i + k], device_id=to, device_id_type=_MESH)

        def chip_of(k):
            return 2 * chips[k][0] + chips[k][1]

        first = [copy(i, chip, c, k, (*chips[k], c)) for k in range(3) for i in range(n)]
        for cp in first:
            cp.start()
        passed = []
        for k in range(3):
            for i in range(n):
                copy(i, chip_of(k), c, k, (*chips[k], c)).wait_recv()
                passed.append(copy(i, chip_of(k), c, 3 + k, sibling))
                passed[-1].start()
        for k in range(3):
            for i in range(n):
                copy(i, chip_of(k), 1 - c, 3 + k, sibling).wait_recv()
        for cp in first + passed:
            cp.wait_send()

    return pl.pallas_call(
        body, name="gather_shards",
        out_shape=tuple(jax.ShapeDtypeStruct(s.shape, s.dtype) for s in slots),
        in_specs=[_ANY] * n, out_specs=tuple([_ANY] * n),
        input_output_aliases={i: i for i in range(n)},
        scratch_shapes=[pltpu.SemaphoreType.DMA((6 * n,)), pltpu.SemaphoreType.DMA((6 * n,))],
    )(*slots)


def _allgather8(block, name):
    rows, width = block.shape

    def body(x_ref, out_ref, send_sems, recv_sems, local_sem):
        x, y, c = _coords()
        me, sibling = (x, y, c), (x, y, 1 - c)
        chips = _other_chips(x, y)

        def slot(px, py, pc):
            return out_ref.at[4 * px + 2 * py + pc]

        def copy(k, blk, to, src=None):
            return pltpu.make_async_remote_copy(src_ref=slot(*blk) if src is None else src, dst_ref=slot(*blk),
                                                send_sem=send_sems.at[k], recv_sem=recv_sems.at[k],
                                                device_id=to, device_id_type=_MESH)

        mine = pltpu.make_async_copy(x_ref, slot(*me), local_sem)
        mine.start()
        first = [copy(0, me, sibling, src=x_ref)]
        first += [copy(1 + j, me, (*chip, c), src=x_ref) for j, chip in enumerate(chips)]
        for cp in first:
            cp.start()
        passed = [copy(4 + j, (*chip, c), sibling) for j, chip in enumerate(chips)]
        for j, chip in enumerate(chips):
            copy(1 + j, (*chip, c), me).wait_recv()
            passed[j].start()
        copy(0, sibling, me).wait_recv()
        for j, chip in enumerate(chips):
            copy(4 + j, (*chip, 1 - c), me).wait_recv()
        for cp in first + passed:
            cp.wait_send()
        mine.wait()

    return pl.pallas_call(
        body, name=name,
        out_shape=jax.ShapeDtypeStruct((N_DEV, rows, width), block.dtype),
        in_specs=[_VM], out_specs=_VM,
        scratch_shapes=[pltpu.SemaphoreType.DMA((7,)), pltpu.SemaphoreType.DMA((7,)), pltpu.SemaphoreType.DMA],
    )(block)


def _pair_swap(arrs):
    def copies(src, dst, send_sems, recv_sems):
        x, y, c = _coords()
        return [pltpu.make_async_remote_copy(src_ref=src[i].at[:, :, _half(1 - c)], dst_ref=dst[i],
                                             send_sem=send_sems.at[i], recv_sem=recv_sems.at[i],
                                             device_id=(x, y, 1 - c), device_id_type=_MESH) for i in range(len(src))]

    shapes = tuple(jax.ShapeDtypeStruct((4, a.shape[1], _HALF), a.dtype) for a in arrs)
    return tuple(arrs), shapes, copies, len(arrs)


def _chip_exchange(arrs):
    def copies(src, dst, send_sems, recv_sems):
        x, y, c = _coords()
        chips = _other_chips(x, y)
        return [pltpu.make_async_remote_copy(src_ref=src[i].at[2 * chips[k][0] + chips[k][1]], dst_ref=dst[i].at[k],
                                             send_sem=send_sems.at[3 * i + k], recv_sem=recv_sems.at[3 * i + k],
                                             device_id=(*chips[k], c), device_id_type=_MESH)
                for k in range(3) for i in range(len(src))]

    shapes = tuple(jax.ShapeDtypeStruct((3,) + a.shape[1:], a.dtype) for a in arrs)
    return tuple(arrs), shapes, copies, 3 * len(arrs)


def _pair_join_halves(fulls):
    n = len(fulls)

    def body(*refs):
        buf = refs[n:2 * n]
        send_sems, recv_sems = refs[2 * n:]
        x, y, c = _coords()

        def remote(i, cc):
            part = buf[i].at[:, _half(cc)]
            return pltpu.make_async_remote_copy(src_ref=part, dst_ref=part, send_sem=send_sems.at[i],
                                                recv_sem=recv_sems.at[i], device_id=(x, y, 1 - c), device_id_type=_MESH)

        for i in range(n):
            remote(i, c).start()
        for i in range(n):
            remote(i, c).wait_send()
            remote(i, 1 - c).wait_recv()

    return pl.pallas_call(
        body, name="pair_join_halves",
        out_shape=tuple(jax.ShapeDtypeStruct(a.shape, a.dtype) for a in fulls),
        in_specs=[_ANY] * n, out_specs=tuple([_ANY] * n),
        input_output_aliases={i: i for i in range(n)},
        scratch_shapes=[pltpu.SemaphoreType.DMA((n,)), pltpu.SemaphoreType.DMA((n,))],
    )(*fulls)


_RED_TC = 128
_RED_NT = _HALF // _RED_TC


def _add_pair(ids, g32, recv_a):
    rows = g32.shape[1]

    def body(ids_ref, g_ref, r_ref, o_ref):
        o_ref[...] = (g_ref[...] + r_ref[...]).astype(BF16)

    blk = pl.BlockSpec((1, rows, _RED_TC), lambda j, l, ids: (j, 0, l))
    return pl.pallas_call(
        body, name="add_pair",
        out_shape=jax.ShapeDtypeStruct((4, rows, _HALF), BF16),
        grid_spec=pltpu.PrefetchScalarGridSpec(
            num_scalar_prefetch=1, grid=(4, _RED_NT),
            in_specs=[pl.BlockSpec((1, rows, _RED_TC), lambda j, l, ids: (j, 0, ids[0] * _RED_NT + l)), blk],
            out_specs=blk),
        compiler_params=_cp("parallel", "parallel"),
    )(ids, g32, recv_a)


def _add_chips(ids, g32, recv_a, recv_b):
    rows = g32.shape[1]

    def body(ids_ref, g_ref, a_ref, b_ref, o_ref):
        acc = g_ref[0] + a_ref[0]
        for k in range(3):
            acc = acc + b_ref[k].astype(F32)
        o_ref[...] = acc

    return pl.pallas_call(
        body, name="add_chips",
        out_shape=jax.ShapeDtypeStruct((rows, 2 * _HALF), F32),
        grid_spec=pltpu.PrefetchScalarGridSpec(
            num_scalar_prefetch=1, grid=(_RED_NT,),
            in_specs=[pl.BlockSpec((1, rows, _RED_TC), lambda l, ids: (ids[1], 0, ids[0] * _RED_NT + l)),
                      pl.BlockSpec((1, rows, _RED_TC), lambda l, ids: (ids[1], 0, l)),
                      pl.BlockSpec((3, rows, _RED_TC), lambda l, ids: (0, 0, l))],
            out_specs=pl.BlockSpec((rows, _RED_TC), lambda l, ids: (0, ids[0] * _RED_NT + l))),
        compiler_params=_cp("parallel"),
    )(ids, g32, recv_a, recv_b)


def _sum8(gathered):
    _, rows, width = gathered.shape

    def body(g_ref, o_ref):
        acc = g_ref[0]
        for d in range(1, N_DEV):
            acc = acc + g_ref[d]
        o_ref[...] = acc

    return pl.pallas_call(
        body, name="sum8",
        out_shape=jax.ShapeDtypeStruct((rows, width), F32),
        in_specs=[_VM], out_specs=_VM,
    )(gathered)


def _adamw(w, g, m, v, name):
    rows, cols = w.shape
    budget = (3 << 20) // 2
    tr, tc = rows, cols
    if rows * cols * 4 > budget:
        if rows % 8 == 0:
            tr = next(c for c in (512, 256, 128, 64, 32, 16, 8) if rows % c == 0 and c * cols * 4 <= budget)
        else:
            tc = next(c for c in (512, 256, 128) if cols % c == 0 and rows * c * 4 <= budget)
    c1 = 1.0 - ADAM_B1 ** ADAM_STEP
    c2 = 1.0 - ADAM_B2 ** ADAM_STEP

    def body(w_ref, g_ref, m_ref, v_ref, d_ref, mo_ref, vo_ref):
        gg = g_ref[...]
        mn = ADAM_B1 * m_ref[...] + (1.0 - ADAM_B1) * gg
        vn = ADAM_B2 * v_ref[...] + (1.0 - ADAM_B2) * (gg * gg)
        mo_ref[...] = mn
        vo_ref[...] = vn
        d_ref[...] = -ADAM_LR * ((mn / c1) / (jnp.sqrt(vn / c2) + ADAM_EPS) + ADAM_WD * w_ref[...])

    blk = pl.BlockSpec((tr, tc), lambda i, j: (i, j))
    shp = jax.ShapeDtypeStruct((rows, cols), F32)
    return pl.pallas_call(
        body, name=name, out_shape=(shp, shp, shp), grid=(rows // tr, cols // tc),
        in_specs=[blk] * 4, out_specs=(blk, blk, blk),
        compiler_params=_cp("parallel", "parallel"),
    )(w, g, m, v)


def _rows128(a):
    return a.reshape(-1, 128)


def _pack_small(norm_pre, conv_b, ssd_norm, gate_bias, norm_post, dt_bias, a_log, d_skip, fgate_bias):
    tiny = jnp.concatenate([dt_bias.reshape(-1), a_log.reshape(-1), d_skip.reshape(-1), fgate_bias.reshape(-1),
                            jnp.zeros((16,), F32)])
    return jnp.concatenate([_rows128(norm_pre), _rows128(conv_b), _rows128(ssd_norm), _rows128(gate_bias),
                            _rows128(norm_post), tiny.reshape(1, 128)], axis=0)


_SMALL_ROWS = 73
_SMALL_PAD = 80


def _unpack_small(p):
    tiny = p[72]
    return dict(norm_pre=p[0:8].reshape(1, 1024), conv_b=p[8:32].reshape(1, 3072), ssd_norm=p[32:48].reshape(1, 2048),
                gate_bias=p[48:64].reshape(1, 2048), norm_post=p[64:72].reshape(1, 1024),
                dt_bias=tiny[0:32].reshape(1, 32), a_log=tiny[32:64].reshape(1, 32),
                d_skip=tiny[64:96].reshape(1, 32), fgate_bias=tiny[96:112].reshape(1, 16))


def _pad_rows(a, rows):
    return jnp.concatenate([a, jnp.zeros((rows - a.shape[0], a.shape[1]), a.dtype)], axis=0)


def kernel(x, meta_tokens, norm_pre, w_in, conv_w, conv_b, dt_bias, a_log, d_skip, ssd_norm, fgate_bias, gate_bias, w_proj_ssd, w_proj_att, w_out, norm_post, loss_target, m_meta_tokens, m_norm_pre, m_w_in, m_conv_w, m_conv_b, m_dt_bias, m_a_log, m_d_skip, m_ssd_norm, m_fgate_bias, m_gate_bias, m_w_proj_ssd, m_w_proj_att, m_w_out, m_norm_post, v_meta_tokens, v_norm_pre, v_w_in, v_conv_w, v_conv_b, v_dt_bias, v_a_log, v_d_skip, v_ssd_norm, v_fgate_bias, v_gate_bias, v_w_proj_ssd, v_w_proj_att, v_w_out, v_norm_post):
    cx, cy, cc = _coords()
    chip = 2 * cx + cy
    ids = jnp.stack([cc, chip]).astype(jnp.int32)
    seq = x.shape[1]

    w_in_sh = jnp.transpose(w_in[0]).astype(BF16)
    w_pr_sh = jnp.concatenate([w_proj_ssd[0], w_proj_att[0], w_out[0]], axis=0).astype(BF16)

    def own_slot(sh):
        return lax.dynamic_update_slice(lax.empty((4,) + sh.shape, sh.dtype), sh[None], (chip, 0, 0))

    g_in, g_pr = _gather_shards([own_slot(w_in_sh), own_slot(w_pr_sh)])
    w_main, w_small = _to_aligned_rows(g_in)
    wps = g_pr[:, 0:512].reshape(D_SSD, D_MODEL)
    wpa = g_pr[:, 512:768].reshape(D_ATT, D_MODEL)
    wout = g_pr[:, 768:1024].reshape(D_MODEL, D_MODEL)
    sm_sh = jnp.concatenate([_rows128(meta_tokens), _rows128(conv_w[0])], axis=0)
    sm_all = _allgather8(sm_sh, "gather_small_weights")[0::2]
    meta_full = jnp.transpose(sm_all[:, 0:32].reshape(4, N_META, 256), (1, 0, 2)).reshape(N_META, D_MODEL)
    conv_w_full = jnp.transpose(sm_all[:, 32:56].reshape(4, CONV_K, 768), (1, 0, 2)).reshape(CONV_K, CONV_DIM)

    h = jnp.concatenate([jnp.zeros((PADF, D_MODEL), F32), meta_full, x[0]], axis=0)
    bias_row = jnp.concatenate([dt_bias[0], fgate_bias[0], jnp.zeros((N_SMALL - H_SSD - H_ATT,), F32)]).reshape(1, N_SMALL)
    a_neg = -jnp.exp(a_log[0])
    a_row = jnp.concatenate([a_neg, jnp.zeros((N_SMALL - H_SSD,), F32)]).reshape(1, N_SMALL)
    dsk_row = jnp.repeat(d_skip[0], 64).reshape(1, D_SSD)
    r = _local_step(h, loss_target[0], w_main, w_small, wps, wpa, wout, norm_pre, conv_w_full, conv_b, bias_row,
                    a_row, dsk_row, ssd_norm, gate_bias, norm_post)

    tm = _mm_tiles(h.shape[0])
    n_row_tiles = h.shape[0] // tm
    d_w_main = _matmul_cat_tn(r["dproj"], r["u"], "inproj_dw", tm)
    d_w_small = _matmul(r["dsmall"], r["u"], "tn", F32, "inproj_small_dw", N_SMALL, D_MODEL, tm)
    g32_in = _from_aligned_rows(d_w_main, d_w_small)
    g32_pr = jnp.concatenate([r["d_wps"].reshape(4, 512, D_MODEL), r["d_wpa"].reshape(4, 256, D_MODEL),
                              r["d_wout"].reshape(4, 256, D_MODEL)], axis=1)
    first = max(n_row_tiles // 3, 1)
    du_first, ra_in, ra_pr = _matmul_cat_nn(r["dproj"], w_main, "inproj_dx_swap", tm, rows=(0, first),
                                            exchange=_pair_swap([g32_in, g32_pr]))
    pb_in = _add_pair(ids, g32_in, ra_in)
    pb_pr = _add_pair(ids, g32_pr, ra_pr)
    du_a, rb_in, rb_pr = _matmul_cat_nn(r["dproj"], w_main, "inproj_dx_exchange", tm,
                                        rows=(first, n_row_tiles - first), fill=du_first,
                                        exchange=_chip_exchange([pb_in, pb_pr]))
    du_b = _matmul(r["dsmall"], w_small, "nn", F32, "inproj_small_dx", tm, D_MODEL, N_SMALL)
    dh, d_norm_pre = _norm1_bwd(du_a, du_b, h, norm_pre, r["dy_out"])
    grad_x = dh[PADF + N_META:].reshape(1, seq, D_MODEL)
    half_in = _add_chips(ids, g32_in, ra_in, rb_in)
    half_pr = _add_chips(ids, g32_pr, ra_pr, rb_pr)
    gw_in, gw_pr = _pair_join_halves([half_in, half_pr])

    tiny = r["d_bias_row"][0]
    part_small = _pack_small(d_norm_pre, r["d_conv_b"], r["d_ssd_norm"], r["d_gate_bias"], r["d_norm_post"],
                             tiny[0:H_SSD], r["d_a"][0, 0:H_SSD] * a_neg, r["d_dsk"].reshape(H_SSD, 64).sum(axis=1),
                             tiny[H_SSD:H_SSD + H_ATT])
    part = jnp.concatenate([_pad_rows(part_small, _SMALL_PAD), _rows128(r["d_conv_w"]),
                            _rows128(dh[PADF:PADF + N_META]), r["loss_blk"]], axis=0)
    tot = _sum8(_allgather8(part, "gather_small_grads"))
    loss = tot[_SMALL_PAD + 96 + 128, 0]
    g_small = tot[0:_SMALL_PAD]
    g_conv_w = lax.dynamic_slice_in_dim(tot[_SMALL_PAD:_SMALL_PAD + 96].reshape(CONV_K, CONV_DIM), chip * 768, 768, axis=1)
    g_meta = lax.dynamic_slice_in_dim(tot[_SMALL_PAD + 96:_SMALL_PAD + 224].reshape(N_META, D_MODEL), chip * 256, 256, axis=1)

    upd = {}
    upd["w_in"] = tuple(jnp.transpose(a) for a in (gw_in,) + _adamw(
        jnp.transpose(w_in[0]), gw_in, jnp.transpose(m_w_in[0]), jnp.transpose(v_w_in[0]), "adamw_w_in"))
    w_pr32 = jnp.concatenate([w_proj_ssd[0], w_proj_att[0], w_out[0]], axis=0)
    m_pr = jnp.concatenate([m_w_proj_ssd[0], m_w_proj_att[0], m_w_out[0]], axis=0)
    v_pr = jnp.concatenate([v_w_proj_ssd[0], v_w_proj_att[0], v_w_out[0]], axis=0)
    pr = (gw_pr,) + _adamw(w_pr32, gw_pr, m_pr, v_pr, "adamw_w_proj")
    upd["w_proj_ssd"] = tuple(a[0:512] for a in pr)
    upd["w_proj_att"] = tuple(a[512:768] for a in pr)
    upd["w_out"] = tuple(a[768:1024] for a in pr)
    upd["conv_w"] = (g_conv_w,) + _adamw(conv_w[0], g_conv_w, m_conv_w[0], v_conv_w[0], "adamw_conv_w")
    upd["meta_tokens"] = (g_meta,) + _adamw(meta_tokens, g_meta, m_meta_tokens, v_meta_tokens, "adamw_meta")
    pk = lambda np_, cb, sn, gb, npo, dtb, al, ds, fg: _pad_rows(_pack_small(np_, cb, sn, gb, npo, dtb, al, ds, fg), _SMALL_PAD)
    w_sm = pk(norm_pre, conv_b, ssd_norm, gate_bias, norm_post, dt_bias, a_log, d_skip, fgate_bias)
    m_sm = pk(m_norm_pre, m_conv_b, m_ssd_norm, m_gate_bias, m_norm_post, m_dt_bias, m_a_log, m_d_skip, m_fgate_bias)
    v_sm = pk(v_norm_pre, v_conv_b, v_ssd_norm, v_gate_bias, v_norm_post, v_dt_bias, v_a_log, v_d_skip, v_fgate_bias)
    sm = [_unpack_small(a) for a in (g_small,) + _adamw(w_sm, g_small, m_sm, v_sm, "adamw_small")]
    for name in ("norm_pre", "conv_b", "dt_bias", "a_log", "d_skip", "ssd_norm", "fgate_bias", "gate_bias", "norm_post"):
        upd[name] = tuple(s[name] for s in sm)
    lead = ("w_in", "conv_w", "w_proj_ssd", "w_proj_att", "w_out")
    order = ("meta_tokens", "norm_pre", "w_in", "conv_w", "conv_b", "dt_bias", "a_log", "d_skip", "ssd_norm",
             "fgate_bias", "gate_bias", "w_proj_ssd", "w_proj_att", "w_out", "norm_post")
    outs = [loss, grad_x]
    for part_i in range(4):
        for name in order:
            a = upd[name][part_i]
            outs.append(a[None] if name in lead else a)
    return tuple(outs)
```

```python
import functools
import math

import jax
import jax.numpy as jnp
from jax import lax
from jax.experimental import pallas as pl
from jax.experimental.pallas import tpu as pltpu

F32 = jnp.float32
BF16 = jnp.bfloat16
HIGHEST = lax.Precision.HIGHEST

D_MODEL = 1024
N_META = 16
CHUNK = 128
PADF = CHUNK - N_META
D_SSD = 2048
H_SSD = 32
G_SSD = 4
N_STATE = 128
CONV_K = 4
CONV_DIM = D_SSD + 2 * G_SSD * N_STATE
H_ATT = 16
D_ATT = 1024
EPS = 1e-6
N_COLS = 11312

C_Z, C_XBC, C_ZA, C_Q, C_K, C_V, C_G = 0, 2048, 5120, 6144, 7168, 8192, 9216
N_MAIN = 11264
N_SMALL = 128
O_Z, O_XBC, O_DT, O_ZA, O_Q, O_K, O_V, O_F, O_G = (
    (0, 2048), (2048, 3072), (5120, 32), (5152, 1024), (6176, 1024), (7200, 1024),
    (8224, 1024), (9248, 16), (9264, 2048))

ADAM_LR, ADAM_B1, ADAM_B2, ADAM_EPS, ADAM_WD, ADAM_STEP = 0.001, 0.9, 0.999, 1e-08, 0.01, 10

VMEM_LIMIT = 56 * 1024 * 1024


def _cp(*sem):
    return pltpu.CompilerParams(dimension_semantics=sem, vmem_limit_bytes=VMEM_LIMIT)


def _tile(n, prefs):
    for p in prefs:
        if n % p == 0:
            return p
    raise ValueError(f"no tile for {n} in {prefs}")


def _iota(shape, dim):
    return lax.broadcasted_iota(jnp.int32, shape, dim)


def _sigmoid(x):
    return 1.0 / (1.0 + jnp.exp(-x))


def _softplus_tail(x):
    return jnp.log(1.0 + jnp.exp(-jnp.abs(x)))


_NN = (((1,), (0,)), ((), ()))
_NT = (((1,), (1,)), ((), ()))
_TN = (((0,), (0,)), ((), ()))


def _dot(a, b, dims=_NN):
    return lax.dot_general(a, b, dims, preferred_element_type=F32)


def _dot_exact(a, b, dims=_NN):
    return lax.dot_general(a, b, dims, precision=HIGHEST, preferred_element_type=F32)


def _hosted_call(body, *, name, grid, in_specs, out_specs, out_shape, scratch_shapes, operands, semantics,
                 exchange=None, aliases=None):
    aliases = dict(aliases or {})
    if exchange is None:
        return pl.pallas_call(body, name=name, out_shape=out_shape, grid=grid, in_specs=in_specs,
                              out_specs=out_specs, scratch_shapes=scratch_shapes, input_output_aliases=aliases,
                              compiler_params=_cp(*semantics))(*operands)
    arrays, shapes, copies, n_sems, in_place = exchange
    n_in, n_out, n_ex = len(operands), len(out_shape), len(arrays)

    def hosted(*refs):
        ex_in = refs[n_in:n_in + n_ex]
        ex_out = refs[n_in + n_ex + n_out:n_in + n_ex + n_out + n_ex]
        own = refs[:n_in] + refs[n_in + n_ex:n_in + n_ex + n_out] + refs[n_in + 2 * n_ex + n_out:-2]
        first = functools.reduce(lambda p, q: p & q, [pl.program_id(d) == 0 for d in range(len(grid))])
        last = functools.reduce(lambda p, q: p & q, [pl.program_id(d) == grid[d] - 1 for d in range(len(grid))])

        def descriptors():
            return copies(ex_out if in_place else ex_in, ex_out, refs[-2], refs[-1])

        @pl.when(first)
        def _():
            for cp in descriptors():
                cp.start()

        body(*own)

        @pl.when(last)
        def _():
            for cp in descriptors():
                cp.wait()

    return pl.pallas_call(
        hosted, name=name,
        out_shape=tuple(out_shape) + tuple(shapes),
        grid=grid,
        in_specs=list(in_specs) + [_ANY] * n_ex,
        out_specs=tuple(out_specs) + (_ANY,) * n_ex,
        input_output_aliases={**aliases, **({n_in + e: n_out + e for e in range(n_ex)} if in_place else {})},
        scratch_shapes=list(scratch_shapes) + [pltpu.SemaphoreType.DMA((n_sems,)), pltpu.SemaphoreType.DMA((n_sems,))],
        compiler_params=_cp(*(("arbitrary",) * len(grid))),
    )(*operands, *arrays)


def _matmul(a, b, mode, out_dtype, name, tm, tn, tk, exchange=None):
    if mode == "tn":
        kdim, m = a.shape
    else:
        m, kdim = a.shape
    n = b.shape[0] if mode == "nt" else b.shape[1]
    nk = kdim // tk
    dims = {"nn": _NN, "nt": _NT, "tn": _TN}[mode]
    a_spec = (pl.BlockSpec((tk, tm), lambda i, j, k: (k, i)) if mode == "tn"
              else pl.BlockSpec((tm, tk), lambda i, j, k: (i, k)))
    b_spec = (pl.BlockSpec((tn, tk), lambda i, j, k: (j, k)) if mode == "nt"
              else pl.BlockSpec((tk, tn), lambda i, j, k: (k, j)))

    def body(a_ref, b_ref, o_ref, acc_ref):
        k = pl.program_id(2)
        p = _dot(a_ref[...].astype(BF16), b_ref[...].astype(BF16), dims)
        if nk == 1:
            o_ref[...] = p.astype(out_dtype)
        else:
            @pl.when(k == 0)
            def _():
                acc_ref[...] = p

            @pl.when(k > 0)
            def _():
                acc_ref[...] += p

            @pl.when(k == nk - 1)
            def _():
                o_ref[...] = acc_ref[...].astype(out_dtype)

    out = _hosted_call(
        body, name=name,
        out_shape=(jax.ShapeDtypeStruct((m, n), out_dtype),),
        grid=(m // tm, n // tn, nk),
        in_specs=[a_spec, b_spec],
        out_specs=(pl.BlockSpec((tm, tn), lambda i, j, k: (i, j)),),
        scratch_shapes=[pltpu.VMEM((tm, tn), F32)],
        operands=(a, b), semantics=("parallel", "parallel", "arbitrary"), exchange=exchange)
    return out[0] if exchange is None else out


_CAT_BLK = 1024


def _piece_ranges(pieces):
    out, off = [], 0
    for p in pieces:
        nb = p.shape[1] // _CAT_BLK
        out.append((off, nb))
        off += nb
    return out, off


def _matmul_cat_nn(pieces, b, name, tm, rows=None, fill=None, exchange=None):
    t = pieces[0].shape[0]
    n = b.shape[1]
    ranges, nk = _piece_ranges(pieces)
    first, ni = rows if rows is not None else (0, t // tm)
    n_in = len(pieces) + 1 + (fill is not None)

    def body(*refs):
        a_refs, b_ref, o_ref, acc_ref = refs[:len(pieces)], refs[len(pieces)], refs[n_in], refs[n_in + 1]
        k = pl.program_id(1)

        @pl.when(k == 0)
        def _():
            acc_ref[...] = jnp.zeros_like(acc_ref)

        for a_ref, (off, nb) in zip(a_refs, ranges):
            @pl.when((k >= off) & (k < off + nb))
            def _(a_ref=a_ref):
                acc_ref[...] += _dot(a_ref[...], b_ref[...])

        @pl.when(k == nk - 1)
        def _():
            o_ref[...] = acc_ref[...]

    def a_spec(off, nb):
        return pl.BlockSpec((tm, _CAT_BLK), lambda i, k: (first + i, jnp.clip(k - off, 0, nb - 1)))

    in_specs = [a_spec(off, nb) for off, nb in ranges] + [pl.BlockSpec((_CAT_BLK, n), lambda i, k: (k, 0))]
    operands = list(pieces) + [b]
    if fill is not None:
        in_specs.append(_ANY)
        operands.append(fill)
    out = _hosted_call(
        body, name=name,
        out_shape=(jax.ShapeDtypeStruct((t, n), F32),),
        grid=(ni, nk),
        in_specs=in_specs,
        out_specs=(pl.BlockSpec((tm, n), lambda i, k: (first + i, 0)),),
        scratch_shapes=[pltpu.VMEM((tm, n), F32)],
        operands=operands, semantics=("parallel", "arbitrary"), exchange=exchange,
        aliases={len(pieces) + 1: 0} if fill is not None else None)
    return out if exchange is not None else out[0]


def _matmul_cat_tn(pieces, b, name, tk):
    t = pieces[0].shape[0]
    n = b.shape[1]
    ranges, nm = _piece_ranges(pieces)
    nk = t // tk

    def body(*refs):
        a_refs, b_ref, o_ref, acc_ref = refs[:len(pieces)], refs[-3], refs[-2], refs[-1]
        m = pl.program_id(0)
        k = pl.program_id(1)

        @pl.when(k == 0)
        def _():
            acc_ref[...] = jnp.zeros_like(acc_ref)

        for a_ref, (off, nb) in zip(a_refs, ranges):
            @pl.when((m >= off) & (m < off + nb))
            def _(a_ref=a_ref):
                acc_ref[...] += _dot(a_ref[...], b_ref[...], _TN)

        @pl.when(k == nk - 1)
        def _():
            o_ref[...] = acc_ref[...]

    def a_spec(off, nb):
        def index(m, k):
            mine = (m >= off) & (m < off + nb)
            return jnp.where(mine, k, 0), jnp.clip(m - off, 0, nb - 1)
        return pl.BlockSpec((tk, _CAT_BLK), index)

    return pl.pallas_call(
        body, name=name,
        out_shape=jax.ShapeDtypeStruct((nm * _CAT_BLK, n), F32),
        grid=(nm, nk),
        in_specs=[a_spec(off, nb) for off, nb in ranges] + [pl.BlockSpec((tk, n), lambda m, k: (k, 0))],
        out_specs=pl.BlockSpec((_CAT_BLK, n), lambda m, k: (m, 0)),
        scratch_shapes=[pltpu.VMEM((_CAT_BLK, n), F32)],
        compiler_params=_cp("parallel", "arbitrary"),
    )(*pieces, b)


def _row_tile(t):
    return _tile(t, (352, 128))


def _row_tile_wide(t):
    return _tile(t, (176, 128))


def _norm1_fwd(h, g):
    t = h.shape[0]
    tm = _row_tile(t)

    def body(h_ref, g_ref, u_ref):
        x = h_ref[...]
        r = lax.rsqrt(jnp.mean(x * x, axis=-1, keepdims=True) + EPS)
        u_ref[...] = (x * r * g_ref[...]).astype(BF16)

    return pl.pallas_call(
        body, name="norm1_fwd",
        out_shape=jax.ShapeDtypeStruct((t, D_MODEL), BF16),
        grid=(t // tm,),
        in_specs=[pl.BlockSpec((tm, D_MODEL), lambda i: (i, 0)),
                  pl.BlockSpec((1, D_MODEL), lambda i: (0, 0))],
        out_specs=pl.BlockSpec((tm, D_MODEL), lambda i: (i, 0)),
        compiler_params=_cp("parallel"),
    )(h, g)


def _norm1_bwd(du_a, du_b, h, g, dy):
    t = h.shape[0]
    tm = _row_tile(t)

    def body(a_ref, b_ref, h_ref, g_ref, dy_ref, dh_ref, dg_ref):
        i = pl.program_id(0)
        x = h_ref[...]
        du = a_ref[...] + b_ref[...]
        r = lax.rsqrt(jnp.mean(x * x, axis=-1, keepdims=True) + EPS)
        gdu = du * g_ref[...]
        dh_ref[...] = dy_ref[...] + r * (gdu - x * (r * r) * jnp.mean(gdu * x, axis=-1, keepdims=True))
        part = jnp.sum(du * x * r, axis=0, keepdims=True)

        @pl.when(i == 0)
        def _():
            dg_ref[...] = part

        @pl.when(i > 0)
        def _():
            dg_ref[...] += part

    row = pl.BlockSpec((tm, D_MODEL), lambda i: (i, 0))
    vec = pl.BlockSpec((1, D_MODEL), lambda i: (0, 0))
    return pl.pallas_call(
        body, name="norm1_bwd",
        out_shape=(jax.ShapeDtypeStruct((t, D_MODEL), F32), jax.ShapeDtypeStruct((1, D_MODEL), F32)),
        grid=(t // tm,),
        in_specs=[row, row, row, vec, row],
        out_specs=(row, vec),
        compiler_params=_cp("arbitrary"),
    )(du_a, du_b, h, g, dy)


def _small_fwd(small, bias_row):
    t = small.shape[0]

    def body(s_ref, b_ref, o_ref, carry_ref):
        c = pl.program_id(0)

        @pl.when(c == 0)
        def _():
            carry_ref[...] = jnp.zeros_like(carry_ref)

        x = s_ref[...] + b_ref[...]
        r0 = _iota((CHUNK, CHUNK), 0)
        r1 = _iota((CHUNK, CHUNK), 1)
        valid = (c * CHUNK + r0) >= PADF
        tail = _softplus_tail(x)
        dt = jnp.where(valid & (r1 < H_SSD), jnp.maximum(x, 0.0) + tail, 0.0)
        lf = jnp.where(valid & (r1 >= H_SSD) & (r1 < H_SSD + H_ATT), jnp.minimum(x, 0.0) - tail, 0.0)
        tri = (r0 >= r1).astype(F32)
        cs = _dot_exact(tri, lf) + carry_ref[...]
        carry_ref[...] = cs[CHUNK - 1:CHUNK, :]
        o_ref[...] = dt + cs

    return pl.pallas_call(
        body, name="small_fwd",
        out_shape=jax.ShapeDtypeStruct((t, N_SMALL), F32),
        grid=(t // CHUNK,),
        in_specs=[pl.BlockSpec((CHUNK, N_SMALL), lambda c: (c, 0)),
                  pl.BlockSpec((1, N_SMALL), lambda c: (0, 0))],
        out_specs=pl.BlockSpec((CHUNK, N_SMALL), lambda c: (c, 0)),
        scratch_shapes=[pltpu.VMEM((1, N_SMALL), F32)],
        compiler_params=_cp("arbitrary"),
    )(small, bias_row)


def _small_bwd(dsm, small, bias_row):
    t = small.shape[0]
    nc = t // CHUNK

    def body(d_ref, s_ref, b_ref, o_ref, db_ref, carry_ref):
        step = pl.program_id(0)
        c = nc - 1 - step

        @pl.when(step == 0)
        def _():
            carry_ref[...] = jnp.zeros_like(carry_ref)
            db_ref[...] = jnp.zeros_like(db_ref)

        x = s_ref[...] + b_ref[...]
        d = d_ref[...]
        r0 = _iota((CHUNK, CHUNK), 0)
        r1 = _iota((CHUNK, CHUNK), 1)
        valid = (c * CHUNK + r0) >= PADF
        is_dt = r1 < H_SSD
        is_f = (r1 >= H_SSD) & (r1 < H_SSD + H_ATT)
        triu = (r1 >= r0).astype(F32)
        dc = jnp.where(is_f, d, 0.0)
        dlf = _dot_exact(triu, dc) + carry_ref[...]
        carry_ref[...] = dlf[0:1, :]
        sg = _sigmoid(x)
        out = jnp.where(valid & is_dt, d * sg, 0.0) + jnp.where(valid & is_f, dlf * (1.0 - sg), 0.0)
        o_ref[...] = out.astype(BF16)
        db_ref[...] += jnp.sum(out, axis=0, keepdims=True)

    blk = pl.BlockSpec((CHUNK, N_SMALL), lambda s: (nc - 1 - s, 0))
    vec = pl.BlockSpec((1, N_SMALL), lambda s: (0, 0))
    return pl.pallas_call(
        body, name="small_bwd",
        out_shape=(jax.ShapeDtypeStruct((t, N_SMALL), BF16), jax.ShapeDtypeStruct((1, N_SMALL), F32)),
        grid=(nc,),
        in_specs=[blk, blk, vec],
        out_specs=(blk, vec),
        scratch_shapes=[pltpu.VMEM((1, N_SMALL), F32)],
        compiler_params=_cp("arbitrary"),
    )(dsm, small, bias_row)


_CONV_TC = 1024
_XBC_BLK = C_XBC // _CONV_TC


def _shift_down(cur, prev8, j):
    rc = pltpu.roll(cur, j, 0)
    rid = _iota(prev8.shape, 0)
    top = jnp.where(rid < j, pltpu.roll(prev8, j, 0), rc[0:8, :])
    return jnp.concatenate([top, rc[8:, :]], axis=0)


def _shift_up(cur, next8, j):
    n = cur.shape[0]
    ru = pltpu.roll(cur, n - j, 0)
    rid = _iota(next8.shape, 0)
    bot = jnp.where(rid >= 8 - j, pltpu.roll(next8, 8 - j, 0), ru[n - 8:, :])
    return jnp.concatenate([ru[:n - 8, :], bot], axis=0)


def _conv_pre(x_ref, p_ref, w_ref, b_ref, i):
    cur = x_ref[...]
    prev = jnp.where(i > 0, p_ref[...], 0.0)
    w = w_ref[...]
    taps = [cur] + [_shift_down(cur, prev, j) for j in (1, 2, 3)]
    acc = b_ref[...] + taps[0] * w[3:4, :]
    for j in (1, 2, 3):
        acc = acc + taps[j] * w[3 - j:4 - j, :]
    return acc, taps


def _conv_fwd(proj, conv_w, conv_b):
    t = proj.shape[0]
    tr = _row_tile(t)

    def body(x_ref, p_ref, w_ref, b_ref, o_ref):
        i = pl.program_id(0)
        acc, _ = _conv_pre(x_ref, p_ref, w_ref, b_ref, i)
        valid = (i * tr + _iota(acc.shape, 0)) >= PADF
        o_ref[...] = jnp.where(valid, acc * _sigmoid(acc), 0.0)

    return pl.pallas_call(
        body, name="conv_fwd",
        out_shape=jax.ShapeDtypeStruct((t, CONV_DIM), F32),
        grid=(t // tr, CONV_DIM // _CONV_TC),
        in_specs=[pl.BlockSpec((tr, _CONV_TC), lambda i, j: (i, _XBC_BLK + j)),
                  pl.BlockSpec((8, _CONV_TC), lambda i, j: (jnp.maximum(i * (tr // 8) - 1, 0), _XBC_BLK + j)),
                  pl.BlockSpec((CONV_K, _CONV_TC), lambda i, j: (0, j)),
                  pl.BlockSpec((1, _CONV_TC), lambda i, j: (0, j))],
        out_specs=pl.BlockSpec((tr, _CONV_TC), lambda i, j: (i, j)),
        compiler_params=_cp("parallel", "parallel"),
    )(proj, proj, conv_w, conv_b)


def _conv_bwd_act(dxbc, proj, conv_w, conv_b):
    t = proj.shape[0]
    tr = _row_tile(t)

    def body(d_ref, x_ref, p_ref, w_ref, b_ref, da_ref, dw_ref, db_ref):
        i = pl.program_id(1)
        acc, taps = _conv_pre(x_ref, p_ref, w_ref, b_ref, i)
        valid = (i * tr + _iota(acc.shape, 0)) >= PADF
        sg = _sigmoid(acc)
        da = jnp.where(valid, d_ref[...] * sg * (1.0 + acc * (1.0 - sg)), 0.0)
        da_ref[...] = da
        dw = jnp.concatenate([jnp.sum(da * taps[3 - k], axis=0, keepdims=True) for k in range(CONV_K)], axis=0)
        db = jnp.sum(da, axis=0, keepdims=True)

        @pl.when(i == 0)
        def _():
            dw_ref[...] = dw
            db_ref[...] = db

        @pl.when(i > 0)
        def _():
            dw_ref[...] += dw
            db_ref[...] += db

    return pl.pallas_call(
        body, name="conv_bwd_act",
        out_shape=(jax.ShapeDtypeStruct((t, CONV_DIM), F32),
                   jax.ShapeDtypeStruct((CONV_K, CONV_DIM), F32),
                   jax.ShapeDtypeStruct((1, CONV_DIM), F32)),
        grid=(CONV_DIM // _CONV_TC, t // tr),
        in_specs=[pl.BlockSpec((tr, _CONV_TC), lambda j, i: (i, j)),
                  pl.BlockSpec((tr, _CONV_TC), lambda j, i: (i, _XBC_BLK + j)),
                  pl.BlockSpec((8, _CONV_TC), lambda j, i: (jnp.maximum(i * (tr // 8) - 1, 0), _XBC_BLK + j)),
                  pl.BlockSpec((CONV_K, _CONV_TC), lambda j, i: (0, j)),
                  pl.BlockSpec((1, _CONV_TC), lambda j, i: (0, j))],
        out_specs=(pl.BlockSpec((tr, _CONV_TC), lambda j, i: (i, j)),
                   pl.BlockSpec((CONV_K, _CONV_TC), lambda j, i: (0, j)),
                   pl.BlockSpec((1, _CONV_TC), lambda j, i: (0, j))),
        compiler_params=_cp("parallel", "arbitrary"),
    )(dxbc, proj, proj, conv_w, conv_b)


def _conv_bwd_in(da, conv_w):
    t = da.shape[0]
    tr = _row_tile(t)
    last8 = t // 8 - 1

    def body(d_ref, n_ref, w_ref, o_ref):
        i = pl.program_id(0)
        cur = d_ref[...]
        nxt = jnp.where(i < pl.num_programs(0) - 1, n_ref[...], 0.0)
        w = w_ref[...]
        acc = cur * w[3:4, :]
        for j in (1, 2, 3):
            acc = acc + _shift_up(cur, nxt, j) * w[3 - j:4 - j, :]
        o_ref[...] = acc.astype(BF16)

    return pl.pallas_call(
        body, name="conv_bwd_in",
        out_shape=jax.ShapeDtypeStruct((t, CONV_DIM), BF16),
        grid=(t // tr, CONV_DIM // _CONV_TC),
        in_specs=[pl.BlockSpec((tr, _CONV_TC), lambda i, j: (i, j)),
                  pl.BlockSpec((8, _CONV_TC), lambda i, j: (jnp.minimum((i + 1) * (tr // 8), last8), j)),
                  pl.BlockSpec((CONV_K, _CONV_TC), lambda i, j: (0, j))],
        out_specs=pl.BlockSpec((tr, _CONV_TC), lambda i, j: (i, j)),
        compiler_params=_cp("parallel", "parallel"),
    )(da, da, conv_w)


_GW = D_SSD // G_SSD


def _ssd_prelude(dt_ref, a_ref, e_scr, es_scr, dte_scr):
    r0 = _iota((CHUNK, CHUNK), 0)
    r1 = _iota((CHUNK, CHUNK), 1)
    dt = jnp.where(r1 < H_SSD, dt_ref[...], 0.0)
    adt = dt * a_ref[...]
    acs = _dot_exact((r0 >= r1).astype(F32), adt)
    acs_t = acs.T
    alast = acs[CHUNK - 1:CHUNK, :]
    exp_a = jnp.exp(acs)
    dec_s = jnp.exp(alast - acs)
    lo = r1 < 64
    for j in range(H_SSD // 2):
        sl = slice(CHUNK * j, CHUNK * (j + 1))
        e_scr[:, sl] = jnp.where(lo, exp_a[:, 2 * j:2 * j + 1], exp_a[:, 2 * j + 1:2 * j + 2])
        es_scr[:, sl] = jnp.where(lo, dec_s[:, 2 * j:2 * j + 1], dec_s[:, 2 * j + 1:2 * j + 2])
        dte_scr[:, sl] = jnp.where(lo, dt[:, 2 * j:2 * j + 1], dt[:, 2 * j + 1:2 * j + 2])
    return dt, acs, acs_t, r0, r1, lo


def _chunk_decay_rows(acs_t, g):
    cd_t = jnp.exp(acs_t[:, CHUNK - 1:CHUNK])
    return jnp.concatenate(
        [jnp.broadcast_to(cd_t[8 * g + hh:8 * g + hh + 1, :], (64, N_STATE)) for hh in range(8)], axis=0)


def _ssd_fwd(xbc, dtlf, a_row, dsk_row):
    t = xbc.shape[0]
    nc = t // CHUNK

    def body(xs_ref, b_ref, c_ref, dt_ref, a_ref, dsk_ref, y_ref, hin_ref, h_scr, e_scr, es_scr, dte_scr):
        c = pl.program_id(0)

        @pl.when(c == 0)
        def _():
            h_scr[...] = jnp.zeros_like(h_scr)

        dt, acs, acs_t, r0, r1, lo = _ssd_prelude(dt_ref, a_ref, e_scr, es_scr, dte_scr)
        causal = r0 >= r1
        for g in range(G_SSD):
            gs = slice(_GW * g, _GW * (g + 1))
            bg = b_ref[:, N_STATE * g:N_STATE * (g + 1)].astype(BF16)
            cg = c_ref[:, N_STATE * g:N_STATE * (g + 1)].astype(BF16)
            cb = _dot(cg, bg, _NT)
            hg = h_scr[gs, :]
            hin_ref[0, gs, :] = hg
            xg = xs_ref[:, gs] * dte_scr[:, gs]
            yoff = _dot(cg, hg.astype(BF16), _NT) * e_scr[:, gs]
            st = _dot((xg * es_scr[:, gs]).astype(BF16), bg, _TN)
            h_scr[gs, :] = hg * _chunk_decay_rows(acs_t, g) + st
            for jj in range(4):
                j = 4 * g + jj
                sl = slice(CHUNK * j, CHUNK * (j + 1))
                xp = xg[:, CHUNK * jj:CHUNK * (jj + 1)]
                acc = yoff[:, CHUNK * jj:CHUNK * (jj + 1)] + dsk_ref[:, sl] * xs_ref[:, sl]
                for hh in range(2):
                    h = 2 * j + hh
                    seg = acs[:, h:h + 1] - acs_t[h:h + 1, :]
                    lm = jnp.exp(jnp.where(causal, seg, -1e30))
                    m = (cb * lm).astype(BF16)
                    xh = jnp.where(lo if hh == 0 else ~lo, xp, 0.0).astype(BF16)
                    acc = acc + _dot(m, xh)
                y_ref[:, sl] = acc

    return pl.pallas_call(
        body, name="ssd_fwd",
        out_shape=(jax.ShapeDtypeStruct((t, D_SSD), F32), jax.ShapeDtypeStruct((nc, D_SSD, N_STATE), F32)),
        grid=(nc,),
        in_specs=[pl.BlockSpec((CHUNK, D_SSD), lambda c: (c, 0)),
                  pl.BlockSpec((CHUNK, _GW), lambda c: (c, 4)),
                  pl.BlockSpec((CHUNK, _GW), lambda c: (c, 5)),
                  pl.BlockSpec((CHUNK, N_SMALL), lambda c: (c, 0)),
                  pl.BlockSpec((1, N_SMALL), lambda c: (0, 0)),
                  pl.BlockSpec((1, D_SSD), lambda c: (0, 0))],
        out_specs=(pl.BlockSpec((CHUNK, D_SSD), lambda c: (c, 0)),
                   pl.BlockSpec((1, D_SSD, N_STATE), lambda c: (c, 0, 0))),
        scratch_shapes=[pltpu.VMEM((D_SSD, N_STATE), F32)] + [pltpu.VMEM((CHUNK, D_SSD), F32)] * 3,
        compiler_params=_cp("arbitrary"),
    )(xbc, xbc, xbc, dtlf, a_row, dsk_row)


def _ssd_bwd(xbc, dtlf, a_row, dsk_row, hin, dy):
    t = xbc.shape[0]
    nc = t // CHUNK

    def body(xs_ref, b_ref, c_ref, dt_ref, a_ref, dsk_ref, hin_ref, dy_ref,
             dxbc_ref, ddt_ref, da_ref, ddsk_ref, dh_scr, e_scr, es_scr, dte_scr, dx_scr, whi_scr, wlo_scr):
        step = pl.program_id(0)

        @pl.when(step == 0)
        def _():
            dh_scr[...] = jnp.zeros_like(dh_scr)
            da_ref[...] = jnp.zeros_like(da_ref)
            ddsk_ref[...] = jnp.zeros_like(ddsk_ref)

        dt, acs, acs_t, r0, r1, lo = _ssd_prelude(dt_ref, a_ref, e_scr, es_scr, dte_scr)
        causal = r0 >= r1
        lane_row = _iota((1, CHUNK), 1)
        dacs = jnp.zeros((CHUNK, CHUNK), F32)
        dacs_t = jnp.zeros((CHUNK, CHUNK), F32)
        dalast = jnp.zeros((1, CHUNK), F32)
        ddt_dir = jnp.zeros((CHUNK, CHUNK), F32)
        ddsk_ref[...] += jnp.sum(dy_ref[...] * xs_ref[...], axis=0, keepdims=True)

        def head_sums(z, pick):
            hi = z.astype(BF16)
            return _dot(hi, pick) + _dot((z - hi.astype(F32)).astype(BF16), pick)

        for g in range(G_SSD):
            gs = slice(_GW * g, _GW * (g + 1))
            pick = (jnp.right_shift(_iota((_GW, CHUNK), 0), 6) + 8 * g == _iota((_GW, CHUNK), 1)).astype(BF16)
            bg = b_ref[:, N_STATE * g:N_STATE * (g + 1)].astype(BF16)
            cg = c_ref[:, N_STATE * g:N_STATE * (g + 1)].astype(BF16)
            cb = _dot(cg, bg, _NT)
            hg = hin_ref[0, gs, :]
            hgb = hg.astype(BF16)
            dhn = dh_scr[gs, :]
            dhnb = dhn.astype(BF16)
            esg = es_scr[:, gs]
            dyg = dy_ref[:, gs]
            xsg = xs_ref[:, gs]
            xg = xsg * dte_scr[:, gs]
            dyeb = (dyg * e_scr[:, gs]).astype(BF16)
            dc = _dot(dyeb, hgb)
            dh_y = _dot(dyeb, cg, _TN)
            dxs = _dot(bg, dhnb, _NT) * esg
            db = _dot((xg * esg).astype(BF16), dhnb)
            cd = _chunk_decay_rows(acs_t, g)
            dh_scr[gs, :] = dhn * cd + dh_y
            end_state = head_sums(jnp.broadcast_to(jnp.sum(xg * dxs, axis=0, keepdims=True), (8, _GW)), pick)[0:1, :]
            carried = dhn * hg * cd
            per_head = jnp.concatenate([jnp.sum(carried[64 * hh:64 * hh + 64, :], axis=0, keepdims=True)
                                        for hh in range(8)], axis=0)
            per_head = jnp.sum(per_head, axis=1, keepdims=True)
            for hh in range(8):
                end_state = end_state + jnp.where(lane_row == 8 * g + hh, per_head[hh:hh + 1, :], 0.0)
            dalast = dalast + end_state
            dcb = jnp.zeros((CHUNK, CHUNK), F32)
            for jj in range(4):
                j = 4 * g + jj
                sl = slice(CHUNK * j, CHUNK * (j + 1))
                ps = slice(CHUNK * jj, CHUNK * (jj + 1))
                xpb = xg[:, ps].astype(BF16)
                dyp = dyg[:, ps]
                dxp = dxs[:, ps]
                for hh in range(2):
                    h = 2 * j + hh
                    ws = slice(CHUNK * (2 * jj + hh), CHUNK * (2 * jj + hh + 1))
                    seg = acs[:, h:h + 1] - acs_t[h:h + 1, :]
                    lm = jnp.exp(jnp.where(causal, seg, -1e30))
                    mf = cb * lm
                    dyh = jnp.where(lo if hh == 0 else ~lo, dyp, 0.0).astype(BF16)
                    gm = _dot(dyh, xpb, _NT)
                    dcb = dcb + gm * lm
                    w = gm * mf
                    whi = w.astype(BF16)
                    whi_scr[:, ws] = whi
                    wlo_scr[:, ws] = (w - whi.astype(F32)).astype(BF16)
                    dacs_t = dacs_t - jnp.where(r0 == h, jnp.sum(w, axis=0, keepdims=True), 0.0)
                    dxp = dxp + _dot(mf.astype(BF16), dyh, _TN)
                dx_scr[:, sl] = dxp
            dxg = dx_scr[:, gs]
            pick_w = (jnp.right_shift(_iota((8 * CHUNK, CHUNK), 0), 7) + 8 * g == _iota((8 * CHUNK, CHUNK), 1)).astype(BF16)
            ch = _dot(cg, hgb, _NT)
            dacs = (dacs + _dot(whi_scr[...], pick_w) + _dot(wlo_scr[...], pick_w)
                    + head_sums(dyg * e_scr[:, gs] * ch - xg * dxs, pick))
            ddt_dir = ddt_dir + head_sums(dxg * xsg, pick)
            dcbb = dcb.astype(BF16)
            dxbc_ref[:, D_SSD + N_STATE * g:D_SSD + N_STATE * (g + 1)] = db + _dot(dcbb, cg, _TN)
            dxbc_ref[:, D_SSD + _GW + N_STATE * g:D_SSD + _GW + N_STATE * (g + 1)] = dc + _dot(dcbb, bg)
        dxbc_ref[:, 0:D_SSD] = dx_scr[...] * dte_scr[...] + dsk_ref[...] * dy_ref[...]
        dacs = dacs + dacs_t.T + jnp.where(r0 == CHUNK - 1, dalast, 0.0)
        dadt = _dot_exact((r1 >= r0).astype(F32), dacs)
        ddt_ref[...] = dadt * a_ref[...] + ddt_dir
        da_ref[...] += jnp.sum(dadt * dt, axis=0, keepdims=True)

    rev = lambda s: (nc - 1 - s, 0)
    return pl.pallas_call(
        body, name="ssd_bwd",
        out_shape=(jax.ShapeDtypeStruct((t, CONV_DIM), F32), jax.ShapeDtypeStruct((t, N_SMALL), F32),
                   jax.ShapeDtypeStruct((1, N_SMALL), F32), jax.ShapeDtypeStruct((1, D_SSD), F32)),
        grid=(nc,),
        in_specs=[pl.BlockSpec((CHUNK, D_SSD), rev),
                  pl.BlockSpec((CHUNK, _GW), lambda s: (nc - 1 - s, 4)),
                  pl.BlockSpec((CHUNK, _GW), lambda s: (nc - 1 - s, 5)),
                  pl.BlockSpec((CHUNK, N_SMALL), rev),
                  pl.BlockSpec((1, N_SMALL), lambda s: (0, 0)),
                  pl.BlockSpec((1, D_SSD), lambda s: (0, 0)),
                  pl.BlockSpec((1, D_SSD, N_STATE), lambda s: (nc - 1 - s, 0, 0)),
                  pl.BlockSpec((CHUNK, D_SSD), rev)],
        out_specs=(pl.BlockSpec((CHUNK, CONV_DIM), rev),
                   pl.BlockSpec((CHUNK, N_SMALL), rev),
                   pl.BlockSpec((1, N_SMALL), lambda s: (0, 0)),
                   pl.BlockSpec((1, D_SSD), lambda s: (0, 0))),
        scratch_shapes=([pltpu.VMEM((D_SSD, N_STATE), F32)] + [pltpu.VMEM((CHUNK, D_SSD), F32)] * 4
                        + [pltpu.VMEM((CHUNK, 8 * CHUNK), BF16)] * 2),
        compiler_params=_cp("arbitrary"),
    )(xbc, xbc, xbc, dtlf, a_row, dsk_row, hin, dy)


_NPAIR = H_ATT // 2
_QB, _KB, _VB = C_Q // 128, C_K // 128, C_V // 128
_SCALE = 1.0 / math.sqrt(64.0)


def _attn_blocks(t):
    return _tile(t, (1408, 384, 256, 128)), _tile(t, (384, 128))


def _split3(c):
    hi = c.astype(BF16).astype(F32)
    rest = c - hi
    mid = rest.astype(BF16).astype(F32)
    return hi, mid, rest - mid


def _head_lanes(lane, hh):
    return (lane < 64, 64) if hh == 0 else (lane >= 64, 0)


def _q_operand(q, cq, lane, hh):
    sel, first = _head_lanes(lane, hh)
    out = jnp.where(sel, q, 0.0)
    for n, col in enumerate(_split3(cq) + (1.0, 1.0, 1.0)):
        out = jnp.where(lane == first + n, col, out)
    return out.astype(BF16)


def _k_operand(k, ck, lane, hh):
    sel, first = _head_lanes(lane, hh)
    hi, mid, lo = _split3(ck)
    out = jnp.where(sel, k, 0.0)
    for n, col in enumerate((1.0, 1.0, 1.0, -hi, -mid, -lo)):
        out = jnp.where(lane == first + n, col, out)
    return out.astype(BF16)


def _needs_mask(i, kk, bq, bk):
    return kk * bk + bk - 1 > i * bq


_C_FILLER = 2.0 ** 30


def _attn_fwd(proj, c_col):
    t = proj.shape[0]
    bq, bk = _attn_blocks(t)
    nq, nk = t // bq, t // bk
    rs = 16

    def last_kv(i):
        return (i * bq + bq - 1) // bk

    def body(q_ref, k_ref, v_ref, cq_ref, ck_ref, o_ref, lse_ref, qs_scr, s_scr, p_scr, m_scr, acc_scr):
        i = pl.program_id(1)
        kk = pl.program_id(2)
        lane_q = _iota((bq, 128), 1)

        @pl.when(kk == 0)
        def _():
            m_scr[...] = jnp.full_like(m_scr, -1e30)
            acc_scr[...] = jnp.zeros_like(acc_scr)
            q = q_ref[...] * _SCALE
            cq = cq_ref[0]
            for hh in range(2):
                qs_scr[hh] = _q_operand(q, cq[:, hh:hh + 1], lane_q, hh)

        def step(masked):
            lane_k = _iota((bk, 128), 1)
            k = k_ref[...]
            v = v_ref[...]
            ck = ck_ref[0]
            ahead = _iota((rs, bq), 0) - _iota((rs, bq), 1)
            vss = []
            for hh in range(2):
                sel, first = _head_lanes(lane_k, hh)
                ks = _k_operand(k, ck[:, hh:hh + 1], lane_k, hh)
                vss.append(jnp.where(sel, v, jnp.where(lane_k == first, 1.0, 0.0)).astype(BF16))
                s_scr[hh] = _dot(ks, qs_scr[hh], _NT)
            for hh in range(2):
                vs = vss[hh]

                def block_max(r, mx):
                    rows = pl.ds(pl.multiple_of(r * rs, rs), rs)
                    s = s_scr[hh, rows, :]
                    if masked:
                        s = jnp.where(ahead <= i * bq - kk * bk - r * rs, s, -1e30)
                        s_scr[hh, rows, :] = s
                    return jnp.maximum(mx, s)

                mx = lax.fori_loop(0, bk // rs, block_max, jnp.full((rs, bq), -1e30, F32), unroll=True)
                m_old = m_scr[hh]
                m_new = jnp.maximum(m_old, jnp.max(mx, axis=0, keepdims=True))
                m_scr[hh] = m_new

                def probs(r, carry):
                    rows = pl.ds(pl.multiple_of(r * rs, rs), rs)
                    p_scr[hh, rows, :] = jnp.exp(s_scr[hh, rows, :] - m_new).astype(BF16)
                    return carry

                lax.fori_loop(0, bk // rs, probs, 0, unroll=True)
                acc_scr[hh] = acc_scr[hh] * jnp.exp(m_old - m_new) + _dot(vs, p_scr[hh], _TN)

        active = kk <= last_kv(i)
        masked = _needs_mask(i, kk, bq, bk)

        @pl.when(active & masked)
        def _():
            step(True)

        @pl.when(active & jnp.logical_not(masked))
        def _():
            step(False)

        @pl.when(kk == nk - 1)
        def _():
            a = acc_scr[0]
            b = acc_scr[1]
            la = a[64:65, :]
            lb = b[0:1, :]
            o_ref[...] = jnp.where(lane_q < 64, (a / la).T, (b / lb).T)
            lse_ref[0] = jnp.concatenate([m_scr[0] + jnp.log(la), m_scr[1] + jnp.log(lb)], axis=0)

    kvi = lambda i, kk: jnp.minimum(kk, last_kv(i))
    kv = lambda off: pl.BlockSpec((bk, 128), lambda j, i, kk: (kvi(i, kk), off + j))
    return pl.pallas_call(
        body, name="attn_fwd",
        out_shape=(jax.ShapeDtypeStruct((t, D_ATT), F32), jax.ShapeDtypeStruct((_NPAIR, 2, t), F32)),
        grid=(_NPAIR, nq, nk),
        in_specs=[pl.BlockSpec((bq, 128), lambda j, i, kk: (i, _QB + j)),
                  kv(_KB), kv(_VB),
                  pl.BlockSpec((1, bq, 2), lambda j, i, kk: (j, i, 0)),
                  pl.BlockSpec((1, bk, 2), lambda j, i, kk: (j, kvi(i, kk), 0))],
        out_specs=(pl.BlockSpec((bq, 128), lambda j, i, kk: (i, j)),
                   pl.BlockSpec((1, 2, bq), lambda j, i, kk: (j, 0, i))),
        scratch_shapes=[pltpu.VMEM((2, bq, 128), BF16), pltpu.VMEM((2, bk, bq), F32), pltpu.VMEM((2, bk, bq), BF16),
                        pltpu.VMEM((2, 1, bq), F32), pltpu.VMEM((2, 128, bq), F32)],
        compiler_params=_cp("parallel", "parallel", "arbitrary"),
    )(proj, proj, proj, c_col, c_col)


def _attn_delta(do, o):
    t = do.shape[0]
    tm = _row_tile(t)

    def body(do_ref, o_ref, d_ref):
        pick = (jnp.right_shift(_iota((D_ATT, 128), 0), 6) == _iota((D_ATT, 128), 1)).astype(F32)
        d_ref[...] = _dot_exact(do_ref[...] * o_ref[...], pick)

    row = pl.BlockSpec((tm, D_ATT), lambda i: (i, 0))
    return pl.pallas_call(
        body, name="attn_delta",
        out_shape=jax.ShapeDtypeStruct((t, 128), F32),
        grid=(t // tm,), in_specs=[row, row], out_specs=pl.BlockSpec((tm, 128), lambda i: (i, 0)),
        compiler_params=_cp("parallel"),
    )(do, o)


def _attn_bwd(proj, c_col, lse_row, dl_row, do, exchange=None):
    t = proj.shape[0]
    bq, bk = _attn_blocks(t)
    nq, nk = t // bq, t // bk
    rs = 16

    def first_q(kk):
        return (kk * bk) // bq

    def body(q_ref, k_ref, v_ref, cq_ref, ck_ref, lse_ref, dl_ref, do_ref,
             dq_ref, dk_ref, dv_ref, dck_ref, dcq_ref,
             qs_scr, doh_scr, ks_scr, s_scr, dp_scr, p_scr, ds_scr, dq_scr, dk_scr, dv_scr):
        kk = pl.program_id(1)
        i = pl.program_id(2)
        lane_q = _iota((bq, 128), 1)
        lane_k = _iota((bk, 128), 1)
        qrows = pl.ds(pl.multiple_of(i * bq, 128), bq)

        @pl.when(kk == 0)
        def _():
            q = q_ref[...] * _SCALE
            cq = cq_ref[0]
            do_ = do_ref[...]
            for hh in range(2):
                qs_scr[hh, qrows, :] = _q_operand(q, cq[:, hh:hh + 1], lane_q, hh)
                doh_scr[hh, qrows, :] = jnp.where(_head_lanes(lane_q, hh)[0], do_, 0.0).astype(BF16)
                dq_scr[hh, qrows, :] = jnp.zeros((bq, 128), F32)

        @pl.when(i == 0)
        def _():
            dk_scr[...] = jnp.zeros_like(dk_scr)
            dv_scr[...] = jnp.zeros_like(dv_scr)
            k = k_ref[...]
            ck = ck_ref[0]
            for hh in range(2):
                ks_scr[hh] = _k_operand(k, ck[:, hh:hh + 1], lane_k, hh)

        def step(masked):
            v16 = v_ref[...].astype(BF16)
            lse = lse_ref[0]
            dl = dl_ref[0]
            ahead = _iota((rs, bq), 0) - _iota((rs, bq), 1)
            for hh in range(2):
                s_scr[hh] = _dot(ks_scr[hh], qs_scr[hh, qrows, :], _NT)
                dp_scr[hh] = _dot(v16, doh_scr[hh, qrows, :], _NT)
            for hh in range(2):
                qs = qs_scr[hh, qrows, :]
                doh = doh_scr[hh, qrows, :]

                def strip(r, carry):
                    rows = pl.ds(pl.multiple_of(r * rs, rs), rs)
                    p = jnp.exp(s_scr[hh, rows, :] - lse[hh:hh + 1, :])
                    if masked:
                        p = jnp.where(ahead <= i * bq - kk * bk - r * rs, p, 0.0)
                    p_scr[hh, rows, :] = p.astype(BF16)
                    ds_scr[hh, rows, :] = (p * (dp_scr[hh, rows, :] - dl[hh:hh + 1, :])).astype(BF16)
                    return carry

                lax.fori_loop(0, bk // rs, strip, 0, unroll=True)
                dv_scr[...] += _dot(p_scr[hh], doh)
                dk_scr[hh] += _dot(ds_scr[hh], qs)
                dq_scr[hh, qrows, :] += _dot(ds_scr[hh], ks_scr[hh], _TN)

        active = i >= first_q(kk)
        masked = _needs_mask(i, kk, bq, bk)

        @pl.when(active & masked)
        def _():
            step(True)

        @pl.when(active & jnp.logical_not(masked))
        def _():
            step(False)

        @pl.when(i == nq - 1)
        def _():
            dka = dk_scr[0]
            dkb = dk_scr[1]
            dk_ref[...] = jnp.where(lane_k < 64, dka, dkb).astype(BF16)
            dv_ref[...] = dv_scr[...].astype(BF16)
            dck_ref[0] = -jnp.where(_iota((bk, 2), 1) == 0, dka[:, 67:68], dkb[:, 3:4])

        @pl.when((kk == nk - 1) & (i == nq - 1))
        def _():
            lane_t = _iota((t, 128), 1)
            dqa = dq_scr[0]
            dqb = dq_scr[1]
            dq_ref[...] = (jnp.where(lane_t < 64, dqa, dqb) * _SCALE).astype(BF16)
            dcq_ref[0] = jnp.where(_iota((t, 2), 1) == 0, dqa[:, 64:65], dqb[:, 0:1])

    qi = lambda kk, i: jnp.where(kk == 0, i, nq - 1)
    qspec = lambda off: pl.BlockSpec((bq, 128), lambda j, kk, i: (qi(kk, i), off + j))
    kspec = lambda off: pl.BlockSpec((bk, 128), lambda j, kk, i: (kk, off + j))
    rowspec = pl.BlockSpec((1, 2, bq), lambda j, kk, i: (j, 0, jnp.maximum(i, first_q(kk))))
    return _hosted_call(
        body, name="attn_bwd",
        out_shape=(jax.ShapeDtypeStruct((t, D_ATT), BF16), jax.ShapeDtypeStruct((t, D_ATT), BF16),
                   jax.ShapeDtypeStruct((t, D_ATT), BF16), jax.ShapeDtypeStruct((_NPAIR, t, 2), F32),
                   jax.ShapeDtypeStruct((_NPAIR, t, 2), F32)),
        grid=(_NPAIR, nk, nq),
        in_specs=[qspec(_QB), kspec(_KB), kspec(_VB),
                  pl.BlockSpec((1, bq, 2), lambda j, kk, i: (j, qi(kk, i), 0)),
                  pl.BlockSpec((1, bk, 2), lambda j, kk, i: (j, kk, 0)),
                  rowspec, rowspec, qspec(0)],
        out_specs=(pl.BlockSpec((t, 128), lambda j, kk, i: (0, j)),
                   pl.BlockSpec((bk, 128), lambda j, kk, i: (kk, j)),
                   pl.BlockSpec((bk, 128), lambda j, kk, i: (kk, j)),
                   pl.BlockSpec((1, bk, 2), lambda j, kk, i: (j, kk, 0)),
                   pl.BlockSpec((1, t, 2), lambda j, kk, i: (j, 0, 0))),
        scratch_shapes=[pltpu.VMEM((2, t, 128), BF16), pltpu.VMEM((2, t, 128), BF16), pltpu.VMEM((2, bk, 128), BF16),
                        pltpu.VMEM((2, bk, bq), F32), pltpu.VMEM((2, bk, bq), F32),
                        pltpu.VMEM((2, bk, bq), BF16), pltpu.VMEM((2, bk, bq), BF16),
                        pltpu.VMEM((2, t, 128), F32), pltpu.VMEM((2, bk, 128), F32), pltpu.VMEM((bk, 128), F32)],
        operands=(proj, proj, proj, c_col, c_col, lse_row, dl_row, do),
        semantics=("parallel", "arbitrary", "arbitrary"), exchange=exchange)


def _premerge_fwd(y, o, proj, gamma):
    t = y.shape[0]
    tm = _row_tile_wide(t)

    def body(y_ref, z_ref, o_ref, za_ref, g_ref, ys_ref, ya_ref):
        z = z_ref[...]
        u = y_ref[...] * (z * _sigmoid(z))
        for g in range(G_SSD):
            gs = slice(_GW * g, _GW * (g + 1))
            ug = u[:, gs]
            r = lax.rsqrt(jnp.mean(ug * ug, axis=-1, keepdims=True) + EPS)
            ys_ref[:, gs] = (ug * r * g_ref[:, gs]).astype(BF16)
        za = za_ref[...]
        ya_ref[...] = (o_ref[...] * (za * _sigmoid(za))).astype(BF16)

    return pl.pallas_call(
        body, name="premerge_fwd",
        out_shape=(jax.ShapeDtypeStruct((t, D_SSD), BF16), jax.ShapeDtypeStruct((t, D_ATT), BF16)),
        grid=(t // tm,),
        in_specs=[pl.BlockSpec((tm, D_SSD), lambda i: (i, 0)),
                  pl.BlockSpec((tm, D_SSD), lambda i: (i, C_Z // D_SSD)),
                  pl.BlockSpec((tm, D_ATT), lambda i: (i, 0)),
                  pl.BlockSpec((tm, D_ATT), lambda i: (i, C_ZA // D_ATT)),
                  pl.BlockSpec((1, D_SSD), lambda i: (0, 0))],
        out_specs=(pl.BlockSpec((tm, D_SSD), lambda i: (i, 0)), pl.BlockSpec((tm, D_ATT), lambda i: (i, 0))),
        compiler_params=_cp("parallel"),
    )(y, proj, o, proj, gamma)


def _premerge_bwd(dys, dya, y, o, proj, gamma, exchange=None):
    t = y.shape[0]
    tm = _row_tile_wide(t)

    def body(dys_ref, dya_ref, y_ref, z_ref, o_ref, za_ref, g_ref, dy_ref, dz_ref, do_ref, dza_ref, dg_ref):
        i = pl.program_id(0)
        z = z_ref[...]
        sz = _sigmoid(z)
        silu = z * sz
        dsilu = sz * (1.0 + z * (1.0 - sz))
        yv = y_ref[...]
        u = yv * silu
        parts = []
        for g in range(G_SSD):
            gs = slice(_GW * g, _GW * (g + 1))
            ug = u[:, gs]
            r = lax.rsqrt(jnp.mean(ug * ug, axis=-1, keepdims=True) + EPS)
            n = ug * r
            dout = dys_ref[:, gs]
            dn = dout * g_ref[:, gs]
            du = r * (dn - n * jnp.mean(dn * n, axis=-1, keepdims=True))
            dy_ref[:, gs] = du * silu[:, gs]
            dz_ref[:, gs] = (du * yv[:, gs] * dsilu[:, gs]).astype(BF16)
            parts.append(jnp.sum(dout * n, axis=0, keepdims=True))
        dg = jnp.concatenate(parts, axis=1)
        za = za_ref[...]
        sa = _sigmoid(za)
        dya_ = dya_ref[...]
        do_ref[...] = dya_ * (za * sa)
        dza_ref[...] = (dya_ * o_ref[...] * (sa * (1.0 + za * (1.0 - sa)))).astype(BF16)

        @pl.when(i == 0)
        def _():
            dg_ref[...] = dg

        @pl.when(i > 0)
        def _():
            dg_ref[...] += dg

    ssd = pl.BlockSpec((tm, D_SSD), lambda i: (i, 0))
    att = pl.BlockSpec((tm, D_ATT), lambda i: (i, 0))
    vec = pl.BlockSpec((1, D_SSD), lambda i: (0, 0))
    return _hosted_call(
        body, name="premerge_bwd",
        out_shape=(jax.ShapeDtypeStruct((t, D_SSD), F32), jax.ShapeDtypeStruct((t, D_SSD), BF16),
                   jax.ShapeDtypeStruct((t, D_ATT), F32), jax.ShapeDtypeStruct((t, D_ATT), BF16),
                   jax.ShapeDtypeStruct((1, D_SSD), F32)),
        grid=(t // tm,),
        in_specs=[ssd, att, ssd, pl.BlockSpec((tm, D_SSD), lambda i: (i, C_Z // D_SSD)), att,
                  pl.BlockSpec((tm, D_ATT), lambda i: (i, C_ZA // D_ATT)), vec],
        out_specs=(ssd, ssd, att, att, vec),
        scratch_shapes=[],
        operands=(dys, dya, y, proj, o, proj, gamma), semantics=("arbitrary",), exchange=exchange)


_G_BLK = C_G // D_MODEL


def _merge_fwd(a, b, proj, gate_bias):
    t = a.shape[0]
    tm = _row_tile(t)

    def body(a_ref, b_ref, gs_ref, ga_ref, bias_ref, m_ref):
        g_ssd = _sigmoid(gs_ref[...] + bias_ref[:, 0:D_MODEL])
        g_att = _sigmoid(ga_ref[...] + bias_ref[:, D_MODEL:2 * D_MODEL])
        m_ref[...] = (g_ssd * a_ref[...] + g_att * b_ref[...]).astype(BF16)

    row = pl.BlockSpec((tm, D_MODEL), lambda i: (i, 0))
    return pl.pallas_call(
        body, name="merge_fwd",
        out_shape=jax.ShapeDtypeStruct((t, D_MODEL), BF16),
        grid=(t // tm,),
        in_specs=[row, row,
                  pl.BlockSpec((tm, D_MODEL), lambda i: (i, _G_BLK)),
                  pl.BlockSpec((tm, D_MODEL), lambda i: (i, _G_BLK + 1)),
                  pl.BlockSpec((1, 2 * D_MODEL), lambda i: (0, 0))],
        out_specs=row,
        compiler_params=_cp("parallel"),
    )(a, b, proj, proj, gate_bias)


def _merge_bwd(dm, a, b, proj, gate_bias):
    t = a.shape[0]
    tm = _row_tile(t)

    def body(dm_ref, a_ref, b_ref, gs_ref, ga_ref, bias_ref, da_ref, db_ref, dg_ref, dbias_ref):
        i = pl.program_id(0)
        dm_ = dm_ref[...]
        g_ssd = _sigmoid(gs_ref[...] + bias_ref[:, 0:D_MODEL])
        g_att = _sigmoid(ga_ref[...] + bias_ref[:, D_MODEL:2 * D_MODEL])
        da_ref[...] = (dm_ * g_ssd).astype(BF16)
        db_ref[...] = (dm_ * g_att).astype(BF16)
        dgs = dm_ * a_ref[...] * g_ssd * (1.0 - g_ssd)
        dga = dm_ * b_ref[...] * g_att * (1.0 - g_att)
        dg_ref[:, 0:D_MODEL] = dgs.astype(BF16)
        dg_ref[:, D_MODEL:2 * D_MODEL] = dga.astype(BF16)
        part = jnp.concatenate([jnp.sum(dgs, axis=0, keepdims=True), jnp.sum(dga, axis=0, keepdims=True)], axis=1)

        @pl.when(i == 0)
        def _():
            dbias_ref[...] = part

        @pl.when(i > 0)
        def _():
            dbias_ref[...] += part

    row = pl.BlockSpec((tm, D_MODEL), lambda i: (i, 0))
    wide = pl.BlockSpec((tm, 2 * D_MODEL), lambda i: (i, 0))
    vec = pl.BlockSpec((1, 2 * D_MODEL), lambda i: (0, 0))
    return pl.pallas_call(
        body, name="merge_bwd",
        out_shape=(jax.ShapeDtypeStruct((t, D_MODEL), BF16), jax.ShapeDtypeStruct((t, D_MODEL), BF16),
                   jax.ShapeDtypeStruct((t, 2 * D_MODEL), BF16), jax.ShapeDtypeStruct((1, 2 * D_MODEL), F32)),
        grid=(t // tm,),
        in_specs=[row, row, row,
                  pl.BlockSpec((tm, D_MODEL), lambda i: (i, _G_BLK)),
                  pl.BlockSpec((tm, D_MODEL), lambda i: (i, _G_BLK + 1)), vec],
        out_specs=(row, row, wide, vec),
        compiler_params=_cp("arbitrary"),
    )(dm, a, b, proj, proj, gate_bias)


def _post(o2, h, target, g):
    t = o2.shape[0]
    nc = t // CHUNK

    def body(o_ref, h_ref, t_ref, g_ref, dy_ref, do_ref, dg_ref, loss_ref):
        c = pl.program_id(0)
        x = o_ref[...]
        r = lax.rsqrt(jnp.mean(x * x, axis=-1, keepdims=True) + EPS)
        n = x * r
        y = h_ref[...] + n * g_ref[...]
        diff = jnp.where(c > 0, y - t_ref[...], 0.0)
        dy = diff * (1.0 / D_MODEL)
        dy_ref[...] = dy
        gdy = dy * g_ref[...]
        do_ref[...] = (r * (gdy - n * jnp.mean(gdy * n, axis=-1, keepdims=True))).astype(BF16)
        dg = jnp.sum(dy * n, axis=0, keepdims=True)
        lpart = 0.5 * jnp.sum(jnp.sum(diff * diff, axis=1, keepdims=True), axis=0, keepdims=True) * (1.0 / D_MODEL)
        sel = (_iota((8, 128), 0) == 0) & (_iota((8, 128), 1) == 0)

        @pl.when(c == 0)
        def _():
            dg_ref[...] = dg
            loss_ref[...] = jnp.zeros_like(loss_ref)

        @pl.when(c > 0)
        def _():
            dg_ref[...] += dg
            loss_ref[...] += jnp.where(sel, lpart, 0.0)

    row = pl.BlockSpec((CHUNK, D_MODEL), lambda c: (c, 0))
    vec = pl.BlockSpec((1, D_MODEL), lambda c: (0, 0))
    return pl.pallas_call(
        body, name="post",
        out_shape=(jax.ShapeDtypeStruct((t, D_MODEL), F32), jax.ShapeDtypeStruct((t, D_MODEL), BF16),
                   jax.ShapeDtypeStruct((1, D_MODEL), F32), jax.ShapeDtypeStruct((8, 128), F32)),
        grid=(nc,),
        in_specs=[row, row, pl.BlockSpec((CHUNK, D_MODEL), lambda c: (jnp.maximum(c - 1, 0), 0)), vec],
        out_specs=(row, row, vec, pl.BlockSpec((8, 128), lambda c: (0, 0))),
        compiler_params=_cp("arbitrary"),
    )(o2, h, target, g)


def _mm_tiles(t):
    return _tile(t, (704, 384, 128))


def _local_step(h, target, w_main, w_small, pr_slots, ids, norm_pre, conv_w, conv_b, bias_row, a_row,
                dsk_row, ssd_norm, gate_bias, norm_post):
    t = h.shape[0]
    tm = _mm_tiles(t)
    u = _norm1_fwd(h, norm_pre)
    proj, pr_slots = _matmul(u, w_main, "nt", F32, "inproj", tm, 1024, D_MODEL,
                             exchange=_gather_stage([pr_slots], to_sibling=False))
    small, pr_slots = _matmul(u, w_small, "nt", F32, "inproj_small", tm, N_SMALL, D_MODEL,
                              exchange=_gather_stage([pr_slots], to_sibling=True))
    wps = pr_slots[:, 0:512].reshape(D_SSD, D_MODEL)
    wpa = pr_slots[:, 512:768].reshape(D_ATT, D_MODEL)
    wout = pr_slots[:, 768:1024].reshape(D_MODEL, D_MODEL)
    dtlf = _small_fwd(small, bias_row)
    xbc = _conv_fwd(proj, conv_w, conv_b)
    y, hin = _ssd_fwd(xbc, dtlf, a_row, dsk_row)
    c_tok = dtlf[:, H_SSD:H_SSD + H_ATT]
    c_tok = jnp.where(jnp.arange(t)[:, None] < PADF, _C_FILLER, c_tok)
    c_col = c_tok.reshape(t, _NPAIR, 2).transpose(1, 0, 2)
    o, lse = _attn_fwd(proj, c_col)
    ys, ya = _premerge_fwd(y, o, proj, ssd_norm)
    a = _matmul(ys, wps, "nn", F32, "proj_ssd", tm, D_MODEL, D_SSD)
    b = _matmul(ya, wpa, "nn", F32, "proj_att", tm, D_MODEL, D_ATT)
    merged = _merge_fwd(a, b, proj, gate_bias)
    o2 = _matmul(merged, wout, "nn", F32, "out_proj", tm, D_MODEL, D_MODEL)
    dy_out, do2, d_norm_post, loss_blk = _post(o2, h, target, norm_post)

    dm = _matmul(do2, wout, "nt", F32, "out_proj_dx", tm, D_MODEL, D_MODEL)
    d_wout = _matmul(merged, do2, "tn", F32, "out_proj_dw", D_MODEL, D_MODEL, tm)
    da, db, dgraw, d_gate_bias = _merge_bwd(dm, a, b, proj, gate_bias)
    dys = _matmul(da, wps, "nt", F32, "proj_ssd_dx", tm, D_SSD, D_MODEL)
    d_wps = _matmul(ys, da, "tn", F32, "proj_ssd_dw", D_SSD, D_MODEL, tm)
    dya = _matmul(db, wpa, "nt", F32, "proj_att_dx", tm, D_ATT, D_MODEL)
    d_wpa = _matmul(ya, db, "tn", F32, "proj_att_dw", D_ATT, D_MODEL, tm)
    g32_pr = jnp.concatenate([d_wps.reshape(4, 512, D_MODEL), d_wpa.reshape(4, 256, D_MODEL),
                              d_wout.reshape(4, 256, D_MODEL)], axis=1)
    dy, dz, do, dza, d_ssd_norm, ra_pr = _premerge_bwd(dys, dya, y, o, proj, ssd_norm,
                                                       exchange=_pair_swap([g32_pr]))
    pb_pr = _add_pair(ids, g32_pr, ra_pr)
    dl_row = _attn_delta(do, o)[:, 0:H_ATT].T.reshape(_NPAIR, 2, t)
    dq, dk, dv, dc_key, dc_qry, rb_pr = _attn_bwd(proj, c_col, lse, dl_row, do, exchange=_chip_exchange([pb_pr]))
    half_pr = _add_chips(ids, g32_pr, ra_pr, rb_pr)
    dxbc, ddt, d_a, d_dsk = _ssd_bwd(xbc, dtlf, a_row, dsk_row, hin, dy)
    dact, d_conv_w, d_conv_b = _conv_bwd_act(dxbc, proj, conv_w, conv_b)
    dxbc_raw = _conv_bwd_in(dact, conv_w)
    dc_tok = jnp.transpose(dc_key + dc_qry, (1, 0, 2)).reshape(t, H_ATT)
    dsm = ddt + jnp.pad(dc_tok, ((0, 0), (H_SSD, N_SMALL - H_SSD - H_ATT)))
    dsmall, d_bias_row = _small_bwd(dsm, small, bias_row)
    dproj = [dz, dxbc_raw, dza, dq, dk, dv, dgraw]
    return dict(loss_blk=loss_blk, u=u, dy_out=dy_out, dproj=dproj, dsmall=dsmall, half_pr=half_pr,
                d_conv_w=d_conv_w, d_conv_b=d_conv_b,
                d_bias_row=d_bias_row, d_a=d_a, d_dsk=d_dsk, d_ssd_norm=d_ssd_norm,
                d_gate_bias=d_gate_bias, d_norm_post=d_norm_post)


def _to_aligned_rows(slots):
    w = slots.reshape(N_COLS, slots.shape[2])

    def cut(o):
        return w[o[0]:o[0] + o[1]]
    main = jnp.concatenate([cut(O_Z), cut(O_XBC), cut(O_ZA), cut(O_Q), cut(O_K), cut(O_V), cut(O_G)], axis=0)
    pad = jnp.zeros((N_SMALL - H_SSD - H_ATT, w.shape[1]), w.dtype)
    small = jnp.concatenate([cut(O_DT), cut(O_F), pad], axis=0)
    return main, small


def _from_aligned_rows(main, small):
    def cm(c0, n):
        return main[c0:c0 + n]
    flat = jnp.concatenate([cm(C_Z, 2048), cm(C_XBC, 3072), small[0:H_SSD], cm(C_ZA, 1024),
                            cm(C_Q, 1024), cm(C_K, 1024), cm(C_V, 1024), small[H_SSD:H_SSD + H_ATT],
                            cm(C_G, 2048)], axis=0)
    return flat.reshape(4, N_COLS // 4, flat.shape[1])


_MESH = pl.DeviceIdType.MESH
_ANY = pl.BlockSpec(memory_space=pl.ANY)
_VM = pl.BlockSpec(memory_space=pltpu.VMEM)
_HALF = 512
N_DEV = 8


def _coords():
    return lax.axis_index("x"), lax.axis_index("y"), lax.axis_index("c")


def _other_chips(x, y):
    return [(1 - x, y), (x, 1 - y), (1 - x, 1 - y)]


def _half(cc):
    return pl.ds(cc * _HALF, _HALF)


def _gather_shards(slots):
    n = len(slots)

    def body(*refs):
        buf = refs[n:2 * n]
        send_sems, recv_sems = refs[2 * n:]
        x, y, c = _coords()
        chip = 2 * x + y
        sibling = (x, y, 1 - c)
        chips = _other_chips(x, y)

        def copy(i, frm, cc, k, to):
            part = buf[i].at[frm, :, _half(cc)]
            return pltpu.make_async_remote_copy(src_ref=part, dst_ref=part, send_sem=send_sems.at[6 * i + k],
                                                recv_sem=recv_sems.at[6 * i + k], device_id=to, device_id_type=_MESH)

        def chip_of(k):
            return 2 * chips[k][0] + chips[k][1]

        first = [copy(i, chip, c, k, (*chips[k], c)) for k in range(3) for i in range(n)]
        for cp in first:
            cp.start()
        passed = []
        for k in range(3):
            for i in range(n):
                copy(i, chip_of(k), c, k, (*chips[k], c)).wait_recv()
                passed.append(copy(i, chip_of(k), c, 3 + k, sibling))
                passed[-1].start()
        for k in range(3):
            for i in range(n):
                copy(i, chip_of(k), 1 - c, 3 + k, sibling).wait_recv()
        for cp in first + passed:
            cp.wait_send()

    return pl.pallas_call(
        body, name="gather_shards",
        out_shape=tuple(jax.ShapeDtypeStruct(s.shape, s.dtype) for s in slots),
        in_specs=[_ANY] * n, out_specs=tuple([_ANY] * n),
        input_output_aliases={i: i for i in range(n)},
        scratch_shapes=[pltpu.SemaphoreType.DMA((6 * n,)), pltpu.SemaphoreType.DMA((6 * n,))],
    )(*slots)


def _allgather8(block, name):
    rows, width = block.shape

    def body(x_ref, out_ref, send_sems, recv_sems, local_sem):
        x, y, c = _coords()
        me, sibling = (x, y, c), (x, y, 1 - c)
        chips = _other_chips(x, y)

        def slot(px, py, pc):
            return out_ref.at[4 * px + 2 * py + pc]

        def copy(k, blk, to, src=None):
            return pltpu.make_async_remote_copy(src_ref=slot(*blk) if src is None else src, dst_ref=slot(*blk),
                                                send_sem=send_sems.at[k], recv_sem=recv_sems.at[k],
                                                device_id=to, device_id_type=_MESH)

        mine = pltpu.make_async_copy(x_ref, slot(*me), local_sem)
        mine.start()
        first = [copy(0, me, sibling, src=x_ref)]
        first += [copy(1 + j, me, (*chip, c), src=x_ref) for j, chip in enumerate(chips)]
        for cp in first:
            cp.start()
        passed = [copy(4 + j, (*chip, c), sibling) for j, chip in enumerate(chips)]
        for j, chip in enumerate(chips):
            copy(1 + j, (*chip, c), me).wait_recv()
            passed[j].start()
        copy(0, sibling, me).wait_recv()
        for j, chip in enumerate(chips):
            copy(4 + j, (*chip, 1 - c), me).wait_recv()
        for cp in first + passed:
            cp.wait_send()
        mine.wait()

    return pl.pallas_call(
        body, name=name,
        out_shape=jax.ShapeDtypeStruct((N_DEV, rows, width), block.dtype),
        in_specs=[_VM], out_specs=_VM,
        scratch_shapes=[pltpu.SemaphoreType.DMA((7,)), pltpu.SemaphoreType.DMA((7,)), pltpu.SemaphoreType.DMA],
    )(block)


def _pair_swap(arrs):
    def copies(src, dst, send_sems, recv_sems):
        x, y, c = _coords()
        return [pltpu.make_async_remote_copy(src_ref=src[i].at[:, :, _half(1 - c)], dst_ref=dst[i],
                                             send_sem=send_sems.at[i], recv_sem=recv_sems.at[i],
                                             device_id=(x, y, 1 - c), device_id_type=_MESH) for i in range(len(src))]

    shapes = tuple(jax.ShapeDtypeStruct((4, a.shape[1], _HALF), a.dtype) for a in arrs)
    return tuple(arrs), shapes, copies, len(arrs), False


def _chip_exchange(arrs):
    def copies(src, dst, send_sems, recv_sems):
        x, y, c = _coords()
        chips = _other_chips(x, y)
        return [pltpu.make_async_remote_copy(src_ref=src[i].at[2 * chips[k][0] + chips[k][1]], dst_ref=dst[i].at[k],
                                             send_sem=send_sems.at[3 * i + k], recv_sem=recv_sems.at[3 * i + k],
                                             device_id=(*chips[k], c), device_id_type=_MESH)
                for k in range(3) for i in range(len(src))]

    shapes = tuple(jax.ShapeDtypeStruct((3,) + a.shape[1:], a.dtype) for a in arrs)
    return tuple(arrs), shapes, copies, 3 * len(arrs), False


def _gather_stage(slots, to_sibling):
    def copies(buf, _, send_sems, recv_sems):
        x, y, c = _coords()
        chips = _other_chips(x, y)
        out = []
        for k in range(3):
            for i in range(len(buf)):
                frm = 2 * chips[k][0] + chips[k][1] if to_sibling else 2 * x + y
                part = buf[i].at[frm, :, _half(c)]
                out.append(pltpu.make_async_remote_copy(
                    src_ref=part, dst_ref=part, send_sem=send_sems.at[3 * i + k], recv_sem=recv_sems.at[3 * i + k],
                    device_id=(x, y, 1 - c) if to_sibling else (*chips[k], c), device_id_type=_MESH))
        return out

    shapes = tuple(jax.ShapeDtypeStruct(s.shape, s.dtype) for s in slots)
    return tuple(slots), shapes, copies, 3 * len(slots), True


def _pair_join_halves(fulls):
    n = len(fulls)

    def body(*refs):
        buf = refs[n:2 * n]
        send_sems, recv_sems = refs[2 * n:]
        x, y, c = _coords()

        def remote(i, cc):
            part = buf[i].at[:, _half(cc)]
            return pltpu.make_async_remote_copy(src_ref=part, dst_ref=part, send_sem=send_sems.at[i],
                                                recv_sem=recv_sems.at[i], device_id=(x, y, 1 - c), device_id_type=_MESH)

        for i in range(n):
            remote(i, c).start()
        for i in range(n):
            remote(i, c).wait_send()
            remote(i, 1 - c).wait_recv()

    return pl.pallas_call(
        body, name="pair_join_halves",
        out_shape=tuple(jax.ShapeDtypeStruct(a.shape, a.dtype) for a in fulls),
        in_specs=[_ANY] * n, out_specs=tuple([_ANY] * n),
        input_output_aliases={i: i for i in range(n)},
        scratch_shapes=[pltpu.SemaphoreType.DMA((n,)), pltpu.SemaphoreType.DMA((n,))],
    )(*fulls)


_RED_TC = 128
_RED_NT = _HALF // _RED_TC


def _add_pair(ids, g32, recv_a):
    rows = g32.shape[1]

    def body(ids_ref, g_ref, r_ref, o_ref):
        o_ref[...] = (g_ref[...] + r_ref[...]).astype(BF16)

    blk = pl.BlockSpec((1, rows, _RED_TC), lambda j, l, ids: (j, 0, l))
    return pl.pallas_call(
        body, name="add_pair",
        out_shape=jax.ShapeDtypeStruct((4, rows, _HALF), BF16),
        grid_spec=pltpu.PrefetchScalarGridSpec(
            num_scalar_prefetch=1, grid=(4, _RED_NT),
            in_specs=[pl.BlockSpec((1, rows, _RED_TC), lambda j, l, ids: (j, 0, ids[0] * _RED_NT + l)), blk],
            out_specs=blk),
        compiler_params=_cp("parallel", "parallel"),
    )(ids, g32, recv_a)


def _add_chips(ids, g32, recv_a, recv_b):
    rows = g32.shape[1]

    def body(ids_ref, g_ref, a_ref, b_ref, o_ref):
        acc = g_ref[0] + a_ref[0]
        for k in range(3):
            acc = acc + b_ref[k].astype(F32)
        o_ref[...] = acc

    return pl.pallas_call(
        body, name="add_chips",
        out_shape=jax.ShapeDtypeStruct((rows, 2 * _HALF), F32),
        grid_spec=pltpu.PrefetchScalarGridSpec(
            num_scalar_prefetch=1, grid=(_RED_NT,),
            in_specs=[pl.BlockSpec((1, rows, _RED_TC), lambda l, ids: (ids[1], 0, ids[0] * _RED_NT + l)),
                      pl.BlockSpec((1, rows, _RED_TC), lambda l, ids: (ids[1], 0, l)),
                      pl.BlockSpec((3, rows, _RED_TC), lambda l, ids: (0, 0, l))],
            out_specs=pl.BlockSpec((rows, _RED_TC), lambda l, ids: (0, ids[0] * _RED_NT + l))),
        compiler_params=_cp("parallel"),
    )(ids, g32, recv_a, recv_b)


def _sum8(gathered):
    _, rows, width = gathered.shape

    def body(g_ref, o_ref):
        acc = g_ref[0]
        for d in range(1, N_DEV):
            acc = acc + g_ref[d]
        o_ref[...] = acc

    return pl.pallas_call(
        body, name="sum8",
        out_shape=jax.ShapeDtypeStruct((rows, width), F32),
        in_specs=[_VM], out_specs=_VM,
    )(gathered)


def _adamw(w, g, m, v, name):
    rows, cols = w.shape
    budget = (3 << 20) // 2
    tr, tc = rows, cols
    if rows * cols * 4 > budget:
        if rows % 8 == 0:
            tr = next(c for c in (512, 256, 128, 64, 32, 16, 8) if rows % c == 0 and c * cols * 4 <= budget)
        else:
            tc = next(c for c in (512, 256, 128) if cols % c == 0 and rows * c * 4 <= budget)
    c1 = 1.0 - ADAM_B1 ** ADAM_STEP
    c2 = 1.0 - ADAM_B2 ** ADAM_STEP

    def body(w_ref, g_ref, m_ref, v_ref, d_ref, mo_ref, vo_ref):
        gg = g_ref[...]
        mn = ADAM_B1 * m_ref[...] + (1.0 - ADAM_B1) * gg
        vn = ADAM_B2 * v_ref[...] + (1.0 - ADAM_B2) * (gg * gg)
        mo_ref[...] = mn
        vo_ref[...] = vn
        d_ref[...] = -ADAM_LR * ((mn / c1) / (jnp.sqrt(vn / c2) + ADAM_EPS) + ADAM_WD * w_ref[...])

    blk = pl.BlockSpec((tr, tc), lambda i, j: (i, j))
    shp = jax.ShapeDtypeStruct((rows, cols), F32)
    return pl.pallas_call(
        body, name=name, out_shape=(shp, shp, shp), grid=(rows // tr, cols // tc),
        in_specs=[blk] * 4, out_specs=(blk, blk, blk),
        compiler_params=_cp("parallel", "parallel"),
    )(w, g, m, v)


def _rows128(a):
    return a.reshape(-1, 128)


def _pack_small(norm_pre, conv_b, ssd_norm, gate_bias, norm_post, dt_bias, a_log, d_skip, fgate_bias):
    tiny = jnp.concatenate([dt_bias.reshape(-1), a_log.reshape(-1), d_skip.reshape(-1), fgate_bias.reshape(-1),
                            jnp.zeros((16,), F32)])
    return jnp.concatenate([_rows128(norm_pre), _rows128(conv_b), _rows128(ssd_norm), _rows128(gate_bias),
                            _rows128(norm_post), tiny.reshape(1, 128)], axis=0)


_SMALL_ROWS = 73
_SMALL_PAD = 80


def _unpack_small(p):
    tiny = p[72]
    return dict(norm_pre=p[0:8].reshape(1, 1024), conv_b=p[8:32].reshape(1, 3072), ssd_norm=p[32:48].reshape(1, 2048),
                gate_bias=p[48:64].reshape(1, 2048), norm_post=p[64:72].reshape(1, 1024),
                dt_bias=tiny[0:32].reshape(1, 32), a_log=tiny[32:64].reshape(1, 32),
                d_skip=tiny[64:96].reshape(1, 32), fgate_bias=tiny[96:112].reshape(1, 16))


def _pad_rows(a, rows):
    return jnp.concatenate([a, jnp.zeros((rows - a.shape[0], a.shape[1]), a.dtype)], axis=0)


def kernel(x, meta_tokens, norm_pre, w_in, conv_w, conv_b, dt_bias, a_log, d_skip, ssd_norm, fgate_bias, gate_bias, w_proj_ssd, w_proj_att, w_out, norm_post, loss_target, m_meta_tokens, m_norm_pre, m_w_in, m_conv_w, m_conv_b, m_dt_bias, m_a_log, m_d_skip, m_ssd_norm, m_fgate_bias, m_gate_bias, m_w_proj_ssd, m_w_proj_att, m_w_out, m_norm_post, v_meta_tokens, v_norm_pre, v_w_in, v_conv_w, v_conv_b, v_dt_bias, v_a_log, v_d_skip, v_ssd_norm, v_fgate_bias, v_gate_bias, v_w_proj_ssd, v_w_proj_att, v_w_out, v_norm_post):
    cx, cy, cc = _coords()
    chip = 2 * cx + cy
    ids = jnp.stack([cc, chip]).astype(jnp.int32)
    seq = x.shape[1]

    w_in_sh = jnp.transpose(w_in[0]).astype(BF16)
    w_pr_sh = jnp.concatenate([w_proj_ssd[0], w_proj_att[0], w_out[0]], axis=0).astype(BF16)

    def own_slot(sh):
        return lax.dynamic_update_slice(lax.empty((4,) + sh.shape, sh.dtype), sh[None], (chip, 0, 0))

    (g_in,) = _gather_shards([own_slot(w_in_sh)])
    w_main, w_small = _to_aligned_rows(g_in)
    sm_sh = jnp.concatenate([_rows128(meta_tokens), _rows128(conv_w[0])], axis=0)
    sm_all = _allgather8(sm_sh, "gather_small_weights")[0::2]
    meta_full = jnp.transpose(sm_all[:, 0:32].reshape(4, N_META, 256), (1, 0, 2)).reshape(N_META, D_MODEL)
    conv_w_full = jnp.transpose(sm_all[:, 32:56].reshape(4, CONV_K, 768), (1, 0, 2)).reshape(CONV_K, CONV_DIM)

    h = jnp.concatenate([jnp.zeros((PADF, D_MODEL), F32), meta_full, x[0]], axis=0)
    bias_row = jnp.concatenate([dt_bias[0], fgate_bias[0], jnp.zeros((N_SMALL - H_SSD - H_ATT,), F32)]).reshape(1, N_SMALL)
    a_neg = -jnp.exp(a_log[0])
    a_row = jnp.concatenate([a_neg, jnp.zeros((N_SMALL - H_SSD,), F32)]).reshape(1, N_SMALL)
    dsk_row = jnp.repeat(d_skip[0], 64).reshape(1, D_SSD)
    r = _local_step(h, loss_target[0], w_main, w_small, own_slot(w_pr_sh), ids, norm_pre, conv_w_full, conv_b,
                    bias_row, a_row, dsk_row, ssd_norm, gate_bias, norm_post)

    tm = _mm_tiles(h.shape[0])
    n_row_tiles = h.shape[0] // tm
    d_w_main = _matmul_cat_tn(r["dproj"], r["u"], "inproj_dw", tm)
    d_w_small = _matmul(r["dsmall"], r["u"], "tn", F32, "inproj_small_dw", N_SMALL, D_MODEL, tm)
    g32_in = _from_aligned_rows(d_w_main, d_w_small)
    first = max(n_row_tiles // 3, 1)
    du_first, ra_in = _matmul_cat_nn(r["dproj"], w_main, "inproj_dx_swap", tm, rows=(0, first),
                                     exchange=_pair_swap([g32_in]))
    pb_in = _add_pair(ids, g32_in, ra_in)
    du_a, rb_in = _matmul_cat_nn(r["dproj"], w_main, "inproj_dx_exchange", tm,
                                 rows=(first, n_row_tiles - first), fill=du_first,
                                 exchange=_chip_exchange([pb_in]))
    du_b = _matmul(r["dsmall"], w_small, "nn", F32, "inproj_small_dx", tm, D_MODEL, N_SMALL)
    dh, d_norm_pre = _norm1_bwd(du_a, du_b, h, norm_pre, r["dy_out"])
    grad_x = dh[PADF + N_META:].reshape(1, seq, D_MODEL)
    half_in = _add_chips(ids, g32_in, ra_in, rb_in)
    gw_in, gw_pr = _pair_join_halves([half_in, r["half_pr"]])

    tiny = r["d_bias_row"][0]
    part_small = _pack_small(d_norm_pre, r["d_conv_b"], r["d_ssd_norm"], r["d_gate_bias"], r["d_norm_post"],
                             tiny[0:H_SSD], r["d_a"][0, 0:H_SSD] * a_neg, r["d_dsk"].reshape(H_SSD, 64).sum(axis=1),
                             tiny[H_SSD:H_SSD + H_ATT])
    part = jnp.concatenate([_pad_rows(part_small, _SMALL_PAD), _rows128(r["d_conv_w"]),
                            _rows128(dh[PADF:PADF + N_META]), r["loss_blk"]], axis=0)
    tot = _sum8(_allgather8(part, "gather_small_grads"))
    loss = tot[_SMALL_PAD + 96 + 128, 0]
    g_small = tot[0:_SMALL_PAD]
    g_conv_w = lax.dynamic_slice_in_dim(tot[_SMALL_PAD:_SMALL_PAD + 96].reshape(CONV_K, CONV_DIM), chip * 768, 768, axis=1)
    g_meta = lax.dynamic_slice_in_dim(tot[_SMALL_PAD + 96:_SMALL_PAD + 224].reshape(N_META, D_MODEL), chip * 256, 256, axis=1)

    upd = {}
    upd["w_in"] = tuple(jnp.transpose(a) for a in (gw_in,) + _adamw(
        jnp.transpose(w_in[0]), gw_in, jnp.transpose(m_w_in[0]), jnp.transpose(v_w_in[0]), "adamw_w_in"))
    w_pr32 = jnp.concatenate([w_proj_ssd[0], w_proj_att[0], w_out[0]], axis=0)
    m_pr = jnp.concatenate([m_w_proj_ssd[0], m_w_proj_att[0], m_w_out[0]], axis=0)
    v_pr = jnp.concatenate([v_w_proj_ssd[0], v_w_proj_att[0], v_w_out[0]], axis=0)
    pr = (gw_pr,) + _adamw(w_pr32, gw_pr, m_pr, v_pr, "adamw_w_proj")
    upd["w_proj_ssd"] = tuple(a[0:512] for a in pr)
    upd["w_proj_att"] = tuple(a[512:768] for a in pr)
    upd["w_out"] = tuple(a[768:1024] for a in pr)
    upd["conv_w"] = (g_conv_w,) + _adamw(conv_w[0], g_conv_w, m_conv_w[0], v_conv_w[0], "adamw_conv_w")
    upd["meta_tokens"] = (g_meta,) + _adamw(meta_tokens, g_meta, m_meta_tokens, v_meta_tokens, "adamw_meta")
    pk = lambda np_, cb, sn, gb, npo, dtb, al, ds, fg: _pad_rows(_pack_small(np_, cb, sn, gb, npo, dtb, al, ds, fg), _SMALL_PAD)
    w_sm = pk(norm_pre, conv_b, ssd_norm, gate_bias, norm_post, dt_bias, a_log, d_skip, fgate_bias)
    m_sm = pk(m_norm_pre, m_conv_b, m_ssd_norm, m_gate_bias, m_norm_post, m_dt_bias, m_a_log, m_d_skip, m_fgate_bias)
    v_sm = pk(v_norm_pre, v_conv_b, v_ssd_norm, v_gate_bias, v_norm_post, v_dt_bias, v_a_log, v_d_skip, v_fgate_bias)
    sm = [_unpack_small(a) for a in (g_small,) + _adamw(w_sm, g_small, m_sm, v_sm, "adamw_small")]
    for name in ("norm_pre", "conv_b", "dt_bias", "a_log", "d_skip", "ssd_norm", "fgate_bias", "gate_bias", "norm_post"):
        upd[name] = tuple(s[name] for s in sm)
    lead = ("w_in", "conv_w", "w_proj_ssd", "w_proj_att", "w_out")
    order = ("meta_tokens", "norm_pre", "w_in", "conv_w", "conv_b", "dt_bias", "a_log", "d_skip", "ssd_norm",
             "fgate_bias", "gate_bias", "w_proj_ssd", "w_proj_att", "w_out", "norm_post")
    outs = [loss, grad_x]
    for part_i in range(4):
        for name in order:
            a = upd[name][part_i]
            outs.append(a[None] if name in lead else a)
    return tuple(outs)
```

```python
import functools
import math

import jax
import jax.numpy as jnp
from jax import lax
from jax.experimental import pallas as pl
from jax.experimental.pallas import tpu as pltpu

F32 = jnp.float32
BF16 = jnp.bfloat16
HIGHEST = lax.Precision.HIGHEST

D_MODEL = 1024
N_META = 16
CHUNK = 128
PADF = CHUNK - N_META
D_SSD = 2048
H_SSD = 32
G_SSD = 4
N_STATE = 128
CONV_K = 4
CONV_DIM = D_SSD + 2 * G_SSD * N_STATE
H_ATT = 16
D_ATT = 1024
EPS = 1e-6
N_COLS = 11312

C_Z, C_XBC, C_ZA, C_Q, C_K, C_V, C_G = 0, 2048, 5120, 6144, 7168, 8192, 9216
N_MAIN = 11264
N_SMALL = 128
O_Z, O_XBC, O_DT, O_ZA, O_Q, O_K, O_V, O_F, O_G = (
    (0, 2048), (2048, 3072), (5120, 32), (5152, 1024), (6176, 1024), (7200, 1024),
    (8224, 1024), (9248, 16), (9264, 2048))

ADAM_LR, ADAM_B1, ADAM_B2, ADAM_EPS, ADAM_WD, ADAM_STEP = 0.001, 0.9, 0.999, 1e-08, 0.01, 10

VMEM_LIMIT = 56 * 1024 * 1024


def _cp(*sem):
    return pltpu.CompilerParams(dimension_semantics=sem, vmem_limit_bytes=VMEM_LIMIT)


def _tile(n, prefs):
    for p in prefs:
        if n % p == 0:
            return p
    raise ValueError(f"no tile for {n} in {prefs}")


def _iota(shape, dim):
    return lax.broadcasted_iota(jnp.int32, shape, dim)


def _sigmoid(x):
    return 1.0 / (1.0 + jnp.exp(-x))


def _softplus_tail(x):
    return jnp.log(1.0 + jnp.exp(-jnp.abs(x)))


_NN = (((1,), (0,)), ((), ()))
_NT = (((1,), (1,)), ((), ()))
_TN = (((0,), (0,)), ((), ()))


def _dot(a, b, dims=_NN):
    return lax.dot_general(a, b, dims, preferred_element_type=F32)


def _dot_exact(a, b, dims=_NN):
    return lax.dot_general(a, b, dims, precision=HIGHEST, preferred_element_type=F32)


def _hosted_call(body, *, name, grid, in_specs, out_specs, out_shape, scratch_shapes, operands, semantics,
                 exchange=None, aliases=None):
    aliases = dict(aliases or {})
    if exchange is None:
        return pl.pallas_call(body, name=name, out_shape=out_shape, grid=grid, in_specs=in_specs,
                              out_specs=out_specs, scratch_shapes=scratch_shapes, input_output_aliases=aliases,
                              compiler_params=_cp(*semantics))(*operands)
    arrays, shapes, copies, n_sems, in_place = exchange
    n_in, n_out, n_ex = len(operands), len(out_shape), len(arrays)

    def hosted(*refs):
        ex_in = refs[n_in:n_in + n_ex]
        ex_out = refs[n_in + n_ex + n_out:n_in + n_ex + n_out + n_ex]
        own = refs[:n_in] + refs[n_in + n_ex:n_in + n_ex + n_out] + refs[n_in + 2 * n_ex + n_out:-2]
        first = functools.reduce(lambda p, q: p & q, [pl.program_id(d) == 0 for d in range(len(grid))])
        last = functools.reduce(lambda p, q: p & q, [pl.program_id(d) == grid[d] - 1 for d in range(len(grid))])

        def descriptors():
            return copies(ex_out if in_place else ex_in, ex_out, refs[-2], refs[-1])

        @pl.when(first)
        def _():
            for cp in descriptors():
                cp.start()

        body(*own)

        @pl.when(last)
        def _():
            for cp in descriptors():
                cp.wait()

    return pl.pallas_call(
        hosted, name=name,
        out_shape=tuple(out_shape) + tuple(shapes),
        grid=grid,
        in_specs=list(in_specs) + [_ANY] * n_ex,
        out_specs=tuple(out_specs) + (_ANY,) * n_ex,
        input_output_aliases={**aliases, **({n_in + e: n_out + e for e in range(n_ex)} if in_place else {})},
        scratch_shapes=list(scratch_shapes) + [pltpu.SemaphoreType.DMA((n_sems,)), pltpu.SemaphoreType.DMA((n_sems,))],
        compiler_params=_cp(*(("arbitrary",) * len(grid))),
    )(*operands, *arrays)


def _matmul(a, b, mode, out_dtype, name, tm, tn, tk, exchange=None):
    if mode == "tn":
        kdim, m = a.shape
    else:
        m, kdim = a.shape
    n = b.shape[0] if mode == "nt" else b.shape[1]
    nk = kdim // tk
    dims = {"nn": _NN, "nt": _NT, "tn": _TN}[mode]
    a_spec = (pl.BlockSpec((tk, tm), lambda i, j, k: (k, i)) if mode == "tn"
              else pl.BlockSpec((tm, tk), lambda i, j, k: (i, k)))
    b_spec = (pl.BlockSpec((tn, tk), lambda i, j, k: (j, k)) if mode == "nt"
              else pl.BlockSpec((tk, tn), lambda i, j, k: (k, j)))

    def body(a_ref, b_ref, o_ref, acc_ref):
        k = pl.program_id(2)
        p = _dot(a_ref[...].astype(BF16), b_ref[...].astype(BF16), dims)
        if nk == 1:
            o_ref[...] = p.astype(out_dtype)
        else:
            @pl.when(k == 0)
            def _():
                acc_ref[...] = p

            @pl.when(k > 0)
            def _():
                acc_ref[...] += p

            @pl.when(k == nk - 1)
            def _():
                o_ref[...] = acc_ref[...].astype(out_dtype)

    out = _hosted_call(
        body, name=name,
        out_shape=(jax.ShapeDtypeStruct((m, n), out_dtype),),
        grid=(m // tm, n // tn, nk),
        in_specs=[a_spec, b_spec],
        out_specs=(pl.BlockSpec((tm, tn), lambda i, j, k: (i, j)),),
        scratch_shapes=[pltpu.VMEM((tm, tn), F32)],
        operands=(a, b), semantics=("parallel", "parallel", "arbitrary"), exchange=exchange)
    return out[0] if exchange is None else out


_CAT_BLK = 1024


def _piece_ranges(pieces):
    out, off = [], 0
    for p in pieces:
        nb = p.shape[1] // _CAT_BLK
        out.append((off, nb))
        off += nb
    return out, off


def _matmul_cat_nn(pieces, b, name, tm, rows=None, fill=None, exchange=None):
    t = pieces[0].shape[0]
    n = b.shape[1]
    ranges, nk = _piece_ranges(pieces)
    first, ni = rows if rows is not None else (0, t // tm)
    n_in = len(pieces) + 1 + (fill is not None)

    def body(*refs):
        a_refs, b_ref, o_ref, acc_ref = refs[:len(pieces)], refs[len(pieces)], refs[n_in], refs[n_in + 1]
        k = pl.program_id(1)

        @pl.when(k == 0)
        def _():
            acc_ref[...] = jnp.zeros_like(acc_ref)

        for a_ref, (off, nb) in zip(a_refs, ranges):
            @pl.when((k >= off) & (k < off + nb))
            def _(a_ref=a_ref):
                acc_ref[...] += _dot(a_ref[...], b_ref[...])

        @pl.when(k == nk - 1)
        def _():
            o_ref[...] = acc_ref[...]

    def a_spec(off, nb):
        return pl.BlockSpec((tm, _CAT_BLK), lambda i, k: (first + i, jnp.clip(k - off, 0, nb - 1)))

    in_specs = [a_spec(off, nb) for off, nb in ranges] + [pl.BlockSpec((_CAT_BLK, n), lambda i, k: (k, 0))]
    operands = list(pieces) + [b]
    if fill is not None:
        in_specs.append(_ANY)
        operands.append(fill)
    out = _hosted_call(
        body, name=name,
        out_shape=(jax.ShapeDtypeStruct((t, n), F32),),
        grid=(ni, nk),
        in_specs=in_specs,
        out_specs=(pl.BlockSpec((tm, n), lambda i, k: (first + i, 0)),),
        scratch_shapes=[pltpu.VMEM((tm, n), F32)],
        operands=operands, semantics=("parallel", "arbitrary"), exchange=exchange,
        aliases={len(pieces) + 1: 0} if fill is not None else None)
    return out if exchange is not None else out[0]


def _matmul_cat_tn(pieces, b, name, tk):
    t = pieces[0].shape[0]
    n = b.shape[1]
    ranges, nm = _piece_ranges(pieces)
    nk = t // tk

    def body(*refs):
        a_refs, b_ref, o_ref, acc_ref = refs[:len(pieces)], refs[-3], refs[-2], refs[-1]
        m = pl.program_id(0)
        k = pl.program_id(1)

        @pl.when(k == 0)
        def _():
            acc_ref[...] = jnp.zeros_like(acc_ref)

        for a_ref, (off, nb) in zip(a_refs, ranges):
            @pl.when((m >= off) & (m < off + nb))
            def _(a_ref=a_ref):
                acc_ref[...] += _dot(a_ref[...], b_ref[...], _TN)

        @pl.when(k == nk - 1)
        def _():
            o_ref[...] = acc_ref[...]

    def a_spec(off, nb):
        def index(m, k):
            mine = (m >= off) & (m < off + nb)
            return jnp.where(mine, k, 0), jnp.clip(m - off, 0, nb - 1)
        return pl.BlockSpec((tk, _CAT_BLK), index)

    return pl.pallas_call(
        body, name=name,
        out_shape=jax.ShapeDtypeStruct((nm * _CAT_BLK, n), F32),
        grid=(nm, nk),
        in_specs=[a_spec(off, nb) for off, nb in ranges] + [pl.BlockSpec((tk, n), lambda m, k: (k, 0))],
        out_specs=pl.BlockSpec((_CAT_BLK, n), lambda m, k: (m, 0)),
        scratch_shapes=[pltpu.VMEM((_CAT_BLK, n), F32)],
        compiler_params=_cp("parallel", "arbitrary"),
    )(*pieces, b)


def _row_tile(t):
    return _tile(t, (352, 128))


def _row_tile_wide(t):
    return _tile(t, (176, 128))


def _norm1_fwd(h, g):
    t = h.shape[0]
    tm = _row_tile(t)

    def body(h_ref, g_ref, u_ref):
        x = h_ref[...]
        r = lax.rsqrt(jnp.mean(x * x, axis=-1, keepdims=True) + EPS)
        u_ref[...] = (x * r * g_ref[...]).astype(BF16)

    return pl.pallas_call(
        body, name="norm1_fwd",
        out_shape=jax.ShapeDtypeStruct((t, D_MODEL), BF16),
        grid=(t // tm,),
        in_specs=[pl.BlockSpec((tm, D_MODEL), lambda i: (i, 0)),
                  pl.BlockSpec((1, D_MODEL), lambda i: (0, 0))],
        out_specs=pl.BlockSpec((tm, D_MODEL), lambda i: (i, 0)),
        compiler_params=_cp("parallel"),
    )(h, g)


def _norm1_bwd(du_a, du_b, h, g, dy):
    t = h.shape[0]
    tm = _row_tile(t)

    def body(a_ref, b_ref, h_ref, g_ref, dy_ref, dh_ref, dg_ref):
        i = pl.program_id(0)
        x = h_ref[...]
        du = a_ref[...] + b_ref[...]
        r = lax.rsqrt(jnp.mean(x * x, axis=-1, keepdims=True) + EPS)
        gdu = du * g_ref[...]
        dh_ref[...] = dy_ref[...] + r * (gdu - x * (r * r) * jnp.mean(gdu * x, axis=-1, keepdims=True))
        part = jnp.sum(du * x * r, axis=0, keepdims=True)

        @pl.when(i == 0)
        def _():
            dg_ref[...] = part

        @pl.when(i > 0)
        def _():
            dg_ref[...] += part

    row = pl.BlockSpec((tm, D_MODEL), lambda i: (i, 0))
    vec = pl.BlockSpec((1, D_MODEL), lambda i: (0, 0))
    return pl.pallas_call(
        body, name="norm1_bwd",
        out_shape=(jax.ShapeDtypeStruct((t, D_MODEL), F32), jax.ShapeDtypeStruct((1, D_MODEL), F32)),
        grid=(t // tm,),
        in_specs=[row, row, row, vec, row],
        out_specs=(row, vec),
        compiler_params=_cp("arbitrary"),
    )(du_a, du_b, h, g, dy)


def _small_fwd(small, bias_row):
    t = small.shape[0]

    def body(s_ref, b_ref, o_ref, carry_ref):
        c = pl.program_id(0)

        @pl.when(c == 0)
        def _():
            carry_ref[...] = jnp.zeros_like(carry_ref)

        x = s_ref[...] + b_ref[...]
        r0 = _iota((CHUNK, CHUNK), 0)
        r1 = _iota((CHUNK, CHUNK), 1)
        valid = (c * CHUNK + r0) >= PADF
        tail = _softplus_tail(x)
        dt = jnp.where(valid & (r1 < H_SSD), jnp.maximum(x, 0.0) + tail, 0.0)
        lf = jnp.where(valid & (r1 >= H_SSD) & (r1 < H_SSD + H_ATT), jnp.minimum(x, 0.0) - tail, 0.0)
        tri = (r0 >= r1).astype(F32)
        cs = _dot_exact(tri, lf) + carry_ref[...]
        carry_ref[...] = cs[CHUNK - 1:CHUNK, :]
        o_ref[...] = dt + cs

    return pl.pallas_call(
        body, name="small_fwd",
        out_shape=jax.ShapeDtypeStruct((t, N_SMALL), F32),
        grid=(t // CHUNK,),
        in_specs=[pl.BlockSpec((CHUNK, N_SMALL), lambda c: (c, 0)),
                  pl.BlockSpec((1, N_SMALL), lambda c: (0, 0))],
        out_specs=pl.BlockSpec((CHUNK, N_SMALL), lambda c: (c, 0)),
        scratch_shapes=[pltpu.VMEM((1, N_SMALL), F32)],
        compiler_params=_cp("arbitrary"),
    )(small, bias_row)


def _small_bwd(dsm, small, bias_row):
    t = small.shape[0]
    nc = t // CHUNK

    def body(d_ref, s_ref, b_ref, o_ref, db_ref, carry_ref):
        step = pl.program_id(0)
        c = nc - 1 - step

        @pl.when(step == 0)
        def _():
            carry_ref[...] = jnp.zeros_like(carry_ref)
            db_ref[...] = jnp.zeros_like(db_ref)

        x = s_ref[...] + b_ref[...]
        d = d_ref[...]
        r0 = _iota((CHUNK, CHUNK), 0)
        r1 = _iota((CHUNK, CHUNK), 1)
        valid = (c * CHUNK + r0) >= PADF
        is_dt = r1 < H_SSD
        is_f = (r1 >= H_SSD) & (r1 < H_SSD + H_ATT)
        triu = (r1 >= r0).astype(F32)
        dc = jnp.where(is_f, d, 0.0)
        dlf = _dot_exact(triu, dc) + carry_ref[...]
        carry_ref[...] = dlf[0:1, :]
        sg = _sigmoid(x)
        out = jnp.where(valid & is_dt, d * sg, 0.0) + jnp.where(valid & is_f, dlf * (1.0 - sg), 0.0)
        o_ref[...] = out.astype(BF16)
        db_ref[...] += jnp.sum(out, axis=0, keepdims=True)

    blk = pl.BlockSpec((CHUNK, N_SMALL), lambda s: (nc - 1 - s, 0))
    vec = pl.BlockSpec((1, N_SMALL), lambda s: (0, 0))
    return pl.pallas_call(
        body, name="small_bwd",
        out_shape=(jax.ShapeDtypeStruct((t, N_SMALL), BF16), jax.ShapeDtypeStruct((1, N_SMALL), F32)),
        grid=(nc,),
        in_specs=[blk, blk, vec],
        out_specs=(blk, vec),
        scratch_shapes=[pltpu.VMEM((1, N_SMALL), F32)],
        compiler_params=_cp("arbitrary"),
    )(dsm, small, bias_row)


_CONV_TC = 1024
_XBC_BLK = C_XBC // _CONV_TC


def _shift_down(cur, prev8, j):
    rc = pltpu.roll(cur, j, 0)
    rid = _iota(prev8.shape, 0)
    top = jnp.where(rid < j, pltpu.roll(prev8, j, 0), rc[0:8, :])
    return top if cur.shape[0] == 8 else jnp.concatenate([top, rc[8:, :]], axis=0)


def _shift_up(cur, next8, j):
    n = cur.shape[0]
    ru = pltpu.roll(cur, n - j, 0)
    rid = _iota(next8.shape, 0)
    bot = jnp.where(rid >= 8 - j, pltpu.roll(next8, 8 - j, 0), ru[n - 8:, :])
    return jnp.concatenate([ru[:n - 8, :], bot], axis=0)


def _conv_taps(cur, prev, w, b):
    taps = [cur] + [_shift_down(cur, prev, j) for j in (1, 2, 3)]
    acc = b + taps[0] * w[3:4, :]
    for j in (1, 2, 3):
        acc = acc + taps[j] * w[3 - j:4 - j, :]
    return acc, taps


def _conv_pre(x_ref, p_ref, w_ref, b_ref, i):
    return _conv_taps(x_ref[...], jnp.where(i > 0, p_ref[...], 0.0), w_ref[...], b_ref[...])


def _dsilu(d, acc):
    sg = _sigmoid(acc)
    return d * sg * (1.0 + acc * (1.0 - sg))


def _conv_fwd(proj, conv_w, conv_b):
    t = proj.shape[0]
    tr = _row_tile(t)

    def body(x_ref, p_ref, w_ref, b_ref, o_ref):
        i = pl.program_id(0)
        acc, _ = _conv_pre(x_ref, p_ref, w_ref, b_ref, i)
        valid = (i * tr + _iota(acc.shape, 0)) >= PADF
        o_ref[...] = jnp.where(valid, acc * _sigmoid(acc), 0.0)

    return pl.pallas_call(
        body, name="conv_fwd",
        out_shape=jax.ShapeDtypeStruct((t, CONV_DIM), F32),
        grid=(t // tr, CONV_DIM // _CONV_TC),
        in_specs=[pl.BlockSpec((tr, _CONV_TC), lambda i, j: (i, _XBC_BLK + j)),
                  pl.BlockSpec((8, _CONV_TC), lambda i, j: (jnp.maximum(i * (tr // 8) - 1, 0), _XBC_BLK + j)),
                  pl.BlockSpec((CONV_K, _CONV_TC), lambda i, j: (0, j)),
                  pl.BlockSpec((1, _CONV_TC), lambda i, j: (0, j))],
        out_specs=pl.BlockSpec((tr, _CONV_TC), lambda i, j: (i, j)),
        compiler_params=_cp("parallel", "parallel"),
    )(proj, proj, conv_w, conv_b)


def _conv_bwd(dxbc, proj, conv_w, conv_b):
    t = proj.shape[0]
    tr = _row_tile(t)
    n_tiles = t // tr
    last8 = t // 8 - 1

    def body(d_ref, dn_ref, x_ref, p_ref, xn_ref, w_ref, b_ref, dx_ref, dw_ref, db_ref):
        i = pl.program_id(1)
        w = w_ref[...]
        b = b_ref[...]
        cur = x_ref[...]
        acc, taps = _conv_taps(cur, jnp.where(i > 0, p_ref[...], 0.0), w, b)
        valid = (i * tr + _iota(acc.shape, 0)) >= PADF
        da = jnp.where(valid, _dsilu(d_ref[...], acc), 0.0)
        acc_n, _ = _conv_taps(xn_ref[...], cur[tr - 8:, :], w, b)
        da_n = jnp.where(i < n_tiles - 1, _dsilu(dn_ref[...], acc_n), 0.0)
        dx = da * w[3:4, :]
        for j in (1, 2, 3):
            dx = dx + _shift_up(da, da_n, j) * w[3 - j:4 - j, :]
        dx_ref[...] = dx.astype(BF16)
        dw = jnp.concatenate([jnp.sum(da * taps[3 - k], axis=0, keepdims=True) for k in range(CONV_K)], axis=0)
        db = jnp.sum(da, axis=0, keepdims=True)

        @pl.when(i == 0)
        def _():
            dw_ref[...] = dw
            db_ref[...] = db

        @pl.when(i > 0)
        def _():
            dw_ref[...] += dw
            db_ref[...] += db

    nxt8 = lambda i: jnp.minimum((i + 1) * (tr // 8), last8)
    return pl.pallas_call(
        body, name="conv_bwd",
        out_shape=(jax.ShapeDtypeStruct((t, CONV_DIM), BF16),
                   jax.ShapeDtypeStruct((CONV_K, CONV_DIM), F32),
                   jax.ShapeDtypeStruct((1, CONV_DIM), F32)),
        grid=(CONV_DIM // _CONV_TC, n_tiles),
        in_specs=[pl.BlockSpec((tr, _CONV_TC), lambda j, i: (i, j)),
                  pl.BlockSpec((8, _CONV_TC), lambda j, i: (nxt8(i), j)),
                  pl.BlockSpec((tr, _CONV_TC), lambda j, i: (i, _XBC_BLK + j)),
                  pl.BlockSpec((8, _CONV_TC), lambda j, i: (jnp.maximum(i * (tr // 8) - 1, 0), _XBC_BLK + j)),
                  pl.BlockSpec((8, _CONV_TC), lambda j, i: (nxt8(i), _XBC_BLK + j)),
                  pl.BlockSpec((CONV_K, _CONV_TC), lambda j, i: (0, j)),
                  pl.BlockSpec((1, _CONV_TC), lambda j, i: (0, j))],
        out_specs=(pl.BlockSpec((tr, _CONV_TC), lambda j, i: (i, j)),
                   pl.BlockSpec((CONV_K, _CONV_TC), lambda j, i: (0, j)),
                   pl.BlockSpec((1, _CONV_TC), lambda j, i: (0, j))),
        compiler_params=_cp("parallel", "arbitrary"),
    )(dxbc, dxbc, proj, proj, proj, conv_w, conv_b)


_GW = D_SSD // G_SSD


def _ssd_prelude(dt_ref, a_ref, e_scr, es_scr, dte_scr):
    r0 = _iota((CHUNK, CHUNK), 0)
    r1 = _iota((CHUNK, CHUNK), 1)
    dt = jnp.where(r1 < H_SSD, dt_ref[...], 0.0)
    adt = dt * a_ref[...]
    acs = _dot_exact((r0 >= r1).astype(F32), adt)
    acs_t = acs.T
    alast = acs[CHUNK - 1:CHUNK, :]
    exp_a = jnp.exp(acs)
    dec_s = jnp.exp(alast - acs)
    lo = r1 < 64
    for j in range(H_SSD // 2):
        sl = slice(CHUNK * j, CHUNK * (j + 1))
        e_scr[:, sl] = jnp.where(lo, exp_a[:, 2 * j:2 * j + 1], exp_a[:, 2 * j + 1:2 * j + 2])
        es_scr[:, sl] = jnp.where(lo, dec_s[:, 2 * j:2 * j + 1], dec_s[:, 2 * j + 1:2 * j + 2])
        dte_scr[:, sl] = jnp.where(lo, dt[:, 2 * j:2 * j + 1], dt[:, 2 * j + 1:2 * j + 2])
    return dt, acs, acs_t, r0, r1, lo


def _chunk_decay_rows(acs_t, g):
    cd_t = jnp.exp(acs_t[:, CHUNK - 1:CHUNK])
    return jnp.concatenate(
        [jnp.broadcast_to(cd_t[8 * g + hh:8 * g + hh + 1, :], (64, N_STATE)) for hh in range(8)], axis=0)


def _ssd_fwd(xbc, dtlf, a_row, dsk_row):
    t = xbc.shape[0]
    nc = t // CHUNK

    def body(xs_ref, b_ref, c_ref, dt_ref, a_ref, dsk_ref, y_ref, hin_ref, h_scr, e_scr, es_scr, dte_scr):
        c = pl.program_id(0)

        @pl.when(c == 0)
        def _():
            h_scr[...] = jnp.zeros_like(h_scr)

        dt, acs, acs_t, r0, r1, lo = _ssd_prelude(dt_ref, a_ref, e_scr, es_scr, dte_scr)
        causal = r0 >= r1
        for g in range(G_SSD):
            gs = slice(_GW * g, _GW * (g + 1))
            bg = b_ref[:, N_STATE * g:N_STATE * (g + 1)].astype(BF16)
            cg = c_ref[:, N_STATE * g:N_STATE * (g + 1)].astype(BF16)
            cb = _dot(cg, bg, _NT)
            hg = h_scr[gs, :]
            hin_ref[0, gs, :] = hg
            xg = xs_ref[:, gs] * dte_scr[:, gs]
            yoff = _dot(cg, hg.astype(BF16), _NT) * e_scr[:, gs]
            st = _dot((xg * es_scr[:, gs]).astype(BF16), bg, _TN)
            h_scr[gs, :] = hg * _chunk_decay_rows(acs_t, g) + st
            for jj in range(4):
                j = 4 * g + jj
                sl = slice(CHUNK * j, CHUNK * (j + 1))
                xp = xg[:, CHUNK * jj:CHUNK * (jj + 1)]
                acc = yoff[:, CHUNK * jj:CHUNK * (jj + 1)] + dsk_ref[:, sl] * xs_ref[:, sl]
                for hh in range(2):
                    h = 2 * j + hh
                    seg = acs[:, h:h + 1] - acs_t[h:h + 1, :]
                    lm = jnp.exp(jnp.where(causal, seg, -1e30))
                    m = (cb * lm).astype(BF16)
                    xh = jnp.where(lo if hh == 0 else ~lo, xp, 0.0).astype(BF16)
                    acc = acc + _dot(m, xh)
                y_ref[:, sl] = acc

    return pl.pallas_call(
        body, name="ssd_fwd",
        out_shape=(jax.ShapeDtypeStruct((t, D_SSD), F32), jax.ShapeDtypeStruct((nc, D_SSD, N_STATE), F32)),
        grid=(nc,),
        in_specs=[pl.BlockSpec((CHUNK, D_SSD), lambda c: (c, 0)),
                  pl.BlockSpec((CHUNK, _GW), lambda c: (c, 4)),
                  pl.BlockSpec((CHUNK, _GW), lambda c: (c, 5)),
                  pl.BlockSpec((CHUNK, N_SMALL), lambda c: (c, 0)),
                  pl.BlockSpec((1, N_SMALL), lambda c: (0, 0)),
                  pl.BlockSpec((1, D_SSD), lambda c: (0, 0))],
        out_specs=(pl.BlockSpec((CHUNK, D_SSD), lambda c: (c, 0)),
                   pl.BlockSpec((1, D_SSD, N_STATE), lambda c: (c, 0, 0))),
        scratch_shapes=[pltpu.VMEM((D_SSD, N_STATE), F32)] + [pltpu.VMEM((CHUNK, D_SSD), F32)] * 3,
        compiler_params=_cp("arbitrary"),
    )(xbc, xbc, xbc, dtlf, a_row, dsk_row)


def _ssd_bwd(xbc, dtlf, a_row, dsk_row, hin, dy):
    t = xbc.shape[0]
    nc = t // CHUNK

    def body(xs_ref, b_ref, c_ref, dt_ref, a_ref, dsk_ref, hin_ref, dy_ref,
             dxbc_ref, ddt_ref, da_ref, ddsk_ref, dh_scr, e_scr, es_scr, dte_scr, dx_scr, whi_scr, wlo_scr):
        step = pl.program_id(0)

        @pl.when(step == 0)
        def _():
            dh_scr[...] = jnp.zeros_like(dh_scr)
            da_ref[...] = jnp.zeros_like(da_ref)
            ddsk_ref[...] = jnp.zeros_like(ddsk_ref)

        dt, acs, acs_t, r0, r1, lo = _ssd_prelude(dt_ref, a_ref, e_scr, es_scr, dte_scr)
        causal = r0 >= r1
        lane_row = _iota((1, CHUNK), 1)
        dacs = jnp.zeros((CHUNK, CHUNK), F32)
        dacs_t = jnp.zeros((CHUNK, CHUNK), F32)
        dalast = jnp.zeros((1, CHUNK), F32)
        ddt_dir = jnp.zeros((CHUNK, CHUNK), F32)
        ddsk_ref[...] += jnp.sum(dy_ref[...] * xs_ref[...], axis=0, keepdims=True)

        def head_sums(z, pick):
            hi = z.astype(BF16)
            return _dot(hi, pick) + _dot((z - hi.astype(F32)).astype(BF16), pick)

        for g in range(G_SSD):
            gs = slice(_GW * g, _GW * (g + 1))
            pick = (jnp.right_shift(_iota((_GW, CHUNK), 0), 6) + 8 * g == _iota((_GW, CHUNK), 1)).astype(BF16)
            bg = b_ref[:, N_STATE * g:N_STATE * (g + 1)].astype(BF16)
            cg = c_ref[:, N_STATE * g:N_STATE * (g + 1)].astype(BF16)
            cb = _dot(cg, bg, _NT)
            hg = hin_ref[0, gs, :]
            hgb = hg.astype(BF16)
            dhn = dh_scr[gs, :]
            dhnb = dhn.astype(BF16)
            esg = es_scr[:, gs]
            dyg = dy_ref[:, gs]
            xsg = xs_ref[:, gs]
            xg = xsg * dte_scr[:, gs]
            dyeb = (dyg * e_scr[:, gs]).astype(BF16)
            dc = _dot(dyeb, hgb)
            dh_y = _dot(dyeb, cg, _TN)
            dxs = _dot(bg, dhnb, _NT) * esg
            db = _dot((xg * esg).astype(BF16), dhnb)
            cd = _chunk_decay_rows(acs_t, g)
            dh_scr[gs, :] = dhn * cd + dh_y
            end_state = head_sums(jnp.broadcast_to(jnp.sum(xg * dxs, axis=0, keepdims=True), (8, _GW)), pick)[0:1, :]
            carried = dhn * hg * cd
            per_head = jnp.concatenate([jnp.sum(carried[64 * hh:64 * hh + 64, :], axis=0, keepdims=True)
                                        for hh in range(8)], axis=0)
            per_head = jnp.sum(per_head, axis=1, keepdims=True)
            for hh in range(8):
                end_state = end_state + jnp.where(lane_row == 8 * g + hh, per_head[hh:hh + 1, :], 0.0)
            dalast = dalast + end_state
            dcb = jnp.zeros((CHUNK, CHUNK), F32)
            for jj in range(4):
                j = 4 * g + jj
                sl = slice(CHUNK * j, CHUNK * (j + 1))
                ps = slice(CHUNK * jj, CHUNK * (jj + 1))
                xpb = xg[:, ps].astype(BF16)
                dyp = dyg[:, ps]
                dxp = dxs[:, ps]
                for hh in range(2):
                    h = 2 * j + hh
                    ws = slice(CHUNK * (2 * jj + hh), CHUNK * (2 * jj + hh + 1))
                    seg = acs[:, h:h + 1] - acs_t[h:h + 1, :]
                    lm = jnp.exp(jnp.where(causal, seg, -1e30))
                    mf = cb * lm
                    dyh = jnp.where(lo if hh == 0 else ~lo, dyp, 0.0).astype(BF16)
                    gm = _dot(dyh, xpb, _NT)
                    dcb = dcb + gm * lm
                    w = gm * mf
                    whi = w.astype(BF16)
                    whi_scr[:, ws] = whi
                    wlo_scr[:, ws] = (w - whi.astype(F32)).astype(BF16)
                    dacs_t = dacs_t - jnp.where(r0 == h, jnp.sum(w, axis=0, keepdims=True), 0.0)
                    dxp = dxp + _dot(mf.astype(BF16), dyh, _TN)
                dx_scr[:, sl] = dxp
            dxg = dx_scr[:, gs]
            pick_w = (jnp.right_shift(_iota((8 * CHUNK, CHUNK), 0), 7) + 8 * g == _iota((8 * CHUNK, CHUNK), 1)).astype(BF16)
            ch = _dot(cg, hgb, _NT)
            dacs = (dacs + _dot(whi_scr[...], pick_w) + _dot(wlo_scr[...], pick_w)
                    + head_sums(dyg * e_scr[:, gs] * ch - xg * dxs, pick))
            ddt_dir = ddt_dir + head_sums(dxg * xsg, pick)
            dcbb = dcb.astype(BF16)
            dxbc_ref[:, D_SSD + N_STATE * g:D_SSD + N_STATE * (g + 1)] = db + _dot(dcbb, cg, _TN)
            dxbc_ref[:, D_SSD + _GW + N_STATE * g:D_SSD + _GW + N_STATE * (g + 1)] = dc + _dot(dcbb, bg)
        dxbc_ref[:, 0:D_SSD] = dx_scr[...] * dte_scr[...] + dsk_ref[...] * dy_ref[...]
        dacs = dacs + dacs_t.T + jnp.where(r0 == CHUNK - 1, dalast, 0.0)
        dadt = _dot_exact((r1 >= r0).astype(F32), dacs)
        ddt_ref[...] = dadt * a_ref[...] + ddt_dir
        da_ref[...] += jnp.sum(dadt * dt, axis=0, keepdims=True)

    rev = lambda s: (nc - 1 - s, 0)
    return pl.pallas_call(
        body, name="ssd_bwd",
        out_shape=(jax.ShapeDtypeStruct((t, CONV_DIM), F32), jax.ShapeDtypeStruct((t, N_SMALL), F32),
                   jax.ShapeDtypeStruct((1, N_SMALL), F32), jax.ShapeDtypeStruct((1, D_SSD), F32)),
        grid=(nc,),
        in_specs=[pl.BlockSpec((CHUNK, D_SSD), rev),
                  pl.BlockSpec((CHUNK, _GW), lambda s: (nc - 1 - s, 4)),
                  pl.BlockSpec((CHUNK, _GW), lambda s: (nc - 1 - s, 5)),
                  pl.BlockSpec((CHUNK, N_SMALL), rev),
                  pl.BlockSpec((1, N_SMALL), lambda s: (0, 0)),
                  pl.BlockSpec((1, D_SSD), lambda s: (0, 0)),
                  pl.BlockSpec((1, D_SSD, N_STATE), lambda s: (nc - 1 - s, 0, 0)),
                  pl.BlockSpec((CHUNK, D_SSD), rev)],
        out_specs=(pl.BlockSpec((CHUNK, CONV_DIM), rev),
                   pl.BlockSpec((CHUNK, N_SMALL), rev),
                   pl.BlockSpec((1, N_SMALL), lambda s: (0, 0)),
                   pl.BlockSpec((1, D_SSD), lambda s: (0, 0))),
        scratch_shapes=([pltpu.VMEM((D_SSD, N_STATE), F32)] + [pltpu.VMEM((CHUNK, D_SSD), F32)] * 4
                        + [pltpu.VMEM((CHUNK, 8 * CHUNK), BF16)] * 2),
        compiler_params=_cp("arbitrary"),
    )(xbc, xbc, xbc, dtlf, a_row, dsk_row, hin, dy)


_NPAIR = H_ATT // 2
_QB, _KB, _VB = C_Q // 128, C_K // 128, C_V // 128
_SCALE = 1.0 / math.sqrt(64.0)


def _attn_blocks(t):
    return _tile(t, (1408, 384, 256, 128)), _tile(t, (384, 128))


def _split3(c):
    hi = c.astype(BF16).astype(F32)
    rest = c - hi
    mid = rest.astype(BF16).astype(F32)
    return hi, mid, rest - mid


def _head_lanes(lane, hh):
    return (lane < 64, 64) if hh == 0 else (lane >= 64, 0)


def _q_operand(q, cq, lane, hh):
    sel, first = _head_lanes(lane, hh)
    out = jnp.where(sel, q, 0.0)
    for n, col in enumerate(_split3(cq) + (1.0, 1.0, 1.0)):
        out = jnp.where(lane == first + n, col, out)
    return out.astype(BF16)


def _k_operand(k, ck, lane, hh):
    sel, first = _head_lanes(lane, hh)
    hi, mid, lo = _split3(ck)
    out = jnp.where(sel, k, 0.0)
    for n, col in enumerate((1.0, 1.0, 1.0, -hi, -mid, -lo)):
        out = jnp.where(lane == first + n, col, out)
    return out.astype(BF16)


def _needs_mask(i, kk, bq, bk):
    return kk * bk + bk - 1 > i * bq


_C_FILLER = 2.0 ** 30


def _attn_fwd(proj, c_col):
    t = proj.shape[0]
    bq, bk = _attn_blocks(t)
    nq, nk = t // bq, t // bk
    rs = 16

    def last_kv(i):
        return (i * bq + bq - 1) // bk

    def body(q_ref, k_ref, v_ref, cq_ref, ck_ref, o_ref, lse_ref, qs_scr, s_scr, p_scr, m_scr, acc_scr):
        i = pl.program_id(1)
        kk = pl.program_id(2)
        lane_q = _iota((bq, 128), 1)

        @pl.when(kk == 0)
        def _():
            m_scr[...] = jnp.full_like(m_scr, -1e30)
            acc_scr[...] = jnp.zeros_like(acc_scr)
            q = q_ref[...] * _SCALE
            cq = cq_ref[0]
            for hh in range(2):
                qs_scr[hh] = _q_operand(q, cq[:, hh:hh + 1], lane_q, hh)

        def step(masked):
            lane_k = _iota((bk, 128), 1)
            k = k_ref[...]
            v = v_ref[...]
            ck = ck_ref[0]
            ahead = _iota((rs, bq), 0) - _iota((rs, bq), 1)
            vss = []
            for hh in range(2):
                sel, first = _head_lanes(lane_k, hh)
                ks = _k_operand(k, ck[:, hh:hh + 1], lane_k, hh)
                vss.append(jnp.where(sel, v, jnp.where(lane_k == first, 1.0, 0.0)).astype(BF16))
                s_scr[hh] = _dot(ks, qs_scr[hh], _NT)
            for hh in range(2):
                vs = vss[hh]

                def block_max(r, mx):
                    rows = pl.ds(pl.multiple_of(r * rs, rs), rs)
                    s = s_scr[hh, rows, :]
                    if masked:
                        s = jnp.where(ahead <= i * bq - kk * bk - r * rs, s, -1e30)
                        s_scr[hh, rows, :] = s
                    return jnp.maximum(mx, s)

                mx = lax.fori_loop(0, bk // rs, block_max, jnp.full((rs, bq), -1e30, F32), unroll=True)
                m_old = m_scr[hh]
                m_new = jnp.maximum(m_old, jnp.max(mx, axis=0, keepdims=True))
                m_scr[hh] = m_new

                def probs(r, carry):
                    rows = pl.ds(pl.multiple_of(r * rs, rs), rs)
                    p_scr[hh, rows, :] = jnp.exp(s_scr[hh, rows, :] - m_new).astype(BF16)
                    return carry

                lax.fori_loop(0, bk // rs, probs, 0, unroll=True)
                acc_scr[hh] = acc_scr[hh] * jnp.exp(m_old - m_new) + _dot(vs, p_scr[hh], _TN)

        active = kk <= last_kv(i)
        masked = _needs_mask(i, kk, bq, bk)

        @pl.when(active & masked)
        def _():
            step(True)

        @pl.when(active & jnp.logical_not(masked))
        def _():
            step(False)

        @pl.when(kk == nk - 1)
        def _():
            a = acc_scr[0]
            b = acc_scr[1]
            la = a[64:65, :]
            lb = b[0:1, :]
            o_ref[...] = jnp.where(lane_q < 64, (a / la).T, (b / lb).T)
            lse_ref[0] = jnp.concatenate([m_scr[0] + jnp.log(la), m_scr[1] + jnp.log(lb)], axis=0)

    kvi = lambda i, kk: jnp.minimum(kk, last_kv(i))
    kv = lambda off: pl.BlockSpec((bk, 128), lambda j, i, kk: (kvi(i, kk), off + j))
    return pl.pallas_call(
        body, name="attn_fwd",
        out_shape=(jax.ShapeDtypeStruct((t, D_ATT), F32), jax.ShapeDtypeStruct((_NPAIR, 2, t), F32)),
        grid=(_NPAIR, nq, nk),
        in_specs=[pl.BlockSpec((bq, 128), lambda j, i, kk: (i, _QB + j)),
                  kv(_KB), kv(_VB),
                  pl.BlockSpec((1, bq, 2), lambda j, i, kk: (j, i, 0)),
                  pl.BlockSpec((1, bk, 2), lambda j, i, kk: (j, kvi(i, kk), 0))],
        out_specs=(pl.BlockSpec((bq, 128), lambda j, i, kk: (i, j)),
                   pl.BlockSpec((1, 2, bq), lambda j, i, kk: (j, 0, i))),
        scratch_shapes=[pltpu.VMEM((2, bq, 128), BF16), pltpu.VMEM((2, bk, bq), F32), pltpu.VMEM((2, bk, bq), BF16),
                        pltpu.VMEM((2, 1, bq), F32), pltpu.VMEM((2, 128, bq), F32)],
        compiler_params=_cp("parallel", "parallel", "arbitrary"),
    )(proj, proj, proj, c_col, c_col)


def _attn_delta(do, o):
    t = do.shape[0]
    tm = _row_tile(t)

    def body(do_ref, o_ref, d_ref):
        pick = (jnp.right_shift(_iota((D_ATT, 128), 0), 6) == _iota((D_ATT, 128), 1)).astype(F32)
        d_ref[...] = _dot_exact(do_ref[...] * o_ref[...], pick)

    row = pl.BlockSpec((tm, D_ATT), lambda i: (i, 0))
    return pl.pallas_call(
        body, name="attn_delta",
        out_shape=jax.ShapeDtypeStruct((t, 128), F32),
        grid=(t // tm,), in_specs=[row, row], out_specs=pl.BlockSpec((tm, 128), lambda i: (i, 0)),
        compiler_params=_cp("parallel"),
    )(do, o)


def _attn_bwd(proj, c_col, lse_row, dl_row, do, exchange=None):
    t = proj.shape[0]
    bq, bk = _attn_blocks(t)
    nq, nk = t // bq, t // bk
    rs = 16

    def first_q(kk):
        return (kk * bk) // bq

    def body(q_ref, k_ref, v_ref, cq_ref, ck_ref, lse_ref, dl_ref, do_ref,
             dq_ref, dk_ref, dv_ref, dck_ref, dcq_ref,
             qs_scr, doh_scr, ks_scr, s_scr, dp_scr, p_scr, ds_scr, dq_scr, dk_scr, dv_scr):
        kk = pl.program_id(1)
        i = pl.program_id(2)
        lane_q = _iota((bq, 128), 1)
        lane_k = _iota((bk, 128), 1)
        qrows = pl.ds(pl.multiple_of(i * bq, 128), bq)

        @pl.when(kk == 0)
        def _():
            q = q_ref[...] * _SCALE
            cq = cq_ref[0]
            do_ = do_ref[...]
            for hh in range(2):
                qs_scr[hh, qrows, :] = _q_operand(q, cq[:, hh:hh + 1], lane_q, hh)
                doh_scr[hh, qrows, :] = jnp.where(_head_lanes(lane_q, hh)[0], do_, 0.0).astype(BF16)
                dq_scr[hh, qrows, :] = jnp.zeros((bq, 128), F32)

        @pl.when(i == 0)
        def _():
            dk_scr[...] = jnp.zeros_like(dk_scr)
            dv_scr[...] = jnp.zeros_like(dv_scr)
            k = k_ref[...]
            ck = ck_ref[0]
            for hh in range(2):
                ks_scr[hh] = _k_operand(k, ck[:, hh:hh + 1], lane_k, hh)

        def step(masked):
            v16 = v_ref[...].astype(BF16)
            lse = lse_ref[0]
            dl = dl_ref[0]
            ahead = _iota((rs, bq), 0) - _iota((rs, bq), 1)
            for hh in range(2):
                s_scr[hh] = _dot(ks_scr[hh], qs_scr[hh, qrows, :], _NT)
                dp_scr[hh] = _dot(v16, doh_scr[hh, qrows, :], _NT)
            for hh in range(2):
                qs = qs_scr[hh, qrows, :]
                doh = doh_scr[hh, qrows, :]

                def strip(r, carry):
                    rows = pl.ds(pl.multiple_of(r * rs, rs), rs)
                    p = jnp.exp(s_scr[hh, rows, :] - lse[hh:hh + 1, :])
                    if masked:
                        p = jnp.where(ahead <= i * bq - kk * bk - r * rs, p, 0.0)
                    p_scr[hh, rows, :] = p.astype(BF16)
                    ds_scr[hh, rows, :] = (p * (dp_scr[hh, rows, :] - dl[hh:hh + 1, :])).astype(BF16)
                    return carry

                lax.fori_loop(0, bk // rs, strip, 0, unroll=True)
                dv_scr[...] += _dot(p_scr[hh], doh)
                dk_scr[hh] += _dot(ds_scr[hh], qs)
                dq_scr[hh, qrows, :] += _dot(ds_scr[hh], ks_scr[hh], _TN)

        active = i >= first_q(kk)
        masked = _needs_mask(i, kk, bq, bk)

        @pl.when(active & masked)
        def _():
            step(True)

        @pl.when(active & jnp.logical_not(masked))
        def _():
            step(False)

        @pl.when(i == nq - 1)
        def _():
            dka = dk_scr[0]
            dkb = dk_scr[1]
            dk_ref[...] = jnp.where(lane_k < 64, dka, dkb).astype(BF16)
            dv_ref[...] = dv_scr[...].astype(BF16)
            dck_ref[0] = -jnp.where(_iota((bk, 2), 1) == 0, dka[:, 67:68], dkb[:, 3:4])

        @pl.when((kk == nk - 1) & (i == nq - 1))
        def _():
            lane_t = _iota((t, 128), 1)
            dqa = dq_scr[0]
            dqb = dq_scr[1]
            dq_ref[...] = (jnp.where(lane_t < 64, dqa, dqb) * _SCALE).astype(BF16)
            dcq_ref[0] = jnp.where(_iota((t, 2), 1) == 0, dqa[:, 64:65], dqb[:, 0:1])

    qi = lambda kk, i: jnp.where(kk == 0, i, nq - 1)
    qspec = lambda off: pl.BlockSpec((bq, 128), lambda j, kk, i: (qi(kk, i), off + j))
    kspec = lambda off: pl.BlockSpec((bk, 128), lambda j, kk, i: (kk, off + j))
    rowspec = pl.BlockSpec((1, 2, bq), lambda j, kk, i: (j, 0, jnp.maximum(i, first_q(kk))))
    return _hosted_call(
        body, name="attn_bwd",
        out_shape=(jax.ShapeDtypeStruct((t, D_ATT), BF16), jax.ShapeDtypeStruct((t, D_ATT), BF16),
                   jax.ShapeDtypeStruct((t, D_ATT), BF16), jax.ShapeDtypeStruct((_NPAIR, t, 2), F32),
                   jax.ShapeDtypeStruct((_NPAIR, t, 2), F32)),
        grid=(_NPAIR, nk, nq),
        in_specs=[qspec(_QB), kspec(_KB), kspec(_VB),
                  pl.BlockSpec((1, bq, 2), lambda j, kk, i: (j, qi(kk, i), 0)),
                  pl.BlockSpec((1, bk, 2), lambda j, kk, i: (j, kk, 0)),
                  rowspec, rowspec, qspec(0)],
        out_specs=(pl.BlockSpec((t, 128), lambda j, kk, i: (0, j)),
                   pl.BlockSpec((bk, 128), lambda j, kk, i: (kk, j)),
                   pl.BlockSpec((bk, 128), lambda j, kk, i: (kk, j)),
                   pl.BlockSpec((1, bk, 2), lambda j, kk, i: (j, kk, 0)),
                   pl.BlockSpec((1, t, 2), lambda j, kk, i: (j, 0, 0))),
        scratch_shapes=[pltpu.VMEM((2, t, 128), BF16), pltpu.VMEM((2, t, 128), BF16), pltpu.VMEM((2, bk, 128), BF16),
                        pltpu.VMEM((2, bk, bq), F32), pltpu.VMEM((2, bk, bq), F32),
                        pltpu.VMEM((2, bk, bq), BF16), pltpu.VMEM((2, bk, bq), BF16),
                        pltpu.VMEM((2, t, 128), F32), pltpu.VMEM((2, bk, 128), F32), pltpu.VMEM((bk, 128), F32)],
        operands=(proj, proj, proj, c_col, c_col, lse_row, dl_row, do),
        semantics=("parallel", "arbitrary", "arbitrary"), exchange=exchange)


def _premerge_fwd(y, o, proj, gamma):
    t = y.shape[0]
    tm = _row_tile_wide(t)

    def body(y_ref, z_ref, o_ref, za_ref, g_ref, ys_ref, ya_ref):
        z = z_ref[...]
        u = y_ref[...] * (z * _sigmoid(z))
        for g in range(G_SSD):
            gs = slice(_GW * g, _GW * (g + 1))
            ug = u[:, gs]
            r = lax.rsqrt(jnp.mean(ug * ug, axis=-1, keepdims=True) + EPS)
            ys_ref[:, gs] = (ug * r * g_ref[:, gs]).astype(BF16)
        za = za_ref[...]
        ya_ref[...] = (o_ref[...] * (za * _sigmoid(za))).astype(BF16)

    return pl.pallas_call(
        body, name="premerge_fwd",
        out_shape=(jax.ShapeDtypeStruct((t, D_SSD), BF16), jax.ShapeDtypeStruct((t, D_ATT), BF16)),
        grid=(t // tm,),
        in_specs=[pl.BlockSpec((tm, D_SSD), lambda i: (i, 0)),
                  pl.BlockSpec((tm, D_SSD), lambda i: (i, C_Z // D_SSD)),
                  pl.BlockSpec((tm, D_ATT), lambda i: (i, 0)),
                  pl.BlockSpec((tm, D_ATT), lambda i: (i, C_ZA // D_ATT)),
                  pl.BlockSpec((1, D_SSD), lambda i: (0, 0))],
        out_specs=(pl.BlockSpec((tm, D_SSD), lambda i: (i, 0)), pl.BlockSpec((tm, D_ATT), lambda i: (i, 0))),
        compiler_params=_cp("parallel"),
    )(y, proj, o, proj, gamma)


def _premerge_bwd(dys, dya, y, o, proj, gamma, exchange=None):
    t = y.shape[0]
    tm = _row_tile_wide(t)

    def body(dys_ref, dya_ref, y_ref, z_ref, o_ref, za_ref, g_ref, dy_ref, dz_ref, do_ref, dza_ref, dg_ref):
        i = pl.program_id(0)
        z = z_ref[...]
        sz = _sigmoid(z)
        silu = z * sz
        dsilu = sz * (1.0 + z * (1.0 - sz))
        yv = y_ref[...]
        u = yv * silu
        parts = []
        for g in range(G_SSD):
            gs = slice(_GW * g, _GW * (g + 1))
            ug = u[:, gs]
            r = lax.rsqrt(jnp.mean(ug * ug, axis=-1, keepdims=True) + EPS)
            n = ug * r
            dout = dys_ref[:, gs]
            dn = dout * g_ref[:, gs]
            du = r * (dn - n * jnp.mean(dn * n, axis=-1, keepdims=True))
            dy_ref[:, gs] = du * silu[:, gs]
            dz_ref[:, gs] = (du * yv[:, gs] * dsilu[:, gs]).astype(BF16)
            parts.append(jnp.sum(dout * n, axis=0, keepdims=True))
        dg = jnp.concatenate(parts, axis=1)
        za = za_ref[...]
        sa = _sigmoid(za)
        dya_ = dya_ref[...]
        do_ref[...] = dya_ * (za * sa)
        dza_ref[...] = (dya_ * o_ref[...] * (sa * (1.0 + za * (1.0 - sa)))).astype(BF16)

        @pl.when(i == 0)
        def _():
            dg_ref[...] = dg

        @pl.when(i > 0)
        def _():
            dg_ref[...] += dg

    ssd = pl.BlockSpec((tm, D_SSD), lambda i: (i, 0))
    att = pl.BlockSpec((tm, D_ATT), lambda i: (i, 0))
    vec = pl.BlockSpec((1, D_SSD), lambda i: (0, 0))
    return _hosted_call(
        body, name="premerge_bwd",
        out_shape=(jax.ShapeDtypeStruct((t, D_SSD), F32), jax.ShapeDtypeStruct((t, D_SSD), BF16),
                   jax.ShapeDtypeStruct((t, D_ATT), F32), jax.ShapeDtypeStruct((t, D_ATT), BF16),
                   jax.ShapeDtypeStruct((1, D_SSD), F32)),
        grid=(t // tm,),
        in_specs=[ssd, att, ssd, pl.BlockSpec((tm, D_SSD), lambda i: (i, C_Z // D_SSD)), att,
                  pl.BlockSpec((tm, D_ATT), lambda i: (i, C_ZA // D_ATT)), vec],
        out_specs=(ssd, ssd, att, att, vec),
        scratch_shapes=[],
        operands=(dys, dya, y, proj, o, proj, gamma), semantics=("arbitrary",), exchange=exchange)


_G_BLK = C_G // D_MODEL


def _merge_fwd(a, b, proj, gate_bias):
    t = a.shape[0]
    tm = _row_tile(t)

    def body(a_ref, b_ref, gs_ref, ga_ref, bias_ref, m_ref):
        g_ssd = _sigmoid(gs_ref[...] + bias_ref[:, 0:D_MODEL])
        g_att = _sigmoid(ga_ref[...] + bias_ref[:, D_MODEL:2 * D_MODEL])
        m_ref[...] = (g_ssd * a_ref[...] + g_att * b_ref[...]).astype(BF16)

    row = pl.BlockSpec((tm, D_MODEL), lambda i: (i, 0))
    return pl.pallas_call(
        body, name="merge_fwd",
        out_shape=jax.ShapeDtypeStruct((t, D_MODEL), BF16),
        grid=(t // tm,),
        in_specs=[row, row,
                  pl.BlockSpec((tm, D_MODEL), lambda i: (i, _G_BLK)),
                  pl.BlockSpec((tm, D_MODEL), lambda i: (i, _G_BLK + 1)),
                  pl.BlockSpec((1, 2 * D_MODEL), lambda i: (0, 0))],
        out_specs=row,
        compiler_params=_cp("parallel"),
    )(a, b, proj, proj, gate_bias)


def _merge_bwd(dm, a, b, proj, gate_bias):
    t = a.shape[0]
    tm = _row_tile(t)

    def body(dm_ref, a_ref, b_ref, gs_ref, ga_ref, bias_ref, da_ref, db_ref, dg_ref, dbias_ref):
        i = pl.program_id(0)
        dm_ = dm_ref[...]
        g_ssd = _sigmoid(gs_ref[...] + bias_ref[:, 0:D_MODEL])
        g_att = _sigmoid(ga_ref[...] + bias_ref[:, D_MODEL:2 * D_MODEL])
        da_ref[...] = (dm_ * g_ssd).astype(BF16)
        db_ref[...] = (dm_ * g_att).astype(BF16)
        dgs = dm_ * a_ref[...] * g_ssd * (1.0 - g_ssd)
        dga = dm_ * b_ref[...] * g_att * (1.0 - g_att)
        dg_ref[:, 0:D_MODEL] = dgs.astype(BF16)
        dg_ref[:, D_MODEL:2 * D_MODEL] = dga.astype(BF16)
        part = jnp.concatenate([jnp.sum(dgs, axis=0, keepdims=True), jnp.sum(dga, axis=0, keepdims=True)], axis=1)

        @pl.when(i == 0)
        def _():
            dbias_ref[...] = part

        @pl.when(i > 0)
        def _():
            dbias_ref[...] += part

    row = pl.BlockSpec((tm, D_MODEL), lambda i: (i, 0))
    wide = pl.BlockSpec((tm, 2 * D_MODEL), lambda i: (i, 0))
    vec = pl.BlockSpec((1, 2 * D_MODEL), lambda i: (0, 0))
    return pl.pallas_call(
        body, name="merge_bwd",
        out_shape=(jax.ShapeDtypeStruct((t, D_MODEL), BF16), jax.ShapeDtypeStruct((t, D_MODEL), BF16),
                   jax.ShapeDtypeStruct((t, 2 * D_MODEL), BF16), jax.ShapeDtypeStruct((1, 2 * D_MODEL), F32)),
        grid=(t // tm,),
        in_specs=[row, row, row,
                  pl.BlockSpec((tm, D_MODEL), lambda i: (i, _G_BLK)),
                  pl.BlockSpec((tm, D_MODEL), lambda i: (i, _G_BLK + 1)), vec],
        out_specs=(row, row, wide, vec),
        compiler_params=_cp("arbitrary"),
    )(dm, a, b, proj, proj, gate_bias)


def _post(o2, h, target, g):
    t = o2.shape[0]
    nc = t // CHUNK

    def body(o_ref, h_ref, t_ref, g_ref, dy_ref, do_ref, dg_ref, loss_ref):
        c = pl.program_id(0)
        x = o_ref[...]
        r = lax.rsqrt(jnp.mean(x * x, axis=-1, keepdims=True) + EPS)
        n = x * r
        y = h_ref[...] + n * g_ref[...]
        diff = jnp.where(c > 0, y - t_ref[...], 0.0)
        dy = diff * (1.0 / D_MODEL)
        dy_ref[...] = dy
        gdy = dy * g_ref[...]
        do_ref[...] = (r * (gdy - n * jnp.mean(gdy * n, axis=-1, keepdims=True))).astype(BF16)
        dg = jnp.sum(dy * n, axis=0, keepdims=True)
        lpart = 0.5 * jnp.sum(jnp.sum(diff * diff, axis=1, keepdims=True), axis=0, keepdims=True) * (1.0 / D_MODEL)
        sel = (_iota((8, 128), 0) == 0) & (_iota((8, 128), 1) == 0)

        @pl.when(c == 0)
        def _():
            dg_ref[...] = dg
            loss_ref[...] = jnp.zeros_like(loss_ref)

        @pl.when(c > 0)
        def _():
            dg_ref[...] += dg
            loss_ref[...] += jnp.where(sel, lpart, 0.0)

    row = pl.BlockSpec((CHUNK, D_MODEL), lambda c: (c, 0))
    vec = pl.BlockSpec((1, D_MODEL), lambda c: (0, 0))
    return pl.pallas_call(
        body, name="post",
        out_shape=(jax.ShapeDtypeStruct((t, D_MODEL), F32), jax.ShapeDtypeStruct((t, D_MODEL), BF16),
                   jax.ShapeDtypeStruct((1, D_MODEL), F32), jax.ShapeDtypeStruct((8, 128), F32)),
        grid=(nc,),
        in_specs=[row, row, pl.BlockSpec((CHUNK, D_MODEL), lambda c: (jnp.maximum(c - 1, 0), 0)), vec],
        out_specs=(row, row, vec, pl.BlockSpec((8, 128), lambda c: (0, 0))),
        compiler_params=_cp("arbitrary"),
    )(o2, h, target, g)


def _mm_tiles(t):
    return _tile(t, (704, 384, 128))


def _local_step(h, target, w_main, w_small, pr_slots, ids, norm_pre, conv_w, conv_b, bias_row, a_row,
                dsk_row, ssd_norm, gate_bias, norm_post):
    t = h.shape[0]
    tm = _mm_tiles(t)
    u = _norm1_fwd(h, norm_pre)
    proj, pr_slots = _matmul(u, w_main, "nt", F32, "inproj", tm, 1024, D_MODEL,
                             exchange=_gather_stage([pr_slots], to_sibling=False))
    small, pr_slots = _matmul(u, w_small, "nt", F32, "inproj_small", tm, N_SMALL, D_MODEL,
                              exchange=_gather_stage([pr_slots], to_sibling=True))
    wps = pr_slots[:, 0:512].reshape(D_SSD, D_MODEL)
    wpa = pr_slots[:, 512:768].reshape(D_ATT, D_MODEL)
    wout = pr_slots[:, 768:1024].reshape(D_MODEL, D_MODEL)
    dtlf = _small_fwd(small, bias_row)
    xbc = _conv_fwd(proj, conv_w, conv_b)
    y, hin = _ssd_fwd(xbc, dtlf, a_row, dsk_row)
    c_tok = dtlf[:, H_SSD:H_SSD + H_ATT]
    c_tok = jnp.where(jnp.arange(t)[:, None] < PADF, _C_FILLER, c_tok)
    c_col = c_tok.reshape(t, _NPAIR, 2).transpose(1, 0, 2)
    o, lse = _attn_fwd(proj, c_col)
    ys, ya = _premerge_fwd(y, o, proj, ssd_norm)
    a = _matmul(ys, wps, "nn", F32, "proj_ssd", tm, D_MODEL, D_SSD)
    b = _matmul(ya, wpa, "nn", F32, "proj_att", tm, D_MODEL, D_ATT)
    merged = _merge_fwd(a, b, proj, gate_bias)
    o2 = _matmul(merged, wout, "nn", F32, "out_proj", tm, D_MODEL, D_MODEL)
    dy_out, do2, d_norm_post, loss_blk = _post(o2, h, target, norm_post)

    dm = _matmul(do2, wout, "nt", F32, "out_proj_dx", tm, D_MODEL, D_MODEL)
    d_wout = _matmul(merged, do2, "tn", F32, "out_proj_dw", D_MODEL, D_MODEL, tm)
    da, db, dgraw, d_gate_bias = _merge_bwd(dm, a, b, proj, gate_bias)
    dys = _matmul(da, wps, "nt", F32, "proj_ssd_dx", tm, D_SSD, D_MODEL)
    d_wps = _matmul(ys, da, "tn", F32, "proj_ssd_dw", D_SSD, D_MODEL, tm)
    dya = _matmul(db, wpa, "nt", F32, "proj_att_dx", tm, D_ATT, D_MODEL)
    d_wpa = _matmul(ya, db, "tn", F32, "proj_att_dw", D_ATT, D_MODEL, tm)
    g32_pr = jnp.concatenate([d_wps.reshape(4, 512, D_MODEL), d_wpa.reshape(4, 256, D_MODEL),
                              d_wout.reshape(4, 256, D_MODEL)], axis=1)
    dy, dz, do, dza, d_ssd_norm, ra_pr = _premerge_bwd(dys, dya, y, o, proj, ssd_norm,
                                                       exchange=_pair_swap([g32_pr]))
    pb_pr = _add_pair(ids, g32_pr, ra_pr)
    dl_row = _attn_delta(do, o)[:, 0:H_ATT].T.reshape(_NPAIR, 2, t)
    dq, dk, dv, dc_key, dc_qry, rb_pr = _attn_bwd(proj, c_col, lse, dl_row, do, exchange=_chip_exchange([pb_pr]))
    half_pr = _add_chips(ids, g32_pr, ra_pr, rb_pr)
    dxbc, ddt, d_a, d_dsk = _ssd_bwd(xbc, dtlf, a_row, dsk_row, hin, dy)
    dxbc_raw, d_conv_w, d_conv_b = _conv_bwd(dxbc, proj, conv_w, conv_b)
    dc_tok = jnp.transpose(dc_key + dc_qry, (1, 0, 2)).reshape(t, H_ATT)
    dsm = ddt + jnp.pad(dc_tok, ((0, 0), (H_SSD, N_SMALL - H_SSD - H_ATT)))
    dsmall, d_bias_row = _small_bwd(dsm, small, bias_row)
    dproj = [dz, dxbc_raw, dza, dq, dk, dv, dgraw]
    return dict(loss_blk=loss_blk, u=u, dy_out=dy_out, dproj=dproj, dsmall=dsmall, half_pr=half_pr,
                d_conv_w=d_conv_w, d_conv_b=d_conv_b,
                d_bias_row=d_bias_row, d_a=d_a, d_dsk=d_dsk, d_ssd_norm=d_ssd_norm,
                d_gate_bias=d_gate_bias, d_norm_post=d_norm_post)


def _to_aligned_rows(slots):
    w = slots.reshape(N_COLS, slots.shape[2])

    def cut(o):
        return w[o[0]:o[0] + o[1]]
    main = jnp.concatenate([cut(O_Z), cut(O_XBC), cut(O_ZA), cut(O_Q), cut(O_K), cut(O_V), cut(O_G)], axis=0)
    pad = jnp.zeros((N_SMALL - H_SSD - H_ATT, w.shape[1]), w.dtype)
    small = jnp.concatenate([cut(O_DT), cut(O_F), pad], axis=0)
    return main, small


def _from_aligned_rows(main, small):
    def cm(c0, n):
        return main[c0:c0 + n]
    flat = jnp.concatenate([cm(C_Z, 2048), cm(C_XBC, 3072), small[0:H_SSD], cm(C_ZA, 1024),
                            cm(C_Q, 1024), cm(C_K, 1024), cm(C_V, 1024), small[H_SSD:H_SSD + H_ATT],
                            cm(C_G, 2048)], axis=0)
    return flat.reshape(4, N_COLS // 4, flat.shape[1])


_MESH = pl.DeviceIdType.MESH
_ANY = pl.BlockSpec(memory_space=pl.ANY)
_VM = pl.BlockSpec(memory_space=pltpu.VMEM)
_HALF = 512
N_DEV = 8


def _coords():
    return lax.axis_index("x"), lax.axis_index("y"), lax.axis_index("c")


def _other_chips(x, y):
    return [(1 - x, y), (x, 1 - y), (1 - x, 1 - y)]


def _half(cc):
    return pl.ds(cc * _HALF, _HALF)


def _gather_shards(slots):
    n = len(slots)

    def body(*refs):
        buf = refs[n:2 * n]
        send_sems, recv_sems = refs[2 * n:]
        x, y, c = _coords()
        chip = 2 * x + y
        sibling = (x, y, 1 - c)
        chips = _other_chips(x, y)

        def copy(i, frm, cc, k, to):
            part = buf[i].at[frm, :, _half(cc)]
            return pltpu.make_async_remote_copy(src_ref=part, dst_ref=part, send_sem=send_sems.at[6 * i + k],
                                                recv_sem=recv_sems.at[6 * i + k], device_id=to, device_id_type=_MESH)

        def chip_of(k):
            return 2 * chips[k][0] + chips[k][1]

        first = [copy(i, chip, c, k, (*chips[k], c)) for k in range(3) for i in range(n)]
        for cp in first:
            cp.start()
        passed = []
        for k in range(3):
            for i in range(n):
                copy(i, chip_of(k), c, k, (*chips[k], c)).wait_recv()
                passed.append(copy(i, chip_of(k), c, 3 + k, sibling))
                passed[-1].start()
        for k in range(3):
            for i in range(n):
                copy(i, chip_of(k), 1 - c, 3 + k, sibling).wait_recv()
        for cp in first + passed:
            cp.wait_send()

    return pl.pallas_call(
        body, name="gather_shards",
        out_shape=tuple(jax.ShapeDtypeStruct(s.shape, s.dtype) for s in slots),
        in_specs=[_ANY] * n, out_specs=tuple([_ANY] * n),
        input_output_aliases={i: i for i in range(n)},
        scratch_shapes=[pltpu.SemaphoreType.DMA((6 * n,)), pltpu.SemaphoreType.DMA((6 * n,))],
    )(*slots)


def _allgather8(block, name):
    rows, width = block.shape

    def body(x_ref, out_ref, send_sems, recv_sems, local_sem):
        x, y, c = _coords()
        me, sibling = (x, y, c), (x, y, 1 - c)
        chips = _other_chips(x, y)

        def slot(px, py, pc):
            return out_ref.at[4 * px + 2 * py + pc]

        def copy(k, blk, to, src=None):
            return pltpu.make_async_remote_copy(src_ref=slot(*blk) if src is None else src, dst_ref=slot(*blk),
                                                send_sem=send_sems.at[k], recv_sem=recv_sems.at[k],
                                                device_id=to, device_id_type=_MESH)

        mine = pltpu.make_async_copy(x_ref, slot(*me), local_sem)
        mine.start()
        first = [copy(0, me, sibling, src=x_ref)]
        first += [copy(1 + j, me, (*chip, c), src=x_ref) for j, chip in enumerate(chips)]
        for cp in first:
            cp.start()
        passed = [copy(4 + j, (*chip, c), sibling) for j, chip in enumerate(chips)]
        for j, chip in enumerate(chips):
            copy(1 + j, (*chip, c), me).wait_recv()
            passed[j].start()
        copy(0, sibling, me).wait_recv()
        for j, chip in enumerate(chips):
            copy(4 + j, (*chip, 1 - c), me).wait_recv()
        for cp in first + passed:
            cp.wait_send()
        mine.wait()

    return pl.pallas_call(
        body, name=name,
        out_shape=jax.ShapeDtypeStruct((N_DEV, rows, width), block.dtype),
        in_specs=[_VM], out_specs=_VM,
        scratch_shapes=[pltpu.SemaphoreType.DMA((7,)), pltpu.SemaphoreType.DMA((7,)), pltpu.SemaphoreType.DMA],
    )(block)


def _pair_swap(arrs):
    def copies(src, dst, send_sems, recv_sems):
        x, y, c = _coords()
        return [pltpu.make_async_remote_copy(src_ref=src[i].at[:, :, _half(1 - c)], dst_ref=dst[i],
                                             send_sem=send_sems.at[i], recv_sem=recv_sems.at[i],
                                             device_id=(x, y, 1 - c), device_id_type=_MESH) for i in range(len(src))]

    shapes = tuple(jax.ShapeDtypeStruct((4, a.shape[1], _HALF), a.dtype) for a in arrs)
    return tuple(arrs), shapes, copies, len(arrs), False


def _chip_exchange(arrs):
    def copies(src, dst, send_sems, recv_sems):
        x, y, c = _coords()
        chips = _other_chips(x, y)
        return [pltpu.make_async_remote_copy(src_ref=src[i].at[2 * chips[k][0] + chips[k][1]], dst_ref=dst[i].at[k],
                                             send_sem=send_sems.at[3 * i + k], recv_sem=recv_sems.at[3 * i + k],
                                             device_id=(*chips[k], c), device_id_type=_MESH)
                for k in range(3) for i in range(len(src))]

    shapes = tuple(jax.ShapeDtypeStruct((3,) + a.shape[1:], a.dtype) for a in arrs)
    return tuple(arrs), shapes, copies, 3 * len(arrs), False


def _gather_stage(slots, to_sibling):
    def copies(buf, _, send_sems, recv_sems):
        x, y, c = _coords()
        chips = _other_chips(x, y)
        out = []
        for k in range(3):
            for i in range(len(buf)):
                frm = 2 * chips[k][0] + chips[k][1] if to_sibling else 2 * x + y
                part = buf[i].at[frm, :, _half(c)]
                out.append(pltpu.make_async_remote_copy(
                    src_ref=part, dst_ref=part, send_sem=send_sems.at[3 * i + k], recv_sem=recv_sems.at[3 * i + k],
                    device_id=(x, y, 1 - c) if to_sibling else (*chips[k], c), device_id_type=_MESH))
        return out

    shapes = tuple(jax.ShapeDtypeStruct(s.shape, s.dtype) for s in slots)
    return tuple(slots), shapes, copies, 3 * len(slots), True


def _pair_join_halves(fulls):
    n = len(fulls)

    def body(*refs):
        buf = refs[n:2 * n]
        send_sems, recv_sems = refs[2 * n:]
        x, y, c = _coords()

        def remote(i, cc):
            part = buf[i].at[:, _half(cc)]
            return pltpu.make_async_remote_copy(src_ref=part, dst_ref=part, send_sem=send_sems.at[i],
                                                recv_sem=recv_sems.at[i], device_id=(x, y, 1 - c), device_id_type=_MESH)

        for i in range(n):
            remote(i, c).start()
        for i in range(n):
            remote(i, c).wait_send()
            remote(i, 1 - c).wait_recv()

    return pl.pallas_call(
        body, name="pair_join_halves",
        out_shape=tuple(jax.ShapeDtypeStruct(a.shape, a.dtype) for a in fulls),
        in_specs=[_ANY] * n, out_specs=tuple([_ANY] * n),
        input_output_aliases={i: i for i in range(n)},
        scratch_shapes=[pltpu.SemaphoreType.DMA((n,)), pltpu.SemaphoreType.DMA((n,))],
    )(*fulls)


_RED_TC = 128
_RED_NT = _HALF // _RED_TC


def _add_pair(ids, g32, recv_a):
    rows = g32.shape[1]

    def body(ids_ref, g_ref, r_ref, o_ref):
        o_ref[...] = (g_ref[...] + r_ref[...]).astype(BF16)

    blk = pl.BlockSpec((1, rows, _RED_TC), lambda j, l, ids: (j, 0, l))
    return pl.pallas_call(
        body, name="add_pair",
        out_shape=jax.ShapeDtypeStruct((4, rows, _HALF), BF16),
        grid_spec=pltpu.PrefetchScalarGridSpec(
            num_scalar_prefetch=1, grid=(4, _RED_NT),
            in_specs=[pl.BlockSpec((1, rows, _RED_TC), lambda j, l, ids: (j, 0, ids[0] * _RED_NT + l)), blk],
            out_specs=blk),
        compiler_params=_cp("parallel", "parallel"),
    )(ids, g32, recv_a)


def _add_chips(ids, g32, recv_a, recv_b):
    rows = g32.shape[1]

    def body(ids_ref, g_ref, a_ref, b_ref, o_ref):
        acc = g_ref[0] + a_ref[0]
        for k in range(3):
            acc = acc + b_ref[k].astype(F32)
        o_ref[...] = acc

    return pl.pallas_call(
        body, name="add_chips",
        out_shape=jax.ShapeDtypeStruct((rows, 2 * _HALF), F32),
        grid_spec=pltpu.PrefetchScalarGridSpec(
            num_scalar_prefetch=1, grid=(_RED_NT,),
            in_specs=[pl.BlockSpec((1, rows, _RED_TC), lambda l, ids: (ids[1], 0, ids[0] * _RED_NT + l)),
                      pl.BlockSpec((1, rows, _RED_TC), lambda l, ids: (ids[1], 0, l)),
                      pl.BlockSpec((3, rows, _RED_TC), lambda l, ids: (0, 0, l))],
            out_specs=pl.BlockSpec((rows, _RED_TC), lambda l, ids: (0, ids[0] * _RED_NT + l))),
        compiler_params=_cp("parallel"),
    )(ids, g32, recv_a, recv_b)


def _sum8(gathered):
    _, rows, width = gathered.shape

    def body(g_ref, o_ref):
        acc = g_ref[0]
        for d in range(1, N_DEV):
            acc = acc + g_ref[d]
        o_ref[...] = acc

    return pl.pallas_call(
        body, name="sum8",
        out_shape=jax.ShapeDtypeStruct((rows, width), F32),
        in_specs=[_VM], out_specs=_VM,
    )(gathered)


def _adamw(w, g, m, v, name):
    rows, cols = w.shape
    budget = (3 << 20) // 2
    tr, tc = rows, cols
    if rows * cols * 4 > budget:
        if rows % 8 == 0:
            tr = next(c for c in (512, 256, 128, 64, 32, 16, 8) if rows % c == 0 and c * cols * 4 <= budget)
        else:
            tc = next(c for c in (512, 256, 128) if cols % c == 0 and rows * c * 4 <= budget)
    c1 = 1.0 - ADAM_B1 ** ADAM_STEP
    c2 = 1.0 - ADAM_B2 ** ADAM_STEP

    def body(w_ref, g_ref, m_ref, v_ref, d_ref, mo_ref, vo_ref):
        gg = g_ref[...]
        mn = ADAM_B1 * m_ref[...] + (1.0 - ADAM_B1) * gg
        vn = ADAM_B2 * v_ref[...] + (1.0 - ADAM_B2) * (gg * gg)
        mo_ref[...] = mn
        vo_ref[...] = vn
        d_ref[...] = -ADAM_LR * ((mn / c1) / (jnp.sqrt(vn / c2) + ADAM_EPS) + ADAM_WD * w_ref[...])

    blk = pl.BlockSpec((tr, tc), lambda i, j: (i, j))
    shp = jax.ShapeDtypeStruct((rows, cols), F32)
    return pl.pallas_call(
        body, name=name, out_shape=(shp, shp, shp), grid=(rows // tr, cols // tc),
        in_specs=[blk] * 4, out_specs=(blk, blk, blk),
        compiler_params=_cp("parallel", "parallel"),
    )(w, g, m, v)


def _rows128(a):
    return a.reshape(-1, 128)


def _pack_small(norm_pre, conv_b, ssd_norm, gate_bias, norm_post, dt_bias, a_log, d_skip, fgate_bias):
    tiny = jnp.concatenate([dt_bias.reshape(-1), a_log.reshape(-1), d_skip.reshape(-1), fgate_bias.reshape(-1),
                            jnp.zeros((16,), F32)])
    return jnp.concatenate([_rows128(norm_pre), _rows128(conv_b), _rows128(ssd_norm), _rows128(gate_bias),
                            _rows128(norm_post), tiny.reshape(1, 128)], axis=0)


_SMALL_ROWS = 73
_SMALL_PAD = 80


def _unpack_small(p):
    tiny = p[72]
    return dict(norm_pre=p[0:8].reshape(1, 1024), conv_b=p[8:32].reshape(1, 3072), ssd_norm=p[32:48].reshape(1, 2048),
                gate_bias=p[48:64].reshape(1, 2048), norm_post=p[64:72].reshape(1, 1024),
                dt_bias=tiny[0:32].reshape(1, 32), a_log=tiny[32:64].reshape(1, 32),
                d_skip=tiny[64:96].reshape(1, 32), fgate_bias=tiny[96:112].reshape(1, 16))


def _pad_rows(a, rows):
    return jnp.concatenate([a, jnp.zeros((rows - a.shape[0], a.shape[1]), a.dtype)], axis=0)


def kernel(x, meta_tokens, norm_pre, w_in, conv_w, conv_b, dt_bias, a_log, d_skip, ssd_norm, fgate_bias, gate_bias, w_proj_ssd, w_proj_att, w_out, norm_post, loss_target, m_meta_tokens, m_norm_pre, m_w_in, m_conv_w, m_conv_b, m_dt_bias, m_a_log, m_d_skip, m_ssd_norm, m_fgate_bias, m_gate_bias, m_w_proj_ssd, m_w_proj_att, m_w_out, m_norm_post, v_meta_tokens, v_norm_pre, v_w_in, v_conv_w, v_conv_b, v_dt_bias, v_a_log, v_d_skip, v_ssd_norm, v_fgate_bias, v_gate_bias, v_w_proj_ssd, v_w_proj_att, v_w_out, v_norm_post):
    cx, cy, cc = _coords()
    chip = 2 * cx + cy
    ids = jnp.stack([cc, chip]).astype(jnp.int32)
    seq = x.shape[1]

    w_in_sh = jnp.transpose(w_in[0]).astype(BF16)
    w_pr_sh = jnp.concatenate([w_proj_ssd[0], w_proj_att[0], w_out[0]], axis=0).astype(BF16)

    def own_slot(sh):
        return lax.dynamic_update_slice(lax.empty((4,) + sh.shape, sh.dtype), sh[None], (chip, 0, 0))

    (g_in,) = _gather_shards([own_slot(w_in_sh)])
    w_main, w_small = _to_aligned_rows(g_in)
    sm_sh = jnp.concatenate([_rows128(meta_tokens), _rows128(conv_w[0])], axis=0)
    sm_all = _allgather8(sm_sh, "gather_small_weights")[0::2]
    meta_full = jnp.transpose(sm_all[:, 0:32].reshape(4, N_META, 256), (1, 0, 2)).reshape(N_META, D_MODEL)
    conv_w_full = jnp.transpose(sm_all[:, 32:56].reshape(4, CONV_K, 768), (1, 0, 2)).reshape(CONV_K, CONV_DIM)

    h = jnp.concatenate([jnp.zeros((PADF, D_MODEL), F32), meta_full, x[0]], axis=0)
    bias_row = jnp.concatenate([dt_bias[0], fgate_bias[0], jnp.zeros((N_SMALL - H_SSD - H_ATT,), F32)]).reshape(1, N_SMALL)
    a_neg = -jnp.exp(a_log[0])
    a_row = jnp.concatenate([a_neg, jnp.zeros((N_SMALL - H_SSD,), F32)]).reshape(1, N_SMALL)
    dsk_row = jnp.repeat(d_skip[0], 64).reshape(1, D_SSD)
    r = _local_step(h, loss_target[0], w_main, w_small, own_slot(w_pr_sh), ids, norm_pre, conv_w_full, conv_b,
                    bias_row, a_row, dsk_row, ssd_norm, gate_bias, norm_post)

    tm = _mm_tiles(h.shape[0])
    n_row_tiles = h.shape[0] // tm
    d_w_main = _matmul_cat_tn(r["dproj"], r["u"], "inproj_dw", tm)
    d_w_small = _matmul(r["dsmall"], r["u"], "tn", F32, "inproj_small_dw", N_SMALL, D_MODEL, tm)
    g32_in = _from_aligned_rows(d_w_main, d_w_small)
    first = max(n_row_tiles // 6, 1)
    du_first, ra_in = _matmul_cat_nn(r["dproj"], w_main, "inproj_dx_swap", tm, rows=(0, first),
                                     exchange=_pair_swap([g32_in]))
    pb_in = _add_pair(ids, g32_in, ra_in)
    du_a, rb_in = _matmul_cat_nn(r["dproj"], w_main, "inproj_dx_exchange", tm,
                                 rows=(first, n_row_tiles - first), fill=du_first,
                                 exchange=_chip_exchange([pb_in]))
    du_b = _matmul(r["dsmall"], w_small, "nn", F32, "inproj_small_dx", tm, D_MODEL, N_SMALL)
    dh, d_norm_pre = _norm1_bwd(du_a, du_b, h, norm_pre, r["dy_out"])
    grad_x = dh[PADF + N_META:].reshape(1, seq, D_MODEL)
    half_in = _add_chips(ids, g32_in, ra_in, rb_in)
    gw_in, gw_pr = _pair_join_halves([half_in, r["half_pr"]])

    tiny = r["d_bias_row"][0]
    part_small = _pack_small(d_norm_pre, r["d_conv_b"], r["d_ssd_norm"], r["d_gate_bias"], r["d_norm_post"],
                             tiny[0:H_SSD], r["d_a"][0, 0:H_SSD] * a_neg, r["d_dsk"].reshape(H_SSD, 64).sum(axis=1),
                             tiny[H_SSD:H_SSD + H_ATT])
    part = jnp.concatenate([_pad_rows(part_small, _SMALL_PAD), _rows128(r["d_conv_w"]),
                            _rows128(dh[PADF:PADF + N_META]), r["loss_blk"]], axis=0)
    tot = _sum8(_allgather8(part, "gather_small_grads"))
    loss = tot[_SMALL_PAD + 96 + 128, 0]
    g_small = tot[0:_SMALL_PAD]
    g_conv_w = lax.dynamic_slice_in_dim(tot[_SMALL_PAD:_SMALL_PAD + 96].reshape(CONV_K, CONV_DIM), chip * 768, 768, axis=1)
    g_meta = lax.dynamic_slice_in_dim(tot[_SMALL_PAD + 96:_SMALL_PAD + 224].reshape(N_META, D_MODEL), chip * 256, 256, axis=1)

    upd = {}
    upd["w_in"] = tuple(jnp.transpose(a) for a in (gw_in,) + _adamw(
        jnp.transpose(w_in[0]), gw_in, jnp.transpose(m_w_in[0]), jnp.transpose(v_w_in[0]), "adamw_w_in"))
    w_pr32 = jnp.concatenate([w_proj_ssd[0], w_proj_att[0], w_out[0]], axis=0)
    m_pr = jnp.concatenate([m_w_proj_ssd[0], m_w_proj_att[0], m_w_out[0]], axis=0)
    v_pr = jnp.concatenate([v_w_proj_ssd[0], v_w_proj_att[0], v_w_out[0]], axis=0)
    pr = (gw_pr,) + _adamw(w_pr32, gw_pr, m_pr, v_pr, "adamw_w_proj")
    upd["w_proj_ssd"] = tuple(a[0:512] for a in pr)
    upd["w_proj_att"] = tuple(a[512:768] for a in pr)
    upd["w_out"] = tuple(a[768:1024] for a in pr)
    upd["conv_w"] = (g_conv_w,) + _adamw(conv_w[0], g_conv_w, m_conv_w[0], v_conv_w[0], "adamw_conv_w")
    upd["meta_tokens"] = (g_meta,) + _adamw(meta_tokens, g_meta, m_meta_tokens, v_meta_tokens, "adamw_meta")
    pk = lambda np_, cb, sn, gb, npo, dtb, al, ds, fg: _pad_rows(_pack_small(np_, cb, sn, gb, npo, dtb, al, ds, fg), _SMALL_PAD)
    w_sm = pk(norm_pre, conv_b, ssd_norm, gate_bias, norm_post, dt_bias, a_log, d_skip, fgate_bias)
    m_sm = pk(m_norm_pre, m_conv_b, m_ssd_norm, m_gate_bias, m_norm_post, m_dt_bias, m_a_log, m_d_skip, m_fgate_bias)
    v_sm = pk(v_norm_pre, v_conv_b, v_ssd_norm, v_gate_bias, v_norm_post, v_dt_bias, v_a_log, v_d_skip, v_fgate_bias)
    sm = [_unpack_small(a) for a in (g_small,) + _adamw(w_sm, g_small, m_sm, v_sm, "adamw_small")]
    for name in ("norm_pre", "conv_b", "dt_bias", "a_log", "d_skip", "ssd_norm", "fgate_bias", "gate_bias", "norm_post"):
        upd[name] = tuple(s[name] for s in sm)
    lead = ("w_in", "conv_w", "w_proj_ssd", "w_proj_att", "w_out")
    order = ("meta_tokens", "norm_pre", "w_in", "conv_w", "conv_b", "dt_bias", "a_log", "d_skip", "ssd_norm",
             "fgate_bias", "gate_bias", "w_proj_ssd", "w_proj_att", "w_out", "norm_post")
    outs = [loss, grad_x]
    for part_i in range(4):
        for name in order:
            a = upd[name][part_i]
            outs.append(a[None] if name in lead else a)
    return tuple(outs)
```

```python
import functools
import math

import jax
import jax.numpy as jnp
from jax import lax
from jax.experimental import pallas as pl
from jax.experimental.pallas import tpu as pltpu

F32 = jnp.float32
BF16 = jnp.bfloat16
HIGHEST = lax.Precision.HIGHEST

D_MODEL = 1024
N_META = 16
CHUNK = 128
PADF = CHUNK - N_META
D_SSD = 2048
H_SSD = 32
G_SSD = 4
N_STATE = 128
CONV_K = 4
CONV_DIM = D_SSD + 2 * G_SSD * N_STATE
H_ATT = 16
D_ATT = 1024
EPS = 1e-6
N_COLS = 11312

C_Z, C_XBC, C_ZA, C_Q, C_K, C_V, C_G = 0, 2048, 5120, 6144, 7168, 8192, 9216
N_MAIN = 11264
N_SMALL = 128
O_Z, O_XBC, O_DT, O_ZA, O_Q, O_K, O_V, O_F, O_G = (
    (0, 2048), (2048, 3072), (5120, 32), (5152, 1024), (6176, 1024), (7200, 1024),
    (8224, 1024), (9248, 16), (9264, 2048))

ADAM_LR, ADAM_B1, ADAM_B2, ADAM_EPS, ADAM_WD, ADAM_STEP = 0.001, 0.9, 0.999, 1e-08, 0.01, 10

VMEM_LIMIT = 56 * 1024 * 1024


def _cp(*sem):
    return pltpu.CompilerParams(dimension_semantics=sem, vmem_limit_bytes=VMEM_LIMIT)


def _tile(n, prefs):
    for p in prefs:
        if n % p == 0:
            return p
    raise ValueError(f"no tile for {n} in {prefs}")


def _iota(shape, dim):
    return lax.broadcasted_iota(jnp.int32, shape, dim)


def _sigmoid(x):
    return 1.0 / (1.0 + jnp.exp(-x))


def _softplus_tail(x):
    return jnp.log(1.0 + jnp.exp(-jnp.abs(x)))


_NN = (((1,), (0,)), ((), ()))
_NT = (((1,), (1,)), ((), ()))
_TN = (((0,), (0,)), ((), ()))


def _dot(a, b, dims=_NN):
    return lax.dot_general(a, b, dims, preferred_element_type=F32)


def _dot_exact(a, b, dims=_NN):
    return lax.dot_general(a, b, dims, precision=HIGHEST, preferred_element_type=F32)


def _hosted_call(body, *, name, grid, in_specs, out_specs, out_shape, scratch_shapes, operands, semantics,
                 exchange=None, aliases=None):
    aliases = dict(aliases or {})
    if exchange is None:
        return pl.pallas_call(body, name=name, out_shape=out_shape, grid=grid, in_specs=in_specs,
                              out_specs=out_specs, scratch_shapes=scratch_shapes, input_output_aliases=aliases,
                              compiler_params=_cp(*semantics))(*operands)
    arrays, shapes, copies, n_sems, in_place = exchange
    n_in, n_out, n_ex = len(operands), len(out_shape), len(arrays)

    def hosted(*refs):
        ex_in = refs[n_in:n_in + n_ex]
        ex_out = refs[n_in + n_ex + n_out:n_in + n_ex + n_out + n_ex]
        own = refs[:n_in] + refs[n_in + n_ex:n_in + n_ex + n_out] + refs[n_in + 2 * n_ex + n_out:-2]
        first = functools.reduce(lambda p, q: p & q, [pl.program_id(d) == 0 for d in range(len(grid))])
        last = functools.reduce(lambda p, q: p & q, [pl.program_id(d) == grid[d] - 1 for d in range(len(grid))])

        def descriptors():
            return copies(ex_out if in_place else ex_in, ex_out, refs[-2], refs[-1])

        @pl.when(first)
        def _():
            for cp in descriptors():
                cp.start()

        body(*own)

        @pl.when(last)
        def _():
            for cp in descriptors():
                cp.wait()

    return pl.pallas_call(
        hosted, name=name,
        out_shape=tuple(out_shape) + tuple(shapes),
        grid=grid,
        in_specs=list(in_specs) + [_ANY] * n_ex,
        out_specs=tuple(out_specs) + (_ANY,) * n_ex,
        input_output_aliases={**aliases, **({n_in + e: n_out + e for e in range(n_ex)} if in_place else {})},
        scratch_shapes=list(scratch_shapes) + [pltpu.SemaphoreType.DMA((n_sems,)), pltpu.SemaphoreType.DMA((n_sems,))],
        compiler_params=_cp(*(("arbitrary",) * len(grid))),
    )(*operands, *arrays)


def _matmul(a, b, mode, out_dtype, name, tm, tn, tk, exchange=None):
    if mode == "tn":
        kdim, m = a.shape
    else:
        m, kdim = a.shape
    n = b.shape[0] if mode == "nt" else b.shape[1]
    nk = kdim // tk
    dims = {"nn": _NN, "nt": _NT, "tn": _TN}[mode]
    a_spec = (pl.BlockSpec((tk, tm), lambda i, j, k: (k, i)) if mode == "tn"
              else pl.BlockSpec((tm, tk), lambda i, j, k: (i, k)))
    b_spec = (pl.BlockSpec((tn, tk), lambda i, j, k: (j, k)) if mode == "nt"
              else pl.BlockSpec((tk, tn), lambda i, j, k: (k, j)))

    def body(a_ref, b_ref, o_ref, acc_ref):
        k = pl.program_id(2)
        p = _dot(a_ref[...].astype(BF16), b_ref[...].astype(BF16), dims)
        if nk == 1:
            o_ref[...] = p.astype(out_dtype)
        else:
            @pl.when(k == 0)
            def _():
                acc_ref[...] = p

            @pl.when(k > 0)
            def _():
                acc_ref[...] += p

            @pl.when(k == nk - 1)
            def _():
                o_ref[...] = acc_ref[...].astype(out_dtype)

    out = _hosted_call(
        body, name=name,
        out_shape=(jax.ShapeDtypeStruct((m, n), out_dtype),),
        grid=(m // tm, n // tn, nk),
        in_specs=[a_spec, b_spec],
        out_specs=(pl.BlockSpec((tm, tn), lambda i, j, k: (i, j)),),
        scratch_shapes=[pltpu.VMEM((tm, tn), F32)],
        operands=(a, b), semantics=("parallel", "parallel", "arbitrary"), exchange=exchange)
    return out[0] if exchange is None else out


_CAT_BLK = 1024


def _piece_ranges(pieces):
    out, off = [], 0
    for p in pieces:
        nb = p.shape[1] // _CAT_BLK
        out.append((off, nb))
        off += nb
    return out, off


def _matmul_cat_nn(pieces, b, name, tm, rows=None, fill=None, exchange=None):
    t = pieces[0].shape[0]
    n = b.shape[1]
    ranges, nk = _piece_ranges(pieces)
    first, ni = rows if rows is not None else (0, t // tm)
    n_in = len(pieces) + 1 + (fill is not None)

    def body(*refs):
        a_refs, b_ref, o_ref, acc_ref = refs[:len(pieces)], refs[len(pieces)], refs[n_in], refs[n_in + 1]
        k = pl.program_id(1)

        @pl.when(k == 0)
        def _():
            acc_ref[...] = jnp.zeros_like(acc_ref)

        for a_ref, (off, nb) in zip(a_refs, ranges):
            @pl.when((k >= off) & (k < off + nb))
            def _(a_ref=a_ref):
                acc_ref[...] += _dot(a_ref[...], b_ref[...])

        @pl.when(k == nk - 1)
        def _():
            o_ref[...] = acc_ref[...]

    def a_spec(off, nb):
        return pl.BlockSpec((tm, _CAT_BLK), lambda i, k: (first + i, jnp.clip(k - off, 0, nb - 1)))

    in_specs = [a_spec(off, nb) for off, nb in ranges] + [pl.BlockSpec((_CAT_BLK, n), lambda i, k: (k, 0))]
    operands = list(pieces) + [b]
    if fill is not None:
        in_specs.append(_ANY)
        operands.append(fill)
    out = _hosted_call(
        body, name=name,
        out_shape=(jax.ShapeDtypeStruct((t, n), F32),),
        grid=(ni, nk),
        in_specs=in_specs,
        out_specs=(pl.BlockSpec((tm, n), lambda i, k: (first + i, 0)),),
        scratch_shapes=[pltpu.VMEM((tm, n), F32)],
        operands=operands, semantics=("parallel", "arbitrary"), exchange=exchange,
        aliases={len(pieces) + 1: 0} if fill is not None else None)
    return out if exchange is not None else out[0]


def _matmul_cat_tn(pieces, b, name, tk):
    t = pieces[0].shape[0]
    n = b.shape[1]
    ranges, nm = _piece_ranges(pieces)
    nk = t // tk

    def body(*refs):
        a_refs, b_ref, o_ref, acc_ref = refs[:len(pieces)], refs[-3], refs[-2], refs[-1]
        m = pl.program_id(0)
        k = pl.program_id(1)

        @pl.when(k == 0)
        def _():
            acc_ref[...] = jnp.zeros_like(acc_ref)

        for a_ref, (off, nb) in zip(a_refs, ranges):
            @pl.when((m >= off) & (m < off + nb))
            def _(a_ref=a_ref):
                acc_ref[...] += _dot(a_ref[...], b_ref[...], _TN)

        @pl.when(k == nk - 1)
        def _():
            o_ref[...] = acc_ref[...]

    def a_spec(off, nb):
        def index(m, k):
            mine = (m >= off) & (m < off + nb)
            return jnp.where(mine, k, 0), jnp.clip(m - off, 0, nb - 1)
        return pl.BlockSpec((tk, _CAT_BLK), index)

    return pl.pallas_call(
        body, name=name,
        out_shape=jax.ShapeDtypeStruct((nm * _CAT_BLK, n), F32),
        grid=(nm, nk),
        in_specs=[a_spec(off, nb) for off, nb in ranges] + [pl.BlockSpec((tk, n), lambda m, k: (k, 0))],
        out_specs=pl.BlockSpec((_CAT_BLK, n), lambda m, k: (m, 0)),
        scratch_shapes=[pltpu.VMEM((_CAT_BLK, n), F32)],
        compiler_params=_cp("parallel", "arbitrary"),
    )(*pieces, b)


def _row_tile(t):
    return _tile(t, (352, 128))


def _row_tile_wide(t):
    return _tile(t, (176, 128))


def _norm1_fwd(h, g):
    t = h.shape[0]
    tm = _row_tile(t)

    def body(h_ref, g_ref, u_ref):
        x = h_ref[...]
        r = lax.rsqrt(jnp.mean(x * x, axis=-1, keepdims=True) + EPS)
        u_ref[...] = (x * r * g_ref[...]).astype(BF16)

    return pl.pallas_call(
        body, name="norm1_fwd",
        out_shape=jax.ShapeDtypeStruct((t, D_MODEL), BF16),
        grid=(t // tm,),
        in_specs=[pl.BlockSpec((tm, D_MODEL), lambda i: (i, 0)),
                  pl.BlockSpec((1, D_MODEL), lambda i: (0, 0))],
        out_specs=pl.BlockSpec((tm, D_MODEL), lambda i: (i, 0)),
        compiler_params=_cp("parallel"),
    )(h, g)


def _norm1_bwd(du_a, du_b, h, g, dy):
    t = h.shape[0]
    tm = _row_tile(t)

    def body(a_ref, b_ref, h_ref, g_ref, dy_ref, dh_ref, dg_ref):
        i = pl.program_id(0)
        x = h_ref[...]
        du = a_ref[...] + b_ref[...]
        r = lax.rsqrt(jnp.mean(x * x, axis=-1, keepdims=True) + EPS)
        gdu = du * g_ref[...]
        dh_ref[...] = dy_ref[...] + r * (gdu - x * (r * r) * jnp.mean(gdu * x, axis=-1, keepdims=True))
        part = jnp.sum(du * x * r, axis=0, keepdims=True)

        @pl.when(i == 0)
        def _():
            dg_ref[...] = part

        @pl.when(i > 0)
        def _():
            dg_ref[...] += part

    row = pl.BlockSpec((tm, D_MODEL), lambda i: (i, 0))
    vec = pl.BlockSpec((1, D_MODEL), lambda i: (0, 0))
    return pl.pallas_call(
        body, name="norm1_bwd",
        out_shape=(jax.ShapeDtypeStruct((t, D_MODEL), F32), jax.ShapeDtypeStruct((1, D_MODEL), F32)),
        grid=(t // tm,),
        in_specs=[row, row, row, vec, row],
        out_specs=(row, vec),
        compiler_params=_cp("arbitrary"),
    )(du_a, du_b, h, g, dy)


def _small_fwd(small, bias_row):
    t = small.shape[0]

    def body(s_ref, b_ref, o_ref, carry_ref):
        c = pl.program_id(0)

        @pl.when(c == 0)
        def _():
            carry_ref[...] = jnp.zeros_like(carry_ref)

        x = s_ref[...] + b_ref[...]
        r0 = _iota((CHUNK, CHUNK), 0)
        r1 = _iota((CHUNK, CHUNK), 1)
        valid = (c * CHUNK + r0) >= PADF
        tail = _softplus_tail(x)
        dt = jnp.where(valid & (r1 < H_SSD), jnp.maximum(x, 0.0) + tail, 0.0)
        lf = jnp.where(valid & (r1 >= H_SSD) & (r1 < H_SSD + H_ATT), jnp.minimum(x, 0.0) - tail, 0.0)
        tri = (r0 >= r1).astype(F32)
        cs = _dot_exact(tri, lf) + carry_ref[...]
        carry_ref[...] = cs[CHUNK - 1:CHUNK, :]
        o_ref[...] = dt + cs

    return pl.pallas_call(
        body, name="small_fwd",
        out_shape=jax.ShapeDtypeStruct((t, N_SMALL), F32),
        grid=(t // CHUNK,),
        in_specs=[pl.BlockSpec((CHUNK, N_SMALL), lambda c: (c, 0)),
                  pl.BlockSpec((1, N_SMALL), lambda c: (0, 0))],
        out_specs=pl.BlockSpec((CHUNK, N_SMALL), lambda c: (c, 0)),
        scratch_shapes=[pltpu.VMEM((1, N_SMALL), F32)],
        compiler_params=_cp("arbitrary"),
    )(small, bias_row)


def _small_bwd(dsm, small, bias_row):
    t = small.shape[0]
    nc = t // CHUNK

    def body(d_ref, s_ref, b_ref, o_ref, db_ref, carry_ref):
        step = pl.program_id(0)
        c = nc - 1 - step

        @pl.when(step == 0)
        def _():
            carry_ref[...] = jnp.zeros_like(carry_ref)
            db_ref[...] = jnp.zeros_like(db_ref)

        x = s_ref[...] + b_ref[...]
        d = d_ref[...]
        r0 = _iota((CHUNK, CHUNK), 0)
        r1 = _iota((CHUNK, CHUNK), 1)
        valid = (c * CHUNK + r0) >= PADF
        is_dt = r1 < H_SSD
        is_f = (r1 >= H_SSD) & (r1 < H_SSD + H_ATT)
        triu = (r1 >= r0).astype(F32)
        dc = jnp.where(is_f, d, 0.0)
        dlf = _dot_exact(triu, dc) + carry_ref[...]
        carry_ref[...] = dlf[0:1, :]
        sg = _sigmoid(x)
        out = jnp.where(valid & is_dt, d * sg, 0.0) + jnp.where(valid & is_f, dlf * (1.0 - sg), 0.0)
        o_ref[...] = out.astype(BF16)
        db_ref[...] += jnp.sum(out, axis=0, keepdims=True)

    blk = pl.BlockSpec((CHUNK, N_SMALL), lambda s: (nc - 1 - s, 0))
    vec = pl.BlockSpec((1, N_SMALL), lambda s: (0, 0))
    return pl.pallas_call(
        body, name="small_bwd",
        out_shape=(jax.ShapeDtypeStruct((t, N_SMALL), BF16), jax.ShapeDtypeStruct((1, N_SMALL), F32)),
        grid=(nc,),
        in_specs=[blk, blk, vec],
        out_specs=(blk, vec),
        scratch_shapes=[pltpu.VMEM((1, N_SMALL), F32)],
        compiler_params=_cp("arbitrary"),
    )(dsm, small, bias_row)


_CONV_TC = 1024
_XBC_BLK = C_XBC // _CONV_TC


def _shift_down(cur, prev8, j):
    rc = pltpu.roll(cur, j, 0)
    rid = _iota(prev8.shape, 0)
    top = jnp.where(rid < j, pltpu.roll(prev8, j, 0), rc[0:8, :])
    return top if cur.shape[0] == 8 else jnp.concatenate([top, rc[8:, :]], axis=0)


def _shift_up(cur, next8, j):
    n = cur.shape[0]
    ru = pltpu.roll(cur, n - j, 0)
    rid = _iota(next8.shape, 0)
    bot = jnp.where(rid >= 8 - j, pltpu.roll(next8, 8 - j, 0), ru[n - 8:, :])
    return jnp.concatenate([ru[:n - 8, :], bot], axis=0)


def _conv_taps(cur, prev, w, b):
    taps = [cur] + [_shift_down(cur, prev, j) for j in (1, 2, 3)]
    acc = b + taps[0] * w[3:4, :]
    for j in (1, 2, 3):
        acc = acc + taps[j] * w[3 - j:4 - j, :]
    return acc, taps


def _conv_pre(x_ref, p_ref, w_ref, b_ref, i):
    return _conv_taps(x_ref[...], jnp.where(i > 0, p_ref[...], 0.0), w_ref[...], b_ref[...])


def _dsilu(d, acc):
    sg = _sigmoid(acc)
    return d * sg * (1.0 + acc * (1.0 - sg))


def _conv_fwd(proj, conv_w, conv_b):
    t = proj.shape[0]
    tr = _row_tile(t)

    def body(x_ref, p_ref, w_ref, b_ref, o_ref):
        i = pl.program_id(0)
        acc, _ = _conv_pre(x_ref, p_ref, w_ref, b_ref, i)
        valid = (i * tr + _iota(acc.shape, 0)) >= PADF
        o_ref[...] = jnp.where(valid, acc * _sigmoid(acc), 0.0)

    return pl.pallas_call(
        body, name="conv_fwd",
        out_shape=jax.ShapeDtypeStruct((t, CONV_DIM), F32),
        grid=(t // tr, CONV_DIM // _CONV_TC),
        in_specs=[pl.BlockSpec((tr, _CONV_TC), lambda i, j: (i, _XBC_BLK + j)),
                  pl.BlockSpec((8, _CONV_TC), lambda i, j: (jnp.maximum(i * (tr // 8) - 1, 0), _XBC_BLK + j)),
                  pl.BlockSpec((CONV_K, _CONV_TC), lambda i, j: (0, j)),
                  pl.BlockSpec((1, _CONV_TC), lambda i, j: (0, j))],
        out_specs=pl.BlockSpec((tr, _CONV_TC), lambda i, j: (i, j)),
        compiler_params=_cp("parallel", "parallel"),
    )(proj, proj, conv_w, conv_b)


def _conv_bwd(dxbc, proj, conv_w, conv_b):
    t = proj.shape[0]
    tr = _row_tile(t)
    n_tiles = t // tr
    last8 = t // 8 - 1

    def body(d_ref, dn_ref, x_ref, p_ref, xn_ref, w_ref, b_ref, dx_ref, dw_ref, db_ref):
        i = pl.program_id(1)
        w = w_ref[...]
        b = b_ref[...]
        cur = x_ref[...]
        acc, taps = _conv_taps(cur, jnp.where(i > 0, p_ref[...], 0.0), w, b)
        valid = (i * tr + _iota(acc.shape, 0)) >= PADF
        da = jnp.where(valid, _dsilu(d_ref[...], acc), 0.0)
        acc_n, _ = _conv_taps(xn_ref[...], cur[tr - 8:, :], w, b)
        da_n = jnp.where(i < n_tiles - 1, _dsilu(dn_ref[...], acc_n), 0.0)
        dx = da * w[3:4, :]
        for j in (1, 2, 3):
            dx = dx + _shift_up(da, da_n, j) * w[3 - j:4 - j, :]
        dx_ref[...] = dx.astype(BF16)
        dw = jnp.concatenate([jnp.sum(da * taps[3 - k], axis=0, keepdims=True) for k in range(CONV_K)], axis=0)
        db = jnp.sum(da, axis=0, keepdims=True)

        @pl.when(i == 0)
        def _():
            dw_ref[...] = dw
            db_ref[...] = db

        @pl.when(i > 0)
        def _():
            dw_ref[...] += dw
            db_ref[...] += db

    nxt8 = lambda i: jnp.minimum((i + 1) * (tr // 8), last8)
    return pl.pallas_call(
        body, name="conv_bwd",
        out_shape=(jax.ShapeDtypeStruct((t, CONV_DIM), BF16),
                   jax.ShapeDtypeStruct((CONV_K, CONV_DIM), F32),
                   jax.ShapeDtypeStruct((1, CONV_DIM), F32)),
        grid=(CONV_DIM // _CONV_TC, n_tiles),
        in_specs=[pl.BlockSpec((tr, _CONV_TC), lambda j, i: (i, j)),
                  pl.BlockSpec((8, _CONV_TC), lambda j, i: (nxt8(i), j)),
                  pl.BlockSpec((tr, _CONV_TC), lambda j, i: (i, _XBC_BLK + j)),
                  pl.BlockSpec((8, _CONV_TC), lambda j, i: (jnp.maximum(i * (tr // 8) - 1, 0), _XBC_BLK + j)),
                  pl.BlockSpec((8, _CONV_TC), lambda j, i: (nxt8(i), _XBC_BLK + j)),
                  pl.BlockSpec((CONV_K, _CONV_TC), lambda j, i: (0, j)),
                  pl.BlockSpec((1, _CONV_TC), lambda j, i: (0, j))],
        out_specs=(pl.BlockSpec((tr, _CONV_TC), lambda j, i: (i, j)),
                   pl.BlockSpec((CONV_K, _CONV_TC), lambda j, i: (0, j)),
                   pl.BlockSpec((1, _CONV_TC), lambda j, i: (0, j))),
        compiler_params=_cp("parallel", "arbitrary"),
    )(dxbc, dxbc, proj, proj, proj, conv_w, conv_b)


_GW = D_SSD // G_SSD


def _ssd_prelude(dt_ref, a_ref, e_scr, es_scr, dte_scr):
    r0 = _iota((CHUNK, CHUNK), 0)
    r1 = _iota((CHUNK, CHUNK), 1)
    dt = jnp.where(r1 < H_SSD, dt_ref[...], 0.0)
    adt = dt * a_ref[...]
    acs = _dot_exact((r0 >= r1).astype(F32), adt)
    acs_t = acs.T
    alast = acs[CHUNK - 1:CHUNK, :]
    exp_a = jnp.exp(acs)
    dec_s = jnp.exp(alast - acs)
    lo = r1 < 64
    for j in range(H_SSD // 2):
        sl = slice(CHUNK * j, CHUNK * (j + 1))
        e_scr[:, sl] = jnp.where(lo, exp_a[:, 2 * j:2 * j + 1], exp_a[:, 2 * j + 1:2 * j + 2])
        es_scr[:, sl] = jnp.where(lo, dec_s[:, 2 * j:2 * j + 1], dec_s[:, 2 * j + 1:2 * j + 2])
        dte_scr[:, sl] = jnp.where(lo, dt[:, 2 * j:2 * j + 1], dt[:, 2 * j + 1:2 * j + 2])
    return dt, acs, acs_t, r0, r1, lo


def _chunk_decay_rows(acs_t, g):
    cd_t = jnp.exp(acs_t[:, CHUNK - 1:CHUNK])
    return jnp.concatenate(
        [jnp.broadcast_to(cd_t[8 * g + hh:8 * g + hh + 1, :], (64, N_STATE)) for hh in range(8)], axis=0)


def _ssd_fwd(xbc, dtlf, a_row, dsk_row):
    t = xbc.shape[0]
    nc = t // CHUNK

    def body(xs_ref, b_ref, c_ref, dt_ref, a_ref, dsk_ref, y_ref, hin_ref, h_scr, e_scr, es_scr, dte_scr):
        c = pl.program_id(0)

        @pl.when(c == 0)
        def _():
            h_scr[...] = jnp.zeros_like(h_scr)

        dt, acs, acs_t, r0, r1, lo = _ssd_prelude(dt_ref, a_ref, e_scr, es_scr, dte_scr)
        causal = r0 >= r1
        for g in range(G_SSD):
            gs = slice(_GW * g, _GW * (g + 1))
            bg = b_ref[:, N_STATE * g:N_STATE * (g + 1)].astype(BF16)
            cg = c_ref[:, N_STATE * g:N_STATE * (g + 1)].astype(BF16)
            cb = _dot(cg, bg, _NT)
            hg = h_scr[gs, :]
            hin_ref[0, gs, :] = hg
            xg = xs_ref[:, gs] * dte_scr[:, gs]
            yoff = _dot(cg, hg.astype(BF16), _NT) * e_scr[:, gs]
            st = _dot((xg * es_scr[:, gs]).astype(BF16), bg, _TN)
            h_scr[gs, :] = hg * _chunk_decay_rows(acs_t, g) + st
            for jj in range(4):
                j = 4 * g + jj
                sl = slice(CHUNK * j, CHUNK * (j + 1))
                xp = xg[:, CHUNK * jj:CHUNK * (jj + 1)]
                acc = yoff[:, CHUNK * jj:CHUNK * (jj + 1)] + dsk_ref[:, sl] * xs_ref[:, sl]
                for hh in range(2):
                    h = 2 * j + hh
                    seg = acs[:, h:h + 1] - acs_t[h:h + 1, :]
                    lm = jnp.exp(jnp.where(causal, seg, -1e30))
                    m = (cb * lm).astype(BF16)
                    xh = jnp.where(lo if hh == 0 else ~lo, xp, 0.0).astype(BF16)
                    acc = acc + _dot(m, xh)
                y_ref[:, sl] = acc

    return pl.pallas_call(
        body, name="ssd_fwd",
        out_shape=(jax.ShapeDtypeStruct((t, D_SSD), F32), jax.ShapeDtypeStruct((nc, D_SSD, N_STATE), F32)),
        grid=(nc,),
        in_specs=[pl.BlockSpec((CHUNK, D_SSD), lambda c: (c, 0)),
                  pl.BlockSpec((CHUNK, _GW), lambda c: (c, 4)),
                  pl.BlockSpec((CHUNK, _GW), lambda c: (c, 5)),
                  pl.BlockSpec((CHUNK, N_SMALL), lambda c: (c, 0)),
                  pl.BlockSpec((1, N_SMALL), lambda c: (0, 0)),
                  pl.BlockSpec((1, D_SSD), lambda c: (0, 0))],
        out_specs=(pl.BlockSpec((CHUNK, D_SSD), lambda c: (c, 0)),
                   pl.BlockSpec((1, D_SSD, N_STATE), lambda c: (c, 0, 0))),
        scratch_shapes=[pltpu.VMEM((D_SSD, N_STATE), F32)] + [pltpu.VMEM((CHUNK, D_SSD), F32)] * 3,
        compiler_params=_cp("arbitrary"),
    )(xbc, xbc, xbc, dtlf, a_row, dsk_row)


def _ssd_bwd(xbc, dtlf, a_row, dsk_row, hin, dy):
    t = xbc.shape[0]
    nc = t // CHUNK

    def body(xs_ref, b_ref, c_ref, dt_ref, a_ref, dsk_ref, hin_ref, dy_ref,
             dxbc_ref, ddt_ref, da_ref, ddsk_ref, dh_scr, e_scr, es_scr, dte_scr, dx_scr, whi_scr, wlo_scr):
        step = pl.program_id(0)

        @pl.when(step == 0)
        def _():
            dh_scr[...] = jnp.zeros_like(dh_scr)
            da_ref[...] = jnp.zeros_like(da_ref)
            ddsk_ref[...] = jnp.zeros_like(ddsk_ref)

        dt, acs, acs_t, r0, r1, lo = _ssd_prelude(dt_ref, a_ref, e_scr, es_scr, dte_scr)
        causal = r0 >= r1
        lane_row = _iota((1, CHUNK), 1)
        dacs = jnp.zeros((CHUNK, CHUNK), F32)
        dacs_t = jnp.zeros((CHUNK, CHUNK), F32)
        dalast = jnp.zeros((1, CHUNK), F32)
        ddt_dir = jnp.zeros((CHUNK, CHUNK), F32)
        ddsk_ref[...] += jnp.sum(dy_ref[...] * xs_ref[...], axis=0, keepdims=True)

        def head_sums(z, pick):
            hi = z.astype(BF16)
            return _dot(hi, pick) + _dot((z - hi.astype(F32)).astype(BF16), pick)

        for g in range(G_SSD):
            gs = slice(_GW * g, _GW * (g + 1))
            pick = (jnp.right_shift(_iota((_GW, CHUNK), 0), 6) + 8 * g == _iota((_GW, CHUNK), 1)).astype(BF16)
            bg = b_ref[:, N_STATE * g:N_STATE * (g + 1)].astype(BF16)
            cg = c_ref[:, N_STATE * g:N_STATE * (g + 1)].astype(BF16)
            cb = _dot(cg, bg, _NT)
            hg = hin_ref[0, gs, :]
            hgb = hg.astype(BF16)
            dhn = dh_scr[gs, :]
            dhnb = dhn.astype(BF16)
            esg = es_scr[:, gs]
            dyg = dy_ref[:, gs]
            xsg = xs_ref[:, gs]
            xg = xsg * dte_scr[:, gs]
            dyeb = (dyg * e_scr[:, gs]).astype(BF16)
            dc = _dot(dyeb, hgb)
            dh_y = _dot(dyeb, cg, _TN)
            dxs = _dot(bg, dhnb, _NT) * esg
            db = _dot((xg * esg).astype(BF16), dhnb)
            cd = _chunk_decay_rows(acs_t, g)
            dh_scr[gs, :] = dhn * cd + dh_y
            end_state = head_sums(jnp.broadcast_to(jnp.sum(xg * dxs, axis=0, keepdims=True), (8, _GW)), pick)[0:1, :]
            carried = dhn * hg * cd
            per_head = jnp.concatenate([jnp.sum(carried[64 * hh:64 * hh + 64, :], axis=0, keepdims=True)
                                        for hh in range(8)], axis=0)
            per_head = jnp.sum(per_head, axis=1, keepdims=True)
            for hh in range(8):
                end_state = end_state + jnp.where(lane_row == 8 * g + hh, per_head[hh:hh + 1, :], 0.0)
            dalast = dalast + end_state
            dcb = jnp.zeros((CHUNK, CHUNK), F32)
            for jj in range(4):
                j = 4 * g + jj
                sl = slice(CHUNK * j, CHUNK * (j + 1))
                ps = slice(CHUNK * jj, CHUNK * (jj + 1))
                xpb = xg[:, ps].astype(BF16)
                dyp = dyg[:, ps]
                dxp = dxs[:, ps]
                for hh in range(2):
                    h = 2 * j + hh
                    ws = slice(CHUNK * (2 * jj + hh), CHUNK * (2 * jj + hh + 1))
                    seg = acs[:, h:h + 1] - acs_t[h:h + 1, :]
                    lm = jnp.exp(jnp.where(causal, seg, -1e30))
                    mf = cb * lm
                    dyh = jnp.where(lo if hh == 0 else ~lo, dyp, 0.0).astype(BF16)
                    gm = _dot(dyh, xpb, _NT)
                    dcb = dcb + gm * lm
                    w = gm * mf
                    whi = w.astype(BF16)
                    whi_scr[:, ws] = whi
                    wlo_scr[:, ws] = (w - whi.astype(F32)).astype(BF16)
                    dacs_t = dacs_t - jnp.where(r0 == h, jnp.sum(w, axis=0, keepdims=True), 0.0)
                    dxp = dxp + _dot(mf.astype(BF16), dyh, _TN)
                dx_scr[:, sl] = dxp
            dxg = dx_scr[:, gs]
            pick_w = (jnp.right_shift(_iota((8 * CHUNK, CHUNK), 0), 7) + 8 * g == _iota((8 * CHUNK, CHUNK), 1)).astype(BF16)
            ch = _dot(cg, hgb, _NT)
            dacs = (dacs + _dot(whi_scr[...], pick_w) + _dot(wlo_scr[...], pick_w)
                    + head_sums(dyg * e_scr[:, gs] * ch - xg * dxs, pick))
            ddt_dir = ddt_dir + head_sums(dxg * xsg, pick)
            dcbb = dcb.astype(BF16)
            dxbc_ref[:, D_SSD + N_STATE * g:D_SSD + N_STATE * (g + 1)] = db + _dot(dcbb, cg, _TN)
            dxbc_ref[:, D_SSD + _GW + N_STATE * g:D_SSD + _GW + N_STATE * (g + 1)] = dc + _dot(dcbb, bg)
        dxbc_ref[:, 0:D_SSD] = dx_scr[...] * dte_scr[...] + dsk_ref[...] * dy_ref[...]
        dacs = dacs + dacs_t.T + jnp.where(r0 == CHUNK - 1, dalast, 0.0)
        dadt = _dot_exact((r1 >= r0).astype(F32), dacs)
        ddt_ref[...] = dadt * a_ref[...] + ddt_dir
        da_ref[...] += jnp.sum(dadt * dt, axis=0, keepdims=True)

    rev = lambda s: (nc - 1 - s, 0)
    return pl.pallas_call(
        body, name="ssd_bwd",
        out_shape=(jax.ShapeDtypeStruct((t, CONV_DIM), F32), jax.ShapeDtypeStruct((t, N_SMALL), F32),
                   jax.ShapeDtypeStruct((1, N_SMALL), F32), jax.ShapeDtypeStruct((1, D_SSD), F32)),
        grid=(nc,),
        in_specs=[pl.BlockSpec((CHUNK, D_SSD), rev),
                  pl.BlockSpec((CHUNK, _GW), lambda s: (nc - 1 - s, 4)),
                  pl.BlockSpec((CHUNK, _GW), lambda s: (nc - 1 - s, 5)),
                  pl.BlockSpec((CHUNK, N_SMALL), rev),
                  pl.BlockSpec((1, N_SMALL), lambda s: (0, 0)),
                  pl.BlockSpec((1, D_SSD), lambda s: (0, 0)),
                  pl.BlockSpec((1, D_SSD, N_STATE), lambda s: (nc - 1 - s, 0, 0)),
                  pl.BlockSpec((CHUNK, D_SSD), rev)],
        out_specs=(pl.BlockSpec((CHUNK, CONV_DIM), rev),
                   pl.BlockSpec((CHUNK, N_SMALL), rev),
                   pl.BlockSpec((1, N_SMALL), lambda s: (0, 0)),
                   pl.BlockSpec((1, D_SSD), lambda s: (0, 0))),
        scratch_shapes=([pltpu.VMEM((D_SSD, N_STATE), F32)] + [pltpu.VMEM((CHUNK, D_SSD), F32)] * 4
                        + [pltpu.VMEM((CHUNK, 8 * CHUNK), BF16)] * 2),
        compiler_params=_cp("arbitrary"),
    )(xbc, xbc, xbc, dtlf, a_row, dsk_row, hin, dy)


_NPAIR = H_ATT // 2
_QB, _KB, _VB = C_Q // 128, C_K // 128, C_V // 128
_SCALE = 1.0 / math.sqrt(64.0)


def _attn_blocks(t):
    return _tile(t, (1408, 384, 256, 128)), _tile(t, (384, 128))


def _split3(c):
    hi = c.astype(BF16).astype(F32)
    rest = c - hi
    mid = rest.astype(BF16).astype(F32)
    return hi, mid, rest - mid


def _head_lanes(lane, hh):
    return (lane < 64, 64) if hh == 0 else (lane >= 64, 0)


def _q_operand(q, cq, lane, hh):
    sel, first = _head_lanes(lane, hh)
    out = jnp.where(sel, q, 0.0)
    for n, col in enumerate(_split3(cq) + (1.0, 1.0, 1.0)):
        out = jnp.where(lane == first + n, col, out)
    return out.astype(BF16)


def _k_operand(k, ck, lane, hh):
    sel, first = _head_lanes(lane, hh)
    hi, mid, lo = _split3(ck)
    out = jnp.where(sel, k, 0.0)
    for n, col in enumerate((1.0, 1.0, 1.0, -hi, -mid, -lo)):
        out = jnp.where(lane == first + n, col, out)
    return out.astype(BF16)


def _needs_mask(i, kk, bq, bk):
    return kk * bk + bk - 1 > i * bq


_C_FILLER = 2.0 ** 30


def _attn_fwd(proj, c_col):
    t = proj.shape[0]
    bq, bk = _attn_blocks(t)
    nq, nk = t // bq, t // bk
    rs = 16

    def last_kv(i):
        return (i * bq + bq - 1) // bk

    def body(q_ref, k_ref, v_ref, cq_ref, ck_ref, o_ref, lse_ref, p_ref, mrun_ref, qs_scr, s_scr, m_scr, acc_scr):
        i = pl.program_id(1)
        kk = pl.program_id(2)
        lane_q = _iota((bq, 128), 1)

        @pl.when(kk == 0)
        def _():
            m_scr[...] = jnp.full_like(m_scr, -1e30)
            acc_scr[...] = jnp.zeros_like(acc_scr)
            q = q_ref[...] * _SCALE
            cq = cq_ref[0]
            for hh in range(2):
                qs_scr[hh] = _q_operand(q, cq[:, hh:hh + 1], lane_q, hh)

        def step(masked):
            lane_k = _iota((bk, 128), 1)
            k = k_ref[...]
            v = v_ref[...]
            ck = ck_ref[0]
            ahead = _iota((rs, bq), 0) - _iota((rs, bq), 1)
            vss = []
            for hh in range(2):
                sel, first = _head_lanes(lane_k, hh)
                ks = _k_operand(k, ck[:, hh:hh + 1], lane_k, hh)
                vss.append(jnp.where(sel, v, jnp.where(lane_k == first, 1.0, 0.0)).astype(BF16))
                s_scr[hh] = _dot(ks, qs_scr[hh], _NT)
            for hh in range(2):
                vs = vss[hh]

                def block_max(r, mx):
                    rows = pl.ds(pl.multiple_of(r * rs, rs), rs)
                    s = s_scr[hh, rows, :]
                    if masked:
                        s = jnp.where(ahead <= i * bq - kk * bk - r * rs, s, -1e30)
                        s_scr[hh, rows, :] = s
                    return jnp.maximum(mx, s)

                mx = lax.fori_loop(0, bk // rs, block_max, jnp.full((rs, bq), -1e30, F32), unroll=True)
                m_old = m_scr[hh]
                m_new = jnp.maximum(m_old, jnp.max(mx, axis=0, keepdims=True))
                m_scr[hh] = m_new
                mrun_ref[0, hh:hh + 1, :] = m_new

                def probs(r, carry):
                    rows = pl.ds(pl.multiple_of(r * rs, rs), rs)
                    p_ref[0, hh, rows, :] = jnp.exp(s_scr[hh, rows, :] - m_new).astype(BF16)
                    return carry

                lax.fori_loop(0, bk // rs, probs, 0, unroll=True)
                acc_scr[hh] = acc_scr[hh] * jnp.exp(m_old - m_new) + _dot(vs, p_ref[0, hh], _TN)

        active = kk <= last_kv(i)
        masked = _needs_mask(i, kk, bq, bk)

        @pl.when(active & masked)
        def _():
            step(True)

        @pl.when(active & jnp.logical_not(masked))
        def _():
            step(False)

        @pl.when(kk == nk - 1)
        def _():
            a = acc_scr[0]
            b = acc_scr[1]
            la = a[64:65, :]
            lb = b[0:1, :]
            o_ref[...] = jnp.where(lane_q < 64, (a / la).T, (b / lb).T)
            lse_ref[0] = jnp.concatenate([m_scr[0] + jnp.log(la), m_scr[1] + jnp.log(lb)], axis=0)

    kvi = lambda i, kk: jnp.minimum(kk, last_kv(i))
    kv = lambda off: pl.BlockSpec((bk, 128), lambda j, i, kk: (kvi(i, kk), off + j))
    blk = lambda j, i, kk: (j * nq + i) * nk + kvi(i, kk)
    return pl.pallas_call(
        body, name="attn_fwd",
        out_shape=(jax.ShapeDtypeStruct((t, D_ATT), F32), jax.ShapeDtypeStruct((_NPAIR, 2, t), F32),
                   jax.ShapeDtypeStruct((_NPAIR * nq * nk, 2, bk, bq), BF16),
                   jax.ShapeDtypeStruct((_NPAIR * nq * nk, 2, bq), F32)),
        grid=(_NPAIR, nq, nk),
        in_specs=[pl.BlockSpec((bq, 128), lambda j, i, kk: (i, _QB + j)),
                  kv(_KB), kv(_VB),
                  pl.BlockSpec((1, bq, 2), lambda j, i, kk: (j, i, 0)),
                  pl.BlockSpec((1, bk, 2), lambda j, i, kk: (j, kvi(i, kk), 0))],
        out_specs=(pl.BlockSpec((bq, 128), lambda j, i, kk: (i, j)),
                   pl.BlockSpec((1, 2, bq), lambda j, i, kk: (j, 0, i)),
                   pl.BlockSpec((1, 2, bk, bq), lambda j, i, kk: (blk(j, i, kk), 0, 0, 0)),
                   pl.BlockSpec((1, 2, bq), lambda j, i, kk: (blk(j, i, kk), 0, 0))),
        scratch_shapes=[pltpu.VMEM((2, bq, 128), BF16), pltpu.VMEM((2, bk, bq), F32),
                        pltpu.VMEM((2, 1, bq), F32), pltpu.VMEM((2, 128, bq), F32)],
        compiler_params=_cp("parallel", "parallel", "arbitrary"),
    )(proj, proj, proj, c_col, c_col)


def _attn_delta(do, o):
    t = do.shape[0]
    tm = _row_tile(t)

    def body(do_ref, o_ref, d_ref):
        pick = (jnp.right_shift(_iota((D_ATT, 128), 0), 6) == _iota((D_ATT, 128), 1)).astype(F32)
        d_ref[...] = _dot_exact(do_ref[...] * o_ref[...], pick)

    row = pl.BlockSpec((tm, D_ATT), lambda i: (i, 0))
    return pl.pallas_call(
        body, name="attn_delta",
        out_shape=jax.ShapeDtypeStruct((t, 128), F32),
        grid=(t // tm,), in_specs=[row, row], out_specs=pl.BlockSpec((tm, 128), lambda i: (i, 0)),
        compiler_params=_cp("parallel"),
    )(do, o)


def _attn_bwd(proj, c_col, lse_row, dl_row, do, p_blocks, m_run, exchange=None):
    t = proj.shape[0]
    bq, bk = _attn_blocks(t)
    nq, nk = t // bq, t // bk
    rs = 16

    def first_q(kk):
        return (kk * bk) // bq

    def body(q_ref, k_ref, v_ref, cq_ref, ck_ref, lse_ref, dl_ref, do_ref, pblk_ref, mrun_ref,
             dq_ref, dk_ref, dv_ref, dck_ref, dcq_ref,
             qs_scr, doh_scr, ks_scr, dp_scr, p_scr, ds_scr, dq_scr, dk_scr, dv_scr):
        kk = pl.program_id(1)
        i = pl.program_id(2)
        lane_q = _iota((bq, 128), 1)
        lane_k = _iota((bk, 128), 1)
        qrows = pl.ds(pl.multiple_of(i * bq, 128), bq)

        @pl.when(kk == 0)
        def _():
            q = q_ref[...] * _SCALE
            cq = cq_ref[0]
            do_ = do_ref[...]
            for hh in range(2):
                qs_scr[hh, qrows, :] = _q_operand(q, cq[:, hh:hh + 1], lane_q, hh)
                doh_scr[hh, qrows, :] = jnp.where(_head_lanes(lane_q, hh)[0], do_, 0.0).astype(BF16)
                dq_scr[hh, i] = jnp.zeros((128, bq), F32)

        @pl.when(i == 0)
        def _():
            dk_scr[...] = jnp.zeros_like(dk_scr)
            dv_scr[...] = jnp.zeros_like(dv_scr)
            k = k_ref[...]
            ck = ck_ref[0]
            for hh in range(2):
                ks_scr[hh] = _k_operand(k, ck[:, hh:hh + 1], lane_k, hh)

        @pl.when(i >= first_q(kk))
        def _():
            v16 = v_ref[...].astype(BF16)
            dl = dl_ref[0]
            rescale = jnp.exp(mrun_ref[0] - lse_ref[0])
            for hh in range(2):
                dp_scr[hh] = _dot(v16, doh_scr[hh, qrows, :], _NT)
            for hh in range(2):
                qs = qs_scr[hh, qrows, :]
                doh = doh_scr[hh, qrows, :]

                def strip(r, carry):
                    rows = pl.ds(pl.multiple_of(r * rs, rs), rs)
                    p = pblk_ref[0, hh, rows, :].astype(F32) * rescale[hh:hh + 1, :]
                    p_scr[hh, rows, :] = p.astype(BF16)
                    ds_scr[hh, rows, :] = (p * (dp_scr[hh, rows, :] - dl[hh:hh + 1, :])).astype(BF16)
                    return carry

                lax.fori_loop(0, bk // rs, strip, 0, unroll=True)
                dv_scr[...] += _dot(p_scr[hh], doh)
                dk_scr[hh] += _dot(ds_scr[hh], qs)
                dq_scr[hh, i] += _dot(ks_scr[hh], ds_scr[hh], _TN)

        @pl.when(i == nq - 1)
        def _():
            dka = dk_scr[0]
            dkb = dk_scr[1]
            dk_ref[...] = jnp.where(lane_k < 64, dka, dkb).astype(BF16)
            dv_ref[...] = dv_scr[...].astype(BF16)
            dck_ref[0] = -jnp.where(_iota((bk, 2), 1) == 0, dka[:, 67:68], dkb[:, 3:4])

        @pl.when((kk == nk - 1) & (i == nq - 1))
        def _():
            for ii in range(nq):
                cols = slice(ii * bq, (ii + 1) * bq)
                dqa = dq_scr[0, ii]
                dqb = dq_scr[1, ii]
                dq_ref[cols, :] = (jnp.where(lane_q < 64, dqa.T, dqb.T) * _SCALE).astype(BF16)
                dcq_ref[0, :, cols] = jnp.concatenate([dqa[64:65, :], dqb[0:1, :]], axis=0)

    qi = lambda kk, i: jnp.where(kk == 0, i, nq - 1)
    qspec = lambda off: pl.BlockSpec((bq, 128), lambda j, kk, i: (qi(kk, i), off + j))
    kspec = lambda off: pl.BlockSpec((bk, 128), lambda j, kk, i: (kk, off + j))
    rowspec = pl.BlockSpec((1, 2, bq), lambda j, kk, i: (j, 0, jnp.maximum(i, first_q(kk))))
    blk = lambda j, kk, i: (j * nq + jnp.maximum(i, first_q(kk))) * nk + kk
    return _hosted_call(
        body, name="attn_bwd",
        out_shape=(jax.ShapeDtypeStruct((t, D_ATT), BF16), jax.ShapeDtypeStruct((t, D_ATT), BF16),
                   jax.ShapeDtypeStruct((t, D_ATT), BF16), jax.ShapeDtypeStruct((_NPAIR, t, 2), F32),
                   jax.ShapeDtypeStruct((_NPAIR, 2, t), F32)),
        grid=(_NPAIR, nk, nq),
        in_specs=[qspec(_QB), kspec(_KB), kspec(_VB),
                  pl.BlockSpec((1, bq, 2), lambda j, kk, i: (j, qi(kk, i), 0)),
                  pl.BlockSpec((1, bk, 2), lambda j, kk, i: (j, kk, 0)),
                  rowspec, rowspec, qspec(0),
                  pl.BlockSpec((1, 2, bk, bq), lambda j, kk, i: (blk(j, kk, i), 0, 0, 0)),
                  pl.BlockSpec((1, 2, bq), lambda j, kk, i: (blk(j, kk, i), 0, 0))],
        out_specs=(pl.BlockSpec((t, 128), lambda j, kk, i: (0, j)),
                   pl.BlockSpec((bk, 128), lambda j, kk, i: (kk, j)),
                   pl.BlockSpec((bk, 128), lambda j, kk, i: (kk, j)),
                   pl.BlockSpec((1, bk, 2), lambda j, kk, i: (j, kk, 0)),
                   pl.BlockSpec((1, 2, t), lambda j, kk, i: (j, 0, 0))),
        scratch_shapes=[pltpu.VMEM((2, t, 128), BF16), pltpu.VMEM((2, t, 128), BF16), pltpu.VMEM((2, bk, 128), BF16),
                        pltpu.VMEM((2, bk, bq), F32),
                        pltpu.VMEM((2, bk, bq), BF16), pltpu.VMEM((2, bk, bq), BF16),
                        pltpu.VMEM((2, nq, 128, bq), F32), pltpu.VMEM((2, bk, 128), F32), pltpu.VMEM((bk, 128), F32)],
        operands=(proj, proj, proj, c_col, c_col, lse_row, dl_row, do, p_blocks, m_run),
        semantics=("parallel", "arbitrary", "arbitrary"), exchange=exchange)


def _premerge_fwd(y, o, proj, gamma):
    t = y.shape[0]
    tm = _row_tile_wide(t)

    def body(y_ref, z_ref, o_ref, za_ref, g_ref, ys_ref, ya_ref):
        z = z_ref[...]
        u = y_ref[...] * (z * _sigmoid(z))
        for g in range(G_SSD):
            gs = slice(_GW * g, _GW * (g + 1))
            ug = u[:, gs]
            r = lax.rsqrt(jnp.mean(ug * ug, axis=-1, keepdims=True) + EPS)
            ys_ref[:, gs] = (ug * r * g_ref[:, gs]).astype(BF16)
        za = za_ref[...]
        ya_ref[...] = (o_ref[...] * (za * _sigmoid(za))).astype(BF16)

    return pl.pallas_call(
        body, name="premerge_fwd",
        out_shape=(jax.ShapeDtypeStruct((t, D_SSD), BF16), jax.ShapeDtypeStruct((t, D_ATT), BF16)),
        grid=(t // tm,),
        in_specs=[pl.BlockSpec((tm, D_SSD), lambda i: (i, 0)),
                  pl.BlockSpec((tm, D_SSD), lambda i: (i, C_Z // D_SSD)),
                  pl.BlockSpec((tm, D_ATT), lambda i: (i, 0)),
                  pl.BlockSpec((tm, D_ATT), lambda i: (i, C_ZA // D_ATT)),
                  pl.BlockSpec((1, D_SSD), lambda i: (0, 0))],
        out_specs=(pl.BlockSpec((tm, D_SSD), lambda i: (i, 0)), pl.BlockSpec((tm, D_ATT), lambda i: (i, 0))),
        compiler_params=_cp("parallel"),
    )(y, proj, o, proj, gamma)


def _premerge_bwd(dys, dya, y, o, proj, gamma, exchange=None):
    t = y.shape[0]
    tm = _row_tile_wide(t)

    def body(dys_ref, dya_ref, y_ref, z_ref, o_ref, za_ref, g_ref, dy_ref, dz_ref, do_ref, dza_ref, dg_ref):
        i = pl.program_id(0)
        z = z_ref[...]
        sz = _sigmoid(z)
        silu = z * sz
        dsilu = sz * (1.0 + z * (1.0 - sz))
        yv = y_ref[...]
        u = yv * silu
        parts = []
        for g in range(G_SSD):
            gs = slice(_GW * g, _GW * (g + 1))
            ug = u[:, gs]
            r = lax.rsqrt(jnp.mean(ug * ug, axis=-1, keepdims=True) + EPS)
            n = ug * r
            dout = dys_ref[:, gs]
            dn = dout * g_ref[:, gs]
            du = r * (dn - n * jnp.mean(dn * n, axis=-1, keepdims=True))
            dy_ref[:, gs] = du * silu[:, gs]
            dz_ref[:, gs] = (du * yv[:, gs] * dsilu[:, gs]).astype(BF16)
            parts.append(jnp.sum(dout * n, axis=0, keepdims=True))
        dg = jnp.concatenate(parts, axis=1)
        za = za_ref[...]
        sa = _sigmoid(za)
        dya_ = dya_ref[...]
        do_ref[...] = dya_ * (za * sa)
        dza_ref[...] = (dya_ * o_ref[...] * (sa * (1.0 + za * (1.0 - sa)))).astype(BF16)

        @pl.when(i == 0)
        def _():
            dg_ref[...] = dg

        @pl.when(i > 0)
        def _():
            dg_ref[...] += dg

    ssd = pl.BlockSpec((tm, D_SSD), lambda i: (i, 0))
    att = pl.BlockSpec((tm, D_ATT), lambda i: (i, 0))
    vec = pl.BlockSpec((1, D_SSD), lambda i: (0, 0))
    return _hosted_call(
        body, name="premerge_bwd",
        out_shape=(jax.ShapeDtypeStruct((t, D_SSD), F32), jax.ShapeDtypeStruct((t, D_SSD), BF16),
                   jax.ShapeDtypeStruct((t, D_ATT), F32), jax.ShapeDtypeStruct((t, D_ATT), BF16),
                   jax.ShapeDtypeStruct((1, D_SSD), F32)),
        grid=(t // tm,),
        in_specs=[ssd, att, ssd, pl.BlockSpec((tm, D_SSD), lambda i: (i, C_Z // D_SSD)), att,
                  pl.BlockSpec((tm, D_ATT), lambda i: (i, C_ZA // D_ATT)), vec],
        out_specs=(ssd, ssd, att, att, vec),
        scratch_shapes=[],
        operands=(dys, dya, y, proj, o, proj, gamma), semantics=("arbitrary",), exchange=exchange)


_G_BLK = C_G // D_MODEL


def _merge_fwd(a, b, proj, gate_bias):
    t = a.shape[0]
    tm = _row_tile(t)

    def body(a_ref, b_ref, gs_ref, ga_ref, bias_ref, m_ref):
        g_ssd = _sigmoid(gs_ref[...] + bias_ref[:, 0:D_MODEL])
        g_att = _sigmoid(ga_ref[...] + bias_ref[:, D_MODEL:2 * D_MODEL])
        m_ref[...] = (g_ssd * a_ref[...] + g_att * b_ref[...]).astype(BF16)

    row = pl.BlockSpec((tm, D_MODEL), lambda i: (i, 0))
    return pl.pallas_call(
        body, name="merge_fwd",
        out_shape=jax.ShapeDtypeStruct((t, D_MODEL), BF16),
        grid=(t // tm,),
        in_specs=[row, row,
                  pl.BlockSpec((tm, D_MODEL), lambda i: (i, _G_BLK)),
                  pl.BlockSpec((tm, D_MODEL), lambda i: (i, _G_BLK + 1)),
                  pl.BlockSpec((1, 2 * D_MODEL), lambda i: (0, 0))],
        out_specs=row,
        compiler_params=_cp("parallel"),
    )(a, b, proj, proj, gate_bias)


def _merge_bwd(dm, a, b, proj, gate_bias):
    t = a.shape[0]
    tm = _row_tile(t)

    def body(dm_ref, a_ref, b_ref, gs_ref, ga_ref, bias_ref, da_ref, db_ref, dg_ref, dbias_ref):
        i = pl.program_id(0)
        dm_ = dm_ref[...]
        g_ssd = _sigmoid(gs_ref[...] + bias_ref[:, 0:D_MODEL])
        g_att = _sigmoid(ga_ref[...] + bias_ref[:, D_MODEL:2 * D_MODEL])
        da_ref[...] = (dm_ * g_ssd).astype(BF16)
        db_ref[...] = (dm_ * g_att).astype(BF16)
        dgs = dm_ * a_ref[...] * g_ssd * (1.0 - g_ssd)
        dga = dm_ * b_ref[...] * g_att * (1.0 - g_att)
        dg_ref[:, 0:D_MODEL] = dgs.astype(BF16)
        dg_ref[:, D_MODEL:2 * D_MODEL] = dga.astype(BF16)
        part = jnp.concatenate([jnp.sum(dgs, axis=0, keepdims=True), jnp.sum(dga, axis=0, keepdims=True)], axis=1)

        @pl.when(i == 0)
        def _():
            dbias_ref[...] = part

        @pl.when(i > 0)
        def _():
            dbias_ref[...] += part

    row = pl.BlockSpec((tm, D_MODEL), lambda i: (i, 0))
    wide = pl.BlockSpec((tm, 2 * D_MODEL), lambda i: (i, 0))
    vec = pl.BlockSpec((1, 2 * D_MODEL), lambda i: (0, 0))
    return pl.pallas_call(
        body, name="merge_bwd",
        out_shape=(jax.ShapeDtypeStruct((t, D_MODEL), BF16), jax.ShapeDtypeStruct((t, D_MODEL), BF16),
                   jax.ShapeDtypeStruct((t, 2 * D_MODEL), BF16), jax.ShapeDtypeStruct((1, 2 * D_MODEL), F32)),
        grid=(t // tm,),
        in_specs=[row, row, row,
                  pl.BlockSpec((tm, D_MODEL), lambda i: (i, _G_BLK)),
                  pl.BlockSpec((tm, D_MODEL), lambda i: (i, _G_BLK + 1)), vec],
        out_specs=(row, row, wide, vec),
        compiler_params=_cp("arbitrary"),
    )(dm, a, b, proj, proj, gate_bias)


def _post(o2, h, target, g):
    t = o2.shape[0]
    nc = t // CHUNK

    def body(o_ref, h_ref, t_ref, g_ref, dy_ref, do_ref, dg_ref, loss_ref):
        c = pl.program_id(0)
        x = o_ref[...]
        r = lax.rsqrt(jnp.mean(x * x, axis=-1, keepdims=True) + EPS)
        n = x * r
        y = h_ref[...] + n * g_ref[...]
        diff = jnp.where(c > 0, y - t_ref[...], 0.0)
        dy = diff * (1.0 / D_MODEL)
        dy_ref[...] = dy
        gdy = dy * g_ref[...]
        do_ref[...] = (r * (gdy - n * jnp.mean(gdy * n, axis=-1, keepdims=True))).astype(BF16)
        dg = jnp.sum(dy * n, axis=0, keepdims=True)
        lpart = 0.5 * jnp.sum(jnp.sum(diff * diff, axis=1, keepdims=True), axis=0, keepdims=True) * (1.0 / D_MODEL)
        sel = (_iota((8, 128), 0) == 0) & (_iota((8, 128), 1) == 0)

        @pl.when(c == 0)
        def _():
            dg_ref[...] = dg
            loss_ref[...] = jnp.zeros_like(loss_ref)

        @pl.when(c > 0)
        def _():
            dg_ref[...] += dg
            loss_ref[...] += jnp.where(sel, lpart, 0.0)

    row = pl.BlockSpec((CHUNK, D_MODEL), lambda c: (c, 0))
    vec = pl.BlockSpec((1, D_MODEL), lambda c: (0, 0))
    return pl.pallas_call(
        body, name="post",
        out_shape=(jax.ShapeDtypeStruct((t, D_MODEL), F32), jax.ShapeDtypeStruct((t, D_MODEL), BF16),
                   jax.ShapeDtypeStruct((1, D_MODEL), F32), jax.ShapeDtypeStruct((8, 128), F32)),
        grid=(nc,),
        in_specs=[row, row, pl.BlockSpec((CHUNK, D_MODEL), lambda c: (jnp.maximum(c - 1, 0), 0)), vec],
        out_specs=(row, row, vec, pl.BlockSpec((8, 128), lambda c: (0, 0))),
        compiler_params=_cp("arbitrary"),
    )(o2, h, target, g)


def _mm_tiles(t):
    return _tile(t, (704, 384, 128))


def _local_step(h, target, w_main, w_small, pr_slots, ids, norm_pre, conv_w, conv_b, bias_row, a_row,
                dsk_row, ssd_norm, gate_bias, norm_post):
    t = h.shape[0]
    tm = _mm_tiles(t)
    u = _norm1_fwd(h, norm_pre)
    proj, pr_slots = _matmul(u, w_main, "nt", F32, "inproj", tm, 1024, D_MODEL,
                             exchange=_gather_stage([pr_slots], to_sibling=False))
    small, pr_slots = _matmul(u, w_small, "nt", F32, "inproj_small", tm, N_SMALL, D_MODEL,
                              exchange=_gather_stage([pr_slots], to_sibling=True))
    wps = pr_slots[:, 0:512].reshape(D_SSD, D_MODEL)
    wpa = pr_slots[:, 512:768].reshape(D_ATT, D_MODEL)
    wout = pr_slots[:, 768:1024].reshape(D_MODEL, D_MODEL)
    dtlf = _small_fwd(small, bias_row)
    xbc = _conv_fwd(proj, conv_w, conv_b)
    y, hin = _ssd_fwd(xbc, dtlf, a_row, dsk_row)
    c_tok = dtlf[:, H_SSD:H_SSD + H_ATT]
    c_tok = jnp.where(jnp.arange(t)[:, None] < PADF, _C_FILLER, c_tok)
    c_col = c_tok.reshape(t, _NPAIR, 2).transpose(1, 0, 2)
    o, lse, p_blocks, m_run = _attn_fwd(proj, c_col)
    ys, ya = _premerge_fwd(y, o, proj, ssd_norm)
    a = _matmul(ys, wps, "nn", F32, "proj_ssd", tm, D_MODEL, D_SSD)
    b = _matmul(ya, wpa, "nn", F32, "proj_att", tm, D_MODEL, D_ATT)
    merged = _merge_fwd(a, b, proj, gate_bias)
    o2 = _matmul(merged, wout, "nn", F32, "out_proj", tm, D_MODEL, D_MODEL)
    dy_out, do2, d_norm_post, loss_blk = _post(o2, h, target, norm_post)

    dm = _matmul(do2, wout, "nt", F32, "out_proj_dx", tm, D_MODEL, D_MODEL)
    d_wout = _matmul(merged, do2, "tn", F32, "out_proj_dw", D_MODEL, D_MODEL, tm)
    da, db, dgraw, d_gate_bias = _merge_bwd(dm, a, b, proj, gate_bias)
    dys = _matmul(da, wps, "nt", F32, "proj_ssd_dx", tm, D_SSD, D_MODEL)
    d_wps = _matmul(ys, da, "tn", F32, "proj_ssd_dw", D_SSD, D_MODEL, tm)
    dya = _matmul(db, wpa, "nt", F32, "proj_att_dx", tm, D_ATT, D_MODEL)
    d_wpa = _matmul(ya, db, "tn", F32, "proj_att_dw", D_ATT, D_MODEL, tm)
    g32_pr = jnp.concatenate([d_wps.reshape(4, 512, D_MODEL), d_wpa.reshape(4, 256, D_MODEL),
                              d_wout.reshape(4, 256, D_MODEL)], axis=1)
    dy, dz, do, dza, d_ssd_norm, ra_pr = _premerge_bwd(dys, dya, y, o, proj, ssd_norm,
                                                       exchange=_pair_swap([g32_pr]))
    pb_pr = _add_pair(ids, g32_pr, ra_pr)
    dl_row = _attn_delta(do, o)[:, 0:H_ATT].T.reshape(_NPAIR, 2, t)
    dq, dk, dv, dc_key, dc_qry, rb_pr = _attn_bwd(proj, c_col, lse, dl_row, do, p_blocks, m_run,
                                                  exchange=_chip_exchange([pb_pr]))
    half_pr = _add_chips(ids, g32_pr, ra_pr, rb_pr)
    dxbc, ddt, d_a, d_dsk = _ssd_bwd(xbc, dtlf, a_row, dsk_row, hin, dy)
    dxbc_raw, d_conv_w, d_conv_b = _conv_bwd(dxbc, proj, conv_w, conv_b)
    dc_tok = jnp.transpose(dc_key, (1, 0, 2)).reshape(t, H_ATT) + dc_qry.reshape(H_ATT, t).T
    dsm = ddt + jnp.pad(dc_tok, ((0, 0), (H_SSD, N_SMALL - H_SSD - H_ATT)))
    dsmall, d_bias_row = _small_bwd(dsm, small, bias_row)
    dproj = [dz, dxbc_raw, dza, dq, dk, dv, dgraw]
    return dict(loss_blk=loss_blk, u=u, dy_out=dy_out, dproj=dproj, dsmall=dsmall, half_pr=half_pr,
                d_conv_w=d_conv_w, d_conv_b=d_conv_b,
                d_bias_row=d_bias_row, d_a=d_a, d_dsk=d_dsk, d_ssd_norm=d_ssd_norm,
                d_gate_bias=d_gate_bias, d_norm_post=d_norm_post)


def _to_aligned_rows(slots):
    w = slots.reshape(N_COLS, slots.shape[2])

    def cut(o):
        return w[o[0]:o[0] + o[1]]
    main = jnp.concatenate([cut(O_Z), cut(O_XBC), cut(O_ZA), cut(O_Q), cut(O_K), cut(O_V), cut(O_G)], axis=0)
    pad = jnp.zeros((N_SMALL - H_SSD - H_ATT, w.shape[1]), w.dtype)
    small = jnp.concatenate([cut(O_DT), cut(O_F), pad], axis=0)
    return main, small


def _from_aligned_rows(main, small):
    def cm(c0, n):
        return main[c0:c0 + n]
    flat = jnp.concatenate([cm(C_Z, 2048), cm(C_XBC, 3072), small[0:H_SSD], cm(C_ZA, 1024),
                            cm(C_Q, 1024), cm(C_K, 1024), cm(C_V, 1024), small[H_SSD:H_SSD + H_ATT],
                            cm(C_G, 2048)], axis=0)
    return flat.reshape(4, N_COLS // 4, flat.shape[1])


_MESH = pl.DeviceIdType.MESH
_ANY = pl.BlockSpec(memory_space=pl.ANY)
_VM = pl.BlockSpec(memory_space=pltpu.VMEM)
_HALF = 512
N_DEV = 8


def _coords():
    return lax.axis_index("x"), lax.axis_index("y"), lax.axis_index("c")


def _other_chips(x, y):
    return [(1 - x, y), (x, 1 - y), (1 - x, 1 - y)]


def _half(cc):
    return pl.ds(cc * _HALF, _HALF)


def _gather_shards(slots):
    n = len(slots)

    def body(*refs):
        buf = refs[n:2 * n]
        send_sems, recv_sems = refs[2 * n:]
        x, y, c = _coords()
        chip = 2 * x + y
        sibling = (x, y, 1 - c)
        chips = _other_chips(x, y)

        def copy(i, frm, cc, k, to):
            part = buf[i].at[frm, :, _half(cc)]
            return pltpu.make_async_remote_copy(src_ref=part, dst_ref=part, send_sem=send_sems.at[6 * i + k],
                                                recv_sem=recv_sems.at[6 * i + k], device_id=to, device_id_type=_MESH)

        def chip_of(k):
            return 2 * chips[k][0] + chips[k][1]

        first = [copy(i, chip, c, k, (*chips[k], c)) for k in range(3) for i in range(n)]
        for cp in first:
            cp.start()
        passed = []
        for k in range(3):
            for i in range(n):
                copy(i, chip_of(k), c, k, (*chips[k], c)).wait_recv()
                passed.append(copy(i, chip_of(k), c, 3 + k, sibling))
                passed[-1].start()
        for k in range(3):
            for i in range(n):
                copy(i, chip_of(k), 1 - c, 3 + k, sibling).wait_recv()
        for cp in first + passed:
            cp.wait_send()

    return pl.pallas_call(
        body, name="gather_shards",
        out_shape=tuple(jax.ShapeDtypeStruct(s.shape, s.dtype) for s in slots),
        in_specs=[_ANY] * n, out_specs=tuple([_ANY] * n),
        input_output_aliases={i: i for i in range(n)},
        scratch_shapes=[pltpu.SemaphoreType.DMA((6 * n,)), pltpu.SemaphoreType.DMA((6 * n,))],
    )(*slots)


def _allgather8(block, name):
    rows, width = block.shape

    def body(x_ref, out_ref, send_sems, recv_sems, local_sem):
        x, y, c = _coords()
        me, sibling = (x, y, c), (x, y, 1 - c)
        chips = _other_chips(x, y)

        def slot(px, py, pc):
            return out_ref.at[4 * px + 2 * py + pc]

        def copy(k, blk, to, src=None):
            return pltpu.make_async_remote_copy(src_ref=slot(*blk) if src is None else src, dst_ref=slot(*blk),
                                                send_sem=send_sems.at[k], recv_sem=recv_sems.at[k],
                                                device_id=to, device_id_type=_MESH)

        mine = pltpu.make_async_copy(x_ref, slot(*me), local_sem)
        mine.start()
        first = [copy(0, me, sibling, src=x_ref)]
        first += [copy(1 + j, me, (*chip, c), src=x_ref) for j, chip in enumerate(chips)]
        for cp in first:
            cp.start()
        passed = [copy(4 + j, (*chip, c), sibling) for j, chip in enumerate(chips)]
        for j, chip in enumerate(chips):
            copy(1 + j, (*chip, c), me).wait_recv()
            passed[j].start()
        copy(0, sibling, me).wait_recv()
        for j, chip in enumerate(chips):
            copy(4 + j, (*chip, 1 - c), me).wait_recv()
        for cp in first + passed:
            cp.wait_send()
        mine.wait()

    return pl.pallas_call(
        body, name=name,
        out_shape=jax.ShapeDtypeStruct((N_DEV, rows, width), block.dtype),
        in_specs=[_VM], out_specs=_VM,
        scratch_shapes=[pltpu.SemaphoreType.DMA((7,)), pltpu.SemaphoreType.DMA((7,)), pltpu.SemaphoreType.DMA],
    )(block)


def _pair_swap(arrs):
    def copies(src, dst, send_sems, recv_sems):
        x, y, c = _coords()
        return [pltpu.make_async_remote_copy(src_ref=src[i].at[:, :, _half(1 - c)], dst_ref=dst[i],
                                             send_sem=send_sems.at[i], recv_sem=recv_sems.at[i],
                                             device_id=(x, y, 1 - c), device_id_type=_MESH) for i in range(len(src))]

    shapes = tuple(jax.ShapeDtypeStruct((4, a.shape[1], _HALF), a.dtype) for a in arrs)
    return tuple(arrs), shapes, copies, len(arrs), False


def _chip_exchange(arrs):
    def copies(src, dst, send_sems, recv_sems):
        x, y, c = _coords()
        chips = _other_chips(x, y)
        return [pltpu.make_async_remote_copy(src_ref=src[i].at[2 * chips[k][0] + chips[k][1]], dst_ref=dst[i].at[k],
                                             send_sem=send_sems.at[3 * i + k], recv_sem=recv_sems.at[3 * i + k],
                                             device_id=(*chips[k], c), device_id_type=_MESH)
                for k in range(3) for i in range(len(src))]

    shapes = tuple(jax.ShapeDtypeStruct((3,) + a.shape[1:], a.dtype) for a in arrs)
    return tuple(arrs), shapes, copies, 3 * len(arrs), False


def _gather_stage(slots, to_sibling):
    def copies(buf, _, send_sems, recv_sems):
        x, y, c = _coords()
        chips = _other_chips(x, y)
        out = []
        for k in range(3):
            for i in range(len(buf)):
                frm = 2 * chips[k][0] + chips[k][1] if to_sibling else 2 * x + y
                part = buf[i].at[frm, :, _half(c)]
                out.append(pltpu.make_async_remote_copy(
                    src_ref=part, dst_ref=part, send_sem=send_sems.at[3 * i + k], recv_sem=recv_sems.at[3 * i + k],
                    device_id=(x, y, 1 - c) if to_sibling else (*chips[k], c), device_id_type=_MESH))
        return out

    shapes = tuple(jax.ShapeDtypeStruct(s.shape, s.dtype) for s in slots)
    return tuple(slots), shapes, copies, 3 * len(slots), True


def _pair_join_halves(fulls):
    n = len(fulls)

    def body(*refs):
        buf = refs[n:2 * n]
        send_sems, recv_sems = refs[2 * n:]
        x, y, c = _coords()

        def remote(i, cc):
            part = buf[i].at[:, _half(cc)]
            return pltpu.make_async_remote_copy(src_ref=part, dst_ref=part, send_sem=send_sems.at[i],
                                                recv_sem=recv_sems.at[i], device_id=(x, y, 1 - c), device_id_type=_MESH)

        for i in range(n):
            remote(i, c).start()
        for i in range(n):
            remote(i, c).wait_send()
            remote(i, 1 - c).wait_recv()

    return pl.pallas_call(
        body, name="pair_join_halves",
        out_shape=tuple(jax.ShapeDtypeStruct(a.shape, a.dtype) for a in fulls),
        in_specs=[_ANY] * n, out_specs=tuple([_ANY] * n),
        input_output_aliases={i: i for i in range(n)},
        scratch_shapes=[pltpu.SemaphoreType.DMA((n,)), pltpu.SemaphoreType.DMA((n,))],
    )(*fulls)


_RED_TC = 128
_RED_NT = _HALF // _RED_TC


def _add_pair(ids, g32, recv_a):
    rows = g32.shape[1]

    def body(ids_ref, g_ref, r_ref, o_ref):
        o_ref[...] = (g_ref[...] + r_ref[...]).astype(BF16)

    blk = pl.BlockSpec((1, rows, _RED_TC), lambda j, l, ids: (j, 0, l))
    return pl.pallas_call(
        body, name="add_pair",
        out_shape=jax.ShapeDtypeStruct((4, rows, _HALF), BF16),
        grid_spec=pltpu.PrefetchScalarGridSpec(
            num_scalar_prefetch=1, grid=(4, _RED_NT),
            in_specs=[pl.BlockSpec((1, rows, _RED_TC), lambda j, l, ids: (j, 0, ids[0] * _RED_NT + l)), blk],
            out_specs=blk),
        compiler_params=_cp("parallel", "parallel"),
    )(ids, g32, recv_a)


def _add_chips(ids, g32, recv_a, recv_b):
    rows = g32.shape[1]

    def body(ids_ref, g_ref, a_ref, b_ref, o_ref):
        acc = g_ref[0] + a_ref[0]
        for k in range(3):
            acc = acc + b_ref[k].astype(F32)
        o_ref[...] = acc

    return pl.pallas_call(
        body, name="add_chips",
        out_shape=jax.ShapeDtypeStruct((rows, 2 * _HALF), F32),
        grid_spec=pltpu.PrefetchScalarGridSpec(
            num_scalar_prefetch=1, grid=(_RED_NT,),
            in_specs=[pl.BlockSpec((1, rows, _RED_TC), lambda l, ids: (ids[1], 0, ids[0] * _RED_NT + l)),
                      pl.BlockSpec((1, rows, _RED_TC), lambda l, ids: (ids[1], 0, l)),
                      pl.BlockSpec((3, rows, _RED_TC), lambda l, ids: (0, 0, l))],
            out_specs=pl.BlockSpec((rows, _RED_TC), lambda l, ids: (0, ids[0] * _RED_NT + l))),
        compiler_params=_cp("parallel"),
    )(ids, g32, recv_a, recv_b)


def _sum8(gathered):
    _, rows, width = gathered.shape

    def body(g_ref, o_ref):
        acc = g_ref[0]
        for d in range(1, N_DEV):
            acc = acc + g_ref[d]
        o_ref[...] = acc

    return pl.pallas_call(
        body, name="sum8",
        out_shape=jax.ShapeDtypeStruct((rows, width), F32),
        in_specs=[_VM], out_specs=_VM,
    )(gathered)


def _adamw(w, g, m, v, name):
    rows, cols = w.shape
    budget = (3 << 20) // 2
    tr, tc = rows, cols
    if rows * cols * 4 > budget:
        if rows % 8 == 0:
            tr = next(c for c in (512, 256, 128, 64, 32, 16, 8) if rows % c == 0 and c * cols * 4 <= budget)
        else:
            tc = next(c for c in (512, 256, 128) if cols % c == 0 and rows * c * 4 <= budget)
    c1 = 1.0 - ADAM_B1 ** ADAM_STEP
    c2 = 1.0 - ADAM_B2 ** ADAM_STEP

    def body(w_ref, g_ref, m_ref, v_ref, d_ref, mo_ref, vo_ref):
        gg = g_ref[...]
        mn = ADAM_B1 * m_ref[...] + (1.0 - ADAM_B1) * gg
        vn = ADAM_B2 * v_ref[...] + (1.0 - ADAM_B2) * (gg * gg)
        mo_ref[...] = mn
        vo_ref[...] = vn
        d_ref[...] = -ADAM_LR * ((mn / c1) / (jnp.sqrt(vn / c2) + ADAM_EPS) + ADAM_WD * w_ref[...])

    blk = pl.BlockSpec((tr, tc), lambda i, j: (i, j))
    shp = jax.ShapeDtypeStruct((rows, cols), F32)
    return pl.pallas_call(
        body, name=name, out_shape=(shp, shp, shp), grid=(rows // tr, cols // tc),
        in_specs=[blk] * 4, out_specs=(blk, blk, blk),
        compiler_params=_cp("parallel", "parallel"),
    )(w, g, m, v)


def _rows128(a):
    return a.reshape(-1, 128)


def _pack_small(norm_pre, conv_b, ssd_norm, gate_bias, norm_post, dt_bias, a_log, d_skip, fgate_bias):
    tiny = jnp.concatenate([dt_bias.reshape(-1), a_log.reshape(-1), d_skip.reshape(-1), fgate_bias.reshape(-1),
                            jnp.zeros((16,), F32)])
    return jnp.concatenate([_rows128(norm_pre), _rows128(conv_b), _rows128(ssd_norm), _rows128(gate_bias),
                            _rows128(norm_post), tiny.reshape(1, 128)], axis=0)


_SMALL_ROWS = 73
_SMALL_PAD = 80


def _unpack_small(p):
    tiny = p[72]
    return dict(norm_pre=p[0:8].reshape(1, 1024), conv_b=p[8:32].reshape(1, 3072), ssd_norm=p[32:48].reshape(1, 2048),
                gate_bias=p[48:64].reshape(1, 2048), norm_post=p[64:72].reshape(1, 1024),
                dt_bias=tiny[0:32].reshape(1, 32), a_log=tiny[32:64].reshape(1, 32),
                d_skip=tiny[64:96].reshape(1, 32), fgate_bias=tiny[96:112].reshape(1, 16))


def _pad_rows(a, rows):
    return jnp.concatenate([a, jnp.zeros((rows - a.shape[0], a.shape[1]), a.dtype)], axis=0)


def kernel(x, meta_tokens, norm_pre, w_in, conv_w, conv_b, dt_bias, a_log, d_skip, ssd_norm, fgate_bias, gate_bias, w_proj_ssd, w_proj_att, w_out, norm_post, loss_target, m_meta_tokens, m_norm_pre, m_w_in, m_conv_w, m_conv_b, m_dt_bias, m_a_log, m_d_skip, m_ssd_norm, m_fgate_bias, m_gate_bias, m_w_proj_ssd, m_w_proj_att, m_w_out, m_norm_post, v_meta_tokens, v_norm_pre, v_w_in, v_conv_w, v_conv_b, v_dt_bias, v_a_log, v_d_skip, v_ssd_norm, v_fgate_bias, v_gate_bias, v_w_proj_ssd, v_w_proj_att, v_w_out, v_norm_post):
    cx, cy, cc = _coords()
    chip = 2 * cx + cy
    ids = jnp.stack([cc, chip]).astype(jnp.int32)
    seq = x.shape[1]

    w_in_sh = jnp.transpose(w_in[0]).astype(BF16)
    w_pr_sh = jnp.concatenate([w_proj_ssd[0], w_proj_att[0], w_out[0]], axis=0).astype(BF16)

    def own_slot(sh):
        return lax.dynamic_update_slice(lax.empty((4,) + sh.shape, sh.dtype), sh[None], (chip, 0, 0))

    (g_in,) = _gather_shards([own_slot(w_in_sh)])
    w_main, w_small = _to_aligned_rows(g_in)
    sm_sh = jnp.concatenate([_rows128(meta_tokens), _rows128(conv_w[0])], axis=0)
    sm_all = _allgather8(sm_sh, "gather_small_weights")[0::2]
    meta_full = jnp.transpose(sm_all[:, 0:32].reshape(4, N_META, 256), (1, 0, 2)).reshape(N_META, D_MODEL)
    conv_w_full = jnp.transpose(sm_all[:, 32:56].reshape(4, CONV_K, 768), (1, 0, 2)).reshape(CONV_K, CONV_DIM)

    h = jnp.concatenate([jnp.zeros((PADF, D_MODEL), F32), meta_full, x[0]], axis=0)
    bias_row = jnp.concatenate([dt_bias[0], fgate_bias[0], jnp.zeros((N_SMALL - H_SSD - H_ATT,), F32)]).reshape(1, N_SMALL)
    a_neg = -jnp.exp(a_log[0])
    a_row = jnp.concatenate([a_neg, jnp.zeros((N_SMALL - H_SSD,), F32)]).reshape(1, N_SMALL)
    dsk_row = jnp.repeat(d_skip[0], 64).reshape(1, D_SSD)
    r = _local_step(h, loss_target[0], w_main, w_small, own_slot(w_pr_sh), ids, norm_pre, conv_w_full, conv_b,
                    bias_row, a_row, dsk_row, ssd_norm, gate_bias, norm_post)

    tm = _mm_tiles(h.shape[0])
    n_row_tiles = h.shape[0] // tm
    d_w_main = _matmul_cat_tn(r["dproj"], r["u"], "inproj_dw", tm)
    d_w_small = _matmul(r["dsmall"], r["u"], "tn", F32, "inproj_small_dw", N_SMALL, D_MODEL, tm)
    g32_in = _from_aligned_rows(d_w_main, d_w_small)
    first = max(n_row_tiles // 6, 1)
    du_first, ra_in = _matmul_cat_nn(r["dproj"], w_main, "inproj_dx_swap", tm, rows=(0, first),
                                     exchange=_pair_swap([g32_in]))
    pb_in = _add_pair(ids, g32_in, ra_in)
    du_a, rb_in = _matmul_cat_nn(r["dproj"], w_main, "inproj_dx_exchange", tm,
                                 rows=(first, n_row_tiles - first), fill=du_first,
                                 exchange=_chip_exchange([pb_in]))
    du_b = _matmul(r["dsmall"], w_small, "nn", F32, "inproj_small_dx", tm, D_MODEL, N_SMALL)
    dh, d_norm_pre = _norm1_bwd(du_a, du_b, h, norm_pre, r["dy_out"])
    grad_x = dh[PADF + N_META:].reshape(1, seq, D_MODEL)
    half_in = _add_chips(ids, g32_in, ra_in, rb_in)
    gw_in, gw_pr = _pair_join_halves([half_in, r["half_pr"]])

    tiny = r["d_bias_row"][0]
    part_small = _pack_small(d_norm_pre, r["d_conv_b"], r["d_ssd_norm"], r["d_gate_bias"], r["d_norm_post"],
                             tiny[0:H_SSD], r["d_a"][0, 0:H_SSD] * a_neg, r["d_dsk"].reshape(H_SSD, 64).sum(axis=1),
                             tiny[H_SSD:H_SSD + H_ATT])
    part = jnp.concatenate([_pad_rows(part_small, _SMALL_PAD), _rows128(r["d_conv_w"]),
                            _rows128(dh[PADF:PADF + N_META]), r["loss_blk"]], axis=0)
    tot = _sum8(_allgather8(part, "gather_small_grads"))
    loss = tot[_SMALL_PAD + 96 + 128, 0]
    g_small = tot[0:_SMALL_PAD]
    g_conv_w = lax.dynamic_slice_in_dim(tot[_SMALL_PAD:_SMALL_PAD + 96].reshape(CONV_K, CONV_DIM), chip * 768, 768, axis=1)
    g_meta = lax.dynamic_slice_in_dim(tot[_SMALL_PAD + 96:_SMALL_PAD + 224].reshape(N_META, D_MODEL), chip * 256, 256, axis=1)

    upd = {}
    upd["w_in"] = tuple(jnp.transpose(a) for a in (gw_in,) + _adamw(
        jnp.transpose(w_in[0]), gw_in, jnp.transpose(m_w_in[0]), jnp.transpose(v_w_in[0]), "adamw_w_in"))
    w_pr32 = jnp.concatenate([w_proj_ssd[0], w_proj_att[0], w_out[0]], axis=0)
    m_pr = jnp.concatenate([m_w_proj_ssd[0], m_w_proj_att[0], m_w_out[0]], axis=0)
    v_pr = jnp.concatenate([v_w_proj_ssd[0], v_w_proj_att[0], v_w_out[0]], axis=0)
    pr = (gw_pr,) + _adamw(w_pr32, gw_pr, m_pr, v_pr, "adamw_w_proj")
    upd["w_proj_ssd"] = tuple(a[0:512] for a in pr)
    upd["w_proj_att"] = tuple(a[512:768] for a in pr)
    upd["w_out"] = tuple(a[768:1024] for a in pr)
    upd["conv_w"] = (g_conv_w,) + _adamw(conv_w[0], g_conv_w, m_conv_w[0], v_conv_w[0], "adamw_conv_w")
    upd["meta_tokens"] = (g_meta,) + _adamw(meta_tokens, g_meta, m_meta_tokens, v_meta_tokens, "adamw_meta")
    pk = lambda np_, cb, sn, gb, npo, dtb, al, ds, fg: _pad_rows(_pack_small(np_, cb, sn, gb, npo, dtb, al, ds, fg), _SMALL_PAD)
    w_sm = pk(norm_pre, conv_b, ssd_norm, gate_bias, norm_post, dt_bias, a_log, d_skip, fgate_bias)
    m_sm = pk(m_norm_pre, m_conv_b, m_ssd_norm, m_gate_bias, m_norm_post, m_dt_bias, m_a_log, m_d_skip, m_fgate_bias)
    v_sm = pk(v_norm_pre, v_conv_b, v_ssd_norm, v_gate_bias, v_norm_post, v_dt_bias, v_a_log, v_d_skip, v_fgate_bias)
    sm = [_unpack_small(a) for a in (g_small,) + _adamw(w_sm, g_small, m_sm, v_sm, "adamw_small")]
    for name in ("norm_pre", "conv_b", "dt_bias", "a_log", "d_skip", "ssd_norm", "fgate_bias", "gate_bias", "norm_post"):
        upd[name] = tuple(s[name] for s in sm)
    lead = ("w_in", "conv_w", "w_proj_ssd", "w_proj_att", "w_out")
    order = ("meta_tokens", "norm_pre", "w_in", "conv_w", "conv_b", "dt_bias", "a_log", "d_skip", "ssd_norm",
             "fgate_bias", "gate_bias", "w_proj_ssd", "w_proj_att", "w_out", "norm_post")
    outs = [loss, grad_x]
    for part_i in range(4):
        for name in order:
            a = upd[name][part_i]
            outs.append(a[None] if name in lead else a)
    return tuple(outs)
```

```python
import functools
import math

import jax
import jax.numpy as jnp
from jax import lax
from jax.experimental import pallas as pl
from jax.experimental.pallas import tpu as pltpu

F32 = jnp.float32
BF16 = jnp.bfloat16
HIGHEST = lax.Precision.HIGHEST

D_MODEL = 1024
N_META = 16
CHUNK = 128
PADF = CHUNK - N_META
D_SSD = 2048
H_SSD = 32
G_SSD = 4
N_STATE = 128
CONV_K = 4
CONV_DIM = D_SSD + 2 * G_SSD * N_STATE
H_ATT = 16
D_ATT = 1024
EPS = 1e-6
N_COLS = 11312

C_Z, C_XBC, C_ZA, C_Q, C_K, C_V, C_G = 0, 2048, 5120, 6144, 7168, 8192, 9216
N_MAIN = 11264
N_SMALL = 128
O_Z, O_XBC, O_DT, O_ZA, O_Q, O_K, O_V, O_F, O_G = (
    (0, 2048), (2048, 3072), (5120, 32), (5152, 1024), (6176, 1024), (7200, 1024),
    (8224, 1024), (9248, 16), (9264, 2048))

ADAM_LR, ADAM_B1, ADAM_B2, ADAM_EPS, ADAM_WD, ADAM_STEP = 0.001, 0.9, 0.999, 1e-08, 0.01, 10

VMEM_LIMIT = 56 * 1024 * 1024


def _cp(*sem):
    return pltpu.CompilerParams(dimension_semantics=sem, vmem_limit_bytes=VMEM_LIMIT)


def _tile(n, prefs):
    for p in prefs:
        if n % p == 0:
            return p
    raise ValueError(f"no tile for {n} in {prefs}")


def _iota(shape, dim):
    return lax.broadcasted_iota(jnp.int32, shape, dim)


def _sigmoid(x):
    return 1.0 / (1.0 + jnp.exp(-x))


def _softplus_tail(x):
    return jnp.log(1.0 + jnp.exp(-jnp.abs(x)))


_NN = (((1,), (0,)), ((), ()))
_NT = (((1,), (1,)), ((), ()))
_TN = (((0,), (0,)), ((), ()))


def _dot(a, b, dims=_NN):
    return lax.dot_general(a, b, dims, preferred_element_type=F32)


def _dot_exact(a, b, dims=_NN):
    return lax.dot_general(a, b, dims, precision=HIGHEST, preferred_element_type=F32)


def _hosted_call(body, *, name, grid, in_specs, out_specs, out_shape, scratch_shapes, operands, semantics,
                 exchange=None, aliases=None):
    aliases = dict(aliases or {})
    if exchange is None:
        return pl.pallas_call(body, name=name, out_shape=out_shape, grid=grid, in_specs=in_specs,
                              out_specs=out_specs, scratch_shapes=scratch_shapes, input_output_aliases=aliases,
                              compiler_params=_cp(*semantics))(*operands)
    arrays, shapes, copies, n_sems, in_place = exchange
    n_in, n_out, n_ex = len(operands), len(out_shape), len(arrays)

    def hosted(*refs):
        ex_in = refs[n_in:n_in + n_ex]
        ex_out = refs[n_in + n_ex + n_out:n_in + n_ex + n_out + n_ex]
        own = refs[:n_in] + refs[n_in + n_ex:n_in + n_ex + n_out] + refs[n_in + 2 * n_ex + n_out:-2]
        first = functools.reduce(lambda p, q: p & q, [pl.program_id(d) == 0 for d in range(len(grid))])
        last = functools.reduce(lambda p, q: p & q, [pl.program_id(d) == grid[d] - 1 for d in range(len(grid))])

        def descriptors():
            return copies(ex_out if in_place else ex_in, ex_out, refs[-2], refs[-1])

        @pl.when(first)
        def _():
            for cp in descriptors():
                cp.start()

        body(*own)

        @pl.when(last)
        def _():
            for cp in descriptors():
                cp.wait()

    return pl.pallas_call(
        hosted, name=name,
        out_shape=tuple(out_shape) + tuple(shapes),
        grid=grid,
        in_specs=list(in_specs) + [_ANY] * n_ex,
        out_specs=tuple(out_specs) + (_ANY,) * n_ex,
        input_output_aliases={**aliases, **({n_in + e: n_out + e for e in range(n_ex)} if in_place else {})},
        scratch_shapes=list(scratch_shapes) + [pltpu.SemaphoreType.DMA((n_sems,)), pltpu.SemaphoreType.DMA((n_sems,))],
        compiler_params=_cp(*(("arbitrary",) * len(grid))),
    )(*operands, *arrays)


def _matmul(a, b, mode, out_dtype, name, tm, tn, tk, exchange=None):
    if mode == "tn":
        kdim, m = a.shape
    else:
        m, kdim = a.shape
    n = b.shape[0] if mode == "nt" else b.shape[1]
    nk = kdim // tk
    dims = {"nn": _NN, "nt": _NT, "tn": _TN}[mode]
    a_spec = (pl.BlockSpec((tk, tm), lambda i, j, k: (k, i)) if mode == "tn"
              else pl.BlockSpec((tm, tk), lambda i, j, k: (i, k)))
    b_spec = (pl.BlockSpec((tn, tk), lambda i, j, k: (j, k)) if mode == "nt"
              else pl.BlockSpec((tk, tn), lambda i, j, k: (k, j)))

    def body(a_ref, b_ref, o_ref, acc_ref):
        k = pl.program_id(2)
        p = _dot(a_ref[...].astype(BF16), b_ref[...].astype(BF16), dims)
        if nk == 1:
            o_ref[...] = p.astype(out_dtype)
        else:
            @pl.when(k == 0)
            def _():
                acc_ref[...] = p

            @pl.when(k > 0)
            def _():
                acc_ref[...] += p

            @pl.when(k == nk - 1)
            def _():
                o_ref[...] = acc_ref[...].astype(out_dtype)

    out = _hosted_call(
        body, name=name,
        out_shape=(jax.ShapeDtypeStruct((m, n), out_dtype),),
        grid=(m // tm, n // tn, nk),
        in_specs=[a_spec, b_spec],
        out_specs=(pl.BlockSpec((tm, tn), lambda i, j, k: (i, j)),),
        scratch_shapes=[pltpu.VMEM((tm, tn), F32)],
        operands=(a, b), semantics=("parallel", "parallel", "arbitrary"), exchange=exchange)
    return out[0] if exchange is None else out


_CAT_BLK = 1024


def _piece_ranges(pieces):
    out, off = [], 0
    for p in pieces:
        nb = p.shape[1] // _CAT_BLK
        out.append((off, nb))
        off += nb
    return out, off


def _matmul_cat_nn(pieces, b, name, tm, rows=None, fill=None, exchange=None):
    t = pieces[0].shape[0]
    n = b.shape[1]
    ranges, nk = _piece_ranges(pieces)
    first, ni = rows if rows is not None else (0, t // tm)
    n_in = len(pieces) + 1 + (fill is not None)

    def body(*refs):
        a_refs, b_ref, o_ref, acc_ref = refs[:len(pieces)], refs[len(pieces)], refs[n_in], refs[n_in + 1]
        k = pl.program_id(1)

        @pl.when(k == 0)
        def _():
            acc_ref[...] = jnp.zeros_like(acc_ref)

        for a_ref, (off, nb) in zip(a_refs, ranges):
            @pl.when((k >= off) & (k < off + nb))
            def _(a_ref=a_ref):
                acc_ref[...] += _dot(a_ref[...], b_ref[...])

        @pl.when(k == nk - 1)
        def _():
            o_ref[...] = acc_ref[...]

    def a_spec(off, nb):
        return pl.BlockSpec((tm, _CAT_BLK), lambda i, k: (first + i, jnp.clip(k - off, 0, nb - 1)))

    in_specs = [a_spec(off, nb) for off, nb in ranges] + [pl.BlockSpec((_CAT_BLK, n), lambda i, k: (k, 0))]
    operands = list(pieces) + [b]
    if fill is not None:
        in_specs.append(_ANY)
        operands.append(fill)
    out = _hosted_call(
        body, name=name,
        out_shape=(jax.ShapeDtypeStruct((t, n), F32),),
        grid=(ni, nk),
        in_specs=in_specs,
        out_specs=(pl.BlockSpec((tm, n), lambda i, k: (first + i, 0)),),
        scratch_shapes=[pltpu.VMEM((tm, n), F32)],
        operands=operands, semantics=("parallel", "arbitrary"), exchange=exchange,
        aliases={len(pieces) + 1: 0} if fill is not None else None)
    return out if exchange is not None else out[0]


def _matmul_cat_tn(pieces, b, name, tk):
    t = pieces[0].shape[0]
    n = b.shape[1]
    ranges, nm = _piece_ranges(pieces)
    nk = t // tk

    def body(*refs):
        a_refs, b_ref, o_ref, acc_ref = refs[:len(pieces)], refs[-3], refs[-2], refs[-1]
        m = pl.program_id(0)
        k = pl.program_id(1)

        @pl.when(k == 0)
        def _():
            acc_ref[...] = jnp.zeros_like(acc_ref)

        for a_ref, (off, nb) in zip(a_refs, ranges):
            @pl.when((m >= off) & (m < off + nb))
            def _(a_ref=a_ref):
                acc_ref[...] += _dot(a_ref[...], b_ref[...], _TN)

        @pl.when(k == nk - 1)
        def _():
            o_ref[...] = acc_ref[...]

    def a_spec(off, nb):
        def index(m, k):
            mine = (m >= off) & (m < off + nb)
            return jnp.where(mine, k, 0), jnp.clip(m - off, 0, nb - 1)
        return pl.BlockSpec((tk, _CAT_BLK), index)

    return pl.pallas_call(
        body, name=name,
        out_shape=jax.ShapeDtypeStruct((nm * _CAT_BLK, n), F32),
        grid=(nm, nk),
        in_specs=[a_spec(off, nb) for off, nb in ranges] + [pl.BlockSpec((tk, n), lambda m, k: (k, 0))],
        out_specs=pl.BlockSpec((_CAT_BLK, n), lambda m, k: (m, 0)),
        scratch_shapes=[pltpu.VMEM((_CAT_BLK, n), F32)],
        compiler_params=_cp("parallel", "arbitrary"),
    )(*pieces, b)


def _row_tile(t):
    return _tile(t, (352, 128))


def _row_tile_wide(t):
    return _tile(t, (176, 128))


def _norm1_fwd(h, g):
    t = h.shape[0]
    tm = _row_tile(t)

    def body(h_ref, g_ref, u_ref):
        x = h_ref[...]
        r = lax.rsqrt(jnp.mean(x * x, axis=-1, keepdims=True) + EPS)
        u_ref[...] = (x * r * g_ref[...]).astype(BF16)

    return pl.pallas_call(
        body, name="norm1_fwd",
        out_shape=jax.ShapeDtypeStruct((t, D_MODEL), BF16),
        grid=(t // tm,),
        in_specs=[pl.BlockSpec((tm, D_MODEL), lambda i: (i, 0)),
                  pl.BlockSpec((1, D_MODEL), lambda i: (0, 0))],
        out_specs=pl.BlockSpec((tm, D_MODEL), lambda i: (i, 0)),
        compiler_params=_cp("parallel"),
    )(h, g)


def _norm1_bwd(du_a, du_b, h, g, dy):
    t = h.shape[0]
    tm = _row_tile(t)

    def body(a_ref, b_ref, h_ref, g_ref, dy_ref, dh_ref, dg_ref):
        i = pl.program_id(0)
        x = h_ref[...]
        du = a_ref[...] + b_ref[...]
        r = lax.rsqrt(jnp.mean(x * x, axis=-1, keepdims=True) + EPS)
        gdu = du * g_ref[...]
        dh_ref[...] = dy_ref[...] + r * (gdu - x * (r * r) * jnp.mean(gdu * x, axis=-1, keepdims=True))
        part = jnp.sum(du * x * r, axis=0, keepdims=True)

        @pl.when(i == 0)
        def _():
            dg_ref[...] = part

        @pl.when(i > 0)
        def _():
            dg_ref[...] += part

    row = pl.BlockSpec((tm, D_MODEL), lambda i: (i, 0))
    vec = pl.BlockSpec((1, D_MODEL), lambda i: (0, 0))
    return pl.pallas_call(
        body, name="norm1_bwd",
        out_shape=(jax.ShapeDtypeStruct((t, D_MODEL), F32), jax.ShapeDtypeStruct((1, D_MODEL), F32)),
        grid=(t // tm,),
        in_specs=[row, row, row, vec, row],
        out_specs=(row, vec),
        compiler_params=_cp("arbitrary"),
    )(du_a, du_b, h, g, dy)


def _small_fwd(small, bias_row):
    t = small.shape[0]

    def body(s_ref, b_ref, o_ref, carry_ref):
        c = pl.program_id(0)

        @pl.when(c == 0)
        def _():
            carry_ref[...] = jnp.zeros_like(carry_ref)

        x = s_ref[...] + b_ref[...]
        r0 = _iota((CHUNK, CHUNK), 0)
        r1 = _iota((CHUNK, CHUNK), 1)
        valid = (c * CHUNK + r0) >= PADF
        tail = _softplus_tail(x)
        dt = jnp.where(valid & (r1 < H_SSD), jnp.maximum(x, 0.0) + tail, 0.0)
        lf = jnp.where(valid & (r1 >= H_SSD) & (r1 < H_SSD + H_ATT), jnp.minimum(x, 0.0) - tail, 0.0)
        tri = (r0 >= r1).astype(F32)
        cs = _dot_exact(tri, lf) + carry_ref[...]
        carry_ref[...] = cs[CHUNK - 1:CHUNK, :]
        o_ref[...] = dt + cs

    return pl.pallas_call(
        body, name="small_fwd",
        out_shape=jax.ShapeDtypeStruct((t, N_SMALL), F32),
        grid=(t // CHUNK,),
        in_specs=[pl.BlockSpec((CHUNK, N_SMALL), lambda c: (c, 0)),
                  pl.BlockSpec((1, N_SMALL), lambda c: (0, 0))],
        out_specs=pl.BlockSpec((CHUNK, N_SMALL), lambda c: (c, 0)),
        scratch_shapes=[pltpu.VMEM((1, N_SMALL), F32)],
        compiler_params=_cp("arbitrary"),
    )(small, bias_row)


def _small_bwd(dsm, small, bias_row):
    t = small.shape[0]
    nc = t // CHUNK

    def body(d_ref, s_ref, b_ref, o_ref, db_ref, carry_ref):
        step = pl.program_id(0)
        c = nc - 1 - step

        @pl.when(step == 0)
        def _():
            carry_ref[...] = jnp.zeros_like(carry_ref)
            db_ref[...] = jnp.zeros_like(db_ref)

        x = s_ref[...] + b_ref[...]
        d = d_ref[...]
        r0 = _iota((CHUNK, CHUNK), 0)
        r1 = _iota((CHUNK, CHUNK), 1)
        valid = (c * CHUNK + r0) >= PADF
        is_dt = r1 < H_SSD
        is_f = (r1 >= H_SSD) & (r1 < H_SSD + H_ATT)
        triu = (r1 >= r0).astype(F32)
        dc = jnp.where(is_f, d, 0.0)
        dlf = _dot_exact(triu, dc) + carry_ref[...]
        carry_ref[...] = dlf[0:1, :]
        sg = _sigmoid(x)
        out = jnp.where(valid & is_dt, d * sg, 0.0) + jnp.where(valid & is_f, dlf * (1.0 - sg), 0.0)
        o_ref[...] = out.astype(BF16)
        db_ref[...] += jnp.sum(out, axis=0, keepdims=True)

    blk = pl.BlockSpec((CHUNK, N_SMALL), lambda s: (nc - 1 - s, 0))
    vec = pl.BlockSpec((1, N_SMALL), lambda s: (0, 0))
    return pl.pallas_call(
        body, name="small_bwd",
        out_shape=(jax.ShapeDtypeStruct((t, N_SMALL), BF16), jax.ShapeDtypeStruct((1, N_SMALL), F32)),
        grid=(nc,),
        in_specs=[blk, blk, vec],
        out_specs=(blk, vec),
        scratch_shapes=[pltpu.VMEM((1, N_SMALL), F32)],
        compiler_params=_cp("arbitrary"),
    )(dsm, small, bias_row)


_CONV_TC = 1024
_XBC_BLK = C_XBC // _CONV_TC


def _shift_down(cur, prev8, j):
    rc = pltpu.roll(cur, j, 0)
    rid = _iota(prev8.shape, 0)
    top = jnp.where(rid < j, pltpu.roll(prev8, j, 0), rc[0:8, :])
    return top if cur.shape[0] == 8 else jnp.concatenate([top, rc[8:, :]], axis=0)


def _shift_up(cur, next8, j):
    n = cur.shape[0]
    ru = pltpu.roll(cur, n - j, 0)
    rid = _iota(next8.shape, 0)
    bot = jnp.where(rid >= 8 - j, pltpu.roll(next8, 8 - j, 0), ru[n - 8:, :])
    return jnp.concatenate([ru[:n - 8, :], bot], axis=0)


def _conv_taps(cur, prev, w, b):
    taps = [cur] + [_shift_down(cur, prev, j) for j in (1, 2, 3)]
    acc = b + taps[0] * w[3:4, :]
    for j in (1, 2, 3):
        acc = acc + taps[j] * w[3 - j:4 - j, :]
    return acc, taps


def _conv_pre(x_ref, p_ref, w_ref, b_ref, i):
    return _conv_taps(x_ref[...], jnp.where(i > 0, p_ref[...], 0.0), w_ref[...], b_ref[...])


def _dsilu(d, acc):
    sg = _sigmoid(acc)
    return d * sg * (1.0 + acc * (1.0 - sg))


def _conv_fwd(proj, conv_w, conv_b):
    t = proj.shape[0]
    tr = _row_tile(t)

    def body(x_ref, p_ref, w_ref, b_ref, o_ref):
        i = pl.program_id(0)
        acc, _ = _conv_pre(x_ref, p_ref, w_ref, b_ref, i)
        valid = (i * tr + _iota(acc.shape, 0)) >= PADF
        o_ref[...] = jnp.where(valid, acc * _sigmoid(acc), 0.0)

    return pl.pallas_call(
        body, name="conv_fwd",
        out_shape=jax.ShapeDtypeStruct((t, CONV_DIM), F32),
        grid=(t // tr, CONV_DIM // _CONV_TC),
        in_specs=[pl.BlockSpec((tr, _CONV_TC), lambda i, j: (i, _XBC_BLK + j)),
                  pl.BlockSpec((8, _CONV_TC), lambda i, j: (jnp.maximum(i * (tr // 8) - 1, 0), _XBC_BLK + j)),
                  pl.BlockSpec((CONV_K, _CONV_TC), lambda i, j: (0, j)),
                  pl.BlockSpec((1, _CONV_TC), lambda i, j: (0, j))],
        out_specs=pl.BlockSpec((tr, _CONV_TC), lambda i, j: (i, j)),
        compiler_params=_cp("parallel", "parallel"),
    )(proj, proj, conv_w, conv_b)


def _conv_bwd(dxbc, proj, conv_w, conv_b):
    t = proj.shape[0]
    tr = _row_tile(t)
    n_tiles = t // tr
    last8 = t // 8 - 1

    def body(d_ref, dn_ref, x_ref, p_ref, xn_ref, w_ref, b_ref, dx_ref, dw_ref, db_ref):
        i = pl.program_id(1)
        w = w_ref[...]
        b = b_ref[...]
        cur = x_ref[...]
        acc, taps = _conv_taps(cur, jnp.where(i > 0, p_ref[...], 0.0), w, b)
        valid = (i * tr + _iota(acc.shape, 0)) >= PADF
        da = jnp.where(valid, _dsilu(d_ref[...], acc), 0.0)
        acc_n, _ = _conv_taps(xn_ref[...], cur[tr - 8:, :], w, b)
        da_n = jnp.where(i < n_tiles - 1, _dsilu(dn_ref[...], acc_n), 0.0)
        dx = da * w[3:4, :]
        for j in (1, 2, 3):
            dx = dx + _shift_up(da, da_n, j) * w[3 - j:4 - j, :]
        dx_ref[...] = dx.astype(BF16)
        dw = jnp.concatenate([jnp.sum(da * taps[3 - k], axis=0, keepdims=True) for k in range(CONV_K)], axis=0)
        db = jnp.sum(da, axis=0, keepdims=True)

        @pl.when(i == 0)
        def _():
            dw_ref[...] = dw
            db_ref[...] = db

        @pl.when(i > 0)
        def _():
            dw_ref[...] += dw
            db_ref[...] += db

    nxt8 = lambda i: jnp.minimum((i + 1) * (tr // 8), last8)
    return pl.pallas_call(
        body, name="conv_bwd",
        out_shape=(jax.ShapeDtypeStruct((t, CONV_DIM), BF16),
                   jax.ShapeDtypeStruct((CONV_K, CONV_DIM), F32),
                   jax.ShapeDtypeStruct((1, CONV_DIM), F32)),
        grid=(CONV_DIM // _CONV_TC, n_tiles),
        in_specs=[pl.BlockSpec((tr, _CONV_TC), lambda j, i: (i, j)),
                  pl.BlockSpec((8, _CONV_TC), lambda j, i: (nxt8(i), j)),
                  pl.BlockSpec((tr, _CONV_TC), lambda j, i: (i, _XBC_BLK + j)),
                  pl.BlockSpec((8, _CONV_TC), lambda j, i: (jnp.maximum(i * (tr // 8) - 1, 0), _XBC_BLK + j)),
                  pl.BlockSpec((8, _CONV_TC), lambda j, i: (nxt8(i), _XBC_BLK + j)),
                  pl.BlockSpec((CONV_K, _CONV_TC), lambda j, i: (0, j)),
                  pl.BlockSpec((1, _CONV_TC), lambda j, i: (0, j))],
        out_specs=(pl.BlockSpec((tr, _CONV_TC), lambda j, i: (i, j)),
                   pl.BlockSpec((CONV_K, _CONV_TC), lambda j, i: (0, j)),
                   pl.BlockSpec((1, _CONV_TC), lambda j, i: (0, j))),
        compiler_params=_cp("parallel", "arbitrary"),
    )(dxbc, dxbc, proj, proj, proj, conv_w, conv_b)


_GW = D_SSD // G_SSD


def _ssd_prelude(dt_ref, a_ref, e_scr, es_scr, dte_scr):
    r0 = _iota((CHUNK, CHUNK), 0)
    r1 = _iota((CHUNK, CHUNK), 1)
    dt = jnp.where(r1 < H_SSD, dt_ref[...], 0.0)
    adt = dt * a_ref[...]
    acs = _dot_exact((r0 >= r1).astype(F32), adt)
    acs_t = acs.T
    alast = acs[CHUNK - 1:CHUNK, :]
    exp_a = jnp.exp(acs)
    dec_s = jnp.exp(alast - acs)
    lo = r1 < 64
    for j in range(H_SSD // 2):
        sl = slice(CHUNK * j, CHUNK * (j + 1))
        e_scr[:, sl] = jnp.where(lo, exp_a[:, 2 * j:2 * j + 1], exp_a[:, 2 * j + 1:2 * j + 2])
        es_scr[:, sl] = jnp.where(lo, dec_s[:, 2 * j:2 * j + 1], dec_s[:, 2 * j + 1:2 * j + 2])
        dte_scr[:, sl] = jnp.where(lo, dt[:, 2 * j:2 * j + 1], dt[:, 2 * j + 1:2 * j + 2])
    return dt, acs, acs_t, r0, r1, lo


def _chunk_decay_rows(acs_t, g):
    cd_t = jnp.exp(acs_t[:, CHUNK - 1:CHUNK])
    return jnp.concatenate(
        [jnp.broadcast_to(cd_t[8 * g + hh:8 * g + hh + 1, :], (64, N_STATE)) for hh in range(8)], axis=0)


def _ssd_fwd(xbc, dtlf, a_row, dsk_row):
    t = xbc.shape[0]
    nc = t // CHUNK

    def body(xs_ref, b_ref, c_ref, dt_ref, a_ref, dsk_ref, y_ref, hin_ref, h_scr, e_scr, es_scr, dte_scr):
        c = pl.program_id(0)

        @pl.when(c == 0)
        def _():
            h_scr[...] = jnp.zeros_like(h_scr)

        dt, acs, acs_t, r0, r1, lo = _ssd_prelude(dt_ref, a_ref, e_scr, es_scr, dte_scr)
        causal = r0 >= r1
        for g in range(G_SSD):
            gs = slice(_GW * g, _GW * (g + 1))
            bg = b_ref[:, N_STATE * g:N_STATE * (g + 1)].astype(BF16)
            cg = c_ref[:, N_STATE * g:N_STATE * (g + 1)].astype(BF16)
            cb = _dot(cg, bg, _NT)
            hg = h_scr[gs, :]
            hin_ref[0, gs, :] = hg
            xg = xs_ref[:, gs] * dte_scr[:, gs]
            yoff = _dot(cg, hg.astype(BF16), _NT) * e_scr[:, gs]
            st = _dot((xg * es_scr[:, gs]).astype(BF16), bg, _TN)
            h_scr[gs, :] = hg * _chunk_decay_rows(acs_t, g) + st
            for jj in range(4):
                j = 4 * g + jj
                sl = slice(CHUNK * j, CHUNK * (j + 1))
                xp = xg[:, CHUNK * jj:CHUNK * (jj + 1)]
                acc = yoff[:, CHUNK * jj:CHUNK * (jj + 1)] + dsk_ref[:, sl] * xs_ref[:, sl]
                for hh in range(2):
                    h = 2 * j + hh
                    seg = acs[:, h:h + 1] - acs_t[h:h + 1, :]
                    lm = jnp.exp(jnp.where(causal, seg, -1e30))
                    m = (cb * lm).astype(BF16)
                    xh = jnp.where(lo if hh == 0 else ~lo, xp, 0.0).astype(BF16)
                    acc = acc + _dot(m, xh)
                y_ref[:, sl] = acc

    return pl.pallas_call(
        body, name="ssd_fwd",
        out_shape=(jax.ShapeDtypeStruct((t, D_SSD), F32), jax.ShapeDtypeStruct((nc, D_SSD, N_STATE), F32)),
        grid=(nc,),
        in_specs=[pl.BlockSpec((CHUNK, D_SSD), lambda c: (c, 0)),
                  pl.BlockSpec((CHUNK, _GW), lambda c: (c, 4)),
                  pl.BlockSpec((CHUNK, _GW), lambda c: (c, 5)),
                  pl.BlockSpec((CHUNK, N_SMALL), lambda c: (c, 0)),
                  pl.BlockSpec((1, N_SMALL), lambda c: (0, 0)),
                  pl.BlockSpec((1, D_SSD), lambda c: (0, 0))],
        out_specs=(pl.BlockSpec((CHUNK, D_SSD), lambda c: (c, 0)),
                   pl.BlockSpec((1, D_SSD, N_STATE), lambda c: (c, 0, 0))),
        scratch_shapes=[pltpu.VMEM((D_SSD, N_STATE), F32)] + [pltpu.VMEM((CHUNK, D_SSD), F32)] * 3,
        compiler_params=_cp("arbitrary"),
    )(xbc, xbc, xbc, dtlf, a_row, dsk_row)


def _ssd_bwd(xbc, dtlf, a_row, dsk_row, hin, dy):
    t = xbc.shape[0]
    nc = t // CHUNK

    def body(xs_ref, b_ref, c_ref, dt_ref, a_ref, dsk_ref, hin_ref, dy_ref,
             dxbc_ref, ddt_ref, da_ref, ddsk_ref, dh_scr, e_scr, es_scr, dte_scr, dx_scr, whi_scr, wlo_scr):
        step = pl.program_id(0)

        @pl.when(step == 0)
        def _():
            dh_scr[...] = jnp.zeros_like(dh_scr)
            da_ref[...] = jnp.zeros_like(da_ref)
            ddsk_ref[...] = jnp.zeros_like(ddsk_ref)

        dt, acs, acs_t, r0, r1, lo = _ssd_prelude(dt_ref, a_ref, e_scr, es_scr, dte_scr)
        causal = r0 >= r1
        lane_row = _iota((1, CHUNK), 1)
        dacs = jnp.zeros((CHUNK, CHUNK), F32)
        dacs_t = jnp.zeros((CHUNK, CHUNK), F32)
        dalast = jnp.zeros((1, CHUNK), F32)
        ddt_dir = jnp.zeros((CHUNK, CHUNK), F32)
        ddsk_ref[...] += jnp.sum(dy_ref[...] * xs_ref[...], axis=0, keepdims=True)

        def head_sums(z, pick):
            hi = z.astype(BF16)
            return _dot(hi, pick) + _dot((z - hi.astype(F32)).astype(BF16), pick)

        for g in range(G_SSD):
            gs = slice(_GW * g, _GW * (g + 1))
            pick = (jnp.right_shift(_iota((_GW, CHUNK), 0), 6) + 8 * g == _iota((_GW, CHUNK), 1)).astype(BF16)
            bg = b_ref[:, N_STATE * g:N_STATE * (g + 1)].astype(BF16)
            cg = c_ref[:, N_STATE * g:N_STATE * (g + 1)].astype(BF16)
            cb = _dot(cg, bg, _NT)
            hg = hin_ref[0, gs, :]
            hgb = hg.astype(BF16)
            dhn = dh_scr[gs, :]
            dhnb = dhn.astype(BF16)
            esg = es_scr[:, gs]
            dyg = dy_ref[:, gs]
            xsg = xs_ref[:, gs]
            xg = xsg * dte_scr[:, gs]
            dyeb = (dyg * e_scr[:, gs]).astype(BF16)
            dc = _dot(dyeb, hgb)
            dh_y = _dot(dyeb, cg, _TN)
            dxs = _dot(bg, dhnb, _NT) * esg
            db = _dot((xg * esg).astype(BF16), dhnb)
            cd = _chunk_decay_rows(acs_t, g)
            dh_scr[gs, :] = dhn * cd + dh_y
            end_state = head_sums(jnp.broadcast_to(jnp.sum(xg * dxs, axis=0, keepdims=True), (8, _GW)), pick)[0:1, :]
            carried = dhn * hg * cd
            per_head = jnp.concatenate([jnp.sum(carried[64 * hh:64 * hh + 64, :], axis=0, keepdims=True)
                                        for hh in range(8)], axis=0)
            per_head = jnp.sum(per_head, axis=1, keepdims=True)
            for hh in range(8):
                end_state = end_state + jnp.where(lane_row == 8 * g + hh, per_head[hh:hh + 1, :], 0.0)
            dalast = dalast + end_state
            dcb = jnp.zeros((CHUNK, CHUNK), F32)
            for jj in range(4):
                j = 4 * g + jj
                sl = slice(CHUNK * j, CHUNK * (j + 1))
                ps = slice(CHUNK * jj, CHUNK * (jj + 1))
                xpb = xg[:, ps].astype(BF16)
                dyp = dyg[:, ps]
                dxp = dxs[:, ps]
                for hh in range(2):
                    h = 2 * j + hh
                    ws = slice(CHUNK * (2 * jj + hh), CHUNK * (2 * jj + hh + 1))
                    seg = acs[:, h:h + 1] - acs_t[h:h + 1, :]
                    lm = jnp.exp(jnp.where(causal, seg, -1e30))
                    mf = cb * lm
                    dyh = jnp.where(lo if hh == 0 else ~lo, dyp, 0.0).astype(BF16)
                    gm = _dot(dyh, xpb, _NT)
                    dcb = dcb + gm * lm
                    w = gm * mf
                    whi = w.astype(BF16)
                    whi_scr[:, ws] = whi
                    wlo_scr[:, ws] = (w - whi.astype(F32)).astype(BF16)
                    dacs_t = dacs_t - jnp.where(r0 == h, jnp.sum(w, axis=0, keepdims=True), 0.0)
                    dxp = dxp + _dot(mf.astype(BF16), dyh, _TN)
                dx_scr[:, sl] = dxp
            dxg = dx_scr[:, gs]
            pick_w = (jnp.right_shift(_iota((8 * CHUNK, CHUNK), 0), 7) + 8 * g == _iota((8 * CHUNK, CHUNK), 1)).astype(BF16)
            ch = _dot(cg, hgb, _NT)
            dacs = (dacs + _dot(whi_scr[...], pick_w) + _dot(wlo_scr[...], pick_w)
                    + head_sums(dyg * e_scr[:, gs] * ch - xg * dxs, pick))
            ddt_dir = ddt_dir + head_sums(dxg * xsg, pick)
            dcbb = dcb.astype(BF16)
            dxbc_ref[:, D_SSD + N_STATE * g:D_SSD + N_STATE * (g + 1)] = db + _dot(dcbb, cg, _TN)
            dxbc_ref[:, D_SSD + _GW + N_STATE * g:D_SSD + _GW + N_STATE * (g + 1)] = dc + _dot(dcbb, bg)
        dxbc_ref[:, 0:D_SSD] = dx_scr[...] * dte_scr[...] + dsk_ref[...] * dy_ref[...]
        dacs = dacs + dacs_t.T + jnp.where(r0 == CHUNK - 1, dalast, 0.0)
        dadt = _dot_exact((r1 >= r0).astype(F32), dacs)
        ddt_ref[...] = dadt * a_ref[...] + ddt_dir
        da_ref[...] += jnp.sum(dadt * dt, axis=0, keepdims=True)

    rev = lambda s: (nc - 1 - s, 0)
    return pl.pallas_call(
        body, name="ssd_bwd",
        out_shape=(jax.ShapeDtypeStruct((t, CONV_DIM), F32), jax.ShapeDtypeStruct((t, N_SMALL), F32),
                   jax.ShapeDtypeStruct((1, N_SMALL), F32), jax.ShapeDtypeStruct((1, D_SSD), F32)),
        grid=(nc,),
        in_specs=[pl.BlockSpec((CHUNK, D_SSD), rev),
                  pl.BlockSpec((CHUNK, _GW), lambda s: (nc - 1 - s, 4)),
                  pl.BlockSpec((CHUNK, _GW), lambda s: (nc - 1 - s, 5)),
                  pl.BlockSpec((CHUNK, N_SMALL), rev),
                  pl.BlockSpec((1, N_SMALL), lambda s: (0, 0)),
                  pl.BlockSpec((1, D_SSD), lambda s: (0, 0)),
                  pl.BlockSpec((1, D_SSD, N_STATE), lambda s: (nc - 1 - s, 0, 0)),
                  pl.BlockSpec((CHUNK, D_SSD), rev)],
        out_specs=(pl.BlockSpec((CHUNK, CONV_DIM), rev),
                   pl.BlockSpec((CHUNK, N_SMALL), rev),
                   pl.BlockSpec((1, N_SMALL), lambda s: (0, 0)),
                   pl.BlockSpec((1, D_SSD), lambda s: (0, 0))),
        scratch_shapes=([pltpu.VMEM((D_SSD, N_STATE), F32)] + [pltpu.VMEM((CHUNK, D_SSD), F32)] * 4
                        + [pltpu.VMEM((CHUNK, 8 * CHUNK), BF16)] * 2),
        compiler_params=_cp("arbitrary"),
    )(xbc, xbc, xbc, dtlf, a_row, dsk_row, hin, dy)


_NPAIR = H_ATT // 2
_QB, _KB, _VB = C_Q // 128, C_K // 128, C_V // 128
_SCALE = 1.0 / math.sqrt(64.0)


def _attn_blocks(t):
    return _tile(t, (1408, 384, 256, 128)), _tile(t, (384, 128))


def _split3(c):
    hi = c.astype(BF16).astype(F32)
    rest = c - hi
    mid = rest.astype(BF16).astype(F32)
    return hi, mid, rest - mid


def _head_lanes(lane, hh):
    return (lane < 64, 64) if hh == 0 else (lane >= 64, 0)


def _q_operand(q, cq, lane, hh):
    sel, first = _head_lanes(lane, hh)
    out = jnp.where(sel, q, 0.0)
    for n, col in enumerate(_split3(cq) + (1.0, 1.0, 1.0)):
        out = jnp.where(lane == first + n, col, out)
    return out.astype(BF16)


def _k_operand(k, ck, lane, hh):
    sel, first = _head_lanes(lane, hh)
    hi, mid, lo = _split3(ck)
    out = jnp.where(sel, k, 0.0)
    for n, col in enumerate((1.0, 1.0, 1.0, -hi, -mid, -lo)):
        out = jnp.where(lane == first + n, col, out)
    return out.astype(BF16)


def _needs_mask(i, kk, bq, bk):
    return kk * bk + bk - 1 > i * bq


_C_FILLER = 2.0 ** 30


def _attn_fwd(proj, c_col):
    t = proj.shape[0]
    bq, bk = _attn_blocks(t)
    nq, nk = t // bq, t // bk
    rs = 16

    def last_kv(i):
        return (i * bq + bq - 1) // bk

    def body(q_ref, k_ref, v_ref, cq_ref, ck_ref, o_ref, lse_ref, p_ref, mrun_ref, qs_scr, s_scr, m_scr, acc_scr):
        i = pl.program_id(1)
        kk = pl.program_id(2)
        lane_q = _iota((bq, 128), 1)

        @pl.when(kk == 0)
        def _():
            m_scr[...] = jnp.full_like(m_scr, -1e30)
            acc_scr[...] = jnp.zeros_like(acc_scr)
            q = q_ref[...] * _SCALE
            cq = cq_ref[0]
            for hh in range(2):
                qs_scr[hh] = _q_operand(q, cq[:, hh:hh + 1], lane_q, hh)

        def step(masked):
            lane_k = _iota((bk, 128), 1)
            k = k_ref[...]
            v = v_ref[...]
            ck = ck_ref[0]
            ahead = _iota((rs, bq), 0) - _iota((rs, bq), 1)
            vss = []
            for hh in range(2):
                sel, first = _head_lanes(lane_k, hh)
                ks = _k_operand(k, ck[:, hh:hh + 1], lane_k, hh)
                vss.append(jnp.where(sel, v, jnp.where(lane_k == first, 1.0, 0.0)).astype(BF16))
                s_scr[hh] = _dot(ks, qs_scr[hh], _NT)
            for hh in range(2):
                vs = vss[hh]

                def block_max(r, mx):
                    rows = pl.ds(pl.multiple_of(r * rs, rs), rs)
                    s = s_scr[hh, rows, :]
                    if masked:
                        s = jnp.where(ahead <= i * bq - kk * bk - r * rs, s, -1e30)
                        s_scr[hh, rows, :] = s
                    return jnp.maximum(mx, s)

                mx = lax.fori_loop(0, bk // rs, block_max, jnp.full((rs, bq), -1e30, F32), unroll=True)
                m_old = m_scr[hh]
                m_new = jnp.maximum(m_old, jnp.max(mx, axis=0, keepdims=True))
                m_scr[hh] = m_new
                mrun_ref[0, hh:hh + 1, :] = m_new

                def probs(r, carry):
                    rows = pl.ds(pl.multiple_of(r * rs, rs), rs)
                    p_ref[0, hh, rows, :] = jnp.exp(s_scr[hh, rows, :] - m_new).astype(BF16)
                    return carry

                lax.fori_loop(0, bk // rs, probs, 0, unroll=True)
                acc_scr[hh] = acc_scr[hh] * jnp.exp(m_old - m_new) + _dot(vs, p_ref[0, hh], _TN)

        active = kk <= last_kv(i)
        masked = _needs_mask(i, kk, bq, bk)

        @pl.when(active & masked)
        def _():
            step(True)

        @pl.when(active & jnp.logical_not(masked))
        def _():
            step(False)

        @pl.when(kk == nk - 1)
        def _():
            a = acc_scr[0]
            b = acc_scr[1]
            la = a[64:65, :]
            lb = b[0:1, :]
            o_ref[...] = jnp.where(lane_q < 64, (a / la).T, (b / lb).T)
            lse_ref[0] = jnp.concatenate([m_scr[0] + jnp.log(la), m_scr[1] + jnp.log(lb)], axis=0)

    kvi = lambda i, kk: jnp.minimum(kk, last_kv(i))
    kv = lambda off: pl.BlockSpec((bk, 128), lambda j, i, kk: (kvi(i, kk), off + j))
    blk = lambda j, i, kk: (j * nq + i) * nk + kvi(i, kk)
    return pl.pallas_call(
        body, name="attn_fwd",
        out_shape=(jax.ShapeDtypeStruct((t, D_ATT), F32), jax.ShapeDtypeStruct((_NPAIR, 2, t), F32),
                   jax.ShapeDtypeStruct((_NPAIR * nq * nk, 2, bk, bq), BF16),
                   jax.ShapeDtypeStruct((_NPAIR * nq * nk, 2, bq), F32)),
        grid=(_NPAIR, nq, nk),
        in_specs=[pl.BlockSpec((bq, 128), lambda j, i, kk: (i, _QB + j)),
                  kv(_KB), kv(_VB),
                  pl.BlockSpec((1, bq, 2), lambda j, i, kk: (j, i, 0)),
                  pl.BlockSpec((1, bk, 2), lambda j, i, kk: (j, kvi(i, kk), 0))],
        out_specs=(pl.BlockSpec((bq, 128), lambda j, i, kk: (i, j)),
                   pl.BlockSpec((1, 2, bq), lambda j, i, kk: (j, 0, i)),
                   pl.BlockSpec((1, 2, bk, bq), lambda j, i, kk: (blk(j, i, kk), 0, 0, 0)),
                   pl.BlockSpec((1, 2, bq), lambda j, i, kk: (blk(j, i, kk), 0, 0))),
        scratch_shapes=[pltpu.VMEM((2, bq, 128), BF16), pltpu.VMEM((2, bk, bq), F32),
                        pltpu.VMEM((2, 1, bq), F32), pltpu.VMEM((2, 128, bq), F32)],
        compiler_params=_cp("parallel", "parallel", "arbitrary"),
    )(proj, proj, proj, c_col, c_col)


def _attn_delta(do, o):
    t = do.shape[0]
    tm = _row_tile(t)

    def body(do_ref, o_ref, d_ref):
        pick = (jnp.right_shift(_iota((D_ATT, 128), 0), 6) == _iota((D_ATT, 128), 1)).astype(F32)
        d_ref[...] = _dot_exact(do_ref[...] * o_ref[...], pick)

    row = pl.BlockSpec((tm, D_ATT), lambda i: (i, 0))
    return pl.pallas_call(
        body, name="attn_delta",
        out_shape=jax.ShapeDtypeStruct((t, 128), F32),
        grid=(t // tm,), in_specs=[row, row], out_specs=pl.BlockSpec((tm, 128), lambda i: (i, 0)),
        compiler_params=_cp("parallel"),
    )(do, o)


def _attn_bwd(proj, c_col, lse_row, dl_row, do, p_blocks, m_run, exchange=None):
    t = proj.shape[0]
    bq, bk = _attn_blocks(t)
    nq, nk = t // bq, t // bk
    rs = 16

    def first_q(kk):
        return (kk * bk) // bq

    def body(q_ref, k_ref, v_ref, cq_ref, ck_ref, lse_ref, dl_ref, do_ref, pblk_ref, mrun_ref,
             dq_ref, dk_ref, dv_ref, dck_ref, dcq_ref,
             qs_scr, doh_scr, ks_scr, dp_scr, p_scr, ds_scr, dq_scr, dk_scr, dv_scr):
        kk = pl.program_id(1)
        i = pl.program_id(2)
        lane_q = _iota((bq, 128), 1)
        lane_k = _iota((bk, 128), 1)
        qrows = pl.ds(pl.multiple_of(i * bq, 128), bq)

        @pl.when(kk == 0)
        def _():
            q = q_ref[...] * _SCALE
            cq = cq_ref[0]
            do_ = do_ref[...]
            for hh in range(2):
                qs_scr[hh, qrows, :] = _q_operand(q, cq[:, hh:hh + 1], lane_q, hh)
                doh_scr[hh, qrows, :] = jnp.where(_head_lanes(lane_q, hh)[0], do_, 0.0).astype(BF16)
                dq_scr[hh, i] = jnp.zeros((128, bq), F32)

        @pl.when(i == 0)
        def _():
            dk_scr[...] = jnp.zeros_like(dk_scr)
            dv_scr[...] = jnp.zeros_like(dv_scr)
            k = k_ref[...]
            ck = ck_ref[0]
            for hh in range(2):
                ks_scr[hh] = _k_operand(k, ck[:, hh:hh + 1], lane_k, hh)

        @pl.when(i >= first_q(kk))
        def _():
            v16 = v_ref[...].astype(BF16)
            dl = dl_ref[0]
            rescale = jnp.exp(mrun_ref[0] - lse_ref[0])
            for hh in range(2):
                dp_scr[hh] = _dot(v16, doh_scr[hh, qrows, :], _NT)
            for hh in range(2):
                qs = qs_scr[hh, qrows, :]
                doh = doh_scr[hh, qrows, :]

                def strip(r, carry):
                    rows = pl.ds(pl.multiple_of(r * rs, rs), rs)
                    p = pblk_ref[0, hh, rows, :].astype(F32) * rescale[hh:hh + 1, :]
                    p_scr[hh, rows, :] = p.astype(BF16)
                    ds_scr[hh, rows, :] = (p * (dp_scr[hh, rows, :] - dl[hh:hh + 1, :])).astype(BF16)
                    return carry

                lax.fori_loop(0, bk // rs, strip, 0, unroll=True)
                dv_scr[...] += _dot(p_scr[hh], doh)
                dk_scr[hh] += _dot(ds_scr[hh], qs)
                dq_scr[hh, i] += _dot(ks_scr[hh], ds_scr[hh], _TN)

        @pl.when(i == nq - 1)
        def _():
            dka = dk_scr[0]
            dkb = dk_scr[1]
            dk_ref[...] = jnp.where(lane_k < 64, dka, dkb).astype(BF16)
            dv_ref[...] = dv_scr[...].astype(BF16)
            dck_ref[0] = -jnp.where(_iota((bk, 2), 1) == 0, dka[:, 67:68], dkb[:, 3:4])

        @pl.when((kk == nk - 1) & (i == nq - 1))
        def _():
            for ii in range(nq):
                cols = slice(ii * bq, (ii + 1) * bq)
                dqa = dq_scr[0, ii]
                dqb = dq_scr[1, ii]
                dq_ref[cols, :] = (jnp.where(lane_q < 64, dqa.T, dqb.T) * _SCALE).astype(BF16)
                dcq_ref[0, :, cols] = jnp.concatenate([dqa[64:65, :], dqb[0:1, :]], axis=0)

    qi = lambda kk, i: jnp.where(kk == 0, i, nq - 1)
    qspec = lambda off: pl.BlockSpec((bq, 128), lambda j, kk, i: (qi(kk, i), off + j))
    kspec = lambda off: pl.BlockSpec((bk, 128), lambda j, kk, i: (kk, off + j))
    rowspec = pl.BlockSpec((1, 2, bq), lambda j, kk, i: (j, 0, jnp.maximum(i, first_q(kk))))
    blk = lambda j, kk, i: (j * nq + jnp.maximum(i, first_q(kk))) * nk + kk
    return _hosted_call(
        body, name="attn_bwd",
        out_shape=(jax.ShapeDtypeStruct((t, D_ATT), BF16), jax.ShapeDtypeStruct((t, D_ATT), BF16),
                   jax.ShapeDtypeStruct((t, D_ATT), BF16), jax.ShapeDtypeStruct((_NPAIR, t, 2), F32),
                   jax.ShapeDtypeStruct((_NPAIR, 2, t), F32)),
        grid=(_NPAIR, nk, nq),
        in_specs=[qspec(_QB), kspec(_KB), kspec(_VB),
                  pl.BlockSpec((1, bq, 2), lambda j, kk, i: (j, qi(kk, i), 0)),
                  pl.BlockSpec((1, bk, 2), lambda j, kk, i: (j, kk, 0)),
                  rowspec, rowspec, qspec(0),
                  pl.BlockSpec((1, 2, bk, bq), lambda j, kk, i: (blk(j, kk, i), 0, 0, 0)),
                  pl.BlockSpec((1, 2, bq), lambda j, kk, i: (blk(j, kk, i), 0, 0))],
        out_specs=(pl.BlockSpec((t, 128), lambda j, kk, i: (0, j)),
                   pl.BlockSpec((bk, 128), lambda j, kk, i: (kk, j)),
                   pl.BlockSpec((bk, 128), lambda j, kk, i: (kk, j)),
                   pl.BlockSpec((1, bk, 2), lambda j, kk, i: (j, kk, 0)),
                   pl.BlockSpec((1, 2, t), lambda j, kk, i: (j, 0, 0))),
        scratch_shapes=[pltpu.VMEM((2, t, 128), BF16), pltpu.VMEM((2, t, 128), BF16), pltpu.VMEM((2, bk, 128), BF16),
                        pltpu.VMEM((2, bk, bq), F32),
                        pltpu.VMEM((2, bk, bq), BF16), pltpu.VMEM((2, bk, bq), BF16),
                        pltpu.VMEM((2, nq, 128, bq), F32), pltpu.VMEM((2, bk, 128), F32), pltpu.VMEM((bk, 128), F32)],
        operands=(proj, proj, proj, c_col, c_col, lse_row, dl_row, do, p_blocks, m_run),
        semantics=("parallel", "arbitrary", "arbitrary"), exchange=exchange)


def _premerge_fwd(y, o, proj, gamma):
    t = y.shape[0]
    tm = _row_tile_wide(t)

    def body(y_ref, z_ref, o_ref, za_ref, g_ref, ys_ref, ya_ref):
        z = z_ref[...]
        u = y_ref[...] * (z * _sigmoid(z))
        for g in range(G_SSD):
            gs = slice(_GW * g, _GW * (g + 1))
            ug = u[:, gs]
            r = lax.rsqrt(jnp.mean(ug * ug, axis=-1, keepdims=True) + EPS)
            ys_ref[:, gs] = (ug * r * g_ref[:, gs]).astype(BF16)
        za = za_ref[...]
        ya_ref[...] = (o_ref[...] * (za * _sigmoid(za))).astype(BF16)

    return pl.pallas_call(
        body, name="premerge_fwd",
        out_shape=(jax.ShapeDtypeStruct((t, D_SSD), BF16), jax.ShapeDtypeStruct((t, D_ATT), BF16)),
        grid=(t // tm,),
        in_specs=[pl.BlockSpec((tm, D_SSD), lambda i: (i, 0)),
                  pl.BlockSpec((tm, D_SSD), lambda i: (i, C_Z // D_SSD)),
                  pl.BlockSpec((tm, D_ATT), lambda i: (i, 0)),
                  pl.BlockSpec((tm, D_ATT), lambda i: (i, C_ZA // D_ATT)),
                  pl.BlockSpec((1, D_SSD), lambda i: (0, 0))],
        out_specs=(pl.BlockSpec((tm, D_SSD), lambda i: (i, 0)), pl.BlockSpec((tm, D_ATT), lambda i: (i, 0))),
        compiler_params=_cp("parallel"),
    )(y, proj, o, proj, gamma)


def _premerge_bwd(dys, dya, y, o, proj, gamma, exchange=None):
    t = y.shape[0]
    tm = _row_tile_wide(t)

    def body(dys_ref, dya_ref, y_ref, z_ref, o_ref, za_ref, g_ref, dy_ref, dz_ref, do_ref, dza_ref, dg_ref):
        i = pl.program_id(0)
        z = z_ref[...]
        sz = _sigmoid(z)
        silu = z * sz
        dsilu = sz * (1.0 + z * (1.0 - sz))
        yv = y_ref[...]
        u = yv * silu
        parts = []
        for g in range(G_SSD):
            gs = slice(_GW * g, _GW * (g + 1))
            ug = u[:, gs]
            r = lax.rsqrt(jnp.mean(ug * ug, axis=-1, keepdims=True) + EPS)
            n = ug * r
            dout = dys_ref[:, gs]
            dn = dout * g_ref[:, gs]
            du = r * (dn - n * jnp.mean(dn * n, axis=-1, keepdims=True))
            dy_ref[:, gs] = du * silu[:, gs]
            dz_ref[:, gs] = (du * yv[:, gs] * dsilu[:, gs]).astype(BF16)
            parts.append(jnp.sum(dout * n, axis=0, keepdims=True))
        dg = jnp.concatenate(parts, axis=1)
        za = za_ref[...]
        sa = _sigmoid(za)
        dya_ = dya_ref[...]
        do_ref[...] = dya_ * (za * sa)
        dza_ref[...] = (dya_ * o_ref[...] * (sa * (1.0 + za * (1.0 - sa)))).astype(BF16)

        @pl.when(i == 0)
        def _():
            dg_ref[...] = dg

        @pl.when(i > 0)
        def _():
            dg_ref[...] += dg

    ssd = pl.BlockSpec((tm, D_SSD), lambda i: (i, 0))
    att = pl.BlockSpec((tm, D_ATT), lambda i: (i, 0))
    vec = pl.BlockSpec((1, D_SSD), lambda i: (0, 0))
    return _hosted_call(
        body, name="premerge_bwd",
        out_shape=(jax.ShapeDtypeStruct((t, D_SSD), F32), jax.ShapeDtypeStruct((t, D_SSD), BF16),
                   jax.ShapeDtypeStruct((t, D_ATT), F32), jax.ShapeDtypeStruct((t, D_ATT), BF16),
                   jax.ShapeDtypeStruct((1, D_SSD), F32)),
        grid=(t // tm,),
        in_specs=[ssd, att, ssd, pl.BlockSpec((tm, D_SSD), lambda i: (i, C_Z // D_SSD)), att,
                  pl.BlockSpec((tm, D_ATT), lambda i: (i, C_ZA // D_ATT)), vec],
        out_specs=(ssd, ssd, att, att, vec),
        scratch_shapes=[],
        operands=(dys, dya, y, proj, o, proj, gamma), semantics=("arbitrary",), exchange=exchange)


_G_BLK = C_G // D_MODEL


def _merge_fwd(a, b, proj, gate_bias):
    t = a.shape[0]
    tm = _row_tile(t)

    def body(a_ref, b_ref, gs_ref, ga_ref, bias_ref, m_ref):
        g_ssd = _sigmoid(gs_ref[...] + bias_ref[:, 0:D_MODEL])
        g_att = _sigmoid(ga_ref[...] + bias_ref[:, D_MODEL:2 * D_MODEL])
        m_ref[...] = (g_ssd * a_ref[...] + g_att * b_ref[...]).astype(BF16)

    row = pl.BlockSpec((tm, D_MODEL), lambda i: (i, 0))
    return pl.pallas_call(
        body, name="merge_fwd",
        out_shape=jax.ShapeDtypeStruct((t, D_MODEL), BF16),
        grid=(t // tm,),
        in_specs=[row, row,
                  pl.BlockSpec((tm, D_MODEL), lambda i: (i, _G_BLK)),
                  pl.BlockSpec((tm, D_MODEL), lambda i: (i, _G_BLK + 1)),
                  pl.BlockSpec((1, 2 * D_MODEL), lambda i: (0, 0))],
        out_specs=row,
        compiler_params=_cp("parallel"),
    )(a, b, proj, proj, gate_bias)


def _merge_bwd(dm, a, b, proj, gate_bias):
    t = a.shape[0]
    tm = _row_tile(t)

    def body(dm_ref, a_ref, b_ref, gs_ref, ga_ref, bias_ref, da_ref, db_ref, dg_ref, dbias_ref):
        i = pl.program_id(0)
        dm_ = dm_ref[...]
        g_ssd = _sigmoid(gs_ref[...] + bias_ref[:, 0:D_MODEL])
        g_att = _sigmoid(ga_ref[...] + bias_ref[:, D_MODEL:2 * D_MODEL])
        da_ref[...] = (dm_ * g_ssd).astype(BF16)
        db_ref[...] = (dm_ * g_att).astype(BF16)
        dgs = dm_ * a_ref[...] * g_ssd * (1.0 - g_ssd)
        dga = dm_ * b_ref[...] * g_att * (1.0 - g_att)
        dg_ref[:, 0:D_MODEL] = dgs.astype(BF16)
        dg_ref[:, D_MODEL:2 * D_MODEL] = dga.astype(BF16)
        part = jnp.concatenate([jnp.sum(dgs, axis=0, keepdims=True), jnp.sum(dga, axis=0, keepdims=True)], axis=1)

        @pl.when(i == 0)
        def _():
            dbias_ref[...] = part

        @pl.when(i > 0)
        def _():
            dbias_ref[...] += part

    row = pl.BlockSpec((tm, D_MODEL), lambda i: (i, 0))
    wide = pl.BlockSpec((tm, 2 * D_MODEL), lambda i: (i, 0))
    vec = pl.BlockSpec((1, 2 * D_MODEL), lambda i: (0, 0))
    return pl.pallas_call(
        body, name="merge_bwd",
        out_shape=(jax.ShapeDtypeStruct((t, D_MODEL), BF16), jax.ShapeDtypeStruct((t, D_MODEL), BF16),
                   jax.ShapeDtypeStruct((t, 2 * D_MODEL), BF16), jax.ShapeDtypeStruct((1, 2 * D_MODEL), F32)),
        grid=(t // tm,),
        in_specs=[row, row, row,
                  pl.BlockSpec((tm, D_MODEL), lambda i: (i, _G_BLK)),
                  pl.BlockSpec((tm, D_MODEL), lambda i: (i, _G_BLK + 1)), vec],
        out_specs=(row, row, wide, vec),
        compiler_params=_cp("arbitrary"),
    )(dm, a, b, proj, proj, gate_bias)


def _post(o2, h, target, g):
    t = o2.shape[0]
    nc = t // CHUNK

    def body(o_ref, h_ref, t_ref, g_ref, dy_ref, do_ref, dg_ref, loss_ref):
        c = pl.program_id(0)
        x = o_ref[...]
        r = lax.rsqrt(jnp.mean(x * x, axis=-1, keepdims=True) + EPS)
        n = x * r
        y = h_ref[...] + n * g_ref[...]
        diff = jnp.where(c > 0, y - t_ref[...], 0.0)
        dy = diff * (1.0 / D_MODEL)
        dy_ref[...] = dy
        gdy = dy * g_ref[...]
        do_ref[...] = (r * (gdy - n * jnp.mean(gdy * n, axis=-1, keepdims=True))).astype(BF16)
        dg = jnp.sum(dy * n, axis=0, keepdims=True)
        lpart = 0.5 * jnp.sum(jnp.sum(diff * diff, axis=1, keepdims=True), axis=0, keepdims=True) * (1.0 / D_MODEL)
        sel = (_iota((8, 128), 0) == 0) & (_iota((8, 128), 1) == 0)

        @pl.when(c == 0)
        def _():
            dg_ref[...] = dg
            loss_ref[...] = jnp.zeros_like(loss_ref)

        @pl.when(c > 0)
        def _():
            dg_ref[...] += dg
            loss_ref[...] += jnp.where(sel, lpart, 0.0)

    row = pl.BlockSpec((CHUNK, D_MODEL), lambda c: (c, 0))
    vec = pl.BlockSpec((1, D_MODEL), lambda c: (0, 0))
    return pl.pallas_call(
        body, name="post",
        out_shape=(jax.ShapeDtypeStruct((t, D_MODEL), F32), jax.ShapeDtypeStruct((t, D_MODEL), BF16),
                   jax.ShapeDtypeStruct((1, D_MODEL), F32), jax.ShapeDtypeStruct((8, 128), F32)),
        grid=(nc,),
        in_specs=[row, row, pl.BlockSpec((CHUNK, D_MODEL), lambda c: (jnp.maximum(c - 1, 0), 0)), vec],
        out_specs=(row, row, vec, pl.BlockSpec((8, 128), lambda c: (0, 0))),
        compiler_params=_cp("arbitrary"),
    )(o2, h, target, g)


def _mm_tiles(t):
    return _tile(t, (704, 384, 128))


def _local_step(h, target, w_main, w_small, pr_slots, ids, norm_pre, conv_w, conv_b, bias_row, a_row,
                dsk_row, ssd_norm, gate_bias, norm_post):
    t = h.shape[0]
    tm = _mm_tiles(t)
    u = _norm1_fwd(h, norm_pre)
    proj, pr_slots = _matmul(u, w_main, "nt", F32, "inproj", tm, 1024, D_MODEL,
                             exchange=_gather_stage([pr_slots], to_sibling=False))
    small, pr_slots = _matmul(u, w_small, "nt", F32, "inproj_small", tm, N_SMALL, D_MODEL,
                              exchange=_gather_stage([pr_slots], to_sibling=True))
    wps = pr_slots[:, 0:512].reshape(D_SSD, D_MODEL)
    wpa = pr_slots[:, 512:768].reshape(D_ATT, D_MODEL)
    wout = pr_slots[:, 768:1024].reshape(D_MODEL, D_MODEL)
    dtlf = _small_fwd(small, bias_row)
    xbc = _conv_fwd(proj, conv_w, conv_b)
    y, hin = _ssd_fwd(xbc, dtlf, a_row, dsk_row)
    c_tok = dtlf[:, H_SSD:H_SSD + H_ATT]
    c_tok = jnp.where(jnp.arange(t)[:, None] < PADF, _C_FILLER, c_tok)
    c_col = c_tok.reshape(t, _NPAIR, 2).transpose(1, 0, 2)
    o, lse, p_blocks, m_run = _attn_fwd(proj, c_col)
    ys, ya = _premerge_fwd(y, o, proj, ssd_norm)
    a = _matmul(ys, wps, "nn", F32, "proj_ssd", tm, D_MODEL, D_SSD)
    b = _matmul(ya, wpa, "nn", F32, "proj_att", tm, D_MODEL, D_ATT)
    merged = _merge_fwd(a, b, proj, gate_bias)
    o2 = _matmul(merged, wout, "nn", F32, "out_proj", tm, D_MODEL, D_MODEL)
    dy_out, do2, d_norm_post, loss_blk = _post(o2, h, target, norm_post)

    dm = _matmul(do2, wout, "nt", F32, "out_proj_dx", tm, D_MODEL, D_MODEL)
    d_wout = _matmul(merged, do2, "tn", F32, "out_proj_dw", D_MODEL, D_MODEL, tm)
    da, db, dgraw, d_gate_bias = _merge_bwd(dm, a, b, proj, gate_bias)
    dys = _matmul(da, wps, "nt", F32, "proj_ssd_dx", tm, D_SSD, D_MODEL)
    d_wps = _matmul(ys, da, "tn", F32, "proj_ssd_dw", D_SSD, D_MODEL, tm)
    dya = _matmul(db, wpa, "nt", F32, "proj_att_dx", tm, D_ATT, D_MODEL)
    d_wpa = _matmul(ya, db, "tn", F32, "proj_att_dw", D_ATT, D_MODEL, tm)
    g32_pr = jnp.concatenate([d_wps.reshape(4, 512, D_MODEL), d_wpa.reshape(4, 256, D_MODEL),
                              d_wout.reshape(4, 256, D_MODEL)], axis=1)
    dy, dz, do, dza, d_ssd_norm, ra_pr = _premerge_bwd(dys, dya, y, o, proj, ssd_norm,
                                                       exchange=_pair_swap([g32_pr]))
    pb_pr = _add_pair(ids, g32_pr, ra_pr)
    dl_row = _attn_delta(do, o)[:, 0:H_ATT].T.reshape(_NPAIR, 2, t)
    dq, dk, dv, dc_key, dc_qry, rb_pr = _attn_bwd(proj, c_col, lse, dl_row, do, p_blocks, m_run,
                                                  exchange=_chip_exchange([pb_pr]))
    half_pr = _add_chips(ids, g32_pr, ra_pr, rb_pr)
    dxbc, ddt, d_a, d_dsk = _ssd_bwd(xbc, dtlf, a_row, dsk_row, hin, dy)
    dxbc_raw, d_conv_w, d_conv_b = _conv_bwd(dxbc, proj, conv_w, conv_b)
    dc_tok = jnp.transpose(dc_key, (1, 0, 2)).reshape(t, H_ATT) + dc_qry.reshape(H_ATT, t).T
    dsm = ddt + jnp.pad(dc_tok, ((0, 0), (H_SSD, N_SMALL - H_SSD - H_ATT)))
    dsmall, d_bias_row = _small_bwd(dsm, small, bias_row)
    dproj = [dz, dxbc_raw, dza, dq, dk, dv, dgraw]
    return dict(loss_blk=loss_blk, u=u, dy_out=dy_out, dproj=dproj, dsmall=dsmall, half_pr=half_pr,
                d_conv_w=d_conv_w, d_conv_b=d_conv_b,
                d_bias_row=d_bias_row, d_a=d_a, d_dsk=d_dsk, d_ssd_norm=d_ssd_norm,
                d_gate_bias=d_gate_bias, d_norm_post=d_norm_post)


def _to_aligned_rows(slots):
    w = slots.reshape(N_COLS, slots.shape[2])

    def cut(o):
        return w[o[0]:o[0] + o[1]]
    main = jnp.concatenate([cut(O_Z), cut(O_XBC), cut(O_ZA), cut(O_Q), cut(O_K), cut(O_V), cut(O_G)], axis=0)
    pad = jnp.zeros((N_SMALL - H_SSD - H_ATT, w.shape[1]), w.dtype)
    small = jnp.concatenate([cut(O_DT), cut(O_F), pad], axis=0)
    return main, small


def _from_aligned_rows(main, small):
    def cm(c0, n):
        return main[c0:c0 + n]
    flat = jnp.concatenate([cm(C_Z, 2048), cm(C_XBC, 3072), small[0:H_SSD], cm(C_ZA, 1024),
                            cm(C_Q, 1024), cm(C_K, 1024), cm(C_V, 1024), small[H_SSD:H_SSD + H_ATT],
                            cm(C_G, 2048)], axis=0)
    return flat.reshape(4, N_COLS // 4, flat.shape[1])


_MESH = pl.DeviceIdType.MESH
_ANY = pl.BlockSpec(memory_space=pl.ANY)
_VM = pl.BlockSpec(memory_space=pltpu.VMEM)
_HALF = 512
N_DEV = 8


def _coords():
    return lax.axis_index("x"), lax.axis_index("y"), lax.axis_index("c")


def _other_chips(x, y):
    return [(1 - x, y), (x, 1 - y), (1 - x, 1 - y)]


def _half(cc):
    return pl.ds(cc * _HALF, _HALF)


def _gather_shards(slots):
    n = len(slots)

    def body(*refs):
        buf = refs[n:2 * n]
        send_sems, recv_sems = refs[2 * n:]
        x, y, c = _coords()
        chip = 2 * x + y
        sibling = (x, y, 1 - c)
        chips = _other_chips(x, y)

        def copy(i, frm, cc, k, to):
            part = buf[i].at[frm, :, _half(cc)]
            return pltpu.make_async_remote_copy(src_ref=part, dst_ref=part, send_sem=send_sems.at[6 * i + k],
                                                recv_sem=recv_sems.at[6 * i + k], device_id=to, device_id_type=_MESH)

        def chip_of(k):
            return 2 * chips[k][0] + chips[k][1]

        first = [copy(i, chip, c, k, (*chips[k], c)) for k in range(3) for i in range(n)]
        for cp in first:
            cp.start()
        passed = []
        for k in range(3):
            for i in range(n):
                copy(i, chip_of(k), c, k, (*chips[k], c)).wait_recv()
                passed.append(copy(i, chip_of(k), c, 3 + k, sibling))
                passed[-1].start()
        for k in range(3):
            for i in range(n):
                copy(i, chip_of(k), 1 - c, 3 + k, sibling).wait_recv()
        for cp in first + passed:
            cp.wait_send()

    return pl.pallas_call(
        body, name="gather_shards",
        out_shape=tuple(jax.ShapeDtypeStruct(s.shape, s.dtype) for s in slots),
        in_specs=[_ANY] * n, out_specs=tuple([_ANY] * n),
        input_output_aliases={i: i for i in range(n)},
        scratch_shapes=[pltpu.SemaphoreType.DMA((6 * n,)), pltpu.SemaphoreType.DMA((6 * n,))],
    )(*slots)


def _allgather8(block, name):
    rows, width = block.shape

    def body(x_ref, out_ref, send_sems, recv_sems, local_sem):
        x, y, c = _coords()
        me, sibling = (x, y, c), (x, y, 1 - c)
        chips = _other_chips(x, y)

        def slot(px, py, pc):
            return out_ref.at[4 * px + 2 * py + pc]

        def copy(k, blk, to, src=None):
            return pltpu.make_async_remote_copy(src_ref=slot(*blk) if src is None else src, dst_ref=slot(*blk),
                                                send_sem=send_sems.at[k], recv_sem=recv_sems.at[k],
                                                device_id=to, device_id_type=_MESH)

        mine = pltpu.make_async_copy(x_ref, slot(*me), local_sem)
        mine.start()
        first = [copy(0, me, sibling, src=x_ref)]
        first += [copy(1 + j, me, (*chip, c), src=x_ref) for j, chip in enumerate(chips)]
        for cp in first:
            cp.start()
        passed = [copy(4 + j, (*chip, c), sibling) for j, chip in enumerate(chips)]
        for j, chip in enumerate(chips):
            copy(1 + j, (*chip, c), me).wait_recv()
            passed[j].start()
        copy(0, sibling, me).wait_recv()
        for j, chip in enumerate(chips):
            copy(4 + j, (*chip, 1 - c), me).wait_recv()
        for cp in first + passed:
            cp.wait_send()
        mine.wait()

    return pl.pallas_call(
        body, name=name,
        out_shape=jax.ShapeDtypeStruct((N_DEV, rows, width), block.dtype),
        in_specs=[_VM], out_specs=_VM,
        scratch_shapes=[pltpu.SemaphoreType.DMA((7,)), pltpu.SemaphoreType.DMA((7,)), pltpu.SemaphoreType.DMA],
    )(block)


def _pair_swap(arrs):
    def copies(src, dst, send_sems, recv_sems):
        x, y, c = _coords()
        return [pltpu.make_async_remote_copy(src_ref=src[i].at[:, :, _half(1 - c)], dst_ref=dst[i],
                                             send_sem=send_sems.at[i], recv_sem=recv_sems.at[i],
                                             device_id=(x, y, 1 - c), device_id_type=_MESH) for i in range(len(src))]

    shapes = tuple(jax.ShapeDtypeStruct((4, a.shape[1], _HALF), a.dtype) for a in arrs)
    return tuple(arrs), shapes, copies, len(arrs), False


def _chip_exchange(arrs):
    def copies(src, dst, send_sems, recv_sems):
        x, y, c = _coords()
        chips = _other_chips(x, y)
        return [pltpu.make_async_remote_copy(src_ref=src[i].at[2 * chips[k][0] + chips[k][1]], dst_ref=dst[i].at[k],
                                             send_sem=send_sems.at[3 * i + k], recv_sem=recv_sems.at[3 * i + k],
                                             device_id=(*chips[k], c), device_id_type=_MESH)
                for k in range(3) for i in range(len(src))]

    shapes = tuple(jax.ShapeDtypeStruct((3,) + a.shape[1:], a.dtype) for a in arrs)
    return tuple(arrs), shapes, copies, 3 * len(arrs), False


def _gather_stage(slots, to_sibling):
    def copies(buf, _, send_sems, recv_sems):
        x, y, c = _coords()
        chips = _other_chips(x, y)
        out = []
        for k in range(3):
            for i in range(len(buf)):
                frm = 2 * chips[k][0] + chips[k][1] if to_sibling else 2 * x + y
                part = buf[i].at[frm, :, _half(c)]
                out.append(pltpu.make_async_remote_copy(
                    src_ref=part, dst_ref=part, send_sem=send_sems.at[3 * i + k], recv_sem=recv_sems.at[3 * i + k],
                    device_id=(x, y, 1 - c) if to_sibling else (*chips[k], c), device_id_type=_MESH))
        return out

    shapes = tuple(jax.ShapeDtypeStruct(s.shape, s.dtype) for s in slots)
    return tuple(slots), shapes, copies, 3 * len(slots), True


def _pair_join_halves(fulls):
    n = len(fulls)

    def body(*refs):
        buf = refs[n:2 * n]
        send_sems, recv_sems = refs[2 * n:]
        x, y, c = _coords()

        def remote(i, cc):
            part = buf[i].at[:, _half(cc)]
            return pltpu.make_async_remote_copy(src_ref=part, dst_ref=part, send_sem=send_sems.at[i],
                                                recv_sem=recv_sems.at[i], device_id=(x, y, 1 - c), device_id_type=_MESH)

        for i in range(n):
            remote(i, c).start()
        for i in range(n):
            remote(i, c).wait_send()
            remote(i, 1 - c).wait_recv()

    return pl.pallas_call(
        body, name="pair_join_halves",
        out_shape=tuple(jax.ShapeDtypeStruct(a.shape, a.dtype) for a in fulls),
        in_specs=[_ANY] * n, out_specs=tuple([_ANY] * n),
        input_output_aliases={i: i for i in range(n)},
        scratch_shapes=[pltpu.SemaphoreType.DMA((n,)), pltpu.SemaphoreType.DMA((n,))],
    )(*fulls)


_RED_TC = 128
_RED_NT = _HALF // _RED_TC


def _add_pair(ids, g32, recv_a):
    rows = g32.shape[1]

    def body(ids_ref, g_ref, r_ref, o_ref):
        o_ref[...] = (g_ref[...] + r_ref[...]).astype(BF16)

    blk = pl.BlockSpec((1, rows, _RED_TC), lambda j, l, ids: (j, 0, l))
    return pl.pallas_call(
        body, name="add_pair",
        out_shape=jax.ShapeDtypeStruct((4, rows, _HALF), BF16),
        grid_spec=pltpu.PrefetchScalarGridSpec(
            num_scalar_prefetch=1, grid=(4, _RED_NT),
            in_specs=[pl.BlockSpec((1, rows, _RED_TC), lambda j, l, ids: (j, 0, ids[0] * _RED_NT + l)), blk],
            out_specs=blk),
        compiler_params=_cp("parallel", "parallel"),
    )(ids, g32, recv_a)


def _add_chips(ids, g32, recv_a, recv_b):
    rows = g32.shape[1]

    def body(ids_ref, g_ref, a_ref, b_ref, o_ref):
        acc = g_ref[0] + a_ref[0]
        for k in range(3):
            acc = acc + b_ref[k].astype(F32)
        o_ref[...] = acc

    return pl.pallas_call(
        body, name="add_chips",
        out_shape=jax.ShapeDtypeStruct((rows, 2 * _HALF), F32),
        grid_spec=pltpu.PrefetchScalarGridSpec(
            num_scalar_prefetch=1, grid=(_RED_NT,),
            in_specs=[pl.BlockSpec((1, rows, _RED_TC), lambda l, ids: (ids[1], 0, ids[0] * _RED_NT + l)),
                      pl.BlockSpec((1, rows, _RED_TC), lambda l, ids: (ids[1], 0, l)),
                      pl.BlockSpec((3, rows, _RED_TC), lambda l, ids: (0, 0, l))],
            out_specs=pl.BlockSpec((rows, _RED_TC), lambda l, ids: (0, ids[0] * _RED_NT + l))),
        compiler_params=_cp("parallel"),
    )(ids, g32, recv_a, recv_b)


def _sum8(gathered):
    _, rows, width = gathered.shape

    def body(g_ref, o_ref):
        acc = g_ref[0]
        for d in range(1, N_DEV):
            acc = acc + g_ref[d]
        o_ref[...] = acc

    return pl.pallas_call(
        body, name="sum8",
        out_shape=jax.ShapeDtypeStruct((rows, width), F32),
        in_specs=[_VM], out_specs=_VM,
    )(gathered)


def _adamw(w, g, m, v, name):
    rows, cols = w.shape
    budget = (3 << 20) // 2
    tr, tc = rows, cols
    if rows * cols * 4 > budget:
        if rows % 8 == 0:
            tr = max(c for c in range(8, rows, 8) if rows % c == 0 and c * cols * 4 <= budget)
        else:
            tc = next(c for c in (512, 256, 128) if cols % c == 0 and rows * c * 4 <= budget)
    c1 = 1.0 - ADAM_B1 ** ADAM_STEP
    c2 = 1.0 - ADAM_B2 ** ADAM_STEP

    def body(w_ref, g_ref, m_ref, v_ref, d_ref, mo_ref, vo_ref):
        gg = g_ref[...]
        mn = ADAM_B1 * m_ref[...] + (1.0 - ADAM_B1) * gg
        vn = ADAM_B2 * v_ref[...] + (1.0 - ADAM_B2) * (gg * gg)
        mo_ref[...] = mn
        vo_ref[...] = vn
        d_ref[...] = -ADAM_LR * ((mn / c1) / (jnp.sqrt(vn / c2) + ADAM_EPS) + ADAM_WD * w_ref[...])

    blk = pl.BlockSpec((tr, tc), lambda i, j: (i, j))
    shp = jax.ShapeDtypeStruct((rows, cols), F32)
    return pl.pallas_call(
        body, name=name, out_shape=(shp, shp, shp), grid=(rows // tr, cols // tc),
        in_specs=[blk] * 4, out_specs=(blk, blk, blk),
        compiler_params=_cp("parallel", "parallel"),
    )(w, g, m, v)


def _rows128(a):
    return a.reshape(-1, 128)


def _pack_small(norm_pre, conv_b, ssd_norm, gate_bias, norm_post, dt_bias, a_log, d_skip, fgate_bias):
    tiny = jnp.concatenate([dt_bias.reshape(-1), a_log.reshape(-1), d_skip.reshape(-1), fgate_bias.reshape(-1),
                            jnp.zeros((16,), F32)])
    return jnp.concatenate([_rows128(norm_pre), _rows128(conv_b), _rows128(ssd_norm), _rows128(gate_bias),
                            _rows128(norm_post), tiny.reshape(1, 128)], axis=0)


_SMALL_ROWS = 73
_SMALL_PAD = 80


def _unpack_small(p):
    tiny = p[72]
    return dict(norm_pre=p[0:8].reshape(1, 1024), conv_b=p[8:32].reshape(1, 3072), ssd_norm=p[32:48].reshape(1, 2048),
                gate_bias=p[48:64].reshape(1, 2048), norm_post=p[64:72].reshape(1, 1024),
                dt_bias=tiny[0:32].reshape(1, 32), a_log=tiny[32:64].reshape(1, 32),
                d_skip=tiny[64:96].reshape(1, 32), fgate_bias=tiny[96:112].reshape(1, 16))


def _pad_rows(a, rows):
    return jnp.concatenate([a, jnp.zeros((rows - a.shape[0], a.shape[1]), a.dtype)], axis=0)


def kernel(x, meta_tokens, norm_pre, w_in, conv_w, conv_b, dt_bias, a_log, d_skip, ssd_norm, fgate_bias, gate_bias, w_proj_ssd, w_proj_att, w_out, norm_post, loss_target, m_meta_tokens, m_norm_pre, m_w_in, m_conv_w, m_conv_b, m_dt_bias, m_a_log, m_d_skip, m_ssd_norm, m_fgate_bias, m_gate_bias, m_w_proj_ssd, m_w_proj_att, m_w_out, m_norm_post, v_meta_tokens, v_norm_pre, v_w_in, v_conv_w, v_conv_b, v_dt_bias, v_a_log, v_d_skip, v_ssd_norm, v_fgate_bias, v_gate_bias, v_w_proj_ssd, v_w_proj_att, v_w_out, v_norm_post):
    cx, cy, cc = _coords()
    chip = 2 * cx + cy
    ids = jnp.stack([cc, chip]).astype(jnp.int32)
    seq = x.shape[1]

    w_in_sh = jnp.transpose(w_in[0]).astype(BF16)
    w_pr_sh = jnp.concatenate([w_proj_ssd[0], w_proj_att[0], w_out[0]], axis=0).astype(BF16)

    def own_slot(sh):
        return lax.dynamic_update_slice(lax.empty((4,) + sh.shape, sh.dtype), sh[None], (chip, 0, 0))

    (g_in,) = _gather_shards([own_slot(w_in_sh)])
    w_main, w_small = _to_aligned_rows(g_in)
    sm_sh = jnp.concatenate([_rows128(meta_tokens), _rows128(conv_w[0])], axis=0)
    sm_all = _allgather8(sm_sh, "gather_small_weights")[0::2]
    meta_full = jnp.transpose(sm_all[:, 0:32].reshape(4, N_META, 256), (1, 0, 2)).reshape(N_META, D_MODEL)
    conv_w_full = jnp.transpose(sm_all[:, 32:56].reshape(4, CONV_K, 768), (1, 0, 2)).reshape(CONV_K, CONV_DIM)

    h = jnp.concatenate([jnp.zeros((PADF, D_MODEL), F32), meta_full, x[0]], axis=0)
    bias_row = jnp.concatenate([dt_bias[0], fgate_bias[0], jnp.zeros((N_SMALL - H_SSD - H_ATT,), F32)]).reshape(1, N_SMALL)
    a_neg = -jnp.exp(a_log[0])
    a_row = jnp.concatenate([a_neg, jnp.zeros((N_SMALL - H_SSD,), F32)]).reshape(1, N_SMALL)
    dsk_row = jnp.repeat(d_skip[0], 64).reshape(1, D_SSD)
    r = _local_step(h, loss_target[0], w_main, w_small, own_slot(w_pr_sh), ids, norm_pre, conv_w_full, conv_b,
                    bias_row, a_row, dsk_row, ssd_norm, gate_bias, norm_post)

    tm = _mm_tiles(h.shape[0])
    n_row_tiles = h.shape[0] // tm
    d_w_main = _matmul_cat_tn(r["dproj"], r["u"], "inproj_dw", tm)
    d_w_small = _matmul(r["dsmall"], r["u"], "tn", F32, "inproj_small_dw", N_SMALL, D_MODEL, tm)
    g32_in = _from_aligned_rows(d_w_main, d_w_small)
    first = max(n_row_tiles // 6, 1)
    du_first, ra_in = _matmul_cat_nn(r["dproj"], w_main, "inproj_dx_swap", tm, rows=(0, first),
                                     exchange=_pair_swap([g32_in]))
    pb_in = _add_pair(ids, g32_in, ra_in)
    du_a, rb_in = _matmul_cat_nn(r["dproj"], w_main, "inproj_dx_exchange", tm,
                                 rows=(first, n_row_tiles - first), fill=du_first,
                                 exchange=_chip_exchange([pb_in]))
    du_b = _matmul(r["dsmall"], w_small, "nn", F32, "inproj_small_dx", tm, D_MODEL, N_SMALL)
    dh, d_norm_pre = _norm1_bwd(du_a, du_b, h, norm_pre, r["dy_out"])
    grad_x = dh[PADF + N_META:].reshape(1, seq, D_MODEL)
    half_in = _add_chips(ids, g32_in, ra_in, rb_in)
    gw_in, gw_pr = _pair_join_halves([half_in, r["half_pr"]])

    tiny = r["d_bias_row"][0]
    part_small = _pack_small(d_norm_pre, r["d_conv_b"], r["d_ssd_norm"], r["d_gate_bias"], r["d_norm_post"],
                             tiny[0:H_SSD], r["d_a"][0, 0:H_SSD] * a_neg, r["d_dsk"].reshape(H_SSD, 64).sum(axis=1),
                             tiny[H_SSD:H_SSD + H_ATT])
    part = jnp.concatenate([_pad_rows(part_small, _SMALL_PAD), _rows128(r["d_conv_w"]),
                            _rows128(dh[PADF:PADF + N_META]), r["loss_blk"]], axis=0)
    tot = _sum8(_allgather8(part, "gather_small_grads"))
    loss = tot[_SMALL_PAD + 96 + 128, 0]
    g_small = tot[0:_SMALL_PAD]
    g_conv_w = lax.dynamic_slice_in_dim(tot[_SMALL_PAD:_SMALL_PAD + 96].reshape(CONV_K, CONV_DIM), chip * 768, 768, axis=1)
    g_meta = lax.dynamic_slice_in_dim(tot[_SMALL_PAD + 96:_SMALL_PAD + 224].reshape(N_META, D_MODEL), chip * 256, 256, axis=1)

    upd = {}
    flat = lambda a: jnp.transpose(a[0]).reshape(-1, 128)
    unflat = lambda a: jnp.transpose(a.reshape(N_COLS // 4, D_MODEL))
    upd["w_in"] = (jnp.transpose(gw_in),) + tuple(unflat(a) for a in _adamw(
        flat(w_in), gw_in.reshape(-1, 128), flat(m_w_in), flat(v_w_in), "adamw_w_in"))
    w_pr32 = jnp.concatenate([w_proj_ssd[0], w_proj_att[0], w_out[0]], axis=0)
    m_pr = jnp.concatenate([m_w_proj_ssd[0], m_w_proj_att[0], m_w_out[0]], axis=0)
    v_pr = jnp.concatenate([v_w_proj_ssd[0], v_w_proj_att[0], v_w_out[0]], axis=0)
    pr = (gw_pr,) + _adamw(w_pr32, gw_pr, m_pr, v_pr, "adamw_w_proj")
    upd["w_proj_ssd"] = tuple(a[0:512] for a in pr)
    upd["w_proj_att"] = tuple(a[512:768] for a in pr)
    upd["w_out"] = tuple(a[768:1024] for a in pr)
    upd["conv_w"] = (g_conv_w,) + _adamw(conv_w[0], g_conv_w, m_conv_w[0], v_conv_w[0], "adamw_conv_w")
    upd["meta_tokens"] = (g_meta,) + _adamw(meta_tokens, g_meta, m_meta_tokens, v_meta_tokens, "adamw_meta")
    pk = lambda np_, cb, sn, gb, npo, dtb, al, ds, fg: _pad_rows(_pack_small(np_, cb, sn, gb, npo, dtb, al, ds, fg), _SMALL_PAD)
    w_sm = pk(norm_pre, conv_b, ssd_norm, gate_bias, norm_post, dt_bias, a_log, d_skip, fgate_bias)
    m_sm = pk(m_norm_pre, m_conv_b, m_ssd_norm, m_gate_bias, m_norm_post, m_dt_bias, m_a_log, m_d_skip, m_fgate_bias)
    v_sm = pk(v_norm_pre, v_conv_b, v_ssd_norm, v_gate_bias, v_norm_post, v_dt_bias, v_a_log, v_d_skip, v_fgate_bias)
    sm = [_unpack_small(a) for a in (g_small,) + _adamw(w_sm, g_small, m_sm, v_sm, "adamw_small")]
    for name in ("norm_pre", "conv_b", "dt_bias", "a_log", "d_skip", "ssd_norm", "fgate_bias", "gate_bias", "norm_post"):
        upd[name] = tuple(s[name] for s in sm)
    lead = ("w_in", "conv_w", "w_proj_ssd", "w_proj_att", "w_out")
    order = ("meta_tokens", "norm_pre", "w_in", "conv_w", "conv_b", "dt_bias", "a_log", "d_skip", "ssd_norm",
             "fgate_bias", "gate_bias", "w_proj_ssd", "w_proj_att", "w_out", "norm_post")
    outs = [loss, grad_x]
    for part_i in range(4):
        for name in order:
            a = upd[name][part_i]
            outs.append(a[None] if name in lead else a)
    return tuple(outs)
```

```python
import functools
import math

import jax
import jax.numpy as jnp
from jax import lax
from jax.experimental import pallas as pl
from jax.experimental.pallas import tpu as pltpu

F32 = jnp.float32
BF16 = jnp.bfloat16
HIGHEST = lax.Precision.HIGHEST

D_MODEL = 1024
N_META = 16
CHUNK = 128
PADF = CHUNK - N_META
D_SSD = 2048
H_SSD = 32
G_SSD = 4
N_STATE = 128
CONV_K = 4
CONV_DIM = D_SSD + 2 * G_SSD * N_STATE
H_ATT = 16
D_ATT = 1024
EPS = 1e-6
N_COLS = 11312

C_Z, C_XBC, C_ZA, C_Q, C_K, C_V, C_G = 0, 2048, 5120, 6144, 7168, 8192, 9216
N_MAIN = 11264
N_SMALL = 128
O_Z, O_XBC, O_DT, O_ZA, O_Q, O_K, O_V, O_F, O_G = (
    (0, 2048), (2048, 3072), (5120, 32), (5152, 1024), (6176, 1024), (7200, 1024),
    (8224, 1024), (9248, 16), (9264, 2048))

ADAM_LR, ADAM_B1, ADAM_B2, ADAM_EPS, ADAM_WD, ADAM_STEP = 0.001, 0.9, 0.999, 1e-08, 0.01, 10

VMEM_LIMIT = 56 * 1024 * 1024


def _cp(*sem):
    return pltpu.CompilerParams(dimension_semantics=sem, vmem_limit_bytes=VMEM_LIMIT)


def _tile(n, prefs):
    for p in prefs:
        if n % p == 0:
            return p
    raise ValueError(f"no tile for {n} in {prefs}")


def _iota(shape, dim):
    return lax.broadcasted_iota(jnp.int32, shape, dim)


def _sigmoid(x):
    return 1.0 / (1.0 + jnp.exp(-x))


def _softplus_tail(x):
    return jnp.log(1.0 + jnp.exp(-jnp.abs(x)))


_NN = (((1,), (0,)), ((), ()))
_NT = (((1,), (1,)), ((), ()))
_TN = (((0,), (0,)), ((), ()))


def _dot(a, b, dims=_NN):
    return lax.dot_general(a, b, dims, preferred_element_type=F32)


def _dot_exact(a, b, dims=_NN):
    return lax.dot_general(a, b, dims, precision=HIGHEST, preferred_element_type=F32)


def _hosted_call(body, *, name, grid, in_specs, out_specs, out_shape, scratch_shapes, operands, semantics,
                 exchange=None, aliases=None):
    aliases = dict(aliases or {})
    if exchange is None:
        return pl.pallas_call(body, name=name, out_shape=out_shape, grid=grid, in_specs=in_specs,
                              out_specs=out_specs, scratch_shapes=scratch_shapes, input_output_aliases=aliases,
                              compiler_params=_cp(*semantics))(*operands)
    arrays, shapes, copies, n_sems, in_place = exchange
    n_in, n_out, n_ex = len(operands), len(out_shape), len(arrays)

    def hosted(*refs):
        ex_in = refs[n_in:n_in + n_ex]
        ex_out = refs[n_in + n_ex + n_out:n_in + n_ex + n_out + n_ex]
        own = refs[:n_in] + refs[n_in + n_ex:n_in + n_ex + n_out] + refs[n_in + 2 * n_ex + n_out:-2]
        first = functools.reduce(lambda p, q: p & q, [pl.program_id(d) == 0 for d in range(len(grid))])
        last = functools.reduce(lambda p, q: p & q, [pl.program_id(d) == grid[d] - 1 for d in range(len(grid))])

        def descriptors():
            return copies(ex_out if in_place else ex_in, ex_out, refs[-2], refs[-1])

        @pl.when(first)
        def _():
            for cp in descriptors():
                cp.start()

        body(*own)

        @pl.when(last)
        def _():
            for cp in descriptors():
                cp.wait()

    return pl.pallas_call(
        hosted, name=name,
        out_shape=tuple(out_shape) + tuple(shapes),
        grid=grid,
        in_specs=list(in_specs) + [_ANY] * n_ex,
        out_specs=tuple(out_specs) + (_ANY,) * n_ex,
        input_output_aliases={**aliases, **({n_in + e: n_out + e for e in range(n_ex)} if in_place else {})},
        scratch_shapes=list(scratch_shapes) + [pltpu.SemaphoreType.DMA((n_sems,)), pltpu.SemaphoreType.DMA((n_sems,))],
        compiler_params=_cp(*(("arbitrary",) * len(grid))),
    )(*operands, *arrays)


def _matmul(a, b, mode, out_dtype, name, tm, tn, tk, exchange=None):
    if mode == "tn":
        kdim, m = a.shape
    else:
        m, kdim = a.shape
    n = b.shape[0] if mode == "nt" else b.shape[1]
    nk = kdim // tk
    dims = {"nn": _NN, "nt": _NT, "tn": _TN}[mode]
    a_spec = (pl.BlockSpec((tk, tm), lambda i, j, k: (k, i)) if mode == "tn"
              else pl.BlockSpec((tm, tk), lambda i, j, k: (i, k)))
    b_spec = (pl.BlockSpec((tn, tk), lambda i, j, k: (j, k)) if mode == "nt"
              else pl.BlockSpec((tk, tn), lambda i, j, k: (k, j)))

    def body(a_ref, b_ref, o_ref, acc_ref):
        k = pl.program_id(2)
        p = _dot(a_ref[...].astype(BF16), b_ref[...].astype(BF16), dims)
        if nk == 1:
            o_ref[...] = p.astype(out_dtype)
        else:
            @pl.when(k == 0)
            def _():
                acc_ref[...] = p

            @pl.when(k > 0)
            def _():
                acc_ref[...] += p

            @pl.when(k == nk - 1)
            def _():
                o_ref[...] = acc_ref[...].astype(out_dtype)

    out = _hosted_call(
        body, name=name,
        out_shape=(jax.ShapeDtypeStruct((m, n), out_dtype),),
        grid=(m // tm, n // tn, nk),
        in_specs=[a_spec, b_spec],
        out_specs=(pl.BlockSpec((tm, tn), lambda i, j, k: (i, j)),),
        scratch_shapes=[pltpu.VMEM((tm, tn), F32)],
        operands=(a, b), semantics=("parallel", "parallel", "arbitrary"), exchange=exchange)
    return out[0] if exchange is None else out


_CAT_BLK = 1024


def _piece_ranges(pieces):
    out, off = [], 0
    for p in pieces:
        nb = p.shape[1] // _CAT_BLK
        out.append((off, nb))
        off += nb
    return out, off


def _matmul_cat_nn(pieces, b, name, tm, rows=None, fill=None, exchange=None):
    t = pieces[0].shape[0]
    n = b.shape[1]
    ranges, nk = _piece_ranges(pieces)
    first, ni = rows if rows is not None else (0, t // tm)
    n_in = len(pieces) + 1 + (fill is not None)

    def body(*refs):
        a_refs, b_ref, o_ref, acc_ref = refs[:len(pieces)], refs[len(pieces)], refs[n_in], refs[n_in + 1]
        k = pl.program_id(1)

        @pl.when(k == 0)
        def _():
            acc_ref[...] = jnp.zeros_like(acc_ref)

        for a_ref, (off, nb) in zip(a_refs, ranges):
            @pl.when((k >= off) & (k < off + nb))
            def _(a_ref=a_ref):
                acc_ref[...] += _dot(a_ref[...], b_ref[...])

        @pl.when(k == nk - 1)
        def _():
            o_ref[...] = acc_ref[...]

    def a_spec(off, nb):
        return pl.BlockSpec((tm, _CAT_BLK), lambda i, k: (first + i, jnp.clip(k - off, 0, nb - 1)))

    in_specs = [a_spec(off, nb) for off, nb in ranges] + [pl.BlockSpec((_CAT_BLK, n), lambda i, k: (k, 0))]
    operands = list(pieces) + [b]
    if fill is not None:
        in_specs.append(_ANY)
        operands.append(fill)
    out = _hosted_call(
        body, name=name,
        out_shape=(jax.ShapeDtypeStruct((t, n), F32),),
        grid=(ni, nk),
        in_specs=in_specs,
        out_specs=(pl.BlockSpec((tm, n), lambda i, k: (first + i, 0)),),
        scratch_shapes=[pltpu.VMEM((tm, n), F32)],
        operands=operands, semantics=("parallel", "arbitrary"), exchange=exchange,
        aliases={len(pieces) + 1: 0} if fill is not None else None)
    return out if exchange is not None else out[0]


def _matmul_cat_tn(pieces, b, name, tk):
    t = pieces[0].shape[0]
    n = b.shape[1]
    ranges, nm = _piece_ranges(pieces)
    nk = t // tk

    def body(*refs):
        a_refs, b_ref, o_ref, acc_ref = refs[:len(pieces)], refs[-3], refs[-2], refs[-1]
        m = pl.program_id(0)
        k = pl.program_id(1)

        @pl.when(k == 0)
        def _():
            acc_ref[...] = jnp.zeros_like(acc_ref)

        for a_ref, (off, nb) in zip(a_refs, ranges):
            @pl.when((m >= off) & (m < off + nb))
            def _(a_ref=a_ref):
                acc_ref[...] += _dot(a_ref[...], b_ref[...], _TN)

        @pl.when(k == nk - 1)
        def _():
            o_ref[...] = acc_ref[...]

    def a_spec(off, nb):
        def index(m, k):
            mine = (m >= off) & (m < off + nb)
            return jnp.where(mine, k, 0), jnp.clip(m - off, 0, nb - 1)
        return pl.BlockSpec((tk, _CAT_BLK), index)

    return pl.pallas_call(
        body, name=name,
        out_shape=jax.ShapeDtypeStruct((nm * _CAT_BLK, n), F32),
        grid=(nm, nk),
        in_specs=[a_spec(off, nb) for off, nb in ranges] + [pl.BlockSpec((tk, n), lambda m, k: (k, 0))],
        out_specs=pl.BlockSpec((_CAT_BLK, n), lambda m, k: (m, 0)),
        scratch_shapes=[pltpu.VMEM((_CAT_BLK, n), F32)],
        compiler_params=_cp("parallel", "arbitrary"),
    )(*pieces, b)


def _row_tile(t):
    return _tile(t, (352, 128))


def _row_tile_wide(t):
    return _tile(t, (176, 128))


def _norm1_fwd(h, g):
    t = h.shape[0]
    tm = _row_tile(t)

    def body(h_ref, g_ref, u_ref):
        x = h_ref[...]
        r = lax.rsqrt(jnp.mean(x * x, axis=-1, keepdims=True) + EPS)
        u_ref[...] = (x * r * g_ref[...]).astype(BF16)

    return pl.pallas_call(
        body, name="norm1_fwd",
        out_shape=jax.ShapeDtypeStruct((t, D_MODEL), BF16),
        grid=(t // tm,),
        in_specs=[pl.BlockSpec((tm, D_MODEL), lambda i: (i, 0)),
                  pl.BlockSpec((1, D_MODEL), lambda i: (0, 0))],
        out_specs=pl.BlockSpec((tm, D_MODEL), lambda i: (i, 0)),
        compiler_params=_cp("parallel"),
    )(h, g)


def _norm1_bwd(du_a, du_b, h, g, dy):
    t = h.shape[0]
    tm = _row_tile(t)

    def body(a_ref, b_ref, h_ref, g_ref, dy_ref, dh_ref, dg_ref):
        i = pl.program_id(0)
        x = h_ref[...]
        du = a_ref[...] + b_ref[...]
        r = lax.rsqrt(jnp.mean(x * x, axis=-1, keepdims=True) + EPS)
        gdu = du * g_ref[...]
        dh_ref[...] = dy_ref[...] + r * (gdu - x * (r * r) * jnp.mean(gdu * x, axis=-1, keepdims=True))
        part = jnp.sum(du * x * r, axis=0, keepdims=True)

        @pl.when(i == 0)
        def _():
            dg_ref[...] = part

        @pl.when(i > 0)
        def _():
            dg_ref[...] += part

    row = pl.BlockSpec((tm, D_MODEL), lambda i: (i, 0))
    vec = pl.BlockSpec((1, D_MODEL), lambda i: (0, 0))
    return pl.pallas_call(
        body, name="norm1_bwd",
        out_shape=(jax.ShapeDtypeStruct((t, D_MODEL), F32), jax.ShapeDtypeStruct((1, D_MODEL), F32)),
        grid=(t // tm,),
        in_specs=[row, row, row, vec, row],
        out_specs=(row, vec),
        compiler_params=_cp("arbitrary"),
    )(du_a, du_b, h, g, dy)


def _small_fwd(small, bias_row):
    t = small.shape[0]

    def body(s_ref, b_ref, o_ref, carry_ref):
        c = pl.program_id(0)

        @pl.when(c == 0)
        def _():
            carry_ref[...] = jnp.zeros_like(carry_ref)

        x = s_ref[...] + b_ref[...]
        r0 = _iota((CHUNK, CHUNK), 0)
        r1 = _iota((CHUNK, CHUNK), 1)
        valid = (c * CHUNK + r0) >= PADF
        tail = _softplus_tail(x)
        dt = jnp.where(valid & (r1 < H_SSD), jnp.maximum(x, 0.0) + tail, 0.0)
        lf = jnp.where(valid & (r1 >= H_SSD) & (r1 < H_SSD + H_ATT), jnp.minimum(x, 0.0) - tail, 0.0)
        tri = (r0 >= r1).astype(F32)
        cs = _dot_exact(tri, lf) + carry_ref[...]
        carry_ref[...] = cs[CHUNK - 1:CHUNK, :]
        o_ref[...] = dt + cs

    return pl.pallas_call(
        body, name="small_fwd",
        out_shape=jax.ShapeDtypeStruct((t, N_SMALL), F32),
        grid=(t // CHUNK,),
        in_specs=[pl.BlockSpec((CHUNK, N_SMALL), lambda c: (c, 0)),
                  pl.BlockSpec((1, N_SMALL), lambda c: (0, 0))],
        out_specs=pl.BlockSpec((CHUNK, N_SMALL), lambda c: (c, 0)),
        scratch_shapes=[pltpu.VMEM((1, N_SMALL), F32)],
        compiler_params=_cp("arbitrary"),
    )(small, bias_row)


def _small_bwd(dsm, small, bias_row):
    t = small.shape[0]
    nc = t // CHUNK

    def body(d_ref, s_ref, b_ref, o_ref, db_ref, carry_ref):
        step = pl.program_id(0)
        c = nc - 1 - step

        @pl.when(step == 0)
        def _():
            carry_ref[...] = jnp.zeros_like(carry_ref)
            db_ref[...] = jnp.zeros_like(db_ref)

        x = s_ref[...] + b_ref[...]
        d = d_ref[...]
        r0 = _iota((CHUNK, CHUNK), 0)
        r1 = _iota((CHUNK, CHUNK), 1)
        valid = (c * CHUNK + r0) >= PADF
        is_dt = r1 < H_SSD
        is_f = (r1 >= H_SSD) & (r1 < H_SSD + H_ATT)
        triu = (r1 >= r0).astype(F32)
        dc = jnp.where(is_f, d, 0.0)
        dlf = _dot_exact(triu, dc) + carry_ref[...]
        carry_ref[...] = dlf[0:1, :]
        sg = _sigmoid(x)
        out = jnp.where(valid & is_dt, d * sg, 0.0) + jnp.where(valid & is_f, dlf * (1.0 - sg), 0.0)
        o_ref[...] = out.astype(BF16)
        db_ref[...] += jnp.sum(out, axis=0, keepdims=True)

    blk = pl.BlockSpec((CHUNK, N_SMALL), lambda s: (nc - 1 - s, 0))
    vec = pl.BlockSpec((1, N_SMALL), lambda s: (0, 0))
    return pl.pallas_call(
        body, name="small_bwd",
        out_shape=(jax.ShapeDtypeStruct((t, N_SMALL), BF16), jax.ShapeDtypeStruct((1, N_SMALL), F32)),
        grid=(nc,),
        in_specs=[blk, blk, vec],
        out_specs=(blk, vec),
        scratch_shapes=[pltpu.VMEM((1, N_SMALL), F32)],
        compiler_params=_cp("arbitrary"),
    )(dsm, small, bias_row)


_CONV_TC = 1024
_XBC_BLK = C_XBC // _CONV_TC


def _shift_down(cur, prev8, j):
    rc = pltpu.roll(cur, j, 0)
    rid = _iota(prev8.shape, 0)
    top = jnp.where(rid < j, pltpu.roll(prev8, j, 0), rc[0:8, :])
    return top if cur.shape[0] == 8 else jnp.concatenate([top, rc[8:, :]], axis=0)


def _shift_up(cur, next8, j):
    n = cur.shape[0]
    ru = pltpu.roll(cur, n - j, 0)
    rid = _iota(next8.shape, 0)
    bot = jnp.where(rid >= 8 - j, pltpu.roll(next8, 8 - j, 0), ru[n - 8:, :])
    return jnp.concatenate([ru[:n - 8, :], bot], axis=0)


def _conv_taps(cur, prev, w, b):
    taps = [cur] + [_shift_down(cur, prev, j) for j in (1, 2, 3)]
    acc = b + taps[0] * w[3:4, :]
    for j in (1, 2, 3):
        acc = acc + taps[j] * w[3 - j:4 - j, :]
    return acc, taps


def _conv_pre(x_ref, p_ref, w_ref, b_ref, i):
    return _conv_taps(x_ref[...], jnp.where(i > 0, p_ref[...], 0.0), w_ref[...], b_ref[...])


def _dsilu(d, acc):
    sg = _sigmoid(acc)
    return d * sg * (1.0 + acc * (1.0 - sg))


def _conv_fwd(proj, conv_w, conv_b):
    t = proj.shape[0]
    tr = _row_tile(t)

    def body(x_ref, p_ref, w_ref, b_ref, o_ref):
        i = pl.program_id(0)
        acc, _ = _conv_pre(x_ref, p_ref, w_ref, b_ref, i)
        valid = (i * tr + _iota(acc.shape, 0)) >= PADF
        o_ref[...] = jnp.where(valid, acc * _sigmoid(acc), 0.0)

    return pl.pallas_call(
        body, name="conv_fwd",
        out_shape=jax.ShapeDtypeStruct((t, CONV_DIM), F32),
        grid=(t // tr, CONV_DIM // _CONV_TC),
        in_specs=[pl.BlockSpec((tr, _CONV_TC), lambda i, j: (i, _XBC_BLK + j)),
                  pl.BlockSpec((8, _CONV_TC), lambda i, j: (jnp.maximum(i * (tr // 8) - 1, 0), _XBC_BLK + j)),
                  pl.BlockSpec((CONV_K, _CONV_TC), lambda i, j: (0, j)),
                  pl.BlockSpec((1, _CONV_TC), lambda i, j: (0, j))],
        out_specs=pl.BlockSpec((tr, _CONV_TC), lambda i, j: (i, j)),
        compiler_params=_cp("parallel", "parallel"),
    )(proj, proj, conv_w, conv_b)


def _conv_bwd(dxbc, proj, conv_w, conv_b):
    t = proj.shape[0]
    tr = _row_tile(t)
    n_tiles = t // tr
    last8 = t // 8 - 1

    def body(d_ref, dn_ref, x_ref, p_ref, xn_ref, w_ref, b_ref, dx_ref, dw_ref, db_ref):
        i = pl.program_id(1)
        w = w_ref[...]
        b = b_ref[...]
        cur = x_ref[...]
        acc, taps = _conv_taps(cur, jnp.where(i > 0, p_ref[...], 0.0), w, b)
        valid = (i * tr + _iota(acc.shape, 0)) >= PADF
        da = jnp.where(valid, _dsilu(d_ref[...], acc), 0.0)
        acc_n, _ = _conv_taps(xn_ref[...], cur[tr - 8:, :], w, b)
        da_n = jnp.where(i < n_tiles - 1, _dsilu(dn_ref[...], acc_n), 0.0)
        dx = da * w[3:4, :]
        for j in (1, 2, 3):
            dx = dx + _shift_up(da, da_n, j) * w[3 - j:4 - j, :]
        dx_ref[...] = dx.astype(BF16)
        dw = jnp.concatenate([jnp.sum(da * taps[3 - k], axis=0, keepdims=True) for k in range(CONV_K)], axis=0)
        db = jnp.sum(da, axis=0, keepdims=True)

        @pl.when(i == 0)
        def _():
            dw_ref[...] = dw
            db_ref[...] = db

        @pl.when(i > 0)
        def _():
            dw_ref[...] += dw
            db_ref[...] += db

    nxt8 = lambda i: jnp.minimum((i + 1) * (tr // 8), last8)
    return pl.pallas_call(
        body, name="conv_bwd",
        out_shape=(jax.ShapeDtypeStruct((t, CONV_DIM), BF16),
                   jax.ShapeDtypeStruct((CONV_K, CONV_DIM), F32),
                   jax.ShapeDtypeStruct((1, CONV_DIM), F32)),
        grid=(CONV_DIM // _CONV_TC, n_tiles),
        in_specs=[pl.BlockSpec((tr, _CONV_TC), lambda j, i: (i, j)),
                  pl.BlockSpec((8, _CONV_TC), lambda j, i: (nxt8(i), j)),
                  pl.BlockSpec((tr, _CONV_TC), lambda j, i: (i, _XBC_BLK + j)),
                  pl.BlockSpec((8, _CONV_TC), lambda j, i: (jnp.maximum(i * (tr // 8) - 1, 0), _XBC_BLK + j)),
                  pl.BlockSpec((8, _CONV_TC), lambda j, i: (nxt8(i), _XBC_BLK + j)),
                  pl.BlockSpec((CONV_K, _CONV_TC), lambda j, i: (0, j)),
                  pl.BlockSpec((1, _CONV_TC), lambda j, i: (0, j))],
        out_specs=(pl.BlockSpec((tr, _CONV_TC), lambda j, i: (i, j)),
                   pl.BlockSpec((CONV_K, _CONV_TC), lambda j, i: (0, j)),
                   pl.BlockSpec((1, _CONV_TC), lambda j, i: (0, j))),
        compiler_params=_cp("parallel", "arbitrary"),
    )(dxbc, dxbc, proj, proj, proj, conv_w, conv_b)


_GW = D_SSD // G_SSD


def _ssd_prelude(dt_ref, a_ref, e_scr, es_scr, dte_scr):
    r0 = _iota((CHUNK, CHUNK), 0)
    r1 = _iota((CHUNK, CHUNK), 1)
    dt = jnp.where(r1 < H_SSD, dt_ref[...], 0.0)
    adt = dt * a_ref[...]
    acs = _dot_exact((r0 >= r1).astype(F32), adt)
    acs_t = acs.T
    alast = acs[CHUNK - 1:CHUNK, :]
    exp_a = jnp.exp(acs)
    dec_s = jnp.exp(alast - acs)
    lo = r1 < 64
    for j in range(H_SSD // 2):
        sl = slice(CHUNK * j, CHUNK * (j + 1))
        e_scr[:, sl] = jnp.where(lo, exp_a[:, 2 * j:2 * j + 1], exp_a[:, 2 * j + 1:2 * j + 2])
        es_scr[:, sl] = jnp.where(lo, dec_s[:, 2 * j:2 * j + 1], dec_s[:, 2 * j + 1:2 * j + 2])
        dte_scr[:, sl] = jnp.where(lo, dt[:, 2 * j:2 * j + 1], dt[:, 2 * j + 1:2 * j + 2])
    return dt, acs, acs_t, r0, r1, lo


def _chunk_decay_rows(acs_t, g):
    cd_t = jnp.exp(acs_t[:, CHUNK - 1:CHUNK])
    return jnp.concatenate(
        [jnp.broadcast_to(cd_t[8 * g + hh:8 * g + hh + 1, :], (64, N_STATE)) for hh in range(8)], axis=0)


def _ssd_fwd(xbc, dtlf, a_row, dsk_row):
    t = xbc.shape[0]
    nc = t // CHUNK

    def body(xs_ref, b_ref, c_ref, dt_ref, a_ref, dsk_ref, y_ref, hin_ref, h_scr, e_scr, es_scr, dte_scr):
        c = pl.program_id(0)

        @pl.when(c == 0)
        def _():
            h_scr[...] = jnp.zeros_like(h_scr)

        dt, acs, acs_t, r0, r1, lo = _ssd_prelude(dt_ref, a_ref, e_scr, es_scr, dte_scr)
        causal = r0 >= r1
        for g in range(G_SSD):
            gs = slice(_GW * g, _GW * (g + 1))
            bg = b_ref[:, N_STATE * g:N_STATE * (g + 1)].astype(BF16)
            cg = c_ref[:, N_STATE * g:N_STATE * (g + 1)].astype(BF16)
            cb = _dot(cg, bg, _NT)
            hg = h_scr[gs, :]
            hin_ref[0, gs, :] = hg
            xg = xs_ref[:, gs] * dte_scr[:, gs]
            yoff = _dot(cg, hg.astype(BF16), _NT) * e_scr[:, gs]
            st = _dot((xg * es_scr[:, gs]).astype(BF16), bg, _TN)
            h_scr[gs, :] = hg * _chunk_decay_rows(acs_t, g) + st
            for jj in range(4):
                j = 4 * g + jj
                sl = slice(CHUNK * j, CHUNK * (j + 1))
                xp = xg[:, CHUNK * jj:CHUNK * (jj + 1)]
                acc = yoff[:, CHUNK * jj:CHUNK * (jj + 1)] + dsk_ref[:, sl] * xs_ref[:, sl]
                for hh in range(2):
                    h = 2 * j + hh
                    seg = acs[:, h:h + 1] - acs_t[h:h + 1, :]
                    lm = jnp.exp(jnp.where(causal, seg, -1e30))
                    m = (cb * lm).astype(BF16)
                    xh = jnp.where(lo if hh == 0 else ~lo, xp, 0.0).astype(BF16)
                    acc = acc + _dot(m, xh)
                y_ref[:, sl] = acc

    return pl.pallas_call(
        body, name="ssd_fwd",
        out_shape=(jax.ShapeDtypeStruct((t, D_SSD), F32), jax.ShapeDtypeStruct((nc, D_SSD, N_STATE), F32)),
        grid=(nc,),
        in_specs=[pl.BlockSpec((CHUNK, D_SSD), lambda c: (c, 0)),
                  pl.BlockSpec((CHUNK, _GW), lambda c: (c, 4)),
                  pl.BlockSpec((CHUNK, _GW), lambda c: (c, 5)),
                  pl.BlockSpec((CHUNK, N_SMALL), lambda c: (c, 0)),
                  pl.BlockSpec((1, N_SMALL), lambda c: (0, 0)),
                  pl.BlockSpec((1, D_SSD), lambda c: (0, 0))],
        out_specs=(pl.BlockSpec((CHUNK, D_SSD), lambda c: (c, 0)),
                   pl.BlockSpec((1, D_SSD, N_STATE), lambda c: (c, 0, 0))),
        scratch_shapes=[pltpu.VMEM((D_SSD, N_STATE), F32)] + [pltpu.VMEM((CHUNK, D_SSD), F32)] * 3,
        compiler_params=_cp("arbitrary"),
    )(xbc, xbc, xbc, dtlf, a_row, dsk_row)


def _ssd_bwd(xbc, dtlf, a_row, dsk_row, hin, dy):
    t = xbc.shape[0]
    nc = t // CHUNK

    def body(xs_ref, b_ref, c_ref, dt_ref, a_ref, dsk_ref, hin_ref, dy_ref,
             dxbc_ref, ddt_ref, da_ref, ddsk_ref, dh_scr, e_scr, es_scr, dte_scr, dx_scr, whi_scr, wlo_scr):
        step = pl.program_id(0)

        @pl.when(step == 0)
        def _():
            dh_scr[...] = jnp.zeros_like(dh_scr)
            da_ref[...] = jnp.zeros_like(da_ref)
            ddsk_ref[...] = jnp.zeros_like(ddsk_ref)

        dt, acs, acs_t, r0, r1, lo = _ssd_prelude(dt_ref, a_ref, e_scr, es_scr, dte_scr)
        causal = r0 >= r1
        lane_row = _iota((1, CHUNK), 1)
        dacs = jnp.zeros((CHUNK, CHUNK), F32)
        dacs_t = jnp.zeros((CHUNK, CHUNK), F32)
        dalast = jnp.zeros((1, CHUNK), F32)
        ddt_dir = jnp.zeros((CHUNK, CHUNK), F32)
        ddsk_ref[...] += jnp.sum(dy_ref[...] * xs_ref[...], axis=0, keepdims=True)

        def head_sums(z, pick):
            hi = z.astype(BF16)
            return _dot(hi, pick) + _dot((z - hi.astype(F32)).astype(BF16), pick)

        for g in range(G_SSD):
            gs = slice(_GW * g, _GW * (g + 1))
            pick = (jnp.right_shift(_iota((_GW, CHUNK), 0), 6) + 8 * g == _iota((_GW, CHUNK), 1)).astype(BF16)
            bg = b_ref[:, N_STATE * g:N_STATE * (g + 1)].astype(BF16)
            cg = c_ref[:, N_STATE * g:N_STATE * (g + 1)].astype(BF16)
            cb = _dot(cg, bg, _NT)
            hg = hin_ref[0, gs, :]
            hgb = hg.astype(BF16)
            dhn = dh_scr[gs, :]
            dhnb = dhn.astype(BF16)
            esg = es_scr[:, gs]
            dyg = dy_ref[:, gs]
            xsg = xs_ref[:, gs]
            xg = xsg * dte_scr[:, gs]
            dyeb = (dyg * e_scr[:, gs]).astype(BF16)
            dc = _dot(dyeb, hgb)
            dh_y = _dot(dyeb, cg, _TN)
            dxs = _dot(bg, dhnb, _NT) * esg
            db = _dot((xg * esg).astype(BF16), dhnb)
            cd = _chunk_decay_rows(acs_t, g)
            dh_scr[gs, :] = dhn * cd + dh_y
            end_state = head_sums(jnp.broadcast_to(jnp.sum(xg * dxs, axis=0, keepdims=True), (8, _GW)), pick)[0:1, :]
            carried = dhn * hg * cd
            per_head = jnp.concatenate([jnp.sum(carried[64 * hh:64 * hh + 64, :], axis=0, keepdims=True)
                                        for hh in range(8)], axis=0)
            per_head = jnp.sum(per_head, axis=1, keepdims=True)
            for hh in range(8):
                end_state = end_state + jnp.where(lane_row == 8 * g + hh, per_head[hh:hh + 1, :], 0.0)
            dalast = dalast + end_state
            dcb = jnp.zeros((CHUNK, CHUNK), F32)
            for jj in range(4):
                j = 4 * g + jj
                sl = slice(CHUNK * j, CHUNK * (j + 1))
                ps = slice(CHUNK * jj, CHUNK * (jj + 1))
                xpb = xg[:, ps].astype(BF16)
                dyp = dyg[:, ps]
                dxp = dxs[:, ps]
                for hh in range(2):
                    h = 2 * j + hh
                    ws = slice(CHUNK * (2 * jj + hh), CHUNK * (2 * jj + hh + 1))
                    seg = acs[:, h:h + 1] - acs_t[h:h + 1, :]
                    lm = jnp.exp(jnp.where(causal, seg, -1e30))
                    mf = cb * lm
                    dyh = jnp.where(lo if hh == 0 else ~lo, dyp, 0.0).astype(BF16)
                    gm = _dot(dyh, xpb, _NT)
                    dcb = dcb + gm * lm
                    w = gm * mf
                    whi = w.astype(BF16)
                    whi_scr[:, ws] = whi
                    wlo_scr[:, ws] = (w - whi.astype(F32)).astype(BF16)
                    dacs_t = dacs_t - jnp.where(r0 == h, jnp.sum(w, axis=0, keepdims=True), 0.0)
                    dxp = dxp + _dot(mf.astype(BF16), dyh, _TN)
                dx_scr[:, sl] = dxp
            dxg = dx_scr[:, gs]
            pick_w = (jnp.right_shift(_iota((8 * CHUNK, CHUNK), 0), 7) + 8 * g == _iota((8 * CHUNK, CHUNK), 1)).astype(BF16)
            ch = _dot(cg, hgb, _NT)
            dacs = (dacs + _dot(whi_scr[...], pick_w) + _dot(wlo_scr[...], pick_w)
                    + head_sums(dyg * e_scr[:, gs] * ch - xg * dxs, pick))
            ddt_dir = ddt_dir + head_sums(dxg * xsg, pick)
            dcbb = dcb.astype(BF16)
            dxbc_ref[:, D_SSD + N_STATE * g:D_SSD + N_STATE * (g + 1)] = db + _dot(dcbb, cg, _TN)
            dxbc_ref[:, D_SSD + _GW + N_STATE * g:D_SSD + _GW + N_STATE * (g + 1)] = dc + _dot(dcbb, bg)
        dxbc_ref[:, 0:D_SSD] = dx_scr[...] * dte_scr[...] + dsk_ref[...] * dy_ref[...]
        dacs = dacs + dacs_t.T + jnp.where(r0 == CHUNK - 1, dalast, 0.0)
        dadt = _dot_exact((r1 >= r0).astype(F32), dacs)
        ddt_ref[...] = dadt * a_ref[...] + ddt_dir
        da_ref[...] += jnp.sum(dadt * dt, axis=0, keepdims=True)

    rev = lambda s: (nc - 1 - s, 0)
    return pl.pallas_call(
        body, name="ssd_bwd",
        out_shape=(jax.ShapeDtypeStruct((t, CONV_DIM), F32), jax.ShapeDtypeStruct((t, N_SMALL), F32),
                   jax.ShapeDtypeStruct((1, N_SMALL), F32), jax.ShapeDtypeStruct((1, D_SSD), F32)),
        grid=(nc,),
        in_specs=[pl.BlockSpec((CHUNK, D_SSD), rev),
                  pl.BlockSpec((CHUNK, _GW), lambda s: (nc - 1 - s, 4)),
                  pl.BlockSpec((CHUNK, _GW), lambda s: (nc - 1 - s, 5)),
                  pl.BlockSpec((CHUNK, N_SMALL), rev),
                  pl.BlockSpec((1, N_SMALL), lambda s: (0, 0)),
                  pl.BlockSpec((1, D_SSD), lambda s: (0, 0)),
                  pl.BlockSpec((1, D_SSD, N_STATE), lambda s: (nc - 1 - s, 0, 0)),
                  pl.BlockSpec((CHUNK, D_SSD), rev)],
        out_specs=(pl.BlockSpec((CHUNK, CONV_DIM), rev),
                   pl.BlockSpec((CHUNK, N_SMALL), rev),
                   pl.BlockSpec((1, N_SMALL), lambda s: (0, 0)),
                   pl.BlockSpec((1, D_SSD), lambda s: (0, 0))),
        scratch_shapes=([pltpu.VMEM((D_SSD, N_STATE), F32)] + [pltpu.VMEM((CHUNK, D_SSD), F32)] * 4
                        + [pltpu.VMEM((CHUNK, 8 * CHUNK), BF16)] * 2),
        compiler_params=_cp("arbitrary"),
    )(xbc, xbc, xbc, dtlf, a_row, dsk_row, hin, dy)


_NPAIR = H_ATT // 2
_QB, _KB, _VB = C_Q // 128, C_K // 128, C_V // 128
_SCALE = 1.0 / math.sqrt(64.0)


def _attn_blocks(t):
    return _tile(t, (1408, 384, 256, 128)), _tile(t, (384, 128))


def _split3(c):
    hi = c.astype(BF16).astype(F32)
    rest = c - hi
    mid = rest.astype(BF16).astype(F32)
    return hi, mid, rest - mid


def _head_lanes(lane, hh):
    return (lane < 64, 64) if hh == 0 else (lane >= 64, 0)


def _q_operand(q, cq, lane, hh):
    sel, first = _head_lanes(lane, hh)
    out = jnp.where(sel, q, 0.0)
    for n, col in enumerate(_split3(cq) + (1.0, 1.0, 1.0)):
        out = jnp.where(lane == first + n, col, out)
    return out.astype(BF16)


def _k_operand(k, ck, lane, hh):
    sel, first = _head_lanes(lane, hh)
    hi, mid, lo = _split3(ck)
    out = jnp.where(sel, k, 0.0)
    for n, col in enumerate((1.0, 1.0, 1.0, -hi, -mid, -lo)):
        out = jnp.where(lane == first + n, col, out)
    return out.astype(BF16)


def _needs_mask(i, kk, bq, bk):
    return kk * bk + bk - 1 > i * bq


_C_FILLER = 2.0 ** 30


def _attn_fwd(proj, c_col):
    t = proj.shape[0]
    bq, bk = _attn_blocks(t)
    nq, nk = t // bq, t // bk
    rs = 16

    def last_kv(i):
        return (i * bq + bq - 1) // bk

    def body(q_ref, k_ref, v_ref, cq_ref, ck_ref, o_ref, lse_ref, p_ref, mrun_ref, qs_scr, s_scr, m_scr, acc_scr):
        i = pl.program_id(1)
        kk = pl.program_id(2)
        lane_q = _iota((bq, 128), 1)

        @pl.when(kk == 0)
        def _():
            m_scr[...] = jnp.full_like(m_scr, -1e30)
            acc_scr[...] = jnp.zeros_like(acc_scr)
            q = q_ref[...] * _SCALE
            cq = cq_ref[0]
            for hh in range(2):
                qs_scr[hh] = _q_operand(q, cq[:, hh:hh + 1], lane_q, hh)

        def step(masked):
            lane_k = _iota((bk, 128), 1)
            k = k_ref[...]
            v = v_ref[...]
            ck = ck_ref[0]
            ahead = _iota((rs, bq), 0) - _iota((rs, bq), 1)
            vss = []
            for hh in range(2):
                sel, first = _head_lanes(lane_k, hh)
                ks = _k_operand(k, ck[:, hh:hh + 1], lane_k, hh)
                vss.append(jnp.where(sel, v, jnp.where(lane_k == first, 1.0, 0.0)).astype(BF16))
                s_scr[hh] = _dot(ks, qs_scr[hh], _NT)
            for hh in range(2):
                vs = vss[hh]

                def block_max(r, mx):
                    rows = pl.ds(pl.multiple_of(r * rs, rs), rs)
                    s = s_scr[hh, rows, :]
                    if masked:
                        s = jnp.where(ahead <= i * bq - kk * bk - r * rs, s, -1e30)
                        s_scr[hh, rows, :] = s
                    return jnp.maximum(mx, s)

                mx = lax.fori_loop(0, bk // rs, block_max, jnp.full((rs, bq), -1e30, F32), unroll=True)
                m_old = m_scr[hh]
                m_new = jnp.maximum(m_old, jnp.max(mx, axis=0, keepdims=True))
                m_scr[hh] = m_new
                mrun_ref[0, hh:hh + 1, :] = m_new

                def probs(r, carry):
                    rows = pl.ds(pl.multiple_of(r * rs, rs), rs)
                    p_ref[0, hh, rows, :] = jnp.exp(s_scr[hh, rows, :] - m_new).astype(BF16)
                    return carry

                lax.fori_loop(0, bk // rs, probs, 0, unroll=True)
                acc_scr[hh] = acc_scr[hh] * jnp.exp(m_old - m_new) + _dot(vs, p_ref[0, hh], _TN)

        active = kk <= last_kv(i)
        masked = _needs_mask(i, kk, bq, bk)

        @pl.when(active & masked)
        def _():
            step(True)

        @pl.when(active & jnp.logical_not(masked))
        def _():
            step(False)

        @pl.when(kk == nk - 1)
        def _():
            a = acc_scr[0]
            b = acc_scr[1]
            la = a[64:65, :]
            lb = b[0:1, :]
            o_ref[...] = jnp.where(lane_q < 64, (a / la).T, (b / lb).T)
            lse_ref[0] = jnp.concatenate([m_scr[0] + jnp.log(la), m_scr[1] + jnp.log(lb)], axis=0)

    kvi = lambda i, kk: jnp.minimum(kk, last_kv(i))
    kv = lambda off: pl.BlockSpec((bk, 128), lambda j, i, kk: (kvi(i, kk), off + j))
    blk = lambda j, i, kk: (j * nq + i) * nk + kvi(i, kk)
    return pl.pallas_call(
        body, name="attn_fwd",
        out_shape=(jax.ShapeDtypeStruct((t, D_ATT), F32), jax.ShapeDtypeStruct((_NPAIR, 2, t), F32),
                   jax.ShapeDtypeStruct((_NPAIR * nq * nk, 2, bk, bq), BF16),
                   jax.ShapeDtypeStruct((_NPAIR * nq * nk, 2, bq), F32)),
        grid=(_NPAIR, nq, nk),
        in_specs=[pl.BlockSpec((bq, 128), lambda j, i, kk: (i, _QB + j)),
                  kv(_KB), kv(_VB),
                  pl.BlockSpec((1, bq, 2), lambda j, i, kk: (j, i, 0)),
                  pl.BlockSpec((1, bk, 2), lambda j, i, kk: (j, kvi(i, kk), 0))],
        out_specs=(pl.BlockSpec((bq, 128), lambda j, i, kk: (i, j)),
                   pl.BlockSpec((1, 2, bq), lambda j, i, kk: (j, 0, i)),
                   pl.BlockSpec((1, 2, bk, bq), lambda j, i, kk: (blk(j, i, kk), 0, 0, 0)),
                   pl.BlockSpec((1, 2, bq), lambda j, i, kk: (blk(j, i, kk), 0, 0))),
        scratch_shapes=[pltpu.VMEM((2, bq, 128), BF16), pltpu.VMEM((2, bk, bq), F32),
                        pltpu.VMEM((2, 1, bq), F32), pltpu.VMEM((2, 128, bq), F32)],
        compiler_params=_cp("parallel", "parallel", "arbitrary"),
    )(proj, proj, proj, c_col, c_col)


def _attn_bwd(proj, c_col, lse_row, dl_row, do, p_blocks, m_run, exchange=None):
    t = proj.shape[0]
    bq, bk = _attn_blocks(t)
    nq, nk = t // bq, t // bk
    rs = 16

    def first_q(kk):
        return (kk * bk) // bq

    def body(q_ref, k_ref, v_ref, cq_ref, ck_ref, lse_ref, dl_ref, do_ref, pblk_ref, mrun_ref,
             dq_ref, dk_ref, dv_ref, dck_ref, dcq_ref,
             qs_scr, doh_scr, ks_scr, dp_scr, p_scr, ds_scr, dq_scr, dk_scr, dv_scr):
        kk = pl.program_id(1)
        i = pl.program_id(2)
        lane_q = _iota((bq, 128), 1)
        lane_k = _iota((bk, 128), 1)
        qrows = pl.ds(pl.multiple_of(i * bq, 128), bq)

        @pl.when(kk == 0)
        def _():
            q = q_ref[...] * _SCALE
            cq = cq_ref[0]
            do_ = do_ref[...]
            for hh in range(2):
                qs_scr[hh, qrows, :] = _q_operand(q, cq[:, hh:hh + 1], lane_q, hh)
                doh_scr[hh, qrows, :] = jnp.where(_head_lanes(lane_q, hh)[0], do_, 0.0).astype(BF16)
                dq_scr[hh, i] = jnp.zeros((128, bq), F32)

        @pl.when(i == 0)
        def _():
            dk_scr[...] = jnp.zeros_like(dk_scr)
            dv_scr[...] = jnp.zeros_like(dv_scr)
            k = k_ref[...]
            ck = ck_ref[0]
            for hh in range(2):
                ks_scr[hh] = _k_operand(k, ck[:, hh:hh + 1], lane_k, hh)

        @pl.when(i >= first_q(kk))
        def _():
            v16 = v_ref[...].astype(BF16)
            dl = dl_ref[0]
            rescale = jnp.exp(mrun_ref[0] - lse_ref[0])
            for hh in range(2):
                dp_scr[hh] = _dot(v16, doh_scr[hh, qrows, :], _NT)
            for hh in range(2):
                qs = qs_scr[hh, qrows, :]
                doh = doh_scr[hh, qrows, :]

                def strip(r, carry):
                    rows = pl.ds(pl.multiple_of(r * rs, rs), rs)
                    p = pblk_ref[0, hh, rows, :].astype(F32) * rescale[hh:hh + 1, :]
                    p_scr[hh, rows, :] = p.astype(BF16)
                    ds_scr[hh, rows, :] = (p * (dp_scr[hh, rows, :] - dl[hh:hh + 1, :])).astype(BF16)
                    return carry

                lax.fori_loop(0, bk // rs, strip, 0, unroll=True)
                dv_scr[...] += _dot(p_scr[hh], doh)
                dk_scr[hh] += _dot(ds_scr[hh], qs)
                dq_scr[hh, i] += _dot(ks_scr[hh], ds_scr[hh], _TN)

        @pl.when(i == nq - 1)
        def _():
            dka = dk_scr[0]
            dkb = dk_scr[1]
            dk_ref[...] = jnp.where(lane_k < 64, dka, dkb).astype(BF16)
            dv_ref[...] = dv_scr[...].astype(BF16)
            dck_ref[0] = -jnp.where(_iota((bk, 2), 1) == 0, dka[:, 67:68], dkb[:, 3:4])

        @pl.when((kk == nk - 1) & (i == nq - 1))
        def _():
            for ii in range(nq):
                cols = slice(ii * bq, (ii + 1) * bq)
                dqa = dq_scr[0, ii]
                dqb = dq_scr[1, ii]
                dq_ref[cols, :] = (jnp.where(lane_q < 64, dqa.T, dqb.T) * _SCALE).astype(BF16)
                dcq_ref[0, :, cols] = jnp.concatenate([dqa[64:65, :], dqb[0:1, :]], axis=0)

    qi = lambda kk, i: jnp.where(kk == 0, i, nq - 1)
    qspec = lambda off: pl.BlockSpec((bq, 128), lambda j, kk, i: (qi(kk, i), off + j))
    kspec = lambda off: pl.BlockSpec((bk, 128), lambda j, kk, i: (kk, off + j))
    rowspec = pl.BlockSpec((1, 2, bq), lambda j, kk, i: (j, 0, jnp.maximum(i, first_q(kk))))
    blk = lambda j, kk, i: (j * nq + jnp.maximum(i, first_q(kk))) * nk + kk
    return _hosted_call(
        body, name="attn_bwd",
        out_shape=(jax.ShapeDtypeStruct((t, D_ATT), BF16), jax.ShapeDtypeStruct((t, D_ATT), BF16),
                   jax.ShapeDtypeStruct((t, D_ATT), BF16), jax.ShapeDtypeStruct((_NPAIR, t, 2), F32),
                   jax.ShapeDtypeStruct((_NPAIR, 2, t), F32)),
        grid=(_NPAIR, nk, nq),
        in_specs=[qspec(_QB), kspec(_KB), kspec(_VB),
                  pl.BlockSpec((1, bq, 2), lambda j, kk, i: (j, qi(kk, i), 0)),
                  pl.BlockSpec((1, bk, 2), lambda j, kk, i: (j, kk, 0)),
                  rowspec, rowspec, qspec(0),
                  pl.BlockSpec((1, 2, bk, bq), lambda j, kk, i: (blk(j, kk, i), 0, 0, 0)),
                  pl.BlockSpec((1, 2, bq), lambda j, kk, i: (blk(j, kk, i), 0, 0))],
        out_specs=(pl.BlockSpec((t, 128), lambda j, kk, i: (0, j)),
                   pl.BlockSpec((bk, 128), lambda j, kk, i: (kk, j)),
                   pl.BlockSpec((bk, 128), lambda j, kk, i: (kk, j)),
                   pl.BlockSpec((1, bk, 2), lambda j, kk, i: (j, kk, 0)),
                   pl.BlockSpec((1, 2, t), lambda j, kk, i: (j, 0, 0))),
        scratch_shapes=[pltpu.VMEM((2, t, 128), BF16), pltpu.VMEM((2, t, 128), BF16), pltpu.VMEM((2, bk, 128), BF16),
                        pltpu.VMEM((2, bk, bq), F32),
                        pltpu.VMEM((2, bk, bq), BF16), pltpu.VMEM((2, bk, bq), BF16),
                        pltpu.VMEM((2, nq, 128, bq), F32), pltpu.VMEM((2, bk, 128), F32), pltpu.VMEM((bk, 128), F32)],
        operands=(proj, proj, proj, c_col, c_col, lse_row, dl_row, do, p_blocks, m_run),
        semantics=("parallel", "arbitrary", "arbitrary"), exchange=exchange)


def _premerge_fwd(y, o, proj, gamma):
    t = y.shape[0]
    tm = _row_tile_wide(t)

    def body(y_ref, z_ref, o_ref, za_ref, g_ref, ys_ref, ya_ref):
        z = z_ref[...]
        u = y_ref[...] * (z * _sigmoid(z))
        for g in range(G_SSD):
            gs = slice(_GW * g, _GW * (g + 1))
            ug = u[:, gs]
            r = lax.rsqrt(jnp.mean(ug * ug, axis=-1, keepdims=True) + EPS)
            ys_ref[:, gs] = (ug * r * g_ref[:, gs]).astype(BF16)
        za = za_ref[...]
        ya_ref[...] = (o_ref[...] * (za * _sigmoid(za))).astype(BF16)

    return pl.pallas_call(
        body, name="premerge_fwd",
        out_shape=(jax.ShapeDtypeStruct((t, D_SSD), BF16), jax.ShapeDtypeStruct((t, D_ATT), BF16)),
        grid=(t // tm,),
        in_specs=[pl.BlockSpec((tm, D_SSD), lambda i: (i, 0)),
                  pl.BlockSpec((tm, D_SSD), lambda i: (i, C_Z // D_SSD)),
                  pl.BlockSpec((tm, D_ATT), lambda i: (i, 0)),
                  pl.BlockSpec((tm, D_ATT), lambda i: (i, C_ZA // D_ATT)),
                  pl.BlockSpec((1, D_SSD), lambda i: (0, 0))],
        out_specs=(pl.BlockSpec((tm, D_SSD), lambda i: (i, 0)), pl.BlockSpec((tm, D_ATT), lambda i: (i, 0))),
        compiler_params=_cp("parallel"),
    )(y, proj, o, proj, gamma)


def _premerge_bwd(dys, dya, y, o, proj, gamma, exchange=None):
    t = y.shape[0]
    tm = _row_tile_wide(t)

    def body(dys_ref, dya_ref, y_ref, z_ref, o_ref, za_ref, g_ref, dy_ref, dz_ref, do_ref, dza_ref, dg_ref, dl_ref):
        i = pl.program_id(0)
        z = z_ref[...]
        sz = _sigmoid(z)
        silu = z * sz
        dsilu = sz * (1.0 + z * (1.0 - sz))
        yv = y_ref[...]
        u = yv * silu
        parts = []
        for g in range(G_SSD):
            gs = slice(_GW * g, _GW * (g + 1))
            ug = u[:, gs]
            r = lax.rsqrt(jnp.mean(ug * ug, axis=-1, keepdims=True) + EPS)
            n = ug * r
            dout = dys_ref[:, gs]
            dn = dout * g_ref[:, gs]
            du = r * (dn - n * jnp.mean(dn * n, axis=-1, keepdims=True))
            dy_ref[:, gs] = du * silu[:, gs]
            dz_ref[:, gs] = (du * yv[:, gs] * dsilu[:, gs]).astype(BF16)
            parts.append(jnp.sum(dout * n, axis=0, keepdims=True))
        dg = jnp.concatenate(parts, axis=1)
        za = za_ref[...]
        sa = _sigmoid(za)
        dya_ = dya_ref[...]
        ov = o_ref[...]
        do = dya_ * (za * sa)
        do_ref[...] = do
        dza_ref[...] = (dya_ * ov * (sa * (1.0 + za * (1.0 - sa)))).astype(BF16)
        pick = (jnp.right_shift(_iota((D_ATT, 128), 0), 6) == _iota((D_ATT, 128), 1)).astype(F32)
        dl_ref[...] = _dot_exact(do * ov, pick)

        @pl.when(i == 0)
        def _():
            dg_ref[...] = dg

        @pl.when(i > 0)
        def _():
            dg_ref[...] += dg

    ssd = pl.BlockSpec((tm, D_SSD), lambda i: (i, 0))
    att = pl.BlockSpec((tm, D_ATT), lambda i: (i, 0))
    vec = pl.BlockSpec((1, D_SSD), lambda i: (0, 0))
    return _hosted_call(
        body, name="premerge_bwd",
        out_shape=(jax.ShapeDtypeStruct((t, D_SSD), F32), jax.ShapeDtypeStruct((t, D_SSD), BF16),
                   jax.ShapeDtypeStruct((t, D_ATT), F32), jax.ShapeDtypeStruct((t, D_ATT), BF16),
                   jax.ShapeDtypeStruct((1, D_SSD), F32), jax.ShapeDtypeStruct((t, 128), F32)),
        grid=(t // tm,),
        in_specs=[ssd, att, ssd, pl.BlockSpec((tm, D_SSD), lambda i: (i, C_Z // D_SSD)), att,
                  pl.BlockSpec((tm, D_ATT), lambda i: (i, C_ZA // D_ATT)), vec],
        out_specs=(ssd, ssd, att, att, vec, pl.BlockSpec((tm, 128), lambda i: (i, 0))),
        scratch_shapes=[],
        operands=(dys, dya, y, proj, o, proj, gamma), semantics=("arbitrary",), exchange=exchange)


_G_BLK = C_G // D_MODEL


def _merge_fwd(a, b, proj, gate_bias):
    t = a.shape[0]
    tm = _row_tile(t)

    def body(a_ref, b_ref, gs_ref, ga_ref, bias_ref, m_ref):
        g_ssd = _sigmoid(gs_ref[...] + bias_ref[:, 0:D_MODEL])
        g_att = _sigmoid(ga_ref[...] + bias_ref[:, D_MODEL:2 * D_MODEL])
        m_ref[...] = (g_ssd * a_ref[...] + g_att * b_ref[...]).astype(BF16)

    row = pl.BlockSpec((tm, D_MODEL), lambda i: (i, 0))
    return pl.pallas_call(
        body, name="merge_fwd",
        out_shape=jax.ShapeDtypeStruct((t, D_MODEL), BF16),
        grid=(t // tm,),
        in_specs=[row, row,
                  pl.BlockSpec((tm, D_MODEL), lambda i: (i, _G_BLK)),
                  pl.BlockSpec((tm, D_MODEL), lambda i: (i, _G_BLK + 1)),
                  pl.BlockSpec((1, 2 * D_MODEL), lambda i: (0, 0))],
        out_specs=row,
        compiler_params=_cp("parallel"),
    )(a, b, proj, proj, gate_bias)


def _merge_bwd(dm, a, b, proj, gate_bias):
    t = a.shape[0]
    tm = _row_tile(t)

    def body(dm_ref, a_ref, b_ref, gs_ref, ga_ref, bias_ref, da_ref, db_ref, dg_ref, dbias_ref):
        i = pl.program_id(0)
        dm_ = dm_ref[...]
        g_ssd = _sigmoid(gs_ref[...] + bias_ref[:, 0:D_MODEL])
        g_att = _sigmoid(ga_ref[...] + bias_ref[:, D_MODEL:2 * D_MODEL])
        da_ref[...] = (dm_ * g_ssd).astype(BF16)
        db_ref[...] = (dm_ * g_att).astype(BF16)
        dgs = dm_ * a_ref[...] * g_ssd * (1.0 - g_ssd)
        dga = dm_ * b_ref[...] * g_att * (1.0 - g_att)
        dg_ref[:, 0:D_MODEL] = dgs.astype(BF16)
        dg_ref[:, D_MODEL:2 * D_MODEL] = dga.astype(BF16)
        part = jnp.concatenate([jnp.sum(dgs, axis=0, keepdims=True), jnp.sum(dga, axis=0, keepdims=True)], axis=1)

        @pl.when(i == 0)
        def _():
            dbias_ref[...] = part

        @pl.when(i > 0)
        def _():
            dbias_ref[...] += part

    row = pl.BlockSpec((tm, D_MODEL), lambda i: (i, 0))
    wide = pl.BlockSpec((tm, 2 * D_MODEL), lambda i: (i, 0))
    vec = pl.BlockSpec((1, 2 * D_MODEL), lambda i: (0, 0))
    return pl.pallas_call(
        body, name="merge_bwd",
        out_shape=(jax.ShapeDtypeStruct((t, D_MODEL), BF16), jax.ShapeDtypeStruct((t, D_MODEL), BF16),
                   jax.ShapeDtypeStruct((t, 2 * D_MODEL), BF16), jax.ShapeDtypeStruct((1, 2 * D_MODEL), F32)),
        grid=(t // tm,),
        in_specs=[row, row, row,
                  pl.BlockSpec((tm, D_MODEL), lambda i: (i, _G_BLK)),
                  pl.BlockSpec((tm, D_MODEL), lambda i: (i, _G_BLK + 1)), vec],
        out_specs=(row, row, wide, vec),
        compiler_params=_cp("arbitrary"),
    )(dm, a, b, proj, proj, gate_bias)


def _post(o2, h, target, g):
    t = o2.shape[0]
    nc = t // CHUNK

    def body(o_ref, h_ref, t_ref, g_ref, dy_ref, do_ref, dg_ref, loss_ref):
        c = pl.program_id(0)
        x = o_ref[...]
        r = lax.rsqrt(jnp.mean(x * x, axis=-1, keepdims=True) + EPS)
        n = x * r
        y = h_ref[...] + n * g_ref[...]
        diff = jnp.where(c > 0, y - t_ref[...], 0.0)
        dy = diff * (1.0 / D_MODEL)
        dy_ref[...] = dy
        gdy = dy * g_ref[...]
        do_ref[...] = (r * (gdy - n * jnp.mean(gdy * n, axis=-1, keepdims=True))).astype(BF16)
        dg = jnp.sum(dy * n, axis=0, keepdims=True)
        lpart = 0.5 * jnp.sum(jnp.sum(diff * diff, axis=1, keepdims=True), axis=0, keepdims=True) * (1.0 / D_MODEL)
        sel = (_iota((8, 128), 0) == 0) & (_iota((8, 128), 1) == 0)

        @pl.when(c == 0)
        def _():
            dg_ref[...] = dg
            loss_ref[...] = jnp.zeros_like(loss_ref)

        @pl.when(c > 0)
        def _():
            dg_ref[...] += dg
            loss_ref[...] += jnp.where(sel, lpart, 0.0)

    row = pl.BlockSpec((CHUNK, D_MODEL), lambda c: (c, 0))
    vec = pl.BlockSpec((1, D_MODEL), lambda c: (0, 0))
    return pl.pallas_call(
        body, name="post",
        out_shape=(jax.ShapeDtypeStruct((t, D_MODEL), F32), jax.ShapeDtypeStruct((t, D_MODEL), BF16),
                   jax.ShapeDtypeStruct((1, D_MODEL), F32), jax.ShapeDtypeStruct((8, 128), F32)),
        grid=(nc,),
        in_specs=[row, row, pl.BlockSpec((CHUNK, D_MODEL), lambda c: (jnp.maximum(c - 1, 0), 0)), vec],
        out_specs=(row, row, vec, pl.BlockSpec((8, 128), lambda c: (0, 0))),
        compiler_params=_cp("arbitrary"),
    )(o2, h, target, g)


def _mm_tiles(t):
    return _tile(t, (704, 384, 128))


def _local_step(h, target, w_main, w_small, pr_slots, ids, norm_pre, conv_w, conv_b, bias_row, a_row,
                dsk_row, ssd_norm, gate_bias, norm_post):
    t = h.shape[0]
    tm = _mm_tiles(t)
    u = _norm1_fwd(h, norm_pre)
    proj, pr_slots = _matmul(u, w_main, "nt", F32, "inproj", tm, 1024, D_MODEL,
                             exchange=_gather_stage([pr_slots], to_sibling=False))
    small, pr_slots = _matmul(u, w_small, "nt", F32, "inproj_small", tm, N_SMALL, D_MODEL,
                              exchange=_gather_stage([pr_slots], to_sibling=True))
    wps = pr_slots[:, 0:512].reshape(D_SSD, D_MODEL)
    wpa = pr_slots[:, 512:768].reshape(D_ATT, D_MODEL)
    wout = pr_slots[:, 768:1024].reshape(D_MODEL, D_MODEL)
    dtlf = _small_fwd(small, bias_row)
    xbc = _conv_fwd(proj, conv_w, conv_b)
    y, hin = _ssd_fwd(xbc, dtlf, a_row, dsk_row)
    c_tok = dtlf[:, H_SSD:H_SSD + H_ATT]
    c_tok = jnp.where(jnp.arange(t)[:, None] < PADF, _C_FILLER, c_tok)
    c_col = c_tok.reshape(t, _NPAIR, 2).transpose(1, 0, 2)
    o, lse, p_blocks, m_run = _attn_fwd(proj, c_col)
    ys, ya = _premerge_fwd(y, o, proj, ssd_norm)
    a = _matmul(ys, wps, "nn", F32, "proj_ssd", tm, D_MODEL, D_SSD)
    b = _matmul(ya, wpa, "nn", F32, "proj_att", tm, D_MODEL, D_ATT)
    merged = _merge_fwd(a, b, proj, gate_bias)
    o2 = _matmul(merged, wout, "nn", F32, "out_proj", tm, D_MODEL, D_MODEL)
    dy_out, do2, d_norm_post, loss_blk = _post(o2, h, target, norm_post)

    dm = _matmul(do2, wout, "nt", F32, "out_proj_dx", tm, D_MODEL, D_MODEL)
    d_wout = _matmul(merged, do2, "tn", F32, "out_proj_dw", D_MODEL, D_MODEL, tm)
    da, db, dgraw, d_gate_bias = _merge_bwd(dm, a, b, proj, gate_bias)
    dys = _matmul(da, wps, "nt", F32, "proj_ssd_dx", tm, D_SSD, D_MODEL)
    d_wps = _matmul(ys, da, "tn", F32, "proj_ssd_dw", D_SSD, D_MODEL, tm)
    dya = _matmul(db, wpa, "nt", F32, "proj_att_dx", tm, D_ATT, D_MODEL)
    d_wpa = _matmul(ya, db, "tn", F32, "proj_att_dw", D_ATT, D_MODEL, tm)
    g32_pr = jnp.concatenate([d_wps.reshape(4, 512, D_MODEL), d_wpa.reshape(4, 256, D_MODEL),
                              d_wout.reshape(4, 256, D_MODEL)], axis=1)
    dy, dz, do, dza, d_ssd_norm, dl, ra_pr = _premerge_bwd(dys, dya, y, o, proj, ssd_norm,
                                                           exchange=_pair_swap([g32_pr]))
    pb_pr = _add_pair(ids, g32_pr, ra_pr)
    dl_row = dl[:, 0:H_ATT].T.reshape(_NPAIR, 2, t)
    dq, dk, dv, dc_key, dc_qry, rb_pr = _attn_bwd(proj, c_col, lse, dl_row, do, p_blocks, m_run,
                                                  exchange=_chip_exchange([pb_pr]))
    half_pr = _add_chips(ids, g32_pr, ra_pr, rb_pr)
    dxbc, ddt, d_a, d_dsk = _ssd_bwd(xbc, dtlf, a_row, dsk_row, hin, dy)
    dxbc_raw, d_conv_w, d_conv_b = _conv_bwd(dxbc, proj, conv_w, conv_b)
    dc_tok = jnp.transpose(dc_key, (1, 0, 2)).reshape(t, H_ATT) + dc_qry.reshape(H_ATT, t).T
    dsm = ddt + jnp.pad(dc_tok, ((0, 0), (H_SSD, N_SMALL - H_SSD - H_ATT)))
    dsmall, d_bias_row = _small_bwd(dsm, small, bias_row)
    dproj = [dz, dxbc_raw, dza, dq, dk, dv, dgraw]
    return dict(loss_blk=loss_blk, u=u, dy_out=dy_out, dproj=dproj, dsmall=dsmall, half_pr=half_pr,
                d_conv_w=d_conv_w, d_conv_b=d_conv_b,
                d_bias_row=d_bias_row, d_a=d_a, d_dsk=d_dsk, d_ssd_norm=d_ssd_norm,
                d_gate_bias=d_gate_bias, d_norm_post=d_norm_post)


def _to_aligned_rows(slots):
    w = slots.reshape(N_COLS, slots.shape[2])

    def cut(o):
        return w[o[0]:o[0] + o[1]]
    main = jnp.concatenate([cut(O_Z), cut(O_XBC), cut(O_ZA), cut(O_Q), cut(O_K), cut(O_V), cut(O_G)], axis=0)
    pad = jnp.zeros((N_SMALL - H_SSD - H_ATT, w.shape[1]), w.dtype)
    small = jnp.concatenate([cut(O_DT), cut(O_F), pad], axis=0)
    return main, small


def _from_aligned_rows(main, small):
    def cm(c0, n):
        return main[c0:c0 + n]
    flat = jnp.concatenate([cm(C_Z, 2048), cm(C_XBC, 3072), small[0:H_SSD], cm(C_ZA, 1024),
                            cm(C_Q, 1024), cm(C_K, 1024), cm(C_V, 1024), small[H_SSD:H_SSD + H_ATT],
                            cm(C_G, 2048)], axis=0)
    return flat.reshape(4, N_COLS // 4, flat.shape[1])


_MESH = pl.DeviceIdType.MESH
_ANY = pl.BlockSpec(memory_space=pl.ANY)
_VM = pl.BlockSpec(memory_space=pltpu.VMEM)
_HALF = 512
N_DEV = 8


def _coords():
    return lax.axis_index("x"), lax.axis_index("y"), lax.axis_index("c")


def _other_chips(x, y):
    return [(1 - x, y), (x, 1 - y), (1 - x, 1 - y)]


def _half(cc):
    return pl.ds(cc * _HALF, _HALF)


def _gather_shards(slots):
    n = len(slots)

    def body(*refs):
        buf = refs[n:2 * n]
        send_sems, recv_sems = refs[2 * n:]
        x, y, c = _coords()
        chip = 2 * x + y
        sibling = (x, y, 1 - c)
        chips = _other_chips(x, y)

        def copy(i, frm, cc, k, to):
            part = buf[i].at[frm, :, _half(cc)]
            return pltpu.make_async_remote_copy(src_ref=part, dst_ref=part, send_sem=send_sems.at[6 * i + k],
                                                recv_sem=recv_sems.at[6 * i + k], device_id=to, device_id_type=_MESH)

        def chip_of(k):
            return 2 * chips[k][0] + chips[k][1]

        first = [copy(i, chip, c, k, (*chips[k], c)) for k in range(3) for i in range(n)]
        for cp in first:
            cp.start()
        passed = []
        for k in range(3):
            for i in range(n):
                copy(i, chip_of(k), c, k, (*chips[k], c)).wait_recv()
                passed.append(copy(i, chip_of(k), c, 3 + k, sibling))
                passed[-1].start()
        for k in range(3):
            for i in range(n):
                copy(i, chip_of(k), 1 - c, 3 + k, sibling).wait_recv()
        for cp in first + passed:
            cp.wait_send()

    return pl.pallas_call(
        body, name="gather_shards",
        out_shape=tuple(jax.ShapeDtypeStruct(s.shape, s.dtype) for s in slots),
        in_specs=[_ANY] * n, out_specs=tuple([_ANY] * n),
        input_output_aliases={i: i for i in range(n)},
        scratch_shapes=[pltpu.SemaphoreType.DMA((6 * n,)), pltpu.SemaphoreType.DMA((6 * n,))],
    )(*slots)


def _allgather8(block, name):
    rows, width = block.shape

    def body(x_ref, out_ref, send_sems, recv_sems, local_sem):
        x, y, c = _coords()
        me, sibling = (x, y, c), (x, y, 1 - c)
        chips = _other_chips(x, y)

        def slot(px, py, pc):
            return out_ref.at[4 * px + 2 * py + pc]

        def copy(k, blk, to, src=None):
            return pltpu.make_async_remote_copy(src_ref=slot(*blk) if src is None else src, dst_ref=slot(*blk),
                                                send_sem=send_sems.at[k], recv_sem=recv_sems.at[k],
                                                device_id=to, device_id_type=_MESH)

        mine = pltpu.make_async_copy(x_ref, slot(*me), local_sem)
        mine.start()
        first = [copy(0, me, sibling, src=x_ref)]
        first += [copy(1 + j, me, (*chip, c), src=x_ref) for j, chip in enumerate(chips)]
        for cp in first:
            cp.start()
        passed = [copy(4 + j, (*chip, c), sibling) for j, chip in enumerate(chips)]
        for j, chip in enumerate(chips):
            copy(1 + j, (*chip, c), me).wait_recv()
            passed[j].start()
        copy(0, sibling, me).wait_recv()
        for j, chip in enumerate(chips):
            copy(4 + j, (*chip, 1 - c), me).wait_recv()
        for cp in first + passed:
            cp.wait_send()
        mine.wait()

    return pl.pallas_call(
        body, name=name,
        out_shape=jax.ShapeDtypeStruct((N_DEV, rows, width), block.dtype),
        in_specs=[_VM], out_specs=_VM,
        scratch_shapes=[pltpu.SemaphoreType.DMA((7,)), pltpu.SemaphoreType.DMA((7,)), pltpu.SemaphoreType.DMA],
    )(block)


def _pair_swap(arrs):
    def copies(src, dst, send_sems, recv_sems):
        x, y, c = _coords()
        return [pltpu.make_async_remote_copy(src_ref=src[i].at[:, :, _half(1 - c)], dst_ref=dst[i],
                                             send_sem=send_sems.at[i], recv_sem=recv_sems.at[i],
                                             device_id=(x, y, 1 - c), device_id_type=_MESH) for i in range(len(src))]

    shapes = tuple(jax.ShapeDtypeStruct((4, a.shape[1], _HALF), a.dtype) for a in arrs)
    return tuple(arrs), shapes, copies, len(arrs), False


def _chip_exchange(arrs):
    def copies(src, dst, send_sems, recv_sems):
        x, y, c = _coords()
        chips = _other_chips(x, y)
        return [pltpu.make_async_remote_copy(src_ref=src[i].at[2 * chips[k][0] + chips[k][1]], dst_ref=dst[i].at[k],
                                             send_sem=send_sems.at[3 * i + k], recv_sem=recv_sems.at[3 * i + k],
                                             device_id=(*chips[k], c), device_id_type=_MESH)
                for k in range(3) for i in range(len(src))]

    shapes = tuple(jax.ShapeDtypeStruct((3,) + a.shape[1:], a.dtype) for a in arrs)
    return tuple(arrs), shapes, copies, 3 * len(arrs), False


def _gather_stage(slots, to_sibling):
    def copies(buf, _, send_sems, recv_sems):
        x, y, c = _coords()
        chips = _other_chips(x, y)
        out = []
        for k in range(3):
            for i in range(len(buf)):
                frm = 2 * chips[k][0] + chips[k][1] if to_sibling else 2 * x + y
                part = buf[i].at[frm, :, _half(c)]
                out.append(pltpu.make_async_remote_copy(
                    src_ref=part, dst_ref=part, send_sem=send_sems.at[3 * i + k], recv_sem=recv_sems.at[3 * i + k],
                    device_id=(x, y, 1 - c) if to_sibling else (*chips[k], c), device_id_type=_MESH))
        return out

    shapes = tuple(jax.ShapeDtypeStruct(s.shape, s.dtype) for s in slots)
    return tuple(slots), shapes, copies, 3 * len(slots), True


def _pair_join_halves(fulls):
    n = len(fulls)

    def body(*refs):
        buf = refs[n:2 * n]
        send_sems, recv_sems = refs[2 * n:]
        x, y, c = _coords()

        def remote(i, cc):
            part = buf[i].at[:, _half(cc)]
            return pltpu.make_async_remote_copy(src_ref=part, dst_ref=part, send_sem=send_sems.at[i],
                                                recv_sem=recv_sems.at[i], device_id=(x, y, 1 - c), device_id_type=_MESH)

        for i in range(n):
            remote(i, c).start()
        for i in range(n):
            remote(i, c).wait_send()
            remote(i, 1 - c).wait_recv()

    return pl.pallas_call(
        body, name="pair_join_halves",
        out_shape=tuple(jax.ShapeDtypeStruct(a.shape, a.dtype) for a in fulls),
        in_specs=[_ANY] * n, out_specs=tuple([_ANY] * n),
        input_output_aliases={i: i for i in range(n)},
        scratch_shapes=[pltpu.SemaphoreType.DMA((n,)), pltpu.SemaphoreType.DMA((n,))],
    )(*fulls)


_RED_TC = 128
_RED_NT = _HALF // _RED_TC


def _add_pair(ids, g32, recv_a):
    rows = g32.shape[1]

    def body(ids_ref, g_ref, r_ref, o_ref):
        o_ref[...] = (g_ref[...] + r_ref[...]).astype(BF16)

    blk = pl.BlockSpec((1, rows, _RED_TC), lambda j, l, ids: (j, 0, l))
    return pl.pallas_call(
        body, name="add_pair",
        out_shape=jax.ShapeDtypeStruct((4, rows, _HALF), BF16),
        grid_spec=pltpu.PrefetchScalarGridSpec(
            num_scalar_prefetch=1, grid=(4, _RED_NT),
            in_specs=[pl.BlockSpec((1, rows, _RED_TC), lambda j, l, ids: (j, 0, ids[0] * _RED_NT + l)), blk],
            out_specs=blk),
        compiler_params=_cp("parallel", "parallel"),
    )(ids, g32, recv_a)


def _add_chips(ids, g32, recv_a, recv_b):
    rows = g32.shape[1]

    def body(ids_ref, g_ref, a_ref, b_ref, o_ref):
        acc = g_ref[0] + a_ref[0]
        for k in range(3):
            acc = acc + b_ref[k].astype(F32)
        o_ref[...] = acc

    return pl.pallas_call(
        body, name="add_chips",
        out_shape=jax.ShapeDtypeStruct((rows, 2 * _HALF), F32),
        grid_spec=pltpu.PrefetchScalarGridSpec(
            num_scalar_prefetch=1, grid=(_RED_NT,),
            in_specs=[pl.BlockSpec((1, rows, _RED_TC), lambda l, ids: (ids[1], 0, ids[0] * _RED_NT + l)),
                      pl.BlockSpec((1, rows, _RED_TC), lambda l, ids: (ids[1], 0, l)),
                      pl.BlockSpec((3, rows, _RED_TC), lambda l, ids: (0, 0, l))],
            out_specs=pl.BlockSpec((rows, _RED_TC), lambda l, ids: (0, ids[0] * _RED_NT + l))),
        compiler_params=_cp("parallel"),
    )(ids, g32, recv_a, recv_b)


def _sum8(gathered):
    _, rows, width = gathered.shape

    def body(g_ref, o_ref):
        acc = g_ref[0]
        for d in range(1, N_DEV):
            acc = acc + g_ref[d]
        o_ref[...] = acc

    return pl.pallas_call(
        body, name="sum8",
        out_shape=jax.ShapeDtypeStruct((rows, width), F32),
        in_specs=[_VM], out_specs=_VM,
    )(gathered)


def _adamw(w, g, m, v, name):
    rows, cols = w.shape
    budget = (3 << 20) // 2
    tr, tc = rows, cols
    if rows * cols * 4 > budget:
        if rows % 8 == 0:
            tr = max(c for c in range(8, rows, 8) if rows % c == 0 and c * cols * 4 <= budget)
        else:
            tc = next(c for c in (512, 256, 128) if cols % c == 0 and rows * c * 4 <= budget)
    c1 = 1.0 - ADAM_B1 ** ADAM_STEP
    c2 = 1.0 - ADAM_B2 ** ADAM_STEP

    def body(w_ref, g_ref, m_ref, v_ref, d_ref, mo_ref, vo_ref):
        gg = g_ref[...]
        mn = ADAM_B1 * m_ref[...] + (1.0 - ADAM_B1) * gg
        vn = ADAM_B2 * v_ref[...] + (1.0 - ADAM_B2) * (gg * gg)
        mo_ref[...] = mn
        vo_ref[...] = vn
        d_ref[...] = -ADAM_LR * ((mn / c1) / (jnp.sqrt(vn / c2) + ADAM_EPS) + ADAM_WD * w_ref[...])

    blk = pl.BlockSpec((tr, tc), lambda i, j: (i, j))
    shp = jax.ShapeDtypeStruct((rows, cols), F32)
    return pl.pallas_call(
        body, name=name, out_shape=(shp, shp, shp), grid=(rows // tr, cols // tc),
        in_specs=[blk] * 4, out_specs=(blk, blk, blk),
        compiler_params=_cp("parallel", "parallel"),
    )(w, g, m, v)


def _rows128(a):
    return a.reshape(-1, 128)


def _pack_small(norm_pre, conv_b, ssd_norm, gate_bias, norm_post, dt_bias, a_log, d_skip, fgate_bias):
    tiny = jnp.concatenate([dt_bias.reshape(-1), a_log.reshape(-1), d_skip.reshape(-1), fgate_bias.reshape(-1),
                            jnp.zeros((16,), F32)])
    return jnp.concatenate([_rows128(norm_pre), _rows128(conv_b), _rows128(ssd_norm), _rows128(gate_bias),
                            _rows128(norm_post), tiny.reshape(1, 128)], axis=0)


_SMALL_ROWS = 73
_SMALL_PAD = 80


def _unpack_small(p):
    tiny = p[72]
    return dict(norm_pre=p[0:8].reshape(1, 1024), conv_b=p[8:32].reshape(1, 3072), ssd_norm=p[32:48].reshape(1, 2048),
                gate_bias=p[48:64].reshape(1, 2048), norm_post=p[64:72].reshape(1, 1024),
                dt_bias=tiny[0:32].reshape(1, 32), a_log=tiny[32:64].reshape(1, 32),
                d_skip=tiny[64:96].reshape(1, 32), fgate_bias=tiny[96:112].reshape(1, 16))


def _pad_rows(a, rows):
    return jnp.concatenate([a, jnp.zeros((rows - a.shape[0], a.shape[1]), a.dtype)], axis=0)


def kernel(x, meta_tokens, norm_pre, w_in, conv_w, conv_b, dt_bias, a_log, d_skip, ssd_norm, fgate_bias, gate_bias, w_proj_ssd, w_proj_att, w_out, norm_post, loss_target, m_meta_tokens, m_norm_pre, m_w_in, m_conv_w, m_conv_b, m_dt_bias, m_a_log, m_d_skip, m_ssd_norm, m_fgate_bias, m_gate_bias, m_w_proj_ssd, m_w_proj_att, m_w_out, m_norm_post, v_meta_tokens, v_norm_pre, v_w_in, v_conv_w, v_conv_b, v_dt_bias, v_a_log, v_d_skip, v_ssd_norm, v_fgate_bias, v_gate_bias, v_w_proj_ssd, v_w_proj_att, v_w_out, v_norm_post):
    cx, cy, cc = _coords()
    chip = 2 * cx + cy
    ids = jnp.stack([cc, chip]).astype(jnp.int32)
    seq = x.shape[1]

    w_in_sh = jnp.transpose(w_in[0]).astype(BF16)
    w_pr_sh = jnp.concatenate([w_proj_ssd[0], w_proj_att[0], w_out[0]], axis=0).astype(BF16)

    def own_slot(sh):
        return lax.dynamic_update_slice(lax.empty((4,) + sh.shape, sh.dtype), sh[None], (chip, 0, 0))

    (g_in,) = _gather_shards([own_slot(w_in_sh)])
    w_main, w_small = _to_aligned_rows(g_in)
    sm_sh = jnp.concatenate([_rows128(meta_tokens), _rows128(conv_w[0])], axis=0)
    sm_all = _allgather8(sm_sh, "gather_small_weights")[0::2]
    meta_full = jnp.transpose(sm_all[:, 0:32].reshape(4, N_META, 256), (1, 0, 2)).reshape(N_META, D_MODEL)
    conv_w_full = jnp.transpose(sm_all[:, 32:56].reshape(4, CONV_K, 768), (1, 0, 2)).reshape(CONV_K, CONV_DIM)

    h = jnp.concatenate([jnp.zeros((PADF, D_MODEL), F32), meta_full, x[0]], axis=0)
    bias_row = jnp.concatenate([dt_bias[0], fgate_bias[0], jnp.zeros((N_SMALL - H_SSD - H_ATT,), F32)]).reshape(1, N_SMALL)
    a_neg = -jnp.exp(a_log[0])
    a_row = jnp.concatenate([a_neg, jnp.zeros((N_SMALL - H_SSD,), F32)]).reshape(1, N_SMALL)
    dsk_row = jnp.repeat(d_skip[0], 64).reshape(1, D_SSD)
    r = _local_step(h, loss_target[0], w_main, w_small, own_slot(w_pr_sh), ids, norm_pre, conv_w_full, conv_b,
                    bias_row, a_row, dsk_row, ssd_norm, gate_bias, norm_post)

    tm = _mm_tiles(h.shape[0])
    n_row_tiles = h.shape[0] // tm
    d_w_main = _matmul_cat_tn(r["dproj"], r["u"], "inproj_dw", tm)
    d_w_small = _matmul(r["dsmall"], r["u"], "tn", F32, "inproj_small_dw", N_SMALL, D_MODEL, tm)
    g32_in = _from_aligned_rows(d_w_main, d_w_small)
    first = max(n_row_tiles // 6, 1)
    du_first, ra_in = _matmul_cat_nn(r["dproj"], w_main, "inproj_dx_swap", tm, rows=(0, first),
                                     exchange=_pair_swap([g32_in]))
    pb_in = _add_pair(ids, g32_in, ra_in)
    du_a, rb_in = _matmul_cat_nn(r["dproj"], w_main, "inproj_dx_exchange", tm,
                                 rows=(first, n_row_tiles - first), fill=du_first,
                                 exchange=_chip_exchange([pb_in]))
    du_b = _matmul(r["dsmall"], w_small, "nn", F32, "inproj_small_dx", tm, D_MODEL, N_SMALL)
    dh, d_norm_pre = _norm1_bwd(du_a, du_b, h, norm_pre, r["dy_out"])
    grad_x = dh[PADF + N_META:].reshape(1, seq, D_MODEL)
    half_in = _add_chips(ids, g32_in, ra_in, rb_in)
    gw_in, gw_pr = _pair_join_halves([half_in, r["half_pr"]])

    tiny = r["d_bias_row"][0]
    part_small = _pack_small(d_norm_pre, r["d_conv_b"], r["d_ssd_norm"], r["d_gate_bias"], r["d_norm_post"],
                             tiny[0:H_SSD], r["d_a"][0, 0:H_SSD] * a_neg, r["d_dsk"].reshape(H_SSD, 64).sum(axis=1),
                             tiny[H_SSD:H_SSD + H_ATT])
    part = jnp.concatenate([_pad_rows(part_small, _SMALL_PAD), _rows128(r["d_conv_w"]),
                            _rows128(dh[PADF:PADF + N_META]), r["loss_blk"]], axis=0)
    tot = _sum8(_allgather8(part, "gather_small_grads"))
    loss = tot[_SMALL_PAD + 96 + 128, 0]
    g_small = tot[0:_SMALL_PAD]
    g_conv_w = lax.dynamic_slice_in_dim(tot[_SMALL_PAD:_SMALL_PAD + 96].reshape(CONV_K, CONV_DIM), chip * 768, 768, axis=1)
    g_meta = lax.dynamic_slice_in_dim(tot[_SMALL_PAD + 96:_SMALL_PAD + 224].reshape(N_META, D_MODEL), chip * 256, 256, axis=1)

    upd = {}
    upd["w_in"] = tuple(jnp.transpose(a) for a in (gw_in,) + _adamw(
        jnp.transpose(w_in[0]), gw_in, jnp.transpose(m_w_in[0]), jnp.transpose(v_w_in[0]), "adamw_w_in"))
    w_pr32 = jnp.concatenate([w_proj_ssd[0], w_proj_att[0], w_out[0]], axis=0)
    m_pr = jnp.concatenate([m_w_proj_ssd[0], m_w_proj_att[0], m_w_out[0]], axis=0)
    v_pr = jnp.concatenate([v_w_proj_ssd[0], v_w_proj_att[0], v_w_out[0]], axis=0)
    pr = (gw_pr,) + _adamw(w_pr32, gw_pr, m_pr, v_pr, "adamw_w_proj")
    upd["w_proj_ssd"] = tuple(a[0:512] for a in pr)
    upd["w_proj_att"] = tuple(a[512:768] for a in pr)
    upd["w_out"] = tuple(a[768:1024] for a in pr)
    upd["conv_w"] = (g_conv_w,) + _adamw(conv_w[0], g_conv_w, m_conv_w[0], v_conv_w[0], "adamw_conv_w")
    upd["meta_tokens"] = (g_meta,) + _adamw(meta_tokens, g_meta, m_meta_tokens, v_meta_tokens, "adamw_meta")
    pk = lambda np_, cb, sn, gb, npo, dtb, al, ds, fg: _pad_rows(_pack_small(np_, cb, sn, gb, npo, dtb, al, ds, fg), _SMALL_PAD)
    w_sm = pk(norm_pre, conv_b, ssd_norm, gate_bias, norm_post, dt_bias, a_log, d_skip, fgate_bias)
    m_sm = pk(m_norm_pre, m_conv_b, m_ssd_norm, m_gate_bias, m_norm_post, m_dt_bias, m_a_log, m_d_skip, m_fgate_bias)
    v_sm = pk(v_norm_pre, v_conv_b, v_ssd_norm, v_gate_bias, v_norm_post, v_dt_bias, v_a_log, v_d_skip, v_fgate_bias)
    sm = [_unpack_small(a) for a in (g_small,) + _adamw(w_sm, g_small, m_sm, v_sm, "adamw_small")]
    for name in ("norm_pre", "conv_b", "dt_bias", "a_log", "d_skip", "ssd_norm", "fgate_bias", "gate_bias", "norm_post"):
        upd[name] = tuple(s[name] for s in sm)
    lead = ("w_in", "conv_w", "w_proj_ssd", "w_proj_att", "w_out")
    order = ("meta_tokens", "norm_pre", "w_in", "conv_w", "conv_b", "dt_bias", "a_log", "d_skip", "ssd_norm",
             "fgate_bias", "gate_bias", "w_proj_ssd", "w_proj_att", "w_out", "norm_post")
    outs = [loss, grad_x]
    for part_i in range(4):
        for name in order:
            a = upd[name][part_i]
            outs.append(a[None] if name in lead else a)
    return tuple(outs)
```

```python
import functools
import math

import jax
import jax.numpy as jnp
from jax import lax
from jax.experimental import pallas as pl
from jax.experimental.pallas import tpu as pltpu

F32 = jnp.float32
BF16 = jnp.bfloat16
HIGHEST = lax.Precision.HIGHEST

D_MODEL = 1024
N_META = 16
CHUNK = 128
PADF = CHUNK - N_META
D_SSD = 2048
H_SSD = 32
G_SSD = 4
N_STATE = 128
CONV_K = 4
CONV_DIM = D_SSD + 2 * G_SSD * N_STATE
H_ATT = 16
D_ATT = 1024
EPS = 1e-6
N_COLS = 11312

C_Z, C_XBC, C_ZA, C_Q, C_K, C_V, C_G = 0, 2048, 5120, 6144, 7168, 8192, 9216
N_MAIN = 11264
N_SMALL = 128
O_Z, O_XBC, O_DT, O_ZA, O_Q, O_K, O_V, O_F, O_G = (
    (0, 2048), (2048, 3072), (5120, 32), (5152, 1024), (6176, 1024), (7200, 1024),
    (8224, 1024), (9248, 16), (9264, 2048))

ADAM_LR, ADAM_B1, ADAM_B2, ADAM_EPS, ADAM_WD, ADAM_STEP = 0.001, 0.9, 0.999, 1e-08, 0.01, 10

VMEM_LIMIT = 56 * 1024 * 1024


def _cp(*sem):
    return pltpu.CompilerParams(dimension_semantics=sem, vmem_limit_bytes=VMEM_LIMIT)


def _tile(n, prefs):
    for p in prefs:
        if n % p == 0:
            return p
    raise ValueError(f"no tile for {n} in {prefs}")


def _iota(shape, dim):
    return lax.broadcasted_iota(jnp.int32, shape, dim)


def _sigmoid(x):
    return 1.0 / (1.0 + jnp.exp(-x))


def _softplus_tail(x):
    return jnp.log(1.0 + jnp.exp(-jnp.abs(x)))


_NN = (((1,), (0,)), ((), ()))
_NT = (((1,), (1,)), ((), ()))
_TN = (((0,), (0,)), ((), ()))


def _dot(a, b, dims=_NN):
    return lax.dot_general(a, b, dims, preferred_element_type=F32)


def _dot_exact(a, b, dims=_NN):
    return lax.dot_general(a, b, dims, precision=HIGHEST, preferred_element_type=F32)


def _hosted_call(body, *, name, grid, in_specs, out_specs, out_shape, scratch_shapes, operands, semantics,
                 exchange=None, aliases=None):
    aliases = dict(aliases or {})
    if exchange is None:
        return pl.pallas_call(body, name=name, out_shape=out_shape, grid=grid, in_specs=in_specs,
                              out_specs=out_specs, scratch_shapes=scratch_shapes, input_output_aliases=aliases,
                              compiler_params=_cp(*semantics))(*operands)
    arrays, shapes, copies, n_sems, in_place = exchange
    n_in, n_out, n_ex = len(operands), len(out_shape), len(arrays)

    def hosted(*refs):
        ex_in = refs[n_in:n_in + n_ex]
        ex_out = refs[n_in + n_ex + n_out:n_in + n_ex + n_out + n_ex]
        own = refs[:n_in] + refs[n_in + n_ex:n_in + n_ex + n_out] + refs[n_in + 2 * n_ex + n_out:-2]
        first = functools.reduce(lambda p, q: p & q, [pl.program_id(d) == 0 for d in range(len(grid))])
        last = functools.reduce(lambda p, q: p & q, [pl.program_id(d) == grid[d] - 1 for d in range(len(grid))])

        def descriptors():
            return copies(ex_out if in_place else ex_in, ex_out, refs[-2], refs[-1])

        @pl.when(first)
        def _():
            for cp in descriptors():
                cp.start()

        body(*own)

        @pl.when(last)
        def _():
            for cp in descriptors():
                cp.wait()

    return pl.pallas_call(
        hosted, name=name,
        out_shape=tuple(out_shape) + tuple(shapes),
        grid=grid,
        in_specs=list(in_specs) + [_ANY] * n_ex,
        out_specs=tuple(out_specs) + (_ANY,) * n_ex,
        input_output_aliases={**aliases, **({n_in + e: n_out + e for e in range(n_ex)} if in_place else {})},
        scratch_shapes=list(scratch_shapes) + [pltpu.SemaphoreType.DMA((n_sems,)), pltpu.SemaphoreType.DMA((n_sems,))],
        compiler_params=_cp(*(("arbitrary",) * len(grid))),
    )(*operands, *arrays)


def _matmul(a, b, mode, out_dtype, name, tm, tn, tk, exchange=None):
    if mode == "tn":
        kdim, m = a.shape
    else:
        m, kdim = a.shape
    n = b.shape[0] if mode == "nt" else b.shape[1]
    nk = kdim // tk
    dims = {"nn": _NN, "nt": _NT, "tn": _TN}[mode]
    a_spec = (pl.BlockSpec((tk, tm), lambda i, j, k: (k, i)) if mode == "tn"
              else pl.BlockSpec((tm, tk), lambda i, j, k: (i, k)))
    b_spec = (pl.BlockSpec((tn, tk), lambda i, j, k: (j, k)) if mode == "nt"
              else pl.BlockSpec((tk, tn), lambda i, j, k: (k, j)))

    def body(a_ref, b_ref, o_ref, acc_ref):
        k = pl.program_id(2)
        p = _dot(a_ref[...].astype(BF16), b_ref[...].astype(BF16), dims)
        if nk == 1:
            o_ref[...] = p.astype(out_dtype)
        else:
            @pl.when(k == 0)
            def _():
                acc_ref[...] = p

            @pl.when(k > 0)
            def _():
                acc_ref[...] += p

            @pl.when(k == nk - 1)
            def _():
                o_ref[...] = acc_ref[...].astype(out_dtype)

    out = _hosted_call(
        body, name=name,
        out_shape=(jax.ShapeDtypeStruct((m, n), out_dtype),),
        grid=(m // tm, n // tn, nk),
        in_specs=[a_spec, b_spec],
        out_specs=(pl.BlockSpec((tm, tn), lambda i, j, k: (i, j)),),
        scratch_shapes=[pltpu.VMEM((tm, tn), F32)],
        operands=(a, b), semantics=("parallel", "parallel", "arbitrary"), exchange=exchange)
    return out[0] if exchange is None else out


_CAT_BLK = 1024


def _piece_ranges(pieces):
    out, off = [], 0
    for p in pieces:
        nb = p.shape[1] // _CAT_BLK
        out.append((off, nb))
        off += nb
    return out, off


def _matmul_cat_nn(pieces, b, name, tm, rows=None, fill=None, exchange=None):
    t = pieces[0].shape[0]
    n = b.shape[1]
    ranges, nk = _piece_ranges(pieces)
    first, ni = rows if rows is not None else (0, t // tm)
    n_in = len(pieces) + 1 + (fill is not None)

    def body(*refs):
        a_refs, b_ref, o_ref, acc_ref = refs[:len(pieces)], refs[len(pieces)], refs[n_in], refs[n_in + 1]
        k = pl.program_id(1)

        @pl.when(k == 0)
        def _():
            acc_ref[...] = jnp.zeros_like(acc_ref)

        for a_ref, (off, nb) in zip(a_refs, ranges):
            @pl.when((k >= off) & (k < off + nb))
            def _(a_ref=a_ref):
                acc_ref[...] += _dot(a_ref[...], b_ref[...])

        @pl.when(k == nk - 1)
        def _():
            o_ref[...] = acc_ref[...]

    def a_spec(off, nb):
        return pl.BlockSpec((tm, _CAT_BLK), lambda i, k: (first + i, jnp.clip(k - off, 0, nb - 1)))

    in_specs = [a_spec(off, nb) for off, nb in ranges] + [pl.BlockSpec((_CAT_BLK, n), lambda i, k: (k, 0))]
    operands = list(pieces) + [b]
    if fill is not None:
        in_specs.append(_ANY)
        operands.append(fill)
    out = _hosted_call(
        body, name=name,
        out_shape=(jax.ShapeDtypeStruct((t, n), F32),),
        grid=(ni, nk),
        in_specs=in_specs,
        out_specs=(pl.BlockSpec((tm, n), lambda i, k: (first + i, 0)),),
        scratch_shapes=[pltpu.VMEM((tm, n), F32)],
        operands=operands, semantics=("parallel", "arbitrary"), exchange=exchange,
        aliases={len(pieces) + 1: 0} if fill is not None else None)
    return out if exchange is not None else out[0]


def _matmul_cat_tn(pieces, b, name, tk):
    t = pieces[0].shape[0]
    n = b.shape[1]
    ranges, nm = _piece_ranges(pieces)
    nk = t // tk

    def body(*refs):
        a_refs, b_ref, o_ref, acc_ref = refs[:len(pieces)], refs[-3], refs[-2], refs[-1]
        m = pl.program_id(0)
        k = pl.program_id(1)

        @pl.when(k == 0)
        def _():
            acc_ref[...] = jnp.zeros_like(acc_ref)

        for a_ref, (off, nb) in zip(a_refs, ranges):
            @pl.when((m >= off) & (m < off + nb))
            def _(a_ref=a_ref):
                acc_ref[...] += _dot(a_ref[...], b_ref[...], _TN)

        @pl.when(k == nk - 1)
        def _():
            o_ref[...] = acc_ref[...]

    def a_spec(off, nb):
        def index(m, k):
            mine = (m >= off) & (m < off + nb)
            return jnp.where(mine, k, 0), jnp.clip(m - off, 0, nb - 1)
        return pl.BlockSpec((tk, _CAT_BLK), index)

    return pl.pallas_call(
        body, name=name,
        out_shape=jax.ShapeDtypeStruct((nm * _CAT_BLK, n), F32),
        grid=(nm, nk),
        in_specs=[a_spec(off, nb) for off, nb in ranges] + [pl.BlockSpec((tk, n), lambda m, k: (k, 0))],
        out_specs=pl.BlockSpec((_CAT_BLK, n), lambda m, k: (m, 0)),
        scratch_shapes=[pltpu.VMEM((_CAT_BLK, n), F32)],
        compiler_params=_cp("parallel", "arbitrary"),
    )(*pieces, b)


def _row_tile(t):
    return _tile(t, (352, 128))


def _row_tile_wide(t):
    return _tile(t, (176, 128))


def _norm1_fwd(h, g):
    t = h.shape[0]
    tm = _row_tile(t)

    def body(h_ref, g_ref, u_ref):
        x = h_ref[...]
        r = lax.rsqrt(jnp.mean(x * x, axis=-1, keepdims=True) + EPS)
        u_ref[...] = (x * r * g_ref[...]).astype(BF16)

    return pl.pallas_call(
        body, name="norm1_fwd",
        out_shape=jax.ShapeDtypeStruct((t, D_MODEL), BF16),
        grid=(t // tm,),
        in_specs=[pl.BlockSpec((tm, D_MODEL), lambda i: (i, 0)),
                  pl.BlockSpec((1, D_MODEL), lambda i: (0, 0))],
        out_specs=pl.BlockSpec((tm, D_MODEL), lambda i: (i, 0)),
        compiler_params=_cp("parallel"),
    )(h, g)


def _norm1_bwd(du_a, du_b, h, g, dy):
    t = h.shape[0]
    tm = _row_tile(t)

    def body(a_ref, b_ref, h_ref, g_ref, dy_ref, dh_ref, dg_ref):
        i = pl.program_id(0)
        x = h_ref[...]
        du = a_ref[...] + b_ref[...]
        r = lax.rsqrt(jnp.mean(x * x, axis=-1, keepdims=True) + EPS)
        gdu = du * g_ref[...]
        dh_ref[...] = dy_ref[...] + r * (gdu - x * (r * r) * jnp.mean(gdu * x, axis=-1, keepdims=True))
        part = jnp.sum(du * x * r, axis=0, keepdims=True)

        @pl.when(i == 0)
        def _():
            dg_ref[...] = part

        @pl.when(i > 0)
        def _():
            dg_ref[...] += part

    row = pl.BlockSpec((tm, D_MODEL), lambda i: (i, 0))
    vec = pl.BlockSpec((1, D_MODEL), lambda i: (0, 0))
    return pl.pallas_call(
        body, name="norm1_bwd",
        out_shape=(jax.ShapeDtypeStruct((t, D_MODEL), F32), jax.ShapeDtypeStruct((1, D_MODEL), F32)),
        grid=(t // tm,),
        in_specs=[row, row, row, vec, row],
        out_specs=(row, vec),
        compiler_params=_cp("arbitrary"),
    )(du_a, du_b, h, g, dy)


def _small_fwd(small, bias_row):
    t = small.shape[0]

    def body(s_ref, b_ref, o_ref, carry_ref):
        c = pl.program_id(0)

        @pl.when(c == 0)
        def _():
            carry_ref[...] = jnp.zeros_like(carry_ref)

        x = s_ref[...] + b_ref[...]
        r0 = _iota((CHUNK, CHUNK), 0)
        r1 = _iota((CHUNK, CHUNK), 1)
        valid = (c * CHUNK + r0) >= PADF
        tail = _softplus_tail(x)
        dt = jnp.where(valid & (r1 < H_SSD), jnp.maximum(x, 0.0) + tail, 0.0)
        lf = jnp.where(valid & (r1 >= H_SSD) & (r1 < H_SSD + H_ATT), jnp.minimum(x, 0.0) - tail, 0.0)
        tri = (r0 >= r1).astype(F32)
        cs = _dot_exact(tri, lf) + carry_ref[...]
        carry_ref[...] = cs[CHUNK - 1:CHUNK, :]
        o_ref[...] = dt + cs

    return pl.pallas_call(
        body, name="small_fwd",
        out_shape=jax.ShapeDtypeStruct((t, N_SMALL), F32),
        grid=(t // CHUNK,),
        in_specs=[pl.BlockSpec((CHUNK, N_SMALL), lambda c: (c, 0)),
                  pl.BlockSpec((1, N_SMALL), lambda c: (0, 0))],
        out_specs=pl.BlockSpec((CHUNK, N_SMALL), lambda c: (c, 0)),
        scratch_shapes=[pltpu.VMEM((1, N_SMALL), F32)],
        compiler_params=_cp("arbitrary"),
    )(small, bias_row)


def _small_bwd(dsm, small, bias_row):
    t = small.shape[0]
    nc = t // CHUNK

    def body(d_ref, s_ref, b_ref, o_ref, db_ref, carry_ref):
        step = pl.program_id(0)
        c = nc - 1 - step

        @pl.when(step == 0)
        def _():
            carry_ref[...] = jnp.zeros_like(carry_ref)
            db_ref[...] = jnp.zeros_like(db_ref)

        x = s_ref[...] + b_ref[...]
        d = d_ref[...]
        r0 = _iota((CHUNK, CHUNK), 0)
        r1 = _iota((CHUNK, CHUNK), 1)
        valid = (c * CHUNK + r0) >= PADF
        is_dt = r1 < H_SSD
        is_f = (r1 >= H_SSD) & (r1 < H_SSD + H_ATT)
        triu = (r1 >= r0).astype(F32)
        dc = jnp.where(is_f, d, 0.0)
        dlf = _dot_exact(triu, dc) + carry_ref[...]
        carry_ref[...] = dlf[0:1, :]
        sg = _sigmoid(x)
        out = jnp.where(valid & is_dt, d * sg, 0.0) + jnp.where(valid & is_f, dlf * (1.0 - sg), 0.0)
        o_ref[...] = out.astype(BF16)
        db_ref[...] += jnp.sum(out, axis=0, keepdims=True)

    blk = pl.BlockSpec((CHUNK, N_SMALL), lambda s: (nc - 1 - s, 0))
    vec = pl.BlockSpec((1, N_SMALL), lambda s: (0, 0))
    return pl.pallas_call(
        body, name="small_bwd",
        out_shape=(jax.ShapeDtypeStruct((t, N_SMALL), BF16), jax.ShapeDtypeStruct((1, N_SMALL), F32)),
        grid=(nc,),
        in_specs=[blk, blk, vec],
        out_specs=(blk, vec),
        scratch_shapes=[pltpu.VMEM((1, N_SMALL), F32)],
        compiler_params=_cp("arbitrary"),
    )(dsm, small, bias_row)


_CONV_TC = 1024
_XBC_BLK = C_XBC // _CONV_TC


def _shift_down(cur, prev8, j):
    rc = pltpu.roll(cur, j, 0)
    rid = _iota(prev8.shape, 0)
    top = jnp.where(rid < j, pltpu.roll(prev8, j, 0), rc[0:8, :])
    return top if cur.shape[0] == 8 else jnp.concatenate([top, rc[8:, :]], axis=0)


def _shift_up(cur, next8, j):
    n = cur.shape[0]
    ru = pltpu.roll(cur, n - j, 0)
    rid = _iota(next8.shape, 0)
    bot = jnp.where(rid >= 8 - j, pltpu.roll(next8, 8 - j, 0), ru[n - 8:, :])
    return jnp.concatenate([ru[:n - 8, :], bot], axis=0)


def _conv_taps(cur, prev, w, b):
    taps = [cur] + [_shift_down(cur, prev, j) for j in (1, 2, 3)]
    acc = b + taps[0] * w[3:4, :]
    for j in (1, 2, 3):
        acc = acc + taps[j] * w[3 - j:4 - j, :]
    return acc, taps


def _conv_pre(x_ref, p_ref, w_ref, b_ref, i):
    return _conv_taps(x_ref[...], jnp.where(i > 0, p_ref[...], 0.0), w_ref[...], b_ref[...])


def _dsilu(d, acc):
    sg = _sigmoid(acc)
    return d * sg * (1.0 + acc * (1.0 - sg))


def _conv_fwd(proj, conv_w, conv_b):
    t = proj.shape[0]
    tr = _row_tile(t)

    def body(x_ref, p_ref, w_ref, b_ref, o_ref):
        i = pl.program_id(0)
        acc, _ = _conv_pre(x_ref, p_ref, w_ref, b_ref, i)
        valid = (i * tr + _iota(acc.shape, 0)) >= PADF
        o_ref[...] = jnp.where(valid, acc * _sigmoid(acc), 0.0)

    return pl.pallas_call(
        body, name="conv_fwd",
        out_shape=jax.ShapeDtypeStruct((t, CONV_DIM), F32),
        grid=(t // tr, CONV_DIM // _CONV_TC),
        in_specs=[pl.BlockSpec((tr, _CONV_TC), lambda i, j: (i, _XBC_BLK + j)),
                  pl.BlockSpec((8, _CONV_TC), lambda i, j: (jnp.maximum(i * (tr // 8) - 1, 0), _XBC_BLK + j)),
                  pl.BlockSpec((CONV_K, _CONV_TC), lambda i, j: (0, j)),
                  pl.BlockSpec((1, _CONV_TC), lambda i, j: (0, j))],
        out_specs=pl.BlockSpec((tr, _CONV_TC), lambda i, j: (i, j)),
        compiler_params=_cp("parallel", "parallel"),
    )(proj, proj, conv_w, conv_b)


def _conv_bwd(dxbc, proj, conv_w, conv_b):
    t = proj.shape[0]
    tr = _row_tile(t)
    n_tiles = t // tr
    last8 = t // 8 - 1

    def body(d_ref, dn_ref, x_ref, p_ref, xn_ref, w_ref, b_ref, dx_ref, dw_ref, db_ref):
        i = pl.program_id(1)
        w = w_ref[...]
        b = b_ref[...]
        cur = x_ref[...]
        acc, taps = _conv_taps(cur, jnp.where(i > 0, p_ref[...], 0.0), w, b)
        valid = (i * tr + _iota(acc.shape, 0)) >= PADF
        da = jnp.where(valid, _dsilu(d_ref[...], acc), 0.0)
        acc_n, _ = _conv_taps(xn_ref[...], cur[tr - 8:, :], w, b)
        da_n = jnp.where(i < n_tiles - 1, _dsilu(dn_ref[...], acc_n), 0.0)
        dx = da * w[3:4, :]
        for j in (1, 2, 3):
            dx = dx + _shift_up(da, da_n, j) * w[3 - j:4 - j, :]
        dx_ref[...] = dx.astype(BF16)
        dw = jnp.concatenate([jnp.sum(da * taps[3 - k], axis=0, keepdims=True) for k in range(CONV_K)], axis=0)
        db = jnp.sum(da, axis=0, keepdims=True)

        @pl.when(i == 0)
        def _():
            dw_ref[...] = dw
            db_ref[...] = db

        @pl.when(i > 0)
        def _():
            dw_ref[...] += dw
            db_ref[...] += db

    nxt8 = lambda i: jnp.minimum((i + 1) * (tr // 8), last8)
    return pl.pallas_call(
        body, name="conv_bwd",
        out_shape=(jax.ShapeDtypeStruct((t, CONV_DIM), BF16),
                   jax.ShapeDtypeStruct((CONV_K, CONV_DIM), F32),
                   jax.ShapeDtypeStruct((1, CONV_DIM), F32)),
        grid=(CONV_DIM // _CONV_TC, n_tiles),
        in_specs=[pl.BlockSpec((tr, _CONV_TC), lambda j, i: (i, j)),
                  pl.BlockSpec((8, _CONV_TC), lambda j, i: (nxt8(i), j)),
                  pl.BlockSpec((tr, _CONV_TC), lambda j, i: (i, _XBC_BLK + j)),
                  pl.BlockSpec((8, _CONV_TC), lambda j, i: (jnp.maximum(i * (tr // 8) - 1, 0), _XBC_BLK + j)),
                  pl.BlockSpec((8, _CONV_TC), lambda j, i: (nxt8(i), _XBC_BLK + j)),
                  pl.BlockSpec((CONV_K, _CONV_TC), lambda j, i: (0, j)),
                  pl.BlockSpec((1, _CONV_TC), lambda j, i: (0, j))],
        out_specs=(pl.BlockSpec((tr, _CONV_TC), lambda j, i: (i, j)),
                   pl.BlockSpec((CONV_K, _CONV_TC), lambda j, i: (0, j)),
                   pl.BlockSpec((1, _CONV_TC), lambda j, i: (0, j))),
        compiler_params=_cp("parallel", "arbitrary"),
    )(dxbc, dxbc, proj, proj, proj, conv_w, conv_b)


_GW = D_SSD // G_SSD


def _ssd_prelude(dt_ref, a_ref, e_scr, es_scr, dte_scr):
    r0 = _iota((CHUNK, CHUNK), 0)
    r1 = _iota((CHUNK, CHUNK), 1)
    dt = jnp.where(r1 < H_SSD, dt_ref[...], 0.0)
    adt = dt * a_ref[...]
    acs = _dot_exact((r0 >= r1).astype(F32), adt)
    acs_t = acs.T
    alast = acs[CHUNK - 1:CHUNK, :]
    exp_a = jnp.exp(acs)
    dec_s = jnp.exp(alast - acs)
    lo = r1 < 64
    for j in range(H_SSD // 2):
        sl = slice(CHUNK * j, CHUNK * (j + 1))
        e_scr[:, sl] = jnp.where(lo, exp_a[:, 2 * j:2 * j + 1], exp_a[:, 2 * j + 1:2 * j + 2])
        es_scr[:, sl] = jnp.where(lo, dec_s[:, 2 * j:2 * j + 1], dec_s[:, 2 * j + 1:2 * j + 2])
        dte_scr[:, sl] = jnp.where(lo, dt[:, 2 * j:2 * j + 1], dt[:, 2 * j + 1:2 * j + 2])
    return dt, acs, acs_t, r0, r1, lo


def _chunk_decay_rows(acs_t, g):
    cd_t = jnp.exp(acs_t[:, CHUNK - 1:CHUNK])
    return jnp.concatenate(
        [jnp.broadcast_to(cd_t[8 * g + hh:8 * g + hh + 1, :], (64, N_STATE)) for hh in range(8)], axis=0)


def _ssd_fwd(xbc, dtlf, a_row, dsk_row):
    t = xbc.shape[0]
    nc = t // CHUNK

    def body(xs_ref, b_ref, c_ref, dt_ref, a_ref, dsk_ref, y_ref, hin_ref, h_scr, e_scr, es_scr, dte_scr):
        c = pl.program_id(0)

        @pl.when(c == 0)
        def _():
            h_scr[...] = jnp.zeros_like(h_scr)

        dt, acs, acs_t, r0, r1, lo = _ssd_prelude(dt_ref, a_ref, e_scr, es_scr, dte_scr)
        causal = r0 >= r1
        for g in range(G_SSD):
            gs = slice(_GW * g, _GW * (g + 1))
            bg = b_ref[:, N_STATE * g:N_STATE * (g + 1)].astype(BF16)
            cg = c_ref[:, N_STATE * g:N_STATE * (g + 1)].astype(BF16)
            cb = _dot(cg, bg, _NT)
            hg = h_scr[gs, :]
            hin_ref[0, gs, :] = hg
            xg = xs_ref[:, gs] * dte_scr[:, gs]
            yoff = _dot(cg, hg.astype(BF16), _NT) * e_scr[:, gs]
            st = _dot((xg * es_scr[:, gs]).astype(BF16), bg, _TN)
            h_scr[gs, :] = hg * _chunk_decay_rows(acs_t, g) + st
            for jj in range(4):
                j = 4 * g + jj
                sl = slice(CHUNK * j, CHUNK * (j + 1))
                xp = xg[:, CHUNK * jj:CHUNK * (jj + 1)]
                acc = yoff[:, CHUNK * jj:CHUNK * (jj + 1)] + dsk_ref[:, sl] * xs_ref[:, sl]
                for hh in range(2):
                    h = 2 * j + hh
                    seg = acs[:, h:h + 1] - acs_t[h:h + 1, :]
                    lm = jnp.exp(jnp.where(causal, seg, -1e30))
                    m = (cb * lm).astype(BF16)
                    xh = jnp.where(lo if hh == 0 else ~lo, xp, 0.0).astype(BF16)
                    acc = acc + _dot(m, xh)
                y_ref[:, sl] = acc

    return pl.pallas_call(
        body, name="ssd_fwd",
        out_shape=(jax.ShapeDtypeStruct((t, D_SSD), F32), jax.ShapeDtypeStruct((nc, D_SSD, N_STATE), F32)),
        grid=(nc,),
        in_specs=[pl.BlockSpec((CHUNK, D_SSD), lambda c: (c, 0)),
                  pl.BlockSpec((CHUNK, _GW), lambda c: (c, 4)),
                  pl.BlockSpec((CHUNK, _GW), lambda c: (c, 5)),
                  pl.BlockSpec((CHUNK, N_SMALL), lambda c: (c, 0)),
                  pl.BlockSpec((1, N_SMALL), lambda c: (0, 0)),
                  pl.BlockSpec((1, D_SSD), lambda c: (0, 0))],
        out_specs=(pl.BlockSpec((CHUNK, D_SSD), lambda c: (c, 0)),
                   pl.BlockSpec((1, D_SSD, N_STATE), lambda c: (c, 0, 0))),
        scratch_shapes=[pltpu.VMEM((D_SSD, N_STATE), F32)] + [pltpu.VMEM((CHUNK, D_SSD), F32)] * 3,
        compiler_params=_cp("arbitrary"),
    )(xbc, xbc, xbc, dtlf, a_row, dsk_row)


def _ssd_bwd(xbc, dtlf, a_row, dsk_row, hin, dy):
    t = xbc.shape[0]
    nc = t // CHUNK

    def body(xs_ref, b_ref, c_ref, dt_ref, a_ref, dsk_ref, hin_ref, dy_ref,
             dxbc_ref, ddt_ref, da_ref, ddsk_ref, dh_scr, e_scr, es_scr, dte_scr, dx_scr, whi_scr, wlo_scr):
        step = pl.program_id(0)

        @pl.when(step == 0)
        def _():
            dh_scr[...] = jnp.zeros_like(dh_scr)
            da_ref[...] = jnp.zeros_like(da_ref)
            ddsk_ref[...] = jnp.zeros_like(ddsk_ref)

        dt, acs, acs_t, r0, r1, lo = _ssd_prelude(dt_ref, a_ref, e_scr, es_scr, dte_scr)
        causal = r0 >= r1
        lane_row = _iota((1, CHUNK), 1)
        dacs = jnp.zeros((CHUNK, CHUNK), F32)
        dacs_t = jnp.zeros((CHUNK, CHUNK), F32)
        dalast = jnp.zeros((1, CHUNK), F32)
        ddt_dir = jnp.zeros((CHUNK, CHUNK), F32)
        ddsk_ref[...] += jnp.sum(dy_ref[...] * xs_ref[...], axis=0, keepdims=True)

        def head_sums(z, pick):
            hi = z.astype(BF16)
            return _dot(hi, pick) + _dot((z - hi.astype(F32)).astype(BF16), pick)

        for g in range(G_SSD):
            gs = slice(_GW * g, _GW * (g + 1))
            pick = (jnp.right_shift(_iota((_GW, CHUNK), 0), 6) + 8 * g == _iota((_GW, CHUNK), 1)).astype(BF16)
            bg = b_ref[:, N_STATE * g:N_STATE * (g + 1)].astype(BF16)
            cg = c_ref[:, N_STATE * g:N_STATE * (g + 1)].astype(BF16)
            cb = _dot(cg, bg, _NT)
            hg = hin_ref[0, gs, :]
            hgb = hg.astype(BF16)
            dhn = dh_scr[gs, :]
            dhnb = dhn.astype(BF16)
            esg = es_scr[:, gs]
            dyg = dy_ref[:, gs]
            xsg = xs_ref[:, gs]
            xg = xsg * dte_scr[:, gs]
            dyeb = (dyg * e_scr[:, gs]).astype(BF16)
            dc = _dot(dyeb, hgb)
            dh_y = _dot(dyeb, cg, _TN)
            dxs = _dot(bg, dhnb, _NT) * esg
            db = _dot((xg * esg).astype(BF16), dhnb)
            cd = _chunk_decay_rows(acs_t, g)
            dh_scr[gs, :] = dhn * cd + dh_y
            end_state = head_sums(jnp.broadcast_to(jnp.sum(xg * dxs, axis=0, keepdims=True), (8, _GW)), pick)[0:1, :]
            carried = dhn * hg * cd
            per_head = jnp.concatenate([jnp.sum(carried[64 * hh:64 * hh + 64, :], axis=0, keepdims=True)
                                        for hh in range(8)], axis=0)
            per_head = jnp.sum(per_head, axis=1, keepdims=True)
            for hh in range(8):
                end_state = end_state + jnp.where(lane_row == 8 * g + hh, per_head[hh:hh + 1, :], 0.0)
            dalast = dalast + end_state
            dcb = jnp.zeros((CHUNK, CHUNK), F32)
            for jj in range(4):
                j = 4 * g + jj
                sl = slice(CHUNK * j, CHUNK * (j + 1))
                ps = slice(CHUNK * jj, CHUNK * (jj + 1))
                xpb = xg[:, ps].astype(BF16)
                dyp = dyg[:, ps]
                dxp = dxs[:, ps]
                for hh in range(2):
                    h = 2 * j + hh
                    ws = slice(CHUNK * (2 * jj + hh), CHUNK * (2 * jj + hh + 1))
                    seg = acs[:, h:h + 1] - acs_t[h:h + 1, :]
                    lm = jnp.exp(jnp.where(causal, seg, -1e30))
                    mf = cb * lm
                    dyh = jnp.where(lo if hh == 0 else ~lo, dyp, 0.0).astype(BF16)
                    gm = _dot(dyh, xpb, _NT)
                    dcb = dcb + gm * lm
                    w = gm * mf
                    whi = w.astype(BF16)
                    whi_scr[:, ws] = whi
                    wlo_scr[:, ws] = (w - whi.astype(F32)).astype(BF16)
                    dacs_t = dacs_t - jnp.where(r0 == h, jnp.sum(w, axis=0, keepdims=True), 0.0)
                    dxp = dxp + _dot(mf.astype(BF16), dyh, _TN)
                dx_scr[:, sl] = dxp
            dxg = dx_scr[:, gs]
            pick_w = (jnp.right_shift(_iota((8 * CHUNK, CHUNK), 0), 7) + 8 * g == _iota((8 * CHUNK, CHUNK), 1)).astype(BF16)
            ch = _dot(cg, hgb, _NT)
            dacs = (dacs + _dot(whi_scr[...], pick_w) + _dot(wlo_scr[...], pick_w)
                    + head_sums(dyg * e_scr[:, gs] * ch - xg * dxs, pick))
            ddt_dir = ddt_dir + head_sums(dxg * xsg, pick)
            dcbb = dcb.astype(BF16)
            dxbc_ref[:, D_SSD + N_STATE * g:D_SSD + N_STATE * (g + 1)] = db + _dot(dcbb, cg, _TN)
            dxbc_ref[:, D_SSD + _GW + N_STATE * g:D_SSD + _GW + N_STATE * (g + 1)] = dc + _dot(dcbb, bg)
        dxbc_ref[:, 0:D_SSD] = dx_scr[...] * dte_scr[...] + dsk_ref[...] * dy_ref[...]
        dacs = dacs + dacs_t.T + jnp.where(r0 == CHUNK - 1, dalast, 0.0)
        dadt = _dot_exact((r1 >= r0).astype(F32), dacs)
        ddt_ref[...] = dadt * a_ref[...] + ddt_dir
        da_ref[...] += jnp.sum(dadt * dt, axis=0, keepdims=True)

    rev = lambda s: (nc - 1 - s, 0)
    return pl.pallas_call(
        body, name="ssd_bwd",
        out_shape=(jax.ShapeDtypeStruct((t, CONV_DIM), F32), jax.ShapeDtypeStruct((t, N_SMALL), F32),
                   jax.ShapeDtypeStruct((1, N_SMALL), F32), jax.ShapeDtypeStruct((1, D_SSD), F32)),
        grid=(nc,),
        in_specs=[pl.BlockSpec((CHUNK, D_SSD), rev),
                  pl.BlockSpec((CHUNK, _GW), lambda s: (nc - 1 - s, 4)),
                  pl.BlockSpec((CHUNK, _GW), lambda s: (nc - 1 - s, 5)),
                  pl.BlockSpec((CHUNK, N_SMALL), rev),
                  pl.BlockSpec((1, N_SMALL), lambda s: (0, 0)),
                  pl.BlockSpec((1, D_SSD), lambda s: (0, 0)),
                  pl.BlockSpec((1, D_SSD, N_STATE), lambda s: (nc - 1 - s, 0, 0)),
                  pl.BlockSpec((CHUNK, D_SSD), rev)],
        out_specs=(pl.BlockSpec((CHUNK, CONV_DIM), rev),
                   pl.BlockSpec((CHUNK, N_SMALL), rev),
                   pl.BlockSpec((1, N_SMALL), lambda s: (0, 0)),
                   pl.BlockSpec((1, D_SSD), lambda s: (0, 0))),
        scratch_shapes=([pltpu.VMEM((D_SSD, N_STATE), F32)] + [pltpu.VMEM((CHUNK, D_SSD), F32)] * 4
                        + [pltpu.VMEM((CHUNK, 8 * CHUNK), BF16)] * 2),
        compiler_params=_cp("arbitrary"),
    )(xbc, xbc, xbc, dtlf, a_row, dsk_row, hin, dy)


_NPAIR = H_ATT // 2
_QB, _KB, _VB = C_Q // 128, C_K // 128, C_V // 128
_SCALE = 1.0 / math.sqrt(64.0)


def _attn_blocks(t):
    return _tile(t, (1408, 384, 256, 128)), _tile(t, (384, 128))


def _split3(c):
    hi = c.astype(BF16).astype(F32)
    rest = c - hi
    mid = rest.astype(BF16).astype(F32)
    return hi, mid, rest - mid


def _head_lanes(lane, hh):
    return (lane < 64, 64) if hh == 0 else (lane >= 64, 0)


def _q_operand(q, cq, lane, hh):
    sel, first = _head_lanes(lane, hh)
    out = jnp.where(sel, q, 0.0)
    for n, col in enumerate(_split3(cq) + (1.0, 1.0, 1.0)):
        out = jnp.where(lane == first + n, col, out)
    return out.astype(BF16)


def _k_operand(k, ck, lane, hh):
    sel, first = _head_lanes(lane, hh)
    hi, mid, lo = _split3(ck)
    out = jnp.where(sel, k, 0.0)
    for n, col in enumerate((1.0, 1.0, 1.0, -hi, -mid, -lo)):
        out = jnp.where(lane == first + n, col, out)
    return out.astype(BF16)


_C_FILLER = 2.0 ** 30
_SKIP_STEP = 512


def _query_skips(bq):
    firsts = list(range(0, bq, _SKIP_STEP))
    far = 1 << 30
    return [(q0 if n else -far, firsts[n + 1] if n + 1 < len(firsts) else far, q0) for n, q0 in enumerate(firsts)]


def _attn_fwd(proj, c_col):
    t = proj.shape[0]
    bq, bk = _attn_blocks(t)
    nq, nk = t // bq, t // bk
    rs = 32

    def last_kv(i):
        return (i * bq + bq - 1) // bk

    def body(q_ref, k_ref, v_ref, cq_ref, ck_ref, o_ref, lse_ref, p_ref, mrun_ref, qs_scr, s_scr, m_scr, acc_scr):
        i = pl.program_id(1)
        kk = pl.program_id(2)
        lane_q = _iota((bq, 128), 1)

        @pl.when(kk == 0)
        def _():
            m_scr[...] = jnp.full_like(m_scr, -1e30)
            acc_scr[...] = jnp.zeros_like(acc_scr)
            q = q_ref[...] * _SCALE
            cq = cq_ref[0]
            for hh in range(2):
                qs_scr[hh] = _q_operand(q, cq[:, hh:hh + 1], lane_q, hh)

        def step(masked, q0):
            nqc = bq - q0
            lane_k = _iota((bk, 128), 1)
            k = k_ref[...]
            v = v_ref[...]
            ck = ck_ref[0]
            ahead = _iota((rs, nqc), 0) - _iota((rs, nqc), 1) - q0
            vss = []
            for hh in range(2):
                sel, first = _head_lanes(lane_k, hh)
                ks = _k_operand(k, ck[:, hh:hh + 1], lane_k, hh)
                vss.append(jnp.where(sel, v, jnp.where(lane_k == first, 1.0, 0.0)).astype(BF16))
                s_scr[hh, :, q0:] = _dot(ks, qs_scr[hh, q0:, :], _NT)
            for hh in range(2):
                vs = vss[hh]

                def block_max(r, mx):
                    rows = pl.ds(pl.multiple_of(r * rs, rs), rs)
                    s = s_scr[hh, rows, q0:]
                    if masked:
                        s = jnp.where(ahead <= i * bq - kk * bk - r * rs, s, -1e30)
                        s_scr[hh, rows, q0:] = s
                    return jnp.maximum(mx, s)

                mx = lax.fori_loop(0, bk // rs, block_max, jnp.full((rs, nqc), -1e30, F32), unroll=True)
                m_old = m_scr[hh, :, q0:]
                m_new = jnp.maximum(m_old, jnp.max(mx, axis=0, keepdims=True))
                m_scr[hh, :, q0:] = m_new
                mrun_ref[0, hh:hh + 1, q0:] = m_new

                def probs(r, carry):
                    rows = pl.ds(pl.multiple_of(r * rs, rs), rs)
                    p_ref[0, hh, rows, q0:] = jnp.exp(s_scr[hh, rows, q0:] - m_new).astype(BF16)
                    return carry

                lax.fori_loop(0, bk // rs, probs, 0, unroll=True)
                acc_scr[hh, :, q0:] = (acc_scr[hh, :, q0:] * jnp.exp(m_old - m_new)
                                       + _dot(vs, p_ref[0, hh, :, q0:], _TN))

        active = kk <= last_kv(i)
        ahead_by = kk * bk - i * bq
        for lo, hi, q0 in _query_skips(bq):
            @pl.when(active & (ahead_by + bk - 1 > 0) & (ahead_by >= lo) & (ahead_by < hi))
            def _(q0=q0):
                step(True, q0)

        @pl.when(active & jnp.logical_not(ahead_by + bk - 1 > 0))
        def _():
            step(False, 0)

        @pl.when(kk == nk - 1)
        def _():
            a = acc_scr[0]
            b = acc_scr[1]
            la = a[64:65, :]
            lb = b[0:1, :]
            o_ref[...] = jnp.where(lane_q < 64, (a / la).T, (b / lb).T)
            lse_ref[0] = jnp.concatenate([m_scr[0] + jnp.log(la), m_scr[1] + jnp.log(lb)], axis=0)

    kvi = lambda i, kk: jnp.minimum(kk, last_kv(i))
    kv = lambda off: pl.BlockSpec((bk, 128), lambda j, i, kk: (kvi(i, kk), off + j))
    blk = lambda j, i, kk: (j * nq + i) * nk + kvi(i, kk)
    return pl.pallas_call(
        body, name="attn_fwd",
        out_shape=(jax.ShapeDtypeStruct((t, D_ATT), F32), jax.ShapeDtypeStruct((_NPAIR, 2, t), F32),
                   jax.ShapeDtypeStruct((_NPAIR * nq * nk, 2, bk, bq), BF16),
                   jax.ShapeDtypeStruct((_NPAIR * nq * nk, 2, bq), F32)),
        grid=(_NPAIR, nq, nk),
        in_specs=[pl.BlockSpec((bq, 128), lambda j, i, kk: (i, _QB + j)),
                  kv(_KB), kv(_VB),
                  pl.BlockSpec((1, bq, 2), lambda j, i, kk: (j, i, 0)),
                  pl.BlockSpec((1, bk, 2), lambda j, i, kk: (j, kvi(i, kk), 0))],
        out_specs=(pl.BlockSpec((bq, 128), lambda j, i, kk: (i, j)),
                   pl.BlockSpec((1, 2, bq), lambda j, i, kk: (j, 0, i)),
                   pl.BlockSpec((1, 2, bk, bq), lambda j, i, kk: (blk(j, i, kk), 0, 0, 0)),
                   pl.BlockSpec((1, 2, bq), lambda j, i, kk: (blk(j, i, kk), 0, 0))),
        scratch_shapes=[pltpu.VMEM((2, bq, 128), BF16), pltpu.VMEM((2, bk, bq), F32),
                        pltpu.VMEM((2, 1, bq), F32), pltpu.VMEM((2, 128, bq), F32)],
        compiler_params=_cp("parallel", "parallel", "arbitrary"),
    )(proj, proj, proj, c_col, c_col)


def _attn_bwd(proj, c_col, lse_row, dl_row, do, p_blocks, m_run, exchange=None):
    t = proj.shape[0]
    bq, bk = _attn_blocks(t)
    nq, nk = t // bq, t // bk
    rs = 16

    def first_q(kk):
        return (kk * bk) // bq

    def body(q_ref, k_ref, v_ref, cq_ref, ck_ref, lse_ref, dl_ref, do_ref, pblk_ref, mrun_ref,
             dq_ref, dk_ref, dv_ref, dck_ref, dcq_ref,
             qs_scr, doh_scr, ks_scr, dp_scr, p_scr, ds_scr, dq_scr, dk_scr, dv_scr):
        kk = pl.program_id(1)
        i = pl.program_id(2)
        lane_q = _iota((bq, 128), 1)
        lane_k = _iota((bk, 128), 1)
        qrows = pl.ds(pl.multiple_of(i * bq, 128), bq)

        @pl.when(kk == 0)
        def _():
            q = q_ref[...] * _SCALE
            cq = cq_ref[0]
            do_ = do_ref[...]
            for hh in range(2):
                qs_scr[hh, qrows, :] = _q_operand(q, cq[:, hh:hh + 1], lane_q, hh)
                doh_scr[hh, qrows, :] = jnp.where(_head_lanes(lane_q, hh)[0], do_, 0.0).astype(BF16)
                dq_scr[hh, i] = jnp.zeros((128, bq), F32)

        @pl.when(i == 0)
        def _():
            dk_scr[...] = jnp.zeros_like(dk_scr)
            dv_scr[...] = jnp.zeros_like(dv_scr)
            k = k_ref[...]
            ck = ck_ref[0]
            for hh in range(2):
                ks_scr[hh] = _k_operand(k, ck[:, hh:hh + 1], lane_k, hh)

        def step(q0):
            seen = pl.ds(pl.multiple_of(i * bq + q0, 128), bq - q0)
            v16 = v_ref[...].astype(BF16)
            dl = dl_ref[0, :, q0:]
            rescale = jnp.exp(mrun_ref[0, :, q0:] - lse_ref[0, :, q0:])
            for hh in range(2):
                dp_scr[hh, :, q0:] = _dot(v16, doh_scr[hh, seen, :], _NT)
            for hh in range(2):
                qs = qs_scr[hh, seen, :]
                doh = doh_scr[hh, seen, :]

                def strip(r, carry):
                    rows = pl.ds(pl.multiple_of(r * rs, rs), rs)
                    p = pblk_ref[0, hh, rows, q0:].astype(F32) * rescale[hh:hh + 1, :]
                    p_scr[hh, rows, q0:] = p.astype(BF16)
                    ds_scr[hh, rows, q0:] = (p * (dp_scr[hh, rows, q0:] - dl[hh:hh + 1, :])).astype(BF16)
                    return carry

                lax.fori_loop(0, bk // rs, strip, 0, unroll=True)
                dv_scr[...] += _dot(p_scr[hh, :, q0:], doh)
                dk_scr[hh] += _dot(ds_scr[hh, :, q0:], qs)
                dq_scr[hh, i, :, q0:] += _dot(ks_scr[hh], ds_scr[hh, :, q0:], _TN)

        ahead_by = kk * bk - i * bq
        for lo, hi, q0 in _query_skips(bq):
            @pl.when((i >= first_q(kk)) & (ahead_by >= lo) & (ahead_by < hi))
            def _(q0=q0):
                step(q0)

        @pl.when(i == nq - 1)
        def _():
            dka = dk_scr[0]
            dkb = dk_scr[1]
            dk_ref[...] = jnp.where(lane_k < 64, dka, dkb).astype(BF16)
            dv_ref[...] = dv_scr[...].astype(BF16)
            dck_ref[0] = -jnp.where(_iota((bk, 2), 1) == 0, dka[:, 67:68], dkb[:, 3:4])

        @pl.when((kk == nk - 1) & (i == nq - 1))
        def _():
            for ii in range(nq):
                cols = slice(ii * bq, (ii + 1) * bq)
                dqa = dq_scr[0, ii]
                dqb = dq_scr[1, ii]
                dq_ref[cols, :] = (jnp.where(lane_q < 64, dqa.T, dqb.T) * _SCALE).astype(BF16)
                dcq_ref[0, :, cols] = jnp.concatenate([dqa[64:65, :], dqb[0:1, :]], axis=0)

    qi = lambda kk, i: jnp.where(kk == 0, i, nq - 1)
    qspec = lambda off: pl.BlockSpec((bq, 128), lambda j, kk, i: (qi(kk, i), off + j))
    kspec = lambda off: pl.BlockSpec((bk, 128), lambda j, kk, i: (kk, off + j))
    rowspec = pl.BlockSpec((1, 2, bq), lambda j, kk, i: (j, 0, jnp.maximum(i, first_q(kk))))
    blk = lambda j, kk, i: (j * nq + jnp.maximum(i, first_q(kk))) * nk + kk
    return _hosted_call(
        body, name="attn_bwd",
        out_shape=(jax.ShapeDtypeStruct((t, D_ATT), BF16), jax.ShapeDtypeStruct((t, D_ATT), BF16),
                   jax.ShapeDtypeStruct((t, D_ATT), BF16), jax.ShapeDtypeStruct((_NPAIR, t, 2), F32),
                   jax.ShapeDtypeStruct((_NPAIR, 2, t), F32)),
        grid=(_NPAIR, nk, nq),
        in_specs=[qspec(_QB), kspec(_KB), kspec(_VB),
                  pl.BlockSpec((1, bq, 2), lambda j, kk, i: (j, qi(kk, i), 0)),
                  pl.BlockSpec((1, bk, 2), lambda j, kk, i: (j, kk, 0)),
                  rowspec, rowspec, qspec(0),
                  pl.BlockSpec((1, 2, bk, bq), lambda j, kk, i: (blk(j, kk, i), 0, 0, 0)),
                  pl.BlockSpec((1, 2, bq), lambda j, kk, i: (blk(j, kk, i), 0, 0))],
        out_specs=(pl.BlockSpec((t, 128), lambda j, kk, i: (0, j)),
                   pl.BlockSpec((bk, 128), lambda j, kk, i: (kk, j)),
                   pl.BlockSpec((bk, 128), lambda j, kk, i: (kk, j)),
                   pl.BlockSpec((1, bk, 2), lambda j, kk, i: (j, kk, 0)),
                   pl.BlockSpec((1, 2, t), lambda j, kk, i: (j, 0, 0))),
        scratch_shapes=[pltpu.VMEM((2, t, 128), BF16), pltpu.VMEM((2, t, 128), BF16), pltpu.VMEM((2, bk, 128), BF16),
                        pltpu.VMEM((2, bk, bq), F32),
                        pltpu.VMEM((2, bk, bq), BF16), pltpu.VMEM((2, bk, bq), BF16),
                        pltpu.VMEM((2, nq, 128, bq), F32), pltpu.VMEM((2, bk, 128), F32), pltpu.VMEM((bk, 128), F32)],
        operands=(proj, proj, proj, c_col, c_col, lse_row, dl_row, do, p_blocks, m_run),
        semantics=("parallel", "arbitrary", "arbitrary"), exchange=exchange)


def _premerge_fwd(y, o, proj, gamma):
    t = y.shape[0]
    tm = _row_tile_wide(t)

    def body(y_ref, z_ref, o_ref, za_ref, g_ref, ys_ref, ya_ref):
        z = z_ref[...]
        u = y_ref[...] * (z * _sigmoid(z))
        for g in range(G_SSD):
            gs = slice(_GW * g, _GW * (g + 1))
            ug = u[:, gs]
            r = lax.rsqrt(jnp.mean(ug * ug, axis=-1, keepdims=True) + EPS)
            ys_ref[:, gs] = (ug * r * g_ref[:, gs]).astype(BF16)
        za = za_ref[...]
        ya_ref[...] = (o_ref[...] * (za * _sigmoid(za))).astype(BF16)

    return pl.pallas_call(
        body, name="premerge_fwd",
        out_shape=(jax.ShapeDtypeStruct((t, D_SSD), BF16), jax.ShapeDtypeStruct((t, D_ATT), BF16)),
        grid=(t // tm,),
        in_specs=[pl.BlockSpec((tm, D_SSD), lambda i: (i, 0)),
                  pl.BlockSpec((tm, D_SSD), lambda i: (i, C_Z // D_SSD)),
                  pl.BlockSpec((tm, D_ATT), lambda i: (i, 0)),
                  pl.BlockSpec((tm, D_ATT), lambda i: (i, C_ZA // D_ATT)),
                  pl.BlockSpec((1, D_SSD), lambda i: (0, 0))],
        out_specs=(pl.BlockSpec((tm, D_SSD), lambda i: (i, 0)), pl.BlockSpec((tm, D_ATT), lambda i: (i, 0))),
        compiler_params=_cp("parallel"),
    )(y, proj, o, proj, gamma)


def _premerge_bwd(dys, dya, y, o, proj, gamma, exchange=None):
    t = y.shape[0]
    tm = _row_tile_wide(t)

    def body(dys_ref, dya_ref, y_ref, z_ref, o_ref, za_ref, g_ref, dy_ref, dz_ref, do_ref, dza_ref, dg_ref, dl_ref):
        i = pl.program_id(0)
        z = z_ref[...]
        sz = _sigmoid(z)
        silu = z * sz
        dsilu = sz * (1.0 + z * (1.0 - sz))
        yv = y_ref[...]
        u = yv * silu
        parts = []
        for g in range(G_SSD):
            gs = slice(_GW * g, _GW * (g + 1))
            ug = u[:, gs]
            r = lax.rsqrt(jnp.mean(ug * ug, axis=-1, keepdims=True) + EPS)
            n = ug * r
            dout = dys_ref[:, gs]
            dn = dout * g_ref[:, gs]
            du = r * (dn - n * jnp.mean(dn * n, axis=-1, keepdims=True))
            dy_ref[:, gs] = du * silu[:, gs]
            dz_ref[:, gs] = (du * yv[:, gs] * dsilu[:, gs]).astype(BF16)
            parts.append(jnp.sum(dout * n, axis=0, keepdims=True))
        dg = jnp.concatenate(parts, axis=1)
        za = za_ref[...]
        sa = _sigmoid(za)
        dya_ = dya_ref[...]
        ov = o_ref[...]
        do = dya_ * (za * sa)
        do_ref[...] = do
        dza_ref[...] = (dya_ * ov * (sa * (1.0 + za * (1.0 - sa)))).astype(BF16)
        pick = (jnp.right_shift(_iota((D_ATT, 128), 0), 6) == _iota((D_ATT, 128), 1)).astype(F32)
        dl_ref[...] = _dot_exact(do * ov, pick)

        @pl.when(i == 0)
        def _():
            dg_ref[...] = dg

        @pl.when(i > 0)
        def _():
            dg_ref[...] += dg

    ssd = pl.BlockSpec((tm, D_SSD), lambda i: (i, 0))
    att = pl.BlockSpec((tm, D_ATT), lambda i: (i, 0))
    vec = pl.BlockSpec((1, D_SSD), lambda i: (0, 0))
    return _hosted_call(
        body, name="premerge_bwd",
        out_shape=(jax.ShapeDtypeStruct((t, D_SSD), F32), jax.ShapeDtypeStruct((t, D_SSD), BF16),
                   jax.ShapeDtypeStruct((t, D_ATT), F32), jax.ShapeDtypeStruct((t, D_ATT), BF16),
                   jax.ShapeDtypeStruct((1, D_SSD), F32), jax.ShapeDtypeStruct((t, 128), F32)),
        grid=(t // tm,),
        in_specs=[ssd, att, ssd, pl.BlockSpec((tm, D_SSD), lambda i: (i, C_Z // D_SSD)), att,
                  pl.BlockSpec((tm, D_ATT), lambda i: (i, C_ZA // D_ATT)), vec],
        out_specs=(ssd, ssd, att, att, vec, pl.BlockSpec((tm, 128), lambda i: (i, 0))),
        scratch_shapes=[],
        operands=(dys, dya, y, proj, o, proj, gamma), semantics=("arbitrary",), exchange=exchange)


_G_BLK = C_G // D_MODEL


def _merge_fwd(a, b, proj, gate_bias):
    t = a.shape[0]
    tm = _row_tile(t)

    def body(a_ref, b_ref, gs_ref, ga_ref, bias_ref, m_ref):
        g_ssd = _sigmoid(gs_ref[...] + bias_ref[:, 0:D_MODEL])
        g_att = _sigmoid(ga_ref[...] + bias_ref[:, D_MODEL:2 * D_MODEL])
        m_ref[...] = (g_ssd * a_ref[...] + g_att * b_ref[...]).astype(BF16)

    row = pl.BlockSpec((tm, D_MODEL), lambda i: (i, 0))
    return pl.pallas_call(
        body, name="merge_fwd",
        out_shape=jax.ShapeDtypeStruct((t, D_MODEL), BF16),
        grid=(t // tm,),
        in_specs=[row, row,
                  pl.BlockSpec((tm, D_MODEL), lambda i: (i, _G_BLK)),
                  pl.BlockSpec((tm, D_MODEL), lambda i: (i, _G_BLK + 1)),
                  pl.BlockSpec((1, 2 * D_MODEL), lambda i: (0, 0))],
        out_specs=row,
        compiler_params=_cp("parallel"),
    )(a, b, proj, proj, gate_bias)


def _merge_bwd(dm, a, b, proj, gate_bias):
    t = a.shape[0]
    tm = _row_tile(t)

    def body(dm_ref, a_ref, b_ref, gs_ref, ga_ref, bias_ref, da_ref, db_ref, dg_ref, dbias_ref):
        i = pl.program_id(0)
        dm_ = dm_ref[...]
        g_ssd = _sigmoid(gs_ref[...] + bias_ref[:, 0:D_MODEL])
        g_att = _sigmoid(ga_ref[...] + bias_ref[:, D_MODEL:2 * D_MODEL])
        da_ref[...] = (dm_ * g_ssd).astype(BF16)
        db_ref[...] = (dm_ * g_att).astype(BF16)
        dgs = dm_ * a_ref[...] * g_ssd * (1.0 - g_ssd)
        dga = dm_ * b_ref[...] * g_att * (1.0 - g_att)
        dg_ref[:, 0:D_MODEL] = dgs.astype(BF16)
        dg_ref[:, D_MODEL:2 * D_MODEL] = dga.astype(BF16)
        part = jnp.concatenate([jnp.sum(dgs, axis=0, keepdims=True), jnp.sum(dga, axis=0, keepdims=True)], axis=1)

        @pl.when(i == 0)
        def _():
            dbias_ref[...] = part

        @pl.when(i > 0)
        def _():
            dbias_ref[...] += part

    row = pl.BlockSpec((tm, D_MODEL), lambda i: (i, 0))
    wide = pl.BlockSpec((tm, 2 * D_MODEL), lambda i: (i, 0))
    vec = pl.BlockSpec((1, 2 * D_MODEL), lambda i: (0, 0))
    return pl.pallas_call(
        body, name="merge_bwd",
        out_shape=(jax.ShapeDtypeStruct((t, D_MODEL), BF16), jax.ShapeDtypeStruct((t, D_MODEL), BF16),
                   jax.ShapeDtypeStruct((t, 2 * D_MODEL), BF16), jax.ShapeDtypeStruct((1, 2 * D_MODEL), F32)),
        grid=(t // tm,),
        in_specs=[row, row, row,
                  pl.BlockSpec((tm, D_MODEL), lambda i: (i, _G_BLK)),
                  pl.BlockSpec((tm, D_MODEL), lambda i: (i, _G_BLK + 1)), vec],
        out_specs=(row, row, wide, vec),
        compiler_params=_cp("arbitrary"),
    )(dm, a, b, proj, proj, gate_bias)


def _post(o2, h, target, g):
    t = o2.shape[0]
    nc = t // CHUNK

    def body(o_ref, h_ref, t_ref, g_ref, dy_ref, do_ref, dg_ref, loss_ref):
        c = pl.program_id(0)
        x = o_ref[...]
        r = lax.rsqrt(jnp.mean(x * x, axis=-1, keepdims=True) + EPS)
        n = x * r
        y = h_ref[...] + n * g_ref[...]
        diff = jnp.where(c > 0, y - t_ref[...], 0.0)
        dy = diff * (1.0 / D_MODEL)
        dy_ref[...] = dy
        gdy = dy * g_ref[...]
        do_ref[...] = (r * (gdy - n * jnp.mean(gdy * n, axis=-1, keepdims=True))).astype(BF16)
        dg = jnp.sum(dy * n, axis=0, keepdims=True)
        lpart = 0.5 * jnp.sum(jnp.sum(diff * diff, axis=1, keepdims=True), axis=0, keepdims=True) * (1.0 / D_MODEL)
        sel = (_iota((8, 128), 0) == 0) & (_iota((8, 128), 1) == 0)

        @pl.when(c == 0)
        def _():
            dg_ref[...] = dg
            loss_ref[...] = jnp.zeros_like(loss_ref)

        @pl.when(c > 0)
        def _():
            dg_ref[...] += dg
            loss_ref[...] += jnp.where(sel, lpart, 0.0)

    row = pl.BlockSpec((CHUNK, D_MODEL), lambda c: (c, 0))
    vec = pl.BlockSpec((1, D_MODEL), lambda c: (0, 0))
    return pl.pallas_call(
        body, name="post",
        out_shape=(jax.ShapeDtypeStruct((t, D_MODEL), F32), jax.ShapeDtypeStruct((t, D_MODEL), BF16),
                   jax.ShapeDtypeStruct((1, D_MODEL), F32), jax.ShapeDtypeStruct((8, 128), F32)),
        grid=(nc,),
        in_specs=[row, row, pl.BlockSpec((CHUNK, D_MODEL), lambda c: (jnp.maximum(c - 1, 0), 0)), vec],
        out_specs=(row, row, vec, pl.BlockSpec((8, 128), lambda c: (0, 0))),
        compiler_params=_cp("arbitrary"),
    )(o2, h, target, g)


def _mm_tiles(t):
    return _tile(t, (704, 384, 128))


def _local_step(h, target, w_main, w_small, pr_slots, ids, norm_pre, conv_w, conv_b, bias_row, a_row,
                dsk_row, ssd_norm, gate_bias, norm_post):
    t = h.shape[0]
    tm = _mm_tiles(t)
    u = _norm1_fwd(h, norm_pre)
    proj, pr_slots = _matmul(u, w_main, "nt", F32, "inproj", tm, 1024, D_MODEL,
                             exchange=_gather_stage([pr_slots], to_sibling=False))
    small, pr_slots = _matmul(u, w_small, "nt", F32, "inproj_small", tm, N_SMALL, D_MODEL,
                              exchange=_gather_stage([pr_slots], to_sibling=True))
    wps = pr_slots[:, 0:512].reshape(D_SSD, D_MODEL)
    wpa = pr_slots[:, 512:768].reshape(D_ATT, D_MODEL)
    wout = pr_slots[:, 768:1024].reshape(D_MODEL, D_MODEL)
    dtlf = _small_fwd(small, bias_row)
    xbc = _conv_fwd(proj, conv_w, conv_b)
    y, hin = _ssd_fwd(xbc, dtlf, a_row, dsk_row)
    c_tok = dtlf[:, H_SSD:H_SSD + H_ATT]
    c_tok = jnp.where(jnp.arange(t)[:, None] < PADF, _C_FILLER, c_tok)
    c_col = c_tok.reshape(t, _NPAIR, 2).transpose(1, 0, 2)
    o, lse, p_blocks, m_run = _attn_fwd(proj, c_col)
    ys, ya = _premerge_fwd(y, o, proj, ssd_norm)
    a = _matmul(ys, wps, "nn", F32, "proj_ssd", tm, D_MODEL, D_SSD)
    b = _matmul(ya, wpa, "nn", F32, "proj_att", tm, D_MODEL, D_ATT)
    merged = _merge_fwd(a, b, proj, gate_bias)
    o2 = _matmul(merged, wout, "nn", F32, "out_proj", tm, D_MODEL, D_MODEL)
    dy_out, do2, d_norm_post, loss_blk = _post(o2, h, target, norm_post)

    dm = _matmul(do2, wout, "nt", F32, "out_proj_dx", tm, D_MODEL, D_MODEL)
    d_wout = _matmul(merged, do2, "tn", F32, "out_proj_dw", D_MODEL, D_MODEL, tm)
    da, db, dgraw, d_gate_bias = _merge_bwd(dm, a, b, proj, gate_bias)
    dys = _matmul(da, wps, "nt", F32, "proj_ssd_dx", tm, D_SSD, D_MODEL)
    d_wps = _matmul(ys, da, "tn", F32, "proj_ssd_dw", D_SSD, D_MODEL, tm)
    dya = _matmul(db, wpa, "nt", F32, "proj_att_dx", tm, D_ATT, D_MODEL)
    d_wpa = _matmul(ya, db, "tn", F32, "proj_att_dw", D_ATT, D_MODEL, tm)
    g32_pr = jnp.concatenate([d_wps.reshape(4, 512, D_MODEL), d_wpa.reshape(4, 256, D_MODEL),
                              d_wout.reshape(4, 256, D_MODEL)], axis=1)
    dy, dz, do, dza, d_ssd_norm, dl, ra_pr = _premerge_bwd(dys, dya, y, o, proj, ssd_norm,
                                                           exchange=_pair_swap([g32_pr]))
    pb_pr = _add_pair(ids, g32_pr, ra_pr)
    dl_row = dl[:, 0:H_ATT].T.reshape(_NPAIR, 2, t)
    dq, dk, dv, dc_key, dc_qry, rb_pr = _attn_bwd(proj, c_col, lse, dl_row, do, p_blocks, m_run,
                                                  exchange=_chip_exchange([pb_pr]))
    half_pr = _add_chips(ids, g32_pr, ra_pr, rb_pr)
    dxbc, ddt, d_a, d_dsk = _ssd_bwd(xbc, dtlf, a_row, dsk_row, hin, dy)
    dxbc_raw, d_conv_w, d_conv_b = _conv_bwd(dxbc, proj, conv_w, conv_b)
    dc_tok = jnp.transpose(dc_key, (1, 0, 2)).reshape(t, H_ATT) + dc_qry.reshape(H_ATT, t).T
    dsm = ddt + jnp.pad(dc_tok, ((0, 0), (H_SSD, N_SMALL - H_SSD - H_ATT)))
    dsmall, d_bias_row = _small_bwd(dsm, small, bias_row)
    dproj = [dz, dxbc_raw, dza, dq, dk, dv, dgraw]
    return dict(loss_blk=loss_blk, u=u, dy_out=dy_out, dproj=dproj, dsmall=dsmall, half_pr=half_pr,
                d_conv_w=d_conv_w, d_conv_b=d_conv_b,
                d_bias_row=d_bias_row, d_a=d_a, d_dsk=d_dsk, d_ssd_norm=d_ssd_norm,
                d_gate_bias=d_gate_bias, d_norm_post=d_norm_post)


def _to_aligned_rows(slots):
    w = slots.reshape(N_COLS, slots.shape[2])

    def cut(o):
        return w[o[0]:o[0] + o[1]]
    main = jnp.concatenate([cut(O_Z), cut(O_XBC), cut(O_ZA), cut(O_Q), cut(O_K), cut(O_V), cut(O_G)], axis=0)
    pad = jnp.zeros((N_SMALL - H_SSD - H_ATT, w.shape[1]), w.dtype)
    small = jnp.concatenate([cut(O_DT), cut(O_F), pad], axis=0)
    return main, small


def _from_aligned_rows(main, small):
    def cm(c0, n):
        return main[c0:c0 + n]
    flat = jnp.concatenate([cm(C_Z, 2048), cm(C_XBC, 3072), small[0:H_SSD], cm(C_ZA, 1024),
                            cm(C_Q, 1024), cm(C_K, 1024), cm(C_V, 1024), small[H_SSD:H_SSD + H_ATT],
                            cm(C_G, 2048)], axis=0)
    return flat.reshape(4, N_COLS // 4, flat.shape[1])


_MESH = pl.DeviceIdType.MESH
_ANY = pl.BlockSpec(memory_space=pl.ANY)
_VM = pl.BlockSpec(memory_space=pltpu.VMEM)
_HALF = 512
N_DEV = 8


def _coords():
    return lax.axis_index("x"), lax.axis_index("y"), lax.axis_index("c")


def _other_chips(x, y):
    return [(1 - x, y), (x, 1 - y), (1 - x, 1 - y)]


def _half(cc):
    return pl.ds(cc * _HALF, _HALF)


def _gather_shards(slots):
    n = len(slots)

    def body(*refs):
        buf = refs[n:2 * n]
        send_sems, recv_sems = refs[2 * n:]
        x, y, c = _coords()
        chip = 2 * x + y
        sibling = (x, y, 1 - c)
        chips = _other_chips(x, y)

        def copy(i, frm, cc, k, to):
            part = buf[i].at[frm, :, _half(cc)]
            return pltpu.make_async_remote_copy(src_ref=part, dst_ref=part, send_sem=send_sems.at[6 * i + k],
                                                recv_sem=recv_sems.at[6 * i + k], device_id=to, device_id_type=_MESH)

        def chip_of(k):
            return 2 * chips[k][0] + chips[k][1]

        first = [copy(i, chip, c, k, (*chips[k], c)) for k in range(3) for i in range(n)]
        for cp in first:
            cp.start()
        passed = []
        for k in range(3):
            for i in range(n):
                copy(i, chip_of(k), c, k, (*chips[k], c)).wait_recv()
                passed.append(copy(i, chip_of(k), c, 3 + k, sibling))
                passed[-1].start()
        for k in range(3):
            for i in range(n):
                copy(i, chip_of(k), 1 - c, 3 + k, sibling).wait_recv()
        for cp in first + passed:
            cp.wait_send()

    return pl.pallas_call(
        body, name="gather_shards",
        out_shape=tuple(jax.ShapeDtypeStruct(s.shape, s.dtype) for s in slots),
        in_specs=[_ANY] * n, out_specs=tuple([_ANY] * n),
        input_output_aliases={i: i for i in range(n)},
        scratch_shapes=[pltpu.SemaphoreType.DMA((6 * n,)), pltpu.SemaphoreType.DMA((6 * n,))],
    )(*slots)


def _allgather8(block, name):
    rows, width = block.shape

    def body(x_ref, out_ref, send_sems, recv_sems, local_sem):
        x, y, c = _coords()
        me, sibling = (x, y, c), (x, y, 1 - c)
        chips = _other_chips(x, y)

        def slot(px, py, pc):
            return out_ref.at[4 * px + 2 * py + pc]

        def copy(k, blk, to, src=None):
            return pltpu.make_async_remote_copy(src_ref=slot(*blk) if src is None else src, dst_ref=slot(*blk),
                                                send_sem=send_sems.at[k], recv_sem=recv_sems.at[k],
                                                device_id=to, device_id_type=_MESH)

        mine = pltpu.make_async_copy(x_ref, slot(*me), local_sem)
        mine.start()
        first = [copy(0, me, sibling, src=x_ref)]
        first += [copy(1 + j, me, (*chip, c), src=x_ref) for j, chip in enumerate(chips)]
        for cp in first:
            cp.start()
        passed = [copy(4 + j, (*chip, c), sibling) for j, chip in enumerate(chips)]
        for j, chip in enumerate(chips):
            copy(1 + j, (*chip, c), me).wait_recv()
            passed[j].start()
        copy(0, sibling, me).wait_recv()
        for j, chip in enumerate(chips):
            copy(4 + j, (*chip, 1 - c), me).wait_recv()
        for cp in first + passed:
            cp.wait_send()
        mine.wait()

    return pl.pallas_call(
        body, name=name,
        out_shape=jax.ShapeDtypeStruct((N_DEV, rows, width), block.dtype),
        in_specs=[_VM], out_specs=_VM,
        scratch_shapes=[pltpu.SemaphoreType.DMA((7,)), pltpu.SemaphoreType.DMA((7,)), pltpu.SemaphoreType.DMA],
    )(block)


def _pair_swap(arrs):
    def copies(src, dst, send_sems, recv_sems):
        x, y, c = _coords()
        return [pltpu.make_async_remote_copy(src_ref=src[i].at[:, :, _half(1 - c)], dst_ref=dst[i],
                                             send_sem=send_sems.at[i], recv_sem=recv_sems.at[i],
                                             device_id=(x, y, 1 - c), device_id_type=_MESH) for i in range(len(src))]

    shapes = tuple(jax.ShapeDtypeStruct((4, a.shape[1], _HALF), a.dtype) for a in arrs)
    return tuple(arrs), shapes, copies, len(arrs), False


def _chip_exchange(arrs):
    def copies(src, dst, send_sems, recv_sems):
        x, y, c = _coords()
        chips = _other_chips(x, y)
        return [pltpu.make_async_remote_copy(src_ref=src[i].at[2 * chips[k][0] + chips[k][1]], dst_ref=dst[i].at[k],
                                             send_sem=send_sems.at[3 * i + k], recv_sem=recv_sems.at[3 * i + k],
                                             device_id=(*chips[k], c), device_id_type=_MESH)
                for k in range(3) for i in range(len(src))]

    shapes = tuple(jax.ShapeDtypeStruct((3,) + a.shape[1:], a.dtype) for a in arrs)
    return tuple(arrs), shapes, copies, 3 * len(arrs), False


def _gather_stage(slots, to_sibling):
    def copies(buf, _, send_sems, recv_sems):
        x, y, c = _coords()
        chips = _other_chips(x, y)
        out = []
        for k in range(3):
            for i in range(len(buf)):
                frm = 2 * chips[k][0] + chips[k][1] if to_sibling else 2 * x + y
                part = buf[i].at[frm, :, _half(c)]
                out.append(pltpu.make_async_remote_copy(
                    src_ref=part, dst_ref=part, send_sem=send_sems.at[3 * i + k], recv_sem=recv_sems.at[3 * i + k],
                    device_id=(x, y, 1 - c) if to_sibling else (*chips[k], c), device_id_type=_MESH))
        return out

    shapes = tuple(jax.ShapeDtypeStruct(s.shape, s.dtype) for s in slots)
    return tuple(slots), shapes, copies, 3 * len(slots), True


def _pair_join_halves(fulls):
    n = len(fulls)

    def body(*refs):
        buf = refs[n:2 * n]
        send_sems, recv_sems = refs[2 * n:]
        x, y, c = _coords()

        def remote(i, cc):
            part = buf[i].at[:, _half(cc)]
            return pltpu.make_async_remote_copy(src_ref=part, dst_ref=part, send_sem=send_sems.at[i],
                                                recv_sem=recv_sems.at[i], device_id=(x, y, 1 - c), device_id_type=_MESH)

        for i in range(n):
            remote(i, c).start()
        for i in range(n):
            remote(i, c).wait_send()
            remote(i, 1 - c).wait_recv()

    return pl.pallas_call(
        body, name="pair_join_halves",
        out_shape=tuple(jax.ShapeDtypeStruct(a.shape, a.dtype) for a in fulls),
        in_specs=[_ANY] * n, out_specs=tuple([_ANY] * n),
        input_output_aliases={i: i for i in range(n)},
        scratch_shapes=[pltpu.SemaphoreType.DMA((n,)), pltpu.SemaphoreType.DMA((n,))],
    )(*fulls)


_RED_TC = 128
_RED_NT = _HALF // _RED_TC


def _add_pair(ids, g32, recv_a):
    rows = g32.shape[1]

    def body(ids_ref, g_ref, r_ref, o_ref):
        o_ref[...] = (g_ref[...] + r_ref[...]).astype(BF16)

    blk = pl.BlockSpec((1, rows, _RED_TC), lambda j, l, ids: (j, 0, l))
    return pl.pallas_call(
        body, name="add_pair",
        out_shape=jax.ShapeDtypeStruct((4, rows, _HALF), BF16),
        grid_spec=pltpu.PrefetchScalarGridSpec(
            num_scalar_prefetch=1, grid=(4, _RED_NT),
            in_specs=[pl.BlockSpec((1, rows, _RED_TC), lambda j, l, ids: (j, 0, ids[0] * _RED_NT + l)), blk],
            out_specs=blk),
        compiler_params=_cp("parallel", "parallel"),
    )(ids, g32, recv_a)


def _add_chips(ids, g32, recv_a, recv_b):
    rows = g32.shape[1]

    def body(ids_ref, g_ref, a_ref, b_ref, o_ref):
        acc = g_ref[0] + a_ref[0]
        for k in range(3):
            acc = acc + b_ref[k].astype(F32)
        o_ref[...] = acc

    return pl.pallas_call(
        body, name="add_chips",
        out_shape=jax.ShapeDtypeStruct((rows, 2 * _HALF), F32),
        grid_spec=pltpu.PrefetchScalarGridSpec(
            num_scalar_prefetch=1, grid=(_RED_NT,),
            in_specs=[pl.BlockSpec((1, rows, _RED_TC), lambda l, ids: (ids[1], 0, ids[0] * _RED_NT + l)),
                      pl.BlockSpec((1, rows, _RED_TC), lambda l, ids: (ids[1], 0, l)),
                      pl.BlockSpec((3, rows, _RED_TC), lambda l, ids: (0, 0, l))],
            out_specs=pl.BlockSpec((rows, _RED_TC), lambda l, ids: (0, ids[0] * _RED_NT + l))),
        compiler_params=_cp("parallel"),
    )(ids, g32, recv_a, recv_b)


def _sum8(gathered):
    _, rows, width = gathered.shape

    def body(g_ref, o_ref):
        acc = g_ref[0]
        for d in range(1, N_DEV):
            acc = acc + g_ref[d]
        o_ref[...] = acc

    return pl.pallas_call(
        body, name="sum8",
        out_shape=jax.ShapeDtypeStruct((rows, width), F32),
        in_specs=[_VM], out_specs=_VM,
    )(gathered)


def _adamw(w, g, m, v, name):
    rows, cols = w.shape
    budget = (3 << 20) // 2
    tr, tc = rows, cols
    if rows * cols * 4 > budget:
        if rows % 8 == 0:
            tr = max(c for c in range(8, rows, 8) if rows % c == 0 and c * cols * 4 <= budget)
        else:
            tc = next(c for c in (512, 256, 128) if cols % c == 0 and rows * c * 4 <= budget)
    c1 = 1.0 - ADAM_B1 ** ADAM_STEP
    c2 = 1.0 - ADAM_B2 ** ADAM_STEP

    def body(w_ref, g_ref, m_ref, v_ref, d_ref, mo_ref, vo_ref):
        gg = g_ref[...]
        mn = ADAM_B1 * m_ref[...] + (1.0 - ADAM_B1) * gg
        vn = ADAM_B2 * v_ref[...] + (1.0 - ADAM_B2) * (gg * gg)
        mo_ref[...] = mn
        vo_ref[...] = vn
        d_ref[...] = -ADAM_LR * ((mn / c1) / (jnp.sqrt(vn / c2) + ADAM_EPS) + ADAM_WD * w_ref[...])

    blk = pl.BlockSpec((tr, tc), lambda i, j: (i, j))
    shp = jax.ShapeDtypeStruct((rows, cols), F32)
    return pl.pallas_call(
        body, name=name, out_shape=(shp, shp, shp), grid=(rows // tr, cols // tc),
        in_specs=[blk] * 4, out_specs=(blk, blk, blk),
        compiler_params=_cp("parallel", "parallel"),
    )(w, g, m, v)


def _rows128(a):
    return a.reshape(-1, 128)


def _pack_small(norm_pre, conv_b, ssd_norm, gate_bias, norm_post, dt_bias, a_log, d_skip, fgate_bias):
    tiny = jnp.concatenate([dt_bias.reshape(-1), a_log.reshape(-1), d_skip.reshape(-1), fgate_bias.reshape(-1),
                            jnp.zeros((16,), F32)])
    return jnp.concatenate([_rows128(norm_pre), _rows128(conv_b), _rows128(ssd_norm), _rows128(gate_bias),
                            _rows128(norm_post), tiny.reshape(1, 128)], axis=0)


_SMALL_ROWS = 73
_SMALL_PAD = 80


def _unpack_small(p):
    tiny = p[72]
    return dict(norm_pre=p[0:8].reshape(1, 1024), conv_b=p[8:32].reshape(1, 3072), ssd_norm=p[32:48].reshape(1, 2048),
                gate_bias=p[48:64].reshape(1, 2048), norm_post=p[64:72].reshape(1, 1024),
                dt_bias=tiny[0:32].reshape(1, 32), a_log=tiny[32:64].reshape(1, 32),
                d_skip=tiny[64:96].reshape(1, 32), fgate_bias=tiny[96:112].reshape(1, 16))


def _pad_rows(a, rows):
    return jnp.concatenate([a, jnp.zeros((rows - a.shape[0], a.shape[1]), a.dtype)], axis=0)


def kernel(x, meta_tokens, norm_pre, w_in, conv_w, conv_b, dt_bias, a_log, d_skip, ssd_norm, fgate_bias, gate_bias, w_proj_ssd, w_proj_att, w_out, norm_post, loss_target, m_meta_tokens, m_norm_pre, m_w_in, m_conv_w, m_conv_b, m_dt_bias, m_a_log, m_d_skip, m_ssd_norm, m_fgate_bias, m_gate_bias, m_w_proj_ssd, m_w_proj_att, m_w_out, m_norm_post, v_meta_tokens, v_norm_pre, v_w_in, v_conv_w, v_conv_b, v_dt_bias, v_a_log, v_d_skip, v_ssd_norm, v_fgate_bias, v_gate_bias, v_w_proj_ssd, v_w_proj_att, v_w_out, v_norm_post):
    cx, cy, cc = _coords()
    chip = 2 * cx + cy
    ids = jnp.stack([cc, chip]).astype(jnp.int32)
    seq = x.shape[1]

    w_in_sh = jnp.transpose(w_in[0]).astype(BF16)
    w_pr_sh = jnp.concatenate([w_proj_ssd[0], w_proj_att[0], w_out[0]], axis=0).astype(BF16)

    def own_slot(sh):
        return lax.dynamic_update_slice(lax.empty((4,) + sh.shape, sh.dtype), sh[None], (chip, 0, 0))

    (g_in,) = _gather_shards([own_slot(w_in_sh)])
    w_main, w_small = _to_aligned_rows(g_in)
    sm_sh = jnp.concatenate([_rows128(meta_tokens), _rows128(conv_w[0])], axis=0)
    sm_all = _allgather8(sm_sh, "gather_small_weights")[0::2]
    meta_full = jnp.transpose(sm_all[:, 0:32].reshape(4, N_META, 256), (1, 0, 2)).reshape(N_META, D_MODEL)
    conv_w_full = jnp.transpose(sm_all[:, 32:56].reshape(4, CONV_K, 768), (1, 0, 2)).reshape(CONV_K, CONV_DIM)

    h = jnp.concatenate([jnp.zeros((PADF, D_MODEL), F32), meta_full, x[0]], axis=0)
    bias_row = jnp.concatenate([dt_bias[0], fgate_bias[0], jnp.zeros((N_SMALL - H_SSD - H_ATT,), F32)]).reshape(1, N_SMALL)
    a_neg = -jnp.exp(a_log[0])
    a_row = jnp.concatenate([a_neg, jnp.zeros((N_SMALL - H_SSD,), F32)]).reshape(1, N_SMALL)
    dsk_row = jnp.repeat(d_skip[0], 64).reshape(1, D_SSD)
    r = _local_step(h, loss_target[0], w_main, w_small, own_slot(w_pr_sh), ids, norm_pre, conv_w_full, conv_b,
                    bias_row, a_row, dsk_row, ssd_norm, gate_bias, norm_post)

    tm = _mm_tiles(h.shape[0])
    n_row_tiles = h.shape[0] // tm
    d_w_main = _matmul_cat_tn(r["dproj"], r["u"], "inproj_dw", tm)
    d_w_small = _matmul(r["dsmall"], r["u"], "tn", F32, "inproj_small_dw", N_SMALL, D_MODEL, tm)
    g32_in = _from_aligned_rows(d_w_main, d_w_small)
    first = max(n_row_tiles // 6, 1)
    du_first, ra_in = _matmul_cat_nn(r["dproj"], w_main, "inproj_dx_swap", tm, rows=(0, first),
                                     exchange=_pair_swap([g32_in]))
    pb_in = _add_pair(ids, g32_in, ra_in)
    du_a, rb_in = _matmul_cat_nn(r["dproj"], w_main, "inproj_dx_exchange", tm,
                                 rows=(first, n_row_tiles - first), fill=du_first,
                                 exchange=_chip_exchange([pb_in]))
    du_b = _matmul(r["dsmall"], w_small, "nn", F32, "inproj_small_dx", tm, D_MODEL, N_SMALL)
    dh, d_norm_pre = _norm1_bwd(du_a, du_b, h, norm_pre, r["dy_out"])
    grad_x = dh[PADF + N_META:].reshape(1, seq, D_MODEL)
    half_in = _add_chips(ids, g32_in, ra_in, rb_in)
    gw_in, gw_pr = _pair_join_halves([half_in, r["half_pr"]])

    tiny = r["d_bias_row"][0]
    part_small = _pack_small(d_norm_pre, r["d_conv_b"], r["d_ssd_norm"], r["d_gate_bias"], r["d_norm_post"],
                             tiny[0:H_SSD], r["d_a"][0, 0:H_SSD] * a_neg, r["d_dsk"].reshape(H_SSD, 64).sum(axis=1),
                             tiny[H_SSD:H_SSD + H_ATT])
    part = jnp.concatenate([_pad_rows(part_small, _SMALL_PAD), _rows128(r["d_conv_w"]),
                            _rows128(dh[PADF:PADF + N_META]), r["loss_blk"]], axis=0)
    tot = _sum8(_allgather8(part, "gather_small_grads"))
    loss = tot[_SMALL_PAD + 96 + 128, 0]
    g_small = tot[0:_SMALL_PAD]
    g_conv_w = lax.dynamic_slice_in_dim(tot[_SMALL_PAD:_SMALL_PAD + 96].reshape(CONV_K, CONV_DIM), chip * 768, 768, axis=1)
    g_meta = lax.dynamic_slice_in_dim(tot[_SMALL_PAD + 96:_SMALL_PAD + 224].reshape(N_META, D_MODEL), chip * 256, 256, axis=1)

    upd = {}
    upd["w_in"] = tuple(jnp.transpose(a) for a in (gw_in,) + _adamw(
        jnp.transpose(w_in[0]), gw_in, jnp.transpose(m_w_in[0]), jnp.transpose(v_w_in[0]), "adamw_w_in"))
    w_pr32 = jnp.concatenate([w_proj_ssd[0], w_proj_att[0], w_out[0]], axis=0)
    m_pr = jnp.concatenate([m_w_proj_ssd[0], m_w_proj_att[0], m_w_out[0]], axis=0)
    v_pr = jnp.concatenate([v_w_proj_ssd[0], v_w_proj_att[0], v_w_out[0]], axis=0)
    pr = (gw_pr,) + _adamw(w_pr32, gw_pr, m_pr, v_pr, "adamw_w_proj")
    upd["w_proj_ssd"] = tuple(a[0:512] for a in pr)
    upd["w_proj_att"] = tuple(a[512:768] for a in pr)
    upd["w_out"] = tuple(a[768:1024] for a in pr)
    upd["conv_w"] = (g_conv_w,) + _adamw(conv_w[0], g_conv_w, m_conv_w[0], v_conv_w[0], "adamw_conv_w")
    upd["meta_tokens"] = (g_meta,) + _adamw(meta_tokens, g_meta, m_meta_tokens, v_meta_tokens, "adamw_meta")
    pk = lambda np_, cb, sn, gb, npo, dtb, al, ds, fg: _pad_rows(_pack_small(np_, cb, sn, gb, npo, dtb, al, ds, fg), _SMALL_PAD)
    w_sm = pk(norm_pre, conv_b, ssd_norm, gate_bias, norm_post, dt_bias, a_log, d_skip, fgate_bias)
    m_sm = pk(m_norm_pre, m_conv_b, m_ssd_norm, m_gate_bias, m_norm_post, m_dt_bias, m_a_log, m_d_skip, m_fgate_bias)
    v_sm = pk(v_norm_pre, v_conv_b, v_ssd_norm, v_gate_bias, v_norm_post, v_dt_bias, v_a_log, v_d_skip, v_fgate_bias)
    sm = [_unpack_small(a) for a in (g_small,) + _adamw(w_sm, g_small, m_sm, v_sm, "adamw_small")]
    for name in ("norm_pre", "conv_b", "dt_bias", "a_log", "d_skip", "ssd_norm", "fgate_bias", "gate_bias", "norm_post"):
        upd[name] = tuple(s[name] for s in sm)
    lead = ("w_in", "conv_w", "w_proj_ssd", "w_proj_att", "w_out")
    order = ("meta_tokens", "norm_pre", "w_in", "conv_w", "conv_b", "dt_bias", "a_log", "d_skip", "ssd_norm",
             "fgate_bias", "gate_bias", "w_proj_ssd", "w_proj_att", "w_out", "norm_post")
    outs = [loss, grad_x]
    for part_i in range(4):
        for name in order:
            a = upd[name][part_i]
            outs.append(a[None] if name in lead else a)
    return tuple(outs)
```

```python
import functools
import math

import jax
import jax.numpy as jnp
from jax import lax
from jax.experimental import pallas as pl
from jax.experimental.pallas import tpu as pltpu

F32 = jnp.float32
BF16 = jnp.bfloat16
HIGHEST = lax.Precision.HIGHEST

D_MODEL = 1024
N_META = 16
CHUNK = 128
PADF = CHUNK - N_META
D_SSD = 2048
H_SSD = 32
G_SSD = 4
N_STATE = 128
CONV_K = 4
CONV_DIM = D_SSD + 2 * G_SSD * N_STATE
H_ATT = 16
D_ATT = 1024
EPS = 1e-6
N_COLS = 11312

C_Z, C_XBC, C_ZA, C_Q, C_K, C_V, C_G = 0, 2048, 5120, 6144, 7168, 8192, 9216
N_MAIN = 11264
N_SMALL = 128
O_Z, O_XBC, O_DT, O_ZA, O_Q, O_K, O_V, O_F, O_G = (
    (0, 2048), (2048, 3072), (5120, 32), (5152, 1024), (6176, 1024), (7200, 1024),
    (8224, 1024), (9248, 16), (9264, 2048))

ADAM_LR, ADAM_B1, ADAM_B2, ADAM_EPS, ADAM_WD, ADAM_STEP = 0.001, 0.9, 0.999, 1e-08, 0.01, 10

VMEM_LIMIT = 56 * 1024 * 1024


def _cp(*sem):
    return pltpu.CompilerParams(dimension_semantics=sem, vmem_limit_bytes=VMEM_LIMIT)


def _tile(n, prefs):
    for p in prefs:
        if n % p == 0:
            return p
    raise ValueError(f"no tile for {n} in {prefs}")


def _iota(shape, dim):
    return lax.broadcasted_iota(jnp.int32, shape, dim)


def _sigmoid(x):
    return 1.0 / (1.0 + jnp.exp(-x))


def _softplus_tail(x):
    return jnp.log(1.0 + jnp.exp(-jnp.abs(x)))


_NN = (((1,), (0,)), ((), ()))
_NT = (((1,), (1,)), ((), ()))
_TN = (((0,), (0,)), ((), ()))


def _dot(a, b, dims=_NN):
    return lax.dot_general(a, b, dims, preferred_element_type=F32)


def _dot_exact(a, b, dims=_NN):
    return lax.dot_general(a, b, dims, precision=HIGHEST, preferred_element_type=F32)


def _hosted_call(body, *, name, grid, in_specs, out_specs, out_shape, scratch_shapes, operands, semantics,
                 exchange=None, aliases=None):
    aliases = dict(aliases or {})
    if exchange is None:
        return pl.pallas_call(body, name=name, out_shape=out_shape, grid=grid, in_specs=in_specs,
                              out_specs=out_specs, scratch_shapes=scratch_shapes, input_output_aliases=aliases,
                              compiler_params=_cp(*semantics))(*operands)
    arrays, shapes, copies, n_sems, in_place = exchange
    n_in, n_out, n_ex = len(operands), len(out_shape), len(arrays)

    def hosted(*refs):
        ex_in = refs[n_in:n_in + n_ex]
        ex_out = refs[n_in + n_ex + n_out:n_in + n_ex + n_out + n_ex]
        own = refs[:n_in] + refs[n_in + n_ex:n_in + n_ex + n_out] + refs[n_in + 2 * n_ex + n_out:-2]
        first = functools.reduce(lambda p, q: p & q, [pl.program_id(d) == 0 for d in range(len(grid))])
        last = functools.reduce(lambda p, q: p & q, [pl.program_id(d) == grid[d] - 1 for d in range(len(grid))])

        def descriptors():
            return copies(ex_out if in_place else ex_in, ex_out, refs[-2], refs[-1])

        @pl.when(first)
        def _():
            for cp in descriptors():
                cp.start()

        body(*own)

        @pl.when(last)
        def _():
            for cp in descriptors():
                cp.wait()

    return pl.pallas_call(
        hosted, name=name,
        out_shape=tuple(out_shape) + tuple(shapes),
        grid=grid,
        in_specs=list(in_specs) + [_ANY] * n_ex,
        out_specs=tuple(out_specs) + (_ANY,) * n_ex,
        input_output_aliases={**aliases, **({n_in + e: n_out + e for e in range(n_ex)} if in_place else {})},
        scratch_shapes=list(scratch_shapes) + [pltpu.SemaphoreType.DMA((n_sems,)), pltpu.SemaphoreType.DMA((n_sems,))],
        compiler_params=_cp(*(("arbitrary",) * len(grid))),
    )(*operands, *arrays)


def _matmul(a, b, mode, out_dtype, name, tm, tn, tk, exchange=None):
    if mode == "tn":
        kdim, m = a.shape
    else:
        m, kdim = a.shape
    n = b.shape[0] if mode == "nt" else b.shape[1]
    nk = kdim // tk
    dims = {"nn": _NN, "nt": _NT, "tn": _TN}[mode]
    a_spec = (pl.BlockSpec((tk, tm), lambda i, j, k: (k, i)) if mode == "tn"
              else pl.BlockSpec((tm, tk), lambda i, j, k: (i, k)))
    b_spec = (pl.BlockSpec((tn, tk), lambda i, j, k: (j, k)) if mode == "nt"
              else pl.BlockSpec((tk, tn), lambda i, j, k: (k, j)))

    def body(a_ref, b_ref, o_ref, acc_ref):
        k = pl.program_id(2)
        p = _dot(a_ref[...].astype(BF16), b_ref[...].astype(BF16), dims)
        if nk == 1:
            o_ref[...] = p.astype(out_dtype)
        else:
            @pl.when(k == 0)
            def _():
                acc_ref[...] = p

            @pl.when(k > 0)
            def _():
                acc_ref[...] += p

            @pl.when(k == nk - 1)
            def _():
                o_ref[...] = acc_ref[...].astype(out_dtype)

    out = _hosted_call(
        body, name=name,
        out_shape=(jax.ShapeDtypeStruct((m, n), out_dtype),),
        grid=(m // tm, n // tn, nk),
        in_specs=[a_spec, b_spec],
        out_specs=(pl.BlockSpec((tm, tn), lambda i, j, k: (i, j)),),
        scratch_shapes=[pltpu.VMEM((tm, tn), F32)],
        operands=(a, b), semantics=("parallel", "parallel", "arbitrary"), exchange=exchange)
    return out[0] if exchange is None else out


_CAT_BLK = 1024


def _piece_ranges(pieces):
    out, off = [], 0
    for p in pieces:
        nb = p.shape[1] // _CAT_BLK
        out.append((off, nb))
        off += nb
    return out, off


def _matmul_cat_nn(pieces, b, name, tm, rows=None, fill=None, exchange=None):
    t = pieces[0].shape[0]
    n = b.shape[1]
    ranges, nk = _piece_ranges(pieces)
    first, ni = rows if rows is not None else (0, t // tm)
    n_in = len(pieces) + 1 + (fill is not None)

    def body(*refs):
        a_refs, b_ref, o_ref, acc_ref = refs[:len(pieces)], refs[len(pieces)], refs[n_in], refs[n_in + 1]
        k = pl.program_id(1)

        @pl.when(k == 0)
        def _():
            acc_ref[...] = jnp.zeros_like(acc_ref)

        for a_ref, (off, nb) in zip(a_refs, ranges):
            @pl.when((k >= off) & (k < off + nb))
            def _(a_ref=a_ref):
                acc_ref[...] += _dot(a_ref[...], b_ref[...])

        @pl.when(k == nk - 1)
        def _():
            o_ref[...] = acc_ref[...]

    def a_spec(off, nb):
        return pl.BlockSpec((tm, _CAT_BLK), lambda i, k: (first + i, jnp.clip(k - off, 0, nb - 1)))

    in_specs = [a_spec(off, nb) for off, nb in ranges] + [pl.BlockSpec((_CAT_BLK, n), lambda i, k: (k, 0))]
    operands = list(pieces) + [b]
    if fill is not None:
        in_specs.append(_ANY)
        operands.append(fill)
    out = _hosted_call(
        body, name=name,
        out_shape=(jax.ShapeDtypeStruct((t, n), F32),),
        grid=(ni, nk),
        in_specs=in_specs,
        out_specs=(pl.BlockSpec((tm, n), lambda i, k: (first + i, 0)),),
        scratch_shapes=[pltpu.VMEM((tm, n), F32)],
        operands=operands, semantics=("parallel", "arbitrary"), exchange=exchange,
        aliases={len(pieces) + 1: 0} if fill is not None else None)
    return out if exchange is not None else out[0]


def _matmul_cat_tn(pieces, b, name, tk):
    t = pieces[0].shape[0]
    n = b.shape[1]
    ranges, nm = _piece_ranges(pieces)
    nk = t // tk

    def body(*refs):
        a_refs, b_ref, o_ref, acc_ref = refs[:len(pieces)], refs[-3], refs[-2], refs[-1]
        m = pl.program_id(0)
        k = pl.program_id(1)

        @pl.when(k == 0)
        def _():
            acc_ref[...] = jnp.zeros_like(acc_ref)

        for a_ref, (off, nb) in zip(a_refs, ranges):
            @pl.when((m >= off) & (m < off + nb))
            def _(a_ref=a_ref):
                acc_ref[...] += _dot(a_ref[...], b_ref[...], _TN)

        @pl.when(k == nk - 1)
        def _():
            o_ref[...] = acc_ref[...]

    def a_spec(off, nb):
        def index(m, k):
            mine = (m >= off) & (m < off + nb)
            return jnp.where(mine, k, 0), jnp.clip(m - off, 0, nb - 1)
        return pl.BlockSpec((tk, _CAT_BLK), index)

    return pl.pallas_call(
        body, name=name,
        out_shape=jax.ShapeDtypeStruct((nm * _CAT_BLK, n), F32),
        grid=(nm, nk),
        in_specs=[a_spec(off, nb) for off, nb in ranges] + [pl.BlockSpec((tk, n), lambda m, k: (k, 0))],
        out_specs=pl.BlockSpec((_CAT_BLK, n), lambda m, k: (m, 0)),
        scratch_shapes=[pltpu.VMEM((_CAT_BLK, n), F32)],
        compiler_params=_cp("parallel", "arbitrary"),
    )(*pieces, b)


def _row_tile(t):
    return _tile(t, (352, 128))


def _row_tile_wide(t):
    return _tile(t, (176, 128))


def _norm1_fwd(h, g):
    t = h.shape[0]
    tm = _row_tile(t)

    def body(h_ref, g_ref, u_ref):
        x = h_ref[...]
        r = lax.rsqrt(jnp.mean(x * x, axis=-1, keepdims=True) + EPS)
        u_ref[...] = (x * r * g_ref[...]).astype(BF16)

    return pl.pallas_call(
        body, name="norm1_fwd",
        out_shape=jax.ShapeDtypeStruct((t, D_MODEL), BF16),
        grid=(t // tm,),
        in_specs=[pl.BlockSpec((tm, D_MODEL), lambda i: (i, 0)),
                  pl.BlockSpec((1, D_MODEL), lambda i: (0, 0))],
        out_specs=pl.BlockSpec((tm, D_MODEL), lambda i: (i, 0)),
        compiler_params=_cp("parallel"),
    )(h, g)


def _norm1_bwd(du_a, du_b, h, g, dy):
    t = h.shape[0]
    tm = _row_tile(t)

    def body(a_ref, b_ref, h_ref, g_ref, dy_ref, dh_ref, dg_ref):
        i = pl.program_id(0)
        x = h_ref[...]
        du = a_ref[...] + b_ref[...]
        r = lax.rsqrt(jnp.mean(x * x, axis=-1, keepdims=True) + EPS)
        gdu = du * g_ref[...]
        dh_ref[...] = dy_ref[...] + r * (gdu - x * (r * r) * jnp.mean(gdu * x, axis=-1, keepdims=True))
        part = jnp.sum(du * x * r, axis=0, keepdims=True)

        @pl.when(i == 0)
        def _():
            dg_ref[...] = part

        @pl.when(i > 0)
        def _():
            dg_ref[...] += part

    row = pl.BlockSpec((tm, D_MODEL), lambda i: (i, 0))
    vec = pl.BlockSpec((1, D_MODEL), lambda i: (0, 0))
    return pl.pallas_call(
        body, name="norm1_bwd",
        out_shape=(jax.ShapeDtypeStruct((t, D_MODEL), F32), jax.ShapeDtypeStruct((1, D_MODEL), F32)),
        grid=(t // tm,),
        in_specs=[row, row, row, vec, row],
        out_specs=(row, vec),
        compiler_params=_cp("arbitrary"),
    )(du_a, du_b, h, g, dy)


def _small_fwd(small, bias_row):
    t = small.shape[0]

    def body(s_ref, b_ref, o_ref, carry_ref):
        c = pl.program_id(0)

        @pl.when(c == 0)
        def _():
            carry_ref[...] = jnp.zeros_like(carry_ref)

        x = s_ref[...] + b_ref[...]
        r0 = _iota((CHUNK, CHUNK), 0)
        r1 = _iota((CHUNK, CHUNK), 1)
        valid = (c * CHUNK + r0) >= PADF
        tail = _softplus_tail(x)
        dt = jnp.where(valid & (r1 < H_SSD), jnp.maximum(x, 0.0) + tail, 0.0)
        lf = jnp.where(valid & (r1 >= H_SSD) & (r1 < H_SSD + H_ATT), jnp.minimum(x, 0.0) - tail, 0.0)
        tri = (r0 >= r1).astype(F32)
        cs = _dot_exact(tri, lf) + carry_ref[...]
        carry_ref[...] = cs[CHUNK - 1:CHUNK, :]
        o_ref[...] = dt + cs

    return pl.pallas_call(
        body, name="small_fwd",
        out_shape=jax.ShapeDtypeStruct((t, N_SMALL), F32),
        grid=(t // CHUNK,),
        in_specs=[pl.BlockSpec((CHUNK, N_SMALL), lambda c: (c, 0)),
                  pl.BlockSpec((1, N_SMALL), lambda c: (0, 0))],
        out_specs=pl.BlockSpec((CHUNK, N_SMALL), lambda c: (c, 0)),
        scratch_shapes=[pltpu.VMEM((1, N_SMALL), F32)],
        compiler_params=_cp("arbitrary"),
    )(small, bias_row)


def _small_bwd(dsm, small, bias_row):
    t = small.shape[0]
    nc = t // CHUNK

    def body(d_ref, s_ref, b_ref, o_ref, db_ref, carry_ref):
        step = pl.program_id(0)
        c = nc - 1 - step

        @pl.when(step == 0)
        def _():
            carry_ref[...] = jnp.zeros_like(carry_ref)
            db_ref[...] = jnp.zeros_like(db_ref)

        x = s_ref[...] + b_ref[...]
        d = d_ref[...]
        r0 = _iota((CHUNK, CHUNK), 0)
        r1 = _iota((CHUNK, CHUNK), 1)
        valid = (c * CHUNK + r0) >= PADF
        is_dt = r1 < H_SSD
        is_f = (r1 >= H_SSD) & (r1 < H_SSD + H_ATT)
        triu = (r1 >= r0).astype(F32)
        dc = jnp.where(is_f, d, 0.0)
        dlf = _dot_exact(triu, dc) + carry_ref[...]
        carry_ref[...] = dlf[0:1, :]
        sg = _sigmoid(x)
        out = jnp.where(valid & is_dt, d * sg, 0.0) + jnp.where(valid & is_f, dlf * (1.0 - sg), 0.0)
        o_ref[...] = out.astype(BF16)
        db_ref[...] += jnp.sum(out, axis=0, keepdims=True)

    blk = pl.BlockSpec((CHUNK, N_SMALL), lambda s: (nc - 1 - s, 0))
    vec = pl.BlockSpec((1, N_SMALL), lambda s: (0, 0))
    return pl.pallas_call(
        body, name="small_bwd",
        out_shape=(jax.ShapeDtypeStruct((t, N_SMALL), BF16), jax.ShapeDtypeStruct((1, N_SMALL), F32)),
        grid=(nc,),
        in_specs=[blk, blk, vec],
        out_specs=(blk, vec),
        scratch_shapes=[pltpu.VMEM((1, N_SMALL), F32)],
        compiler_params=_cp("arbitrary"),
    )(dsm, small, bias_row)


_CONV_TC = 1024
_XBC_BLK = C_XBC // _CONV_TC


def _shift_down(cur, prev8, j):
    rc = pltpu.roll(cur, j, 0)
    rid = _iota(prev8.shape, 0)
    top = jnp.where(rid < j, pltpu.roll(prev8, j, 0), rc[0:8, :])
    return top if cur.shape[0] == 8 else jnp.concatenate([top, rc[8:, :]], axis=0)


def _shift_up(cur, next8, j):
    n = cur.shape[0]
    ru = pltpu.roll(cur, n - j, 0)
    rid = _iota(next8.shape, 0)
    bot = jnp.where(rid >= 8 - j, pltpu.roll(next8, 8 - j, 0), ru[n - 8:, :])
    return jnp.concatenate([ru[:n - 8, :], bot], axis=0)


def _conv_taps(cur, prev, w, b):
    taps = [cur] + [_shift_down(cur, prev, j) for j in (1, 2, 3)]
    acc = b + taps[0] * w[3:4, :]
    for j in (1, 2, 3):
        acc = acc + taps[j] * w[3 - j:4 - j, :]
    return acc, taps


def _conv_pre(x_ref, p_ref, w_ref, b_ref, i):
    return _conv_taps(x_ref[...], jnp.where(i > 0, p_ref[...], 0.0), w_ref[...], b_ref[...])


def _dsilu(d, acc):
    sg = _sigmoid(acc)
    return d * sg * (1.0 + acc * (1.0 - sg))


def _conv_fwd(proj, conv_w, conv_b):
    t = proj.shape[0]
    tr = _row_tile(t)

    def body(x_ref, p_ref, w_ref, b_ref, o_ref):
        i = pl.program_id(0)
        acc, _ = _conv_pre(x_ref, p_ref, w_ref, b_ref, i)
        valid = (i * tr + _iota(acc.shape, 0)) >= PADF
        o_ref[...] = jnp.where(valid, acc * _sigmoid(acc), 0.0)

    return pl.pallas_call(
        body, name="conv_fwd",
        out_shape=jax.ShapeDtypeStruct((t, CONV_DIM), F32),
        grid=(t // tr, CONV_DIM // _CONV_TC),
        in_specs=[pl.BlockSpec((tr, _CONV_TC), lambda i, j: (i, _XBC_BLK + j)),
                  pl.BlockSpec((8, _CONV_TC), lambda i, j: (jnp.maximum(i * (tr // 8) - 1, 0), _XBC_BLK + j)),
                  pl.BlockSpec((CONV_K, _CONV_TC), lambda i, j: (0, j)),
                  pl.BlockSpec((1, _CONV_TC), lambda i, j: (0, j))],
        out_specs=pl.BlockSpec((tr, _CONV_TC), lambda i, j: (i, j)),
        compiler_params=_cp("parallel", "parallel"),
    )(proj, proj, conv_w, conv_b)


def _conv_bwd(dxbc, proj, conv_w, conv_b):
    t = proj.shape[0]
    tr = _row_tile(t)
    n_tiles = t // tr
    last8 = t // 8 - 1

    def body(d_ref, dn_ref, x_ref, p_ref, xn_ref, w_ref, b_ref, dx_ref, dw_ref, db_ref):
        i = pl.program_id(1)
        w = w_ref[...]
        b = b_ref[...]
        cur = x_ref[...]
        acc, taps = _conv_taps(cur, jnp.where(i > 0, p_ref[...], 0.0), w, b)
        valid = (i * tr + _iota(acc.shape, 0)) >= PADF
        da = jnp.where(valid, _dsilu(d_ref[...], acc), 0.0)
        acc_n, _ = _conv_taps(xn_ref[...], cur[tr - 8:, :], w, b)
        da_n = jnp.where(i < n_tiles - 1, _dsilu(dn_ref[...], acc_n), 0.0)
        dx = da * w[3:4, :]
        for j in (1, 2, 3):
            dx = dx + _shift_up(da, da_n, j) * w[3 - j:4 - j, :]
        dx_ref[...] = dx.astype(BF16)
        dw = jnp.concatenate([jnp.sum(da * taps[3 - k], axis=0, keepdims=True) for k in range(CONV_K)], axis=0)
        db = jnp.sum(da, axis=0, keepdims=True)

        @pl.when(i == 0)
        def _():
            dw_ref[...] = dw
            db_ref[...] = db

        @pl.when(i > 0)
        def _():
            dw_ref[...] += dw
            db_ref[...] += db

    nxt8 = lambda i: jnp.minimum((i + 1) * (tr // 8), last8)
    return pl.pallas_call(
        body, name="conv_bwd",
        out_shape=(jax.ShapeDtypeStruct((t, CONV_DIM), BF16),
                   jax.ShapeDtypeStruct((CONV_K, CONV_DIM), F32),
                   jax.ShapeDtypeStruct((1, CONV_DIM), F32)),
        grid=(CONV_DIM // _CONV_TC, n_tiles),
        in_specs=[pl.BlockSpec((tr, _CONV_TC), lambda j, i: (i, j)),
                  pl.BlockSpec((8, _CONV_TC), lambda j, i: (nxt8(i), j)),
                  pl.BlockSpec((tr, _CONV_TC), lambda j, i: (i, _XBC_BLK + j)),
                  pl.BlockSpec((8, _CONV_TC), lambda j, i: (jnp.maximum(i * (tr // 8) - 1, 0), _XBC_BLK + j)),
                  pl.BlockSpec((8, _CONV_TC), lambda j, i: (nxt8(i), _XBC_BLK + j)),
                  pl.BlockSpec((CONV_K, _CONV_TC), lambda j, i: (0, j)),
                  pl.BlockSpec((1, _CONV_TC), lambda j, i: (0, j))],
        out_specs=(pl.BlockSpec((tr, _CONV_TC), lambda j, i: (i, j)),
                   pl.BlockSpec((CONV_K, _CONV_TC), lambda j, i: (0, j)),
                   pl.BlockSpec((1, _CONV_TC), lambda j, i: (0, j))),
        compiler_params=_cp("parallel", "arbitrary"),
    )(dxbc, dxbc, proj, proj, proj, conv_w, conv_b)


_GW = D_SSD // G_SSD


def _ssd_prelude(dt_ref, a_ref, e_scr, es_scr, dte_scr):
    r0 = _iota((CHUNK, CHUNK), 0)
    r1 = _iota((CHUNK, CHUNK), 1)
    dt = jnp.where(r1 < H_SSD, dt_ref[...], 0.0)
    adt = dt * a_ref[...]
    acs = _dot_exact((r0 >= r1).astype(F32), adt)
    acs_t = acs.T
    alast = acs[CHUNK - 1:CHUNK, :]
    exp_a = jnp.exp(acs)
    dec_s = jnp.exp(alast - acs)
    lo = r1 < 64
    for j in range(H_SSD // 2):
        sl = slice(CHUNK * j, CHUNK * (j + 1))
        e_scr[:, sl] = jnp.where(lo, exp_a[:, 2 * j:2 * j + 1], exp_a[:, 2 * j + 1:2 * j + 2])
        es_scr[:, sl] = jnp.where(lo, dec_s[:, 2 * j:2 * j + 1], dec_s[:, 2 * j + 1:2 * j + 2])
        dte_scr[:, sl] = jnp.where(lo, dt[:, 2 * j:2 * j + 1], dt[:, 2 * j + 1:2 * j + 2])
    return dt, acs, acs_t, r0, r1, lo


def _chunk_decay_rows(acs_t, g):
    cd_t = jnp.exp(acs_t[:, CHUNK - 1:CHUNK])
    return jnp.concatenate(
        [jnp.broadcast_to(cd_t[8 * g + hh:8 * g + hh + 1, :], (64, N_STATE)) for hh in range(8)], axis=0)


def _ssd_fwd(xbc, dtlf, a_row, dsk_row):
    t = xbc.shape[0]
    nc = t // CHUNK

    def body(xs_ref, b_ref, c_ref, dt_ref, a_ref, dsk_ref, y_ref, hin_ref, h_scr, e_scr, es_scr, dte_scr):
        c = pl.program_id(0)

        @pl.when(c == 0)
        def _():
            h_scr[...] = jnp.zeros_like(h_scr)

        dt, acs, acs_t, r0, r1, lo = _ssd_prelude(dt_ref, a_ref, e_scr, es_scr, dte_scr)
        causal = r0 >= r1
        for g in range(G_SSD):
            gs = slice(_GW * g, _GW * (g + 1))
            bg = b_ref[:, N_STATE * g:N_STATE * (g + 1)].astype(BF16)
            cg = c_ref[:, N_STATE * g:N_STATE * (g + 1)].astype(BF16)
            cb = _dot(cg, bg, _NT)
            hg = h_scr[gs, :]
            hin_ref[0, gs, :] = hg
            xg = xs_ref[:, gs] * dte_scr[:, gs]
            yoff = _dot(cg, hg.astype(BF16), _NT) * e_scr[:, gs]
            st = _dot((xg * es_scr[:, gs]).astype(BF16), bg, _TN)
            h_scr[gs, :] = hg * _chunk_decay_rows(acs_t, g) + st
            for jj in range(4):
                j = 4 * g + jj
                sl = slice(CHUNK * j, CHUNK * (j + 1))
                xp = xg[:, CHUNK * jj:CHUNK * (jj + 1)]
                acc = yoff[:, CHUNK * jj:CHUNK * (jj + 1)] + dsk_ref[:, sl] * xs_ref[:, sl]
                for hh in range(2):
                    h = 2 * j + hh
                    seg = acs[:, h:h + 1] - acs_t[h:h + 1, :]
                    lm = jnp.exp(jnp.where(causal, seg, -1e30))
                    m = (cb * lm).astype(BF16)
                    xh = jnp.where(lo if hh == 0 else ~lo, xp, 0.0).astype(BF16)
                    acc = acc + _dot(m, xh)
                y_ref[:, sl] = acc

    return pl.pallas_call(
        body, name="ssd_fwd",
        out_shape=(jax.ShapeDtypeStruct((t, D_SSD), F32), jax.ShapeDtypeStruct((nc, D_SSD, N_STATE), F32)),
        grid=(nc,),
        in_specs=[pl.BlockSpec((CHUNK, D_SSD), lambda c: (c, 0)),
                  pl.BlockSpec((CHUNK, _GW), lambda c: (c, 4)),
                  pl.BlockSpec((CHUNK, _GW), lambda c: (c, 5)),
                  pl.BlockSpec((CHUNK, N_SMALL), lambda c: (c, 0)),
                  pl.BlockSpec((1, N_SMALL), lambda c: (0, 0)),
                  pl.BlockSpec((1, D_SSD), lambda c: (0, 0))],
        out_specs=(pl.BlockSpec((CHUNK, D_SSD), lambda c: (c, 0)),
                   pl.BlockSpec((1, D_SSD, N_STATE), lambda c: (c, 0, 0))),
        scratch_shapes=[pltpu.VMEM((D_SSD, N_STATE), F32)] + [pltpu.VMEM((CHUNK, D_SSD), F32)] * 3,
        compiler_params=_cp("arbitrary"),
    )(xbc, xbc, xbc, dtlf, a_row, dsk_row)


def _ssd_bwd(xbc, dtlf, a_row, dsk_row, hin, dy):
    t = xbc.shape[0]
    nc = t // CHUNK

    def body(xs_ref, b_ref, c_ref, dt_ref, a_ref, dsk_ref, hin_ref, dy_ref,
             dxbc_ref, ddt_ref, da_ref, ddsk_ref, dh_scr, e_scr, es_scr, dte_scr, dx_scr, whi_scr, wlo_scr):
        step = pl.program_id(0)

        @pl.when(step == 0)
        def _():
            dh_scr[...] = jnp.zeros_like(dh_scr)
            da_ref[...] = jnp.zeros_like(da_ref)
            ddsk_ref[...] = jnp.zeros_like(ddsk_ref)

        dt, acs, acs_t, r0, r1, lo = _ssd_prelude(dt_ref, a_ref, e_scr, es_scr, dte_scr)
        causal = r0 >= r1
        lane_row = _iota((1, CHUNK), 1)
        dacs = jnp.zeros((CHUNK, CHUNK), F32)
        dacs_t = jnp.zeros((CHUNK, CHUNK), F32)
        dalast = jnp.zeros((1, CHUNK), F32)
        ddt_dir = jnp.zeros((CHUNK, CHUNK), F32)
        ddsk_ref[...] += jnp.sum(dy_ref[...] * xs_ref[...], axis=0, keepdims=True)

        def head_sums(z, pick):
            hi = z.astype(BF16)
            return _dot(hi, pick) + _dot((z - hi.astype(F32)).astype(BF16), pick)

        for g in range(G_SSD):
            gs = slice(_GW * g, _GW * (g + 1))
            pick = (jnp.right_shift(_iota((_GW, CHUNK), 0), 6) + 8 * g == _iota((_GW, CHUNK), 1)).astype(BF16)
            bg = b_ref[:, N_STATE * g:N_STATE * (g + 1)].astype(BF16)
            cg = c_ref[:, N_STATE * g:N_STATE * (g + 1)].astype(BF16)
            cb = _dot(cg, bg, _NT)
            hg = hin_ref[0, gs, :]
            hgb = hg.astype(BF16)
            dhn = dh_scr[gs, :]
            dhnb = dhn.astype(BF16)
            esg = es_scr[:, gs]
            dyg = dy_ref[:, gs]
            xsg = xs_ref[:, gs]
            xg = xsg * dte_scr[:, gs]
            dyeb = (dyg * e_scr[:, gs]).astype(BF16)
            dc = _dot(dyeb, hgb)
            dh_y = _dot(dyeb, cg, _TN)
            dxs = _dot(bg, dhnb, _NT) * esg
            db = _dot((xg * esg).astype(BF16), dhnb)
            cd = _chunk_decay_rows(acs_t, g)
            dh_scr[gs, :] = dhn * cd + dh_y
            end_state = head_sums(jnp.broadcast_to(jnp.sum(xg * dxs, axis=0, keepdims=True), (8, _GW)), pick)[0:1, :]
            carried = dhn * hg * cd
            per_head = jnp.concatenate([jnp.sum(carried[64 * hh:64 * hh + 64, :], axis=0, keepdims=True)
                                        for hh in range(8)], axis=0)
            per_head = jnp.sum(per_head, axis=1, keepdims=True)
            for hh in range(8):
                end_state = end_state + jnp.where(lane_row == 8 * g + hh, per_head[hh:hh + 1, :], 0.0)
            dalast = dalast + end_state
            dcb = jnp.zeros((CHUNK, CHUNK), F32)
            for jj in range(4):
                j = 4 * g + jj
                sl = slice(CHUNK * j, CHUNK * (j + 1))
                ps = slice(CHUNK * jj, CHUNK * (jj + 1))
                xpb = xg[:, ps].astype(BF16)
                dyp = dyg[:, ps]
                dxp = dxs[:, ps]
                for hh in range(2):
                    h = 2 * j + hh
                    ws = slice(CHUNK * (2 * jj + hh), CHUNK * (2 * jj + hh + 1))
                    seg = acs[:, h:h + 1] - acs_t[h:h + 1, :]
                    lm = jnp.exp(jnp.where(causal, seg, -1e30))
                    mf = cb * lm
                    dyh = jnp.where(lo if hh == 0 else ~lo, dyp, 0.0).astype(BF16)
                    gm = _dot(dyh, xpb, _NT)
                    dcb = dcb + gm * lm
                    w = gm * mf
                    whi = w.astype(BF16)
                    whi_scr[:, ws] = whi
                    wlo_scr[:, ws] = (w - whi.astype(F32)).astype(BF16)
                    dacs_t = dacs_t - jnp.where(r0 == h, jnp.sum(w, axis=0, keepdims=True), 0.0)
                    dxp = dxp + _dot(mf.astype(BF16), dyh, _TN)
                dx_scr[:, sl] = dxp
            dxg = dx_scr[:, gs]
            pick_w = (jnp.right_shift(_iota((8 * CHUNK, CHUNK), 0), 7) + 8 * g == _iota((8 * CHUNK, CHUNK), 1)).astype(BF16)
            ch = _dot(cg, hgb, _NT)
            dacs = (dacs + _dot(whi_scr[...], pick_w) + _dot(wlo_scr[...], pick_w)
                    + head_sums(dyg * e_scr[:, gs] * ch - xg * dxs, pick))
            ddt_dir = ddt_dir + head_sums(dxg * xsg, pick)
            dcbb = dcb.astype(BF16)
            dxbc_ref[:, D_SSD + N_STATE * g:D_SSD + N_STATE * (g + 1)] = db + _dot(dcbb, cg, _TN)
            dxbc_ref[:, D_SSD + _GW + N_STATE * g:D_SSD + _GW + N_STATE * (g + 1)] = dc + _dot(dcbb, bg)
        dxbc_ref[:, 0:D_SSD] = dx_scr[...] * dte_scr[...] + dsk_ref[...] * dy_ref[...]
        dacs = dacs + dacs_t.T + jnp.where(r0 == CHUNK - 1, dalast, 0.0)
        dadt = _dot_exact((r1 >= r0).astype(F32), dacs)
        ddt_ref[...] = dadt * a_ref[...] + ddt_dir
        da_ref[...] += jnp.sum(dadt * dt, axis=0, keepdims=True)

    rev = lambda s: (nc - 1 - s, 0)
    return pl.pallas_call(
        body, name="ssd_bwd",
        out_shape=(jax.ShapeDtypeStruct((t, CONV_DIM), F32), jax.ShapeDtypeStruct((t, N_SMALL), F32),
                   jax.ShapeDtypeStruct((1, N_SMALL), F32), jax.ShapeDtypeStruct((1, D_SSD), F32)),
        grid=(nc,),
        in_specs=[pl.BlockSpec((CHUNK, D_SSD), rev),
                  pl.BlockSpec((CHUNK, _GW), lambda s: (nc - 1 - s, 4)),
                  pl.BlockSpec((CHUNK, _GW), lambda s: (nc - 1 - s, 5)),
                  pl.BlockSpec((CHUNK, N_SMALL), rev),
                  pl.BlockSpec((1, N_SMALL), lambda s: (0, 0)),
                  pl.BlockSpec((1, D_SSD), lambda s: (0, 0)),
                  pl.BlockSpec((1, D_SSD, N_STATE), lambda s: (nc - 1 - s, 0, 0)),
                  pl.BlockSpec((CHUNK, D_SSD), rev)],
        out_specs=(pl.BlockSpec((CHUNK, CONV_DIM), rev),
                   pl.BlockSpec((CHUNK, N_SMALL), rev),
                   pl.BlockSpec((1, N_SMALL), lambda s: (0, 0)),
                   pl.BlockSpec((1, D_SSD), lambda s: (0, 0))),
        scratch_shapes=([pltpu.VMEM((D_SSD, N_STATE), F32)] + [pltpu.VMEM((CHUNK, D_SSD), F32)] * 4
                        + [pltpu.VMEM((CHUNK, 8 * CHUNK), BF16)] * 2),
        compiler_params=_cp("arbitrary"),
    )(xbc, xbc, xbc, dtlf, a_row, dsk_row, hin, dy)


_NPAIR = H_ATT // 2
_QB, _KB, _VB = C_Q // 128, C_K // 128, C_V // 128
_SCALE = 1.0 / math.sqrt(64.0)


def _attn_blocks(t):
    return _tile(t, (1408, 384, 256, 128)), _tile(t, (384, 128))


def _split3(c):
    hi = c.astype(BF16).astype(F32)
    rest = c - hi
    mid = rest.astype(BF16).astype(F32)
    return hi, mid, rest - mid


def _head_lanes(lane, hh):
    return (lane < 64, 64) if hh == 0 else (lane >= 64, 0)


def _q_operand(q, cq, lane, hh):
    sel, first = _head_lanes(lane, hh)
    out = jnp.where(sel, q, 0.0)
    for n, col in enumerate(_split3(cq) + (1.0, 1.0, 1.0)):
        out = jnp.where(lane == first + n, col, out)
    return out.astype(BF16)


def _k_operand(k, ck, lane, hh):
    sel, first = _head_lanes(lane, hh)
    hi, mid, lo = _split3(ck)
    out = jnp.where(sel, k, 0.0)
    for n, col in enumerate((1.0, 1.0, 1.0, -hi, -mid, -lo)):
        out = jnp.where(lane == first + n, col, out)
    return out.astype(BF16)


def _sum_operand(x, lane, hh, at):
    sel, first = _head_lanes(lane, hh)
    return jnp.where(sel, x, jnp.where(lane == first + at, 1.0, 0.0)).astype(BF16)


_C_FILLER = 2.0 ** 30
_SKIP_STEP = 256


def _query_skips(bq):
    firsts = list(range(0, bq, _SKIP_STEP))
    far = 1 << 30
    return [(q0 if n else -far, firsts[n + 1] if n + 1 < len(firsts) else far, q0) for n, q0 in enumerate(firsts)]


def _attn_fwd(proj, c_col):
    t = proj.shape[0]
    bq, bk = _attn_blocks(t)
    nq, nk = t // bq, t // bk
    rs = 32

    def last_kv(i):
        return (i * bq + bq - 1) // bk

    def body(q_ref, k_ref, v_ref, cq_ref, ck_ref, o_ref, lse_ref, p_ref, mrun_ref, qs_scr, s_scr, m_scr, acc_scr):
        i = pl.program_id(1)
        kk = pl.program_id(2)
        lane_q = _iota((bq, 128), 1)

        @pl.when(kk == 0)
        def _():
            m_scr[...] = jnp.full_like(m_scr, -1e30)
            acc_scr[...] = jnp.zeros_like(acc_scr)
            q = q_ref[...] * _SCALE
            cq = cq_ref[0]
            for hh in range(2):
                qs_scr[hh] = _q_operand(q, cq[:, hh:hh + 1], lane_q, hh)

        def step(masked, q0):
            nqc = bq - q0
            lane_k = _iota((bk, 128), 1)
            k = k_ref[...]
            v = v_ref[...]
            ck = ck_ref[0]
            ahead = _iota((rs, nqc), 0) - _iota((rs, nqc), 1) - q0
            vss = []
            for hh in range(2):
                sel, first = _head_lanes(lane_k, hh)
                ks = _k_operand(k, ck[:, hh:hh + 1], lane_k, hh)
                vss.append(jnp.where(sel, v, jnp.where(lane_k == first, 1.0, 0.0)).astype(BF16))
                s_scr[hh, :, q0:] = _dot(ks, qs_scr[hh, q0:, :], _NT)
            for hh in range(2):
                vs = vss[hh]

                def block_max(r, mx):
                    rows = pl.ds(pl.multiple_of(r * rs, rs), rs)
                    s = s_scr[hh, rows, q0:]
                    if masked:
                        s = jnp.where(ahead <= i * bq - kk * bk - r * rs, s, -1e30)
                        s_scr[hh, rows, q0:] = s
                    return jnp.maximum(mx, s)

                mx = lax.fori_loop(0, bk // rs, block_max, jnp.full((rs, nqc), -1e30, F32), unroll=True)
                m_old = m_scr[hh, :, q0:]
                m_new = jnp.maximum(m_old, jnp.max(mx, axis=0, keepdims=True))
                m_scr[hh, :, q0:] = m_new
                mrun_ref[0, hh:hh + 1, q0:] = m_new

                def probs(r, carry):
                    rows = pl.ds(pl.multiple_of(r * rs, rs), rs)
                    p_ref[0, hh, rows, q0:] = jnp.exp(s_scr[hh, rows, q0:] - m_new).astype(BF16)
                    return carry

                lax.fori_loop(0, bk // rs, probs, 0, unroll=True)
                acc_scr[hh, :, q0:] = (acc_scr[hh, :, q0:] * jnp.exp(m_old - m_new)
                                       + _dot(vs, p_ref[0, hh, :, q0:], _TN))

        active = kk <= last_kv(i)
        ahead_by = kk * bk - i * bq
        for lo, hi, q0 in _query_skips(bq):
            @pl.when(active & (ahead_by + bk - 1 > 0) & (ahead_by >= lo) & (ahead_by < hi))
            def _(q0=q0):
                step(True, q0)

        @pl.when(active & jnp.logical_not(ahead_by + bk - 1 > 0))
        def _():
            step(False, 0)

        @pl.when(kk == nk - 1)
        def _():
            a = acc_scr[0]
            b = acc_scr[1]
            la = a[64:65, :]
            lb = b[0:1, :]
            o_ref[...] = jnp.where(lane_q < 64, (a / la).T, (b / lb).T)
            lse_ref[0] = jnp.concatenate([m_scr[0] + jnp.log(la), m_scr[1] + jnp.log(lb)], axis=0)

    kvi = lambda i, kk: jnp.minimum(kk, last_kv(i))
    kv = lambda off: pl.BlockSpec((bk, 128), lambda j, i, kk: (kvi(i, kk), off + j))
    blk = lambda j, i, kk: (j * nq + i) * nk + kvi(i, kk)
    return pl.pallas_call(
        body, name="attn_fwd",
        out_shape=(jax.ShapeDtypeStruct((t, D_ATT), F32), jax.ShapeDtypeStruct((_NPAIR, 2, t), F32),
                   jax.ShapeDtypeStruct((_NPAIR * nq * nk, 2, bk, bq), BF16),
                   jax.ShapeDtypeStruct((_NPAIR * nq * nk, 2, bq), F32)),
        grid=(_NPAIR, nq, nk),
        in_specs=[pl.BlockSpec((bq, 128), lambda j, i, kk: (i, _QB + j)),
                  kv(_KB), kv(_VB),
                  pl.BlockSpec((1, bq, 2), lambda j, i, kk: (j, i, 0)),
                  pl.BlockSpec((1, bk, 2), lambda j, i, kk: (j, kvi(i, kk), 0))],
        out_specs=(pl.BlockSpec((bq, 128), lambda j, i, kk: (i, j)),
                   pl.BlockSpec((1, 2, bq), lambda j, i, kk: (j, 0, i)),
                   pl.BlockSpec((1, 2, bk, bq), lambda j, i, kk: (blk(j, i, kk), 0, 0, 0)),
                   pl.BlockSpec((1, 2, bq), lambda j, i, kk: (blk(j, i, kk), 0, 0))),
        scratch_shapes=[pltpu.VMEM((2, bq, 128), BF16), pltpu.VMEM((2, bk, bq), F32),
                        pltpu.VMEM((2, 1, bq), F32), pltpu.VMEM((2, 128, bq), F32)],
        compiler_params=_cp("parallel", "parallel", "arbitrary"),
    )(proj, proj, proj, c_col, c_col)


def _attn_bwd(proj, lse_row, dl_row, do, p_blocks, m_run, exchange=None):
    t = proj.shape[0]
    bq, bk = _attn_blocks(t)
    nq, nk = t // bq, t // bk
    rs = 16

    def first_q(kk):
        return (kk * bk) // bq

    def body(q_ref, k_ref, v_ref, lse_ref, dl_ref, do_ref, pblk_ref, mrun_ref,
             dq_ref, dk_ref, dv_ref, dck_ref, dcq_ref,
             qs_scr, doh_scr, ks_scr, dp_scr, p_scr, ds_scr, dq_scr, dk_scr, dv_scr):
        kk = pl.program_id(1)
        i = pl.program_id(2)
        lane_q = _iota((bq, 128), 1)
        lane_k = _iota((bk, 128), 1)
        qrows = pl.ds(pl.multiple_of(i * bq, 128), bq)

        @pl.when(kk == 0)
        def _():
            q = q_ref[...] * _SCALE
            do_ = do_ref[...]
            for hh in range(2):
                qs_scr[hh, qrows, :] = _sum_operand(q, lane_q, hh, 3)
                doh_scr[hh, qrows, :] = jnp.where(_head_lanes(lane_q, hh)[0], do_, 0.0).astype(BF16)
                dq_scr[hh, i] = jnp.zeros((128, bq), F32)

        @pl.when(i == 0)
        def _():
            dk_scr[...] = jnp.zeros_like(dk_scr)
            dv_scr[...] = jnp.zeros_like(dv_scr)
            k = k_ref[...]
            for hh in range(2):
                ks_scr[hh] = _sum_operand(k, lane_k, hh, 0)

        def step(q0):
            seen = pl.ds(pl.multiple_of(i * bq + q0, 128), bq - q0)
            v16 = v_ref[...].astype(BF16)
            dl = dl_ref[0, :, q0:]
            rescale = jnp.exp(mrun_ref[0, :, q0:] - lse_ref[0, :, q0:])
            for hh in range(2):
                dp_scr[hh, :, q0:] = _dot(v16, doh_scr[hh, seen, :], _NT)
            for hh in range(2):
                qs = qs_scr[hh, seen, :]
                doh = doh_scr[hh, seen, :]

                def strip(r, carry):
                    rows = pl.ds(pl.multiple_of(r * rs, rs), rs)
                    p = pblk_ref[0, hh, rows, q0:].astype(F32) * rescale[hh:hh + 1, :]
                    p_scr[hh, rows, q0:] = p.astype(BF16)
                    ds_scr[hh, rows, q0:] = (p * (dp_scr[hh, rows, q0:] - dl[hh:hh + 1, :])).astype(BF16)
                    return carry

                lax.fori_loop(0, bk // rs, strip, 0, unroll=True)
                dv_scr[...] += _dot(p_scr[hh, :, q0:], doh)
                dk_scr[hh] += _dot(ds_scr[hh, :, q0:], qs)
                dq_scr[hh, i, :, q0:] += _dot(ks_scr[hh], ds_scr[hh, :, q0:], _TN)

        ahead_by = kk * bk - i * bq
        for lo, hi, q0 in _query_skips(bq):
            @pl.when((i >= first_q(kk)) & (ahead_by >= lo) & (ahead_by < hi))
            def _(q0=q0):
                step(q0)

        @pl.when(i == nq - 1)
        def _():
            dka = dk_scr[0]
            dkb = dk_scr[1]
            dk_ref[...] = jnp.where(lane_k < 64, dka, dkb).astype(BF16)
            dv_ref[...] = dv_scr[...].astype(BF16)
            dck_ref[0] = -jnp.where(_iota((bk, 2), 1) == 0, dka[:, 67:68], dkb[:, 3:4])

        @pl.when((kk == nk - 1) & (i == nq - 1))
        def _():
            for ii in range(nq):
                cols = slice(ii * bq, (ii + 1) * bq)
                dqa = dq_scr[0, ii]
                dqb = dq_scr[1, ii]
                dq_ref[cols, :] = (jnp.where(lane_q < 64, dqa.T, dqb.T) * _SCALE).astype(BF16)
                dcq_ref[0, :, cols] = jnp.concatenate([dqa[64:65, :], dqb[0:1, :]], axis=0)

    qi = lambda kk, i: jnp.where(kk == 0, i, nq - 1)
    qspec = lambda off: pl.BlockSpec((bq, 128), lambda j, kk, i: (qi(kk, i), off + j))
    kspec = lambda off: pl.BlockSpec((bk, 128), lambda j, kk, i: (kk, off + j))
    rowspec = pl.BlockSpec((1, 2, bq), lambda j, kk, i: (j, 0, jnp.maximum(i, first_q(kk))))
    blk = lambda j, kk, i: (j * nq + jnp.maximum(i, first_q(kk))) * nk + kk
    return _hosted_call(
        body, name="attn_bwd",
        out_shape=(jax.ShapeDtypeStruct((t, D_ATT), BF16), jax.ShapeDtypeStruct((t, D_ATT), BF16),
                   jax.ShapeDtypeStruct((t, D_ATT), BF16), jax.ShapeDtypeStruct((_NPAIR, t, 2), F32),
                   jax.ShapeDtypeStruct((_NPAIR, 2, t), F32)),
        grid=(_NPAIR, nk, nq),
        in_specs=[qspec(_QB), kspec(_KB), kspec(_VB),
                  rowspec, rowspec, qspec(0),
                  pl.BlockSpec((1, 2, bk, bq), lambda j, kk, i: (blk(j, kk, i), 0, 0, 0)),
                  pl.BlockSpec((1, 2, bq), lambda j, kk, i: (blk(j, kk, i), 0, 0))],
        out_specs=(pl.BlockSpec((t, 128), lambda j, kk, i: (0, j)),
                   pl.BlockSpec((bk, 128), lambda j, kk, i: (kk, j)),
                   pl.BlockSpec((bk, 128), lambda j, kk, i: (kk, j)),
                   pl.BlockSpec((1, bk, 2), lambda j, kk, i: (j, kk, 0)),
                   pl.BlockSpec((1, 2, t), lambda j, kk, i: (j, 0, 0))),
        scratch_shapes=[pltpu.VMEM((2, t, 128), BF16), pltpu.VMEM((2, t, 128), BF16), pltpu.VMEM((2, bk, 128), BF16),
                        pltpu.VMEM((2, bk, bq), F32),
                        pltpu.VMEM((2, bk, bq), BF16), pltpu.VMEM((2, bk, bq), BF16),
                        pltpu.VMEM((2, nq, 128, bq), F32), pltpu.VMEM((2, bk, 128), F32), pltpu.VMEM((bk, 128), F32)],
        operands=(proj, proj, proj, lse_row, dl_row, do, p_blocks, m_run),
        semantics=("parallel", "arbitrary", "arbitrary"), exchange=exchange)


def _premerge_fwd(y, o, proj, gamma):
    t = y.shape[0]
    tm = _row_tile_wide(t)

    def body(y_ref, z_ref, o_ref, za_ref, g_ref, ys_ref, ya_ref):
        z = z_ref[...]
        u = y_ref[...] * (z * _sigmoid(z))
        for g in range(G_SSD):
            gs = slice(_GW * g, _GW * (g + 1))
            ug = u[:, gs]
            r = lax.rsqrt(jnp.mean(ug * ug, axis=-1, keepdims=True) + EPS)
            ys_ref[:, gs] = (ug * r * g_ref[:, gs]).astype(BF16)
        za = za_ref[...]
        ya_ref[...] = (o_ref[...] * (za * _sigmoid(za))).astype(BF16)

    return pl.pallas_call(
        body, name="premerge_fwd",
        out_shape=(jax.ShapeDtypeStruct((t, D_SSD), BF16), jax.ShapeDtypeStruct((t, D_ATT), BF16)),
        grid=(t // tm,),
        in_specs=[pl.BlockSpec((tm, D_SSD), lambda i: (i, 0)),
                  pl.BlockSpec((tm, D_SSD), lambda i: (i, C_Z // D_SSD)),
                  pl.BlockSpec((tm, D_ATT), lambda i: (i, 0)),
                  pl.BlockSpec((tm, D_ATT), lambda i: (i, C_ZA // D_ATT)),
                  pl.BlockSpec((1, D_SSD), lambda i: (0, 0))],
        out_specs=(pl.BlockSpec((tm, D_SSD), lambda i: (i, 0)), pl.BlockSpec((tm, D_ATT), lambda i: (i, 0))),
        compiler_params=_cp("parallel"),
    )(y, proj, o, proj, gamma)


def _premerge_bwd(dys, dya, y, o, proj, gamma, exchange=None):
    t = y.shape[0]
    tm = _row_tile_wide(t)

    def body(dys_ref, dya_ref, y_ref, z_ref, o_ref, za_ref, g_ref, dy_ref, dz_ref, do_ref, dza_ref, dg_ref, dl_ref):
        i = pl.program_id(0)
        z = z_ref[...]
        sz = _sigmoid(z)
        silu = z * sz
        dsilu = sz * (1.0 + z * (1.0 - sz))
        yv = y_ref[...]
        u = yv * silu
        parts = []
        for g in range(G_SSD):
            gs = slice(_GW * g, _GW * (g + 1))
            ug = u[:, gs]
            r = lax.rsqrt(jnp.mean(ug * ug, axis=-1, keepdims=True) + EPS)
            n = ug * r
            dout = dys_ref[:, gs]
            dn = dout * g_ref[:, gs]
            du = r * (dn - n * jnp.mean(dn * n, axis=-1, keepdims=True))
            dy_ref[:, gs] = du * silu[:, gs]
            dz_ref[:, gs] = (du * yv[:, gs] * dsilu[:, gs]).astype(BF16)
            parts.append(jnp.sum(dout * n, axis=0, keepdims=True))
        dg = jnp.concatenate(parts, axis=1)
        za = za_ref[...]
        sa = _sigmoid(za)
        dya_ = dya_ref[...]
        ov = o_ref[...]
        do = dya_ * (za * sa)
        do_ref[...] = do
        dza_ref[...] = (dya_ * ov * (sa * (1.0 + za * (1.0 - sa)))).astype(BF16)
        pick = (jnp.right_shift(_iota((D_ATT, 128), 0), 6) == _iota((D_ATT, 128), 1)).astype(F32)
        dl_ref[...] = _dot_exact(do * ov, pick)

        @pl.when(i == 0)
        def _():
            dg_ref[...] = dg

        @pl.when(i > 0)
        def _():
            dg_ref[...] += dg

    ssd = pl.BlockSpec((tm, D_SSD), lambda i: (i, 0))
    att = pl.BlockSpec((tm, D_ATT), lambda i: (i, 0))
    vec = pl.BlockSpec((1, D_SSD), lambda i: (0, 0))
    return _hosted_call(
        body, name="premerge_bwd",
        out_shape=(jax.ShapeDtypeStruct((t, D_SSD), F32), jax.ShapeDtypeStruct((t, D_SSD), BF16),
                   jax.ShapeDtypeStruct((t, D_ATT), F32), jax.ShapeDtypeStruct((t, D_ATT), BF16),
                   jax.ShapeDtypeStruct((1, D_SSD), F32), jax.ShapeDtypeStruct((t, 128), F32)),
        grid=(t // tm,),
        in_specs=[ssd, att, ssd, pl.BlockSpec((tm, D_SSD), lambda i: (i, C_Z // D_SSD)), att,
                  pl.BlockSpec((tm, D_ATT), lambda i: (i, C_ZA // D_ATT)), vec],
        out_specs=(ssd, ssd, att, att, vec, pl.BlockSpec((tm, 128), lambda i: (i, 0))),
        scratch_shapes=[],
        operands=(dys, dya, y, proj, o, proj, gamma), semantics=("arbitrary",), exchange=exchange)


_G_BLK = C_G // D_MODEL


def _merge_fwd(a, b, proj, gate_bias):
    t = a.shape[0]
    tm = _row_tile(t)

    def body(a_ref, b_ref, gs_ref, ga_ref, bias_ref, m_ref):
        g_ssd = _sigmoid(gs_ref[...] + bias_ref[:, 0:D_MODEL])
        g_att = _sigmoid(ga_ref[...] + bias_ref[:, D_MODEL:2 * D_MODEL])
        m_ref[...] = (g_ssd * a_ref[...] + g_att * b_ref[...]).astype(BF16)

    row = pl.BlockSpec((tm, D_MODEL), lambda i: (i, 0))
    return pl.pallas_call(
        body, name="merge_fwd",
        out_shape=jax.ShapeDtypeStruct((t, D_MODEL), BF16),
        grid=(t // tm,),
        in_specs=[row, row,
                  pl.BlockSpec((tm, D_MODEL), lambda i: (i, _G_BLK)),
                  pl.BlockSpec((tm, D_MODEL), lambda i: (i, _G_BLK + 1)),
                  pl.BlockSpec((1, 2 * D_MODEL), lambda i: (0, 0))],
        out_specs=row,
        compiler_params=_cp("parallel"),
    )(a, b, proj, proj, gate_bias)


def _merge_bwd(dm, a, b, proj, gate_bias):
    t = a.shape[0]
    tm = _row_tile(t)

    def body(dm_ref, a_ref, b_ref, gs_ref, ga_ref, bias_ref, da_ref, db_ref, dg_ref, dbias_ref):
        i = pl.program_id(0)
        dm_ = dm_ref[...]
        g_ssd = _sigmoid(gs_ref[...] + bias_ref[:, 0:D_MODEL])
        g_att = _sigmoid(ga_ref[...] + bias_ref[:, D_MODEL:2 * D_MODEL])
        da_ref[...] = (dm_ * g_ssd).astype(BF16)
        db_ref[...] = (dm_ * g_att).astype(BF16)
        dgs = dm_ * a_ref[...] * g_ssd * (1.0 - g_ssd)
        dga = dm_ * b_ref[...] * g_att * (1.0 - g_att)
        dg_ref[:, 0:D_MODEL] = dgs.astype(BF16)
        dg_ref[:, D_MODEL:2 * D_MODEL] = dga.astype(BF16)
        part = jnp.concatenate([jnp.sum(dgs, axis=0, keepdims=True), jnp.sum(dga, axis=0, keepdims=True)], axis=1)

        @pl.when(i == 0)
        def _():
            dbias_ref[...] = part

        @pl.when(i > 0)
        def _():
            dbias_ref[...] += part

    row = pl.BlockSpec((tm, D_MODEL), lambda i: (i, 0))
    wide = pl.BlockSpec((tm, 2 * D_MODEL), lambda i: (i, 0))
    vec = pl.BlockSpec((1, 2 * D_MODEL), lambda i: (0, 0))
    return pl.pallas_call(
        body, name="merge_bwd",
        out_shape=(jax.ShapeDtypeStruct((t, D_MODEL), BF16), jax.ShapeDtypeStruct((t, D_MODEL), BF16),
                   jax.ShapeDtypeStruct((t, 2 * D_MODEL), BF16), jax.ShapeDtypeStruct((1, 2 * D_MODEL), F32)),
        grid=(t // tm,),
        in_specs=[row, row, row,
                  pl.BlockSpec((tm, D_MODEL), lambda i: (i, _G_BLK)),
                  pl.BlockSpec((tm, D_MODEL), lambda i: (i, _G_BLK + 1)), vec],
        out_specs=(row, row, wide, vec),
        compiler_params=_cp("arbitrary"),
    )(dm, a, b, proj, proj, gate_bias)


def _post(o2, h, target, g):
    t = o2.shape[0]
    nc = t // CHUNK

    def body(o_ref, h_ref, t_ref, g_ref, dy_ref, do_ref, dg_ref, loss_ref):
        c = pl.program_id(0)
        x = o_ref[...]
        r = lax.rsqrt(jnp.mean(x * x, axis=-1, keepdims=True) + EPS)
        n = x * r
        y = h_ref[...] + n * g_ref[...]
        diff = jnp.where(c > 0, y - t_ref[...], 0.0)
        dy = diff * (1.0 / D_MODEL)
        dy_ref[...] = dy
        gdy = dy * g_ref[...]
        do_ref[...] = (r * (gdy - n * jnp.mean(gdy * n, axis=-1, keepdims=True))).astype(BF16)
        dg = jnp.sum(dy * n, axis=0, keepdims=True)
        lpart = 0.5 * jnp.sum(jnp.sum(diff * diff, axis=1, keepdims=True), axis=0, keepdims=True) * (1.0 / D_MODEL)
        sel = (_iota((8, 128), 0) == 0) & (_iota((8, 128), 1) == 0)

        @pl.when(c == 0)
        def _():
            dg_ref[...] = dg
            loss_ref[...] = jnp.zeros_like(loss_ref)

        @pl.when(c > 0)
        def _():
            dg_ref[...] += dg
            loss_ref[...] += jnp.where(sel, lpart, 0.0)

    row = pl.BlockSpec((CHUNK, D_MODEL), lambda c: (c, 0))
    vec = pl.BlockSpec((1, D_MODEL), lambda c: (0, 0))
    return pl.pallas_call(
        body, name="post",
        out_shape=(jax.ShapeDtypeStruct((t, D_MODEL), F32), jax.ShapeDtypeStruct((t, D_MODEL), BF16),
                   jax.ShapeDtypeStruct((1, D_MODEL), F32), jax.ShapeDtypeStruct((8, 128), F32)),
        grid=(nc,),
        in_specs=[row, row, pl.BlockSpec((CHUNK, D_MODEL), lambda c: (jnp.maximum(c - 1, 0), 0)), vec],
        out_specs=(row, row, vec, pl.BlockSpec((8, 128), lambda c: (0, 0))),
        compiler_params=_cp("arbitrary"),
    )(o2, h, target, g)


def _mm_tiles(t):
    return _tile(t, (704, 384, 128))


def _local_step(h, target, w_main, w_small, pr_slots, ids, norm_pre, conv_w, conv_b, bias_row, a_row,
                dsk_row, ssd_norm, gate_bias, norm_post):
    t = h.shape[0]
    tm = _mm_tiles(t)
    u = _norm1_fwd(h, norm_pre)
    proj, pr_slots = _matmul(u, w_main, "nt", F32, "inproj", tm, 1024, D_MODEL,
                             exchange=_gather_stage([pr_slots], to_sibling=False))
    small, pr_slots = _matmul(u, w_small, "nt", F32, "inproj_small", tm, N_SMALL, D_MODEL,
                              exchange=_gather_stage([pr_slots], to_sibling=True))
    wps = pr_slots[:, 0:512].reshape(D_SSD, D_MODEL)
    wpa = pr_slots[:, 512:768].reshape(D_ATT, D_MODEL)
    wout = pr_slots[:, 768:1024].reshape(D_MODEL, D_MODEL)
    dtlf = _small_fwd(small, bias_row)
    xbc = _conv_fwd(proj, conv_w, conv_b)
    y, hin = _ssd_fwd(xbc, dtlf, a_row, dsk_row)
    c_tok = dtlf[:, H_SSD:H_SSD + H_ATT]
    c_tok = jnp.where(jnp.arange(t)[:, None] < PADF, _C_FILLER, c_tok)
    c_col = c_tok.reshape(t, _NPAIR, 2).transpose(1, 0, 2)
    o, lse, p_blocks, m_run = _attn_fwd(proj, c_col)
    ys, ya = _premerge_fwd(y, o, proj, ssd_norm)
    a = _matmul(ys, wps, "nn", F32, "proj_ssd", tm, D_MODEL, D_SSD)
    b = _matmul(ya, wpa, "nn", F32, "proj_att", tm, D_MODEL, D_ATT)
    merged = _merge_fwd(a, b, proj, gate_bias)
    o2 = _matmul(merged, wout, "nn", F32, "out_proj", tm, D_MODEL, D_MODEL)
    dy_out, do2, d_norm_post, loss_blk = _post(o2, h, target, norm_post)

    dm = _matmul(do2, wout, "nt", F32, "out_proj_dx", tm, D_MODEL, D_MODEL)
    d_wout = _matmul(merged, do2, "tn", F32, "out_proj_dw", D_MODEL, D_MODEL, tm)
    da, db, dgraw, d_gate_bias = _merge_bwd(dm, a, b, proj, gate_bias)
    dys = _matmul(da, wps, "nt", F32, "proj_ssd_dx", tm, D_SSD, D_MODEL)
    d_wps = _matmul(ys, da, "tn", F32, "proj_ssd_dw", D_SSD, D_MODEL, tm)
    dya = _matmul(db, wpa, "nt", F32, "proj_att_dx", tm, D_ATT, D_MODEL)
    d_wpa = _matmul(ya, db, "tn", F32, "proj_att_dw", D_ATT, D_MODEL, tm)
    g32_pr = jnp.concatenate([d_wps.reshape(4, 512, D_MODEL), d_wpa.reshape(4, 256, D_MODEL),
                              d_wout.reshape(4, 256, D_MODEL)], axis=1)
    dy, dz, do, dza, d_ssd_norm, dl, ra_pr = _premerge_bwd(dys, dya, y, o, proj, ssd_norm,
                                                           exchange=_pair_swap([g32_pr]))
    pb_pr = _add_pair(ids, g32_pr, ra_pr)
    dl_row = dl[:, 0:H_ATT].T.reshape(_NPAIR, 2, t)
    dq, dk, dv, dc_key, dc_qry, rb_pr = _attn_bwd(proj, lse, dl_row, do, p_blocks, m_run,
                                                  exchange=_chip_exchange([pb_pr]))
    half_pr = _add_chips(ids, g32_pr, ra_pr, rb_pr)
    dxbc, ddt, d_a, d_dsk = _ssd_bwd(xbc, dtlf, a_row, dsk_row, hin, dy)
    dxbc_raw, d_conv_w, d_conv_b = _conv_bwd(dxbc, proj, conv_w, conv_b)
    dc_tok = jnp.transpose(dc_key, (1, 0, 2)).reshape(t, H_ATT) + dc_qry.reshape(H_ATT, t).T
    dsm = ddt + jnp.pad(dc_tok, ((0, 0), (H_SSD, N_SMALL - H_SSD - H_ATT)))
    dsmall, d_bias_row = _small_bwd(dsm, small, bias_row)
    dproj = [dz, dxbc_raw, dza, dq, dk, dv, dgraw]
    return dict(loss_blk=loss_blk, u=u, dy_out=dy_out, dproj=dproj, dsmall=dsmall, half_pr=half_pr,
                d_conv_w=d_conv_w, d_conv_b=d_conv_b,
                d_bias_row=d_bias_row, d_a=d_a, d_dsk=d_dsk, d_ssd_norm=d_ssd_norm,
                d_gate_bias=d_gate_bias, d_norm_post=d_norm_post)


def _to_aligned_rows(slots):
    w = slots.reshape(N_COLS, slots.shape[2])

    def cut(o):
        return w[o[0]:o[0] + o[1]]
    main = jnp.concatenate([cut(O_Z), cut(O_XBC), cut(O_ZA), cut(O_Q), cut(O_K), cut(O_V), cut(O_G)], axis=0)
    pad = jnp.zeros((N_SMALL - H_SSD - H_ATT, w.shape[1]), w.dtype)
    small = jnp.concatenate([cut(O_DT), cut(O_F), pad], axis=0)
    return main, small


def _from_aligned_rows(main, small):
    def cm(c0, n):
        return main[c0:c0 + n]
    flat = jnp.concatenate([cm(C_Z, 2048), cm(C_XBC, 3072), small[0:H_SSD], cm(C_ZA, 1024),
                            cm(C_Q, 1024), cm(C_K, 1024), cm(C_V, 1024), small[H_SSD:H_SSD + H_ATT],
                            cm(C_G, 2048)], axis=0)
    return flat.reshape(4, N_COLS // 4, flat.shape[1])


_MESH = pl.DeviceIdType.MESH
_ANY = pl.BlockSpec(memory_space=pl.ANY)
_VM = pl.BlockSpec(memory_space=pltpu.VMEM)
_HALF = 512
N_DEV = 8


def _coords():
    return lax.axis_index("x"), lax.axis_index("y"), lax.axis_index("c")


def _other_chips(x, y):
    return [(1 - x, y), (x, 1 - y), (1 - x, 1 - y)]


def _half(cc):
    return pl.ds(cc * _HALF, _HALF)


def _gather_shards(slots):
    n = len(slots)

    def body(*refs):
        buf = refs[n:2 * n]
        send_sems, recv_sems = refs[2 * n:]
        x, y, c = _coords()
        chip = 2 * x + y
        sibling = (x, y, 1 - c)
        chips = _other_chips(x, y)

        def copy(i, frm, cc, k, to):
            part = buf[i].at[frm, :, _half(cc)]
            return pltpu.make_async_remote_copy(src_ref=part, dst_ref=part, send_sem=send_sems.at[6 * i + k],
                                                recv_sem=recv_sems.at[6 * i + k], device_id=to, device_id_type=_MESH)

        def chip_of(k):
            return 2 * chips[k][0] + chips[k][1]

        first = [copy(i, chip, c, k, (*chips[k], c)) for k in range(3) for i in range(n)]
        for cp in first:
            cp.start()
        passed = []
        for k in range(3):
            for i in range(n):
                copy(i, chip_of(k), c, k, (*chips[k], c)).wait_recv()
                passed.append(copy(i, chip_of(k), c, 3 + k, sibling))
                passed[-1].start()
        for k in range(3):
            for i in range(n):
                copy(i, chip_of(k), 1 - c, 3 + k, sibling).wait_recv()
        for cp in first + passed:
            cp.wait_send()

    return pl.pallas_call(
        body, name="gather_shards",
        out_shape=tuple(jax.ShapeDtypeStruct(s.shape, s.dtype) for s in slots),
        in_specs=[_ANY] * n, out_specs=tuple([_ANY] * n),
        input_output_aliases={i: i for i in range(n)},
        scratch_shapes=[pltpu.SemaphoreType.DMA((6 * n,)), pltpu.SemaphoreType.DMA((6 * n,))],
    )(*slots)


def _allgather8(block, name):
    rows, width = block.shape

    def body(x_ref, out_ref, send_sems, recv_sems, local_sem):
        x, y, c = _coords()
        me, sibling = (x, y, c), (x, y, 1 - c)
        chips = _other_chips(x, y)

        def slot(px, py, pc):
            return out_ref.at[4 * px + 2 * py + pc]

        def copy(k, blk, to, src=None):
            return pltpu.make_async_remote_copy(src_ref=slot(*blk) if src is None else src, dst_ref=slot(*blk),
                                                send_sem=send_sems.at[k], recv_sem=recv_sems.at[k],
                                                device_id=to, device_id_type=_MESH)

        mine = pltpu.make_async_copy(x_ref, slot(*me), local_sem)
        mine.start()
        first = [copy(0, me, sibling, src=x_ref)]
        first += [copy(1 + j, me, (*chip, c), src=x_ref) for j, chip in enumerate(chips)]
        for cp in first:
            cp.start()
        passed = [copy(4 + j, (*chip, c), sibling) for j, chip in enumerate(chips)]
        for j, chip in enumerate(chips):
            copy(1 + j, (*chip, c), me).wait_recv()
            passed[j].start()
        copy(0, sibling, me).wait_recv()
        for j, chip in enumerate(chips):
            copy(4 + j, (*chip, 1 - c), me).wait_recv()
        for cp in first + passed:
            cp.wait_send()
        mine.wait()

    return pl.pallas_call(
        body, name=name,
        out_shape=jax.ShapeDtypeStruct((N_DEV, rows, width), block.dtype),
        in_specs=[_VM], out_specs=_VM,
        scratch_shapes=[pltpu.SemaphoreType.DMA((7,)), pltpu.SemaphoreType.DMA((7,)), pltpu.SemaphoreType.DMA],
    )(block)


def _pair_swap(arrs):
    def copies(src, dst, send_sems, recv_sems):
        x, y, c = _coords()
        return [pltpu.make_async_remote_copy(src_ref=src[i].at[:, :, _half(1 - c)], dst_ref=dst[i],
                                             send_sem=send_sems.at[i], recv_sem=recv_sems.at[i],
                                             device_id=(x, y, 1 - c), device_id_type=_MESH) for i in range(len(src))]

    shapes = tuple(jax.ShapeDtypeStruct((4, a.shape[1], _HALF), a.dtype) for a in arrs)
    return tuple(arrs), shapes, copies, len(arrs), False


def _chip_exchange(arrs):
    def copies(src, dst, send_sems, recv_sems):
        x, y, c = _coords()
        chips = _other_chips(x, y)
        return [pltpu.make_async_remote_copy(src_ref=src[i].at[2 * chips[k][0] + chips[k][1]], dst_ref=dst[i].at[k],
                                             send_sem=send_sems.at[3 * i + k], recv_sem=recv_sems.at[3 * i + k],
                                             device_id=(*chips[k], c), device_id_type=_MESH)
                for k in range(3) for i in range(len(src))]

    shapes = tuple(jax.ShapeDtypeStruct((3,) + a.shape[1:], a.dtype) for a in arrs)
    return tuple(arrs), shapes, copies, 3 * len(arrs), False


def _gather_stage(slots, to_sibling):
    def copies(buf, _, send_sems, recv_sems):
        x, y, c = _coords()
        chips = _other_chips(x, y)
        out = []
        for k in range(3):
            for i in range(len(buf)):
                frm = 2 * chips[k][0] + chips[k][1] if to_sibling else 2 * x + y
                part = buf[i].at[frm, :, _half(c)]
                out.append(pltpu.make_async_remote_copy(
                    src_ref=part, dst_ref=part, send_sem=send_sems.at[3 * i + k], recv_sem=recv_sems.at[3 * i + k],
                    device_id=(x, y, 1 - c) if to_sibling else (*chips[k], c), device_id_type=_MESH))
        return out

    shapes = tuple(jax.ShapeDtypeStruct(s.shape, s.dtype) for s in slots)
    return tuple(slots), shapes, copies, 3 * len(slots), True


def _pair_join_halves(fulls):
    n = len(fulls)

    def body(*refs):
        buf = refs[n:2 * n]
        send_sems, recv_sems = refs[2 * n:]
        x, y, c = _coords()

        def remote(i, cc):
            part = buf[i].at[:, _half(cc)]
            return pltpu.make_async_remote_copy(src_ref=part, dst_ref=part, send_sem=send_sems.at[i],
                                                recv_sem=recv_sems.at[i], device_id=(x, y, 1 - c), device_id_type=_MESH)

        for i in range(n):
            remote(i, c).start()
        for i in range(n):
            remote(i, c).wait_send()
            remote(i, 1 - c).wait_recv()

    return pl.pallas_call(
        body, name="pair_join_halves",
        out_shape=tuple(jax.ShapeDtypeStruct(a.shape, a.dtype) for a in fulls),
        in_specs=[_ANY] * n, out_specs=tuple([_ANY] * n),
        input_output_aliases={i: i for i in range(n)},
        scratch_shapes=[pltpu.SemaphoreType.DMA((n,)), pltpu.SemaphoreType.DMA((n,))],
    )(*fulls)


_RED_TC = 128
_RED_NT = _HALF // _RED_TC


def _add_pair(ids, g32, recv_a):
    rows = g32.shape[1]

    def body(ids_ref, g_ref, r_ref, o_ref):
        o_ref[...] = (g_ref[...] + r_ref[...]).astype(BF16)

    blk = pl.BlockSpec((1, rows, _RED_TC), lambda j, l, ids: (j, 0, l))
    return pl.pallas_call(
        body, name="add_pair",
        out_shape=jax.ShapeDtypeStruct((4, rows, _HALF), BF16),
        grid_spec=pltpu.PrefetchScalarGridSpec(
            num_scalar_prefetch=1, grid=(4, _RED_NT),
            in_specs=[pl.BlockSpec((1, rows, _RED_TC), lambda j, l, ids: (j, 0, ids[0] * _RED_NT + l)), blk],
            out_specs=blk),
        compiler_params=_cp("parallel", "parallel"),
    )(ids, g32, recv_a)


def _add_chips(ids, g32, recv_a, recv_b):
    rows = g32.shape[1]

    def body(ids_ref, g_ref, a_ref, b_ref, o_ref):
        acc = g_ref[0] + a_ref[0]
        for k in range(3):
            acc = acc + b_ref[k].astype(F32)
        o_ref[...] = acc

    return pl.pallas_call(
        body, name="add_chips",
        out_shape=jax.ShapeDtypeStruct((rows, 2 * _HALF), F32),
        grid_spec=pltpu.PrefetchScalarGridSpec(
            num_scalar_prefetch=1, grid=(_RED_NT,),
            in_specs=[pl.BlockSpec((1, rows, _RED_TC), lambda l, ids: (ids[1], 0, ids[0] * _RED_NT + l)),
                      pl.BlockSpec((1, rows, _RED_TC), lambda l, ids: (ids[1], 0, l)),
                      pl.BlockSpec((3, rows, _RED_TC), lambda l, ids: (0, 0, l))],
            out_specs=pl.BlockSpec((rows, _RED_TC), lambda l, ids: (0, ids[0] * _RED_NT + l))),
        compiler_params=_cp("parallel"),
    )(ids, g32, recv_a, recv_b)


def _sum8(gathered):
    _, rows, width = gathered.shape

    def body(g_ref, o_ref):
        acc = g_ref[0]
        for d in range(1, N_DEV):
            acc = acc + g_ref[d]
        o_ref[...] = acc

    return pl.pallas_call(
        body, name="sum8",
        out_shape=jax.ShapeDtypeStruct((rows, width), F32),
        in_specs=[_VM], out_specs=_VM,
    )(gathered)


def _adamw(w, g, m, v, name):
    rows, cols = w.shape
    budget = (3 << 20) // 2
    tr, tc = rows, cols
    if rows * cols * 4 > budget:
        if rows % 8 == 0:
            tr = max(c for c in range(8, rows, 8) if rows % c == 0 and c * cols * 4 <= budget)
        else:
            tc = next(c for c in (512, 256, 128) if cols % c == 0 and rows * c * 4 <= budget)
    c1 = 1.0 - ADAM_B1 ** ADAM_STEP
    c2 = 1.0 - ADAM_B2 ** ADAM_STEP

    def body(w_ref, g_ref, m_ref, v_ref, d_ref, mo_ref, vo_ref):
        gg = g_ref[...]
        mn = ADAM_B1 * m_ref[...] + (1.0 - ADAM_B1) * gg
        vn = ADAM_B2 * v_ref[...] + (1.0 - ADAM_B2) * (gg * gg)
        mo_ref[...] = mn
        vo_ref[...] = vn
        d_ref[...] = -ADAM_LR * ((mn / c1) / (jnp.sqrt(vn / c2) + ADAM_EPS) + ADAM_WD * w_ref[...])

    blk = pl.BlockSpec((tr, tc), lambda i, j: (i, j))
    shp = jax.ShapeDtypeStruct((rows, cols), F32)
    return pl.pallas_call(
        body, name=name, out_shape=(shp, shp, shp), grid=(rows // tr, cols // tc),
        in_specs=[blk] * 4, out_specs=(blk, blk, blk),
        compiler_params=_cp("parallel", "parallel"),
    )(w, g, m, v)


def _rows128(a):
    return a.reshape(-1, 128)


def _pack_small(norm_pre, conv_b, ssd_norm, gate_bias, norm_post, dt_bias, a_log, d_skip, fgate_bias):
    tiny = jnp.concatenate([dt_bias.reshape(-1), a_log.reshape(-1), d_skip.reshape(-1), fgate_bias.reshape(-1),
                            jnp.zeros((16,), F32)])
    return jnp.concatenate([_rows128(norm_pre), _rows128(conv_b), _rows128(ssd_norm), _rows128(gate_bias),
                            _rows128(norm_post), tiny.reshape(1, 128)], axis=0)


_SMALL_ROWS = 73
_SMALL_PAD = 80


def _unpack_small(p):
    tiny = p[72]
    return dict(norm_pre=p[0:8].reshape(1, 1024), conv_b=p[8:32].reshape(1, 3072), ssd_norm=p[32:48].reshape(1, 2048),
                gate_bias=p[48:64].reshape(1, 2048), norm_post=p[64:72].reshape(1, 1024),
                dt_bias=tiny[0:32].reshape(1, 32), a_log=tiny[32:64].reshape(1, 32),
                d_skip=tiny[64:96].reshape(1, 32), fgate_bias=tiny[96:112].reshape(1, 16))


def _pad_rows(a, rows):
    return jnp.concatenate([a, jnp.zeros((rows - a.shape[0], a.shape[1]), a.dtype)], axis=0)


def kernel(x, meta_tokens, norm_pre, w_in, conv_w, conv_b, dt_bias, a_log, d_skip, ssd_norm, fgate_bias, gate_bias, w_proj_ssd, w_proj_att, w_out, norm_post, loss_target, m_meta_tokens, m_norm_pre, m_w_in, m_conv_w, m_conv_b, m_dt_bias, m_a_log, m_d_skip, m_ssd_norm, m_fgate_bias, m_gate_bias, m_w_proj_ssd, m_w_proj_att, m_w_out, m_norm_post, v_meta_tokens, v_norm_pre, v_w_in, v_conv_w, v_conv_b, v_dt_bias, v_a_log, v_d_skip, v_ssd_norm, v_fgate_bias, v_gate_bias, v_w_proj_ssd, v_w_proj_att, v_w_out, v_norm_post):
    cx, cy, cc = _coords()
    chip = 2 * cx + cy
    ids = jnp.stack([cc, chip]).astype(jnp.int32)
    seq = x.shape[1]

    w_in_sh = jnp.transpose(w_in[0]).astype(BF16)
    w_pr_sh = jnp.concatenate([w_proj_ssd[0], w_proj_att[0], w_out[0]], axis=0).astype(BF16)

    def own_slot(sh):
        return lax.dynamic_update_slice(lax.empty((4,) + sh.shape, sh.dtype), sh[None], (chip, 0, 0))

    (g_in,) = _gather_shards([own_slot(w_in_sh)])
    w_main, w_small = _to_aligned_rows(g_in)
    sm_sh = jnp.concatenate([_rows128(meta_tokens), _rows128(conv_w[0])], axis=0)
    sm_all = _allgather8(sm_sh, "gather_small_weights")[0::2]
    meta_full = jnp.transpose(sm_all[:, 0:32].reshape(4, N_META, 256), (1, 0, 2)).reshape(N_META, D_MODEL)
    conv_w_full = jnp.transpose(sm_all[:, 32:56].reshape(4, CONV_K, 768), (1, 0, 2)).reshape(CONV_K, CONV_DIM)

    h = jnp.concatenate([jnp.zeros((PADF, D_MODEL), F32), meta_full, x[0]], axis=0)
    bias_row = jnp.concatenate([dt_bias[0], fgate_bias[0], jnp.zeros((N_SMALL - H_SSD - H_ATT,), F32)]).reshape(1, N_SMALL)
    a_neg = -jnp.exp(a_log[0])
    a_row = jnp.concatenate([a_neg, jnp.zeros((N_SMALL - H_SSD,), F32)]).reshape(1, N_SMALL)
    dsk_row = jnp.repeat(d_skip[0], 64).reshape(1, D_SSD)
    r = _local_step(h, loss_target[0], w_main, w_small, own_slot(w_pr_sh), ids, norm_pre, conv_w_full, conv_b,
                    bias_row, a_row, dsk_row, ssd_norm, gate_bias, norm_post)

    tm = _mm_tiles(h.shape[0])
    n_row_tiles = h.shape[0] // tm
    d_w_main = _matmul_cat_tn(r["dproj"], r["u"], "inproj_dw", tm)
    d_w_small = _matmul(r["dsmall"], r["u"], "tn", F32, "inproj_small_dw", N_SMALL, D_MODEL, tm)
    g32_in = _from_aligned_rows(d_w_main, d_w_small)
    first = max(n_row_tiles // 6, 1)
    du_first, ra_in = _matmul_cat_nn(r["dproj"], w_main, "inproj_dx_swap", tm, rows=(0, first),
                                     exchange=_pair_swap([g32_in]))
    pb_in = _add_pair(ids, g32_in, ra_in)
    du_a, rb_in = _matmul_cat_nn(r["dproj"], w_main, "inproj_dx_exchange", tm,
                                 rows=(first, n_row_tiles - first), fill=du_first,
                                 exchange=_chip_exchange([pb_in]))
    du_b = _matmul(r["dsmall"], w_small, "nn", F32, "inproj_small_dx", tm, D_MODEL, N_SMALL)
    dh, d_norm_pre = _norm1_bwd(du_a, du_b, h, norm_pre, r["dy_out"])
    grad_x = dh[PADF + N_META:].reshape(1, seq, D_MODEL)
    half_in = _add_chips(ids, g32_in, ra_in, rb_in)
    gw_in, gw_pr = _pair_join_halves([half_in, r["half_pr"]])

    tiny = r["d_bias_row"][0]
    part_small = _pack_small(d_norm_pre, r["d_conv_b"], r["d_ssd_norm"], r["d_gate_bias"], r["d_norm_post"],
                             tiny[0:H_SSD], r["d_a"][0, 0:H_SSD] * a_neg, r["d_dsk"].reshape(H_SSD, 64).sum(axis=1),
                             tiny[H_SSD:H_SSD + H_ATT])
    part = jnp.concatenate([_pad_rows(part_small, _SMALL_PAD), _rows128(r["d_conv_w"]),
                            _rows128(dh[PADF:PADF + N_META]), r["loss_blk"]], axis=0)
    tot = _sum8(_allgather8(part, "gather_small_grads"))
    loss = tot[_SMALL_PAD + 96 + 128, 0]
    g_small = tot[0:_SMALL_PAD]
    g_conv_w = lax.dynamic_slice_in_dim(tot[_SMALL_PAD:_SMALL_PAD + 96].reshape(CONV_K, CONV_DIM), chip * 768, 768, axis=1)
    g_meta = lax.dynamic_slice_in_dim(tot[_SMALL_PAD + 96:_SMALL_PAD + 224].reshape(N_META, D_MODEL), chip * 256, 256, axis=1)

    upd = {}
    upd["w_in"] = tuple(jnp.transpose(a) for a in (gw_in,) + _adamw(
        jnp.transpose(w_in[0]), gw_in, jnp.transpose(m_w_in[0]), jnp.transpose(v_w_in[0]), "adamw_w_in"))
    w_pr32 = jnp.concatenate([w_proj_ssd[0], w_proj_att[0], w_out[0]], axis=0)
    m_pr = jnp.concatenate([m_w_proj_ssd[0], m_w_proj_att[0], m_w_out[0]], axis=0)
    v_pr = jnp.concatenate([v_w_proj_ssd[0], v_w_proj_att[0], v_w_out[0]], axis=0)
    pr = (gw_pr,) + _adamw(w_pr32, gw_pr, m_pr, v_pr, "adamw_w_proj")
    upd["w_proj_ssd"] = tuple(a[0:512] for a in pr)
    upd["w_proj_att"] = tuple(a[512:768] for a in pr)
    upd["w_out"] = tuple(a[768:1024] for a in pr)
    upd["conv_w"] = (g_conv_w,) + _adamw(conv_w[0], g_conv_w, m_conv_w[0], v_conv_w[0], "adamw_conv_w")
    upd["meta_tokens"] = (g_meta,) + _adamw(meta_tokens, g_meta, m_meta_tokens, v_meta_tokens, "adamw_meta")
    pk = lambda np_, cb, sn, gb, npo, dtb, al, ds, fg: _pad_rows(_pack_small(np_, cb, sn, gb, npo, dtb, al, ds, fg), _SMALL_PAD)
    w_sm = pk(norm_pre, conv_b, ssd_norm, gate_bias, norm_post, dt_bias, a_log, d_skip, fgate_bias)
    m_sm = pk(m_norm_pre, m_conv_b, m_ssd_norm, m_gate_bias, m_norm_post, m_dt_bias, m_a_log, m_d_skip, m_fgate_bias)
    v_sm = pk(v_norm_pre, v_conv_b, v_ssd_norm, v_gate_bias, v_norm_post, v_dt_bias, v_a_log, v_d_skip, v_fgate_bias)
    sm = [_unpack_small(a) for a in (g_small,) + _adamw(w_sm, g_small, m_sm, v_sm, "adamw_small")]
    for name in ("norm_pre", "conv_b", "dt_bias", "a_log", "d_skip", "ssd_norm", "fgate_bias", "gate_bias", "norm_post"):
        upd[name] = tuple(s[name] for s in sm)
    lead = ("w_in", "conv_w", "w_proj_ssd", "w_proj_att", "w_out")
    order = ("meta_tokens", "norm_pre", "w_in", "conv_w", "conv_b", "dt_bias", "a_log", "d_skip", "ssd_norm",
             "fgate_bias", "gate_bias", "w_proj_ssd", "w_proj_att", "w_out", "norm_post")
    outs = [loss, grad_x]
    for part_i in range(4):
        for name in order:
            a = upd[name][part_i]
            outs.append(a[None] if name in lead else a)
    return tuple(outs)
```

```python
import functools
import math

import jax
import jax.numpy as jnp
from jax import lax
from jax.experimental import pallas as pl
from jax.experimental.pallas import tpu as pltpu

F32 = jnp.float32
BF16 = jnp.bfloat16
HIGHEST = lax.Precision.HIGHEST

D_MODEL = 1024
N_META = 16
CHUNK = 128
PADF = CHUNK - N_META
D_SSD = 2048
H_SSD = 32
G_SSD = 4
N_STATE = 128
CONV_K = 4
CONV_DIM = D_SSD + 2 * G_SSD * N_STATE
H_ATT = 16
D_ATT = 1024
EPS = 1e-6
N_COLS = 11312

C_Z, C_XBC, C_ZA, C_Q, C_K, C_V, C_G = 0, 2048, 5120, 6144, 7168, 8192, 9216
N_MAIN = 11264
N_SMALL = 128
O_Z, O_XBC, O_DT, O_ZA, O_Q, O_K, O_V, O_F, O_G = (
    (0, 2048), (2048, 3072), (5120, 32), (5152, 1024), (6176, 1024), (7200, 1024),
    (8224, 1024), (9248, 16), (9264, 2048))

ADAM_LR, ADAM_B1, ADAM_B2, ADAM_EPS, ADAM_WD, ADAM_STEP = 0.001, 0.9, 0.999, 1e-08, 0.01, 10

VMEM_LIMIT = 56 * 1024 * 1024


def _cp(*sem):
    return pltpu.CompilerParams(dimension_semantics=sem, vmem_limit_bytes=VMEM_LIMIT)


def _tile(n, prefs):
    for p in prefs:
        if n % p == 0:
            return p
    raise ValueError(f"no tile for {n} in {prefs}")


def _iota(shape, dim):
    return lax.broadcasted_iota(jnp.int32, shape, dim)


def _sigmoid(x):
    return 1.0 / (1.0 + jnp.exp(-x))


def _softplus_tail(x):
    return jnp.log(1.0 + jnp.exp(-jnp.abs(x)))


_NN = (((1,), (0,)), ((), ()))
_NT = (((1,), (1,)), ((), ()))
_TN = (((0,), (0,)), ((), ()))


def _dot(a, b, dims=_NN):
    return lax.dot_general(a, b, dims, preferred_element_type=F32)


def _dot_exact(a, b, dims=_NN):
    return lax.dot_general(a, b, dims, precision=HIGHEST, preferred_element_type=F32)


def _hosted_call(body, *, name, grid, in_specs, out_specs, out_shape, scratch_shapes, operands, semantics,
                 exchange=None, aliases=None):
    aliases = dict(aliases or {})
    if exchange is None:
        return pl.pallas_call(body, name=name, out_shape=out_shape, grid=grid, in_specs=in_specs,
                              out_specs=out_specs, scratch_shapes=scratch_shapes, input_output_aliases=aliases,
                              compiler_params=_cp(*semantics))(*operands)
    arrays, shapes, copies, n_sems, in_place = exchange
    n_in, n_out, n_ex = len(operands), len(out_shape), len(arrays)

    def hosted(*refs):
        ex_in = refs[n_in:n_in + n_ex]
        ex_out = refs[n_in + n_ex + n_out:n_in + n_ex + n_out + n_ex]
        own = refs[:n_in] + refs[n_in + n_ex:n_in + n_ex + n_out] + refs[n_in + 2 * n_ex + n_out:-2]
        first = functools.reduce(lambda p, q: p & q, [pl.program_id(d) == 0 for d in range(len(grid))])
        last = functools.reduce(lambda p, q: p & q, [pl.program_id(d) == grid[d] - 1 for d in range(len(grid))])

        def descriptors():
            return copies(ex_out if in_place else ex_in, ex_out, refs[-2], refs[-1])

        @pl.when(first)
        def _():
            for cp in descriptors():
                cp.start()

        body(*own)

        @pl.when(last)
        def _():
            for cp in descriptors():
                cp.wait()

    return pl.pallas_call(
        hosted, name=name,
        out_shape=tuple(out_shape) + tuple(shapes),
        grid=grid,
        in_specs=list(in_specs) + [_ANY] * n_ex,
        out_specs=tuple(out_specs) + (_ANY,) * n_ex,
        input_output_aliases={**aliases, **({n_in + e: n_out + e for e in range(n_ex)} if in_place else {})},
        scratch_shapes=list(scratch_shapes) + [pltpu.SemaphoreType.DMA((n_sems,)), pltpu.SemaphoreType.DMA((n_sems,))],
        compiler_params=_cp(*(("arbitrary",) * len(grid))),
    )(*operands, *arrays)


def _matmul(a, b, mode, out_dtype, name, tm, tn, tk, exchange=None):
    if mode == "tn":
        kdim, m = a.shape
    else:
        m, kdim = a.shape
    n = b.shape[0] if mode == "nt" else b.shape[1]
    nk = kdim // tk
    dims = {"nn": _NN, "nt": _NT, "tn": _TN}[mode]
    a_spec = (pl.BlockSpec((tk, tm), lambda i, j, k: (k, i)) if mode == "tn"
              else pl.BlockSpec((tm, tk), lambda i, j, k: (i, k)))
    b_spec = (pl.BlockSpec((tn, tk), lambda i, j, k: (j, k)) if mode == "nt"
              else pl.BlockSpec((tk, tn), lambda i, j, k: (k, j)))

    def body(a_ref, b_ref, o_ref, acc_ref):
        k = pl.program_id(2)
        p = _dot(a_ref[...].astype(BF16), b_ref[...].astype(BF16), dims)
        if nk == 1:
            o_ref[...] = p.astype(out_dtype)
        else:
            @pl.when(k == 0)
            def _():
                acc_ref[...] = p

            @pl.when(k > 0)
            def _():
                acc_ref[...] += p

            @pl.when(k == nk - 1)
            def _():
                o_ref[...] = acc_ref[...].astype(out_dtype)

    out = _hosted_call(
        body, name=name,
        out_shape=(jax.ShapeDtypeStruct((m, n), out_dtype),),
        grid=(m // tm, n // tn, nk),
        in_specs=[a_spec, b_spec],
        out_specs=(pl.BlockSpec((tm, tn), lambda i, j, k: (i, j)),),
        scratch_shapes=[pltpu.VMEM((tm, tn), F32)],
        operands=(a, b), semantics=("parallel", "parallel", "arbitrary"), exchange=exchange)
    return out[0] if exchange is None else out


_CAT_BLK = 1024


def _piece_ranges(pieces):
    out, off = [], 0
    for p in pieces:
        nb = p.shape[1] // _CAT_BLK
        out.append((off, nb))
        off += nb
    return out, off


def _matmul_cat_nn(pieces, b, name, tm, rows=None, fill=None, exchange=None):
    t = pieces[0].shape[0]
    n = b.shape[1]
    ranges, nk = _piece_ranges(pieces)
    first, ni = rows if rows is not None else (0, t // tm)
    n_in = len(pieces) + 1 + (fill is not None)

    def body(*refs):
        a_refs, b_ref, o_ref, acc_ref = refs[:len(pieces)], refs[len(pieces)], refs[n_in], refs[n_in + 1]
        k = pl.program_id(1)

        @pl.when(k == 0)
        def _():
            acc_ref[...] = jnp.zeros_like(acc_ref)

        for a_ref, (off, nb) in zip(a_refs, ranges):
            @pl.when((k >= off) & (k < off + nb))
            def _(a_ref=a_ref):
                acc_ref[...] += _dot(a_ref[...], b_ref[...])

        @pl.when(k == nk - 1)
        def _():
            o_ref[...] = acc_ref[...]

    def a_spec(off, nb):
        return pl.BlockSpec((tm, _CAT_BLK), lambda i, k: (first + i, jnp.clip(k - off, 0, nb - 1)))

    in_specs = [a_spec(off, nb) for off, nb in ranges] + [pl.BlockSpec((_CAT_BLK, n), lambda i, k: (k, 0))]
    operands = list(pieces) + [b]
    if fill is not None:
        in_specs.append(_ANY)
        operands.append(fill)
    out = _hosted_call(
        body, name=name,
        out_shape=(jax.ShapeDtypeStruct((t, n), F32),),
        grid=(ni, nk),
        in_specs=in_specs,
        out_specs=(pl.BlockSpec((tm, n), lambda i, k: (first + i, 0)),),
        scratch_shapes=[pltpu.VMEM((tm, n), F32)],
        operands=operands, semantics=("parallel", "arbitrary"), exchange=exchange,
        aliases={len(pieces) + 1: 0} if fill is not None else None)
    return out if exchange is not None else out[0]


def _matmul_cat_tn(pieces, b, name, tk):
    t = pieces[0].shape[0]
    n = b.shape[1]
    ranges, nm = _piece_ranges(pieces)
    nk = t // tk

    def body(*refs):
        a_refs, b_ref, o_ref, acc_ref = refs[:len(pieces)], refs[-3], refs[-2], refs[-1]
        m = pl.program_id(0)
        k = pl.program_id(1)

        @pl.when(k == 0)
        def _():
            acc_ref[...] = jnp.zeros_like(acc_ref)

        for a_ref, (off, nb) in zip(a_refs, ranges):
            @pl.when((m >= off) & (m < off + nb))
            def _(a_ref=a_ref):
                acc_ref[...] += _dot(a_ref[...], b_ref[...], _TN)

        @pl.when(k == nk - 1)
        def _():
            o_ref[...] = acc_ref[...]

    def a_spec(off, nb):
        def index(m, k):
            mine = (m >= off) & (m < off + nb)
            return jnp.where(mine, k, 0), jnp.clip(m - off, 0, nb - 1)
        return pl.BlockSpec((tk, _CAT_BLK), index)

    return pl.pallas_call(
        body, name=name,
        out_shape=jax.ShapeDtypeStruct((nm * _CAT_BLK, n), F32),
        grid=(nm, nk),
        in_specs=[a_spec(off, nb) for off, nb in ranges] + [pl.BlockSpec((tk, n), lambda m, k: (k, 0))],
        out_specs=pl.BlockSpec((_CAT_BLK, n), lambda m, k: (m, 0)),
        scratch_shapes=[pltpu.VMEM((_CAT_BLK, n), F32)],
        compiler_params=_cp("parallel", "arbitrary"),
    )(*pieces, b)


def _row_tile(t):
    return _tile(t, (352, 128))


def _row_tile_wide(t):
    return _tile(t, (176, 128))


def _norm1_fwd(h, g):
    t = h.shape[0]
    tm = _row_tile(t)

    def body(h_ref, g_ref, u_ref):
        x = h_ref[...]
        r = lax.rsqrt(jnp.mean(x * x, axis=-1, keepdims=True) + EPS)
        u_ref[...] = (x * r * g_ref[...]).astype(BF16)

    return pl.pallas_call(
        body, name="norm1_fwd",
        out_shape=jax.ShapeDtypeStruct((t, D_MODEL), BF16),
        grid=(t // tm,),
        in_specs=[pl.BlockSpec((tm, D_MODEL), lambda i: (i, 0)),
                  pl.BlockSpec((1, D_MODEL), lambda i: (0, 0))],
        out_specs=pl.BlockSpec((tm, D_MODEL), lambda i: (i, 0)),
        compiler_params=_cp("parallel"),
    )(h, g)


def _norm1_bwd(du_a, du_b, h, g, dy):
    t = h.shape[0]
    tm = _row_tile(t)

    def body(a_ref, b_ref, h_ref, g_ref, dy_ref, dh_ref, dg_ref):
        i = pl.program_id(0)
        x = h_ref[...]
        du = a_ref[...] + b_ref[...]
        r = lax.rsqrt(jnp.mean(x * x, axis=-1, keepdims=True) + EPS)
        gdu = du * g_ref[...]
        dh_ref[...] = dy_ref[...] + r * (gdu - x * (r * r) * jnp.mean(gdu * x, axis=-1, keepdims=True))
        part = jnp.sum(du * x * r, axis=0, keepdims=True)

        @pl.when(i == 0)
        def _():
            dg_ref[...] = part

        @pl.when(i > 0)
        def _():
            dg_ref[...] += part

    row = pl.BlockSpec((tm, D_MODEL), lambda i: (i, 0))
    vec = pl.BlockSpec((1, D_MODEL), lambda i: (0, 0))
    return pl.pallas_call(
        body, name="norm1_bwd",
        out_shape=(jax.ShapeDtypeStruct((t, D_MODEL), F32), jax.ShapeDtypeStruct((1, D_MODEL), F32)),
        grid=(t // tm,),
        in_specs=[row, row, row, vec, row],
        out_specs=(row, vec),
        compiler_params=_cp("arbitrary"),
    )(du_a, du_b, h, g, dy)


def _small_fwd(small, bias_row):
    t = small.shape[0]
    rt = _tile(t, (384, 128))

    def body(s_ref, b_ref, o_ref, carry_ref):
        c = pl.program_id(0)

        @pl.when(c == 0)
        def _():
            carry_ref[...] = jnp.zeros_like(carry_ref)

        x = s_ref[...] + b_ref[...]
        lane = _iota((rt, N_SMALL), 1)
        valid = (c * rt + _iota((rt, N_SMALL), 0)) >= PADF
        tail = _softplus_tail(x)
        dt = jnp.where(valid & (lane < H_SSD), jnp.maximum(x, 0.0) + tail, 0.0)
        lf = jnp.where(valid & (lane >= H_SSD) & (lane < H_SSD + H_ATT), jnp.minimum(x, 0.0) - tail, 0.0)
        tri = (_iota((rt, rt), 0) >= _iota((rt, rt), 1)).astype(F32)
        cs = _dot_exact(tri, lf) + carry_ref[...]
        carry_ref[...] = cs[rt - 1:rt, :]
        o_ref[...] = dt + cs

    return pl.pallas_call(
        body, name="small_fwd",
        out_shape=jax.ShapeDtypeStruct((t, N_SMALL), F32),
        grid=(t // rt,),
        in_specs=[pl.BlockSpec((rt, N_SMALL), lambda c: (c, 0)),
                  pl.BlockSpec((1, N_SMALL), lambda c: (0, 0))],
        out_specs=pl.BlockSpec((rt, N_SMALL), lambda c: (c, 0)),
        scratch_shapes=[pltpu.VMEM((1, N_SMALL), F32)],
        compiler_params=_cp("arbitrary"),
    )(small, bias_row)


def _small_bwd(dsm, small, bias_row):
    t = small.shape[0]
    rt = _tile(t, (384, 128))
    nc = t // rt

    def body(d_ref, s_ref, b_ref, o_ref, db_ref, carry_ref):
        step = pl.program_id(0)
        c = nc - 1 - step

        @pl.when(step == 0)
        def _():
            carry_ref[...] = jnp.zeros_like(carry_ref)
            db_ref[...] = jnp.zeros_like(db_ref)

        x = s_ref[...] + b_ref[...]
        d = d_ref[...]
        lane = _iota((rt, N_SMALL), 1)
        valid = (c * rt + _iota((rt, N_SMALL), 0)) >= PADF
        is_dt = lane < H_SSD
        is_f = (lane >= H_SSD) & (lane < H_SSD + H_ATT)
        triu = (_iota((rt, rt), 1) >= _iota((rt, rt), 0)).astype(F32)
        dc = jnp.where(is_f, d, 0.0)
        dlf = _dot_exact(triu, dc) + carry_ref[...]
        carry_ref[...] = dlf[0:1, :]
        sg = _sigmoid(x)
        out = jnp.where(valid & is_dt, d * sg, 0.0) + jnp.where(valid & is_f, dlf * (1.0 - sg), 0.0)
        o_ref[...] = out.astype(BF16)
        db_ref[...] += jnp.sum(out, axis=0, keepdims=True)

    blk = pl.BlockSpec((rt, N_SMALL), lambda s: (nc - 1 - s, 0))
    vec = pl.BlockSpec((1, N_SMALL), lambda s: (0, 0))
    return pl.pallas_call(
        body, name="small_bwd",
        out_shape=(jax.ShapeDtypeStruct((t, N_SMALL), BF16), jax.ShapeDtypeStruct((1, N_SMALL), F32)),
        grid=(nc,),
        in_specs=[blk, blk, vec],
        out_specs=(blk, vec),
        scratch_shapes=[pltpu.VMEM((1, N_SMALL), F32)],
        compiler_params=_cp("arbitrary"),
    )(dsm, small, bias_row)


_CONV_TC = 1024
_XBC_BLK = C_XBC // _CONV_TC


def _shift_down(cur, prev8, j):
    rc = pltpu.roll(cur, j, 0)
    rid = _iota(prev8.shape, 0)
    top = jnp.where(rid < j, pltpu.roll(prev8, j, 0), rc[0:8, :])
    return top if cur.shape[0] == 8 else jnp.concatenate([top, rc[8:, :]], axis=0)


def _shift_up(cur, next8, j):
    n = cur.shape[0]
    ru = pltpu.roll(cur, n - j, 0)
    rid = _iota(next8.shape, 0)
    bot = jnp.where(rid >= 8 - j, pltpu.roll(next8, 8 - j, 0), ru[n - 8:, :])
    return jnp.concatenate([ru[:n - 8, :], bot], axis=0)


def _conv_taps(cur, prev, w, b):
    taps = [cur] + [_shift_down(cur, prev, j) for j in (1, 2, 3)]
    acc = b + taps[0] * w[3:4, :]
    for j in (1, 2, 3):
        acc = acc + taps[j] * w[3 - j:4 - j, :]
    return acc, taps


def _conv_pre(x_ref, p_ref, w_ref, b_ref, i):
    return _conv_taps(x_ref[...], jnp.where(i > 0, p_ref[...], 0.0), w_ref[...], b_ref[...])


def _dsilu(d, acc):
    sg = _sigmoid(acc)
    return d * sg * (1.0 + acc * (1.0 - sg))


def _conv_fwd(proj, conv_w, conv_b):
    t = proj.shape[0]
    tr = _row_tile(t)

    def body(x_ref, p_ref, w_ref, b_ref, o_ref):
        i = pl.program_id(0)
        acc, _ = _conv_pre(x_ref, p_ref, w_ref, b_ref, i)
        valid = (i * tr + _iota(acc.shape, 0)) >= PADF
        o_ref[...] = jnp.where(valid, acc * _sigmoid(acc), 0.0)

    return pl.pallas_call(
        body, name="conv_fwd",
        out_shape=jax.ShapeDtypeStruct((t, CONV_DIM), F32),
        grid=(t // tr, CONV_DIM // _CONV_TC),
        in_specs=[pl.BlockSpec((tr, _CONV_TC), lambda i, j: (i, _XBC_BLK + j)),
                  pl.BlockSpec((8, _CONV_TC), lambda i, j: (jnp.maximum(i * (tr // 8) - 1, 0), _XBC_BLK + j)),
                  pl.BlockSpec((CONV_K, _CONV_TC), lambda i, j: (0, j)),
                  pl.BlockSpec((1, _CONV_TC), lambda i, j: (0, j))],
        out_specs=pl.BlockSpec((tr, _CONV_TC), lambda i, j: (i, j)),
        compiler_params=_cp("parallel", "parallel"),
    )(proj, proj, conv_w, conv_b)


def _conv_bwd(dxbc, proj, conv_w, conv_b):
    t = proj.shape[0]
    tr = _row_tile(t)
    n_tiles = t // tr
    last8 = t // 8 - 1

    def body(d_ref, dn_ref, x_ref, p_ref, xn_ref, w_ref, b_ref, dx_ref, dw_ref, db_ref):
        i = pl.program_id(1)
        w = w_ref[...]
        b = b_ref[...]
        cur = x_ref[...]
        acc, taps = _conv_taps(cur, jnp.where(i > 0, p_ref[...], 0.0), w, b)
        valid = (i * tr + _iota(acc.shape, 0)) >= PADF
        da = jnp.where(valid, _dsilu(d_ref[...], acc), 0.0)
        acc_n, _ = _conv_taps(xn_ref[...], cur[tr - 8:, :], w, b)
        da_n = jnp.where(i < n_tiles - 1, _dsilu(dn_ref[...], acc_n), 0.0)
        dx = da * w[3:4, :]
        for j in (1, 2, 3):
            dx = dx + _shift_up(da, da_n, j) * w[3 - j:4 - j, :]
        dx_ref[...] = dx.astype(BF16)
        dw = jnp.concatenate([jnp.sum(da * taps[3 - k], axis=0, keepdims=True) for k in range(CONV_K)], axis=0)
        db = jnp.sum(da, axis=0, keepdims=True)

        @pl.when(i == 0)
        def _():
            dw_ref[...] = dw
            db_ref[...] = db

        @pl.when(i > 0)
        def _():
            dw_ref[...] += dw
            db_ref[...] += db

    nxt8 = lambda i: jnp.minimum((i + 1) * (tr // 8), last8)
    return pl.pallas_call(
        body, name="conv_bwd",
        out_shape=(jax.ShapeDtypeStruct((t, CONV_DIM), BF16),
                   jax.ShapeDtypeStruct((CONV_K, CONV_DIM), F32),
                   jax.ShapeDtypeStruct((1, CONV_DIM), F32)),
        grid=(CONV_DIM // _CONV_TC, n_tiles),
        in_specs=[pl.BlockSpec((tr, _CONV_TC), lambda j, i: (i, j)),
                  pl.BlockSpec((8, _CONV_TC), lambda j, i: (nxt8(i), j)),
                  pl.BlockSpec((tr, _CONV_TC), lambda j, i: (i, _XBC_BLK + j)),
                  pl.BlockSpec((8, _CONV_TC), lambda j, i: (jnp.maximum(i * (tr // 8) - 1, 0), _XBC_BLK + j)),
                  pl.BlockSpec((8, _CONV_TC), lambda j, i: (nxt8(i), _XBC_BLK + j)),
                  pl.BlockSpec((CONV_K, _CONV_TC), lambda j, i: (0, j)),
                  pl.BlockSpec((1, _CONV_TC), lambda j, i: (0, j))],
        out_specs=(pl.BlockSpec((tr, _CONV_TC), lambda j, i: (i, j)),
                   pl.BlockSpec((CONV_K, _CONV_TC), lambda j, i: (0, j)),
                   pl.BlockSpec((1, _CONV_TC), lambda j, i: (0, j))),
        compiler_params=_cp("parallel", "arbitrary"),
    )(dxbc, dxbc, proj, proj, proj, conv_w, conv_b)


_GW = D_SSD // G_SSD


def _ssd_prelude(dt_ref, a_ref, e_scr, es_scr, dte_scr):
    r0 = _iota((CHUNK, CHUNK), 0)
    r1 = _iota((CHUNK, CHUNK), 1)
    dt = jnp.where(r1 < H_SSD, dt_ref[...], 0.0)
    adt = dt * a_ref[...]
    acs = _dot_exact((r0 >= r1).astype(F32), adt)
    acs_t = acs.T
    alast = acs[CHUNK - 1:CHUNK, :]
    exp_a = jnp.exp(acs)
    dec_s = jnp.exp(alast - acs)
    lo = r1 < 64
    for j in range(H_SSD // 2):
        sl = slice(CHUNK * j, CHUNK * (j + 1))
        e_scr[:, sl] = jnp.where(lo, exp_a[:, 2 * j:2 * j + 1], exp_a[:, 2 * j + 1:2 * j + 2])
        es_scr[:, sl] = jnp.where(lo, dec_s[:, 2 * j:2 * j + 1], dec_s[:, 2 * j + 1:2 * j + 2])
        dte_scr[:, sl] = jnp.where(lo, dt[:, 2 * j:2 * j + 1], dt[:, 2 * j + 1:2 * j + 2])
    return dt, acs, acs_t, r0, r1, lo


def _chunk_decay_rows(acs_t, g):
    cd_t = jnp.exp(acs_t[:, CHUNK - 1:CHUNK])
    return jnp.concatenate(
        [jnp.broadcast_to(cd_t[8 * g + hh:8 * g + hh + 1, :], (64, N_STATE)) for hh in range(8)], axis=0)


def _ssd_fwd(xbc, dtlf, a_row, dsk_row):
    t = xbc.shape[0]
    nc = t // CHUNK

    def body(xs_ref, b_ref, c_ref, dt_ref, a_ref, dsk_ref, y_ref, hin_ref, h_scr, e_scr, es_scr, dte_scr):
        c = pl.program_id(0)

        @pl.when(c == 0)
        def _():
            h_scr[...] = jnp.zeros_like(h_scr)

        dt, acs, acs_t, r0, r1, lo = _ssd_prelude(dt_ref, a_ref, e_scr, es_scr, dte_scr)
        causal = r0 >= r1
        for g in range(G_SSD):
            gs = slice(_GW * g, _GW * (g + 1))
            bg = b_ref[:, N_STATE * g:N_STATE * (g + 1)].astype(BF16)
            cg = c_ref[:, N_STATE * g:N_STATE * (g + 1)].astype(BF16)
            cb = _dot(cg, bg, _NT)
            hg = h_scr[gs, :]
            hin_ref[0, gs, :] = hg
            xg = xs_ref[:, gs] * dte_scr[:, gs]
            yoff = _dot(cg, hg.astype(BF16), _NT) * e_scr[:, gs]
            st = _dot((xg * es_scr[:, gs]).astype(BF16), bg, _TN)
            h_scr[gs, :] = hg * _chunk_decay_rows(acs_t, g) + st
            for jj in range(4):
                j = 4 * g + jj
                sl = slice(CHUNK * j, CHUNK * (j + 1))
                xp = xg[:, CHUNK * jj:CHUNK * (jj + 1)]
                acc = yoff[:, CHUNK * jj:CHUNK * (jj + 1)] + dsk_ref[:, sl] * xs_ref[:, sl]
                for hh in range(2):
                    h = 2 * j + hh
                    seg = acs[:, h:h + 1] - acs_t[h:h + 1, :]
                    lm = jnp.exp(jnp.where(causal, seg, -1e30))
                    m = (cb * lm).astype(BF16)
                    xh = jnp.where(lo if hh == 0 else ~lo, xp, 0.0).astype(BF16)
                    acc = acc + _dot(m, xh)
                y_ref[:, sl] = acc

    return pl.pallas_call(
        body, name="ssd_fwd",
        out_shape=(jax.ShapeDtypeStruct((t, D_SSD), F32), jax.ShapeDtypeStruct((nc, D_SSD, N_STATE), F32)),
        grid=(nc,),
        in_specs=[pl.BlockSpec((CHUNK, D_SSD), lambda c: (c, 0)),
                  pl.BlockSpec((CHUNK, _GW), lambda c: (c, 4)),
                  pl.BlockSpec((CHUNK, _GW), lambda c: (c, 5)),
                  pl.BlockSpec((CHUNK, N_SMALL), lambda c: (c, 0)),
                  pl.BlockSpec((1, N_SMALL), lambda c: (0, 0)),
                  pl.BlockSpec((1, D_SSD), lambda c: (0, 0))],
        out_specs=(pl.BlockSpec((CHUNK, D_SSD), lambda c: (c, 0)),
                   pl.BlockSpec((1, D_SSD, N_STATE), lambda c: (c, 0, 0))),
        scratch_shapes=[pltpu.VMEM((D_SSD, N_STATE), F32)] + [pltpu.VMEM((CHUNK, D_SSD), F32)] * 3,
        compiler_params=_cp("arbitrary"),
    )(xbc, xbc, xbc, dtlf, a_row, dsk_row)


def _ssd_bwd(xbc, dtlf, a_row, dsk_row, hin, dy):
    t = xbc.shape[0]
    nc = t // CHUNK

    def body(xs_ref, b_ref, c_ref, dt_ref, a_ref, dsk_ref, hin_ref, dy_ref,
             dxbc_ref, ddt_ref, da_ref, ddsk_ref, dh_scr, e_scr, es_scr, dte_scr, dx_scr, whi_scr, wlo_scr):
        step = pl.program_id(0)

        @pl.when(step == 0)
        def _():
            dh_scr[...] = jnp.zeros_like(dh_scr)
            da_ref[...] = jnp.zeros_like(da_ref)
            ddsk_ref[...] = jnp.zeros_like(ddsk_ref)

        dt, acs, acs_t, r0, r1, lo = _ssd_prelude(dt_ref, a_ref, e_scr, es_scr, dte_scr)
        causal = r0 >= r1
        lane_row = _iota((1, CHUNK), 1)
        dacs = jnp.zeros((CHUNK, CHUNK), F32)
        dacs_t = jnp.zeros((CHUNK, CHUNK), F32)
        dalast = jnp.zeros((1, CHUNK), F32)
        ddt_dir = jnp.zeros((CHUNK, CHUNK), F32)
        ddsk_ref[...] += jnp.sum(dy_ref[...] * xs_ref[...], axis=0, keepdims=True)

        def head_sums(z, pick):
            hi = z.astype(BF16)
            return _dot(hi, pick) + _dot((z - hi.astype(F32)).astype(BF16), pick)

        for g in range(G_SSD):
            gs = slice(_GW * g, _GW * (g + 1))
            pick = (jnp.right_shift(_iota((_GW, CHUNK), 0), 6) + 8 * g == _iota((_GW, CHUNK), 1)).astype(BF16)
            bg = b_ref[:, N_STATE * g:N_STATE * (g + 1)].astype(BF16)
            cg = c_ref[:, N_STATE * g:N_STATE * (g + 1)].astype(BF16)
            cb = _dot(cg, bg, _NT)
            hg = hin_ref[0, gs, :]
            hgb = hg.astype(BF16)
            dhn = dh_scr[gs, :]
            dhnb = dhn.astype(BF16)
            esg = es_scr[:, gs]
            dyg = dy_ref[:, gs]
            xsg = xs_ref[:, gs]
            xg = xsg * dte_scr[:, gs]
            dyeb = (dyg * e_scr[:, gs]).astype(BF16)
            dc = _dot(dyeb, hgb)
            dh_y = _dot(dyeb, cg, _TN)
            dxs = _dot(bg, dhnb, _NT) * esg
            db = _dot((xg * esg).astype(BF16), dhnb)
            cd = _chunk_decay_rows(acs_t, g)
            dh_scr[gs, :] = dhn * cd + dh_y
            end_state = head_sums(jnp.broadcast_to(jnp.sum(xg * dxs, axis=0, keepdims=True), (8, _GW)), pick)[0:1, :]
            carried = dhn * hg * cd
            per_head = jnp.concatenate([jnp.sum(carried[64 * hh:64 * hh + 64, :], axis=0, keepdims=True)
                                        for hh in range(8)], axis=0)
            per_head = jnp.sum(per_head, axis=1, keepdims=True)
            for hh in range(8):
                end_state = end_state + jnp.where(lane_row == 8 * g + hh, per_head[hh:hh + 1, :], 0.0)
            dalast = dalast + end_state
            dcb = jnp.zeros((CHUNK, CHUNK), F32)
            for jj in range(4):
                j = 4 * g + jj
                sl = slice(CHUNK * j, CHUNK * (j + 1))
                ps = slice(CHUNK * jj, CHUNK * (jj + 1))
                xpb = xg[:, ps].astype(BF16)
                dyp = dyg[:, ps]
                dxp = dxs[:, ps]
                for hh in range(2):
                    h = 2 * j + hh
                    ws = slice(CHUNK * (2 * jj + hh), CHUNK * (2 * jj + hh + 1))
                    seg = acs[:, h:h + 1] - acs_t[h:h + 1, :]
                    lm = jnp.exp(jnp.where(causal, seg, -1e30))
                    mf = cb * lm
                    dyh = jnp.where(lo if hh == 0 else ~lo, dyp, 0.0).astype(BF16)
                    gm = _dot(dyh, xpb, _NT)
                    dcb = dcb + gm * lm
                    w = gm * mf
                    whi = w.astype(BF16)
                    whi_scr[:, ws] = whi
                    wlo_scr[:, ws] = (w - whi.astype(F32)).astype(BF16)
                    dacs_t = dacs_t - jnp.where(r0 == h, jnp.sum(w, axis=0, keepdims=True), 0.0)
                    dxp = dxp + _dot(mf.astype(BF16), dyh, _TN)
                dx_scr[:, sl] = dxp
            dxg = dx_scr[:, gs]
            pick_w = (jnp.right_shift(_iota((8 * CHUNK, CHUNK), 0), 7) + 8 * g == _iota((8 * CHUNK, CHUNK), 1)).astype(BF16)
            ch = _dot(cg, hgb, _NT)
            dacs = (dacs + _dot(whi_scr[...], pick_w) + _dot(wlo_scr[...], pick_w)
                    + head_sums(dyg * e_scr[:, gs] * ch - xg * dxs, pick))
            ddt_dir = ddt_dir + head_sums(dxg * xsg, pick)
            dcbb = dcb.astype(BF16)
            dxbc_ref[:, D_SSD + N_STATE * g:D_SSD + N_STATE * (g + 1)] = db + _dot(dcbb, cg, _TN)
            dxbc_ref[:, D_SSD + _GW + N_STATE * g:D_SSD + _GW + N_STATE * (g + 1)] = dc + _dot(dcbb, bg)
        dxbc_ref[:, 0:D_SSD] = dx_scr[...] * dte_scr[...] + dsk_ref[...] * dy_ref[...]
        dacs = dacs + dacs_t.T + jnp.where(r0 == CHUNK - 1, dalast, 0.0)
        dadt = _dot_exact((r1 >= r0).astype(F32), dacs)
        ddt_ref[...] = dadt * a_ref[...] + ddt_dir
        da_ref[...] += jnp.sum(dadt * dt, axis=0, keepdims=True)

    rev = lambda s: (nc - 1 - s, 0)
    return pl.pallas_call(
        body, name="ssd_bwd",
        out_shape=(jax.ShapeDtypeStruct((t, CONV_DIM), F32), jax.ShapeDtypeStruct((t, N_SMALL), F32),
                   jax.ShapeDtypeStruct((1, N_SMALL), F32), jax.ShapeDtypeStruct((1, D_SSD), F32)),
        grid=(nc,),
        in_specs=[pl.BlockSpec((CHUNK, D_SSD), rev),
                  pl.BlockSpec((CHUNK, _GW), lambda s: (nc - 1 - s, 4)),
                  pl.BlockSpec((CHUNK, _GW), lambda s: (nc - 1 - s, 5)),
                  pl.BlockSpec((CHUNK, N_SMALL), rev),
                  pl.BlockSpec((1, N_SMALL), lambda s: (0, 0)),
                  pl.BlockSpec((1, D_SSD), lambda s: (0, 0)),
                  pl.BlockSpec((1, D_SSD, N_STATE), lambda s: (nc - 1 - s, 0, 0)),
                  pl.BlockSpec((CHUNK, D_SSD), rev)],
        out_specs=(pl.BlockSpec((CHUNK, CONV_DIM), rev),
                   pl.BlockSpec((CHUNK, N_SMALL), rev),
                   pl.BlockSpec((1, N_SMALL), lambda s: (0, 0)),
                   pl.BlockSpec((1, D_SSD), lambda s: (0, 0))),
        scratch_shapes=([pltpu.VMEM((D_SSD, N_STATE), F32)] + [pltpu.VMEM((CHUNK, D_SSD), F32)] * 4
                        + [pltpu.VMEM((CHUNK, 8 * CHUNK), BF16)] * 2),
        compiler_params=_cp("arbitrary"),
    )(xbc, xbc, xbc, dtlf, a_row, dsk_row, hin, dy)


_NPAIR = H_ATT // 2
_QB, _KB, _VB = C_Q // 128, C_K // 128, C_V // 128
_SCALE = 1.0 / math.sqrt(64.0)


def _attn_blocks(t):
    return _tile(t, (1408, 384, 256, 128)), _tile(t, (384, 128))


def _split3(c):
    hi = c.astype(BF16).astype(F32)
    rest = c - hi
    mid = rest.astype(BF16).astype(F32)
    return hi, mid, rest - mid


def _head_lanes(lane, hh):
    return (lane < 64, 64) if hh == 0 else (lane >= 64, 0)


def _q_operand(q, cq, lane, hh):
    sel, first = _head_lanes(lane, hh)
    out = jnp.where(sel, q, 0.0)
    for n, col in enumerate(_split3(cq) + (1.0, 1.0, 1.0)):
        out = jnp.where(lane == first + n, col, out)
    return out.astype(BF16)


def _k_operand(k, ck, lane, hh):
    sel, first = _head_lanes(lane, hh)
    hi, mid, lo = _split3(ck)
    out = jnp.where(sel, k, 0.0)
    for n, col in enumerate((1.0, 1.0, 1.0, -hi, -mid, -lo)):
        out = jnp.where(lane == first + n, col, out)
    return out.astype(BF16)


def _sum_operand(x, lane, hh, at):
    sel, first = _head_lanes(lane, hh)
    return jnp.where(sel, x, jnp.where(lane == first + at, 1.0, 0.0)).astype(BF16)


_C_FILLER = 2.0 ** 30
_SKIP_STEP = 256


def _query_skips(bq):
    firsts = list(range(0, bq, _SKIP_STEP))
    far = 1 << 30
    return [(q0 if n else -far, firsts[n + 1] if n + 1 < len(firsts) else far, q0) for n, q0 in enumerate(firsts)]


def _attn_fwd(proj, c_col):
    t = proj.shape[0]
    bq, bk = _attn_blocks(t)
    nq, nk = t // bq, t // bk
    rs = 32

    def last_kv(i):
        return (i * bq + bq - 1) // bk

    def body(q_ref, k_ref, v_ref, cq_ref, ck_ref, o_ref, lse_ref, p_ref, mrun_ref, qs_scr, s_scr, m_scr, acc_scr):
        i = pl.program_id(1)
        kk = pl.program_id(2)
        lane_q = _iota((bq, 128), 1)

        @pl.when(kk == 0)
        def _():
            m_scr[...] = jnp.full_like(m_scr, -1e30)
            acc_scr[...] = jnp.zeros_like(acc_scr)
            q = q_ref[...] * _SCALE
            cq = cq_ref[0]
            for hh in range(2):
                qs_scr[hh] = _q_operand(q, cq[:, hh:hh + 1], lane_q, hh)

        def step(masked, q0):
            nqc = bq - q0
            lane_k = _iota((bk, 128), 1)
            k = k_ref[...]
            v = v_ref[...]
            ck = ck_ref[0]
            ahead = _iota((rs, nqc), 0) - _iota((rs, nqc), 1) - q0
            vss = []
            for hh in range(2):
                sel, first = _head_lanes(lane_k, hh)
                ks = _k_operand(k, ck[:, hh:hh + 1], lane_k, hh)
                vss.append(jnp.where(sel, v, jnp.where(lane_k == first, 1.0, 0.0)).astype(BF16))
                s_scr[hh, :, q0:] = _dot(ks, qs_scr[hh, q0:, :], _NT)
            for hh in range(2):
                vs = vss[hh]

                def block_max(r, mx):
                    rows = pl.ds(pl.multiple_of(r * rs, rs), rs)
                    s = s_scr[hh, rows, q0:]
                    if masked:
                        s = jnp.where(ahead <= i * bq - kk * bk - r * rs, s, -1e30)
                        s_scr[hh, rows, q0:] = s
                    return jnp.maximum(mx, s)

                mx = lax.fori_loop(0, bk // rs, block_max, jnp.full((rs, nqc), -1e30, F32), unroll=True)
                m_old = m_scr[hh, :, q0:]
                m_new = jnp.maximum(m_old, jnp.max(mx, axis=0, keepdims=True))
                m_scr[hh, :, q0:] = m_new
                mrun_ref[0, hh:hh + 1, q0:] = m_new

                def probs(r, carry):
                    rows = pl.ds(pl.multiple_of(r * rs, rs), rs)
                    p_ref[0, hh, rows, q0:] = jnp.exp(s_scr[hh, rows, q0:] - m_new).astype(BF16)
                    return carry

                lax.fori_loop(0, bk // rs, probs, 0, unroll=True)
                acc_scr[hh, :, q0:] = (acc_scr[hh, :, q0:] * jnp.exp(m_old - m_new)
                                       + _dot(vs, p_ref[0, hh, :, q0:], _TN))

        active = kk <= last_kv(i)
        ahead_by = kk * bk - i * bq
        for lo, hi, q0 in _query_skips(bq):
            @pl.when(active & (ahead_by + bk - 1 > 0) & (ahead_by >= lo) & (ahead_by < hi))
            def _(q0=q0):
                step(True, q0)

        @pl.when(active & jnp.logical_not(ahead_by + bk - 1 > 0))
        def _():
            step(False, 0)

        @pl.when(kk == nk - 1)
        def _():
            a = acc_scr[0]
            b = acc_scr[1]
            la = a[64:65, :]
            lb = b[0:1, :]
            o_ref[...] = jnp.where(lane_q < 64, (a / la).T, (b / lb).T)
            lse_ref[0] = jnp.concatenate([m_scr[0] + jnp.log(la), m_scr[1] + jnp.log(lb)], axis=0)

    kvi = lambda i, kk: jnp.minimum(kk, last_kv(i))
    kv = lambda off: pl.BlockSpec((bk, 128), lambda j, i, kk: (kvi(i, kk), off + j))
    blk = lambda j, i, kk: (j * nq + i) * nk + kvi(i, kk)
    return pl.pallas_call(
        body, name="attn_fwd",
        out_shape=(jax.ShapeDtypeStruct((t, D_ATT), F32), jax.ShapeDtypeStruct((_NPAIR, 2, t), F32),
                   jax.ShapeDtypeStruct((_NPAIR * nq * nk, 2, bk, bq), BF16),
                   jax.ShapeDtypeStruct((_NPAIR * nq * nk, 2, bq), F32)),
        grid=(_NPAIR, nq, nk),
        in_specs=[pl.BlockSpec((bq, 128), lambda j, i, kk: (i, _QB + j)),
                  kv(_KB), kv(_VB),
                  pl.BlockSpec((1, bq, 2), lambda j, i, kk: (j, i, 0)),
                  pl.BlockSpec((1, bk, 2), lambda j, i, kk: (j, kvi(i, kk), 0))],
        out_specs=(pl.BlockSpec((bq, 128), lambda j, i, kk: (i, j)),
                   pl.BlockSpec((1, 2, bq), lambda j, i, kk: (j, 0, i)),
                   pl.BlockSpec((1, 2, bk, bq), lambda j, i, kk: (blk(j, i, kk), 0, 0, 0)),
                   pl.BlockSpec((1, 2, bq), lambda j, i, kk: (blk(j, i, kk), 0, 0))),
        scratch_shapes=[pltpu.VMEM((2, bq, 128), BF16), pltpu.VMEM((2, bk, bq), F32),
                        pltpu.VMEM((2, 1, bq), F32), pltpu.VMEM((2, 128, bq), F32)],
        compiler_params=_cp("parallel", "parallel", "arbitrary"),
    )(proj, proj, proj, c_col, c_col)


def _attn_bwd(proj, lse_row, dl_row, do, p_blocks, m_run, exchange=None):
    t = proj.shape[0]
    bq, bk = _attn_blocks(t)
    nq, nk = t // bq, t // bk
    rs = 16

    def first_q(kk):
        return (kk * bk) // bq

    def body(q_ref, k_ref, v_ref, lse_ref, dl_ref, do_ref, pblk_ref, mrun_ref,
             dq_ref, dk_ref, dv_ref, dck_ref, dcq_ref,
             qs_scr, doh_scr, ks_scr, dp_scr, p_scr, ds_scr, dq_scr, dk_scr, dv_scr):
        kk = pl.program_id(1)
        i = pl.program_id(2)
        lane_q = _iota((bq, 128), 1)
        lane_k = _iota((bk, 128), 1)
        qrows = pl.ds(pl.multiple_of(i * bq, 128), bq)

        @pl.when(kk == 0)
        def _():
            q = q_ref[...] * _SCALE
            do_ = do_ref[...]
            for hh in range(2):
                qs_scr[hh, qrows, :] = _sum_operand(q, lane_q, hh, 3)
                doh_scr[hh, qrows, :] = jnp.where(_head_lanes(lane_q, hh)[0], do_, 0.0).astype(BF16)
                dq_scr[hh, i] = jnp.zeros((128, bq), F32)

        @pl.when(i == 0)
        def _():
            dk_scr[...] = jnp.zeros_like(dk_scr)
            dv_scr[...] = jnp.zeros_like(dv_scr)
            k = k_ref[...]
            for hh in range(2):
                ks_scr[hh] = _sum_operand(k, lane_k, hh, 0)

        def step(q0):
            seen = pl.ds(pl.multiple_of(i * bq + q0, 128), bq - q0)
            v16 = v_ref[...].astype(BF16)
            dl = dl_ref[0, :, q0:]
            rescale = jnp.exp(mrun_ref[0, :, q0:] - lse_ref[0, :, q0:])
            for hh in range(2):
                dp_scr[hh, :, q0:] = _dot(v16, doh_scr[hh, seen, :], _NT)
            for hh in range(2):
                qs = qs_scr[hh, seen, :]
                doh = doh_scr[hh, seen, :]

                def strip(r, carry):
                    rows = pl.ds(pl.multiple_of(r * rs, rs), rs)
                    p = pblk_ref[0, hh, rows, q0:].astype(F32) * rescale[hh:hh + 1, :]
                    p_scr[hh, rows, q0:] = p.astype(BF16)
                    ds_scr[hh, rows, q0:] = (p * (dp_scr[hh, rows, q0:] - dl[hh:hh + 1, :])).astype(BF16)
                    return carry

                lax.fori_loop(0, bk // rs, strip, 0, unroll=True)
                dv_scr[...] += _dot(p_scr[hh, :, q0:], doh)
                dk_scr[hh] += _dot(ds_scr[hh, :, q0:], qs)
                dq_scr[hh, i, :, q0:] += _dot(ks_scr[hh], ds_scr[hh, :, q0:], _TN)

        ahead_by = kk * bk - i * bq
        for lo, hi, q0 in _query_skips(bq):
            @pl.when((i >= first_q(kk)) & (ahead_by >= lo) & (ahead_by < hi))
            def _(q0=q0):
                step(q0)

        @pl.when(i == nq - 1)
        def _():
            dka = dk_scr[0]
            dkb = dk_scr[1]
            dk_ref[...] = jnp.where(lane_k < 64, dka, dkb).astype(BF16)
            dv_ref[...] = dv_scr[...].astype(BF16)
            dck_ref[0] = -jnp.where(_iota((bk, 2), 1) == 0, dka[:, 67:68], dkb[:, 3:4])

        @pl.when((kk == nk - 1) & (i == nq - 1))
        def _():
            for ii in range(nq):
                cols = slice(ii * bq, (ii + 1) * bq)
                dqa = dq_scr[0, ii]
                dqb = dq_scr[1, ii]
                dq_ref[cols, :] = (jnp.where(lane_q < 64, dqa.T, dqb.T) * _SCALE).astype(BF16)
                dcq_ref[0, :, cols] = jnp.concatenate([dqa[64:65, :], dqb[0:1, :]], axis=0)

    qi = lambda kk, i: jnp.where(kk == 0, i, nq - 1)
    qspec = lambda off: pl.BlockSpec((bq, 128), lambda j, kk, i: (qi(kk, i), off + j))
    kspec = lambda off: pl.BlockSpec((bk, 128), lambda j, kk, i: (kk, off + j))
    rowspec = pl.BlockSpec((1, 2, bq), lambda j, kk, i: (j, 0, jnp.maximum(i, first_q(kk))))
    blk = lambda j, kk, i: (j * nq + jnp.maximum(i, first_q(kk))) * nk + kk
    return _hosted_call(
        body, name="attn_bwd",
        out_shape=(jax.ShapeDtypeStruct((t, D_ATT), BF16), jax.ShapeDtypeStruct((t, D_ATT), BF16),
                   jax.ShapeDtypeStruct((t, D_ATT), BF16), jax.ShapeDtypeStruct((_NPAIR, t, 2), F32),
                   jax.ShapeDtypeStruct((_NPAIR, 2, t), F32)),
        grid=(_NPAIR, nk, nq),
        in_specs=[qspec(_QB), kspec(_KB), kspec(_VB),
                  rowspec, rowspec, qspec(0),
                  pl.BlockSpec((1, 2, bk, bq), lambda j, kk, i: (blk(j, kk, i), 0, 0, 0)),
                  pl.BlockSpec((1, 2, bq), lambda j, kk, i: (blk(j, kk, i), 0, 0))],
        out_specs=(pl.BlockSpec((t, 128), lambda j, kk, i: (0, j)),
                   pl.BlockSpec((bk, 128), lambda j, kk, i: (kk, j)),
                   pl.BlockSpec((bk, 128), lambda j, kk, i: (kk, j)),
                   pl.BlockSpec((1, bk, 2), lambda j, kk, i: (j, kk, 0)),
                   pl.BlockSpec((1, 2, t), lambda j, kk, i: (j, 0, 0))),
        scratch_shapes=[pltpu.VMEM((2, t, 128), BF16), pltpu.VMEM((2, t, 128), BF16), pltpu.VMEM((2, bk, 128), BF16),
                        pltpu.VMEM((2, bk, bq), F32),
                        pltpu.VMEM((2, bk, bq), BF16), pltpu.VMEM((2, bk, bq), BF16),
                        pltpu.VMEM((2, nq, 128, bq), F32), pltpu.VMEM((2, bk, 128), F32), pltpu.VMEM((bk, 128), F32)],
        operands=(proj, proj, proj, lse_row, dl_row, do, p_blocks, m_run),
        semantics=("parallel", "arbitrary", "arbitrary"), exchange=exchange)


def _premerge_fwd(y, o, proj, gamma):
    t = y.shape[0]
    tm = _row_tile_wide(t)

    def body(y_ref, z_ref, o_ref, za_ref, g_ref, ys_ref, ya_ref):
        z = z_ref[...]
        u = y_ref[...] * (z * _sigmoid(z))
        for g in range(G_SSD):
            gs = slice(_GW * g, _GW * (g + 1))
            ug = u[:, gs]
            r = lax.rsqrt(jnp.mean(ug * ug, axis=-1, keepdims=True) + EPS)
            ys_ref[:, gs] = (ug * r * g_ref[:, gs]).astype(BF16)
        za = za_ref[...]
        ya_ref[...] = (o_ref[...] * (za * _sigmoid(za))).astype(BF16)

    return pl.pallas_call(
        body, name="premerge_fwd",
        out_shape=(jax.ShapeDtypeStruct((t, D_SSD), BF16), jax.ShapeDtypeStruct((t, D_ATT), BF16)),
        grid=(t // tm,),
        in_specs=[pl.BlockSpec((tm, D_SSD), lambda i: (i, 0)),
                  pl.BlockSpec((tm, D_SSD), lambda i: (i, C_Z // D_SSD)),
                  pl.BlockSpec((tm, D_ATT), lambda i: (i, 0)),
                  pl.BlockSpec((tm, D_ATT), lambda i: (i, C_ZA // D_ATT)),
                  pl.BlockSpec((1, D_SSD), lambda i: (0, 0))],
        out_specs=(pl.BlockSpec((tm, D_SSD), lambda i: (i, 0)), pl.BlockSpec((tm, D_ATT), lambda i: (i, 0))),
        compiler_params=_cp("parallel"),
    )(y, proj, o, proj, gamma)


def _premerge_bwd(dys, dya, y, o, proj, gamma, exchange=None):
    t = y.shape[0]
    tm = _row_tile_wide(t)

    def body(dys_ref, dya_ref, y_ref, z_ref, o_ref, za_ref, g_ref, dy_ref, dz_ref, do_ref, dza_ref, dg_ref, dl_ref):
        i = pl.program_id(0)
        z = z_ref[...]
        sz = _sigmoid(z)
        silu = z * sz
        dsilu = sz * (1.0 + z * (1.0 - sz))
        yv = y_ref[...]
        u = yv * silu
        parts = []
        for g in range(G_SSD):
            gs = slice(_GW * g, _GW * (g + 1))
            ug = u[:, gs]
            r = lax.rsqrt(jnp.mean(ug * ug, axis=-1, keepdims=True) + EPS)
            n = ug * r
            dout = dys_ref[:, gs]
            dn = dout * g_ref[:, gs]
            du = r * (dn - n * jnp.mean(dn * n, axis=-1, keepdims=True))
            dy_ref[:, gs] = du * silu[:, gs]
            dz_ref[:, gs] = (du * yv[:, gs] * dsilu[:, gs]).astype(BF16)
            parts.append(jnp.sum(dout * n, axis=0, keepdims=True))
        dg = jnp.concatenate(parts, axis=1)
        za = za_ref[...]
        sa = _sigmoid(za)
        dya_ = dya_ref[...]
        ov = o_ref[...]
        do = dya_ * (za * sa)
        do_ref[...] = do
        dza_ref[...] = (dya_ * ov * (sa * (1.0 + za * (1.0 - sa)))).astype(BF16)
        pick = (jnp.right_shift(_iota((D_ATT, 128), 0), 6) == _iota((D_ATT, 128), 1)).astype(F32)
        dl_ref[...] = _dot_exact(do * ov, pick)

        @pl.when(i == 0)
        def _():
            dg_ref[...] = dg

        @pl.when(i > 0)
        def _():
            dg_ref[...] += dg

    ssd = pl.BlockSpec((tm, D_SSD), lambda i: (i, 0))
    att = pl.BlockSpec((tm, D_ATT), lambda i: (i, 0))
    vec = pl.BlockSpec((1, D_SSD), lambda i: (0, 0))
    return _hosted_call(
        body, name="premerge_bwd",
        out_shape=(jax.ShapeDtypeStruct((t, D_SSD), F32), jax.ShapeDtypeStruct((t, D_SSD), BF16),
                   jax.ShapeDtypeStruct((t, D_ATT), F32), jax.ShapeDtypeStruct((t, D_ATT), BF16),
                   jax.ShapeDtypeStruct((1, D_SSD), F32), jax.ShapeDtypeStruct((t, 128), F32)),
        grid=(t // tm,),
        in_specs=[ssd, att, ssd, pl.BlockSpec((tm, D_SSD), lambda i: (i, C_Z // D_SSD)), att,
                  pl.BlockSpec((tm, D_ATT), lambda i: (i, C_ZA // D_ATT)), vec],
        out_specs=(ssd, ssd, att, att, vec, pl.BlockSpec((tm, 128), lambda i: (i, 0))),
        scratch_shapes=[],
        operands=(dys, dya, y, proj, o, proj, gamma), semantics=("arbitrary",), exchange=exchange)


_G_BLK = C_G // D_MODEL


def _merge_fwd(a, b, proj, gate_bias):
    t = a.shape[0]
    tm = _row_tile(t)

    def body(a_ref, b_ref, gs_ref, ga_ref, bias_ref, m_ref):
        g_ssd = _sigmoid(gs_ref[...] + bias_ref[:, 0:D_MODEL])
        g_att = _sigmoid(ga_ref[...] + bias_ref[:, D_MODEL:2 * D_MODEL])
        m_ref[...] = (g_ssd * a_ref[...] + g_att * b_ref[...]).astype(BF16)

    row = pl.BlockSpec((tm, D_MODEL), lambda i: (i, 0))
    return pl.pallas_call(
        body, name="merge_fwd",
        out_shape=jax.ShapeDtypeStruct((t, D_MODEL), BF16),
        grid=(t // tm,),
        in_specs=[row, row,
                  pl.BlockSpec((tm, D_MODEL), lambda i: (i, _G_BLK)),
                  pl.BlockSpec((tm, D_MODEL), lambda i: (i, _G_BLK + 1)),
                  pl.BlockSpec((1, 2 * D_MODEL), lambda i: (0, 0))],
        out_specs=row,
        compiler_params=_cp("parallel"),
    )(a, b, proj, proj, gate_bias)


def _merge_bwd(dm, a, b, proj, gate_bias):
    t = a.shape[0]
    tm = _row_tile(t)

    def body(dm_ref, a_ref, b_ref, gs_ref, ga_ref, bias_ref, da_ref, db_ref, dg_ref, dbias_ref):
        i = pl.program_id(0)
        dm_ = dm_ref[...]
        g_ssd = _sigmoid(gs_ref[...] + bias_ref[:, 0:D_MODEL])
        g_att = _sigmoid(ga_ref[...] + bias_ref[:, D_MODEL:2 * D_MODEL])
        da_ref[...] = (dm_ * g_ssd).astype(BF16)
        db_ref[...] = (dm_ * g_att).astype(BF16)
        dgs = dm_ * a_ref[...] * g_ssd * (1.0 - g_ssd)
        dga = dm_ * b_ref[...] * g_att * (1.0 - g_att)
        dg_ref[:, 0:D_MODEL] = dgs.astype(BF16)
        dg_ref[:, D_MODEL:2 * D_MODEL] = dga.astype(BF16)
        part = jnp.concatenate([jnp.sum(dgs, axis=0, keepdims=True), jnp.sum(dga, axis=0, keepdims=True)], axis=1)

        @pl.when(i == 0)
        def _():
            dbias_ref[...] = part

        @pl.when(i > 0)
        def _():
            dbias_ref[...] += part

    row = pl.BlockSpec((tm, D_MODEL), lambda i: (i, 0))
    wide = pl.BlockSpec((tm, 2 * D_MODEL), lambda i: (i, 0))
    vec = pl.BlockSpec((1, 2 * D_MODEL), lambda i: (0, 0))
    return pl.pallas_call(
        body, name="merge_bwd",
        out_shape=(jax.ShapeDtypeStruct((t, D_MODEL), BF16), jax.ShapeDtypeStruct((t, D_MODEL), BF16),
                   jax.ShapeDtypeStruct((t, 2 * D_MODEL), BF16), jax.ShapeDtypeStruct((1, 2 * D_MODEL), F32)),
        grid=(t // tm,),
        in_specs=[row, row, row,
                  pl.BlockSpec((tm, D_MODEL), lambda i: (i, _G_BLK)),
                  pl.BlockSpec((tm, D_MODEL), lambda i: (i, _G_BLK + 1)), vec],
        out_specs=(row, row, wide, vec),
        compiler_params=_cp("arbitrary"),
    )(dm, a, b, proj, proj, gate_bias)


def _post(o2, h, target, g):
    t = o2.shape[0]
    nc = t // CHUNK

    def body(o_ref, h_ref, t_ref, g_ref, dy_ref, do_ref, dg_ref, loss_ref):
        c = pl.program_id(0)
        x = o_ref[...]
        r = lax.rsqrt(jnp.mean(x * x, axis=-1, keepdims=True) + EPS)
        n = x * r
        y = h_ref[...] + n * g_ref[...]
        diff = jnp.where(c > 0, y - t_ref[...], 0.0)
        dy = diff * (1.0 / D_MODEL)
        dy_ref[...] = dy
        gdy = dy * g_ref[...]
        do_ref[...] = (r * (gdy - n * jnp.mean(gdy * n, axis=-1, keepdims=True))).astype(BF16)
        dg = jnp.sum(dy * n, axis=0, keepdims=True)
        lpart = 0.5 * jnp.sum(jnp.sum(diff * diff, axis=1, keepdims=True), axis=0, keepdims=True) * (1.0 / D_MODEL)
        sel = (_iota((8, 128), 0) == 0) & (_iota((8, 128), 1) == 0)

        @pl.when(c == 0)
        def _():
            dg_ref[...] = dg
            loss_ref[...] = jnp.zeros_like(loss_ref)

        @pl.when(c > 0)
        def _():
            dg_ref[...] += dg
            loss_ref[...] += jnp.where(sel, lpart, 0.0)

    row = pl.BlockSpec((CHUNK, D_MODEL), lambda c: (c, 0))
    vec = pl.BlockSpec((1, D_MODEL), lambda c: (0, 0))
    return pl.pallas_call(
        body, name="post",
        out_shape=(jax.ShapeDtypeStruct((t, D_MODEL), F32), jax.ShapeDtypeStruct((t, D_MODEL), BF16),
                   jax.ShapeDtypeStruct((1, D_MODEL), F32), jax.ShapeDtypeStruct((8, 128), F32)),
        grid=(nc,),
        in_specs=[row, row, pl.BlockSpec((CHUNK, D_MODEL), lambda c: (jnp.maximum(c - 1, 0), 0)), vec],
        out_specs=(row, row, vec, pl.BlockSpec((8, 128), lambda c: (0, 0))),
        compiler_params=_cp("arbitrary"),
    )(o2, h, target, g)


def _mm_tiles(t):
    return _tile(t, (704, 384, 128))


def _local_step(h, target, w_main, w_small, pr_slots, ids, norm_pre, conv_w, conv_b, bias_row, a_row,
                dsk_row, ssd_norm, gate_bias, norm_post):
    t = h.shape[0]
    tm = _mm_tiles(t)
    u = _norm1_fwd(h, norm_pre)
    proj, pr_slots = _matmul(u, w_main, "nt", F32, "inproj", tm, 1024, D_MODEL,
                             exchange=_gather_stage([pr_slots], to_sibling=False))
    small, pr_slots = _matmul(u, w_small, "nt", F32, "inproj_small", tm, N_SMALL, D_MODEL,
                              exchange=_gather_stage([pr_slots], to_sibling=True))
    wps = pr_slots[:, 0:512].reshape(D_SSD, D_MODEL)
    wpa = pr_slots[:, 512:768].reshape(D_ATT, D_MODEL)
    wout = pr_slots[:, 768:1024].reshape(D_MODEL, D_MODEL)
    dtlf = _small_fwd(small, bias_row)
    xbc = _conv_fwd(proj, conv_w, conv_b)
    y, hin = _ssd_fwd(xbc, dtlf, a_row, dsk_row)
    c_tok = dtlf[:, H_SSD:H_SSD + H_ATT]
    c_tok = jnp.where(jnp.arange(t)[:, None] < PADF, _C_FILLER, c_tok)
    c_col = c_tok.reshape(t, _NPAIR, 2).transpose(1, 0, 2)
    o, lse, p_blocks, m_run = _attn_fwd(proj, c_col)
    ys, ya = _premerge_fwd(y, o, proj, ssd_norm)
    a = _matmul(ys, wps, "nn", F32, "proj_ssd", tm, D_MODEL, D_SSD)
    b = _matmul(ya, wpa, "nn", F32, "proj_att", tm, D_MODEL, D_ATT)
    merged = _merge_fwd(a, b, proj, gate_bias)
    o2 = _matmul(merged, wout, "nn", F32, "out_proj", tm, D_MODEL, D_MODEL)
    dy_out, do2, d_norm_post, loss_blk = _post(o2, h, target, norm_post)

    dm = _matmul(do2, wout, "nt", F32, "out_proj_dx", tm, D_MODEL, D_MODEL)
    d_wout = _matmul(merged, do2, "tn", F32, "out_proj_dw", D_MODEL, D_MODEL, tm)
    da, db, dgraw, d_gate_bias = _merge_bwd(dm, a, b, proj, gate_bias)
    dys = _matmul(da, wps, "nt", F32, "proj_ssd_dx", tm, D_SSD, D_MODEL)
    d_wps = _matmul(ys, da, "tn", F32, "proj_ssd_dw", D_SSD, D_MODEL, tm)
    dya = _matmul(db, wpa, "nt", F32, "proj_att_dx", tm, D_ATT, D_MODEL)
    d_wpa = _matmul(ya, db, "tn", F32, "proj_att_dw", D_ATT, D_MODEL, tm)
    g32_pr = jnp.concatenate([d_wps.reshape(4, 512, D_MODEL), d_wpa.reshape(4, 256, D_MODEL),
                              d_wout.reshape(4, 256, D_MODEL)], axis=1)
    dy, dz, do, dza, d_ssd_norm, dl, ra_pr = _premerge_bwd(dys, dya, y, o, proj, ssd_norm,
                                                           exchange=_pair_swap([g32_pr]))
    pb_pr = _add_pair(ids, g32_pr, ra_pr)
    dl_row = dl[:, 0:H_ATT].T.reshape(_NPAIR, 2, t)
    dq, dk, dv, dc_key, dc_qry, rb_pr = _attn_bwd(proj, lse, dl_row, do, p_blocks, m_run,
                                                  exchange=_chip_exchange([pb_pr]))
    half_pr = _add_chips(ids, g32_pr, ra_pr, rb_pr)
    dxbc, ddt, d_a, d_dsk = _ssd_bwd(xbc, dtlf, a_row, dsk_row, hin, dy)
    dxbc_raw, d_conv_w, d_conv_b = _conv_bwd(dxbc, proj, conv_w, conv_b)
    dc_tok = jnp.transpose(dc_key, (1, 0, 2)).reshape(t, H_ATT) + dc_qry.reshape(H_ATT, t).T
    dsm = ddt + jnp.pad(dc_tok, ((0, 0), (H_SSD, N_SMALL - H_SSD - H_ATT)))
    dsmall, d_bias_row = _small_bwd(dsm, small, bias_row)
    dproj = [dz, dxbc_raw, dza, dq, dk, dv, dgraw]
    return dict(loss_blk=loss_blk, u=u, dy_out=dy_out, dproj=dproj, dsmall=dsmall, half_pr=half_pr,
                d_conv_w=d_conv_w, d_conv_b=d_conv_b,
                d_bias_row=d_bias_row, d_a=d_a, d_dsk=d_dsk, d_ssd_norm=d_ssd_norm,
                d_gate_bias=d_gate_bias, d_norm_post=d_norm_post)


def _to_aligned_rows(slots):
    w = slots.reshape(N_COLS, slots.shape[2])

    def cut(o):
        return w[o[0]:o[0] + o[1]]
    main = jnp.concatenate([cut(O_Z), cut(O_XBC), cut(O_ZA), cut(O_Q), cut(O_K), cut(O_V), cut(O_G)], axis=0)
    pad = jnp.zeros((N_SMALL - H_SSD - H_ATT, w.shape[1]), w.dtype)
    small = jnp.concatenate([cut(O_DT), cut(O_F), pad], axis=0)
    return main, small


def _from_aligned_rows(main, small):
    def cm(c0, n):
        return main[c0:c0 + n]
    flat = jnp.concatenate([cm(C_Z, 2048), cm(C_XBC, 3072), small[0:H_SSD], cm(C_ZA, 1024),
                            cm(C_Q, 1024), cm(C_K, 1024), cm(C_V, 1024), small[H_SSD:H_SSD + H_ATT],
                            cm(C_G, 2048)], axis=0)
    return flat.reshape(4, N_COLS // 4, flat.shape[1])


_MESH = pl.DeviceIdType.MESH
_ANY = pl.BlockSpec(memory_space=pl.ANY)
_VM = pl.BlockSpec(memory_space=pltpu.VMEM)
_HALF = 512
N_DEV = 8


def _coords():
    return lax.axis_index("x"), lax.axis_index("y"), lax.axis_index("c")


def _other_chips(x, y):
    return [(1 - x, y), (x, 1 - y), (1 - x, 1 - y)]


def _half(cc):
    return pl.ds(cc * _HALF, _HALF)


def _gather_shards(slots):
    n = len(slots)

    def body(*refs):
        buf = refs[n:2 * n]
        send_sems, recv_sems = refs[2 * n:]
        x, y, c = _coords()
        chip = 2 * x + y
        sibling = (x, y, 1 - c)
        chips = _other_chips(x, y)

        def copy(i, frm, cc, k, to):
            part = buf[i].at[frm, :, _half(cc)]
            return pltpu.make_async_remote_copy(src_ref=part, dst_ref=part, send_sem=send_sems.at[6 * i + k],
                                                recv_sem=recv_sems.at[6 * i + k], device_id=to, device_id_type=_MESH)

        def chip_of(k):
            return 2 * chips[k][0] + chips[k][1]

        first = [copy(i, chip, c, k, (*chips[k], c)) for k in range(3) for i in range(n)]
        for cp in first:
            cp.start()
        passed = []
        for k in range(3):
            for i in range(n):
                copy(i, chip_of(k), c, k, (*chips[k], c)).wait_recv()
                passed.append(copy(i, chip_of(k), c, 3 + k, sibling))
                passed[-1].start()
        for k in range(3):
            for i in range(n):
                copy(i, chip_of(k), 1 - c, 3 + k, sibling).wait_recv()
        for cp in first + passed:
            cp.wait_send()

    return pl.pallas_call(
        body, name="gather_shards",
        out_shape=tuple(jax.ShapeDtypeStruct(s.shape, s.dtype) for s in slots),
        in_specs=[_ANY] * n, out_specs=tuple([_ANY] * n),
        input_output_aliases={i: i for i in range(n)},
        scratch_shapes=[pltpu.SemaphoreType.DMA((6 * n,)), pltpu.SemaphoreType.DMA((6 * n,))],
    )(*slots)


def _allgather8(block, name):
    rows, width = block.shape

    def body(x_ref, out_ref, send_sems, recv_sems, local_sem):
        x, y, c = _coords()
        me, sibling = (x, y, c), (x, y, 1 - c)
        chips = _other_chips(x, y)

        def slot(px, py, pc):
            return out_ref.at[4 * px + 2 * py + pc]

        def copy(k, blk, to, src=None):
            return pltpu.make_async_remote_copy(src_ref=slot(*blk) if src is None else src, dst_ref=slot(*blk),
                                                send_sem=send_sems.at[k], recv_sem=recv_sems.at[k],
                                                device_id=to, device_id_type=_MESH)

        mine = pltpu.make_async_copy(x_ref, slot(*me), local_sem)
        mine.start()
        first = [copy(0, me, sibling, src=x_ref)]
        first += [copy(1 + j, me, (*chip, c), src=x_ref) for j, chip in enumerate(chips)]
        for cp in first:
            cp.start()
        passed = [copy(4 + j, (*chip, c), sibling) for j, chip in enumerate(chips)]
        for j, chip in enumerate(chips):
            copy(1 + j, (*chip, c), me).wait_recv()
            passed[j].start()
        copy(0, sibling, me).wait_recv()
        for j, chip in enumerate(chips):
            copy(4 + j, (*chip, 1 - c), me).wait_recv()
        for cp in first + passed:
            cp.wait_send()
        mine.wait()

    return pl.pallas_call(
        body, name=name,
        out_shape=jax.ShapeDtypeStruct((N_DEV, rows, width), block.dtype),
        in_specs=[_VM], out_specs=_VM,
        scratch_shapes=[pltpu.SemaphoreType.DMA((7,)), pltpu.SemaphoreType.DMA((7,)), pltpu.SemaphoreType.DMA],
    )(block)


def _pair_swap(arrs):
    def copies(src, dst, send_sems, recv_sems):
        x, y, c = _coords()
        return [pltpu.make_async_remote_copy(src_ref=src[i].at[:, :, _half(1 - c)], dst_ref=dst[i],
                                             send_sem=send_sems.at[i], recv_sem=recv_sems.at[i],
                                             device_id=(x, y, 1 - c), device_id_type=_MESH) for i in range(len(src))]

    shapes = tuple(jax.ShapeDtypeStruct((4, a.shape[1], _HALF), a.dtype) for a in arrs)
    return tuple(arrs), shapes, copies, len(arrs), False


def _chip_exchange(arrs):
    def copies(src, dst, send_sems, recv_sems):
        x, y, c = _coords()
        chips = _other_chips(x, y)
        return [pltpu.make_async_remote_copy(src_ref=src[i].at[2 * chips[k][0] + chips[k][1]], dst_ref=dst[i].at[k],
                                             send_sem=send_sems.at[3 * i + k], recv_sem=recv_sems.at[3 * i + k],
                                             device_id=(*chips[k], c), device_id_type=_MESH)
                for k in range(3) for i in range(len(src))]

    shapes = tuple(jax.ShapeDtypeStruct((3,) + a.shape[1:], a.dtype) for a in arrs)
    return tuple(arrs), shapes, copies, 3 * len(arrs), False


def _gather_stage(slots, to_sibling):
    def copies(buf, _, send_sems, recv_sems):
        x, y, c = _coords()
        chips = _other_chips(x, y)
        out = []
        for k in range(3):
            for i in range(len(buf)):
                frm = 2 * chips[k][0] + chips[k][1] if to_sibling else 2 * x + y
                part = buf[i].at[frm, :, _half(c)]
                out.append(pltpu.make_async_remote_copy(
                    src_ref=part, dst_ref=part, send_sem=send_sems.at[3 * i + k], recv_sem=recv_sems.at[3 * i + k],
                    device_id=(x, y, 1 - c) if to_sibling else (*chips[k], c), device_id_type=_MESH))
        return out

    shapes = tuple(jax.ShapeDtypeStruct(s.shape, s.dtype) for s in slots)
    return tuple(slots), shapes, copies, 3 * len(slots), True


def _pair_join_halves(fulls):
    n = len(fulls)

    def body(*refs):
        buf = refs[n:2 * n]
        send_sems, recv_sems = refs[2 * n:]
        x, y, c = _coords()

        def remote(i, cc):
            part = buf[i].at[:, _half(cc)]
            return pltpu.make_async_remote_copy(src_ref=part, dst_ref=part, send_sem=send_sems.at[i],
                                                recv_sem=recv_sems.at[i], device_id=(x, y, 1 - c), device_id_type=_MESH)

        for i in range(n):
            remote(i, c).start()
        for i in range(n):
            remote(i, c).wait_send()
            remote(i, 1 - c).wait_recv()

    return pl.pallas_call(
        body, name="pair_join_halves",
        out_shape=tuple(jax.ShapeDtypeStruct(a.shape, a.dtype) for a in fulls),
        in_specs=[_ANY] * n, out_specs=tuple([_ANY] * n),
        input_output_aliases={i: i for i in range(n)},
        scratch_shapes=[pltpu.SemaphoreType.DMA((n,)), pltpu.SemaphoreType.DMA((n,))],
    )(*fulls)


_RED_TC = 128
_RED_NT = _HALF // _RED_TC


def _add_pair(ids, g32, recv_a):
    rows = g32.shape[1]

    def body(ids_ref, g_ref, r_ref, o_ref):
        o_ref[...] = (g_ref[...] + r_ref[...]).astype(BF16)

    blk = pl.BlockSpec((1, rows, _RED_TC), lambda j, l, ids: (j, 0, l))
    return pl.pallas_call(
        body, name="add_pair",
        out_shape=jax.ShapeDtypeStruct((4, rows, _HALF), BF16),
        grid_spec=pltpu.PrefetchScalarGridSpec(
            num_scalar_prefetch=1, grid=(4, _RED_NT),
            in_specs=[pl.BlockSpec((1, rows, _RED_TC), lambda j, l, ids: (j, 0, ids[0] * _RED_NT + l)), blk],
            out_specs=blk),
        compiler_params=_cp("parallel", "parallel"),
    )(ids, g32, recv_a)


def _add_chips(ids, g32, recv_a, recv_b):
    rows = g32.shape[1]

    def body(ids_ref, g_ref, a_ref, b_ref, o_ref):
        acc = g_ref[0] + a_ref[0]
        for k in range(3):
            acc = acc + b_ref[k].astype(F32)
        o_ref[...] = acc

    return pl.pallas_call(
        body, name="add_chips",
        out_shape=jax.ShapeDtypeStruct((rows, 2 * _HALF), F32),
        grid_spec=pltpu.PrefetchScalarGridSpec(
            num_scalar_prefetch=1, grid=(_RED_NT,),
            in_specs=[pl.BlockSpec((1, rows, _RED_TC), lambda l, ids: (ids[1], 0, ids[0] * _RED_NT + l)),
                      pl.BlockSpec((1, rows, _RED_TC), lambda l, ids: (ids[1], 0, l)),
                      pl.BlockSpec((3, rows, _RED_TC), lambda l, ids: (0, 0, l))],
            out_specs=pl.BlockSpec((rows, _RED_TC), lambda l, ids: (0, ids[0] * _RED_NT + l))),
        compiler_params=_cp("parallel"),
    )(ids, g32, recv_a, recv_b)


def _sum8(gathered):
    _, rows, width = gathered.shape

    def body(g_ref, o_ref):
        acc = g_ref[0]
        for d in range(1, N_DEV):
            acc = acc + g_ref[d]
        o_ref[...] = acc

    return pl.pallas_call(
        body, name="sum8",
        out_shape=jax.ShapeDtypeStruct((rows, width), F32),
        in_specs=[_VM], out_specs=_VM,
    )(gathered)


def _adamw(w, g, m, v, name):
    rows, cols = w.shape
    budget = (3 << 20) // 2
    tr, tc = rows, cols
    if rows * cols * 4 > budget:
        if rows % 8 == 0:
            tr = max(c for c in range(8, rows, 8) if rows % c == 0 and c * cols * 4 <= budget)
        else:
            tc = next(c for c in (512, 256, 128) if cols % c == 0 and rows * c * 4 <= budget)
    c1 = 1.0 - ADAM_B1 ** ADAM_STEP
    c2 = 1.0 - ADAM_B2 ** ADAM_STEP

    def body(w_ref, g_ref, m_ref, v_ref, d_ref, mo_ref, vo_ref):
        gg = g_ref[...]
        mn = ADAM_B1 * m_ref[...] + (1.0 - ADAM_B1) * gg
        vn = ADAM_B2 * v_ref[...] + (1.0 - ADAM_B2) * (gg * gg)
        mo_ref[...] = mn
        vo_ref[...] = vn
        d_ref[...] = -ADAM_LR * ((mn / c1) / (jnp.sqrt(vn / c2) + ADAM_EPS) + ADAM_WD * w_ref[...])

    blk = pl.BlockSpec((tr, tc), lambda i, j: (i, j))
    shp = jax.ShapeDtypeStruct((rows, cols), F32)
    return pl.pallas_call(
        body, name=name, out_shape=(shp, shp, shp), grid=(rows // tr, cols // tc),
        in_specs=[blk] * 4, out_specs=(blk, blk, blk),
        compiler_params=_cp("parallel", "parallel"),
    )(w, g, m, v)


def _rows128(a):
    return a.reshape(-1, 128)


def _pack_small(norm_pre, conv_b, ssd_norm, gate_bias, norm_post, dt_bias, a_log, d_skip, fgate_bias):
    tiny = jnp.concatenate([dt_bias.reshape(-1), a_log.reshape(-1), d_skip.reshape(-1), fgate_bias.reshape(-1),
                            jnp.zeros((16,), F32)])
    return jnp.concatenate([_rows128(norm_pre), _rows128(conv_b), _rows128(ssd_norm), _rows128(gate_bias),
                            _rows128(norm_post), tiny.reshape(1, 128)], axis=0)


_SMALL_ROWS = 73
_SMALL_PAD = 80


def _unpack_small(p):
    tiny = p[72]
    return dict(norm_pre=p[0:8].reshape(1, 1024), conv_b=p[8:32].reshape(1, 3072), ssd_norm=p[32:48].reshape(1, 2048),
                gate_bias=p[48:64].reshape(1, 2048), norm_post=p[64:72].reshape(1, 1024),
                dt_bias=tiny[0:32].reshape(1, 32), a_log=tiny[32:64].reshape(1, 32),
                d_skip=tiny[64:96].reshape(1, 32), fgate_bias=tiny[96:112].reshape(1, 16))


def _pad_rows(a, rows):
    return jnp.concatenate([a, jnp.zeros((rows - a.shape[0], a.shape[1]), a.dtype)], axis=0)


def kernel(x, meta_tokens, norm_pre, w_in, conv_w, conv_b, dt_bias, a_log, d_skip, ssd_norm, fgate_bias, gate_bias, w_proj_ssd, w_proj_att, w_out, norm_post, loss_target, m_meta_tokens, m_norm_pre, m_w_in, m_conv_w, m_conv_b, m_dt_bias, m_a_log, m_d_skip, m_ssd_norm, m_fgate_bias, m_gate_bias, m_w_proj_ssd, m_w_proj_att, m_w_out, m_norm_post, v_meta_tokens, v_norm_pre, v_w_in, v_conv_w, v_conv_b, v_dt_bias, v_a_log, v_d_skip, v_ssd_norm, v_fgate_bias, v_gate_bias, v_w_proj_ssd, v_w_proj_att, v_w_out, v_norm_post):
    cx, cy, cc = _coords()
    chip = 2 * cx + cy
    ids = jnp.stack([cc, chip]).astype(jnp.int32)
    seq = x.shape[1]

    w_in_sh = jnp.transpose(w_in[0]).astype(BF16)
    w_pr_sh = jnp.concatenate([w_proj_ssd[0], w_proj_att[0], w_out[0]], axis=0).astype(BF16)

    def own_slot(sh):
        return lax.dynamic_update_slice(lax.empty((4,) + sh.shape, sh.dtype), sh[None], (chip, 0, 0))

    (g_in,) = _gather_shards([own_slot(w_in_sh)])
    w_main, w_small = _to_aligned_rows(g_in)
    sm_sh = jnp.concatenate([_rows128(meta_tokens), _rows128(conv_w[0])], axis=0)
    sm_all = _allgather8(sm_sh, "gather_small_weights")[0::2]
    meta_full = jnp.transpose(sm_all[:, 0:32].reshape(4, N_META, 256), (1, 0, 2)).reshape(N_META, D_MODEL)
    conv_w_full = jnp.transpose(sm_all[:, 32:56].reshape(4, CONV_K, 768), (1, 0, 2)).reshape(CONV_K, CONV_DIM)

    h = jnp.concatenate([jnp.zeros((PADF, D_MODEL), F32), meta_full, x[0]], axis=0)
    bias_row = jnp.concatenate([dt_bias[0], fgate_bias[0], jnp.zeros((N_SMALL - H_SSD - H_ATT,), F32)]).reshape(1, N_SMALL)
    a_neg = -jnp.exp(a_log[0])
    a_row = jnp.concatenate([a_neg, jnp.zeros((N_SMALL - H_SSD,), F32)]).reshape(1, N_SMALL)
    dsk_row = jnp.repeat(d_skip[0], 64).reshape(1, D_SSD)
    r = _local_step(h, loss_target[0], w_main, w_small, own_slot(w_pr_sh), ids, norm_pre, conv_w_full, conv_b,
                    bias_row, a_row, dsk_row, ssd_norm, gate_bias, norm_post)

    tm = _mm_tiles(h.shape[0])
    n_row_tiles = h.shape[0] // tm
    d_w_main = _matmul_cat_tn(r["dproj"], r["u"], "inproj_dw", tm)
    d_w_small = _matmul(r["dsmall"], r["u"], "tn", F32, "inproj_small_dw", N_SMALL, D_MODEL, tm)
    g32_in = _from_aligned_rows(d_w_main, d_w_small)
    first = max(n_row_tiles // 6, 1)
    du_first, ra_in = _matmul_cat_nn(r["dproj"], w_main, "inproj_dx_swap", tm, rows=(0, first),
                                     exchange=_pair_swap([g32_in]))
    pb_in = _add_pair(ids, g32_in, ra_in)
    du_a, rb_in = _matmul_cat_nn(r["dproj"], w_main, "inproj_dx_exchange", tm,
                                 rows=(first, n_row_tiles - first), fill=du_first,
                                 exchange=_chip_exchange([pb_in]))
    du_b = _matmul(r["dsmall"], w_small, "nn", F32, "inproj_small_dx", tm, D_MODEL, N_SMALL)
    dh, d_norm_pre = _norm1_bwd(du_a, du_b, h, norm_pre, r["dy_out"])
    grad_x = dh[PADF + N_META:].reshape(1, seq, D_MODEL)
    half_in = _add_chips(ids, g32_in, ra_in, rb_in)
    gw_in, gw_pr = _pair_join_halves([half_in, r["half_pr"]])

    tiny = r["d_bias_row"][0]
    part_small = _pack_small(d_norm_pre, r["d_conv_b"], r["d_ssd_norm"], r["d_gate_bias"], r["d_norm_post"],
                             tiny[0:H_SSD], r["d_a"][0, 0:H_SSD] * a_neg, r["d_dsk"].reshape(H_SSD, 64).sum(axis=1),
                             tiny[H_SSD:H_SSD + H_ATT])
    part = jnp.concatenate([_pad_rows(part_small, _SMALL_PAD), _rows128(r["d_conv_w"]),
                            _rows128(dh[PADF:PADF + N_META]), r["loss_blk"]], axis=0)
    tot = _sum8(_allgather8(part, "gather_small_grads"))
    loss = tot[_SMALL_PAD + 96 + 128, 0]
    g_small = tot[0:_SMALL_PAD]
    g_conv_w = lax.dynamic_slice_in_dim(tot[_SMALL_PAD:_SMALL_PAD + 96].reshape(CONV_K, CONV_DIM), chip * 768, 768, axis=1)
    g_meta = lax.dynamic_slice_in_dim(tot[_SMALL_PAD + 96:_SMALL_PAD + 224].reshape(N_META, D_MODEL), chip * 256, 256, axis=1)

    upd = {}
    upd["w_in"] = tuple(jnp.transpose(a) for a in (gw_in,) + _adamw(
        jnp.transpose(w_in[0]), gw_in, jnp.transpose(m_w_in[0]), jnp.transpose(v_w_in[0]), "adamw_w_in"))
    w_pr32 = jnp.concatenate([w_proj_ssd[0], w_proj_att[0], w_out[0]], axis=0)
    m_pr = jnp.concatenate([m_w_proj_ssd[0], m_w_proj_att[0], m_w_out[0]], axis=0)
    v_pr = jnp.concatenate([v_w_proj_ssd[0], v_w_proj_att[0], v_w_out[0]], axis=0)
    pr = (gw_pr,) + _adamw(w_pr32, gw_pr, m_pr, v_pr, "adamw_w_proj")
    upd["w_proj_ssd"] = tuple(a[0:512] for a in pr)
    upd["w_proj_att"] = tuple(a[512:768] for a in pr)
    upd["w_out"] = tuple(a[768:1024] for a in pr)
    upd["conv_w"] = (g_conv_w,) + _adamw(conv_w[0], g_conv_w, m_conv_w[0], v_conv_w[0], "adamw_conv_w")
    upd["meta_tokens"] = (g_meta,) + _adamw(meta_tokens, g_meta, m_meta_tokens, v_meta_tokens, "adamw_meta")
    pk = lambda np_, cb, sn, gb, npo, dtb, al, ds, fg: _pad_rows(_pack_small(np_, cb, sn, gb, npo, dtb, al, ds, fg), _SMALL_PAD)
    w_sm = pk(norm_pre, conv_b, ssd_norm, gate_bias, norm_post, dt_bias, a_log, d_skip, fgate_bias)
    m_sm = pk(m_norm_pre, m_conv_b, m_ssd_norm, m_gate_bias, m_norm_post, m_dt_bias, m_a_log, m_d_skip, m_fgate_bias)
    v_sm = pk(v_norm_pre, v_conv_b, v_ssd_norm, v_gate_bias, v_norm_post, v_dt_bias, v_a_log, v_d_skip, v_fgate_bias)
    sm = [_unpack_small(a) for a in (g_small,) + _adamw(w_sm, g_small, m_sm, v_sm, "adamw_small")]
    for name in ("norm_pre", "conv_b", "dt_bias", "a_log", "d_skip", "ssd_norm", "fgate_bias", "gate_bias", "norm_post"):
        upd[name] = tuple(s[name] for s in sm)
    lead = ("w_in", "conv_w", "w_proj_ssd", "w_proj_att", "w_out")
    order = ("meta_tokens", "norm_pre", "w_in", "conv_w", "conv_b", "dt_bias", "a_log", "d_skip", "ssd_norm",
             "fgate_bias", "gate_bias", "w_proj_ssd", "w_proj_att", "w_out", "norm_post")
    outs = [loss, grad_x]
    for part_i in range(4):
        for name in order:
            a = upd[name][part_i]
            outs.append(a[None] if name in lead else a)
    return tuple(outs)
```

```python
import functools
import math

import jax
import jax.numpy as jnp
from jax import lax
from jax.experimental import pallas as pl
from jax.experimental.pallas import tpu as pltpu

F32 = jnp.float32
BF16 = jnp.bfloat16
HIGHEST = lax.Precision.HIGHEST

D_MODEL = 1024
N_META = 16
CHUNK = 128
PADF = CHUNK - N_META
D_SSD = 2048
H_SSD = 32
G_SSD = 4
N_STATE = 128
CONV_K = 4
CONV_DIM = D_SSD + 2 * G_SSD * N_STATE
H_ATT = 16
D_ATT = 1024
EPS = 1e-6
N_COLS = 11312

C_Z, C_XBC, C_ZA, C_Q, C_K, C_V, C_G = 0, 2048, 5120, 6144, 7168, 8192, 9216
N_MAIN = 11264
N_SMALL = 128
O_Z, O_XBC, O_DT, O_ZA, O_Q, O_K, O_V, O_F, O_G = (
    (0, 2048), (2048, 3072), (5120, 32), (5152, 1024), (6176, 1024), (7200, 1024),
    (8224, 1024), (9248, 16), (9264, 2048))

ADAM_LR, ADAM_B1, ADAM_B2, ADAM_EPS, ADAM_WD, ADAM_STEP = 0.001, 0.9, 0.999, 1e-08, 0.01, 10

VMEM_LIMIT = 56 * 1024 * 1024


def _cp(*sem):
    return pltpu.CompilerParams(dimension_semantics=sem, vmem_limit_bytes=VMEM_LIMIT)


def _tile(n, prefs):
    for p in prefs:
        if n % p == 0:
            return p
    raise ValueError(f"no tile for {n} in {prefs}")


def _iota(shape, dim):
    return lax.broadcasted_iota(jnp.int32, shape, dim)


def _sigmoid(x):
    return 1.0 / (1.0 + jnp.exp(-x))


def _softplus_tail(x):
    return jnp.log(1.0 + jnp.exp(-jnp.abs(x)))


_NN = (((1,), (0,)), ((), ()))
_NT = (((1,), (1,)), ((), ()))
_TN = (((0,), (0,)), ((), ()))


def _dot(a, b, dims=_NN):
    return lax.dot_general(a, b, dims, preferred_element_type=F32)


def _dot_exact(a, b, dims=_NN):
    return lax.dot_general(a, b, dims, precision=HIGHEST, preferred_element_type=F32)


def _hosted_call(body, *, name, grid, in_specs, out_specs, out_shape, scratch_shapes, operands, semantics,
                 exchange=None, aliases=None):
    aliases = dict(aliases or {})
    if exchange is None:
        return pl.pallas_call(body, name=name, out_shape=out_shape, grid=grid, in_specs=in_specs,
                              out_specs=out_specs, scratch_shapes=scratch_shapes, input_output_aliases=aliases,
                              compiler_params=_cp(*semantics))(*operands)
    arrays, shapes, copies, n_sems, in_place = exchange
    n_in, n_out, n_ex = len(operands), len(out_shape), len(arrays)

    def hosted(*refs):
        ex_in = refs[n_in:n_in + n_ex]
        ex_out = refs[n_in + n_ex + n_out:n_in + n_ex + n_out + n_ex]
        own = refs[:n_in] + refs[n_in + n_ex:n_in + n_ex + n_out] + refs[n_in + 2 * n_ex + n_out:-2]
        first = functools.reduce(lambda p, q: p & q, [pl.program_id(d) == 0 for d in range(len(grid))])
        last = functools.reduce(lambda p, q: p & q, [pl.program_id(d) == grid[d] - 1 for d in range(len(grid))])

        def descriptors():
            return copies(ex_out if in_place else ex_in, ex_out, refs[-2], refs[-1])

        @pl.when(first)
        def _():
            for cp in descriptors():
                cp.start()

        body(*own)

        @pl.when(last)
        def _():
            for cp in descriptors():
                cp.wait()

    return pl.pallas_call(
        hosted, name=name,
        out_shape=tuple(out_shape) + tuple(shapes),
        grid=grid,
        in_specs=list(in_specs) + [_ANY] * n_ex,
        out_specs=tuple(out_specs) + (_ANY,) * n_ex,
        input_output_aliases={**aliases, **({n_in + e: n_out + e for e in range(n_ex)} if in_place else {})},
        scratch_shapes=list(scratch_shapes) + [pltpu.SemaphoreType.DMA((n_sems,)), pltpu.SemaphoreType.DMA((n_sems,))],
        compiler_params=_cp(*(("arbitrary",) * len(grid))),
    )(*operands, *arrays)


def _matmul(a, b, mode, out_dtype, name, tm, tn, tk, exchange=None):
    if mode == "tn":
        kdim, m = a.shape
    else:
        m, kdim = a.shape
    n = b.shape[0] if mode == "nt" else b.shape[1]
    nk = kdim // tk
    dims = {"nn": _NN, "nt": _NT, "tn": _TN}[mode]
    a_spec = (pl.BlockSpec((tk, tm), lambda i, j, k: (k, i)) if mode == "tn"
              else pl.BlockSpec((tm, tk), lambda i, j, k: (i, k)))
    b_spec = (pl.BlockSpec((tn, tk), lambda i, j, k: (j, k)) if mode == "nt"
              else pl.BlockSpec((tk, tn), lambda i, j, k: (k, j)))

    def body(a_ref, b_ref, o_ref, acc_ref):
        k = pl.program_id(2)
        p = _dot(a_ref[...].astype(BF16), b_ref[...].astype(BF16), dims)
        if nk == 1:
            o_ref[...] = p.astype(out_dtype)
        else:
            @pl.when(k == 0)
            def _():
                acc_ref[...] = p

            @pl.when(k > 0)
            def _():
                acc_ref[...] += p

            @pl.when(k == nk - 1)
            def _():
                o_ref[...] = acc_ref[...].astype(out_dtype)

    out = _hosted_call(
        body, name=name,
        out_shape=(jax.ShapeDtypeStruct((m, n), out_dtype),),
        grid=(m // tm, n // tn, nk),
        in_specs=[a_spec, b_spec],
        out_specs=(pl.BlockSpec((tm, tn), lambda i, j, k: (i, j)),),
        scratch_shapes=[pltpu.VMEM((tm, tn), F32)],
        operands=(a, b), semantics=("parallel", "parallel", "arbitrary"), exchange=exchange)
    return out[0] if exchange is None else out


_CAT_BLK = 1024


def _piece_ranges(pieces):
    out, off = [], 0
    for p in pieces:
        nb = p.shape[1] // _CAT_BLK
        out.append((off, nb))
        off += nb
    return out, off


def _matmul_cat_nn(pieces, b, name, tm, rows=None, fill=None, exchange=None):
    t = pieces[0].shape[0]
    n = b.shape[1]
    ranges, nk = _piece_ranges(pieces)
    first, ni = rows if rows is not None else (0, t // tm)
    n_in = len(pieces) + 1 + (fill is not None)

    def body(*refs):
        a_refs, b_ref, o_ref, acc_ref = refs[:len(pieces)], refs[len(pieces)], refs[n_in], refs[n_in + 1]
        k = pl.program_id(1)

        @pl.when(k == 0)
        def _():
            acc_ref[...] = jnp.zeros_like(acc_ref)

        for a_ref, (off, nb) in zip(a_refs, ranges):
            @pl.when((k >= off) & (k < off + nb))
            def _(a_ref=a_ref):
                acc_ref[...] += _dot(a_ref[...], b_ref[...])

        @pl.when(k == nk - 1)
        def _():
            o_ref[...] = acc_ref[...]

    def a_spec(off, nb):
        return pl.BlockSpec((tm, _CAT_BLK), lambda i, k: (first + i, jnp.clip(k - off, 0, nb - 1)))

    in_specs = [a_spec(off, nb) for off, nb in ranges] + [pl.BlockSpec((_CAT_BLK, n), lambda i, k: (k, 0))]
    operands = list(pieces) + [b]
    if fill is not None:
        in_specs.append(_ANY)
        operands.append(fill)
    out = _hosted_call(
        body, name=name,
        out_shape=(jax.ShapeDtypeStruct((t, n), F32),),
        grid=(ni, nk),
        in_specs=in_specs,
        out_specs=(pl.BlockSpec((tm, n), lambda i, k: (first + i, 0)),),
        scratch_shapes=[pltpu.VMEM((tm, n), F32)],
        operands=operands, semantics=("parallel", "arbitrary"), exchange=exchange,
        aliases={len(pieces) + 1: 0} if fill is not None else None)
    return out if exchange is not None else out[0]


def _matmul_cat_tn(pieces, b, name, tk):
    t = pieces[0].shape[0]
    n = b.shape[1]
    ranges, nm = _piece_ranges(pieces)
    nk = t // tk

    def body(*refs):
        a_refs, b_ref, o_ref, acc_ref = refs[:len(pieces)], refs[-3], refs[-2], refs[-1]
        m = pl.program_id(0)
        k = pl.program_id(1)

        @pl.when(k == 0)
        def _():
            acc_ref[...] = jnp.zeros_like(acc_ref)

        for a_ref, (off, nb) in zip(a_refs, ranges):
            @pl.when((m >= off) & (m < off + nb))
            def _(a_ref=a_ref):
                acc_ref[...] += _dot(a_ref[...], b_ref[...], _TN)

        @pl.when(k == nk - 1)
        def _():
            o_ref[...] = acc_ref[...]

    def a_spec(off, nb):
        def index(m, k):
            mine = (m >= off) & (m < off + nb)
            return jnp.where(mine, k, 0), jnp.clip(m - off, 0, nb - 1)
        return pl.BlockSpec((tk, _CAT_BLK), index)

    return pl.pallas_call(
        body, name=name,
        out_shape=jax.ShapeDtypeStruct((nm * _CAT_BLK, n), F32),
        grid=(nm, nk),
        in_specs=[a_spec(off, nb) for off, nb in ranges] + [pl.BlockSpec((tk, n), lambda m, k: (k, 0))],
        out_specs=pl.BlockSpec((_CAT_BLK, n), lambda m, k: (m, 0)),
        scratch_shapes=[pltpu.VMEM((_CAT_BLK, n), F32)],
        compiler_params=_cp("parallel", "arbitrary"),
    )(*pieces, b)


def _row_tile(t):
    return _tile(t, (352, 128))


def _row_tile_wide(t):
    return _tile(t, (176, 128))


def _norm1_fwd(h, g):
    t = h.shape[0]
    tm = _row_tile(t)

    def body(h_ref, g_ref, u_ref):
        x = h_ref[...]
        r = lax.rsqrt(jnp.mean(x * x, axis=-1, keepdims=True) + EPS)
        u_ref[...] = (x * r * g_ref[...]).astype(BF16)

    return pl.pallas_call(
        body, name="norm1_fwd",
        out_shape=jax.ShapeDtypeStruct((t, D_MODEL), BF16),
        grid=(t // tm,),
        in_specs=[pl.BlockSpec((tm, D_MODEL), lambda i: (i, 0)),
                  pl.BlockSpec((1, D_MODEL), lambda i: (0, 0))],
        out_specs=pl.BlockSpec((tm, D_MODEL), lambda i: (i, 0)),
        compiler_params=_cp("parallel"),
    )(h, g)


def _norm1_bwd(du_a, du_b, h, g, dy):
    t = h.shape[0]
    tm = _row_tile(t)

    def body(a_ref, b_ref, h_ref, g_ref, dy_ref, dh_ref, dg_ref):
        i = pl.program_id(0)
        x = h_ref[...]
        du = a_ref[...] + b_ref[...]
        r = lax.rsqrt(jnp.mean(x * x, axis=-1, keepdims=True) + EPS)
        gdu = du * g_ref[...]
        dh_ref[...] = dy_ref[...] + r * (gdu - x * (r * r) * jnp.mean(gdu * x, axis=-1, keepdims=True))
        part = jnp.sum(du * x * r, axis=0, keepdims=True)

        @pl.when(i == 0)
        def _():
            dg_ref[...] = part

        @pl.when(i > 0)
        def _():
            dg_ref[...] += part

    row = pl.BlockSpec((tm, D_MODEL), lambda i: (i, 0))
    vec = pl.BlockSpec((1, D_MODEL), lambda i: (0, 0))
    return pl.pallas_call(
        body, name="norm1_bwd",
        out_shape=(jax.ShapeDtypeStruct((t, D_MODEL), F32), jax.ShapeDtypeStruct((1, D_MODEL), F32)),
        grid=(t // tm,),
        in_specs=[row, row, row, vec, row],
        out_specs=(row, vec),
        compiler_params=_cp("arbitrary"),
    )(du_a, du_b, h, g, dy)


def _small_fwd(small, bias_row):
    t = small.shape[0]
    rt = _tile(t, (384, 128))

    def body(s_ref, b_ref, o_ref, carry_ref):
        c = pl.program_id(0)

        @pl.when(c == 0)
        def _():
            carry_ref[...] = jnp.zeros_like(carry_ref)

        x = s_ref[...] + b_ref[...]
        lane = _iota((rt, N_SMALL), 1)
        valid = (c * rt + _iota((rt, N_SMALL), 0)) >= PADF
        tail = _softplus_tail(x)
        dt = jnp.where(valid & (lane < H_SSD), jnp.maximum(x, 0.0) + tail, 0.0)
        lf = jnp.where(valid & (lane >= H_SSD) & (lane < H_SSD + H_ATT), jnp.minimum(x, 0.0) - tail, 0.0)
        tri = (_iota((rt, rt), 0) >= _iota((rt, rt), 1)).astype(F32)
        cs = _dot_exact(tri, lf) + carry_ref[...]
        carry_ref[...] = cs[rt - 1:rt, :]
        o_ref[...] = dt + cs

    return pl.pallas_call(
        body, name="small_fwd",
        out_shape=jax.ShapeDtypeStruct((t, N_SMALL), F32),
        grid=(t // rt,),
        in_specs=[pl.BlockSpec((rt, N_SMALL), lambda c: (c, 0)),
                  pl.BlockSpec((1, N_SMALL), lambda c: (0, 0))],
        out_specs=pl.BlockSpec((rt, N_SMALL), lambda c: (c, 0)),
        scratch_shapes=[pltpu.VMEM((1, N_SMALL), F32)],
        compiler_params=_cp("arbitrary"),
    )(small, bias_row)


def _small_bwd(dsm, small, bias_row):
    t = small.shape[0]
    rt = _tile(t, (384, 128))
    nc = t // rt

    def body(d_ref, s_ref, b_ref, o_ref, db_ref, carry_ref):
        step = pl.program_id(0)
        c = nc - 1 - step

        @pl.when(step == 0)
        def _():
            carry_ref[...] = jnp.zeros_like(carry_ref)
            db_ref[...] = jnp.zeros_like(db_ref)

        x = s_ref[...] + b_ref[...]
        d = d_ref[...]
        lane = _iota((rt, N_SMALL), 1)
        valid = (c * rt + _iota((rt, N_SMALL), 0)) >= PADF
        is_dt = lane < H_SSD
        is_f = (lane >= H_SSD) & (lane < H_SSD + H_ATT)
        triu = (_iota((rt, rt), 1) >= _iota((rt, rt), 0)).astype(F32)
        dc = jnp.where(is_f, d, 0.0)
        dlf = _dot_exact(triu, dc) + carry_ref[...]
        carry_ref[...] = dlf[0:1, :]
        sg = _sigmoid(x)
        out = jnp.where(valid & is_dt, d * sg, 0.0) + jnp.where(valid & is_f, dlf * (1.0 - sg), 0.0)
        o_ref[...] = out.astype(BF16)
        db_ref[...] += jnp.sum(out, axis=0, keepdims=True)

    blk = pl.BlockSpec((rt, N_SMALL), lambda s: (nc - 1 - s, 0))
    vec = pl.BlockSpec((1, N_SMALL), lambda s: (0, 0))
    return pl.pallas_call(
        body, name="small_bwd",
        out_shape=(jax.ShapeDtypeStruct((t, N_SMALL), BF16), jax.ShapeDtypeStruct((1, N_SMALL), F32)),
        grid=(nc,),
        in_specs=[blk, blk, vec],
        out_specs=(blk, vec),
        scratch_shapes=[pltpu.VMEM((1, N_SMALL), F32)],
        compiler_params=_cp("arbitrary"),
    )(dsm, small, bias_row)


_CONV_TC = 1024
_XBC_BLK = C_XBC // _CONV_TC


def _shift_down(cur, prev8, j):
    rc = pltpu.roll(cur, j, 0)
    rid = _iota(prev8.shape, 0)
    top = jnp.where(rid < j, pltpu.roll(prev8, j, 0), rc[0:8, :])
    return top if cur.shape[0] == 8 else jnp.concatenate([top, rc[8:, :]], axis=0)


def _shift_up(cur, next8, j):
    n = cur.shape[0]
    ru = pltpu.roll(cur, n - j, 0)
    rid = _iota(next8.shape, 0)
    bot = jnp.where(rid >= 8 - j, pltpu.roll(next8, 8 - j, 0), ru[n - 8:, :])
    return jnp.concatenate([ru[:n - 8, :], bot], axis=0)


def _conv_taps(cur, prev, w, b):
    taps = [cur] + [_shift_down(cur, prev, j) for j in (1, 2, 3)]
    acc = b + taps[0] * w[3:4, :]
    for j in (1, 2, 3):
        acc = acc + taps[j] * w[3 - j:4 - j, :]
    return acc, taps


def _conv_pre(x_ref, p_ref, w_ref, b_ref, i):
    return _conv_taps(x_ref[...], jnp.where(i > 0, p_ref[...], 0.0), w_ref[...], b_ref[...])


def _dsilu(d, acc):
    sg = _sigmoid(acc)
    return d * sg * (1.0 + acc * (1.0 - sg))


def _conv_fwd(proj, conv_w, conv_b):
    t = proj.shape[0]
    tr = _row_tile(t)

    def body(x_ref, p_ref, w_ref, b_ref, o_ref):
        i = pl.program_id(0)
        acc, _ = _conv_pre(x_ref, p_ref, w_ref, b_ref, i)
        valid = (i * tr + _iota(acc.shape, 0)) >= PADF
        o_ref[...] = jnp.where(valid, acc * _sigmoid(acc), 0.0)

    return pl.pallas_call(
        body, name="conv_fwd",
        out_shape=jax.ShapeDtypeStruct((t, CONV_DIM), F32),
        grid=(t // tr, CONV_DIM // _CONV_TC),
        in_specs=[pl.BlockSpec((tr, _CONV_TC), lambda i, j: (i, _XBC_BLK + j)),
                  pl.BlockSpec((8, _CONV_TC), lambda i, j: (jnp.maximum(i * (tr // 8) - 1, 0), _XBC_BLK + j)),
                  pl.BlockSpec((CONV_K, _CONV_TC), lambda i, j: (0, j)),
                  pl.BlockSpec((1, _CONV_TC), lambda i, j: (0, j))],
        out_specs=pl.BlockSpec((tr, _CONV_TC), lambda i, j: (i, j)),
        compiler_params=_cp("parallel", "parallel"),
    )(proj, proj, conv_w, conv_b)


def _conv_bwd(dxbc, proj, conv_w, conv_b):
    t = proj.shape[0]
    tr = _row_tile(t)
    n_tiles = t // tr
    last8 = t // 8 - 1

    def body(d_ref, dn_ref, x_ref, p_ref, xn_ref, w_ref, b_ref, dx_ref, dw_ref, db_ref):
        i = pl.program_id(1)
        w = w_ref[...]
        b = b_ref[...]
        cur = x_ref[...]
        acc, taps = _conv_taps(cur, jnp.where(i > 0, p_ref[...], 0.0), w, b)
        valid = (i * tr + _iota(acc.shape, 0)) >= PADF
        da = jnp.where(valid, _dsilu(d_ref[...], acc), 0.0)
        acc_n, _ = _conv_taps(xn_ref[...], cur[tr - 8:, :], w, b)
        da_n = jnp.where(i < n_tiles - 1, _dsilu(dn_ref[...], acc_n), 0.0)
        dx = da * w[3:4, :]
        for j in (1, 2, 3):
            dx = dx + _shift_up(da, da_n, j) * w[3 - j:4 - j, :]
        dx_ref[...] = dx.astype(BF16)
        dw = jnp.concatenate([jnp.sum(da * taps[3 - k], axis=0, keepdims=True) for k in range(CONV_K)], axis=0)
        db = jnp.sum(da, axis=0, keepdims=True)

        @pl.when(i == 0)
        def _():
            dw_ref[...] = dw
            db_ref[...] = db

        @pl.when(i > 0)
        def _():
            dw_ref[...] += dw
            db_ref[...] += db

    nxt8 = lambda i: jnp.minimum((i + 1) * (tr // 8), last8)
    return pl.pallas_call(
        body, name="conv_bwd",
        out_shape=(jax.ShapeDtypeStruct((t, CONV_DIM), BF16),
                   jax.ShapeDtypeStruct((CONV_K, CONV_DIM), F32),
                   jax.ShapeDtypeStruct((1, CONV_DIM), F32)),
        grid=(CONV_DIM // _CONV_TC, n_tiles),
        in_specs=[pl.BlockSpec((tr, _CONV_TC), lambda j, i: (i, j)),
                  pl.BlockSpec((8, _CONV_TC), lambda j, i: (nxt8(i), j)),
                  pl.BlockSpec((tr, _CONV_TC), lambda j, i: (i, _XBC_BLK + j)),
                  pl.BlockSpec((8, _CONV_TC), lambda j, i: (jnp.maximum(i * (tr // 8) - 1, 0), _XBC_BLK + j)),
                  pl.BlockSpec((8, _CONV_TC), lambda j, i: (nxt8(i), _XBC_BLK + j)),
                  pl.BlockSpec((CONV_K, _CONV_TC), lambda j, i: (0, j)),
                  pl.BlockSpec((1, _CONV_TC), lambda j, i: (0, j))],
        out_specs=(pl.BlockSpec((tr, _CONV_TC), lambda j, i: (i, j)),
                   pl.BlockSpec((CONV_K, _CONV_TC), lambda j, i: (0, j)),
                   pl.BlockSpec((1, _CONV_TC), lambda j, i: (0, j))),
        compiler_params=_cp("parallel", "arbitrary"),
    )(dxbc, dxbc, proj, proj, proj, conv_w, conv_b)


_GW = D_SSD // G_SSD


def _ssd_prelude(dt_ref, a_ref, e_scr, es_scr, dte_scr):
    r0 = _iota((CHUNK, CHUNK), 0)
    r1 = _iota((CHUNK, CHUNK), 1)
    dt = jnp.where(r1 < H_SSD, dt_ref[...], 0.0)
    adt = dt * a_ref[...]
    acs = _dot_exact((r0 >= r1).astype(F32), adt)
    acs_t = acs.T
    alast = acs[CHUNK - 1:CHUNK, :]
    exp_a = jnp.exp(acs)
    dec_s = jnp.exp(alast - acs)
    lo = r1 < 64
    for j in range(H_SSD // 2):
        sl = slice(CHUNK * j, CHUNK * (j + 1))
        e_scr[:, sl] = jnp.where(lo, exp_a[:, 2 * j:2 * j + 1], exp_a[:, 2 * j + 1:2 * j + 2])
        es_scr[:, sl] = jnp.where(lo, dec_s[:, 2 * j:2 * j + 1], dec_s[:, 2 * j + 1:2 * j + 2])
        dte_scr[:, sl] = jnp.where(lo, dt[:, 2 * j:2 * j + 1], dt[:, 2 * j + 1:2 * j + 2])
    return dt, acs, acs_t, r0, r1, lo


def _chunk_decay_rows(acs_t, g):
    cd_t = jnp.exp(acs_t[:, CHUNK - 1:CHUNK])
    return jnp.concatenate(
        [jnp.broadcast_to(cd_t[8 * g + hh:8 * g + hh + 1, :], (64, N_STATE)) for hh in range(8)], axis=0)


def _ssd_fwd(xbc, dtlf, a_row, dsk_row):
    t = xbc.shape[0]
    nc = t // CHUNK

    def body(xs_ref, b_ref, c_ref, dt_ref, a_ref, dsk_ref, y_ref, hin_ref, h_scr, e_scr, es_scr, dte_scr):
        c = pl.program_id(0)

        @pl.when(c == 0)
        def _():
            h_scr[...] = jnp.zeros_like(h_scr)

        dt, acs, acs_t, r0, r1, lo = _ssd_prelude(dt_ref, a_ref, e_scr, es_scr, dte_scr)
        causal = r0 >= r1
        for g in range(G_SSD):
            gs = slice(_GW * g, _GW * (g + 1))
            bg = b_ref[:, N_STATE * g:N_STATE * (g + 1)].astype(BF16)
            cg = c_ref[:, N_STATE * g:N_STATE * (g + 1)].astype(BF16)
            cb = _dot(cg, bg, _NT)
            hg = h_scr[gs, :]
            hin_ref[0, gs, :] = hg
            xg = xs_ref[:, gs] * dte_scr[:, gs]
            yoff = _dot(cg, hg.astype(BF16), _NT) * e_scr[:, gs]
            st = _dot((xg * es_scr[:, gs]).astype(BF16), bg, _TN)
            h_scr[gs, :] = hg * _chunk_decay_rows(acs_t, g) + st
            for jj in range(4):
                j = 4 * g + jj
                sl = slice(CHUNK * j, CHUNK * (j + 1))
                xp = xg[:, CHUNK * jj:CHUNK * (jj + 1)]
                acc = yoff[:, CHUNK * jj:CHUNK * (jj + 1)] + dsk_ref[:, sl] * xs_ref[:, sl]
                for hh in range(2):
                    h = 2 * j + hh
                    seg = acs[:, h:h + 1] - acs_t[h:h + 1, :]
                    lm = jnp.exp(jnp.where(causal, seg, -1e30))
                    m = (cb * lm).astype(BF16)
                    xh = jnp.where(lo if hh == 0 else ~lo, xp, 0.0).astype(BF16)
                    acc = acc + _dot(m, xh)
                y_ref[:, sl] = acc

    return pl.pallas_call(
        body, name="ssd_fwd",
        out_shape=(jax.ShapeDtypeStruct((t, D_SSD), F32), jax.ShapeDtypeStruct((nc, D_SSD, N_STATE), F32)),
        grid=(nc,),
        in_specs=[pl.BlockSpec((CHUNK, D_SSD), lambda c: (c, 0)),
                  pl.BlockSpec((CHUNK, _GW), lambda c: (c, 4)),
                  pl.BlockSpec((CHUNK, _GW), lambda c: (c, 5)),
                  pl.BlockSpec((CHUNK, N_SMALL), lambda c: (c, 0)),
                  pl.BlockSpec((1, N_SMALL), lambda c: (0, 0)),
                  pl.BlockSpec((1, D_SSD), lambda c: (0, 0))],
        out_specs=(pl.BlockSpec((CHUNK, D_SSD), lambda c: (c, 0)),
                   pl.BlockSpec((1, D_SSD, N_STATE), lambda c: (c, 0, 0))),
        scratch_shapes=[pltpu.VMEM((D_SSD, N_STATE), F32)] + [pltpu.VMEM((CHUNK, D_SSD), F32)] * 3,
        compiler_params=_cp("arbitrary"),
    )(xbc, xbc, xbc, dtlf, a_row, dsk_row)


def _ssd_bwd(xbc, dtlf, a_row, dsk_row, hin, dy):
    t = xbc.shape[0]
    nc = t // CHUNK

    def body(xs_ref, b_ref, c_ref, dt_ref, a_ref, dsk_ref, hin_ref, dy_ref,
             dxbc_ref, ddt_ref, da_ref, ddsk_ref, dh_scr, e_scr, es_scr, dte_scr, dx_scr, whi_scr, wlo_scr):
        step = pl.program_id(0)

        @pl.when(step == 0)
        def _():
            dh_scr[...] = jnp.zeros_like(dh_scr)
            da_ref[...] = jnp.zeros_like(da_ref)
            ddsk_ref[...] = jnp.zeros_like(ddsk_ref)

        dt, acs, acs_t, r0, r1, lo = _ssd_prelude(dt_ref, a_ref, e_scr, es_scr, dte_scr)
        causal = r0 >= r1
        lane_row = _iota((1, CHUNK), 1)
        dacs = jnp.zeros((CHUNK, CHUNK), F32)
        dacs_t = jnp.zeros((CHUNK, CHUNK), F32)
        dalast = jnp.zeros((1, CHUNK), F32)
        ddt_dir = jnp.zeros((CHUNK, CHUNK), F32)
        ddsk_ref[...] += jnp.sum(dy_ref[...] * xs_ref[...], axis=0, keepdims=True)

        def head_sums(z, pick):
            hi = z.astype(BF16)
            return _dot(hi, pick) + _dot((z - hi.astype(F32)).astype(BF16), pick)

        for g in range(G_SSD):
            gs = slice(_GW * g, _GW * (g + 1))
            pick = (jnp.right_shift(_iota((_GW, CHUNK), 0), 6) + 8 * g == _iota((_GW, CHUNK), 1)).astype(BF16)
            bg = b_ref[:, N_STATE * g:N_STATE * (g + 1)].astype(BF16)
            cg = c_ref[:, N_STATE * g:N_STATE * (g + 1)].astype(BF16)
            cb = _dot(cg, bg, _NT)
            hg = hin_ref[0, gs, :]
            hgb = hg.astype(BF16)
            dhn = dh_scr[gs, :]
            dhnb = dhn.astype(BF16)
            esg = es_scr[:, gs]
            dyg = dy_ref[:, gs]
            xsg = xs_ref[:, gs]
            xg = xsg * dte_scr[:, gs]
            dyeb = (dyg * e_scr[:, gs]).astype(BF16)
            dc = _dot(dyeb, hgb)
            dh_y = _dot(dyeb, cg, _TN)
            dxs = _dot(bg, dhnb, _NT) * esg
            db = _dot((xg * esg).astype(BF16), dhnb)
            cd = _chunk_decay_rows(acs_t, g)
            dh_scr[gs, :] = dhn * cd + dh_y
            end_state = head_sums(jnp.broadcast_to(jnp.sum(xg * dxs, axis=0, keepdims=True), (8, _GW)), pick)[0:1, :]
            carried = dhn * hg * cd
            per_head = jnp.concatenate([jnp.sum(carried[64 * hh:64 * hh + 64, :], axis=0, keepdims=True)
                                        for hh in range(8)], axis=0)
            per_head = jnp.sum(per_head, axis=1, keepdims=True)
            for hh in range(8):
                end_state = end_state + jnp.where(lane_row == 8 * g + hh, per_head[hh:hh + 1, :], 0.0)
            dalast = dalast + end_state
            dcb = jnp.zeros((CHUNK, CHUNK), F32)
            for jj in range(4):
                j = 4 * g + jj
                sl = slice(CHUNK * j, CHUNK * (j + 1))
                ps = slice(CHUNK * jj, CHUNK * (jj + 1))
                xpb = xg[:, ps].astype(BF16)
                dyp = dyg[:, ps]
                dxp = dxs[:, ps]
                for hh in range(2):
                    h = 2 * j + hh
                    ws = slice(CHUNK * (2 * jj + hh), CHUNK * (2 * jj + hh + 1))
                    seg = acs[:, h:h + 1] - acs_t[h:h + 1, :]
                    lm = jnp.exp(jnp.where(causal, seg, -1e30))
                    mf = cb * lm
                    dyh = jnp.where(lo if hh == 0 else ~lo, dyp, 0.0).astype(BF16)
                    gm = _dot(dyh, xpb, _NT)
                    dcb = dcb + gm * lm
                    w = gm * mf
                    whi = w.astype(BF16)
                    whi_scr[:, ws] = whi
                    wlo_scr[:, ws] = (w - whi.astype(F32)).astype(BF16)
                    dacs_t = dacs_t - jnp.where(r0 == h, jnp.sum(w, axis=0, keepdims=True), 0.0)
                    dxp = dxp + _dot(mf.astype(BF16), dyh, _TN)
                dx_scr[:, sl] = dxp
            dxg = dx_scr[:, gs]
            pick_w = (jnp.right_shift(_iota((8 * CHUNK, CHUNK), 0), 7) + 8 * g == _iota((8 * CHUNK, CHUNK), 1)).astype(BF16)
            ch = _dot(cg, hgb, _NT)
            dacs = (dacs + _dot(whi_scr[...], pick_w) + _dot(wlo_scr[...], pick_w)
                    + head_sums(dyg * e_scr[:, gs] * ch - xg * dxs, pick))
            ddt_dir = ddt_dir + head_sums(dxg * xsg, pick)
            dcbb = dcb.astype(BF16)
            dxbc_ref[:, D_SSD + N_STATE * g:D_SSD + N_STATE * (g + 1)] = db + _dot(dcbb, cg, _TN)
            dxbc_ref[:, D_SSD + _GW + N_STATE * g:D_SSD + _GW + N_STATE * (g + 1)] = dc + _dot(dcbb, bg)
        dxbc_ref[:, 0:D_SSD] = dx_scr[...] * dte_scr[...] + dsk_ref[...] * dy_ref[...]
        dacs = dacs + dacs_t.T + jnp.where(r0 == CHUNK - 1, dalast, 0.0)
        dadt = _dot_exact((r1 >= r0).astype(F32), dacs)
        ddt_ref[...] = dadt * a_ref[...] + ddt_dir
        da_ref[...] += jnp.sum(dadt * dt, axis=0, keepdims=True)

    rev = lambda s: (nc - 1 - s, 0)
    return pl.pallas_call(
        body, name="ssd_bwd",
        out_shape=(jax.ShapeDtypeStruct((t, CONV_DIM), F32), jax.ShapeDtypeStruct((t, N_SMALL), F32),
                   jax.ShapeDtypeStruct((1, N_SMALL), F32), jax.ShapeDtypeStruct((1, D_SSD), F32)),
        grid=(nc,),
        in_specs=[pl.BlockSpec((CHUNK, D_SSD), rev),
                  pl.BlockSpec((CHUNK, _GW), lambda s: (nc - 1 - s, 4)),
                  pl.BlockSpec((CHUNK, _GW), lambda s: (nc - 1 - s, 5)),
                  pl.BlockSpec((CHUNK, N_SMALL), rev),
                  pl.BlockSpec((1, N_SMALL), lambda s: (0, 0)),
                  pl.BlockSpec((1, D_SSD), lambda s: (0, 0)),
                  pl.BlockSpec((1, D_SSD, N_STATE), lambda s: (nc - 1 - s, 0, 0)),
                  pl.BlockSpec((CHUNK, D_SSD), rev)],
        out_specs=(pl.BlockSpec((CHUNK, CONV_DIM), rev),
                   pl.BlockSpec((CHUNK, N_SMALL), rev),
                   pl.BlockSpec((1, N_SMALL), lambda s: (0, 0)),
                   pl.BlockSpec((1, D_SSD), lambda s: (0, 0))),
        scratch_shapes=([pltpu.VMEM((D_SSD, N_STATE), F32)] + [pltpu.VMEM((CHUNK, D_SSD), F32)] * 4
                        + [pltpu.VMEM((CHUNK, 8 * CHUNK), BF16)] * 2),
        compiler_params=_cp("arbitrary"),
    )(xbc, xbc, xbc, dtlf, a_row, dsk_row, hin, dy)


_NPAIR = H_ATT // 2
_QB, _KB, _VB = C_Q // 128, C_K // 128, C_V // 128
_SCALE = 1.0 / math.sqrt(64.0)
_LOG2E = math.log2(math.e)


def _attn_blocks(t):
    return _tile(t, (1408, 384, 256, 128)), _tile(t, (384, 128))


def _split3(c):
    hi = c.astype(BF16).astype(F32)
    rest = c - hi
    mid = rest.astype(BF16).astype(F32)
    return hi, mid, rest - mid


def _head_lanes(lane, hh):
    return (lane < 64, 64) if hh == 0 else (lane >= 64, 0)


def _q_operand(q, cq, lane, hh):
    sel, first = _head_lanes(lane, hh)
    out = jnp.where(sel, q, 0.0)
    for n, col in enumerate(_split3(cq) + (1.0, 1.0, 1.0)):
        out = jnp.where(lane == first + n, col, out)
    return out.astype(BF16)


def _k_operand(k, ck, lane, hh):
    sel, first = _head_lanes(lane, hh)
    hi, mid, lo = _split3(ck)
    out = jnp.where(sel, k, 0.0)
    for n, col in enumerate((1.0, 1.0, 1.0, -hi, -mid, -lo)):
        out = jnp.where(lane == first + n, col, out)
    return out.astype(BF16)


def _sum_operand(x, lane, hh, at):
    sel, first = _head_lanes(lane, hh)
    return jnp.where(sel, x, jnp.where(lane == first + at, 1.0, 0.0)).astype(BF16)


_C_FILLER = 2.0 ** 30
_SKIP_STEP = 256


def _query_skips(bq):
    firsts = list(range(0, bq, _SKIP_STEP))
    far = 1 << 30
    return [(q0 if n else -far, firsts[n + 1] if n + 1 < len(firsts) else far, q0) for n, q0 in enumerate(firsts)]


def _attn_fwd(proj, c_col):
    t = proj.shape[0]
    bq, bk = _attn_blocks(t)
    nq, nk = t // bq, t // bk
    rs = 32

    def last_kv(i):
        return (i * bq + bq - 1) // bk

    def body(q_ref, k_ref, v_ref, cq_ref, ck_ref, o_ref, lse_ref, p_ref, mrun_ref, qs_scr, s_scr, m_scr, acc_scr):
        i = pl.program_id(1)
        kk = pl.program_id(2)
        lane_q = _iota((bq, 128), 1)

        @pl.when(kk == 0)
        def _():
            m_scr[...] = jnp.full_like(m_scr, -1e30)
            acc_scr[...] = jnp.zeros_like(acc_scr)
            q = q_ref[...] * (_SCALE * _LOG2E)
            cq = cq_ref[0] * _LOG2E
            for hh in range(2):
                qs_scr[hh] = _q_operand(q, cq[:, hh:hh + 1], lane_q, hh)

        def step(masked, q0):
            nqc = bq - q0
            lane_k = _iota((bk, 128), 1)
            k = k_ref[...]
            v = v_ref[...]
            ck = ck_ref[0] * _LOG2E
            ahead = _iota((rs, nqc), 0) - _iota((rs, nqc), 1) - q0
            vss = []
            for hh in range(2):
                sel, first = _head_lanes(lane_k, hh)
                ks = _k_operand(k, ck[:, hh:hh + 1], lane_k, hh)
                vss.append(jnp.where(sel, v, jnp.where(lane_k == first, 1.0, 0.0)).astype(BF16))
                s_scr[hh, :, q0:] = _dot(ks, qs_scr[hh, q0:, :], _NT)
            for hh in range(2):
                vs = vss[hh]

                def block_max(r, mx):
                    rows = pl.ds(pl.multiple_of(r * rs, rs), rs)
                    s = s_scr[hh, rows, q0:]
                    if masked:
                        s = jnp.where(ahead <= i * bq - kk * bk - r * rs, s, -1e30)
                        s_scr[hh, rows, q0:] = s
                    return jnp.maximum(mx, s)

                mx = lax.fori_loop(0, bk // rs, block_max, jnp.full((rs, nqc), -1e30, F32), unroll=True)
                m_old = m_scr[hh, :, q0:]
                m_new = jnp.maximum(m_old, jnp.max(mx, axis=0, keepdims=True))
                m_scr[hh, :, q0:] = m_new
                mrun_ref[0, hh:hh + 1, q0:] = m_new

                def probs(r, carry):
                    rows = pl.ds(pl.multiple_of(r * rs, rs), rs)
                    p_ref[0, hh, rows, q0:] = jnp.exp2(s_scr[hh, rows, q0:] - m_new).astype(BF16)
                    return carry

                lax.fori_loop(0, bk // rs, probs, 0, unroll=True)
                acc_scr[hh, :, q0:] = (acc_scr[hh, :, q0:] * jnp.exp2(m_old - m_new)
                                       + _dot(vs, p_ref[0, hh, :, q0:], _TN))

        active = kk <= last_kv(i)
        ahead_by = kk * bk - i * bq
        for lo, hi, q0 in _query_skips(bq):
            @pl.when(active & (ahead_by + bk - 1 > 0) & (ahead_by >= lo) & (ahead_by < hi))
            def _(q0=q0):
                step(True, q0)

        @pl.when(active & jnp.logical_not(ahead_by + bk - 1 > 0))
        def _():
            step(False, 0)

        @pl.when(kk == nk - 1)
        def _():
            a = acc_scr[0]
            b = acc_scr[1]
            la = a[64:65, :]
            lb = b[0:1, :]
            o_ref[...] = jnp.where(lane_q < 64, (a / la).T, (b / lb).T)
            lse_ref[0] = jnp.concatenate([m_scr[0] + jnp.log(la) * _LOG2E, m_scr[1] + jnp.log(lb) * _LOG2E], axis=0)

    kvi = lambda i, kk: jnp.minimum(kk, last_kv(i))
    kv = lambda off: pl.BlockSpec((bk, 128), lambda j, i, kk: (kvi(i, kk), off + j))
    blk = lambda j, i, kk: (j * nq + i) * nk + kvi(i, kk)
    return pl.pallas_call(
        body, name="attn_fwd",
        out_shape=(jax.ShapeDtypeStruct((t, D_ATT), F32), jax.ShapeDtypeStruct((_NPAIR, 2, t), F32),
                   jax.ShapeDtypeStruct((_NPAIR * nq * nk, 2, bk, bq), BF16),
                   jax.ShapeDtypeStruct((_NPAIR * nq * nk, 2, bq), F32)),
        grid=(_NPAIR, nq, nk),
        in_specs=[pl.BlockSpec((bq, 128), lambda j, i, kk: (i, _QB + j)),
                  kv(_KB), kv(_VB),
                  pl.BlockSpec((1, bq, 2), lambda j, i, kk: (j, i, 0)),
                  pl.BlockSpec((1, bk, 2), lambda j, i, kk: (j, kvi(i, kk), 0))],
        out_specs=(pl.BlockSpec((bq, 128), lambda j, i, kk: (i, j)),
                   pl.BlockSpec((1, 2, bq), lambda j, i, kk: (j, 0, i)),
                   pl.BlockSpec((1, 2, bk, bq), lambda j, i, kk: (blk(j, i, kk), 0, 0, 0)),
                   pl.BlockSpec((1, 2, bq), lambda j, i, kk: (blk(j, i, kk), 0, 0))),
        scratch_shapes=[pltpu.VMEM((2, bq, 128), BF16), pltpu.VMEM((2, bk, bq), F32),
                        pltpu.VMEM((2, 1, bq), F32), pltpu.VMEM((2, 128, bq), F32)],
        compiler_params=_cp("parallel", "parallel", "arbitrary"),
    )(proj, proj, proj, c_col, c_col)


def _attn_bwd(proj, lse_row, dl_row, do, p_blocks, m_run, exchange=None):
    t = proj.shape[0]
    bq, bk = _attn_blocks(t)
    nq, nk = t // bq, t // bk
    rs = 16

    def first_q(kk):
        return (kk * bk) // bq

    def body(q_ref, k_ref, v_ref, lse_ref, dl_ref, do_ref, pblk_ref, mrun_ref,
             dq_ref, dk_ref, dv_ref, dck_ref, dcq_ref,
             qs_scr, doh_scr, ks_scr, dp_scr, p_scr, ds_scr, dq_scr, dk_scr, dv_scr):
        kk = pl.program_id(1)
        i = pl.program_id(2)
        lane_q = _iota((bq, 128), 1)
        lane_k = _iota((bk, 128), 1)
        qrows = pl.ds(pl.multiple_of(i * bq, 128), bq)

        @pl.when(kk == 0)
        def _():
            q = q_ref[...] * _SCALE
            do_ = do_ref[...]
            for hh in range(2):
                qs_scr[hh, qrows, :] = _sum_operand(q, lane_q, hh, 3)
                doh_scr[hh, qrows, :] = jnp.where(_head_lanes(lane_q, hh)[0], do_, 0.0).astype(BF16)
                dq_scr[hh, i] = jnp.zeros((128, bq), F32)

        @pl.when(i == 0)
        def _():
            dk_scr[...] = jnp.zeros_like(dk_scr)
            dv_scr[...] = jnp.zeros_like(dv_scr)
            k = k_ref[...]
            for hh in range(2):
                ks_scr[hh] = _sum_operand(k, lane_k, hh, 0)

        def step(q0):
            seen = pl.ds(pl.multiple_of(i * bq + q0, 128), bq - q0)
            v16 = v_ref[...].astype(BF16)
            dl = dl_ref[0, :, q0:]
            rescale = jnp.exp2(mrun_ref[0, :, q0:] - lse_ref[0, :, q0:])
            for hh in range(2):
                dp_scr[hh, :, q0:] = _dot(v16, doh_scr[hh, seen, :], _NT)
            for hh in range(2):
                qs = qs_scr[hh, seen, :]
                doh = doh_scr[hh, seen, :]

                def strip(r, carry):
                    rows = pl.ds(pl.multiple_of(r * rs, rs), rs)
                    p = pblk_ref[0, hh, rows, q0:].astype(F32) * rescale[hh:hh + 1, :]
                    p_scr[hh, rows, q0:] = p.astype(BF16)
                    ds_scr[hh, rows, q0:] = (p * (dp_scr[hh, rows, q0:] - dl[hh:hh + 1, :])).astype(BF16)
                    return carry

                lax.fori_loop(0, bk // rs, strip, 0, unroll=True)
                dv_scr[...] += _dot(p_scr[hh, :, q0:], doh)
                dk_scr[hh] += _dot(ds_scr[hh, :, q0:], qs)
                dq_scr[hh, i, :, q0:] += _dot(ks_scr[hh], ds_scr[hh, :, q0:], _TN)

        ahead_by = kk * bk - i * bq
        for lo, hi, q0 in _query_skips(bq):
            @pl.when((i >= first_q(kk)) & (ahead_by >= lo) & (ahead_by < hi))
            def _(q0=q0):
                step(q0)

        @pl.when(i == nq - 1)
        def _():
            dka = dk_scr[0]
            dkb = dk_scr[1]
            dk_ref[...] = jnp.where(lane_k < 64, dka, dkb).astype(BF16)
            dv_ref[...] = dv_scr[...].astype(BF16)
            dck_ref[0] = -jnp.where(_iota((bk, 2), 1) == 0, dka[:, 67:68], dkb[:, 3:4])

        @pl.when((kk == nk - 1) & (i == nq - 1))
        def _():
            for ii in range(nq):
                cols = slice(ii * bq, (ii + 1) * bq)
                dqa = dq_scr[0, ii]
                dqb = dq_scr[1, ii]
                dq_ref[cols, :] = (jnp.where(lane_q < 64, dqa.T, dqb.T) * _SCALE).astype(BF16)
                dcq_ref[0, :, cols] = jnp.concatenate([dqa[64:65, :], dqb[0:1, :]], axis=0)

    qi = lambda kk, i: jnp.where(kk == 0, i, nq - 1)
    qspec = lambda off: pl.BlockSpec((bq, 128), lambda j, kk, i: (qi(kk, i), off + j))
    kspec = lambda off: pl.BlockSpec((bk, 128), lambda j, kk, i: (kk, off + j))
    rowspec = pl.BlockSpec((1, 2, bq), lambda j, kk, i: (j, 0, jnp.maximum(i, first_q(kk))))
    blk = lambda j, kk, i: (j * nq + jnp.maximum(i, first_q(kk))) * nk + kk
    return _hosted_call(
        body, name="attn_bwd",
        out_shape=(jax.ShapeDtypeStruct((t, D_ATT), BF16), jax.ShapeDtypeStruct((t, D_ATT), BF16),
                   jax.ShapeDtypeStruct((t, D_ATT), BF16), jax.ShapeDtypeStruct((_NPAIR, t, 2), F32),
                   jax.ShapeDtypeStruct((_NPAIR, 2, t), F32)),
        grid=(_NPAIR, nk, nq),
        in_specs=[qspec(_QB), kspec(_KB), kspec(_VB),
                  rowspec, rowspec, qspec(0),
                  pl.BlockSpec((1, 2, bk, bq), lambda j, kk, i: (blk(j, kk, i), 0, 0, 0)),
                  pl.BlockSpec((1, 2, bq), lambda j, kk, i: (blk(j, kk, i), 0, 0))],
        out_specs=(pl.BlockSpec((t, 128), lambda j, kk, i: (0, j)),
                   pl.BlockSpec((bk, 128), lambda j, kk, i: (kk, j)),
                   pl.BlockSpec((bk, 128), lambda j, kk, i: (kk, j)),
                   pl.BlockSpec((1, bk, 2), lambda j, kk, i: (j, kk, 0)),
                   pl.BlockSpec((1, 2, t), lambda j, kk, i: (j, 0, 0))),
        scratch_shapes=[pltpu.VMEM((2, t, 128), BF16), pltpu.VMEM((2, t, 128), BF16), pltpu.VMEM((2, bk, 128), BF16),
                        pltpu.VMEM((2, bk, bq), F32),
                        pltpu.VMEM((2, bk, bq), BF16), pltpu.VMEM((2, bk, bq), BF16),
                        pltpu.VMEM((2, nq, 128, bq), F32), pltpu.VMEM((2, bk, 128), F32), pltpu.VMEM((bk, 128), F32)],
        operands=(proj, proj, proj, lse_row, dl_row, do, p_blocks, m_run),
        semantics=("parallel", "arbitrary", "arbitrary"), exchange=exchange)


def _premerge_fwd(y, o, proj, gamma):
    t = y.shape[0]
    tm = _row_tile_wide(t)

    def body(y_ref, z_ref, o_ref, za_ref, g_ref, ys_ref, ya_ref):
        z = z_ref[...]
        u = y_ref[...] * (z * _sigmoid(z))
        for g in range(G_SSD):
            gs = slice(_GW * g, _GW * (g + 1))
            ug = u[:, gs]
            r = lax.rsqrt(jnp.mean(ug * ug, axis=-1, keepdims=True) + EPS)
            ys_ref[:, gs] = (ug * r * g_ref[:, gs]).astype(BF16)
        za = za_ref[...]
        ya_ref[...] = (o_ref[...] * (za * _sigmoid(za))).astype(BF16)

    return pl.pallas_call(
        body, name="premerge_fwd",
        out_shape=(jax.ShapeDtypeStruct((t, D_SSD), BF16), jax.ShapeDtypeStruct((t, D_ATT), BF16)),
        grid=(t // tm,),
        in_specs=[pl.BlockSpec((tm, D_SSD), lambda i: (i, 0)),
                  pl.BlockSpec((tm, D_SSD), lambda i: (i, C_Z // D_SSD)),
                  pl.BlockSpec((tm, D_ATT), lambda i: (i, 0)),
                  pl.BlockSpec((tm, D_ATT), lambda i: (i, C_ZA // D_ATT)),
                  pl.BlockSpec((1, D_SSD), lambda i: (0, 0))],
        out_specs=(pl.BlockSpec((tm, D_SSD), lambda i: (i, 0)), pl.BlockSpec((tm, D_ATT), lambda i: (i, 0))),
        compiler_params=_cp("parallel"),
    )(y, proj, o, proj, gamma)


def _premerge_bwd(dys, dya, y, o, proj, gamma, exchange=None):
    t = y.shape[0]
    tm = _row_tile_wide(t)

    def body(dys_ref, dya_ref, y_ref, z_ref, o_ref, za_ref, g_ref, dy_ref, dz_ref, do_ref, dza_ref, dg_ref, dl_ref):
        i = pl.program_id(0)
        z = z_ref[...]
        sz = _sigmoid(z)
        silu = z * sz
        dsilu = sz * (1.0 + z * (1.0 - sz))
        yv = y_ref[...]
        u = yv * silu
        parts = []
        for g in range(G_SSD):
            gs = slice(_GW * g, _GW * (g + 1))
            ug = u[:, gs]
            r = lax.rsqrt(jnp.mean(ug * ug, axis=-1, keepdims=True) + EPS)
            n = ug * r
            dout = dys_ref[:, gs]
            dn = dout * g_ref[:, gs]
            du = r * (dn - n * jnp.mean(dn * n, axis=-1, keepdims=True))
            dy_ref[:, gs] = du * silu[:, gs]
            dz_ref[:, gs] = (du * yv[:, gs] * dsilu[:, gs]).astype(BF16)
            parts.append(jnp.sum(dout * n, axis=0, keepdims=True))
        dg = jnp.concatenate(parts, axis=1)
        za = za_ref[...]
        sa = _sigmoid(za)
        dya_ = dya_ref[...]
        ov = o_ref[...]
        do = dya_ * (za * sa)
        do_ref[...] = do
        dza_ref[...] = (dya_ * ov * (sa * (1.0 + za * (1.0 - sa)))).astype(BF16)
        pick = (jnp.right_shift(_iota((D_ATT, 128), 0), 6) == _iota((D_ATT, 128), 1)).astype(F32)
        dl_ref[...] = _dot_exact(do * ov, pick)

        @pl.when(i == 0)
        def _():
            dg_ref[...] = dg

        @pl.when(i > 0)
        def _():
            dg_ref[...] += dg

    ssd = pl.BlockSpec((tm, D_SSD), lambda i: (i, 0))
    att = pl.BlockSpec((tm, D_ATT), lambda i: (i, 0))
    vec = pl.BlockSpec((1, D_SSD), lambda i: (0, 0))
    return _hosted_call(
        body, name="premerge_bwd",
        out_shape=(jax.ShapeDtypeStruct((t, D_SSD), F32), jax.ShapeDtypeStruct((t, D_SSD), BF16),
                   jax.ShapeDtypeStruct((t, D_ATT), F32), jax.ShapeDtypeStruct((t, D_ATT), BF16),
                   jax.ShapeDtypeStruct((1, D_SSD), F32), jax.ShapeDtypeStruct((t, 128), F32)),
        grid=(t // tm,),
        in_specs=[ssd, att, ssd, pl.BlockSpec((tm, D_SSD), lambda i: (i, C_Z // D_SSD)), att,
                  pl.BlockSpec((tm, D_ATT), lambda i: (i, C_ZA // D_ATT)), vec],
        out_specs=(ssd, ssd, att, att, vec, pl.BlockSpec((tm, 128), lambda i: (i, 0))),
        scratch_shapes=[],
        operands=(dys, dya, y, proj, o, proj, gamma), semantics=("arbitrary",), exchange=exchange)


_G_BLK = C_G // D_MODEL


def _merge_fwd(a, b, proj, gate_bias):
    t = a.shape[0]
    tm = _row_tile(t)

    def body(a_ref, b_ref, gs_ref, ga_ref, bias_ref, m_ref):
        g_ssd = _sigmoid(gs_ref[...] + bias_ref[:, 0:D_MODEL])
        g_att = _sigmoid(ga_ref[...] + bias_ref[:, D_MODEL:2 * D_MODEL])
        m_ref[...] = (g_ssd * a_ref[...] + g_att * b_ref[...]).astype(BF16)

    row = pl.BlockSpec((tm, D_MODEL), lambda i: (i, 0))
    return pl.pallas_call(
        body, name="merge_fwd",
        out_shape=jax.ShapeDtypeStruct((t, D_MODEL), BF16),
        grid=(t // tm,),
        in_specs=[row, row,
                  pl.BlockSpec((tm, D_MODEL), lambda i: (i, _G_BLK)),
                  pl.BlockSpec((tm, D_MODEL), lambda i: (i, _G_BLK + 1)),
                  pl.BlockSpec((1, 2 * D_MODEL), lambda i: (0, 0))],
        out_specs=row,
        compiler_params=_cp("parallel"),
    )(a, b, proj, proj, gate_bias)


def _merge_bwd(dm, a, b, proj, gate_bias):
    t = a.shape[0]
    tm = _row_tile(t)

    def body(dm_ref, a_ref, b_ref, gs_ref, ga_ref, bias_ref, da_ref, db_ref, dg_ref, dbias_ref):
        i = pl.program_id(0)
        dm_ = dm_ref[...]
        g_ssd = _sigmoid(gs_ref[...] + bias_ref[:, 0:D_MODEL])
        g_att = _sigmoid(ga_ref[...] + bias_ref[:, D_MODEL:2 * D_MODEL])
        da_ref[...] = (dm_ * g_ssd).astype(BF16)
        db_ref[...] = (dm_ * g_att).astype(BF16)
        dgs = dm_ * a_ref[...] * g_ssd * (1.0 - g_ssd)
        dga = dm_ * b_ref[...] * g_att * (1.0 - g_att)
        dg_ref[:, 0:D_MODEL] = dgs.astype(BF16)
        dg_ref[:, D_MODEL:2 * D_MODEL] = dga.astype(BF16)
        part = jnp.concatenate([jnp.sum(dgs, axis=0, keepdims=True), jnp.sum(dga, axis=0, keepdims=True)], axis=1)

        @pl.when(i == 0)
        def _():
            dbias_ref[...] = part

        @pl.when(i > 0)
        def _():
            dbias_ref[...] += part

    row = pl.BlockSpec((tm, D_MODEL), lambda i: (i, 0))
    wide = pl.BlockSpec((tm, 2 * D_MODEL), lambda i: (i, 0))
    vec = pl.BlockSpec((1, 2 * D_MODEL), lambda i: (0, 0))
    return pl.pallas_call(
        body, name="merge_bwd",
        out_shape=(jax.ShapeDtypeStruct((t, D_MODEL), BF16), jax.ShapeDtypeStruct((t, D_MODEL), BF16),
                   jax.ShapeDtypeStruct((t, 2 * D_MODEL), BF16), jax.ShapeDtypeStruct((1, 2 * D_MODEL), F32)),
        grid=(t // tm,),
        in_specs=[row, row, row,
                  pl.BlockSpec((tm, D_MODEL), lambda i: (i, _G_BLK)),
                  pl.BlockSpec((tm, D_MODEL), lambda i: (i, _G_BLK + 1)), vec],
        out_specs=(row, row, wide, vec),
        compiler_params=_cp("arbitrary"),
    )(dm, a, b, proj, proj, gate_bias)


def _post(o2, h, target, g):
    t = o2.shape[0]
    nc = t // CHUNK

    def body(o_ref, h_ref, t_ref, g_ref, dy_ref, do_ref, dg_ref, loss_ref):
        c = pl.program_id(0)
        x = o_ref[...]
        r = lax.rsqrt(jnp.mean(x * x, axis=-1, keepdims=True) + EPS)
        n = x * r
        y = h_ref[...] + n * g_ref[...]
        diff = jnp.where(c > 0, y - t_ref[...], 0.0)
        dy = diff * (1.0 / D_MODEL)
        dy_ref[...] = dy
        gdy = dy * g_ref[...]
        do_ref[...] = (r * (gdy - n * jnp.mean(gdy * n, axis=-1, keepdims=True))).astype(BF16)
        dg = jnp.sum(dy * n, axis=0, keepdims=True)
        lpart = 0.5 * jnp.sum(jnp.sum(diff * diff, axis=1, keepdims=True), axis=0, keepdims=True) * (1.0 / D_MODEL)
        sel = (_iota((8, 128), 0) == 0) & (_iota((8, 128), 1) == 0)

        @pl.when(c == 0)
        def _():
            dg_ref[...] = dg
            loss_ref[...] = jnp.zeros_like(loss_ref)

        @pl.when(c > 0)
        def _():
            dg_ref[...] += dg
            loss_ref[...] += jnp.where(sel, lpart, 0.0)

    row = pl.BlockSpec((CHUNK, D_MODEL), lambda c: (c, 0))
    vec = pl.BlockSpec((1, D_MODEL), lambda c: (0, 0))
    return pl.pallas_call(
        body, name="post",
        out_shape=(jax.ShapeDtypeStruct((t, D_MODEL), F32), jax.ShapeDtypeStruct((t, D_MODEL), BF16),
                   jax.ShapeDtypeStruct((1, D_MODEL), F32), jax.ShapeDtypeStruct((8, 128), F32)),
        grid=(nc,),
        in_specs=[row, row, pl.BlockSpec((CHUNK, D_MODEL), lambda c: (jnp.maximum(c - 1, 0), 0)), vec],
        out_specs=(row, row, vec, pl.BlockSpec((8, 128), lambda c: (0, 0))),
        compiler_params=_cp("arbitrary"),
    )(o2, h, target, g)


def _mm_tiles(t):
    return _tile(t, (704, 384, 128))


def _local_step(h, target, w_main, w_small, pr_slots, ids, norm_pre, conv_w, conv_b, bias_row, a_row,
                dsk_row, ssd_norm, gate_bias, norm_post):
    t = h.shape[0]
    tm = _mm_tiles(t)
    u = _norm1_fwd(h, norm_pre)
    proj, pr_slots = _matmul(u, w_main, "nt", F32, "inproj", tm, 1024, D_MODEL,
                             exchange=_gather_stage([pr_slots], to_sibling=False))
    small, pr_slots = _matmul(u, w_small, "nt", F32, "inproj_small", tm, N_SMALL, D_MODEL,
                              exchange=_gather_stage([pr_slots], to_sibling=True))
    wps = pr_slots[:, 0:512].reshape(D_SSD, D_MODEL)
    wpa = pr_slots[:, 512:768].reshape(D_ATT, D_MODEL)
    wout = pr_slots[:, 768:1024].reshape(D_MODEL, D_MODEL)
    dtlf = _small_fwd(small, bias_row)
    xbc = _conv_fwd(proj, conv_w, conv_b)
    y, hin = _ssd_fwd(xbc, dtlf, a_row, dsk_row)
    c_tok = dtlf[:, H_SSD:H_SSD + H_ATT]
    c_tok = jnp.where(jnp.arange(t)[:, None] < PADF, _C_FILLER, c_tok)
    c_col = c_tok.reshape(t, _NPAIR, 2).transpose(1, 0, 2)
    o, lse, p_blocks, m_run = _attn_fwd(proj, c_col)
    ys, ya = _premerge_fwd(y, o, proj, ssd_norm)
    a = _matmul(ys, wps, "nn", F32, "proj_ssd", tm, D_MODEL, D_SSD)
    b = _matmul(ya, wpa, "nn", F32, "proj_att", tm, D_MODEL, D_ATT)
    merged = _merge_fwd(a, b, proj, gate_bias)
    o2 = _matmul(merged, wout, "nn", F32, "out_proj", tm, D_MODEL, D_MODEL)
    dy_out, do2, d_norm_post, loss_blk = _post(o2, h, target, norm_post)

    dm = _matmul(do2, wout, "nt", F32, "out_proj_dx", tm, D_MODEL, D_MODEL)
    d_wout = _matmul(merged, do2, "tn", F32, "out_proj_dw", D_MODEL, D_MODEL, tm)
    da, db, dgraw, d_gate_bias = _merge_bwd(dm, a, b, proj, gate_bias)
    dys = _matmul(da, wps, "nt", F32, "proj_ssd_dx", tm, D_SSD, D_MODEL)
    d_wps = _matmul(ys, da, "tn", F32, "proj_ssd_dw", D_SSD, D_MODEL, tm)
    dya = _matmul(db, wpa, "nt", F32, "proj_att_dx", tm, D_ATT, D_MODEL)
    d_wpa = _matmul(ya, db, "tn", F32, "proj_att_dw", D_ATT, D_MODEL, tm)
    g32_pr = jnp.concatenate([d_wps.reshape(4, 512, D_MODEL), d_wpa.reshape(4, 256, D_MODEL),
                              d_wout.reshape(4, 256, D_MODEL)], axis=1)
    dy, dz, do, dza, d_ssd_norm, dl, ra_pr = _premerge_bwd(dys, dya, y, o, proj, ssd_norm,
                                                           exchange=_pair_swap([g32_pr]))
    pb_pr = _add_pair(ids, g32_pr, ra_pr)
    dl_row = dl[:, 0:H_ATT].T.reshape(_NPAIR, 2, t)
    dq, dk, dv, dc_key, dc_qry, rb_pr = _attn_bwd(proj, lse, dl_row, do, p_blocks, m_run,
                                                  exchange=_chip_exchange([pb_pr]))
    half_pr = _add_chips(ids, g32_pr, ra_pr, rb_pr)
    dxbc, ddt, d_a, d_dsk = _ssd_bwd(xbc, dtlf, a_row, dsk_row, hin, dy)
    dxbc_raw, d_conv_w, d_conv_b = _conv_bwd(dxbc, proj, conv_w, conv_b)
    dc_tok = jnp.transpose(dc_key, (1, 0, 2)).reshape(t, H_ATT) + dc_qry.reshape(H_ATT, t).T
    dsm = ddt + jnp.pad(dc_tok, ((0, 0), (H_SSD, N_SMALL - H_SSD - H_ATT)))
    dsmall, d_bias_row = _small_bwd(dsm, small, bias_row)
    dproj = [dz, dxbc_raw, dza, dq, dk, dv, dgraw]
    return dict(loss_blk=loss_blk, u=u, dy_out=dy_out, dproj=dproj, dsmall=dsmall, half_pr=half_pr,
                d_conv_w=d_conv_w, d_conv_b=d_conv_b,
                d_bias_row=d_bias_row, d_a=d_a, d_dsk=d_dsk, d_ssd_norm=d_ssd_norm,
                d_gate_bias=d_gate_bias, d_norm_post=d_norm_post)


def _to_aligned_rows(slots):
    w = slots.reshape(N_COLS, slots.shape[2])

    def cut(o):
        return w[o[0]:o[0] + o[1]]
    main = jnp.concatenate([cut(O_Z), cut(O_XBC), cut(O_ZA), cut(O_Q), cut(O_K), cut(O_V), cut(O_G)], axis=0)
    pad = jnp.zeros((N_SMALL - H_SSD - H_ATT, w.shape[1]), w.dtype)
    small = jnp.concatenate([cut(O_DT), cut(O_F), pad], axis=0)
    assert main.shape[0] == N_MAIN and small.shape[0] == N_SMALL
    return main, small


def _from_aligned_rows(main, small):
    def cm(c0, n):
        return main[c0:c0 + n]
    flat = jnp.concatenate([cm(C_Z, 2048), cm(C_XBC, 3072), small[0:H_SSD], cm(C_ZA, 1024),
                            cm(C_Q, 1024), cm(C_K, 1024), cm(C_V, 1024), small[H_SSD:H_SSD + H_ATT],
                            cm(C_G, 2048)], axis=0)
    return flat.reshape(4, N_COLS // 4, flat.shape[1])


_MESH = pl.DeviceIdType.MESH
_ANY = pl.BlockSpec(memory_space=pl.ANY)
_VM = pl.BlockSpec(memory_space=pltpu.VMEM)
_HALF = 512
N_DEV = 8


def _coords():
    return lax.axis_index("x"), lax.axis_index("y"), lax.axis_index("c")


def _other_chips(x, y):
    return [(1 - x, y), (x, 1 - y), (1 - x, 1 - y)]


def _half(cc):
    return pl.ds(cc * _HALF, _HALF)


def _gather_shards(slots):
    n = len(slots)

    def body(*refs):
        buf = refs[n:2 * n]
        send_sems, recv_sems = refs[2 * n:]
        x, y, c = _coords()
        chip = 2 * x + y
        sibling = (x, y, 1 - c)
        chips = _other_chips(x, y)

        def copy(i, frm, cc, k, to):
            part = buf[i].at[frm, :, _half(cc)]
            return pltpu.make_async_remote_copy(src_ref=part, dst_ref=part, send_sem=send_sems.at[6 * i + k],
                                                recv_sem=recv_sems.at[6 * i + k], device_id=to, device_id_type=_MESH)

        def chip_of(k):
            return 2 * chips[k][0] + chips[k][1]

        first = [copy(i, chip, c, k, (*chips[k], c)) for k in range(3) for i in range(n)]
        for cp in first:
            cp.start()
        passed = []
        for k in range(3):
            for i in range(n):
                copy(i, chip_of(k), c, k, (*chips[k], c)).wait_recv()
                passed.append(copy(i, chip_of(k), c, 3 + k, sibling))
                passed[-1].start()
        for k in range(3):
            for i in range(n):
                copy(i, chip_of(k), 1 - c, 3 + k, sibling).wait_recv()
        for cp in first + passed:
            cp.wait_send()

    return pl.pallas_call(
        body, name="gather_shards",
        out_shape=tuple(jax.ShapeDtypeStruct(s.shape, s.dtype) for s in slots),
        in_specs=[_ANY] * n, out_specs=tuple([_ANY] * n),
        input_output_aliases={i: i for i in range(n)},
        scratch_shapes=[pltpu.SemaphoreType.DMA((6 * n,)), pltpu.SemaphoreType.DMA((6 * n,))],
    )(*slots)


def _allgather8(block, name):
    rows, width = block.shape

    def body(x_ref, out_ref, send_sems, recv_sems, local_sem):
        x, y, c = _coords()
        me, sibling = (x, y, c), (x, y, 1 - c)
        chips = _other_chips(x, y)

        def slot(px, py, pc):
            return out_ref.at[4 * px + 2 * py + pc]

        def copy(k, blk, to, src=None):
            return pltpu.make_async_remote_copy(src_ref=slot(*blk) if src is None else src, dst_ref=slot(*blk),
                                                send_sem=send_sems.at[k], recv_sem=recv_sems.at[k],
                                                device_id=to, device_id_type=_MESH)

        mine = pltpu.make_async_copy(x_ref, slot(*me), local_sem)
        mine.start()
        first = [copy(0, me, sibling, src=x_ref)]
        first += [copy(1 + j, me, (*chip, c), src=x_ref) for j, chip in enumerate(chips)]
        for cp in first:
            cp.start()
        passed = [copy(4 + j, (*chip, c), sibling) for j, chip in enumerate(chips)]
        for j, chip in enumerate(chips):
            copy(1 + j, (*chip, c), me).wait_recv()
            passed[j].start()
        copy(0, sibling, me).wait_recv()
        for j, chip in enumerate(chips):
            copy(4 + j, (*chip, 1 - c), me).wait_recv()
        for cp in first + passed:
            cp.wait_send()
        mine.wait()

    return pl.pallas_call(
        body, name=name,
        out_shape=jax.ShapeDtypeStruct((N_DEV, rows, width), block.dtype),
        in_specs=[_VM], out_specs=_VM,
        scratch_shapes=[pltpu.SemaphoreType.DMA((7,)), pltpu.SemaphoreType.DMA((7,)), pltpu.SemaphoreType.DMA],
    )(block)


def _pair_swap(arrs):
    def copies(src, dst, send_sems, recv_sems):
        x, y, c = _coords()
        return [pltpu.make_async_remote_copy(src_ref=src[i].at[:, :, _half(1 - c)], dst_ref=dst[i],
                                             send_sem=send_sems.at[i], recv_sem=recv_sems.at[i],
                                             device_id=(x, y, 1 - c), device_id_type=_MESH) for i in range(len(src))]

    shapes = tuple(jax.ShapeDtypeStruct((4, a.shape[1], _HALF), a.dtype) for a in arrs)
    return tuple(arrs), shapes, copies, len(arrs), False


def _chip_exchange(arrs):
    def copies(src, dst, send_sems, recv_sems):
        x, y, c = _coords()
        chips = _other_chips(x, y)
        return [pltpu.make_async_remote_copy(src_ref=src[i].at[2 * chips[k][0] + chips[k][1]], dst_ref=dst[i].at[k],
                                             send_sem=send_sems.at[3 * i + k], recv_sem=recv_sems.at[3 * i + k],
                                             device_id=(*chips[k], c), device_id_type=_MESH)
                for k in range(3) for i in range(len(src))]

    shapes = tuple(jax.ShapeDtypeStruct((3,) + a.shape[1:], a.dtype) for a in arrs)
    return tuple(arrs), shapes, copies, 3 * len(arrs), False


def _gather_stage(slots, to_sibling):
    def copies(buf, _, send_sems, recv_sems):
        x, y, c = _coords()
        chips = _other_chips(x, y)
        out = []
        for k in range(3):
            for i in range(len(buf)):
                frm = 2 * chips[k][0] + chips[k][1] if to_sibling else 2 * x + y
                part = buf[i].at[frm, :, _half(c)]
                out.append(pltpu.make_async_remote_copy(
                    src_ref=part, dst_ref=part, send_sem=send_sems.at[3 * i + k], recv_sem=recv_sems.at[3 * i + k],
                    device_id=(x, y, 1 - c) if to_sibling else (*chips[k], c), device_id_type=_MESH))
        return out

    shapes = tuple(jax.ShapeDtypeStruct(s.shape, s.dtype) for s in slots)
    return tuple(slots), shapes, copies, 3 * len(slots), True


def _pair_join_halves(fulls):
    n = len(fulls)

    def body(*refs):
        buf = refs[n:2 * n]
        send_sems, recv_sems = refs[2 * n:]
        x, y, c = _coords()

        def remote(i, cc):
            part = buf[i].at[:, _half(cc)]
            return pltpu.make_async_remote_copy(src_ref=part, dst_ref=part, send_sem=send_sems.at[i],
                                                recv_sem=recv_sems.at[i], device_id=(x, y, 1 - c), device_id_type=_MESH)

        for i in range(n):
            remote(i, c).start()
        for i in range(n):
            remote(i, c).wait_send()
            remote(i, 1 - c).wait_recv()

    return pl.pallas_call(
        body, name="pair_join_halves",
        out_shape=tuple(jax.ShapeDtypeStruct(a.shape, a.dtype) for a in fulls),
        in_specs=[_ANY] * n, out_specs=tuple([_ANY] * n),
        input_output_aliases={i: i for i in range(n)},
        scratch_shapes=[pltpu.SemaphoreType.DMA((n,)), pltpu.SemaphoreType.DMA((n,))],
    )(*fulls)


_RED_TC = 128
_RED_NT = _HALF // _RED_TC


def _add_pair(ids, g32, recv_a):
    rows = g32.shape[1]

    def body(ids_ref, g_ref, r_ref, o_ref):
        o_ref[...] = (g_ref[...] + r_ref[...]).astype(BF16)

    blk = pl.BlockSpec((1, rows, _RED_TC), lambda j, l, ids: (j, 0, l))
    return pl.pallas_call(
        body, name="add_pair",
        out_shape=jax.ShapeDtypeStruct((4, rows, _HALF), BF16),
        grid_spec=pltpu.PrefetchScalarGridSpec(
            num_scalar_prefetch=1, grid=(4, _RED_NT),
            in_specs=[pl.BlockSpec((1, rows, _RED_TC), lambda j, l, ids: (j, 0, ids[0] * _RED_NT + l)), blk],
            out_specs=blk),
        compiler_params=_cp("parallel", "parallel"),
    )(ids, g32, recv_a)


def _add_chips(ids, g32, recv_a, recv_b):
    rows = g32.shape[1]

    def body(ids_ref, g_ref, a_ref, b_ref, o_ref):
        acc = g_ref[0] + a_ref[0]
        for k in range(3):
            acc = acc + b_ref[k].astype(F32)
        o_ref[...] = acc

    return pl.pallas_call(
        body, name="add_chips",
        out_shape=jax.ShapeDtypeStruct((rows, 2 * _HALF), F32),
        grid_spec=pltpu.PrefetchScalarGridSpec(
            num_scalar_prefetch=1, grid=(_RED_NT,),
            in_specs=[pl.BlockSpec((1, rows, _RED_TC), lambda l, ids: (ids[1], 0, ids[0] * _RED_NT + l)),
                      pl.BlockSpec((1, rows, _RED_TC), lambda l, ids: (ids[1], 0, l)),
                      pl.BlockSpec((3, rows, _RED_TC), lambda l, ids: (0, 0, l))],
            out_specs=pl.BlockSpec((rows, _RED_TC), lambda l, ids: (0, ids[0] * _RED_NT + l))),
        compiler_params=_cp("parallel"),
    )(ids, g32, recv_a, recv_b)


def _sum8(gathered):
    _, rows, width = gathered.shape

    def body(g_ref, o_ref):
        acc = g_ref[0]
        for d in range(1, N_DEV):
            acc = acc + g_ref[d]
        o_ref[...] = acc

    return pl.pallas_call(
        body, name="sum8",
        out_shape=jax.ShapeDtypeStruct((rows, width), F32),
        in_specs=[_VM], out_specs=_VM,
    )(gathered)


def _adamw(w, g, m, v, name):
    rows, cols = w.shape
    budget = (3 << 20) // 2
    tr, tc = rows, cols
    if rows * cols * 4 > budget:
        if rows % 8 == 0:
            tr = max(c for c in range(8, rows, 8) if rows % c == 0 and c * cols * 4 <= budget)
        else:
            tc = next(c for c in (512, 256, 128) if cols % c == 0 and rows * c * 4 <= budget)
    c1 = 1.0 - ADAM_B1 ** ADAM_STEP
    c2 = 1.0 - ADAM_B2 ** ADAM_STEP

    def body(w_ref, g_ref, m_ref, v_ref, d_ref, mo_ref, vo_ref):
        gg = g_ref[...]
        mn = ADAM_B1 * m_ref[...] + (1.0 - ADAM_B1) * gg
        vn = ADAM_B2 * v_ref[...] + (1.0 - ADAM_B2) * (gg * gg)
        mo_ref[...] = mn
        vo_ref[...] = vn
        d_ref[...] = -ADAM_LR * ((mn / c1) / (jnp.sqrt(vn / c2) + ADAM_EPS) + ADAM_WD * w_ref[...])

    blk = pl.BlockSpec((tr, tc), lambda i, j: (i, j))
    shp = jax.ShapeDtypeStruct((rows, cols), F32)
    return pl.pallas_call(
        body, name=name, out_shape=(shp, shp, shp), grid=(rows // tr, cols // tc),
        in_specs=[blk] * 4, out_specs=(blk, blk, blk),
        compiler_params=_cp("parallel", "parallel"),
    )(w, g, m, v)


def _rows128(a):
    return a.reshape(-1, 128)


def _pack_small(norm_pre, conv_b, ssd_norm, gate_bias, norm_post, dt_bias, a_log, d_skip, fgate_bias):
    tiny = jnp.concatenate([dt_bias.reshape(-1), a_log.reshape(-1), d_skip.reshape(-1), fgate_bias.reshape(-1),
                            jnp.zeros((16,), F32)])
    return jnp.concatenate([_rows128(norm_pre), _rows128(conv_b), _rows128(ssd_norm), _rows128(gate_bias),
                            _rows128(norm_post), tiny.reshape(1, 128)], axis=0)


_SMALL_PAD = 80


def _unpack_small(p):
    tiny = p[72]
    return dict(norm_pre=p[0:8].reshape(1, 1024), conv_b=p[8:32].reshape(1, 3072), ssd_norm=p[32:48].reshape(1, 2048),
                gate_bias=p[48:64].reshape(1, 2048), norm_post=p[64:72].reshape(1, 1024),
                dt_bias=tiny[0:32].reshape(1, 32), a_log=tiny[32:64].reshape(1, 32),
                d_skip=tiny[64:96].reshape(1, 32), fgate_bias=tiny[96:112].reshape(1, 16))


def _pad_rows(a, rows):
    return jnp.concatenate([a, jnp.zeros((rows - a.shape[0], a.shape[1]), a.dtype)], axis=0)


def kernel(x, meta_tokens, norm_pre, w_in, conv_w, conv_b, dt_bias, a_log, d_skip, ssd_norm, fgate_bias, gate_bias, w_proj_ssd, w_proj_att, w_out, norm_post, loss_target, m_meta_tokens, m_norm_pre, m_w_in, m_conv_w, m_conv_b, m_dt_bias, m_a_log, m_d_skip, m_ssd_norm, m_fgate_bias, m_gate_bias, m_w_proj_ssd, m_w_proj_att, m_w_out, m_norm_post, v_meta_tokens, v_norm_pre, v_w_in, v_conv_w, v_conv_b, v_dt_bias, v_a_log, v_d_skip, v_ssd_norm, v_fgate_bias, v_gate_bias, v_w_proj_ssd, v_w_proj_att, v_w_out, v_norm_post):
    cx, cy, cc = _coords()
    chip = 2 * cx + cy
    ids = jnp.stack([cc, chip]).astype(jnp.int32)
    seq = x.shape[1]

    w_in_sh = jnp.transpose(w_in[0]).astype(BF16)
    w_pr_sh = jnp.concatenate([w_proj_ssd[0], w_proj_att[0], w_out[0]], axis=0).astype(BF16)

    def own_slot(sh):
        return lax.dynamic_update_slice(lax.empty((4,) + sh.shape, sh.dtype), sh[None], (chip, 0, 0))

    (g_in,) = _gather_shards([own_slot(w_in_sh)])
    w_main, w_small = _to_aligned_rows(g_in)
    sm_sh = jnp.concatenate([_rows128(meta_tokens), _rows128(conv_w[0])], axis=0)
    sm_all = _allgather8(sm_sh, "gather_small_weights")[0::2]
    meta_full = jnp.transpose(sm_all[:, 0:32].reshape(4, N_META, 256), (1, 0, 2)).reshape(N_META, D_MODEL)
    conv_w_full = jnp.transpose(sm_all[:, 32:56].reshape(4, CONV_K, 768), (1, 0, 2)).reshape(CONV_K, CONV_DIM)

    h = jnp.concatenate([jnp.zeros((PADF, D_MODEL), F32), meta_full, x[0]], axis=0)
    bias_row = jnp.concatenate([dt_bias[0], fgate_bias[0], jnp.zeros((N_SMALL - H_SSD - H_ATT,), F32)]).reshape(1, N_SMALL)
    a_neg = -jnp.exp(a_log[0])
    a_row = jnp.concatenate([a_neg, jnp.zeros((N_SMALL - H_SSD,), F32)]).reshape(1, N_SMALL)
    dsk_row = jnp.repeat(d_skip[0], 64).reshape(1, D_SSD)
    r = _local_step(h, loss_target[0], w_main, w_small, own_slot(w_pr_sh), ids, norm_pre, conv_w_full, conv_b,
                    bias_row, a_row, dsk_row, ssd_norm, gate_bias, norm_post)

    tm = _mm_tiles(h.shape[0])
    n_row_tiles = h.shape[0] // tm
    d_w_main = _matmul_cat_tn(r["dproj"], r["u"], "inproj_dw", tm)
    d_w_small = _matmul(r["dsmall"], r["u"], "tn", F32, "inproj_small_dw", N_SMALL, D_MODEL, tm)
    g32_in = _from_aligned_rows(d_w_main, d_w_small)
    first = max(n_row_tiles // 6, 1)
    du_first, ra_in = _matmul_cat_nn(r["dproj"], w_main, "inproj_dx_swap", tm, rows=(0, first),
                                     exchange=_pair_swap([g32_in]))
    pb_in = _add_pair(ids, g32_in, ra_in)
    du_a, rb_in = _matmul_cat_nn(r["dproj"], w_main, "inproj_dx_exchange", tm,
                                 rows=(first, n_row_tiles - first), fill=du_first,
                                 exchange=_chip_exchange([pb_in]))
    du_b = _matmul(r["dsmall"], w_small, "nn", F32, "inproj_small_dx", tm, D_MODEL, N_SMALL)
    dh, d_norm_pre = _norm1_bwd(du_a, du_b, h, norm_pre, r["dy_out"])
    grad_x = dh[PADF + N_META:].reshape(1, seq, D_MODEL)
    half_in = _add_chips(ids, g32_in, ra_in, rb_in)
    gw_in, gw_pr = _pair_join_halves([half_in, r["half_pr"]])

    tiny = r["d_bias_row"][0]
    part_small = _pack_small(d_norm_pre, r["d_conv_b"], r["d_ssd_norm"], r["d_gate_bias"], r["d_norm_post"],
                             tiny[0:H_SSD], r["d_a"][0, 0:H_SSD] * a_neg, r["d_dsk"].reshape(H_SSD, 64).sum(axis=1),
                             tiny[H_SSD:H_SSD + H_ATT])
    part = jnp.concatenate([_pad_rows(part_small, _SMALL_PAD), _rows128(r["d_conv_w"]),
                            _rows128(dh[PADF:PADF + N_META]), r["loss_blk"]], axis=0)
    tot = _sum8(_allgather8(part, "gather_small_grads"))
    loss = tot[_SMALL_PAD + 96 + 128, 0]
    g_small = tot[0:_SMALL_PAD]
    g_conv_w = lax.dynamic_slice_in_dim(tot[_SMALL_PAD:_SMALL_PAD + 96].reshape(CONV_K, CONV_DIM), chip * 768, 768, axis=1)
    g_meta = lax.dynamic_slice_in_dim(tot[_SMALL_PAD + 96:_SMALL_PAD + 224].reshape(N_META, D_MODEL), chip * 256, 256, axis=1)

    upd = {}
    upd["w_in"] = tuple(jnp.transpose(a) for a in (gw_in,) + _adamw(
        jnp.transpose(w_in[0]), gw_in, jnp.transpose(m_w_in[0]), jnp.transpose(v_w_in[0]), "adamw_w_in"))
    w_pr32 = jnp.concatenate([w_proj_ssd[0], w_proj_att[0], w_out[0]], axis=0)
    m_pr = jnp.concatenate([m_w_proj_ssd[0], m_w_proj_att[0], m_w_out[0]], axis=0)
    v_pr = jnp.concatenate([v_w_proj_ssd[0], v_w_proj_att[0], v_w_out[0]], axis=0)
    pr = (gw_pr,) + _adamw(w_pr32, gw_pr, m_pr, v_pr, "adamw_w_proj")
    upd["w_proj_ssd"] = tuple(a[0:512] for a in pr)
    upd["w_proj_att"] = tuple(a[512:768] for a in pr)
    upd["w_out"] = tuple(a[768:1024] for a in pr)
    upd["conv_w"] = (g_conv_w,) + _adamw(conv_w[0], g_conv_w, m_conv_w[0], v_conv_w[0], "adamw_conv_w")
    upd["meta_tokens"] = (g_meta,) + _adamw(meta_tokens, g_meta, m_meta_tokens, v_meta_tokens, "adamw_meta")
    pk = lambda np_, cb, sn, gb, npo, dtb, al, ds, fg: _pad_rows(_pack_small(np_, cb, sn, gb, npo, dtb, al, ds, fg), _SMALL_PAD)
    w_sm = pk(norm_pre, conv_b, ssd_norm, gate_bias, norm_post, dt_bias, a_log, d_skip, fgate_bias)
    m_sm = pk(m_norm_pre, m_conv_b, m_ssd_norm, m_gate_bias, m_norm_post, m_dt_bias, m_a_log, m_d_skip, m_fgate_bias)
    v_sm = pk(v_norm_pre, v_conv_b, v_ssd_norm, v_gate_bias, v_norm_post, v_dt_bias, v_a_log, v_d_skip, v_fgate_bias)
    sm = [_unpack_small(a) for a in (g_small,) + _adamw(w_sm, g_small, m_sm, v_sm, "adamw_small")]
    for name in ("norm_pre", "conv_b", "dt_bias", "a_log", "d_skip", "ssd_norm", "fgate_bias", "gate_bias", "norm_post"):
        upd[name] = tuple(s[name] for s in sm)
    lead = ("w_in", "conv_w", "w_proj_ssd", "w_proj_att", "w_out")
    order = ("meta_tokens", "norm_pre", "w_in", "conv_w", "conv_b", "dt_bias", "a_log", "d_skip", "ssd_norm",
             "fgate_bias", "gate_bias", "w_proj_ssd", "w_proj_att", "w_out", "norm_post")
    outs = [loss, grad_x]
    for part_i in range(4):
        for name in order:
            a = upd[name][part_i]
            outs.append(a[None] if name in lead else a)
    return tuple(outs)
```

```python
import functools
import math

import jax
import jax.numpy as jnp
from jax import lax
from jax.experimental import pallas as pl
from jax.experimental.pallas import tpu as pltpu

F32 = jnp.float32
BF16 = jnp.bfloat16
HIGHEST = lax.Precision.HIGHEST

D_MODEL = 1024
N_META = 16
CHUNK = 128
PADF = CHUNK - N_META
D_SSD = 2048
H_SSD = 32
G_SSD = 4
N_STATE = 128
CONV_K = 4
CONV_DIM = D_SSD + 2 * G_SSD * N_STATE
H_ATT = 16
D_ATT = 1024
EPS = 1e-6
N_COLS = 11312

C_Z, C_XBC, C_ZA, C_Q, C_K, C_V, C_G = 0, 2048, 5120, 6144, 7168, 8192, 9216
N_MAIN = 11264
N_SMALL = 128
O_Z, O_XBC, O_DT, O_ZA, O_Q, O_K, O_V, O_F, O_G = (
    (0, 2048), (2048, 3072), (5120, 32), (5152, 1024), (6176, 1024), (7200, 1024),
    (8224, 1024), (9248, 16), (9264, 2048))

ADAM_LR, ADAM_B1, ADAM_B2, ADAM_EPS, ADAM_WD, ADAM_STEP = 0.001, 0.9, 0.999, 1e-08, 0.01, 10

VMEM_LIMIT = 56 * 1024 * 1024


def _cp(*sem):
    return pltpu.CompilerParams(dimension_semantics=sem, vmem_limit_bytes=VMEM_LIMIT)


def _tile(n, prefs):
    for p in prefs:
        if n % p == 0:
            return p
    raise ValueError(f"no tile for {n} in {prefs}")


def _iota(shape, dim):
    return lax.broadcasted_iota(jnp.int32, shape, dim)


def _sigmoid(x):
    return 1.0 / (1.0 + jnp.exp(-x))


def _softplus_tail(x):
    return jnp.log(1.0 + jnp.exp(-jnp.abs(x)))


_NN = (((1,), (0,)), ((), ()))
_NT = (((1,), (1,)), ((), ()))
_TN = (((0,), (0,)), ((), ()))


def _dot(a, b, dims=_NN):
    return lax.dot_general(a, b, dims, preferred_element_type=F32)


def _dot_exact(a, b, dims=_NN):
    return lax.dot_general(a, b, dims, precision=HIGHEST, preferred_element_type=F32)


def _hosted_call(body, *, name, grid, in_specs, out_specs, out_shape, scratch_shapes, operands, semantics,
                 exchange=None, aliases=None):
    aliases = dict(aliases or {})
    if exchange is None:
        return pl.pallas_call(body, name=name, out_shape=out_shape, grid=grid, in_specs=in_specs,
                              out_specs=out_specs, scratch_shapes=scratch_shapes, input_output_aliases=aliases,
                              compiler_params=_cp(*semantics))(*operands)
    arrays, shapes, copies, n_sems, in_place = exchange
    n_in, n_out, n_ex = len(operands), len(out_shape), len(arrays)

    def hosted(*refs):
        ex_in = refs[n_in:n_in + n_ex]
        ex_out = refs[n_in + n_ex + n_out:n_in + n_ex + n_out + n_ex]
        own = refs[:n_in] + refs[n_in + n_ex:n_in + n_ex + n_out] + refs[n_in + 2 * n_ex + n_out:-2]
        first = functools.reduce(lambda p, q: p & q, [pl.program_id(d) == 0 for d in range(len(grid))])
        last = functools.reduce(lambda p, q: p & q, [pl.program_id(d) == grid[d] - 1 for d in range(len(grid))])

        def descriptors():
            return copies(ex_out if in_place else ex_in, ex_out, refs[-2], refs[-1])

        @pl.when(first)
        def _():
            for cp in descriptors():
                cp.start()

        body(*own)

        @pl.when(last)
        def _():
            for cp in descriptors():
                cp.wait()

    return pl.pallas_call(
        hosted, name=name,
        out_shape=tuple(out_shape) + tuple(shapes),
        grid=grid,
        in_specs=list(in_specs) + [_ANY] * n_ex,
        out_specs=tuple(out_specs) + (_ANY,) * n_ex,
        input_output_aliases={**aliases, **({n_in + e: n_out + e for e in range(n_ex)} if in_place else {})},
        scratch_shapes=list(scratch_shapes) + [pltpu.SemaphoreType.DMA((n_sems,)), pltpu.SemaphoreType.DMA((n_sems,))],
        compiler_params=_cp(*(("arbitrary",) * len(grid))),
    )(*operands, *arrays)


def _matmul(a, b, mode, out_dtype, name, tm, tn, tk, exchange=None):
    if mode == "tn":
        kdim, m = a.shape
    else:
        m, kdim = a.shape
    n = b.shape[0] if mode == "nt" else b.shape[1]
    nk = kdim // tk
    dims = {"nn": _NN, "nt": _NT, "tn": _TN}[mode]
    a_spec = (pl.BlockSpec((tk, tm), lambda i, j, k: (k, i)) if mode == "tn"
              else pl.BlockSpec((tm, tk), lambda i, j, k: (i, k)))
    b_spec = (pl.BlockSpec((tn, tk), lambda i, j, k: (j, k)) if mode == "nt"
              else pl.BlockSpec((tk, tn), lambda i, j, k: (k, j)))

    def body(a_ref, b_ref, o_ref, acc_ref):
        k = pl.program_id(2)
        p = _dot(a_ref[...].astype(BF16), b_ref[...].astype(BF16), dims)
        if nk == 1:
            o_ref[...] = p.astype(out_dtype)
        else:
            @pl.when(k == 0)
            def _():
                acc_ref[...] = p

            @pl.when(k > 0)
            def _():
                acc_ref[...] += p

            @pl.when(k == nk - 1)
            def _():
                o_ref[...] = acc_ref[...].astype(out_dtype)

    out = _hosted_call(
        body, name=name,
        out_shape=(jax.ShapeDtypeStruct((m, n), out_dtype),),
        grid=(m // tm, n // tn, nk),
        in_specs=[a_spec, b_spec],
        out_specs=(pl.BlockSpec((tm, tn), lambda i, j, k: (i, j)),),
        scratch_shapes=[pltpu.VMEM((tm, tn), F32)],
        operands=(a, b), semantics=("parallel", "parallel", "arbitrary"), exchange=exchange)
    return out[0] if exchange is None else out


_CAT_BLK = 1024


def _piece_ranges(pieces):
    out, off = [], 0
    for p in pieces:
        nb = p.shape[1] // _CAT_BLK
        out.append((off, nb))
        off += nb
    return out, off


def _matmul_cat_nn(pieces, b, name, tm, rows=None, fill=None, exchange=None):
    t = pieces[0].shape[0]
    n = b.shape[1]
    ranges, nk = _piece_ranges(pieces)
    first, ni = rows if rows is not None else (0, t // tm)
    n_in = len(pieces) + 1 + (fill is not None)

    def body(*refs):
        a_refs, b_ref, o_ref, acc_ref = refs[:len(pieces)], refs[len(pieces)], refs[n_in], refs[n_in + 1]
        k = pl.program_id(1)

        @pl.when(k == 0)
        def _():
            acc_ref[...] = jnp.zeros_like(acc_ref)

        for a_ref, (off, nb) in zip(a_refs, ranges):
            @pl.when((k >= off) & (k < off + nb))
            def _(a_ref=a_ref):
                acc_ref[...] += _dot(a_ref[...], b_ref[...])

        @pl.when(k == nk - 1)
        def _():
            o_ref[...] = acc_ref[...]

    def a_spec(off, nb):
        return pl.BlockSpec((tm, _CAT_BLK), lambda i, k: (first + i, jnp.clip(k - off, 0, nb - 1)))

    in_specs = [a_spec(off, nb) for off, nb in ranges] + [pl.BlockSpec((_CAT_BLK, n), lambda i, k: (k, 0))]
    operands = list(pieces) + [b]
    if fill is not None:
        in_specs.append(_ANY)
        operands.append(fill)
    out = _hosted_call(
        body, name=name,
        out_shape=(jax.ShapeDtypeStruct((t, n), F32),),
        grid=(ni, nk),
        in_specs=in_specs,
        out_specs=(pl.BlockSpec((tm, n), lambda i, k: (first + i, 0)),),
        scratch_shapes=[pltpu.VMEM((tm, n), F32)],
        operands=operands, semantics=("parallel", "arbitrary"), exchange=exchange,
        aliases={len(pieces) + 1: 0} if fill is not None else None)
    return out if exchange is not None else out[0]


def _matmul_cat_tn(pieces, b, name, tk):
    t = pieces[0].shape[0]
    n = b.shape[1]
    ranges, nm = _piece_ranges(pieces)
    nk = t // tk

    def body(*refs):
        a_refs, b_ref, o_ref, acc_ref = refs[:len(pieces)], refs[-3], refs[-2], refs[-1]
        m = pl.program_id(0)
        k = pl.program_id(1)

        @pl.when(k == 0)
        def _():
            acc_ref[...] = jnp.zeros_like(acc_ref)

        for a_ref, (off, nb) in zip(a_refs, ranges):
            @pl.when((m >= off) & (m < off + nb))
            def _(a_ref=a_ref):
                acc_ref[...] += _dot(a_ref[...], b_ref[...], _TN)

        @pl.when(k == nk - 1)
        def _():
            o_ref[...] = acc_ref[...]

    def a_spec(off, nb):
        def index(m, k):
            mine = (m >= off) & (m < off + nb)
            return jnp.where(mine, k, 0), jnp.clip(m - off, 0, nb - 1)
        return pl.BlockSpec((tk, _CAT_BLK), index)

    return pl.pallas_call(
        body, name=name,
        out_shape=jax.ShapeDtypeStruct((nm * _CAT_BLK, n), F32),
        grid=(nm, nk),
        in_specs=[a_spec(off, nb) for off, nb in ranges] + [pl.BlockSpec((tk, n), lambda m, k: (k, 0))],
        out_specs=pl.BlockSpec((_CAT_BLK, n), lambda m, k: (m, 0)),
        scratch_shapes=[pltpu.VMEM((_CAT_BLK, n), F32)],
        compiler_params=_cp("parallel", "arbitrary"),
    )(*pieces, b)


def _row_tile(t):
    return _tile(t, (528, 128))


def _row_tile_wide(t):
    return _tile(t, (176, 128))


def _norm1_fwd(h, g):
    t = h.shape[0]
    tm = _row_tile(t)

    def body(h_ref, g_ref, u_ref):
        x = h_ref[...]
        r = lax.rsqrt(jnp.mean(x * x, axis=-1, keepdims=True) + EPS)
        u_ref[...] = (x * r * g_ref[...]).astype(BF16)

    return pl.pallas_call(
        body, name="norm1_fwd",
        out_shape=jax.ShapeDtypeStruct((t, D_MODEL), BF16),
        grid=(t // tm,),
        in_specs=[pl.BlockSpec((tm, D_MODEL), lambda i: (i, 0)),
                  pl.BlockSpec((1, D_MODEL), lambda i: (0, 0))],
        out_specs=pl.BlockSpec((tm, D_MODEL), lambda i: (i, 0)),
        compiler_params=_cp("parallel"),
    )(h, g)


def _norm1_bwd(du_a, du_b, h, g, dy):
    t = h.shape[0]
    tm = _row_tile(t)

    def body(a_ref, b_ref, h_ref, g_ref, dy_ref, dh_ref, dg_ref):
        i = pl.program_id(0)
        x = h_ref[...]
        du = a_ref[...] + b_ref[...]
        r = lax.rsqrt(jnp.mean(x * x, axis=-1, keepdims=True) + EPS)
        gdu = du * g_ref[...]
        dh_ref[...] = dy_ref[...] + r * (gdu - x * (r * r) * jnp.mean(gdu * x, axis=-1, keepdims=True))
        part = jnp.sum(du * x * r, axis=0, keepdims=True)

        @pl.when(i == 0)
        def _():
            dg_ref[...] = part

        @pl.when(i > 0)
        def _():
            dg_ref[...] += part

    row = pl.BlockSpec((tm, D_MODEL), lambda i: (i, 0))
    vec = pl.BlockSpec((1, D_MODEL), lambda i: (0, 0))
    return pl.pallas_call(
        body, name="norm1_bwd",
        out_shape=(jax.ShapeDtypeStruct((t, D_MODEL), F32), jax.ShapeDtypeStruct((1, D_MODEL), F32)),
        grid=(t // tm,),
        in_specs=[row, row, row, vec, row],
        out_specs=(row, vec),
        compiler_params=_cp("arbitrary"),
    )(du_a, du_b, h, g, dy)


def _small_fwd(small, bias_row):
    t = small.shape[0]
    rt = _tile(t, (384, 128))

    def body(s_ref, b_ref, o_ref, carry_ref):
        c = pl.program_id(0)

        @pl.when(c == 0)
        def _():
            carry_ref[...] = jnp.zeros_like(carry_ref)

        x = s_ref[...] + b_ref[...]
        lane = _iota((rt, N_SMALL), 1)
        valid = (c * rt + _iota((rt, N_SMALL), 0)) >= PADF
        tail = _softplus_tail(x)
        dt = jnp.where(valid & (lane < H_SSD), jnp.maximum(x, 0.0) + tail, 0.0)
        lf = jnp.where(valid & (lane >= H_SSD) & (lane < H_SSD + H_ATT), jnp.minimum(x, 0.0) - tail, 0.0)
        tri = (_iota((rt, rt), 0) >= _iota((rt, rt), 1)).astype(F32)
        cs = _dot_exact(tri, lf) + carry_ref[...]
        carry_ref[...] = cs[rt - 1:rt, :]
        o_ref[...] = dt + cs

    return pl.pallas_call(
        body, name="small_fwd",
        out_shape=jax.ShapeDtypeStruct((t, N_SMALL), F32),
        grid=(t // rt,),
        in_specs=[pl.BlockSpec((rt, N_SMALL), lambda c: (c, 0)),
                  pl.BlockSpec((1, N_SMALL), lambda c: (0, 0))],
        out_specs=pl.BlockSpec((rt, N_SMALL), lambda c: (c, 0)),
        scratch_shapes=[pltpu.VMEM((1, N_SMALL), F32)],
        compiler_params=_cp("arbitrary"),
    )(small, bias_row)


def _small_bwd(dsm, small, bias_row):
    t = small.shape[0]
    rt = _tile(t, (384, 128))
    nc = t // rt

    def body(d_ref, s_ref, b_ref, o_ref, db_ref, carry_ref):
        step = pl.program_id(0)
        c = nc - 1 - step

        @pl.when(step == 0)
        def _():
            carry_ref[...] = jnp.zeros_like(carry_ref)
            db_ref[...] = jnp.zeros_like(db_ref)

        x = s_ref[...] + b_ref[...]
        d = d_ref[...]
        lane = _iota((rt, N_SMALL), 1)
        valid = (c * rt + _iota((rt, N_SMALL), 0)) >= PADF
        is_dt = lane < H_SSD
        is_f = (lane >= H_SSD) & (lane < H_SSD + H_ATT)
        triu = (_iota((rt, rt), 1) >= _iota((rt, rt), 0)).astype(F32)
        dc = jnp.where(is_f, d, 0.0)
        dlf = _dot_exact(triu, dc) + carry_ref[...]
        carry_ref[...] = dlf[0:1, :]
        sg = _sigmoid(x)
        out = jnp.where(valid & is_dt, d * sg, 0.0) + jnp.where(valid & is_f, dlf * (1.0 - sg), 0.0)
        o_ref[...] = out.astype(BF16)
        db_ref[...] += jnp.sum(out, axis=0, keepdims=True)

    blk = pl.BlockSpec((rt, N_SMALL), lambda s: (nc - 1 - s, 0))
    vec = pl.BlockSpec((1, N_SMALL), lambda s: (0, 0))
    return pl.pallas_call(
        body, name="small_bwd",
        out_shape=(jax.ShapeDtypeStruct((t, N_SMALL), BF16), jax.ShapeDtypeStruct((1, N_SMALL), F32)),
        grid=(nc,),
        in_specs=[blk, blk, vec],
        out_specs=(blk, vec),
        scratch_shapes=[pltpu.VMEM((1, N_SMALL), F32)],
        compiler_params=_cp("arbitrary"),
    )(dsm, small, bias_row)


_CONV_TC = 1024
_XBC_BLK = C_XBC // _CONV_TC


def _shift_down(cur, prev8, j):
    rc = pltpu.roll(cur, j, 0)
    rid = _iota(prev8.shape, 0)
    top = jnp.where(rid < j, pltpu.roll(prev8, j, 0), rc[0:8, :])
    return top if cur.shape[0] == 8 else jnp.concatenate([top, rc[8:, :]], axis=0)


def _shift_up(cur, next8, j):
    n = cur.shape[0]
    ru = pltpu.roll(cur, n - j, 0)
    rid = _iota(next8.shape, 0)
    bot = jnp.where(rid >= 8 - j, pltpu.roll(next8, 8 - j, 0), ru[n - 8:, :])
    return jnp.concatenate([ru[:n - 8, :], bot], axis=0)


def _conv_taps(cur, prev, w, b):
    taps = [cur] + [_shift_down(cur, prev, j) for j in (1, 2, 3)]
    acc = b + taps[0] * w[3:4, :]
    for j in (1, 2, 3):
        acc = acc + taps[j] * w[3 - j:4 - j, :]
    return acc, taps


def _conv_pre(x_ref, p_ref, w_ref, b_ref, i):
    return _conv_taps(x_ref[...], jnp.where(i > 0, p_ref[...], 0.0), w_ref[...], b_ref[...])


def _dsilu(d, acc):
    sg = _sigmoid(acc)
    return d * sg * (1.0 + acc * (1.0 - sg))


def _conv_fwd(proj, conv_w, conv_b):
    t = proj.shape[0]
    tr = _row_tile(t)

    def body(x_ref, p_ref, w_ref, b_ref, o_ref):
        i = pl.program_id(0)
        acc, _ = _conv_pre(x_ref, p_ref, w_ref, b_ref, i)
        valid = (i * tr + _iota(acc.shape, 0)) >= PADF
        o_ref[...] = jnp.where(valid, acc * _sigmoid(acc), 0.0)

    return pl.pallas_call(
        body, name="conv_fwd",
        out_shape=jax.ShapeDtypeStruct((t, CONV_DIM), F32),
        grid=(t // tr, CONV_DIM // _CONV_TC),
        in_specs=[pl.BlockSpec((tr, _CONV_TC), lambda i, j: (i, _XBC_BLK + j)),
                  pl.BlockSpec((8, _CONV_TC), lambda i, j: (jnp.maximum(i * (tr // 8) - 1, 0), _XBC_BLK + j)),
                  pl.BlockSpec((CONV_K, _CONV_TC), lambda i, j: (0, j)),
                  pl.BlockSpec((1, _CONV_TC), lambda i, j: (0, j))],
        out_specs=pl.BlockSpec((tr, _CONV_TC), lambda i, j: (i, j)),
        compiler_params=_cp("parallel", "parallel"),
    )(proj, proj, conv_w, conv_b)


def _conv_bwd(dxbc, proj, conv_w, conv_b):
    t = proj.shape[0]
    tr = _row_tile(t)
    n_tiles = t // tr
    last8 = t // 8 - 1

    def body(d_ref, dn_ref, x_ref, p_ref, xn_ref, w_ref, b_ref, dx_ref, dw_ref, db_ref):
        i = pl.program_id(1)
        w = w_ref[...]
        b = b_ref[...]
        cur = x_ref[...]
        acc, taps = _conv_taps(cur, jnp.where(i > 0, p_ref[...], 0.0), w, b)
        valid = (i * tr + _iota(acc.shape, 0)) >= PADF
        da = jnp.where(valid, _dsilu(d_ref[...], acc), 0.0)
        acc_n, _ = _conv_taps(xn_ref[...], cur[tr - 8:, :], w, b)
        da_n = jnp.where(i < n_tiles - 1, _dsilu(dn_ref[...], acc_n), 0.0)
        dx = da * w[3:4, :]
        for j in (1, 2, 3):
            dx = dx + _shift_up(da, da_n, j) * w[3 - j:4 - j, :]
        dx_ref[...] = dx.astype(BF16)
        dw = jnp.concatenate([jnp.sum(da * taps[3 - k], axis=0, keepdims=True) for k in range(CONV_K)], axis=0)
        db = jnp.sum(da, axis=0, keepdims=True)

        @pl.when(i == 0)
        def _():
            dw_ref[...] = dw
            db_ref[...] = db

        @pl.when(i > 0)
        def _():
            dw_ref[...] += dw
            db_ref[...] += db

    nxt8 = lambda i: jnp.minimum((i + 1) * (tr // 8), last8)
    return pl.pallas_call(
        body, name="conv_bwd",
        out_shape=(jax.ShapeDtypeStruct((t, CONV_DIM), BF16),
                   jax.ShapeDtypeStruct((CONV_K, CONV_DIM), F32),
                   jax.ShapeDtypeStruct((1, CONV_DIM), F32)),
        grid=(CONV_DIM // _CONV_TC, n_tiles),
        in_specs=[pl.BlockSpec((tr, _CONV_TC), lambda j, i: (i, j)),
                  pl.BlockSpec((8, _CONV_TC), lambda j, i: (nxt8(i), j)),
                  pl.BlockSpec((tr, _CONV_TC), lambda j, i: (i, _XBC_BLK + j)),
                  pl.BlockSpec((8, _CONV_TC), lambda j, i: (jnp.maximum(i * (tr // 8) - 1, 0), _XBC_BLK + j)),
                  pl.BlockSpec((8, _CONV_TC), lambda j, i: (nxt8(i), _XBC_BLK + j)),
                  pl.BlockSpec((CONV_K, _CONV_TC), lambda j, i: (0, j)),
                  pl.BlockSpec((1, _CONV_TC), lambda j, i: (0, j))],
        out_specs=(pl.BlockSpec((tr, _CONV_TC), lambda j, i: (i, j)),
                   pl.BlockSpec((CONV_K, _CONV_TC), lambda j, i: (0, j)),
                   pl.BlockSpec((1, _CONV_TC), lambda j, i: (0, j))),
        compiler_params=_cp("parallel", "arbitrary"),
    )(dxbc, dxbc, proj, proj, proj, conv_w, conv_b)


_GW = D_SSD // G_SSD


def _ssd_prelude(dt_ref, a_ref, e_scr, es_scr, dte_scr):
    r0 = _iota((CHUNK, CHUNK), 0)
    r1 = _iota((CHUNK, CHUNK), 1)
    dt = jnp.where(r1 < H_SSD, dt_ref[...], 0.0)
    adt = dt * a_ref[...]
    acs = _dot_exact((r0 >= r1).astype(F32), adt)
    acs_t = acs.T
    alast = acs[CHUNK - 1:CHUNK, :]
    exp_a = jnp.exp(acs)
    dec_s = jnp.exp(alast - acs)
    lo = r1 < 64
    for j in range(H_SSD // 2):
        sl = slice(CHUNK * j, CHUNK * (j + 1))
        e_scr[:, sl] = jnp.where(lo, exp_a[:, 2 * j:2 * j + 1], exp_a[:, 2 * j + 1:2 * j + 2])
        es_scr[:, sl] = jnp.where(lo, dec_s[:, 2 * j:2 * j + 1], dec_s[:, 2 * j + 1:2 * j + 2])
        dte_scr[:, sl] = jnp.where(lo, dt[:, 2 * j:2 * j + 1], dt[:, 2 * j + 1:2 * j + 2])
    return dt, acs, acs_t, r0, r1, lo


def _chunk_decay_rows(acs_t, g):
    cd_t = jnp.exp(acs_t[:, CHUNK - 1:CHUNK])
    return jnp.concatenate(
        [jnp.broadcast_to(cd_t[8 * g + hh:8 * g + hh + 1, :], (64, N_STATE)) for hh in range(8)], axis=0)


def _ssd_fwd(xbc, dtlf, a_row, dsk_row):
    t = xbc.shape[0]
    nc = t // CHUNK

    def body(xs_ref, b_ref, c_ref, dt_ref, a_ref, dsk_ref, y_ref, hin_ref, h_scr, e_scr, es_scr, dte_scr):
        c = pl.program_id(0)

        @pl.when(c == 0)
        def _():
            h_scr[...] = jnp.zeros_like(h_scr)

        dt, acs, acs_t, r0, r1, lo = _ssd_prelude(dt_ref, a_ref, e_scr, es_scr, dte_scr)
        causal = r0 >= r1
        for g in range(G_SSD):
            gs = slice(_GW * g, _GW * (g + 1))
            bg = b_ref[:, N_STATE * g:N_STATE * (g + 1)].astype(BF16)
            cg = c_ref[:, N_STATE * g:N_STATE * (g + 1)].astype(BF16)
            cb = _dot(cg, bg, _NT)
            hg = h_scr[gs, :]
            hin_ref[0, gs, :] = hg
            xg = xs_ref[:, gs] * dte_scr[:, gs]
            yoff = _dot(cg, hg.astype(BF16), _NT) * e_scr[:, gs]
            st = _dot((xg * es_scr[:, gs]).astype(BF16), bg, _TN)
            h_scr[gs, :] = hg * _chunk_decay_rows(acs_t, g) + st
            for jj in range(4):
                j = 4 * g + jj
                sl = slice(CHUNK * j, CHUNK * (j + 1))
                xp = xg[:, CHUNK * jj:CHUNK * (jj + 1)]
                acc = yoff[:, CHUNK * jj:CHUNK * (jj + 1)] + dsk_ref[:, sl] * xs_ref[:, sl]
                for hh in range(2):
                    h = 2 * j + hh
                    seg = acs[:, h:h + 1] - acs_t[h:h + 1, :]
                    lm = jnp.exp(jnp.where(causal, seg, -1e30))
                    m = (cb * lm).astype(BF16)
                    xh = jnp.where(lo if hh == 0 else ~lo, xp, 0.0).astype(BF16)
                    acc = acc + _dot(m, xh)
                y_ref[:, sl] = acc

    return pl.pallas_call(
        body, name="ssd_fwd",
        out_shape=(jax.ShapeDtypeStruct((t, D_SSD), F32), jax.ShapeDtypeStruct((nc, D_SSD, N_STATE), F32)),
        grid=(nc,),
        in_specs=[pl.BlockSpec((CHUNK, D_SSD), lambda c: (c, 0)),
                  pl.BlockSpec((CHUNK, _GW), lambda c: (c, 4)),
                  pl.BlockSpec((CHUNK, _GW), lambda c: (c, 5)),
                  pl.BlockSpec((CHUNK, N_SMALL), lambda c: (c, 0)),
                  pl.BlockSpec((1, N_SMALL), lambda c: (0, 0)),
                  pl.BlockSpec((1, D_SSD), lambda c: (0, 0))],
        out_specs=(pl.BlockSpec((CHUNK, D_SSD), lambda c: (c, 0)),
                   pl.BlockSpec((1, D_SSD, N_STATE), lambda c: (c, 0, 0))),
        scratch_shapes=[pltpu.VMEM((D_SSD, N_STATE), F32)] + [pltpu.VMEM((CHUNK, D_SSD), F32)] * 3,
        compiler_params=_cp("arbitrary"),
    )(xbc, xbc, xbc, dtlf, a_row, dsk_row)


def _ssd_bwd(xbc, dtlf, a_row, dsk_row, hin, dy):
    t = xbc.shape[0]
    nc = t // CHUNK

    def body(xs_ref, b_ref, c_ref, dt_ref, a_ref, dsk_ref, hin_ref, dy_ref,
             dxbc_ref, ddt_ref, da_ref, ddsk_ref, dh_scr, e_scr, es_scr, dte_scr, dx_scr, whi_scr, wlo_scr):
        step = pl.program_id(0)

        @pl.when(step == 0)
        def _():
            dh_scr[...] = jnp.zeros_like(dh_scr)
            da_ref[...] = jnp.zeros_like(da_ref)
            ddsk_ref[...] = jnp.zeros_like(ddsk_ref)

        dt, acs, acs_t, r0, r1, lo = _ssd_prelude(dt_ref, a_ref, e_scr, es_scr, dte_scr)
        causal = r0 >= r1
        lane_row = _iota((1, CHUNK), 1)
        dacs = jnp.zeros((CHUNK, CHUNK), F32)
        dacs_t = jnp.zeros((CHUNK, CHUNK), F32)
        dalast = jnp.zeros((1, CHUNK), F32)
        ddt_dir = jnp.zeros((CHUNK, CHUNK), F32)
        ddsk_ref[...] += jnp.sum(dy_ref[...] * xs_ref[...], axis=0, keepdims=True)

        def head_sums(z, pick):
            hi = z.astype(BF16)
            return _dot(hi, pick) + _dot((z - hi.astype(F32)).astype(BF16), pick)

        for g in range(G_SSD):
            gs = slice(_GW * g, _GW * (g + 1))
            pick = (jnp.right_shift(_iota((_GW, CHUNK), 0), 6) + 8 * g == _iota((_GW, CHUNK), 1)).astype(BF16)
            bg = b_ref[:, N_STATE * g:N_STATE * (g + 1)].astype(BF16)
            cg = c_ref[:, N_STATE * g:N_STATE * (g + 1)].astype(BF16)
            cb = _dot(cg, bg, _NT)
            hg = hin_ref[0, gs, :]
            hgb = hg.astype(BF16)
            dhn = dh_scr[gs, :]
            dhnb = dhn.astype(BF16)
            esg = es_scr[:, gs]
            dyg = dy_ref[:, gs]
            xsg = xs_ref[:, gs]
            xg = xsg * dte_scr[:, gs]
            dyeb = (dyg * e_scr[:, gs]).astype(BF16)
            dc = _dot(dyeb, hgb)
            dh_y = _dot(dyeb, cg, _TN)
            dxs = _dot(bg, dhnb, _NT) * esg
            db = _dot((xg * esg).astype(BF16), dhnb)
            cd = _chunk_decay_rows(acs_t, g)
            dh_scr[gs, :] = dhn * cd + dh_y
            end_state = head_sums(jnp.broadcast_to(jnp.sum(xg * dxs, axis=0, keepdims=True), (8, _GW)), pick)[0:1, :]
            carried = dhn * hg * cd
            per_head = jnp.concatenate([jnp.sum(carried[64 * hh:64 * hh + 64, :], axis=0, keepdims=True)
                                        for hh in range(8)], axis=0)
            per_head = jnp.sum(per_head, axis=1, keepdims=True)
            for hh in range(8):
                end_state = end_state + jnp.where(lane_row == 8 * g + hh, per_head[hh:hh + 1, :], 0.0)
            dalast = dalast + end_state
            dcb = jnp.zeros((CHUNK, CHUNK), F32)
            for jj in range(4):
                j = 4 * g + jj
                sl = slice(CHUNK * j, CHUNK * (j + 1))
                ps = slice(CHUNK * jj, CHUNK * (jj + 1))
                xpb = xg[:, ps].astype(BF16)
                dyp = dyg[:, ps]
                dxp = dxs[:, ps]
                for hh in range(2):
                    h = 2 * j + hh
                    ws = slice(CHUNK * (2 * jj + hh), CHUNK * (2 * jj + hh + 1))
                    seg = acs[:, h:h + 1] - acs_t[h:h + 1, :]
                    lm = jnp.exp(jnp.where(causal, seg, -1e30))
                    mf = cb * lm
                    dyh = jnp.where(lo if hh == 0 else ~lo, dyp, 0.0).astype(BF16)
                    gm = _dot(dyh, xpb, _NT)
                    dcb = dcb + gm * lm
                    w = gm * mf
                    whi = w.astype(BF16)
                    whi_scr[:, ws] = whi
                    wlo_scr[:, ws] = (w - whi.astype(F32)).astype(BF16)
                    dacs_t = dacs_t - jnp.where(r0 == h, jnp.sum(w, axis=0, keepdims=True), 0.0)
                    dxp = dxp + _dot(mf.astype(BF16), dyh, _TN)
                dx_scr[:, sl] = dxp
            dxg = dx_scr[:, gs]
            pick_w = (jnp.right_shift(_iota((8 * CHUNK, CHUNK), 0), 7) + 8 * g == _iota((8 * CHUNK, CHUNK), 1)).astype(BF16)
            ch = _dot(cg, hgb, _NT)
            dacs = (dacs + _dot(whi_scr[...], pick_w) + _dot(wlo_scr[...], pick_w)
                    + head_sums(dyg * e_scr[:, gs] * ch - xg * dxs, pick))
            ddt_dir = ddt_dir + head_sums(dxg * xsg, pick)
            dcbb = dcb.astype(BF16)
            dxbc_ref[:, D_SSD + N_STATE * g:D_SSD + N_STATE * (g + 1)] = db + _dot(dcbb, cg, _TN)
            dxbc_ref[:, D_SSD + _GW + N_STATE * g:D_SSD + _GW + N_STATE * (g + 1)] = dc + _dot(dcbb, bg)
        dxbc_ref[:, 0:D_SSD] = dx_scr[...] * dte_scr[...] + dsk_ref[...] * dy_ref[...]
        dacs = dacs + dacs_t.T + jnp.where(r0 == CHUNK - 1, dalast, 0.0)
        dadt = _dot_exact((r1 >= r0).astype(F32), dacs)
        ddt_ref[...] = dadt * a_ref[...] + ddt_dir
        da_ref[...] += jnp.sum(dadt * dt, axis=0, keepdims=True)

    rev = lambda s: (nc - 1 - s, 0)
    return pl.pallas_call(
        body, name="ssd_bwd",
        out_shape=(jax.ShapeDtypeStruct((t, CONV_DIM), F32), jax.ShapeDtypeStruct((t, N_SMALL), F32),
                   jax.ShapeDtypeStruct((1, N_SMALL), F32), jax.ShapeDtypeStruct((1, D_SSD), F32)),
        grid=(nc,),
        in_specs=[pl.BlockSpec((CHUNK, D_SSD), rev),
                  pl.BlockSpec((CHUNK, _GW), lambda s: (nc - 1 - s, 4)),
                  pl.BlockSpec((CHUNK, _GW), lambda s: (nc - 1 - s, 5)),
                  pl.BlockSpec((CHUNK, N_SMALL), rev),
                  pl.BlockSpec((1, N_SMALL), lambda s: (0, 0)),
                  pl.BlockSpec((1, D_SSD), lambda s: (0, 0)),
                  pl.BlockSpec((1, D_SSD, N_STATE), lambda s: (nc - 1 - s, 0, 0)),
                  pl.BlockSpec((CHUNK, D_SSD), rev)],
        out_specs=(pl.BlockSpec((CHUNK, CONV_DIM), rev),
                   pl.BlockSpec((CHUNK, N_SMALL), rev),
                   pl.BlockSpec((1, N_SMALL), lambda s: (0, 0)),
                   pl.BlockSpec((1, D_SSD), lambda s: (0, 0))),
        scratch_shapes=([pltpu.VMEM((D_SSD, N_STATE), F32)] + [pltpu.VMEM((CHUNK, D_SSD), F32)] * 4
                        + [pltpu.VMEM((CHUNK, 8 * CHUNK), BF16)] * 2),
        compiler_params=_cp("arbitrary"),
    )(xbc, xbc, xbc, dtlf, a_row, dsk_row, hin, dy)


_NPAIR = H_ATT // 2
_QB, _KB, _VB = C_Q // 128, C_K // 128, C_V // 128
_SCALE = 1.0 / math.sqrt(64.0)
_LOG2E = math.log2(math.e)


def _attn_blocks(t):
    return _tile(t, (1408, 384, 256, 128)), _tile(t, (384, 128))


def _split3(c):
    hi = c.astype(BF16).astype(F32)
    rest = c - hi
    mid = rest.astype(BF16).astype(F32)
    return hi, mid, rest - mid


def _head_lanes(lane, hh):
    return (lane < 64, 64) if hh == 0 else (lane >= 64, 0)


def _q_operand(q, cq, lane, hh):
    sel, first = _head_lanes(lane, hh)
    out = jnp.where(sel, q, 0.0)
    for n, col in enumerate(_split3(cq) + (1.0, 1.0, 1.0)):
        out = jnp.where(lane == first + n, col, out)
    return out.astype(BF16)


def _k_operand(k, ck, lane, hh):
    sel, first = _head_lanes(lane, hh)
    hi, mid, lo = _split3(ck)
    out = jnp.where(sel, k, 0.0)
    for n, col in enumerate((1.0, 1.0, 1.0, -hi, -mid, -lo)):
        out = jnp.where(lane == first + n, col, out)
    return out.astype(BF16)


def _sum_operand(x, lane, hh, at):
    sel, first = _head_lanes(lane, hh)
    return jnp.where(sel, x, jnp.where(lane == first + at, 1.0, 0.0)).astype(BF16)


_C_FILLER = 2.0 ** 30
_SKIP_STEP = 256


def _query_skips(bq):
    firsts = list(range(0, bq, _SKIP_STEP))
    far = 1 << 30
    return [(q0 if n else -far, firsts[n + 1] if n + 1 < len(firsts) else far, q0) for n, q0 in enumerate(firsts)]


def _attn_fwd(proj, c_col):
    t = proj.shape[0]
    bq, bk = _attn_blocks(t)
    nq, nk = t // bq, t // bk
    rs = 32

    def last_kv(i):
        return (i * bq + bq - 1) // bk

    def body(q_ref, k_ref, v_ref, cq_ref, ck_ref, o_ref, lse_ref, p_ref, mrun_ref, qs_scr, s_scr, m_scr, acc_scr):
        i = pl.program_id(1)
        kk = pl.program_id(2)
        lane_q = _iota((bq, 128), 1)

        @pl.when(kk == 0)
        def _():
            m_scr[...] = jnp.full_like(m_scr, -1e30)
            acc_scr[...] = jnp.zeros_like(acc_scr)
            q = q_ref[...] * (_SCALE * _LOG2E)
            cq = cq_ref[0] * _LOG2E
            for hh in range(2):
                qs_scr[hh] = _q_operand(q, cq[:, hh:hh + 1], lane_q, hh)

        def step(masked, q0):
            nqc = bq - q0
            lane_k = _iota((bk, 128), 1)
            k = k_ref[...]
            v = v_ref[...]
            ck = ck_ref[0] * _LOG2E
            ahead = _iota((rs, nqc), 0) - _iota((rs, nqc), 1) - q0
            vss = []
            for hh in range(2):
                sel, first = _head_lanes(lane_k, hh)
                ks = _k_operand(k, ck[:, hh:hh + 1], lane_k, hh)
                vss.append(jnp.where(sel, v, jnp.where(lane_k == first, 1.0, 0.0)).astype(BF16))
                s_scr[hh, :, q0:] = _dot(ks, qs_scr[hh, q0:, :], _NT)
            for hh in range(2):
                vs = vss[hh]

                def block_max(r, mx):
                    rows = pl.ds(pl.multiple_of(r * rs, rs), rs)
                    s = s_scr[hh, rows, q0:]
                    if masked:
                        s = jnp.where(ahead <= i * bq - kk * bk - r * rs, s, -1e30)
                        s_scr[hh, rows, q0:] = s
                    return jnp.maximum(mx, s)

                mx = lax.fori_loop(0, bk // rs, block_max, jnp.full((rs, nqc), -1e30, F32), unroll=True)
                m_old = m_scr[hh, :, q0:]
                m_new = jnp.maximum(m_old, jnp.max(mx, axis=0, keepdims=True))
                m_scr[hh, :, q0:] = m_new
                mrun_ref[0, hh:hh + 1, q0:] = m_new

                def probs(r, carry):
                    rows = pl.ds(pl.multiple_of(r * rs, rs), rs)
                    p_ref[0, hh, rows, q0:] = jnp.exp2(s_scr[hh, rows, q0:] - m_new).astype(BF16)
                    return carry

                lax.fori_loop(0, bk // rs, probs, 0, unroll=True)
                acc_scr[hh, :, q0:] = (acc_scr[hh, :, q0:] * jnp.exp2(m_old - m_new)
                                       + _dot(vs, p_ref[0, hh, :, q0:], _TN))

        active = kk <= last_kv(i)
        ahead_by = kk * bk - i * bq
        for lo, hi, q0 in _query_skips(bq):
            @pl.when(active & (ahead_by + bk - 1 > 0) & (ahead_by >= lo) & (ahead_by < hi))
            def _(q0=q0):
                step(True, q0)

        @pl.when(active & jnp.logical_not(ahead_by + bk - 1 > 0))
        def _():
            step(False, 0)

        @pl.when(kk == nk - 1)
        def _():
            a = acc_scr[0]
            b = acc_scr[1]
            la = a[64:65, :]
            lb = b[0:1, :]
            o_ref[...] = jnp.where(lane_q < 64, (a / la).T, (b / lb).T)
            lse_ref[0] = jnp.concatenate([m_scr[0] + jnp.log(la) * _LOG2E, m_scr[1] + jnp.log(lb) * _LOG2E], axis=0)

    kvi = lambda i, kk: jnp.minimum(kk, last_kv(i))
    kv = lambda off: pl.BlockSpec((bk, 128), lambda j, i, kk: (kvi(i, kk), off + j))
    blk = lambda j, i, kk: (j * nq + i) * nk + kvi(i, kk)
    return pl.pallas_call(
        body, name="attn_fwd",
        out_shape=(jax.ShapeDtypeStruct((t, D_ATT), F32), jax.ShapeDtypeStruct((_NPAIR, 2, t), F32),
                   jax.ShapeDtypeStruct((_NPAIR * nq * nk, 2, bk, bq), BF16),
                   jax.ShapeDtypeStruct((_NPAIR * nq * nk, 2, bq), F32)),
        grid=(_NPAIR, nq, nk),
        in_specs=[pl.BlockSpec((bq, 128), lambda j, i, kk: (i, _QB + j)),
                  kv(_KB), kv(_VB),
                  pl.BlockSpec((1, bq, 2), lambda j, i, kk: (j, i, 0)),
                  pl.BlockSpec((1, bk, 2), lambda j, i, kk: (j, kvi(i, kk), 0))],
        out_specs=(pl.BlockSpec((bq, 128), lambda j, i, kk: (i, j)),
                   pl.BlockSpec((1, 2, bq), lambda j, i, kk: (j, 0, i)),
                   pl.BlockSpec((1, 2, bk, bq), lambda j, i, kk: (blk(j, i, kk), 0, 0, 0)),
                   pl.BlockSpec((1, 2, bq), lambda j, i, kk: (blk(j, i, kk), 0, 0))),
        scratch_shapes=[pltpu.VMEM((2, bq, 128), BF16), pltpu.VMEM((2, bk, bq), F32),
                        pltpu.VMEM((2, 1, bq), F32), pltpu.VMEM((2, 128, bq), F32)],
        compiler_params=_cp("parallel", "parallel", "arbitrary"),
    )(proj, proj, proj, c_col, c_col)


def _attn_bwd(proj, lse_row, dl_row, do, p_blocks, m_run, exchange=None):
    t = proj.shape[0]
    bq, bk = _attn_blocks(t)
    nq, nk = t // bq, t // bk
    rs = 16

    def first_q(kk):
        return (kk * bk) // bq

    def body(q_ref, k_ref, v_ref, lse_ref, dl_ref, do_ref, pblk_ref, mrun_ref,
             dq_ref, dk_ref, dv_ref, dck_ref, dcq_ref,
             qs_scr, doh_scr, ks_scr, dp_scr, p_scr, ds_scr, dq_scr, dk_scr, dv_scr):
        kk = pl.program_id(1)
        i = pl.program_id(2)
        lane_q = _iota((bq, 128), 1)
        lane_k = _iota((bk, 128), 1)
        qrows = pl.ds(pl.multiple_of(i * bq, 128), bq)

        @pl.when(kk == 0)
        def _():
            q = q_ref[...] * _SCALE
            do_ = do_ref[...]
            for hh in range(2):
                qs_scr[hh, qrows, :] = _sum_operand(q, lane_q, hh, 3)
                doh_scr[hh, qrows, :] = jnp.where(_head_lanes(lane_q, hh)[0], do_, 0.0).astype(BF16)
                dq_scr[hh, i] = jnp.zeros((128, bq), F32)

        @pl.when(i == 0)
        def _():
            dk_scr[...] = jnp.zeros_like(dk_scr)
            dv_scr[...] = jnp.zeros_like(dv_scr)
            k = k_ref[...]
            for hh in range(2):
                ks_scr[hh] = _sum_operand(k, lane_k, hh, 0)

        def step(q0):
            seen = pl.ds(pl.multiple_of(i * bq + q0, 128), bq - q0)
            v16 = v_ref[...].astype(BF16)
            dl = dl_ref[0, :, q0:]
            rescale = jnp.exp2(mrun_ref[0, :, q0:] - lse_ref[0, :, q0:])
            for hh in range(2):
                dp_scr[hh, :, q0:] = _dot(v16, doh_scr[hh, seen, :], _NT)
            for hh in range(2):
                qs = qs_scr[hh, seen, :]
                doh = doh_scr[hh, seen, :]

                def strip(r, carry):
                    rows = pl.ds(pl.multiple_of(r * rs, rs), rs)
                    p = pblk_ref[0, hh, rows, q0:].astype(F32) * rescale[hh:hh + 1, :]
                    p_scr[hh, rows, q0:] = p.astype(BF16)
                    ds_scr[hh, rows, q0:] = (p * (dp_scr[hh, rows, q0:] - dl[hh:hh + 1, :])).astype(BF16)
                    return carry

                lax.fori_loop(0, bk // rs, strip, 0, unroll=True)
                dv_scr[...] += _dot(p_scr[hh, :, q0:], doh)
                dk_scr[hh] += _dot(ds_scr[hh, :, q0:], qs)
                dq_scr[hh, i, :, q0:] += _dot(ks_scr[hh], ds_scr[hh, :, q0:], _TN)

        ahead_by = kk * bk - i * bq
        for lo, hi, q0 in _query_skips(bq):
            @pl.when((i >= first_q(kk)) & (ahead_by >= lo) & (ahead_by < hi))
            def _(q0=q0):
                step(q0)

        @pl.when(i == nq - 1)
        def _():
            dka = dk_scr[0]
            dkb = dk_scr[1]
            dk_ref[...] = jnp.where(lane_k < 64, dka, dkb).astype(BF16)
            dv_ref[...] = dv_scr[...].astype(BF16)
            dck_ref[0] = -jnp.where(_iota((bk, 2), 1) == 0, dka[:, 67:68], dkb[:, 3:4])

        @pl.when((kk == nk - 1) & (i == nq - 1))
        def _():
            for ii in range(nq):
                cols = slice(ii * bq, (ii + 1) * bq)
                dqa = dq_scr[0, ii]
                dqb = dq_scr[1, ii]
                dq_ref[cols, :] = (jnp.where(lane_q < 64, dqa.T, dqb.T) * _SCALE).astype(BF16)
                dcq_ref[0, :, cols] = jnp.concatenate([dqa[64:65, :], dqb[0:1, :]], axis=0)

    qi = lambda kk, i: jnp.where(kk == 0, i, nq - 1)
    qspec = lambda off: pl.BlockSpec((bq, 128), lambda j, kk, i: (qi(kk, i), off + j))
    kspec = lambda off: pl.BlockSpec((bk, 128), lambda j, kk, i: (kk, off + j))
    rowspec = pl.BlockSpec((1, 2, bq), lambda j, kk, i: (j, 0, jnp.maximum(i, first_q(kk))))
    blk = lambda j, kk, i: (j * nq + jnp.maximum(i, first_q(kk))) * nk + kk
    return _hosted_call(
        body, name="attn_bwd",
        out_shape=(jax.ShapeDtypeStruct((t, D_ATT), BF16), jax.ShapeDtypeStruct((t, D_ATT), BF16),
                   jax.ShapeDtypeStruct((t, D_ATT), BF16), jax.ShapeDtypeStruct((_NPAIR, t, 2), F32),
                   jax.ShapeDtypeStruct((_NPAIR, 2, t), F32)),
        grid=(_NPAIR, nk, nq),
        in_specs=[qspec(_QB), kspec(_KB), kspec(_VB),
                  rowspec, rowspec, qspec(0),
                  pl.BlockSpec((1, 2, bk, bq), lambda j, kk, i: (blk(j, kk, i), 0, 0, 0)),
                  pl.BlockSpec((1, 2, bq), lambda j, kk, i: (blk(j, kk, i), 0, 0))],
        out_specs=(pl.BlockSpec((t, 128), lambda j, kk, i: (0, j)),
                   pl.BlockSpec((bk, 128), lambda j, kk, i: (kk, j)),
                   pl.BlockSpec((bk, 128), lambda j, kk, i: (kk, j)),
                   pl.BlockSpec((1, bk, 2), lambda j, kk, i: (j, kk, 0)),
                   pl.BlockSpec((1, 2, t), lambda j, kk, i: (j, 0, 0))),
        scratch_shapes=[pltpu.VMEM((2, t, 128), BF16), pltpu.VMEM((2, t, 128), BF16), pltpu.VMEM((2, bk, 128), BF16),
                        pltpu.VMEM((2, bk, bq), F32),
                        pltpu.VMEM((2, bk, bq), BF16), pltpu.VMEM((2, bk, bq), BF16),
                        pltpu.VMEM((2, nq, 128, bq), F32), pltpu.VMEM((2, bk, 128), F32), pltpu.VMEM((bk, 128), F32)],
        operands=(proj, proj, proj, lse_row, dl_row, do, p_blocks, m_run),
        semantics=("parallel", "arbitrary", "arbitrary"), exchange=exchange)


def _premerge_fwd(y, o, proj, gamma):
    t = y.shape[0]
    tm = _row_tile_wide(t)

    def body(y_ref, z_ref, o_ref, za_ref, g_ref, ys_ref, ya_ref):
        z = z_ref[...]
        u = y_ref[...] * (z * _sigmoid(z))
        for g in range(G_SSD):
            gs = slice(_GW * g, _GW * (g + 1))
            ug = u[:, gs]
            r = lax.rsqrt(jnp.mean(ug * ug, axis=-1, keepdims=True) + EPS)
            ys_ref[:, gs] = (ug * r * g_ref[:, gs]).astype(BF16)
        za = za_ref[...]
        ya_ref[...] = (o_ref[...] * (za * _sigmoid(za))).astype(BF16)

    return pl.pallas_call(
        body, name="premerge_fwd",
        out_shape=(jax.ShapeDtypeStruct((t, D_SSD), BF16), jax.ShapeDtypeStruct((t, D_ATT), BF16)),
        grid=(t // tm,),
        in_specs=[pl.BlockSpec((tm, D_SSD), lambda i: (i, 0)),
                  pl.BlockSpec((tm, D_SSD), lambda i: (i, C_Z // D_SSD)),
                  pl.BlockSpec((tm, D_ATT), lambda i: (i, 0)),
                  pl.BlockSpec((tm, D_ATT), lambda i: (i, C_ZA // D_ATT)),
                  pl.BlockSpec((1, D_SSD), lambda i: (0, 0))],
        out_specs=(pl.BlockSpec((tm, D_SSD), lambda i: (i, 0)), pl.BlockSpec((tm, D_ATT), lambda i: (i, 0))),
        compiler_params=_cp("parallel"),
    )(y, proj, o, proj, gamma)


def _premerge_bwd(dys, dya, y, o, proj, gamma, exchange=None):
    t = y.shape[0]
    tm = _row_tile_wide(t)

    def body(dys_ref, dya_ref, y_ref, z_ref, o_ref, za_ref, g_ref, dy_ref, dz_ref, do_ref, dza_ref, dg_ref, dl_ref):
        i = pl.program_id(0)
        z = z_ref[...]
        sz = _sigmoid(z)
        silu = z * sz
        dsilu = sz * (1.0 + z * (1.0 - sz))
        yv = y_ref[...]
        u = yv * silu
        parts = []
        for g in range(G_SSD):
            gs = slice(_GW * g, _GW * (g + 1))
            ug = u[:, gs]
            r = lax.rsqrt(jnp.mean(ug * ug, axis=-1, keepdims=True) + EPS)
            n = ug * r
            dout = dys_ref[:, gs]
            dn = dout * g_ref[:, gs]
            du = r * (dn - n * jnp.mean(dn * n, axis=-1, keepdims=True))
            dy_ref[:, gs] = du * silu[:, gs]
            dz_ref[:, gs] = (du * yv[:, gs] * dsilu[:, gs]).astype(BF16)
            parts.append(jnp.sum(dout * n, axis=0, keepdims=True))
        dg = jnp.concatenate(parts, axis=1)
        za = za_ref[...]
        sa = _sigmoid(za)
        dya_ = dya_ref[...]
        ov = o_ref[...]
        do = dya_ * (za * sa)
        do_ref[...] = do
        dza_ref[...] = (dya_ * ov * (sa * (1.0 + za * (1.0 - sa)))).astype(BF16)
        pick = (jnp.right_shift(_iota((D_ATT, 128), 0), 6) == _iota((D_ATT, 128), 1)).astype(F32)
        dl_ref[...] = _dot_exact(do * ov, pick)

        @pl.when(i == 0)
        def _():
            dg_ref[...] = dg

        @pl.when(i > 0)
        def _():
            dg_ref[...] += dg

    ssd = pl.BlockSpec((tm, D_SSD), lambda i: (i, 0))
    att = pl.BlockSpec((tm, D_ATT), lambda i: (i, 0))
    vec = pl.BlockSpec((1, D_SSD), lambda i: (0, 0))
    return _hosted_call(
        body, name="premerge_bwd",
        out_shape=(jax.ShapeDtypeStruct((t, D_SSD), F32), jax.ShapeDtypeStruct((t, D_SSD), BF16),
                   jax.ShapeDtypeStruct((t, D_ATT), F32), jax.ShapeDtypeStruct((t, D_ATT), BF16),
                   jax.ShapeDtypeStruct((1, D_SSD), F32), jax.ShapeDtypeStruct((t, 128), F32)),
        grid=(t // tm,),
        in_specs=[ssd, att, ssd, pl.BlockSpec((tm, D_SSD), lambda i: (i, C_Z // D_SSD)), att,
                  pl.BlockSpec((tm, D_ATT), lambda i: (i, C_ZA // D_ATT)), vec],
        out_specs=(ssd, ssd, att, att, vec, pl.BlockSpec((tm, 128), lambda i: (i, 0))),
        scratch_shapes=[],
        operands=(dys, dya, y, proj, o, proj, gamma), semantics=("arbitrary",), exchange=exchange)


_G_BLK = C_G // D_MODEL


def _merge_fwd(a, b, proj, gate_bias):
    t = a.shape[0]
    tm = _row_tile(t)

    def body(a_ref, b_ref, gs_ref, ga_ref, bias_ref, m_ref):
        g_ssd = _sigmoid(gs_ref[...] + bias_ref[:, 0:D_MODEL])
        g_att = _sigmoid(ga_ref[...] + bias_ref[:, D_MODEL:2 * D_MODEL])
        m_ref[...] = (g_ssd * a_ref[...] + g_att * b_ref[...]).astype(BF16)

    row = pl.BlockSpec((tm, D_MODEL), lambda i: (i, 0))
    return pl.pallas_call(
        body, name="merge_fwd",
        out_shape=jax.ShapeDtypeStruct((t, D_MODEL), BF16),
        grid=(t // tm,),
        in_specs=[row, row,
                  pl.BlockSpec((tm, D_MODEL), lambda i: (i, _G_BLK)),
                  pl.BlockSpec((tm, D_MODEL), lambda i: (i, _G_BLK + 1)),
                  pl.BlockSpec((1, 2 * D_MODEL), lambda i: (0, 0))],
        out_specs=row,
        compiler_params=_cp("parallel"),
    )(a, b, proj, proj, gate_bias)


def _merge_bwd(dm, a, b, proj, gate_bias):
    t = a.shape[0]
    tm = _row_tile(t)

    def body(dm_ref, a_ref, b_ref, gs_ref, ga_ref, bias_ref, da_ref, db_ref, dg_ref, dbias_ref):
        i = pl.program_id(0)
        dm_ = dm_ref[...]
        g_ssd = _sigmoid(gs_ref[...] + bias_ref[:, 0:D_MODEL])
        g_att = _sigmoid(ga_ref[...] + bias_ref[:, D_MODEL:2 * D_MODEL])
        da_ref[...] = (dm_ * g_ssd).astype(BF16)
        db_ref[...] = (dm_ * g_att).astype(BF16)
        dgs = dm_ * a_ref[...] * g_ssd * (1.0 - g_ssd)
        dga = dm_ * b_ref[...] * g_att * (1.0 - g_att)
        dg_ref[:, 0:D_MODEL] = dgs.astype(BF16)
        dg_ref[:, D_MODEL:2 * D_MODEL] = dga.astype(BF16)
        part = jnp.concatenate([jnp.sum(dgs, axis=0, keepdims=True), jnp.sum(dga, axis=0, keepdims=True)], axis=1)

        @pl.when(i == 0)
        def _():
            dbias_ref[...] = part

        @pl.when(i > 0)
        def _():
            dbias_ref[...] += part

    row = pl.BlockSpec((tm, D_MODEL), lambda i: (i, 0))
    wide = pl.BlockSpec((tm, 2 * D_MODEL), lambda i: (i, 0))
    vec = pl.BlockSpec((1, 2 * D_MODEL), lambda i: (0, 0))
    return pl.pallas_call(
        body, name="merge_bwd",
        out_shape=(jax.ShapeDtypeStruct((t, D_MODEL), BF16), jax.ShapeDtypeStruct((t, D_MODEL), BF16),
                   jax.ShapeDtypeStruct((t, 2 * D_MODEL), BF16), jax.ShapeDtypeStruct((1, 2 * D_MODEL), F32)),
        grid=(t // tm,),
        in_specs=[row, row, row,
                  pl.BlockSpec((tm, D_MODEL), lambda i: (i, _G_BLK)),
                  pl.BlockSpec((tm, D_MODEL), lambda i: (i, _G_BLK + 1)), vec],
        out_specs=(row, row, wide, vec),
        compiler_params=_cp("arbitrary"),
    )(dm, a, b, proj, proj, gate_bias)


def _post(o2, h, target, g):
    t = o2.shape[0]
    nc = t // CHUNK

    def body(o_ref, h_ref, t_ref, g_ref, dy_ref, do_ref, dg_ref, loss_ref):
        c = pl.program_id(0)
        x = o_ref[...]
        r = lax.rsqrt(jnp.mean(x * x, axis=-1, keepdims=True) + EPS)
        n = x * r
        y = h_ref[...] + n * g_ref[...]
        diff = jnp.where(c > 0, y - t_ref[...], 0.0)
        dy = diff * (1.0 / D_MODEL)
        dy_ref[...] = dy
        gdy = dy * g_ref[...]
        do_ref[...] = (r * (gdy - n * jnp.mean(gdy * n, axis=-1, keepdims=True))).astype(BF16)
        dg = jnp.sum(dy * n, axis=0, keepdims=True)
        lpart = 0.5 * jnp.sum(jnp.sum(diff * diff, axis=1, keepdims=True), axis=0, keepdims=True) * (1.0 / D_MODEL)
        sel = (_iota((8, 128), 0) == 0) & (_iota((8, 128), 1) == 0)

        @pl.when(c == 0)
        def _():
            dg_ref[...] = dg
            loss_ref[...] = jnp.zeros_like(loss_ref)

        @pl.when(c > 0)
        def _():
            dg_ref[...] += dg
            loss_ref[...] += jnp.where(sel, lpart, 0.0)

    row = pl.BlockSpec((CHUNK, D_MODEL), lambda c: (c, 0))
    vec = pl.BlockSpec((1, D_MODEL), lambda c: (0, 0))
    return pl.pallas_call(
        body, name="post",
        out_shape=(jax.ShapeDtypeStruct((t, D_MODEL), F32), jax.ShapeDtypeStruct((t, D_MODEL), BF16),
                   jax.ShapeDtypeStruct((1, D_MODEL), F32), jax.ShapeDtypeStruct((8, 128), F32)),
        grid=(nc,),
        in_specs=[row, row, pl.BlockSpec((CHUNK, D_MODEL), lambda c: (jnp.maximum(c - 1, 0), 0)), vec],
        out_specs=(row, row, vec, pl.BlockSpec((8, 128), lambda c: (0, 0))),
        compiler_params=_cp("arbitrary"),
    )(o2, h, target, g)


def _mm_tiles(t):
    return _tile(t, (704, 384, 128))


def _local_step(h, target, w_main, w_small, pr_slots, ids, norm_pre, conv_w, conv_b, bias_row, a_row,
                dsk_row, ssd_norm, gate_bias, norm_post):
    t = h.shape[0]
    tm = _mm_tiles(t)
    u = _norm1_fwd(h, norm_pre)
    proj, pr_slots = _matmul(u, w_main, "nt", F32, "inproj", tm, 1024, D_MODEL,
                             exchange=_gather_stage([pr_slots], to_sibling=False))
    small, pr_slots = _matmul(u, w_small, "nt", F32, "inproj_small", tm, N_SMALL, D_MODEL,
                              exchange=_gather_stage([pr_slots], to_sibling=True))
    wps = pr_slots[:, 0:512].reshape(D_SSD, D_MODEL)
    wpa = pr_slots[:, 512:768].reshape(D_ATT, D_MODEL)
    wout = pr_slots[:, 768:1024].reshape(D_MODEL, D_MODEL)
    dtlf = _small_fwd(small, bias_row)
    xbc = _conv_fwd(proj, conv_w, conv_b)
    y, hin = _ssd_fwd(xbc, dtlf, a_row, dsk_row)
    c_tok = dtlf[:, H_SSD:H_SSD + H_ATT]
    c_tok = jnp.where(jnp.arange(t)[:, None] < PADF, _C_FILLER, c_tok)
    c_col = c_tok.reshape(t, _NPAIR, 2).transpose(1, 0, 2)
    o, lse, p_blocks, m_run = _attn_fwd(proj, c_col)
    ys, ya = _premerge_fwd(y, o, proj, ssd_norm)
    a = _matmul(ys, wps, "nn", F32, "proj_ssd", tm, D_MODEL, D_SSD)
    b = _matmul(ya, wpa, "nn", F32, "proj_att", tm, D_MODEL, D_ATT)
    merged = _merge_fwd(a, b, proj, gate_bias)
    o2 = _matmul(merged, wout, "nn", F32, "out_proj", tm, D_MODEL, D_MODEL)
    dy_out, do2, d_norm_post, loss_blk = _post(o2, h, target, norm_post)

    dm = _matmul(do2, wout, "nt", F32, "out_proj_dx", tm, D_MODEL, D_MODEL)
    d_wout = _matmul(merged, do2, "tn", F32, "out_proj_dw", D_MODEL, D_MODEL, tm)
    da, db, dgraw, d_gate_bias = _merge_bwd(dm, a, b, proj, gate_bias)
    dys = _matmul(da, wps, "nt", F32, "proj_ssd_dx", tm, D_SSD, D_MODEL)
    d_wps = _matmul(ys, da, "tn", F32, "proj_ssd_dw", D_SSD, D_MODEL, tm)
    dya = _matmul(db, wpa, "nt", F32, "proj_att_dx", tm, D_ATT, D_MODEL)
    d_wpa = _matmul(ya, db, "tn", F32, "proj_att_dw", D_ATT, D_MODEL, tm)
    g32_pr = jnp.concatenate([d_wps.reshape(4, 512, D_MODEL), d_wpa.reshape(4, 256, D_MODEL),
                              d_wout.reshape(4, 256, D_MODEL)], axis=1)
    dy, dz, do, dza, d_ssd_norm, dl, ra_pr = _premerge_bwd(dys, dya, y, o, proj, ssd_norm,
                                                           exchange=_pair_swap([g32_pr]))
    pb_pr = _add_pair(ids, g32_pr, ra_pr)
    dl_row = dl[:, 0:H_ATT].T.reshape(_NPAIR, 2, t)
    dq, dk, dv, dc_key, dc_qry, rb_pr = _attn_bwd(proj, lse, dl_row, do, p_blocks, m_run,
                                                  exchange=_chip_exchange([pb_pr]))
    half_pr = _add_chips(ids, g32_pr, ra_pr, rb_pr)
    dxbc, ddt, d_a, d_dsk = _ssd_bwd(xbc, dtlf, a_row, dsk_row, hin, dy)
    dxbc_raw, d_conv_w, d_conv_b = _conv_bwd(dxbc, proj, conv_w, conv_b)
    dc_tok = jnp.transpose(dc_key, (1, 0, 2)).reshape(t, H_ATT) + dc_qry.reshape(H_ATT, t).T
    dsm = ddt + jnp.pad(dc_tok, ((0, 0), (H_SSD, N_SMALL - H_SSD - H_ATT)))
    dsmall, d_bias_row = _small_bwd(dsm, small, bias_row)
    dproj = [dz, dxbc_raw, dza, dq, dk, dv, dgraw]
    return dict(loss_blk=loss_blk, u=u, dy_out=dy_out, dproj=dproj, dsmall=dsmall, half_pr=half_pr,
                d_conv_w=d_conv_w, d_conv_b=d_conv_b,
                d_bias_row=d_bias_row, d_a=d_a, d_dsk=d_dsk, d_ssd_norm=d_ssd_norm,
                d_gate_bias=d_gate_bias, d_norm_post=d_norm_post)


def _to_aligned_rows(slots):
    w = slots.reshape(N_COLS, slots.shape[2])

    def cut(o):
        return w[o[0]:o[0] + o[1]]
    main = jnp.concatenate([cut(O_Z), cut(O_XBC), cut(O_ZA), cut(O_Q), cut(O_K), cut(O_V), cut(O_G)], axis=0)
    pad = jnp.zeros((N_SMALL - H_SSD - H_ATT, w.shape[1]), w.dtype)
    small = jnp.concatenate([cut(O_DT), cut(O_F), pad], axis=0)
    assert main.shape[0] == N_MAIN and small.shape[0] == N_SMALL
    return main, small


def _from_aligned_rows(main, small):
    def cm(c0, n):
        return main[c0:c0 + n]
    flat = jnp.concatenate([cm(C_Z, 2048), cm(C_XBC, 3072), small[0:H_SSD], cm(C_ZA, 1024),
                            cm(C_Q, 1024), cm(C_K, 1024), cm(C_V, 1024), small[H_SSD:H_SSD + H_ATT],
                            cm(C_G, 2048)], axis=0)
    return flat.reshape(4, N_COLS // 4, flat.shape[1])


_MESH = pl.DeviceIdType.MESH
_ANY = pl.BlockSpec(memory_space=pl.ANY)
_VM = pl.BlockSpec(memory_space=pltpu.VMEM)
_HALF = 512
N_DEV = 8


def _coords():
    return lax.axis_index("x"), lax.axis_index("y"), lax.axis_index("c")


def _other_chips(x, y):
    return [(1 - x, y), (x, 1 - y), (1 - x, 1 - y)]


def _half(cc):
    return pl.ds(cc * _HALF, _HALF)


def _gather_shards(slots):
    n = len(slots)

    def body(*refs):
        buf = refs[n:2 * n]
        send_sems, recv_sems = refs[2 * n:]
        x, y, c = _coords()
        chip = 2 * x + y
        sibling = (x, y, 1 - c)
        chips = _other_chips(x, y)

        def copy(i, frm, cc, k, to):
            part = buf[i].at[frm, :, _half(cc)]
            return pltpu.make_async_remote_copy(src_ref=part, dst_ref=part, send_sem=send_sems.at[6 * i + k],
                                                recv_sem=recv_sems.at[6 * i + k], device_id=to, device_id_type=_MESH)

        def chip_of(k):
            return 2 * chips[k][0] + chips[k][1]

        first = [copy(i, chip, c, k, (*chips[k], c)) for k in range(3) for i in range(n)]
        for cp in first:
            cp.start()
        passed = []
        for k in range(3):
            for i in range(n):
                copy(i, chip_of(k), c, k, (*chips[k], c)).wait_recv()
                passed.append(copy(i, chip_of(k), c, 3 + k, sibling))
                passed[-1].start()
        for k in range(3):
            for i in range(n):
                copy(i, chip_of(k), 1 - c, 3 + k, sibling).wait_recv()
        for cp in first + passed:
            cp.wait_send()

    return pl.pallas_call(
        body, name="gather_shards",
        out_shape=tuple(jax.ShapeDtypeStruct(s.shape, s.dtype) for s in slots),
        in_specs=[_ANY] * n, out_specs=tuple([_ANY] * n),
        input_output_aliases={i: i for i in range(n)},
        scratch_shapes=[pltpu.SemaphoreType.DMA((6 * n,)), pltpu.SemaphoreType.DMA((6 * n,))],
    )(*slots)


def _allgather8(block, name):
    rows, width = block.shape

    def body(x_ref, out_ref, send_sems, recv_sems, local_sem):
        x, y, c = _coords()
        me, sibling = (x, y, c), (x, y, 1 - c)
        chips = _other_chips(x, y)

        def slot(px, py, pc):
            return out_ref.at[4 * px + 2 * py + pc]

        def copy(k, blk, to, src=None):
            return pltpu.make_async_remote_copy(src_ref=slot(*blk) if src is None else src, dst_ref=slot(*blk),
                                                send_sem=send_sems.at[k], recv_sem=recv_sems.at[k],
                                                device_id=to, device_id_type=_MESH)

        mine = pltpu.make_async_copy(x_ref, slot(*me), local_sem)
        mine.start()
        first = [copy(0, me, sibling, src=x_ref)]
        first += [copy(1 + j, me, (*chip, c), src=x_ref) for j, chip in enumerate(chips)]
        for cp in first:
            cp.start()
        passed = [copy(4 + j, (*chip, c), sibling) for j, chip in enumerate(chips)]
        for j, chip in enumerate(chips):
            copy(1 + j, (*chip, c), me).wait_recv()
            passed[j].start()
        copy(0, sibling, me).wait_recv()
        for j, chip in enumerate(chips):
            copy(4 + j, (*chip, 1 - c), me).wait_recv()
        for cp in first + passed:
            cp.wait_send()
        mine.wait()

    return pl.pallas_call(
        body, name=name,
        out_shape=jax.ShapeDtypeStruct((N_DEV, rows, width), block.dtype),
        in_specs=[_VM], out_specs=_VM,
        scratch_shapes=[pltpu.SemaphoreType.DMA((7,)), pltpu.SemaphoreType.DMA((7,)), pltpu.SemaphoreType.DMA],
    )(block)


def _pair_swap(arrs):
    def copies(src, dst, send_sems, recv_sems):
        x, y, c = _coords()
        return [pltpu.make_async_remote_copy(src_ref=src[i].at[:, :, _half(1 - c)], dst_ref=dst[i],
                                             send_sem=send_sems.at[i], recv_sem=recv_sems.at[i],
                                             device_id=(x, y, 1 - c), device_id_type=_MESH) for i in range(len(src))]

    shapes = tuple(jax.ShapeDtypeStruct((4, a.shape[1], _HALF), a.dtype) for a in arrs)
    return tuple(arrs), shapes, copies, len(arrs), False


def _chip_exchange(arrs):
    def copies(src, dst, send_sems, recv_sems):
        x, y, c = _coords()
        chips = _other_chips(x, y)
        return [pltpu.make_async_remote_copy(src_ref=src[i].at[2 * chips[k][0] + chips[k][1]], dst_ref=dst[i].at[k],
                                             send_sem=send_sems.at[3 * i + k], recv_sem=recv_sems.at[3 * i + k],
                                             device_id=(*chips[k], c), device_id_type=_MESH)
                for k in range(3) for i in range(len(src))]

    shapes = tuple(jax.ShapeDtypeStruct((3,) + a.shape[1:], a.dtype) for a in arrs)
    return tuple(arrs), shapes, copies, 3 * len(arrs), False


def _gather_stage(slots, to_sibling):
    def copies(buf, _, send_sems, recv_sems):
        x, y, c = _coords()
        chips = _other_chips(x, y)
        out = []
        for k in range(3):
            for i in range(len(buf)):
                frm = 2 * chips[k][0] + chips[k][1] if to_sibling else 2 * x + y
                part = buf[i].at[frm, :, _half(c)]
                out.append(pltpu.make_async_remote_copy(
                    src_ref=part, dst_ref=part, send_sem=send_sems.at[3 * i + k], recv_sem=recv_sems.at[3 * i + k],
                    device_id=(x, y, 1 - c) if to_sibling else (*chips[k], c), device_id_type=_MESH))
        return out

    shapes = tuple(jax.ShapeDtypeStruct(s.shape, s.dtype) for s in slots)
    return tuple(slots), shapes, copies, 3 * len(slots), True


def _pair_join_halves(fulls):
    n = len(fulls)

    def body(*refs):
        buf = refs[n:2 * n]
        send_sems, recv_sems = refs[2 * n:]
        x, y, c = _coords()

        def remote(i, cc):
            part = buf[i].at[:, _half(cc)]
            return pltpu.make_async_remote_copy(src_ref=part, dst_ref=part, send_sem=send_sems.at[i],
                                                recv_sem=recv_sems.at[i], device_id=(x, y, 1 - c), device_id_type=_MESH)

        for i in range(n):
            remote(i, c).start()
        for i in range(n):
            remote(i, c).wait_send()
            remote(i, 1 - c).wait_recv()

    return pl.pallas_call(
        body, name="pair_join_halves",
        out_shape=tuple(jax.ShapeDtypeStruct(a.shape, a.dtype) for a in fulls),
        in_specs=[_ANY] * n, out_specs=tuple([_ANY] * n),
        input_output_aliases={i: i for i in range(n)},
        scratch_shapes=[pltpu.SemaphoreType.DMA((n,)), pltpu.SemaphoreType.DMA((n,))],
    )(*fulls)


_RED_TC = 128
_RED_NT = _HALF // _RED_TC


def _add_pair(ids, g32, recv_a):
    rows = g32.shape[1]

    def body(ids_ref, g_ref, r_ref, o_ref):
        o_ref[...] = (g_ref[...] + r_ref[...]).astype(BF16)

    blk = pl.BlockSpec((1, rows, _RED_TC), lambda j, l, ids: (j, 0, l))
    return pl.pallas_call(
        body, name="add_pair",
        out_shape=jax.ShapeDtypeStruct((4, rows, _HALF), BF16),
        grid_spec=pltpu.PrefetchScalarGridSpec(
            num_scalar_prefetch=1, grid=(4, _RED_NT),
            in_specs=[pl.BlockSpec((1, rows, _RED_TC), lambda j, l, ids: (j, 0, ids[0] * _RED_NT + l)), blk],
            out_specs=blk),
        compiler_params=_cp("parallel", "parallel"),
    )(ids, g32, recv_a)


def _add_chips(ids, g32, recv_a, recv_b):
    rows = g32.shape[1]

    def body(ids_ref, g_ref, a_ref, b_ref, o_ref):
        acc = g_ref[0] + a_ref[0]
        for k in range(3):
            acc = acc + b_ref[k].astype(F32)
        o_ref[...] = acc

    return pl.pallas_call(
        body, name="add_chips",
        out_shape=jax.ShapeDtypeStruct((rows, 2 * _HALF), F32),
        grid_spec=pltpu.PrefetchScalarGridSpec(
            num_scalar_prefetch=1, grid=(_RED_NT,),
            in_specs=[pl.BlockSpec((1, rows, _RED_TC), lambda l, ids: (ids[1], 0, ids[0] * _RED_NT + l)),
                      pl.BlockSpec((1, rows, _RED_TC), lambda l, ids: (ids[1], 0, l)),
                      pl.BlockSpec((3, rows, _RED_TC), lambda l, ids: (0, 0, l))],
            out_specs=pl.BlockSpec((rows, _RED_TC), lambda l, ids: (0, ids[0] * _RED_NT + l))),
        compiler_params=_cp("parallel"),
    )(ids, g32, recv_a, recv_b)


def _sum8(gathered):
    _, rows, width = gathered.shape

    def body(g_ref, o_ref):
        acc = g_ref[0]
        for d in range(1, N_DEV):
            acc = acc + g_ref[d]
        o_ref[...] = acc

    return pl.pallas_call(
        body, name="sum8",
        out_shape=jax.ShapeDtypeStruct((rows, width), F32),
        in_specs=[_VM], out_specs=_VM,
    )(gathered)


def _adamw(w, g, m, v, name):
    rows, cols = w.shape
    budget = (3 << 20) // 2
    tr, tc = rows, cols
    if rows * cols * 4 > budget:
        if rows % 8 == 0:
            tr = max(c for c in range(8, rows, 8) if rows % c == 0 and c * cols * 4 <= budget)
        else:
            tc = next(c for c in (512, 256, 128) if cols % c == 0 and rows * c * 4 <= budget)
    c1 = 1.0 - ADAM_B1 ** ADAM_STEP
    c2 = 1.0 - ADAM_B2 ** ADAM_STEP

    def body(w_ref, g_ref, m_ref, v_ref, d_ref, mo_ref, vo_ref):
        gg = g_ref[...]
        mn = ADAM_B1 * m_ref[...] + (1.0 - ADAM_B1) * gg
        vn = ADAM_B2 * v_ref[...] + (1.0 - ADAM_B2) * (gg * gg)
        mo_ref[...] = mn
        vo_ref[...] = vn
        d_ref[...] = -ADAM_LR * ((mn / c1) / (jnp.sqrt(vn / c2) + ADAM_EPS) + ADAM_WD * w_ref[...])

    blk = pl.BlockSpec((tr, tc), lambda i, j: (i, j))
    shp = jax.ShapeDtypeStruct((rows, cols), F32)
    return pl.pallas_call(
        body, name=name, out_shape=(shp, shp, shp), grid=(rows // tr, cols // tc),
        in_specs=[blk] * 4, out_specs=(blk, blk, blk),
        compiler_params=_cp("parallel", "parallel"),
    )(w, g, m, v)


def _rows128(a):
    return a.reshape(-1, 128)


def _pack_small(norm_pre, conv_b, ssd_norm, gate_bias, norm_post, dt_bias, a_log, d_skip, fgate_bias):
    tiny = jnp.concatenate([dt_bias.reshape(-1), a_log.reshape(-1), d_skip.reshape(-1), fgate_bias.reshape(-1),
                            jnp.zeros((16,), F32)])
    return jnp.concatenate([_rows128(norm_pre), _rows128(conv_b), _rows128(ssd_norm), _rows128(gate_bias),
                            _rows128(norm_post), tiny.reshape(1, 128)], axis=0)


_SMALL_PAD = 80


def _unpack_small(p):
    tiny = p[72]
    return dict(norm_pre=p[0:8].reshape(1, 1024), conv_b=p[8:32].reshape(1, 3072), ssd_norm=p[32:48].reshape(1, 2048),
                gate_bias=p[48:64].reshape(1, 2048), norm_post=p[64:72].reshape(1, 1024),
                dt_bias=tiny[0:32].reshape(1, 32), a_log=tiny[32:64].reshape(1, 32),
                d_skip=tiny[64:96].reshape(1, 32), fgate_bias=tiny[96:112].reshape(1, 16))


def _pad_rows(a, rows):
    return jnp.concatenate([a, jnp.zeros((rows - a.shape[0], a.shape[1]), a.dtype)], axis=0)


def kernel(x, meta_tokens, norm_pre, w_in, conv_w, conv_b, dt_bias, a_log, d_skip, ssd_norm, fgate_bias, gate_bias, w_proj_ssd, w_proj_att, w_out, norm_post, loss_target, m_meta_tokens, m_norm_pre, m_w_in, m_conv_w, m_conv_b, m_dt_bias, m_a_log, m_d_skip, m_ssd_norm, m_fgate_bias, m_gate_bias, m_w_proj_ssd, m_w_proj_att, m_w_out, m_norm_post, v_meta_tokens, v_norm_pre, v_w_in, v_conv_w, v_conv_b, v_dt_bias, v_a_log, v_d_skip, v_ssd_norm, v_fgate_bias, v_gate_bias, v_w_proj_ssd, v_w_proj_att, v_w_out, v_norm_post):
    cx, cy, cc = _coords()
    chip = 2 * cx + cy
    ids = jnp.stack([cc, chip]).astype(jnp.int32)
    seq = x.shape[1]

    w_in_sh = jnp.transpose(w_in[0]).astype(BF16)
    w_pr_sh = jnp.concatenate([w_proj_ssd[0], w_proj_att[0], w_out[0]], axis=0).astype(BF16)

    def own_slot(sh):
        return lax.dynamic_update_slice(lax.empty((4,) + sh.shape, sh.dtype), sh[None], (chip, 0, 0))

    (g_in,) = _gather_shards([own_slot(w_in_sh)])
    w_main, w_small = _to_aligned_rows(g_in)
    sm_sh = jnp.concatenate([_rows128(meta_tokens), _rows128(conv_w[0])], axis=0)
    sm_all = _allgather8(sm_sh, "gather_small_weights")[0::2]
    meta_full = jnp.transpose(sm_all[:, 0:32].reshape(4, N_META, 256), (1, 0, 2)).reshape(N_META, D_MODEL)
    conv_w_full = jnp.transpose(sm_all[:, 32:56].reshape(4, CONV_K, 768), (1, 0, 2)).reshape(CONV_K, CONV_DIM)

    h = jnp.concatenate([jnp.zeros((PADF, D_MODEL), F32), meta_full, x[0]], axis=0)
    bias_row = jnp.concatenate([dt_bias[0], fgate_bias[0], jnp.zeros((N_SMALL - H_SSD - H_ATT,), F32)]).reshape(1, N_SMALL)
    a_neg = -jnp.exp(a_log[0])
    a_row = jnp.concatenate([a_neg, jnp.zeros((N_SMALL - H_SSD,), F32)]).reshape(1, N_SMALL)
    dsk_row = jnp.repeat(d_skip[0], 64).reshape(1, D_SSD)
    r = _local_step(h, loss_target[0], w_main, w_small, own_slot(w_pr_sh), ids, norm_pre, conv_w_full, conv_b,
                    bias_row, a_row, dsk_row, ssd_norm, gate_bias, norm_post)

    tm = _mm_tiles(h.shape[0])
    n_row_tiles = h.shape[0] // tm
    d_w_main = _matmul_cat_tn(r["dproj"], r["u"], "inproj_dw", tm)
    d_w_small = _matmul(r["dsmall"], r["u"], "tn", F32, "inproj_small_dw", N_SMALL, D_MODEL, tm)
    g32_in = _from_aligned_rows(d_w_main, d_w_small)
    first = max(n_row_tiles // 6, 1)
    du_first, ra_in = _matmul_cat_nn(r["dproj"], w_main, "inproj_dx_swap", tm, rows=(0, first),
                                     exchange=_pair_swap([g32_in]))
    pb_in = _add_pair(ids, g32_in, ra_in)
    du_a, rb_in = _matmul_cat_nn(r["dproj"], w_main, "inproj_dx_exchange", tm,
                                 rows=(first, n_row_tiles - first), fill=du_first,
                                 exchange=_chip_exchange([pb_in]))
    du_b = _matmul(r["dsmall"], w_small, "nn", F32, "inproj_small_dx", tm, D_MODEL, N_SMALL)
    dh, d_norm_pre = _norm1_bwd(du_a, du_b, h, norm_pre, r["dy_out"])
    grad_x = dh[PADF + N_META:].reshape(1, seq, D_MODEL)
    half_in = _add_chips(ids, g32_in, ra_in, rb_in)
    gw_in, gw_pr = _pair_join_halves([half_in, r["half_pr"]])

    tiny = r["d_bias_row"][0]
    part_small = _pack_small(d_norm_pre, r["d_conv_b"], r["d_ssd_norm"], r["d_gate_bias"], r["d_norm_post"],
                             tiny[0:H_SSD], r["d_a"][0, 0:H_SSD] * a_neg, r["d_dsk"].reshape(H_SSD, 64).sum(axis=1),
                             tiny[H_SSD:H_SSD + H_ATT])
    part = jnp.concatenate([_pad_rows(part_small, _SMALL_PAD), _rows128(r["d_conv_w"]),
                            _rows128(dh[PADF:PADF + N_META]), r["loss_blk"]], axis=0)
    tot = _sum8(_allgather8(part, "gather_small_grads"))
    loss = tot[_SMALL_PAD + 96 + 128, 0]
    g_small = tot[0:_SMALL_PAD]
    g_conv_w = lax.dynamic_slice_in_dim(tot[_SMALL_PAD:_SMALL_PAD + 96].reshape(CONV_K, CONV_DIM), chip * 768, 768, axis=1)
    g_meta = lax.dynamic_slice_in_dim(tot[_SMALL_PAD + 96:_SMALL_PAD + 224].reshape(N_META, D_MODEL), chip * 256, 256, axis=1)

    upd = {}
    upd["w_in"] = tuple(jnp.transpose(a) for a in (gw_in,) + _adamw(
        jnp.transpose(w_in[0]), gw_in, jnp.transpose(m_w_in[0]), jnp.transpose(v_w_in[0]), "adamw_w_in"))
    w_pr32 = jnp.concatenate([w_proj_ssd[0], w_proj_att[0], w_out[0]], axis=0)
    m_pr = jnp.concatenate([m_w_proj_ssd[0], m_w_proj_att[0], m_w_out[0]], axis=0)
    v_pr = jnp.concatenate([v_w_proj_ssd[0], v_w_proj_att[0], v_w_out[0]], axis=0)
    pr = (gw_pr,) + _adamw(w_pr32, gw_pr, m_pr, v_pr, "adamw_w_proj")
    upd["w_proj_ssd"] = tuple(a[0:512] for a in pr)
    upd["w_proj_att"] = tuple(a[512:768] for a in pr)
    upd["w_out"] = tuple(a[768:1024] for a in pr)
    upd["conv_w"] = (g_conv_w,) + _adamw(conv_w[0], g_conv_w, m_conv_w[0], v_conv_w[0], "adamw_conv_w")
    upd["meta_tokens"] = (g_meta,) + _adamw(meta_tokens, g_meta, m_meta_tokens, v_meta_tokens, "adamw_meta")
    pk = lambda np_, cb, sn, gb, npo, dtb, al, ds, fg: _pad_rows(_pack_small(np_, cb, sn, gb, npo, dtb, al, ds, fg), _SMALL_PAD)
    w_sm = pk(norm_pre, conv_b, ssd_norm, gate_bias, norm_post, dt_bias, a_log, d_skip, fgate_bias)
    m_sm = pk(m_norm_pre, m_conv_b, m_ssd_norm, m_gate_bias, m_norm_post, m_dt_bias, m_a_log, m_d_skip, m_fgate_bias)
    v_sm = pk(v_norm_pre, v_conv_b, v_ssd_norm, v_gate_bias, v_norm_post, v_dt_bias, v_a_log, v_d_skip, v_fgate_bias)
    sm = [_unpack_small(a) for a in (g_small,) + _adamw(w_sm, g_small, m_sm, v_sm, "adamw_small")]
    for name in ("norm_pre", "conv_b", "dt_bias", "a_log", "d_skip", "ssd_norm", "fgate_bias", "gate_bias", "norm_post"):
        upd[name] = tuple(s[name] for s in sm)
    lead = ("w_in", "conv_w", "w_proj_ssd", "w_proj_att", "w_out")
    order = ("meta_tokens", "norm_pre", "w_in", "conv_w", "conv_b", "dt_bias", "a_log", "d_skip", "ssd_norm",
             "fgate_bias", "gate_bias", "w_proj_ssd", "w_proj_att", "w_out", "norm_post")
    outs = [loss, grad_x]
    for part_i in range(4):
        for name in order:
            a = upd[name][part_i]
            outs.append(a[None] if name in lead else a)
    return tuple(outs)
```

```python
import functools
import math

import jax
import jax.numpy as jnp
from jax import lax
from jax.experimental import pallas as pl
from jax.experimental.pallas import tpu as pltpu

F32 = jnp.float32
BF16 = jnp.bfloat16
HIGHEST = lax.Precision.HIGHEST

D_MODEL = 1024
N_META = 16
CHUNK = 128
PADF = CHUNK - N_META
D_SSD = 2048
H_SSD = 32
G_SSD = 4
N_STATE = 128
CONV_K = 4
CONV_DIM = D_SSD + 2 * G_SSD * N_STATE
H_ATT = 16
D_ATT = 1024
EPS = 1e-6
N_COLS = 11312

C_Z, C_XBC, C_ZA, C_Q, C_K, C_V, C_G = 0, 2048, 5120, 6144, 7168, 8192, 9216
N_MAIN = 11264
N_SMALL = 128
O_Z, O_XBC, O_DT, O_ZA, O_Q, O_K, O_V, O_F, O_G = (
    (0, 2048), (2048, 3072), (5120, 32), (5152, 1024), (6176, 1024), (7200, 1024),
    (8224, 1024), (9248, 16), (9264, 2048))

ADAM_LR, ADAM_B1, ADAM_B2, ADAM_EPS, ADAM_WD, ADAM_STEP = 0.001, 0.9, 0.999, 1e-08, 0.01, 10

VMEM_LIMIT = 56 * 1024 * 1024


def _cp(*sem):
    return pltpu.CompilerParams(dimension_semantics=sem, vmem_limit_bytes=VMEM_LIMIT)


def _tile(n, prefs):
    for p in prefs:
        if n % p == 0:
            return p
    raise ValueError(f"no tile for {n} in {prefs}")


def _iota(shape, dim):
    return lax.broadcasted_iota(jnp.int32, shape, dim)


def _sigmoid(x):
    return 1.0 / (1.0 + jnp.exp(-x))


def _softplus_tail(x):
    return jnp.log(1.0 + jnp.exp(-jnp.abs(x)))


_NN = (((1,), (0,)), ((), ()))
_NT = (((1,), (1,)), ((), ()))
_TN = (((0,), (0,)), ((), ()))


def _dot(a, b, dims=_NN):
    return lax.dot_general(a, b, dims, preferred_element_type=F32)


def _dot_exact(a, b, dims=_NN):
    return lax.dot_general(a, b, dims, precision=HIGHEST, preferred_element_type=F32)


def _hosted_call(body, *, name, grid, in_specs, out_specs, out_shape, scratch_shapes, operands, semantics,
                 exchange=None, aliases=None):
    aliases = dict(aliases or {})
    if exchange is None:
        return pl.pallas_call(body, name=name, out_shape=out_shape, grid=grid, in_specs=in_specs,
                              out_specs=out_specs, scratch_shapes=scratch_shapes, input_output_aliases=aliases,
                              compiler_params=_cp(*semantics))(*operands)
    arrays, shapes, copies, n_sems, in_place = exchange
    n_in, n_out, n_ex = len(operands), len(out_shape), len(arrays)

    def hosted(*refs):
        ex_in = refs[n_in:n_in + n_ex]
        ex_out = refs[n_in + n_ex + n_out:n_in + n_ex + n_out + n_ex]
        own = refs[:n_in] + refs[n_in + n_ex:n_in + n_ex + n_out] + refs[n_in + 2 * n_ex + n_out:-2]
        first = functools.reduce(lambda p, q: p & q, [pl.program_id(d) == 0 for d in range(len(grid))])
        last = functools.reduce(lambda p, q: p & q, [pl.program_id(d) == grid[d] - 1 for d in range(len(grid))])

        def descriptors():
            return copies(ex_out if in_place else ex_in, ex_out, refs[-2], refs[-1])

        @pl.when(first)
        def _():
            for cp in descriptors():
                cp.start()

        body(*own)

        @pl.when(last)
        def _():
            for cp in descriptors():
                cp.wait()

    return pl.pallas_call(
        hosted, name=name,
        out_shape=tuple(out_shape) + tuple(shapes),
        grid=grid,
        in_specs=list(in_specs) + [_ANY] * n_ex,
        out_specs=tuple(out_specs) + (_ANY,) * n_ex,
        input_output_aliases={**aliases, **({n_in + e: n_out + e for e in range(n_ex)} if in_place else {})},
        scratch_shapes=list(scratch_shapes) + [pltpu.SemaphoreType.DMA((n_sems,)), pltpu.SemaphoreType.DMA((n_sems,))],
        compiler_params=_cp(*(("arbitrary",) * len(grid))),
    )(*operands, *arrays)


def _matmul(a, b, mode, out_dtype, name, tm, tn, tk, exchange=None):
    if mode == "tn":
        kdim, m = a.shape
    else:
        m, kdim = a.shape
    n = b.shape[0] if mode == "nt" else b.shape[1]
    nk = kdim // tk
    dims = {"nn": _NN, "nt": _NT, "tn": _TN}[mode]
    a_spec = (pl.BlockSpec((tk, tm), lambda i, j, k: (k, i)) if mode == "tn"
              else pl.BlockSpec((tm, tk), lambda i, j, k: (i, k)))
    b_spec = (pl.BlockSpec((tn, tk), lambda i, j, k: (j, k)) if mode == "nt"
              else pl.BlockSpec((tk, tn), lambda i, j, k: (k, j)))

    def body(a_ref, b_ref, o_ref, acc_ref):
        k = pl.program_id(2)
        p = _dot(a_ref[...].astype(BF16), b_ref[...].astype(BF16), dims)
        if nk == 1:
            o_ref[...] = p.astype(out_dtype)
        else:
            @pl.when(k == 0)
            def _():
                acc_ref[...] = p

            @pl.when(k > 0)
            def _():
                acc_ref[...] += p

            @pl.when(k == nk - 1)
            def _():
                o_ref[...] = acc_ref[...].astype(out_dtype)

    out = _hosted_call(
        body, name=name,
        out_shape=(jax.ShapeDtypeStruct((m, n), out_dtype),),
        grid=(m // tm, n // tn, nk),
        in_specs=[a_spec, b_spec],
        out_specs=(pl.BlockSpec((tm, tn), lambda i, j, k: (i, j)),),
        scratch_shapes=[pltpu.VMEM((tm, tn), F32)],
        operands=(a, b), semantics=("parallel", "parallel", "arbitrary"), exchange=exchange)
    return out[0] if exchange is None else out


_CAT_BLK = 1024


def _piece_ranges(pieces):
    out, off = [], 0
    for p in pieces:
        nb = p.shape[1] // _CAT_BLK
        out.append((off, nb))
        off += nb
    return out, off


def _matmul_cat_nn(pieces, b, name, tm, rows=None, fill=None, exchange=None):
    t = pieces[0].shape[0]
    n = b.shape[1]
    ranges, nk = _piece_ranges(pieces)
    first, ni = rows if rows is not None else (0, t // tm)
    n_in = len(pieces) + 1 + (fill is not None)

    def body(*refs):
        a_refs, b_ref, o_ref, acc_ref = refs[:len(pieces)], refs[len(pieces)], refs[n_in], refs[n_in + 1]
        k = pl.program_id(1)

        @pl.when(k == 0)
        def _():
            acc_ref[...] = jnp.zeros_like(acc_ref)

        for a_ref, (off, nb) in zip(a_refs, ranges):
            @pl.when((k >= off) & (k < off + nb))
            def _(a_ref=a_ref):
                acc_ref[...] += _dot(a_ref[...], b_ref[...])

        @pl.when(k == nk - 1)
        def _():
            o_ref[...] = acc_ref[...]

    def a_spec(off, nb):
        return pl.BlockSpec((tm, _CAT_BLK), lambda i, k: (first + i, jnp.clip(k - off, 0, nb - 1)))

    in_specs = [a_spec(off, nb) for off, nb in ranges] + [pl.BlockSpec((_CAT_BLK, n), lambda i, k: (k, 0))]
    operands = list(pieces) + [b]
    if fill is not None:
        in_specs.append(_ANY)
        operands.append(fill)
    out = _hosted_call(
        body, name=name,
        out_shape=(jax.ShapeDtypeStruct((t, n), F32),),
        grid=(ni, nk),
        in_specs=in_specs,
        out_specs=(pl.BlockSpec((tm, n), lambda i, k: (first + i, 0)),),
        scratch_shapes=[pltpu.VMEM((tm, n), F32)],
        operands=operands, semantics=("parallel", "arbitrary"), exchange=exchange,
        aliases={len(pieces) + 1: 0} if fill is not None else None)
    return out if exchange is not None else out[0]


def _matmul_cat_tn(pieces, b, name, tk):
    t = pieces[0].shape[0]
    n = b.shape[1]
    ranges, nm = _piece_ranges(pieces)
    nk = t // tk

    def body(*refs):
        a_refs, b_ref, o_ref, acc_ref = refs[:len(pieces)], refs[-3], refs[-2], refs[-1]
        m = pl.program_id(0)
        k = pl.program_id(1)

        @pl.when(k == 0)
        def _():
            acc_ref[...] = jnp.zeros_like(acc_ref)

        for a_ref, (off, nb) in zip(a_refs, ranges):
            @pl.when((m >= off) & (m < off + nb))
            def _(a_ref=a_ref):
                acc_ref[...] += _dot(a_ref[...], b_ref[...], _TN)

        @pl.when(k == nk - 1)
        def _():
            o_ref[...] = acc_ref[...]

    def a_spec(off, nb):
        def index(m, k):
            mine = (m >= off) & (m < off + nb)
            return jnp.where(mine, k, 0), jnp.clip(m - off, 0, nb - 1)
        return pl.BlockSpec((tk, _CAT_BLK), index)

    return pl.pallas_call(
        body, name=name,
        out_shape=jax.ShapeDtypeStruct((nm * _CAT_BLK, n), F32),
        grid=(nm, nk),
        in_specs=[a_spec(off, nb) for off, nb in ranges] + [pl.BlockSpec((tk, n), lambda m, k: (k, 0))],
        out_specs=pl.BlockSpec((_CAT_BLK, n), lambda m, k: (m, 0)),
        scratch_shapes=[pltpu.VMEM((_CAT_BLK, n), F32)],
        compiler_params=_cp("parallel", "arbitrary"),
    )(*pieces, b)


def _row_tile(t):
    return _tile(t, (528, 128))


def _row_tile_wide(t):
    return _tile(t, (352, 128))


def _norm1_fwd(h, g):
    t = h.shape[0]
    tm = _row_tile(t)

    def body(h_ref, g_ref, u_ref):
        x = h_ref[...]
        r = lax.rsqrt(jnp.mean(x * x, axis=-1, keepdims=True) + EPS)
        u_ref[...] = (x * r * g_ref[...]).astype(BF16)

    return pl.pallas_call(
        body, name="norm1_fwd",
        out_shape=jax.ShapeDtypeStruct((t, D_MODEL), BF16),
        grid=(t // tm,),
        in_specs=[pl.BlockSpec((tm, D_MODEL), lambda i: (i, 0)),
                  pl.BlockSpec((1, D_MODEL), lambda i: (0, 0))],
        out_specs=pl.BlockSpec((tm, D_MODEL), lambda i: (i, 0)),
        compiler_params=_cp("parallel"),
    )(h, g)


def _norm1_bwd(du_a, du_b, h, g, dy):
    t = h.shape[0]
    tm = _row_tile(t)

    def body(a_ref, b_ref, h_ref, g_ref, dy_ref, dh_ref, dg_ref):
        i = pl.program_id(0)
        x = h_ref[...]
        du = a_ref[...] + b_ref[...]
        r = lax.rsqrt(jnp.mean(x * x, axis=-1, keepdims=True) + EPS)
        gdu = du * g_ref[...]
        dh_ref[...] = dy_ref[...] + r * (gdu - x * (r * r) * jnp.mean(gdu * x, axis=-1, keepdims=True))
        part = jnp.sum(du * x * r, axis=0, keepdims=True)

        @pl.when(i == 0)
        def _():
            dg_ref[...] = part

        @pl.when(i > 0)
        def _():
            dg_ref[...] += part

    row = pl.BlockSpec((tm, D_MODEL), lambda i: (i, 0))
    vec = pl.BlockSpec((1, D_MODEL), lambda i: (0, 0))
    return pl.pallas_call(
        body, name="norm1_bwd",
        out_shape=(jax.ShapeDtypeStruct((t, D_MODEL), F32), jax.ShapeDtypeStruct((1, D_MODEL), F32)),
        grid=(t // tm,),
        in_specs=[row, row, row, vec, row],
        out_specs=(row, vec),
        compiler_params=_cp("arbitrary"),
    )(du_a, du_b, h, g, dy)


def _small_fwd(small, bias_row):
    t = small.shape[0]
    rt = _tile(t, (384, 128))

    def body(s_ref, b_ref, o_ref, carry_ref):
        c = pl.program_id(0)

        @pl.when(c == 0)
        def _():
            carry_ref[...] = jnp.zeros_like(carry_ref)

        x = s_ref[...] + b_ref[...]
        lane = _iota((rt, N_SMALL), 1)
        valid = (c * rt + _iota((rt, N_SMALL), 0)) >= PADF
        tail = _softplus_tail(x)
        dt = jnp.where(valid & (lane < H_SSD), jnp.maximum(x, 0.0) + tail, 0.0)
        lf = jnp.where(valid & (lane >= H_SSD) & (lane < H_SSD + H_ATT), jnp.minimum(x, 0.0) - tail, 0.0)
        tri = (_iota((rt, rt), 0) >= _iota((rt, rt), 1)).astype(F32)
        cs = _dot_exact(tri, lf) + carry_ref[...]
        carry_ref[...] = cs[rt - 1:rt, :]
        o_ref[...] = dt + cs

    return pl.pallas_call(
        body, name="small_fwd",
        out_shape=jax.ShapeDtypeStruct((t, N_SMALL), F32),
        grid=(t // rt,),
        in_specs=[pl.BlockSpec((rt, N_SMALL), lambda c: (c, 0)),
                  pl.BlockSpec((1, N_SMALL), lambda c: (0, 0))],
        out_specs=pl.BlockSpec((rt, N_SMALL), lambda c: (c, 0)),
        scratch_shapes=[pltpu.VMEM((1, N_SMALL), F32)],
        compiler_params=_cp("arbitrary"),
    )(small, bias_row)


def _small_bwd(dsm, small, bias_row):
    t = small.shape[0]
    rt = _tile(t, (384, 128))
    nc = t // rt

    def body(d_ref, s_ref, b_ref, o_ref, db_ref, carry_ref):
        step = pl.program_id(0)
        c = nc - 1 - step

        @pl.when(step == 0)
        def _():
            carry_ref[...] = jnp.zeros_like(carry_ref)
            db_ref[...] = jnp.zeros_like(db_ref)

        x = s_ref[...] + b_ref[...]
        d = d_ref[...]
        lane = _iota((rt, N_SMALL), 1)
        valid = (c * rt + _iota((rt, N_SMALL), 0)) >= PADF
        is_dt = lane < H_SSD
        is_f = (lane >= H_SSD) & (lane < H_SSD + H_ATT)
        triu = (_iota((rt, rt), 1) >= _iota((rt, rt), 0)).astype(F32)
        dc = jnp.where(is_f, d, 0.0)
        dlf = _dot_exact(triu, dc) + carry_ref[...]
        carry_ref[...] = dlf[0:1, :]
        sg = _sigmoid(x)
        out = jnp.where(valid & is_dt, d * sg, 0.0) + jnp.where(valid & is_f, dlf * (1.0 - sg), 0.0)
        o_ref[...] = out.astype(BF16)
        db_ref[...] += jnp.sum(out, axis=0, keepdims=True)

    blk = pl.BlockSpec((rt, N_SMALL), lambda s: (nc - 1 - s, 0))
    vec = pl.BlockSpec((1, N_SMALL), lambda s: (0, 0))
    return pl.pallas_call(
        body, name="small_bwd",
        out_shape=(jax.ShapeDtypeStruct((t, N_SMALL), BF16), jax.ShapeDtypeStruct((1, N_SMALL), F32)),
        grid=(nc,),
        in_specs=[blk, blk, vec],
        out_specs=(blk, vec),
        scratch_shapes=[pltpu.VMEM((1, N_SMALL), F32)],
        compiler_params=_cp("arbitrary"),
    )(dsm, small, bias_row)


_CONV_TC = 1024
_XBC_BLK = C_XBC // _CONV_TC


def _shift_down(cur, prev8, j):
    rc = pltpu.roll(cur, j, 0)
    rid = _iota(prev8.shape, 0)
    top = jnp.where(rid < j, pltpu.roll(prev8, j, 0), rc[0:8, :])
    return top if cur.shape[0] == 8 else jnp.concatenate([top, rc[8:, :]], axis=0)


def _shift_up(cur, next8, j):
    n = cur.shape[0]
    ru = pltpu.roll(cur, n - j, 0)
    rid = _iota(next8.shape, 0)
    bot = jnp.where(rid >= 8 - j, pltpu.roll(next8, 8 - j, 0), ru[n - 8:, :])
    return jnp.concatenate([ru[:n - 8, :], bot], axis=0)


def _conv_taps(cur, prev, w, b):
    taps = [cur] + [_shift_down(cur, prev, j) for j in (1, 2, 3)]
    acc = b + taps[0] * w[3:4, :]
    for j in (1, 2, 3):
        acc = acc + taps[j] * w[3 - j:4 - j, :]
    return acc, taps


def _conv_pre(x_ref, p_ref, w_ref, b_ref, i):
    return _conv_taps(x_ref[...], jnp.where(i > 0, p_ref[...], 0.0), w_ref[...], b_ref[...])


def _dsilu(d, acc):
    sg = _sigmoid(acc)
    return d * sg * (1.0 + acc * (1.0 - sg))


def _conv_fwd(proj, conv_w, conv_b):
    t = proj.shape[0]
    tr = _row_tile(t)

    def body(x_ref, p_ref, w_ref, b_ref, o_ref):
        i = pl.program_id(0)
        acc, _ = _conv_pre(x_ref, p_ref, w_ref, b_ref, i)
        valid = (i * tr + _iota(acc.shape, 0)) >= PADF
        o_ref[...] = jnp.where(valid, acc * _sigmoid(acc), 0.0)

    return pl.pallas_call(
        body, name="conv_fwd",
        out_shape=jax.ShapeDtypeStruct((t, CONV_DIM), F32),
        grid=(t // tr, CONV_DIM // _CONV_TC),
        in_specs=[pl.BlockSpec((tr, _CONV_TC), lambda i, j: (i, _XBC_BLK + j)),
                  pl.BlockSpec((8, _CONV_TC), lambda i, j: (jnp.maximum(i * (tr // 8) - 1, 0), _XBC_BLK + j)),
                  pl.BlockSpec((CONV_K, _CONV_TC), lambda i, j: (0, j)),
                  pl.BlockSpec((1, _CONV_TC), lambda i, j: (0, j))],
        out_specs=pl.BlockSpec((tr, _CONV_TC), lambda i, j: (i, j)),
        compiler_params=_cp("parallel", "parallel"),
    )(proj, proj, conv_w, conv_b)


def _conv_bwd(dxbc, proj, conv_w, conv_b):
    t = proj.shape[0]
    tr = _row_tile(t)
    n_tiles = t // tr
    last8 = t // 8 - 1

    def body(d_ref, dn_ref, x_ref, p_ref, xn_ref, w_ref, b_ref, dx_ref, dw_ref, db_ref):
        i = pl.program_id(1)
        w = w_ref[...]
        b = b_ref[...]
        cur = x_ref[...]
        acc, taps = _conv_taps(cur, jnp.where(i > 0, p_ref[...], 0.0), w, b)
        valid = (i * tr + _iota(acc.shape, 0)) >= PADF
        da = jnp.where(valid, _dsilu(d_ref[...], acc), 0.0)
        acc_n, _ = _conv_taps(xn_ref[...], cur[tr - 8:, :], w, b)
        da_n = jnp.where(i < n_tiles - 1, _dsilu(dn_ref[...], acc_n), 0.0)
        dx = da * w[3:4, :]
        for j in (1, 2, 3):
            dx = dx + _shift_up(da, da_n, j) * w[3 - j:4 - j, :]
        dx_ref[...] = dx.astype(BF16)
        dw = jnp.concatenate([jnp.sum(da * taps[3 - k], axis=0, keepdims=True) for k in range(CONV_K)], axis=0)
        db = jnp.sum(da, axis=0, keepdims=True)

        @pl.when(i == 0)
        def _():
            dw_ref[...] = dw
            db_ref[...] = db

        @pl.when(i > 0)
        def _():
            dw_ref[...] += dw
            db_ref[...] += db

    nxt8 = lambda i: jnp.minimum((i + 1) * (tr // 8), last8)
    return pl.pallas_call(
        body, name="conv_bwd",
        out_shape=(jax.ShapeDtypeStruct((t, CONV_DIM), BF16),
                   jax.ShapeDtypeStruct((CONV_K, CONV_DIM), F32),
                   jax.ShapeDtypeStruct((1, CONV_DIM), F32)),
        grid=(CONV_DIM // _CONV_TC, n_tiles),
        in_specs=[pl.BlockSpec((tr, _CONV_TC), lambda j, i: (i, j)),
                  pl.BlockSpec((8, _CONV_TC), lambda j, i: (nxt8(i), j)),
                  pl.BlockSpec((tr, _CONV_TC), lambda j, i: (i, _XBC_BLK + j)),
                  pl.BlockSpec((8, _CONV_TC), lambda j, i: (jnp.maximum(i * (tr // 8) - 1, 0), _XBC_BLK + j)),
                  pl.BlockSpec((8, _CONV_TC), lambda j, i: (nxt8(i), _XBC_BLK + j)),
                  pl.BlockSpec((CONV_K, _CONV_TC), lambda j, i: (0, j)),
                  pl.BlockSpec((1, _CONV_TC), lambda j, i: (0, j))],
        out_specs=(pl.BlockSpec((tr, _CONV_TC), lambda j, i: (i, j)),
                   pl.BlockSpec((CONV_K, _CONV_TC), lambda j, i: (0, j)),
                   pl.BlockSpec((1, _CONV_TC), lambda j, i: (0, j))),
        compiler_params=_cp("parallel", "arbitrary"),
    )(dxbc, dxbc, proj, proj, proj, conv_w, conv_b)


_GW = D_SSD // G_SSD


def _ssd_prelude(dt_ref, a_ref, e_scr, es_scr, dte_scr):
    r0 = _iota((CHUNK, CHUNK), 0)
    r1 = _iota((CHUNK, CHUNK), 1)
    dt = jnp.where(r1 < H_SSD, dt_ref[...], 0.0)
    adt = dt * a_ref[...]
    acs = _dot_exact((r0 >= r1).astype(F32), adt)
    acs_t = acs.T
    alast = acs[CHUNK - 1:CHUNK, :]
    exp_a = jnp.exp(acs)
    dec_s = jnp.exp(alast - acs)
    lo = r1 < 64
    for j in range(H_SSD // 2):
        sl = slice(CHUNK * j, CHUNK * (j + 1))
        e_scr[:, sl] = jnp.where(lo, exp_a[:, 2 * j:2 * j + 1], exp_a[:, 2 * j + 1:2 * j + 2])
        es_scr[:, sl] = jnp.where(lo, dec_s[:, 2 * j:2 * j + 1], dec_s[:, 2 * j + 1:2 * j + 2])
        dte_scr[:, sl] = jnp.where(lo, dt[:, 2 * j:2 * j + 1], dt[:, 2 * j + 1:2 * j + 2])
    return dt, acs, acs_t, r0, r1, lo


def _chunk_decay_rows(acs_t, g):
    cd_t = jnp.exp(acs_t[:, CHUNK - 1:CHUNK])
    return jnp.concatenate(
        [jnp.broadcast_to(cd_t[8 * g + hh:8 * g + hh + 1, :], (64, N_STATE)) for hh in range(8)], axis=0)


def _ssd_fwd(xbc, dtlf, a_row, dsk_row):
    t = xbc.shape[0]
    nc = t // CHUNK

    def body(xs_ref, b_ref, c_ref, dt_ref, a_ref, dsk_ref, y_ref, hin_ref, h_scr, e_scr, es_scr, dte_scr):
        c = pl.program_id(0)

        @pl.when(c == 0)
        def _():
            h_scr[...] = jnp.zeros_like(h_scr)

        dt, acs, acs_t, r0, r1, lo = _ssd_prelude(dt_ref, a_ref, e_scr, es_scr, dte_scr)
        causal = r0 >= r1
        for g in range(G_SSD):
            gs = slice(_GW * g, _GW * (g + 1))
            bg = b_ref[:, N_STATE * g:N_STATE * (g + 1)].astype(BF16)
            cg = c_ref[:, N_STATE * g:N_STATE * (g + 1)].astype(BF16)
            cb = _dot(cg, bg, _NT)
            hg = h_scr[gs, :]
            hin_ref[0, gs, :] = hg
            xg = xs_ref[:, gs] * dte_scr[:, gs]
            yoff = _dot(cg, hg.astype(BF16), _NT) * e_scr[:, gs]
            st = _dot((xg * es_scr[:, gs]).astype(BF16), bg, _TN)
            h_scr[gs, :] = hg * _chunk_decay_rows(acs_t, g) + st
            for jj in range(4):
                j = 4 * g + jj
                sl = slice(CHUNK * j, CHUNK * (j + 1))
                xp = xg[:, CHUNK * jj:CHUNK * (jj + 1)]
                acc = yoff[:, CHUNK * jj:CHUNK * (jj + 1)] + dsk_ref[:, sl] * xs_ref[:, sl]
                for hh in range(2):
                    h = 2 * j + hh
                    seg = acs[:, h:h + 1] - acs_t[h:h + 1, :]
                    lm = jnp.exp(jnp.where(causal, seg, -1e30))
                    m = (cb * lm).astype(BF16)
                    xh = jnp.where(lo if hh == 0 else ~lo, xp, 0.0).astype(BF16)
                    acc = acc + _dot(m, xh)
                y_ref[:, sl] = acc

    return pl.pallas_call(
        body, name="ssd_fwd",
        out_shape=(jax.ShapeDtypeStruct((t, D_SSD), F32), jax.ShapeDtypeStruct((nc, D_SSD, N_STATE), F32)),
        grid=(nc,),
        in_specs=[pl.BlockSpec((CHUNK, D_SSD), lambda c: (c, 0)),
                  pl.BlockSpec((CHUNK, _GW), lambda c: (c, 4)),
                  pl.BlockSpec((CHUNK, _GW), lambda c: (c, 5)),
                  pl.BlockSpec((CHUNK, N_SMALL), lambda c: (c, 0)),
                  pl.BlockSpec((1, N_SMALL), lambda c: (0, 0)),
                  pl.BlockSpec((1, D_SSD), lambda c: (0, 0))],
        out_specs=(pl.BlockSpec((CHUNK, D_SSD), lambda c: (c, 0)),
                   pl.BlockSpec((1, D_SSD, N_STATE), lambda c: (c, 0, 0))),
        scratch_shapes=[pltpu.VMEM((D_SSD, N_STATE), F32)] + [pltpu.VMEM((CHUNK, D_SSD), F32)] * 3,
        compiler_params=_cp("arbitrary"),
    )(xbc, xbc, xbc, dtlf, a_row, dsk_row)


def _ssd_bwd(xbc, dtlf, a_row, dsk_row, hin, dy):
    t = xbc.shape[0]
    nc = t // CHUNK

    def body(xs_ref, b_ref, c_ref, dt_ref, a_ref, dsk_ref, hin_ref, dy_ref,
             dxbc_ref, ddt_ref, da_ref, ddsk_ref, dh_scr, e_scr, es_scr, dte_scr, dx_scr, whi_scr, wlo_scr):
        step = pl.program_id(0)

        @pl.when(step == 0)
        def _():
            dh_scr[...] = jnp.zeros_like(dh_scr)
            da_ref[...] = jnp.zeros_like(da_ref)
            ddsk_ref[...] = jnp.zeros_like(ddsk_ref)

        dt, acs, acs_t, r0, r1, lo = _ssd_prelude(dt_ref, a_ref, e_scr, es_scr, dte_scr)
        causal = r0 >= r1
        lane_row = _iota((1, CHUNK), 1)
        dacs = jnp.zeros((CHUNK, CHUNK), F32)
        dacs_t = jnp.zeros((CHUNK, CHUNK), F32)
        dalast = jnp.zeros((1, CHUNK), F32)
        ddt_dir = jnp.zeros((CHUNK, CHUNK), F32)
        ddsk_ref[...] += jnp.sum(dy_ref[...] * xs_ref[...], axis=0, keepdims=True)

        def head_sums(z, pick):
            hi = z.astype(BF16)
            return _dot(hi, pick) + _dot((z - hi.astype(F32)).astype(BF16), pick)

        for g in range(G_SSD):
            gs = slice(_GW * g, _GW * (g + 1))
            pick = (jnp.right_shift(_iota((_GW, CHUNK), 0), 6) + 8 * g == _iota((_GW, CHUNK), 1)).astype(BF16)
            bg = b_ref[:, N_STATE * g:N_STATE * (g + 1)].astype(BF16)
            cg = c_ref[:, N_STATE * g:N_STATE * (g + 1)].astype(BF16)
            cb = _dot(cg, bg, _NT)
            hg = hin_ref[0, gs, :]
            hgb = hg.astype(BF16)
            dhn = dh_scr[gs, :]
            dhnb = dhn.astype(BF16)
            esg = es_scr[:, gs]
            dyg = dy_ref[:, gs]
            xsg = xs_ref[:, gs]
            xg = xsg * dte_scr[:, gs]
            dyeb = (dyg * e_scr[:, gs]).astype(BF16)
            dc = _dot(dyeb, hgb)
            dh_y = _dot(dyeb, cg, _TN)
            dxs = _dot(bg, dhnb, _NT) * esg
            db = _dot((xg * esg).astype(BF16), dhnb)
            cd = _chunk_decay_rows(acs_t, g)
            dh_scr[gs, :] = dhn * cd + dh_y
            end_state = head_sums(jnp.broadcast_to(jnp.sum(xg * dxs, axis=0, keepdims=True), (8, _GW)), pick)[0:1, :]
            carried = dhn * hg * cd
            per_head = jnp.concatenate([jnp.sum(carried[64 * hh:64 * hh + 64, :], axis=0, keepdims=True)
                                        for hh in range(8)], axis=0)
            per_head = jnp.sum(per_head, axis=1, keepdims=True)
            for hh in range(8):
                end_state = end_state + jnp.where(lane_row == 8 * g + hh, per_head[hh:hh + 1, :], 0.0)
            dalast = dalast + end_state
            dcb = jnp.zeros((CHUNK, CHUNK), F32)
            for jj in range(4):
                j = 4 * g + jj
                sl = slice(CHUNK * j, CHUNK * (j + 1))
                ps = slice(CHUNK * jj, CHUNK * (jj + 1))
                xpb = xg[:, ps].astype(BF16)
                dyp = dyg[:, ps]
                dxp = dxs[:, ps]
                for hh in range(2):
                    h = 2 * j + hh
                    ws = slice(CHUNK * (2 * jj + hh), CHUNK * (2 * jj + hh + 1))
                    seg = acs[:, h:h + 1] - acs_t[h:h + 1, :]
                    lm = jnp.exp(jnp.where(causal, seg, -1e30))
                    mf = cb * lm
                    dyh = jnp.where(lo if hh == 0 else ~lo, dyp, 0.0).astype(BF16)
                    gm = _dot(dyh, xpb, _NT)
                    dcb = dcb + gm * lm
                    w = gm * mf
                    whi = w.astype(BF16)
                    whi_scr[:, ws] = whi
                    wlo_scr[:, ws] = (w - whi.astype(F32)).astype(BF16)
                    dacs_t = dacs_t - jnp.where(r0 == h, jnp.sum(w, axis=0, keepdims=True), 0.0)
                    dxp = dxp + _dot(mf.astype(BF16), dyh, _TN)
                dx_scr[:, sl] = dxp
            dxg = dx_scr[:, gs]
            pick_w = (jnp.right_shift(_iota((8 * CHUNK, CHUNK), 0), 7) + 8 * g == _iota((8 * CHUNK, CHUNK), 1)).astype(BF16)
            ch = _dot(cg, hgb, _NT)
            dacs = (dacs + _dot(whi_scr[...], pick_w) + _dot(wlo_scr[...], pick_w)
                    + head_sums(dyg * e_scr[:, gs] * ch - xg * dxs, pick))
            ddt_dir = ddt_dir + head_sums(dxg * xsg, pick)
            dcbb = dcb.astype(BF16)
            dxbc_ref[:, D_SSD + N_STATE * g:D_SSD + N_STATE * (g + 1)] = db + _dot(dcbb, cg, _TN)
            dxbc_ref[:, D_SSD + _GW + N_STATE * g:D_SSD + _GW + N_STATE * (g + 1)] = dc + _dot(dcbb, bg)
        dxbc_ref[:, 0:D_SSD] = dx_scr[...] * dte_scr[...] + dsk_ref[...] * dy_ref[...]
        dacs = dacs + dacs_t.T + jnp.where(r0 == CHUNK - 1, dalast, 0.0)
        dadt = _dot_exact((r1 >= r0).astype(F32), dacs)
        ddt_ref[...] = dadt * a_ref[...] + ddt_dir
        da_ref[...] += jnp.sum(dadt * dt, axis=0, keepdims=True)

    rev = lambda s: (nc - 1 - s, 0)
    return pl.pallas_call(
        body, name="ssd_bwd",
        out_shape=(jax.ShapeDtypeStruct((t, CONV_DIM), F32), jax.ShapeDtypeStruct((t, N_SMALL), F32),
                   jax.ShapeDtypeStruct((1, N_SMALL), F32), jax.ShapeDtypeStruct((1, D_SSD), F32)),
        grid=(nc,),
        in_specs=[pl.BlockSpec((CHUNK, D_SSD), rev),
                  pl.BlockSpec((CHUNK, _GW), lambda s: (nc - 1 - s, 4)),
                  pl.BlockSpec((CHUNK, _GW), lambda s: (nc - 1 - s, 5)),
                  pl.BlockSpec((CHUNK, N_SMALL), rev),
                  pl.BlockSpec((1, N_SMALL), lambda s: (0, 0)),
                  pl.BlockSpec((1, D_SSD), lambda s: (0, 0)),
                  pl.BlockSpec((1, D_SSD, N_STATE), lambda s: (nc - 1 - s, 0, 0)),
                  pl.BlockSpec((CHUNK, D_SSD), rev)],
        out_specs=(pl.BlockSpec((CHUNK, CONV_DIM), rev),
                   pl.BlockSpec((CHUNK, N_SMALL), rev),
                   pl.BlockSpec((1, N_SMALL), lambda s: (0, 0)),
                   pl.BlockSpec((1, D_SSD), lambda s: (0, 0))),
        scratch_shapes=([pltpu.VMEM((D_SSD, N_STATE), F32)] + [pltpu.VMEM((CHUNK, D_SSD), F32)] * 4
                        + [pltpu.VMEM((CHUNK, 8 * CHUNK), BF16)] * 2),
        compiler_params=_cp("arbitrary"),
    )(xbc, xbc, xbc, dtlf, a_row, dsk_row, hin, dy)


_NPAIR = H_ATT // 2
_QB, _KB, _VB = C_Q // 128, C_K // 128, C_V // 128
_SCALE = 1.0 / math.sqrt(64.0)
_LOG2E = math.log2(math.e)


def _attn_blocks(t):
    return _tile(t, (1408, 384, 256, 128)), _tile(t, (384, 128))


def _split3(c):
    hi = c.astype(BF16).astype(F32)
    rest = c - hi
    mid = rest.astype(BF16).astype(F32)
    return hi, mid, rest - mid


def _head_lanes(lane, hh):
    return (lane < 64, 64) if hh == 0 else (lane >= 64, 0)


def _q_operand(q, cq, lane, hh):
    sel, first = _head_lanes(lane, hh)
    out = jnp.where(sel, q, 0.0)
    for n, col in enumerate(_split3(cq) + (1.0, 1.0, 1.0)):
        out = jnp.where(lane == first + n, col, out)
    return out.astype(BF16)


def _k_operand(k, ck, lane, hh):
    sel, first = _head_lanes(lane, hh)
    hi, mid, lo = _split3(ck)
    out = jnp.where(sel, k, 0.0)
    for n, col in enumerate((1.0, 1.0, 1.0, -hi, -mid, -lo)):
        out = jnp.where(lane == first + n, col, out)
    return out.astype(BF16)


def _sum_operand(x, lane, hh, at):
    sel, first = _head_lanes(lane, hh)
    return jnp.where(sel, x, jnp.where(lane == first + at, 1.0, 0.0)).astype(BF16)


_C_FILLER = 2.0 ** 30
_SKIP_STEP = 256


def _query_skips(bq):
    firsts = list(range(0, bq, _SKIP_STEP))
    far = 1 << 30
    return [(q0 if n else -far, firsts[n + 1] if n + 1 < len(firsts) else far, q0) for n, q0 in enumerate(firsts)]


def _attn_fwd(proj, c_col):
    t = proj.shape[0]
    bq, bk = _attn_blocks(t)
    nq, nk = t // bq, t // bk
    rs = 32

    def last_kv(i):
        return (i * bq + bq - 1) // bk

    def body(q_ref, k_ref, v_ref, cq_ref, ck_ref, o_ref, lse_ref, p_ref, mrun_ref, qs_scr, s_scr, m_scr, acc_scr):
        i = pl.program_id(1)
        kk = pl.program_id(2)
        lane_q = _iota((bq, 128), 1)

        @pl.when(kk == 0)
        def _():
            m_scr[...] = jnp.full_like(m_scr, -1e30)
            acc_scr[...] = jnp.zeros_like(acc_scr)
            q = q_ref[...] * (_SCALE * _LOG2E)
            cq = cq_ref[0] * _LOG2E
            for hh in range(2):
                qs_scr[hh] = _q_operand(q, cq[:, hh:hh + 1], lane_q, hh)

        def step(masked, q0):
            nqc = bq - q0
            lane_k = _iota((bk, 128), 1)
            k = k_ref[...]
            v = v_ref[...]
            ck = ck_ref[0] * _LOG2E
            ahead = _iota((rs, nqc), 0) - _iota((rs, nqc), 1) - q0
            vss = []
            for hh in range(2):
                sel, first = _head_lanes(lane_k, hh)
                ks = _k_operand(k, ck[:, hh:hh + 1], lane_k, hh)
                vss.append(jnp.where(sel, v, jnp.where(lane_k == first, 1.0, 0.0)).astype(BF16))
                s_scr[hh, :, q0:] = _dot(ks, qs_scr[hh, q0:, :], _NT)
            for hh in range(2):
                vs = vss[hh]

                def block_max(r, mx):
                    rows = pl.ds(pl.multiple_of(r * rs, rs), rs)
                    s = s_scr[hh, rows, q0:]
                    if masked:
                        s = jnp.where(ahead <= i * bq - kk * bk - r * rs, s, -1e30)
                        s_scr[hh, rows, q0:] = s
                    return jnp.maximum(mx, s)

                mx = lax.fori_loop(0, bk // rs, block_max, jnp.full((rs, nqc), -1e30, F32), unroll=True)
                m_old = m_scr[hh, :, q0:]
                m_new = jnp.maximum(m_old, jnp.max(mx, axis=0, keepdims=True))
                m_scr[hh, :, q0:] = m_new
                mrun_ref[0, hh:hh + 1, q0:] = m_new

                def probs(r, carry):
                    rows = pl.ds(pl.multiple_of(r * rs, rs), rs)
                    p_ref[0, hh, rows, q0:] = jnp.exp2(s_scr[hh, rows, q0:] - m_new).astype(BF16)
                    return carry

                lax.fori_loop(0, bk // rs, probs, 0, unroll=True)
                acc_scr[hh, :, q0:] = (acc_scr[hh, :, q0:] * jnp.exp2(m_old - m_new)
                                       + _dot(vs, p_ref[0, hh, :, q0:], _TN))

        active = kk <= last_kv(i)
        ahead_by = kk * bk - i * bq
        for lo, hi, q0 in _query_skips(bq):
            @pl.when(active & (ahead_by + bk - 1 > 0) & (ahead_by >= lo) & (ahead_by < hi))
            def _(q0=q0):
                step(True, q0)

        @pl.when(active & jnp.logical_not(ahead_by + bk - 1 > 0))
        def _():
            step(False, 0)

        @pl.when(kk == nk - 1)
        def _():
            a = acc_scr[0]
            b = acc_scr[1]
            la = a[64:65, :]
            lb = b[0:1, :]
            o_ref[...] = jnp.where(lane_q < 64, (a / la).T, (b / lb).T)
            lse_ref[0] = jnp.concatenate([m_scr[0] + jnp.log(la) * _LOG2E, m_scr[1] + jnp.log(lb) * _LOG2E], axis=0)

    kvi = lambda i, kk: jnp.minimum(kk, last_kv(i))
    kv = lambda off: pl.BlockSpec((bk, 128), lambda j, i, kk: (kvi(i, kk), off + j))
    blk = lambda j, i, kk: (j * nq + i) * nk + kvi(i, kk)
    return pl.pallas_call(
        body, name="attn_fwd",
        out_shape=(jax.ShapeDtypeStruct((t, D_ATT), F32), jax.ShapeDtypeStruct((_NPAIR, 2, t), F32),
                   jax.ShapeDtypeStruct((_NPAIR * nq * nk, 2, bk, bq), BF16),
                   jax.ShapeDtypeStruct((_NPAIR * nq * nk, 2, bq), F32)),
        grid=(_NPAIR, nq, nk),
        in_specs=[pl.BlockSpec((bq, 128), lambda j, i, kk: (i, _QB + j)),
                  kv(_KB), kv(_VB),
                  pl.BlockSpec((1, bq, 2), lambda j, i, kk: (j, i, 0)),
                  pl.BlockSpec((1, bk, 2), lambda j, i, kk: (j, kvi(i, kk), 0))],
        out_specs=(pl.BlockSpec((bq, 128), lambda j, i, kk: (i, j)),
                   pl.BlockSpec((1, 2, bq), lambda j, i, kk: (j, 0, i)),
                   pl.BlockSpec((1, 2, bk, bq), lambda j, i, kk: (blk(j, i, kk), 0, 0, 0)),
                   pl.BlockSpec((1, 2, bq), lambda j, i, kk: (blk(j, i, kk), 0, 0))),
        scratch_shapes=[pltpu.VMEM((2, bq, 128), BF16), pltpu.VMEM((2, bk, bq), F32),
                        pltpu.VMEM((2, 1, bq), F32), pltpu.VMEM((2, 128, bq), F32)],
        compiler_params=_cp("parallel", "parallel", "arbitrary"),
    )(proj, proj, proj, c_col, c_col)


def _attn_bwd(proj, lse_row, dl_row, do, p_blocks, m_run, exchange=None):
    t = proj.shape[0]
    bq, bk = _attn_blocks(t)
    nq, nk = t // bq, t // bk
    rs = 16

    def first_q(kk):
        return (kk * bk) // bq

    def body(q_ref, k_ref, v_ref, lse_ref, dl_ref, do_ref, pblk_ref, mrun_ref,
             dq_ref, dk_ref, dv_ref, dck_ref, dcq_ref,
             qs_scr, doh_scr, ks_scr, dp_scr, p_scr, ds_scr, dq_scr, dk_scr, dv_scr):
        kk = pl.program_id(1)
        i = pl.program_id(2)
        lane_q = _iota((bq, 128), 1)
        lane_k = _iota((bk, 128), 1)
        qrows = pl.ds(pl.multiple_of(i * bq, 128), bq)

        @pl.when(kk == 0)
        def _():
            q = q_ref[...] * _SCALE
            do_ = do_ref[...]
            for hh in range(2):
                qs_scr[hh, qrows, :] = _sum_operand(q, lane_q, hh, 3)
                doh_scr[hh, qrows, :] = jnp.where(_head_lanes(lane_q, hh)[0], do_, 0.0).astype(BF16)
                dq_scr[hh, i] = jnp.zeros((128, bq), F32)

        @pl.when(i == 0)
        def _():
            dk_scr[...] = jnp.zeros_like(dk_scr)
            dv_scr[...] = jnp.zeros_like(dv_scr)
            k = k_ref[...]
            for hh in range(2):
                ks_scr[hh] = _sum_operand(k, lane_k, hh, 0)

        def step(q0):
            seen = pl.ds(pl.multiple_of(i * bq + q0, 128), bq - q0)
            v16 = v_ref[...].astype(BF16)
            dl = dl_ref[0, :, q0:]
            rescale = jnp.exp2(mrun_ref[0, :, q0:] - lse_ref[0, :, q0:])
            for hh in range(2):
                dp_scr[hh, :, q0:] = _dot(v16, doh_scr[hh, seen, :], _NT)
            for hh in range(2):
                qs = qs_scr[hh, seen, :]
                doh = doh_scr[hh, seen, :]

                def strip(r, carry):
                    rows = pl.ds(pl.multiple_of(r * rs, rs), rs)
                    p = pblk_ref[0, hh, rows, q0:].astype(F32) * rescale[hh:hh + 1, :]
                    p_scr[hh, rows, q0:] = p.astype(BF16)
                    ds_scr[hh, rows, q0:] = (p * (dp_scr[hh, rows, q0:] - dl[hh:hh + 1, :])).astype(BF16)
                    return carry

                lax.fori_loop(0, bk // rs, strip, 0, unroll=True)
                dv_scr[...] += _dot(p_scr[hh, :, q0:], doh)
                dk_scr[hh] += _dot(ds_scr[hh, :, q0:], qs)
                dq_scr[hh, i, :, q0:] += _dot(ks_scr[hh], ds_scr[hh, :, q0:], _TN)

        ahead_by = kk * bk - i * bq
        for lo, hi, q0 in _query_skips(bq):
            @pl.when((i >= first_q(kk)) & (ahead_by >= lo) & (ahead_by < hi))
            def _(q0=q0):
                step(q0)

        @pl.when(i == nq - 1)
        def _():
            dka = dk_scr[0]
            dkb = dk_scr[1]
            dk_ref[...] = jnp.where(lane_k < 64, dka, dkb).astype(BF16)
            dv_ref[...] = dv_scr[...].astype(BF16)
            dck_ref[0] = -jnp.where(_iota((bk, 2), 1) == 0, dka[:, 67:68], dkb[:, 3:4])

        @pl.when((kk == nk - 1) & (i == nq - 1))
        def _():
            for ii in range(nq):
                cols = slice(ii * bq, (ii + 1) * bq)
                dqa = dq_scr[0, ii]
                dqb = dq_scr[1, ii]
                dq_ref[cols, :] = (jnp.where(lane_q < 64, dqa.T, dqb.T) * _SCALE).astype(BF16)
                dcq_ref[0, :, cols] = jnp.concatenate([dqa[64:65, :], dqb[0:1, :]], axis=0)

    qi = lambda kk, i: jnp.where(kk == 0, i, nq - 1)
    qspec = lambda off: pl.BlockSpec((bq, 128), lambda j, kk, i: (qi(kk, i), off + j))
    kspec = lambda off: pl.BlockSpec((bk, 128), lambda j, kk, i: (kk, off + j))
    rowspec = pl.BlockSpec((1, 2, bq), lambda j, kk, i: (j, 0, jnp.maximum(i, first_q(kk))))
    blk = lambda j, kk, i: (j * nq + jnp.maximum(i, first_q(kk))) * nk + kk
    return _hosted_call(
        body, name="attn_bwd",
        out_shape=(jax.ShapeDtypeStruct((t, D_ATT), BF16), jax.ShapeDtypeStruct((t, D_ATT), BF16),
                   jax.ShapeDtypeStruct((t, D_ATT), BF16), jax.ShapeDtypeStruct((_NPAIR, t, 2), F32),
                   jax.ShapeDtypeStruct((_NPAIR, 2, t), F32)),
        grid=(_NPAIR, nk, nq),
        in_specs=[qspec(_QB), kspec(_KB), kspec(_VB),
                  rowspec, rowspec, qspec(0),
                  pl.BlockSpec((1, 2, bk, bq), lambda j, kk, i: (blk(j, kk, i), 0, 0, 0)),
                  pl.BlockSpec((1, 2, bq), lambda j, kk, i: (blk(j, kk, i), 0, 0))],
        out_specs=(pl.BlockSpec((t, 128), lambda j, kk, i: (0, j)),
                   pl.BlockSpec((bk, 128), lambda j, kk, i: (kk, j)),
                   pl.BlockSpec((bk, 128), lambda j, kk, i: (kk, j)),
                   pl.BlockSpec((1, bk, 2), lambda j, kk, i: (j, kk, 0)),
                   pl.BlockSpec((1, 2, t), lambda j, kk, i: (j, 0, 0))),
        scratch_shapes=[pltpu.VMEM((2, t, 128), BF16), pltpu.VMEM((2, t, 128), BF16), pltpu.VMEM((2, bk, 128), BF16),
                        pltpu.VMEM((2, bk, bq), F32),
                        pltpu.VMEM((2, bk, bq), BF16), pltpu.VMEM((2, bk, bq), BF16),
                        pltpu.VMEM((2, nq, 128, bq), F32), pltpu.VMEM((2, bk, 128), F32), pltpu.VMEM((bk, 128), F32)],
        operands=(proj, proj, proj, lse_row, dl_row, do, p_blocks, m_run),
        semantics=("parallel", "arbitrary", "arbitrary"), exchange=exchange)


def _premerge_fwd(y, o, proj, gamma):
    t = y.shape[0]
    tm = _row_tile_wide(t)

    def body(y_ref, z_ref, o_ref, za_ref, g_ref, ys_ref, ya_ref):
        z = z_ref[...]
        u = y_ref[...] * (z * _sigmoid(z))
        for g in range(G_SSD):
            gs = slice(_GW * g, _GW * (g + 1))
            ug = u[:, gs]
            r = lax.rsqrt(jnp.mean(ug * ug, axis=-1, keepdims=True) + EPS)
            ys_ref[:, gs] = (ug * r * g_ref[:, gs]).astype(BF16)
        za = za_ref[...]
        ya_ref[...] = (o_ref[...] * (za * _sigmoid(za))).astype(BF16)

    return pl.pallas_call(
        body, name="premerge_fwd",
        out_shape=(jax.ShapeDtypeStruct((t, D_SSD), BF16), jax.ShapeDtypeStruct((t, D_ATT), BF16)),
        grid=(t // tm,),
        in_specs=[pl.BlockSpec((tm, D_SSD), lambda i: (i, 0)),
                  pl.BlockSpec((tm, D_SSD), lambda i: (i, C_Z // D_SSD)),
                  pl.BlockSpec((tm, D_ATT), lambda i: (i, 0)),
                  pl.BlockSpec((tm, D_ATT), lambda i: (i, C_ZA // D_ATT)),
                  pl.BlockSpec((1, D_SSD), lambda i: (0, 0))],
        out_specs=(pl.BlockSpec((tm, D_SSD), lambda i: (i, 0)), pl.BlockSpec((tm, D_ATT), lambda i: (i, 0))),
        compiler_params=_cp("parallel"),
    )(y, proj, o, proj, gamma)


def _premerge_bwd(dys, dya, y, o, proj, gamma, exchange=None):
    t = y.shape[0]
    tm = _row_tile_wide(t)

    def body(dys_ref, dya_ref, y_ref, z_ref, o_ref, za_ref, g_ref, dy_ref, dz_ref, do_ref, dza_ref, dg_ref, dl_ref):
        i = pl.program_id(0)
        z = z_ref[...]
        sz = _sigmoid(z)
        silu = z * sz
        dsilu = sz * (1.0 + z * (1.0 - sz))
        yv = y_ref[...]
        u = yv * silu
        parts = []
        for g in range(G_SSD):
            gs = slice(_GW * g, _GW * (g + 1))
            ug = u[:, gs]
            r = lax.rsqrt(jnp.mean(ug * ug, axis=-1, keepdims=True) + EPS)
            n = ug * r
            dout = dys_ref[:, gs]
            dn = dout * g_ref[:, gs]
            du = r * (dn - n * jnp.mean(dn * n, axis=-1, keepdims=True))
            dy_ref[:, gs] = du * silu[:, gs]
            dz_ref[:, gs] = (du * yv[:, gs] * dsilu[:, gs]).astype(BF16)
            parts.append(jnp.sum(dout * n, axis=0, keepdims=True))
        dg = jnp.concatenate(parts, axis=1)
        za = za_ref[...]
        sa = _sigmoid(za)
        dya_ = dya_ref[...]
        ov = o_ref[...]
        do = dya_ * (za * sa)
        do_ref[...] = do
        dza_ref[...] = (dya_ * ov * (sa * (1.0 + za * (1.0 - sa)))).astype(BF16)
        pick = (jnp.right_shift(_iota((D_ATT, 128), 0), 6) == _iota((D_ATT, 128), 1)).astype(F32)
        dl_ref[...] = _dot_exact(do * ov, pick)

        @pl.when(i == 0)
        def _():
            dg_ref[...] = dg

        @pl.when(i > 0)
        def _():
            dg_ref[...] += dg

    ssd = pl.BlockSpec((tm, D_SSD), lambda i: (i, 0))
    att = pl.BlockSpec((tm, D_ATT), lambda i: (i, 0))
    vec = pl.BlockSpec((1, D_SSD), lambda i: (0, 0))
    return _hosted_call(
        body, name="premerge_bwd",
        out_shape=(jax.ShapeDtypeStruct((t, D_SSD), F32), jax.ShapeDtypeStruct((t, D_SSD), BF16),
                   jax.ShapeDtypeStruct((t, D_ATT), F32), jax.ShapeDtypeStruct((t, D_ATT), BF16),
                   jax.ShapeDtypeStruct((1, D_SSD), F32), jax.ShapeDtypeStruct((t, 128), F32)),
        grid=(t // tm,),
        in_specs=[ssd, att, ssd, pl.BlockSpec((tm, D_SSD), lambda i: (i, C_Z // D_SSD)), att,
                  pl.BlockSpec((tm, D_ATT), lambda i: (i, C_ZA // D_ATT)), vec],
        out_specs=(ssd, ssd, att, att, vec, pl.BlockSpec((tm, 128), lambda i: (i, 0))),
        scratch_shapes=[],
        operands=(dys, dya, y, proj, o, proj, gamma), semantics=("arbitrary",), exchange=exchange)


_G_BLK = C_G // D_MODEL


def _merge_fwd(a, b, proj, gate_bias):
    t = a.shape[0]
    tm = _row_tile(t)

    def body(a_ref, b_ref, gs_ref, ga_ref, bias_ref, m_ref):
        g_ssd = _sigmoid(gs_ref[...] + bias_ref[:, 0:D_MODEL])
        g_att = _sigmoid(ga_ref[...] + bias_ref[:, D_MODEL:2 * D_MODEL])
        m_ref[...] = (g_ssd * a_ref[...] + g_att * b_ref[...]).astype(BF16)

    row = pl.BlockSpec((tm, D_MODEL), lambda i: (i, 0))
    return pl.pallas_call(
        body, name="merge_fwd",
        out_shape=jax.ShapeDtypeStruct((t, D_MODEL), BF16),
        grid=(t // tm,),
        in_specs=[row, row,
                  pl.BlockSpec((tm, D_MODEL), lambda i: (i, _G_BLK)),
                  pl.BlockSpec((tm, D_MODEL), lambda i: (i, _G_BLK + 1)),
                  pl.BlockSpec((1, 2 * D_MODEL), lambda i: (0, 0))],
        out_specs=row,
        compiler_params=_cp("parallel"),
    )(a, b, proj, proj, gate_bias)


def _merge_bwd(dm, a, b, proj, gate_bias):
    t = a.shape[0]
    tm = _row_tile(t)

    def body(dm_ref, a_ref, b_ref, gs_ref, ga_ref, bias_ref, da_ref, db_ref, dg_ref, dbias_ref):
        i = pl.program_id(0)
        dm_ = dm_ref[...]
        g_ssd = _sigmoid(gs_ref[...] + bias_ref[:, 0:D_MODEL])
        g_att = _sigmoid(ga_ref[...] + bias_ref[:, D_MODEL:2 * D_MODEL])
        da_ref[...] = (dm_ * g_ssd).astype(BF16)
        db_ref[...] = (dm_ * g_att).astype(BF16)
        dgs = dm_ * a_ref[...] * g_ssd * (1.0 - g_ssd)
        dga = dm_ * b_ref[...] * g_att * (1.0 - g_att)
        dg_ref[:, 0:D_MODEL] = dgs.astype(BF16)
        dg_ref[:, D_MODEL:2 * D_MODEL] = dga.astype(BF16)
        part = jnp.concatenate([jnp.sum(dgs, axis=0, keepdims=True), jnp.sum(dga, axis=0, keepdims=True)], axis=1)

        @pl.when(i == 0)
        def _():
            dbias_ref[...] = part

        @pl.when(i > 0)
        def _():
            dbias_ref[...] += part

    row = pl.BlockSpec((tm, D_MODEL), lambda i: (i, 0))
    wide = pl.BlockSpec((tm, 2 * D_MODEL), lambda i: (i, 0))
    vec = pl.BlockSpec((1, 2 * D_MODEL), lambda i: (0, 0))
    return pl.pallas_call(
        body, name="merge_bwd",
        out_shape=(jax.ShapeDtypeStruct((t, D_MODEL), BF16), jax.ShapeDtypeStruct((t, D_MODEL), BF16),
                   jax.ShapeDtypeStruct((t, 2 * D_MODEL), BF16), jax.ShapeDtypeStruct((1, 2 * D_MODEL), F32)),
        grid=(t // tm,),
        in_specs=[row, row, row,
                  pl.BlockSpec((tm, D_MODEL), lambda i: (i, _G_BLK)),
                  pl.BlockSpec((tm, D_MODEL), lambda i: (i, _G_BLK + 1)), vec],
        out_specs=(row, row, wide, vec),
        compiler_params=_cp("arbitrary"),
    )(dm, a, b, proj, proj, gate_bias)


def _post(o2, h, target, g):
    t = o2.shape[0]
    nc = t // CHUNK

    def body(o_ref, h_ref, t_ref, g_ref, dy_ref, do_ref, dg_ref, loss_ref):
        c = pl.program_id(0)
        x = o_ref[...]
        r = lax.rsqrt(jnp.mean(x * x, axis=-1, keepdims=True) + EPS)
        n = x * r
        y = h_ref[...] + n * g_ref[...]
        diff = jnp.where(c > 0, y - t_ref[...], 0.0)
        dy = diff * (1.0 / D_MODEL)
        dy_ref[...] = dy
        gdy = dy * g_ref[...]
        do_ref[...] = (r * (gdy - n * jnp.mean(gdy * n, axis=-1, keepdims=True))).astype(BF16)
        dg = jnp.sum(dy * n, axis=0, keepdims=True)
        lpart = 0.5 * jnp.sum(jnp.sum(diff * diff, axis=1, keepdims=True), axis=0, keepdims=True) * (1.0 / D_MODEL)
        sel = (_iota((8, 128), 0) == 0) & (_iota((8, 128), 1) == 0)

        @pl.when(c == 0)
        def _():
            dg_ref[...] = dg
            loss_ref[...] = jnp.zeros_like(loss_ref)

        @pl.when(c > 0)
        def _():
            dg_ref[...] += dg
            loss_ref[...] += jnp.where(sel, lpart, 0.0)

    row = pl.BlockSpec((CHUNK, D_MODEL), lambda c: (c, 0))
    vec = pl.BlockSpec((1, D_MODEL), lambda c: (0, 0))
    return pl.pallas_call(
        body, name="post",
        out_shape=(jax.ShapeDtypeStruct((t, D_MODEL), F32), jax.ShapeDtypeStruct((t, D_MODEL), BF16),
                   jax.ShapeDtypeStruct((1, D_MODEL), F32), jax.ShapeDtypeStruct((8, 128), F32)),
        grid=(nc,),
        in_specs=[row, row, pl.BlockSpec((CHUNK, D_MODEL), lambda c: (jnp.maximum(c - 1, 0), 0)), vec],
        out_specs=(row, row, vec, pl.BlockSpec((8, 128), lambda c: (0, 0))),
        compiler_params=_cp("arbitrary"),
    )(o2, h, target, g)


def _mm_tiles(t):
    return _tile(t, (704, 384, 128))


def _local_step(h, target, w_main, w_small, pr_slots, ids, norm_pre, conv_w, conv_b, bias_row, a_row,
                dsk_row, ssd_norm, gate_bias, norm_post):
    t = h.shape[0]
    tm = _mm_tiles(t)
    u = _norm1_fwd(h, norm_pre)
    proj, pr_slots = _matmul(u, w_main, "nt", F32, "inproj", tm, 1024, D_MODEL,
                             exchange=_gather_stage([pr_slots], to_sibling=False))
    small, pr_slots = _matmul(u, w_small, "nt", F32, "inproj_small", tm, N_SMALL, D_MODEL,
                              exchange=_gather_stage([pr_slots], to_sibling=True))
    wps = pr_slots[:, 0:512].reshape(D_SSD, D_MODEL)
    wpa = pr_slots[:, 512:768].reshape(D_ATT, D_MODEL)
    wout = pr_slots[:, 768:1024].reshape(D_MODEL, D_MODEL)
    dtlf = _small_fwd(small, bias_row)
    xbc = _conv_fwd(proj, conv_w, conv_b)
    y, hin = _ssd_fwd(xbc, dtlf, a_row, dsk_row)
    c_tok = dtlf[:, H_SSD:H_SSD + H_ATT]
    c_tok = jnp.where(jnp.arange(t)[:, None] < PADF, _C_FILLER, c_tok)
    c_col = c_tok.reshape(t, _NPAIR, 2).transpose(1, 0, 2)
    o, lse, p_blocks, m_run = _attn_fwd(proj, c_col)
    ys, ya = _premerge_fwd(y, o, proj, ssd_norm)
    a = _matmul(ys, wps, "nn", F32, "proj_ssd", tm, D_MODEL, D_SSD)
    b = _matmul(ya, wpa, "nn", F32, "proj_att", tm, D_MODEL, D_ATT)
    merged = _merge_fwd(a, b, proj, gate_bias)
    o2 = _matmul(merged, wout, "nn", F32, "out_proj", tm, D_MODEL, D_MODEL)
    dy_out, do2, d_norm_post, loss_blk = _post(o2, h, target, norm_post)

    dm = _matmul(do2, wout, "nt", F32, "out_proj_dx", tm, D_MODEL, D_MODEL)
    d_wout = _matmul(merged, do2, "tn", F32, "out_proj_dw", D_MODEL, D_MODEL, tm)
    da, db, dgraw, d_gate_bias = _merge_bwd(dm, a, b, proj, gate_bias)
    dys = _matmul(da, wps, "nt", F32, "proj_ssd_dx", tm, D_SSD, D_MODEL)
    d_wps = _matmul(ys, da, "tn", F32, "proj_ssd_dw", D_SSD, D_MODEL, tm)
    dya = _matmul(db, wpa, "nt", F32, "proj_att_dx", tm, D_ATT, D_MODEL)
    d_wpa = _matmul(ya, db, "tn", F32, "proj_att_dw", D_ATT, D_MODEL, tm)
    g32_pr = jnp.concatenate([d_wps.reshape(4, 512, D_MODEL), d_wpa.reshape(4, 256, D_MODEL),
                              d_wout.reshape(4, 256, D_MODEL)], axis=1)
    dy, dz, do, dza, d_ssd_norm, dl, ra_pr = _premerge_bwd(dys, dya, y, o, proj, ssd_norm,
                                                           exchange=_pair_swap([g32_pr]))
    pb_pr = _add_pair(ids, g32_pr, ra_pr)
    dl_row = dl[:, 0:H_ATT].T.reshape(_NPAIR, 2, t)
    dq, dk, dv, dc_key, dc_qry, rb_pr = _attn_bwd(proj, lse, dl_row, do, p_blocks, m_run,
                                                  exchange=_chip_exchange([pb_pr]))
    half_pr = _add_chips(ids, g32_pr, ra_pr, rb_pr)
    dxbc, ddt, d_a, d_dsk = _ssd_bwd(xbc, dtlf, a_row, dsk_row, hin, dy)
    dxbc_raw, d_conv_w, d_conv_b = _conv_bwd(dxbc, proj, conv_w, conv_b)
    dc_tok = jnp.transpose(dc_key, (1, 0, 2)).reshape(t, H_ATT) + dc_qry.reshape(H_ATT, t).T
    dsm = ddt + jnp.pad(dc_tok, ((0, 0), (H_SSD, N_SMALL - H_SSD - H_ATT)))
    dsmall, d_bias_row = _small_bwd(dsm, small, bias_row)
    dproj = [dz, dxbc_raw, dza, dq, dk, dv, dgraw]
    return dict(loss_blk=loss_blk, u=u, dy_out=dy_out, dproj=dproj, dsmall=dsmall, half_pr=half_pr,
                d_conv_w=d_conv_w, d_conv_b=d_conv_b,
                d_bias_row=d_bias_row, d_a=d_a, d_dsk=d_dsk, d_ssd_norm=d_ssd_norm,
                d_gate_bias=d_gate_bias, d_norm_post=d_norm_post)


def _to_aligned_rows(slots):
    w = slots.reshape(N_COLS, slots.shape[2])

    def cut(o):
        return w[o[0]:o[0] + o[1]]
    main = jnp.concatenate([cut(O_Z), cut(O_XBC), cut(O_ZA), cut(O_Q), cut(O_K), cut(O_V), cut(O_G)], axis=0)
    pad = jnp.zeros((N_SMALL - H_SSD - H_ATT, w.shape[1]), w.dtype)
    small = jnp.concatenate([cut(O_DT), cut(O_F), pad], axis=0)
    assert main.shape[0] == N_MAIN and small.shape[0] == N_SMALL
    return main, small


def _from_aligned_rows(main, small):
    def cm(c0, n):
        return main[c0:c0 + n]
    flat = jnp.concatenate([cm(C_Z, 2048), cm(C_XBC, 3072), small[0:H_SSD], cm(C_ZA, 1024),
                            cm(C_Q, 1024), cm(C_K, 1024), cm(C_V, 1024), small[H_SSD:H_SSD + H_ATT],
                            cm(C_G, 2048)], axis=0)
    return flat.reshape(4, N_COLS // 4, flat.shape[1])


_MESH = pl.DeviceIdType.MESH
_ANY = pl.BlockSpec(memory_space=pl.ANY)
_VM = pl.BlockSpec(memory_space=pltpu.VMEM)
_HALF = 512
N_DEV = 8


def _coords():
    return lax.axis_index("x"), lax.axis_index("y"), lax.axis_index("c")


def _other_chips(x, y):
    return [(1 - x, y), (x, 1 - y), (1 - x, 1 - y)]


def _half(cc):
    return pl.ds(cc * _HALF, _HALF)


def _gather_shards(slots):
    n = len(slots)

    def body(*refs):
        buf = refs[n:2 * n]
        send_sems, recv_sems = refs[2 * n:]
        x, y, c = _coords()
        chip = 2 * x + y
        sibling = (x, y, 1 - c)
        chips = _other_chips(x, y)

        def copy(i, frm, cc, k, to):
            part = buf[i].at[frm, :, _half(cc)]
            return pltpu.make_async_remote_copy(src_ref=part, dst_ref=part, send_sem=send_sems.at[6 * i + k],
                                                recv_sem=recv_sems.at[6 * i + k], device_id=to, device_id_type=_MESH)

        def chip_of(k):
            return 2 * chips[k][0] + chips[k][1]

        first = [copy(i, chip, c, k, (*chips[k], c)) for k in range(3) for i in range(n)]
        for cp in first:
            cp.start()
        passed = []
        for k in range(3):
            for i in range(n):
                copy(i, chip_of(k), c, k, (*chips[k], c)).wait_recv()
                passed.append(copy(i, chip_of(k), c, 3 + k, sibling))
                passed[-1].start()
        for k in range(3):
            for i in range(n):
                copy(i, chip_of(k), 1 - c, 3 + k, sibling).wait_recv()
        for cp in first + passed:
            cp.wait_send()

    return pl.pallas_call(
        body, name="gather_shards",
        out_shape=tuple(jax.ShapeDtypeStruct(s.shape, s.dtype) for s in slots),
        in_specs=[_ANY] * n, out_specs=tuple([_ANY] * n),
        input_output_aliases={i: i for i in range(n)},
        scratch_shapes=[pltpu.SemaphoreType.DMA((6 * n,)), pltpu.SemaphoreType.DMA((6 * n,))],
    )(*slots)


def _allgather8(block, name):
    rows, width = block.shape

    def body(x_ref, out_ref, send_sems, recv_sems, local_sem):
        x, y, c = _coords()
        me, sibling = (x, y, c), (x, y, 1 - c)
        chips = _other_chips(x, y)

        def slot(px, py, pc):
            return out_ref.at[4 * px + 2 * py + pc]

        def copy(k, blk, to, src=None):
            return pltpu.make_async_remote_copy(src_ref=slot(*blk) if src is None else src, dst_ref=slot(*blk),
                                                send_sem=send_sems.at[k], recv_sem=recv_sems.at[k],
                                                device_id=to, device_id_type=_MESH)

        mine = pltpu.make_async_copy(x_ref, slot(*me), local_sem)
        mine.start()
        first = [copy(0, me, sibling, src=x_ref)]
        first += [copy(1 + j, me, (*chip, c), src=x_ref) for j, chip in enumerate(chips)]
        for cp in first:
            cp.start()
        passed = [copy(4 + j, (*chip, c), sibling) for j, chip in enumerate(chips)]
        for j, chip in enumerate(chips):
            copy(1 + j, (*chip, c), me).wait_recv()
            passed[j].start()
        copy(0, sibling, me).wait_recv()
        for j, chip in enumerate(chips):
            copy(4 + j, (*chip, 1 - c), me).wait_recv()
        for cp in first + passed:
            cp.wait_send()
        mine.wait()

    return pl.pallas_call(
        body, name=name,
        out_shape=jax.ShapeDtypeStruct((N_DEV, rows, width), block.dtype),
        in_specs=[_VM], out_specs=_VM,
        scratch_shapes=[pltpu.SemaphoreType.DMA((7,)), pltpu.SemaphoreType.DMA((7,)), pltpu.SemaphoreType.DMA],
    )(block)


def _pair_swap(arrs):
    def copies(src, dst, send_sems, recv_sems):
        x, y, c = _coords()
        return [pltpu.make_async_remote_copy(src_ref=src[i].at[:, :, _half(1 - c)], dst_ref=dst[i],
                                             send_sem=send_sems.at[i], recv_sem=recv_sems.at[i],
                                             device_id=(x, y, 1 - c), device_id_type=_MESH) for i in range(len(src))]

    shapes = tuple(jax.ShapeDtypeStruct((4, a.shape[1], _HALF), a.dtype) for a in arrs)
    return tuple(arrs), shapes, copies, len(arrs), False


def _chip_exchange(arrs):
    def copies(src, dst, send_sems, recv_sems):
        x, y, c = _coords()
        chips = _other_chips(x, y)
        return [pltpu.make_async_remote_copy(src_ref=src[i].at[2 * chips[k][0] + chips[k][1]], dst_ref=dst[i].at[k],
                                             send_sem=send_sems.at[3 * i + k], recv_sem=recv_sems.at[3 * i + k],
                                             device_id=(*chips[k], c), device_id_type=_MESH)
                for k in range(3) for i in range(len(src))]

    shapes = tuple(jax.ShapeDtypeStruct((3,) + a.shape[1:], a.dtype) for a in arrs)
    return tuple(arrs), shapes, copies, 3 * len(arrs), False


def _gather_stage(slots, to_sibling):
    def copies(buf, _, send_sems, recv_sems):
        x, y, c = _coords()
        chips = _other_chips(x, y)
        out = []
        for k in range(3):
            for i in range(len(buf)):
                frm = 2 * chips[k][0] + chips[k][1] if to_sibling else 2 * x + y
                part = buf[i].at[frm, :, _half(c)]
                out.append(pltpu.make_async_remote_copy(
                    src_ref=part, dst_ref=part, send_sem=send_sems.at[3 * i + k], recv_sem=recv_sems.at[3 * i + k],
                    device_id=(x, y, 1 - c) if to_sibling else (*chips[k], c), device_id_type=_MESH))
        return out

    shapes = tuple(jax.ShapeDtypeStruct(s.shape, s.dtype) for s in slots)
    return tuple(slots), shapes, copies, 3 * len(slots), True


def _pair_join_halves(fulls):
    n = len(fulls)

    def body(*refs):
        buf = refs[n:2 * n]
        send_sems, recv_sems = refs[2 * n:]
        x, y, c = _coords()

        def remote(i, cc):
            part = buf[i].at[:, _half(cc)]
            return pltpu.make_async_remote_copy(src_ref=part, dst_ref=part, send_sem=send_sems.at[i],
                                                recv_sem=recv_sems.at[i], device_id=(x, y, 1 - c), device_id_type=_MESH)

        for i in range(n):
            remote(i, c).start()
        for i in range(n):
            remote(i, c).wait_send()
            remote(i, 1 - c).wait_recv()

    return pl.pallas_call(
        body, name="pair_join_halves",
        out_shape=tuple(jax.ShapeDtypeStruct(a.shape, a.dtype) for a in fulls),
        in_specs=[_ANY] * n, out_specs=tuple([_ANY] * n),
        input_output_aliases={i: i for i in range(n)},
        scratch_shapes=[pltpu.SemaphoreType.DMA((n,)), pltpu.SemaphoreType.DMA((n,))],
    )(*fulls)


_RED_TC = 128
_RED_NT = _HALF // _RED_TC


def _add_pair(ids, g32, recv_a):
    rows = g32.shape[1]

    def body(ids_ref, g_ref, r_ref, o_ref):
        o_ref[...] = (g_ref[...] + r_ref[...]).astype(BF16)

    blk = pl.BlockSpec((1, rows, _RED_TC), lambda j, l, ids: (j, 0, l))
    return pl.pallas_call(
        body, name="add_pair",
        out_shape=jax.ShapeDtypeStruct((4, rows, _HALF), BF16),
        grid_spec=pltpu.PrefetchScalarGridSpec(
            num_scalar_prefetch=1, grid=(4, _RED_NT),
            in_specs=[pl.BlockSpec((1, rows, _RED_TC), lambda j, l, ids: (j, 0, ids[0] * _RED_NT + l)), blk],
            out_specs=blk),
        compiler_params=_cp("parallel", "parallel"),
    )(ids, g32, recv_a)


def _add_chips(ids, g32, recv_a, recv_b):
    rows = g32.shape[1]

    def body(ids_ref, g_ref, a_ref, b_ref, o_ref):
        acc = g_ref[0] + a_ref[0]
        for k in range(3):
            acc = acc + b_ref[k].astype(F32)
        o_ref[...] = acc

    return pl.pallas_call(
        body, name="add_chips",
        out_shape=jax.ShapeDtypeStruct((rows, 2 * _HALF), F32),
        grid_spec=pltpu.PrefetchScalarGridSpec(
            num_scalar_prefetch=1, grid=(_RED_NT,),
            in_specs=[pl.BlockSpec((1, rows, _RED_TC), lambda l, ids: (ids[1], 0, ids[0] * _RED_NT + l)),
                      pl.BlockSpec((1, rows, _RED_TC), lambda l, ids: (ids[1], 0, l)),
                      pl.BlockSpec((3, rows, _RED_TC), lambda l, ids: (0, 0, l))],
            out_specs=pl.BlockSpec((rows, _RED_TC), lambda l, ids: (0, ids[0] * _RED_NT + l))),
        compiler_params=_cp("parallel"),
    )(ids, g32, recv_a, recv_b)


def _sum8(gathered):
    _, rows, width = gathered.shape

    def body(g_ref, o_ref):
        acc = g_ref[0]
        for d in range(1, N_DEV):
            acc = acc + g_ref[d]
        o_ref[...] = acc

    return pl.pallas_call(
        body, name="sum8",
        out_shape=jax.ShapeDtypeStruct((rows, width), F32),
        in_specs=[_VM], out_specs=_VM,
    )(gathered)


def _adamw(w, g, m, v, name):
    rows, cols = w.shape
    budget = (3 << 20) // 2
    tr, tc = rows, cols
    if rows * cols * 4 > budget:
        if rows % 8 == 0:
            tr = max(c for c in range(8, rows, 8) if rows % c == 0 and c * cols * 4 <= budget)
        else:
            tc = next(c for c in (512, 256, 128) if cols % c == 0 and rows * c * 4 <= budget)
    c1 = 1.0 - ADAM_B1 ** ADAM_STEP
    c2 = 1.0 - ADAM_B2 ** ADAM_STEP

    def body(w_ref, g_ref, m_ref, v_ref, d_ref, mo_ref, vo_ref):
        gg = g_ref[...]
        mn = ADAM_B1 * m_ref[...] + (1.0 - ADAM_B1) * gg
        vn = ADAM_B2 * v_ref[...] + (1.0 - ADAM_B2) * (gg * gg)
        mo_ref[...] = mn
        vo_ref[...] = vn
        d_ref[...] = -ADAM_LR * ((mn / c1) / (jnp.sqrt(vn / c2) + ADAM_EPS) + ADAM_WD * w_ref[...])

    blk = pl.BlockSpec((tr, tc), lambda i, j: (i, j))
    shp = jax.ShapeDtypeStruct((rows, cols), F32)
    return pl.pallas_call(
        body, name=name, out_shape=(shp, shp, shp), grid=(rows // tr, cols // tc),
        in_specs=[blk] * 4, out_specs=(blk, blk, blk),
        compiler_params=_cp("parallel", "parallel"),
    )(w, g, m, v)


def _rows128(a):
    return a.reshape(-1, 128)


def _pack_small(norm_pre, conv_b, ssd_norm, gate_bias, norm_post, dt_bias, a_log, d_skip, fgate_bias):
    tiny = jnp.concatenate([dt_bias.reshape(-1), a_log.reshape(-1), d_skip.reshape(-1), fgate_bias.reshape(-1),
                            jnp.zeros((16,), F32)])
    return jnp.concatenate([_rows128(norm_pre), _rows128(conv_b), _rows128(ssd_norm), _rows128(gate_bias),
                            _rows128(norm_post), tiny.reshape(1, 128)], axis=0)


_SMALL_PAD = 80


def _unpack_small(p):
    tiny = p[72]
    return dict(norm_pre=p[0:8].reshape(1, 1024), conv_b=p[8:32].reshape(1, 3072), ssd_norm=p[32:48].reshape(1, 2048),
                gate_bias=p[48:64].reshape(1, 2048), norm_post=p[64:72].reshape(1, 1024),
                dt_bias=tiny[0:32].reshape(1, 32), a_log=tiny[32:64].reshape(1, 32),
                d_skip=tiny[64:96].reshape(1, 32), fgate_bias=tiny[96:112].reshape(1, 16))


def _pad_rows(a, rows):
    return jnp.concatenate([a, jnp.zeros((rows - a.shape[0], a.shape[1]), a.dtype)], axis=0)


def kernel(x, meta_tokens, norm_pre, w_in, conv_w, conv_b, dt_bias, a_log, d_skip, ssd_norm, fgate_bias, gate_bias, w_proj_ssd, w_proj_att, w_out, norm_post, loss_target, m_meta_tokens, m_norm_pre, m_w_in, m_conv_w, m_conv_b, m_dt_bias, m_a_log, m_d_skip, m_ssd_norm, m_fgate_bias, m_gate_bias, m_w_proj_ssd, m_w_proj_att, m_w_out, m_norm_post, v_meta_tokens, v_norm_pre, v_w_in, v_conv_w, v_conv_b, v_dt_bias, v_a_log, v_d_skip, v_ssd_norm, v_fgate_bias, v_gate_bias, v_w_proj_ssd, v_w_proj_att, v_w_out, v_norm_post):
    cx, cy, cc = _coords()
    chip = 2 * cx + cy
    ids = jnp.stack([cc, chip]).astype(jnp.int32)
    seq = x.shape[1]

    w_in_sh = jnp.transpose(w_in[0]).astype(BF16)
    w_pr_sh = jnp.concatenate([w_proj_ssd[0], w_proj_att[0], w_out[0]], axis=0).astype(BF16)

    def own_slot(sh):
        return lax.dynamic_update_slice(lax.empty((4,) + sh.shape, sh.dtype), sh[None], (chip, 0, 0))

    (g_in,) = _gather_shards([own_slot(w_in_sh)])
    w_main, w_small = _to_aligned_rows(g_in)
    sm_sh = jnp.concatenate([_rows128(meta_tokens), _rows128(conv_w[0])], axis=0)
    sm_all = _allgather8(sm_sh, "gather_small_weights")[0::2]
    meta_full = jnp.transpose(sm_all[:, 0:32].reshape(4, N_META, 256), (1, 0, 2)).reshape(N_META, D_MODEL)
    conv_w_full = jnp.transpose(sm_all[:, 32:56].reshape(4, CONV_K, 768), (1, 0, 2)).reshape(CONV_K, CONV_DIM)

    h = jnp.concatenate([jnp.zeros((PADF, D_MODEL), F32), meta_full, x[0]], axis=0)
    bias_row = jnp.concatenate([dt_bias[0], fgate_bias[0], jnp.zeros((N_SMALL - H_SSD - H_ATT,), F32)]).reshape(1, N_SMALL)
    a_neg = -jnp.exp(a_log[0])
    a_row = jnp.concatenate([a_neg, jnp.zeros((N_SMALL - H_SSD,), F32)]).reshape(1, N_SMALL)
    dsk_row = jnp.repeat(d_skip[0], 64).reshape(1, D_SSD)
    r = _local_step(h, loss_target[0], w_main, w_small, own_slot(w_pr_sh), ids, norm_pre, conv_w_full, conv_b,
                    bias_row, a_row, dsk_row, ssd_norm, gate_bias, norm_post)

    tm = _mm_tiles(h.shape[0])
    n_row_tiles = h.shape[0] // tm
    d_w_main = _matmul_cat_tn(r["dproj"], r["u"], "inproj_dw", tm)
    d_w_small = _matmul(r["dsmall"], r["u"], "tn", F32, "inproj_small_dw", N_SMALL, D_MODEL, tm)
    g32_in = _from_aligned_rows(d_w_main, d_w_small)
    first = max(n_row_tiles // 6, 1)
    du_first, ra_in = _matmul_cat_nn(r["dproj"], w_main, "inproj_dx_swap", tm, rows=(0, first),
                                     exchange=_pair_swap([g32_in]))
    pb_in = _add_pair(ids, g32_in, ra_in)
    du_a, rb_in = _matmul_cat_nn(r["dproj"], w_main, "inproj_dx_exchange", tm,
                                 rows=(first, n_row_tiles - first), fill=du_first,
                                 exchange=_chip_exchange([pb_in]))
    du_b = _matmul(r["dsmall"], w_small, "nn", F32, "inproj_small_dx", tm, D_MODEL, N_SMALL)
    dh, d_norm_pre = _norm1_bwd(du_a, du_b, h, norm_pre, r["dy_out"])
    grad_x = dh[PADF + N_META:].reshape(1, seq, D_MODEL)
    half_in = _add_chips(ids, g32_in, ra_in, rb_in)
    gw_in, gw_pr = _pair_join_halves([half_in, r["half_pr"]])

    tiny = r["d_bias_row"][0]
    part_small = _pack_small(d_norm_pre, r["d_conv_b"], r["d_ssd_norm"], r["d_gate_bias"], r["d_norm_post"],
                             tiny[0:H_SSD], r["d_a"][0, 0:H_SSD] * a_neg, r["d_dsk"].reshape(H_SSD, 64).sum(axis=1),
                             tiny[H_SSD:H_SSD + H_ATT])
    part = jnp.concatenate([_pad_rows(part_small, _SMALL_PAD), _rows128(r["d_conv_w"]),
                            _rows128(dh[PADF:PADF + N_META]), r["loss_blk"]], axis=0)
    tot = _sum8(_allgather8(part, "gather_small_grads"))
    loss = tot[_SMALL_PAD + 96 + 128, 0]
    g_small = tot[0:_SMALL_PAD]
    g_conv_w = lax.dynamic_slice_in_dim(tot[_SMALL_PAD:_SMALL_PAD + 96].reshape(CONV_K, CONV_DIM), chip * 768, 768, axis=1)
    g_meta = lax.dynamic_slice_in_dim(tot[_SMALL_PAD + 96:_SMALL_PAD + 224].reshape(N_META, D_MODEL), chip * 256, 256, axis=1)

    upd = {}
    upd["w_in"] = tuple(jnp.transpose(a) for a in (gw_in,) + _adamw(
        jnp.transpose(w_in[0]), gw_in, jnp.transpose(m_w_in[0]), jnp.transpose(v_w_in[0]), "adamw_w_in"))
    w_pr32 = jnp.concatenate([w_proj_ssd[0], w_proj_att[0], w_out[0]], axis=0)
    m_pr = jnp.concatenate([m_w_proj_ssd[0], m_w_proj_att[0], m_w_out[0]], axis=0)
    v_pr = jnp.concatenate([v_w_proj_ssd[0], v_w_proj_att[0], v_w_out[0]], axis=0)
    pr = (gw_pr,) + _adamw(w_pr32, gw_pr, m_pr, v_pr, "adamw_w_proj")
    upd["w_proj_ssd"] = tuple(a[0:512] for a in pr)
    upd["w_proj_att"] = tuple(a[512:768] for a in pr)
    upd["w_out"] = tuple(a[768:1024] for a in pr)
    upd["conv_w"] = (g_conv_w,) + _adamw(conv_w[0], g_conv_w, m_conv_w[0], v_conv_w[0], "adamw_conv_w")
    upd["meta_tokens"] = (g_meta,) + _adamw(meta_tokens, g_meta, m_meta_tokens, v_meta_tokens, "adamw_meta")
    pk = lambda np_, cb, sn, gb, npo, dtb, al, ds, fg: _pad_rows(_pack_small(np_, cb, sn, gb, npo, dtb, al, ds, fg), _SMALL_PAD)
    w_sm = pk(norm_pre, conv_b, ssd_norm, gate_bias, norm_post, dt_bias, a_log, d_skip, fgate_bias)
    m_sm = pk(m_norm_pre, m_conv_b, m_ssd_norm, m_gate_bias, m_norm_post, m_dt_bias, m_a_log, m_d_skip, m_fgate_bias)
    v_sm = pk(v_norm_pre, v_conv_b, v_ssd_norm, v_gate_bias, v_norm_post, v_dt_bias, v_a_log, v_d_skip, v_fgate_bias)
    sm = [_unpack_small(a) for a in (g_small,) + _adamw(w_sm, g_small, m_sm, v_sm, "adamw_small")]
    for name in ("norm_pre", "conv_b", "dt_bias", "a_log", "d_skip", "ssd_norm", "fgate_bias", "gate_bias", "norm_post"):
        upd[name] = tuple(s[name] for s in sm)
    lead = ("w_in", "conv_w", "w_proj_ssd", "w_proj_att", "w_out")
    order = ("meta_tokens", "norm_pre", "w_in", "conv_w", "conv_b", "dt_bias", "a_log", "d_skip", "ssd_norm",
             "fgate_bias", "gate_bias", "w_proj_ssd", "w_proj_att", "w_out", "norm_post")
    outs = [loss, grad_x]
    for part_i in range(4):
        for name in order:
            a = upd[name][part_i]
            outs.append(a[None] if name in lead else a)
    return tuple(outs)
```

```python
import functools
import math

import jax
import jax.numpy as jnp
from jax import lax
from jax.experimental import pallas as pl
from jax.experimental.pallas import tpu as pltpu

F32 = jnp.float32
BF16 = jnp.bfloat16
HIGHEST = lax.Precision.HIGHEST

D_MODEL = 1024
N_META = 16
CHUNK = 128
PADF = CHUNK - N_META
D_SSD = 2048
H_SSD = 32
G_SSD = 4
N_STATE = 128
CONV_K = 4
CONV_DIM = D_SSD + 2 * G_SSD * N_STATE
H_ATT = 16
D_ATT = 1024
EPS = 1e-6
N_COLS = 11312

C_Z, C_XBC, C_ZA, C_Q, C_K, C_V, C_G = 0, 2048, 5120, 6144, 7168, 8192, 9216
N_MAIN = 11264
N_SMALL = 128
O_Z, O_XBC, O_DT, O_ZA, O_Q, O_K, O_V, O_F, O_G = (
    (0, 2048), (2048, 3072), (5120, 32), (5152, 1024), (6176, 1024), (7200, 1024),
    (8224, 1024), (9248, 16), (9264, 2048))

ADAM_LR, ADAM_B1, ADAM_B2, ADAM_EPS, ADAM_WD, ADAM_STEP = 0.001, 0.9, 0.999, 1e-08, 0.01, 10

VMEM_LIMIT = 56 * 1024 * 1024


def _cp(*sem):
    return pltpu.CompilerParams(dimension_semantics=sem, vmem_limit_bytes=VMEM_LIMIT)


def _tile(n, prefs):
    for p in prefs:
        if n % p == 0:
            return p
    raise ValueError(f"no tile for {n} in {prefs}")


def _iota(shape, dim):
    return lax.broadcasted_iota(jnp.int32, shape, dim)


def _sigmoid(x):
    return 1.0 / (1.0 + jnp.exp(-x))


def _softplus_tail(x):
    return jnp.log(1.0 + jnp.exp(-jnp.abs(x)))


_NN = (((1,), (0,)), ((), ()))
_NT = (((1,), (1,)), ((), ()))
_TN = (((0,), (0,)), ((), ()))


def _dot(a, b, dims=_NN):
    return lax.dot_general(a, b, dims, preferred_element_type=F32)


def _dot_exact(a, b, dims=_NN):
    return lax.dot_general(a, b, dims, precision=HIGHEST, preferred_element_type=F32)


def _hosted_call(body, *, name, grid, in_specs, out_specs, out_shape, scratch_shapes, operands, semantics,
                 exchange=None, aliases=None):
    aliases = dict(aliases or {})
    if exchange is None:
        return pl.pallas_call(body, name=name, out_shape=out_shape, grid=grid, in_specs=in_specs,
                              out_specs=out_specs, scratch_shapes=scratch_shapes, input_output_aliases=aliases,
                              compiler_params=_cp(*semantics))(*operands)
    arrays, shapes, copies, n_sems, in_place = exchange
    n_in, n_out, n_ex = len(operands), len(out_shape), len(arrays)

    def hosted(*refs):
        ex_in = refs[n_in:n_in + n_ex]
        ex_out = refs[n_in + n_ex + n_out:n_in + n_ex + n_out + n_ex]
        own = refs[:n_in] + refs[n_in + n_ex:n_in + n_ex + n_out] + refs[n_in + 2 * n_ex + n_out:-2]
        first = functools.reduce(lambda p, q: p & q, [pl.program_id(d) == 0 for d in range(len(grid))])
        last = functools.reduce(lambda p, q: p & q, [pl.program_id(d) == grid[d] - 1 for d in range(len(grid))])

        def descriptors():
            return copies(ex_out if in_place else ex_in, ex_out, refs[-2], refs[-1])

        @pl.when(first)
        def _():
            for cp in descriptors():
                cp.start()

        body(*own)

        @pl.when(last)
        def _():
            for cp in descriptors():
                cp.wait()

    return pl.pallas_call(
        hosted, name=name,
        out_shape=tuple(out_shape) + tuple(shapes),
        grid=grid,
        in_specs=list(in_specs) + [_ANY] * n_ex,
        out_specs=tuple(out_specs) + (_ANY,) * n_ex,
        input_output_aliases={**aliases, **({n_in + e: n_out + e for e in range(n_ex)} if in_place else {})},
        scratch_shapes=list(scratch_shapes) + [pltpu.SemaphoreType.DMA((n_sems,)), pltpu.SemaphoreType.DMA((n_sems,))],
        compiler_params=_cp(*(("arbitrary",) * len(grid))),
    )(*operands, *arrays)


def _matmul(a, b, mode, out_dtype, name, tm, tn, tk, exchange=None):
    if mode == "tn":
        kdim, m = a.shape
    else:
        m, kdim = a.shape
    n = b.shape[0] if mode == "nt" else b.shape[1]
    nk = kdim // tk
    dims = {"nn": _NN, "nt": _NT, "tn": _TN}[mode]
    a_spec = (pl.BlockSpec((tk, tm), lambda i, j, k: (k, i)) if mode == "tn"
              else pl.BlockSpec((tm, tk), lambda i, j, k: (i, k)))
    b_spec = (pl.BlockSpec((tn, tk), lambda i, j, k: (j, k)) if mode == "nt"
              else pl.BlockSpec((tk, tn), lambda i, j, k: (k, j)))

    def body(a_ref, b_ref, o_ref, acc_ref):
        k = pl.program_id(2)
        p = _dot(a_ref[...].astype(BF16), b_ref[...].astype(BF16), dims)
        if nk == 1:
            o_ref[...] = p.astype(out_dtype)
        else:
            @pl.when(k == 0)
            def _():
                acc_ref[...] = p

            @pl.when(k > 0)
            def _():
                acc_ref[...] += p

            @pl.when(k == nk - 1)
            def _():
                o_ref[...] = acc_ref[...].astype(out_dtype)

    out = _hosted_call(
        body, name=name,
        out_shape=(jax.ShapeDtypeStruct((m, n), out_dtype),),
        grid=(m // tm, n // tn, nk),
        in_specs=[a_spec, b_spec],
        out_specs=(pl.BlockSpec((tm, tn), lambda i, j, k: (i, j)),),
        scratch_shapes=[pltpu.VMEM((tm, tn), F32)],
        operands=(a, b), semantics=("parallel", "parallel", "arbitrary"), exchange=exchange)
    return out[0] if exchange is None else out


_CAT_BLK = 1024


def _piece_ranges(pieces):
    out, off = [], 0
    for p in pieces:
        nb = p.shape[1] // _CAT_BLK
        out.append((off, nb))
        off += nb
    return out, off


def _matmul_cat_nn(pieces, b, name, tm, rows=None, fill=None, exchange=None):
    t = pieces[0].shape[0]
    n = b.shape[1]
    ranges, nk = _piece_ranges(pieces)
    first, ni = rows if rows is not None else (0, t // tm)
    n_in = len(pieces) + 1 + (fill is not None)

    def body(*refs):
        a_refs, b_ref, o_ref, acc_ref = refs[:len(pieces)], refs[len(pieces)], refs[n_in], refs[n_in + 1]
        k = pl.program_id(1)

        @pl.when(k == 0)
        def _():
            acc_ref[...] = jnp.zeros_like(acc_ref)

        for a_ref, (off, nb) in zip(a_refs, ranges):
            @pl.when((k >= off) & (k < off + nb))
            def _(a_ref=a_ref):
                acc_ref[...] += _dot(a_ref[...], b_ref[...])

        @pl.when(k == nk - 1)
        def _():
            o_ref[...] = acc_ref[...]

    def a_spec(off, nb):
        return pl.BlockSpec((tm, _CAT_BLK), lambda i, k: (first + i, jnp.clip(k - off, 0, nb - 1)))

    in_specs = [a_spec(off, nb) for off, nb in ranges] + [pl.BlockSpec((_CAT_BLK, n), lambda i, k: (k, 0))]
    operands = list(pieces) + [b]
    if fill is not None:
        in_specs.append(_ANY)
        operands.append(fill)
    out = _hosted_call(
        body, name=name,
        out_shape=(jax.ShapeDtypeStruct((t, n), F32),),
        grid=(ni, nk),
        in_specs=in_specs,
        out_specs=(pl.BlockSpec((tm, n), lambda i, k: (first + i, 0)),),
        scratch_shapes=[pltpu.VMEM((tm, n), F32)],
        operands=operands, semantics=("parallel", "arbitrary"), exchange=exchange,
        aliases={len(pieces) + 1: 0} if fill is not None else None)
    return out if exchange is not None else out[0]


def _matmul_cat_tn(pieces, b, name, tk):
    t = pieces[0].shape[0]
    n = b.shape[1]
    ranges, nm = _piece_ranges(pieces)
    nk = t // tk

    def body(*refs):
        a_refs, b_ref, o_ref, acc_ref = refs[:len(pieces)], refs[-3], refs[-2], refs[-1]
        m = pl.program_id(0)
        k = pl.program_id(1)

        @pl.when(k == 0)
        def _():
            acc_ref[...] = jnp.zeros_like(acc_ref)

        for a_ref, (off, nb) in zip(a_refs, ranges):
            @pl.when((m >= off) & (m < off + nb))
            def _(a_ref=a_ref):
                acc_ref[...] += _dot(a_ref[...], b_ref[...], _TN)

        @pl.when(k == nk - 1)
        def _():
            o_ref[...] = acc_ref[...]

    def a_spec(off, nb):
        def index(m, k):
            mine = (m >= off) & (m < off + nb)
            return jnp.where(mine, k, 0), jnp.clip(m - off, 0, nb - 1)
        return pl.BlockSpec((tk, _CAT_BLK), index)

    return pl.pallas_call(
        body, name=name,
        out_shape=jax.ShapeDtypeStruct((nm * _CAT_BLK, n), F32),
        grid=(nm, nk),
        in_specs=[a_spec(off, nb) for off, nb in ranges] + [pl.BlockSpec((tk, n), lambda m, k: (k, 0))],
        out_specs=pl.BlockSpec((_CAT_BLK, n), lambda m, k: (m, 0)),
        scratch_shapes=[pltpu.VMEM((_CAT_BLK, n), F32)],
        compiler_params=_cp("parallel", "arbitrary"),
    )(*pieces, b)


def _row_tile(t):
    return _tile(t, (528, 128))


def _row_tile_wide(t):
    return _tile(t, (352, 128))


def _norm1_fwd(h, g):
    t = h.shape[0]
    tm = _row_tile(t)

    def body(h_ref, g_ref, u_ref):
        x = h_ref[...]
        r = lax.rsqrt(jnp.mean(x * x, axis=-1, keepdims=True) + EPS)
        u_ref[...] = (x * r * g_ref[...]).astype(BF16)

    return pl.pallas_call(
        body, name="norm1_fwd",
        out_shape=jax.ShapeDtypeStruct((t, D_MODEL), BF16),
        grid=(t // tm,),
        in_specs=[pl.BlockSpec((tm, D_MODEL), lambda i: (i, 0)),
                  pl.BlockSpec((1, D_MODEL), lambda i: (0, 0))],
        out_specs=pl.BlockSpec((tm, D_MODEL), lambda i: (i, 0)),
        compiler_params=_cp("parallel"),
    )(h, g)


def _norm1_bwd(du_a, du_b, h, g, dy):
    t = h.shape[0]
    tm = _row_tile(t)

    def body(a_ref, b_ref, h_ref, g_ref, dy_ref, dh_ref, dg_ref):
        i = pl.program_id(0)
        x = h_ref[...]
        du = a_ref[...] + b_ref[...]
        r = lax.rsqrt(jnp.mean(x * x, axis=-1, keepdims=True) + EPS)
        gdu = du * g_ref[...]
        dh_ref[...] = dy_ref[...] + r * (gdu - x * (r * r) * jnp.mean(gdu * x, axis=-1, keepdims=True))
        part = jnp.sum(du * x * r, axis=0, keepdims=True)

        @pl.when(i == 0)
        def _():
            dg_ref[...] = part

        @pl.when(i > 0)
        def _():
            dg_ref[...] += part

    row = pl.BlockSpec((tm, D_MODEL), lambda i: (i, 0))
    vec = pl.BlockSpec((1, D_MODEL), lambda i: (0, 0))
    return pl.pallas_call(
        body, name="norm1_bwd",
        out_shape=(jax.ShapeDtypeStruct((t, D_MODEL), F32), jax.ShapeDtypeStruct((1, D_MODEL), F32)),
        grid=(t // tm,),
        in_specs=[row, row, row, vec, row],
        out_specs=(row, vec),
        compiler_params=_cp("arbitrary"),
    )(du_a, du_b, h, g, dy)


def _small_fwd(small, bias_row):
    t = small.shape[0]
    rt = _tile(t, (384, 128))

    def body(s_ref, b_ref, o_ref, carry_ref):
        c = pl.program_id(0)

        @pl.when(c == 0)
        def _():
            carry_ref[...] = jnp.zeros_like(carry_ref)

        x = s_ref[...] + b_ref[...]
        lane = _iota((rt, N_SMALL), 1)
        valid = (c * rt + _iota((rt, N_SMALL), 0)) >= PADF
        tail = _softplus_tail(x)
        dt = jnp.where(valid & (lane < H_SSD), jnp.maximum(x, 0.0) + tail, 0.0)
        lf = jnp.where(valid & (lane >= H_SSD) & (lane < H_SSD + H_ATT), jnp.minimum(x, 0.0) - tail, 0.0)
        tri = (_iota((rt, rt), 0) >= _iota((rt, rt), 1)).astype(F32)
        cs = _dot_exact(tri, lf) + carry_ref[...]
        carry_ref[...] = cs[rt - 1:rt, :]
        o_ref[...] = dt + cs

    return pl.pallas_call(
        body, name="small_fwd",
        out_shape=jax.ShapeDtypeStruct((t, N_SMALL), F32),
        grid=(t // rt,),
        in_specs=[pl.BlockSpec((rt, N_SMALL), lambda c: (c, 0)),
                  pl.BlockSpec((1, N_SMALL), lambda c: (0, 0))],
        out_specs=pl.BlockSpec((rt, N_SMALL), lambda c: (c, 0)),
        scratch_shapes=[pltpu.VMEM((1, N_SMALL), F32)],
        compiler_params=_cp("arbitrary"),
    )(small, bias_row)


def _small_bwd(dsm, small, bias_row):
    t = small.shape[0]
    rt = _tile(t, (384, 128))
    nc = t // rt

    def body(d_ref, s_ref, b_ref, o_ref, db_ref, carry_ref):
        step = pl.program_id(0)
        c = nc - 1 - step

        @pl.when(step == 0)
        def _():
            carry_ref[...] = jnp.zeros_like(carry_ref)
            db_ref[...] = jnp.zeros_like(db_ref)

        x = s_ref[...] + b_ref[...]
        d = d_ref[...]
        lane = _iota((rt, N_SMALL), 1)
        valid = (c * rt + _iota((rt, N_SMALL), 0)) >= PADF
        is_dt = lane < H_SSD
        is_f = (lane >= H_SSD) & (lane < H_SSD + H_ATT)
        triu = (_iota((rt, rt), 1) >= _iota((rt, rt), 0)).astype(F32)
        dc = jnp.where(is_f, d, 0.0)
        dlf = _dot_exact(triu, dc) + carry_ref[...]
        carry_ref[...] = dlf[0:1, :]
        sg = _sigmoid(x)
        out = jnp.where(valid & is_dt, d * sg, 0.0) + jnp.where(valid & is_f, dlf * (1.0 - sg), 0.0)
        o_ref[...] = out.astype(BF16)
        db_ref[...] += jnp.sum(out, axis=0, keepdims=True)

    blk = pl.BlockSpec((rt, N_SMALL), lambda s: (nc - 1 - s, 0))
    vec = pl.BlockSpec((1, N_SMALL), lambda s: (0, 0))
    return pl.pallas_call(
        body, name="small_bwd",
        out_shape=(jax.ShapeDtypeStruct((t, N_SMALL), BF16), jax.ShapeDtypeStruct((1, N_SMALL), F32)),
        grid=(nc,),
        in_specs=[blk, blk, vec],
        out_specs=(blk, vec),
        scratch_shapes=[pltpu.VMEM((1, N_SMALL), F32)],
        compiler_params=_cp("arbitrary"),
    )(dsm, small, bias_row)


_CONV_TC = 1024
_XBC_BLK = C_XBC // _CONV_TC


def _shift_down(cur, prev8, j):
    rc = pltpu.roll(cur, j, 0)
    rid = _iota(prev8.shape, 0)
    top = jnp.where(rid < j, pltpu.roll(prev8, j, 0), rc[0:8, :])
    return top if cur.shape[0] == 8 else jnp.concatenate([top, rc[8:, :]], axis=0)


def _shift_up(cur, next8, j):
    n = cur.shape[0]
    ru = pltpu.roll(cur, n - j, 0)
    rid = _iota(next8.shape, 0)
    bot = jnp.where(rid >= 8 - j, pltpu.roll(next8, 8 - j, 0), ru[n - 8:, :])
    return jnp.concatenate([ru[:n - 8, :], bot], axis=0)


def _conv_taps(cur, prev, w, b):
    taps = [cur] + [_shift_down(cur, prev, j) for j in (1, 2, 3)]
    acc = b + taps[0] * w[3:4, :]
    for j in (1, 2, 3):
        acc = acc + taps[j] * w[3 - j:4 - j, :]
    return acc, taps


def _conv_pre(x_ref, p_ref, w_ref, b_ref, i):
    return _conv_taps(x_ref[...], jnp.where(i > 0, p_ref[...], 0.0), w_ref[...], b_ref[...])


def _dsilu(d, acc):
    sg = _sigmoid(acc)
    return d * sg * (1.0 + acc * (1.0 - sg))


def _conv_fwd(proj, conv_w, conv_b):
    t = proj.shape[0]
    tr = _row_tile(t)

    def body(x_ref, p_ref, w_ref, b_ref, o_ref):
        i = pl.program_id(0)
        acc, _ = _conv_pre(x_ref, p_ref, w_ref, b_ref, i)
        valid = (i * tr + _iota(acc.shape, 0)) >= PADF
        o_ref[...] = jnp.where(valid, acc * _sigmoid(acc), 0.0)

    return pl.pallas_call(
        body, name="conv_fwd",
        out_shape=jax.ShapeDtypeStruct((t, CONV_DIM), F32),
        grid=(t // tr, CONV_DIM // _CONV_TC),
        in_specs=[pl.BlockSpec((tr, _CONV_TC), lambda i, j: (i, _XBC_BLK + j)),
                  pl.BlockSpec((8, _CONV_TC), lambda i, j: (jnp.maximum(i * (tr // 8) - 1, 0), _XBC_BLK + j)),
                  pl.BlockSpec((CONV_K, _CONV_TC), lambda i, j: (0, j)),
                  pl.BlockSpec((1, _CONV_TC), lambda i, j: (0, j))],
        out_specs=pl.BlockSpec((tr, _CONV_TC), lambda i, j: (i, j)),
        compiler_params=_cp("parallel", "parallel"),
    )(proj, proj, conv_w, conv_b)


def _conv_bwd(dxbc, proj, conv_w, conv_b):
    t = proj.shape[0]
    tr = _row_tile(t)
    n_tiles = t // tr
    last8 = t // 8 - 1

    def body(d_ref, dn_ref, x_ref, p_ref, xn_ref, w_ref, b_ref, dx_ref, dw_ref, db_ref):
        i = pl.program_id(1)
        w = w_ref[...]
        b = b_ref[...]
        cur = x_ref[...]
        acc, taps = _conv_taps(cur, jnp.where(i > 0, p_ref[...], 0.0), w, b)
        valid = (i * tr + _iota(acc.shape, 0)) >= PADF
        da = jnp.where(valid, _dsilu(d_ref[...], acc), 0.0)
        acc_n, _ = _conv_taps(xn_ref[...], cur[tr - 8:, :], w, b)
        da_n = jnp.where(i < n_tiles - 1, _dsilu(dn_ref[...], acc_n), 0.0)
        dx = da * w[3:4, :]
        for j in (1, 2, 3):
            dx = dx + _shift_up(da, da_n, j) * w[3 - j:4 - j, :]
        dx_ref[...] = dx.astype(BF16)
        dw = jnp.concatenate([jnp.sum(da * taps[3 - k], axis=0, keepdims=True) for k in range(CONV_K)], axis=0)
        db = jnp.sum(da, axis=0, keepdims=True)

        @pl.when(i == 0)
        def _():
            dw_ref[...] = dw
            db_ref[...] = db

        @pl.when(i > 0)
        def _():
            dw_ref[...] += dw
            db_ref[...] += db

    nxt8 = lambda i: jnp.minimum((i + 1) * (tr // 8), last8)
    return pl.pallas_call(
        body, name="conv_bwd",
        out_shape=(jax.ShapeDtypeStruct((t, CONV_DIM), BF16),
                   jax.ShapeDtypeStruct((CONV_K, CONV_DIM), F32),
                   jax.ShapeDtypeStruct((1, CONV_DIM), F32)),
        grid=(CONV_DIM // _CONV_TC, n_tiles),
        in_specs=[pl.BlockSpec((tr, _CONV_TC), lambda j, i: (i, j)),
                  pl.BlockSpec((8, _CONV_TC), lambda j, i: (nxt8(i), j)),
                  pl.BlockSpec((tr, _CONV_TC), lambda j, i: (i, _XBC_BLK + j)),
                  pl.BlockSpec((8, _CONV_TC), lambda j, i: (jnp.maximum(i * (tr // 8) - 1, 0), _XBC_BLK + j)),
                  pl.BlockSpec((8, _CONV_TC), lambda j, i: (nxt8(i), _XBC_BLK + j)),
                  pl.BlockSpec((CONV_K, _CONV_TC), lambda j, i: (0, j)),
                  pl.BlockSpec((1, _CONV_TC), lambda j, i: (0, j))],
        out_specs=(pl.BlockSpec((tr, _CONV_TC), lambda j, i: (i, j)),
                   pl.BlockSpec((CONV_K, _CONV_TC), lambda j, i: (0, j)),
                   pl.BlockSpec((1, _CONV_TC), lambda j, i: (0, j))),
        compiler_params=_cp("parallel", "arbitrary"),
    )(dxbc, dxbc, proj, proj, proj, conv_w, conv_b)


_GW = D_SSD // G_SSD


def _ssd_prelude(dt_ref, a_ref, e_scr, es_scr, dte_scr):
    r0 = _iota((CHUNK, CHUNK), 0)
    r1 = _iota((CHUNK, CHUNK), 1)
    dt = jnp.where(r1 < H_SSD, dt_ref[...], 0.0)
    adt = dt * a_ref[...]
    acs = _dot_exact((r0 >= r1).astype(F32), adt)
    acs_t = acs.T
    alast = acs[CHUNK - 1:CHUNK, :]
    exp_a = jnp.exp(acs)
    dec_s = jnp.exp(alast - acs)
    lo = r1 < 64
    for j in range(H_SSD // 2):
        sl = slice(CHUNK * j, CHUNK * (j + 1))
        e_scr[:, sl] = jnp.where(lo, exp_a[:, 2 * j:2 * j + 1], exp_a[:, 2 * j + 1:2 * j + 2])
        es_scr[:, sl] = jnp.where(lo, dec_s[:, 2 * j:2 * j + 1], dec_s[:, 2 * j + 1:2 * j + 2])
        dte_scr[:, sl] = jnp.where(lo, dt[:, 2 * j:2 * j + 1], dt[:, 2 * j + 1:2 * j + 2])
    return dt, acs, acs_t, r0, r1, lo


def _chunk_decay_rows(acs_t, g):
    cd_t = jnp.exp(acs_t[:, CHUNK - 1:CHUNK])
    return jnp.concatenate(
        [jnp.broadcast_to(cd_t[8 * g + hh:8 * g + hh + 1, :], (64, N_STATE)) for hh in range(8)], axis=0)


def _ssd_fwd(xbc, dtlf, a_row, dsk_row):
    t = xbc.shape[0]
    nc = t // CHUNK

    def body(xs_ref, b_ref, c_ref, dt_ref, a_ref, dsk_ref, y_ref, hin_ref, h_scr, e_scr, es_scr, dte_scr):
        c = pl.program_id(0)

        @pl.when(c == 0)
        def _():
            h_scr[...] = jnp.zeros_like(h_scr)

        dt, acs, acs_t, r0, r1, lo = _ssd_prelude(dt_ref, a_ref, e_scr, es_scr, dte_scr)
        causal = r0 >= r1
        for g in range(G_SSD):
            gs = slice(_GW * g, _GW * (g + 1))
            bg = b_ref[:, N_STATE * g:N_STATE * (g + 1)].astype(BF16)
            cg = c_ref[:, N_STATE * g:N_STATE * (g + 1)].astype(BF16)
            cb = _dot(cg, bg, _NT)
            hg = h_scr[gs, :]
            hin_ref[0, gs, :] = hg
            xg = xs_ref[:, gs] * dte_scr[:, gs]
            yoff = _dot(cg, hg.astype(BF16), _NT) * e_scr[:, gs]
            st = _dot((xg * es_scr[:, gs]).astype(BF16), bg, _TN)
            h_scr[gs, :] = hg * _chunk_decay_rows(acs_t, g) + st
            for jj in range(4):
                j = 4 * g + jj
                sl = slice(CHUNK * j, CHUNK * (j + 1))
                xp = xg[:, CHUNK * jj:CHUNK * (jj + 1)]
                acc = yoff[:, CHUNK * jj:CHUNK * (jj + 1)] + dsk_ref[:, sl] * xs_ref[:, sl]
                ms, xhs = [], []
                for hh in range(2):
                    h = 2 * j + hh
                    seg = acs[:, h:h + 1] - acs_t[h:h + 1, :]
                    lm = jnp.exp(jnp.where(causal, seg, -1e30))
                    ms.append((cb * lm).astype(BF16))
                    xhs.append(jnp.where(lo if hh == 0 else ~lo, xp, 0.0).astype(BF16))
                acc = acc + _dot(jnp.concatenate(ms, axis=1), jnp.concatenate(xhs, axis=0))
                y_ref[:, sl] = acc

    return pl.pallas_call(
        body, name="ssd_fwd",
        out_shape=(jax.ShapeDtypeStruct((t, D_SSD), F32), jax.ShapeDtypeStruct((nc, D_SSD, N_STATE), F32)),
        grid=(nc,),
        in_specs=[pl.BlockSpec((CHUNK, D_SSD), lambda c: (c, 0)),
                  pl.BlockSpec((CHUNK, _GW), lambda c: (c, 4)),
                  pl.BlockSpec((CHUNK, _GW), lambda c: (c, 5)),
                  pl.BlockSpec((CHUNK, N_SMALL), lambda c: (c, 0)),
                  pl.BlockSpec((1, N_SMALL), lambda c: (0, 0)),
                  pl.BlockSpec((1, D_SSD), lambda c: (0, 0))],
        out_specs=(pl.BlockSpec((CHUNK, D_SSD), lambda c: (c, 0)),
                   pl.BlockSpec((1, D_SSD, N_STATE), lambda c: (c, 0, 0))),
        scratch_shapes=[pltpu.VMEM((D_SSD, N_STATE), F32)] + [pltpu.VMEM((CHUNK, D_SSD), F32)] * 3,
        compiler_params=_cp("arbitrary"),
    )(xbc, xbc, xbc, dtlf, a_row, dsk_row)


def _ssd_bwd(xbc, dtlf, a_row, dsk_row, hin, dy):
    t = xbc.shape[0]
    nc = t // CHUNK

    def body(xs_ref, b_ref, c_ref, dt_ref, a_ref, dsk_ref, hin_ref, dy_ref,
             dxbc_ref, ddt_ref, da_ref, ddsk_ref, dh_scr, e_scr, es_scr, dte_scr, dx_scr, whi_scr, wlo_scr):
        step = pl.program_id(0)

        @pl.when(step == 0)
        def _():
            dh_scr[...] = jnp.zeros_like(dh_scr)
            da_ref[...] = jnp.zeros_like(da_ref)
            ddsk_ref[...] = jnp.zeros_like(ddsk_ref)

        dt, acs, acs_t, r0, r1, lo = _ssd_prelude(dt_ref, a_ref, e_scr, es_scr, dte_scr)
        causal = r0 >= r1
        lane_row = _iota((1, CHUNK), 1)
        dacs = jnp.zeros((CHUNK, CHUNK), F32)
        dacs_t = jnp.zeros((CHUNK, CHUNK), F32)
        dalast = jnp.zeros((1, CHUNK), F32)
        ddt_dir = jnp.zeros((CHUNK, CHUNK), F32)
        ddsk_ref[...] += jnp.sum(dy_ref[...] * xs_ref[...], axis=0, keepdims=True)

        def head_sums(z, pick):
            hi = z.astype(BF16)
            return _dot(hi, pick) + _dot((z - hi.astype(F32)).astype(BF16), pick)

        for g in range(G_SSD):
            gs = slice(_GW * g, _GW * (g + 1))
            pick = (jnp.right_shift(_iota((_GW, CHUNK), 0), 6) + 8 * g == _iota((_GW, CHUNK), 1)).astype(BF16)
            bg = b_ref[:, N_STATE * g:N_STATE * (g + 1)].astype(BF16)
            cg = c_ref[:, N_STATE * g:N_STATE * (g + 1)].astype(BF16)
            cb = _dot(cg, bg, _NT)
            hg = hin_ref[0, gs, :]
            hgb = hg.astype(BF16)
            dhn = dh_scr[gs, :]
            dhnb = dhn.astype(BF16)
            esg = es_scr[:, gs]
            dyg = dy_ref[:, gs]
            xsg = xs_ref[:, gs]
            xg = xsg * dte_scr[:, gs]
            dyeb = (dyg * e_scr[:, gs]).astype(BF16)
            dc = _dot(dyeb, hgb)
            dh_y = _dot(dyeb, cg, _TN)
            dxs = _dot(bg, dhnb, _NT) * esg
            db = _dot((xg * esg).astype(BF16), dhnb)
            cd = _chunk_decay_rows(acs_t, g)
            dh_scr[gs, :] = dhn * cd + dh_y
            end_state = head_sums(jnp.broadcast_to(jnp.sum(xg * dxs, axis=0, keepdims=True), (8, _GW)), pick)[0:1, :]
            carried = dhn * hg * cd
            per_head = jnp.concatenate([jnp.sum(carried[64 * hh:64 * hh + 64, :], axis=0, keepdims=True)
                                        for hh in range(8)], axis=0)
            per_head = jnp.sum(per_head, axis=1, keepdims=True)
            for hh in range(8):
                end_state = end_state + jnp.where(lane_row == 8 * g + hh, per_head[hh:hh + 1, :], 0.0)
            dalast = dalast + end_state
            dcb = jnp.zeros((CHUNK, CHUNK), F32)
            for jj in range(4):
                j = 4 * g + jj
                sl = slice(CHUNK * j, CHUNK * (j + 1))
                ps = slice(CHUNK * jj, CHUNK * (jj + 1))
                xpb = xg[:, ps].astype(BF16)
                dyp = dyg[:, ps]
                mfs, dyhs = [], []
                for hh in range(2):
                    h = 2 * j + hh
                    ws = slice(CHUNK * (2 * jj + hh), CHUNK * (2 * jj + hh + 1))
                    seg = acs[:, h:h + 1] - acs_t[h:h + 1, :]
                    lm = jnp.exp(jnp.where(causal, seg, -1e30))
                    mf = cb * lm
                    dyh = jnp.where(lo if hh == 0 else ~lo, dyp, 0.0).astype(BF16)
                    gm = _dot(dyh, xpb, _NT)
                    dcb = dcb + gm * lm
                    w = gm * mf
                    whi = w.astype(BF16)
                    whi_scr[:, ws] = whi
                    wlo_scr[:, ws] = (w - whi.astype(F32)).astype(BF16)
                    dacs_t = dacs_t - jnp.where(r0 == h, jnp.sum(w, axis=0, keepdims=True), 0.0)
                    mfs.append(mf.astype(BF16))
                    dyhs.append(dyh)
                dx_scr[:, sl] = dxs[:, ps] + _dot(jnp.concatenate(mfs, axis=0), jnp.concatenate(dyhs, axis=0), _TN)
            dxg = dx_scr[:, gs]
            pick_w = (jnp.right_shift(_iota((8 * CHUNK, CHUNK), 0), 7) + 8 * g == _iota((8 * CHUNK, CHUNK), 1)).astype(BF16)
            ch = _dot(cg, hgb, _NT)
            dacs = (dacs + _dot(whi_scr[...], pick_w) + _dot(wlo_scr[...], pick_w)
                    + head_sums(dyg * e_scr[:, gs] * ch - xg * dxs, pick))
            ddt_dir = ddt_dir + head_sums(dxg * xsg, pick)
            dcbb = dcb.astype(BF16)
            dxbc_ref[:, D_SSD + N_STATE * g:D_SSD + N_STATE * (g + 1)] = db + _dot(dcbb, cg, _TN)
            dxbc_ref[:, D_SSD + _GW + N_STATE * g:D_SSD + _GW + N_STATE * (g + 1)] = dc + _dot(dcbb, bg)
        dxbc_ref[:, 0:D_SSD] = dx_scr[...] * dte_scr[...] + dsk_ref[...] * dy_ref[...]
        dacs = dacs + dacs_t.T + jnp.where(r0 == CHUNK - 1, dalast, 0.0)
        dadt = _dot_exact((r1 >= r0).astype(F32), dacs)
        ddt_ref[...] = dadt * a_ref[...] + ddt_dir
        da_ref[...] += jnp.sum(dadt * dt, axis=0, keepdims=True)

    rev = lambda s: (nc - 1 - s, 0)
    return pl.pallas_call(
        body, name="ssd_bwd",
        out_shape=(jax.ShapeDtypeStruct((t, CONV_DIM), F32), jax.ShapeDtypeStruct((t, N_SMALL), F32),
                   jax.ShapeDtypeStruct((1, N_SMALL), F32), jax.ShapeDtypeStruct((1, D_SSD), F32)),
        grid=(nc,),
        in_specs=[pl.BlockSpec((CHUNK, D_SSD), rev),
                  pl.BlockSpec((CHUNK, _GW), lambda s: (nc - 1 - s, 4)),
                  pl.BlockSpec((CHUNK, _GW), lambda s: (nc - 1 - s, 5)),
                  pl.BlockSpec((CHUNK, N_SMALL), rev),
                  pl.BlockSpec((1, N_SMALL), lambda s: (0, 0)),
                  pl.BlockSpec((1, D_SSD), lambda s: (0, 0)),
                  pl.BlockSpec((1, D_SSD, N_STATE), lambda s: (nc - 1 - s, 0, 0)),
                  pl.BlockSpec((CHUNK, D_SSD), rev)],
        out_specs=(pl.BlockSpec((CHUNK, CONV_DIM), rev),
                   pl.BlockSpec((CHUNK, N_SMALL), rev),
                   pl.BlockSpec((1, N_SMALL), lambda s: (0, 0)),
                   pl.BlockSpec((1, D_SSD), lambda s: (0, 0))),
        scratch_shapes=([pltpu.VMEM((D_SSD, N_STATE), F32)] + [pltpu.VMEM((CHUNK, D_SSD), F32)] * 4
                        + [pltpu.VMEM((CHUNK, 8 * CHUNK), BF16)] * 2),
        compiler_params=_cp("arbitrary"),
    )(xbc, xbc, xbc, dtlf, a_row, dsk_row, hin, dy)


_NPAIR = H_ATT // 2
_QB, _KB, _VB = C_Q // 128, C_K // 128, C_V // 128
_SCALE = 1.0 / math.sqrt(64.0)
_LOG2E = math.log2(math.e)


def _attn_blocks(t):
    return _tile(t, (1408, 384, 256, 128)), _tile(t, (384, 128))


def _split3(c):
    hi = c.astype(BF16).astype(F32)
    rest = c - hi
    mid = rest.astype(BF16).astype(F32)
    return hi, mid, rest - mid


def _head_lanes(lane, hh):
    return (lane < 64, 64) if hh == 0 else (lane >= 64, 0)


def _q_operand(q, cq, lane, hh):
    sel, first = _head_lanes(lane, hh)
    out = jnp.where(sel, q, 0.0)
    for n, col in enumerate(_split3(cq) + (1.0, 1.0, 1.0)):
        out = jnp.where(lane == first + n, col, out)
    return out.astype(BF16)


def _k_operand(k, ck, lane, hh):
    sel, first = _head_lanes(lane, hh)
    hi, mid, lo = _split3(ck)
    out = jnp.where(sel, k, 0.0)
    for n, col in enumerate((1.0, 1.0, 1.0, -hi, -mid, -lo)):
        out = jnp.where(lane == first + n, col, out)
    return out.astype(BF16)


def _sum_operand(x, lane, hh, at):
    sel, first = _head_lanes(lane, hh)
    return jnp.where(sel, x, jnp.where(lane == first + at, 1.0, 0.0)).astype(BF16)


_C_FILLER = 2.0 ** 30
_SKIP_STEP = 256


def _query_skips(bq):
    firsts = list(range(0, bq, _SKIP_STEP))
    far = 1 << 30
    return [(q0 if n else -far, firsts[n + 1] if n + 1 < len(firsts) else far, q0) for n, q0 in enumerate(firsts)]


def _attn_fwd(proj, c_col):
    t = proj.shape[0]
    bq, bk = _attn_blocks(t)
    nq, nk = t // bq, t // bk
    rs = 32

    def last_kv(i):
        return (i * bq + bq - 1) // bk

    def body(q_ref, k_ref, v_ref, cq_ref, ck_ref, o_ref, lse_ref, p_ref, mrun_ref, qs_scr, s_scr, m_scr, acc_scr):
        i = pl.program_id(1)
        kk = pl.program_id(2)
        lane_q = _iota((bq, 128), 1)

        @pl.when(kk == 0)
        def _():
            m_scr[...] = jnp.full_like(m_scr, -1e30)
            acc_scr[...] = jnp.zeros_like(acc_scr)
            q = q_ref[...] * (_SCALE * _LOG2E)
            cq = cq_ref[0] * _LOG2E
            for hh in range(2):
                qs_scr[hh] = _q_operand(q, cq[:, hh:hh + 1], lane_q, hh)

        def step(masked, q0):
            nqc = bq - q0
            lane_k = _iota((bk, 128), 1)
            k = k_ref[...]
            v = v_ref[...]
            ck = ck_ref[0] * _LOG2E
            ahead = _iota((rs, nqc), 0) - _iota((rs, nqc), 1) - q0
            vss = []
            for hh in range(2):
                sel, first = _head_lanes(lane_k, hh)
                ks = _k_operand(k, ck[:, hh:hh + 1], lane_k, hh)
                vss.append(jnp.where(sel, v, jnp.where(lane_k == first, 1.0, 0.0)).astype(BF16))
                s_scr[hh, :, q0:] = _dot(ks, qs_scr[hh, q0:, :], _NT)
            for hh in range(2):
                vs = vss[hh]

                def block_max(r, mx):
                    rows = pl.ds(pl.multiple_of(r * rs, rs), rs)
                    s = s_scr[hh, rows, q0:]
                    if masked:
                        s = jnp.where(ahead <= i * bq - kk * bk - r * rs, s, -1e30)
                        s_scr[hh, rows, q0:] = s
                    return jnp.maximum(mx, s)

                mx = lax.fori_loop(0, bk // rs, block_max, jnp.full((rs, nqc), -1e30, F32), unroll=True)
                m_old = m_scr[hh, :, q0:]
                m_new = jnp.maximum(m_old, jnp.max(mx, axis=0, keepdims=True))
                m_scr[hh, :, q0:] = m_new
                mrun_ref[0, hh:hh + 1, q0:] = m_new

                def probs(r, carry):
                    rows = pl.ds(pl.multiple_of(r * rs, rs), rs)
                    p_ref[0, hh, rows, q0:] = jnp.exp2(s_scr[hh, rows, q0:] - m_new).astype(BF16)
                    return carry

                lax.fori_loop(0, bk // rs, probs, 0, unroll=True)
                acc_scr[hh, :, q0:] = (acc_scr[hh, :, q0:] * jnp.exp2(m_old - m_new)
                                       + _dot(vs, p_ref[0, hh, :, q0:], _TN))

        active = kk <= last_kv(i)
        ahead_by = kk * bk - i * bq
        for lo, hi, q0 in _query_skips(bq):
            @pl.when(active & (ahead_by + bk - 1 > 0) & (ahead_by >= lo) & (ahead_by < hi))
            def _(q0=q0):
                step(True, q0)

        @pl.when(active & jnp.logical_not(ahead_by + bk - 1 > 0))
        def _():
            step(False, 0)

        @pl.when(kk == nk - 1)
        def _():
            a = acc_scr[0]
            b = acc_scr[1]
            la = a[64:65, :]
            lb = b[0:1, :]
            o_ref[...] = jnp.where(lane_q < 64, (a / la).T, (b / lb).T)
            lse_ref[0] = jnp.concatenate([m_scr[0] + jnp.log(la) * _LOG2E, m_scr[1] + jnp.log(lb) * _LOG2E], axis=0)

    kvi = lambda i, kk: jnp.minimum(kk, last_kv(i))
    kv = lambda off: pl.BlockSpec((bk, 128), lambda j, i, kk: (kvi(i, kk), off + j))
    blk = lambda j, i, kk: (j * nq + i) * nk + kvi(i, kk)
    return pl.pallas_call(
        body, name="attn_fwd",
        out_shape=(jax.ShapeDtypeStruct((t, D_ATT), F32), jax.ShapeDtypeStruct((_NPAIR, 2, t), F32),
                   jax.ShapeDtypeStruct((_NPAIR * nq * nk, 2, bk, bq), BF16),
                   jax.ShapeDtypeStruct((_NPAIR * nq * nk, 2, bq), F32)),
        grid=(_NPAIR, nq, nk),
        in_specs=[pl.BlockSpec((bq, 128), lambda j, i, kk: (i, _QB + j)),
                  kv(_KB), kv(_VB),
                  pl.BlockSpec((1, bq, 2), lambda j, i, kk: (j, i, 0)),
                  pl.BlockSpec((1, bk, 2), lambda j, i, kk: (j, kvi(i, kk), 0))],
        out_specs=(pl.BlockSpec((bq, 128), lambda j, i, kk: (i, j)),
                   pl.BlockSpec((1, 2, bq), lambda j, i, kk: (j, 0, i)),
                   pl.BlockSpec((1, 2, bk, bq), lambda j, i, kk: (blk(j, i, kk), 0, 0, 0)),
                   pl.BlockSpec((1, 2, bq), lambda j, i, kk: (blk(j, i, kk), 0, 0))),
        scratch_shapes=[pltpu.VMEM((2, bq, 128), BF16), pltpu.VMEM((2, bk, bq), F32),
                        pltpu.VMEM((2, 1, bq), F32), pltpu.VMEM((2, 128, bq), F32)],
        compiler_params=_cp("parallel", "parallel", "arbitrary"),
    )(proj, proj, proj, c_col, c_col)


def _attn_bwd(proj, lse_row, dl_row, do, p_blocks, m_run, exchange=None):
    t = proj.shape[0]
    bq, bk = _attn_blocks(t)
    nq, nk = t // bq, t // bk
    rs = 16

    def first_q(kk):
        return (kk * bk) // bq

    def body(q_ref, k_ref, v_ref, lse_ref, dl_ref, do_ref, pblk_ref, mrun_ref,
             dq_ref, dk_ref, dv_ref, dck_ref, dcq_ref,
             qs_scr, doh_scr, ks_scr, dp_scr, p_scr, ds_scr, dq_scr, dk_scr, dv_scr):
        kk = pl.program_id(1)
        i = pl.program_id(2)
        lane_q = _iota((bq, 128), 1)
        lane_k = _iota((bk, 128), 1)
        qrows = pl.ds(pl.multiple_of(i * bq, 128), bq)

        @pl.when(kk == 0)
        def _():
            q = q_ref[...] * _SCALE
            do_ = do_ref[...]
            for hh in range(2):
                qs_scr[hh, qrows, :] = _sum_operand(q, lane_q, hh, 3)
                doh_scr[hh, qrows, :] = jnp.where(_head_lanes(lane_q, hh)[0], do_, 0.0).astype(BF16)
                dq_scr[hh, i] = jnp.zeros((128, bq), F32)

        @pl.when(i == 0)
        def _():
            dk_scr[...] = jnp.zeros_like(dk_scr)
            dv_scr[...] = jnp.zeros_like(dv_scr)
            k = k_ref[...]
            for hh in range(2):
                ks_scr[hh] = _sum_operand(k, lane_k, hh, 0)

        def step(q0):
            seen = pl.ds(pl.multiple_of(i * bq + q0, 128), bq - q0)
            v16 = v_ref[...].astype(BF16)
            dl = dl_ref[0, :, q0:]
            rescale = jnp.exp2(mrun_ref[0, :, q0:] - lse_ref[0, :, q0:])
            for hh in range(2):
                dp_scr[hh, :, q0:] = _dot(v16, doh_scr[hh, seen, :], _NT)
            for hh in range(2):
                qs = qs_scr[hh, seen, :]
                doh = doh_scr[hh, seen, :]

                def strip(r, carry):
                    rows = pl.ds(pl.multiple_of(r * rs, rs), rs)
                    p = pblk_ref[0, hh, rows, q0:].astype(F32) * rescale[hh:hh + 1, :]
                    p_scr[hh, rows, q0:] = p.astype(BF16)
                    ds_scr[hh, rows, q0:] = (p * (dp_scr[hh, rows, q0:] - dl[hh:hh + 1, :])).astype(BF16)
                    return carry

                lax.fori_loop(0, bk // rs, strip, 0, unroll=True)
                dv_scr[...] += _dot(p_scr[hh, :, q0:], doh)
                dk_scr[hh] += _dot(ds_scr[hh, :, q0:], qs)
                dq_scr[hh, i, :, q0:] += _dot(ks_scr[hh], ds_scr[hh, :, q0:], _TN)

        ahead_by = kk * bk - i * bq
        for lo, hi, q0 in _query_skips(bq):
            @pl.when((i >= first_q(kk)) & (ahead_by >= lo) & (ahead_by < hi))
            def _(q0=q0):
                step(q0)

        @pl.when(i == nq - 1)
        def _():
            dka = dk_scr[0]
            dkb = dk_scr[1]
            dk_ref[...] = jnp.where(lane_k < 64, dka, dkb).astype(BF16)
            dv_ref[...] = dv_scr[...].astype(BF16)
            dck_ref[0] = -jnp.where(_iota((bk, 2), 1) == 0, dka[:, 67:68], dkb[:, 3:4])

        @pl.when((kk == nk - 1) & (i == nq - 1))
        def _():
            for ii in range(nq):
                cols = slice(ii * bq, (ii + 1) * bq)
                dqa = dq_scr[0, ii]
                dqb = dq_scr[1, ii]
                dq_ref[cols, :] = (jnp.where(lane_q < 64, dqa.T, dqb.T) * _SCALE).astype(BF16)
                dcq_ref[0, :, cols] = jnp.concatenate([dqa[64:65, :], dqb[0:1, :]], axis=0)

    qi = lambda kk, i: jnp.where(kk == 0, i, nq - 1)
    qspec = lambda off: pl.BlockSpec((bq, 128), lambda j, kk, i: (qi(kk, i), off + j))
    kspec = lambda off: pl.BlockSpec((bk, 128), lambda j, kk, i: (kk, off + j))
    rowspec = pl.BlockSpec((1, 2, bq), lambda j, kk, i: (j, 0, jnp.maximum(i, first_q(kk))))
    blk = lambda j, kk, i: (j * nq + jnp.maximum(i, first_q(kk))) * nk + kk
    return _hosted_call(
        body, name="attn_bwd",
        out_shape=(jax.ShapeDtypeStruct((t, D_ATT), BF16), jax.ShapeDtypeStruct((t, D_ATT), BF16),
                   jax.ShapeDtypeStruct((t, D_ATT), BF16), jax.ShapeDtypeStruct((_NPAIR, t, 2), F32),
                   jax.ShapeDtypeStruct((_NPAIR, 2, t), F32)),
        grid=(_NPAIR, nk, nq),
        in_specs=[qspec(_QB), kspec(_KB), kspec(_VB),
                  rowspec, rowspec, qspec(0),
                  pl.BlockSpec((1, 2, bk, bq), lambda j, kk, i: (blk(j, kk, i), 0, 0, 0)),
                  pl.BlockSpec((1, 2, bq), lambda j, kk, i: (blk(j, kk, i), 0, 0))],
        out_specs=(pl.BlockSpec((t, 128), lambda j, kk, i: (0, j)),
                   pl.BlockSpec((bk, 128), lambda j, kk, i: (kk, j)),
                   pl.BlockSpec((bk, 128), lambda j, kk, i: (kk, j)),
                   pl.BlockSpec((1, bk, 2), lambda j, kk, i: (j, kk, 0)),
                   pl.BlockSpec((1, 2, t), lambda j, kk, i: (j, 0, 0))),
        scratch_shapes=[pltpu.VMEM((2, t, 128), BF16), pltpu.VMEM((2, t, 128), BF16), pltpu.VMEM((2, bk, 128), BF16),
                        pltpu.VMEM((2, bk, bq), F32),
                        pltpu.VMEM((2, bk, bq), BF16), pltpu.VMEM((2, bk, bq), BF16),
                        pltpu.VMEM((2, nq, 128, bq), F32), pltpu.VMEM((2, bk, 128), F32), pltpu.VMEM((bk, 128), F32)],
        operands=(proj, proj, proj, lse_row, dl_row, do, p_blocks, m_run),
        semantics=("parallel", "arbitrary", "arbitrary"), exchange=exchange)


def _premerge_fwd(y, o, proj, gamma):
    t = y.shape[0]
    tm = _row_tile_wide(t)

    def body(y_ref, z_ref, o_ref, za_ref, g_ref, ys_ref, ya_ref):
        z = z_ref[...]
        u = y_ref[...] * (z * _sigmoid(z))
        for g in range(G_SSD):
            gs = slice(_GW * g, _GW * (g + 1))
            ug = u[:, gs]
            r = lax.rsqrt(jnp.mean(ug * ug, axis=-1, keepdims=True) + EPS)
            ys_ref[:, gs] = (ug * r * g_ref[:, gs]).astype(BF16)
        za = za_ref[...]
        ya_ref[...] = (o_ref[...] * (za * _sigmoid(za))).astype(BF16)

    return pl.pallas_call(
        body, name="premerge_fwd",
        out_shape=(jax.ShapeDtypeStruct((t, D_SSD), BF16), jax.ShapeDtypeStruct((t, D_ATT), BF16)),
        grid=(t // tm,),
        in_specs=[pl.BlockSpec((tm, D_SSD), lambda i: (i, 0)),
                  pl.BlockSpec((tm, D_SSD), lambda i: (i, C_Z // D_SSD)),
                  pl.BlockSpec((tm, D_ATT), lambda i: (i, 0)),
                  pl.BlockSpec((tm, D_ATT), lambda i: (i, C_ZA // D_ATT)),
                  pl.BlockSpec((1, D_SSD), lambda i: (0, 0))],
        out_specs=(pl.BlockSpec((tm, D_SSD), lambda i: (i, 0)), pl.BlockSpec((tm, D_ATT), lambda i: (i, 0))),
        compiler_params=_cp("parallel"),
    )(y, proj, o, proj, gamma)


def _premerge_bwd(dys, dya, y, o, proj, gamma, exchange=None):
    t = y.shape[0]
    tm = _row_tile_wide(t)

    def body(dys_ref, dya_ref, y_ref, z_ref, o_ref, za_ref, g_ref, dy_ref, dz_ref, do_ref, dza_ref, dg_ref, dl_ref):
        i = pl.program_id(0)
        z = z_ref[...]
        sz = _sigmoid(z)
        silu = z * sz
        dsilu = sz * (1.0 + z * (1.0 - sz))
        yv = y_ref[...]
        u = yv * silu
        parts = []
        for g in range(G_SSD):
            gs = slice(_GW * g, _GW * (g + 1))
            ug = u[:, gs]
            r = lax.rsqrt(jnp.mean(ug * ug, axis=-1, keepdims=True) + EPS)
            n = ug * r
            dout = dys_ref[:, gs]
            dn = dout * g_ref[:, gs]
            du = r * (dn - n * jnp.mean(dn * n, axis=-1, keepdims=True))
            dy_ref[:, gs] = du * silu[:, gs]
            dz_ref[:, gs] = (du * yv[:, gs] * dsilu[:, gs]).astype(BF16)
            parts.append(jnp.sum(dout * n, axis=0, keepdims=True))
        dg = jnp.concatenate(parts, axis=1)
        za = za_ref[...]
        sa = _sigmoid(za)
        dya_ = dya_ref[...]
        ov = o_ref[...]
        do = dya_ * (za * sa)
        do_ref[...] = do
        dza_ref[...] = (dya_ * ov * (sa * (1.0 + za * (1.0 - sa)))).astype(BF16)
        pick = (jnp.right_shift(_iota((D_ATT, 128), 0), 6) == _iota((D_ATT, 128), 1)).astype(F32)
        dl_ref[...] = _dot_exact(do * ov, pick)

        @pl.when(i == 0)
        def _():
            dg_ref[...] = dg

        @pl.when(i > 0)
        def _():
            dg_ref[...] += dg

    ssd = pl.BlockSpec((tm, D_SSD), lambda i: (i, 0))
    att = pl.BlockSpec((tm, D_ATT), lambda i: (i, 0))
    vec = pl.BlockSpec((1, D_SSD), lambda i: (0, 0))
    return _hosted_call(
        body, name="premerge_bwd",
        out_shape=(jax.ShapeDtypeStruct((t, D_SSD), F32), jax.ShapeDtypeStruct((t, D_SSD), BF16),
                   jax.ShapeDtypeStruct((t, D_ATT), F32), jax.ShapeDtypeStruct((t, D_ATT), BF16),
                   jax.ShapeDtypeStruct((1, D_SSD), F32), jax.ShapeDtypeStruct((t, 128), F32)),
        grid=(t // tm,),
        in_specs=[ssd, att, ssd, pl.BlockSpec((tm, D_SSD), lambda i: (i, C_Z // D_SSD)), att,
                  pl.BlockSpec((tm, D_ATT), lambda i: (i, C_ZA // D_ATT)), vec],
        out_specs=(ssd, ssd, att, att, vec, pl.BlockSpec((tm, 128), lambda i: (i, 0))),
        scratch_shapes=[],
        operands=(dys, dya, y, proj, o, proj, gamma), semantics=("arbitrary",), exchange=exchange)


_G_BLK = C_G // D_MODEL


def _merge_fwd(a, b, proj, gate_bias):
    t = a.shape[0]
    tm = _row_tile(t)

    def body(a_ref, b_ref, gs_ref, ga_ref, bias_ref, m_ref):
        g_ssd = _sigmoid(gs_ref[...] + bias_ref[:, 0:D_MODEL])
        g_att = _sigmoid(ga_ref[...] + bias_ref[:, D_MODEL:2 * D_MODEL])
        m_ref[...] = (g_ssd * a_ref[...] + g_att * b_ref[...]).astype(BF16)

    row = pl.BlockSpec((tm, D_MODEL), lambda i: (i, 0))
    return pl.pallas_call(
        body, name="merge_fwd",
        out_shape=jax.ShapeDtypeStruct((t, D_MODEL), BF16),
        grid=(t // tm,),
        in_specs=[row, row,
                  pl.BlockSpec((tm, D_MODEL), lambda i: (i, _G_BLK)),
                  pl.BlockSpec((tm, D_MODEL), lambda i: (i, _G_BLK + 1)),
                  pl.BlockSpec((1, 2 * D_MODEL), lambda i: (0, 0))],
        out_specs=row,
        compiler_params=_cp("parallel"),
    )(a, b, proj, proj, gate_bias)


def _merge_bwd(dm, a, b, proj, gate_bias):
    t = a.shape[0]
    tm = _row_tile(t)

    def body(dm_ref, a_ref, b_ref, gs_ref, ga_ref, bias_ref, da_ref, db_ref, dg_ref, dbias_ref):
        i = pl.program_id(0)
        dm_ = dm_ref[...]
        g_ssd = _sigmoid(gs_ref[...] + bias_ref[:, 0:D_MODEL])
        g_att = _sigmoid(ga_ref[...] + bias_ref[:, D_MODEL:2 * D_MODEL])
        da_ref[...] = (dm_ * g_ssd).astype(BF16)
        db_ref[...] = (dm_ * g_att).astype(BF16)
        dgs = dm_ * a_ref[...] * g_ssd * (1.0 - g_ssd)
        dga = dm_ * b_ref[...] * g_att * (1.0 - g_att)
        dg_ref[:, 0:D_MODEL] = dgs.astype(BF16)
        dg_ref[:, D_MODEL:2 * D_MODEL] = dga.astype(BF16)
        part = jnp.concatenate([jnp.sum(dgs, axis=0, keepdims=True), jnp.sum(dga, axis=0, keepdims=True)], axis=1)

        @pl.when(i == 0)
        def _():
            dbias_ref[...] = part

        @pl.when(i > 0)
        def _():
            dbias_ref[...] += part

    row = pl.BlockSpec((tm, D_MODEL), lambda i: (i, 0))
    wide = pl.BlockSpec((tm, 2 * D_MODEL), lambda i: (i, 0))
    vec = pl.BlockSpec((1, 2 * D_MODEL), lambda i: (0, 0))
    return pl.pallas_call(
        body, name="merge_bwd",
        out_shape=(jax.ShapeDtypeStruct((t, D_MODEL), BF16), jax.ShapeDtypeStruct((t, D_MODEL), BF16),
                   jax.ShapeDtypeStruct((t, 2 * D_MODEL), BF16), jax.ShapeDtypeStruct((1, 2 * D_MODEL), F32)),
        grid=(t // tm,),
        in_specs=[row, row, row,
                  pl.BlockSpec((tm, D_MODEL), lambda i: (i, _G_BLK)),
                  pl.BlockSpec((tm, D_MODEL), lambda i: (i, _G_BLK + 1)), vec],
        out_specs=(row, row, wide, vec),
        compiler_params=_cp("arbitrary"),
    )(dm, a, b, proj, proj, gate_bias)


def _post(o2, h, target, g):
    t = o2.shape[0]
    nc = t // CHUNK

    def body(o_ref, h_ref, t_ref, g_ref, dy_ref, do_ref, dg_ref, loss_ref):
        c = pl.program_id(0)
        x = o_ref[...]
        r = lax.rsqrt(jnp.mean(x * x, axis=-1, keepdims=True) + EPS)
        n = x * r
        y = h_ref[...] + n * g_ref[...]
        diff = jnp.where(c > 0, y - t_ref[...], 0.0)
        dy = diff * (1.0 / D_MODEL)
        dy_ref[...] = dy
        gdy = dy * g_ref[...]
        do_ref[...] = (r * (gdy - n * jnp.mean(gdy * n, axis=-1, keepdims=True))).astype(BF16)
        dg = jnp.sum(dy * n, axis=0, keepdims=True)
        lpart = 0.5 * jnp.sum(jnp.sum(diff * diff, axis=1, keepdims=True), axis=0, keepdims=True) * (1.0 / D_MODEL)
        sel = (_iota((8, 128), 0) == 0) & (_iota((8, 128), 1) == 0)

        @pl.when(c == 0)
        def _():
            dg_ref[...] = dg
            loss_ref[...] = jnp.zeros_like(loss_ref)

        @pl.when(c > 0)
        def _():
            dg_ref[...] += dg
            loss_ref[...] += jnp.where(sel, lpart, 0.0)

    row = pl.BlockSpec((CHUNK, D_MODEL), lambda c: (c, 0))
    vec = pl.BlockSpec((1, D_MODEL), lambda c: (0, 0))
    return pl.pallas_call(
        body, name="post",
        out_shape=(jax.ShapeDtypeStruct((t, D_MODEL), F32), jax.ShapeDtypeStruct((t, D_MODEL), BF16),
                   jax.ShapeDtypeStruct((1, D_MODEL), F32), jax.ShapeDtypeStruct((8, 128), F32)),
        grid=(nc,),
        in_specs=[row, row, pl.BlockSpec((CHUNK, D_MODEL), lambda c: (jnp.maximum(c - 1, 0), 0)), vec],
        out_specs=(row, row, vec, pl.BlockSpec((8, 128), lambda c: (0, 0))),
        compiler_params=_cp("arbitrary"),
    )(o2, h, target, g)


def _mm_tiles(t):
    return _tile(t, (704, 384, 128))


def _local_step(h, target, w_main, w_small, pr_slots, ids, norm_pre, conv_w, conv_b, bias_row, a_row,
                dsk_row, ssd_norm, gate_bias, norm_post):
    t = h.shape[0]
    tm = _mm_tiles(t)
    u = _norm1_fwd(h, norm_pre)
    proj, pr_slots = _matmul(u, w_main, "nt", F32, "inproj", tm, 1024, D_MODEL,
                             exchange=_gather_stage([pr_slots], to_sibling=False))
    small, pr_slots = _matmul(u, w_small, "nt", F32, "inproj_small", tm, N_SMALL, D_MODEL,
                              exchange=_gather_stage([pr_slots], to_sibling=True))
    wps = pr_slots[:, 0:512].reshape(D_SSD, D_MODEL)
    wpa = pr_slots[:, 512:768].reshape(D_ATT, D_MODEL)
    wout = pr_slots[:, 768:1024].reshape(D_MODEL, D_MODEL)
    dtlf = _small_fwd(small, bias_row)
    xbc = _conv_fwd(proj, conv_w, conv_b)
    y, hin = _ssd_fwd(xbc, dtlf, a_row, dsk_row)
    c_tok = dtlf[:, H_SSD:H_SSD + H_ATT]
    c_tok = jnp.where(jnp.arange(t)[:, None] < PADF, _C_FILLER, c_tok)
    c_col = c_tok.reshape(t, _NPAIR, 2).transpose(1, 0, 2)
    o, lse, p_blocks, m_run = _attn_fwd(proj, c_col)
    ys, ya = _premerge_fwd(y, o, proj, ssd_norm)
    a = _matmul(ys, wps, "nn", F32, "proj_ssd", tm, D_MODEL, D_SSD)
    b = _matmul(ya, wpa, "nn", F32, "proj_att", tm, D_MODEL, D_ATT)
    merged = _merge_fwd(a, b, proj, gate_bias)
    o2 = _matmul(merged, wout, "nn", F32, "out_proj", tm, D_MODEL, D_MODEL)
    dy_out, do2, d_norm_post, loss_blk = _post(o2, h, target, norm_post)

    dm = _matmul(do2, wout, "nt", F32, "out_proj_dx", tm, D_MODEL, D_MODEL)
    d_wout = _matmul(merged, do2, "tn", F32, "out_proj_dw", D_MODEL, D_MODEL, tm)
    da, db, dgraw, d_gate_bias = _merge_bwd(dm, a, b, proj, gate_bias)
    dys = _matmul(da, wps, "nt", F32, "proj_ssd_dx", tm, D_SSD, D_MODEL)
    d_wps = _matmul(ys, da, "tn", F32, "proj_ssd_dw", D_SSD, D_MODEL, tm)
    dya = _matmul(db, wpa, "nt", F32, "proj_att_dx", tm, D_ATT, D_MODEL)
    d_wpa = _matmul(ya, db, "tn", F32, "proj_att_dw", D_ATT, D_MODEL, tm)
    g32_pr = jnp.concatenate([d_wps.reshape(4, 512, D_MODEL), d_wpa.reshape(4, 256, D_MODEL),
                              d_wout.reshape(4, 256, D_MODEL)], axis=1)
    dy, dz, do, dza, d_ssd_norm, dl, ra_pr = _premerge_bwd(dys, dya, y, o, proj, ssd_norm,
                                                           exchange=_pair_swap([g32_pr]))
    pb_pr = _add_pair(ids, g32_pr, ra_pr)
    dl_row = dl[:, 0:H_ATT].T.reshape(_NPAIR, 2, t)
    dq, dk, dv, dc_key, dc_qry, rb_pr = _attn_bwd(proj, lse, dl_row, do, p_blocks, m_run,
                                                  exchange=_chip_exchange([pb_pr]))
    half_pr = _add_chips(ids, g32_pr, ra_pr, rb_pr)
    dxbc, ddt, d_a, d_dsk = _ssd_bwd(xbc, dtlf, a_row, dsk_row, hin, dy)
    dxbc_raw, d_conv_w, d_conv_b = _conv_bwd(dxbc, proj, conv_w, conv_b)
    dc_tok = jnp.transpose(dc_key, (1, 0, 2)).reshape(t, H_ATT) + dc_qry.reshape(H_ATT, t).T
    dsm = ddt + jnp.pad(dc_tok, ((0, 0), (H_SSD, N_SMALL - H_SSD - H_ATT)))
    dsmall, d_bias_row = _small_bwd(dsm, small, bias_row)
    dproj = [dz, dxbc_raw, dza, dq, dk, dv, dgraw]
    return dict(loss_blk=loss_blk, u=u, dy_out=dy_out, dproj=dproj, dsmall=dsmall, half_pr=half_pr,
                d_conv_w=d_conv_w, d_conv_b=d_conv_b,
                d_bias_row=d_bias_row, d_a=d_a, d_dsk=d_dsk, d_ssd_norm=d_ssd_norm,
                d_gate_bias=d_gate_bias, d_norm_post=d_norm_post)


def _to_aligned_rows(slots):
    w = slots.reshape(N_COLS, slots.shape[2])

    def cut(o):
        return w[o[0]:o[0] + o[1]]
    main = jnp.concatenate([cut(O_Z), cut(O_XBC), cut(O_ZA), cut(O_Q), cut(O_K), cut(O_V), cut(O_G)], axis=0)
    pad = jnp.zeros((N_SMALL - H_SSD - H_ATT, w.shape[1]), w.dtype)
    small = jnp.concatenate([cut(O_DT), cut(O_F), pad], axis=0)
    assert main.shape[0] == N_MAIN and small.shape[0] == N_SMALL
    return main, small


def _from_aligned_rows(main, small):
    def cm(c0, n):
        return main[c0:c0 + n]
    flat = jnp.concatenate([cm(C_Z, 2048), cm(C_XBC, 3072), small[0:H_SSD], cm(C_ZA, 1024),
                            cm(C_Q, 1024), cm(C_K, 1024), cm(C_V, 1024), small[H_SSD:H_SSD + H_ATT],
                            cm(C_G, 2048)], axis=0)
    return flat.reshape(4, N_COLS // 4, flat.shape[1])


_MESH = pl.DeviceIdType.MESH
_ANY = pl.BlockSpec(memory_space=pl.ANY)
_VM = pl.BlockSpec(memory_space=pltpu.VMEM)
_HALF = 512
N_DEV = 8


def _coords():
    return lax.axis_index("x"), lax.axis_index("y"), lax.axis_index("c")


def _other_chips(x, y):
    return [(1 - x, y), (x, 1 - y), (1 - x, 1 - y)]


def _half(cc):
    return pl.ds(cc * _HALF, _HALF)


def _gather_shards(slots):
    n = len(slots)

    def body(*refs):
        buf = refs[n:2 * n]
        send_sems, recv_sems = refs[2 * n:]
        x, y, c = _coords()
        chip = 2 * x + y
        sibling = (x, y, 1 - c)
        chips = _other_chips(x, y)

        def copy(i, frm, cc, k, to):
            part = buf[i].at[frm, :, _half(cc)]
            return pltpu.make_async_remote_copy(src_ref=part, dst_ref=part, send_sem=send_sems.at[6 * i + k],
                                                recv_sem=recv_sems.at[6 * i + k], device_id=to, device_id_type=_MESH)

        def chip_of(k):
            return 2 * chips[k][0] + chips[k][1]

        first = [copy(i, chip, c, k, (*chips[k], c)) for k in range(3) for i in range(n)]
        for cp in first:
            cp.start()
        passed = []
        for k in range(3):
            for i in range(n):
                copy(i, chip_of(k), c, k, (*chips[k], c)).wait_recv()
                passed.append(copy(i, chip_of(k), c, 3 + k, sibling))
                passed[-1].start()
        for k in range(3):
            for i in range(n):
                copy(i, chip_of(k), 1 - c, 3 + k, sibling).wait_recv()
        for cp in first + passed:
            cp.wait_send()

    return pl.pallas_call(
        body, name="gather_shards",
        out_shape=tuple(jax.ShapeDtypeStruct(s.shape, s.dtype) for s in slots),
        in_specs=[_ANY] * n, out_specs=tuple([_ANY] * n),
        input_output_aliases={i: i for i in range(n)},
        scratch_shapes=[pltpu.SemaphoreType.DMA((6 * n,)), pltpu.SemaphoreType.DMA((6 * n,))],
    )(*slots)


def _allgather8(block, name):
    rows, width = block.shape

    def body(x_ref, out_ref, send_sems, recv_sems, local_sem):
        x, y, c = _coords()
        me, sibling = (x, y, c), (x, y, 1 - c)
        chips = _other_chips(x, y)

        def slot(px, py, pc):
            return out_ref.at[4 * px + 2 * py + pc]

        def copy(k, blk, to, src=None):
            return pltpu.make_async_remote_copy(src_ref=slot(*blk) if src is None else src, dst_ref=slot(*blk),
                                                send_sem=send_sems.at[k], recv_sem=recv_sems.at[k],
                                                device_id=to, device_id_type=_MESH)

        mine = pltpu.make_async_copy(x_ref, slot(*me), local_sem)
        mine.start()
        first = [copy(0, me, sibling, src=x_ref)]
        first += [copy(1 + j, me, (*chip, c), src=x_ref) for j, chip in enumerate(chips)]
        for cp in first:
            cp.start()
        passed = [copy(4 + j, (*chip, c), sibling) for j, chip in enumerate(chips)]
        for j, chip in enumerate(chips):
            copy(1 + j, (*chip, c), me).wait_recv()
            passed[j].start()
        copy(0, sibling, me).wait_recv()
        for j, chip in enumerate(chips):
            copy(4 + j, (*chip, 1 - c), me).wait_recv()
        for cp in first + passed:
            cp.wait_send()
        mine.wait()

    return pl.pallas_call(
        body, name=name,
        out_shape=jax.ShapeDtypeStruct((N_DEV, rows, width), block.dtype),
        in_specs=[_VM], out_specs=_VM,
        scratch_shapes=[pltpu.SemaphoreType.DMA((7,)), pltpu.SemaphoreType.DMA((7,)), pltpu.SemaphoreType.DMA],
    )(block)


def _pair_swap(arrs):
    def copies(src, dst, send_sems, recv_sems):
        x, y, c = _coords()
        return [pltpu.make_async_remote_copy(src_ref=src[i].at[:, :, _half(1 - c)], dst_ref=dst[i],
                                             send_sem=send_sems.at[i], recv_sem=recv_sems.at[i],
                                             device_id=(x, y, 1 - c), device_id_type=_MESH) for i in range(len(src))]

    shapes = tuple(jax.ShapeDtypeStruct((4, a.shape[1], _HALF), a.dtype) for a in arrs)
    return tuple(arrs), shapes, copies, len(arrs), False


def _chip_exchange(arrs):
    def copies(src, dst, send_sems, recv_sems):
        x, y, c = _coords()
        chips = _other_chips(x, y)
        return [pltpu.make_async_remote_copy(src_ref=src[i].at[2 * chips[k][0] + chips[k][1]], dst_ref=dst[i].at[k],
                                             send_sem=send_sems.at[3 * i + k], recv_sem=recv_sems.at[3 * i + k],
                                             device_id=(*chips[k], c), device_id_type=_MESH)
                for k in range(3) for i in range(len(src))]

    shapes = tuple(jax.ShapeDtypeStruct((3,) + a.shape[1:], a.dtype) for a in arrs)
    return tuple(arrs), shapes, copies, 3 * len(arrs), False


def _gather_stage(slots, to_sibling):
    def copies(buf, _, send_sems, recv_sems):
        x, y, c = _coords()
        chips = _other_chips(x, y)
        out = []
        for k in range(3):
            for i in range(len(buf)):
                frm = 2 * chips[k][0] + chips[k][1] if to_sibling else 2 * x + y
                part = buf[i].at[frm, :, _half(c)]
                out.append(pltpu.make_async_remote_copy(
                    src_ref=part, dst_ref=part, send_sem=send_sems.at[3 * i + k], recv_sem=recv_sems.at[3 * i + k],
                    device_id=(x, y, 1 - c) if to_sibling else (*chips[k], c), device_id_type=_MESH))
        return out

    shapes = tuple(jax.ShapeDtypeStruct(s.shape, s.dtype) for s in slots)
    return tuple(slots), shapes, copies, 3 * len(slots), True


def _pair_join_halves(fulls):
    n = len(fulls)

    def body(*refs):
        buf = refs[n:2 * n]
        send_sems, recv_sems = refs[2 * n:]
        x, y, c = _coords()

        def remote(i, cc):
            part = buf[i].at[:, _half(cc)]
            return pltpu.make_async_remote_copy(src_ref=part, dst_ref=part, send_sem=send_sems.at[i],
                                                recv_sem=recv_sems.at[i], device_id=(x, y, 1 - c), device_id_type=_MESH)

        for i in range(n):
            remote(i, c).start()
        for i in range(n):
            remote(i, c).wait_send()
            remote(i, 1 - c).wait_recv()

    return pl.pallas_call(
        body, name="pair_join_halves",
        out_shape=tuple(jax.ShapeDtypeStruct(a.shape, a.dtype) for a in fulls),
        in_specs=[_ANY] * n, out_specs=tuple([_ANY] * n),
        input_output_aliases={i: i for i in range(n)},
        scratch_shapes=[pltpu.SemaphoreType.DMA((n,)), pltpu.SemaphoreType.DMA((n,))],
    )(*fulls)


_RED_TC = 128
_RED_NT = _HALF // _RED_TC


def _add_pair(ids, g32, recv_a):
    rows = g32.shape[1]

    def body(ids_ref, g_ref, r_ref, o_ref):
        o_ref[...] = (g_ref[...] + r_ref[...]).astype(BF16)

    blk = pl.BlockSpec((1, rows, _RED_TC), lambda j, l, ids: (j, 0, l))
    return pl.pallas_call(
        body, name="add_pair",
        out_shape=jax.ShapeDtypeStruct((4, rows, _HALF), BF16),
        grid_spec=pltpu.PrefetchScalarGridSpec(
            num_scalar_prefetch=1, grid=(4, _RED_NT),
            in_specs=[pl.BlockSpec((1, rows, _RED_TC), lambda j, l, ids: (j, 0, ids[0] * _RED_NT + l)), blk],
            out_specs=blk),
        compiler_params=_cp("parallel", "parallel"),
    )(ids, g32, recv_a)


def _add_chips(ids, g32, recv_a, recv_b):
    rows = g32.shape[1]

    def body(ids_ref, g_ref, a_ref, b_ref, o_ref):
        acc = g_ref[0] + a_ref[0]
        for k in range(3):
            acc = acc + b_ref[k].astype(F32)
        o_ref[...] = acc

    return pl.pallas_call(
        body, name="add_chips",
        out_shape=jax.ShapeDtypeStruct((rows, 2 * _HALF), F32),
        grid_spec=pltpu.PrefetchScalarGridSpec(
            num_scalar_prefetch=1, grid=(_RED_NT,),
            in_specs=[pl.BlockSpec((1, rows, _RED_TC), lambda l, ids: (ids[1], 0, ids[0] * _RED_NT + l)),
                      pl.BlockSpec((1, rows, _RED_TC), lambda l, ids: (ids[1], 0, l)),
                      pl.BlockSpec((3, rows, _RED_TC), lambda l, ids: (0, 0, l))],
            out_specs=pl.BlockSpec((rows, _RED_TC), lambda l, ids: (0, ids[0] * _RED_NT + l))),
        compiler_params=_cp("parallel"),
    )(ids, g32, recv_a, recv_b)


def _sum8(gathered):
    _, rows, width = gathered.shape

    def body(g_ref, o_ref):
        acc = g_ref[0]
        for d in range(1, N_DEV):
            acc = acc + g_ref[d]
        o_ref[...] = acc

    return pl.pallas_call(
        body, name="sum8",
        out_shape=jax.ShapeDtypeStruct((rows, width), F32),
        in_specs=[_VM], out_specs=_VM,
    )(gathered)


def _adamw(w, g, m, v, name):
    rows, cols = w.shape
    budget = (3 << 20) // 2
    tr, tc = rows, cols
    if rows * cols * 4 > budget:
        if rows % 8 == 0:
            tr = max(c for c in range(8, rows, 8) if rows % c == 0 and c * cols * 4 <= budget)
        else:
            tc = next(c for c in (512, 256, 128) if cols % c == 0 and rows * c * 4 <= budget)
    c1 = 1.0 - ADAM_B1 ** ADAM_STEP
    c2 = 1.0 - ADAM_B2 ** ADAM_STEP

    def body(w_ref, g_ref, m_ref, v_ref, d_ref, mo_ref, vo_ref):
        gg = g_ref[...]
        mn = ADAM_B1 * m_ref[...] + (1.0 - ADAM_B1) * gg
        vn = ADAM_B2 * v_ref[...] + (1.0 - ADAM_B2) * (gg * gg)
        mo_ref[...] = mn
        vo_ref[...] = vn
        d_ref[...] = -ADAM_LR * ((mn / c1) / (jnp.sqrt(vn / c2) + ADAM_EPS) + ADAM_WD * w_ref[...])

    blk = pl.BlockSpec((tr, tc), lambda i, j: (i, j))
    shp = jax.ShapeDtypeStruct((rows, cols), F32)
    return pl.pallas_call(
        body, name=name, out_shape=(shp, shp, shp), grid=(rows // tr, cols // tc),
        in_specs=[blk] * 4, out_specs=(blk, blk, blk),
        compiler_params=_cp("parallel", "parallel"),
    )(w, g, m, v)


def _rows128(a):
    return a.reshape(-1, 128)


def _pack_small(norm_pre, conv_b, ssd_norm, gate_bias, norm_post, dt_bias, a_log, d_skip, fgate_bias):
    tiny = jnp.concatenate([dt_bias.reshape(-1), a_log.reshape(-1), d_skip.reshape(-1), fgate_bias.reshape(-1),
                            jnp.zeros((16,), F32)])
    return jnp.concatenate([_rows128(norm_pre), _rows128(conv_b), _rows128(ssd_norm), _rows128(gate_bias),
                            _rows128(norm_post), tiny.reshape(1, 128)], axis=0)


_SMALL_PAD = 80


def _unpack_small(p):
    tiny = p[72]
    return dict(norm_pre=p[0:8].reshape(1, 1024), conv_b=p[8:32].reshape(1, 3072), ssd_norm=p[32:48].reshape(1, 2048),
                gate_bias=p[48:64].reshape(1, 2048), norm_post=p[64:72].reshape(1, 1024),
                dt_bias=tiny[0:32].reshape(1, 32), a_log=tiny[32:64].reshape(1, 32),
                d_skip=tiny[64:96].reshape(1, 32), fgate_bias=tiny[96:112].reshape(1, 16))


def _pad_rows(a, rows):
    return jnp.concatenate([a, jnp.zeros((rows - a.shape[0], a.shape[1]), a.dtype)], axis=0)


def kernel(x, meta_tokens, norm_pre, w_in, conv_w, conv_b, dt_bias, a_log, d_skip, ssd_norm, fgate_bias, gate_bias, w_proj_ssd, w_proj_att, w_out, norm_post, loss_target, m_meta_tokens, m_norm_pre, m_w_in, m_conv_w, m_conv_b, m_dt_bias, m_a_log, m_d_skip, m_ssd_norm, m_fgate_bias, m_gate_bias, m_w_proj_ssd, m_w_proj_att, m_w_out, m_norm_post, v_meta_tokens, v_norm_pre, v_w_in, v_conv_w, v_conv_b, v_dt_bias, v_a_log, v_d_skip, v_ssd_norm, v_fgate_bias, v_gate_bias, v_w_proj_ssd, v_w_proj_att, v_w_out, v_norm_post):
    cx, cy, cc = _coords()
    chip = 2 * cx + cy
    ids = jnp.stack([cc, chip]).astype(jnp.int32)
    seq = x.shape[1]

    w_in_sh = jnp.transpose(w_in[0]).astype(BF16)
    w_pr_sh = jnp.concatenate([w_proj_ssd[0], w_proj_att[0], w_out[0]], axis=0).astype(BF16)

    def own_slot(sh):
        return lax.dynamic_update_slice(lax.empty((4,) + sh.shape, sh.dtype), sh[None], (chip, 0, 0))

    (g_in,) = _gather_shards([own_slot(w_in_sh)])
    w_main, w_small = _to_aligned_rows(g_in)
    sm_sh = jnp.concatenate([_rows128(meta_tokens), _rows128(conv_w[0])], axis=0)
    sm_all = _allgather8(sm_sh, "gather_small_weights")[0::2]
    meta_full = jnp.transpose(sm_all[:, 0:32].reshape(4, N_META, 256), (1, 0, 2)).reshape(N_META, D_MODEL)
    conv_w_full = jnp.transpose(sm_all[:, 32:56].reshape(4, CONV_K, 768), (1, 0, 2)).reshape(CONV_K, CONV_DIM)

    h = jnp.concatenate([jnp.zeros((PADF, D_MODEL), F32), meta_full, x[0]], axis=0)
    bias_row = jnp.concatenate([dt_bias[0], fgate_bias[0], jnp.zeros((N_SMALL - H_SSD - H_ATT,), F32)]).reshape(1, N_SMALL)
    a_neg = -jnp.exp(a_log[0])
    a_row = jnp.concatenate([a_neg, jnp.zeros((N_SMALL - H_SSD,), F32)]).reshape(1, N_SMALL)
    dsk_row = jnp.repeat(d_skip[0], 64).reshape(1, D_SSD)
    r = _local_step(h, loss_target[0], w_main, w_small, own_slot(w_pr_sh), ids, norm_pre, conv_w_full, conv_b,
                    bias_row, a_row, dsk_row, ssd_norm, gate_bias, norm_post)

    tm = _mm_tiles(h.shape[0])
    n_row_tiles = h.shape[0] // tm
    d_w_main = _matmul_cat_tn(r["dproj"], r["u"], "inproj_dw", tm)
    d_w_small = _matmul(r["dsmall"], r["u"], "tn", F32, "inproj_small_dw", N_SMALL, D_MODEL, tm)
    g32_in = _from_aligned_rows(d_w_main, d_w_small)
    first = max(n_row_tiles // 6, 1)
    du_first, ra_in = _matmul_cat_nn(r["dproj"], w_main, "inproj_dx_swap", tm, rows=(0, first),
                                     exchange=_pair_swap([g32_in]))
    pb_in = _add_pair(ids, g32_in, ra_in)
    du_a, rb_in = _matmul_cat_nn(r["dproj"], w_main, "inproj_dx_exchange", tm,
                                 rows=(first, n_row_tiles - first), fill=du_first,
                                 exchange=_chip_exchange([pb_in]))
    du_b = _matmul(r["dsmall"], w_small, "nn", F32, "inproj_small_dx", tm, D_MODEL, N_SMALL)
    dh, d_norm_pre = _norm1_bwd(du_a, du_b, h, norm_pre, r["dy_out"])
    grad_x = dh[PADF + N_META:].reshape(1, seq, D_MODEL)
    half_in = _add_chips(ids, g32_in, ra_in, rb_in)
    gw_in, gw_pr = _pair_join_halves([half_in, r["half_pr"]])

    tiny = r["d_bias_row"][0]
    part_small = _pack_small(d_norm_pre, r["d_conv_b"], r["d_ssd_norm"], r["d_gate_bias"], r["d_norm_post"],
                             tiny[0:H_SSD], r["d_a"][0, 0:H_SSD] * a_neg, r["d_dsk"].reshape(H_SSD, 64).sum(axis=1),
                             tiny[H_SSD:H_SSD + H_ATT])
    part = jnp.concatenate([_pad_rows(part_small, _SMALL_PAD), _rows128(r["d_conv_w"]),
                            _rows128(dh[PADF:PADF + N_META]), r["loss_blk"]], axis=0)
    tot = _sum8(_allgather8(part, "gather_small_grads"))
    loss = tot[_SMALL_PAD + 96 + 128, 0]
    g_small = tot[0:_SMALL_PAD]
    g_conv_w = lax.dynamic_slice_in_dim(tot[_SMALL_PAD:_SMALL_PAD + 96].reshape(CONV_K, CONV_DIM), chip * 768, 768, axis=1)
    g_meta = lax.dynamic_slice_in_dim(tot[_SMALL_PAD + 96:_SMALL_PAD + 224].reshape(N_META, D_MODEL), chip * 256, 256, axis=1)

    upd = {}
    upd["w_in"] = tuple(jnp.transpose(a) for a in (gw_in,) + _adamw(
        jnp.transpose(w_in[0]), gw_in, jnp.transpose(m_w_in[0]), jnp.transpose(v_w_in[0]), "adamw_w_in"))
    w_pr32 = jnp.concatenate([w_proj_ssd[0], w_proj_att[0], w_out[0]], axis=0)
    m_pr = jnp.concatenate([m_w_proj_ssd[0], m_w_proj_att[0], m_w_out[0]], axis=0)
    v_pr = jnp.concatenate([v_w_proj_ssd[0], v_w_proj_att[0], v_w_out[0]], axis=0)
    pr = (gw_pr,) + _adamw(w_pr32, gw_pr, m_pr, v_pr, "adamw_w_proj")
    upd["w_proj_ssd"] = tuple(a[0:512] for a in pr)
    upd["w_proj_att"] = tuple(a[512:768] for a in pr)
    upd["w_out"] = tuple(a[768:1024] for a in pr)
    upd["conv_w"] = (g_conv_w,) + _adamw(conv_w[0], g_conv_w, m_conv_w[0], v_conv_w[0], "adamw_conv_w")
    upd["meta_tokens"] = (g_meta,) + _adamw(meta_tokens, g_meta, m_meta_tokens, v_meta_tokens, "adamw_meta")
    pk = lambda np_, cb, sn, gb, npo, dtb, al, ds, fg: _pad_rows(_pack_small(np_, cb, sn, gb, npo, dtb, al, ds, fg), _SMALL_PAD)
    w_sm = pk(norm_pre, conv_b, ssd_norm, gate_bias, norm_post, dt_bias, a_log, d_skip, fgate_bias)
    m_sm = pk(m_norm_pre, m_conv_b, m_ssd_norm, m_gate_bias, m_norm_post, m_dt_bias, m_a_log, m_d_skip, m_fgate_bias)
    v_sm = pk(v_norm_pre, v_conv_b, v_ssd_norm, v_gate_bias, v_norm_post, v_dt_bias, v_a_log, v_d_skip, v_fgate_bias)
    sm = [_unpack_small(a) for a in (g_small,) + _adamw(w_sm, g_small, m_sm, v_sm, "adamw_small")]
    for name in ("norm_pre", "conv_b", "dt_bias", "a_log", "d_skip", "ssd_norm", "fgate_bias", "gate_bias", "norm_post"):
        upd[name] = tuple(s[name] for s in sm)
    lead = ("w_in", "conv_w", "w_proj_ssd", "w_proj_att", "w_out")
    order = ("meta_tokens", "norm_pre", "w_in", "conv_w", "conv_b", "dt_bias", "a_log", "d_skip", "ssd_norm",
             "fgate_bias", "gate_bias", "w_proj_ssd", "w_proj_att", "w_out", "norm_post")
    outs = [loss, grad_x]
    for part_i in range(4):
        for name in order:
            a = upd[name][part_i]
            outs.append(a[None] if name in lead else a)
    return tuple(outs)
```
